```python
import jax, jax.numpy as jnp
from jax import lax
import numpy as np

D_MODEL = 1024
BATCH = 8
SEQ = 8192
DEPTH = 1

D_MIX = D_MODEL
RG_WIDTH = D_MIX // 2
RG_BLOCKS = 8
RG_BLOCK = RG_WIDTH // RG_BLOCKS
RG_C = 8.0
CONV_W = 4
GDN_HEADS = 4
GDN_DK = 128
GDN_DV = 128
GDN_QK = GDN_HEADS * GDN_DK
GDN_VW = GDN_HEADS * GDN_DV
CHUNK = 64
D_FF = 2816
N_DIR = 2
EPS = 1e-6

OFF_RG_X = 0
OFF_RG_G = OFF_RG_X + RG_WIDTH
OFF_QKV = OFF_RG_G + RG_WIDTH
OFF_Z = OFF_QKV + 2 * GDN_QK + GDN_VW
OFF_BETA = OFF_Z + GDN_VW
OFF_ALPHA = OFF_BETA + N_DIR * GDN_HEADS
D_IN_PROJ = OFF_ALPHA + N_DIR * GDN_HEADS

kernel_name = "hymba_style_rglru_gdn_macaron_encoder"


def rmsnorm(x, g):
    xf = x.astype(jnp.float32)
    y = xf * lax.rsqrt(jnp.mean(xf * xf, axis=-1, keepdims=True) + EPS)
    return (y * g.astype(jnp.float32)).astype(x.dtype)


def l2norm(t):
    return t * lax.rsqrt(jnp.sum(t * t, axis=-1, keepdims=True) + EPS)


def centred_dwconv(x, w):
    left = CONV_W // 2
    return lax.conv_general_dilated(
        x, w[:, None, :].astype(x.dtype), window_strides=(1,),
        padding=[(left, CONV_W - 1 - left)],
        dimension_numbers=("NWC", "WIO", "NWC"),
        feature_group_count=x.shape[-1])


def swiglu_ffn(x, g, w_gate, w_up, w_down):
    h = rmsnorm(x, g)
    return (jax.nn.silu(h @ w_gate) * (h @ w_up)) @ w_down


def linear_scan(a, b, reverse):
    def combine(l, r):
        return (l[0] * r[0], r[0] * l[1] + r[1])
    _, h = lax.associative_scan(combine, (a, b), reverse=reverse, axis=1)
    return h


def rg_lru_bidir(xc, wa, ba, wx, bx, lam):
    B, S, _ = xc.shape
    xb = xc.reshape(B, S, RG_BLOCKS, RG_BLOCK)
    r = jax.nn.sigmoid(jnp.einsum("bsni,dnij->dbsnj", xb, wa.astype(jnp.float32)).reshape(N_DIR, B, S, RG_WIDTH)
                       + ba.astype(jnp.float32)[:, None, None, :])
    i = jax.nn.sigmoid(jnp.einsum("bsni,dnij->dbsnj", xb, wx.astype(jnp.float32)).reshape(N_DIR, B, S, RG_WIDTH)
                       + bx.astype(jnp.float32)[:, None, None, :])
    log_a = -RG_C * r * jax.nn.softplus(-lam.astype(jnp.float32))[:, None, None, :]
    a = jnp.exp(log_a)
    b = jnp.sqrt(-jnp.expm1(2.0 * log_a)) * (i * xc[None])
    h_f = linear_scan(a[0], b[0], reverse=False)
    h_b = linear_scan(a[1], b[1], reverse=True)
    return h_f + h_b


def gdn_chunked(q, k, v, beta, g):
    B, S, H, DK = q.shape
    DV = v.shape[-1]
    N = S // CHUNK

    def chunks(t):
        t = t.reshape((B, N, CHUNK, H) + t.shape[3:])
        return jnp.moveaxis(t, 3, 1)

    q, k, v, beta, g = chunks(q), chunks(k), chunks(v), chunks(beta), chunks(g)
    g_cum = jnp.cumsum(g, axis=-1)
    idx = jnp.arange(CHUNK)
    incl = idx[:, None] >= idx[None, :]
    strict = idx[:, None] > idx[None, :]
    decay = jnp.exp(jnp.where(incl, g_cum[..., :, None] - g_cum[..., None, :], -jnp.inf))
    k_beta = k * beta[..., None]
    v_beta = v * beta[..., None]
    L = jnp.where(strict, jnp.einsum("bhnik,bhnjk->bhnij", k_beta, k) * decay, 0.0)
    eye = jnp.broadcast_to(jnp.eye(CHUNK, dtype=q.dtype), L.shape)
    T = lax.linalg.triangular_solve(L, eye, left_side=True, lower=True, unit_diagonal=True)
    u = jnp.einsum("bhnij,bhnjv->bhniv", T, v_beta)
    w = jnp.einsum("bhnij,bhnjk->bhnik", T, k_beta * jnp.exp(g_cum)[..., None])
    attn = jnp.einsum("bhnik,bhnjk->bhnij", q, k) * decay
    g_last = g_cum[..., -1:]
    q_dec = q * jnp.exp(g_cum)[..., None]
    k_dec = k * jnp.exp(g_last - g_cum)[..., None]
    c_decay = jnp.exp(g_last[..., 0])

    xs = tuple(jnp.moveaxis(t, 2, 0) for t in (w, u, q_dec, k_dec, attn, c_decay))

    def step(state, inp):
        w_n, u_n, qd_n, kd_n, a_n, cd_n = inp
        v_new = u_n - jnp.einsum("bhck,bhkv->bhcv", w_n, state)
        o_n = jnp.einsum("bhck,bhkv->bhcv", qd_n, state) + jnp.einsum("bhij,bhjv->bhiv", a_n, v_new)
        state = state * cd_n[..., None, None] + jnp.einsum("bhck,bhcv->bhkv", kd_n, v_new)
        return state, o_n

    s0 = jnp.zeros((B, H, DK, DV), q.dtype)
    _, o = lax.scan(step, s0, xs)
    return jnp.transpose(o, (1, 0, 3, 2, 4)).reshape(B, S, H, DV)


def hybrid_mixer(h, w_in, w_out, rg_conv_w, rg_conv_b, rg_gate_a_w, rg_gate_a_b,
                 rg_gate_x_w, rg_gate_x_b, rg_lambda, gdn_conv_w, gdn_a_log, gdn_dt_bias, gdn_norm):
    B, S, _ = h.shape
    f32 = jnp.float32
    p = h @ w_in
    x_rg = p[..., OFF_RG_X:OFF_RG_G]
    gate_rg = p[..., OFF_RG_G:OFF_QKV]
    qkv = p[..., OFF_QKV:OFF_Z]
    z = p[..., OFF_Z:OFF_BETA]
    beta_raw = p[..., OFF_BETA:OFF_ALPHA].reshape(B, S, N_DIR, GDN_HEADS)
    alpha_raw = p[..., OFF_ALPHA:D_IN_PROJ].reshape(B, S, N_DIR, GDN_HEADS)

    xc = (centred_dwconv(x_rg, rg_conv_w) + rg_conv_b).astype(f32)
    hr = rg_lru_bidir(xc, rg_gate_a_w, rg_gate_a_b, rg_gate_x_w, rg_gate_x_b, rg_lambda)
    y_rg = (hr * jax.nn.gelu(gate_rg.astype(f32))).astype(h.dtype)

    qkv = jax.nn.silu(centred_dwconv(qkv, gdn_conv_w)).astype(f32)
    q = l2norm(qkv[..., :GDN_QK].reshape(B, S, GDN_HEADS, GDN_DK)) * (GDN_DK ** -0.5)
    k = l2norm(qkv[..., GDN_QK:2 * GDN_QK].reshape(B, S, GDN_HEADS, GDN_DK))
    v = qkv[..., 2 * GDN_QK:].reshape(B, S, GDN_HEADS, GDN_DV)
    beta = jax.nn.sigmoid(beta_raw.astype(f32))
    g = -jnp.exp(gdn_a_log.astype(f32)) * jax.nn.softplus(alpha_raw.astype(f32) + gdn_dt_bias.astype(f32))
    flip = lambda t: jnp.flip(t, axis=1)
    o_f = gdn_chunked(q, k, v, beta[:, :, 0], g[:, :, 0])
    o_b = flip(gdn_chunked(flip(q), flip(k), flip(v), flip(beta[:, :, 1]), flip(g[:, :, 1])))
    o = rmsnorm(o_f + o_b, gdn_norm) * jax.nn.silu(z.astype(f32).reshape(B, S, GDN_HEADS, GDN_DV))
    y_gdn = o.reshape(B, S, GDN_VW).astype(h.dtype)

    return jnp.concatenate([y_rg, y_gdn], axis=-1) @ w_out


def _fwd_setup_inputs(seed: int = 0) -> dict:
    key = jax.random.key(seed)
    ks = iter(jax.random.split(key, 32))
    nrm = lambda shape, scale: jax.random.normal(next(ks), shape, jnp.float32) * scale
    gain = lambda shape: 1.0 + nrm(shape, 0.01)
    L = DEPTH
    a_c = jax.random.uniform(next(ks), (L, N_DIR, RG_WIDTH), jnp.float32, 0.9, 0.999)
    s = a_c ** (1.0 / RG_C)
    rg_lambda = jnp.log(s) - jnp.log1p(-s)
    gdn_a_log = jnp.log(jax.random.uniform(next(ks), (L, N_DIR, GDN_HEADS), jnp.float32, 1.0, 16.0))
    dt = jnp.exp(jax.random.uniform(next(ks), (L, N_DIR, GDN_HEADS), jnp.float32, np.log(1e-3), np.log(1e-1)))
    gdn_dt_bias = dt + jnp.log(-jnp.expm1(-dt))
    return {
        "x": nrm((BATCH, SEQ, D_MODEL), 1.0),
        "ffn1_norm": gain((L, D_MODEL)),
        "ffn1_w_gate": nrm((L, D_MODEL, D_FF), D_MODEL ** -0.5),
        "ffn1_w_up": nrm((L, D_MODEL, D_FF), D_MODEL ** -0.5),
        "ffn1_w_down": nrm((L, D_FF, D_MODEL), D_FF ** -0.5),
        "mix_norm": gain((L, D_MODEL)),
        "w_in": nrm((L, D_MODEL, D_IN_PROJ), D_MODEL ** -0.5),
        "w_out": nrm((L, D_MIX, D_MODEL), D_MIX ** -0.5),
        "rg_conv_w": nrm((L, CONV_W, RG_WIDTH), CONV_W ** -0.5),
        "rg_conv_b": nrm((L, RG_WIDTH), 0.01),
        "rg_gate_a_w": nrm((L, N_DIR, RG_BLOCKS, RG_BLOCK, RG_BLOCK), RG_BLOCK ** -0.5),
        "rg_gate_a_b": nrm((L, N_DIR, RG_WIDTH), 0.01),
        "rg_gate_x_w": nrm((L, N_DIR, RG_BLOCKS, RG_BLOCK, RG_BLOCK), RG_BLOCK ** -0.5),
        "rg_gate_x_b": nrm((L, N_DIR, RG_WIDTH), 0.01),
        "rg_lambda": rg_lambda,
        "gdn_conv_w": nrm((L, CONV_W, 2 * GDN_QK + GDN_VW), CONV_W ** -0.5),
        "gdn_a_log": gdn_a_log,
        "gdn_dt_bias": gdn_dt_bias,
        "gdn_norm": gain((L, GDN_DV)),
        "ffn2_norm": gain((L, D_MODEL)),
        "ffn2_w_gate": nrm((L, D_MODEL, D_FF), D_MODEL ** -0.5),
        "ffn2_w_up": nrm((L, D_MODEL, D_FF), D_MODEL ** -0.5),
        "ffn2_w_down": nrm((L, D_FF, D_MODEL), D_FF ** -0.5),
        "final_norm": gain((D_MODEL,)),
    }


def _fwd_reference(x, ffn1_norm, ffn1_w_gate, ffn1_w_up, ffn1_w_down, mix_norm, w_in, w_out,
              rg_conv_w, rg_conv_b, rg_gate_a_w, rg_gate_a_b, rg_gate_x_w, rg_gate_x_b, rg_lambda,
              gdn_conv_w, gdn_a_log, gdn_dt_bias, gdn_norm,
              ffn2_norm, ffn2_w_gate, ffn2_w_up, ffn2_w_down, final_norm):
    for l in range(DEPTH):
        x = x + 0.5 * swiglu_ffn(x, ffn1_norm[l], ffn1_w_gate[l], ffn1_w_up[l], ffn1_w_down[l])
        x = x + hybrid_mixer(rmsnorm(x, mix_norm[l]), w_in[l], w_out[l],
                             rg_conv_w[l], rg_conv_b[l], rg_gate_a_w[l], rg_gate_a_b[l],
                             rg_gate_x_w[l], rg_gate_x_b[l], rg_lambda[l],
                             gdn_conv_w[l], gdn_a_log[l], gdn_dt_bias[l], gdn_norm[l])
        x = x + 0.5 * swiglu_ffn(x, ffn2_norm[l], ffn2_w_gate[l], ffn2_w_up[l], ffn2_w_down[l])
    return rmsnorm(x, final_norm)


import jax as _jax
import jax.numpy as _jnp

TWIN_FORMAT = 'train_step'
FWD_PARAMS = ['x', 'ffn1_norm', 'ffn1_w_gate', 'ffn1_w_up', 'ffn1_w_down', 'mix_norm', 'w_in', 'w_out', 'rg_conv_w', 'rg_conv_b', 'rg_gate_a_w', 'rg_gate_a_b', 'rg_gate_x_w', 'rg_gate_x_b', 'rg_lambda', 'gdn_conv_w', 'gdn_a_log', 'gdn_dt_bias', 'gdn_norm', 'ffn2_norm', 'ffn2_w_gate', 'ffn2_w_up', 'ffn2_w_down', 'final_norm']
TWIN_WEIGHTS = ['ffn1_norm', 'ffn1_w_gate', 'ffn1_w_up', 'ffn1_w_down', 'mix_norm', 'w_in', 'w_out', 'rg_conv_w', 'rg_conv_b', 'rg_gate_a_w', 'rg_gate_a_b', 'rg_gate_x_w', 'rg_gate_x_b', 'rg_lambda', 'gdn_conv_w', 'gdn_a_log', 'gdn_dt_bias', 'gdn_norm', 'ffn2_norm', 'ffn2_w_gate', 'ffn2_w_up', 'ffn2_w_down', 'final_norm']
TWIN_DIFF_INPUT = 'x'
TWIN_INPUTS = ['x', 'ffn1_norm', 'ffn1_w_gate', 'ffn1_w_up', 'ffn1_w_down', 'mix_norm', 'w_in', 'w_out', 'rg_conv_w', 'rg_conv_b', 'rg_gate_a_w', 'rg_gate_a_b', 'rg_gate_x_w', 'rg_gate_x_b', 'rg_lambda', 'gdn_conv_w', 'gdn_a_log', 'gdn_dt_bias', 'gdn_norm', 'ffn2_norm', 'ffn2_w_gate', 'ffn2_w_up', 'ffn2_w_down', 'final_norm', 'loss_target', 'm_ffn1_norm', 'm_ffn1_w_gate', 'm_ffn1_w_up', 'm_ffn1_w_down', 'm_mix_norm', 'm_w_in', 'm_w_out', 'm_rg_conv_w', 'm_rg_conv_b', 'm_rg_gate_a_w', 'm_rg_gate_a_b', 'm_rg_gate_x_w', 'm_rg_gate_x_b', 'm_rg_lambda', 'm_gdn_conv_w', 'm_gdn_a_log', 'm_gdn_dt_bias', 'm_gdn_norm', 'm_ffn2_norm', 'm_ffn2_w_gate', 'm_ffn2_w_up', 'm_ffn2_w_down', 'm_final_norm', 'v_ffn1_norm', 'v_ffn1_w_gate', 'v_ffn1_w_up', 'v_ffn1_w_down', 'v_mix_norm', 'v_w_in', 'v_w_out', 'v_rg_conv_w', 'v_rg_conv_b', 'v_rg_gate_a_w', 'v_rg_gate_a_b', 'v_rg_gate_x_w', 'v_rg_gate_x_b', 'v_rg_lambda', 'v_gdn_conv_w', 'v_gdn_a_log', 'v_gdn_dt_bias', 'v_gdn_norm', 'v_ffn2_norm', 'v_ffn2_w_gate', 'v_ffn2_w_up', 'v_ffn2_w_down', 'v_final_norm']
TWIN_OUTPUTS = ['loss', 'grad_x', 'grad_ffn1_norm', 'grad_ffn1_w_gate', 'grad_ffn1_w_up', 'grad_ffn1_w_down', 'grad_mix_norm', 'grad_w_in', 'grad_w_out', 'grad_rg_conv_w', 'grad_rg_conv_b', 'grad_rg_gate_a_w', 'grad_rg_gate_a_b', 'grad_rg_gate_x_w', 'grad_rg_gate_x_b', 'grad_rg_lambda', 'grad_gdn_conv_w', 'grad_gdn_a_log', 'grad_gdn_dt_bias', 'grad_gdn_norm', 'grad_ffn2_norm', 'grad_ffn2_w_gate', 'grad_ffn2_w_up', 'grad_ffn2_w_down', 'grad_final_norm', 'delta_ffn1_norm', 'delta_ffn1_w_gate', 'delta_ffn1_w_up', 'delta_ffn1_w_down', 'delta_mix_norm', 'delta_w_in', 'delta_w_out', 'delta_rg_conv_w', 'delta_rg_conv_b', 'delta_rg_gate_a_w', 'delta_rg_gate_a_b', 'delta_rg_gate_x_w', 'delta_rg_gate_x_b', 'delta_rg_lambda', 'delta_gdn_conv_w', 'delta_gdn_a_log', 'delta_gdn_dt_bias', 'delta_gdn_norm', 'delta_ffn2_norm', 'delta_ffn2_w_gate', 'delta_ffn2_w_up', 'delta_ffn2_w_down', 'delta_final_norm', 'new_m_ffn1_norm', 'new_m_ffn1_w_gate', 'new_m_ffn1_w_up', 'new_m_ffn1_w_down', 'new_m_mix_norm', 'new_m_w_in', 'new_m_w_out', 'new_m_rg_conv_w', 'new_m_rg_conv_b', 'new_m_rg_gate_a_w', 'new_m_rg_gate_a_b', 'new_m_rg_gate_x_w', 'new_m_rg_gate_x_b', 'new_m_rg_lambda', 'new_m_gdn_conv_w', 'new_m_gdn_a_log', 'new_m_gdn_dt_bias', 'new_m_gdn_norm', 'new_m_ffn2_norm', 'new_m_ffn2_w_gate', 'new_m_ffn2_w_up', 'new_m_ffn2_w_down', 'new_m_final_norm', 'new_v_ffn1_norm', 'new_v_ffn1_w_gate', 'new_v_ffn1_w_up', 'new_v_ffn1_w_down', 'new_v_mix_norm', 'new_v_w_in', 'new_v_w_out', 'new_v_rg_conv_w', 'new_v_rg_conv_b', 'new_v_rg_gate_a_w', 'new_v_rg_gate_a_b', 'new_v_rg_gate_x_w', 'new_v_rg_gate_x_b', 'new_v_rg_lambda', 'new_v_gdn_conv_w', 'new_v_gdn_a_log', 'new_v_gdn_dt_bias', 'new_v_gdn_norm', 'new_v_ffn2_norm', 'new_v_ffn2_w_gate', 'new_v_ffn2_w_up', 'new_v_ffn2_w_down', 'new_v_final_norm']
TWIN_LEAF_KINDS = {'loss': 'loss', 'grad_x': 'grad_x', 'grad_ffn1_norm': 'grad_w', 'grad_ffn1_w_gate': 'grad_w', 'grad_ffn1_w_up': 'grad_w', 'grad_ffn1_w_down': 'grad_w', 'grad_mix_norm': 'grad_w', 'grad_w_in': 'grad_w', 'grad_w_out': 'grad_w', 'grad_rg_conv_w': 'grad_w', 'grad_rg_conv_b': 'grad_w', 'grad_rg_gate_a_w': 'grad_w', 'grad_rg_gate_a_b': 'grad_w', 'grad_rg_gate_x_w': 'grad_w', 'grad_rg_gate_x_b': 'grad_w', 'grad_rg_lambda': 'grad_w', 'grad_gdn_conv_w': 'grad_w', 'grad_gdn_a_log': 'grad_w', 'grad_gdn_dt_bias': 'grad_w', 'grad_gdn_norm': 'grad_w', 'grad_ffn2_norm': 'grad_w', 'grad_ffn2_w_gate': 'grad_w', 'grad_ffn2_w_up': 'grad_w', 'grad_ffn2_w_down': 'grad_w', 'grad_final_norm': 'grad_w', 'delta_ffn1_norm': 'delta_w', 'delta_ffn1_w_gate': 'delta_w', 'delta_ffn1_w_up': 'delta_w', 'delta_ffn1_w_down': 'delta_w', 'delta_mix_norm': 'delta_w', 'delta_w_in': 'delta_w', 'delta_w_out': 'delta_w', 'delta_rg_conv_w': 'delta_w', 'delta_rg_conv_b': 'delta_w', 'delta_rg_gate_a_w': 'delta_w', 'delta_rg_gate_a_b': 'delta_w', 'delta_rg_gate_x_w': 'delta_w', 'delta_rg_gate_x_b': 'delta_w', 'delta_rg_lambda': 'delta_w', 'delta_gdn_conv_w': 'delta_w', 'delta_gdn_a_log': 'delta_w', 'delta_gdn_dt_bias': 'delta_w', 'delta_gdn_norm': 'delta_w', 'delta_ffn2_norm': 'delta_w', 'delta_ffn2_w_gate': 'delta_w', 'delta_ffn2_w_up': 'delta_w', 'delta_ffn2_w_down': 'delta_w', 'delta_final_norm': 'delta_w', 'new_m_ffn1_norm': 'new_m', 'new_m_ffn1_w_gate': 'new_m', 'new_m_ffn1_w_up': 'new_m', 'new_m_ffn1_w_down': 'new_m', 'new_m_mix_norm': 'new_m', 'new_m_w_in': 'new_m', 'new_m_w_out': 'new_m', 'new_m_rg_conv_w': 'new_m', 'new_m_rg_conv_b': 'new_m', 'new_m_rg_gate_a_w': 'new_m', 'new_m_rg_gate_a_b': 'new_m', 'new_m_rg_gate_x_w': 'new_m', 'new_m_rg_gate_x_b': 'new_m', 'new_m_rg_lambda': 'new_m', 'new_m_gdn_conv_w': 'new_m', 'new_m_gdn_a_log': 'new_m', 'new_m_gdn_dt_bias': 'new_m', 'new_m_gdn_norm': 'new_m', 'new_m_ffn2_norm': 'new_m', 'new_m_ffn2_w_gate': 'new_m', 'new_m_ffn2_w_up': 'new_m', 'new_m_ffn2_w_down': 'new_m', 'new_m_final_norm': 'new_m', 'new_v_ffn1_norm': 'new_v', 'new_v_ffn1_w_gate': 'new_v', 'new_v_ffn1_w_up': 'new_v', 'new_v_ffn1_w_down': 'new_v', 'new_v_mix_norm': 'new_v', 'new_v_w_in': 'new_v', 'new_v_w_out': 'new_v', 'new_v_rg_conv_w': 'new_v', 'new_v_rg_conv_b': 'new_v', 'new_v_rg_gate_a_w': 'new_v', 'new_v_rg_gate_a_b': 'new_v', 'new_v_rg_gate_x_w': 'new_v', 'new_v_rg_gate_x_b': 'new_v', 'new_v_rg_lambda': 'new_v', 'new_v_gdn_conv_w': 'new_v', 'new_v_gdn_a_log': 'new_v', 'new_v_gdn_dt_bias': 'new_v', 'new_v_gdn_norm': 'new_v', 'new_v_ffn2_norm': 'new_v', 'new_v_ffn2_w_gate': 'new_v', 'new_v_ffn2_w_up': 'new_v', 'new_v_ffn2_w_down': 'new_v', 'new_v_final_norm': 'new_v'}


def _forward(args):
    return _fwd_reference(*[args[k] for k in FWD_PARAMS])


def _output_shape():
    def fwd():
        inp = _fwd_setup_inputs(0)
        return _fwd_reference(*[inp[k] for k in FWD_PARAMS])
    out = _jax.eval_shape(fwd)
    return out.shape, out.dtype

N_MICROBATCH = 1
ADAM_LR = 0.001
ADAM_B1 = 0.9
ADAM_B2 = 0.999
ADAM_EPS = 1e-08
ADAM_WD = 0.01
ADAM_STEP = 10
PER_EXAMPLE_BATCH_AXIS = {'x': 0, 'loss_target': 0}
SHARED_INPUTS = []
_WEIGHT_DTYPES = {'ffn1_norm': _jnp.float32, 'ffn1_w_gate': _jnp.float32, 'ffn1_w_up': _jnp.float32, 'ffn1_w_down': _jnp.float32, 'mix_norm': _jnp.float32, 'w_in': _jnp.float32, 'w_out': _jnp.float32, 'rg_conv_w': _jnp.float32, 'rg_conv_b': _jnp.float32, 'rg_gate_a_w': _jnp.float32, 'rg_gate_a_b': _jnp.float32, 'rg_gate_x_w': _jnp.float32, 'rg_gate_x_b': _jnp.float32, 'rg_lambda': _jnp.float32, 'gdn_conv_w': _jnp.float32, 'gdn_a_log': _jnp.float32, 'gdn_dt_bias': _jnp.float32, 'gdn_norm': _jnp.float32, 'ffn2_norm': _jnp.float32, 'ffn2_w_gate': _jnp.float32, 'ffn2_w_up': _jnp.float32, 'ffn2_w_down': _jnp.float32, 'final_norm': _jnp.float32}
MOMENT_SCALE = {'ffn1_norm': 1.192476e-01, 'ffn1_w_gate': 5.173071e-02, 'ffn1_w_up': 5.014269e-02, 'ffn1_w_down': 8.303613e-02, 'mix_norm': 1.936099e-01, 'w_in': 1.077782e-01, 'w_out': 1.336121e-01, 'rg_conv_w': 1.383493e-01, 'rg_conv_b': 2.089978e+00, 'rg_gate_a_w': 3.563750e-02, 'rg_gate_a_b': 2.975754e-02, 'rg_gate_x_w': 6.443664e-02, 'rg_gate_x_b': 2.947811e-02, 'rg_lambda': 4.952707e-02, 'gdn_conv_w': 8.868025e-02, 'gdn_a_log': 3.094195e-01, 'gdn_dt_bias': 2.987715e-01, 'gdn_norm': 2.556777e-01, 'ffn2_norm': 9.281403e-02, 'ffn2_w_gate': 3.797671e-02, 'ffn2_w_up': 3.677046e-02, 'ffn2_w_down': 6.084247e-02, 'final_norm': 6.383057e+01}


def _to_microbatches(a, axis):
    t = _jnp.moveaxis(a, axis, 0)
    t = t.reshape((N_MICROBATCH, t.shape[0] // N_MICROBATCH) + t.shape[1:])
    return _jnp.moveaxis(t, 1, axis + 1)


def setup_inputs(seed: int = 0) -> dict:
    inp = _fwd_setup_inputs(seed)
    key = _jax.random.fold_in(_jax.random.key(seed), 7919)
    shape, _ = _output_shape()
    out = dict(inp)
    out["loss_target"] = _jax.random.normal(_jax.random.fold_in(key, 0), shape, _jnp.float32)
    for i, name in enumerate(TWIN_WEIGHTS):
        w = inp[name].astype(_jnp.float32)
        if MOMENT_SCALE is None:
            s = _jnp.sqrt(_jnp.mean(_jnp.square(w)) + 1e-30)
        else:
            s = MOMENT_SCALE[name]
        km, kv = _jax.random.split(_jax.random.fold_in(key, i + 1))
        out[name] = w
        out["m_" + name] = s * _jax.random.normal(km, w.shape, _jnp.float32)
        out["v_" + name] = (s * s) * _jax.random.uniform(kv, w.shape, _jnp.float32, 0.5, 1.5)
    if N_MICROBATCH > 1:
        for name, axis in PER_EXAMPLE_BATCH_AXIS.items():
            out[name] = _to_microbatches(out[name], axis)
    return {'x': out['x'], 'ffn1_norm': out['ffn1_norm'], 'ffn1_w_gate': out['ffn1_w_gate'], 'ffn1_w_up': out['ffn1_w_up'], 'ffn1_w_down': out['ffn1_w_down'], 'mix_norm': out['mix_norm'], 'w_in': out['w_in'], 'w_out': out['w_out'], 'rg_conv_w': out['rg_conv_w'], 'rg_conv_b': out['rg_conv_b'], 'rg_gate_a_w': out['rg_gate_a_w'], 'rg_gate_a_b': out['rg_gate_a_b'], 'rg_gate_x_w': out['rg_gate_x_w'], 'rg_gate_x_b': out['rg_gate_x_b'], 'rg_lambda': out['rg_lambda'], 'gdn_conv_w': out['gdn_conv_w'], 'gdn_a_log': out['gdn_a_log'], 'gdn_dt_bias': out['gdn_dt_bias'], 'gdn_norm': out['gdn_norm'], 'ffn2_norm': out['ffn2_norm'], 'ffn2_w_gate': out['ffn2_w_gate'], 'ffn2_w_up': out['ffn2_w_up'], 'ffn2_w_down': out['ffn2_w_down'], 'final_norm': out['final_norm'], 'loss_target': out['loss_target'], 'm_ffn1_norm': out['m_ffn1_norm'], 'm_ffn1_w_gate': out['m_ffn1_w_gate'], 'm_ffn1_w_up': out['m_ffn1_w_up'], 'm_ffn1_w_down': out['m_ffn1_w_down'], 'm_mix_norm': out['m_mix_norm'], 'm_w_in': out['m_w_in'], 'm_w_out': out['m_w_out'], 'm_rg_conv_w': out['m_rg_conv_w'], 'm_rg_conv_b': out['m_rg_conv_b'], 'm_rg_gate_a_w': out['m_rg_gate_a_w'], 'm_rg_gate_a_b': out['m_rg_gate_a_b'], 'm_rg_gate_x_w': out['m_rg_gate_x_w'], 'm_rg_gate_x_b': out['m_rg_gate_x_b'], 'm_rg_lambda': out['m_rg_lambda'], 'm_gdn_conv_w': out['m_gdn_conv_w'], 'm_gdn_a_log': out['m_gdn_a_log'], 'm_gdn_dt_bias': out['m_gdn_dt_bias'], 'm_gdn_norm': out['m_gdn_norm'], 'm_ffn2_norm': out['m_ffn2_norm'], 'm_ffn2_w_gate': out['m_ffn2_w_gate'], 'm_ffn2_w_up': out['m_ffn2_w_up'], 'm_ffn2_w_down': out['m_ffn2_w_down'], 'm_final_norm': out['m_final_norm'], 'v_ffn1_norm': out['v_ffn1_norm'], 'v_ffn1_w_gate': out['v_ffn1_w_gate'], 'v_ffn1_w_up': out['v_ffn1_w_up'], 'v_ffn1_w_down': out['v_ffn1_w_down'], 'v_mix_norm': out['v_mix_norm'], 'v_w_in': out['v_w_in'], 'v_w_out': out['v_w_out'], 'v_rg_conv_w': out['v_rg_conv_w'], 'v_rg_conv_b': out['v_rg_conv_b'], 'v_rg_gate_a_w': out['v_rg_gate_a_w'], 'v_rg_gate_a_b': out['v_rg_gate_a_b'], 'v_rg_gate_x_w': out['v_rg_gate_x_w'], 'v_rg_gate_x_b': out['v_rg_gate_x_b'], 'v_rg_lambda': out['v_rg_lambda'], 'v_gdn_conv_w': out['v_gdn_conv_w'], 'v_gdn_a_log': out['v_gdn_a_log'], 'v_gdn_dt_bias': out['v_gdn_dt_bias'], 'v_gdn_norm': out['v_gdn_norm'], 'v_ffn2_norm': out['v_ffn2_norm'], 'v_ffn2_w_gate': out['v_ffn2_w_gate'], 'v_ffn2_w_up': out['v_ffn2_w_up'], 'v_ffn2_w_down': out['v_ffn2_w_down'], 'v_final_norm': out['v_final_norm']}


def _loss(weights, diff, rest, loss_target):
    with _jax.named_scope("forward"):
        args = {**rest, TWIN_DIFF_INPUT: diff, **{k: w.astype(_WEIGHT_DTYPES[k]) for k, w in weights.items()}}
        y = _forward(args)
    with _jax.named_scope("loss_head"):
        err = _jnp.square(y.astype(_jnp.float32) - loss_target)
        return 0.5 * _jnp.sum(_jnp.mean(err, axis=-1)) if err.ndim else 0.5 * err


def _adamw(w, g, m, v):
    m = ADAM_B1 * m + (1.0 - ADAM_B1) * g
    v = ADAM_B2 * v + (1.0 - ADAM_B2) * _jnp.square(g)
    m_hat = m / (1.0 - ADAM_B1 ** ADAM_STEP)
    v_hat = v / (1.0 - ADAM_B2 ** ADAM_STEP)
    delta = -ADAM_LR * (m_hat / (_jnp.sqrt(v_hat) + ADAM_EPS) + ADAM_WD * w)
    return delta, m, v


def reference(x, ffn1_norm, ffn1_w_gate, ffn1_w_up, ffn1_w_down, mix_norm, w_in, w_out, rg_conv_w, rg_conv_b, rg_gate_a_w, rg_gate_a_b, rg_gate_x_w, rg_gate_x_b, rg_lambda, gdn_conv_w, gdn_a_log, gdn_dt_bias, gdn_norm, ffn2_norm, ffn2_w_gate, ffn2_w_up, ffn2_w_down, final_norm, loss_target, m_ffn1_norm, m_ffn1_w_gate, m_ffn1_w_up, m_ffn1_w_down, m_mix_norm, m_w_in, m_w_out, m_rg_conv_w, m_rg_conv_b, m_rg_gate_a_w, m_rg_gate_a_b, m_rg_gate_x_w, m_rg_gate_x_b, m_rg_lambda, m_gdn_conv_w, m_gdn_a_log, m_gdn_dt_bias, m_gdn_norm, m_ffn2_norm, m_ffn2_w_gate, m_ffn2_w_up, m_ffn2_w_down, m_final_norm, v_ffn1_norm, v_ffn1_w_gate, v_ffn1_w_up, v_ffn1_w_down, v_mix_norm, v_w_in, v_w_out, v_rg_conv_w, v_rg_conv_b, v_rg_gate_a_w, v_rg_gate_a_b, v_rg_gate_x_w, v_rg_gate_x_b, v_rg_lambda, v_gdn_conv_w, v_gdn_a_log, v_gdn_dt_bias, v_gdn_norm, v_ffn2_norm, v_ffn2_w_gate, v_ffn2_w_up, v_ffn2_w_down, v_final_norm):
    given = dict(x=x, ffn1_norm=ffn1_norm, ffn1_w_gate=ffn1_w_gate, ffn1_w_up=ffn1_w_up, ffn1_w_down=ffn1_w_down, mix_norm=mix_norm, w_in=w_in, w_out=w_out, rg_conv_w=rg_conv_w, rg_conv_b=rg_conv_b, rg_gate_a_w=rg_gate_a_w, rg_gate_a_b=rg_gate_a_b, rg_gate_x_w=rg_gate_x_w, rg_gate_x_b=rg_gate_x_b, rg_lambda=rg_lambda, gdn_conv_w=gdn_conv_w, gdn_a_log=gdn_a_log, gdn_dt_bias=gdn_dt_bias, gdn_norm=gdn_norm, ffn2_norm=ffn2_norm, ffn2_w_gate=ffn2_w_gate, ffn2_w_up=ffn2_w_up, ffn2_w_down=ffn2_w_down, final_norm=final_norm, loss_target=loss_target, m_ffn1_norm=m_ffn1_norm, m_ffn1_w_gate=m_ffn1_w_gate, m_ffn1_w_up=m_ffn1_w_up, m_ffn1_w_down=m_ffn1_w_down, m_mix_norm=m_mix_norm, m_w_in=m_w_in, m_w_out=m_w_out, m_rg_conv_w=m_rg_conv_w, m_rg_conv_b=m_rg_conv_b, m_rg_gate_a_w=m_rg_gate_a_w, m_rg_gate_a_b=m_rg_gate_a_b, m_rg_gate_x_w=m_rg_gate_x_w, m_rg_gate_x_b=m_rg_gate_x_b, m_rg_lambda=m_rg_lambda, m_gdn_conv_w=m_gdn_conv_w, m_gdn_a_log=m_gdn_a_log, m_gdn_dt_bias=m_gdn_dt_bias, m_gdn_norm=m_gdn_norm, m_ffn2_norm=m_ffn2_norm, m_ffn2_w_gate=m_ffn2_w_gate, m_ffn2_w_up=m_ffn2_w_up, m_ffn2_w_down=m_ffn2_w_down, m_final_norm=m_final_norm, v_ffn1_norm=v_ffn1_norm, v_ffn1_w_gate=v_ffn1_w_gate, v_ffn1_w_up=v_ffn1_w_up, v_ffn1_w_down=v_ffn1_w_down, v_mix_norm=v_mix_norm, v_w_in=v_w_in, v_w_out=v_w_out, v_rg_conv_w=v_rg_conv_w, v_rg_conv_b=v_rg_conv_b, v_rg_gate_a_w=v_rg_gate_a_w, v_rg_gate_a_b=v_rg_gate_a_b, v_rg_gate_x_w=v_rg_gate_x_w, v_rg_gate_x_b=v_rg_gate_x_b, v_rg_lambda=v_rg_lambda, v_gdn_conv_w=v_gdn_conv_w, v_gdn_a_log=v_gdn_a_log, v_gdn_dt_bias=v_gdn_dt_bias, v_gdn_norm=v_gdn_norm, v_ffn2_norm=v_ffn2_norm, v_ffn2_w_gate=v_ffn2_w_gate, v_ffn2_w_up=v_ffn2_w_up, v_ffn2_w_down=v_ffn2_w_down, v_final_norm=v_final_norm)
    weights = {n: given[n] for n in TWIN_WEIGHTS}
    shared = {n: given[n] for n in SHARED_INPUTS}
    per_example = {n: given[n] for n in ['x']}
    grad_fn = _jax.value_and_grad(_loss, argnums=(0, 1))

    def one_microbatch(ex, loss_target):
        ex = dict(ex)
        diff = ex.pop(TWIN_DIFF_INPUT)
        return grad_fn(weights, diff, {**shared, **ex}, loss_target)

    if N_MICROBATCH == 1:
        loss, (grad_w, grad_x) = one_microbatch(per_example, given["loss_target"])
    else:
        def body(carry, xs):
            loss_sum, grad_sum = carry
            l_k, (gw_k, gx_k) = one_microbatch(xs[0], xs[1])
            with _jax.named_scope("update"):
                return (loss_sum + l_k, _jax.tree.map(_jnp.add, grad_sum, gw_k)), gx_k

        init = (_jnp.zeros((), _jnp.float32), _jax.tree.map(_jnp.zeros_like, weights))
        (loss, grad_w), grad_x = _jax.lax.scan(body, init, (per_example, given["loss_target"]))
    with _jax.named_scope("update"):
        delta_w, new_m, new_v = {}, {}, {}
        for n in TWIN_WEIGHTS:
            delta_w[n], new_m[n], new_v[n] = _adamw(weights[n], grad_w[n], given["m_" + n], given["v_" + n])
    return (loss, grad_x, *[grad_w[n] for n in TWIN_WEIGHTS], *[delta_w[n] for n in TWIN_WEIGHTS],
            *[new_m[n] for n in TWIN_WEIGHTS], *[new_v[n] for n in TWIN_WEIGHTS])
```

```python
import functools

import jax
import jax.numpy as jnp
from jax import lax
from jax.experimental import pallas as pl
from jax.experimental.pallas import tpu as pltpu

F32 = jnp.float32
BF16 = jnp.bfloat16
EPS = 1e-6
D = 1024
NSH = 4
FSH = 704
RGW = 512
QKVW = 1536
ZW = 512
BAW = 16
BAP = 128
INSH = 772
OUTSH = 256
CHUNK = 64
NH = 4
DH = 128
RG_C = 8.0
VMEM_LIMIT = 52 * 1024 * 1024
PACK_ROWS = 5280
HALF = PACK_ROWS // 2
MESH = pl.DeviceIdType.MESH

ADAM_LR = 0.001
ADAM_B1 = 0.9
ADAM_B2 = 0.999
ADAM_EPS = 1e-08
ADAM_WD = 0.01
ADAM_STEP = 10


def _cparams(n_grid):
    return pltpu.CompilerParams(dimension_semantics=("arbitrary",) * n_grid, vmem_limit_bytes=VMEM_LIMIT)


def _sig(x):
    return 1.0 / (1.0 + jnp.exp(-x))


def _softplus(x):
    return jnp.maximum(x, 0.0) + jnp.log(1.0 + jnp.exp(-jnp.abs(x)))


def _neg_expm1(y):
    series = -y * (1.0 + y * (0.5 + y * (1.0 / 6 + y * (1.0 / 24 + y * (1.0 / 120 + y * (1.0 / 720 + y / 5040))))))
    return jnp.where(y > -0.3, series, 1.0 - jnp.exp(y))


_GELU_C = 0.7978845608028654


def _gelu(x):
    t = jnp.tanh(_GELU_C * (x + 0.044715 * x * x * x))
    return 0.5 * x * (1.0 + t)


def _gelu_grad(x):
    t = jnp.tanh(_GELU_C * (x + 0.044715 * x * x * x))
    return 0.5 * (1.0 + t) + 0.5 * x * (1.0 - t * t) * _GELU_C * (1.0 + 3 * 0.044715 * x * x)


def _silu_grad(x):
    s = _sig(x)
    return s * (1.0 + x * (1.0 - s))


def _dot(a, b):
    return jnp.dot(a.astype(BF16), b.astype(BF16), preferred_element_type=F32)


def _dot_nt(a, b):
    return lax.dot_general(a.astype(BF16), b.astype(BF16), (((1,), (1,)), ((), ())), preferred_element_type=F32)


def _dot_tn(a, b):
    return lax.dot_general(a.astype(BF16), b.astype(BF16), (((0,), (0,)), ((), ())), preferred_element_type=F32)


_HI = lax.Precision.HIGHEST


def _hp(a, b):
    return jnp.dot(a, b, precision=_HI, preferred_element_type=F32)


def _hp_nt(a, b):
    return lax.dot_general(a, b, (((1,), (1,)), ((), ())), precision=_HI, preferred_element_type=F32)


def _hp_tn(a, b):
    return lax.dot_general(a, b, (((0,), (0,)), ((), ())), precision=_HI, preferred_element_type=F32)


def _rms(xv):
    r = lax.rsqrt(jnp.mean(xv * xv, axis=-1, keepdims=True) + EPS)
    return r, xv * r


def _rms_bwd(dy, xh, r, gain):
    dxh = dy * gain
    return r * (dxh - xh * jnp.mean(dxh * xh, axis=-1, keepdims=True))


def _colsum(v):
    return jnp.sum(v, axis=0, keepdims=True)


def _rows(t, c):
    return pl.BlockSpec((t, c), lambda i: (i, 0))


def _full(shape):
    n = len(shape)
    return pl.BlockSpec(shape, lambda i: (0,) * n)


def _sds(shape, dtype=F32):
    return jax.ShapeDtypeStruct(shape, dtype)


def _ffn_fwd(x, gain, wg, wu, wd, name):
    s = x.shape[0]
    tm = min(512, s)

    def body(x_ref, g_ref, wg_ref, wu_ref, wd_ref, xo_ref, a_ref, b_ref, h_sc, acc):
        j = pl.program_id(1)

        @pl.when(j == 0)
        def _():
            _, xh = _rms(x_ref[...])
            h_sc[...] = (xh * g_ref[...]).astype(BF16)
            acc[...] = jnp.zeros_like(acc)

        h = h_sc[...]
        a = jnp.dot(h, wg_ref[0], preferred_element_type=F32)
        b = jnp.dot(h, wu_ref[0], preferred_element_type=F32)
        a_ref[0] = a
        b_ref[0] = b
        f = (a * _sig(a) * b).astype(BF16)
        acc[...] += jnp.dot(f, wd_ref[0], preferred_element_type=F32)

        @pl.when(j == NSH - 1)
        def _():
            xo_ref[...] = x_ref[...] + 0.5 * acc[...]

    return pl.pallas_call(
        body, name=name, grid=(s // tm, NSH),
        in_specs=[pl.BlockSpec((tm, D), lambda i, j: (i, 0)), pl.BlockSpec((1, D), lambda i, j: (0, 0)),
                  pl.BlockSpec((1, D, FSH), lambda i, j: (j, 0, 0)), pl.BlockSpec((1, D, FSH), lambda i, j: (j, 0, 0)),
                  pl.BlockSpec((1, FSH, D), lambda i, j: (j, 0, 0))],
        out_specs=[pl.BlockSpec((tm, D), lambda i, j: (i, 0)), pl.BlockSpec((1, tm, FSH), lambda i, j: (j, i, 0)),
                   pl.BlockSpec((1, tm, FSH), lambda i, j: (j, i, 0))],
        out_shape=[_sds((s, D)), _sds((NSH, s, FSH)), _sds((NSH, s, FSH))],
        scratch_shapes=[pltpu.VMEM((tm, D), BF16), pltpu.VMEM((tm, D), F32)],
        compiler_params=_cparams(2),
    )(x, gain, wg, wu, wd)


def _ffn_bwd(x, dout, gain, a, b, wg, wu, wd, name):
    s = x.shape[0]
    tm = min(512, s)

    def body(x_ref, d_ref, g_ref, a_ref, b_ref, wg_ref, wu_ref, wd_ref,
             dx_ref, dg_ref, h_ref, do_ref, f_ref, da_ref, db_ref, do_sc, dh_acc):
        i = pl.program_id(0)
        j = pl.program_id(1)

        @pl.when(jnp.logical_and(i == 0, j == 0))
        def _():
            dg_ref[...] = jnp.zeros_like(dg_ref)

        @pl.when(j == 0)
        def _():
            _, xh = _rms(x_ref[...])
            h_ref[...] = (xh * g_ref[...]).astype(BF16)
            do = (0.5 * d_ref[...]).astype(BF16)
            do_sc[...] = do
            do_ref[...] = do
            dh_acc[...] = jnp.zeros_like(dh_acc)

        do = do_sc[...]
        df = _dot_nt(do, wd_ref[0])
        av = a_ref[0]
        bv = b_ref[0]
        sa = _sig(av)
        f_ref[0] = (av * sa * bv).astype(BF16)
        da = (df * bv * sa * (1.0 + av * (1.0 - sa))).astype(BF16)
        db = (df * av * sa).astype(BF16)
        da_ref[0] = da
        db_ref[0] = db
        dh_acc[...] += _dot_nt(da, wg_ref[0]) + _dot_nt(db, wu_ref[0])

        @pl.when(j == NSH - 1)
        def _():
            r, xh = _rms(x_ref[...])
            dh = dh_acc[...]
            dg_ref[...] += _colsum(dh * xh)
            dx_ref[...] = d_ref[...] + _rms_bwd(dh, xh, r, g_ref[...])

    tok = pl.BlockSpec((tm, D), lambda i, j: (i, 0))
    sh = pl.BlockSpec((1, tm, FSH), lambda i, j: (j, i, 0))
    return pl.pallas_call(
        body, name=name, grid=(s // tm, NSH),
        in_specs=[tok, tok, pl.BlockSpec((1, D), lambda i, j: (0, 0)), sh, sh,
                  pl.BlockSpec((1, D, FSH), lambda i, j: (j, 0, 0)), pl.BlockSpec((1, D, FSH), lambda i, j: (j, 0, 0)),
                  pl.BlockSpec((1, FSH, D), lambda i, j: (j, 0, 0))],
        out_specs=[tok, pl.BlockSpec((1, D), lambda i, j: (0, 0)), tok, tok, sh, sh, sh],
        out_shape=[_sds((s, D)), _sds((1, D)), _sds((s, D), BF16), _sds((s, D), BF16),
                   _sds((NSH, s, FSH), BF16), _sds((NSH, s, FSH), BF16), _sds((NSH, s, FSH), BF16)],
        scratch_shapes=[pltpu.VMEM((tm, D), BF16), pltpu.VMEM((tm, D), F32)],
        compiler_params=_cparams(2),
    )(x, dout, gain, a, b, wg, wu, wd)


def _tn(a, b, name):
    a_g = a.ndim == 3
    b_g = b.ndim == 3
    g = a.shape[0] if a_g else (b.shape[0] if b_g else 1)
    s, k = a.shape[-2:]
    n = b.shape[-1]
    ts = min(512, s)

    def body(a_ref, b_ref, o_ref):
        @pl.when(pl.program_id(1) == 0)
        def _():
            o_ref[...] = jnp.zeros_like(o_ref)

        av = a_ref[0] if a_g else a_ref[...]
        bv = b_ref[0] if b_g else b_ref[...]
        o_ref[0] += _dot_tn(av, bv)

    a_spec = pl.BlockSpec((1, ts, k), lambda gi, si: (gi, si, 0)) if a_g else pl.BlockSpec((ts, k), lambda gi, si: (si, 0))
    b_spec = pl.BlockSpec((1, ts, n), lambda gi, si: (gi, si, 0)) if b_g else pl.BlockSpec((ts, n), lambda gi, si: (si, 0))
    return pl.pallas_call(
        body, name=name, grid=(g, s // ts), in_specs=[a_spec, b_spec],
        out_specs=pl.BlockSpec((1, k, n), lambda gi, si: (gi, 0, 0)),
        out_shape=_sds((g, k, n)), compiler_params=_cparams(2),
    )(a, b)


_P_WIDTHS = (RGW, RGW, QKVW, ZW, BAP)


def _inproj(x1, gain, ws, name):
    s = x1.shape[0]
    tm = min(256, s)

    def body(x_ref, g_ref, *refs):
        w_refs = refs[:5]
        h_ref = refs[5]
        p_refs = refs[6:]
        _, xh = _rms(x_ref[...])
        h = (xh * g_ref[...]).astype(BF16)
        h_ref[...] = h
        for w_ref, p_ref in zip(w_refs, p_refs):
            p_ref[...] = jnp.dot(h, w_ref[...], preferred_element_type=F32)

    return pl.pallas_call(
        body, name=name, grid=(s // tm,),
        in_specs=[_rows(tm, D), _full((1, D))] + [_full((D, w)) for w in _P_WIDTHS],
        out_specs=[_rows(tm, D)] + [_rows(tm, w) for w in _P_WIDTHS],
        out_shape=[_sds((s, D), BF16)] + [_sds((s, w)) for w in _P_WIDTHS],
        compiler_params=_cparams(1),
    )(x1, gain, *ws)


def _inproj_bwd(x1, dx2, gain, dps, ws, name):
    s = x1.shape[0]
    tm = min(256, s)

    def body(x_ref, d_ref, g_ref, *refs):
        dp_refs = refs[:5]
        w_refs = refs[5:10]
        dx_ref, dg_ref = refs[10:]

        @pl.when(pl.program_id(0) == 0)
        def _():
            dg_ref[...] = jnp.zeros_like(dg_ref)

        dh = jnp.zeros((tm, D), F32)
        for dp_ref, w_ref in zip(dp_refs, w_refs):
            dh = dh + _dot_nt(dp_ref[...], w_ref[...])
        r, xh = _rms(x_ref[...])
        dg_ref[...] += _colsum(dh * xh)
        dx_ref[...] = d_ref[...] + _rms_bwd(dh, xh, r, g_ref[...])

    return pl.pallas_call(
        body, name=name, grid=(s // tm,),
        in_specs=[_rows(tm, D), _rows(tm, D), _full((1, D))] + [_rows(tm, w) for w in _P_WIDTHS]
        + [_full((D, w)) for w in _P_WIDTHS],
        out_specs=[_rows(tm, D), _full((1, D))],
        out_shape=[_sds((s, D)), _sds((1, D))],
        compiler_params=_cparams(1),
    )(x1, dx2, gain, *dps, *ws)


def _halo_specs(s, t, c):
    nb8 = s // 8
    tb = t // 8
    prev = pl.BlockSpec((8, c), lambda i: (jnp.maximum(i * tb - 1, 0), 0))
    nxt = pl.BlockSpec((8, c), lambda i: (jnp.minimum((i + 1) * tb, nb8 - 1), 0))
    return prev, nxt


def _edge_masks(nb):
    i = pl.program_id(0)
    return jnp.where(i > 0, 1.0, 0.0).astype(F32), jnp.where(i < nb - 1, 1.0, 0.0).astype(F32)


def _shifted(xx, off, t):
    n = t + 16
    sh = (-off) % n
    rolled = xx if sh == 0 else pltpu.roll(xx, sh, 0)
    return rolled[8:8 + t]


def _conv(x, w8, bias, name):
    s, c = x.shape
    t = min(256, s)
    nb = s // t

    def body(x_ref, xp_ref, xn_ref, w_ref, b_ref, o_ref):
        pm, nm = _edge_masks(nb)
        for c0 in range(0, c, 512):
            cols = slice(c0, c0 + 512)
            xx = jnp.concatenate([xp_ref[:, cols] * pm, x_ref[:, cols], xn_ref[:, cols] * nm], axis=0)
            acc = jnp.zeros((t, 512), F32) + b_ref[:, cols]
            for j in range(4):
                acc = acc + w_ref[j:j + 1, cols] * _shifted(xx, j - 2, t)
            o_ref[:, cols] = acc

    prev, nxt = _halo_specs(s, t, c)
    return pl.pallas_call(
        body, name=name, grid=(nb,),
        in_specs=[_rows(t, c), prev, nxt, _full((8, c)), _full((1, c))],
        out_specs=_rows(t, c), out_shape=_sds((s, c)), compiler_params=_cparams(1),
    )(x, x, x, w8, bias)


def _conv_bwd(x, dc, w8, name):
    s, c = x.shape
    t = min(256, s)
    nb = s // t

    def body(x_ref, d_ref, dp_ref, dn_ref, w_ref, dx_ref, dw_ref, db_ref):
        @pl.when(pl.program_id(0) == 0)
        def _():
            dw_ref[...] = jnp.zeros_like(dw_ref)
            db_ref[...] = jnp.zeros_like(db_ref)

        pm, nm = _edge_masks(nb)
        for c0 in range(0, c, 512):
            cols = slice(c0, c0 + 512)
            dd = jnp.concatenate([dp_ref[:, cols] * pm, d_ref[:, cols], dn_ref[:, cols] * nm], axis=0)
            xv = x_ref[:, cols]
            acc = jnp.zeros((t, 512), F32)
            for j in range(4):
                dsh = _shifted(dd, 2 - j, t)
                acc = acc + w_ref[j:j + 1, cols] * dsh
                dw_ref[j:j + 1, cols] += _colsum(dsh * xv)
            dx_ref[:, cols] = acc
            db_ref[:, cols] += _colsum(d_ref[:, cols])

    prev, nxt = _halo_specs(s, t, c)
    return pl.pallas_call(
        body, name=name, grid=(nb,),
        in_specs=[_rows(t, c), _rows(t, c), prev, nxt, _full((8, c))],
        out_specs=[_rows(t, c), _full((8, c)), _full((1, c))],
        out_shape=[_sds((s, c)), _sds((8, c)), _sds((1, c))], compiler_params=_cparams(1),
    )(x, dc, dc, dc, w8)


def _shift_rows(x, direction, name):
    s, c = x.shape
    t = min(256, s)
    nb = s // t

    def body(x_ref, xp_ref, xn_ref, o_ref):
        pm, nm = _edge_masks(nb)
        xx = jnp.concatenate([xp_ref[...] * pm, x_ref[...], xn_ref[...] * nm], axis=0)
        o_ref[...] = _shifted(xx, direction, t)

    prev, nxt = _halo_specs(s, t, c)
    return pl.pallas_call(
        body, name=name, grid=(nb,), in_specs=[_rows(t, c), prev, nxt],
        out_specs=_rows(t, c), out_shape=_sds((s, c)), compiler_params=_cparams(1),
    )(x, x, x)


def _rg_gates(xc, pre, lam_row):
    sp8 = RG_C * _softplus(-lam_row)
    out = []
    for d in range(2):
        r = _sig(pre[:, RGW * d:RGW * (d + 1)])
        gi = _sig(pre[:, 2 * RGW + RGW * d:2 * RGW + RGW * (d + 1)])
        la = -r * sp8[:, RGW * d:RGW * (d + 1)]
        a = jnp.exp(la)
        mult = jnp.sqrt(_neg_expm1(2.0 * la))
        out.append((r, gi, a, mult))
    return out


def _mix_prep(c_rg, c_qkv, p_ba, wgates, gbias, lam_row, alog_row, dtb_row, name):
    s = c_rg.shape[0]
    t = min(256, s)

    def body(xc_ref, cq_ref, pc_ref, wg_ref, gb_ref, lam_ref, alog_ref, dtb_ref,
             a0_ref, b0_ref, a1_ref, b1_ref, q_ref, k_ref, v_ref, bg_ref):
        xc = xc_ref[...]
        pre = _dot(xc, wg_ref[...]) + gb_ref[...]
        gates = _rg_gates(xc, pre, lam_ref[...])
        for (r, gi, a, mult), a_ref, b_ref in zip(gates, (a0_ref, a1_ref), (b0_ref, b1_ref)):
            a_ref[...] = a
            b_ref[...] = mult * gi * xc
        cq = cq_ref[...]
        sq = cq * _sig(cq)
        for h in range(NH):
            sl = slice(DH * h, DH * (h + 1))
            qh = sq[:, sl]
            q_ref[:, sl] = qh * lax.rsqrt(jnp.sum(qh * qh, axis=-1, keepdims=True) + EPS) * (DH ** -0.5)
            kh = sq[:, RGW + DH * h:RGW + DH * (h + 1)]
            k_ref[:, sl] = kh * lax.rsqrt(jnp.sum(kh * kh, axis=-1, keepdims=True) + EPS)
        v_ref[...] = sq[:, 2 * RGW:]
        pc = pc_ref[...]
        lane = lax.broadcasted_iota(jnp.int32, pc.shape, 1)
        beta = _sig(pc)
        g = -jnp.exp(alog_ref[...]) * _softplus(pc + dtb_ref[...])
        bg_ref[...] = jnp.where(lane < 8, beta, jnp.where(lane < 16, g, 0.0))

    return pl.pallas_call(
        body, name=name, grid=(s // t,),
        in_specs=[_rows(t, RGW), _rows(t, QKVW), _rows(t, BAP), _full((RGW, 4 * RGW)), _full((1, 4 * RGW)),
                  _full((1, 2 * RGW)), _full((1, BAP)), _full((1, BAP))],
        out_specs=[_rows(t, RGW)] * 7 + [_rows(t, BAP)],
        out_shape=[_sds((s, RGW))] * 7 + [_sds((s, BAP))],
        compiler_params=_cparams(1),
    )(c_rg, c_qkv, p_ba, wgates, gbias, lam_row, alog_row, dtb_row)


def _scan(a, b, reverse, name):
    s, c = a.shape
    t = min(512, s)
    nb = s // t
    ng = t // 8
    idx = (lambda i: (nb - 1 - i, 0)) if reverse else (lambda i: (i, 0))

    def body(a_ref, b_ref, h_ref, carry):
        @pl.when(pl.program_id(0) == 0)
        def _():
            carry[...] = jnp.zeros_like(carry)

        row = lax.broadcasted_iota(jnp.int32, (8, c), 0)

        def group(gi, cv):
            g = (ng - 1 - gi) if reverse else gi
            r0 = pl.multiple_of(g * 8, 8)
            av = a_ref[pl.ds(r0, 8), :]
            bv = b_ref[pl.ds(r0, 8), :]
            for k in (1, 2, 4):
                sh = (8 - k) if reverse else k
                m = (row < 8 - k) if reverse else (row >= k)
                a_s = pltpu.roll(av, sh, 0)
                b_s = pltpu.roll(bv, sh, 0)
                bv = jnp.where(m, av * b_s + bv, bv)
                av = jnp.where(m, av * a_s, av)
            hv = av * cv + bv
            h_ref[pl.ds(r0, 8), :] = hv
            return hv[0:1, :] if reverse else hv[7:8, :]

        carry[0:1, :] = lax.fori_loop(0, ng, group, carry[0:1, :])

    return pl.pallas_call(
        body, name=name, grid=(nb,), in_specs=[pl.BlockSpec((t, c), idx), pl.BlockSpec((t, c), idx)],
        out_specs=pl.BlockSpec((t, c), idx), out_shape=_sds((s, c)),
        scratch_shapes=[pltpu.VMEM((8, c), F32)], compiler_params=_cparams(1),
    )(a, b)


def _gates_bwd(xc, wgates, gbias, lam_row, lam0, lam1, h0s, h1s, name):
    s = xc.shape[0]
    t = min(256, s)

    def body(xc_ref, wg_ref, gb_ref, lam_ref, l0_ref, l1_ref, h0_ref, h1_ref,
             dxc_ref, dpre_ref, xcb_ref, dgb_ref, dlam_ref):
        @pl.when(pl.program_id(0) == 0)
        def _():
            dgb_ref[...] = jnp.zeros_like(dgb_ref)
            dlam_ref[...] = jnp.zeros_like(dlam_ref)

        xv = xc_ref[...]
        pre = _dot(xv, wg_ref[...]) + gb_ref[...]
        lam_row_v = lam_ref[...]
        sp8 = RG_C * _softplus(-lam_row_v)
        dsp_dlam = -RG_C * _sig(-lam_row_v)
        gates = _rg_gates(xv, pre, lam_row_v)
        dxc = jnp.zeros((t, RGW), F32)
        dpre_r = []
        dpre_i = []
        for d, ((r, gi, a, mult), l_ref, h_ref) in enumerate(zip(gates, (l0_ref, l1_ref), (h0_ref, h1_ref))):
            dbb = l_ref[...]
            da = dbb * h_ref[...]
            cs = slice(RGW * d, RGW * (d + 1))
            dmult = dbb * gi * xv
            dgi = dbb * mult * xv
            dxc = dxc + dbb * mult * gi
            dla = da * a - dmult * a * a / mult
            dr = -dla * sp8[:, cs]
            dlam_ref[:, cs] += _colsum(-dla * r) * dsp_dlam[:, cs]
            dpre_r.append(dr * r * (1.0 - r))
            dpre_i.append(dgi * gi * (1.0 - gi))
        dpre = jnp.concatenate(dpre_r + dpre_i, axis=1)
        dgb_ref[...] += _colsum(dpre)
        dpre_b = dpre.astype(BF16)
        dpre_ref[...] = dpre_b
        xcb_ref[...] = xv.astype(BF16)
        dxc_ref[...] = dxc + _dot_nt(dpre_b, wg_ref[...])

    return pl.pallas_call(
        body, name=name, grid=(s // t,),
        in_specs=[_rows(t, RGW), _full((RGW, 4 * RGW)), _full((1, 4 * RGW)), _full((1, 2 * RGW))] + [_rows(t, RGW)] * 4,
        out_specs=[_rows(t, RGW), _rows(t, 4 * RGW), _rows(t, RGW), _full((1, 4 * RGW)), _full((1, 2 * RGW))],
        out_shape=[_sds((s, RGW)), _sds((s, 4 * RGW), BF16), _sds((s, RGW), BF16), _sds((1, 4 * RGW)), _sds((1, 2 * RGW))],
        compiler_params=_cparams(1),
    )(xc, wgates, gbias, lam_row, lam0, lam1, h0s, h1s)


class _GdnMasks:
    def __init__(self, d):
        ri = lax.broadcasted_iota(jnp.int32, (CHUNK, CHUNK), 0)
        ci = lax.broadcasted_iota(jnp.int32, (CHUNK, CHUNK), 1)
        self.incl = (ri >= ci) if d == 0 else (ri <= ci)
        self.strict = (ri > ci) if d == 0 else (ri < ci)
        b16 = jnp.right_shift(ri, 4) == jnp.right_shift(ci, 4)
        b32 = jnp.right_shift(ri, 5) == jnp.right_shift(ci, 5)
        self.diag16 = b16
        self.off32 = jnp.logical_and(b32, jnp.logical_not(b16))
        self.off64 = jnp.logical_not(b32)
        self.eye = jnp.where(ri == ci, 1.0, 0.0).astype(F32)
        self.tri = jnp.where(self.incl, 1.0, 0.0).astype(F32)
        self.last = CHUNK - 1 if d == 0 else 0


def _tri_inv(lmat, m):
    n = jnp.where(m.diag16, lmat, 0.0)
    p = m.eye - n
    q = _hp(n, n)
    p = _hp(p, m.eye + q)
    q = _hp(q, q)
    p = _hp(p, m.eye + q)
    q = _hp(q, q)
    p = _hp(p, m.eye + q)
    p = p - _hp(_hp(p, jnp.where(m.off32, lmat, 0.0)), p)
    p = p - _hp(_hp(p, jnp.where(m.off64, lmat, 0.0)), p)
    return p


class _GdnHead:
    def __init__(self, qh, kh, vh, bg, gcs, gcs_t, d, h, m):
        cb = 4 * d + h
        cg = 8 + 4 * d + h
        self.q, self.k, self.v = qh, kh, vh
        self.beta = bg[:, cb:cb + 1]
        gcol = gcs[:, cg:cg + 1]
        grow = gcs_t[cg:cg + 1, :]
        gl = gcs[m.last:m.last + 1, cg:cg + 1]
        self.decay = jnp.exp(jnp.where(m.incl, gcol - grow, -1e30))
        self.kb = kh * self.beta
        self.vb = vh * self.beta
        self.a0 = _dot_nt(self.kb, kh)
        self.q0 = _dot_nt(qh, kh)
        self.lmat = jnp.where(m.strict, self.a0 * self.decay, 0.0)
        self.attn = self.q0 * self.decay
        self.eg = jnp.exp(gcol)
        self.ek = jnp.exp(gl - gcol)
        self.cd = jnp.exp(gl)
        self.kg = self.kb * self.eg
        self.qd = qh * self.eg
        self.kd = kh * self.ek


def _gdn_specs(s, d):
    n = s // CHUNK
    ci = (lambda i: i) if d == 0 else (lambda i: n - 1 - i)
    tok = pl.BlockSpec((CHUNK, NH * DH), lambda i: (ci(i), 0))
    bg = pl.BlockSpec((CHUNK, BAP), lambda i: (ci(i), 0))
    tsp = pl.BlockSpec((1, NH, CHUNK, CHUNK), lambda i: (ci(i), 0, 0, 0))
    ssp = pl.BlockSpec((1, NH, DH, DH), lambda i: (ci(i), 0, 0, 0))
    return n, tok, bg, tsp, ssp


def _gdn_fwd(q, k, v, bg, d, name):
    s = q.shape[0]
    n, tok, bgs, tsp, ssp = _gdn_specs(s, d)

    def body(q_ref, k_ref, v_ref, bg_ref, o_ref, t_ref, s_ref, st):
        @pl.when(pl.program_id(0) == 0)
        def _():
            st[...] = jnp.zeros_like(st)

        m = _GdnMasks(d)
        bgv = bg_ref[...]
        gcs = _hp(m.tri, bgv)
        gcs_t = lax.dot_general(bgv, m.tri, (((0,), (1,)), ((), ())), precision=_HI, preferred_element_type=F32)
        for h in range(NH):
            sl = slice(DH * h, DH * (h + 1))
            c = _GdnHead(q_ref[:, sl], k_ref[:, sl], v_ref[:, sl], bgv, gcs, gcs_t, d, h, m)
            tm = _tri_inv(c.lmat, m)
            u = _dot(tm, c.vb)
            w = _dot(tm, c.kg)
            sh = st[h]
            vn = u - _dot(w, sh)
            o_ref[:, sl] = _dot(c.qd, sh) + _dot(c.attn, vn)
            s_ref[0, h] = sh
            t_ref[0, h] = tm
            st[h] = sh * c.cd + _dot_tn(c.kd, vn)

    return pl.pallas_call(
        body, name=name, grid=(n,), in_specs=[tok, tok, tok, bgs], out_specs=[tok, tsp, ssp],
        out_shape=[_sds((s, NH * DH)), _sds((n, NH, CHUNK, CHUNK)), _sds((n, NH, DH, DH))],
        scratch_shapes=[pltpu.VMEM((NH, DH, DH), F32)], compiler_params=_cparams(1),
    )(q, k, v, bg)


def _gdn_bwd(q, k, v, bg, tmat, states, do, d, name):
    s = q.shape[0]
    n, tok, bgs, tsp, ssp = _gdn_specs(s, 1 - d)

    def body(q_ref, k_ref, v_ref, bg_ref, t_ref, s_ref, do_ref, dq_ref, dk_ref, dv_ref, dbg_ref, dst):
        @pl.when(pl.program_id(0) == 0)
        def _():
            dst[...] = jnp.zeros_like(dst)

        m = _GdnMasks(d)
        bgv = bg_ref[...]
        gcs = _hp(m.tri, bgv)
        gcs_t = lax.dot_general(bgv, m.tri, (((0,), (1,)), ((), ())), precision=_HI, preferred_element_type=F32)
        lane = lax.broadcasted_iota(jnp.int32, (CHUNK, BAP), 1)
        rowi = lax.broadcasted_iota(jnp.int32, (CHUNK, 1), 0)
        ones = jnp.ones((CHUNK, DH), F32)
        dbg = jnp.zeros((CHUNK, BAP), F32)
        for h in range(NH):
            sl = slice(DH * h, DH * (h + 1))
            c = _GdnHead(q_ref[:, sl], k_ref[:, sl], v_ref[:, sl], bgv, gcs, gcs_t, d, h, m)
            tm = t_ref[0, h]
            sh = s_ref[0, h]
            dsn = dst[h]
            doh = do_ref[:, sl]
            u = _dot(tm, c.vb)
            w = _dot(tm, c.kg)
            vn = u - _dot(w, sh)
            d_vn = _dot_tn(c.attn, doh) + _dot(c.kd, dsn)
            d_attn = jnp.where(m.incl, _dot_nt(doh, vn), 0.0)
            d_qd = _dot_nt(doh, sh)
            d_kd = _dot_nt(vn, dsn)
            d_cd = jnp.sum(jnp.sum(sh * dsn, axis=1, keepdims=True), axis=0, keepdims=True)
            dst[h] = c.cd * dsn + _dot_tn(c.qd, doh) - _dot_tn(w, d_vn)
            d_w = -_dot_nt(d_vn, sh)
            d_t = _dot_nt(d_vn, c.vb) + _dot_nt(d_w, c.kg)
            d_vb = _dot_tn(tm, d_vn)
            d_kg = _dot_tn(tm, d_w)
            d_l = jnp.where(m.strict, -_hp_nt(_hp_tn(tm, d_t), tm), 0.0)
            d_a0 = d_l * c.decay
            d_q0 = d_attn * c.decay
            e = (d_l * c.a0 + d_attn * c.q0) * c.decay
            d_kb = _dot(d_a0, c.k) + d_kg * c.eg
            dk_ref[:, sl] = _dot_tn(d_a0, c.kb) + _dot_tn(d_q0, c.q) + d_kd * c.ek + d_kb * c.beta
            dq_ref[:, sl] = _dot(d_q0, c.k) + d_qd * c.eg
            dv_ref[:, sl] = d_vb * c.beta
            s_kd = jnp.sum(d_kd * c.kd, axis=1, keepdims=True)
            d_gc = (jnp.sum(d_kg * c.kg, axis=1, keepdims=True) + jnp.sum(d_qd * c.qd, axis=1, keepdims=True) - s_kd
                    + jnp.sum(e, axis=1, keepdims=True) - _hp_tn(e, ones)[:, 0:1])
            d_gl = jnp.sum(s_kd, axis=0, keepdims=True) + d_cd * c.cd
            d_gc = d_gc + jnp.where(rowi == m.last, d_gl, 0.0)
            d_g = _hp_tn(m.tri, d_gc * ones)[:, 0:1]
            d_beta = jnp.sum(d_kb * c.k, axis=1, keepdims=True) + jnp.sum(d_vb * c.v, axis=1, keepdims=True)
            dbg = dbg + jnp.where(lane == 4 * d + h, d_beta, 0.0) + jnp.where(lane == 8 + 4 * d + h, d_g, 0.0)
        dbg_ref[...] = dbg

    return pl.pallas_call(
        body, name=name, grid=(n,), in_specs=[tok, tok, tok, bgs, tsp, ssp, tok], out_specs=[tok, tok, tok, bgs],
        out_shape=[_sds((s, NH * DH))] * 3 + [_sds((s, BAP))],
        scratch_shapes=[pltpu.VMEM((NH, DH, DH), F32)], compiler_params=_cparams(1),
    )(q, k, v, bg, tmat, states, do)


def _prep_bwd(c_qkv, p_ba, alog_row, dtb_row, dqs, dks, dvs, dbgs, name):
    s = c_qkv.shape[0]
    t = min(256, s)

    def body(cq_ref, pc_ref, alog_ref, dtb_ref, dq0, dq1, dk0, dk1, dv0, dv1, dbg0, dbg1,
             dcq_ref, dpc_ref, dalog_ref, ddtb_ref):
        @pl.when(pl.program_id(0) == 0)
        def _():
            dalog_ref[...] = jnp.zeros_like(dalog_ref)
            ddtb_ref[...] = jnp.zeros_like(ddtb_ref)

        cq = cq_ref[...]
        sq = cq * _sig(cq)
        sg = _silu_grad(cq)
        for h in range(NH):
            sl = slice(DH * h, DH * (h + 1))
            for off, d0, d1, scale in ((0, dq0, dq1, DH ** -0.5), (RGW, dk0, dk1, 1.0)):
                csl = slice(off + DH * h, off + DH * (h + 1))
                xh = sq[:, csl]
                nrm = lax.rsqrt(jnp.sum(xh * xh, axis=-1, keepdims=True) + EPS)
                y = xh * nrm
                dy = (d0[:, sl] + d1[:, sl]) * scale
                dcq_ref[:, csl] = nrm * (dy - y * jnp.sum(dy * y, axis=-1, keepdims=True)) * sg[:, csl]
        dcq_ref[:, 2 * RGW:] = (dv0[...] + dv1[...]) * sg[:, 2 * RGW:]
        pc = pc_ref[...]
        lane = lax.broadcasted_iota(jnp.int32, pc.shape, 1)
        dbg = dbg0[...] + dbg1[...]
        beta = _sig(pc)
        ea = jnp.exp(alog_ref[...])
        z = pc + dtb_ref[...]
        g = -ea * _softplus(z)
        is_g = jnp.logical_and(lane >= 8, lane < 16)
        d_alpha = jnp.where(is_g, dbg * (-ea) * _sig(z), 0.0)
        dpc_ref[...] = jnp.where(lane < 8, dbg * beta * (1.0 - beta), d_alpha)
        dalog_ref[...] += _colsum(jnp.where(is_g, dbg * g, 0.0))
        ddtb_ref[...] += _colsum(d_alpha)

    return pl.pallas_call(
        body, name=name, grid=(s // t,),
        in_specs=[_rows(t, QKVW), _rows(t, BAP), _full((1, BAP)), _full((1, BAP))] + [_rows(t, NH * DH)] * 6 + [_rows(t, BAP)] * 2,
        out_specs=[_rows(t, QKVW), _rows(t, BAP), _full((1, BAP)), _full((1, BAP))],
        out_shape=[_sds((s, QKVW)), _sds((s, BAP)), _sds((1, BAP)), _sds((1, BAP))],
        compiler_params=_cparams(1),
    )(c_qkv, p_ba, alog_row, dtb_row, dqs[0], dqs[1], dks[0], dks[1], dvs[0], dvs[1], dbgs[0], dbgs[1])


def _mix_out_values(hf, hb, gate, of, ob, z, gn):
    hr = hf + hb
    y_rg = hr * _gelu(gate)
    osum = of + ob
    parts = []
    for h in range(NH):
        sl = slice(DH * h, DH * (h + 1))
        oh = osum[:, sl]
        r, ohat = _rms(oh)
        zh = z[:, sl]
        parts.append((r, ohat, zh))
    y_gdn = jnp.concatenate([ohat * gn * (zh * _sig(zh)) for (r, ohat, zh) in parts], axis=1)
    return hr, y_rg, y_gdn, parts


def _outproj(x1, hf, hb, gate, of, ob, z, gn, wout, name):
    s = x1.shape[0]
    t = min(256, s)

    def body(x_ref, hf_ref, hb_ref, gate_ref, of_ref, ob_ref, z_ref, gn_ref, w_ref, xo_ref, y_ref):
        _, y_rg, y_gdn, _ = _mix_out_values(hf_ref[...], hb_ref[...], gate_ref[...], of_ref[...], ob_ref[...],
                                            z_ref[...], gn_ref[...])
        y = jnp.concatenate([y_rg, y_gdn], axis=1).astype(BF16)
        y_ref[...] = y
        xo_ref[...] = x_ref[...] + jnp.dot(y, w_ref[...], preferred_element_type=F32)

    return pl.pallas_call(
        body, name=name, grid=(s // t,),
        in_specs=[_rows(t, D)] + [_rows(t, RGW)] * 6 + [_full((1, DH)), _full((D, D))],
        out_specs=[_rows(t, D), _rows(t, D)], out_shape=[_sds((s, D)), _sds((s, D), BF16)],
        compiler_params=_cparams(1),
    )(x1, hf, hb, gate, of, ob, z, gn, wout)


def _outproj_bwd(dx2, hf, hb, gate, of, ob, z, gn, wout, name):
    s = dx2.shape[0]
    t = min(256, s)

    def body(d_ref, hf_ref, hb_ref, gate_ref, of_ref, ob_ref, z_ref, gn_ref, w_ref,
             dhr_ref, dgate_ref, dos_ref, dz_ref, dgn_ref, db_ref):
        @pl.when(pl.program_id(0) == 0)
        def _():
            dgn_ref[...] = jnp.zeros_like(dgn_ref)

        gate = gate_ref[...]
        gn_v = gn_ref[...]
        hr, _, _, parts = _mix_out_values(hf_ref[...], hb_ref[...], gate, of_ref[...], ob_ref[...], z_ref[...], gn_v)
        dbf = d_ref[...].astype(BF16)
        db_ref[...] = dbf
        dy = _dot_nt(dbf, w_ref[...])
        dyr = dy[:, :RGW]
        dhr_ref[...] = dyr * _gelu(gate)
        dgate_ref[...] = dyr * hr * _gelu_grad(gate)
        dgn = jnp.zeros((1, DH), F32)
        for h, (r, ohat, zh) in enumerate(parts):
            sl = slice(DH * h, DH * (h + 1))
            dyh = dy[:, RGW + DH * h:RGW + DH * (h + 1)]
            sz = zh * _sig(zh)
            dn = dyh * sz
            dz_ref[:, sl] = dyh * ohat * gn_v * _silu_grad(zh)
            dgn = dgn + _colsum(dn * ohat)
            dos_ref[:, sl] = _rms_bwd(dn, ohat, r, gn_v)
        dgn_ref[...] += dgn

    return pl.pallas_call(
        body, name=name, grid=(s // t,),
        in_specs=[_rows(t, D)] + [_rows(t, RGW)] * 6 + [_full((1, DH)), _full((D, D))],
        out_specs=[_rows(t, RGW)] * 4 + [_full((1, DH)), _rows(t, D)],
        out_shape=[_sds((s, RGW))] * 4 + [_sds((1, DH)), _sds((s, D), BF16)],
        compiler_params=_cparams(1),
    )(dx2, hf, hb, gate, of, ob, z, gn, wout)


def _loss_head(x3, target, gain, name):
    s = x3.shape[0]
    t = min(256, s)

    def body(x_ref, t_ref, g_ref, dx_ref, loss_ref, dg_ref):
        @pl.when(pl.program_id(0) == 0)
        def _():
            loss_ref[...] = jnp.zeros_like(loss_ref)
            dg_ref[...] = jnp.zeros_like(dg_ref)

        r, xh = _rms(x_ref[...])
        gv = g_ref[...]
        err = xh * gv - t_ref[...]
        per_tok = jnp.mean(err * err, axis=-1, keepdims=True)
        loss_ref[...] += 0.5 * jnp.sum(per_tok, axis=0, keepdims=True)
        dy = err * (1.0 / D)
        dg_ref[...] += _colsum(dy * xh)
        dx_ref[...] = _rms_bwd(dy, xh, r, gv)

    return pl.pallas_call(
        body, name=name, grid=(s // t,), in_specs=[_rows(t, D), _rows(t, D), _full((1, D))],
        out_specs=[_rows(t, D), _full((8, 128)), _full((1, D))],
        out_shape=[_sds((s, D)), _sds((8, 128)), _sds((1, D))], compiler_params=_cparams(1),
    )(x3, target, gain)


def _adamw(w, g, m, v, name):
    r, c = w.shape
    tr = r
    while tr * c * 4 > (1 << 20) and tr % 16 == 0:
        tr //= 2

    def body(w_ref, g_ref, m_ref, v_ref, d_ref, nm_ref, nv_ref):
        gv = g_ref[...]
        mn = ADAM_B1 * m_ref[...] + (1.0 - ADAM_B1) * gv
        vn = ADAM_B2 * v_ref[...] + (1.0 - ADAM_B2) * (gv * gv)
        m_hat = mn / (1.0 - ADAM_B1 ** ADAM_STEP)
        v_hat = vn / (1.0 - ADAM_B2 ** ADAM_STEP)
        d_ref[...] = -ADAM_LR * (m_hat / (jnp.sqrt(v_hat) + ADAM_EPS) + ADAM_WD * w_ref[...])
        nm_ref[...] = mn
        nv_ref[...] = vn

    return pl.pallas_call(
        body, name=name, grid=(r // tr,), in_specs=[_rows(tr, c)] * 4, out_specs=[_rows(tr, c)] * 3,
        out_shape=[_sds((r, c))] * 3, compiler_params=_cparams(1),
    )(w, g, m, v)


def _mesh_pos():
    return lax.axis_index("x"), lax.axis_index("y"), lax.axis_index("c")


def _other_chips(x, y):
    return [(1 - x, y), (x, 1 - y), (1 - x, 1 - y)]


def _all_gather(block_of, out_shape, in_space, name):
    def body(x_ref, out_ref, send_sems, recv_sems, local_sem):
        x, y, c = _mesh_pos()
        me, sibling = (x, y, c), (x, y, 1 - c)
        chips = _other_chips(x, y)
        mine_src = block_of(x_ref, c)

        def slot(px, py, pc):
            return out_ref.at[4 * px + 2 * py + pc]

        def copy(k, block, to, src=None):
            return pltpu.make_async_remote_copy(
                src_ref=slot(*block) if src is None else src, dst_ref=slot(*block),
                send_sem=send_sems.at[k], recv_sem=recv_sems.at[k], device_id=to, device_id_type=MESH)

        mine = pltpu.make_async_copy(mine_src, slot(*me), local_sem)
        mine.start()
        first = [copy(0, me, sibling, src=mine_src)]
        first += [copy(1 + j, me, (*chip, c), src=mine_src) for j, chip in enumerate(chips)]
        for cp in first:
            cp.start()
        passed = [copy(4 + j, (*chip, c), sibling) for j, chip in enumerate(chips)]
        for j, chip in enumerate(chips):
            copy(1 + j, (*chip, c), me).wait_recv()
            passed[j].start()
        copy(0, sibling, me).wait_recv()
        for j, chip in enumerate(chips):
            copy(4 + j, (*chip, 1 - c), me).wait_recv()
        for cp in first + passed:
            cp.wait_send()
        mine.wait()

    return pl.pallas_call(
        body, name=name, out_shape=out_shape,
        in_specs=[pl.BlockSpec(memory_space=in_space)], out_specs=pl.BlockSpec(memory_space=in_space),
        scratch_shapes=[pltpu.SemaphoreType.DMA((7,)), pltpu.SemaphoreType.DMA((7,)), pltpu.SemaphoreType.DMA],
    )


def _gather_weights(wpack):
    def block_of(x_ref, c):
        return x_ref.at[pl.ds(pl.multiple_of(c * HALF, 16), HALF), :]

    out = _all_gather(block_of, _sds((8, HALF, D), BF16), pltpu.HBM, "gather_weights")(wpack)
    return out.reshape(NSH, PACK_ROWS, D)


def _gather_small(block, name):
    r, c = block.shape
    return _all_gather(lambda x_ref, c_: x_ref, _sds((8, r, c)), pltpu.VMEM, name)(block)


def _sibling_exchange(gpack):
    def body(g_ref, land_ref, send_sems, recv_sems):
        x, y, c = _mesh_pos()
        copies = [pltpu.make_async_remote_copy(
            src_ref=g_ref.at[s, 1 - c], dst_ref=land_ref.at[s], send_sem=send_sems.at[s], recv_sem=recv_sems.at[s],
            device_id=(x, y, 1 - c), device_id_type=MESH) for s in range(NSH)]
        for cp in copies:
            cp.start()
        for cp in copies:
            cp.wait()

    return pl.pallas_call(
        body, name="grad_sibling_exchange", out_shape=_sds((NSH, HALF, D)),
        in_specs=[pl.BlockSpec(memory_space=pltpu.HBM)], out_specs=pl.BlockSpec(memory_space=pltpu.HBM),
        scratch_shapes=[pltpu.SemaphoreType.DMA((NSH,)), pltpu.SemaphoreType.DMA((NSH,))],
    )(gpack)


def _chip_sum(gpack, land, c_arr):
    th = HALF // 5

    def body(c_ref, g_ref, l_ref, o_ref):
        o_ref[0] = (g_ref[0, 0] + l_ref[0]).astype(BF16)

    return pl.pallas_call(
        body, name="grad_chip_sum", out_shape=_sds((NSH, HALF, D), BF16),
        grid_spec=pltpu.PrefetchScalarGridSpec(
            num_scalar_prefetch=1, grid=(NSH, HALF // th),
            in_specs=[pl.BlockSpec((1, 1, th, D), lambda s, i, c_ref: (s, c_ref[0], i, 0)),
                      pl.BlockSpec((1, th, D), lambda s, i, c_ref: (s, i, 0))],
            out_specs=pl.BlockSpec((1, th, D), lambda s, i, c_ref: (s, i, 0))),
        compiler_params=_cparams(2),
    )(c_arr, gpack, land)


def _chip_scatter(part):
    def body(p_ref, land_ref, send_sems, recv_sems, local_sem):
        x, y, c = _mesh_pos()
        my_chip = 2 * x + y
        mine = pltpu.make_async_copy(p_ref.at[my_chip], land_ref.at[my_chip], local_sem)
        mine.start()
        copies = [pltpu.make_async_remote_copy(
            src_ref=p_ref.at[2 * px + py], dst_ref=land_ref.at[my_chip], send_sem=send_sems.at[j], recv_sem=recv_sems.at[j],
            device_id=(px, py, c), device_id_type=MESH) for j, (px, py) in enumerate(_other_chips(x, y))]
        for cp in copies:
            cp.start()
        for cp in copies:
            cp.wait()
        mine.wait()

    return pl.pallas_call(
        body, name="grad_chip_scatter", out_shape=_sds((NSH, HALF, D), BF16),
        in_specs=[pl.BlockSpec(memory_space=pltpu.HBM)], out_specs=pl.BlockSpec(memory_space=pltpu.HBM),
        scratch_shapes=[pltpu.SemaphoreType.DMA((3,)), pltpu.SemaphoreType.DMA((3,)), pltpu.SemaphoreType.DMA],
    )(part)


def _sum_slots(land, name):
    k, r, c = land.shape
    tr = r // 5 if r % 40 == 0 else r

    def body(l_ref, o_ref):
        acc = l_ref[0].astype(F32)
        for i in range(1, k):
            acc = acc + l_ref[i].astype(F32)
        o_ref[...] = acc

    return pl.pallas_call(
        body, name=name, grid=(r // tr,), in_specs=[pl.BlockSpec((k, tr, c), lambda i: (0, i, 0))],
        out_specs=_rows(tr, c), out_shape=_sds((r, c)), compiler_params=_cparams(1),
    )(land)


def _sibling_swap(half):
    def body(h_ref, out_ref, send_sem, recv_sem, local_sem):
        x, y, c = _mesh_pos()
        mine = pltpu.make_async_copy(h_ref, out_ref.at[c], local_sem)
        mine.start()
        cp = pltpu.make_async_remote_copy(src_ref=h_ref, dst_ref=out_ref.at[c], send_sem=send_sem, recv_sem=recv_sem,
                                          device_id=(x, y, 1 - c), device_id_type=MESH)
        cp.start()
        cp.wait()
        mine.wait()

    return pl.pallas_call(
        body, name="grad_sibling_swap", out_shape=_sds((2, HALF, D)),
        in_specs=[pl.BlockSpec(memory_space=pltpu.HBM)], out_specs=pl.BlockSpec(memory_space=pltpu.HBM),
        scratch_shapes=[pltpu.SemaphoreType.DMA, pltpu.SemaphoreType.DMA, pltpu.SemaphoreType.DMA],
    )(half)


_FFN_N = D * FSH
_PACK_SIZES = (_FFN_N, _FFN_N, _FFN_N, D * INSH, OUTSH * D, _FFN_N, _FFN_N, _FFN_N)
_PACK_SHAPES = ((D, FSH), (D, FSH), (FSH, D), (D, INSH), (OUTSH, D), (D, FSH), (D, FSH), (FSH, D))
_PACK_PAD = PACK_ROWS * D - sum(_PACK_SIZES)


def _pack(pieces, dtype):
    flat = [p.reshape(-1).astype(dtype) for p in pieces] + [jnp.zeros((_PACK_PAD,), dtype)]
    return jnp.concatenate(flat).reshape(PACK_ROWS, D)


def _unpack(flat2d, lead=()):
    flat = flat2d.reshape(lead + (PACK_ROWS * D,))
    out, off = [], 0
    for n, shp in zip(_PACK_SIZES, _PACK_SHAPES):
        out.append(flat[..., off:off + n].reshape(lead + shp))
        off += n
    return out


def _pad_rows(v, width):
    flat = v.reshape(-1)
    rows = -(-flat.shape[0] // width)
    rows = -(-rows // 8) * 8
    return jnp.pad(flat, (0, rows * width - flat.shape[0])).reshape(rows, width)


def _block_diag(w):
    eye = jnp.eye(8, dtype=w.dtype)
    return (w[:, :, None, :] * eye[:, None, :, None]).reshape(RGW, RGW)


def _diag_blocks(dense):
    r = dense.reshape(8, 64, 8, 64)
    return jnp.stack([r[n, :, n, :] for n in range(8)])


def _lane_row(v8):
    return jnp.zeros((1, BAP), F32).at[0, 8:16].set(v8.reshape(8))


def _local_step(x, target, wts):
    (g1, wg1, wu1, wd1, gmix, w_in_groups, wout, rg_cw8, rg_cb, wgates, gbias, lam_row, gdn_cw8,
     alog_row, dtb_row, gn, g2, wg2, wu2, wd2, gfin) = wts
    s = x.shape[0]

    x1, a1, b1 = _ffn_fwd(x, g1, wg1, wu1, wd1, "ffn1_fwd")
    h2, p_rgx, p_gate, p_qkv, p_z, p_ba = _inproj(x1, gmix, w_in_groups, "in_proj")
    c_rg = _conv(p_rgx, rg_cw8, rg_cb, "rg_conv")
    c_qkv = _conv(p_qkv, gdn_cw8, jnp.zeros((1, QKVW), F32), "gdn_conv")
    a0, bb0, a1s, bb1, q, k, v, bg = _mix_prep(c_rg, c_qkv, p_ba, wgates, gbias, lam_row, alog_row, dtb_row, "mix_prep")
    hf = _scan(a0, bb0, False, "rg_scan_f")
    hb = _scan(a1s, bb1, True, "rg_scan_b")
    of, t0, s0 = _gdn_fwd(q, k, v, bg, 0, "gdn_fwd_f")
    ob, t1, s1 = _gdn_fwd(q, k, v, bg, 1, "gdn_fwd_b")
    x2, ymix = _outproj(x1, hf, hb, p_gate, of, ob, p_z, gn, wout, "out_proj")
    x3, a2, b2 = _ffn_fwd(x2, g2, wg2, wu2, wd2, "ffn2_fwd")
    dx3, loss_blk, d_gfin = _loss_head(x3, target, gfin, "loss_head")

    dx2, d_g2, hb2, dob2, fb2, dab2, dbb2 = _ffn_bwd(x2, dx3, g2, a2, b2, wg2, wu2, wd2, "ffn2_bwd")
    d_wg2 = _tn(hb2, dab2, "ffn2_dwg")
    d_wu2 = _tn(hb2, dbb2, "ffn2_dwu")
    d_wd2 = _tn(fb2, dob2, "ffn2_dwd")

    d_hr, d_gate, d_os, d_z, d_gn, dx2b = _outproj_bwd(dx2, hf, hb, p_gate, of, ob, p_z, gn, wout, "out_proj_bwd")
    d_wout = _tn(ymix, dx2b, "dw_out")[0]

    a0_up = _shift_rows(a0, 1, "shift_a0")
    a1_dn = _shift_rows(a1s, -1, "shift_a1")
    hf_dn = _shift_rows(hf, -1, "shift_hf")
    hb_up = _shift_rows(hb, 1, "shift_hb")
    lam0 = _scan(a0_up, d_hr, True, "rg_scan_f_bwd")
    lam1 = _scan(a1_dn, d_hr, False, "rg_scan_b_bwd")
    d_xc, d_pre, xcb, d_gbias, d_lam = _gates_bwd(c_rg, wgates, gbias, lam_row, lam0, lam1, hf_dn, hb_up, "rg_gates_bwd")
    d_wgates = _tn(xcb, d_pre, "dw_gates")[0]
    d_prgx, d_rgcw8, d_rgcb = _conv_bwd(p_rgx, d_xc, rg_cw8, "rg_conv_bwd")

    dq0, dk0, dv0, dbg0 = _gdn_bwd(q, k, v, bg, t0, s0, d_os, 0, "gdn_bwd_f")
    dq1, dk1, dv1, dbg1 = _gdn_bwd(q, k, v, bg, t1, s1, d_os, 1, "gdn_bwd_b")
    d_cqkv, d_pba, d_alog, d_dtb = _prep_bwd(c_qkv, p_ba, alog_row, dtb_row, (dq0, dq1), (dk0, dk1), (dv0, dv1),
                                             (dbg0, dbg1), "gdn_prep_bwd")
    d_pqkv, d_gdncw8, _ = _conv_bwd(p_qkv, d_cqkv, gdn_cw8, "gdn_conv_bwd")

    dps = (d_prgx, d_gate, d_pqkv, d_z, d_pba)
    dx1, d_gmix = _inproj_bwd(x1, dx2, gmix, dps, w_in_groups, "in_proj_bwd")
    d_win_groups = [_tn(h2, dp, "dw_in_%d" % i)[0] for i, dp in enumerate(dps)]

    gx, d_g1, hb1, dob1, fb1, dab1, dbb1 = _ffn_bwd(x, dx1, g1, a1, b1, wg1, wu1, wd1, "ffn1_bwd")
    d_wg1 = _tn(hb1, dab1, "ffn1_dwg")
    d_wu1 = _tn(hb1, dbb1, "ffn1_dwu")
    d_wd1 = _tn(fb1, dob1, "ffn1_dwd")

    d_win = jnp.concatenate(d_win_groups[:4] + [d_win_groups[4][:, :BAW]], axis=1)
    big = (d_wg1, d_wu1, d_wd1, d_win, d_wout, d_wg2, d_wu2, d_wd2)
    small = dict(
        ffn1_norm=d_g1, mix_norm=d_gmix, rg_conv_w=d_rgcw8[:4], rg_conv_b=d_rgcb,
        rg_gate_a_w=jnp.stack([_diag_blocks(d_wgates[:, RGW * i:RGW * (i + 1)]) for i in (0, 1)]),
        rg_gate_x_w=jnp.stack([_diag_blocks(d_wgates[:, RGW * i:RGW * (i + 1)]) for i in (2, 3)]),
        rg_gate_a_b=d_gbias[0, :2 * RGW].reshape(2, RGW), rg_gate_x_b=d_gbias[0, 2 * RGW:].reshape(2, RGW),
        rg_lambda=d_lam.reshape(2, RGW), gdn_conv_w=d_gdncw8[:4],
        gdn_a_log=d_alog[0, 8:16].reshape(2, NH), gdn_dt_bias=d_dtb[0, 8:16].reshape(2, NH),
        gdn_norm=d_gn, ffn2_norm=d_g2, final_norm=d_gfin)
    return loss_blk, gx, big, small


_SMALL_NAMES = ("ffn1_norm", "mix_norm", "rg_conv_w", "rg_conv_b", "rg_gate_a_w", "rg_gate_a_b", "rg_gate_x_w",
                "rg_gate_x_b", "rg_lambda", "gdn_conv_w", "gdn_a_log", "gdn_dt_bias", "gdn_norm", "ffn2_norm", "final_norm")
_SMALL_SHARDED = dict(rg_conv_w=128, rg_gate_a_b=128, rg_gate_x_b=128, rg_lambda=128, gdn_conv_w=384)
_OUT_ORDER = ("ffn1_norm", "ffn1_w_gate", "ffn1_w_up", "ffn1_w_down", "mix_norm", "w_in", "w_out", "rg_conv_w", "rg_conv_b",
              "rg_gate_a_w", "rg_gate_a_b", "rg_gate_x_w", "rg_gate_x_b", "rg_lambda", "gdn_conv_w", "gdn_a_log",
              "gdn_dt_bias", "gdn_norm", "ffn2_norm", "ffn2_w_gate", "ffn2_w_up", "ffn2_w_down", "final_norm")
_BIG_NAMES = ("ffn1_w_gate", "ffn1_w_up", "ffn1_w_down", "w_in", "w_out", "ffn2_w_gate", "ffn2_w_up", "ffn2_w_down")


def kernel(x, ffn1_norm, ffn1_w_gate, ffn1_w_up, ffn1_w_down, mix_norm, w_in, w_out, rg_conv_w, rg_conv_b, rg_gate_a_w, rg_gate_a_b, rg_gate_x_w, rg_gate_x_b, rg_lambda, gdn_conv_w, gdn_a_log, gdn_dt_bias, gdn_norm, ffn2_norm, ffn2_w_gate, ffn2_w_up, ffn2_w_down, final_norm, loss_target, m_ffn1_norm, m_ffn1_w_gate, m_ffn1_w_up, m_ffn1_w_down, m_mix_norm, m_w_in, m_w_out, m_rg_conv_w, m_rg_conv_b, m_rg_gate_a_w, m_rg_gate_a_b, m_rg_gate_x_w, m_rg_gate_x_b, m_rg_lambda, m_gdn_conv_w, m_gdn_a_log, m_gdn_dt_bias, m_gdn_norm, m_ffn2_norm, m_ffn2_w_gate, m_ffn2_w_up, m_ffn2_w_down, m_final_norm, v_ffn1_norm, v_ffn1_w_gate, v_ffn1_w_up, v_ffn1_w_down, v_mix_norm, v_w_in, v_w_out, v_rg_conv_w, v_rg_conv_b, v_rg_gate_a_w, v_rg_gate_a_b, v_rg_gate_x_w, v_rg_gate_x_b, v_rg_lambda, v_gdn_conv_w, v_gdn_a_log, v_gdn_dt_bias, v_gdn_norm, v_ffn2_norm, v_ffn2_w_gate, v_ffn2_w_up, v_ffn2_w_down, v_final_norm):
    args = dict(locals())
    w = {n: args[n] for n in _OUT_ORDER}
    mom = {n: args["m_" + n] for n in _OUT_ORDER}
    var = {n: args["v_" + n] for n in _OUT_ORDER}
    xi, yi, ci = _mesh_pos()
    shard = 2 * xi + yi

    big_local = [w[n][0] for n in _BIG_NAMES]
    wfull = _unpack(_gather_weights(_pack(big_local, BF16)), (NSH,))
    wg1, wu1, wd1, win_sh, wout_sh, wg2, wu2, wd2 = wfull
    w_in_full = jnp.transpose(win_sh, (1, 0, 2)).reshape(D, NSH * INSH)
    w_out_full = wout_sh.reshape(D, D)
    sm_local = _pad_rows(jnp.concatenate([w[n][0].reshape(-1) for n in _SMALL_SHARDED]), 128)
    sm_all = _gather_small(sm_local, "gather_small_weights")[0::2].reshape(NSH, -1)
    sm_full, off = {}, 0
    for n, wd_ in _SMALL_SHARDED.items():
        rows = w[n].shape[1]
        piece = sm_all[:, off:off + rows * wd_].reshape(NSH, rows, wd_)
        sm_full[n] = jnp.transpose(piece, (1, 0, 2)).reshape(rows, NSH * wd_)
        off += rows * wd_

    w_in_groups = (w_in_full[:, 0:512], w_in_full[:, 512:1024], w_in_full[:, 1024:2560], w_in_full[:, 2560:3072],
                   jnp.pad(w_in_full[:, 3072:3088], ((0, 0), (0, BAP - BAW))))
    wa, wx = rg_gate_a_w[0], rg_gate_x_w[0]
    wgates = jnp.concatenate([_block_diag(wa[0]), _block_diag(wa[1]), _block_diag(wx[0]), _block_diag(wx[1])],
                             axis=1).astype(BF16)
    gbias = jnp.concatenate([sm_full["rg_gate_a_b"].reshape(1, -1), sm_full["rg_gate_x_b"].reshape(1, -1)], axis=1)
    wts = (ffn1_norm, wg1, wu1, wd1, mix_norm, w_in_groups, w_out_full,
           jnp.pad(sm_full["rg_conv_w"], ((0, 4), (0, 0))), rg_conv_b, wgates, gbias, sm_full["rg_lambda"].reshape(1, -1),
           jnp.pad(sm_full["gdn_conv_w"], ((0, 4), (0, 0))), _lane_row(gdn_a_log), _lane_row(gdn_dt_bias),
           gdn_norm, ffn2_norm, wg2, wu2, wd2, final_norm.reshape(1, D))

    loss_blk, gx, big, small = _local_step(x[0], loss_target[0], wts)
    loss = lax.psum(loss_blk[0, 0], ("x", "y", "c"))

    d_wg1, d_wu1, d_wd1, d_win, d_wout, d_wg2, d_wu2, d_wd2 = big
    gpack = jnp.stack([_pack((d_wg1[s], d_wu1[s], d_wd1[s], d_win[:, INSH * s:INSH * (s + 1)],
                              d_wout[OUTSH * s:OUTSH * (s + 1)], d_wg2[s], d_wu2[s], d_wd2[s]), F32)
                       for s in range(NSH)]).reshape(NSH, 2, HALF, D)
    land = _sibling_exchange(gpack)
    part = _chip_sum(gpack, land, ci.reshape(1).astype(jnp.int32))
    mine_half = _sum_slots(_chip_scatter(part), "grad_sum_chips")
    gshard = _unpack(_sibling_swap(mine_half).reshape(PACK_ROWS, D))
    grads = {n: g[None] for n, g in zip(_BIG_NAMES, gshard)}

    sm_sizes = [(n, small[n].shape) for n in _SMALL_NAMES]
    sm_grad = _pad_rows(jnp.concatenate([small[n].reshape(-1) for n in _SMALL_NAMES]), D)
    sm_sum = _sum_slots(_gather_small(sm_grad, "gather_small_grads"), "small_grad_sum").reshape(-1)
    off = 0
    for n, shp in sm_sizes:
        cnt = 1
        for dsz in shp:
            cnt *= dsz
        g = sm_sum[off:off + cnt].reshape(shp)
        off += cnt
        if n in _SMALL_SHARDED:
            wd_ = _SMALL_SHARDED[n]
            g = lax.dynamic_slice_in_dim(g, shard * wd_, wd_, axis=1)
        grads[n] = g.reshape(w[n].shape)

    delta, new_m, new_v = {}, {}, {}
    for n in _BIG_NAMES:
        shp = w[n].shape
        d_, m_, v_ = _adamw(w[n][0], grads[n][0], mom[n][0], var[n][0], "adamw_" + n)
        delta[n], new_m[n], new_v[n] = d_.reshape(shp), m_.reshape(shp), v_.reshape(shp)
    packs = [_pad_rows(jnp.concatenate([t[n].reshape(-1) for n in _SMALL_NAMES]), D) for t in (w, grads, mom, var)]
    sm_out = _adamw(*packs, "adamw_small")
    off = 0
    for n in _SMALL_NAMES:
        cnt = w[n].size
        for dst, src in zip((delta, new_m, new_v), sm_out):
            dst[n] = src.reshape(-1)[off:off + cnt].reshape(w[n].shape)
        off += cnt

    outs = [loss, gx[None]]
    for group in (grads, delta, new_m, new_v):
        outs += [group[n] for n in _OUT_ORDER]
    return tuple(outs)
```

```python
import functools

import jax
import jax.numpy as jnp
from jax import lax
from jax.experimental import pallas as pl
from jax.experimental.pallas import tpu as pltpu

F32 = jnp.float32
BF16 = jnp.bfloat16
EPS = 1e-6
D = 1024
NSH = 4
FSH = 704
RGW = 512
QKVW = 1536
ZW = 512
BAW = 16
BAP = 128
INSH = 772
OUTSH = 256
CHUNK = 64
NH = 4
DH = 128
RG_C = 8.0
VMEM_LIMIT = 52 * 1024 * 1024
MESH = pl.DeviceIdType.MESH

ADAM_LR = 0.001
ADAM_B1 = 0.9
ADAM_B2 = 0.999
ADAM_EPS = 1e-08
ADAM_WD = 0.01
ADAM_STEP = 10


def _cparams(n_grid):
    return pltpu.CompilerParams(dimension_semantics=("arbitrary",) * n_grid, vmem_limit_bytes=VMEM_LIMIT)


def _sig(x):
    return 1.0 / (1.0 + jnp.exp(-x))


def _softplus(x):
    return jnp.maximum(x, 0.0) + jnp.log(1.0 + jnp.exp(-jnp.abs(x)))


def _neg_expm1(y):
    series = -y * (1.0 + y * (0.5 + y * (1.0 / 6 + y * (1.0 / 24 + y * (1.0 / 120 + y * (1.0 / 720 + y / 5040))))))
    return jnp.where(y > -0.3, series, 1.0 - jnp.exp(y))


_GELU_C = 0.7978845608028654


def _gelu(x):
    t = jnp.tanh(_GELU_C * (x + 0.044715 * x * x * x))
    return 0.5 * x * (1.0 + t)


def _gelu_grad(x):
    t = jnp.tanh(_GELU_C * (x + 0.044715 * x * x * x))
    return 0.5 * (1.0 + t) + 0.5 * x * (1.0 - t * t) * _GELU_C * (1.0 + 3 * 0.044715 * x * x)


def _silu_grad(x):
    s = _sig(x)
    return s * (1.0 + x * (1.0 - s))


def _dot(a, b):
    return jnp.dot(a.astype(BF16), b.astype(BF16), preferred_element_type=F32)


def _dot_nt(a, b):
    return lax.dot_general(a.astype(BF16), b.astype(BF16), (((1,), (1,)), ((), ())), preferred_element_type=F32)


def _dot_tn(a, b):
    return lax.dot_general(a.astype(BF16), b.astype(BF16), (((0,), (0,)), ((), ())), preferred_element_type=F32)


_HI = lax.Precision.HIGHEST


def _hp(a, b):
    return jnp.dot(a, b, precision=_HI, preferred_element_type=F32)


def _hp_nt(a, b):
    return lax.dot_general(a, b, (((1,), (1,)), ((), ())), precision=_HI, preferred_element_type=F32)


def _hp_tn(a, b):
    return lax.dot_general(a, b, (((0,), (0,)), ((), ())), precision=_HI, preferred_element_type=F32)


def _rms(xv):
    r = lax.rsqrt(jnp.mean(xv * xv, axis=-1, keepdims=True) + EPS)
    return r, xv * r


def _rms_bwd(dy, xh, r, gain):
    dxh = dy * gain
    return r * (dxh - xh * jnp.mean(dxh * xh, axis=-1, keepdims=True))


def _colsum(v):
    return jnp.sum(v, axis=0, keepdims=True)


def _rows(t, c):
    return pl.BlockSpec((t, c), lambda i: (i, 0))


def _full(shape):
    n = len(shape)
    return pl.BlockSpec(shape, lambda i: (0,) * n)


def _sds(shape, dtype=F32):
    return jax.ShapeDtypeStruct(shape, dtype)


def _ffn_fwd(x, gain, wg, wu, wd, name):
    s = x.shape[0]
    tm = min(512, s)

    def body(x_ref, g_ref, wg_ref, wu_ref, wd_ref, xo_ref, a_ref, b_ref, h_sc, acc):
        j = pl.program_id(1)

        @pl.when(j == 0)
        def _():
            _, xh = _rms(x_ref[...])
            h_sc[...] = (xh * g_ref[...]).astype(BF16)
            acc[...] = jnp.zeros_like(acc)

        h = h_sc[...]
        a = jnp.dot(h, wg_ref[0], preferred_element_type=F32)
        b = jnp.dot(h, wu_ref[0], preferred_element_type=F32)
        a_ref[0] = a
        b_ref[0] = b
        f = (a * _sig(a) * b).astype(BF16)
        acc[...] += jnp.dot(f, wd_ref[0], preferred_element_type=F32)

        @pl.when(j == NSH - 1)
        def _():
            xo_ref[...] = x_ref[...] + 0.5 * acc[...]

    return pl.pallas_call(
        body, name=name, grid=(s // tm, NSH),
        in_specs=[pl.BlockSpec((tm, D), lambda i, j: (i, 0)), pl.BlockSpec((1, D), lambda i, j: (0, 0)),
                  pl.BlockSpec((1, D, FSH), lambda i, j: (j, 0, 0)), pl.BlockSpec((1, D, FSH), lambda i, j: (j, 0, 0)),
                  pl.BlockSpec((1, FSH, D), lambda i, j: (j, 0, 0))],
        out_specs=[pl.BlockSpec((tm, D), lambda i, j: (i, 0)), pl.BlockSpec((1, tm, FSH), lambda i, j: (j, i, 0)),
                   pl.BlockSpec((1, tm, FSH), lambda i, j: (j, i, 0))],
        out_shape=[_sds((s, D)), _sds((NSH, s, FSH)), _sds((NSH, s, FSH))],
        scratch_shapes=[pltpu.VMEM((tm, D), BF16), pltpu.VMEM((tm, D), F32)],
        compiler_params=_cparams(2),
    )(x, gain, wg, wu, wd)


def _ffn_bwd(x, dout, gain, a, b, wg, wu, wd, name):
    s = x.shape[0]
    tm = min(512, s)

    def body(x_ref, d_ref, g_ref, a_ref, b_ref, wg_ref, wu_ref, wd_ref,
             dx_ref, dg_ref, h_ref, do_ref, f_ref, da_ref, db_ref, do_sc, dh_acc):
        i = pl.program_id(0)
        j = pl.program_id(1)

        @pl.when(jnp.logical_and(i == 0, j == 0))
        def _():
            dg_ref[...] = jnp.zeros_like(dg_ref)

        @pl.when(j == 0)
        def _():
            _, xh = _rms(x_ref[...])
            h_ref[...] = (xh * g_ref[...]).astype(BF16)
            do = (0.5 * d_ref[...]).astype(BF16)
            do_sc[...] = do
            do_ref[...] = do
            dh_acc[...] = jnp.zeros_like(dh_acc)

        do = do_sc[...]
        df = _dot_nt(do, wd_ref[0])
        av = a_ref[0]
        bv = b_ref[0]
        sa = _sig(av)
        f_ref[0] = (av * sa * bv).astype(BF16)
        da = (df * bv * sa * (1.0 + av * (1.0 - sa))).astype(BF16)
        db = (df * av * sa).astype(BF16)
        da_ref[0] = da
        db_ref[0] = db
        dh_acc[...] += _dot_nt(da, wg_ref[0]) + _dot_nt(db, wu_ref[0])

        @pl.when(j == NSH - 1)
        def _():
            r, xh = _rms(x_ref[...])
            dh = dh_acc[...]
            dg_ref[...] += _colsum(dh * xh)
            dx_ref[...] = d_ref[...] + _rms_bwd(dh, xh, r, g_ref[...])

    tok = pl.BlockSpec((tm, D), lambda i, j: (i, 0))
    sh = pl.BlockSpec((1, tm, FSH), lambda i, j: (j, i, 0))
    return pl.pallas_call(
        body, name=name, grid=(s // tm, NSH),
        in_specs=[tok, tok, pl.BlockSpec((1, D), lambda i, j: (0, 0)), sh, sh,
                  pl.BlockSpec((1, D, FSH), lambda i, j: (j, 0, 0)), pl.BlockSpec((1, D, FSH), lambda i, j: (j, 0, 0)),
                  pl.BlockSpec((1, FSH, D), lambda i, j: (j, 0, 0))],
        out_specs=[tok, pl.BlockSpec((1, D), lambda i, j: (0, 0)), tok, tok, sh, sh, sh],
        out_shape=[_sds((s, D)), _sds((1, D)), _sds((s, D), BF16), _sds((s, D), BF16),
                   _sds((NSH, s, FSH), BF16), _sds((NSH, s, FSH), BF16), _sds((NSH, s, FSH), BF16)],
        scratch_shapes=[pltpu.VMEM((tm, D), BF16), pltpu.VMEM((tm, D), F32)],
        compiler_params=_cparams(2),
    )(x, dout, gain, a, b, wg, wu, wd)


def _tn(a, b, name):
    a_g = a.ndim == 3
    b_g = b.ndim == 3
    g = a.shape[0] if a_g else (b.shape[0] if b_g else 1)
    s, k = a.shape[-2:]
    n = b.shape[-1]
    ts = min(512, s)

    def body(a_ref, b_ref, o_ref):
        @pl.when(pl.program_id(1) == 0)
        def _():
            o_ref[...] = jnp.zeros_like(o_ref)

        av = a_ref[0] if a_g else a_ref[...]
        bv = b_ref[0] if b_g else b_ref[...]
        o_ref[0] += _dot_tn(av, bv)

    a_spec = pl.BlockSpec((1, ts, k), lambda gi, si: (gi, si, 0)) if a_g else pl.BlockSpec((ts, k), lambda gi, si: (si, 0))
    b_spec = pl.BlockSpec((1, ts, n), lambda gi, si: (gi, si, 0)) if b_g else pl.BlockSpec((ts, n), lambda gi, si: (si, 0))
    return pl.pallas_call(
        body, name=name, grid=(g, s // ts), in_specs=[a_spec, b_spec],
        out_specs=pl.BlockSpec((1, k, n), lambda gi, si: (gi, 0, 0)),
        out_shape=_sds((g, k, n)), compiler_params=_cparams(2),
    )(a, b)


_P_WIDTHS = (RGW, RGW, QKVW, ZW, BAP)


def _inproj(x1, gain, ws, name):
    s = x1.shape[0]
    tm = min(256, s)

    def body(x_ref, g_ref, *refs):
        w_refs = refs[:5]
        h_ref = refs[5]
        p_refs = refs[6:]
        _, xh = _rms(x_ref[...])
        h = (xh * g_ref[...]).astype(BF16)
        h_ref[...] = h
        for w_ref, p_ref in zip(w_refs, p_refs):
            p_ref[...] = jnp.dot(h, w_ref[...], preferred_element_type=F32)

    return pl.pallas_call(
        body, name=name, grid=(s // tm,),
        in_specs=[_rows(tm, D), _full((1, D))] + [_full((D, w)) for w in _P_WIDTHS],
        out_specs=[_rows(tm, D)] + [_rows(tm, w) for w in _P_WIDTHS],
        out_shape=[_sds((s, D), BF16)] + [_sds((s, w)) for w in _P_WIDTHS],
        compiler_params=_cparams(1),
    )(x1, gain, *ws)


def _inproj_bwd(x1, dx2, gain, dps, ws, name):
    s = x1.shape[0]
    tm = min(256, s)

    def body(x_ref, d_ref, g_ref, *refs):
        dp_refs = refs[:5]
        w_refs = refs[5:10]
        dx_ref, dg_ref = refs[10:]

        @pl.when(pl.program_id(0) == 0)
        def _():
            dg_ref[...] = jnp.zeros_like(dg_ref)

        dh = jnp.zeros((tm, D), F32)
        for dp_ref, w_ref in zip(dp_refs, w_refs):
            dh = dh + _dot_nt(dp_ref[...], w_ref[...])
        r, xh = _rms(x_ref[...])
        dg_ref[...] += _colsum(dh * xh)
        dx_ref[...] = d_ref[...] + _rms_bwd(dh, xh, r, g_ref[...])

    return pl.pallas_call(
        body, name=name, grid=(s // tm,),
        in_specs=[_rows(tm, D), _rows(tm, D), _full((1, D))] + [_rows(tm, w) for w in _P_WIDTHS]
        + [_full((D, w)) for w in _P_WIDTHS],
        out_specs=[_rows(tm, D), _full((1, D))],
        out_shape=[_sds((s, D)), _sds((1, D))],
        compiler_params=_cparams(1),
    )(x1, dx2, gain, *dps, *ws)


def _halo_specs(s, t, c):
    nb8 = s // 8
    tb = t // 8
    prev = pl.BlockSpec((8, c), lambda i: (jnp.maximum(i * tb - 1, 0), 0))
    nxt = pl.BlockSpec((8, c), lambda i: (jnp.minimum((i + 1) * tb, nb8 - 1), 0))
    return prev, nxt


def _edge_masks(nb):
    i = pl.program_id(0)
    return jnp.where(i > 0, 1.0, 0.0).astype(F32), jnp.where(i < nb - 1, 1.0, 0.0).astype(F32)


def _shifted(xx, off, t):
    n = t + 16
    sh = (-off) % n
    rolled = xx if sh == 0 else pltpu.roll(xx, sh, 0)
    return rolled[8:8 + t]


def _conv(x, w8, bias, name):
    s, c = x.shape
    t = min(256, s)
    nb = s // t

    def body(x_ref, xp_ref, xn_ref, w_ref, b_ref, o_ref):
        pm, nm = _edge_masks(nb)
        for c0 in range(0, c, 512):
            cols = slice(c0, c0 + 512)
            xx = jnp.concatenate([xp_ref[:, cols] * pm, x_ref[:, cols], xn_ref[:, cols] * nm], axis=0)
            acc = jnp.zeros((t, 512), F32) + b_ref[:, cols]
            for j in range(4):
                acc = acc + w_ref[j:j + 1, cols] * _shifted(xx, j - 2, t)
            o_ref[:, cols] = acc

    prev, nxt = _halo_specs(s, t, c)
    return pl.pallas_call(
        body, name=name, grid=(nb,),
        in_specs=[_rows(t, c), prev, nxt, _full((8, c)), _full((1, c))],
        out_specs=_rows(t, c), out_shape=_sds((s, c)), compiler_params=_cparams(1),
    )(x, x, x, w8, bias)


def _conv_bwd(x, dc, w8, name):
    s, c = x.shape
    t = min(256, s)
    nb = s // t

    def body(x_ref, d_ref, dp_ref, dn_ref, w_ref, dx_ref, dw_ref, db_ref):
        @pl.when(pl.program_id(0) == 0)
        def _():
            dw_ref[...] = jnp.zeros_like(dw_ref)
            db_ref[...] = jnp.zeros_like(db_ref)

        pm, nm = _edge_masks(nb)
        for c0 in range(0, c, 512):
            cols = slice(c0, c0 + 512)
            dd = jnp.concatenate([dp_ref[:, cols] * pm, d_ref[:, cols], dn_ref[:, cols] * nm], axis=0)
            xv = x_ref[:, cols]
            acc = jnp.zeros((t, 512), F32)
            for j in range(4):
                dsh = _shifted(dd, 2 - j, t)
                acc = acc + w_ref[j:j + 1, cols] * dsh
                dw_ref[j:j + 1, cols] += _colsum(dsh * xv)
            dx_ref[:, cols] = acc
            db_ref[:, cols] += _colsum(d_ref[:, cols])

    prev, nxt = _halo_specs(s, t, c)
    return pl.pallas_call(
        body, name=name, grid=(nb,),
        in_specs=[_rows(t, c), _rows(t, c), prev, nxt, _full((8, c))],
        out_specs=[_rows(t, c), _full((8, c)), _full((1, c))],
        out_shape=[_sds((s, c)), _sds((8, c)), _sds((1, c))], compiler_params=_cparams(1),
    )(x, dc, dc, dc, w8)


def _shift_rows(x, direction, name):
    s, c = x.shape
    t = min(256, s)
    nb = s // t

    def body(x_ref, xp_ref, xn_ref, o_ref):
        pm, nm = _edge_masks(nb)
        xx = jnp.concatenate([xp_ref[...] * pm, x_ref[...], xn_ref[...] * nm], axis=0)
        o_ref[...] = _shifted(xx, direction, t)

    prev, nxt = _halo_specs(s, t, c)
    return pl.pallas_call(
        body, name=name, grid=(nb,), in_specs=[_rows(t, c), prev, nxt],
        out_specs=_rows(t, c), out_shape=_sds((s, c)), compiler_params=_cparams(1),
    )(x, x, x)


def _rg_gates(xc, pre, lam_row):
    sp8 = RG_C * _softplus(-lam_row)
    out = []
    for d in range(2):
        r = _sig(pre[:, RGW * d:RGW * (d + 1)])
        gi = _sig(pre[:, 2 * RGW + RGW * d:2 * RGW + RGW * (d + 1)])
        la = -r * sp8[:, RGW * d:RGW * (d + 1)]
        a = jnp.exp(la)
        mult = jnp.sqrt(_neg_expm1(2.0 * la))
        out.append((r, gi, a, mult))
    return out


def _mix_prep(c_rg, c_qkv, p_ba, wgates, gbias, lam_row, alog_row, dtb_row, name):
    s = c_rg.shape[0]
    t = min(256, s)

    def body(xc_ref, cq_ref, pc_ref, wg_ref, gb_ref, lam_ref, alog_ref, dtb_ref,
             a0_ref, b0_ref, a1_ref, b1_ref, q_ref, k_ref, v_ref, bg_ref):
        xc = xc_ref[...]
        pre = _dot(xc, wg_ref[...]) + gb_ref[...]
        gates = _rg_gates(xc, pre, lam_ref[...])
        for (r, gi, a, mult), a_ref, b_ref in zip(gates, (a0_ref, a1_ref), (b0_ref, b1_ref)):
            a_ref[...] = a
            b_ref[...] = mult * gi * xc
        cq = cq_ref[...]
        sq = cq * _sig(cq)
        for h in range(NH):
            sl = slice(DH * h, DH * (h + 1))
            qh = sq[:, sl]
            q_ref[:, sl] = qh * lax.rsqrt(jnp.sum(qh * qh, axis=-1, keepdims=True) + EPS) * (DH ** -0.5)
            kh = sq[:, RGW + DH * h:RGW + DH * (h + 1)]
            k_ref[:, sl] = kh * lax.rsqrt(jnp.sum(kh * kh, axis=-1, keepdims=True) + EPS)
        v_ref[...] = sq[:, 2 * RGW:]
        pc = pc_ref[...]
        lane = lax.broadcasted_iota(jnp.int32, pc.shape, 1)
        beta = _sig(pc)
        g = -jnp.exp(alog_ref[...]) * _softplus(pc + dtb_ref[...])
        bg_ref[...] = jnp.where(lane < 8, beta, jnp.where(lane < 16, g, 0.0))

    return pl.pallas_call(
        body, name=name, grid=(s // t,),
        in_specs=[_rows(t, RGW), _rows(t, QKVW), _rows(t, BAP), _full((RGW, 4 * RGW)), _full((1, 4 * RGW)),
                  _full((1, 2 * RGW)), _full((1, BAP)), _full((1, BAP))],
        out_specs=[_rows(t, RGW)] * 7 + [_rows(t, BAP)],
        out_shape=[_sds((s, RGW))] * 7 + [_sds((s, BAP))],
        compiler_params=_cparams(1),
    )(c_rg, c_qkv, p_ba, wgates, gbias, lam_row, alog_row, dtb_row)


def _scan(a, b, reverse, name):
    s, c = a.shape
    t = min(512, s)
    nb = s // t
    ng = t // 8
    idx = (lambda i: (nb - 1 - i, 0)) if reverse else (lambda i: (i, 0))

    def body(a_ref, b_ref, h_ref, carry):
        @pl.when(pl.program_id(0) == 0)
        def _():
            carry[...] = jnp.zeros_like(carry)

        row = lax.broadcasted_iota(jnp.int32, (8, c), 0)

        def group(gi, cv):
            g = (ng - 1 - gi) if reverse else gi
            r0 = pl.multiple_of(g * 8, 8)
            av = a_ref[pl.ds(r0, 8), :]
            bv = b_ref[pl.ds(r0, 8), :]
            for k in (1, 2, 4):
                sh = (8 - k) if reverse else k
                m = (row < 8 - k) if reverse else (row >= k)
                a_s = pltpu.roll(av, sh, 0)
                b_s = pltpu.roll(bv, sh, 0)
                bv = jnp.where(m, av * b_s + bv, bv)
                av = jnp.where(m, av * a_s, av)
            hv = av * cv + bv
            h_ref[pl.ds(r0, 8), :] = hv
            return hv[0:1, :] if reverse else hv[7:8, :]

        carry[0:1, :] = lax.fori_loop(0, ng, group, carry[0:1, :])

    return pl.pallas_call(
        body, name=name, grid=(nb,), in_specs=[pl.BlockSpec((t, c), idx), pl.BlockSpec((t, c), idx)],
        out_specs=pl.BlockSpec((t, c), idx), out_shape=_sds((s, c)),
        scratch_shapes=[pltpu.VMEM((8, c), F32)], compiler_params=_cparams(1),
    )(a, b)


def _gates_bwd(xc, wgates, gbias, lam_row, lam0, lam1, h0s, h1s, name):
    s = xc.shape[0]
    t = min(256, s)

    def body(xc_ref, wg_ref, gb_ref, lam_ref, l0_ref, l1_ref, h0_ref, h1_ref,
             dxc_ref, dpre_ref, xcb_ref, dgb_ref, dlam_ref):
        @pl.when(pl.program_id(0) == 0)
        def _():
            dgb_ref[...] = jnp.zeros_like(dgb_ref)
            dlam_ref[...] = jnp.zeros_like(dlam_ref)

        xv = xc_ref[...]
        pre = _dot(xv, wg_ref[...]) + gb_ref[...]
        lam_row_v = lam_ref[...]
        sp8 = RG_C * _softplus(-lam_row_v)
        dsp_dlam = -RG_C * _sig(-lam_row_v)
        gates = _rg_gates(xv, pre, lam_row_v)
        dxc = jnp.zeros((t, RGW), F32)
        dpre_r = []
        dpre_i = []
        for d, ((r, gi, a, mult), l_ref, h_ref) in enumerate(zip(gates, (l0_ref, l1_ref), (h0_ref, h1_ref))):
            dbb = l_ref[...]
            da = dbb * h_ref[...]
            cs = slice(RGW * d, RGW * (d + 1))
            dmult = dbb * gi * xv
            dgi = dbb * mult * xv
            dxc = dxc + dbb * mult * gi
            dla = da * a - dmult * a * a / mult
            dr = -dla * sp8[:, cs]
            dlam_ref[:, cs] += _colsum(-dla * r) * dsp_dlam[:, cs]
            dpre_r.append(dr * r * (1.0 - r))
            dpre_i.append(dgi * gi * (1.0 - gi))
        dpre = jnp.concatenate(dpre_r + dpre_i, axis=1)
        dgb_ref[...] += _colsum(dpre)
        dpre_b = dpre.astype(BF16)
        dpre_ref[...] = dpre_b
        xcb_ref[...] = xv.astype(BF16)
        dxc_ref[...] = dxc + _dot_nt(dpre_b, wg_ref[...])

    return pl.pallas_call(
        body, name=name, grid=(s // t,),
        in_specs=[_rows(t, RGW), _full((RGW, 4 * RGW)), _full((1, 4 * RGW)), _full((1, 2 * RGW))] + [_rows(t, RGW)] * 4,
        out_specs=[_rows(t, RGW), _rows(t, 4 * RGW), _rows(t, RGW), _full((1, 4 * RGW)), _full((1, 2 * RGW))],
        out_shape=[_sds((s, RGW)), _sds((s, 4 * RGW), BF16), _sds((s, RGW), BF16), _sds((1, 4 * RGW)), _sds((1, 2 * RGW))],
        compiler_params=_cparams(1),
    )(xc, wgates, gbias, lam_row, lam0, lam1, h0s, h1s)


class _GdnMasks:
    def __init__(self, d):
        ri = lax.broadcasted_iota(jnp.int32, (CHUNK, CHUNK), 0)
        ci = lax.broadcasted_iota(jnp.int32, (CHUNK, CHUNK), 1)
        self.incl = (ri >= ci) if d == 0 else (ri <= ci)
        self.strict = (ri > ci) if d == 0 else (ri < ci)
        b16 = jnp.right_shift(ri, 4) == jnp.right_shift(ci, 4)
        b32 = jnp.right_shift(ri, 5) == jnp.right_shift(ci, 5)
        self.diag16 = b16
        self.off32 = jnp.logical_and(b32, jnp.logical_not(b16))
        self.off64 = jnp.logical_not(b32)
        self.eye = jnp.where(ri == ci, 1.0, 0.0).astype(F32)
        self.tri = jnp.where(self.incl, 1.0, 0.0).astype(F32)
        self.last = CHUNK - 1 if d == 0 else 0


def _tri_inv(lmat, m):
    n = jnp.where(m.diag16, lmat, 0.0)
    p = m.eye - n
    q = _hp(n, n)
    p = _hp(p, m.eye + q)
    q = _hp(q, q)
    p = _hp(p, m.eye + q)
    q = _hp(q, q)
    p = _hp(p, m.eye + q)
    p = p - _hp(_hp(p, jnp.where(m.off32, lmat, 0.0)), p)
    p = p - _hp(_hp(p, jnp.where(m.off64, lmat, 0.0)), p)
    return p


class _GdnHead:
    def __init__(self, qh, kh, vh, bg, gcs, gcs_t, d, h, m):
        cb = 4 * d + h
        cg = 8 + 4 * d + h
        self.q, self.k, self.v = qh, kh, vh
        self.beta = bg[:, cb:cb + 1]
        gcol = gcs[:, cg:cg + 1]
        grow = gcs_t[cg:cg + 1, :]
        gl = gcs[m.last:m.last + 1, cg:cg + 1]
        self.decay = jnp.exp(jnp.where(m.incl, gcol - grow, -1e30))
        self.kb = kh * self.beta
        self.vb = vh * self.beta
        self.a0 = _dot_nt(self.kb, kh)
        self.q0 = _dot_nt(qh, kh)
        self.lmat = jnp.where(m.strict, self.a0 * self.decay, 0.0)
        self.attn = self.q0 * self.decay
        self.eg = jnp.exp(gcol)
        self.ek = jnp.exp(gl - gcol)
        self.cd = jnp.exp(gl)
        self.kg = self.kb * self.eg
        self.qd = qh * self.eg
        self.kd = kh * self.ek


def _gdn_specs(s, d):
    n = s // CHUNK
    ci = (lambda i: i) if d == 0 else (lambda i: n - 1 - i)
    tok = pl.BlockSpec((CHUNK, NH * DH), lambda i: (ci(i), 0))
    bg = pl.BlockSpec((CHUNK, BAP), lambda i: (ci(i), 0))
    tsp = pl.BlockSpec((1, NH, CHUNK, CHUNK), lambda i: (ci(i), 0, 0, 0))
    ssp = pl.BlockSpec((1, NH, DH, DH), lambda i: (ci(i), 0, 0, 0))
    return n, tok, bg, tsp, ssp


def _gdn_fwd(q, k, v, bg, d, name):
    s = q.shape[0]
    n, tok, bgs, tsp, ssp = _gdn_specs(s, d)

    def body(q_ref, k_ref, v_ref, bg_ref, o_ref, t_ref, s_ref, st):
        @pl.when(pl.program_id(0) == 0)
        def _():
            st[...] = jnp.zeros_like(st)

        m = _GdnMasks(d)
        bgv = bg_ref[...]
        gcs = _hp(m.tri, bgv)
        gcs_t = lax.dot_general(bgv, m.tri, (((0,), (1,)), ((), ())), precision=_HI, preferred_element_type=F32)
        for h in range(NH):
            sl = slice(DH * h, DH * (h + 1))
            c = _GdnHead(q_ref[:, sl], k_ref[:, sl], v_ref[:, sl], bgv, gcs, gcs_t, d, h, m)
            tm = _tri_inv(c.lmat, m)
            u = _dot(tm, c.vb)
            w = _dot(tm, c.kg)
            sh = st[h]
            vn = u - _dot(w, sh)
            o_ref[:, sl] = _dot(c.qd, sh) + _dot(c.attn, vn)
            s_ref[0, h] = sh
            t_ref[0, h] = tm
            st[h] = sh * c.cd + _dot_tn(c.kd, vn)

    return pl.pallas_call(
        body, name=name, grid=(n,), in_specs=[tok, tok, tok, bgs], out_specs=[tok, tsp, ssp],
        out_shape=[_sds((s, NH * DH)), _sds((n, NH, CHUNK, CHUNK)), _sds((n, NH, DH, DH))],
        scratch_shapes=[pltpu.VMEM((NH, DH, DH), F32)], compiler_params=_cparams(1),
    )(q, k, v, bg)


def _gdn_bwd(q, k, v, bg, tmat, states, do, d, name):
    s = q.shape[0]
    n, tok, bgs, tsp, ssp = _gdn_specs(s, 1 - d)

    def body(q_ref, k_ref, v_ref, bg_ref, t_ref, s_ref, do_ref, dq_ref, dk_ref, dv_ref, dbg_ref, dst):
        @pl.when(pl.program_id(0) == 0)
        def _():
            dst[...] = jnp.zeros_like(dst)

        m = _GdnMasks(d)
        bgv = bg_ref[...]
        gcs = _hp(m.tri, bgv)
        gcs_t = lax.dot_general(bgv, m.tri, (((0,), (1,)), ((), ())), precision=_HI, preferred_element_type=F32)
        lane = lax.broadcasted_iota(jnp.int32, (CHUNK, BAP), 1)
        rowi = lax.broadcasted_iota(jnp.int32, (CHUNK, 1), 0)
        ones = jnp.ones((CHUNK, DH), F32)
        dbg = jnp.zeros((CHUNK, BAP), F32)
        for h in range(NH):
            sl = slice(DH * h, DH * (h + 1))
            c = _GdnHead(q_ref[:, sl], k_ref[:, sl], v_ref[:, sl], bgv, gcs, gcs_t, d, h, m)
            tm = t_ref[0, h]
            sh = s_ref[0, h]
            dsn = dst[h]
            doh = do_ref[:, sl]
            u = _dot(tm, c.vb)
            w = _dot(tm, c.kg)
            vn = u - _dot(w, sh)
            d_vn = _dot_tn(c.attn, doh) + _dot(c.kd, dsn)
            d_attn = jnp.where(m.incl, _dot_nt(doh, vn), 0.0)
            d_qd = _dot_nt(doh, sh)
            d_kd = _dot_nt(vn, dsn)
            d_cd = jnp.sum(jnp.sum(sh * dsn, axis=1, keepdims=True), axis=0, keepdims=True)
            dst[h] = c.cd * dsn + _dot_tn(c.qd, doh) - _dot_tn(w, d_vn)
            d_w = -_dot_nt(d_vn, sh)
            d_t = _dot_nt(d_vn, c.vb) + _dot_nt(d_w, c.kg)
            d_vb = _dot_tn(tm, d_vn)
            d_kg = _dot_tn(tm, d_w)
            d_l = jnp.where(m.strict, -_hp_nt(_hp_tn(tm, d_t), tm), 0.0)
            d_a0 = d_l * c.decay
            d_q0 = d_attn * c.decay
            e = (d_l * c.a0 + d_attn * c.q0) * c.decay
            d_kb = _dot(d_a0, c.k) + d_kg * c.eg
            dk_ref[:, sl] = _dot_tn(d_a0, c.kb) + _dot_tn(d_q0, c.q) + d_kd * c.ek + d_kb * c.beta
            dq_ref[:, sl] = _dot(d_q0, c.k) + d_qd * c.eg
            dv_ref[:, sl] = d_vb * c.beta
            s_kd = jnp.sum(d_kd * c.kd, axis=1, keepdims=True)
            d_gc = (jnp.sum(d_kg * c.kg, axis=1, keepdims=True) + jnp.sum(d_qd * c.qd, axis=1, keepdims=True) - s_kd
                    + jnp.sum(e, axis=1, keepdims=True) - _hp_tn(e, ones)[:, 0:1])
            d_gl = jnp.sum(s_kd, axis=0, keepdims=True) + d_cd * c.cd
            d_gc = d_gc + jnp.where(rowi == m.last, d_gl, 0.0)
            d_g = _hp_tn(m.tri, d_gc * ones)[:, 0:1]
            d_beta = jnp.sum(d_kb * c.k, axis=1, keepdims=True) + jnp.sum(d_vb * c.v, axis=1, keepdims=True)
            dbg = dbg + jnp.where(lane == 4 * d + h, d_beta, 0.0) + jnp.where(lane == 8 + 4 * d + h, d_g, 0.0)
        dbg_ref[...] = dbg

    return pl.pallas_call(
        body, name=name, grid=(n,), in_specs=[tok, tok, tok, bgs, tsp, ssp, tok], out_specs=[tok, tok, tok, bgs],
        out_shape=[_sds((s, NH * DH))] * 3 + [_sds((s, BAP))],
        scratch_shapes=[pltpu.VMEM((NH, DH, DH), F32)], compiler_params=_cparams(1),
    )(q, k, v, bg, tmat, states, do)


def _prep_bwd(c_qkv, p_ba, alog_row, dtb_row, dqs, dks, dvs, dbgs, name):
    s = c_qkv.shape[0]
    t = min(256, s)

    def body(cq_ref, pc_ref, alog_ref, dtb_ref, dq0, dq1, dk0, dk1, dv0, dv1, dbg0, dbg1,
             dcq_ref, dpc_ref, dalog_ref, ddtb_ref):
        @pl.when(pl.program_id(0) == 0)
        def _():
            dalog_ref[...] = jnp.zeros_like(dalog_ref)
            ddtb_ref[...] = jnp.zeros_like(ddtb_ref)

        cq = cq_ref[...]
        sq = cq * _sig(cq)
        sg = _silu_grad(cq)
        for h in range(NH):
            sl = slice(DH * h, DH * (h + 1))
            for off, d0, d1, scale in ((0, dq0, dq1, DH ** -0.5), (RGW, dk0, dk1, 1.0)):
                csl = slice(off + DH * h, off + DH * (h + 1))
                xh = sq[:, csl]
                nrm = lax.rsqrt(jnp.sum(xh * xh, axis=-1, keepdims=True) + EPS)
                y = xh * nrm
                dy = (d0[:, sl] + d1[:, sl]) * scale
                dcq_ref[:, csl] = nrm * (dy - y * jnp.sum(dy * y, axis=-1, keepdims=True)) * sg[:, csl]
        dcq_ref[:, 2 * RGW:] = (dv0[...] + dv1[...]) * sg[:, 2 * RGW:]
        pc = pc_ref[...]
        lane = lax.broadcasted_iota(jnp.int32, pc.shape, 1)
        dbg = dbg0[...] + dbg1[...]
        beta = _sig(pc)
        ea = jnp.exp(alog_ref[...])
        z = pc + dtb_ref[...]
        g = -ea * _softplus(z)
        is_g = jnp.logical_and(lane >= 8, lane < 16)
        d_alpha = jnp.where(is_g, dbg * (-ea) * _sig(z), 0.0)
        dpc_ref[...] = jnp.where(lane < 8, dbg * beta * (1.0 - beta), d_alpha)
        dalog_ref[...] += _colsum(jnp.where(is_g, dbg * g, 0.0))
        ddtb_ref[...] += _colsum(d_alpha)

    return pl.pallas_call(
        body, name=name, grid=(s // t,),
        in_specs=[_rows(t, QKVW), _rows(t, BAP), _full((1, BAP)), _full((1, BAP))] + [_rows(t, NH * DH)] * 6 + [_rows(t, BAP)] * 2,
        out_specs=[_rows(t, QKVW), _rows(t, BAP), _full((1, BAP)), _full((1, BAP))],
        out_shape=[_sds((s, QKVW)), _sds((s, BAP)), _sds((1, BAP)), _sds((1, BAP))],
        compiler_params=_cparams(1),
    )(c_qkv, p_ba, alog_row, dtb_row, dqs[0], dqs[1], dks[0], dks[1], dvs[0], dvs[1], dbgs[0], dbgs[1])


def _mix_out_values(hf, hb, gate, of, ob, z, gn):
    hr = hf + hb
    y_rg = hr * _gelu(gate)
    osum = of + ob
    parts = []
    for h in range(NH):
        sl = slice(DH * h, DH * (h + 1))
        oh = osum[:, sl]
        r, ohat = _rms(oh)
        zh = z[:, sl]
        parts.append((r, ohat, zh))
    y_gdn = jnp.concatenate([ohat * gn * (zh * _sig(zh)) for (r, ohat, zh) in parts], axis=1)
    return hr, y_rg, y_gdn, parts


def _outproj(x1, hf, hb, gate, of, ob, z, gn, wout, name):
    s = x1.shape[0]
    t = min(256, s)

    def body(x_ref, hf_ref, hb_ref, gate_ref, of_ref, ob_ref, z_ref, gn_ref, w_ref, xo_ref, y_ref):
        _, y_rg, y_gdn, _ = _mix_out_values(hf_ref[...], hb_ref[...], gate_ref[...], of_ref[...], ob_ref[...],
                                            z_ref[...], gn_ref[...])
        y = jnp.concatenate([y_rg, y_gdn], axis=1).astype(BF16)
        y_ref[...] = y
        xo_ref[...] = x_ref[...] + jnp.dot(y, w_ref[...], preferred_element_type=F32)

    return pl.pallas_call(
        body, name=name, grid=(s // t,),
        in_specs=[_rows(t, D)] + [_rows(t, RGW)] * 6 + [_full((1, DH)), _full((D, D))],
        out_specs=[_rows(t, D), _rows(t, D)], out_shape=[_sds((s, D)), _sds((s, D), BF16)],
        compiler_params=_cparams(1),
    )(x1, hf, hb, gate, of, ob, z, gn, wout)


def _outproj_bwd(dx2, hf, hb, gate, of, ob, z, gn, wout, name):
    s = dx2.shape[0]
    t = min(256, s)

    def body(d_ref, hf_ref, hb_ref, gate_ref, of_ref, ob_ref, z_ref, gn_ref, w_ref,
             dhr_ref, dgate_ref, dos_ref, dz_ref, dgn_ref, db_ref):
        @pl.when(pl.program_id(0) == 0)
        def _():
            dgn_ref[...] = jnp.zeros_like(dgn_ref)

        gate = gate_ref[...]
        gn_v = gn_ref[...]
        hr, _, _, parts = _mix_out_values(hf_ref[...], hb_ref[...], gate, of_ref[...], ob_ref[...], z_ref[...], gn_v)
        dbf = d_ref[...].astype(BF16)
        db_ref[...] = dbf
        dy = _dot_nt(dbf, w_ref[...])
        dyr = dy[:, :RGW]
        dhr_ref[...] = dyr * _gelu(gate)
        dgate_ref[...] = dyr * hr * _gelu_grad(gate)
        dgn = jnp.zeros((1, DH), F32)
        for h, (r, ohat, zh) in enumerate(parts):
            sl = slice(DH * h, DH * (h + 1))
            dyh = dy[:, RGW + DH * h:RGW + DH * (h + 1)]
            sz = zh * _sig(zh)
            dn = dyh * sz
            dz_ref[:, sl] = dyh * ohat * gn_v * _silu_grad(zh)
            dgn = dgn + _colsum(dn * ohat)
            dos_ref[:, sl] = _rms_bwd(dn, ohat, r, gn_v)
        dgn_ref[...] += dgn

    return pl.pallas_call(
        body, name=name, grid=(s // t,),
        in_specs=[_rows(t, D)] + [_rows(t, RGW)] * 6 + [_full((1, DH)), _full((D, D))],
        out_specs=[_rows(t, RGW)] * 4 + [_full((1, DH)), _rows(t, D)],
        out_shape=[_sds((s, RGW))] * 4 + [_sds((1, DH)), _sds((s, D), BF16)],
        compiler_params=_cparams(1),
    )(dx2, hf, hb, gate, of, ob, z, gn, wout)


def _loss_head(x3, target, gain, name):
    s = x3.shape[0]
    t = min(256, s)

    def body(x_ref, t_ref, g_ref, dx_ref, loss_ref, dg_ref):
        @pl.when(pl.program_id(0) == 0)
        def _():
            loss_ref[...] = jnp.zeros_like(loss_ref)
            dg_ref[...] = jnp.zeros_like(dg_ref)

        r, xh = _rms(x_ref[...])
        gv = g_ref[...]
        err = xh * gv - t_ref[...]
        per_tok = jnp.mean(err * err, axis=-1, keepdims=True)
        loss_ref[...] += 0.5 * jnp.sum(per_tok, axis=0, keepdims=True)
        dy = err * (1.0 / D)
        dg_ref[...] += _colsum(dy * xh)
        dx_ref[...] = _rms_bwd(dy, xh, r, gv)

    return pl.pallas_call(
        body, name=name, grid=(s // t,), in_specs=[_rows(t, D), _rows(t, D), _full((1, D))],
        out_specs=[_rows(t, D), _full((8, 128)), _full((1, D))],
        out_shape=[_sds((s, D)), _sds((8, 128)), _sds((1, D))], compiler_params=_cparams(1),
    )(x3, target, gain)


def _adamw(w, g, m, v, name):
    r, c = w.shape
    tr = r
    while tr * c * 4 > (1 << 20) and tr % 16 == 0:
        tr //= 2

    def body(w_ref, g_ref, m_ref, v_ref, d_ref, nm_ref, nv_ref):
        gv = g_ref[...]
        mn = ADAM_B1 * m_ref[...] + (1.0 - ADAM_B1) * gv
        vn = ADAM_B2 * v_ref[...] + (1.0 - ADAM_B2) * (gv * gv)
        m_hat = mn / (1.0 - ADAM_B1 ** ADAM_STEP)
        v_hat = vn / (1.0 - ADAM_B2 ** ADAM_STEP)
        d_ref[...] = -ADAM_LR * (m_hat / (jnp.sqrt(v_hat) + ADAM_EPS) + ADAM_WD * w_ref[...])
        nm_ref[...] = mn
        nv_ref[...] = vn

    return pl.pallas_call(
        body, name=name, grid=(r // tr,), in_specs=[_rows(tr, c)] * 4, out_specs=[_rows(tr, c)] * 3,
        out_shape=[_sds((r, c))] * 3, compiler_params=_cparams(1),
    )(w, g, m, v)


def _mesh_pos():
    return lax.axis_index("x"), lax.axis_index("y"), lax.axis_index("c")


def _other_chips(x, y):
    return [(1 - x, y), (x, 1 - y), (1 - x, 1 - y)]


def _all_gather(n_arr, space, out_shapes, block_of, name):
    def body(*refs):
        x_refs, out_refs = refs[:n_arr], refs[n_arr:2 * n_arr]
        send_sems, recv_sems, local_sems = refs[2 * n_arr:]
        x, y, c = _mesh_pos()
        me, sibling = (x, y, c), (x, y, 1 - c)
        chips = _other_chips(x, y)

        def slot(a, px, py, pc):
            return out_refs[a].at[4 * px + 2 * py + pc]

        def copy(a, k, block, to, src=None):
            return pltpu.make_async_remote_copy(
                src_ref=slot(a, *block) if src is None else src, dst_ref=slot(a, *block),
                send_sem=send_sems.at[7 * a + k], recv_sem=recv_sems.at[7 * a + k], device_id=to, device_id_type=MESH)

        srcs = [block_of(a, x_refs[a], c) for a in range(n_arr)]
        local = [pltpu.make_async_copy(srcs[a], slot(a, *me), local_sems.at[a]) for a in range(n_arr)]
        for cp in local:
            cp.start()
        first = []
        for a in range(n_arr):
            first += [copy(a, 1 + j, me, (*chip, c), src=srcs[a]) for j, chip in enumerate(chips)]
            first.append(copy(a, 0, me, sibling, src=srcs[a]))
        for cp in first:
            cp.start()
        passed = []
        for j, chip in enumerate(chips):
            for a in range(n_arr):
                copy(a, 1 + j, (*chip, c), me).wait_recv()
                fwd = copy(a, 4 + j, (*chip, c), sibling)
                fwd.start()
                passed.append(fwd)
        for a in range(n_arr):
            copy(a, 0, sibling, me).wait_recv()
            for j, chip in enumerate(chips):
                copy(a, 4 + j, (*chip, 1 - c), me).wait_recv()
        for cp in first + passed:
            cp.wait_send()
        for cp in local:
            cp.wait()

    return pl.pallas_call(
        body, name=name, out_shape=out_shapes,
        in_specs=[pl.BlockSpec(memory_space=space)] * n_arr, out_specs=[pl.BlockSpec(memory_space=space)] * n_arr,
        scratch_shapes=[pltpu.SemaphoreType.DMA((7 * n_arr,)), pltpu.SemaphoreType.DMA((7 * n_arr,)),
                        pltpu.SemaphoreType.DMA((n_arr,))],
    )


def _gather_weights(shards):
    halves = [w.shape[0] // 2 for w in shards]

    def block_of(a, x_ref, c):
        return x_ref.at[pl.ds(pl.multiple_of(c * halves[a], 16), halves[a]), :]

    outs = _all_gather(len(shards), pltpu.HBM, [_sds((8, h, w.shape[1]), BF16) for h, w in zip(halves, shards)],
                       block_of, "gather_weights")(*shards)
    return [o.reshape(NSH, 2 * h, o.shape[2]) for o, h in zip(outs, halves)]


def _gather_small(block, name):
    r, c = block.shape
    return _all_gather(1, pltpu.VMEM, [_sds((8, r, c))], lambda a, x_ref, c_: x_ref, name)(block)[0]


def _sibling_exchange(gs):
    n = len(gs)
    halves = [g.shape[1] // 2 for g in gs]

    def body(*refs):
        g_refs, land_refs = refs[:n], refs[n:2 * n]
        send_sems, recv_sems = refs[2 * n:]
        x, y, c = _mesh_pos()
        copies = []
        for a in range(n):
            h = halves[a]
            for s in range(NSH):
                copies.append(pltpu.make_async_remote_copy(
                    src_ref=g_refs[a].at[s, pl.ds(pl.multiple_of((1 - c) * h, 8), h), :], dst_ref=land_refs[a].at[s],
                    send_sem=send_sems.at[NSH * a + s], recv_sem=recv_sems.at[NSH * a + s],
                    device_id=(x, y, 1 - c), device_id_type=MESH))
        for cp in copies:
            cp.start()
        for cp in copies:
            cp.wait()

    return pl.pallas_call(
        body, name="grad_sibling_exchange", out_shape=[_sds((NSH, h, g.shape[2])) for h, g in zip(halves, gs)],
        in_specs=[pl.BlockSpec(memory_space=pltpu.HBM)] * n, out_specs=[pl.BlockSpec(memory_space=pltpu.HBM)] * n,
        scratch_shapes=[pltpu.SemaphoreType.DMA((NSH * n,)), pltpu.SemaphoreType.DMA((NSH * n,))],
    )(*gs)


def _chip_sum(g, land, c_arr, name):
    _, h, cols = land.shape

    def body(c_ref, g_ref, l_ref, o_ref):
        o_ref[...] = (g_ref[...] + l_ref[...]).astype(BF16)

    return pl.pallas_call(
        body, name=name, out_shape=_sds((NSH, h, cols), BF16),
        grid_spec=pltpu.PrefetchScalarGridSpec(
            num_scalar_prefetch=1, grid=(NSH,),
            in_specs=[pl.BlockSpec((1, h, cols), lambda s, c_ref: (s, c_ref[0], 0)),
                      pl.BlockSpec((1, h, cols), lambda s, c_ref: (s, 0, 0))],
            out_specs=pl.BlockSpec((1, h, cols), lambda s, c_ref: (s, 0, 0))),
        compiler_params=_cparams(1),
    )(c_arr, g, land)


def _chip_scatter(parts):
    n = len(parts)

    def body(*refs):
        p_refs, land_refs = refs[:n], refs[n:2 * n]
        send_sems, recv_sems, local_sems = refs[2 * n:]
        x, y, c = _mesh_pos()
        my_chip = 2 * x + y
        local = [pltpu.make_async_copy(p_refs[a].at[my_chip], land_refs[a].at[my_chip], local_sems.at[a]) for a in range(n)]
        for cp in local:
            cp.start()
        copies = []
        for a in range(n):
            for j, (px, py) in enumerate(_other_chips(x, y)):
                copies.append(pltpu.make_async_remote_copy(
                    src_ref=p_refs[a].at[2 * px + py], dst_ref=land_refs[a].at[my_chip],
                    send_sem=send_sems.at[3 * a + j], recv_sem=recv_sems.at[3 * a + j],
                    device_id=(px, py, c), device_id_type=MESH))
        for cp in copies:
            cp.start()
        for cp in copies:
            cp.wait()
        for cp in local:
            cp.wait()

    return pl.pallas_call(
        body, name="grad_chip_scatter", out_shape=[_sds(p.shape, BF16) for p in parts],
        in_specs=[pl.BlockSpec(memory_space=pltpu.HBM)] * n, out_specs=[pl.BlockSpec(memory_space=pltpu.HBM)] * n,
        scratch_shapes=[pltpu.SemaphoreType.DMA((3 * n,)), pltpu.SemaphoreType.DMA((3 * n,)), pltpu.SemaphoreType.DMA((n,))],
    )(*parts)


def _sum_slots(land, name):
    k, r, c = land.shape
    tr = r // 2 if r % 32 == 0 else r

    def body(l_ref, o_ref):
        acc = l_ref[0].astype(F32)
        for i in range(1, k):
            acc = acc + l_ref[i].astype(F32)
        o_ref[...] = acc

    return pl.pallas_call(
        body, name=name, grid=(r // tr,), in_specs=[pl.BlockSpec((k, tr, c), lambda i: (0, i, 0))],
        out_specs=_rows(tr, c), out_shape=_sds((r, c)), compiler_params=_cparams(1),
    )(land)


def _sibling_swap(halves):
    n = len(halves)

    def body(*refs):
        h_refs, out_refs = refs[:n], refs[n:2 * n]
        send_sems, recv_sems, local_sems = refs[2 * n:]
        x, y, c = _mesh_pos()
        local = [pltpu.make_async_copy(h_refs[a], out_refs[a].at[c], local_sems.at[a]) for a in range(n)]
        for cp in local:
            cp.start()
        copies = [pltpu.make_async_remote_copy(
            src_ref=h_refs[a], dst_ref=out_refs[a].at[c], send_sem=send_sems.at[a], recv_sem=recv_sems.at[a],
            device_id=(x, y, 1 - c), device_id_type=MESH) for a in range(n)]
        for cp in copies:
            cp.start()
        for cp in copies:
            cp.wait()
        for cp in local:
            cp.wait()

    outs = pl.pallas_call(
        body, name="grad_sibling_swap", out_shape=[_sds((2,) + h.shape) for h in halves],
        in_specs=[pl.BlockSpec(memory_space=pltpu.HBM)] * n, out_specs=[pl.BlockSpec(memory_space=pltpu.HBM)] * n,
        scratch_shapes=[pltpu.SemaphoreType.DMA((n,)), pltpu.SemaphoreType.DMA((n,)), pltpu.SemaphoreType.DMA((n,))],
    )(*halves)
    return [o.reshape(2 * o.shape[1], o.shape[2]) for o in outs]


def _pad_rows(v, width):
    flat = v.reshape(-1)
    rows = -(-flat.shape[0] // width)
    rows = -(-rows // 8) * 8
    return jnp.pad(flat, (0, rows * width - flat.shape[0])).reshape(rows, width)


def _block_diag(w):
    eye = jnp.eye(8, dtype=w.dtype)
    return (w[:, :, None, :] * eye[:, None, :, None]).reshape(RGW, RGW)


def _diag_blocks(dense):
    r = dense.reshape(8, 64, 8, 64)
    return jnp.stack([r[n, :, n, :] for n in range(8)])


def _lane_row(v8):
    return jnp.zeros((1, BAP), F32).at[0, 8:16].set(v8.reshape(8))


def _local_step(x, target, wts):
    (g1, wg1, wu1, wd1, gmix, w_in_groups, wout, rg_cw8, rg_cb, wgates, gbias, lam_row, gdn_cw8,
     alog_row, dtb_row, gn, g2, wg2, wu2, wd2, gfin) = wts
    s = x.shape[0]

    x1, a1, b1 = _ffn_fwd(x, g1, wg1, wu1, wd1, "ffn1_fwd")
    h2, p_rgx, p_gate, p_qkv, p_z, p_ba = _inproj(x1, gmix, w_in_groups, "in_proj")
    c_rg = _conv(p_rgx, rg_cw8, rg_cb, "rg_conv")
    c_qkv = _conv(p_qkv, gdn_cw8, jnp.zeros((1, QKVW), F32), "gdn_conv")
    a0, bb0, a1s, bb1, q, k, v, bg = _mix_prep(c_rg, c_qkv, p_ba, wgates, gbias, lam_row, alog_row, dtb_row, "mix_prep")
    hf = _scan(a0, bb0, False, "rg_scan_f")
    hb = _scan(a1s, bb1, True, "rg_scan_b")
    of, t0, s0 = _gdn_fwd(q, k, v, bg, 0, "gdn_fwd_f")
    ob, t1, s1 = _gdn_fwd(q, k, v, bg, 1, "gdn_fwd_b")
    x2, ymix = _outproj(x1, hf, hb, p_gate, of, ob, p_z, gn, wout, "out_proj")
    x3, a2, b2 = _ffn_fwd(x2, g2, wg2, wu2, wd2, "ffn2_fwd")
    dx3, loss_blk, d_gfin = _loss_head(x3, target, gfin, "loss_head")

    dx2, d_g2, hb2, dob2, fb2, dab2, dbb2 = _ffn_bwd(x2, dx3, g2, a2, b2, wg2, wu2, wd2, "ffn2_bwd")
    d_wg2 = _tn(hb2, dab2, "ffn2_dwg")
    d_wu2 = _tn(hb2, dbb2, "ffn2_dwu")
    d_wd2 = _tn(fb2, dob2, "ffn2_dwd")

    d_hr, d_gate, d_os, d_z, d_gn, dx2b = _outproj_bwd(dx2, hf, hb, p_gate, of, ob, p_z, gn, wout, "out_proj_bwd")
    d_wout = _tn(ymix, dx2b, "dw_out")[0]

    a0_up = _shift_rows(a0, 1, "shift_a0")
    a1_dn = _shift_rows(a1s, -1, "shift_a1")
    hf_dn = _shift_rows(hf, -1, "shift_hf")
    hb_up = _shift_rows(hb, 1, "shift_hb")
    lam0 = _scan(a0_up, d_hr, True, "rg_scan_f_bwd")
    lam1 = _scan(a1_dn, d_hr, False, "rg_scan_b_bwd")
    d_xc, d_pre, xcb, d_gbias, d_lam = _gates_bwd(c_rg, wgates, gbias, lam_row, lam0, lam1, hf_dn, hb_up, "rg_gates_bwd")
    d_wgates = _tn(xcb, d_pre, "dw_gates")[0]
    d_prgx, d_rgcw8, d_rgcb = _conv_bwd(p_rgx, d_xc, rg_cw8, "rg_conv_bwd")

    dq0, dk0, dv0, dbg0 = _gdn_bwd(q, k, v, bg, t0, s0, d_os, 0, "gdn_bwd_f")
    dq1, dk1, dv1, dbg1 = _gdn_bwd(q, k, v, bg, t1, s1, d_os, 1, "gdn_bwd_b")
    d_cqkv, d_pba, d_alog, d_dtb = _prep_bwd(c_qkv, p_ba, alog_row, dtb_row, (dq0, dq1), (dk0, dk1), (dv0, dv1),
                                             (dbg0, dbg1), "gdn_prep_bwd")
    d_pqkv, d_gdncw8, _ = _conv_bwd(p_qkv, d_cqkv, gdn_cw8, "gdn_conv_bwd")

    dps = (d_prgx, d_gate, d_pqkv, d_z, d_pba)
    dx1, d_gmix = _inproj_bwd(x1, dx2, gmix, dps, w_in_groups, "in_proj_bwd")
    d_win_groups = [_tn(h2, dp, "dw_in_%d" % i)[0] for i, dp in enumerate(dps)]

    gx, d_g1, hb1, dob1, fb1, dab1, dbb1 = _ffn_bwd(x, dx1, g1, a1, b1, wg1, wu1, wd1, "ffn1_bwd")
    d_wg1 = _tn(hb1, dab1, "ffn1_dwg")
    d_wu1 = _tn(hb1, dbb1, "ffn1_dwu")
    d_wd1 = _tn(fb1, dob1, "ffn1_dwd")

    d_win = jnp.concatenate(d_win_groups[:4] + [d_win_groups[4][:, :BAW]], axis=1)
    big = (d_wg1, d_wu1, d_wd1, d_win, d_wout, d_wg2, d_wu2, d_wd2)
    small = dict(
        ffn1_norm=d_g1, mix_norm=d_gmix, rg_conv_w=d_rgcw8[:4], rg_conv_b=d_rgcb,
        rg_gate_a_w=jnp.stack([_diag_blocks(d_wgates[:, RGW * i:RGW * (i + 1)]) for i in (0, 1)]),
        rg_gate_x_w=jnp.stack([_diag_blocks(d_wgates[:, RGW * i:RGW * (i + 1)]) for i in (2, 3)]),
        rg_gate_a_b=d_gbias[0, :2 * RGW].reshape(2, RGW), rg_gate_x_b=d_gbias[0, 2 * RGW:].reshape(2, RGW),
        rg_lambda=d_lam.reshape(2, RGW), gdn_conv_w=d_gdncw8[:4],
        gdn_a_log=d_alog[0, 8:16].reshape(2, NH), gdn_dt_bias=d_dtb[0, 8:16].reshape(2, NH),
        gdn_norm=d_gn, ffn2_norm=d_g2, final_norm=d_gfin)
    return loss_blk, gx, big, small


_SMALL_NAMES = ("ffn1_norm", "mix_norm", "rg_conv_w", "rg_conv_b", "rg_gate_a_w", "rg_gate_a_b", "rg_gate_x_w",
                "rg_gate_x_b", "rg_lambda", "gdn_conv_w", "gdn_a_log", "gdn_dt_bias", "gdn_norm", "ffn2_norm", "final_norm")
_SMALL_SHARDED = dict(rg_conv_w=128, rg_gate_a_b=128, rg_gate_x_b=128, rg_lambda=128, gdn_conv_w=384)
_OUT_ORDER = ("ffn1_norm", "ffn1_w_gate", "ffn1_w_up", "ffn1_w_down", "mix_norm", "w_in", "w_out", "rg_conv_w", "rg_conv_b",
              "rg_gate_a_w", "rg_gate_a_b", "rg_gate_x_w", "rg_gate_x_b", "rg_lambda", "gdn_conv_w", "gdn_a_log",
              "gdn_dt_bias", "gdn_norm", "ffn2_norm", "ffn2_w_gate", "ffn2_w_up", "ffn2_w_down", "final_norm")
_BIG_NAMES = ("ffn1_w_gate", "ffn1_w_up", "ffn1_w_down", "w_in", "w_out", "ffn2_w_gate", "ffn2_w_up", "ffn2_w_down")


def kernel(x, ffn1_norm, ffn1_w_gate, ffn1_w_up, ffn1_w_down, mix_norm, w_in, w_out, rg_conv_w, rg_conv_b, rg_gate_a_w, rg_gate_a_b, rg_gate_x_w, rg_gate_x_b, rg_lambda, gdn_conv_w, gdn_a_log, gdn_dt_bias, gdn_norm, ffn2_norm, ffn2_w_gate, ffn2_w_up, ffn2_w_down, final_norm, loss_target, m_ffn1_norm, m_ffn1_w_gate, m_ffn1_w_up, m_ffn1_w_down, m_mix_norm, m_w_in, m_w_out, m_rg_conv_w, m_rg_conv_b, m_rg_gate_a_w, m_rg_gate_a_b, m_rg_gate_x_w, m_rg_gate_x_b, m_rg_lambda, m_gdn_conv_w, m_gdn_a_log, m_gdn_dt_bias, m_gdn_norm, m_ffn2_norm, m_ffn2_w_gate, m_ffn2_w_up, m_ffn2_w_down, m_final_norm, v_ffn1_norm, v_ffn1_w_gate, v_ffn1_w_up, v_ffn1_w_down, v_mix_norm, v_w_in, v_w_out, v_rg_conv_w, v_rg_conv_b, v_rg_gate_a_w, v_rg_gate_a_b, v_rg_gate_x_w, v_rg_gate_x_b, v_rg_lambda, v_gdn_conv_w, v_gdn_a_log, v_gdn_dt_bias, v_gdn_norm, v_ffn2_norm, v_ffn2_w_gate, v_ffn2_w_up, v_ffn2_w_down, v_final_norm):
    args = dict(locals())
    w = {n: args[n] for n in _OUT_ORDER}
    mom = {n: args["m_" + n] for n in _OUT_ORDER}
    var = {n: args["v_" + n] for n in _OUT_ORDER}
    xi, yi, ci = _mesh_pos()
    shard = 2 * xi + yi

    wg1, wu1, wd1, win_sh, wout_sh, wg2, wu2, wd2 = _gather_weights([w[n][0].astype(BF16) for n in _BIG_NAMES])
    w_in_full = jnp.transpose(win_sh, (1, 0, 2)).reshape(D, NSH * INSH)
    w_out_full = wout_sh.reshape(D, D)
    sm_local = _pad_rows(jnp.concatenate([w[n][0].reshape(-1) for n in _SMALL_SHARDED]), 128)
    sm_all = _gather_small(sm_local, "gather_small_weights")[0::2].reshape(NSH, -1)
    sm_full, off = {}, 0
    for n, wd_ in _SMALL_SHARDED.items():
        rows = w[n].shape[1]
        piece = sm_all[:, off:off + rows * wd_].reshape(NSH, rows, wd_)
        sm_full[n] = jnp.transpose(piece, (1, 0, 2)).reshape(rows, NSH * wd_)
        off += rows * wd_

    w_in_groups = (w_in_full[:, 0:512], w_in_full[:, 512:1024], w_in_full[:, 1024:2560], w_in_full[:, 2560:3072],
                   jnp.pad(w_in_full[:, 3072:3088], ((0, 0), (0, BAP - BAW))))
    wa, wx = rg_gate_a_w[0], rg_gate_x_w[0]
    wgates = jnp.concatenate([_block_diag(wa[0]), _block_diag(wa[1]), _block_diag(wx[0]), _block_diag(wx[1])],
                             axis=1).astype(BF16)
    gbias = jnp.concatenate([sm_full["rg_gate_a_b"].reshape(1, -1), sm_full["rg_gate_x_b"].reshape(1, -1)], axis=1)
    wts = (ffn1_norm, wg1, wu1, wd1, mix_norm, w_in_groups, w_out_full,
           jnp.pad(sm_full["rg_conv_w"], ((0, 4), (0, 0))), rg_conv_b, wgates, gbias, sm_full["rg_lambda"].reshape(1, -1),
           jnp.pad(sm_full["gdn_conv_w"], ((0, 4), (0, 0))), _lane_row(gdn_a_log), _lane_row(gdn_dt_bias),
           gdn_norm, ffn2_norm, wg2, wu2, wd2, final_norm.reshape(1, D))

    loss_blk, gx, big, small = _local_step(x[0], loss_target[0], wts)
    loss = lax.psum(loss_blk[0, 0], ("x", "y", "c"))

    d_wg1, d_wu1, d_wd1, d_win, d_wout, d_wg2, d_wu2, d_wd2 = big
    gs = [d_wg1, d_wu1, d_wd1, jnp.transpose(d_win.reshape(D, NSH, INSH), (1, 0, 2)), d_wout.reshape(NSH, OUTSH, D),
          d_wg2, d_wu2, d_wd2]
    lands = _sibling_exchange(gs)
    c_arr = ci.reshape(1).astype(jnp.int32)
    parts = [_chip_sum(g, l, c_arr, "chip_sum_" + n) for g, l, n in zip(gs, lands, _BIG_NAMES)]
    halves = [_sum_slots(l, "sum_chips_" + n) for l, n in zip(_chip_scatter(parts), _BIG_NAMES)]
    grads = {n: g[None] for n, g in zip(_BIG_NAMES, _sibling_swap(halves))}

    sm_sizes = [(n, small[n].shape) for n in _SMALL_NAMES]
    sm_grad = _pad_rows(jnp.concatenate([small[n].reshape(-1) for n in _SMALL_NAMES]), D)
    sm_sum = _sum_slots(_gather_small(sm_grad, "gather_small_grads"), "small_grad_sum").reshape(-1)
    off = 0
    for n, shp in sm_sizes:
        cnt = 1
        for dsz in shp:
            cnt *= dsz
        g = sm_sum[off:off + cnt].reshape(shp)
        off += cnt
        if n in _SMALL_SHARDED:
            wd_ = _SMALL_SHARDED[n]
            g = lax.dynamic_slice_in_dim(g, shard * wd_, wd_, axis=1)
        grads[n] = g.reshape(w[n].shape)

    delta, new_m, new_v = {}, {}, {}
    for n in _BIG_NAMES:
        shp = w[n].shape
        d_, m_, v_ = _adamw(w[n][0], grads[n][0], mom[n][0], var[n][0], "adamw_" + n)
        delta[n], new_m[n], new_v[n] = d_.reshape(shp), m_.reshape(shp), v_.reshape(shp)
    packs = [_pad_rows(jnp.concatenate([t[n].reshape(-1) for n in _SMALL_NAMES]), D) for t in (w, grads, mom, var)]
    sm_out = _adamw(*packs, "adamw_small")
    off = 0
    for n in _SMALL_NAMES:
        cnt = w[n].size
        for dst, src in zip((delta, new_m, new_v), sm_out):
            dst[n] = src.reshape(-1)[off:off + cnt].reshape(w[n].shape)
        off += cnt

    outs = [loss, gx[None]]
    for group in (grads, delta, new_m, new_v):
        outs += [group[n] for n in _OUT_ORDER]
    return tuple(outs)
```

```python
import functools

import jax
import jax.numpy as jnp
from jax import lax
from jax.experimental import pallas as pl
from jax.experimental.pallas import tpu as pltpu

F32 = jnp.float32
BF16 = jnp.bfloat16
EPS = 1e-6
D = 1024
NSH = 4
FSH = 704
RGW = 512
QKVW = 1536
ZW = 512
BAW = 16
BAP = 128
INSH = 772
OUTSH = 256
CHUNK = 64
NH = 4
DH = 128
RG_C = 8.0
VMEM_LIMIT = 52 * 1024 * 1024
MESH = pl.DeviceIdType.MESH

ADAM_LR = 0.001
ADAM_B1 = 0.9
ADAM_B2 = 0.999
ADAM_EPS = 1e-08
ADAM_WD = 0.01
ADAM_STEP = 10


def _cparams(n_grid):
    return pltpu.CompilerParams(dimension_semantics=("arbitrary",) * n_grid, vmem_limit_bytes=VMEM_LIMIT)


def _sig(x):
    return 1.0 / (1.0 + jnp.exp(-x))


def _softplus(x):
    return jnp.maximum(x, 0.0) + jnp.log(1.0 + jnp.exp(-jnp.abs(x)))


def _neg_expm1(y):
    series = -y * (1.0 + y * (0.5 + y * (1.0 / 6 + y * (1.0 / 24 + y * (1.0 / 120 + y * (1.0 / 720 + y / 5040))))))
    return jnp.where(y > -0.3, series, 1.0 - jnp.exp(y))


_GELU_C = 0.7978845608028654


def _gelu(x):
    t = jnp.tanh(_GELU_C * (x + 0.044715 * x * x * x))
    return 0.5 * x * (1.0 + t)


def _gelu_grad(x):
    t = jnp.tanh(_GELU_C * (x + 0.044715 * x * x * x))
    return 0.5 * (1.0 + t) + 0.5 * x * (1.0 - t * t) * _GELU_C * (1.0 + 3 * 0.044715 * x * x)


def _silu_grad(x):
    s = _sig(x)
    return s * (1.0 + x * (1.0 - s))


def _dot(a, b):
    return jnp.dot(a.astype(BF16), b.astype(BF16), preferred_element_type=F32)


def _dot_nt(a, b):
    return lax.dot_general(a.astype(BF16), b.astype(BF16), (((1,), (1,)), ((), ())), preferred_element_type=F32)


def _dot_tn(a, b):
    return lax.dot_general(a.astype(BF16), b.astype(BF16), (((0,), (0,)), ((), ())), preferred_element_type=F32)


_NN = ((1,), (0,))
_NT = ((1,), (1,))
_TN = ((0,), (0,))


def _dg(a, b, dims):
    return lax.dot_general(a, b, (dims, ((), ())), preferred_element_type=F32)


def _split2(a):
    hi = a.astype(BF16)
    return hi, (a - hi.astype(F32)).astype(BF16)


def _dot3(a, b, dims=_NN):
    ah, al = _split2(a)
    bh, bl = _split2(b)
    return _dg(ah, bh, dims) + _dg(ah, bl, dims) + _dg(al, bh, dims)


def _dot_exact(e, x, dims, e_is_lhs):
    x0 = x.astype(BF16)
    r = x - x0.astype(F32)
    x1 = r.astype(BF16)
    x2 = (r - x1.astype(F32)).astype(BF16)
    eb = e.astype(BF16)
    if e_is_lhs:
        return _dg(eb, x0, dims) + _dg(eb, x1, dims) + _dg(eb, x2, dims)
    return _dg(x0, eb, dims) + _dg(x1, eb, dims) + _dg(x2, eb, dims)


def _rms(xv):
    r = lax.rsqrt(jnp.mean(xv * xv, axis=-1, keepdims=True) + EPS)
    return r, xv * r


def _rms_bwd(dy, xh, r, gain):
    dxh = dy * gain
    return r * (dxh - xh * jnp.mean(dxh * xh, axis=-1, keepdims=True))


def _colsum(v):
    return jnp.sum(v, axis=0, keepdims=True)


def _rows(t, c):
    return pl.BlockSpec((t, c), lambda i: (i, 0))


def _full(shape):
    n = len(shape)
    return pl.BlockSpec(shape, lambda i: (0,) * n)


def _sds(shape, dtype=F32):
    return jax.ShapeDtypeStruct(shape, dtype)


def _ffn_fwd(x, gain, wg, wu, wd, name):
    s = x.shape[0]
    tm = min(512, s)

    def body(x_ref, g_ref, wg_ref, wu_ref, wd_ref, xo_ref, a_ref, b_ref, h_sc, acc):
        j = pl.program_id(1)

        @pl.when(j == 0)
        def _():
            _, xh = _rms(x_ref[...])
            h_sc[...] = (xh * g_ref[...]).astype(BF16)
            acc[...] = jnp.zeros_like(acc)

        h = h_sc[...]
        a = jnp.dot(h, wg_ref[0], preferred_element_type=F32)
        b = jnp.dot(h, wu_ref[0], preferred_element_type=F32)
        a_ref[0] = a
        b_ref[0] = b
        f = (a * _sig(a) * b).astype(BF16)
        acc[...] += jnp.dot(f, wd_ref[0], preferred_element_type=F32)

        @pl.when(j == NSH - 1)
        def _():
            xo_ref[...] = x_ref[...] + 0.5 * acc[...]

    return pl.pallas_call(
        body, name=name, grid=(s // tm, NSH),
        in_specs=[pl.BlockSpec((tm, D), lambda i, j: (i, 0)), pl.BlockSpec((1, D), lambda i, j: (0, 0)),
                  pl.BlockSpec((1, D, FSH), lambda i, j: (j, 0, 0)), pl.BlockSpec((1, D, FSH), lambda i, j: (j, 0, 0)),
                  pl.BlockSpec((1, FSH, D), lambda i, j: (j, 0, 0))],
        out_specs=[pl.BlockSpec((tm, D), lambda i, j: (i, 0)), pl.BlockSpec((1, tm, FSH), lambda i, j: (j, i, 0)),
                   pl.BlockSpec((1, tm, FSH), lambda i, j: (j, i, 0))],
        out_shape=[_sds((s, D)), _sds((NSH, s, FSH)), _sds((NSH, s, FSH))],
        scratch_shapes=[pltpu.VMEM((tm, D), BF16), pltpu.VMEM((tm, D), F32)],
        compiler_params=_cparams(2),
    )(x, gain, wg, wu, wd)


def _ffn_bwd(x, dout, gain, a, b, wg, wu, wd, name):
    s = x.shape[0]
    tm = min(512, s)

    def body(x_ref, d_ref, g_ref, a_ref, b_ref, wg_ref, wu_ref, wd_ref,
             dx_ref, dg_ref, h_ref, do_ref, f_ref, da_ref, db_ref, do_sc, dh_acc):
        i = pl.program_id(0)
        j = pl.program_id(1)

        @pl.when(jnp.logical_and(i == 0, j == 0))
        def _():
            dg_ref[...] = jnp.zeros_like(dg_ref)

        @pl.when(j == 0)
        def _():
            _, xh = _rms(x_ref[...])
            h_ref[...] = (xh * g_ref[...]).astype(BF16)
            do = (0.5 * d_ref[...]).astype(BF16)
            do_sc[...] = do
            do_ref[...] = do
            dh_acc[...] = jnp.zeros_like(dh_acc)

        do = do_sc[...]
        df = _dot_nt(do, wd_ref[0])
        av = a_ref[0]
        bv = b_ref[0]
        sa = _sig(av)
        f_ref[0] = (av * sa * bv).astype(BF16)
        da = (df * bv * sa * (1.0 + av * (1.0 - sa))).astype(BF16)
        db = (df * av * sa).astype(BF16)
        da_ref[0] = da
        db_ref[0] = db
        dh_acc[...] += _dot_nt(da, wg_ref[0]) + _dot_nt(db, wu_ref[0])

        @pl.when(j == NSH - 1)
        def _():
            r, xh = _rms(x_ref[...])
            dh = dh_acc[...]
            dg_ref[...] += _colsum(dh * xh)
            dx_ref[...] = d_ref[...] + _rms_bwd(dh, xh, r, g_ref[...])

    tok = pl.BlockSpec((tm, D), lambda i, j: (i, 0))
    sh = pl.BlockSpec((1, tm, FSH), lambda i, j: (j, i, 0))
    return pl.pallas_call(
        body, name=name, grid=(s // tm, NSH),
        in_specs=[tok, tok, pl.BlockSpec((1, D), lambda i, j: (0, 0)), sh, sh,
                  pl.BlockSpec((1, D, FSH), lambda i, j: (j, 0, 0)), pl.BlockSpec((1, D, FSH), lambda i, j: (j, 0, 0)),
                  pl.BlockSpec((1, FSH, D), lambda i, j: (j, 0, 0))],
        out_specs=[tok, pl.BlockSpec((1, D), lambda i, j: (0, 0)), tok, tok, sh, sh, sh],
        out_shape=[_sds((s, D)), _sds((1, D)), _sds((s, D), BF16), _sds((s, D), BF16),
                   _sds((NSH, s, FSH), BF16), _sds((NSH, s, FSH), BF16), _sds((NSH, s, FSH), BF16)],
        scratch_shapes=[pltpu.VMEM((tm, D), BF16), pltpu.VMEM((tm, D), F32)],
        compiler_params=_cparams(2),
    )(x, dout, gain, a, b, wg, wu, wd)


def _tn(a, b, name):
    a_g = a.ndim == 3
    b_g = b.ndim == 3
    g = a.shape[0] if a_g else (b.shape[0] if b_g else 1)
    s, k = a.shape[-2:]
    n = b.shape[-1]
    ts = min(512, s)

    def body(a_ref, b_ref, o_ref):
        @pl.when(pl.program_id(1) == 0)
        def _():
            o_ref[...] = jnp.zeros_like(o_ref)

        av = a_ref[0] if a_g else a_ref[...]
        bv = b_ref[0] if b_g else b_ref[...]
        o_ref[0] += _dot_tn(av, bv)

    a_spec = pl.BlockSpec((1, ts, k), lambda gi, si: (gi, si, 0)) if a_g else pl.BlockSpec((ts, k), lambda gi, si: (si, 0))
    b_spec = pl.BlockSpec((1, ts, n), lambda gi, si: (gi, si, 0)) if b_g else pl.BlockSpec((ts, n), lambda gi, si: (si, 0))
    return pl.pallas_call(
        body, name=name, grid=(g, s // ts), in_specs=[a_spec, b_spec],
        out_specs=pl.BlockSpec((1, k, n), lambda gi, si: (gi, 0, 0)),
        out_shape=_sds((g, k, n)), compiler_params=_cparams(2),
    )(a, b)


_P_WIDTHS = (RGW, RGW, QKVW, ZW, BAP)


def _inproj(x1, gain, ws, name):
    s = x1.shape[0]
    tm = min(256, s)

    def body(x_ref, g_ref, *refs):
        w_refs = refs[:5]
        h_ref = refs[5]
        p_refs = refs[6:]
        _, xh = _rms(x_ref[...])
        h = (xh * g_ref[...]).astype(BF16)
        h_ref[...] = h
        for w_ref, p_ref in zip(w_refs, p_refs):
            p_ref[...] = jnp.dot(h, w_ref[...], preferred_element_type=F32)

    return pl.pallas_call(
        body, name=name, grid=(s // tm,),
        in_specs=[_rows(tm, D), _full((1, D))] + [_full((D, w)) for w in _P_WIDTHS],
        out_specs=[_rows(tm, D)] + [_rows(tm, w) for w in _P_WIDTHS],
        out_shape=[_sds((s, D), BF16)] + [_sds((s, w)) for w in _P_WIDTHS],
        compiler_params=_cparams(1),
    )(x1, gain, *ws)


def _inproj_bwd(x1, dx2, gain, dps, ws, name):
    s = x1.shape[0]
    tm = min(256, s)

    def body(x_ref, d_ref, g_ref, *refs):
        dp_refs = refs[:5]
        w_refs = refs[5:10]
        dx_ref, dg_ref = refs[10:]

        @pl.when(pl.program_id(0) == 0)
        def _():
            dg_ref[...] = jnp.zeros_like(dg_ref)

        dh = jnp.zeros((tm, D), F32)
        for dp_ref, w_ref in zip(dp_refs, w_refs):
            dh = dh + _dot_nt(dp_ref[...], w_ref[...])
        r, xh = _rms(x_ref[...])
        dg_ref[...] += _colsum(dh * xh)
        dx_ref[...] = d_ref[...] + _rms_bwd(dh, xh, r, g_ref[...])

    return pl.pallas_call(
        body, name=name, grid=(s // tm,),
        in_specs=[_rows(tm, D), _rows(tm, D), _full((1, D))] + [_rows(tm, w) for w in _P_WIDTHS]
        + [_full((D, w)) for w in _P_WIDTHS],
        out_specs=[_rows(tm, D), _full((1, D))],
        out_shape=[_sds((s, D)), _sds((1, D))],
        compiler_params=_cparams(1),
    )(x1, dx2, gain, *dps, *ws)


def _halo_specs(s, t, c):
    nb8 = s // 8
    tb = t // 8
    prev = pl.BlockSpec((8, c), lambda i: (jnp.maximum(i * tb - 1, 0), 0))
    nxt = pl.BlockSpec((8, c), lambda i: (jnp.minimum((i + 1) * tb, nb8 - 1), 0))
    return prev, nxt


def _edge_masks(nb):
    i = pl.program_id(0)
    return jnp.where(i > 0, 1.0, 0.0).astype(F32), jnp.where(i < nb - 1, 1.0, 0.0).astype(F32)


def _shifted(xx, off, t):
    n = t + 16
    sh = (-off) % n
    rolled = xx if sh == 0 else pltpu.roll(xx, sh, 0)
    return rolled[8:8 + t]


def _conv(x, w8, bias, name):
    s, c = x.shape
    t = min(256, s)
    nb = s // t

    def body(x_ref, xp_ref, xn_ref, w_ref, b_ref, o_ref):
        pm, nm = _edge_masks(nb)
        for c0 in range(0, c, 512):
            cols = slice(c0, c0 + 512)
            xx = jnp.concatenate([xp_ref[:, cols] * pm, x_ref[:, cols], xn_ref[:, cols] * nm], axis=0)
            acc = jnp.zeros((t, 512), F32) + b_ref[:, cols]
            for j in range(4):
                acc = acc + w_ref[j:j + 1, cols] * _shifted(xx, j - 2, t)
            o_ref[:, cols] = acc

    prev, nxt = _halo_specs(s, t, c)
    return pl.pallas_call(
        body, name=name, grid=(nb,),
        in_specs=[_rows(t, c), prev, nxt, _full((8, c)), _full((1, c))],
        out_specs=_rows(t, c), out_shape=_sds((s, c)), compiler_params=_cparams(1),
    )(x, x, x, w8, bias)


def _conv_bwd(x, dc, w8, name):
    s, c = x.shape
    t = min(256, s)
    nb = s // t

    def body(x_ref, d_ref, dp_ref, dn_ref, w_ref, dx_ref, dw_ref, db_ref):
        @pl.when(pl.program_id(0) == 0)
        def _():
            dw_ref[...] = jnp.zeros_like(dw_ref)
            db_ref[...] = jnp.zeros_like(db_ref)

        pm, nm = _edge_masks(nb)
        for c0 in range(0, c, 512):
            cols = slice(c0, c0 + 512)
            dd = jnp.concatenate([dp_ref[:, cols] * pm, d_ref[:, cols], dn_ref[:, cols] * nm], axis=0)
            xv = x_ref[:, cols]
            acc = jnp.zeros((t, 512), F32)
            for j in range(4):
                dsh = _shifted(dd, 2 - j, t)
                acc = acc + w_ref[j:j + 1, cols] * dsh
                dw_ref[j:j + 1, cols] += _colsum(dsh * xv)
            dx_ref[:, cols] = acc
            db_ref[:, cols] += _colsum(d_ref[:, cols])

    prev, nxt = _halo_specs(s, t, c)
    return pl.pallas_call(
        body, name=name, grid=(nb,),
        in_specs=[_rows(t, c), _rows(t, c), prev, nxt, _full((8, c))],
        out_specs=[_rows(t, c), _full((8, c)), _full((1, c))],
        out_shape=[_sds((s, c)), _sds((8, c)), _sds((1, c))], compiler_params=_cparams(1),
    )(x, dc, dc, dc, w8)


def _shift_rows(x, direction, name):
    s, c = x.shape
    t = min(256, s)
    nb = s // t

    def body(x_ref, xp_ref, xn_ref, o_ref):
        pm, nm = _edge_masks(nb)
        xx = jnp.concatenate([xp_ref[...] * pm, x_ref[...], xn_ref[...] * nm], axis=0)
        o_ref[...] = _shifted(xx, direction, t)

    prev, nxt = _halo_specs(s, t, c)
    return pl.pallas_call(
        body, name=name, grid=(nb,), in_specs=[_rows(t, c), prev, nxt],
        out_specs=_rows(t, c), out_shape=_sds((s, c)), compiler_params=_cparams(1),
    )(x, x, x)


def _rg_gates(xc, pre, lam_row):
    sp8 = RG_C * _softplus(-lam_row)
    out = []
    for d in range(2):
        r = _sig(pre[:, RGW * d:RGW * (d + 1)])
        gi = _sig(pre[:, 2 * RGW + RGW * d:2 * RGW + RGW * (d + 1)])
        la = -r * sp8[:, RGW * d:RGW * (d + 1)]
        a = jnp.exp(la)
        mult = jnp.sqrt(_neg_expm1(2.0 * la))
        out.append((r, gi, a, mult))
    return out


def _mix_prep(c_rg, c_qkv, p_ba, wgates, gbias, lam_row, alog_row, dtb_row, name):
    s = c_rg.shape[0]
    t = min(256, s)

    def body(xc_ref, cq_ref, pc_ref, wg_ref, gb_ref, lam_ref, alog_ref, dtb_ref,
             a0_ref, b0_ref, a1_ref, b1_ref, q_ref, k_ref, v_ref, bg_ref):
        xc = xc_ref[...]
        pre = _dot(xc, wg_ref[...]) + gb_ref[...]
        gates = _rg_gates(xc, pre, lam_ref[...])
        for (r, gi, a, mult), a_ref, b_ref in zip(gates, (a0_ref, a1_ref), (b0_ref, b1_ref)):
            a_ref[...] = a
            b_ref[...] = mult * gi * xc
        cq = cq_ref[...]
        sq = cq * _sig(cq)
        for h in range(NH):
            sl = slice(DH * h, DH * (h + 1))
            qh = sq[:, sl]
            q_ref[:, sl] = qh * lax.rsqrt(jnp.sum(qh * qh, axis=-1, keepdims=True) + EPS) * (DH ** -0.5)
            kh = sq[:, RGW + DH * h:RGW + DH * (h + 1)]
            k_ref[:, sl] = kh * lax.rsqrt(jnp.sum(kh * kh, axis=-1, keepdims=True) + EPS)
        v_ref[...] = sq[:, 2 * RGW:]
        pc = pc_ref[...]
        lane = lax.broadcasted_iota(jnp.int32, pc.shape, 1)
        beta = _sig(pc)
        g = -jnp.exp(alog_ref[...]) * _softplus(pc + dtb_ref[...])
        bg_ref[...] = jnp.where(lane < 8, beta, jnp.where(lane < 16, g, 0.0))

    return pl.pallas_call(
        body, name=name, grid=(s // t,),
        in_specs=[_rows(t, RGW), _rows(t, QKVW), _rows(t, BAP), _full((RGW, 4 * RGW)), _full((1, 4 * RGW)),
                  _full((1, 2 * RGW)), _full((1, BAP)), _full((1, BAP))],
        out_specs=[_rows(t, RGW)] * 7 + [_rows(t, BAP)],
        out_shape=[_sds((s, RGW))] * 7 + [_sds((s, BAP))],
        compiler_params=_cparams(1),
    )(c_rg, c_qkv, p_ba, wgates, gbias, lam_row, alog_row, dtb_row)


def _scan(a, b, reverse, name):
    s, c = a.shape
    t = min(512, s)
    nb = s // t
    ng = t // 8
    idx = (lambda i: (nb - 1 - i, 0)) if reverse else (lambda i: (i, 0))

    def body(a_ref, b_ref, h_ref, carry):
        @pl.when(pl.program_id(0) == 0)
        def _():
            carry[...] = jnp.zeros_like(carry)

        row = lax.broadcasted_iota(jnp.int32, (8, c), 0)

        def group(gi, cv):
            g = (ng - 1 - gi) if reverse else gi
            r0 = pl.multiple_of(g * 8, 8)
            av = a_ref[pl.ds(r0, 8), :]
            bv = b_ref[pl.ds(r0, 8), :]
            for k in (1, 2, 4):
                sh = (8 - k) if reverse else k
                m = (row < 8 - k) if reverse else (row >= k)
                a_s = pltpu.roll(av, sh, 0)
                b_s = pltpu.roll(bv, sh, 0)
                bv = jnp.where(m, av * b_s + bv, bv)
                av = jnp.where(m, av * a_s, av)
            hv = av * cv + bv
            h_ref[pl.ds(r0, 8), :] = hv
            return hv[0:1, :] if reverse else hv[7:8, :]

        carry[0:1, :] = lax.fori_loop(0, ng, group, carry[0:1, :])

    return pl.pallas_call(
        body, name=name, grid=(nb,), in_specs=[pl.BlockSpec((t, c), idx), pl.BlockSpec((t, c), idx)],
        out_specs=pl.BlockSpec((t, c), idx), out_shape=_sds((s, c)),
        scratch_shapes=[pltpu.VMEM((8, c), F32)], compiler_params=_cparams(1),
    )(a, b)


def _gates_bwd(xc, wgates, gbias, lam_row, lam0, lam1, h0s, h1s, name):
    s = xc.shape[0]
    t = min(256, s)

    def body(xc_ref, wg_ref, gb_ref, lam_ref, l0_ref, l1_ref, h0_ref, h1_ref,
             dxc_ref, dpre_ref, xcb_ref, dgb_ref, dlam_ref):
        @pl.when(pl.program_id(0) == 0)
        def _():
            dgb_ref[...] = jnp.zeros_like(dgb_ref)
            dlam_ref[...] = jnp.zeros_like(dlam_ref)

        xv = xc_ref[...]
        pre = _dot(xv, wg_ref[...]) + gb_ref[...]
        lam_row_v = lam_ref[...]
        sp8 = RG_C * _softplus(-lam_row_v)
        dsp_dlam = -RG_C * _sig(-lam_row_v)
        gates = _rg_gates(xv, pre, lam_row_v)
        dxc = jnp.zeros((t, RGW), F32)
        dpre_r = []
        dpre_i = []
        for d, ((r, gi, a, mult), l_ref, h_ref) in enumerate(zip(gates, (l0_ref, l1_ref), (h0_ref, h1_ref))):
            dbb = l_ref[...]
            da = dbb * h_ref[...]
            cs = slice(RGW * d, RGW * (d + 1))
            dmult = dbb * gi * xv
            dgi = dbb * mult * xv
            dxc = dxc + dbb * mult * gi
            dla = da * a - dmult * a * a / mult
            dr = -dla * sp8[:, cs]
            dlam_ref[:, cs] += _colsum(-dla * r) * dsp_dlam[:, cs]
            dpre_r.append(dr * r * (1.0 - r))
            dpre_i.append(dgi * gi * (1.0 - gi))
        dpre = jnp.concatenate(dpre_r + dpre_i, axis=1)
        dgb_ref[...] += _colsum(dpre)
        dpre_b = dpre.astype(BF16)
        dpre_ref[...] = dpre_b
        xcb_ref[...] = xv.astype(BF16)
        dxc_ref[...] = dxc + _dot_nt(dpre_b, wg_ref[...])

    return pl.pallas_call(
        body, name=name, grid=(s // t,),
        in_specs=[_rows(t, RGW), _full((RGW, 4 * RGW)), _full((1, 4 * RGW)), _full((1, 2 * RGW))] + [_rows(t, RGW)] * 4,
        out_specs=[_rows(t, RGW), _rows(t, 4 * RGW), _rows(t, RGW), _full((1, 4 * RGW)), _full((1, 2 * RGW))],
        out_shape=[_sds((s, RGW)), _sds((s, 4 * RGW), BF16), _sds((s, RGW), BF16), _sds((1, 4 * RGW)), _sds((1, 2 * RGW))],
        compiler_params=_cparams(1),
    )(xc, wgates, gbias, lam_row, lam0, lam1, h0s, h1s)


class _GdnMasks:
    def __init__(self, d):
        ri = lax.broadcasted_iota(jnp.int32, (CHUNK, CHUNK), 0)
        ci = lax.broadcasted_iota(jnp.int32, (CHUNK, CHUNK), 1)
        self.incl = (ri >= ci) if d == 0 else (ri <= ci)
        self.strict = (ri > ci) if d == 0 else (ri < ci)
        b16 = jnp.right_shift(ri, 4) == jnp.right_shift(ci, 4)
        b32 = jnp.right_shift(ri, 5) == jnp.right_shift(ci, 5)
        self.diag16 = b16
        self.off32 = jnp.logical_and(b32, jnp.logical_not(b16))
        self.off64 = jnp.logical_not(b32)
        self.eye = jnp.where(ri == ci, 1.0, 0.0).astype(F32)
        self.tri = jnp.where(self.incl, 1.0, 0.0).astype(F32)
        self.last = CHUNK - 1 if d == 0 else 0


def _tri_inv(lmat, m):
    n = jnp.where(m.diag16, lmat, 0.0)
    p = m.eye - n
    q = _dot3(n, n)
    p = _dot3(p, m.eye + q)
    q = _dot3(q, q)
    p = _dot3(p, m.eye + q)
    q = _dot3(q, q)
    p = _dot3(p, m.eye + q)
    p = p - _dot3(_dot3(p, jnp.where(m.off32, lmat, 0.0)), p)
    p = p - _dot3(_dot3(p, jnp.where(m.off64, lmat, 0.0)), p)
    return p


def _chunk_cumsums(m, bgv):
    return _dot_exact(m.tri, bgv, _NN, True), _dot_exact(m.tri, bgv, ((0,), (1,)), False)


class _GdnHead:
    def __init__(self, qh, kh, vh, kk, q0, bg, gcs, gcs_t, d, h, m):
        cb = 4 * d + h
        cg = 8 + 4 * d + h
        self.q, self.k, self.v = qh, kh, vh
        self.beta = bg[:, cb:cb + 1]
        gcol = gcs[:, cg:cg + 1]
        grow = gcs_t[cg:cg + 1, :]
        gl = gcs[m.last:m.last + 1, cg:cg + 1]
        self.decay = jnp.exp(jnp.where(m.incl, gcol - grow, -1e30))
        self.kb = kh * self.beta
        self.vb = vh * self.beta
        self.a0 = kk * self.beta
        self.q0 = q0
        self.lmat = jnp.where(m.strict, self.a0 * self.decay, 0.0)
        self.attn = self.q0 * self.decay
        self.eg = jnp.exp(gcol)
        self.ek = jnp.exp(gl - gcol)
        self.cd = jnp.exp(gl)
        self.kg = self.kb * self.eg
        self.qd = qh * self.eg
        self.kd = kh * self.ek


HW = NH * DH
SEQ_CB = 4


def _head(h):
    return slice(DH * h, DH * (h + 1))


def _gdn_local_fwd(q, k, v, bg, name):
    s = q.shape[0]
    n = s // CHUNK

    def body(q_ref, k_ref, v_ref, bg_ref, t_ref, u_ref, w_ref, qd_ref, kd_ref, at_ref, cd_ref):
        bgv = bg_ref[...]
        qs = [q_ref[:, _head(h)] for h in range(NH)]
        ks = [k_ref[:, _head(h)] for h in range(NH)]
        kk = [_dot_nt(ks[h], ks[h]) for h in range(NH)]
        q0 = [_dot_nt(qs[h], ks[h]) for h in range(NH)]
        for d in range(2):
            m = _GdnMasks(d)
            gcs, gcs_t = _chunk_cumsums(m, bgv)
            for h in range(NH):
                sl = _head(h)
                c = _GdnHead(qs[h], ks[h], v_ref[:, sl], kk[h], q0[h], bgv, gcs, gcs_t, d, h, m)
                tm = _tri_inv(c.lmat, m)
                t_ref[0, d, h] = tm
                u_ref[d, :, sl] = _dot(tm, c.vb)
                w_ref[d, :, sl] = _dot(tm, c.kg).astype(BF16)
                qd_ref[d, :, sl] = c.qd.astype(BF16)
                kd_ref[d, :, sl] = c.kd.astype(BF16)
                at_ref[0, d, h] = c.attn.astype(BF16)
                cd_ref[0, 4 * d + h:4 * d + h + 1, :] = jnp.broadcast_to(c.cd, (1, DH))

    tok = _rows(CHUNK, HW)
    tok2 = pl.BlockSpec((2, CHUNK, HW), lambda i: (0, i, 0))
    mat = pl.BlockSpec((1, 2, NH, CHUNK, CHUNK), lambda i: (i, 0, 0, 0, 0))
    return pl.pallas_call(
        body, name=name, grid=(n,), in_specs=[tok, tok, tok, _rows(CHUNK, BAP)],
        out_specs=[mat, tok2, tok2, tok2, tok2, mat, pl.BlockSpec((1, 8, DH), lambda i: (i, 0, 0))],
        out_shape=[_sds((n, 2, NH, CHUNK, CHUNK)), _sds((2, s, HW)), _sds((2, s, HW), BF16), _sds((2, s, HW), BF16),
                   _sds((2, s, HW), BF16), _sds((n, 2, NH, CHUNK, CHUNK), BF16), _sds((n, 8, DH))],
        compiler_params=_cparams(1),
    )(q, k, v, bg)


def _seq_specs(s, order):
    n = s // CHUNK
    cb = min(SEQ_CB, n)
    nb = n // cb
    tb = cb * CHUNK

    def blk(d):
        return (lambda i: i) if order[d] else (lambda i: nb - 1 - i)

    def per_dir(make):
        return [make(d, blk(d)) for d in range(2)]

    tok2 = per_dir(lambda d, f: pl.BlockSpec((1, tb, HW), lambda i: (d, f(i), 0)))
    tok = per_dir(lambda d, f: pl.BlockSpec((tb, HW), lambda i: (f(i), 0)))
    mat = per_dir(lambda d, f: pl.BlockSpec((cb, 1, NH, CHUNK, CHUNK), lambda i: (f(i), d, 0, 0, 0)))
    cds = per_dir(lambda d, f: pl.BlockSpec((cb, 8, DH), lambda i: (f(i), 0, 0)))
    sts = per_dir(lambda d, f: pl.BlockSpec((cb, NH, DH, DH), lambda i: (f(i), 0, 0, 0)))
    dcd = per_dir(lambda d, f: pl.BlockSpec((cb, NH, DH), lambda i: (f(i), 0, 0)))
    return n, cb, nb, tok2, tok, mat, cds, sts, dcd


def _gdn_seq_fwd(u, w, qd, kd, at, cd, name):
    s = u.shape[1]
    n, cb, nb, tok2, tok, mat, cds, sts, _ = _seq_specs(s, (True, False))

    def body(*refs):
        ins = (refs[0:6], refs[6:12])
        outs = (refs[12:15], refs[15:18])
        st = refs[18]

        @pl.when(pl.program_id(0) == 0)
        def _():
            st[...] = jnp.zeros_like(st)

        for j in range(cb):
            for d in range(2):
                u_r, w_r, qd_r, kd_r, at_r, cd_r = ins[d]
                o_r, s_r, vn_r = outs[d]
                jj = j if d == 0 else cb - 1 - j
                rows = slice(CHUNK * jj, CHUNK * (jj + 1))
                for h in range(NH):
                    sl = _head(h)
                    sh = st[d, h]
                    s_r[jj, h] = sh
                    vn = u_r[0, rows, sl] - _dot(w_r[0, rows, sl], sh)
                    vn_r[rows, sl] = vn
                    o_r[rows, sl] = _dot(qd_r[0, rows, sl], sh) + _dot(at_r[jj, 0, h], vn)
                    st[d, h] = sh * cd_r[jj, 4 * d + h:4 * d + h + 1, :] + _dot_tn(kd_r[0, rows, sl], vn)

    in_specs, out_specs, out_shape = [], [], []
    for d in range(2):
        in_specs += [tok2[d]] * 4 + [mat[d], cds[d]]
        out_specs += [tok[d], sts[d], tok[d]]
        out_shape += [_sds((s, HW)), _sds((n, NH, DH, DH)), _sds((s, HW))]
    return pl.pallas_call(
        body, name=name, grid=(nb,), in_specs=in_specs, out_specs=out_specs, out_shape=out_shape,
        scratch_shapes=[pltpu.VMEM((2, NH, DH, DH), F32)], compiler_params=_cparams(1),
    )(u, w, qd, kd, at, cd, u, w, qd, kd, at, cd)


def _gdn_seq_bwd(do, w, qd, kd, at, cd, states, vns, name):
    s = do.shape[0]
    n, cb, nb, tok2, tok, mat, cds, sts, dcd = _seq_specs(s, (False, True))

    def body(*refs):
        ins = (refs[0:8], refs[8:16])
        outs = (refs[16:21], refs[21:26])
        dst = refs[26]

        @pl.when(pl.program_id(0) == 0)
        def _():
            dst[...] = jnp.zeros_like(dst)

        for j in range(cb):
            for d in range(2):
                do_r, w_r, qd_r, kd_r, at_r, cd_r, s_r, vn_r = ins[d]
                dvn_r, dkd_r, dqd_r, dw_r, dcd_r = outs[d]
                jj = cb - 1 - j if d == 0 else j
                rows = slice(CHUNK * jj, CHUNK * (jj + 1))
                for h in range(NH):
                    sl = _head(h)
                    dsn = dst[d, h]
                    sh = s_r[jj, h]
                    doh = do_r[rows, sl]
                    d_vn = _dot_tn(at_r[jj, 0, h], doh) + _dot(kd_r[0, rows, sl], dsn)
                    dvn_r[rows, sl] = d_vn
                    dkd_r[rows, sl] = _dot_nt(vn_r[rows, sl], dsn)
                    dqd_r[rows, sl] = _dot_nt(doh, sh)
                    dw_r[rows, sl] = -_dot_nt(d_vn, sh)
                    d_cd = jnp.sum(jnp.sum(sh * dsn, axis=1, keepdims=True), axis=0, keepdims=True)
                    dcd_r[jj, h:h + 1, :] = jnp.broadcast_to(d_cd, (1, DH))
                    dst[d, h] = (cd_r[jj, 4 * d + h:4 * d + h + 1, :] * dsn + _dot_tn(qd_r[0, rows, sl], doh)
                                 - _dot_tn(w_r[0, rows, sl], d_vn))

    in_specs, out_specs, out_shape, args = [], [], [], []
    for d in range(2):
        in_specs += [tok[d]] + [tok2[d]] * 3 + [mat[d], cds[d], sts[d], tok[d]]
        args += [do, w, qd, kd, at, cd, states[d], vns[d]]
        out_specs += [tok[d]] * 4 + [dcd[d]]
        out_shape += [_sds((s, HW))] * 4 + [_sds((n, NH, DH))]
    return pl.pallas_call(
        body, name=name, grid=(nb,), in_specs=in_specs, out_specs=out_specs, out_shape=out_shape,
        scratch_shapes=[pltpu.VMEM((2, NH, DH, DH), F32)], compiler_params=_cparams(1),
    )(*args)


def _gdn_local_bwd(q, k, v, bg, tmat, do, vns, seq_grads, name):
    s = q.shape[0]
    n = s // CHUNK

    def body(*refs):
        q_ref, k_ref, v_ref, bg_ref, t_ref, do_ref = refs[0:6]
        vn_refs = refs[6:8]
        sg = (refs[8:13], refs[13:18])
        dq_ref, dk_ref, dv_ref, dbg_ref = refs[18:]
        bgv = bg_ref[...]
        qs = [q_ref[:, _head(h)] for h in range(NH)]
        ks = [k_ref[:, _head(h)] for h in range(NH)]
        kk = [_dot_nt(ks[h], ks[h]) for h in range(NH)]
        q0 = [_dot_nt(qs[h], ks[h]) for h in range(NH)]
        lane = lax.broadcasted_iota(jnp.int32, (CHUNK, BAP), 1)
        rowi = lax.broadcasted_iota(jnp.int32, (CHUNK, 1), 0)
        ones = jnp.ones((CHUNK, DH), F32)
        dbg = jnp.zeros((CHUNK, BAP), F32)
        acc = [[None, None, None] for _ in range(NH)]
        for d in range(2):
            m = _GdnMasks(d)
            gcs, gcs_t = _chunk_cumsums(m, bgv)
            dvn_r, dkd_r, dqd_r, dw_r, dcd_r = sg[d]
            for h in range(NH):
                sl = _head(h)
                c = _GdnHead(qs[h], ks[h], v_ref[:, sl], kk[h], q0[h], bgv, gcs, gcs_t, d, h, m)
                tm = t_ref[0, d, h]
                d_vn, d_kd, d_qd, d_w = dvn_r[:, sl], dkd_r[:, sl], dqd_r[:, sl], dw_r[:, sl]
                d_cd = dcd_r[0, h:h + 1, 0:1]
                d_attn = jnp.where(m.incl, _dot_nt(do_ref[:, sl], vn_refs[d][:, sl]), 0.0)
                d_t = _dot_nt(d_vn, c.vb) + _dot_nt(d_w, c.kg)
                d_vb = _dot_tn(tm, d_vn)
                d_kg = _dot_tn(tm, d_w)
                d_l = jnp.where(m.strict, -_dot3(_dot3(tm, d_t, _TN), tm, _NT), 0.0)
                d_a0 = d_l * c.decay
                d_q0 = d_attn * c.decay
                e = (d_l * c.a0 + d_attn * c.q0) * c.decay
                d_kb = _dot(d_a0, c.k) + d_kg * c.eg
                parts = (_dot(d_q0, c.k) + d_qd * c.eg,
                         _dot_tn(d_a0, c.kb) + _dot_tn(d_q0, c.q) + d_kd * c.ek + d_kb * c.beta,
                         d_vb * c.beta)
                acc[h] = [p if a is None else a + p for a, p in zip(acc[h], parts)]
                s_kd = jnp.sum(d_kd * c.kd, axis=1, keepdims=True)
                d_gc = (jnp.sum(d_kg * c.kg, axis=1, keepdims=True) + jnp.sum(d_qd * c.qd, axis=1, keepdims=True) - s_kd
                        + jnp.sum(e, axis=1, keepdims=True) - _dot_exact(ones, e, _TN, False)[:, 0:1])
                d_gl = jnp.sum(s_kd, axis=0, keepdims=True) + d_cd * c.cd
                d_gc = d_gc + jnp.where(rowi == m.last, d_gl, 0.0)
                d_g = _dot_exact(m.tri, d_gc * ones, _TN, True)[:, 0:1]
                d_beta = jnp.sum(d_kb * c.k, axis=1, keepdims=True) + jnp.sum(d_vb * c.v, axis=1, keepdims=True)
                dbg = dbg + jnp.where(lane == 4 * d + h, d_beta, 0.0) + jnp.where(lane == 8 + 4 * d + h, d_g, 0.0)
        for h in range(NH):
            dq_ref[:, _head(h)], dk_ref[:, _head(h)], dv_ref[:, _head(h)] = acc[h]
        dbg_ref[...] = dbg

    tok = _rows(CHUNK, HW)
    bgs = _rows(CHUNK, BAP)
    mat = pl.BlockSpec((1, 2, NH, CHUNK, CHUNK), lambda i: (i, 0, 0, 0, 0))
    dcd = pl.BlockSpec((1, NH, DH), lambda i: (i, 0, 0))
    args = [q, k, v, bg, tmat, do, vns[0], vns[1]]
    in_specs = [tok, tok, tok, bgs, mat, tok, tok, tok]
    for d in range(2):
        args += list(seq_grads[d])
        in_specs += [tok] * 4 + [dcd]
    return pl.pallas_call(
        body, name=name, grid=(n,), in_specs=in_specs, out_specs=[tok, tok, tok, bgs],
        out_shape=[_sds((s, HW))] * 3 + [_sds((s, BAP))], compiler_params=_cparams(1),
    )(*args)


def _prep_bwd(c_qkv, p_ba, alog_row, dtb_row, dq, dk, dv, dbg, name):
    s = c_qkv.shape[0]
    t = min(256, s)

    def body(cq_ref, pc_ref, alog_ref, dtb_ref, dq_ref, dk_ref, dv_ref, dbg_ref,
             dcq_ref, dpc_ref, dalog_ref, ddtb_ref):
        @pl.when(pl.program_id(0) == 0)
        def _():
            dalog_ref[...] = jnp.zeros_like(dalog_ref)
            ddtb_ref[...] = jnp.zeros_like(ddtb_ref)

        cq = cq_ref[...]
        sq = cq * _sig(cq)
        sg = _silu_grad(cq)
        for h in range(NH):
            sl = slice(DH * h, DH * (h + 1))
            for off, d_ref, scale in ((0, dq_ref, DH ** -0.5), (RGW, dk_ref, 1.0)):
                csl = slice(off + DH * h, off + DH * (h + 1))
                xh = sq[:, csl]
                nrm = lax.rsqrt(jnp.sum(xh * xh, axis=-1, keepdims=True) + EPS)
                y = xh * nrm
                dy = d_ref[:, sl] * scale
                dcq_ref[:, csl] = nrm * (dy - y * jnp.sum(dy * y, axis=-1, keepdims=True)) * sg[:, csl]
        dcq_ref[:, 2 * RGW:] = dv_ref[...] * sg[:, 2 * RGW:]
        pc = pc_ref[...]
        lane = lax.broadcasted_iota(jnp.int32, pc.shape, 1)
        dbg = dbg_ref[...]
        beta = _sig(pc)
        ea = jnp.exp(alog_ref[...])
        z = pc + dtb_ref[...]
        g = -ea * _softplus(z)
        is_g = jnp.logical_and(lane >= 8, lane < 16)
        d_alpha = jnp.where(is_g, dbg * (-ea) * _sig(z), 0.0)
        dpc_ref[...] = jnp.where(lane < 8, dbg * beta * (1.0 - beta), d_alpha)
        dalog_ref[...] += _colsum(jnp.where(is_g, dbg * g, 0.0))
        ddtb_ref[...] += _colsum(d_alpha)

    return pl.pallas_call(
        body, name=name, grid=(s // t,),
        in_specs=[_rows(t, QKVW), _rows(t, BAP), _full((1, BAP)), _full((1, BAP))] + [_rows(t, HW)] * 3 + [_rows(t, BAP)],
        out_specs=[_rows(t, QKVW), _rows(t, BAP), _full((1, BAP)), _full((1, BAP))],
        out_shape=[_sds((s, QKVW)), _sds((s, BAP)), _sds((1, BAP)), _sds((1, BAP))],
        compiler_params=_cparams(1),
    )(c_qkv, p_ba, alog_row, dtb_row, dq, dk, dv, dbg)


def _mix_out_values(hf, hb, gate, of, ob, z, gn):
    hr = hf + hb
    y_rg = hr * _gelu(gate)
    osum = of + ob
    parts = []
    for h in range(NH):
        sl = slice(DH * h, DH * (h + 1))
        oh = osum[:, sl]
        r, ohat = _rms(oh)
        zh = z[:, sl]
        parts.append((r, ohat, zh))
    y_gdn = jnp.concatenate([ohat * gn * (zh * _sig(zh)) for (r, ohat, zh) in parts], axis=1)
    return hr, y_rg, y_gdn, parts


def _outproj(x1, hf, hb, gate, of, ob, z, gn, wout, name):
    s = x1.shape[0]
    t = min(256, s)

    def body(x_ref, hf_ref, hb_ref, gate_ref, of_ref, ob_ref, z_ref, gn_ref, w_ref, xo_ref, y_ref):
        _, y_rg, y_gdn, _ = _mix_out_values(hf_ref[...], hb_ref[...], gate_ref[...], of_ref[...], ob_ref[...],
                                            z_ref[...], gn_ref[...])
        y = jnp.concatenate([y_rg, y_gdn], axis=1).astype(BF16)
        y_ref[...] = y
        xo_ref[...] = x_ref[...] + jnp.dot(y, w_ref[...], preferred_element_type=F32)

    return pl.pallas_call(
        body, name=name, grid=(s // t,),
        in_specs=[_rows(t, D)] + [_rows(t, RGW)] * 6 + [_full((1, DH)), _full((D, D))],
        out_specs=[_rows(t, D), _rows(t, D)], out_shape=[_sds((s, D)), _sds((s, D), BF16)],
        compiler_params=_cparams(1),
    )(x1, hf, hb, gate, of, ob, z, gn, wout)


def _outproj_bwd(dx2, hf, hb, gate, of, ob, z, gn, wout, name):
    s = dx2.shape[0]
    t = min(256, s)

    def body(d_ref, hf_ref, hb_ref, gate_ref, of_ref, ob_ref, z_ref, gn_ref, w_ref,
             dhr_ref, dgate_ref, dos_ref, dz_ref, dgn_ref, db_ref):
        @pl.when(pl.program_id(0) == 0)
        def _():
            dgn_ref[...] = jnp.zeros_like(dgn_ref)

        gate = gate_ref[...]
        gn_v = gn_ref[...]
        hr, _, _, parts = _mix_out_values(hf_ref[...], hb_ref[...], gate, of_ref[...], ob_ref[...], z_ref[...], gn_v)
        dbf = d_ref[...].astype(BF16)
        db_ref[...] = dbf
        dy = _dot_nt(dbf, w_ref[...])
        dyr = dy[:, :RGW]
        dhr_ref[...] = dyr * _gelu(gate)
        dgate_ref[...] = dyr * hr * _gelu_grad(gate)
        dgn = jnp.zeros((1, DH), F32)
        for h, (r, ohat, zh) in enumerate(parts):
            sl = slice(DH * h, DH * (h + 1))
            dyh = dy[:, RGW + DH * h:RGW + DH * (h + 1)]
            sz = zh * _sig(zh)
            dn = dyh * sz
            dz_ref[:, sl] = dyh * ohat * gn_v * _silu_grad(zh)
            dgn = dgn + _colsum(dn * ohat)
            dos_ref[:, sl] = _rms_bwd(dn, ohat, r, gn_v)
        dgn_ref[...] += dgn

    return pl.pallas_call(
        body, name=name, grid=(s // t,),
        in_specs=[_rows(t, D)] + [_rows(t, RGW)] * 6 + [_full((1, DH)), _full((D, D))],
        out_specs=[_rows(t, RGW)] * 4 + [_full((1, DH)), _rows(t, D)],
        out_shape=[_sds((s, RGW))] * 4 + [_sds((1, DH)), _sds((s, D), BF16)],
        compiler_params=_cparams(1),
    )(dx2, hf, hb, gate, of, ob, z, gn, wout)


def _loss_head(x3, target, gain, name):
    s = x3.shape[0]
    t = min(256, s)

    def body(x_ref, t_ref, g_ref, dx_ref, loss_ref, dg_ref):
        @pl.when(pl.program_id(0) == 0)
        def _():
            loss_ref[...] = jnp.zeros_like(loss_ref)
            dg_ref[...] = jnp.zeros_like(dg_ref)

        r, xh = _rms(x_ref[...])
        gv = g_ref[...]
        err = xh * gv - t_ref[...]
        per_tok = jnp.mean(err * err, axis=-1, keepdims=True)
        loss_ref[...] += 0.5 * jnp.sum(per_tok, axis=0, keepdims=True)
        dy = err * (1.0 / D)
        dg_ref[...] += _colsum(dy * xh)
        dx_ref[...] = _rms_bwd(dy, xh, r, gv)

    return pl.pallas_call(
        body, name=name, grid=(s // t,), in_specs=[_rows(t, D), _rows(t, D), _full((1, D))],
        out_specs=[_rows(t, D), _full((8, 128)), _full((1, D))],
        out_shape=[_sds((s, D)), _sds((8, 128)), _sds((1, D))], compiler_params=_cparams(1),
    )(x3, target, gain)


def _adamw(w, g, m, v, name):
    r, c = w.shape
    tr = r
    while tr * c * 4 > (1 << 20) and tr % 16 == 0:
        tr //= 2

    def body(w_ref, g_ref, m_ref, v_ref, d_ref, nm_ref, nv_ref):
        gv = g_ref[...]
        mn = ADAM_B1 * m_ref[...] + (1.0 - ADAM_B1) * gv
        vn = ADAM_B2 * v_ref[...] + (1.0 - ADAM_B2) * (gv * gv)
        m_hat = mn / (1.0 - ADAM_B1 ** ADAM_STEP)
        v_hat = vn / (1.0 - ADAM_B2 ** ADAM_STEP)
        d_ref[...] = -ADAM_LR * (m_hat / (jnp.sqrt(v_hat) + ADAM_EPS) + ADAM_WD * w_ref[...])
        nm_ref[...] = mn
        nv_ref[...] = vn

    return pl.pallas_call(
        body, name=name, grid=(r // tr,), in_specs=[_rows(tr, c)] * 4, out_specs=[_rows(tr, c)] * 3,
        out_shape=[_sds((r, c))] * 3, compiler_params=_cparams(1),
    )(w, g, m, v)


def _mesh_pos():
    return lax.axis_index("x"), lax.axis_index("y"), lax.axis_index("c")


def _other_chips(x, y):
    return [(1 - x, y), (x, 1 - y), (1 - x, 1 - y)]


def _all_gather(n_arr, space, out_shapes, block_of, name):
    def body(*refs):
        x_refs, out_refs = refs[:n_arr], refs[n_arr:2 * n_arr]
        send_sems, recv_sems, local_sems = refs[2 * n_arr:]
        x, y, c = _mesh_pos()
        me, sibling = (x, y, c), (x, y, 1 - c)
        chips = _other_chips(x, y)

        def slot(a, px, py, pc):
            return out_refs[a].at[4 * px + 2 * py + pc]

        def copy(a, k, block, to, src=None):
            return pltpu.make_async_remote_copy(
                src_ref=slot(a, *block) if src is None else src, dst_ref=slot(a, *block),
                send_sem=send_sems.at[7 * a + k], recv_sem=recv_sems.at[7 * a + k], device_id=to, device_id_type=MESH)

        srcs = [block_of(a, x_refs[a], c) for a in range(n_arr)]
        local = [pltpu.make_async_copy(srcs[a], slot(a, *me), local_sems.at[a]) for a in range(n_arr)]
        for cp in local:
            cp.start()
        first = []
        for a in range(n_arr):
            first += [copy(a, 1 + j, me, (*chip, c), src=srcs[a]) for j, chip in enumerate(chips)]
            first.append(copy(a, 0, me, sibling, src=srcs[a]))
        for cp in first:
            cp.start()
        passed = []
        for j, chip in enumerate(chips):
            for a in range(n_arr):
                copy(a, 1 + j, (*chip, c), me).wait_recv()
                fwd = copy(a, 4 + j, (*chip, c), sibling)
                fwd.start()
                passed.append(fwd)
        for a in range(n_arr):
            copy(a, 0, sibling, me).wait_recv()
            for j, chip in enumerate(chips):
                copy(a, 4 + j, (*chip, 1 - c), me).wait_recv()
        for cp in first + passed:
            cp.wait_send()
        for cp in local:
            cp.wait()

    return pl.pallas_call(
        body, name=name, out_shape=out_shapes,
        in_specs=[pl.BlockSpec(memory_space=space)] * n_arr, out_specs=[pl.BlockSpec(memory_space=space)] * n_arr,
        scratch_shapes=[pltpu.SemaphoreType.DMA((7 * n_arr,)), pltpu.SemaphoreType.DMA((7 * n_arr,)),
                        pltpu.SemaphoreType.DMA((n_arr,))],
    )


def _gather_weights(shards):
    halves = [w.shape[0] // 2 for w in shards]

    def block_of(a, x_ref, c):
        return x_ref.at[pl.ds(pl.multiple_of(c * halves[a], 16), halves[a]), :]

    outs = _all_gather(len(shards), pltpu.HBM, [_sds((8, h, w.shape[1]), BF16) for h, w in zip(halves, shards)],
                       block_of, "gather_weights")(*shards)
    return [o.reshape(NSH, 2 * h, o.shape[2]) for o, h in zip(outs, halves)]


def _gather_small(block, name):
    r, c = block.shape
    return _all_gather(1, pltpu.VMEM, [_sds((8, r, c))], lambda a, x_ref, c_: x_ref, name)(block)[0]


def _sibling_exchange(gs):
    n = len(gs)
    halves = [g.shape[1] // 2 for g in gs]

    def body(*refs):
        g_refs, land_refs = refs[:n], refs[n:2 * n]
        send_sems, recv_sems = refs[2 * n:]
        x, y, c = _mesh_pos()
        copies = []
        for a in range(n):
            h = halves[a]
            for s in range(NSH):
                copies.append(pltpu.make_async_remote_copy(
                    src_ref=g_refs[a].at[s, pl.ds(pl.multiple_of((1 - c) * h, 8), h), :], dst_ref=land_refs[a].at[s],
                    send_sem=send_sems.at[NSH * a + s], recv_sem=recv_sems.at[NSH * a + s],
                    device_id=(x, y, 1 - c), device_id_type=MESH))
        for cp in copies:
            cp.start()
        for cp in copies:
            cp.wait()

    return pl.pallas_call(
        body, name="grad_sibling_exchange", out_shape=[_sds((NSH, h, g.shape[2])) for h, g in zip(halves, gs)],
        in_specs=[pl.BlockSpec(memory_space=pltpu.HBM)] * n, out_specs=[pl.BlockSpec(memory_space=pltpu.HBM)] * n,
        scratch_shapes=[pltpu.SemaphoreType.DMA((NSH * n,)), pltpu.SemaphoreType.DMA((NSH * n,))],
    )(*gs)


def _chip_sum(g, land, c_arr, name):
    _, h, cols = land.shape

    def body(c_ref, g_ref, l_ref, o_ref):
        o_ref[...] = (g_ref[...] + l_ref[...]).astype(BF16)

    return pl.pallas_call(
        body, name=name, out_shape=_sds((NSH, h, cols), BF16),
        grid_spec=pltpu.PrefetchScalarGridSpec(
            num_scalar_prefetch=1, grid=(NSH,),
            in_specs=[pl.BlockSpec((1, h, cols), lambda s, c_ref: (s, c_ref[0], 0)),
                      pl.BlockSpec((1, h, cols), lambda s, c_ref: (s, 0, 0))],
            out_specs=pl.BlockSpec((1, h, cols), lambda s, c_ref: (s, 0, 0))),
        compiler_params=_cparams(1),
    )(c_arr, g, land)


def _chip_scatter(parts):
    n = len(parts)

    def body(*refs):
        p_refs, land_refs = refs[:n], refs[n:2 * n]
        send_sems, recv_sems, local_sems = refs[2 * n:]
        x, y, c = _mesh_pos()
        my_chip = 2 * x + y
        local = [pltpu.make_async_copy(p_refs[a].at[my_chip], land_refs[a].at[my_chip], local_sems.at[a]) for a in range(n)]
        for cp in local:
            cp.start()
        copies = []
        for a in range(n):
            for j, (px, py) in enumerate(_other_chips(x, y)):
                copies.append(pltpu.make_async_remote_copy(
                    src_ref=p_refs[a].at[2 * px + py], dst_ref=land_refs[a].at[my_chip],
                    send_sem=send_sems.at[3 * a + j], recv_sem=recv_sems.at[3 * a + j],
                    device_id=(px, py, c), device_id_type=MESH))
        for cp in copies:
            cp.start()
        for cp in copies:
            cp.wait()
        for cp in local:
            cp.wait()

    return pl.pallas_call(
        body, name="grad_chip_scatter", out_shape=[_sds(p.shape, BF16) for p in parts],
        in_specs=[pl.BlockSpec(memory_space=pltpu.HBM)] * n, out_specs=[pl.BlockSpec(memory_space=pltpu.HBM)] * n,
        scratch_shapes=[pltpu.SemaphoreType.DMA((3 * n,)), pltpu.SemaphoreType.DMA((3 * n,)), pltpu.SemaphoreType.DMA((n,))],
    )(*parts)


def _sum_slots(land, name):
    k, r, c = land.shape
    tr = r // 2 if r % 32 == 0 else r

    def body(l_ref, o_ref):
        acc = l_ref[0].astype(F32)
        for i in range(1, k):
            acc = acc + l_ref[i].astype(F32)
        o_ref[...] = acc

    return pl.pallas_call(
        body, name=name, grid=(r // tr,), in_specs=[pl.BlockSpec((k, tr, c), lambda i: (0, i, 0))],
        out_specs=_rows(tr, c), out_shape=_sds((r, c)), compiler_params=_cparams(1),
    )(land)


def _sibling_swap(halves):
    n = len(halves)

    def body(*refs):
        h_refs, out_refs = refs[:n], refs[n:2 * n]
        send_sems, recv_sems, local_sems = refs[2 * n:]
        x, y, c = _mesh_pos()
        local = [pltpu.make_async_copy(h_refs[a], out_refs[a].at[c], local_sems.at[a]) for a in range(n)]
        for cp in local:
            cp.start()
        copies = [pltpu.make_async_remote_copy(
            src_ref=h_refs[a], dst_ref=out_refs[a].at[c], send_sem=send_sems.at[a], recv_sem=recv_sems.at[a],
            device_id=(x, y, 1 - c), device_id_type=MESH) for a in range(n)]
        for cp in copies:
            cp.start()
        for cp in copies:
            cp.wait()
        for cp in local:
            cp.wait()

    outs = pl.pallas_call(
        body, name="grad_sibling_swap", out_shape=[_sds((2,) + h.shape) for h in halves],
        in_specs=[pl.BlockSpec(memory_space=pltpu.HBM)] * n, out_specs=[pl.BlockSpec(memory_space=pltpu.HBM)] * n,
        scratch_shapes=[pltpu.SemaphoreType.DMA((n,)), pltpu.SemaphoreType.DMA((n,)), pltpu.SemaphoreType.DMA((n,))],
    )(*halves)
    return [o.reshape(2 * o.shape[1], o.shape[2]) for o in outs]


def _pad_rows(v, width):
    flat = v.reshape(-1)
    rows = -(-flat.shape[0] // width)
    rows = -(-rows // 8) * 8
    return jnp.pad(flat, (0, rows * width - flat.shape[0])).reshape(rows, width)


def _block_diag(w):
    eye = jnp.eye(8, dtype=w.dtype)
    return (w[:, :, None, :] * eye[:, None, :, None]).reshape(RGW, RGW)


def _diag_blocks(dense):
    r = dense.reshape(8, 64, 8, 64)
    return jnp.stack([r[n, :, n, :] for n in range(8)])


def _lane_row(v8):
    return jnp.zeros((1, BAP), F32).at[0, 8:16].set(v8.reshape(8))


def _local_step(x, target, wts):
    (g1, wg1, wu1, wd1, gmix, w_in_groups, wout, rg_cw8, rg_cb, wgates, gbias, lam_row, gdn_cw8,
     alog_row, dtb_row, gn, g2, wg2, wu2, wd2, gfin) = wts
    s = x.shape[0]

    x1, a1, b1 = _ffn_fwd(x, g1, wg1, wu1, wd1, "ffn1_fwd")
    h2, p_rgx, p_gate, p_qkv, p_z, p_ba = _inproj(x1, gmix, w_in_groups, "in_proj")
    c_rg = _conv(p_rgx, rg_cw8, rg_cb, "rg_conv")
    c_qkv = _conv(p_qkv, gdn_cw8, jnp.zeros((1, QKVW), F32), "gdn_conv")
    a0, bb0, a1s, bb1, q, k, v, bg = _mix_prep(c_rg, c_qkv, p_ba, wgates, gbias, lam_row, alog_row, dtb_row, "mix_prep")
    hf = _scan(a0, bb0, False, "rg_scan_f")
    hb = _scan(a1s, bb1, True, "rg_scan_b")
    tmat, gu, gw, gqd, gkd, gat, gcd = _gdn_local_fwd(q, k, v, bg, "gdn_local_fwd")
    of, s0, vn0, ob, s1, vn1 = _gdn_seq_fwd(gu, gw, gqd, gkd, gat, gcd, "gdn_seq_fwd")
    x2, ymix = _outproj(x1, hf, hb, p_gate, of, ob, p_z, gn, wout, "out_proj")
    x3, a2, b2 = _ffn_fwd(x2, g2, wg2, wu2, wd2, "ffn2_fwd")
    dx3, loss_blk, d_gfin = _loss_head(x3, target, gfin, "loss_head")

    dx2, d_g2, hb2, dob2, fb2, dab2, dbb2 = _ffn_bwd(x2, dx3, g2, a2, b2, wg2, wu2, wd2, "ffn2_bwd")
    d_wg2 = _tn(hb2, dab2, "ffn2_dwg")
    d_wu2 = _tn(hb2, dbb2, "ffn2_dwu")
    d_wd2 = _tn(fb2, dob2, "ffn2_dwd")

    d_hr, d_gate, d_os, d_z, d_gn, dx2b = _outproj_bwd(dx2, hf, hb, p_gate, of, ob, p_z, gn, wout, "out_proj_bwd")
    d_wout = _tn(ymix, dx2b, "dw_out")[0]

    a0_up = _shift_rows(a0, 1, "shift_a0")
    a1_dn = _shift_rows(a1s, -1, "shift_a1")
    hf_dn = _shift_rows(hf, -1, "shift_hf")
    hb_up = _shift_rows(hb, 1, "shift_hb")
    lam0 = _scan(a0_up, d_hr, True, "rg_scan_f_bwd")
    lam1 = _scan(a1_dn, d_hr, False, "rg_scan_b_bwd")
    d_xc, d_pre, xcb, d_gbias, d_lam = _gates_bwd(c_rg, wgates, gbias, lam_row, lam0, lam1, hf_dn, hb_up, "rg_gates_bwd")
    d_wgates = _tn(xcb, d_pre, "dw_gates")[0]
    d_prgx, d_rgcw8, d_rgcb = _conv_bwd(p_rgx, d_xc, rg_cw8, "rg_conv_bwd")

    sg = _gdn_seq_bwd(d_os, gw, gqd, gkd, gat, gcd, (s0, s1), (vn0, vn1), "gdn_seq_bwd")
    dq, dk, dv, dbg = _gdn_local_bwd(q, k, v, bg, tmat, d_os, (vn0, vn1), (sg[0:5], sg[5:10]), "gdn_local_bwd")
    d_cqkv, d_pba, d_alog, d_dtb = _prep_bwd(c_qkv, p_ba, alog_row, dtb_row, dq, dk, dv, dbg, "gdn_prep_bwd")
    d_pqkv, d_gdncw8, _ = _conv_bwd(p_qkv, d_cqkv, gdn_cw8, "gdn_conv_bwd")

    dps = (d_prgx, d_gate, d_pqkv, d_z, d_pba)
    dx1, d_gmix = _inproj_bwd(x1, dx2, gmix, dps, w_in_groups, "in_proj_bwd")
    d_win_groups = [_tn(h2, dp, "dw_in_%d" % i)[0] for i, dp in enumerate(dps)]

    gx, d_g1, hb1, dob1, fb1, dab1, dbb1 = _ffn_bwd(x, dx1, g1, a1, b1, wg1, wu1, wd1, "ffn1_bwd")
    d_wg1 = _tn(hb1, dab1, "ffn1_dwg")
    d_wu1 = _tn(hb1, dbb1, "ffn1_dwu")
    d_wd1 = _tn(fb1, dob1, "ffn1_dwd")

    d_win = jnp.concatenate(d_win_groups[:4] + [d_win_groups[4][:, :BAW]], axis=1)
    big = (d_wg1, d_wu1, d_wd1, d_win, d_wout, d_wg2, d_wu2, d_wd2)
    small = dict(
        ffn1_norm=d_g1, mix_norm=d_gmix, rg_conv_w=d_rgcw8[:4], rg_conv_b=d_rgcb,
        rg_gate_a_w=jnp.stack([_diag_blocks(d_wgates[:, RGW * i:RGW * (i + 1)]) for i in (0, 1)]),
        rg_gate_x_w=jnp.stack([_diag_blocks(d_wgates[:, RGW * i:RGW * (i + 1)]) for i in (2, 3)]),
        rg_gate_a_b=d_gbias[0, :2 * RGW].reshape(2, RGW), rg_gate_x_b=d_gbias[0, 2 * RGW:].reshape(2, RGW),
        rg_lambda=d_lam.reshape(2, RGW), gdn_conv_w=d_gdncw8[:4],
        gdn_a_log=d_alog[0, 8:16].reshape(2, NH), gdn_dt_bias=d_dtb[0, 8:16].reshape(2, NH),
        gdn_norm=d_gn, ffn2_norm=d_g2, final_norm=d_gfin)
    return loss_blk, gx, big, small


_SMALL_NAMES = ("ffn1_norm", "mix_norm", "rg_conv_w", "rg_conv_b", "rg_gate_a_w", "rg_gate_a_b", "rg_gate_x_w",
                "rg_gate_x_b", "rg_lambda", "gdn_conv_w", "gdn_a_log", "gdn_dt_bias", "gdn_norm", "ffn2_norm", "final_norm")
_SMALL_SHARDED = dict(rg_conv_w=128, rg_gate_a_b=128, rg_gate_x_b=128, rg_lambda=128, gdn_conv_w=384)
_OUT_ORDER = ("ffn1_norm", "ffn1_w_gate", "ffn1_w_up", "ffn1_w_down", "mix_norm", "w_in", "w_out", "rg_conv_w", "rg_conv_b",
              "rg_gate_a_w", "rg_gate_a_b", "rg_gate_x_w", "rg_gate_x_b", "rg_lambda", "gdn_conv_w", "gdn_a_log",
              "gdn_dt_bias", "gdn_norm", "ffn2_norm", "ffn2_w_gate", "ffn2_w_up", "ffn2_w_down", "final_norm")
_BIG_NAMES = ("ffn1_w_gate", "ffn1_w_up", "ffn1_w_down", "w_in", "w_out", "ffn2_w_gate", "ffn2_w_up", "ffn2_w_down")


def kernel(x, ffn1_norm, ffn1_w_gate, ffn1_w_up, ffn1_w_down, mix_norm, w_in, w_out, rg_conv_w, rg_conv_b, rg_gate_a_w, rg_gate_a_b, rg_gate_x_w, rg_gate_x_b, rg_lambda, gdn_conv_w, gdn_a_log, gdn_dt_bias, gdn_norm, ffn2_norm, ffn2_w_gate, ffn2_w_up, ffn2_w_down, final_norm, loss_target, m_ffn1_norm, m_ffn1_w_gate, m_ffn1_w_up, m_ffn1_w_down, m_mix_norm, m_w_in, m_w_out, m_rg_conv_w, m_rg_conv_b, m_rg_gate_a_w, m_rg_gate_a_b, m_rg_gate_x_w, m_rg_gate_x_b, m_rg_lambda, m_gdn_conv_w, m_gdn_a_log, m_gdn_dt_bias, m_gdn_norm, m_ffn2_norm, m_ffn2_w_gate, m_ffn2_w_up, m_ffn2_w_down, m_final_norm, v_ffn1_norm, v_ffn1_w_gate, v_ffn1_w_up, v_ffn1_w_down, v_mix_norm, v_w_in, v_w_out, v_rg_conv_w, v_rg_conv_b, v_rg_gate_a_w, v_rg_gate_a_b, v_rg_gate_x_w, v_rg_gate_x_b, v_rg_lambda, v_gdn_conv_w, v_gdn_a_log, v_gdn_dt_bias, v_gdn_norm, v_ffn2_norm, v_ffn2_w_gate, v_ffn2_w_up, v_ffn2_w_down, v_final_norm):
    args = dict(locals())
    w = {n: args[n] for n in _OUT_ORDER}
    mom = {n: args["m_" + n] for n in _OUT_ORDER}
    var = {n: args["v_" + n] for n in _OUT_ORDER}
    xi, yi, ci = _mesh_pos()
    shard = 2 * xi + yi

    wg1, wu1, wd1, win_sh, wout_sh, wg2, wu2, wd2 = _gather_weights([w[n][0].astype(BF16) for n in _BIG_NAMES])
    w_in_full = jnp.transpose(win_sh, (1, 0, 2)).reshape(D, NSH * INSH)
    w_out_full = wout_sh.reshape(D, D)
    sm_local = _pad_rows(jnp.concatenate([w[n][0].reshape(-1) for n in _SMALL_SHARDED]), 128)
    sm_all = _gather_small(sm_local, "gather_small_weights")[0::2].reshape(NSH, -1)
    sm_full, off = {}, 0
    for n, wd_ in _SMALL_SHARDED.items():
        rows = w[n].shape[1]
        piece = sm_all[:, off:off + rows * wd_].reshape(NSH, rows, wd_)
        sm_full[n] = jnp.transpose(piece, (1, 0, 2)).reshape(rows, NSH * wd_)
        off += rows * wd_

    w_in_groups = (w_in_full[:, 0:512], w_in_full[:, 512:1024], w_in_full[:, 1024:2560], w_in_full[:, 2560:3072],
                   jnp.pad(w_in_full[:, 3072:3088], ((0, 0), (0, BAP - BAW))))
    wa, wx = rg_gate_a_w[0], rg_gate_x_w[0]
    wgates = jnp.concatenate([_block_diag(wa[0]), _block_diag(wa[1]), _block_diag(wx[0]), _block_diag(wx[1])],
                             axis=1).astype(BF16)
    gbias = jnp.concatenate([sm_full["rg_gate_a_b"].reshape(1, -1), sm_full["rg_gate_x_b"].reshape(1, -1)], axis=1)
    wts = (ffn1_norm, wg1, wu1, wd1, mix_norm, w_in_groups, w_out_full,
           jnp.pad(sm_full["rg_conv_w"], ((0, 4), (0, 0))), rg_conv_b, wgates, gbias, sm_full["rg_lambda"].reshape(1, -1),
           jnp.pad(sm_full["gdn_conv_w"], ((0, 4), (0, 0))), _lane_row(gdn_a_log), _lane_row(gdn_dt_bias),
           gdn_norm, ffn2_norm, wg2, wu2, wd2, final_norm.reshape(1, D))

    loss_blk, gx, big, small = _local_step(x[0], loss_target[0], wts)
    loss = lax.psum(loss_blk[0, 0], ("x", "y", "c"))

    d_wg1, d_wu1, d_wd1, d_win, d_wout, d_wg2, d_wu2, d_wd2 = big
    gs = [d_wg1, d_wu1, d_wd1, jnp.transpose(d_win.reshape(D, NSH, INSH), (1, 0, 2)), d_wout.reshape(NSH, OUTSH, D),
          d_wg2, d_wu2, d_wd2]
    lands = _sibling_exchange(gs)
    c_arr = ci.reshape(1).astype(jnp.int32)
    parts = [_chip_sum(g, l, c_arr, "chip_sum_" + n) for g, l, n in zip(gs, lands, _BIG_NAMES)]
    halves = [_sum_slots(l, "sum_chips_" + n) for l, n in zip(_chip_scatter(parts), _BIG_NAMES)]
    grads = {n: g[None] for n, g in zip(_BIG_NAMES, _sibling_swap(halves))}

    sm_sizes = [(n, small[n].shape) for n in _SMALL_NAMES]
    sm_grad = _pad_rows(jnp.concatenate([small[n].reshape(-1) for n in _SMALL_NAMES]), D)
    sm_sum = _sum_slots(_gather_small(sm_grad, "gather_small_grads"), "small_grad_sum").reshape(-1)
    off = 0
    for n, shp in sm_sizes:
        cnt = 1
        for dsz in shp:
            cnt *= dsz
        g = sm_sum[off:off + cnt].reshape(shp)
        off += cnt
        if n in _SMALL_SHARDED:
            wd_ = _SMALL_SHARDED[n]
            g = lax.dynamic_slice_in_dim(g, shard * wd_, wd_, axis=1)
        grads[n] = g.reshape(w[n].shape)

    delta, new_m, new_v = {}, {}, {}
    for n in _BIG_NAMES:
        shp = w[n].shape
        d_, m_, v_ = _adamw(w[n][0], grads[n][0], mom[n][0], var[n][0], "adamw_" + n)
        delta[n], new_m[n], new_v[n] = d_.reshape(shp), m_.reshape(shp), v_.reshape(shp)
    packs = [_pad_rows(jnp.concatenate([t[n].reshape(-1) for n in _SMALL_NAMES]), D) for t in (w, grads, mom, var)]
    sm_out = _adamw(*packs, "adamw_small")
    off = 0
    for n in _SMALL_NAMES:
        cnt = w[n].size
        for dst, src in zip((delta, new_m, new_v), sm_out):
            dst[n] = src.reshape(-1)[off:off + cnt].reshape(w[n].shape)
        off += cnt

    outs = [loss, gx[None]]
    for group in (grads, delta, new_m, new_v):
        outs += [group[n] for n in _OUT_ORDER]
    return tuple(outs)
```

```python
import functools

import jax
import jax.numpy as jnp
from jax import lax
from jax.experimental import pallas as pl
from jax.experimental.pallas import tpu as pltpu

F32 = jnp.float32
BF16 = jnp.bfloat16
EPS = 1e-6
D = 1024
NSH = 4
FSH = 704
RGW = 512
QKVW = 1536
ZW = 512
BAW = 16
BAP = 128
INSH = 772
OUTSH = 256
CHUNK = 64
NH = 4
DH = 128
RG_C = 8.0
VMEM_LIMIT = 52 * 1024 * 1024
MESH = pl.DeviceIdType.MESH

ADAM_LR = 0.001
ADAM_B1 = 0.9
ADAM_B2 = 0.999
ADAM_EPS = 1e-08
ADAM_WD = 0.01
ADAM_STEP = 10


def _cparams(n_grid):
    return pltpu.CompilerParams(dimension_semantics=("arbitrary",) * n_grid, vmem_limit_bytes=VMEM_LIMIT)


def _sig(x):
    return 1.0 / (1.0 + jnp.exp(-x))


def _softplus(x):
    return jnp.maximum(x, 0.0) + jnp.log(1.0 + jnp.exp(-jnp.abs(x)))


def _neg_expm1(y):
    series = -y * (1.0 + y * (0.5 + y * (1.0 / 6 + y * (1.0 / 24 + y * (1.0 / 120 + y * (1.0 / 720 + y / 5040))))))
    return jnp.where(y > -0.3, series, 1.0 - jnp.exp(y))


_GELU_C = 0.7978845608028654


def _gelu(x):
    t = jnp.tanh(_GELU_C * (x + 0.044715 * x * x * x))
    return 0.5 * x * (1.0 + t)


def _gelu_grad(x):
    t = jnp.tanh(_GELU_C * (x + 0.044715 * x * x * x))
    return 0.5 * (1.0 + t) + 0.5 * x * (1.0 - t * t) * _GELU_C * (1.0 + 3 * 0.044715 * x * x)


def _silu_grad(x):
    s = _sig(x)
    return s * (1.0 + x * (1.0 - s))


def _dot(a, b):
    return jnp.dot(a.astype(BF16), b.astype(BF16), preferred_element_type=F32)


def _dot_nt(a, b):
    return lax.dot_general(a.astype(BF16), b.astype(BF16), (((1,), (1,)), ((), ())), preferred_element_type=F32)


def _dot_tn(a, b):
    return lax.dot_general(a.astype(BF16), b.astype(BF16), (((0,), (0,)), ((), ())), preferred_element_type=F32)


_NN = ((1,), (0,))
_NT = ((1,), (1,))
_TN = ((0,), (0,))


def _dg(a, b, dims):
    return lax.dot_general(a, b, (dims, ((), ())), preferred_element_type=F32)


def _split2(a):
    hi = a.astype(BF16)
    return hi, (a - hi.astype(F32)).astype(BF16)


def _dot3(a, b, dims=_NN):
    ah, al = _split2(a)
    bh, bl = _split2(b)
    return _dg(ah, bh, dims) + _dg(ah, bl, dims) + _dg(al, bh, dims)


def _dot_exact(e, x, dims, e_is_lhs):
    x0 = x.astype(BF16)
    r = x - x0.astype(F32)
    x1 = r.astype(BF16)
    x2 = (r - x1.astype(F32)).astype(BF16)
    eb = e.astype(BF16)
    if e_is_lhs:
        return _dg(eb, x0, dims) + _dg(eb, x1, dims) + _dg(eb, x2, dims)
    return _dg(x0, eb, dims) + _dg(x1, eb, dims) + _dg(x2, eb, dims)


def _rms(xv):
    r = lax.rsqrt(jnp.mean(xv * xv, axis=-1, keepdims=True) + EPS)
    return r, xv * r


def _rms_bwd(dy, xh, r, gain):
    dxh = dy * gain
    return r * (dxh - xh * jnp.mean(dxh * xh, axis=-1, keepdims=True))


def _colsum(v):
    return jnp.sum(v, axis=0, keepdims=True)


def _rows(t, c):
    return pl.BlockSpec((t, c), lambda i: (i, 0))


def _full(shape):
    n = len(shape)
    return pl.BlockSpec(shape, lambda i: (0,) * n)


def _sds(shape, dtype=F32):
    return jax.ShapeDtypeStruct(shape, dtype)


def _ffn_fwd(x, gain, wg, wu, wd, name):
    s = x.shape[0]
    tm = min(512, s)

    def body(x_ref, g_ref, wg_ref, wu_ref, wd_ref, xo_ref, a_ref, b_ref, h_sc, acc):
        j = pl.program_id(1)

        @pl.when(j == 0)
        def _():
            _, xh = _rms(x_ref[...])
            h_sc[...] = (xh * g_ref[...]).astype(BF16)
            acc[...] = jnp.zeros_like(acc)

        h = h_sc[...]
        a = jnp.dot(h, wg_ref[0], preferred_element_type=F32)
        b = jnp.dot(h, wu_ref[0], preferred_element_type=F32)
        a_ref[0] = a
        b_ref[0] = b
        f = (a * _sig(a) * b).astype(BF16)
        acc[...] += jnp.dot(f, wd_ref[0], preferred_element_type=F32)

        @pl.when(j == NSH - 1)
        def _():
            xo_ref[...] = x_ref[...] + 0.5 * acc[...]

    return pl.pallas_call(
        body, name=name, grid=(s // tm, NSH),
        in_specs=[pl.BlockSpec((tm, D), lambda i, j: (i, 0)), pl.BlockSpec((1, D), lambda i, j: (0, 0)),
                  pl.BlockSpec((1, D, FSH), lambda i, j: (j, 0, 0)), pl.BlockSpec((1, D, FSH), lambda i, j: (j, 0, 0)),
                  pl.BlockSpec((1, FSH, D), lambda i, j: (j, 0, 0))],
        out_specs=[pl.BlockSpec((tm, D), lambda i, j: (i, 0)), pl.BlockSpec((1, tm, FSH), lambda i, j: (j, i, 0)),
                   pl.BlockSpec((1, tm, FSH), lambda i, j: (j, i, 0))],
        out_shape=[_sds((s, D)), _sds((NSH, s, FSH)), _sds((NSH, s, FSH))],
        scratch_shapes=[pltpu.VMEM((tm, D), BF16), pltpu.VMEM((tm, D), F32)],
        compiler_params=_cparams(2),
    )(x, gain, wg, wu, wd)


def _ffn_bwd(x, dout, gain, a, b, wg, wu, wd, name):
    s = x.shape[0]
    tm = min(512, s)

    def body(x_ref, d_ref, g_ref, a_ref, b_ref, wg_ref, wu_ref, wd_ref,
             dx_ref, dg_ref, h_ref, do_ref, f_ref, da_ref, db_ref, do_sc, dh_acc):
        i = pl.program_id(0)
        j = pl.program_id(1)

        @pl.when(jnp.logical_and(i == 0, j == 0))
        def _():
            dg_ref[...] = jnp.zeros_like(dg_ref)

        @pl.when(j == 0)
        def _():
            _, xh = _rms(x_ref[...])
            h_ref[...] = (xh * g_ref[...]).astype(BF16)
            do = (0.5 * d_ref[...]).astype(BF16)
            do_sc[...] = do
            do_ref[...] = do
            dh_acc[...] = jnp.zeros_like(dh_acc)

        do = do_sc[...]
        df = _dot_nt(do, wd_ref[0])
        av = a_ref[0]
        bv = b_ref[0]
        sa = _sig(av)
        f_ref[0] = (av * sa * bv).astype(BF16)
        da = (df * bv * sa * (1.0 + av * (1.0 - sa))).astype(BF16)
        db = (df * av * sa).astype(BF16)
        da_ref[0] = da
        db_ref[0] = db
        dh_acc[...] += _dot_nt(da, wg_ref[0]) + _dot_nt(db, wu_ref[0])

        @pl.when(j == NSH - 1)
        def _():
            r, xh = _rms(x_ref[...])
            dh = dh_acc[...]
            dg_ref[...] += _colsum(dh * xh)
            dx_ref[...] = d_ref[...] + _rms_bwd(dh, xh, r, g_ref[...])

    tok = pl.BlockSpec((tm, D), lambda i, j: (i, 0))
    sh = pl.BlockSpec((1, tm, FSH), lambda i, j: (j, i, 0))
    return pl.pallas_call(
        body, name=name, grid=(s // tm, NSH),
        in_specs=[tok, tok, pl.BlockSpec((1, D), lambda i, j: (0, 0)), sh, sh,
                  pl.BlockSpec((1, D, FSH), lambda i, j: (j, 0, 0)), pl.BlockSpec((1, D, FSH), lambda i, j: (j, 0, 0)),
                  pl.BlockSpec((1, FSH, D), lambda i, j: (j, 0, 0))],
        out_specs=[tok, pl.BlockSpec((1, D), lambda i, j: (0, 0)), tok, tok, sh, sh, sh],
        out_shape=[_sds((s, D)), _sds((1, D)), _sds((s, D), BF16), _sds((s, D), BF16),
                   _sds((NSH, s, FSH), BF16), _sds((NSH, s, FSH), BF16), _sds((NSH, s, FSH), BF16)],
        scratch_shapes=[pltpu.VMEM((tm, D), BF16), pltpu.VMEM((tm, D), F32)],
        compiler_params=_cparams(2),
    )(x, dout, gain, a, b, wg, wu, wd)


def _tn(a, b, name):
    a_g = a.ndim == 3
    b_g = b.ndim == 3
    g = a.shape[0] if a_g else (b.shape[0] if b_g else 1)
    s, k = a.shape[-2:]
    n = b.shape[-1]
    ts = min(512, s)

    def body(a_ref, b_ref, o_ref):
        @pl.when(pl.program_id(1) == 0)
        def _():
            o_ref[...] = jnp.zeros_like(o_ref)

        av = a_ref[0] if a_g else a_ref[...]
        bv = b_ref[0] if b_g else b_ref[...]
        o_ref[0] += _dot_tn(av, bv)

    a_spec = pl.BlockSpec((1, ts, k), lambda gi, si: (gi, si, 0)) if a_g else pl.BlockSpec((ts, k), lambda gi, si: (si, 0))
    b_spec = pl.BlockSpec((1, ts, n), lambda gi, si: (gi, si, 0)) if b_g else pl.BlockSpec((ts, n), lambda gi, si: (si, 0))
    return pl.pallas_call(
        body, name=name, grid=(g, s // ts), in_specs=[a_spec, b_spec],
        out_specs=pl.BlockSpec((1, k, n), lambda gi, si: (gi, 0, 0)),
        out_shape=_sds((g, k, n)), compiler_params=_cparams(2),
    )(a, b)


_P_WIDTHS = (RGW, RGW, QKVW, ZW, BAP)


def _inproj(x1, gain, ws, name):
    s = x1.shape[0]
    tm = min(256, s)

    def body(x_ref, g_ref, *refs):
        w_refs = refs[:5]
        h_ref = refs[5]
        p_refs = refs[6:]
        _, xh = _rms(x_ref[...])
        h = (xh * g_ref[...]).astype(BF16)
        h_ref[...] = h
        for w_ref, p_ref in zip(w_refs, p_refs):
            p_ref[...] = jnp.dot(h, w_ref[...], preferred_element_type=F32)

    return pl.pallas_call(
        body, name=name, grid=(s // tm,),
        in_specs=[_rows(tm, D), _full((1, D))] + [_full((D, w)) for w in _P_WIDTHS],
        out_specs=[_rows(tm, D)] + [_rows(tm, w) for w in _P_WIDTHS],
        out_shape=[_sds((s, D), BF16)] + [_sds((s, w)) for w in _P_WIDTHS],
        compiler_params=_cparams(1),
    )(x1, gain, *ws)


def _inproj_bwd(x1, dx2, gain, dps, ws, name):
    s = x1.shape[0]
    tm = min(256, s)

    def body(x_ref, d_ref, g_ref, *refs):
        dp_refs = refs[:5]
        w_refs = refs[5:10]
        dx_ref, dg_ref = refs[10:]

        @pl.when(pl.program_id(0) == 0)
        def _():
            dg_ref[...] = jnp.zeros_like(dg_ref)

        dh = jnp.zeros((tm, D), F32)
        for dp_ref, w_ref in zip(dp_refs, w_refs):
            dh = dh + _dot_nt(dp_ref[...], w_ref[...])
        r, xh = _rms(x_ref[...])
        dg_ref[...] += _colsum(dh * xh)
        dx_ref[...] = d_ref[...] + _rms_bwd(dh, xh, r, g_ref[...])

    return pl.pallas_call(
        body, name=name, grid=(s // tm,),
        in_specs=[_rows(tm, D), _rows(tm, D), _full((1, D))] + [_rows(tm, w) for w in _P_WIDTHS]
        + [_full((D, w)) for w in _P_WIDTHS],
        out_specs=[_rows(tm, D), _full((1, D))],
        out_shape=[_sds((s, D)), _sds((1, D))],
        compiler_params=_cparams(1),
    )(x1, dx2, gain, *dps, *ws)


def _halo_specs(s, t, c):
    nb8 = s // 8
    tb = t // 8
    prev = pl.BlockSpec((8, c), lambda i: (jnp.maximum(i * tb - 1, 0), 0))
    nxt = pl.BlockSpec((8, c), lambda i: (jnp.minimum((i + 1) * tb, nb8 - 1), 0))
    return prev, nxt


def _edge_masks(nb):
    i = pl.program_id(0)
    return jnp.where(i > 0, 1.0, 0.0).astype(F32), jnp.where(i < nb - 1, 1.0, 0.0).astype(F32)


def _shifted(xx, off, t):
    n = t + 16
    sh = (-off) % n
    rolled = xx if sh == 0 else pltpu.roll(xx, sh, 0)
    return rolled[8:8 + t]


def _conv(x, w8, bias, name):
    s, c = x.shape
    t = min(256, s)
    nb = s // t

    def body(x_ref, xp_ref, xn_ref, w_ref, b_ref, o_ref):
        pm, nm = _edge_masks(nb)
        for c0 in range(0, c, 512):
            cols = slice(c0, c0 + 512)
            xx = jnp.concatenate([xp_ref[:, cols] * pm, x_ref[:, cols], xn_ref[:, cols] * nm], axis=0)
            acc = jnp.zeros((t, 512), F32) + b_ref[:, cols]
            for j in range(4):
                acc = acc + w_ref[j:j + 1, cols] * _shifted(xx, j - 2, t)
            o_ref[:, cols] = acc

    prev, nxt = _halo_specs(s, t, c)
    return pl.pallas_call(
        body, name=name, grid=(nb,),
        in_specs=[_rows(t, c), prev, nxt, _full((8, c)), _full((1, c))],
        out_specs=_rows(t, c), out_shape=_sds((s, c)), compiler_params=_cparams(1),
    )(x, x, x, w8, bias)


def _conv_bwd(x, dc, w8, name):
    s, c = x.shape
    t = min(256, s)
    nb = s // t

    def body(x_ref, d_ref, dp_ref, dn_ref, w_ref, dx_ref, dw_ref, db_ref):
        @pl.when(pl.program_id(0) == 0)
        def _():
            dw_ref[...] = jnp.zeros_like(dw_ref)
            db_ref[...] = jnp.zeros_like(db_ref)

        pm, nm = _edge_masks(nb)
        for c0 in range(0, c, 512):
            cols = slice(c0, c0 + 512)
            dd = jnp.concatenate([dp_ref[:, cols] * pm, d_ref[:, cols], dn_ref[:, cols] * nm], axis=0)
            xv = x_ref[:, cols]
            acc = jnp.zeros((t, 512), F32)
            for j in range(4):
                dsh = _shifted(dd, 2 - j, t)
                acc = acc + w_ref[j:j + 1, cols] * dsh
                dw_ref[j:j + 1, cols] += _colsum(dsh * xv)
            dx_ref[:, cols] = acc
            db_ref[:, cols] += _colsum(d_ref[:, cols])

    prev, nxt = _halo_specs(s, t, c)
    return pl.pallas_call(
        body, name=name, grid=(nb,),
        in_specs=[_rows(t, c), _rows(t, c), prev, nxt, _full((8, c))],
        out_specs=[_rows(t, c), _full((8, c)), _full((1, c))],
        out_shape=[_sds((s, c)), _sds((8, c)), _sds((1, c))], compiler_params=_cparams(1),
    )(x, dc, dc, dc, w8)


def _shift_rows(x, direction, name):
    s, c = x.shape
    t = min(256, s)
    nb = s // t

    def body(x_ref, xp_ref, xn_ref, o_ref):
        pm, nm = _edge_masks(nb)
        xx = jnp.concatenate([xp_ref[...] * pm, x_ref[...], xn_ref[...] * nm], axis=0)
        o_ref[...] = _shifted(xx, direction, t)

    prev, nxt = _halo_specs(s, t, c)
    return pl.pallas_call(
        body, name=name, grid=(nb,), in_specs=[_rows(t, c), prev, nxt],
        out_specs=_rows(t, c), out_shape=_sds((s, c)), compiler_params=_cparams(1),
    )(x, x, x)


def _rg_gates(xc, pre, lam_row):
    sp8 = RG_C * _softplus(-lam_row)
    out = []
    for d in range(2):
        r = _sig(pre[:, RGW * d:RGW * (d + 1)])
        gi = _sig(pre[:, 2 * RGW + RGW * d:2 * RGW + RGW * (d + 1)])
        la = -r * sp8[:, RGW * d:RGW * (d + 1)]
        a = jnp.exp(la)
        mult = jnp.sqrt(_neg_expm1(2.0 * la))
        out.append((r, gi, a, mult))
    return out


def _mix_prep(c_rg, c_qkv, p_ba, wgates, gbias, lam_row, alog_row, dtb_row, name):
    s = c_rg.shape[0]
    t = min(256, s)

    def body(xc_ref, cq_ref, pc_ref, wg_ref, gb_ref, lam_ref, alog_ref, dtb_ref,
             a0_ref, b0_ref, a1_ref, b1_ref, q_ref, k_ref, v_ref, bg_ref):
        xc = xc_ref[...]
        pre = _dot(xc, wg_ref[...]) + gb_ref[...]
        gates = _rg_gates(xc, pre, lam_ref[...])
        for (r, gi, a, mult), a_ref, b_ref in zip(gates, (a0_ref, a1_ref), (b0_ref, b1_ref)):
            a_ref[...] = a
            b_ref[...] = mult * gi * xc
        cq = cq_ref[...]
        sq = cq * _sig(cq)
        for h in range(NH):
            sl = slice(DH * h, DH * (h + 1))
            qh = sq[:, sl]
            q_ref[:, sl] = qh * lax.rsqrt(jnp.sum(qh * qh, axis=-1, keepdims=True) + EPS) * (DH ** -0.5)
            kh = sq[:, RGW + DH * h:RGW + DH * (h + 1)]
            k_ref[:, sl] = kh * lax.rsqrt(jnp.sum(kh * kh, axis=-1, keepdims=True) + EPS)
        v_ref[...] = sq[:, 2 * RGW:]
        pc = pc_ref[...]
        lane = lax.broadcasted_iota(jnp.int32, pc.shape, 1)
        beta = _sig(pc)
        g = -jnp.exp(alog_ref[...]) * _softplus(pc + dtb_ref[...])
        bg_ref[...] = jnp.where(lane < 8, beta, jnp.where(lane < 16, g, 0.0))

    return pl.pallas_call(
        body, name=name, grid=(s // t,),
        in_specs=[_rows(t, RGW), _rows(t, QKVW), _rows(t, BAP), _full((RGW, 4 * RGW)), _full((1, 4 * RGW)),
                  _full((1, 2 * RGW)), _full((1, BAP)), _full((1, BAP))],
        out_specs=[_rows(t, RGW)] * 7 + [_rows(t, BAP)],
        out_shape=[_sds((s, RGW))] * 7 + [_sds((s, BAP))],
        compiler_params=_cparams(1),
    )(c_rg, c_qkv, p_ba, wgates, gbias, lam_row, alog_row, dtb_row)


def _scan(a, b, reverse, name):
    s, c = a.shape
    t = min(512, s)
    nb = s // t
    ng = t // 8
    idx = (lambda i: (nb - 1 - i, 0)) if reverse else (lambda i: (i, 0))

    def body(a_ref, b_ref, h_ref, carry):
        @pl.when(pl.program_id(0) == 0)
        def _():
            carry[...] = jnp.zeros_like(carry)

        row = lax.broadcasted_iota(jnp.int32, (8, c), 0)

        def group(gi, cv):
            g = (ng - 1 - gi) if reverse else gi
            r0 = pl.multiple_of(g * 8, 8)
            av = a_ref[pl.ds(r0, 8), :]
            bv = b_ref[pl.ds(r0, 8), :]
            for k in (1, 2, 4):
                sh = (8 - k) if reverse else k
                m = (row < 8 - k) if reverse else (row >= k)
                a_s = pltpu.roll(av, sh, 0)
                b_s = pltpu.roll(bv, sh, 0)
                bv = jnp.where(m, av * b_s + bv, bv)
                av = jnp.where(m, av * a_s, av)
            hv = av * cv + bv
            h_ref[pl.ds(r0, 8), :] = hv
            return hv[0:1, :] if reverse else hv[7:8, :]

        carry[0:1, :] = lax.fori_loop(0, ng, group, carry[0:1, :])

    return pl.pallas_call(
        body, name=name, grid=(nb,), in_specs=[pl.BlockSpec((t, c), idx), pl.BlockSpec((t, c), idx)],
        out_specs=pl.BlockSpec((t, c), idx), out_shape=_sds((s, c)),
        scratch_shapes=[pltpu.VMEM((8, c), F32)], compiler_params=_cparams(1),
    )(a, b)


def _gates_bwd(xc, wgates, gbias, lam_row, lam0, lam1, h0s, h1s, name):
    s = xc.shape[0]
    t = min(256, s)

    def body(xc_ref, wg_ref, gb_ref, lam_ref, l0_ref, l1_ref, h0_ref, h1_ref,
             dxc_ref, dpre_ref, xcb_ref, dgb_ref, dlam_ref):
        @pl.when(pl.program_id(0) == 0)
        def _():
            dgb_ref[...] = jnp.zeros_like(dgb_ref)
            dlam_ref[...] = jnp.zeros_like(dlam_ref)

        xv = xc_ref[...]
        pre = _dot(xv, wg_ref[...]) + gb_ref[...]
        lam_row_v = lam_ref[...]
        sp8 = RG_C * _softplus(-lam_row_v)
        dsp_dlam = -RG_C * _sig(-lam_row_v)
        gates = _rg_gates(xv, pre, lam_row_v)
        dxc = jnp.zeros((t, RGW), F32)
        dpre_r = []
        dpre_i = []
        for d, ((r, gi, a, mult), l_ref, h_ref) in enumerate(zip(gates, (l0_ref, l1_ref), (h0_ref, h1_ref))):
            dbb = l_ref[...]
            da = dbb * h_ref[...]
            cs = slice(RGW * d, RGW * (d + 1))
            dmult = dbb * gi * xv
            dgi = dbb * mult * xv
            dxc = dxc + dbb * mult * gi
            dla = da * a - dmult * a * a / mult
            dr = -dla * sp8[:, cs]
            dlam_ref[:, cs] += _colsum(-dla * r) * dsp_dlam[:, cs]
            dpre_r.append(dr * r * (1.0 - r))
            dpre_i.append(dgi * gi * (1.0 - gi))
        dpre = jnp.concatenate(dpre_r + dpre_i, axis=1)
        dgb_ref[...] += _colsum(dpre)
        dpre_b = dpre.astype(BF16)
        dpre_ref[...] = dpre_b
        xcb_ref[...] = xv.astype(BF16)
        dxc_ref[...] = dxc + _dot_nt(dpre_b, wg_ref[...])

    return pl.pallas_call(
        body, name=name, grid=(s // t,),
        in_specs=[_rows(t, RGW), _full((RGW, 4 * RGW)), _full((1, 4 * RGW)), _full((1, 2 * RGW))] + [_rows(t, RGW)] * 4,
        out_specs=[_rows(t, RGW), _rows(t, 4 * RGW), _rows(t, RGW), _full((1, 4 * RGW)), _full((1, 2 * RGW))],
        out_shape=[_sds((s, RGW)), _sds((s, 4 * RGW), BF16), _sds((s, RGW), BF16), _sds((1, 4 * RGW)), _sds((1, 2 * RGW))],
        compiler_params=_cparams(1),
    )(xc, wgates, gbias, lam_row, lam0, lam1, h0s, h1s)


class _GdnMasks:
    def __init__(self, d):
        ri = lax.broadcasted_iota(jnp.int32, (CHUNK, CHUNK), 0)
        ci = lax.broadcasted_iota(jnp.int32, (CHUNK, CHUNK), 1)
        self.incl = (ri >= ci) if d == 0 else (ri <= ci)
        self.strict = (ri > ci) if d == 0 else (ri < ci)
        b16 = jnp.right_shift(ri, 4) == jnp.right_shift(ci, 4)
        b32 = jnp.right_shift(ri, 5) == jnp.right_shift(ci, 5)
        self.diag16 = b16
        self.off32 = jnp.logical_and(b32, jnp.logical_not(b16))
        self.off64 = jnp.logical_not(b32)
        self.eye = jnp.where(ri == ci, 1.0, 0.0).astype(F32)
        self.tri = jnp.where(self.incl, 1.0, 0.0).astype(F32)
        self.last = CHUNK - 1 if d == 0 else 0


def _tri_inv(lmat, m):
    return _tri_inv_many([lmat], [m])[0]


def _tri_inv_many(lmats, masks):
    n = len(lmats)
    ns = [jnp.where(masks[i].diag16, lmats[i], 0.0) for i in range(n)]
    ps = [masks[i].eye - ns[i] for i in range(n)]
    qs = [_dot3(ns[i], ns[i]) for i in range(n)]
    for step in range(3):
        ps = [_dot3(ps[i], masks[i].eye + qs[i]) for i in range(n)]
        if step < 2:
            qs = [_dot3(qs[i], qs[i]) for i in range(n)]
    for off in ("off32", "off64"):
        ts = [_dot3(ps[i], jnp.where(getattr(masks[i], off), lmats[i], 0.0)) for i in range(n)]
        ps = [ps[i] - _dot3(ts[i], ps[i]) for i in range(n)]
    return ps


def _chunk_cumsums(m, bgv):
    return _dot_exact(m.tri, bgv, _NN, True), _dot_exact(m.tri, bgv, ((0,), (1,)), False)


class _GdnHead:
    def __init__(self, qh, kh, vh, kk, q0, bg, gcs, gcs_t, d, h, m):
        cb = 4 * d + h
        cg = 8 + 4 * d + h
        self.q, self.k, self.v = qh, kh, vh
        self.beta = bg[:, cb:cb + 1]
        gcol = gcs[:, cg:cg + 1]
        grow = gcs_t[cg:cg + 1, :]
        gl = gcs[m.last:m.last + 1, cg:cg + 1]
        self.decay = jnp.exp(jnp.where(m.incl, gcol - grow, -1e30))
        self.kb = kh * self.beta
        self.vb = vh * self.beta
        self.a0 = kk * self.beta
        self.q0 = q0
        self.lmat = jnp.where(m.strict, self.a0 * self.decay, 0.0)
        self.attn = self.q0 * self.decay
        self.eg = jnp.exp(gcol)
        self.ek = jnp.exp(gl - gcol)
        self.cd = jnp.exp(gl)
        self.kg = self.kb * self.eg
        self.qd = qh * self.eg
        self.kd = kh * self.ek


HW = NH * DH
SEQ_CB = 4


def _head(h):
    return slice(DH * h, DH * (h + 1))


def _gdn_local_fwd(q, k, v, bg, name):
    s = q.shape[0]
    n = s // CHUNK

    def body(q_ref, k_ref, v_ref, bg_ref, t_ref, u_ref, w_ref, qd_ref, kd_ref, at_ref, cd_ref):
        bgv = bg_ref[...]
        qs = [q_ref[:, _head(h)] for h in range(NH)]
        ks = [k_ref[:, _head(h)] for h in range(NH)]
        kk = [_dot_nt(ks[h], ks[h]) for h in range(NH)]
        q0 = [_dot_nt(qs[h], ks[h]) for h in range(NH)]
        inst = []
        for d in range(2):
            m = _GdnMasks(d)
            gcs, gcs_t = _chunk_cumsums(m, bgv)
            for h in range(NH):
                c = _GdnHead(qs[h], ks[h], v_ref[:, _head(h)], kk[h], q0[h], bgv, gcs, gcs_t, d, h, m)
                inst.append((d, h, m, c))
        tms = _tri_inv_many([c.lmat for _, _, _, c in inst], [m for _, _, m, _ in inst])
        for (d, h, m, c), tm in zip(inst, tms):
            sl = _head(h)
            t_ref[0, d, h] = tm
            u_ref[d, :, sl] = _dot(tm, c.vb)
            w_ref[d, :, sl] = _dot(tm, c.kg).astype(BF16)
            qd_ref[d, :, sl] = c.qd.astype(BF16)
            kd_ref[d, :, sl] = c.kd.astype(BF16)
            at_ref[0, d, h] = c.attn.astype(BF16)
            cd_ref[0, 4 * d + h:4 * d + h + 1, :] = jnp.broadcast_to(c.cd, (1, DH))

    tok = _rows(CHUNK, HW)
    tok2 = pl.BlockSpec((2, CHUNK, HW), lambda i: (0, i, 0))
    mat = pl.BlockSpec((1, 2, NH, CHUNK, CHUNK), lambda i: (i, 0, 0, 0, 0))
    return pl.pallas_call(
        body, name=name, grid=(n,), in_specs=[tok, tok, tok, _rows(CHUNK, BAP)],
        out_specs=[mat, tok2, tok2, tok2, tok2, mat, pl.BlockSpec((1, 8, DH), lambda i: (i, 0, 0))],
        out_shape=[_sds((n, 2, NH, CHUNK, CHUNK)), _sds((2, s, HW)), _sds((2, s, HW), BF16), _sds((2, s, HW), BF16),
                   _sds((2, s, HW), BF16), _sds((n, 2, NH, CHUNK, CHUNK), BF16), _sds((n, 8, DH))],
        compiler_params=_cparams(1),
    )(q, k, v, bg)


def _seq_specs(s, order):
    n = s // CHUNK
    cb = min(SEQ_CB, n)
    nb = n // cb
    tb = cb * CHUNK

    def blk(d):
        return (lambda i: i) if order[d] else (lambda i: nb - 1 - i)

    def per_dir(make):
        return [make(d, blk(d)) for d in range(2)]

    tok2 = per_dir(lambda d, f: pl.BlockSpec((1, tb, HW), lambda i: (d, f(i), 0)))
    tok = per_dir(lambda d, f: pl.BlockSpec((tb, HW), lambda i: (f(i), 0)))
    mat = per_dir(lambda d, f: pl.BlockSpec((cb, 1, NH, CHUNK, CHUNK), lambda i: (f(i), d, 0, 0, 0)))
    cds = per_dir(lambda d, f: pl.BlockSpec((cb, 8, DH), lambda i: (f(i), 0, 0)))
    sts = per_dir(lambda d, f: pl.BlockSpec((cb, NH, DH, DH), lambda i: (f(i), 0, 0, 0)))
    dcd = per_dir(lambda d, f: pl.BlockSpec((cb, NH, DH), lambda i: (f(i), 0, 0)))
    return n, cb, nb, tok2, tok, mat, cds, sts, dcd


def _gdn_seq_fwd(u, w, qd, kd, at, cd, name):
    s = u.shape[1]
    n, cb, nb, tok2, tok, mat, cds, sts, _ = _seq_specs(s, (True, False))

    def body(*refs):
        ins = (refs[0:6], refs[6:12])
        outs = (refs[12:15], refs[15:18])
        st = refs[18]

        @pl.when(pl.program_id(0) == 0)
        def _():
            st[...] = jnp.zeros_like(st)

        for j in range(cb):
            items = []
            for d in range(2):
                jj = j if d == 0 else cb - 1 - j
                items += [(d, h, jj, slice(CHUNK * jj, CHUNK * (jj + 1)), _head(h)) for h in range(NH)]
            shs = [st[d, h] for d, h, _, _, _ in items]
            wss = [_dot(ins[d][1][0, rows, sl], sh) for (d, h, jj, rows, sl), sh in zip(items, shs)]
            vns = [ins[d][0][0, rows, sl] - ws for (d, h, jj, rows, sl), ws in zip(items, wss)]
            news = [sh * ins[d][5][jj, 4 * d + h:4 * d + h + 1, :] + _dot_tn(ins[d][3][0, rows, sl], vn)
                    for (d, h, jj, rows, sl), sh, vn in zip(items, shs, vns)]
            for (d, h, jj, rows, sl), sh, vn, new in zip(items, shs, vns, news):
                o_r, s_r, vn_r = outs[d]
                st[d, h] = new
                s_r[jj, h] = sh
                vn_r[rows, sl] = vn
                o_r[rows, sl] = _dot(ins[d][2][0, rows, sl], sh) + _dot(ins[d][4][jj, 0, h], vn)

    in_specs, out_specs, out_shape = [], [], []
    for d in range(2):
        in_specs += [tok2[d]] * 4 + [mat[d], cds[d]]
        out_specs += [tok[d], sts[d], tok[d]]
        out_shape += [_sds((s, HW)), _sds((n, NH, DH, DH)), _sds((s, HW))]
    return pl.pallas_call(
        body, name=name, grid=(nb,), in_specs=in_specs, out_specs=out_specs, out_shape=out_shape,
        scratch_shapes=[pltpu.VMEM((2, NH, DH, DH), F32)], compiler_params=_cparams(1),
    )(u, w, qd, kd, at, cd, u, w, qd, kd, at, cd)


def _gdn_seq_bwd(do, w, qd, kd, at, cd, states, vns, name):
    s = do.shape[0]
    n, cb, nb, tok2, tok, mat, cds, sts, dcd = _seq_specs(s, (False, True))

    def body(*refs):
        ins = (refs[0:8], refs[8:16])
        outs = (refs[16:21], refs[21:26])
        dst = refs[26]

        @pl.when(pl.program_id(0) == 0)
        def _():
            dst[...] = jnp.zeros_like(dst)

        for j in range(cb):
            items = []
            for d in range(2):
                jj = cb - 1 - j if d == 0 else j
                items += [(d, h, jj, slice(CHUNK * jj, CHUNK * (jj + 1)), _head(h)) for h in range(NH)]
            dsns = [dst[d, h] for d, h, _, _, _ in items]
            dohs = [ins[d][0][rows, sl] for d, h, jj, rows, sl in items]
            d_vns = [_dot_tn(ins[d][4][jj, 0, h], doh) + _dot(ins[d][3][0, rows, sl], dsn)
                     for (d, h, jj, rows, sl), doh, dsn in zip(items, dohs, dsns)]
            news = [ins[d][5][jj, 4 * d + h:4 * d + h + 1, :] * dsn + _dot_tn(ins[d][2][0, rows, sl], doh)
                    - _dot_tn(ins[d][1][0, rows, sl], d_vn)
                    for (d, h, jj, rows, sl), doh, dsn, d_vn in zip(items, dohs, dsns, d_vns)]
            for (d, h, jj, rows, sl), doh, dsn, d_vn, new in zip(items, dohs, dsns, d_vns, news):
                dvn_r, dkd_r, dqd_r, dw_r, dcd_r = outs[d]
                sh = ins[d][6][jj, h]
                dst[d, h] = new
                dvn_r[rows, sl] = d_vn
                dkd_r[rows, sl] = _dot_nt(ins[d][7][rows, sl], dsn)
                dqd_r[rows, sl] = _dot_nt(doh, sh)
                dw_r[rows, sl] = -_dot_nt(d_vn, sh)
                d_cd = jnp.sum(jnp.sum(sh * dsn, axis=1, keepdims=True), axis=0, keepdims=True)
                dcd_r[jj, h:h + 1, :] = jnp.broadcast_to(d_cd, (1, DH))

    in_specs, out_specs, out_shape, args = [], [], [], []
    for d in range(2):
        in_specs += [tok[d]] + [tok2[d]] * 3 + [mat[d], cds[d], sts[d], tok[d]]
        args += [do, w, qd, kd, at, cd, states[d], vns[d]]
        out_specs += [tok[d]] * 4 + [dcd[d]]
        out_shape += [_sds((s, HW))] * 4 + [_sds((n, NH, DH))]
    return pl.pallas_call(
        body, name=name, grid=(nb,), in_specs=in_specs, out_specs=out_specs, out_shape=out_shape,
        scratch_shapes=[pltpu.VMEM((2, NH, DH, DH), F32)], compiler_params=_cparams(1),
    )(*args)


def _gdn_local_bwd(q, k, v, bg, tmat, do, vns, seq_grads, name):
    s = q.shape[0]
    n = s // CHUNK

    def body(*refs):
        q_ref, k_ref, v_ref, bg_ref, t_ref, do_ref = refs[0:6]
        vn_refs = refs[6:8]
        sg = (refs[8:13], refs[13:18])
        dq_ref, dk_ref, dv_ref, dbg_ref = refs[18:]
        bgv = bg_ref[...]
        qs = [q_ref[:, _head(h)] for h in range(NH)]
        ks = [k_ref[:, _head(h)] for h in range(NH)]
        kk = [_dot_nt(ks[h], ks[h]) for h in range(NH)]
        q0 = [_dot_nt(qs[h], ks[h]) for h in range(NH)]
        lane = lax.broadcasted_iota(jnp.int32, (CHUNK, BAP), 1)
        rowi = lax.broadcasted_iota(jnp.int32, (CHUNK, 1), 0)
        ones = jnp.ones((CHUNK, DH), F32)
        dbg = jnp.zeros((CHUNK, BAP), F32)
        acc = [[None, None, None] for _ in range(NH)]
        inst = []
        for d in range(2):
            m = _GdnMasks(d)
            gcs, gcs_t = _chunk_cumsums(m, bgv)
            for h in range(NH):
                c = _GdnHead(qs[h], ks[h], v_ref[:, _head(h)], kk[h], q0[h], bgv, gcs, gcs_t, d, h, m)
                inst.append((d, h, m, c))
        tms = [t_ref[0, d, h] for d, h, _, _ in inst]
        d_vns = [sg[d][0][:, _head(h)] for d, h, _, _ in inst]
        d_ws = [sg[d][3][:, _head(h)] for d, h, _, _ in inst]
        d_ts = [_dot_nt(d_vns[i], c.vb) + _dot_nt(d_ws[i], c.kg) for i, (_, _, _, c) in enumerate(inst)]
        xs = [_dot3(tms[i], d_ts[i], _TN) for i in range(8)]
        d_ls = [jnp.where(inst[i][2].strict, -_dot3(xs[i], tms[i], _NT), 0.0) for i in range(8)]
        d_attns = [jnp.where(m.incl, _dot_nt(do_ref[:, _head(h)], vn_refs[d][:, _head(h)]), 0.0) for d, h, m, _ in inst]
        d_vbs = [_dot_tn(tms[i], d_vns[i]) for i in range(8)]
        d_kgs = [_dot_tn(tms[i], d_ws[i]) for i in range(8)]
        d_a0s = [d_ls[i] * c.decay for i, (_, _, _, c) in enumerate(inst)]
        d_q0s = [d_attns[i] * c.decay for i, (_, _, _, c) in enumerate(inst)]
        es = [(d_ls[i] * c.a0 + d_attns[i] * c.q0) * c.decay for i, (_, _, _, c) in enumerate(inst)]
        kb_mm = [_dot(d_a0s[i], c.k) for i, (_, _, _, c) in enumerate(inst)]
        q_mm = [_dot(d_q0s[i], c.k) for i, (_, _, _, c) in enumerate(inst)]
        k_mm = [_dot_tn(d_a0s[i], c.kb) + _dot_tn(d_q0s[i], c.q) for i, (_, _, _, c) in enumerate(inst)]
        e_cols = [_dot_exact(ones, es[i], _TN, False)[:, 0:1] for i in range(8)]
        d_gcs, d_betas = [], []
        for i, (d, h, m, c) in enumerate(inst):
            sl = _head(h)
            d_kd, d_qd = sg[d][1][:, sl], sg[d][2][:, sl]
            d_cd = sg[d][4][0, h:h + 1, 0:1]
            d_vb, d_kg = d_vbs[i], d_kgs[i]
            d_kb = kb_mm[i] + d_kg * c.eg
            parts = (q_mm[i] + d_qd * c.eg, k_mm[i] + d_kd * c.ek + d_kb * c.beta, d_vb * c.beta)
            acc[h] = [p if a is None else a + p for a, p in zip(acc[h], parts)]
            s_kd = jnp.sum(d_kd * c.kd, axis=1, keepdims=True)
            d_gc = (jnp.sum(d_kg * c.kg, axis=1, keepdims=True) + jnp.sum(d_qd * c.qd, axis=1, keepdims=True) - s_kd
                    + jnp.sum(es[i], axis=1, keepdims=True) - e_cols[i])
            d_gl = jnp.sum(s_kd, axis=0, keepdims=True) + d_cd * c.cd
            d_gcs.append(d_gc + jnp.where(rowi == m.last, d_gl, 0.0))
            d_betas.append(jnp.sum(d_kb * c.k, axis=1, keepdims=True) + jnp.sum(d_vb * c.v, axis=1, keepdims=True))
        d_gs = [_dot_exact(m.tri, d_gcs[i] * ones, _TN, True)[:, 0:1] for i, (_, _, m, _) in enumerate(inst)]
        for i, (d, h, _, _) in enumerate(inst):
            dbg = dbg + jnp.where(lane == 4 * d + h, d_betas[i], 0.0) + jnp.where(lane == 8 + 4 * d + h, d_gs[i], 0.0)
        for h in range(NH):
            dq_ref[:, _head(h)], dk_ref[:, _head(h)], dv_ref[:, _head(h)] = acc[h]
        dbg_ref[...] = dbg

    tok = _rows(CHUNK, HW)
    bgs = _rows(CHUNK, BAP)
    mat = pl.BlockSpec((1, 2, NH, CHUNK, CHUNK), lambda i: (i, 0, 0, 0, 0))
    dcd = pl.BlockSpec((1, NH, DH), lambda i: (i, 0, 0))
    args = [q, k, v, bg, tmat, do, vns[0], vns[1]]
    in_specs = [tok, tok, tok, bgs, mat, tok, tok, tok]
    for d in range(2):
        args += list(seq_grads[d])
        in_specs += [tok] * 4 + [dcd]
    return pl.pallas_call(
        body, name=name, grid=(n,), in_specs=in_specs, out_specs=[tok, tok, tok, bgs],
        out_shape=[_sds((s, HW))] * 3 + [_sds((s, BAP))], compiler_params=_cparams(1),
    )(*args)


def _prep_bwd(c_qkv, p_ba, alog_row, dtb_row, dq, dk, dv, dbg, name):
    s = c_qkv.shape[0]
    t = min(256, s)

    def body(cq_ref, pc_ref, alog_ref, dtb_ref, dq_ref, dk_ref, dv_ref, dbg_ref,
             dcq_ref, dpc_ref, dalog_ref, ddtb_ref):
        @pl.when(pl.program_id(0) == 0)
        def _():
            dalog_ref[...] = jnp.zeros_like(dalog_ref)
            ddtb_ref[...] = jnp.zeros_like(ddtb_ref)

        cq = cq_ref[...]
        sq = cq * _sig(cq)
        sg = _silu_grad(cq)
        for h in range(NH):
            sl = slice(DH * h, DH * (h + 1))
            for off, d_ref, scale in ((0, dq_ref, DH ** -0.5), (RGW, dk_ref, 1.0)):
                csl = slice(off + DH * h, off + DH * (h + 1))
                xh = sq[:, csl]
                nrm = lax.rsqrt(jnp.sum(xh * xh, axis=-1, keepdims=True) + EPS)
                y = xh * nrm
                dy = d_ref[:, sl] * scale
                dcq_ref[:, csl] = nrm * (dy - y * jnp.sum(dy * y, axis=-1, keepdims=True)) * sg[:, csl]
        dcq_ref[:, 2 * RGW:] = dv_ref[...] * sg[:, 2 * RGW:]
        pc = pc_ref[...]
        lane = lax.broadcasted_iota(jnp.int32, pc.shape, 1)
        dbg = dbg_ref[...]
        beta = _sig(pc)
        ea = jnp.exp(alog_ref[...])
        z = pc + dtb_ref[...]
        g = -ea * _softplus(z)
        is_g = jnp.logical_and(lane >= 8, lane < 16)
        d_alpha = jnp.where(is_g, dbg * (-ea) * _sig(z), 0.0)
        dpc_ref[...] = jnp.where(lane < 8, dbg * beta * (1.0 - beta), d_alpha)
        dalog_ref[...] += _colsum(jnp.where(is_g, dbg * g, 0.0))
        ddtb_ref[...] += _colsum(d_alpha)

    return pl.pallas_call(
        body, name=name, grid=(s // t,),
        in_specs=[_rows(t, QKVW), _rows(t, BAP), _full((1, BAP)), _full((1, BAP))] + [_rows(t, HW)] * 3 + [_rows(t, BAP)],
        out_specs=[_rows(t, QKVW), _rows(t, BAP), _full((1, BAP)), _full((1, BAP))],
        out_shape=[_sds((s, QKVW)), _sds((s, BAP)), _sds((1, BAP)), _sds((1, BAP))],
        compiler_params=_cparams(1),
    )(c_qkv, p_ba, alog_row, dtb_row, dq, dk, dv, dbg)


def _mix_out_values(hf, hb, gate, of, ob, z, gn):
    hr = hf + hb
    y_rg = hr * _gelu(gate)
    osum = of + ob
    parts = []
    for h in range(NH):
        sl = slice(DH * h, DH * (h + 1))
        oh = osum[:, sl]
        r, ohat = _rms(oh)
        zh = z[:, sl]
        parts.append((r, ohat, zh))
    y_gdn = jnp.concatenate([ohat * gn * (zh * _sig(zh)) for (r, ohat, zh) in parts], axis=1)
    return hr, y_rg, y_gdn, parts


def _outproj(x1, hf, hb, gate, of, ob, z, gn, wout, name):
    s = x1.shape[0]
    t = min(256, s)

    def body(x_ref, hf_ref, hb_ref, gate_ref, of_ref, ob_ref, z_ref, gn_ref, w_ref, xo_ref, y_ref):
        _, y_rg, y_gdn, _ = _mix_out_values(hf_ref[...], hb_ref[...], gate_ref[...], of_ref[...], ob_ref[...],
                                            z_ref[...], gn_ref[...])
        y = jnp.concatenate([y_rg, y_gdn], axis=1).astype(BF16)
        y_ref[...] = y
        xo_ref[...] = x_ref[...] + jnp.dot(y, w_ref[...], preferred_element_type=F32)

    return pl.pallas_call(
        body, name=name, grid=(s // t,),
        in_specs=[_rows(t, D)] + [_rows(t, RGW)] * 6 + [_full((1, DH)), _full((D, D))],
        out_specs=[_rows(t, D), _rows(t, D)], out_shape=[_sds((s, D)), _sds((s, D), BF16)],
        compiler_params=_cparams(1),
    )(x1, hf, hb, gate, of, ob, z, gn, wout)


def _outproj_bwd(dx2, hf, hb, gate, of, ob, z, gn, wout, name):
    s = dx2.shape[0]
    t = min(256, s)

    def body(d_ref, hf_ref, hb_ref, gate_ref, of_ref, ob_ref, z_ref, gn_ref, w_ref,
             dhr_ref, dgate_ref, dos_ref, dz_ref, dgn_ref, db_ref):
        @pl.when(pl.program_id(0) == 0)
        def _():
            dgn_ref[...] = jnp.zeros_like(dgn_ref)

        gate = gate_ref[...]
        gn_v = gn_ref[...]
        hr, _, _, parts = _mix_out_values(hf_ref[...], hb_ref[...], gate, of_ref[...], ob_ref[...], z_ref[...], gn_v)
        dbf = d_ref[...].astype(BF16)
        db_ref[...] = dbf
        dy = _dot_nt(dbf, w_ref[...])
        dyr = dy[:, :RGW]
        dhr_ref[...] = dyr * _gelu(gate)
        dgate_ref[...] = dyr * hr * _gelu_grad(gate)
        dgn = jnp.zeros((1, DH), F32)
        for h, (r, ohat, zh) in enumerate(parts):
            sl = slice(DH * h, DH * (h + 1))
            dyh = dy[:, RGW + DH * h:RGW + DH * (h + 1)]
            sz = zh * _sig(zh)
            dn = dyh * sz
            dz_ref[:, sl] = dyh * ohat * gn_v * _silu_grad(zh)
            dgn = dgn + _colsum(dn * ohat)
            dos_ref[:, sl] = _rms_bwd(dn, ohat, r, gn_v)
        dgn_ref[...] += dgn

    return pl.pallas_call(
        body, name=name, grid=(s // t,),
        in_specs=[_rows(t, D)] + [_rows(t, RGW)] * 6 + [_full((1, DH)), _full((D, D))],
        out_specs=[_rows(t, RGW)] * 4 + [_full((1, DH)), _rows(t, D)],
        out_shape=[_sds((s, RGW))] * 4 + [_sds((1, DH)), _sds((s, D), BF16)],
        compiler_params=_cparams(1),
    )(dx2, hf, hb, gate, of, ob, z, gn, wout)


def _loss_head(x3, target, gain, name):
    s = x3.shape[0]
    t = min(256, s)

    def body(x_ref, t_ref, g_ref, dx_ref, loss_ref, dg_ref):
        @pl.when(pl.program_id(0) == 0)
        def _():
            loss_ref[...] = jnp.zeros_like(loss_ref)
            dg_ref[...] = jnp.zeros_like(dg_ref)

        r, xh = _rms(x_ref[...])
        gv = g_ref[...]
        err = xh * gv - t_ref[...]
        per_tok = jnp.mean(err * err, axis=-1, keepdims=True)
        loss_ref[...] += 0.5 * jnp.sum(per_tok, axis=0, keepdims=True)
        dy = err * (1.0 / D)
        dg_ref[...] += _colsum(dy * xh)
        dx_ref[...] = _rms_bwd(dy, xh, r, gv)

    return pl.pallas_call(
        body, name=name, grid=(s // t,), in_specs=[_rows(t, D), _rows(t, D), _full((1, D))],
        out_specs=[_rows(t, D), _full((8, 128)), _full((1, D))],
        out_shape=[_sds((s, D)), _sds((8, 128)), _sds((1, D))], compiler_params=_cparams(1),
    )(x3, target, gain)


def _adamw(w, g, m, v, name):
    r, c = w.shape
    tr = r
    while tr * c * 4 > (1 << 20) and tr % 16 == 0:
        tr //= 2

    def body(w_ref, g_ref, m_ref, v_ref, d_ref, nm_ref, nv_ref):
        gv = g_ref[...]
        mn = ADAM_B1 * m_ref[...] + (1.0 - ADAM_B1) * gv
        vn = ADAM_B2 * v_ref[...] + (1.0 - ADAM_B2) * (gv * gv)
        m_hat = mn / (1.0 - ADAM_B1 ** ADAM_STEP)
        v_hat = vn / (1.0 - ADAM_B2 ** ADAM_STEP)
        d_ref[...] = -ADAM_LR * (m_hat / (jnp.sqrt(v_hat) + ADAM_EPS) + ADAM_WD * w_ref[...])
        nm_ref[...] = mn
        nv_ref[...] = vn

    return pl.pallas_call(
        body, name=name, grid=(r // tr,), in_specs=[_rows(tr, c)] * 4, out_specs=[_rows(tr, c)] * 3,
        out_shape=[_sds((r, c))] * 3, compiler_params=_cparams(1),
    )(w, g, m, v)


def _mesh_pos():
    return lax.axis_index("x"), lax.axis_index("y"), lax.axis_index("c")


def _other_chips(x, y):
    return [(1 - x, y), (x, 1 - y), (1 - x, 1 - y)]


def _all_gather(n_arr, space, out_shapes, block_of, name):
    def body(*refs):
        x_refs, out_refs = refs[:n_arr], refs[n_arr:2 * n_arr]
        send_sems, recv_sems, local_sems = refs[2 * n_arr:]
        x, y, c = _mesh_pos()
        me, sibling = (x, y, c), (x, y, 1 - c)
        chips = _other_chips(x, y)

        def slot(a, px, py, pc):
            return out_refs[a].at[4 * px + 2 * py + pc]

        def copy(a, k, block, to, src=None):
            return pltpu.make_async_remote_copy(
                src_ref=slot(a, *block) if src is None else src, dst_ref=slot(a, *block),
                send_sem=send_sems.at[7 * a + k], recv_sem=recv_sems.at[7 * a + k], device_id=to, device_id_type=MESH)

        srcs = [block_of(a, x_refs[a], c) for a in range(n_arr)]
        local = [pltpu.make_async_copy(srcs[a], slot(a, *me), local_sems.at[a]) for a in range(n_arr)]
        for cp in local:
            cp.start()
        first = []
        for a in range(n_arr):
            first += [copy(a, 1 + j, me, (*chip, c), src=srcs[a]) for j, chip in enumerate(chips)]
            first.append(copy(a, 0, me, sibling, src=srcs[a]))
        for cp in first:
            cp.start()
        passed = []
        for j, chip in enumerate(chips):
            for a in range(n_arr):
                copy(a, 1 + j, (*chip, c), me).wait_recv()
                fwd = copy(a, 4 + j, (*chip, c), sibling)
                fwd.start()
                passed.append(fwd)
        for a in range(n_arr):
            copy(a, 0, sibling, me).wait_recv()
            for j, chip in enumerate(chips):
                copy(a, 4 + j, (*chip, 1 - c), me).wait_recv()
        for cp in first + passed:
            cp.wait_send()
        for cp in local:
            cp.wait()

    return pl.pallas_call(
        body, name=name, out_shape=out_shapes,
        in_specs=[pl.BlockSpec(memory_space=space)] * n_arr, out_specs=[pl.BlockSpec(memory_space=space)] * n_arr,
        scratch_shapes=[pltpu.SemaphoreType.DMA((7 * n_arr,)), pltpu.SemaphoreType.DMA((7 * n_arr,)),
                        pltpu.SemaphoreType.DMA((n_arr,))],
    )


def _gather_weights(shards):
    halves = [w.shape[0] // 2 for w in shards]

    def block_of(a, x_ref, c):
        return x_ref.at[pl.ds(pl.multiple_of(c * halves[a], 16), halves[a]), :]

    outs = _all_gather(len(shards), pltpu.HBM, [_sds((8, h, w.shape[1]), BF16) for h, w in zip(halves, shards)],
                       block_of, "gather_weights")(*shards)
    return [o.reshape(NSH, 2 * h, o.shape[2]) for o, h in zip(outs, halves)]


def _gather_small(block, name):
    r, c = block.shape
    return _all_gather(1, pltpu.VMEM, [_sds((8, r, c))], lambda a, x_ref, c_: x_ref, name)(block)[0]


def _sibling_exchange(gs):
    n = len(gs)
    halves = [g.shape[1] // 2 for g in gs]

    def body(*refs):
        g_refs, land_refs = refs[:n], refs[n:2 * n]
        send_sems, recv_sems = refs[2 * n:]
        x, y, c = _mesh_pos()
        copies = []
        for a in range(n):
            h = halves[a]
            for s in range(NSH):
                copies.append(pltpu.make_async_remote_copy(
                    src_ref=g_refs[a].at[s, pl.ds(pl.multiple_of((1 - c) * h, 8), h), :], dst_ref=land_refs[a].at[s],
                    send_sem=send_sems.at[NSH * a + s], recv_sem=recv_sems.at[NSH * a + s],
                    device_id=(x, y, 1 - c), device_id_type=MESH))
        for cp in copies:
            cp.start()
        for cp in copies:
            cp.wait()

    return pl.pallas_call(
        body, name="grad_sibling_exchange", out_shape=[_sds((NSH, h, g.shape[2])) for h, g in zip(halves, gs)],
        in_specs=[pl.BlockSpec(memory_space=pltpu.HBM)] * n, out_specs=[pl.BlockSpec(memory_space=pltpu.HBM)] * n,
        scratch_shapes=[pltpu.SemaphoreType.DMA((NSH * n,)), pltpu.SemaphoreType.DMA((NSH * n,))],
    )(*gs)


def _chip_sum(g, land, c_arr, name):
    _, h, cols = land.shape

    def body(c_ref, g_ref, l_ref, o_ref):
        o_ref[...] = (g_ref[...] + l_ref[...]).astype(BF16)

    return pl.pallas_call(
        body, name=name, out_shape=_sds((NSH, h, cols), BF16),
        grid_spec=pltpu.PrefetchScalarGridSpec(
            num_scalar_prefetch=1, grid=(NSH,),
            in_specs=[pl.BlockSpec((1, h, cols), lambda s, c_ref: (s, c_ref[0], 0)),
                      pl.BlockSpec((1, h, cols), lambda s, c_ref: (s, 0, 0))],
            out_specs=pl.BlockSpec((1, h, cols), lambda s, c_ref: (s, 0, 0))),
        compiler_params=_cparams(1),
    )(c_arr, g, land)


def _chip_scatter(parts):
    n = len(parts)

    def body(*refs):
        p_refs, land_refs = refs[:n], refs[n:2 * n]
        send_sems, recv_sems, local_sems = refs[2 * n:]
        x, y, c = _mesh_pos()
        my_chip = 2 * x + y
        local = [pltpu.make_async_copy(p_refs[a].at[my_chip], land_refs[a].at[my_chip], local_sems.at[a]) for a in range(n)]
        for cp in local:
            cp.start()
        copies = []
        for a in range(n):
            for j, (px, py) in enumerate(_other_chips(x, y)):
                copies.append(pltpu.make_async_remote_copy(
                    src_ref=p_refs[a].at[2 * px + py], dst_ref=land_refs[a].at[my_chip],
                    send_sem=send_sems.at[3 * a + j], recv_sem=recv_sems.at[3 * a + j],
                    device_id=(px, py, c), device_id_type=MESH))
        for cp in copies:
            cp.start()
        for cp in copies:
            cp.wait()
        for cp in local:
            cp.wait()

    return pl.pallas_call(
        body, name="grad_chip_scatter", out_shape=[_sds(p.shape, BF16) for p in parts],
        in_specs=[pl.BlockSpec(memory_space=pltpu.HBM)] * n, out_specs=[pl.BlockSpec(memory_space=pltpu.HBM)] * n,
        scratch_shapes=[pltpu.SemaphoreType.DMA((3 * n,)), pltpu.SemaphoreType.DMA((3 * n,)), pltpu.SemaphoreType.DMA((n,))],
    )(*parts)


def _sum_slots(land, name):
    k, r, c = land.shape
    tr = r // 2 if r % 32 == 0 else r

    def body(l_ref, o_ref):
        acc = l_ref[0].astype(F32)
        for i in range(1, k):
            acc = acc + l_ref[i].astype(F32)
        o_ref[...] = acc

    return pl.pallas_call(
        body, name=name, grid=(r // tr,), in_specs=[pl.BlockSpec((k, tr, c), lambda i: (0, i, 0))],
        out_specs=_rows(tr, c), out_shape=_sds((r, c)), compiler_params=_cparams(1),
    )(land)


def _sibling_swap(halves):
    n = len(halves)

    def body(*refs):
        h_refs, out_refs = refs[:n], refs[n:2 * n]
        send_sems, recv_sems, local_sems = refs[2 * n:]
        x, y, c = _mesh_pos()
        local = [pltpu.make_async_copy(h_refs[a], out_refs[a].at[c], local_sems.at[a]) for a in range(n)]
        for cp in local:
            cp.start()
        copies = [pltpu.make_async_remote_copy(
            src_ref=h_refs[a], dst_ref=out_refs[a].at[c], send_sem=send_sems.at[a], recv_sem=recv_sems.at[a],
            device_id=(x, y, 1 - c), device_id_type=MESH) for a in range(n)]
        for cp in copies:
            cp.start()
        for cp in copies:
            cp.wait()
        for cp in local:
            cp.wait()

    outs = pl.pallas_call(
        body, name="grad_sibling_swap", out_shape=[_sds((2,) + h.shape) for h in halves],
        in_specs=[pl.BlockSpec(memory_space=pltpu.HBM)] * n, out_specs=[pl.BlockSpec(memory_space=pltpu.HBM)] * n,
        scratch_shapes=[pltpu.SemaphoreType.DMA((n,)), pltpu.SemaphoreType.DMA((n,)), pltpu.SemaphoreType.DMA((n,))],
    )(*halves)
    return [o.reshape(2 * o.shape[1], o.shape[2]) for o in outs]


def _pad_rows(v, width):
    flat = v.reshape(-1)
    rows = -(-flat.shape[0] // width)
    rows = -(-rows // 8) * 8
    return jnp.pad(flat, (0, rows * width - flat.shape[0])).reshape(rows, width)


def _block_diag(w):
    eye = jnp.eye(8, dtype=w.dtype)
    return (w[:, :, None, :] * eye[:, None, :, None]).reshape(RGW, RGW)


def _diag_blocks(dense):
    r = dense.reshape(8, 64, 8, 64)
    return jnp.stack([r[n, :, n, :] for n in range(8)])


def _lane_row(v8):
    return jnp.zeros((1, BAP), F32).at[0, 8:16].set(v8.reshape(8))


def _local_step(x, target, wts):
    (g1, wg1, wu1, wd1, gmix, w_in_groups, wout, rg_cw8, rg_cb, wgates, gbias, lam_row, gdn_cw8,
     alog_row, dtb_row, gn, g2, wg2, wu2, wd2, gfin) = wts
    s = x.shape[0]

    x1, a1, b1 = _ffn_fwd(x, g1, wg1, wu1, wd1, "ffn1_fwd")
    h2, p_rgx, p_gate, p_qkv, p_z, p_ba = _inproj(x1, gmix, w_in_groups, "in_proj")
    c_rg = _conv(p_rgx, rg_cw8, rg_cb, "rg_conv")
    c_qkv = _conv(p_qkv, gdn_cw8, jnp.zeros((1, QKVW), F32), "gdn_conv")
    a0, bb0, a1s, bb1, q, k, v, bg = _mix_prep(c_rg, c_qkv, p_ba, wgates, gbias, lam_row, alog_row, dtb_row, "mix_prep")
    hf = _scan(a0, bb0, False, "rg_scan_f")
    hb = _scan(a1s, bb1, True, "rg_scan_b")
    tmat, gu, gw, gqd, gkd, gat, gcd = _gdn_local_fwd(q, k, v, bg, "gdn_local_fwd")
    of, s0, vn0, ob, s1, vn1 = _gdn_seq_fwd(gu, gw, gqd, gkd, gat, gcd, "gdn_seq_fwd")
    x2, ymix = _outproj(x1, hf, hb, p_gate, of, ob, p_z, gn, wout, "out_proj")
    x3, a2, b2 = _ffn_fwd(x2, g2, wg2, wu2, wd2, "ffn2_fwd")
    dx3, loss_blk, d_gfin = _loss_head(x3, target, gfin, "loss_head")

    dx2, d_g2, hb2, dob2, fb2, dab2, dbb2 = _ffn_bwd(x2, dx3, g2, a2, b2, wg2, wu2, wd2, "ffn2_bwd")
    d_wg2 = _tn(hb2, dab2, "ffn2_dwg")
    d_wu2 = _tn(hb2, dbb2, "ffn2_dwu")
    d_wd2 = _tn(fb2, dob2, "ffn2_dwd")

    d_hr, d_gate, d_os, d_z, d_gn, dx2b = _outproj_bwd(dx2, hf, hb, p_gate, of, ob, p_z, gn, wout, "out_proj_bwd")
    d_wout = _tn(ymix, dx2b, "dw_out")[0]

    a0_up = _shift_rows(a0, 1, "shift_a0")
    a1_dn = _shift_rows(a1s, -1, "shift_a1")
    hf_dn = _shift_rows(hf, -1, "shift_hf")
    hb_up = _shift_rows(hb, 1, "shift_hb")
    lam0 = _scan(a0_up, d_hr, True, "rg_scan_f_bwd")
    lam1 = _scan(a1_dn, d_hr, False, "rg_scan_b_bwd")
    d_xc, d_pre, xcb, d_gbias, d_lam = _gates_bwd(c_rg, wgates, gbias, lam_row, lam0, lam1, hf_dn, hb_up, "rg_gates_bwd")
    d_wgates = _tn(xcb, d_pre, "dw_gates")[0]
    d_prgx, d_rgcw8, d_rgcb = _conv_bwd(p_rgx, d_xc, rg_cw8, "rg_conv_bwd")

    sg = _gdn_seq_bwd(d_os, gw, gqd, gkd, gat, gcd, (s0, s1), (vn0, vn1), "gdn_seq_bwd")
    dq, dk, dv, dbg = _gdn_local_bwd(q, k, v, bg, tmat, d_os, (vn0, vn1), (sg[0:5], sg[5:10]), "gdn_local_bwd")
    d_cqkv, d_pba, d_alog, d_dtb = _prep_bwd(c_qkv, p_ba, alog_row, dtb_row, dq, dk, dv, dbg, "gdn_prep_bwd")
    d_pqkv, d_gdncw8, _ = _conv_bwd(p_qkv, d_cqkv, gdn_cw8, "gdn_conv_bwd")

    dps = (d_prgx, d_gate, d_pqkv, d_z, d_pba)
    dx1, d_gmix = _inproj_bwd(x1, dx2, gmix, dps, w_in_groups, "in_proj_bwd")
    d_win_groups = [_tn(h2, dp, "dw_in_%d" % i)[0] for i, dp in enumerate(dps)]

    gx, d_g1, hb1, dob1, fb1, dab1, dbb1 = _ffn_bwd(x, dx1, g1, a1, b1, wg1, wu1, wd1, "ffn1_bwd")
    d_wg1 = _tn(hb1, dab1, "ffn1_dwg")
    d_wu1 = _tn(hb1, dbb1, "ffn1_dwu")
    d_wd1 = _tn(fb1, dob1, "ffn1_dwd")

    d_win = jnp.concatenate(d_win_groups[:4] + [d_win_groups[4][:, :BAW]], axis=1)
    big = (d_wg1, d_wu1, d_wd1, d_win, d_wout, d_wg2, d_wu2, d_wd2)
    small = dict(
        ffn1_norm=d_g1, mix_norm=d_gmix, rg_conv_w=d_rgcw8[:4], rg_conv_b=d_rgcb,
        rg_gate_a_w=jnp.stack([_diag_blocks(d_wgates[:, RGW * i:RGW * (i + 1)]) for i in (0, 1)]),
        rg_gate_x_w=jnp.stack([_diag_blocks(d_wgates[:, RGW * i:RGW * (i + 1)]) for i in (2, 3)]),
        rg_gate_a_b=d_gbias[0, :2 * RGW].reshape(2, RGW), rg_gate_x_b=d_gbias[0, 2 * RGW:].reshape(2, RGW),
        rg_lambda=d_lam.reshape(2, RGW), gdn_conv_w=d_gdncw8[:4],
        gdn_a_log=d_alog[0, 8:16].reshape(2, NH), gdn_dt_bias=d_dtb[0, 8:16].reshape(2, NH),
        gdn_norm=d_gn, ffn2_norm=d_g2, final_norm=d_gfin)
    return loss_blk, gx, big, small


_SMALL_NAMES = ("ffn1_norm", "mix_norm", "rg_conv_w", "rg_conv_b", "rg_gate_a_w", "rg_gate_a_b", "rg_gate_x_w",
                "rg_gate_x_b", "rg_lambda", "gdn_conv_w", "gdn_a_log", "gdn_dt_bias", "gdn_norm", "ffn2_norm", "final_norm")
_SMALL_SHARDED = dict(rg_conv_w=128, rg_gate_a_b=128, rg_gate_x_b=128, rg_lambda=128, gdn_conv_w=384)
_OUT_ORDER = ("ffn1_norm", "ffn1_w_gate", "ffn1_w_up", "ffn1_w_down", "mix_norm", "w_in", "w_out", "rg_conv_w", "rg_conv_b",
              "rg_gate_a_w", "rg_gate_a_b", "rg_gate_x_w", "rg_gate_x_b", "rg_lambda", "gdn_conv_w", "gdn_a_log",
              "gdn_dt_bias", "gdn_norm", "ffn2_norm", "ffn2_w_gate", "ffn2_w_up", "ffn2_w_down", "final_norm")
_BIG_NAMES = ("ffn1_w_gate", "ffn1_w_up", "ffn1_w_down", "w_in", "w_out", "ffn2_w_gate", "ffn2_w_up", "ffn2_w_down")


def kernel(x, ffn1_norm, ffn1_w_gate, ffn1_w_up, ffn1_w_down, mix_norm, w_in, w_out, rg_conv_w, rg_conv_b, rg_gate_a_w, rg_gate_a_b, rg_gate_x_w, rg_gate_x_b, rg_lambda, gdn_conv_w, gdn_a_log, gdn_dt_bias, gdn_norm, ffn2_norm, ffn2_w_gate, ffn2_w_up, ffn2_w_down, final_norm, loss_target, m_ffn1_norm, m_ffn1_w_gate, m_ffn1_w_up, m_ffn1_w_down, m_mix_norm, m_w_in, m_w_out, m_rg_conv_w, m_rg_conv_b, m_rg_gate_a_w, m_rg_gate_a_b, m_rg_gate_x_w, m_rg_gate_x_b, m_rg_lambda, m_gdn_conv_w, m_gdn_a_log, m_gdn_dt_bias, m_gdn_norm, m_ffn2_norm, m_ffn2_w_gate, m_ffn2_w_up, m_ffn2_w_down, m_final_norm, v_ffn1_norm, v_ffn1_w_gate, v_ffn1_w_up, v_ffn1_w_down, v_mix_norm, v_w_in, v_w_out, v_rg_conv_w, v_rg_conv_b, v_rg_gate_a_w, v_rg_gate_a_b, v_rg_gate_x_w, v_rg_gate_x_b, v_rg_lambda, v_gdn_conv_w, v_gdn_a_log, v_gdn_dt_bias, v_gdn_norm, v_ffn2_norm, v_ffn2_w_gate, v_ffn2_w_up, v_ffn2_w_down, v_final_norm):
    args = dict(locals())
    w = {n: args[n] for n in _OUT_ORDER}
    mom = {n: args["m_" + n] for n in _OUT_ORDER}
    var = {n: args["v_" + n] for n in _OUT_ORDER}
    xi, yi, ci = _mesh_pos()
    shard = 2 * xi + yi

    wg1, wu1, wd1, win_sh, wout_sh, wg2, wu2, wd2 = _gather_weights([w[n][0].astype(BF16) for n in _BIG_NAMES])
    w_in_full = jnp.transpose(win_sh, (1, 0, 2)).reshape(D, NSH * INSH)
    w_out_full = wout_sh.reshape(D, D)
    sm_local = _pad_rows(jnp.concatenate([w[n][0].reshape(-1) for n in _SMALL_SHARDED]), 128)
    sm_all = _gather_small(sm_local, "gather_small_weights")[0::2].reshape(NSH, -1)
    sm_full, off = {}, 0
    for n, wd_ in _SMALL_SHARDED.items():
        rows = w[n].shape[1]
        piece = sm_all[:, off:off + rows * wd_].reshape(NSH, rows, wd_)
        sm_full[n] = jnp.transpose(piece, (1, 0, 2)).reshape(rows, NSH * wd_)
        off += rows * wd_

    w_in_groups = (w_in_full[:, 0:512], w_in_full[:, 512:1024], w_in_full[:, 1024:2560], w_in_full[:, 2560:3072],
                   jnp.pad(w_in_full[:, 3072:3088], ((0, 0), (0, BAP - BAW))))
    wa, wx = rg_gate_a_w[0], rg_gate_x_w[0]
    wgates = jnp.concatenate([_block_diag(wa[0]), _block_diag(wa[1]), _block_diag(wx[0]), _block_diag(wx[1])],
                             axis=1).astype(BF16)
    gbias = jnp.concatenate([sm_full["rg_gate_a_b"].reshape(1, -1), sm_full["rg_gate_x_b"].reshape(1, -1)], axis=1)
    wts = (ffn1_norm, wg1, wu1, wd1, mix_norm, w_in_groups, w_out_full,
           jnp.pad(sm_full["rg_conv_w"], ((0, 4), (0, 0))), rg_conv_b, wgates, gbias, sm_full["rg_lambda"].reshape(1, -1),
           jnp.pad(sm_full["gdn_conv_w"], ((0, 4), (0, 0))), _lane_row(gdn_a_log), _lane_row(gdn_dt_bias),
           gdn_norm, ffn2_norm, wg2, wu2, wd2, final_norm.reshape(1, D))

    loss_blk, gx, big, small = _local_step(x[0], loss_target[0], wts)
    loss = lax.psum(loss_blk[0, 0], ("x", "y", "c"))

    d_wg1, d_wu1, d_wd1, d_win, d_wout, d_wg2, d_wu2, d_wd2 = big
    gs = [d_wg1, d_wu1, d_wd1, jnp.transpose(d_win.reshape(D, NSH, INSH), (1, 0, 2)), d_wout.reshape(NSH, OUTSH, D),
          d_wg2, d_wu2, d_wd2]
    lands = _sibling_exchange(gs)
    c_arr = ci.reshape(1).astype(jnp.int32)
    parts = [_chip_sum(g, l, c_arr, "chip_sum_" + n) for g, l, n in zip(gs, lands, _BIG_NAMES)]
    halves = [_sum_slots(l, "sum_chips_" + n) for l, n in zip(_chip_scatter(parts), _BIG_NAMES)]
    grads = {n: g[None] for n, g in zip(_BIG_NAMES, _sibling_swap(halves))}

    sm_sizes = [(n, small[n].shape) for n in _SMALL_NAMES]
    sm_grad = _pad_rows(jnp.concatenate([small[n].reshape(-1) for n in _SMALL_NAMES]), D)
    sm_sum = _sum_slots(_gather_small(sm_grad, "gather_small_grads"), "small_grad_sum").reshape(-1)
    off = 0
    for n, shp in sm_sizes:
        cnt = 1
        for dsz in shp:
            cnt *= dsz
        g = sm_sum[off:off + cnt].reshape(shp)
        off += cnt
        if n in _SMALL_SHARDED:
            wd_ = _SMALL_SHARDED[n]
            g = lax.dynamic_slice_in_dim(g, shard * wd_, wd_, axis=1)
        grads[n] = g.reshape(w[n].shape)

    delta, new_m, new_v = {}, {}, {}
    for n in _BIG_NAMES:
        shp = w[n].shape
        d_, m_, v_ = _adamw(w[n][0], grads[n][0], mom[n][0], var[n][0], "adamw_" + n)
        delta[n], new_m[n], new_v[n] = d_.reshape(shp), m_.reshape(shp), v_.reshape(shp)
    packs = [_pad_rows(jnp.concatenate([t[n].reshape(-1) for n in _SMALL_NAMES]), D) for t in (w, grads, mom, var)]
    sm_out = _adamw(*packs, "adamw_small")
    off = 0
    for n in _SMALL_NAMES:
        cnt = w[n].size
        for dst, src in zip((delta, new_m, new_v), sm_out):
            dst[n] = src.reshape(-1)[off:off + cnt].reshape(w[n].shape)
        off += cnt

    outs = [loss, gx[None]]
    for group in (grads, delta, new_m, new_v):
        outs += [group[n] for n in _OUT_ORDER]
    return tuple(outs)
```

```python
import functools

import jax
import jax.numpy as jnp
from jax import lax
from jax.experimental import pallas as pl
from jax.experimental.pallas import tpu as pltpu

F32 = jnp.float32
BF16 = jnp.bfloat16
EPS = 1e-6
D = 1024
NSH = 4
FSH = 704
RGW = 512
QKVW = 1536
ZW = 512
BAW = 16
BAP = 128
INSH = 772
OUTSH = 256
CHUNK = 64
NH = 4
DH = 128
RG_C = 8.0
VMEM_LIMIT = 52 * 1024 * 1024
MESH = pl.DeviceIdType.MESH

ADAM_LR = 0.001
ADAM_B1 = 0.9
ADAM_B2 = 0.999
ADAM_EPS = 1e-08
ADAM_WD = 0.01
ADAM_STEP = 10


def _cparams(n_grid):
    return pltpu.CompilerParams(dimension_semantics=("arbitrary",) * n_grid, vmem_limit_bytes=VMEM_LIMIT)


def _sig(x):
    return 1.0 / (1.0 + jnp.exp(-x))


def _softplus(x):
    return jnp.maximum(x, 0.0) + jnp.log(1.0 + jnp.exp(-jnp.abs(x)))


def _neg_expm1(y):
    series = -y * (1.0 + y * (0.5 + y * (1.0 / 6 + y * (1.0 / 24 + y * (1.0 / 120 + y * (1.0 / 720 + y / 5040))))))
    return jnp.where(y > -0.3, series, 1.0 - jnp.exp(y))


_GELU_C = 0.7978845608028654


def _gelu(x):
    t = jnp.tanh(_GELU_C * (x + 0.044715 * x * x * x))
    return 0.5 * x * (1.0 + t)


def _gelu_grad(x):
    t = jnp.tanh(_GELU_C * (x + 0.044715 * x * x * x))
    return 0.5 * (1.0 + t) + 0.5 * x * (1.0 - t * t) * _GELU_C * (1.0 + 3 * 0.044715 * x * x)


def _silu_grad(x):
    s = _sig(x)
    return s * (1.0 + x * (1.0 - s))


def _dot(a, b):
    return jnp.dot(a.astype(BF16), b.astype(BF16), preferred_element_type=F32)


def _dot_nt(a, b):
    return lax.dot_general(a.astype(BF16), b.astype(BF16), (((1,), (1,)), ((), ())), preferred_element_type=F32)


def _dot_tn(a, b):
    return lax.dot_general(a.astype(BF16), b.astype(BF16), (((0,), (0,)), ((), ())), preferred_element_type=F32)


_NN = ((1,), (0,))
_NT = ((1,), (1,))
_TN = ((0,), (0,))


def _dg(a, b, dims):
    return lax.dot_general(a, b, (dims, ((), ())), preferred_element_type=F32)


def _split2(a):
    hi = a.astype(BF16)
    return hi, (a - hi.astype(F32)).astype(BF16)


def _dot3(a, b, dims=_NN):
    ah, al = _split2(a)
    bh, bl = _split2(b)
    return _dg(ah, bh, dims) + _dg(ah, bl, dims) + _dg(al, bh, dims)


def _dot_exact(e, x, dims, e_is_lhs):
    x0 = x.astype(BF16)
    r = x - x0.astype(F32)
    x1 = r.astype(BF16)
    x2 = (r - x1.astype(F32)).astype(BF16)
    eb = e.astype(BF16)
    if e_is_lhs:
        return _dg(eb, x0, dims) + _dg(eb, x1, dims) + _dg(eb, x2, dims)
    return _dg(x0, eb, dims) + _dg(x1, eb, dims) + _dg(x2, eb, dims)


def _rms(xv):
    r = lax.rsqrt(jnp.mean(xv * xv, axis=-1, keepdims=True) + EPS)
    return r, xv * r


def _rms_bwd(dy, xh, r, gain):
    dxh = dy * gain
    return r * (dxh - xh * jnp.mean(dxh * xh, axis=-1, keepdims=True))


def _colsum(v):
    return jnp.sum(v, axis=0, keepdims=True)


def _rows(t, c):
    return pl.BlockSpec((t, c), lambda i: (i, 0))


def _full(shape):
    n = len(shape)
    return pl.BlockSpec(shape, lambda i: (0,) * n)


def _sds(shape, dtype=F32):
    return jax.ShapeDtypeStruct(shape, dtype)


def _ffn_fwd(x, gain, wg, wu, wd, name):
    s = x.shape[0]
    tm = min(512, s)

    def body(x_ref, g_ref, wg_ref, wu_ref, wd_ref, xo_ref, a_ref, b_ref, h_sc, acc):
        j = pl.program_id(1)

        @pl.when(j == 0)
        def _():
            _, xh = _rms(x_ref[...])
            h_sc[...] = (xh * g_ref[...]).astype(BF16)
            acc[...] = jnp.zeros_like(acc)

        h = h_sc[...]
        a = jnp.dot(h, wg_ref[0], preferred_element_type=F32)
        b = jnp.dot(h, wu_ref[0], preferred_element_type=F32)
        a_ref[0] = a
        b_ref[0] = b
        f = (a * _sig(a) * b).astype(BF16)
        acc[...] += jnp.dot(f, wd_ref[0], preferred_element_type=F32)

        @pl.when(j == NSH - 1)
        def _():
            xo_ref[...] = x_ref[...] + 0.5 * acc[...]

    return pl.pallas_call(
        body, name=name, grid=(s // tm, NSH),
        in_specs=[pl.BlockSpec((tm, D), lambda i, j: (i, 0)), pl.BlockSpec((1, D), lambda i, j: (0, 0)),
                  pl.BlockSpec((1, D, FSH), lambda i, j: (j, 0, 0)), pl.BlockSpec((1, D, FSH), lambda i, j: (j, 0, 0)),
                  pl.BlockSpec((1, FSH, D), lambda i, j: (j, 0, 0))],
        out_specs=[pl.BlockSpec((tm, D), lambda i, j: (i, 0)), pl.BlockSpec((1, tm, FSH), lambda i, j: (j, i, 0)),
                   pl.BlockSpec((1, tm, FSH), lambda i, j: (j, i, 0))],
        out_shape=[_sds((s, D)), _sds((NSH, s, FSH)), _sds((NSH, s, FSH))],
        scratch_shapes=[pltpu.VMEM((tm, D), BF16), pltpu.VMEM((tm, D), F32)],
        compiler_params=_cparams(2),
    )(x, gain, wg, wu, wd)


def _ffn_bwd(x, dout, gain, a, b, wg, wu, wd, name):
    s = x.shape[0]
    tm = min(512, s)

    def body(x_ref, d_ref, g_ref, a_ref, b_ref, wg_ref, wu_ref, wd_ref,
             dx_ref, dg_ref, h_ref, do_ref, f_ref, da_ref, db_ref, do_sc, dh_acc):
        i = pl.program_id(0)
        j = pl.program_id(1)

        @pl.when(jnp.logical_and(i == 0, j == 0))
        def _():
            dg_ref[...] = jnp.zeros_like(dg_ref)

        @pl.when(j == 0)
        def _():
            _, xh = _rms(x_ref[...])
            h_ref[...] = (xh * g_ref[...]).astype(BF16)
            do = (0.5 * d_ref[...]).astype(BF16)
            do_sc[...] = do
            do_ref[...] = do
            dh_acc[...] = jnp.zeros_like(dh_acc)

        do = do_sc[...]
        df = _dot_nt(do, wd_ref[0])
        av = a_ref[0]
        bv = b_ref[0]
        sa = _sig(av)
        f_ref[0] = (av * sa * bv).astype(BF16)
        da = (df * bv * sa * (1.0 + av * (1.0 - sa))).astype(BF16)
        db = (df * av * sa).astype(BF16)
        da_ref[0] = da
        db_ref[0] = db
        dh_acc[...] += _dot_nt(da, wg_ref[0]) + _dot_nt(db, wu_ref[0])

        @pl.when(j == NSH - 1)
        def _():
            r, xh = _rms(x_ref[...])
            dh = dh_acc[...]
            dg_ref[...] += _colsum(dh * xh)
            dx_ref[...] = d_ref[...] + _rms_bwd(dh, xh, r, g_ref[...])

    tok = pl.BlockSpec((tm, D), lambda i, j: (i, 0))
    sh = pl.BlockSpec((1, tm, FSH), lambda i, j: (j, i, 0))
    return pl.pallas_call(
        body, name=name, grid=(s // tm, NSH),
        in_specs=[tok, tok, pl.BlockSpec((1, D), lambda i, j: (0, 0)), sh, sh,
                  pl.BlockSpec((1, D, FSH), lambda i, j: (j, 0, 0)), pl.BlockSpec((1, D, FSH), lambda i, j: (j, 0, 0)),
                  pl.BlockSpec((1, FSH, D), lambda i, j: (j, 0, 0))],
        out_specs=[tok, pl.BlockSpec((1, D), lambda i, j: (0, 0)), tok, tok, sh, sh, sh],
        out_shape=[_sds((s, D)), _sds((1, D)), _sds((s, D), BF16), _sds((s, D), BF16),
                   _sds((NSH, s, FSH), BF16), _sds((NSH, s, FSH), BF16), _sds((NSH, s, FSH), BF16)],
        scratch_shapes=[pltpu.VMEM((tm, D), BF16), pltpu.VMEM((tm, D), F32)],
        compiler_params=_cparams(2),
    )(x, dout, gain, a, b, wg, wu, wd)


def _tn(a, b, name):
    a_g = a.ndim == 3
    b_g = b.ndim == 3
    g = a.shape[0] if a_g else (b.shape[0] if b_g else 1)
    s, k = a.shape[-2:]
    n = b.shape[-1]
    ts = min(512, s)

    def body(a_ref, b_ref, o_ref):
        @pl.when(pl.program_id(1) == 0)
        def _():
            o_ref[...] = jnp.zeros_like(o_ref)

        av = a_ref[0] if a_g else a_ref[...]
        bv = b_ref[0] if b_g else b_ref[...]
        o_ref[0] += _dot_tn(av, bv)

    a_spec = pl.BlockSpec((1, ts, k), lambda gi, si: (gi, si, 0)) if a_g else pl.BlockSpec((ts, k), lambda gi, si: (si, 0))
    b_spec = pl.BlockSpec((1, ts, n), lambda gi, si: (gi, si, 0)) if b_g else pl.BlockSpec((ts, n), lambda gi, si: (si, 0))
    return pl.pallas_call(
        body, name=name, grid=(g, s // ts), in_specs=[a_spec, b_spec],
        out_specs=pl.BlockSpec((1, k, n), lambda gi, si: (gi, 0, 0)),
        out_shape=_sds((g, k, n)), compiler_params=_cparams(2),
    )(a, b)


_P_WIDTHS = (RGW, RGW, QKVW, ZW, BAP)


def _inproj(x1, gain, ws, name):
    s = x1.shape[0]
    tm = min(256, s)

    def body(x_ref, g_ref, *refs):
        w_refs = refs[:5]
        h_ref = refs[5]
        p_refs = refs[6:]
        _, xh = _rms(x_ref[...])
        h = (xh * g_ref[...]).astype(BF16)
        h_ref[...] = h
        for w_ref, p_ref in zip(w_refs, p_refs):
            p_ref[...] = jnp.dot(h, w_ref[...], preferred_element_type=F32)

    return pl.pallas_call(
        body, name=name, grid=(s // tm,),
        in_specs=[_rows(tm, D), _full((1, D))] + [_full((D, w)) for w in _P_WIDTHS],
        out_specs=[_rows(tm, D)] + [_rows(tm, w) for w in _P_WIDTHS],
        out_shape=[_sds((s, D), BF16)] + [_sds((s, w)) for w in _P_WIDTHS],
        compiler_params=_cparams(1),
    )(x1, gain, *ws)


def _inproj_bwd(x1, dx2, gain, dps, ws, name):
    s = x1.shape[0]
    tm = min(256, s)

    def body(x_ref, d_ref, g_ref, *refs):
        dp_refs = refs[:5]
        w_refs = refs[5:10]
        dx_ref, dg_ref = refs[10:]

        @pl.when(pl.program_id(0) == 0)
        def _():
            dg_ref[...] = jnp.zeros_like(dg_ref)

        dh = jnp.zeros((tm, D), F32)
        for dp_ref, w_ref in zip(dp_refs, w_refs):
            dh = dh + _dot_nt(dp_ref[...], w_ref[...])
        r, xh = _rms(x_ref[...])
        dg_ref[...] += _colsum(dh * xh)
        dx_ref[...] = d_ref[...] + _rms_bwd(dh, xh, r, g_ref[...])

    return pl.pallas_call(
        body, name=name, grid=(s // tm,),
        in_specs=[_rows(tm, D), _rows(tm, D), _full((1, D))] + [_rows(tm, w) for w in _P_WIDTHS]
        + [_full((D, w)) for w in _P_WIDTHS],
        out_specs=[_rows(tm, D), _full((1, D))],
        out_shape=[_sds((s, D)), _sds((1, D))],
        compiler_params=_cparams(1),
    )(x1, dx2, gain, *dps, *ws)


def _halo_specs(s, t, c):
    nb8 = s // 8
    tb = t // 8
    prev = pl.BlockSpec((8, c), lambda i: (jnp.maximum(i * tb - 1, 0), 0))
    nxt = pl.BlockSpec((8, c), lambda i: (jnp.minimum((i + 1) * tb, nb8 - 1), 0))
    return prev, nxt


def _edge_masks(nb):
    i = pl.program_id(0)
    return jnp.where(i > 0, 1.0, 0.0).astype(F32), jnp.where(i < nb - 1, 1.0, 0.0).astype(F32)


def _shifted(xx, off, t):
    n = t + 16
    sh = (-off) % n
    rolled = xx if sh == 0 else pltpu.roll(xx, sh, 0)
    return rolled[8:8 + t]


def _conv(x, w8, bias, name):
    s, c = x.shape
    t = min(256, s)
    nb = s // t

    def body(x_ref, xp_ref, xn_ref, w_ref, b_ref, o_ref):
        pm, nm = _edge_masks(nb)
        for c0 in range(0, c, 512):
            cols = slice(c0, c0 + 512)
            xx = jnp.concatenate([xp_ref[:, cols] * pm, x_ref[:, cols], xn_ref[:, cols] * nm], axis=0)
            acc = jnp.zeros((t, 512), F32) + b_ref[:, cols]
            for j in range(4):
                acc = acc + w_ref[j:j + 1, cols] * _shifted(xx, j - 2, t)
            o_ref[:, cols] = acc

    prev, nxt = _halo_specs(s, t, c)
    return pl.pallas_call(
        body, name=name, grid=(nb,),
        in_specs=[_rows(t, c), prev, nxt, _full((8, c)), _full((1, c))],
        out_specs=_rows(t, c), out_shape=_sds((s, c)), compiler_params=_cparams(1),
    )(x, x, x, w8, bias)


def _conv_bwd(x, dc, w8, name):
    s, c = x.shape
    t = min(256, s)
    nb = s // t

    def body(x_ref, d_ref, dp_ref, dn_ref, w_ref, dx_ref, dw_ref, db_ref):
        @pl.when(pl.program_id(0) == 0)
        def _():
            dw_ref[...] = jnp.zeros_like(dw_ref)
            db_ref[...] = jnp.zeros_like(db_ref)

        pm, nm = _edge_masks(nb)
        for c0 in range(0, c, 512):
            cols = slice(c0, c0 + 512)
            dd = jnp.concatenate([dp_ref[:, cols] * pm, d_ref[:, cols], dn_ref[:, cols] * nm], axis=0)
            xv = x_ref[:, cols]
            acc = jnp.zeros((t, 512), F32)
            for j in range(4):
                dsh = _shifted(dd, 2 - j, t)
                acc = acc + w_ref[j:j + 1, cols] * dsh
                dw_ref[j:j + 1, cols] += _colsum(dsh * xv)
            dx_ref[:, cols] = acc
            db_ref[:, cols] += _colsum(d_ref[:, cols])

    prev, nxt = _halo_specs(s, t, c)
    return pl.pallas_call(
        body, name=name, grid=(nb,),
        in_specs=[_rows(t, c), _rows(t, c), prev, nxt, _full((8, c))],
        out_specs=[_rows(t, c), _full((8, c)), _full((1, c))],
        out_shape=[_sds((s, c)), _sds((8, c)), _sds((1, c))], compiler_params=_cparams(1),
    )(x, dc, dc, dc, w8)


def _shift_rows(x, direction, name):
    s, c = x.shape
    t = min(256, s)
    nb = s // t

    def body(x_ref, xp_ref, xn_ref, o_ref):
        pm, nm = _edge_masks(nb)
        xx = jnp.concatenate([xp_ref[...] * pm, x_ref[...], xn_ref[...] * nm], axis=0)
        o_ref[...] = _shifted(xx, direction, t)

    prev, nxt = _halo_specs(s, t, c)
    return pl.pallas_call(
        body, name=name, grid=(nb,), in_specs=[_rows(t, c), prev, nxt],
        out_specs=_rows(t, c), out_shape=_sds((s, c)), compiler_params=_cparams(1),
    )(x, x, x)


def _rg_gates(xc, pre, lam_row):
    sp8 = RG_C * _softplus(-lam_row)
    out = []
    for d in range(2):
        r = _sig(pre[:, RGW * d:RGW * (d + 1)])
        gi = _sig(pre[:, 2 * RGW + RGW * d:2 * RGW + RGW * (d + 1)])
        la = -r * sp8[:, RGW * d:RGW * (d + 1)]
        a = jnp.exp(la)
        mult = jnp.sqrt(_neg_expm1(2.0 * la))
        out.append((r, gi, a, mult))
    return out


def _mix_prep(c_rg, c_qkv, p_ba, wgates, gbias, lam_row, alog_row, dtb_row, name):
    s = c_rg.shape[0]
    t = min(256, s)

    def body(xc_ref, cq_ref, pc_ref, wg_ref, gb_ref, lam_ref, alog_ref, dtb_ref,
             a0_ref, b0_ref, a1_ref, b1_ref, q_ref, k_ref, v_ref, bg_ref):
        xc = xc_ref[...]
        pre = _dot(xc, wg_ref[...]) + gb_ref[...]
        gates = _rg_gates(xc, pre, lam_ref[...])
        for (r, gi, a, mult), a_ref, b_ref in zip(gates, (a0_ref, a1_ref), (b0_ref, b1_ref)):
            a_ref[...] = a
            b_ref[...] = mult * gi * xc
        cq = cq_ref[...]
        sq = cq * _sig(cq)
        for h in range(NH):
            sl = slice(DH * h, DH * (h + 1))
            qh = sq[:, sl]
            q_ref[:, sl] = qh * lax.rsqrt(jnp.sum(qh * qh, axis=-1, keepdims=True) + EPS) * (DH ** -0.5)
            kh = sq[:, RGW + DH * h:RGW + DH * (h + 1)]
            k_ref[:, sl] = kh * lax.rsqrt(jnp.sum(kh * kh, axis=-1, keepdims=True) + EPS)
        v_ref[...] = sq[:, 2 * RGW:]
        pc = pc_ref[...]
        lane = lax.broadcasted_iota(jnp.int32, pc.shape, 1)
        beta = _sig(pc)
        g = -jnp.exp(alog_ref[...]) * _softplus(pc + dtb_ref[...])
        bg_ref[...] = jnp.where(lane < 8, beta, jnp.where(lane < 16, g, 0.0))

    return pl.pallas_call(
        body, name=name, grid=(s // t,),
        in_specs=[_rows(t, RGW), _rows(t, QKVW), _rows(t, BAP), _full((RGW, 4 * RGW)), _full((1, 4 * RGW)),
                  _full((1, 2 * RGW)), _full((1, BAP)), _full((1, BAP))],
        out_specs=[_rows(t, RGW)] * 7 + [_rows(t, BAP)],
        out_shape=[_sds((s, RGW))] * 7 + [_sds((s, BAP))],
        compiler_params=_cparams(1),
    )(c_rg, c_qkv, p_ba, wgates, gbias, lam_row, alog_row, dtb_row)


def _scan(a, b, reverse, name):
    s, c = a.shape
    t = min(512, s)
    nb = s // t
    ng = t // 8
    idx = (lambda i: (nb - 1 - i, 0)) if reverse else (lambda i: (i, 0))

    def body(a_ref, b_ref, h_ref, carry):
        @pl.when(pl.program_id(0) == 0)
        def _():
            carry[...] = jnp.zeros_like(carry)

        row = lax.broadcasted_iota(jnp.int32, (8, c), 0)

        def group(gi, cv):
            g = (ng - 1 - gi) if reverse else gi
            r0 = pl.multiple_of(g * 8, 8)
            av = a_ref[pl.ds(r0, 8), :]
            bv = b_ref[pl.ds(r0, 8), :]
            for k in (1, 2, 4):
                sh = (8 - k) if reverse else k
                m = (row < 8 - k) if reverse else (row >= k)
                a_s = pltpu.roll(av, sh, 0)
                b_s = pltpu.roll(bv, sh, 0)
                bv = jnp.where(m, av * b_s + bv, bv)
                av = jnp.where(m, av * a_s, av)
            hv = av * cv + bv
            h_ref[pl.ds(r0, 8), :] = hv
            return hv[0:1, :] if reverse else hv[7:8, :]

        carry[0:1, :] = lax.fori_loop(0, ng, group, carry[0:1, :])

    return pl.pallas_call(
        body, name=name, grid=(nb,), in_specs=[pl.BlockSpec((t, c), idx), pl.BlockSpec((t, c), idx)],
        out_specs=pl.BlockSpec((t, c), idx), out_shape=_sds((s, c)),
        scratch_shapes=[pltpu.VMEM((8, c), F32)], compiler_params=_cparams(1),
    )(a, b)


def _gates_bwd(xc, wgates, gbias, lam_row, lam0, lam1, h0s, h1s, name):
    s = xc.shape[0]
    t = min(256, s)

    def body(xc_ref, wg_ref, gb_ref, lam_ref, l0_ref, l1_ref, h0_ref, h1_ref,
             dxc_ref, dpre_ref, xcb_ref, dgb_ref, dlam_ref):
        @pl.when(pl.program_id(0) == 0)
        def _():
            dgb_ref[...] = jnp.zeros_like(dgb_ref)
            dlam_ref[...] = jnp.zeros_like(dlam_ref)

        xv = xc_ref[...]
        pre = _dot(xv, wg_ref[...]) + gb_ref[...]
        lam_row_v = lam_ref[...]
        sp8 = RG_C * _softplus(-lam_row_v)
        dsp_dlam = -RG_C * _sig(-lam_row_v)
        gates = _rg_gates(xv, pre, lam_row_v)
        dxc = jnp.zeros((t, RGW), F32)
        dpre_r = []
        dpre_i = []
        for d, ((r, gi, a, mult), l_ref, h_ref) in enumerate(zip(gates, (l0_ref, l1_ref), (h0_ref, h1_ref))):
            dbb = l_ref[...]
            da = dbb * h_ref[...]
            cs = slice(RGW * d, RGW * (d + 1))
            dmult = dbb * gi * xv
            dgi = dbb * mult * xv
            dxc = dxc + dbb * mult * gi
            dla = da * a - dmult * a * a / mult
            dr = -dla * sp8[:, cs]
            dlam_ref[:, cs] += _colsum(-dla * r) * dsp_dlam[:, cs]
            dpre_r.append(dr * r * (1.0 - r))
            dpre_i.append(dgi * gi * (1.0 - gi))
        dpre = jnp.concatenate(dpre_r + dpre_i, axis=1)
        dgb_ref[...] += _colsum(dpre)
        dpre_b = dpre.astype(BF16)
        dpre_ref[...] = dpre_b
        xcb_ref[...] = xv.astype(BF16)
        dxc_ref[...] = dxc + _dot_nt(dpre_b, wg_ref[...])

    return pl.pallas_call(
        body, name=name, grid=(s // t,),
        in_specs=[_rows(t, RGW), _full((RGW, 4 * RGW)), _full((1, 4 * RGW)), _full((1, 2 * RGW))] + [_rows(t, RGW)] * 4,
        out_specs=[_rows(t, RGW), _rows(t, 4 * RGW), _rows(t, RGW), _full((1, 4 * RGW)), _full((1, 2 * RGW))],
        out_shape=[_sds((s, RGW)), _sds((s, 4 * RGW), BF16), _sds((s, RGW), BF16), _sds((1, 4 * RGW)), _sds((1, 2 * RGW))],
        compiler_params=_cparams(1),
    )(xc, wgates, gbias, lam_row, lam0, lam1, h0s, h1s)


class _GdnMasks:
    def __init__(self, d):
        ri = lax.broadcasted_iota(jnp.int32, (CHUNK, CHUNK), 0)
        ci = lax.broadcasted_iota(jnp.int32, (CHUNK, CHUNK), 1)
        self.incl = (ri >= ci) if d == 0 else (ri <= ci)
        self.strict = (ri > ci) if d == 0 else (ri < ci)
        b16 = jnp.right_shift(ri, 4) == jnp.right_shift(ci, 4)
        b32 = jnp.right_shift(ri, 5) == jnp.right_shift(ci, 5)
        self.diag16 = b16
        self.off32 = jnp.logical_and(b32, jnp.logical_not(b16))
        self.off64 = jnp.logical_not(b32)
        self.eye = jnp.where(ri == ci, 1.0, 0.0).astype(F32)
        self.tri = jnp.where(self.incl, 1.0, 0.0).astype(F32)
        self.last = CHUNK - 1 if d == 0 else 0


def _tri_inv(lmat, m):
    return _tri_inv_many([lmat], [m])[0]


def _tri_inv_many(lmats, masks):
    n = len(lmats)
    ns = [jnp.where(masks[i].diag16, lmats[i], 0.0) for i in range(n)]
    ps = [masks[i].eye - ns[i] for i in range(n)]
    qs = [_dot3(ns[i], ns[i]) for i in range(n)]
    for step in range(3):
        ps = [_dot3(ps[i], masks[i].eye + qs[i]) for i in range(n)]
        if step < 2:
            qs = [_dot3(qs[i], qs[i]) for i in range(n)]
    for off in ("off32", "off64"):
        ts = [_dot3(ps[i], jnp.where(getattr(masks[i], off), lmats[i], 0.0)) for i in range(n)]
        ps = [ps[i] - _dot3(ts[i], ps[i]) for i in range(n)]
    return ps


def _chunk_cumsums(m, bgv):
    return _dot_exact(m.tri, bgv, _NN, True), _dot_exact(m.tri, bgv, ((0,), (1,)), False)


class _GdnHead:
    def __init__(self, qh, kh, vh, kk, q0, bg, gcs, gcs_t, d, h, m):
        cb = 4 * d + h
        cg = 8 + 4 * d + h
        self.q, self.k, self.v = qh, kh, vh
        self.beta = bg[:, cb:cb + 1]
        gcol = gcs[:, cg:cg + 1]
        grow = gcs_t[cg:cg + 1, :]
        gl = gcs[m.last:m.last + 1, cg:cg + 1]
        self.decay = jnp.exp(jnp.where(m.incl, gcol - grow, -1e30))
        self.kb = kh * self.beta
        self.vb = vh * self.beta
        self.a0 = kk * self.beta
        self.q0 = q0
        self.lmat = jnp.where(m.strict, self.a0 * self.decay, 0.0)
        self.attn = self.q0 * self.decay
        self.eg = jnp.exp(gcol)
        self.ek = jnp.exp(gl - gcol)
        self.cd = jnp.exp(gl)
        self.kg = self.kb * self.eg
        self.qd = qh * self.eg
        self.kd = kh * self.ek


HW = NH * DH
SEQ_CB = 4


def _head(h):
    return slice(DH * h, DH * (h + 1))


def _gdn_local_fwd(q, k, v, bg, name):
    s = q.shape[0]
    n = s // CHUNK

    def body(q_ref, k_ref, v_ref, bg_ref, t_ref, u_ref, w_ref, qd_ref, kd_ref, at_ref, cd_ref):
        bgv = bg_ref[...]
        qs = [q_ref[:, _head(h)] for h in range(NH)]
        ks = [k_ref[:, _head(h)] for h in range(NH)]
        kk = [_dot_nt(ks[h], ks[h]) for h in range(NH)]
        q0 = [_dot_nt(qs[h], ks[h]) for h in range(NH)]
        inst = []
        for d in range(2):
            m = _GdnMasks(d)
            gcs, gcs_t = _chunk_cumsums(m, bgv)
            for h in range(NH):
                c = _GdnHead(qs[h], ks[h], v_ref[:, _head(h)], kk[h], q0[h], bgv, gcs, gcs_t, d, h, m)
                inst.append((d, h, m, c))
        tms = _tri_inv_many([c.lmat for _, _, _, c in inst], [m for _, _, m, _ in inst])
        for (d, h, m, c), tm in zip(inst, tms):
            sl = _head(h)
            t_ref[0, d, h] = tm
            u_ref[d, :, sl] = _dot(tm, c.vb)
            w_ref[d, :, sl] = _dot(tm, c.kg).astype(BF16)
            qd_ref[d, :, sl] = c.qd.astype(BF16)
            kd_ref[d, :, sl] = c.kd.astype(BF16)
            at_ref[0, d, h] = c.attn.astype(BF16)
            cd_ref[0, 4 * d + h:4 * d + h + 1, :] = jnp.broadcast_to(c.cd, (1, DH))

    tok = _rows(CHUNK, HW)
    tok2 = pl.BlockSpec((2, CHUNK, HW), lambda i: (0, i, 0))
    mat = pl.BlockSpec((1, 2, NH, CHUNK, CHUNK), lambda i: (i, 0, 0, 0, 0))
    return pl.pallas_call(
        body, name=name, grid=(n,), in_specs=[tok, tok, tok, _rows(CHUNK, BAP)],
        out_specs=[mat, tok2, tok2, tok2, tok2, mat, pl.BlockSpec((1, 8, DH), lambda i: (i, 0, 0))],
        out_shape=[_sds((n, 2, NH, CHUNK, CHUNK)), _sds((2, s, HW)), _sds((2, s, HW), BF16), _sds((2, s, HW), BF16),
                   _sds((2, s, HW), BF16), _sds((n, 2, NH, CHUNK, CHUNK), BF16), _sds((n, 8, DH))],
        compiler_params=_cparams(1),
    )(q, k, v, bg)


def _seq_specs(s, order):
    n = s // CHUNK
    cb = min(SEQ_CB, n)
    nb = n // cb
    tb = cb * CHUNK

    def blk(d):
        return (lambda i: i) if order[d] else (lambda i: nb - 1 - i)

    def per_dir(make):
        return [make(d, blk(d)) for d in range(2)]

    tok2 = per_dir(lambda d, f: pl.BlockSpec((1, tb, HW), lambda i: (d, f(i), 0)))
    tok = per_dir(lambda d, f: pl.BlockSpec((tb, HW), lambda i: (f(i), 0)))
    mat = per_dir(lambda d, f: pl.BlockSpec((cb, 1, NH, CHUNK, CHUNK), lambda i: (f(i), d, 0, 0, 0)))
    cds = per_dir(lambda d, f: pl.BlockSpec((cb, 8, DH), lambda i: (f(i), 0, 0)))
    sts = per_dir(lambda d, f: pl.BlockSpec((cb, NH, DH, DH), lambda i: (f(i), 0, 0, 0)))
    dcd = per_dir(lambda d, f: pl.BlockSpec((cb, NH, DH), lambda i: (f(i), 0, 0)))
    return n, cb, nb, tok2, tok, mat, cds, sts, dcd


def _gdn_seq_fwd(u, w, qd, kd, at, cd, name):
    s = u.shape[1]
    n, cb, nb, tok2, tok, mat, cds, sts, _ = _seq_specs(s, (True, False))

    def body(*refs):
        ins = (refs[0:6], refs[6:12])
        outs = (refs[12:15], refs[15:18])
        st = refs[18]

        @pl.when(pl.program_id(0) == 0)
        def _():
            st[...] = jnp.zeros_like(st)

        for j in range(cb):
            items = []
            for d in range(2):
                jj = j if d == 0 else cb - 1 - j
                items += [(d, h, jj, slice(CHUNK * jj, CHUNK * (jj + 1)), _head(h)) for h in range(NH)]
            shs = [st[d, h] for d, h, _, _, _ in items]
            wss = [_dot(ins[d][1][0, rows, sl], sh) for (d, h, jj, rows, sl), sh in zip(items, shs)]
            vns = [ins[d][0][0, rows, sl] - ws for (d, h, jj, rows, sl), ws in zip(items, wss)]
            news = [sh * ins[d][5][jj, 4 * d + h:4 * d + h + 1, :] + _dot_tn(ins[d][3][0, rows, sl], vn)
                    for (d, h, jj, rows, sl), sh, vn in zip(items, shs, vns)]
            for (d, h, jj, rows, sl), sh, vn, new in zip(items, shs, vns, news):
                o_r, s_r, vn_r = outs[d]
                st[d, h] = new
                s_r[jj, h] = sh
                vn_r[rows, sl] = vn
                o_r[rows, sl] = _dot(ins[d][2][0, rows, sl], sh) + _dot(ins[d][4][jj, 0, h], vn)

    in_specs, out_specs, out_shape = [], [], []
    for d in range(2):
        in_specs += [tok2[d]] * 4 + [mat[d], cds[d]]
        out_specs += [tok[d], sts[d], tok[d]]
        out_shape += [_sds((s, HW)), _sds((n, NH, DH, DH)), _sds((s, HW))]
    return pl.pallas_call(
        body, name=name, grid=(nb,), in_specs=in_specs, out_specs=out_specs, out_shape=out_shape,
        scratch_shapes=[pltpu.VMEM((2, NH, DH, DH), F32)], compiler_params=_cparams(1),
    )(u, w, qd, kd, at, cd, u, w, qd, kd, at, cd)


def _gdn_seq_bwd(do, w, qd, kd, at, cd, states, vns, name):
    s = do.shape[0]
    n, cb, nb, tok2, tok, mat, cds, sts, dcd = _seq_specs(s, (False, True))

    def body(*refs):
        ins = (refs[0:8], refs[8:16])
        outs = (refs[16:21], refs[21:26])
        dst = refs[26]

        @pl.when(pl.program_id(0) == 0)
        def _():
            dst[...] = jnp.zeros_like(dst)

        for j in range(cb):
            items = []
            for d in range(2):
                jj = cb - 1 - j if d == 0 else j
                items += [(d, h, jj, slice(CHUNK * jj, CHUNK * (jj + 1)), _head(h)) for h in range(NH)]
            dsns = [dst[d, h] for d, h, _, _, _ in items]
            dohs = [ins[d][0][rows, sl] for d, h, jj, rows, sl in items]
            d_vns = [_dot_tn(ins[d][4][jj, 0, h], doh) + _dot(ins[d][3][0, rows, sl], dsn)
                     for (d, h, jj, rows, sl), doh, dsn in zip(items, dohs, dsns)]
            news = [ins[d][5][jj, 4 * d + h:4 * d + h + 1, :] * dsn + _dot_tn(ins[d][2][0, rows, sl], doh)
                    - _dot_tn(ins[d][1][0, rows, sl], d_vn)
                    for (d, h, jj, rows, sl), doh, dsn, d_vn in zip(items, dohs, dsns, d_vns)]
            for (d, h, jj, rows, sl), doh, dsn, d_vn, new in zip(items, dohs, dsns, d_vns, news):
                dvn_r, dkd_r, dqd_r, dw_r, dcd_r = outs[d]
                sh = ins[d][6][jj, h]
                dst[d, h] = new
                dvn_r[rows, sl] = d_vn
                dkd_r[rows, sl] = _dot_nt(ins[d][7][rows, sl], dsn)
                dqd_r[rows, sl] = _dot_nt(doh, sh)
                dw_r[rows, sl] = -_dot_nt(d_vn, sh)
                d_cd = jnp.sum(jnp.sum(sh * dsn, axis=1, keepdims=True), axis=0, keepdims=True)
                dcd_r[jj, h:h + 1, :] = jnp.broadcast_to(d_cd, (1, DH))

    in_specs, out_specs, out_shape, args = [], [], [], []
    for d in range(2):
        in_specs += [tok[d]] + [tok2[d]] * 3 + [mat[d], cds[d], sts[d], tok[d]]
        args += [do, w, qd, kd, at, cd, states[d], vns[d]]
        out_specs += [tok[d]] * 4 + [dcd[d]]
        out_shape += [_sds((s, HW))] * 4 + [_sds((n, NH, DH))]
    return pl.pallas_call(
        body, name=name, grid=(nb,), in_specs=in_specs, out_specs=out_specs, out_shape=out_shape,
        scratch_shapes=[pltpu.VMEM((2, NH, DH, DH), F32)], compiler_params=_cparams(1),
    )(*args)


def _gdn_local_bwd(q, k, v, bg, tmat, do, vns, seq_grads, name):
    s = q.shape[0]
    n = s // CHUNK

    def body(*refs):
        q_ref, k_ref, v_ref, bg_ref, t_ref, do_ref = refs[0:6]
        vn_refs = refs[6:8]
        sg = (refs[8:13], refs[13:18])
        dq_ref, dk_ref, dv_ref, dbg_ref = refs[18:]
        bgv = bg_ref[...]
        qs = [q_ref[:, _head(h)] for h in range(NH)]
        ks = [k_ref[:, _head(h)] for h in range(NH)]
        kk = [_dot_nt(ks[h], ks[h]) for h in range(NH)]
        q0 = [_dot_nt(qs[h], ks[h]) for h in range(NH)]
        lane = lax.broadcasted_iota(jnp.int32, (CHUNK, BAP), 1)
        rowi = lax.broadcasted_iota(jnp.int32, (CHUNK, 1), 0)
        ones = jnp.ones((CHUNK, DH), F32)
        dbg = jnp.zeros((CHUNK, BAP), F32)
        acc = [[None, None, None] for _ in range(NH)]
        inst = []
        for d in range(2):
            m = _GdnMasks(d)
            gcs, gcs_t = _chunk_cumsums(m, bgv)
            for h in range(NH):
                c = _GdnHead(qs[h], ks[h], v_ref[:, _head(h)], kk[h], q0[h], bgv, gcs, gcs_t, d, h, m)
                inst.append((d, h, m, c))
        tms = [t_ref[0, d, h] for d, h, _, _ in inst]
        d_vns = [sg[d][0][:, _head(h)] for d, h, _, _ in inst]
        d_ws = [sg[d][3][:, _head(h)] for d, h, _, _ in inst]
        d_ts = [_dot_nt(d_vns[i], c.vb) + _dot_nt(d_ws[i], c.kg) for i, (_, _, _, c) in enumerate(inst)]
        xs = [_dot3(tms[i], d_ts[i], _TN) for i in range(8)]
        d_ls = [jnp.where(inst[i][2].strict, -_dot3(xs[i], tms[i], _NT), 0.0) for i in range(8)]
        d_attns = [jnp.where(m.incl, _dot_nt(do_ref[:, _head(h)], vn_refs[d][:, _head(h)]), 0.0) for d, h, m, _ in inst]
        d_vbs = [_dot_tn(tms[i], d_vns[i]) for i in range(8)]
        d_kgs = [_dot_tn(tms[i], d_ws[i]) for i in range(8)]
        d_a0s = [d_ls[i] * c.decay for i, (_, _, _, c) in enumerate(inst)]
        d_q0s = [d_attns[i] * c.decay for i, (_, _, _, c) in enumerate(inst)]
        es = [(d_ls[i] * c.a0 + d_attns[i] * c.q0) * c.decay for i, (_, _, _, c) in enumerate(inst)]
        kb_mm = [_dot(d_a0s[i], c.k) for i, (_, _, _, c) in enumerate(inst)]
        q_mm = [_dot(d_q0s[i], c.k) for i, (_, _, _, c) in enumerate(inst)]
        k_mm = [_dot_tn(d_a0s[i], c.kb) + _dot_tn(d_q0s[i], c.q) for i, (_, _, _, c) in enumerate(inst)]
        e_cols = [_dot_exact(ones, es[i], _TN, False)[:, 0:1] for i in range(8)]
        d_gcs, d_betas = [], []
        for i, (d, h, m, c) in enumerate(inst):
            sl = _head(h)
            d_kd, d_qd = sg[d][1][:, sl], sg[d][2][:, sl]
            d_cd = sg[d][4][0, h:h + 1, 0:1]
            d_vb, d_kg = d_vbs[i], d_kgs[i]
            d_kb = kb_mm[i] + d_kg * c.eg
            parts = (q_mm[i] + d_qd * c.eg, k_mm[i] + d_kd * c.ek + d_kb * c.beta, d_vb * c.beta)
            acc[h] = [p if a is None else a + p for a, p in zip(acc[h], parts)]
            s_kd = jnp.sum(d_kd * c.kd, axis=1, keepdims=True)
            d_gc = (jnp.sum(d_kg * c.kg, axis=1, keepdims=True) + jnp.sum(d_qd * c.qd, axis=1, keepdims=True) - s_kd
                    + jnp.sum(es[i], axis=1, keepdims=True) - e_cols[i])
            d_gl = jnp.sum(s_kd, axis=0, keepdims=True) + d_cd * c.cd
            d_gcs.append(d_gc + jnp.where(rowi == m.last, d_gl, 0.0))
            d_betas.append(jnp.sum(d_kb * c.k, axis=1, keepdims=True) + jnp.sum(d_vb * c.v, axis=1, keepdims=True))
        d_gs = [_dot_exact(m.tri, d_gcs[i] * ones, _TN, True)[:, 0:1] for i, (_, _, m, _) in enumerate(inst)]
        for i, (d, h, _, _) in enumerate(inst):
            dbg = dbg + jnp.where(lane == 4 * d + h, d_betas[i], 0.0) + jnp.where(lane == 8 + 4 * d + h, d_gs[i], 0.0)
        for h in range(NH):
            dq_ref[:, _head(h)], dk_ref[:, _head(h)], dv_ref[:, _head(h)] = acc[h]
        dbg_ref[...] = dbg

    tok = _rows(CHUNK, HW)
    bgs = _rows(CHUNK, BAP)
    mat = pl.BlockSpec((1, 2, NH, CHUNK, CHUNK), lambda i: (i, 0, 0, 0, 0))
    dcd = pl.BlockSpec((1, NH, DH), lambda i: (i, 0, 0))
    args = [q, k, v, bg, tmat, do, vns[0], vns[1]]
    in_specs = [tok, tok, tok, bgs, mat, tok, tok, tok]
    for d in range(2):
        args += list(seq_grads[d])
        in_specs += [tok] * 4 + [dcd]
    return pl.pallas_call(
        body, name=name, grid=(n,), in_specs=in_specs, out_specs=[tok, tok, tok, bgs],
        out_shape=[_sds((s, HW))] * 3 + [_sds((s, BAP))], compiler_params=_cparams(1),
    )(*args)


def _prep_bwd(c_qkv, p_ba, alog_row, dtb_row, dq, dk, dv, dbg, name):
    s = c_qkv.shape[0]
    t = min(256, s)

    def body(cq_ref, pc_ref, alog_ref, dtb_ref, dq_ref, dk_ref, dv_ref, dbg_ref,
             dcq_ref, dpc_ref, dalog_ref, ddtb_ref):
        @pl.when(pl.program_id(0) == 0)
        def _():
            dalog_ref[...] = jnp.zeros_like(dalog_ref)
            ddtb_ref[...] = jnp.zeros_like(ddtb_ref)

        cq = cq_ref[...]
        sq = cq * _sig(cq)
        sg = _silu_grad(cq)
        for h in range(NH):
            sl = slice(DH * h, DH * (h + 1))
            for off, d_ref, scale in ((0, dq_ref, DH ** -0.5), (RGW, dk_ref, 1.0)):
                csl = slice(off + DH * h, off + DH * (h + 1))
                xh = sq[:, csl]
                nrm = lax.rsqrt(jnp.sum(xh * xh, axis=-1, keepdims=True) + EPS)
                y = xh * nrm
                dy = d_ref[:, sl] * scale
                dcq_ref[:, csl] = nrm * (dy - y * jnp.sum(dy * y, axis=-1, keepdims=True)) * sg[:, csl]
        dcq_ref[:, 2 * RGW:] = dv_ref[...] * sg[:, 2 * RGW:]
        pc = pc_ref[...]
        lane = lax.broadcasted_iota(jnp.int32, pc.shape, 1)
        dbg = dbg_ref[...]
        beta = _sig(pc)
        ea = jnp.exp(alog_ref[...])
        z = pc + dtb_ref[...]
        g = -ea * _softplus(z)
        is_g = jnp.logical_and(lane >= 8, lane < 16)
        d_alpha = jnp.where(is_g, dbg * (-ea) * _sig(z), 0.0)
        dpc_ref[...] = jnp.where(lane < 8, dbg * beta * (1.0 - beta), d_alpha)
        dalog_ref[...] += _colsum(jnp.where(is_g, dbg * g, 0.0))
        ddtb_ref[...] += _colsum(d_alpha)

    return pl.pallas_call(
        body, name=name, grid=(s // t,),
        in_specs=[_rows(t, QKVW), _rows(t, BAP), _full((1, BAP)), _full((1, BAP))] + [_rows(t, HW)] * 3 + [_rows(t, BAP)],
        out_specs=[_rows(t, QKVW), _rows(t, BAP), _full((1, BAP)), _full((1, BAP))],
        out_shape=[_sds((s, QKVW)), _sds((s, BAP)), _sds((1, BAP)), _sds((1, BAP))],
        compiler_params=_cparams(1),
    )(c_qkv, p_ba, alog_row, dtb_row, dq, dk, dv, dbg)


def _mix_out_values(hf, hb, gate, of, ob, z, gn):
    hr = hf + hb
    y_rg = hr * _gelu(gate)
    osum = of + ob
    parts = []
    for h in range(NH):
        sl = slice(DH * h, DH * (h + 1))
        oh = osum[:, sl]
        r, ohat = _rms(oh)
        zh = z[:, sl]
        parts.append((r, ohat, zh))
    y_gdn = jnp.concatenate([ohat * gn * (zh * _sig(zh)) for (r, ohat, zh) in parts], axis=1)
    return hr, y_rg, y_gdn, parts


def _outproj(x1, hf, hb, gate, of, ob, z, gn, wout, name):
    s = x1.shape[0]
    t = min(256, s)

    def body(x_ref, hf_ref, hb_ref, gate_ref, of_ref, ob_ref, z_ref, gn_ref, w_ref, xo_ref, y_ref):
        _, y_rg, y_gdn, _ = _mix_out_values(hf_ref[...], hb_ref[...], gate_ref[...], of_ref[...], ob_ref[...],
                                            z_ref[...], gn_ref[...])
        y = jnp.concatenate([y_rg, y_gdn], axis=1).astype(BF16)
        y_ref[...] = y
        xo_ref[...] = x_ref[...] + jnp.dot(y, w_ref[...], preferred_element_type=F32)

    return pl.pallas_call(
        body, name=name, grid=(s // t,),
        in_specs=[_rows(t, D)] + [_rows(t, RGW)] * 6 + [_full((1, DH)), _full((D, D))],
        out_specs=[_rows(t, D), _rows(t, D)], out_shape=[_sds((s, D)), _sds((s, D), BF16)],
        compiler_params=_cparams(1),
    )(x1, hf, hb, gate, of, ob, z, gn, wout)


def _outproj_bwd(dx2, hf, hb, gate, of, ob, z, gn, wout, name):
    s = dx2.shape[0]
    t = min(256, s)

    def body(d_ref, hf_ref, hb_ref, gate_ref, of_ref, ob_ref, z_ref, gn_ref, w_ref,
             dhr_ref, dgate_ref, dos_ref, dz_ref, dgn_ref, db_ref):
        @pl.when(pl.program_id(0) == 0)
        def _():
            dgn_ref[...] = jnp.zeros_like(dgn_ref)

        gate = gate_ref[...]
        gn_v = gn_ref[...]
        hr, _, _, parts = _mix_out_values(hf_ref[...], hb_ref[...], gate, of_ref[...], ob_ref[...], z_ref[...], gn_v)
        dbf = d_ref[...].astype(BF16)
        db_ref[...] = dbf
        dy = _dot_nt(dbf, w_ref[...])
        dyr = dy[:, :RGW]
        dhr_ref[...] = dyr * _gelu(gate)
        dgate_ref[...] = dyr * hr * _gelu_grad(gate)
        dgn = jnp.zeros((1, DH), F32)
        for h, (r, ohat, zh) in enumerate(parts):
            sl = slice(DH * h, DH * (h + 1))
            dyh = dy[:, RGW + DH * h:RGW + DH * (h + 1)]
            sz = zh * _sig(zh)
            dn = dyh * sz
            dz_ref[:, sl] = dyh * ohat * gn_v * _silu_grad(zh)
            dgn = dgn + _colsum(dn * ohat)
            dos_ref[:, sl] = _rms_bwd(dn, ohat, r, gn_v)
        dgn_ref[...] += dgn

    return pl.pallas_call(
        body, name=name, grid=(s // t,),
        in_specs=[_rows(t, D)] + [_rows(t, RGW)] * 6 + [_full((1, DH)), _full((D, D))],
        out_specs=[_rows(t, RGW)] * 4 + [_full((1, DH)), _rows(t, D)],
        out_shape=[_sds((s, RGW))] * 4 + [_sds((1, DH)), _sds((s, D), BF16)],
        compiler_params=_cparams(1),
    )(dx2, hf, hb, gate, of, ob, z, gn, wout)


def _loss_head(x3, target, gain, name):
    s = x3.shape[0]
    t = min(256, s)

    def body(x_ref, t_ref, g_ref, dx_ref, loss_ref, dg_ref):
        @pl.when(pl.program_id(0) == 0)
        def _():
            loss_ref[...] = jnp.zeros_like(loss_ref)
            dg_ref[...] = jnp.zeros_like(dg_ref)

        r, xh = _rms(x_ref[...])
        gv = g_ref[...]
        err = xh * gv - t_ref[...]
        per_tok = jnp.mean(err * err, axis=-1, keepdims=True)
        loss_ref[...] += 0.5 * jnp.sum(per_tok, axis=0, keepdims=True)
        dy = err * (1.0 / D)
        dg_ref[...] += _colsum(dy * xh)
        dx_ref[...] = _rms_bwd(dy, xh, r, gv)

    return pl.pallas_call(
        body, name=name, grid=(s // t,), in_specs=[_rows(t, D), _rows(t, D), _full((1, D))],
        out_specs=[_rows(t, D), _full((8, 128)), _full((1, D))],
        out_shape=[_sds((s, D)), _sds((8, 128)), _sds((1, D))], compiler_params=_cparams(1),
    )(x3, target, gain)


def _adamw_math(wv, gv, mv, vv):
    mn = ADAM_B1 * mv + (1.0 - ADAM_B1) * gv
    vn = ADAM_B2 * vv + (1.0 - ADAM_B2) * (gv * gv)
    m_hat = mn / (1.0 - ADAM_B1 ** ADAM_STEP)
    v_hat = vn / (1.0 - ADAM_B2 ** ADAM_STEP)
    return -ADAM_LR * (m_hat / (jnp.sqrt(v_hat) + ADAM_EPS) + ADAM_WD * wv), mn, vn


def _row_tile(r, c):
    tr = r
    while tr * c * 4 > (1 << 20) and tr % 16 == 0:
        tr //= 2
    return tr


def _adamw(w, g, m, v, name):
    r, c = w.shape
    tr = _row_tile(r, c)

    def body(w_ref, g_ref, m_ref, v_ref, d_ref, nm_ref, nv_ref):
        d_ref[...], nm_ref[...], nv_ref[...] = _adamw_math(w_ref[...], g_ref[...], m_ref[...], v_ref[...])

    return pl.pallas_call(
        body, name=name, grid=(r // tr,), in_specs=[_rows(tr, c)] * 4, out_specs=[_rows(tr, c)] * 3,
        out_shape=[_sds((r, c))] * 3, compiler_params=_cparams(1),
    )(w, g, m, v)


def _adamw_halves(w, own, recv, m, v, c_arr, name):
    r, c = w.shape
    h = r // 2
    tr = _row_tile(h, c)
    nh = h // tr

    def body(c_ref, w_ref, own_ref, recv_ref, m_ref, v_ref, g_ref, d_ref, nm_ref, nv_ref):
        first_half = pl.program_id(0) < nh
        use_own = first_half == (c_ref[0] == 0)
        gv = jnp.where(use_own, own_ref[...], recv_ref[...])
        g_ref[...] = gv
        d_ref[...], nm_ref[...], nv_ref[...] = _adamw_math(w_ref[...], gv, m_ref[...], v_ref[...])

    full = pl.BlockSpec((tr, c), lambda i, c_ref: (i, 0))
    half = pl.BlockSpec((tr, c), lambda i, c_ref: (i % nh, 0))
    return pl.pallas_call(
        body, name=name, out_shape=[_sds((r, c))] * 4,
        grid_spec=pltpu.PrefetchScalarGridSpec(
            num_scalar_prefetch=1, grid=(2 * nh,), in_specs=[full, half, half, full, full], out_specs=[full] * 4),
        compiler_params=_cparams(1),
    )(c_arr, w, own, recv, m, v)


def _mesh_pos():
    return lax.axis_index("x"), lax.axis_index("y"), lax.axis_index("c")


def _other_chips(x, y):
    return [(1 - x, y), (x, 1 - y), (1 - x, 1 - y)]


def _all_gather(n_arr, space, out_shapes, block_of, name):
    def body(*refs):
        x_refs, out_refs = refs[:n_arr], refs[n_arr:2 * n_arr]
        send_sems, recv_sems, local_sems = refs[2 * n_arr:]
        x, y, c = _mesh_pos()
        me, sibling = (x, y, c), (x, y, 1 - c)
        chips = _other_chips(x, y)

        def slot(a, px, py, pc):
            return out_refs[a].at[4 * px + 2 * py + pc]

        def copy(a, k, block, to, src=None):
            return pltpu.make_async_remote_copy(
                src_ref=slot(a, *block) if src is None else src, dst_ref=slot(a, *block),
                send_sem=send_sems.at[7 * a + k], recv_sem=recv_sems.at[7 * a + k], device_id=to, device_id_type=MESH)

        srcs = [block_of(a, x_refs[a], c) for a in range(n_arr)]
        local = [pltpu.make_async_copy(srcs[a], slot(a, *me), local_sems.at[a]) for a in range(n_arr)]
        for cp in local:
            cp.start()
        first = []
        for a in range(n_arr):
            first += [copy(a, 1 + j, me, (*chip, c), src=srcs[a]) for j, chip in enumerate(chips)]
            first.append(copy(a, 0, me, sibling, src=srcs[a]))
        for cp in first:
            cp.start()
        passed = []
        for j, chip in enumerate(chips):
            for a in range(n_arr):
                copy(a, 1 + j, (*chip, c), me).wait_recv()
                fwd = copy(a, 4 + j, (*chip, c), sibling)
                fwd.start()
                passed.append(fwd)
        for a in range(n_arr):
            copy(a, 0, sibling, me).wait_recv()
            for j, chip in enumerate(chips):
                copy(a, 4 + j, (*chip, 1 - c), me).wait_recv()
        for cp in first + passed:
            cp.wait_send()
        for cp in local:
            cp.wait()

    return pl.pallas_call(
        body, name=name, out_shape=out_shapes,
        in_specs=[pl.BlockSpec(memory_space=space)] * n_arr, out_specs=[pl.BlockSpec(memory_space=space)] * n_arr,
        scratch_shapes=[pltpu.SemaphoreType.DMA((7 * n_arr,)), pltpu.SemaphoreType.DMA((7 * n_arr,)),
                        pltpu.SemaphoreType.DMA((n_arr,))],
    )


def _gather_weights(shards):
    halves = [w.shape[0] // 2 for w in shards]

    def block_of(a, x_ref, c):
        return x_ref.at[pl.ds(pl.multiple_of(c * halves[a], 16), halves[a]), :]

    outs = _all_gather(len(shards), pltpu.HBM, [_sds((8, h, w.shape[1]), BF16) for h, w in zip(halves, shards)],
                       block_of, "gather_weights")(*shards)
    return [o.reshape(NSH, 2 * h, o.shape[2]) for o, h in zip(outs, halves)]


def _gather_small(block, name):
    r, c = block.shape
    return _all_gather(1, pltpu.VMEM, [_sds((8, r, c))], lambda a, x_ref, c_: x_ref, name)(block)[0]


def _sibling_exchange(gs):
    n = len(gs)
    halves = [g.shape[1] // 2 for g in gs]

    def body(*refs):
        g_refs, land_refs = refs[:n], refs[n:2 * n]
        send_sems, recv_sems = refs[2 * n:]
        x, y, c = _mesh_pos()
        copies = []
        for a in range(n):
            h = halves[a]
            for s in range(NSH):
                copies.append(pltpu.make_async_remote_copy(
                    src_ref=g_refs[a].at[s, pl.ds(pl.multiple_of((1 - c) * h, 8), h), :], dst_ref=land_refs[a].at[s],
                    send_sem=send_sems.at[NSH * a + s], recv_sem=recv_sems.at[NSH * a + s],
                    device_id=(x, y, 1 - c), device_id_type=MESH))
        for cp in copies:
            cp.start()
        for cp in copies:
            cp.wait()

    return pl.pallas_call(
        body, name="grad_sibling_exchange", out_shape=[_sds((NSH, h, g.shape[2])) for h, g in zip(halves, gs)],
        in_specs=[pl.BlockSpec(memory_space=pltpu.HBM)] * n, out_specs=[pl.BlockSpec(memory_space=pltpu.HBM)] * n,
        scratch_shapes=[pltpu.SemaphoreType.DMA((NSH * n,)), pltpu.SemaphoreType.DMA((NSH * n,))],
    )(*gs)


def _chip_sum(g, land, c_arr, name):
    _, h, cols = land.shape

    def body(c_ref, g_ref, l_ref, o_ref):
        o_ref[...] = (g_ref[...] + l_ref[...]).astype(BF16)

    return pl.pallas_call(
        body, name=name, out_shape=_sds((NSH, h, cols), BF16),
        grid_spec=pltpu.PrefetchScalarGridSpec(
            num_scalar_prefetch=1, grid=(NSH,),
            in_specs=[pl.BlockSpec((1, h, cols), lambda s, c_ref: (s, c_ref[0], 0)),
                      pl.BlockSpec((1, h, cols), lambda s, c_ref: (s, 0, 0))],
            out_specs=pl.BlockSpec((1, h, cols), lambda s, c_ref: (s, 0, 0))),
        compiler_params=_cparams(1),
    )(c_arr, g, land)


def _chip_scatter(parts):
    n = len(parts)

    def body(*refs):
        p_refs, land_refs = refs[:n], refs[n:2 * n]
        send_sems, recv_sems, local_sems = refs[2 * n:]
        x, y, c = _mesh_pos()
        my_chip = 2 * x + y
        local = [pltpu.make_async_copy(p_refs[a].at[my_chip], land_refs[a].at[my_chip], local_sems.at[a]) for a in range(n)]
        for cp in local:
            cp.start()
        copies = []
        for a in range(n):
            for j, (px, py) in enumerate(_other_chips(x, y)):
                copies.append(pltpu.make_async_remote_copy(
                    src_ref=p_refs[a].at[2 * px + py], dst_ref=land_refs[a].at[my_chip],
                    send_sem=send_sems.at[3 * a + j], recv_sem=recv_sems.at[3 * a + j],
                    device_id=(px, py, c), device_id_type=MESH))
        for cp in copies:
            cp.start()
        for cp in copies:
            cp.wait()
        for cp in local:
            cp.wait()

    return pl.pallas_call(
        body, name="grad_chip_scatter", out_shape=[_sds(p.shape, BF16) for p in parts],
        in_specs=[pl.BlockSpec(memory_space=pltpu.HBM)] * n, out_specs=[pl.BlockSpec(memory_space=pltpu.HBM)] * n,
        scratch_shapes=[pltpu.SemaphoreType.DMA((3 * n,)), pltpu.SemaphoreType.DMA((3 * n,)), pltpu.SemaphoreType.DMA((n,))],
    )(*parts)


def _sum_slots(land, name):
    k, r, c = land.shape
    tr = r // 2 if r % 32 == 0 else r

    def body(l_ref, o_ref):
        acc = l_ref[0].astype(F32)
        for i in range(1, k):
            acc = acc + l_ref[i].astype(F32)
        o_ref[...] = acc

    return pl.pallas_call(
        body, name=name, grid=(r // tr,), in_specs=[pl.BlockSpec((k, tr, c), lambda i: (0, i, 0))],
        out_specs=_rows(tr, c), out_shape=_sds((r, c)), compiler_params=_cparams(1),
    )(land)


def _sibling_swap(halves):
    n = len(halves)

    def body(*refs):
        h_refs, out_refs = refs[:n], refs[n:2 * n]
        send_sems, recv_sems = refs[2 * n:]
        x, y, c = _mesh_pos()
        copies = [pltpu.make_async_remote_copy(
            src_ref=h_refs[a], dst_ref=out_refs[a], send_sem=send_sems.at[a], recv_sem=recv_sems.at[a],
            device_id=(x, y, 1 - c), device_id_type=MESH) for a in range(n)]
        for cp in copies:
            cp.start()
        for cp in copies:
            cp.wait()

    return pl.pallas_call(
        body, name="grad_sibling_swap", out_shape=[_sds(h.shape) for h in halves],
        in_specs=[pl.BlockSpec(memory_space=pltpu.HBM)] * n, out_specs=[pl.BlockSpec(memory_space=pltpu.HBM)] * n,
        scratch_shapes=[pltpu.SemaphoreType.DMA((n,)), pltpu.SemaphoreType.DMA((n,))],
    )(*halves)


def _pad_rows(v, width):
    flat = v.reshape(-1)
    rows = -(-flat.shape[0] // width)
    rows = -(-rows // 8) * 8
    return jnp.pad(flat, (0, rows * width - flat.shape[0])).reshape(rows, width)


def _size(shape):
    n = 1
    for dim in shape:
        n *= dim
    return n


def _row_pack(arrs):
    pieces = []
    for a in arrs:
        rows = -(-a.size // D)
        pieces.append(jnp.pad(a.reshape(-1), (0, rows * D - a.size)).reshape(rows, D))
    total = sum(p.shape[0] for p in pieces)
    if total % 8:
        pieces.append(jnp.zeros((8 - total % 8, D), F32))
    return jnp.concatenate(pieces, axis=0)


def _row_unpack(packed, shapes):
    out, r0 = [], 0
    for shp in shapes:
        n = _size(shp)
        rows = -(-n // D)
        out.append(packed[r0:r0 + rows].reshape(-1)[:n].reshape(shp))
        r0 += rows
    return out


def _block_diag(w):
    eye = jnp.eye(8, dtype=w.dtype)
    return (w[:, :, None, :] * eye[:, None, :, None]).reshape(RGW, RGW)


def _diag_blocks(dense):
    r = dense.reshape(8, 64, 8, 64)
    return jnp.stack([r[n, :, n, :] for n in range(8)])


def _lane_row(v8):
    return jnp.zeros((1, BAP), F32).at[0, 8:16].set(v8.reshape(8))


def _local_step(x, target, wts):
    (g1, wg1, wu1, wd1, gmix, w_in_groups, wout, rg_cw8, rg_cb, wgates, gbias, lam_row, gdn_cw8,
     alog_row, dtb_row, gn, g2, wg2, wu2, wd2, gfin) = wts
    s = x.shape[0]

    x1, a1, b1 = _ffn_fwd(x, g1, wg1, wu1, wd1, "ffn1_fwd")
    h2, p_rgx, p_gate, p_qkv, p_z, p_ba = _inproj(x1, gmix, w_in_groups, "in_proj")
    c_rg = _conv(p_rgx, rg_cw8, rg_cb, "rg_conv")
    c_qkv = _conv(p_qkv, gdn_cw8, jnp.zeros((1, QKVW), F32), "gdn_conv")
    a0, bb0, a1s, bb1, q, k, v, bg = _mix_prep(c_rg, c_qkv, p_ba, wgates, gbias, lam_row, alog_row, dtb_row, "mix_prep")
    hf = _scan(a0, bb0, False, "rg_scan_f")
    hb = _scan(a1s, bb1, True, "rg_scan_b")
    tmat, gu, gw, gqd, gkd, gat, gcd = _gdn_local_fwd(q, k, v, bg, "gdn_local_fwd")
    of, s0, vn0, ob, s1, vn1 = _gdn_seq_fwd(gu, gw, gqd, gkd, gat, gcd, "gdn_seq_fwd")
    x2, ymix = _outproj(x1, hf, hb, p_gate, of, ob, p_z, gn, wout, "out_proj")
    x3, a2, b2 = _ffn_fwd(x2, g2, wg2, wu2, wd2, "ffn2_fwd")
    dx3, loss_blk, d_gfin = _loss_head(x3, target, gfin, "loss_head")

    dx2, d_g2, hb2, dob2, fb2, dab2, dbb2 = _ffn_bwd(x2, dx3, g2, a2, b2, wg2, wu2, wd2, "ffn2_bwd")
    d_wg2 = _tn(hb2, dab2, "ffn2_dwg")
    d_wu2 = _tn(hb2, dbb2, "ffn2_dwu")
    d_wd2 = _tn(fb2, dob2, "ffn2_dwd")

    d_hr, d_gate, d_os, d_z, d_gn, dx2b = _outproj_bwd(dx2, hf, hb, p_gate, of, ob, p_z, gn, wout, "out_proj_bwd")
    d_wout = _tn(ymix, dx2b, "dw_out")[0]

    a0_up = _shift_rows(a0, 1, "shift_a0")
    a1_dn = _shift_rows(a1s, -1, "shift_a1")
    hf_dn = _shift_rows(hf, -1, "shift_hf")
    hb_up = _shift_rows(hb, 1, "shift_hb")
    lam0 = _scan(a0_up, d_hr, True, "rg_scan_f_bwd")
    lam1 = _scan(a1_dn, d_hr, False, "rg_scan_b_bwd")
    d_xc, d_pre, xcb, d_gbias, d_lam = _gates_bwd(c_rg, wgates, gbias, lam_row, lam0, lam1, hf_dn, hb_up, "rg_gates_bwd")
    d_wgates = _tn(xcb, d_pre, "dw_gates")[0]
    d_prgx, d_rgcw8, d_rgcb = _conv_bwd(p_rgx, d_xc, rg_cw8, "rg_conv_bwd")

    sg = _gdn_seq_bwd(d_os, gw, gqd, gkd, gat, gcd, (s0, s1), (vn0, vn1), "gdn_seq_bwd")
    dq, dk, dv, dbg = _gdn_local_bwd(q, k, v, bg, tmat, d_os, (vn0, vn1), (sg[0:5], sg[5:10]), "gdn_local_bwd")
    d_cqkv, d_pba, d_alog, d_dtb = _prep_bwd(c_qkv, p_ba, alog_row, dtb_row, dq, dk, dv, dbg, "gdn_prep_bwd")
    d_pqkv, d_gdncw8, _ = _conv_bwd(p_qkv, d_cqkv, gdn_cw8, "gdn_conv_bwd")

    dps = (d_prgx, d_gate, d_pqkv, d_z, d_pba)
    dx1, d_gmix = _inproj_bwd(x1, dx2, gmix, dps, w_in_groups, "in_proj_bwd")
    d_win_groups = [_tn(h2, dp, "dw_in_%d" % i)[0] for i, dp in enumerate(dps)]

    gx, d_g1, hb1, dob1, fb1, dab1, dbb1 = _ffn_bwd(x, dx1, g1, a1, b1, wg1, wu1, wd1, "ffn1_bwd")
    d_wg1 = _tn(hb1, dab1, "ffn1_dwg")
    d_wu1 = _tn(hb1, dbb1, "ffn1_dwu")
    d_wd1 = _tn(fb1, dob1, "ffn1_dwd")

    d_win = jnp.concatenate(d_win_groups[:4] + [d_win_groups[4][:, :BAW]], axis=1)
    big = (d_wg1, d_wu1, d_wd1, d_win, d_wout, d_wg2, d_wu2, d_wd2)
    small = dict(
        ffn1_norm=d_g1, mix_norm=d_gmix, rg_conv_w=d_rgcw8[:4], rg_conv_b=d_rgcb,
        rg_gate_a_w=jnp.stack([_diag_blocks(d_wgates[:, RGW * i:RGW * (i + 1)]) for i in (0, 1)]),
        rg_gate_x_w=jnp.stack([_diag_blocks(d_wgates[:, RGW * i:RGW * (i + 1)]) for i in (2, 3)]),
        rg_gate_a_b=d_gbias[0, :2 * RGW].reshape(2, RGW), rg_gate_x_b=d_gbias[0, 2 * RGW:].reshape(2, RGW),
        rg_lambda=d_lam.reshape(2, RGW), gdn_conv_w=d_gdncw8[:4],
        gdn_a_log=d_alog[0, 8:16].reshape(2, NH), gdn_dt_bias=d_dtb[0, 8:16].reshape(2, NH),
        gdn_norm=d_gn, ffn2_norm=d_g2, final_norm=d_gfin)
    return loss_blk, gx, big, small


_SMALL_NAMES = ("ffn1_norm", "mix_norm", "rg_conv_w", "rg_conv_b", "rg_gate_a_w", "rg_gate_a_b", "rg_gate_x_w",
                "rg_gate_x_b", "rg_lambda", "gdn_conv_w", "gdn_a_log", "gdn_dt_bias", "gdn_norm", "ffn2_norm", "final_norm")
_SMALL_SHARDED = dict(rg_conv_w=128, rg_gate_a_b=128, rg_gate_x_b=128, rg_lambda=128, gdn_conv_w=384)
_OUT_ORDER = ("ffn1_norm", "ffn1_w_gate", "ffn1_w_up", "ffn1_w_down", "mix_norm", "w_in", "w_out", "rg_conv_w", "rg_conv_b",
              "rg_gate_a_w", "rg_gate_a_b", "rg_gate_x_w", "rg_gate_x_b", "rg_lambda", "gdn_conv_w", "gdn_a_log",
              "gdn_dt_bias", "gdn_norm", "ffn2_norm", "ffn2_w_gate", "ffn2_w_up", "ffn2_w_down", "final_norm")
_BIG_NAMES = ("ffn1_w_gate", "ffn1_w_up", "ffn1_w_down", "w_in", "w_out", "ffn2_w_gate", "ffn2_w_up", "ffn2_w_down")


def kernel(x, ffn1_norm, ffn1_w_gate, ffn1_w_up, ffn1_w_down, mix_norm, w_in, w_out, rg_conv_w, rg_conv_b, rg_gate_a_w, rg_gate_a_b, rg_gate_x_w, rg_gate_x_b, rg_lambda, gdn_conv_w, gdn_a_log, gdn_dt_bias, gdn_norm, ffn2_norm, ffn2_w_gate, ffn2_w_up, ffn2_w_down, final_norm, loss_target, m_ffn1_norm, m_ffn1_w_gate, m_ffn1_w_up, m_ffn1_w_down, m_mix_norm, m_w_in, m_w_out, m_rg_conv_w, m_rg_conv_b, m_rg_gate_a_w, m_rg_gate_a_b, m_rg_gate_x_w, m_rg_gate_x_b, m_rg_lambda, m_gdn_conv_w, m_gdn_a_log, m_gdn_dt_bias, m_gdn_norm, m_ffn2_norm, m_ffn2_w_gate, m_ffn2_w_up, m_ffn2_w_down, m_final_norm, v_ffn1_norm, v_ffn1_w_gate, v_ffn1_w_up, v_ffn1_w_down, v_mix_norm, v_w_in, v_w_out, v_rg_conv_w, v_rg_conv_b, v_rg_gate_a_w, v_rg_gate_a_b, v_rg_gate_x_w, v_rg_gate_x_b, v_rg_lambda, v_gdn_conv_w, v_gdn_a_log, v_gdn_dt_bias, v_gdn_norm, v_ffn2_norm, v_ffn2_w_gate, v_ffn2_w_up, v_ffn2_w_down, v_final_norm):
    args = dict(locals())
    w = {n: args[n] for n in _OUT_ORDER}
    mom = {n: args["m_" + n] for n in _OUT_ORDER}
    var = {n: args["v_" + n] for n in _OUT_ORDER}
    xi, yi, ci = _mesh_pos()
    shard = 2 * xi + yi

    wg1, wu1, wd1, win_sh, wout_sh, wg2, wu2, wd2 = _gather_weights([w[n][0].astype(BF16) for n in _BIG_NAMES])
    w_in_full = jnp.transpose(win_sh, (1, 0, 2)).reshape(D, NSH * INSH)
    w_out_full = wout_sh.reshape(D, D)
    sm_local = _pad_rows(jnp.concatenate([w[n][0].reshape(-1) for n in _SMALL_SHARDED]), 128)
    sm_all = _gather_small(sm_local, "gather_small_weights")[0::2].reshape(NSH, -1)
    sm_full, off = {}, 0
    for n, wd_ in _SMALL_SHARDED.items():
        rows = w[n].shape[1]
        piece = sm_all[:, off:off + rows * wd_].reshape(NSH, rows, wd_)
        sm_full[n] = jnp.transpose(piece, (1, 0, 2)).reshape(rows, NSH * wd_)
        off += rows * wd_

    w_in_groups = (w_in_full[:, 0:512], w_in_full[:, 512:1024], w_in_full[:, 1024:2560], w_in_full[:, 2560:3072],
                   jnp.pad(w_in_full[:, 3072:3088], ((0, 0), (0, BAP - BAW))))
    wa, wx = rg_gate_a_w[0], rg_gate_x_w[0]
    wgates = jnp.concatenate([_block_diag(wa[0]), _block_diag(wa[1]), _block_diag(wx[0]), _block_diag(wx[1])],
                             axis=1).astype(BF16)
    gbias = jnp.concatenate([sm_full["rg_gate_a_b"].reshape(1, -1), sm_full["rg_gate_x_b"].reshape(1, -1)], axis=1)
    wts = (ffn1_norm, wg1, wu1, wd1, mix_norm, w_in_groups, w_out_full,
           jnp.pad(sm_full["rg_conv_w"], ((0, 4), (0, 0))), rg_conv_b, wgates, gbias, sm_full["rg_lambda"].reshape(1, -1),
           jnp.pad(sm_full["gdn_conv_w"], ((0, 4), (0, 0))), _lane_row(gdn_a_log), _lane_row(gdn_dt_bias),
           gdn_norm, ffn2_norm, wg2, wu2, wd2, final_norm.reshape(1, D))

    loss_blk, gx, big, small = _local_step(x[0], loss_target[0], wts)
    loss = lax.psum(loss_blk[0, 0], ("x", "y", "c"))

    d_wg1, d_wu1, d_wd1, d_win, d_wout, d_wg2, d_wu2, d_wd2 = big
    gs = [d_wg1, d_wu1, d_wd1, jnp.transpose(d_win.reshape(D, NSH, INSH), (1, 0, 2)), d_wout.reshape(NSH, OUTSH, D),
          d_wg2, d_wu2, d_wd2]
    lands = _sibling_exchange(gs)
    c_arr = ci.reshape(1).astype(jnp.int32)
    parts = [_chip_sum(g, l, c_arr, "chip_sum_" + n) for g, l, n in zip(gs, lands, _BIG_NAMES)]
    halves = [_sum_slots(l, "sum_chips_" + n) for l, n in zip(_chip_scatter(parts), _BIG_NAMES)]
    grads = {}

    sm_grad = _row_pack([small[n] for n in _SMALL_NAMES])
    sm_sum = _sum_slots(_gather_small(sm_grad, "gather_small_grads"), "small_grad_sum")
    for n, g in zip(_SMALL_NAMES, _row_unpack(sm_sum, [small[n].shape for n in _SMALL_NAMES])):
        if n in _SMALL_SHARDED:
            wd_ = _SMALL_SHARDED[n]
            g = lax.dynamic_slice_in_dim(g, shard * wd_, wd_, axis=1)
        grads[n] = g.reshape(w[n].shape)

    delta, new_m, new_v = {}, {}, {}
    for n, own, recv in zip(_BIG_NAMES, halves, _sibling_swap(halves)):
        shp = w[n].shape
        outs4 = _adamw_halves(w[n][0], own, recv, mom[n][0], var[n][0], c_arr, "adamw_" + n)
        grads[n], delta[n], new_m[n], new_v[n] = [o.reshape(shp) for o in outs4]
    packs = [_row_pack([t[n] for n in _SMALL_NAMES]) for t in (w, grads, mom, var)]
    sm_shapes = [w[n].shape for n in _SMALL_NAMES]
    for dst, src in zip((delta, new_m, new_v), _adamw(*packs, "adamw_small")):
        for n, val in zip(_SMALL_NAMES, _row_unpack(src, sm_shapes)):
            dst[n] = val

    outs = [loss, gx[None]]
    for group in (grads, delta, new_m, new_v):
        outs += [group[n] for n in _OUT_ORDER]
    return tuple(outs)
```

```python
import functools

import jax
import jax.numpy as jnp
from jax import lax
from jax.experimental import pallas as pl
from jax.experimental.pallas import tpu as pltpu

F32 = jnp.float32
BF16 = jnp.bfloat16
EPS = 1e-6
D = 1024
NSH = 4
FSH = 704
RGW = 512
QKVW = 1536
ZW = 512
BAW = 16
BAP = 128
INSH = 772
OUTSH = 256
CHUNK = 64
NH = 4
DH = 128
RG_C = 8.0
VMEM_LIMIT = 52 * 1024 * 1024
MESH = pl.DeviceIdType.MESH

ADAM_LR = 0.001
ADAM_B1 = 0.9
ADAM_B2 = 0.999
ADAM_EPS = 1e-08
ADAM_WD = 0.01
ADAM_STEP = 10


def _cparams(n_grid):
    return pltpu.CompilerParams(dimension_semantics=("arbitrary",) * n_grid, vmem_limit_bytes=VMEM_LIMIT)


def _sig(x):
    return 0.5 + 0.5 * jnp.tanh(0.5 * x)


def _sig_pos(x):
    return 1.0 / (1.0 + jnp.exp(-x))


def _softplus(x):
    return jnp.maximum(x, 0.0) + jnp.log(1.0 + jnp.exp(-jnp.abs(x)))


def _neg_expm1(y):
    series = -y * (1.0 + y * (0.5 + y * (1.0 / 6 + y * (1.0 / 24 + y * (1.0 / 120 + y * (1.0 / 720 + y / 5040))))))
    return jnp.where(y > -0.3, series, 1.0 - jnp.exp(y))


_GELU_C = 0.7978845608028654


def _gelu(x):
    t = jnp.tanh(_GELU_C * (x + 0.044715 * x * x * x))
    return 0.5 * x * (1.0 + t)


def _gelu_grad(x):
    t = jnp.tanh(_GELU_C * (x + 0.044715 * x * x * x))
    return 0.5 * (1.0 + t) + 0.5 * x * (1.0 - t * t) * _GELU_C * (1.0 + 3 * 0.044715 * x * x)


def _silu_grad(x):
    s = _sig(x)
    return s * (1.0 + x * (1.0 - s))


def _dot(a, b):
    return jnp.dot(a.astype(BF16), b.astype(BF16), preferred_element_type=F32)


def _dot_nt(a, b):
    return lax.dot_general(a.astype(BF16), b.astype(BF16), (((1,), (1,)), ((), ())), preferred_element_type=F32)


def _dot_tn(a, b):
    return lax.dot_general(a.astype(BF16), b.astype(BF16), (((0,), (0,)), ((), ())), preferred_element_type=F32)


_NN = ((1,), (0,))
_NT = ((1,), (1,))
_TN = ((0,), (0,))


def _dg(a, b, dims):
    return lax.dot_general(a, b, (dims, ((), ())), preferred_element_type=F32)


def _split2(a):
    hi = a.astype(BF16)
    return hi, (a - hi.astype(F32)).astype(BF16)


def _dot3(a, b, dims=_NN):
    ah, al = _split2(a)
    bh, bl = _split2(b)
    return _dg(ah, bh, dims) + _dg(ah, bl, dims) + _dg(al, bh, dims)


def _dot_exact(e, x, dims, e_is_lhs):
    x0 = x.astype(BF16)
    r = x - x0.astype(F32)
    x1 = r.astype(BF16)
    x2 = (r - x1.astype(F32)).astype(BF16)
    eb = e.astype(BF16)
    if e_is_lhs:
        return _dg(eb, x0, dims) + _dg(eb, x1, dims) + _dg(eb, x2, dims)
    return _dg(x0, eb, dims) + _dg(x1, eb, dims) + _dg(x2, eb, dims)


def _rms(xv):
    r = lax.rsqrt(jnp.mean(xv * xv, axis=-1, keepdims=True) + EPS)
    return r, xv * r


def _rms_bwd(dy, xh, r, gain):
    dxh = dy * gain
    return r * (dxh - xh * jnp.mean(dxh * xh, axis=-1, keepdims=True))


def _colsum(v):
    return jnp.sum(v, axis=0, keepdims=True)


def _rows(t, c):
    return pl.BlockSpec((t, c), lambda i: (i, 0))


def _full(shape):
    n = len(shape)
    return pl.BlockSpec(shape, lambda i: (0,) * n)


def _sds(shape, dtype=F32):
    return jax.ShapeDtypeStruct(shape, dtype)


def _ffn_fwd(x, gain, wg, wu, wd, name):
    s = x.shape[0]
    tm = min(512, s)

    def body(x_ref, g_ref, wg_ref, wu_ref, wd_ref, xo_ref, a_ref, b_ref, h_sc, acc):
        j = pl.program_id(1)

        @pl.when(j == 0)
        def _():
            _, xh = _rms(x_ref[...])
            h_sc[...] = (xh * g_ref[...]).astype(BF16)
            acc[...] = jnp.zeros_like(acc)

        h = h_sc[...]

        a = jnp.dot(h, wg_ref[0], preferred_element_type=F32)
        b = jnp.dot(h, wu_ref[0], preferred_element_type=F32)
        a_ref[0] = a.astype(BF16)
        b_ref[0] = b.astype(BF16)
        f = (a * _sig(a) * b).astype(BF16)
        acc[...] += jnp.dot(f, wd_ref[0], preferred_element_type=F32)

        @pl.when(j == NSH - 1)
        def _():
            xo_ref[...] = x_ref[...] + 0.5 * acc[...]

    return pl.pallas_call(
        body, name=name, grid=(s // tm, NSH),
        in_specs=[pl.BlockSpec((tm, D), lambda i, j: (i, 0)), pl.BlockSpec((1, D), lambda i, j: (0, 0)),
                  pl.BlockSpec((1, D, FSH), lambda i, j: (j, 0, 0)), pl.BlockSpec((1, D, FSH), lambda i, j: (j, 0, 0)),
                  pl.BlockSpec((1, FSH, D), lambda i, j: (j, 0, 0))],
        out_specs=[pl.BlockSpec((tm, D), lambda i, j: (i, 0)), pl.BlockSpec((1, tm, FSH), lambda i, j: (j, i, 0)),
                   pl.BlockSpec((1, tm, FSH), lambda i, j: (j, i, 0))],
        out_shape=[_sds((s, D)), _sds((NSH, s, FSH), BF16), _sds((NSH, s, FSH), BF16)],
        scratch_shapes=[pltpu.VMEM((tm, D), BF16), pltpu.VMEM((tm, D), F32)],
        compiler_params=_cparams(2),
    )(x, gain, wg, wu, wd)


def _ffn_bwd(x, dout, gain, a, b, wg, wu, wd, name):
    s = x.shape[0]
    tm = min(512, s)

    def body(x_ref, d_ref, g_ref, a_ref, b_ref, wg_ref, wu_ref, wd_ref,
             dx_ref, dg_ref, h_ref, do_ref, f_ref, da_ref, db_ref, do_sc, dh_acc):
        i = pl.program_id(0)
        j = pl.program_id(1)

        @pl.when(jnp.logical_and(i == 0, j == 0))
        def _():
            dg_ref[...] = jnp.zeros_like(dg_ref)

        @pl.when(j == 0)
        def _():
            _, xh = _rms(x_ref[...])
            h_ref[...] = (xh * g_ref[...]).astype(BF16)
            do = (0.5 * d_ref[...]).astype(BF16)
            do_sc[...] = do
            do_ref[...] = do
            dh_acc[...] = jnp.zeros_like(dh_acc)

        do = do_sc[...]

        df = _dot_nt(do, wd_ref[0])
        av = a_ref[0].astype(F32)
        bv = b_ref[0].astype(F32)
        sa = _sig(av)
        f_ref[0] = (av * sa * bv).astype(BF16)
        da = (df * bv * sa * (1.0 + av * (1.0 - sa))).astype(BF16)
        db = (df * av * sa).astype(BF16)
        da_ref[0] = da
        db_ref[0] = db
        dh_acc[...] += _dot_nt(da, wg_ref[0]) + _dot_nt(db, wu_ref[0])

        @pl.when(j == NSH - 1)
        def _():
            r, xh = _rms(x_ref[...])
            dh = dh_acc[...]
            dg_ref[...] += _colsum(dh * xh)
            dx_ref[...] = d_ref[...] + _rms_bwd(dh, xh, r, g_ref[...])

    tok = pl.BlockSpec((tm, D), lambda i, j: (i, 0))
    sh = pl.BlockSpec((1, tm, FSH), lambda i, j: (j, i, 0))
    return pl.pallas_call(
        body, name=name, grid=(s // tm, NSH),
        in_specs=[tok, tok, pl.BlockSpec((1, D), lambda i, j: (0, 0)), sh, sh,
                  pl.BlockSpec((1, D, FSH), lambda i, j: (j, 0, 0)), pl.BlockSpec((1, D, FSH), lambda i, j: (j, 0, 0)),
                  pl.BlockSpec((1, FSH, D), lambda i, j: (j, 0, 0))],
        out_specs=[tok, pl.BlockSpec((1, D), lambda i, j: (0, 0)), tok, tok, sh, sh, sh],
        out_shape=[_sds((s, D)), _sds((1, D)), _sds((s, D), BF16), _sds((s, D), BF16),
                   _sds((NSH, s, FSH), BF16), _sds((NSH, s, FSH), BF16), _sds((NSH, s, FSH), BF16)],
        scratch_shapes=[pltpu.VMEM((tm, D), BF16), pltpu.VMEM((tm, D), F32)],
        compiler_params=_cparams(2),
    )(x, dout, gain, a, b, wg, wu, wd)


def _tn(a, b, name):
    a_g = a.ndim == 3
    b_g = b.ndim == 3
    g = a.shape[0] if a_g else (b.shape[0] if b_g else 1)
    s, k = a.shape[-2:]
    n = b.shape[-1]
    ts = min(1024, s)

    def body(a_ref, b_ref, o_ref):
        @pl.when(pl.program_id(1) == 0)
        def _():
            o_ref[...] = jnp.zeros_like(o_ref)

        av = a_ref[0] if a_g else a_ref[...]
        bv = b_ref[0] if b_g else b_ref[...]
        o_ref[0] += _dot_tn(av, bv)

    a_spec = pl.BlockSpec((1, ts, k), lambda gi, si: (gi, si, 0)) if a_g else pl.BlockSpec((ts, k), lambda gi, si: (si, 0))
    b_spec = pl.BlockSpec((1, ts, n), lambda gi, si: (gi, si, 0)) if b_g else pl.BlockSpec((ts, n), lambda gi, si: (si, 0))
    return pl.pallas_call(
        body, name=name, grid=(g, s // ts), in_specs=[a_spec, b_spec],
        out_specs=pl.BlockSpec((1, k, n), lambda gi, si: (gi, 0, 0)),
        out_shape=_sds((g, k, n)), compiler_params=_cparams(2),
    )(a, b)


_P_WIDTHS = (RGW, RGW, QKVW, ZW, BAP)


def _inproj(x1, gain, ws, name):
    s = x1.shape[0]
    tm = min(256, s)

    def body(x_ref, g_ref, *refs):
        w_refs = refs[:5]
        h_ref = refs[5]
        p_refs = refs[6:]
        _, xh = _rms(x_ref[...])
        h = (xh * g_ref[...]).astype(BF16)
        h_ref[...] = h
        for w_ref, p_ref in zip(w_refs, p_refs):
            p_ref[...] = jnp.dot(h, w_ref[...], preferred_element_type=F32)

    return pl.pallas_call(
        body, name=name, grid=(s // tm,),
        in_specs=[_rows(tm, D), _full((1, D))] + [_full((D, w)) for w in _P_WIDTHS],
        out_specs=[_rows(tm, D)] + [_rows(tm, w) for w in _P_WIDTHS],
        out_shape=[_sds((s, D), BF16)] + [_sds((s, w)) for w in _P_WIDTHS],
        compiler_params=_cparams(1),
    )(x1, gain, *ws)


def _inproj_bwd(x1, dx2, gain, dps, ws, name):
    s = x1.shape[0]
    tm = min(256, s)

    def body(x_ref, d_ref, g_ref, *refs):
        dp_refs = refs[:5]
        w_refs = refs[5:10]
        dx_ref, dg_ref = refs[10:]

        @pl.when(pl.program_id(0) == 0)
        def _():
            dg_ref[...] = jnp.zeros_like(dg_ref)

        dh = jnp.zeros((tm, D), F32)
        for dp_ref, w_ref in zip(dp_refs, w_refs):
            dh = dh + _dot_nt(dp_ref[...], w_ref[...])
        r, xh = _rms(x_ref[...])
        dg_ref[...] += _colsum(dh * xh)
        dx_ref[...] = d_ref[...] + _rms_bwd(dh, xh, r, g_ref[...])

    return pl.pallas_call(
        body, name=name, grid=(s // tm,),
        in_specs=[_rows(tm, D), _rows(tm, D), _full((1, D))] + [_rows(tm, w) for w in _P_WIDTHS]
        + [_full((D, w)) for w in _P_WIDTHS],
        out_specs=[_rows(tm, D), _full((1, D))],
        out_shape=[_sds((s, D)), _sds((1, D))],
        compiler_params=_cparams(1),
    )(x1, dx2, gain, *dps, *ws)


def _halo_specs(s, t, c):
    nb8 = s // 8
    tb = t // 8
    prev = pl.BlockSpec((8, c), lambda i: (jnp.maximum(i * tb - 1, 0), 0))
    nxt = pl.BlockSpec((8, c), lambda i: (jnp.minimum((i + 1) * tb, nb8 - 1), 0))
    return prev, nxt


def _edge_masks(nb):
    i = pl.program_id(0)
    return jnp.where(i > 0, 1.0, 0.0).astype(F32), jnp.where(i < nb - 1, 1.0, 0.0).astype(F32)


def _shifted(xx, off, t):
    n = t + 16
    sh = (-off) % n
    rolled = xx if sh == 0 else pltpu.roll(xx, sh, 0)
    return rolled[8:8 + t]


def _conv(x, w8, bias, name):
    s, c = x.shape
    t = min(256, s)
    nb = s // t

    def body(x_ref, xp_ref, xn_ref, w_ref, b_ref, o_ref):
        pm, nm = _edge_masks(nb)
        for c0 in range(0, c, 512):
            cols = slice(c0, c0 + 512)
            xx = jnp.concatenate([xp_ref[:, cols] * pm, x_ref[:, cols], xn_ref[:, cols] * nm], axis=0)
            acc = jnp.zeros((t, 512), F32) + b_ref[:, cols]
            for j in range(4):
                acc = acc + w_ref[j:j + 1, cols] * _shifted(xx, j - 2, t)
            o_ref[:, cols] = acc

    prev, nxt = _halo_specs(s, t, c)
    return pl.pallas_call(
        body, name=name, grid=(nb,),
        in_specs=[_rows(t, c), prev, nxt, _full((8, c)), _full((1, c))],
        out_specs=_rows(t, c), out_shape=_sds((s, c)), compiler_params=_cparams(1),
    )(x, x, x, w8, bias)


def _conv_bwd(x, dc, w8, name):
    s, c = x.shape
    t = min(256, s)
    nb = s // t

    def body(x_ref, d_ref, dp_ref, dn_ref, w_ref, dx_ref, dw_ref, db_ref):
        @pl.when(pl.program_id(0) == 0)
        def _():
            dw_ref[...] = jnp.zeros_like(dw_ref)
            db_ref[...] = jnp.zeros_like(db_ref)

        pm, nm = _edge_masks(nb)
        for c0 in range(0, c, 512):
            cols = slice(c0, c0 + 512)
            dd = jnp.concatenate([dp_ref[:, cols] * pm, d_ref[:, cols], dn_ref[:, cols] * nm], axis=0)
            xv = x_ref[:, cols]
            acc = jnp.zeros((t, 512), F32)
            for j in range(4):
                dsh = _shifted(dd, 2 - j, t)
                acc = acc + w_ref[j:j + 1, cols] * dsh
                dw_ref[j:j + 1, cols] += _colsum(dsh * xv)
            dx_ref[:, cols] = acc
            db_ref[:, cols] += _colsum(d_ref[:, cols])

    prev, nxt = _halo_specs(s, t, c)
    return pl.pallas_call(
        body, name=name, grid=(nb,),
        in_specs=[_rows(t, c), _rows(t, c), prev, nxt, _full((8, c))],
        out_specs=[_rows(t, c), _full((8, c)), _full((1, c))],
        out_shape=[_sds((s, c)), _sds((8, c)), _sds((1, c))], compiler_params=_cparams(1),
    )(x, dc, dc, dc, w8)


def _rg_gates(xc, pre, lam_row):
    sp8 = RG_C * _softplus(-lam_row)
    out = []
    for d in range(2):
        r = _sig_pos(pre[:, RGW * d:RGW * (d + 1)])
        gi = _sig(pre[:, 2 * RGW + RGW * d:2 * RGW + RGW * (d + 1)])
        la = -r * sp8[:, RGW * d:RGW * (d + 1)]
        a = jnp.exp(la)
        mult = jnp.sqrt(_neg_expm1(2.0 * la))
        out.append((r, gi, a, mult))
    return out


def _mix_prep(c_rg, c_qkv, p_ba, wgates, gbias, lam_row, alog_row, dtb_row, name):
    s = c_rg.shape[0]
    t = min(256, s)

    def body(xc_ref, cq_ref, pc_ref, wg_ref, gb_ref, lam_ref, alog_ref, dtb_ref,
             a0_ref, b0_ref, a1_ref, b1_ref, q_ref, k_ref, v_ref, bg_ref):
        xc = xc_ref[...]
        pre = _dot(xc, wg_ref[...]) + gb_ref[...]
        gates = _rg_gates(xc, pre, lam_ref[...])
        for (r, gi, a, mult), a_ref, b_ref in zip(gates, (a0_ref, a1_ref), (b0_ref, b1_ref)):
            a_ref[...] = a
            b_ref[...] = mult * gi * xc
        cq = cq_ref[...]
        sq = cq * _sig(cq)
        for h in range(NH):
            sl = slice(DH * h, DH * (h + 1))
            qh = sq[:, sl]
            q_ref[:, sl] = qh * lax.rsqrt(jnp.sum(qh * qh, axis=-1, keepdims=True) + EPS) * (DH ** -0.5)
            kh = sq[:, RGW + DH * h:RGW + DH * (h + 1)]
            k_ref[:, sl] = kh * lax.rsqrt(jnp.sum(kh * kh, axis=-1, keepdims=True) + EPS)
        v_ref[...] = sq[:, 2 * RGW:]
        pc = pc_ref[...]
        lane = lax.broadcasted_iota(jnp.int32, pc.shape, 1)
        beta = _sig(pc)
        g = -jnp.exp(alog_ref[...]) * _softplus(pc + dtb_ref[...])
        bg_ref[...] = jnp.where(lane < 8, beta, jnp.where(lane < 16, g, 0.0))

    return pl.pallas_call(
        body, name=name, grid=(s // t,),
        in_specs=[_rows(t, RGW), _rows(t, QKVW), _rows(t, BAP), _full((RGW, 4 * RGW)), _full((1, 4 * RGW)),
                  _full((1, 2 * RGW)), _full((1, BAP)), _full((1, BAP))],
        out_specs=[_rows(t, RGW)] * 7 + [_rows(t, BAP)],
        out_shape=[_sds((s, RGW))] * 7 + [_sds((s, BAP))],
        compiler_params=_cparams(1),
    )(c_rg, c_qkv, p_ba, wgates, gbias, lam_row, alog_row, dtb_row)


def _scan_pair(af, bf, ar, br, shifted, name):
    s, c = af.shape
    t = min(512, s)
    nb = s // t
    ng = t // 8
    tb = t // 8
    up = lambda i: (i, 0)
    down = lambda i: (nb - 1 - i, 0)

    def body(*refs):
        if shifted:
            af_ref, bf_ref, ar_ref, br_ref, afp_ref, arn_ref, hf_ref, hr_ref, carry, fbuf, rbuf = refs
        else:
            af_ref, bf_ref, ar_ref, br_ref, hf_ref, hr_ref, carry = refs
        i = pl.program_id(0)

        @pl.when(i == 0)
        def _():
            carry[...] = jnp.zeros_like(carry)

        if shifted:
            edge = jnp.where(i > 0, 1.0, 0.0).astype(F32)
            fbuf[0:8, :] = afp_ref[...] * edge
            fbuf[8:t + 8, :] = af_ref[...]
            rbuf[0:t, :] = ar_ref[...]
            rbuf[t:t + 8, :] = arn_ref[...] * edge
        row = lax.broadcasted_iota(jnp.int32, (8, c), 0)

        def block_scan(av, bv, downwards):
            for k in (1, 2, 4):
                sh = (8 - k) if downwards else k
                m = (row < 8 - k) if downwards else (row >= k)
                a_s = pltpu.roll(av, sh, 0)
                b_s = pltpu.roll(bv, sh, 0)
                bv = jnp.where(m, av * b_s + bv, bv)
                av = jnp.where(m, av * a_s, av)
            return av, bv

        def group(gi, cvs):
            cf, cr = cvs
            rf = pl.multiple_of(gi * 8, 8)
            rr = pl.multiple_of((ng - 1 - gi) * 8, 8)
            if shifted:
                a_f = jnp.where(row > 0, pltpu.roll(fbuf[pl.ds(rf + 8, 8), :], 1, 0), pltpu.roll(fbuf[pl.ds(rf, 8), :], 1, 0))
                a_r = jnp.where(row < 7, pltpu.roll(rbuf[pl.ds(rr, 8), :], 7, 0), pltpu.roll(rbuf[pl.ds(rr + 8, 8), :], 7, 0))
            else:
                a_f = af_ref[pl.ds(rf, 8), :]
                a_r = ar_ref[pl.ds(rr, 8), :]
            a_f, b_f = block_scan(a_f, bf_ref[pl.ds(rf, 8), :], False)
            a_r, b_r = block_scan(a_r, br_ref[pl.ds(rr, 8), :], True)
            h_f = a_f * cf + b_f
            h_r = a_r * cr + b_r
            hf_ref[pl.ds(rf, 8), :] = h_f
            hr_ref[pl.ds(rr, 8), :] = h_r
            return h_f[7:8, :], h_r[0:1, :]

        cf, cr = lax.fori_loop(0, ng, group, (carry[0:1, :], carry[8:9, :]))
        carry[0:1, :] = cf
        carry[8:9, :] = cr

    in_specs = [pl.BlockSpec((t, c), up), pl.BlockSpec((t, c), up), pl.BlockSpec((t, c), down), pl.BlockSpec((t, c), down)]
    args = [af, bf, ar, br]
    scratch = [pltpu.VMEM((16, c), F32)]
    if shifted:
        in_specs += [pl.BlockSpec((8, c), lambda i: (jnp.maximum(i * tb - 1, 0), 0)),
                     pl.BlockSpec((8, c), lambda i: (jnp.minimum((nb - i) * tb, s // 8 - 1), 0))]
        args += [af, ar]
        scratch += [pltpu.VMEM((t + 8, c), F32), pltpu.VMEM((t + 8, c), F32)]
    return pl.pallas_call(
        body, name=name, grid=(nb,), in_specs=in_specs,
        out_specs=[pl.BlockSpec((t, c), up), pl.BlockSpec((t, c), down)], out_shape=[_sds((s, c)), _sds((s, c))],
        scratch_shapes=scratch, compiler_params=_cparams(1),
    )(*args)


def _gates_bwd(xc, wgates, gbias, lam_row, lam0, lam1, hf, hb, name):
    s = xc.shape[0]
    t = min(256, s)
    nb = s // t

    def body(xc_ref, wg_ref, gb_ref, lam_ref, l0_ref, l1_ref, hf_ref, hfp_ref, hfn_ref, hb_ref, hbp_ref, hbn_ref,
             dxc_ref, dpre_ref, xcb_ref, dgb_ref, dlam_ref):
        @pl.when(pl.program_id(0) == 0)
        def _():
            dgb_ref[...] = jnp.zeros_like(dgb_ref)
            dlam_ref[...] = jnp.zeros_like(dlam_ref)

        pm, nm = _edge_masks(nb)
        h_prev = _shifted(jnp.concatenate([hfp_ref[...] * pm, hf_ref[...], hfn_ref[...] * nm], axis=0), -1, t)
        h_next = _shifted(jnp.concatenate([hbp_ref[...] * pm, hb_ref[...], hbn_ref[...] * nm], axis=0), 1, t)
        h_shift = (h_prev, h_next)
        xv = xc_ref[...]
        pre = _dot(xv, wg_ref[...]) + gb_ref[...]
        lam_row_v = lam_ref[...]
        sp8 = RG_C * _softplus(-lam_row_v)
        dsp_dlam = -RG_C * _sig(-lam_row_v)
        gates = _rg_gates(xv, pre, lam_row_v)
        dxc = jnp.zeros((t, RGW), F32)
        dpre_r = []
        dpre_i = []
        for d, ((r, gi, a, mult), l_ref, hs) in enumerate(zip(gates, (l0_ref, l1_ref), h_shift)):
            dbb = l_ref[...]
            da = dbb * hs
            cs = slice(RGW * d, RGW * (d + 1))
            dmult = dbb * gi * xv
            dgi = dbb * mult * xv
            dxc = dxc + dbb * mult * gi
            dla = da * a - dmult * a * a / mult
            dr = -dla * sp8[:, cs]
            dlam_ref[:, cs] += _colsum(-dla * r) * dsp_dlam[:, cs]
            dpre_r.append(dr * r * (1.0 - r))
            dpre_i.append(dgi * gi * (1.0 - gi))
        dpre = jnp.concatenate(dpre_r + dpre_i, axis=1)
        dgb_ref[...] += _colsum(dpre)
        dpre_b = dpre.astype(BF16)
        dpre_ref[...] = dpre_b
        xcb_ref[...] = xv.astype(BF16)
        dxc_ref[...] = dxc + _dot_nt(dpre_b, wg_ref[...])

    prev, nxt = _halo_specs(s, t, RGW)
    return pl.pallas_call(
        body, name=name, grid=(s // t,),
        in_specs=[_rows(t, RGW), _full((RGW, 4 * RGW)), _full((1, 4 * RGW)), _full((1, 2 * RGW))] + [_rows(t, RGW)] * 2
        + [_rows(t, RGW), prev, nxt] * 2,
        out_specs=[_rows(t, RGW), _rows(t, 4 * RGW), _rows(t, RGW), _full((1, 4 * RGW)), _full((1, 2 * RGW))],
        out_shape=[_sds((s, RGW)), _sds((s, 4 * RGW), BF16), _sds((s, RGW), BF16), _sds((1, 4 * RGW)), _sds((1, 2 * RGW))],
        compiler_params=_cparams(1),
    )(xc, wgates, gbias, lam_row, lam0, lam1, hf, hf, hf, hb, hb, hb)


class _GdnMasks:
    def __init__(self, d):
        ri = lax.broadcasted_iota(jnp.int32, (CHUNK, CHUNK), 0)
        ci = lax.broadcasted_iota(jnp.int32, (CHUNK, CHUNK), 1)
        self.incl = (ri >= ci) if d == 0 else (ri <= ci)
        self.strict = (ri > ci) if d == 0 else (ri < ci)
        b16 = jnp.right_shift(ri, 4) == jnp.right_shift(ci, 4)
        b32 = jnp.right_shift(ri, 5) == jnp.right_shift(ci, 5)
        self.diag16 = b16
        self.off32 = jnp.logical_and(b32, jnp.logical_not(b16))
        self.off64 = jnp.logical_not(b32)
        self.eye = jnp.where(ri == ci, 1.0, 0.0).astype(F32)
        self.tri = jnp.where(self.incl, 1.0, 0.0).astype(F32)
        self.last = CHUNK - 1 if d == 0 else 0


def _tri_inv(lmat, m):
    return _tri_inv_many([lmat], [m])[0]


def _tri_inv_many(lmats, masks):
    n = len(lmats)
    ns = [jnp.where(masks[i].diag16, lmats[i], 0.0) for i in range(n)]
    ps = [masks[i].eye - ns[i] for i in range(n)]
    qs = [_dot3(ns[i], ns[i]) for i in range(n)]
    for step in range(3):
        ps = [_dot3(ps[i], masks[i].eye + qs[i]) for i in range(n)]
        if step < 2:
            qs = [_dot3(qs[i], qs[i]) for i in range(n)]
    for off in ("off32", "off64"):
        ts = [_dot3(ps[i], jnp.where(getattr(masks[i], off), lmats[i], 0.0)) for i in range(n)]
        ps = [ps[i] - _dot3(ts[i], ps[i]) for i in range(n)]
    return ps


def _chunk_cumsums(m, bgv):
    return _dot_exact(m.tri, bgv, _NN, True), _dot_exact(m.tri, bgv, ((0,), (1,)), False)


class _GdnHead:
    def __init__(self, qh, kh, vh, kk, q0, bg, gcs, gcs_t, d, h, m):
        cb = 4 * d + h
        cg = 8 + 4 * d + h
        self.q, self.k, self.v = qh, kh, vh
        self.beta = bg[:, cb:cb + 1]
        gcol = gcs[:, cg:cg + 1]
        grow = gcs_t[cg:cg + 1, :]
        gl = gcs[m.last:m.last + 1, cg:cg + 1]
        self.decay = jnp.exp(jnp.where(m.incl, gcol - grow, -1e30))
        self.kb = kh * self.beta
        self.vb = vh * self.beta
        self.a0 = kk * self.beta
        self.q0 = q0
        self.lmat = jnp.where(m.strict, self.a0 * self.decay, 0.0)
        self.attn = self.q0 * self.decay
        self.eg = jnp.exp(gcol)
        self.ek = jnp.exp(gl - gcol)
        self.cd = jnp.exp(gl)
        self.kg = self.kb * self.eg
        self.qd = qh * self.eg
        self.kd = kh * self.ek


HW = NH * DH
SEQ_CB = 4


def _head(h):
    return slice(DH * h, DH * (h + 1))


def _gdn_local_fwd(q, k, v, bg, name):
    s = q.shape[0]
    n = s // CHUNK

    def body(q_ref, k_ref, v_ref, bg_ref, t_ref, u_ref, w_ref, qd_ref, kd_ref, at_ref, cd_ref):
        bgv = bg_ref[...]
        qs = [q_ref[:, _head(h)] for h in range(NH)]
        ks = [k_ref[:, _head(h)] for h in range(NH)]
        kk = [_dot_nt(ks[h], ks[h]) for h in range(NH)]
        q0 = [_dot_nt(qs[h], ks[h]) for h in range(NH)]
        inst = []
        for d in range(2):
            m = _GdnMasks(d)
            gcs, gcs_t = _chunk_cumsums(m, bgv)
            for h in range(NH):
                c = _GdnHead(qs[h], ks[h], v_ref[:, _head(h)], kk[h], q0[h], bgv, gcs, gcs_t, d, h, m)
                inst.append((d, h, m, c))
        tms = _tri_inv_many([c.lmat for _, _, _, c in inst], [m for _, _, m, _ in inst])
        for (d, h, m, c), tm in zip(inst, tms):
            sl = _head(h)
            t_ref[0, d, h] = tm
            u_ref[d, :, sl] = _dot(tm, c.vb)
            w_ref[d, :, sl] = _dot(tm, c.kg).astype(BF16)
            qd_ref[d, :, sl] = c.qd.astype(BF16)
            kd_ref[d, :, sl] = c.kd.astype(BF16)
            at_ref[0, d, h] = c.attn.astype(BF16)
            cd_ref[0, 4 * d + h:4 * d + h + 1, :] = jnp.broadcast_to(c.cd, (1, DH))

    tok = _rows(CHUNK, HW)
    tok2 = pl.BlockSpec((2, CHUNK, HW), lambda i: (0, i, 0))
    mat = pl.BlockSpec((1, 2, NH, CHUNK, CHUNK), lambda i: (i, 0, 0, 0, 0))
    return pl.pallas_call(
        body, name=name, grid=(n,), in_specs=[tok, tok, tok, _rows(CHUNK, BAP)],
        out_specs=[mat, tok2, tok2, tok2, tok2, mat, pl.BlockSpec((1, 8, DH), lambda i: (i, 0, 0))],
        out_shape=[_sds((n, 2, NH, CHUNK, CHUNK)), _sds((2, s, HW)), _sds((2, s, HW), BF16), _sds((2, s, HW), BF16),
                   _sds((2, s, HW), BF16), _sds((n, 2, NH, CHUNK, CHUNK), BF16), _sds((n, 8, DH))],
        compiler_params=_cparams(1),
    )(q, k, v, bg)


def _seq_specs(s, order):
    n = s // CHUNK
    cb = min(SEQ_CB, n)
    nb = n // cb
    tb = cb * CHUNK

    def blk(d):
        return (lambda i: i) if order[d] else (lambda i: nb - 1 - i)

    def per_dir(make):
        return [make(d, blk(d)) for d in range(2)]

    tok2 = per_dir(lambda d, f: pl.BlockSpec((1, tb, HW), lambda i: (d, f(i), 0)))
    tok = per_dir(lambda d, f: pl.BlockSpec((tb, HW), lambda i: (f(i), 0)))
    mat = per_dir(lambda d, f: pl.BlockSpec((cb, 1, NH, CHUNK, CHUNK), lambda i: (f(i), d, 0, 0, 0)))
    cds = per_dir(lambda d, f: pl.BlockSpec((cb, 8, DH), lambda i: (f(i), 0, 0)))
    sts = per_dir(lambda d, f: pl.BlockSpec((cb, NH, DH, DH), lambda i: (f(i), 0, 0, 0)))
    dcd = per_dir(lambda d, f: pl.BlockSpec((cb, NH, DH), lambda i: (f(i), 0, 0)))
    return n, cb, nb, tok2, tok, mat, cds, sts, dcd


def _gdn_seq_fwd(u, w, qd, kd, at, cd, name):
    s = u.shape[1]
    n, cb, nb, tok2, tok, mat, cds, sts, _ = _seq_specs(s, (True, False))

    def body(*refs):
        ins = (refs[0:6], refs[6:12])
        outs = (refs[12:15], refs[15:18])
        st = refs[18]

        @pl.when(pl.program_id(0) == 0)
        def _():
            st[...] = jnp.zeros_like(st)

        for j in range(cb):
            items = []
            for d in range(2):
                jj = j if d == 0 else cb - 1 - j
                items += [(d, h, jj, slice(CHUNK * jj, CHUNK * (jj + 1)), _head(h)) for h in range(NH)]
            shs = [st[d, h] for d, h, _, _, _ in items]
            wss = [_dot(ins[d][1][0, rows, sl], sh) for (d, h, jj, rows, sl), sh in zip(items, shs)]
            vns = [ins[d][0][0, rows, sl] - ws for (d, h, jj, rows, sl), ws in zip(items, wss)]
            news = [sh * ins[d][5][jj, 4 * d + h:4 * d + h + 1, :] + _dot_tn(ins[d][3][0, rows, sl], vn)
                    for (d, h, jj, rows, sl), sh, vn in zip(items, shs, vns)]
            for (d, h, jj, rows, sl), sh, vn, new in zip(items, shs, vns, news):
                o_r, s_r, vn_r = outs[d]
                st[d, h] = new
                s_r[jj, h] = sh
                vn_r[rows, sl] = vn
                o_r[rows, sl] = _dot(ins[d][2][0, rows, sl], sh) + _dot(ins[d][4][jj, 0, h], vn)

    in_specs, out_specs, out_shape = [], [], []
    for d in range(2):
        in_specs += [tok2[d]] * 4 + [mat[d], cds[d]]
        out_specs += [tok[d], sts[d], tok[d]]
        out_shape += [_sds((s, HW)), _sds((n, NH, DH, DH)), _sds((s, HW))]
    return pl.pallas_call(
        body, name=name, grid=(nb,), in_specs=in_specs, out_specs=out_specs, out_shape=out_shape,
        scratch_shapes=[pltpu.VMEM((2, NH, DH, DH), F32)], compiler_params=_cparams(1),
    )(u, w, qd, kd, at, cd, u, w, qd, kd, at, cd)


def _gdn_seq_bwd(do, w, qd, kd, at, cd, states, vns, name):
    s = do.shape[0]
    n, cb, nb, tok2, tok, mat, cds, sts, dcd = _seq_specs(s, (False, True))

    def body(*refs):
        ins = (refs[0:8], refs[8:16])
        outs = (refs[16:21], refs[21:26])
        dst = refs[26]

        @pl.when(pl.program_id(0) == 0)
        def _():
            dst[...] = jnp.zeros_like(dst)

        for j in range(cb):
            items = []
            for d in range(2):
                jj = cb - 1 - j if d == 0 else j
                items += [(d, h, jj, slice(CHUNK * jj, CHUNK * (jj + 1)), _head(h)) for h in range(NH)]
            dsns = [dst[d, h] for d, h, _, _, _ in items]
            dohs = [ins[d][0][rows, sl] for d, h, jj, rows, sl in items]
            d_vns = [_dot_tn(ins[d][4][jj, 0, h], doh) + _dot(ins[d][3][0, rows, sl], dsn)
                     for (d, h, jj, rows, sl), doh, dsn in zip(items, dohs, dsns)]
            news = [ins[d][5][jj, 4 * d + h:4 * d + h + 1, :] * dsn + _dot_tn(ins[d][2][0, rows, sl], doh)
                    - _dot_tn(ins[d][1][0, rows, sl], d_vn)
                    for (d, h, jj, rows, sl), doh, dsn, d_vn in zip(items, dohs, dsns, d_vns)]
            for (d, h, jj, rows, sl), doh, dsn, d_vn, new in zip(items, dohs, dsns, d_vns, news):
                dvn_r, dkd_r, dqd_r, dw_r, dcd_r = outs[d]
                sh = ins[d][6][jj, h]
                dst[d, h] = new
                dvn_r[rows, sl] = d_vn
                dkd_r[rows, sl] = _dot_nt(ins[d][7][rows, sl], dsn)
                dqd_r[rows, sl] = _dot_nt(doh, sh)
                dw_r[rows, sl] = -_dot_nt(d_vn, sh)
                d_cd = jnp.sum(jnp.sum(sh * dsn, axis=1, keepdims=True), axis=0, keepdims=True)
                dcd_r[jj, h:h + 1, :] = jnp.broadcast_to(d_cd, (1, DH))

    in_specs, out_specs, out_shape, args = [], [], [], []
    for d in range(2):
        in_specs += [tok[d]] + [tok2[d]] * 3 + [mat[d], cds[d], sts[d], tok[d]]
        args += [do, w, qd, kd, at, cd, states[d], vns[d]]
        out_specs += [tok[d]] * 4 + [dcd[d]]
        out_shape += [_sds((s, HW))] * 4 + [_sds((n, NH, DH))]
    return pl.pallas_call(
        body, name=name, grid=(nb,), in_specs=in_specs, out_specs=out_specs, out_shape=out_shape,
        scratch_shapes=[pltpu.VMEM((2, NH, DH, DH), F32)], compiler_params=_cparams(1),
    )(*args)


def _gdn_local_bwd(q, k, v, bg, tmat, do, vns, seq_grads, name):
    s = q.shape[0]
    n = s // CHUNK

    def body(*refs):
        q_ref, k_ref, v_ref, bg_ref, t_ref, do_ref = refs[0:6]
        vn_refs = refs[6:8]
        sg = (refs[8:13], refs[13:18])
        dq_ref, dk_ref, dv_ref, dbg_ref = refs[18:]
        bgv = bg_ref[...]
        qs = [q_ref[:, _head(h)] for h in range(NH)]
        ks = [k_ref[:, _head(h)] for h in range(NH)]
        kk = [_dot_nt(ks[h], ks[h]) for h in range(NH)]
        q0 = [_dot_nt(qs[h], ks[h]) for h in range(NH)]
        lane = lax.broadcasted_iota(jnp.int32, (CHUNK, BAP), 1)
        rowi = lax.broadcasted_iota(jnp.int32, (CHUNK, 1), 0)
        ones = jnp.ones((CHUNK, DH), F32)
        dbg = jnp.zeros((CHUNK, BAP), F32)
        acc = [[None, None, None] for _ in range(NH)]
        inst = []
        for d in range(2):
            m = _GdnMasks(d)
            gcs, gcs_t = _chunk_cumsums(m, bgv)
            for h in range(NH):
                c = _GdnHead(qs[h], ks[h], v_ref[:, _head(h)], kk[h], q0[h], bgv, gcs, gcs_t, d, h, m)
                inst.append((d, h, m, c))
        tms = [t_ref[0, d, h] for d, h, _, _ in inst]
        d_vns = [sg[d][0][:, _head(h)] for d, h, _, _ in inst]
        d_ws = [sg[d][3][:, _head(h)] for d, h, _, _ in inst]
        d_ts = [_dot_nt(d_vns[i], c.vb) + _dot_nt(d_ws[i], c.kg) for i, (_, _, _, c) in enumerate(inst)]
        xs = [_dot3(tms[i], d_ts[i], _TN) for i in range(8)]
        d_ls = [jnp.where(inst[i][2].strict, -_dot3(xs[i], tms[i], _NT), 0.0) for i in range(8)]
        d_attns = [jnp.where(m.incl, _dot_nt(do_ref[:, _head(h)], vn_refs[d][:, _head(h)]), 0.0) for d, h, m, _ in inst]
        d_vbs = [_dot_tn(tms[i], d_vns[i]) for i in range(8)]
        d_kgs = [_dot_tn(tms[i], d_ws[i]) for i in range(8)]
        d_a0s = [d_ls[i] * c.decay for i, (_, _, _, c) in enumerate(inst)]
        d_q0s = [d_attns[i] * c.decay for i, (_, _, _, c) in enumerate(inst)]
        es = [(d_ls[i] * c.a0 + d_attns[i] * c.q0) * c.decay for i, (_, _, _, c) in enumerate(inst)]
        kb_mm = [_dot(d_a0s[i], c.k) for i, (_, _, _, c) in enumerate(inst)]
        q_mm = [_dot(d_q0s[i], c.k) for i, (_, _, _, c) in enumerate(inst)]
        k_mm = [_dot_tn(d_a0s[i], c.kb) + _dot_tn(d_q0s[i], c.q) for i, (_, _, _, c) in enumerate(inst)]
        e_cols = [_dot_exact(ones, es[i], _TN, False)[:, 0:1] for i in range(8)]
        d_gcs, d_betas = [], []
        for i, (d, h, m, c) in enumerate(inst):
            sl = _head(h)
            d_kd, d_qd = sg[d][1][:, sl], sg[d][2][:, sl]
            d_cd = sg[d][4][0, h:h + 1, 0:1]
            d_vb, d_kg = d_vbs[i], d_kgs[i]
            d_kb = kb_mm[i] + d_kg * c.eg
            parts = (q_mm[i] + d_qd * c.eg, k_mm[i] + d_kd * c.ek + d_kb * c.beta, d_vb * c.beta)
            acc[h] = [p if a is None else a + p for a, p in zip(acc[h], parts)]
            s_kd = jnp.sum(d_kd * c.kd, axis=1, keepdims=True)
            d_gc = (jnp.sum(d_kg * c.kg, axis=1, keepdims=True) + jnp.sum(d_qd * c.qd, axis=1, keepdims=True) - s_kd
                    + jnp.sum(es[i], axis=1, keepdims=True) - e_cols[i])
            d_gl = jnp.sum(s_kd, axis=0, keepdims=True) + d_cd * c.cd
            d_gcs.append(d_gc + jnp.where(rowi == m.last, d_gl, 0.0))
            d_betas.append(jnp.sum(d_kb * c.k, axis=1, keepdims=True) + jnp.sum(d_vb * c.v, axis=1, keepdims=True))
        d_gs = [_dot_exact(m.tri, d_gcs[i] * ones, _TN, True)[:, 0:1] for i, (_, _, m, _) in enumerate(inst)]
        for i, (d, h, _, _) in enumerate(inst):
            dbg = dbg + jnp.where(lane == 4 * d + h, d_betas[i], 0.0) + jnp.where(lane == 8 + 4 * d + h, d_gs[i], 0.0)
        for h in range(NH):
            dq_ref[:, _head(h)], dk_ref[:, _head(h)], dv_ref[:, _head(h)] = acc[h]
        dbg_ref[...] = dbg

    tok = _rows(CHUNK, HW)
    bgs = _rows(CHUNK, BAP)
    mat = pl.BlockSpec((1, 2, NH, CHUNK, CHUNK), lambda i: (i, 0, 0, 0, 0))
    dcd = pl.BlockSpec((1, NH, DH), lambda i: (i, 0, 0))
    args = [q, k, v, bg, tmat, do, vns[0], vns[1]]
    in_specs = [tok, tok, tok, bgs, mat, tok, tok, tok]
    for d in range(2):
        args += list(seq_grads[d])
        in_specs += [tok] * 4 + [dcd]
    return pl.pallas_call(
        body, name=name, grid=(n,), in_specs=in_specs, out_specs=[tok, tok, tok, bgs],
        out_shape=[_sds((s, HW))] * 3 + [_sds((s, BAP))], compiler_params=_cparams(1),
    )(*args)


def _prep_bwd(c_qkv, p_ba, alog_row, dtb_row, dq, dk, dv, dbg, name):
    s = c_qkv.shape[0]
    t = min(256, s)

    def body(cq_ref, pc_ref, alog_ref, dtb_ref, dq_ref, dk_ref, dv_ref, dbg_ref,
             dcq_ref, dpc_ref, dalog_ref, ddtb_ref):
        @pl.when(pl.program_id(0) == 0)
        def _():
            dalog_ref[...] = jnp.zeros_like(dalog_ref)
            ddtb_ref[...] = jnp.zeros_like(ddtb_ref)

        cq = cq_ref[...]
        sq = cq * _sig(cq)
        sg = _silu_grad(cq)
        for h in range(NH):
            sl = slice(DH * h, DH * (h + 1))
            for off, d_ref, scale in ((0, dq_ref, DH ** -0.5), (RGW, dk_ref, 1.0)):
                csl = slice(off + DH * h, off + DH * (h + 1))
                xh = sq[:, csl]
                nrm = lax.rsqrt(jnp.sum(xh * xh, axis=-1, keepdims=True) + EPS)
                y = xh * nrm
                dy = d_ref[:, sl] * scale
                dcq_ref[:, csl] = nrm * (dy - y * jnp.sum(dy * y, axis=-1, keepdims=True)) * sg[:, csl]
        dcq_ref[:, 2 * RGW:] = dv_ref[...] * sg[:, 2 * RGW:]
        pc = pc_ref[...]
        lane = lax.broadcasted_iota(jnp.int32, pc.shape, 1)
        dbg = dbg_ref[...]
        beta = _sig(pc)
        ea = jnp.exp(alog_ref[...])
        z = pc + dtb_ref[...]
        g = -ea * _softplus(z)
        is_g = jnp.logical_and(lane >= 8, lane < 16)
        d_alpha = jnp.where(is_g, dbg * (-ea) * _sig(z), 0.0)
        dpc_ref[...] = jnp.where(lane < 8, dbg * beta * (1.0 - beta), d_alpha)
        dalog_ref[...] += _colsum(jnp.where(is_g, dbg * g, 0.0))
        ddtb_ref[...] += _colsum(d_alpha)

    return pl.pallas_call(
        body, name=name, grid=(s // t,),
        in_specs=[_rows(t, QKVW), _rows(t, BAP), _full((1, BAP)), _full((1, BAP))] + [_rows(t, HW)] * 3 + [_rows(t, BAP)],
        out_specs=[_rows(t, QKVW), _rows(t, BAP), _full((1, BAP)), _full((1, BAP))],
        out_shape=[_sds((s, QKVW)), _sds((s, BAP)), _sds((1, BAP)), _sds((1, BAP))],
        compiler_params=_cparams(1),
    )(c_qkv, p_ba, alog_row, dtb_row, dq, dk, dv, dbg)


def _mix_out_values(hf, hb, gate, of, ob, z, gn):
    hr = hf + hb
    y_rg = hr * _gelu(gate)
    osum = of + ob
    parts = []
    for h in range(NH):
        sl = slice(DH * h, DH * (h + 1))
        oh = osum[:, sl]
        r, ohat = _rms(oh)
        zh = z[:, sl]
        parts.append((r, ohat, zh))
    y_gdn = jnp.concatenate([ohat * gn * (zh * _sig(zh)) for (r, ohat, zh) in parts], axis=1)
    return hr, y_rg, y_gdn, parts


def _outproj(x1, hf, hb, gate, of, ob, z, gn, wout, name):
    s = x1.shape[0]
    t = min(256, s)

    def body(x_ref, hf_ref, hb_ref, gate_ref, of_ref, ob_ref, z_ref, gn_ref, w_ref, xo_ref, y_ref):
        _, y_rg, y_gdn, _ = _mix_out_values(hf_ref[...], hb_ref[...], gate_ref[...], of_ref[...], ob_ref[...],
                                            z_ref[...], gn_ref[...])
        y = jnp.concatenate([y_rg, y_gdn], axis=1).astype(BF16)
        y_ref[...] = y
        xo_ref[...] = x_ref[...] + jnp.dot(y, w_ref[...], preferred_element_type=F32)

    return pl.pallas_call(
        body, name=name, grid=(s // t,),
        in_specs=[_rows(t, D)] + [_rows(t, RGW)] * 6 + [_full((1, DH)), _full((D, D))],
        out_specs=[_rows(t, D), _rows(t, D)], out_shape=[_sds((s, D)), _sds((s, D), BF16)],
        compiler_params=_cparams(1),
    )(x1, hf, hb, gate, of, ob, z, gn, wout)


def _outproj_bwd(dx2, hf, hb, gate, of, ob, z, gn, wout, name):
    s = dx2.shape[0]
    t = min(256, s)

    def body(d_ref, hf_ref, hb_ref, gate_ref, of_ref, ob_ref, z_ref, gn_ref, w_ref,
             dhr_ref, dgate_ref, dos_ref, dz_ref, dgn_ref, db_ref):
        @pl.when(pl.program_id(0) == 0)
        def _():
            dgn_ref[...] = jnp.zeros_like(dgn_ref)

        gate = gate_ref[...]
        gn_v = gn_ref[...]
        hr, _, _, parts = _mix_out_values(hf_ref[...], hb_ref[...], gate, of_ref[...], ob_ref[...], z_ref[...], gn_v)
        dbf = d_ref[...].astype(BF16)
        db_ref[...] = dbf
        dy = _dot_nt(dbf, w_ref[...])
        dyr = dy[:, :RGW]
        dhr_ref[...] = dyr * _gelu(gate)
        dgate_ref[...] = dyr * hr * _gelu_grad(gate)
        dgn = jnp.zeros((1, DH), F32)
        for h, (r, ohat, zh) in enumerate(parts):
            sl = slice(DH * h, DH * (h + 1))
            dyh = dy[:, RGW + DH * h:RGW + DH * (h + 1)]
            sz = zh * _sig(zh)
            dn = dyh * sz
            dz_ref[:, sl] = dyh * ohat * gn_v * _silu_grad(zh)
            dgn = dgn + _colsum(dn * ohat)
            dos_ref[:, sl] = _rms_bwd(dn, ohat, r, gn_v)
        dgn_ref[...] += dgn

    return pl.pallas_call(
        body, name=name, grid=(s // t,),
        in_specs=[_rows(t, D)] + [_rows(t, RGW)] * 6 + [_full((1, DH)), _full((D, D))],
        out_specs=[_rows(t, RGW)] * 4 + [_full((1, DH)), _rows(t, D)],
        out_shape=[_sds((s, RGW))] * 4 + [_sds((1, DH)), _sds((s, D), BF16)],
        compiler_params=_cparams(1),
    )(dx2, hf, hb, gate, of, ob, z, gn, wout)


def _loss_head(x3, target, gain, name):
    s = x3.shape[0]
    t = min(256, s)

    def body(x_ref, t_ref, g_ref, dx_ref, loss_ref, dg_ref):
        @pl.when(pl.program_id(0) == 0)
        def _():
            loss_ref[...] = jnp.zeros_like(loss_ref)
            dg_ref[...] = jnp.zeros_like(dg_ref)

        r, xh = _rms(x_ref[...])
        gv = g_ref[...]
        err = xh * gv - t_ref[...]
        per_tok = jnp.mean(err * err, axis=-1, keepdims=True)
        loss_ref[...] += 0.5 * jnp.sum(per_tok, axis=0, keepdims=True)
        dy = err * (1.0 / D)
        dg_ref[...] += _colsum(dy * xh)
        dx_ref[...] = _rms_bwd(dy, xh, r, gv)

    return pl.pallas_call(
        body, name=name, grid=(s // t,), in_specs=[_rows(t, D), _rows(t, D), _full((1, D))],
        out_specs=[_rows(t, D), _full((8, 128)), _full((1, D))],
        out_shape=[_sds((s, D)), _sds((8, 128)), _sds((1, D))], compiler_params=_cparams(1),
    )(x3, target, gain)


def _adamw_math(wv, gv, mv, vv):
    mn = ADAM_B1 * mv + (1.0 - ADAM_B1) * gv
    vn = ADAM_B2 * vv + (1.0 - ADAM_B2) * (gv * gv)
    m_hat = mn / (1.0 - ADAM_B1 ** ADAM_STEP)
    v_hat = vn / (1.0 - ADAM_B2 ** ADAM_STEP)
    return -ADAM_LR * (m_hat / (jnp.sqrt(v_hat) + ADAM_EPS) + ADAM_WD * wv), mn, vn


def _row_tile(r, c):
    tr = r
    while tr * c * 4 > (1 << 20) and tr % 16 == 0:
        tr //= 2
    return tr


def _adamw(w, g, m, v, name):
    r, c = w.shape
    tr = _row_tile(r, c)

    def body(w_ref, g_ref, m_ref, v_ref, d_ref, nm_ref, nv_ref):
        d_ref[...], nm_ref[...], nv_ref[...] = _adamw_math(w_ref[...], g_ref[...], m_ref[...], v_ref[...])

    return pl.pallas_call(
        body, name=name, grid=(r // tr,), in_specs=[_rows(tr, c)] * 4, out_specs=[_rows(tr, c)] * 3,
        out_shape=[_sds((r, c))] * 3, compiler_params=_cparams(1),
    )(w, g, m, v)


def _adamw_halves(w, own, recv, m, v, c_arr, name):
    r, c = w.shape
    h = r // 2
    tr = _row_tile(h, c)
    nh = h // tr

    def body(c_ref, w_ref, own_ref, recv_ref, m_ref, v_ref, g_ref, d_ref, nm_ref, nv_ref):
        first_half = pl.program_id(0) < nh
        use_own = first_half == (c_ref[0] == 0)
        gv = jnp.where(use_own, own_ref[...], recv_ref[...])
        g_ref[...] = gv
        d_ref[...], nm_ref[...], nv_ref[...] = _adamw_math(w_ref[...], gv, m_ref[...], v_ref[...])

    full = pl.BlockSpec((tr, c), lambda i, c_ref: (i, 0))
    half = pl.BlockSpec((tr, c), lambda i, c_ref: (i % nh, 0))
    return pl.pallas_call(
        body, name=name, out_shape=[_sds((r, c))] * 4,
        grid_spec=pltpu.PrefetchScalarGridSpec(
            num_scalar_prefetch=1, grid=(2 * nh,), in_specs=[full, half, half, full, full], out_specs=[full] * 4),
        compiler_params=_cparams(1),
    )(c_arr, w, own, recv, m, v)


def _mesh_pos():
    return lax.axis_index("x"), lax.axis_index("y"), lax.axis_index("c")


def _other_chips(x, y):
    return [(1 - x, y), (x, 1 - y), (1 - x, 1 - y)]


def _all_gather(n_arr, space, out_shapes, block_of, name):
    def body(*refs):
        x_refs, out_refs = refs[:n_arr], refs[n_arr:2 * n_arr]
        send_sems, recv_sems, local_sems = refs[2 * n_arr:]
        x, y, c = _mesh_pos()
        me, sibling = (x, y, c), (x, y, 1 - c)
        chips = _other_chips(x, y)

        def slot(a, px, py, pc):
            return out_refs[a].at[4 * px + 2 * py + pc]

        def copy(a, k, block, to, src=None):
            return pltpu.make_async_remote_copy(
                src_ref=slot(a, *block) if src is None else src, dst_ref=slot(a, *block),
                send_sem=send_sems.at[7 * a + k], recv_sem=recv_sems.at[7 * a + k], device_id=to, device_id_type=MESH)

        srcs = [block_of(a, x_refs[a], c) for a in range(n_arr)]
        local = [pltpu.make_async_copy(srcs[a], slot(a, *me), local_sems.at[a]) for a in range(n_arr)]
        for cp in local:
            cp.start()
        first = []
        for a in range(n_arr):
            first += [copy(a, 1 + j, me, (*chip, c), src=srcs[a]) for j, chip in enumerate(chips)]
            first.append(copy(a, 0, me, sibling, src=srcs[a]))
        for cp in first:
            cp.start()
        passed = []
        for j, chip in enumerate(chips):
            for a in range(n_arr):
                copy(a, 1 + j, (*chip, c), me).wait_recv()
                fwd = copy(a, 4 + j, (*chip, c), sibling)
                fwd.start()
                passed.append(fwd)
        for a in range(n_arr):
            copy(a, 0, sibling, me).wait_recv()
            for j, chip in enumerate(chips):
                copy(a, 4 + j, (*chip, 1 - c), me).wait_recv()
        for cp in first + passed:
            cp.wait_send()
        for cp in local:
            cp.wait()

    return pl.pallas_call(
        body, name=name, out_shape=out_shapes,
        in_specs=[pl.BlockSpec(memory_space=space)] * n_arr, out_specs=[pl.BlockSpec(memory_space=space)] * n_arr,
        scratch_shapes=[pltpu.SemaphoreType.DMA((7 * n_arr,)), pltpu.SemaphoreType.DMA((7 * n_arr,)),
                        pltpu.SemaphoreType.DMA((n_arr,))],
    )


def _gather_weights(shards):
    halves = [w.shape[0] // 2 for w in shards]

    def block_of(a, x_ref, c):
        return x_ref.at[pl.ds(pl.multiple_of(c * halves[a], 16), halves[a]), :]

    outs = _all_gather(len(shards), pltpu.HBM, [_sds((8, h, w.shape[1]), BF16) for h, w in zip(halves, shards)],
                       block_of, "gather_weights")(*shards)
    return [o.reshape(NSH, 2 * h, o.shape[2]) for o, h in zip(outs, halves)]


def _gather_small(block, name):
    r, c = block.shape
    return _all_gather(1, pltpu.VMEM, [_sds((8, r, c))], lambda a, x_ref, c_: x_ref, name)(block)[0]


def _sibling_exchange(gs):
    n = len(gs)
    halves = [g.shape[1] // 2 for g in gs]

    def body(*refs):
        g_refs, land_refs = refs[:n], refs[n:2 * n]
        send_sems, recv_sems = refs[2 * n:]
        x, y, c = _mesh_pos()
        copies = []
        for a in range(n):
            h = halves[a]
            for s in range(NSH):
                copies.append(pltpu.make_async_remote_copy(
                    src_ref=g_refs[a].at[s, pl.ds(pl.multiple_of((1 - c) * h, 8), h), :], dst_ref=land_refs[a].at[s],
                    send_sem=send_sems.at[NSH * a + s], recv_sem=recv_sems.at[NSH * a + s],
                    device_id=(x, y, 1 - c), device_id_type=MESH))
        for cp in copies:
            cp.start()
        for cp in copies:
            cp.wait()

    return pl.pallas_call(
        body, name="grad_sibling_exchange", out_shape=[_sds((NSH, h, g.shape[2])) for h, g in zip(halves, gs)],
        in_specs=[pl.BlockSpec(memory_space=pltpu.HBM)] * n, out_specs=[pl.BlockSpec(memory_space=pltpu.HBM)] * n,
        scratch_shapes=[pltpu.SemaphoreType.DMA((NSH * n,)), pltpu.SemaphoreType.DMA((NSH * n,))],
    )(*gs)


def _chip_sum(g, land, c_arr, name):
    _, h, cols = land.shape

    def body(c_ref, g_ref, l_ref, o_ref):
        o_ref[...] = (g_ref[...] + l_ref[...]).astype(BF16)

    return pl.pallas_call(
        body, name=name, out_shape=_sds((NSH, h, cols), BF16),
        grid_spec=pltpu.PrefetchScalarGridSpec(
            num_scalar_prefetch=1, grid=(NSH,),
            in_specs=[pl.BlockSpec((1, h, cols), lambda s, c_ref: (s, c_ref[0], 0)),
                      pl.BlockSpec((1, h, cols), lambda s, c_ref: (s, 0, 0))],
            out_specs=pl.BlockSpec((1, h, cols), lambda s, c_ref: (s, 0, 0))),
        compiler_params=_cparams(1),
    )(c_arr, g, land)


def _chip_scatter(parts):
    n = len(parts)

    def body(*refs):
        p_refs, land_refs = refs[:n], refs[n:2 * n]
        send_sems, recv_sems, local_sems = refs[2 * n:]
        x, y, c = _mesh_pos()
        my_chip = 2 * x + y
        local = [pltpu.make_async_copy(p_refs[a].at[my_chip], land_refs[a].at[my_chip], local_sems.at[a]) for a in range(n)]
        for cp in local:
            cp.start()
        copies = []
        for a in range(n):
            for j, (px, py) in enumerate(_other_chips(x, y)):
                copies.append(pltpu.make_async_remote_copy(
                    src_ref=p_refs[a].at[2 * px + py], dst_ref=land_refs[a].at[my_chip],
                    send_sem=send_sems.at[3 * a + j], recv_sem=recv_sems.at[3 * a + j],
                    device_id=(px, py, c), device_id_type=MESH))
        for cp in copies:
            cp.start()
        for cp in copies:
            cp.wait()
        for cp in local:
            cp.wait()

    return pl.pallas_call(
        body, name="grad_chip_scatter", out_shape=[_sds(p.shape, BF16) for p in parts],
        in_specs=[pl.BlockSpec(memory_space=pltpu.HBM)] * n, out_specs=[pl.BlockSpec(memory_space=pltpu.HBM)] * n,
        scratch_shapes=[pltpu.SemaphoreType.DMA((3 * n,)), pltpu.SemaphoreType.DMA((3 * n,)), pltpu.SemaphoreType.DMA((n,))],
    )(*parts)


def _sum_slots(land, name):
    k, r, c = land.shape
    tr = r // 2 if r % 32 == 0 else r

    def body(l_ref, o_ref):
        acc = l_ref[0].astype(F32)
        for i in range(1, k):
            acc = acc + l_ref[i].astype(F32)
        o_ref[...] = acc

    return pl.pallas_call(
        body, name=name, grid=(r // tr,), in_specs=[pl.BlockSpec((k, tr, c), lambda i: (0, i, 0))],
        out_specs=_rows(tr, c), out_shape=_sds((r, c)), compiler_params=_cparams(1),
    )(land)


def _sibling_swap(halves):
    n = len(halves)

    def body(*refs):
        h_refs, out_refs = refs[:n], refs[n:2 * n]
        send_sems, recv_sems = refs[2 * n:]
        x, y, c = _mesh_pos()
        copies = [pltpu.make_async_remote_copy(
            src_ref=h_refs[a], dst_ref=out_refs[a], send_sem=send_sems.at[a], recv_sem=recv_sems.at[a],
            device_id=(x, y, 1 - c), device_id_type=MESH) for a in range(n)]
        for cp in copies:
            cp.start()
        for cp in copies:
            cp.wait()

    return pl.pallas_call(
        body, name="grad_sibling_swap", out_shape=[_sds(h.shape) for h in halves],
        in_specs=[pl.BlockSpec(memory_space=pltpu.HBM)] * n, out_specs=[pl.BlockSpec(memory_space=pltpu.HBM)] * n,
        scratch_shapes=[pltpu.SemaphoreType.DMA((n,)), pltpu.SemaphoreType.DMA((n,))],
    )(*halves)


def _pad_rows(v, width):
    flat = v.reshape(-1)
    rows = -(-flat.shape[0] // width)
    rows = -(-rows // 8) * 8
    return jnp.pad(flat, (0, rows * width - flat.shape[0])).reshape(rows, width)


def _size(shape):
    n = 1
    for dim in shape:
        n *= dim
    return n


def _row_pack(arrs):
    pieces = []
    for a in arrs:
        rows = -(-a.size // D)
        pieces.append(jnp.pad(a.reshape(-1), (0, rows * D - a.size)).reshape(rows, D))
    total = sum(p.shape[0] for p in pieces)
    if total % 8:
        pieces.append(jnp.zeros((8 - total % 8, D), F32))
    return jnp.concatenate(pieces, axis=0)


def _row_unpack(packed, shapes):
    out, r0 = [], 0
    for shp in shapes:
        n = _size(shp)
        rows = -(-n // D)
        out.append(packed[r0:r0 + rows].reshape(-1)[:n].reshape(shp))
        r0 += rows
    return out


def _block_diag(w):
    eye = jnp.eye(8, dtype=w.dtype)
    return (w[:, :, None, :] * eye[:, None, :, None]).reshape(RGW, RGW)


def _diag_blocks(dense):
    r = dense.reshape(8, 64, 8, 64)
    return jnp.stack([r[n, :, n, :] for n in range(8)])


def _lane_row(v8):
    return jnp.zeros((1, BAP), F32).at[0, 8:16].set(v8.reshape(8))


def _local_step(x, target, wts):
    (g1, wg1, wu1, wd1, gmix, w_in_groups, wout, rg_cw8, rg_cb, wgates, gbias, lam_row, gdn_cw8,
     alog_row, dtb_row, gn, g2, wg2, wu2, wd2, gfin) = wts
    s = x.shape[0]

    x1, a1, b1 = _ffn_fwd(x, g1, wg1, wu1, wd1, "ffn1_fwd")
    h2, p_rgx, p_gate, p_qkv, p_z, p_ba = _inproj(x1, gmix, w_in_groups, "in_proj")
    c_rg = _conv(p_rgx, rg_cw8, rg_cb, "rg_conv")
    c_qkv = _conv(p_qkv, gdn_cw8, jnp.zeros((1, QKVW), F32), "gdn_conv")
    a0, bb0, a1s, bb1, q, k, v, bg = _mix_prep(c_rg, c_qkv, p_ba, wgates, gbias, lam_row, alog_row, dtb_row, "mix_prep")
    hf, hb = _scan_pair(a0, bb0, a1s, bb1, False, "rg_scan")
    tmat, gu, gw, gqd, gkd, gat, gcd = _gdn_local_fwd(q, k, v, bg, "gdn_local_fwd")
    of, s0, vn0, ob, s1, vn1 = _gdn_seq_fwd(gu, gw, gqd, gkd, gat, gcd, "gdn_seq_fwd")
    x2, ymix = _outproj(x1, hf, hb, p_gate, of, ob, p_z, gn, wout, "out_proj")
    x3, a2, b2 = _ffn_fwd(x2, g2, wg2, wu2, wd2, "ffn2_fwd")
    dx3, loss_blk, d_gfin = _loss_head(x3, target, gfin, "loss_head")

    dx2, d_g2, hb2, dob2, fb2, dab2, dbb2 = _ffn_bwd(x2, dx3, g2, a2, b2, wg2, wu2, wd2, "ffn2_bwd")
    d_wg2 = _tn(hb2, dab2, "ffn2_dwg")
    d_wu2 = _tn(hb2, dbb2, "ffn2_dwu")
    d_wd2 = _tn(fb2, dob2, "ffn2_dwd")

    d_hr, d_gate, d_os, d_z, d_gn, dx2b = _outproj_bwd(dx2, hf, hb, p_gate, of, ob, p_z, gn, wout, "out_proj_bwd")
    d_wout = _tn(ymix, dx2b, "dw_out")[0]

    lam1, lam0 = _scan_pair(a1s, d_hr, a0, d_hr, True, "rg_scan_bwd")
    d_xc, d_pre, xcb, d_gbias, d_lam = _gates_bwd(c_rg, wgates, gbias, lam_row, lam0, lam1, hf, hb, "rg_gates_bwd")
    d_wgates = _tn(xcb, d_pre, "dw_gates")[0]
    d_prgx, d_rgcw8, d_rgcb = _conv_bwd(p_rgx, d_xc, rg_cw8, "rg_conv_bwd")

    sg = _gdn_seq_bwd(d_os, gw, gqd, gkd, gat, gcd, (s0, s1), (vn0, vn1), "gdn_seq_bwd")
    dq, dk, dv, dbg = _gdn_local_bwd(q, k, v, bg, tmat, d_os, (vn0, vn1), (sg[0:5], sg[5:10]), "gdn_local_bwd")
    d_cqkv, d_pba, d_alog, d_dtb = _prep_bwd(c_qkv, p_ba, alog_row, dtb_row, dq, dk, dv, dbg, "gdn_prep_bwd")
    d_pqkv, d_gdncw8, _ = _conv_bwd(p_qkv, d_cqkv, gdn_cw8, "gdn_conv_bwd")

    dps = (d_prgx, d_gate, d_pqkv, d_z, d_pba)
    dx1, d_gmix = _inproj_bwd(x1, dx2, gmix, dps, w_in_groups, "in_proj_bwd")
    d_win_groups = [_tn(h2, dp, "dw_in_%d" % i)[0] for i, dp in enumerate(dps)]

    gx, d_g1, hb1, dob1, fb1, dab1, dbb1 = _ffn_bwd(x, dx1, g1, a1, b1, wg1, wu1, wd1, "ffn1_bwd")
    d_wg1 = _tn(hb1, dab1, "ffn1_dwg")
    d_wu1 = _tn(hb1, dbb1, "ffn1_dwu")
    d_wd1 = _tn(fb1, dob1, "ffn1_dwd")

    d_win = jnp.concatenate(d_win_groups[:4] + [d_win_groups[4][:, :BAW]], axis=1)
    big = (d_wg1, d_wu1, d_wd1, d_win, d_wout, d_wg2, d_wu2, d_wd2)
    small = dict(
        ffn1_norm=d_g1, mix_norm=d_gmix, rg_conv_w=d_rgcw8[:4], rg_conv_b=d_rgcb,
        rg_gate_a_w=jnp.stack([_diag_blocks(d_wgates[:, RGW * i:RGW * (i + 1)]) for i in (0, 1)]),
        rg_gate_x_w=jnp.stack([_diag_blocks(d_wgates[:, RGW * i:RGW * (i + 1)]) for i in (2, 3)]),
        rg_gate_a_b=d_gbias[0, :2 * RGW].reshape(2, RGW), rg_gate_x_b=d_gbias[0, 2 * RGW:].reshape(2, RGW),
        rg_lambda=d_lam.reshape(2, RGW), gdn_conv_w=d_gdncw8[:4],
        gdn_a_log=d_alog[0, 8:16].reshape(2, NH), gdn_dt_bias=d_dtb[0, 8:16].reshape(2, NH),
        gdn_norm=d_gn, ffn2_norm=d_g2, final_norm=d_gfin)
    return loss_blk, gx, big, small


_SMALL_NAMES = ("ffn1_norm", "mix_norm", "rg_conv_w", "rg_conv_b", "rg_gate_a_w", "rg_gate_a_b", "rg_gate_x_w",
                "rg_gate_x_b", "rg_lambda", "gdn_conv_w", "gdn_a_log", "gdn_dt_bias", "gdn_norm", "ffn2_norm", "final_norm")
_SMALL_SHARDED = dict(rg_conv_w=128, rg_gate_a_b=128, rg_gate_x_b=128, rg_lambda=128, gdn_conv_w=384)
_OUT_ORDER = ("ffn1_norm", "ffn1_w_gate", "ffn1_w_up", "ffn1_w_down", "mix_norm", "w_in", "w_out", "rg_conv_w", "rg_conv_b",
              "rg_gate_a_w", "rg_gate_a_b", "rg_gate_x_w", "rg_gate_x_b", "rg_lambda", "gdn_conv_w", "gdn_a_log",
              "gdn_dt_bias", "gdn_norm", "ffn2_norm", "ffn2_w_gate", "ffn2_w_up", "ffn2_w_down", "final_norm")
_BIG_NAMES = ("ffn1_w_gate", "ffn1_w_up", "ffn1_w_down", "w_in", "w_out", "ffn2_w_gate", "ffn2_w_up", "ffn2_w_down")


def kernel(x, ffn1_norm, ffn1_w_gate, ffn1_w_up, ffn1_w_down, mix_norm, w_in, w_out, rg_conv_w, rg_conv_b, rg_gate_a_w, rg_gate_a_b, rg_gate_x_w, rg_gate_x_b, rg_lambda, gdn_conv_w, gdn_a_log, gdn_dt_bias, gdn_norm, ffn2_norm, ffn2_w_gate, ffn2_w_up, ffn2_w_down, final_norm, loss_target, m_ffn1_norm, m_ffn1_w_gate, m_ffn1_w_up, m_ffn1_w_down, m_mix_norm, m_w_in, m_w_out, m_rg_conv_w, m_rg_conv_b, m_rg_gate_a_w, m_rg_gate_a_b, m_rg_gate_x_w, m_rg_gate_x_b, m_rg_lambda, m_gdn_conv_w, m_gdn_a_log, m_gdn_dt_bias, m_gdn_norm, m_ffn2_norm, m_ffn2_w_gate, m_ffn2_w_up, m_ffn2_w_down, m_final_norm, v_ffn1_norm, v_ffn1_w_gate, v_ffn1_w_up, v_ffn1_w_down, v_mix_norm, v_w_in, v_w_out, v_rg_conv_w, v_rg_conv_b, v_rg_gate_a_w, v_rg_gate_a_b, v_rg_gate_x_w, v_rg_gate_x_b, v_rg_lambda, v_gdn_conv_w, v_gdn_a_log, v_gdn_dt_bias, v_gdn_norm, v_ffn2_norm, v_ffn2_w_gate, v_ffn2_w_up, v_ffn2_w_down, v_final_norm):
    args = dict(locals())
    w = {n: args[n] for n in _OUT_ORDER}
    mom = {n: args["m_" + n] for n in _OUT_ORDER}
    var = {n: args["v_" + n] for n in _OUT_ORDER}
    xi, yi, ci = _mesh_pos()
    shard = 2 * xi + yi

    wg1, wu1, wd1, win_sh, wout_sh, wg2, wu2, wd2 = _gather_weights([w[n][0].astype(BF16) for n in _BIG_NAMES])
    w_in_full = jnp.transpose(win_sh, (1, 0, 2)).reshape(D, NSH * INSH)
    w_out_full = wout_sh.reshape(D, D)
    sm_local = _pad_rows(jnp.concatenate([w[n][0].reshape(-1) for n in _SMALL_SHARDED]), 128)
    sm_all = _gather_small(sm_local, "gather_small_weights")[0::2].reshape(NSH, -1)
    sm_full, off = {}, 0
    for n, wd_ in _SMALL_SHARDED.items():
        rows = w[n].shape[1]
        piece = sm_all[:, off:off + rows * wd_].reshape(NSH, rows, wd_)
        sm_full[n] = jnp.transpose(piece, (1, 0, 2)).reshape(rows, NSH * wd_)
        off += rows * wd_

    w_in_groups = (w_in_full[:, 0:512], w_in_full[:, 512:1024], w_in_full[:, 1024:2560], w_in_full[:, 2560:3072],
                   jnp.pad(w_in_full[:, 3072:3088], ((0, 0), (0, BAP - BAW))))
    wa, wx = rg_gate_a_w[0], rg_gate_x_w[0]
    wgates = jnp.concatenate([_block_diag(wa[0]), _block_diag(wa[1]), _block_diag(wx[0]), _block_diag(wx[1])],
                             axis=1).astype(BF16)
    gbias = jnp.concatenate([sm_full["rg_gate_a_b"].reshape(1, -1), sm_full["rg_gate_x_b"].reshape(1, -1)], axis=1)
    wts = (ffn1_norm, wg1, wu1, wd1, mix_norm, w_in_groups, w_out_full,
           jnp.pad(sm_full["rg_conv_w"], ((0, 4), (0, 0))), rg_conv_b, wgates, gbias, sm_full["rg_lambda"].reshape(1, -1),
           jnp.pad(sm_full["gdn_conv_w"], ((0, 4), (0, 0))), _lane_row(gdn_a_log), _lane_row(gdn_dt_bias),
           gdn_norm, ffn2_norm, wg2, wu2, wd2, final_norm.reshape(1, D))

    loss_blk, gx, big, small = _local_step(x[0], loss_target[0], wts)
    loss = lax.psum(loss_blk[0, 0], ("x", "y", "c"))

    d_wg1, d_wu1, d_wd1, d_win, d_wout, d_wg2, d_wu2, d_wd2 = big
    gs = [d_wg1, d_wu1, d_wd1, jnp.transpose(d_win.reshape(D, NSH, INSH), (1, 0, 2)), d_wout.reshape(NSH, OUTSH, D),
          d_wg2, d_wu2, d_wd2]
    lands = _sibling_exchange(gs)
    c_arr = ci.reshape(1).astype(jnp.int32)
    parts = [_chip_sum(g, l, c_arr, "chip_sum_" + n) for g, l, n in zip(gs, lands, _BIG_NAMES)]
    halves = [_sum_slots(l, "sum_chips_" + n) for l, n in zip(_chip_scatter(parts), _BIG_NAMES)]
    grads = {}

    sm_grad = _row_pack([small[n] for n in _SMALL_NAMES])
    sm_sum = _sum_slots(_gather_small(sm_grad, "gather_small_grads"), "small_grad_sum")
    for n, g in zip(_SMALL_NAMES, _row_unpack(sm_sum, [small[n].shape for n in _SMALL_NAMES])):
        if n in _SMALL_SHARDED:
            wd_ = _SMALL_SHARDED[n]
            g = lax.dynamic_slice_in_dim(g, shard * wd_, wd_, axis=1)
        grads[n] = g.reshape(w[n].shape)

    delta, new_m, new_v = {}, {}, {}
    for n, own, recv in zip(_BIG_NAMES, halves, _sibling_swap(halves)):
        shp = w[n].shape
        outs4 = _adamw_halves(w[n][0], own, recv, mom[n][0], var[n][0], c_arr, "adamw_" + n)
        grads[n], delta[n], new_m[n], new_v[n] = [o.reshape(shp) for o in outs4]
    packs = [_row_pack([t[n] for n in _SMALL_NAMES]) for t in (w, grads, mom, var)]
    sm_shapes = [w[n].shape for n in _SMALL_NAMES]
    for dst, src in zip((delta, new_m, new_v), _adamw(*packs, "adamw_small")):
        for n, val in zip(_SMALL_NAMES, _row_unpack(src, sm_shapes)):
            dst[n] = val

    outs = [loss, gx[None]]
    for group in (grads, delta, new_m, new_v):
        outs += [group[n] for n in _OUT_ORDER]
    return tuple(outs)
```

```python
import functools

import jax
import jax.numpy as jnp
from jax import lax
from jax.experimental import pallas as pl
from jax.experimental.pallas import tpu as pltpu

F32 = jnp.float32
BF16 = jnp.bfloat16
EPS = 1e-6
D = 1024
NSH = 4
FSH = 704
RGW = 512
QKVW = 1536
ZW = 512
BAW = 16
BAP = 128
INSH = 772
OUTSH = 256
CHUNK = 64
NH = 4
DH = 128
RG_C = 8.0
VMEM_LIMIT = 52 * 1024 * 1024
MESH = pl.DeviceIdType.MESH

ADAM_LR = 0.001
ADAM_B1 = 0.9
ADAM_B2 = 0.999
ADAM_EPS = 1e-08
ADAM_WD = 0.01
ADAM_STEP = 10


def _cparams(n_grid):
    return pltpu.CompilerParams(dimension_semantics=("arbitrary",) * n_grid, vmem_limit_bytes=VMEM_LIMIT)


def _sig(x):
    return 0.5 + 0.5 * jnp.tanh(0.5 * x)


def _sig_pos(x):
    return 1.0 / (1.0 + jnp.exp(-x))


def _softplus(x):
    return jnp.maximum(x, 0.0) + jnp.log(1.0 + jnp.exp(-jnp.abs(x)))


def _neg_expm1(y):
    series = -y * (1.0 + y * (0.5 + y * (1.0 / 6 + y * (1.0 / 24 + y * (1.0 / 120 + y * (1.0 / 720 + y / 5040))))))
    return jnp.where(y > -0.3, series, 1.0 - jnp.exp(y))


_GELU_C = 0.7978845608028654


def _gelu(x):
    t = jnp.tanh(_GELU_C * (x + 0.044715 * x * x * x))
    return 0.5 * x * (1.0 + t)


def _gelu_grad(x):
    t = jnp.tanh(_GELU_C * (x + 0.044715 * x * x * x))
    return 0.5 * (1.0 + t) + 0.5 * x * (1.0 - t * t) * _GELU_C * (1.0 + 3 * 0.044715 * x * x)


def _silu_grad(x):
    s = _sig(x)
    return s * (1.0 + x * (1.0 - s))


def _dot(a, b):
    return jnp.dot(a.astype(BF16), b.astype(BF16), preferred_element_type=F32)


def _dot_nt(a, b):
    return lax.dot_general(a.astype(BF16), b.astype(BF16), (((1,), (1,)), ((), ())), preferred_element_type=F32)


def _dot_tn(a, b):
    return lax.dot_general(a.astype(BF16), b.astype(BF16), (((0,), (0,)), ((), ())), preferred_element_type=F32)


_NN = ((1,), (0,))
_NT = ((1,), (1,))
_TN = ((0,), (0,))


def _dg(a, b, dims):
    return lax.dot_general(a, b, (dims, ((), ())), preferred_element_type=F32)


def _split2(a):
    hi = a.astype(BF16)
    return hi, (a - hi.astype(F32)).astype(BF16)


def _dot3(a, b, dims=_NN):
    ah, al = _split2(a)
    bh, bl = _split2(b)
    return _dg(ah, bh, dims) + _dg(ah, bl, dims) + _dg(al, bh, dims)


def _dot_exact(e, x, dims, e_is_lhs):
    x0 = x.astype(BF16)
    r = x - x0.astype(F32)
    x1 = r.astype(BF16)
    x2 = (r - x1.astype(F32)).astype(BF16)
    eb = e.astype(BF16)
    if e_is_lhs:
        return _dg(eb, x0, dims) + _dg(eb, x1, dims) + _dg(eb, x2, dims)
    return _dg(x0, eb, dims) + _dg(x1, eb, dims) + _dg(x2, eb, dims)


def _rms(xv):
    r = lax.rsqrt(jnp.mean(xv * xv, axis=-1, keepdims=True) + EPS)
    return r, xv * r


def _rms_bwd(dy, xh, r, gain):
    dxh = dy * gain
    return r * (dxh - xh * jnp.mean(dxh * xh, axis=-1, keepdims=True))


def _colsum(v):
    return jnp.sum(v, axis=0, keepdims=True)


def _rows(t, c):
    return pl.BlockSpec((t, c), lambda i: (i, 0))


def _full(shape):
    n = len(shape)
    return pl.BlockSpec(shape, lambda i: (0,) * n)


def _sds(shape, dtype=F32):
    return jax.ShapeDtypeStruct(shape, dtype)


def _ffn_fwd(x, gain, wg, wu, wd, name, comm=None):
    s = x.shape[0]
    tm = min(512, s)

    def body(x_ref, g_ref, wg_ref, wu_ref, wd_ref, xo_ref, a_ref, b_ref, h_sc, acc):
        j = pl.program_id(1)

        @pl.when(j == 0)
        def _():
            _, xh = _rms(x_ref[...])
            h_sc[...] = (xh * g_ref[...]).astype(BF16)
            acc[...] = jnp.zeros_like(acc)

        h = h_sc[...]

        a = jnp.dot(h, wg_ref[0], preferred_element_type=F32)
        b = jnp.dot(h, wu_ref[0], preferred_element_type=F32)
        a_ref[0] = a.astype(BF16)
        b_ref[0] = b.astype(BF16)
        f = (a * _sig(a) * b).astype(BF16)
        acc[...] += jnp.dot(f, wd_ref[0], preferred_element_type=F32)

        @pl.when(j == NSH - 1)
        def _():
            xo_ref[...] = x_ref[...] + 0.5 * acc[...]

    return _pallas(
        body, comm, name=name, grid=(s // tm, NSH),
        in_specs=[pl.BlockSpec((tm, D), lambda i, j: (i, 0)), pl.BlockSpec((1, D), lambda i, j: (0, 0)),
                  pl.BlockSpec((1, D, FSH), lambda i, j: (j, 0, 0)), pl.BlockSpec((1, D, FSH), lambda i, j: (j, 0, 0)),
                  pl.BlockSpec((1, FSH, D), lambda i, j: (j, 0, 0))],
        out_specs=[pl.BlockSpec((tm, D), lambda i, j: (i, 0)), pl.BlockSpec((1, tm, FSH), lambda i, j: (j, i, 0)),
                   pl.BlockSpec((1, tm, FSH), lambda i, j: (j, i, 0))],
        out_shape=[_sds((s, D)), _sds((NSH, s, FSH), BF16), _sds((NSH, s, FSH), BF16)],
        scratch_shapes=[pltpu.VMEM((tm, D), BF16), pltpu.VMEM((tm, D), F32)],
        args=(x, gain, wg, wu, wd))


def _ffn_bwd(x, dout, gain, a, b, wg, wu, wd, name, comm=None):
    s = x.shape[0]
    tm = min(512, s)

    def body(x_ref, d_ref, g_ref, a_ref, b_ref, wg_ref, wu_ref, wd_ref,
             dx_ref, dg_ref, h_ref, do_ref, f_ref, da_ref, db_ref, do_sc, dh_acc):
        i = pl.program_id(0)
        j = pl.program_id(1)

        @pl.when(jnp.logical_and(i == 0, j == 0))
        def _():
            dg_ref[...] = jnp.zeros_like(dg_ref)

        @pl.when(j == 0)
        def _():
            _, xh = _rms(x_ref[...])
            h_ref[...] = (xh * g_ref[...]).astype(BF16)
            do = (0.5 * d_ref[...]).astype(BF16)
            do_sc[...] = do
            do_ref[...] = do
            dh_acc[...] = jnp.zeros_like(dh_acc)

        do = do_sc[...]

        df = _dot_nt(do, wd_ref[0])
        av = a_ref[0].astype(F32)
        bv = b_ref[0].astype(F32)
        sa = _sig(av)
        f_ref[0] = (av * sa * bv).astype(BF16)
        da = (df * bv * sa * (1.0 + av * (1.0 - sa))).astype(BF16)
        db = (df * av * sa).astype(BF16)
        da_ref[0] = da
        db_ref[0] = db
        dh_acc[...] += _dot_nt(da, wg_ref[0]) + _dot_nt(db, wu_ref[0])

        @pl.when(j == NSH - 1)
        def _():
            r, xh = _rms(x_ref[...])
            dh = dh_acc[...]
            dg_ref[...] += _colsum(dh * xh)
            dx_ref[...] = d_ref[...] + _rms_bwd(dh, xh, r, g_ref[...])

    tok = pl.BlockSpec((tm, D), lambda i, j: (i, 0))
    sh = pl.BlockSpec((1, tm, FSH), lambda i, j: (j, i, 0))
    return _pallas(
        body, comm, name=name, grid=(s // tm, NSH),
        in_specs=[tok, tok, pl.BlockSpec((1, D), lambda i, j: (0, 0)), sh, sh,
                  pl.BlockSpec((1, D, FSH), lambda i, j: (j, 0, 0)), pl.BlockSpec((1, D, FSH), lambda i, j: (j, 0, 0)),
                  pl.BlockSpec((1, FSH, D), lambda i, j: (j, 0, 0))],
        out_specs=[tok, pl.BlockSpec((1, D), lambda i, j: (0, 0)), tok, tok, sh, sh, sh],
        out_shape=[_sds((s, D)), _sds((1, D)), _sds((s, D), BF16), _sds((s, D), BF16),
                   _sds((NSH, s, FSH), BF16), _sds((NSH, s, FSH), BF16), _sds((NSH, s, FSH), BF16)],
        scratch_shapes=[pltpu.VMEM((tm, D), BF16), pltpu.VMEM((tm, D), F32)],
        args=(x, dout, gain, a, b, wg, wu, wd))


def _tn(a, b, name):
    a_g = a.ndim == 3
    b_g = b.ndim == 3
    g = a.shape[0] if a_g else (b.shape[0] if b_g else 1)
    s, k = a.shape[-2:]
    n = b.shape[-1]
    ts = min(1024, s)

    def body(a_ref, b_ref, o_ref):
        @pl.when(pl.program_id(1) == 0)
        def _():
            o_ref[...] = jnp.zeros_like(o_ref)

        av = a_ref[0] if a_g else a_ref[...]
        bv = b_ref[0] if b_g else b_ref[...]
        o_ref[0] += _dot_tn(av, bv)

    a_spec = pl.BlockSpec((1, ts, k), lambda gi, si: (gi, si, 0)) if a_g else pl.BlockSpec((ts, k), lambda gi, si: (si, 0))
    b_spec = pl.BlockSpec((1, ts, n), lambda gi, si: (gi, si, 0)) if b_g else pl.BlockSpec((ts, n), lambda gi, si: (si, 0))
    return pl.pallas_call(
        body, name=name, grid=(g, s // ts), in_specs=[a_spec, b_spec],
        out_specs=pl.BlockSpec((1, k, n), lambda gi, si: (gi, 0, 0)),
        out_shape=_sds((g, k, n)), compiler_params=_cparams(2),
    )(a, b)


_P_WIDTHS = (RGW, RGW, QKVW, ZW, BAP)


def _inproj(x1, gain, ws, name):
    s = x1.shape[0]
    tm = min(256, s)

    def body(x_ref, g_ref, *refs):
        w_refs = refs[:5]
        h_ref = refs[5]
        p_refs = refs[6:]
        _, xh = _rms(x_ref[...])
        h = (xh * g_ref[...]).astype(BF16)
        h_ref[...] = h
        for w_ref, p_ref in zip(w_refs, p_refs):
            p_ref[...] = jnp.dot(h, w_ref[...], preferred_element_type=F32)

    return pl.pallas_call(
        body, name=name, grid=(s // tm,),
        in_specs=[_rows(tm, D), _full((1, D))] + [_full((D, w)) for w in _P_WIDTHS],
        out_specs=[_rows(tm, D)] + [_rows(tm, w) for w in _P_WIDTHS],
        out_shape=[_sds((s, D), BF16)] + [_sds((s, w)) for w in _P_WIDTHS],
        compiler_params=_cparams(1),
    )(x1, gain, *ws)


def _inproj_bwd(x1, dx2, gain, dps, ws, name):
    s = x1.shape[0]
    tm = min(256, s)

    def body(x_ref, d_ref, g_ref, *refs):
        dp_refs = refs[:5]
        w_refs = refs[5:10]
        dx_ref, dg_ref = refs[10:]

        @pl.when(pl.program_id(0) == 0)
        def _():
            dg_ref[...] = jnp.zeros_like(dg_ref)

        dh = jnp.zeros((tm, D), F32)
        for dp_ref, w_ref in zip(dp_refs, w_refs):
            dh = dh + _dot_nt(dp_ref[...], w_ref[...])
        r, xh = _rms(x_ref[...])
        dg_ref[...] += _colsum(dh * xh)
        dx_ref[...] = d_ref[...] + _rms_bwd(dh, xh, r, g_ref[...])

    return pl.pallas_call(
        body, name=name, grid=(s // tm,),
        in_specs=[_rows(tm, D), _rows(tm, D), _full((1, D))] + [_rows(tm, w) for w in _P_WIDTHS]
        + [_full((D, w)) for w in _P_WIDTHS],
        out_specs=[_rows(tm, D), _full((1, D))],
        out_shape=[_sds((s, D)), _sds((1, D))],
        compiler_params=_cparams(1),
    )(x1, dx2, gain, *dps, *ws)


def _halo_specs(s, t, c):
    nb8 = s // 8
    tb = t // 8
    prev = pl.BlockSpec((8, c), lambda i: (jnp.maximum(i * tb - 1, 0), 0))
    nxt = pl.BlockSpec((8, c), lambda i: (jnp.minimum((i + 1) * tb, nb8 - 1), 0))
    return prev, nxt


def _edge_masks(nb):
    i = pl.program_id(0)
    return jnp.where(i > 0, 1.0, 0.0).astype(F32), jnp.where(i < nb - 1, 1.0, 0.0).astype(F32)


def _shifted(xx, off, t):
    n = t + 16
    sh = (-off) % n
    rolled = xx if sh == 0 else pltpu.roll(xx, sh, 0)
    return rolled[8:8 + t]


def _conv(x, w8, bias, name):
    s, c = x.shape
    t = min(256, s)
    nb = s // t

    def body(x_ref, xp_ref, xn_ref, w_ref, b_ref, o_ref):
        pm, nm = _edge_masks(nb)
        for c0 in range(0, c, 512):
            cols = slice(c0, c0 + 512)
            xx = jnp.concatenate([xp_ref[:, cols] * pm, x_ref[:, cols], xn_ref[:, cols] * nm], axis=0)
            acc = jnp.zeros((t, 512), F32) + b_ref[:, cols]
            for j in range(4):
                acc = acc + w_ref[j:j + 1, cols] * _shifted(xx, j - 2, t)
            o_ref[:, cols] = acc

    prev, nxt = _halo_specs(s, t, c)
    return pl.pallas_call(
        body, name=name, grid=(nb,),
        in_specs=[_rows(t, c), prev, nxt, _full((8, c)), _full((1, c))],
        out_specs=_rows(t, c), out_shape=_sds((s, c)), compiler_params=_cparams(1),
    )(x, x, x, w8, bias)


def _conv_bwd(x, dc, w8, name):
    s, c = x.shape
    t = min(256, s)
    nb = s // t

    def body(x_ref, d_ref, dp_ref, dn_ref, w_ref, dx_ref, dw_ref, db_ref):
        @pl.when(pl.program_id(0) == 0)
        def _():
            dw_ref[...] = jnp.zeros_like(dw_ref)
            db_ref[...] = jnp.zeros_like(db_ref)

        pm, nm = _edge_masks(nb)
        for c0 in range(0, c, 512):
            cols = slice(c0, c0 + 512)
            dd = jnp.concatenate([dp_ref[:, cols] * pm, d_ref[:, cols], dn_ref[:, cols] * nm], axis=0)
            xv = x_ref[:, cols]
            acc = jnp.zeros((t, 512), F32)
            for j in range(4):
                dsh = _shifted(dd, 2 - j, t)
                acc = acc + w_ref[j:j + 1, cols] * dsh
                dw_ref[j:j + 1, cols] += _colsum(dsh * xv)
            dx_ref[:, cols] = acc
            db_ref[:, cols] += _colsum(d_ref[:, cols])

    prev, nxt = _halo_specs(s, t, c)
    return pl.pallas_call(
        body, name=name, grid=(nb,),
        in_specs=[_rows(t, c), _rows(t, c), prev, nxt, _full((8, c))],
        out_specs=[_rows(t, c), _full((8, c)), _full((1, c))],
        out_shape=[_sds((s, c)), _sds((8, c)), _sds((1, c))], compiler_params=_cparams(1),
    )(x, dc, dc, dc, w8)


def _rg_gates(xc, pre, lam_row):
    sp8 = RG_C * _softplus(-lam_row)
    out = []
    for d in range(2):
        r = _sig_pos(pre[:, RGW * d:RGW * (d + 1)])
        gi = _sig(pre[:, 2 * RGW + RGW * d:2 * RGW + RGW * (d + 1)])
        la = -r * sp8[:, RGW * d:RGW * (d + 1)]
        a = jnp.exp(la)
        mult = jnp.sqrt(_neg_expm1(2.0 * la))
        out.append((r, gi, a, mult))
    return out


def _mix_prep(c_rg, c_qkv, p_ba, wgates, gbias, lam_row, alog_row, dtb_row, name):
    s = c_rg.shape[0]
    t = min(256, s)

    def body(xc_ref, cq_ref, pc_ref, wg_ref, gb_ref, lam_ref, alog_ref, dtb_ref,
             a0_ref, b0_ref, a1_ref, b1_ref, q_ref, k_ref, v_ref, bg_ref):
        xc = xc_ref[...]
        pre = _dot(xc, wg_ref[...]) + gb_ref[...]
        gates = _rg_gates(xc, pre, lam_ref[...])
        for (r, gi, a, mult), a_ref, b_ref in zip(gates, (a0_ref, a1_ref), (b0_ref, b1_ref)):
            a_ref[...] = a
            b_ref[...] = mult * gi * xc
        cq = cq_ref[...]
        sq = cq * _sig(cq)
        for h in range(NH):
            sl = slice(DH * h, DH * (h + 1))
            qh = sq[:, sl]
            q_ref[:, sl] = qh * lax.rsqrt(jnp.sum(qh * qh, axis=-1, keepdims=True) + EPS) * (DH ** -0.5)
            kh = sq[:, RGW + DH * h:RGW + DH * (h + 1)]
            k_ref[:, sl] = kh * lax.rsqrt(jnp.sum(kh * kh, axis=-1, keepdims=True) + EPS)
        v_ref[...] = sq[:, 2 * RGW:]
        pc = pc_ref[...]
        lane = lax.broadcasted_iota(jnp.int32, pc.shape, 1)
        beta = _sig(pc)
        g = -jnp.exp(alog_ref[...]) * _softplus(pc + dtb_ref[...])
        bg_ref[...] = jnp.where(lane < 8, beta, jnp.where(lane < 16, g, 0.0))

    return pl.pallas_call(
        body, name=name, grid=(s // t,),
        in_specs=[_rows(t, RGW), _rows(t, QKVW), _rows(t, BAP), _full((RGW, 4 * RGW)), _full((1, 4 * RGW)),
                  _full((1, 2 * RGW)), _full((1, BAP)), _full((1, BAP))],
        out_specs=[_rows(t, RGW)] * 7 + [_rows(t, BAP)],
        out_shape=[_sds((s, RGW))] * 7 + [_sds((s, BAP))],
        compiler_params=_cparams(1),
    )(c_rg, c_qkv, p_ba, wgates, gbias, lam_row, alog_row, dtb_row)


def _scan_pair(af, bf, ar, br, shifted, name):
    s, c = af.shape
    t = min(512, s)
    nb = s // t
    ng = t // 8
    tb = t // 8
    up = lambda i: (i, 0)
    down = lambda i: (nb - 1 - i, 0)

    def body(*refs):
        if shifted:
            af_ref, bf_ref, ar_ref, br_ref, afp_ref, arn_ref, hf_ref, hr_ref, carry, fbuf, rbuf = refs
        else:
            af_ref, bf_ref, ar_ref, br_ref, hf_ref, hr_ref, carry = refs
        i = pl.program_id(0)

        @pl.when(i == 0)
        def _():
            carry[...] = jnp.zeros_like(carry)

        if shifted:
            edge = jnp.where(i > 0, 1.0, 0.0).astype(F32)
            fbuf[0:8, :] = afp_ref[...] * edge
            fbuf[8:t + 8, :] = af_ref[...]
            rbuf[0:t, :] = ar_ref[...]
            rbuf[t:t + 8, :] = arn_ref[...] * edge
        row = lax.broadcasted_iota(jnp.int32, (8, c), 0)

        def block_scan(av, bv, downwards):
            for k in (1, 2, 4):
                sh = (8 - k) if downwards else k
                m = (row < 8 - k) if downwards else (row >= k)
                a_s = pltpu.roll(av, sh, 0)
                b_s = pltpu.roll(bv, sh, 0)
                bv = jnp.where(m, av * b_s + bv, bv)
                av = jnp.where(m, av * a_s, av)
            return av, bv

        def group(gi, cvs):
            cf, cr = cvs
            rf = pl.multiple_of(gi * 8, 8)
            rr = pl.multiple_of((ng - 1 - gi) * 8, 8)
            if shifted:
                a_f = jnp.where(row > 0, pltpu.roll(fbuf[pl.ds(rf + 8, 8), :], 1, 0), pltpu.roll(fbuf[pl.ds(rf, 8), :], 1, 0))
                a_r = jnp.where(row < 7, pltpu.roll(rbuf[pl.ds(rr, 8), :], 7, 0), pltpu.roll(rbuf[pl.ds(rr + 8, 8), :], 7, 0))
            else:
                a_f = af_ref[pl.ds(rf, 8), :]
                a_r = ar_ref[pl.ds(rr, 8), :]
            a_f, b_f = block_scan(a_f, bf_ref[pl.ds(rf, 8), :], False)
            a_r, b_r = block_scan(a_r, br_ref[pl.ds(rr, 8), :], True)
            h_f = a_f * cf + b_f
            h_r = a_r * cr + b_r
            hf_ref[pl.ds(rf, 8), :] = h_f
            hr_ref[pl.ds(rr, 8), :] = h_r
            return h_f[7:8, :], h_r[0:1, :]

        cf, cr = lax.fori_loop(0, ng, group, (carry[0:1, :], carry[8:9, :]))
        carry[0:1, :] = cf
        carry[8:9, :] = cr

    in_specs = [pl.BlockSpec((t, c), up), pl.BlockSpec((t, c), up), pl.BlockSpec((t, c), down), pl.BlockSpec((t, c), down)]
    args = [af, bf, ar, br]
    scratch = [pltpu.VMEM((16, c), F32)]
    if shifted:
        in_specs += [pl.BlockSpec((8, c), lambda i: (jnp.maximum(i * tb - 1, 0), 0)),
                     pl.BlockSpec((8, c), lambda i: (jnp.minimum((nb - i) * tb, s // 8 - 1), 0))]
        args += [af, ar]
        scratch += [pltpu.VMEM((t + 8, c), F32), pltpu.VMEM((t + 8, c), F32)]
    return pl.pallas_call(
        body, name=name, grid=(nb,), in_specs=in_specs,
        out_specs=[pl.BlockSpec((t, c), up), pl.BlockSpec((t, c), down)], out_shape=[_sds((s, c)), _sds((s, c))],
        scratch_shapes=scratch, compiler_params=_cparams(1),
    )(*args)


def _gates_bwd(xc, wgates, gbias, lam_row, lam0, lam1, hf, hb, name):
    s = xc.shape[0]
    t = min(256, s)
    nb = s // t

    def body(xc_ref, wg_ref, gb_ref, lam_ref, l0_ref, l1_ref, hf_ref, hfp_ref, hfn_ref, hb_ref, hbp_ref, hbn_ref,
             dxc_ref, dpre_ref, xcb_ref, dgb_ref, dlam_ref):
        @pl.when(pl.program_id(0) == 0)
        def _():
            dgb_ref[...] = jnp.zeros_like(dgb_ref)
            dlam_ref[...] = jnp.zeros_like(dlam_ref)

        pm, nm = _edge_masks(nb)
        h_prev = _shifted(jnp.concatenate([hfp_ref[...] * pm, hf_ref[...], hfn_ref[...] * nm], axis=0), -1, t)
        h_next = _shifted(jnp.concatenate([hbp_ref[...] * pm, hb_ref[...], hbn_ref[...] * nm], axis=0), 1, t)
        h_shift = (h_prev, h_next)
        xv = xc_ref[...]
        pre = _dot(xv, wg_ref[...]) + gb_ref[...]
        lam_row_v = lam_ref[...]
        sp8 = RG_C * _softplus(-lam_row_v)
        dsp_dlam = -RG_C * _sig(-lam_row_v)
        gates = _rg_gates(xv, pre, lam_row_v)
        dxc = jnp.zeros((t, RGW), F32)
        dpre_r = []
        dpre_i = []
        for d, ((r, gi, a, mult), l_ref, hs) in enumerate(zip(gates, (l0_ref, l1_ref), h_shift)):
            dbb = l_ref[...]
            da = dbb * hs
            cs = slice(RGW * d, RGW * (d + 1))
            dmult = dbb * gi * xv
            dgi = dbb * mult * xv
            dxc = dxc + dbb * mult * gi
            dla = da * a - dmult * a * a / mult
            dr = -dla * sp8[:, cs]
            dlam_ref[:, cs] += _colsum(-dla * r) * dsp_dlam[:, cs]
            dpre_r.append(dr * r * (1.0 - r))
            dpre_i.append(dgi * gi * (1.0 - gi))
        dpre = jnp.concatenate(dpre_r + dpre_i, axis=1)
        dgb_ref[...] += _colsum(dpre)
        dpre_b = dpre.astype(BF16)
        dpre_ref[...] = dpre_b
        xcb_ref[...] = xv.astype(BF16)
        dxc_ref[...] = dxc + _dot_nt(dpre_b, wg_ref[...])

    prev, nxt = _halo_specs(s, t, RGW)
    return pl.pallas_call(
        body, name=name, grid=(s // t,),
        in_specs=[_rows(t, RGW), _full((RGW, 4 * RGW)), _full((1, 4 * RGW)), _full((1, 2 * RGW))] + [_rows(t, RGW)] * 2
        + [_rows(t, RGW), prev, nxt] * 2,
        out_specs=[_rows(t, RGW), _rows(t, 4 * RGW), _rows(t, RGW), _full((1, 4 * RGW)), _full((1, 2 * RGW))],
        out_shape=[_sds((s, RGW)), _sds((s, 4 * RGW), BF16), _sds((s, RGW), BF16), _sds((1, 4 * RGW)), _sds((1, 2 * RGW))],
        compiler_params=_cparams(1),
    )(xc, wgates, gbias, lam_row, lam0, lam1, hf, hf, hf, hb, hb, hb)


class _GdnMasks:
    def __init__(self, d):
        ri = lax.broadcasted_iota(jnp.int32, (CHUNK, CHUNK), 0)
        ci = lax.broadcasted_iota(jnp.int32, (CHUNK, CHUNK), 1)
        self.incl = (ri >= ci) if d == 0 else (ri <= ci)
        self.strict = (ri > ci) if d == 0 else (ri < ci)
        b16 = jnp.right_shift(ri, 4) == jnp.right_shift(ci, 4)
        b32 = jnp.right_shift(ri, 5) == jnp.right_shift(ci, 5)
        self.diag16 = b16
        self.off32 = jnp.logical_and(b32, jnp.logical_not(b16))
        self.off64 = jnp.logical_not(b32)
        self.eye = jnp.where(ri == ci, 1.0, 0.0).astype(F32)
        self.tri = jnp.where(self.incl, 1.0, 0.0).astype(F32)
        self.last = CHUNK - 1 if d == 0 else 0


def _tri_inv(lmat, m):
    return _tri_inv_many([lmat], [m])[0]


def _tri_inv_many(lmats, masks):
    n = len(lmats)
    ns = [jnp.where(masks[i].diag16, lmats[i], 0.0) for i in range(n)]
    ps = [masks[i].eye - ns[i] for i in range(n)]
    qs = [_dot3(ns[i], ns[i]) for i in range(n)]
    for step in range(3):
        ps = [_dot3(ps[i], masks[i].eye + qs[i]) for i in range(n)]
        if step < 2:
            qs = [_dot3(qs[i], qs[i]) for i in range(n)]
    for off in ("off32", "off64"):
        ts = [_dot3(ps[i], jnp.where(getattr(masks[i], off), lmats[i], 0.0)) for i in range(n)]
        ps = [ps[i] - _dot3(ts[i], ps[i]) for i in range(n)]
    return ps


def _chunk_cumsums(m, bgv):
    return _dot_exact(m.tri, bgv, _NN, True), _dot_exact(m.tri, bgv, ((0,), (1,)), False)


class _GdnHead:
    def __init__(self, qh, kh, vh, kk, q0, bg, gcs, gcs_t, d, h, m):
        cb = 4 * d + h
        cg = 8 + 4 * d + h
        self.q, self.k, self.v = qh, kh, vh
        self.beta = bg[:, cb:cb + 1]
        gcol = gcs[:, cg:cg + 1]
        grow = gcs_t[cg:cg + 1, :]
        gl = gcs[m.last:m.last + 1, cg:cg + 1]
        self.decay = jnp.exp(jnp.where(m.incl, gcol - grow, -1e30))
        self.kb = kh * self.beta
        self.vb = vh * self.beta
        self.a0 = kk * self.beta
        self.q0 = q0
        self.lmat = jnp.where(m.strict, self.a0 * self.decay, 0.0)
        self.attn = self.q0 * self.decay
        self.eg = jnp.exp(gcol)
        self.ek = jnp.exp(gl - gcol)
        self.cd = jnp.exp(gl)
        self.kg = self.kb * self.eg
        self.qd = qh * self.eg
        self.kd = kh * self.ek


HW = NH * DH
SEQ_CB = 4


def _head(h):
    return slice(DH * h, DH * (h + 1))


def _gdn_local_fwd(q, k, v, bg, name):
    s = q.shape[0]
    n = s // CHUNK

    def body(q_ref, k_ref, v_ref, bg_ref, t_ref, u_ref, w_ref, qd_ref, kd_ref, at_ref, cd_ref):
        bgv = bg_ref[...]
        qs = [q_ref[:, _head(h)] for h in range(NH)]
        ks = [k_ref[:, _head(h)] for h in range(NH)]
        kk = [_dot_nt(ks[h], ks[h]) for h in range(NH)]
        q0 = [_dot_nt(qs[h], ks[h]) for h in range(NH)]
        inst = []
        for d in range(2):
            m = _GdnMasks(d)
            gcs, gcs_t = _chunk_cumsums(m, bgv)
            for h in range(NH):
                c = _GdnHead(qs[h], ks[h], v_ref[:, _head(h)], kk[h], q0[h], bgv, gcs, gcs_t, d, h, m)
                inst.append((d, h, m, c))
        tms = _tri_inv_many([c.lmat for _, _, _, c in inst], [m for _, _, m, _ in inst])
        for (d, h, m, c), tm in zip(inst, tms):
            sl = _head(h)
            t_ref[0, d, h] = tm
            u_ref[d, :, sl] = _dot(tm, c.vb)
            w_ref[d, :, sl] = _dot(tm, c.kg).astype(BF16)
            qd_ref[d, :, sl] = c.qd.astype(BF16)
            kd_ref[d, :, sl] = c.kd.astype(BF16)
            at_ref[0, d, h] = c.attn.astype(BF16)
            cd_ref[0, 4 * d + h:4 * d + h + 1, :] = jnp.broadcast_to(c.cd, (1, DH))

    tok = _rows(CHUNK, HW)
    tok2 = pl.BlockSpec((2, CHUNK, HW), lambda i: (0, i, 0))
    mat = pl.BlockSpec((1, 2, NH, CHUNK, CHUNK), lambda i: (i, 0, 0, 0, 0))
    return pl.pallas_call(
        body, name=name, grid=(n,), in_specs=[tok, tok, tok, _rows(CHUNK, BAP)],
        out_specs=[mat, tok2, tok2, tok2, tok2, mat, pl.BlockSpec((1, 8, DH), lambda i: (i, 0, 0))],
        out_shape=[_sds((n, 2, NH, CHUNK, CHUNK)), _sds((2, s, HW)), _sds((2, s, HW), BF16), _sds((2, s, HW), BF16),
                   _sds((2, s, HW), BF16), _sds((n, 2, NH, CHUNK, CHUNK), BF16), _sds((n, 8, DH))],
        compiler_params=_cparams(1),
    )(q, k, v, bg)


def _seq_specs(s, order):
    n = s // CHUNK
    cb = min(SEQ_CB, n)
    nb = n // cb
    tb = cb * CHUNK

    def blk(d):
        return (lambda i: i) if order[d] else (lambda i: nb - 1 - i)

    def per_dir(make):
        return [make(d, blk(d)) for d in range(2)]

    tok2 = per_dir(lambda d, f: pl.BlockSpec((1, tb, HW), lambda i: (d, f(i), 0)))
    tok = per_dir(lambda d, f: pl.BlockSpec((tb, HW), lambda i: (f(i), 0)))
    mat = per_dir(lambda d, f: pl.BlockSpec((cb, 1, NH, CHUNK, CHUNK), lambda i: (f(i), d, 0, 0, 0)))
    cds = per_dir(lambda d, f: pl.BlockSpec((cb, 8, DH), lambda i: (f(i), 0, 0)))
    sts = per_dir(lambda d, f: pl.BlockSpec((cb, NH, DH, DH), lambda i: (f(i), 0, 0, 0)))
    dcd = per_dir(lambda d, f: pl.BlockSpec((cb, NH, DH), lambda i: (f(i), 0, 0)))
    return n, cb, nb, tok2, tok, mat, cds, sts, dcd


def _gdn_seq_fwd(u, w, qd, kd, at, cd, name):
    s = u.shape[1]
    n, cb, nb, tok2, tok, mat, cds, sts, _ = _seq_specs(s, (True, False))

    def body(*refs):
        ins = (refs[0:6], refs[6:12])
        outs = (refs[12:15], refs[15:18])
        st = refs[18]

        @pl.when(pl.program_id(0) == 0)
        def _():
            st[...] = jnp.zeros_like(st)

        for j in range(cb):
            items = []
            for d in range(2):
                jj = j if d == 0 else cb - 1 - j
                items += [(d, h, jj, slice(CHUNK * jj, CHUNK * (jj + 1)), _head(h)) for h in range(NH)]
            shs = [st[d, h] for d, h, _, _, _ in items]
            wss = [_dot(ins[d][1][0, rows, sl], sh) for (d, h, jj, rows, sl), sh in zip(items, shs)]
            vns = [ins[d][0][0, rows, sl] - ws for (d, h, jj, rows, sl), ws in zip(items, wss)]
            news = [sh * ins[d][5][jj, 4 * d + h:4 * d + h + 1, :] + _dot_tn(ins[d][3][0, rows, sl], vn)
                    for (d, h, jj, rows, sl), sh, vn in zip(items, shs, vns)]
            for (d, h, jj, rows, sl), sh, vn, new in zip(items, shs, vns, news):
                o_r, s_r, vn_r = outs[d]
                st[d, h] = new
                s_r[jj, h] = sh
                vn_r[rows, sl] = vn
                o_r[rows, sl] = _dot(ins[d][2][0, rows, sl], sh) + _dot(ins[d][4][jj, 0, h], vn)

    in_specs, out_specs, out_shape = [], [], []
    for d in range(2):
        in_specs += [tok2[d]] * 4 + [mat[d], cds[d]]
        out_specs += [tok[d], sts[d], tok[d]]
        out_shape += [_sds((s, HW)), _sds((n, NH, DH, DH)), _sds((s, HW))]
    return pl.pallas_call(
        body, name=name, grid=(nb,), in_specs=in_specs, out_specs=out_specs, out_shape=out_shape,
        scratch_shapes=[pltpu.VMEM((2, NH, DH, DH), F32)], compiler_params=_cparams(1),
    )(u, w, qd, kd, at, cd, u, w, qd, kd, at, cd)


def _gdn_seq_bwd(do, w, qd, kd, at, cd, states, vns, name):
    s = do.shape[0]
    n, cb, nb, tok2, tok, mat, cds, sts, dcd = _seq_specs(s, (False, True))

    def body(*refs):
        ins = (refs[0:8], refs[8:16])
        outs = (refs[16:21], refs[21:26])
        dst = refs[26]

        @pl.when(pl.program_id(0) == 0)
        def _():
            dst[...] = jnp.zeros_like(dst)

        for j in range(cb):
            items = []
            for d in range(2):
                jj = cb - 1 - j if d == 0 else j
                items += [(d, h, jj, slice(CHUNK * jj, CHUNK * (jj + 1)), _head(h)) for h in range(NH)]
            dsns = [dst[d, h] for d, h, _, _, _ in items]
            dohs = [ins[d][0][rows, sl] for d, h, jj, rows, sl in items]
            d_vns = [_dot_tn(ins[d][4][jj, 0, h], doh) + _dot(ins[d][3][0, rows, sl], dsn)
                     for (d, h, jj, rows, sl), doh, dsn in zip(items, dohs, dsns)]
            news = [ins[d][5][jj, 4 * d + h:4 * d + h + 1, :] * dsn + _dot_tn(ins[d][2][0, rows, sl], doh)
                    - _dot_tn(ins[d][1][0, rows, sl], d_vn)
                    for (d, h, jj, rows, sl), doh, dsn, d_vn in zip(items, dohs, dsns, d_vns)]
            for (d, h, jj, rows, sl), doh, dsn, d_vn, new in zip(items, dohs, dsns, d_vns, news):
                dvn_r, dkd_r, dqd_r, dw_r, dcd_r = outs[d]
                sh = ins[d][6][jj, h]
                dst[d, h] = new
                dvn_r[rows, sl] = d_vn
                dkd_r[rows, sl] = _dot_nt(ins[d][7][rows, sl], dsn)
                dqd_r[rows, sl] = _dot_nt(doh, sh)
                dw_r[rows, sl] = -_dot_nt(d_vn, sh)
                d_cd = jnp.sum(jnp.sum(sh * dsn, axis=1, keepdims=True), axis=0, keepdims=True)
                dcd_r[jj, h:h + 1, :] = jnp.broadcast_to(d_cd, (1, DH))

    in_specs, out_specs, out_shape, args = [], [], [], []
    for d in range(2):
        in_specs += [tok[d]] + [tok2[d]] * 3 + [mat[d], cds[d], sts[d], tok[d]]
        args += [do, w, qd, kd, at, cd, states[d], vns[d]]
        out_specs += [tok[d]] * 4 + [dcd[d]]
        out_shape += [_sds((s, HW))] * 4 + [_sds((n, NH, DH))]
    return pl.pallas_call(
        body, name=name, grid=(nb,), in_specs=in_specs, out_specs=out_specs, out_shape=out_shape,
        scratch_shapes=[pltpu.VMEM((2, NH, DH, DH), F32)], compiler_params=_cparams(1),
    )(*args)


def _gdn_local_bwd(q, k, v, bg, tmat, do, vns, seq_grads, name, comm=None):
    s = q.shape[0]
    n = s // CHUNK

    def body(*refs):
        q_ref, k_ref, v_ref, bg_ref, t_ref, do_ref = refs[0:6]
        vn_refs = refs[6:8]
        sg = (refs[8:13], refs[13:18])
        dq_ref, dk_ref, dv_ref, dbg_ref = refs[18:]
        bgv = bg_ref[...]
        qs = [q_ref[:, _head(h)] for h in range(NH)]
        ks = [k_ref[:, _head(h)] for h in range(NH)]
        kk = [_dot_nt(ks[h], ks[h]) for h in range(NH)]
        q0 = [_dot_nt(qs[h], ks[h]) for h in range(NH)]
        lane = lax.broadcasted_iota(jnp.int32, (CHUNK, BAP), 1)
        rowi = lax.broadcasted_iota(jnp.int32, (CHUNK, 1), 0)
        ones = jnp.ones((CHUNK, DH), F32)
        dbg = jnp.zeros((CHUNK, BAP), F32)
        acc = [[None, None, None] for _ in range(NH)]
        inst = []
        for d in range(2):
            m = _GdnMasks(d)
            gcs, gcs_t = _chunk_cumsums(m, bgv)
            for h in range(NH):
                c = _GdnHead(qs[h], ks[h], v_ref[:, _head(h)], kk[h], q0[h], bgv, gcs, gcs_t, d, h, m)
                inst.append((d, h, m, c))
        tms = [t_ref[0, d, h] for d, h, _, _ in inst]
        d_vns = [sg[d][0][:, _head(h)] for d, h, _, _ in inst]
        d_ws = [sg[d][3][:, _head(h)] for d, h, _, _ in inst]
        d_ts = [_dot_nt(d_vns[i], c.vb) + _dot_nt(d_ws[i], c.kg) for i, (_, _, _, c) in enumerate(inst)]
        xs = [_dot3(tms[i], d_ts[i], _TN) for i in range(8)]
        d_ls = [jnp.where(inst[i][2].strict, -_dot3(xs[i], tms[i], _NT), 0.0) for i in range(8)]
        d_attns = [jnp.where(m.incl, _dot_nt(do_ref[:, _head(h)], vn_refs[d][:, _head(h)]), 0.0) for d, h, m, _ in inst]
        d_vbs = [_dot_tn(tms[i], d_vns[i]) for i in range(8)]
        d_kgs = [_dot_tn(tms[i], d_ws[i]) for i in range(8)]
        d_a0s = [d_ls[i] * c.decay for i, (_, _, _, c) in enumerate(inst)]
        d_q0s = [d_attns[i] * c.decay for i, (_, _, _, c) in enumerate(inst)]
        es = [(d_ls[i] * c.a0 + d_attns[i] * c.q0) * c.decay for i, (_, _, _, c) in enumerate(inst)]
        kb_mm = [_dot(d_a0s[i], c.k) for i, (_, _, _, c) in enumerate(inst)]
        q_mm = [_dot(d_q0s[i], c.k) for i, (_, _, _, c) in enumerate(inst)]
        k_mm = [_dot_tn(d_a0s[i], c.kb) + _dot_tn(d_q0s[i], c.q) for i, (_, _, _, c) in enumerate(inst)]
        e_cols = [_dot_exact(ones, es[i], _TN, False)[:, 0:1] for i in range(8)]
        d_gcs, d_betas = [], []
        for i, (d, h, m, c) in enumerate(inst):
            sl = _head(h)
            d_kd, d_qd = sg[d][1][:, sl], sg[d][2][:, sl]
            d_cd = sg[d][4][0, h:h + 1, 0:1]
            d_vb, d_kg = d_vbs[i], d_kgs[i]
            d_kb = kb_mm[i] + d_kg * c.eg
            parts = (q_mm[i] + d_qd * c.eg, k_mm[i] + d_kd * c.ek + d_kb * c.beta, d_vb * c.beta)
            acc[h] = [p if a is None else a + p for a, p in zip(acc[h], parts)]
            s_kd = jnp.sum(d_kd * c.kd, axis=1, keepdims=True)
            d_gc = (jnp.sum(d_kg * c.kg, axis=1, keepdims=True) + jnp.sum(d_qd * c.qd, axis=1, keepdims=True) - s_kd
                    + jnp.sum(es[i], axis=1, keepdims=True) - e_cols[i])
            d_gl = jnp.sum(s_kd, axis=0, keepdims=True) + d_cd * c.cd
            d_gcs.append(d_gc + jnp.where(rowi == m.last, d_gl, 0.0))
            d_betas.append(jnp.sum(d_kb * c.k, axis=1, keepdims=True) + jnp.sum(d_vb * c.v, axis=1, keepdims=True))
        d_gs = [_dot_exact(m.tri, d_gcs[i] * ones, _TN, True)[:, 0:1] for i, (_, _, m, _) in enumerate(inst)]
        for i, (d, h, _, _) in enumerate(inst):
            dbg = dbg + jnp.where(lane == 4 * d + h, d_betas[i], 0.0) + jnp.where(lane == 8 + 4 * d + h, d_gs[i], 0.0)
        for h in range(NH):
            dq_ref[:, _head(h)], dk_ref[:, _head(h)], dv_ref[:, _head(h)] = acc[h]
        dbg_ref[...] = dbg

    tok = _rows(CHUNK, HW)
    bgs = _rows(CHUNK, BAP)
    mat = pl.BlockSpec((1, 2, NH, CHUNK, CHUNK), lambda i: (i, 0, 0, 0, 0))
    dcd = pl.BlockSpec((1, NH, DH), lambda i: (i, 0, 0))
    args = [q, k, v, bg, tmat, do, vns[0], vns[1]]
    in_specs = [tok, tok, tok, bgs, mat, tok, tok, tok]
    for d in range(2):
        args += list(seq_grads[d])
        in_specs += [tok] * 4 + [dcd]
    return _pallas(body, comm, name=name, grid=(n,), in_specs=in_specs, out_specs=[tok, tok, tok, bgs],
                   out_shape=[_sds((s, HW))] * 3 + [_sds((s, BAP))], scratch_shapes=[], args=args)


def _prep_bwd(c_qkv, p_ba, alog_row, dtb_row, dq, dk, dv, dbg, name):
    s = c_qkv.shape[0]
    t = min(256, s)

    def body(cq_ref, pc_ref, alog_ref, dtb_ref, dq_ref, dk_ref, dv_ref, dbg_ref,
             dcq_ref, dpc_ref, dalog_ref, ddtb_ref):
        @pl.when(pl.program_id(0) == 0)
        def _():
            dalog_ref[...] = jnp.zeros_like(dalog_ref)
            ddtb_ref[...] = jnp.zeros_like(ddtb_ref)

        cq = cq_ref[...]
        sq = cq * _sig(cq)
        sg = _silu_grad(cq)
        for h in range(NH):
            sl = slice(DH * h, DH * (h + 1))
            for off, d_ref, scale in ((0, dq_ref, DH ** -0.5), (RGW, dk_ref, 1.0)):
                csl = slice(off + DH * h, off + DH * (h + 1))
                xh = sq[:, csl]
                nrm = lax.rsqrt(jnp.sum(xh * xh, axis=-1, keepdims=True) + EPS)
                y = xh * nrm
                dy = d_ref[:, sl] * scale
                dcq_ref[:, csl] = nrm * (dy - y * jnp.sum(dy * y, axis=-1, keepdims=True)) * sg[:, csl]
        dcq_ref[:, 2 * RGW:] = dv_ref[...] * sg[:, 2 * RGW:]
        pc = pc_ref[...]
        lane = lax.broadcasted_iota(jnp.int32, pc.shape, 1)
        dbg = dbg_ref[...]
        beta = _sig(pc)
        ea = jnp.exp(alog_ref[...])
        z = pc + dtb_ref[...]
        g = -ea * _softplus(z)
        is_g = jnp.logical_and(lane >= 8, lane < 16)
        d_alpha = jnp.where(is_g, dbg * (-ea) * _sig(z), 0.0)
        dpc_ref[...] = jnp.where(lane < 8, dbg * beta * (1.0 - beta), d_alpha)
        dalog_ref[...] += _colsum(jnp.where(is_g, dbg * g, 0.0))
        ddtb_ref[...] += _colsum(d_alpha)

    return pl.pallas_call(
        body, name=name, grid=(s // t,),
        in_specs=[_rows(t, QKVW), _rows(t, BAP), _full((1, BAP)), _full((1, BAP))] + [_rows(t, HW)] * 3 + [_rows(t, BAP)],
        out_specs=[_rows(t, QKVW), _rows(t, BAP), _full((1, BAP)), _full((1, BAP))],
        out_shape=[_sds((s, QKVW)), _sds((s, BAP)), _sds((1, BAP)), _sds((1, BAP))],
        compiler_params=_cparams(1),
    )(c_qkv, p_ba, alog_row, dtb_row, dq, dk, dv, dbg)


def _mix_out_values(hf, hb, gate, of, ob, z, gn):
    hr = hf + hb
    y_rg = hr * _gelu(gate)
    osum = of + ob
    parts = []
    for h in range(NH):
        sl = slice(DH * h, DH * (h + 1))
        oh = osum[:, sl]
        r, ohat = _rms(oh)
        zh = z[:, sl]
        parts.append((r, ohat, zh))
    y_gdn = jnp.concatenate([ohat * gn * (zh * _sig(zh)) for (r, ohat, zh) in parts], axis=1)
    return hr, y_rg, y_gdn, parts


def _outproj(x1, hf, hb, gate, of, ob, z, gn, wout, name):
    s = x1.shape[0]
    t = min(256, s)

    def body(x_ref, hf_ref, hb_ref, gate_ref, of_ref, ob_ref, z_ref, gn_ref, w_ref, xo_ref, y_ref):
        _, y_rg, y_gdn, _ = _mix_out_values(hf_ref[...], hb_ref[...], gate_ref[...], of_ref[...], ob_ref[...],
                                            z_ref[...], gn_ref[...])
        y = jnp.concatenate([y_rg, y_gdn], axis=1).astype(BF16)
        y_ref[...] = y
        xo_ref[...] = x_ref[...] + jnp.dot(y, w_ref[...], preferred_element_type=F32)

    return pl.pallas_call(
        body, name=name, grid=(s // t,),
        in_specs=[_rows(t, D)] + [_rows(t, RGW)] * 6 + [_full((1, DH)), _full((D, D))],
        out_specs=[_rows(t, D), _rows(t, D)], out_shape=[_sds((s, D)), _sds((s, D), BF16)],
        compiler_params=_cparams(1),
    )(x1, hf, hb, gate, of, ob, z, gn, wout)


def _outproj_bwd(dx2, hf, hb, gate, of, ob, z, gn, wout, name):
    s = dx2.shape[0]
    t = min(256, s)

    def body(d_ref, hf_ref, hb_ref, gate_ref, of_ref, ob_ref, z_ref, gn_ref, w_ref,
             dhr_ref, dgate_ref, dos_ref, dz_ref, dgn_ref, db_ref):
        @pl.when(pl.program_id(0) == 0)
        def _():
            dgn_ref[...] = jnp.zeros_like(dgn_ref)

        gate = gate_ref[...]
        gn_v = gn_ref[...]
        hr, _, _, parts = _mix_out_values(hf_ref[...], hb_ref[...], gate, of_ref[...], ob_ref[...], z_ref[...], gn_v)
        dbf = d_ref[...].astype(BF16)
        db_ref[...] = dbf
        dy = _dot_nt(dbf, w_ref[...])
        dyr = dy[:, :RGW]
        dhr_ref[...] = dyr * _gelu(gate)
        dgate_ref[...] = dyr * hr * _gelu_grad(gate)
        dgn = jnp.zeros((1, DH), F32)
        for h, (r, ohat, zh) in enumerate(parts):
            sl = slice(DH * h, DH * (h + 1))
            dyh = dy[:, RGW + DH * h:RGW + DH * (h + 1)]
            sz = zh * _sig(zh)
            dn = dyh * sz
            dz_ref[:, sl] = dyh * ohat * gn_v * _silu_grad(zh)
            dgn = dgn + _colsum(dn * ohat)
            dos_ref[:, sl] = _rms_bwd(dn, ohat, r, gn_v)
        dgn_ref[...] += dgn

    return pl.pallas_call(
        body, name=name, grid=(s // t,),
        in_specs=[_rows(t, D)] + [_rows(t, RGW)] * 6 + [_full((1, DH)), _full((D, D))],
        out_specs=[_rows(t, RGW)] * 4 + [_full((1, DH)), _rows(t, D)],
        out_shape=[_sds((s, RGW))] * 4 + [_sds((1, DH)), _sds((s, D), BF16)],
        compiler_params=_cparams(1),
    )(dx2, hf, hb, gate, of, ob, z, gn, wout)


def _loss_head(x3, target, gain, name):
    s = x3.shape[0]
    t = min(256, s)

    def body(x_ref, t_ref, g_ref, dx_ref, loss_ref, dg_ref):
        @pl.when(pl.program_id(0) == 0)
        def _():
            loss_ref[...] = jnp.zeros_like(loss_ref)
            dg_ref[...] = jnp.zeros_like(dg_ref)

        r, xh = _rms(x_ref[...])
        gv = g_ref[...]
        err = xh * gv - t_ref[...]
        per_tok = jnp.mean(err * err, axis=-1, keepdims=True)
        loss_ref[...] += 0.5 * jnp.sum(per_tok, axis=0, keepdims=True)
        dy = err * (1.0 / D)
        dg_ref[...] += _colsum(dy * xh)
        dx_ref[...] = _rms_bwd(dy, xh, r, gv)

    return pl.pallas_call(
        body, name=name, grid=(s // t,), in_specs=[_rows(t, D), _rows(t, D), _full((1, D))],
        out_specs=[_rows(t, D), _full((8, 128)), _full((1, D))],
        out_shape=[_sds((s, D)), _sds((8, 128)), _sds((1, D))], compiler_params=_cparams(1),
    )(x3, target, gain)


def _adamw_math(wv, gv, mv, vv):
    mn = ADAM_B1 * mv + (1.0 - ADAM_B1) * gv
    vn = ADAM_B2 * vv + (1.0 - ADAM_B2) * (gv * gv)
    m_hat = mn / (1.0 - ADAM_B1 ** ADAM_STEP)
    v_hat = vn / (1.0 - ADAM_B2 ** ADAM_STEP)
    return -ADAM_LR * (m_hat / (jnp.sqrt(v_hat) + ADAM_EPS) + ADAM_WD * wv), mn, vn


def _row_tile(r, c):
    tr = r
    while tr * c * 4 > (1 << 20) and tr % 16 == 0:
        tr //= 2
    return tr


def _adamw(w, g, m, v, name):
    r, c = w.shape
    tr = _row_tile(r, c)

    def body(w_ref, g_ref, m_ref, v_ref, d_ref, nm_ref, nv_ref):
        d_ref[...], nm_ref[...], nv_ref[...] = _adamw_math(w_ref[...], g_ref[...], m_ref[...], v_ref[...])

    return pl.pallas_call(
        body, name=name, grid=(r // tr,), in_specs=[_rows(tr, c)] * 4, out_specs=[_rows(tr, c)] * 3,
        out_shape=[_sds((r, c))] * 3, compiler_params=_cparams(1),
    )(w, g, m, v)


def _adamw_halves(w, own, recv, m, v, c_arr, name):
    r, c = w.shape
    h = r // 2
    tr = _row_tile(h, c)
    nh = h // tr

    def body(c_ref, w_ref, own_ref, recv_ref, m_ref, v_ref, g_ref, d_ref, nm_ref, nv_ref):
        first_half = pl.program_id(0) < nh
        use_own = first_half == (c_ref[0] == 0)
        gv = jnp.where(use_own, own_ref[...], recv_ref[...])
        g_ref[...] = gv
        d_ref[...], nm_ref[...], nv_ref[...] = _adamw_math(w_ref[...], gv, m_ref[...], v_ref[...])

    full = pl.BlockSpec((tr, c), lambda i, c_ref: (i, 0))
    half = pl.BlockSpec((tr, c), lambda i, c_ref: (i % nh, 0))
    return pl.pallas_call(
        body, name=name, out_shape=[_sds((r, c))] * 4,
        grid_spec=pltpu.PrefetchScalarGridSpec(
            num_scalar_prefetch=1, grid=(2 * nh,), in_specs=[full, half, half, full, full], out_specs=[full] * 4),
        compiler_params=_cparams(1),
    )(c_arr, w, own, recv, m, v)


def _mesh_pos():
    return lax.axis_index("x"), lax.axis_index("y"), lax.axis_index("c")


def _other_chips(x, y):
    return [(1 - x, y), (x, 1 - y), (1 - x, 1 - y)]


class _Comm:
    def __init__(self, inputs, out_shapes, scratch, start, finish, space=pltpu.HBM):
        self.inputs, self.out_shapes, self.scratch = list(inputs), list(out_shapes), list(scratch)
        self.start, self.finish, self.space = start, finish, space


def _comm_call(comm, name):
    ni, no = len(comm.inputs), len(comm.out_shapes)

    def body(*refs):
        comm.start(refs[:ni], refs[ni:ni + no], refs[ni + no:])
        comm.finish(refs[:ni], refs[ni:ni + no], refs[ni + no:])

    spec = pl.BlockSpec(memory_space=comm.space)
    return list(pl.pallas_call(body, name=name, out_shape=comm.out_shapes, in_specs=[spec] * ni, out_specs=[spec] * no,
                               scratch_shapes=comm.scratch)(*comm.inputs))


def _pallas(body, comm, *, name, grid, in_specs, out_specs, out_shape, scratch_shapes, args):
    params = _cparams(len(grid))
    if comm is None:
        outs = pl.pallas_call(body, name=name, grid=grid, in_specs=in_specs, out_specs=out_specs, out_shape=out_shape,
                              scratch_shapes=scratch_shapes, compiler_params=params)(*args)
        return list(outs), []
    n_in, n_out, n_sc = len(in_specs), len(out_specs), len(scratch_shapes)
    ci, co = len(comm.inputs), len(comm.out_shapes)

    def carried(*refs):
        bounds = [0, n_in, n_in + ci, n_in + ci + n_out, n_in + ci + n_out + co, n_in + ci + n_out + co + n_sc, len(refs)]
        ins, cins, outs, couts, scr, csems = [refs[lo:hi] for lo, hi in zip(bounds[:-1], bounds[1:])]
        ids = [pl.program_id(k) for k in range(len(grid))]
        first = functools.reduce(jnp.logical_and, [i == 0 for i in ids])
        last = functools.reduce(jnp.logical_and, [i == g - 1 for i, g in zip(ids, grid)])

        @pl.when(first)
        def _():
            comm.start(cins, couts, csems)

        body(*ins, *outs, *scr)

        @pl.when(last)
        def _():
            comm.finish(cins, couts, csems)

    hbm = pl.BlockSpec(memory_space=pltpu.HBM)
    outs = pl.pallas_call(
        carried, name=name, grid=grid, in_specs=list(in_specs) + [hbm] * ci, out_specs=list(out_specs) + [hbm] * co,
        out_shape=list(out_shape) + comm.out_shapes, scratch_shapes=list(scratch_shapes) + comm.scratch,
        compiler_params=params)(*args, *comm.inputs)
    return list(outs[:n_out]), list(outs[n_out:])


def _gather_comm(arrays, space, block_rows):
    n_arr = len(arrays)

    def plan(x_refs, out_refs, sems):
        send_sems, recv_sems, local_sems = sems
        x, y, c = _mesh_pos()
        me, sibling = (x, y, c), (x, y, 1 - c)
        chips = _other_chips(x, y)

        def slot(a, px, py, pc):
            return out_refs[a].at[4 * px + 2 * py + pc]

        def copy(a, k, block, to, src=None):
            return pltpu.make_async_remote_copy(
                src_ref=slot(a, *block) if src is None else src, dst_ref=slot(a, *block),
                send_sem=send_sems.at[7 * a + k], recv_sem=recv_sems.at[7 * a + k], device_id=to, device_id_type=MESH)

        srcs = [x_refs[a] if block_rows[a] is None else
                x_refs[a].at[pl.ds(pl.multiple_of(c * block_rows[a], 16), block_rows[a]), :] for a in range(n_arr)]
        local = [pltpu.make_async_copy(srcs[a], slot(a, *me), local_sems.at[a]) for a in range(n_arr)]
        first = []
        for a in range(n_arr):
            first += [copy(a, 1 + j, me, (*chip, c), src=srcs[a]) for j, chip in enumerate(chips)]
            first.append(copy(a, 0, me, sibling, src=srcs[a]))
        return me, sibling, chips, c, copy, local, first

    def start(x_refs, out_refs, sems):
        _, _, _, _, _, local, first = plan(x_refs, out_refs, sems)
        for cp in local + first:
            cp.start()

    def finish(x_refs, out_refs, sems):
        me, sibling, chips, c, copy, local, first = plan(x_refs, out_refs, sems)
        passed = []
        for j, chip in enumerate(chips):
            for a in range(n_arr):
                copy(a, 1 + j, (*chip, c), me).wait_recv()
                fwd = copy(a, 4 + j, (*chip, c), sibling)
                fwd.start()
                passed.append(fwd)
        for a in range(n_arr):
            copy(a, 0, sibling, me).wait_recv()
            for j, chip in enumerate(chips):
                copy(a, 4 + j, (*chip, 1 - c), me).wait_recv()
        for cp in first + passed:
            cp.wait_send()
        for cp in local:
            cp.wait()

    out_shapes = [_sds((8, w.shape[0] if r is None else r) + w.shape[1:], w.dtype) for w, r in zip(arrays, block_rows)]
    scratch = [pltpu.SemaphoreType.DMA((7 * n_arr,)), pltpu.SemaphoreType.DMA((7 * n_arr,)), pltpu.SemaphoreType.DMA((n_arr,))]
    return _Comm(arrays, out_shapes, scratch, start, finish, space)


def _weights_gather_comm(shards):
    return _gather_comm(shards, pltpu.HBM, [w.shape[0] // 2 for w in shards])


def _all_shards(gathered):
    return [o.reshape(NSH, 2 * o.shape[1], o.shape[2]) for o in gathered]


def _gather_small(block, name):
    return _comm_call(_gather_comm([block], pltpu.VMEM, [None]), name)[0]


def _sibling_exchange(gs, name):
    n = len(gs)
    halves = [g.shape[1] // 2 for g in gs]

    def body(*refs):
        g_refs, land_refs = refs[:n], refs[n:2 * n]
        send_sems, recv_sems = refs[2 * n:]
        x, y, c = _mesh_pos()
        copies = []
        for a in range(n):
            h = halves[a]
            for s in range(NSH):
                copies.append(pltpu.make_async_remote_copy(
                    src_ref=g_refs[a].at[s, pl.ds(pl.multiple_of((1 - c) * h, 8), h), :], dst_ref=land_refs[a].at[s],
                    send_sem=send_sems.at[NSH * a + s], recv_sem=recv_sems.at[NSH * a + s],
                    device_id=(x, y, 1 - c), device_id_type=MESH))
        for cp in copies:
            cp.start()
        for cp in copies:
            cp.wait()

    return pl.pallas_call(
        body, name=name, out_shape=[_sds((NSH, h, g.shape[2])) for h, g in zip(halves, gs)],
        in_specs=[pl.BlockSpec(memory_space=pltpu.HBM)] * n, out_specs=[pl.BlockSpec(memory_space=pltpu.HBM)] * n,
        scratch_shapes=[pltpu.SemaphoreType.DMA((NSH * n,)), pltpu.SemaphoreType.DMA((NSH * n,))],
    )(*gs)


def _chip_sum(g, land, c_arr, name):
    _, h, cols = land.shape

    def body(c_ref, g_ref, l_ref, o_ref):
        o_ref[...] = (g_ref[...] + l_ref[...]).astype(BF16)

    return pl.pallas_call(
        body, name=name, out_shape=_sds((NSH, h, cols), BF16),
        grid_spec=pltpu.PrefetchScalarGridSpec(
            num_scalar_prefetch=1, grid=(NSH,),
            in_specs=[pl.BlockSpec((1, h, cols), lambda s, c_ref: (s, c_ref[0], 0)),
                      pl.BlockSpec((1, h, cols), lambda s, c_ref: (s, 0, 0))],
            out_specs=pl.BlockSpec((1, h, cols), lambda s, c_ref: (s, 0, 0))),
        compiler_params=_cparams(1),
    )(c_arr, g, land)


def _scatter_comm(parts):
    n = len(parts)

    def plan(p_refs, land_refs, sems):
        send_sems, recv_sems, local_sems = sems
        x, y, c = _mesh_pos()
        my_chip = 2 * x + y
        local = [pltpu.make_async_copy(p_refs[a].at[my_chip], land_refs[a].at[my_chip], local_sems.at[a]) for a in range(n)]
        copies = []
        for a in range(n):
            for j, (px, py) in enumerate(_other_chips(x, y)):
                copies.append(pltpu.make_async_remote_copy(
                    src_ref=p_refs[a].at[2 * px + py], dst_ref=land_refs[a].at[my_chip],
                    send_sem=send_sems.at[3 * a + j], recv_sem=recv_sems.at[3 * a + j],
                    device_id=(px, py, c), device_id_type=MESH))
        return local, copies

    def start(p_refs, land_refs, sems):
        local, copies = plan(p_refs, land_refs, sems)
        for cp in local + copies:
            cp.start()

    def finish(p_refs, land_refs, sems):
        local, copies = plan(p_refs, land_refs, sems)
        for cp in copies:
            cp.wait()
        for cp in local:
            cp.wait()

    scratch = [pltpu.SemaphoreType.DMA((3 * n,)), pltpu.SemaphoreType.DMA((3 * n,)), pltpu.SemaphoreType.DMA((n,))]
    return _Comm(parts, [_sds(p.shape, BF16) for p in parts], scratch, start, finish)


def _sum_slots(land, name):
    k, r, c = land.shape
    tr = r // 2 if r % 32 == 0 else r

    def body(l_ref, o_ref):
        acc = l_ref[0].astype(F32)
        for i in range(1, k):
            acc = acc + l_ref[i].astype(F32)
        o_ref[...] = acc

    return pl.pallas_call(
        body, name=name, grid=(r // tr,), in_specs=[pl.BlockSpec((k, tr, c), lambda i: (0, i, 0))],
        out_specs=_rows(tr, c), out_shape=_sds((r, c)), compiler_params=_cparams(1),
    )(land)


def _sibling_swap(halves):
    n = len(halves)

    def body(*refs):
        h_refs, out_refs = refs[:n], refs[n:2 * n]
        send_sems, recv_sems = refs[2 * n:]
        x, y, c = _mesh_pos()
        copies = [pltpu.make_async_remote_copy(
            src_ref=h_refs[a], dst_ref=out_refs[a], send_sem=send_sems.at[a], recv_sem=recv_sems.at[a],
            device_id=(x, y, 1 - c), device_id_type=MESH) for a in range(n)]
        for cp in copies:
            cp.start()
        for cp in copies:
            cp.wait()

    return pl.pallas_call(
        body, name="grad_sibling_swap", out_shape=[_sds(h.shape) for h in halves],
        in_specs=[pl.BlockSpec(memory_space=pltpu.HBM)] * n, out_specs=[pl.BlockSpec(memory_space=pltpu.HBM)] * n,
        scratch_shapes=[pltpu.SemaphoreType.DMA((n,)), pltpu.SemaphoreType.DMA((n,))],
    )(*halves)


def _pad_rows(v, width):
    flat = v.reshape(-1)
    rows = -(-flat.shape[0] // width)
    rows = -(-rows // 8) * 8
    return jnp.pad(flat, (0, rows * width - flat.shape[0])).reshape(rows, width)


def _size(shape):
    n = 1
    for dim in shape:
        n *= dim
    return n


def _row_pack(arrs):
    pieces = []
    for a in arrs:
        rows = -(-a.size // D)
        pieces.append(jnp.pad(a.reshape(-1), (0, rows * D - a.size)).reshape(rows, D))
    total = sum(p.shape[0] for p in pieces)
    if total % 8:
        pieces.append(jnp.zeros((8 - total % 8, D), F32))
    return jnp.concatenate(pieces, axis=0)


def _row_unpack(packed, shapes):
    out, r0 = [], 0
    for shp in shapes:
        n = _size(shp)
        rows = -(-n // D)
        out.append(packed[r0:r0 + rows].reshape(-1)[:n].reshape(shp))
        r0 += rows
    return out


def _block_diag(w):
    eye = jnp.eye(8, dtype=w.dtype)
    return (w[:, :, None, :] * eye[:, None, :, None]).reshape(RGW, RGW)


def _diag_blocks(dense):
    r = dense.reshape(8, 64, 8, 64)
    return jnp.stack([r[n, :, n, :] for n in range(8)])


def _lane_row(v8):
    return jnp.zeros((1, BAP), F32).at[0, 8:16].set(v8.reshape(8))


def _reduce_parts(gs, names, c_arr, tag):
    lands = _sibling_exchange(gs, "grad_sibling_exchange_" + tag)
    return [_chip_sum(g, l, c_arr, "chip_sum_" + n) for g, l, n in zip(gs, lands, names)]


def _local_step(x, target, sw, ffn1_w, later_shards, c_arr):
    (g1, gmix, rg_cw8, rg_cb, wgates, gbias, lam_row, gdn_cw8, alog_row, dtb_row, gn, g2, gfin) = sw
    wg1, wu1, wd1 = ffn1_w

    (x1, a1, b1), gathered = _ffn_fwd(x, g1, wg1, wu1, wd1, "ffn1_fwd", comm=_weights_gather_comm(later_shards))
    win_sh, wout_sh, wg2, wu2, wd2 = _all_shards(gathered)
    w_in_full = jnp.transpose(win_sh, (1, 0, 2)).reshape(D, NSH * INSH)
    wout = wout_sh.reshape(D, D)
    w_in_groups = (w_in_full[:, 0:512], w_in_full[:, 512:1024], w_in_full[:, 1024:2560], w_in_full[:, 2560:3072],
                   jnp.pad(w_in_full[:, 3072:3088], ((0, 0), (0, BAP - BAW))))
    h2, p_rgx, p_gate, p_qkv, p_z, p_ba = _inproj(x1, gmix, w_in_groups, "in_proj")
    c_rg = _conv(p_rgx, rg_cw8, rg_cb, "rg_conv")
    c_qkv = _conv(p_qkv, gdn_cw8, jnp.zeros((1, QKVW), F32), "gdn_conv")
    a0, bb0, a1s, bb1, q, k, v, bg = _mix_prep(c_rg, c_qkv, p_ba, wgates, gbias, lam_row, alog_row, dtb_row, "mix_prep")
    hf, hb = _scan_pair(a0, bb0, a1s, bb1, False, "rg_scan")
    tmat, gu, gw, gqd, gkd, gat, gcd = _gdn_local_fwd(q, k, v, bg, "gdn_local_fwd")
    of, s0, vn0, ob, s1, vn1 = _gdn_seq_fwd(gu, gw, gqd, gkd, gat, gcd, "gdn_seq_fwd")
    x2, ymix = _outproj(x1, hf, hb, p_gate, of, ob, p_z, gn, wout, "out_proj")
    (x3, a2, b2), _ = _ffn_fwd(x2, g2, wg2, wu2, wd2, "ffn2_fwd")
    dx3, loss_blk, d_gfin = _loss_head(x3, target, gfin, "loss_head")

    (dx2, d_g2, hb2, dob2, fb2, dab2, dbb2), _ = _ffn_bwd(x2, dx3, g2, a2, b2, wg2, wu2, wd2, "ffn2_bwd")
    d_ffn2 = [_tn(hb2, dab2, "ffn2_dwg"), _tn(hb2, dbb2, "ffn2_dwu"), _tn(fb2, dob2, "ffn2_dwd")]
    parts_ffn2 = _reduce_parts(d_ffn2, _BIG_NAMES[5:8], c_arr, "ffn2")

    d_hr, d_gate, d_os, d_z, d_gn, dx2b = _outproj_bwd(dx2, hf, hb, p_gate, of, ob, p_z, gn, wout, "out_proj_bwd")
    d_wout = _tn(ymix, dx2b, "dw_out")[0]

    lam1, lam0 = _scan_pair(a1s, d_hr, a0, d_hr, True, "rg_scan_bwd")
    d_xc, d_pre, xcb, d_gbias, d_lam = _gates_bwd(c_rg, wgates, gbias, lam_row, lam0, lam1, hf, hb, "rg_gates_bwd")
    d_wgates = _tn(xcb, d_pre, "dw_gates")[0]
    d_prgx, d_rgcw8, d_rgcb = _conv_bwd(p_rgx, d_xc, rg_cw8, "rg_conv_bwd")

    sg = _gdn_seq_bwd(d_os, gw, gqd, gkd, gat, gcd, (s0, s1), (vn0, vn1), "gdn_seq_bwd")
    (dq, dk, dv, dbg), lands_ffn2 = _gdn_local_bwd(q, k, v, bg, tmat, d_os, (vn0, vn1), (sg[0:5], sg[5:10]), "gdn_local_bwd",
                                                  comm=_scatter_comm(parts_ffn2))
    d_cqkv, d_pba, d_alog, d_dtb = _prep_bwd(c_qkv, p_ba, alog_row, dtb_row, dq, dk, dv, dbg, "gdn_prep_bwd")
    d_pqkv, d_gdncw8, _ = _conv_bwd(p_qkv, d_cqkv, gdn_cw8, "gdn_conv_bwd")

    dps = (d_prgx, d_gate, d_pqkv, d_z, d_pba)
    dx1, d_gmix = _inproj_bwd(x1, dx2, gmix, dps, w_in_groups, "in_proj_bwd")
    d_win_groups = [_tn(h2, dp, "dw_in_%d" % i)[0] for i, dp in enumerate(dps)]
    d_win = jnp.concatenate(d_win_groups[:4] + [d_win_groups[4][:, :BAW]], axis=1)
    d_mix = [jnp.transpose(d_win.reshape(D, NSH, INSH), (1, 0, 2)), d_wout.reshape(NSH, OUTSH, D)]
    parts_mix = _reduce_parts(d_mix, _BIG_NAMES[3:5], c_arr, "mix")

    (gx, d_g1, hb1, dob1, fb1, dab1, dbb1), lands_mix = _ffn_bwd(x, dx1, g1, a1, b1, wg1, wu1, wd1, "ffn1_bwd",
                                                                comm=_scatter_comm(parts_mix))
    d_ffn1 = [_tn(hb1, dab1, "ffn1_dwg"), _tn(hb1, dbb1, "ffn1_dwu"), _tn(fb1, dob1, "ffn1_dwd")]
    parts_ffn1 = _reduce_parts(d_ffn1, _BIG_NAMES[0:3], c_arr, "ffn1")
    lands_ffn1 = _comm_call(_scatter_comm(parts_ffn1), "grad_chip_scatter_ffn1")

    halves = [_sum_slots(l, "sum_chips_" + n) for l, n in zip(lands_ffn1 + lands_mix + lands_ffn2, _BIG_NAMES)]
    small = dict(
        ffn1_norm=d_g1, mix_norm=d_gmix, rg_conv_w=d_rgcw8[:4], rg_conv_b=d_rgcb,
        rg_gate_a_w=jnp.stack([_diag_blocks(d_wgates[:, RGW * i:RGW * (i + 1)]) for i in (0, 1)]),
        rg_gate_x_w=jnp.stack([_diag_blocks(d_wgates[:, RGW * i:RGW * (i + 1)]) for i in (2, 3)]),
        rg_gate_a_b=d_gbias[0, :2 * RGW].reshape(2, RGW), rg_gate_x_b=d_gbias[0, 2 * RGW:].reshape(2, RGW),
        rg_lambda=d_lam.reshape(2, RGW), gdn_conv_w=d_gdncw8[:4],
        gdn_a_log=d_alog[0, 8:16].reshape(2, NH), gdn_dt_bias=d_dtb[0, 8:16].reshape(2, NH),
        gdn_norm=d_gn, ffn2_norm=d_g2, final_norm=d_gfin)
    return loss_blk, gx, halves, small


_SMALL_NAMES = ("ffn1_norm", "mix_norm", "rg_conv_w", "rg_conv_b", "rg_gate_a_w", "rg_gate_a_b", "rg_gate_x_w",
                "rg_gate_x_b", "rg_lambda", "gdn_conv_w", "gdn_a_log", "gdn_dt_bias", "gdn_norm", "ffn2_norm", "final_norm")
_SMALL_SHARDED = dict(rg_conv_w=128, rg_gate_a_b=128, rg_gate_x_b=128, rg_lambda=128, gdn_conv_w=384)
_OUT_ORDER = ("ffn1_norm", "ffn1_w_gate", "ffn1_w_up", "ffn1_w_down", "mix_norm", "w_in", "w_out", "rg_conv_w", "rg_conv_b",
              "rg_gate_a_w", "rg_gate_a_b", "rg_gate_x_w", "rg_gate_x_b", "rg_lambda", "gdn_conv_w", "gdn_a_log",
              "gdn_dt_bias", "gdn_norm", "ffn2_norm", "ffn2_w_gate", "ffn2_w_up", "ffn2_w_down", "final_norm")
_BIG_NAMES = ("ffn1_w_gate", "ffn1_w_up", "ffn1_w_down", "w_in", "w_out", "ffn2_w_gate", "ffn2_w_up", "ffn2_w_down")


def kernel(x, ffn1_norm, ffn1_w_gate, ffn1_w_up, ffn1_w_down, mix_norm, w_in, w_out, rg_conv_w, rg_conv_b, rg_gate_a_w, rg_gate_a_b, rg_gate_x_w, rg_gate_x_b, rg_lambda, gdn_conv_w, gdn_a_log, gdn_dt_bias, gdn_norm, ffn2_norm, ffn2_w_gate, ffn2_w_up, ffn2_w_down, final_norm, loss_target, m_ffn1_norm, m_ffn1_w_gate, m_ffn1_w_up, m_ffn1_w_down, m_mix_norm, m_w_in, m_w_out, m_rg_conv_w, m_rg_conv_b, m_rg_gate_a_w, m_rg_gate_a_b, m_rg_gate_x_w, m_rg_gate_x_b, m_rg_lambda, m_gdn_conv_w, m_gdn_a_log, m_gdn_dt_bias, m_gdn_norm, m_ffn2_norm, m_ffn2_w_gate, m_ffn2_w_up, m_ffn2_w_down, m_final_norm, v_ffn1_norm, v_ffn1_w_gate, v_ffn1_w_up, v_ffn1_w_down, v_mix_norm, v_w_in, v_w_out, v_rg_conv_w, v_rg_conv_b, v_rg_gate_a_w, v_rg_gate_a_b, v_rg_gate_x_w, v_rg_gate_x_b, v_rg_lambda, v_gdn_conv_w, v_gdn_a_log, v_gdn_dt_bias, v_gdn_norm, v_ffn2_norm, v_ffn2_w_gate, v_ffn2_w_up, v_ffn2_w_down, v_final_norm):
    args = dict(locals())
    w = {n: args[n] for n in _OUT_ORDER}
    mom = {n: args["m_" + n] for n in _OUT_ORDER}
    var = {n: args["v_" + n] for n in _OUT_ORDER}
    xi, yi, ci = _mesh_pos()
    shard = 2 * xi + yi

    big_bf16 = [w[n][0].astype(BF16) for n in _BIG_NAMES]
    ffn1_w = _all_shards(_comm_call(_weights_gather_comm(big_bf16[0:3]), "gather_ffn1_weights"))
    sm_local = _pad_rows(jnp.concatenate([w[n][0].reshape(-1) for n in _SMALL_SHARDED]), 128)
    sm_all = _gather_small(sm_local, "gather_small_weights")[0::2].reshape(NSH, -1)
    sm_full, off = {}, 0
    for n, wd_ in _SMALL_SHARDED.items():
        rows = w[n].shape[1]
        piece = sm_all[:, off:off + rows * wd_].reshape(NSH, rows, wd_)
        sm_full[n] = jnp.transpose(piece, (1, 0, 2)).reshape(rows, NSH * wd_)
        off += rows * wd_

    wa, wx = rg_gate_a_w[0], rg_gate_x_w[0]
    wgates = jnp.concatenate([_block_diag(wa[0]), _block_diag(wa[1]), _block_diag(wx[0]), _block_diag(wx[1])],
                             axis=1).astype(BF16)
    gbias = jnp.concatenate([sm_full["rg_gate_a_b"].reshape(1, -1), sm_full["rg_gate_x_b"].reshape(1, -1)], axis=1)
    sw = (ffn1_norm, mix_norm, jnp.pad(sm_full["rg_conv_w"], ((0, 4), (0, 0))), rg_conv_b, wgates, gbias,
          sm_full["rg_lambda"].reshape(1, -1), jnp.pad(sm_full["gdn_conv_w"], ((0, 4), (0, 0))), _lane_row(gdn_a_log),
          _lane_row(gdn_dt_bias), gdn_norm, ffn2_norm, final_norm.reshape(1, D))
    c_arr = ci.reshape(1).astype(jnp.int32)

    loss_blk, gx, halves, small = _local_step(x[0], loss_target[0], sw, ffn1_w, big_bf16[3:], c_arr)
    loss = lax.psum(loss_blk[0, 0], ("x", "y", "c"))
    grads = {}

    sm_grad = _row_pack([small[n] for n in _SMALL_NAMES])
    sm_sum = _sum_slots(_gather_small(sm_grad, "gather_small_grads"), "small_grad_sum")
    for n, g in zip(_SMALL_NAMES, _row_unpack(sm_sum, [small[n].shape for n in _SMALL_NAMES])):
        if n in _SMALL_SHARDED:
            wd_ = _SMALL_SHARDED[n]
            g = lax.dynamic_slice_in_dim(g, shard * wd_, wd_, axis=1)
        grads[n] = g.reshape(w[n].shape)

    delta, new_m, new_v = {}, {}, {}
    for n, own, recv in zip(_BIG_NAMES, halves, _sibling_swap(halves)):
        shp = w[n].shape
        outs4 = _adamw_halves(w[n][0], own, recv, mom[n][0], var[n][0], c_arr, "adamw_" + n)
        grads[n], delta[n], new_m[n], new_v[n] = [o.reshape(shp) for o in outs4]
    packs = [_row_pack([t[n] for n in _SMALL_NAMES]) for t in (w, grads, mom, var)]
    sm_shapes = [w[n].shape for n in _SMALL_NAMES]
    for dst, src in zip((delta, new_m, new_v), _adamw(*packs, "adamw_small")):
        for n, val in zip(_SMALL_NAMES, _row_unpack(src, sm_shapes)):
            dst[n] = val

    outs = [loss, gx[None]]
    for group in (grads, delta, new_m, new_v):
        outs += [group[n] for n in _OUT_ORDER]
    return tuple(outs)
```

```python
import functools

import jax
import jax.numpy as jnp
from jax import lax
from jax.experimental import pallas as pl
from jax.experimental.pallas import tpu as pltpu

F32 = jnp.float32
BF16 = jnp.bfloat16
EPS = 1e-6
D = 1024
NSH = 4
FSH = 704
RGW = 512
QKVW = 1536
ZW = 512
BAW = 16
BAP = 128
INSH = 772
OUTSH = 256
CHUNK = 64
NH = 4
DH = 128
RG_C = 8.0
VMEM_LIMIT = 52 * 1024 * 1024
MESH = pl.DeviceIdType.MESH

ADAM_LR = 0.001
ADAM_B1 = 0.9
ADAM_B2 = 0.999
ADAM_EPS = 1e-08
ADAM_WD = 0.01
ADAM_STEP = 10


def _cparams(n_grid):
    return pltpu.CompilerParams(dimension_semantics=("arbitrary",) * n_grid, vmem_limit_bytes=VMEM_LIMIT)


def _sig(x):
    return 0.5 + 0.5 * jnp.tanh(0.5 * x)


def _sig_pos(x):
    return 1.0 / (1.0 + jnp.exp(-x))


def _softplus(x):
    return jnp.maximum(x, 0.0) + jnp.log(1.0 + jnp.exp(-jnp.abs(x)))


def _neg_expm1(y):
    series = -y * (1.0 + y * (0.5 + y * (1.0 / 6 + y * (1.0 / 24 + y * (1.0 / 120 + y * (1.0 / 720 + y / 5040))))))
    return jnp.where(y > -0.3, series, 1.0 - jnp.exp(y))


_GELU_C = 0.7978845608028654


def _gelu(x):
    t = jnp.tanh(_GELU_C * (x + 0.044715 * x * x * x))
    return 0.5 * x * (1.0 + t)


def _gelu_grad(x):
    t = jnp.tanh(_GELU_C * (x + 0.044715 * x * x * x))
    return 0.5 * (1.0 + t) + 0.5 * x * (1.0 - t * t) * _GELU_C * (1.0 + 3 * 0.044715 * x * x)


def _silu_grad(x):
    s = _sig(x)
    return s * (1.0 + x * (1.0 - s))


def _dot(a, b):
    return jnp.dot(a.astype(BF16), b.astype(BF16), preferred_element_type=F32)


def _dot_nt(a, b):
    return lax.dot_general(a.astype(BF16), b.astype(BF16), (((1,), (1,)), ((), ())), preferred_element_type=F32)


def _dot_tn(a, b):
    return lax.dot_general(a.astype(BF16), b.astype(BF16), (((0,), (0,)), ((), ())), preferred_element_type=F32)


_NN = ((1,), (0,))
_NT = ((1,), (1,))
_TN = ((0,), (0,))


def _dg(a, b, dims):
    return lax.dot_general(a, b, (dims, ((), ())), preferred_element_type=F32)


def _split2(a):
    hi = a.astype(BF16)
    return hi, (a - hi.astype(F32)).astype(BF16)


def _dot3(a, b, dims=_NN):
    ah, al = _split2(a)
    bh, bl = _split2(b)
    return _dg(ah, bh, dims) + _dg(ah, bl, dims) + _dg(al, bh, dims)


def _dot_exact(e, x, dims, e_is_lhs):
    x0 = x.astype(BF16)
    r = x - x0.astype(F32)
    x1 = r.astype(BF16)
    x2 = (r - x1.astype(F32)).astype(BF16)
    eb = e.astype(BF16)
    if e_is_lhs:
        return _dg(eb, x0, dims) + _dg(eb, x1, dims) + _dg(eb, x2, dims)
    return _dg(x0, eb, dims) + _dg(x1, eb, dims) + _dg(x2, eb, dims)


def _rms(xv):
    r = lax.rsqrt(jnp.mean(xv * xv, axis=-1, keepdims=True) + EPS)
    return r, xv * r


def _rms_bwd(dy, xh, r, gain):
    dxh = dy * gain
    return r * (dxh - xh * jnp.mean(dxh * xh, axis=-1, keepdims=True))


def _colsum(v):
    return jnp.sum(v, axis=0, keepdims=True)


def _rows(t, c):
    return pl.BlockSpec((t, c), lambda i: (i, 0))


def _full(shape):
    n = len(shape)
    return pl.BlockSpec(shape, lambda i: (0,) * n)


def _sds(shape, dtype=F32):
    return jax.ShapeDtypeStruct(shape, dtype)


def _ffn_fwd(x, gain, wg, wu, wd, name, comm=None):
    s = x.shape[0]
    tm = min(512, s)

    def body(x_ref, g_ref, wg_ref, wu_ref, wd_ref, xo_ref, a_ref, b_ref, h_sc, acc):
        j = pl.program_id(1)

        @pl.when(j == 0)
        def _():
            _, xh = _rms(x_ref[...])
            h_sc[...] = (xh * g_ref[...]).astype(BF16)
            acc[...] = jnp.zeros_like(acc)

        h = h_sc[...]

        a = jnp.dot(h, wg_ref[0], preferred_element_type=F32)
        b = jnp.dot(h, wu_ref[0], preferred_element_type=F32)
        a_ref[0] = a.astype(BF16)
        b_ref[0] = b.astype(BF16)
        f = (a * _sig(a) * b).astype(BF16)
        acc[...] += jnp.dot(f, wd_ref[0], preferred_element_type=F32)

        @pl.when(j == NSH - 1)
        def _():
            xo_ref[...] = x_ref[...] + 0.5 * acc[...]

    return _pallas(
        body, comm, name=name, grid=(s // tm, NSH),
        in_specs=[pl.BlockSpec((tm, D), lambda i, j: (i, 0)), pl.BlockSpec((1, D), lambda i, j: (0, 0)),
                  pl.BlockSpec((1, D, FSH), lambda i, j: (j, 0, 0)), pl.BlockSpec((1, D, FSH), lambda i, j: (j, 0, 0)),
                  pl.BlockSpec((1, FSH, D), lambda i, j: (j, 0, 0))],
        out_specs=[pl.BlockSpec((tm, D), lambda i, j: (i, 0)), pl.BlockSpec((1, tm, FSH), lambda i, j: (j, i, 0)),
                   pl.BlockSpec((1, tm, FSH), lambda i, j: (j, i, 0))],
        out_shape=[_sds((s, D)), _sds((NSH, s, FSH), BF16), _sds((NSH, s, FSH), BF16)],
        scratch_shapes=[pltpu.VMEM((tm, D), BF16), pltpu.VMEM((tm, D), F32)],
        args=(x, gain, wg, wu, wd))


def _ffn_bwd(x, dout, gain, a, b, wg, wu, wd, name, comm=None):
    s = x.shape[0]
    tm = min(512, s)

    def body(x_ref, d_ref, g_ref, a_ref, b_ref, wg_ref, wu_ref, wd_ref,
             dx_ref, dg_ref, h_ref, do_ref, f_ref, da_ref, db_ref, do_sc, dh_acc):
        i = pl.program_id(0)
        j = pl.program_id(1)

        @pl.when(jnp.logical_and(i == 0, j == 0))
        def _():
            dg_ref[...] = jnp.zeros_like(dg_ref)

        @pl.when(j == 0)
        def _():
            _, xh = _rms(x_ref[...])
            h_ref[...] = (xh * g_ref[...]).astype(BF16)
            do = (0.5 * d_ref[...]).astype(BF16)
            do_sc[...] = do
            do_ref[...] = do
            dh_acc[...] = jnp.zeros_like(dh_acc)

        do = do_sc[...]

        df = _dot_nt(do, wd_ref[0])
        av = a_ref[0].astype(F32)
        bv = b_ref[0].astype(F32)
        sa = _sig(av)
        f_ref[0] = (av * sa * bv).astype(BF16)
        da = (df * bv * sa * (1.0 + av * (1.0 - sa))).astype(BF16)
        db = (df * av * sa).astype(BF16)
        da_ref[0] = da
        db_ref[0] = db
        dh_acc[...] += _dot_nt(da, wg_ref[0]) + _dot_nt(db, wu_ref[0])

        @pl.when(j == NSH - 1)
        def _():
            r, xh = _rms(x_ref[...])
            dh = dh_acc[...]
            dg_ref[...] += _colsum(dh * xh)
            dx_ref[...] = d_ref[...] + _rms_bwd(dh, xh, r, g_ref[...])

    tok = pl.BlockSpec((tm, D), lambda i, j: (i, 0))
    sh = pl.BlockSpec((1, tm, FSH), lambda i, j: (j, i, 0))
    return _pallas(
        body, comm, name=name, grid=(s // tm, NSH),
        in_specs=[tok, tok, pl.BlockSpec((1, D), lambda i, j: (0, 0)), sh, sh,
                  pl.BlockSpec((1, D, FSH), lambda i, j: (j, 0, 0)), pl.BlockSpec((1, D, FSH), lambda i, j: (j, 0, 0)),
                  pl.BlockSpec((1, FSH, D), lambda i, j: (j, 0, 0))],
        out_specs=[tok, pl.BlockSpec((1, D), lambda i, j: (0, 0)), tok, tok, sh, sh, sh],
        out_shape=[_sds((s, D)), _sds((1, D)), _sds((s, D), BF16), _sds((s, D), BF16),
                   _sds((NSH, s, FSH), BF16), _sds((NSH, s, FSH), BF16), _sds((NSH, s, FSH), BF16)],
        scratch_shapes=[pltpu.VMEM((tm, D), BF16), pltpu.VMEM((tm, D), F32)],
        args=(x, dout, gain, a, b, wg, wu, wd))


def _tn(a, b, name, comm=None):
    a_g = a.ndim == 3
    b_g = b.ndim == 3
    g = a.shape[0] if a_g else (b.shape[0] if b_g else 1)
    s, k = a.shape[-2:]
    n = b.shape[-1]
    ts = min(1024, s)

    def body(a_ref, b_ref, o_ref):
        @pl.when(pl.program_id(1) == 0)
        def _():
            o_ref[...] = jnp.zeros_like(o_ref)

        av = a_ref[0] if a_g else a_ref[...]
        bv = b_ref[0] if b_g else b_ref[...]
        o_ref[0] += _dot_tn(av, bv)

    a_spec = pl.BlockSpec((1, ts, k), lambda gi, si: (gi, si, 0)) if a_g else pl.BlockSpec((ts, k), lambda gi, si: (si, 0))
    b_spec = pl.BlockSpec((1, ts, n), lambda gi, si: (gi, si, 0)) if b_g else pl.BlockSpec((ts, n), lambda gi, si: (si, 0))
    outs, carried = _pallas(body, comm, name=name, grid=(g, s // ts), in_specs=[a_spec, b_spec],
                            out_specs=[pl.BlockSpec((1, k, n), lambda gi, si: (gi, 0, 0))], out_shape=[_sds((g, k, n))],
                            scratch_shapes=[], args=(a, b))
    return outs[0] if comm is None else (outs[0], carried)


_P_WIDTHS = (RGW, RGW, QKVW, ZW, BAP)


def _inproj(x1, gain, ws, name):
    s = x1.shape[0]
    tm = min(256, s)

    def body(x_ref, g_ref, *refs):
        w_refs = refs[:5]
        h_ref = refs[5]
        p_refs = refs[6:]
        _, xh = _rms(x_ref[...])
        h = (xh * g_ref[...]).astype(BF16)
        h_ref[...] = h
        for w_ref, p_ref in zip(w_refs, p_refs):
            p_ref[...] = jnp.dot(h, w_ref[...], preferred_element_type=F32)

    return pl.pallas_call(
        body, name=name, grid=(s // tm,),
        in_specs=[_rows(tm, D), _full((1, D))] + [_full((D, w)) for w in _P_WIDTHS],
        out_specs=[_rows(tm, D)] + [_rows(tm, w) for w in _P_WIDTHS],
        out_shape=[_sds((s, D), BF16)] + [_sds((s, w)) for w in _P_WIDTHS],
        compiler_params=_cparams(1),
    )(x1, gain, *ws)


def _inproj_bwd(x1, dx2, gain, dps, ws, name):
    s = x1.shape[0]
    tm = min(256, s)

    def body(x_ref, d_ref, g_ref, *refs):
        dp_refs = refs[:5]
        w_refs = refs[5:10]
        dx_ref, dg_ref = refs[10:]

        @pl.when(pl.program_id(0) == 0)
        def _():
            dg_ref[...] = jnp.zeros_like(dg_ref)

        dh = jnp.zeros((tm, D), F32)
        for dp_ref, w_ref in zip(dp_refs, w_refs):
            dh = dh + _dot_nt(dp_ref[...], w_ref[...])
        r, xh = _rms(x_ref[...])
        dg_ref[...] += _colsum(dh * xh)
        dx_ref[...] = d_ref[...] + _rms_bwd(dh, xh, r, g_ref[...])

    return pl.pallas_call(
        body, name=name, grid=(s // tm,),
        in_specs=[_rows(tm, D), _rows(tm, D), _full((1, D))] + [_rows(tm, w) for w in _P_WIDTHS]
        + [_full((D, w)) for w in _P_WIDTHS],
        out_specs=[_rows(tm, D), _full((1, D))],
        out_shape=[_sds((s, D)), _sds((1, D))],
        compiler_params=_cparams(1),
    )(x1, dx2, gain, *dps, *ws)


def _halo_specs(s, t, c):
    nb8 = s // 8
    tb = t // 8
    prev = pl.BlockSpec((8, c), lambda i: (jnp.maximum(i * tb - 1, 0), 0))
    nxt = pl.BlockSpec((8, c), lambda i: (jnp.minimum((i + 1) * tb, nb8 - 1), 0))
    return prev, nxt


def _edge_masks(nb):
    i = pl.program_id(0)
    return jnp.where(i > 0, 1.0, 0.0).astype(F32), jnp.where(i < nb - 1, 1.0, 0.0).astype(F32)


def _shifted(xx, off, t):
    n = t + 16
    sh = (-off) % n
    rolled = xx if sh == 0 else pltpu.roll(xx, sh, 0)
    return rolled[8:8 + t]


def _conv(x, w8, bias, name):
    s, c = x.shape
    t = min(256, s)
    nb = s // t

    def body(x_ref, xp_ref, xn_ref, w_ref, b_ref, o_ref):
        pm, nm = _edge_masks(nb)
        for c0 in range(0, c, 512):
            cols = slice(c0, c0 + 512)
            xx = jnp.concatenate([xp_ref[:, cols] * pm, x_ref[:, cols], xn_ref[:, cols] * nm], axis=0)
            acc = jnp.zeros((t, 512), F32) + b_ref[:, cols]
            for j in range(4):
                acc = acc + w_ref[j:j + 1, cols] * _shifted(xx, j - 2, t)
            o_ref[:, cols] = acc

    prev, nxt = _halo_specs(s, t, c)
    return pl.pallas_call(
        body, name=name, grid=(nb,),
        in_specs=[_rows(t, c), prev, nxt, _full((8, c)), _full((1, c))],
        out_specs=_rows(t, c), out_shape=_sds((s, c)), compiler_params=_cparams(1),
    )(x, x, x, w8, bias)


def _conv_bwd(x, dc, w8, name):
    s, c = x.shape
    t = min(256, s)
    nb = s // t

    def body(x_ref, d_ref, dp_ref, dn_ref, w_ref, dx_ref, dw_ref, db_ref):
        @pl.when(pl.program_id(0) == 0)
        def _():
            dw_ref[...] = jnp.zeros_like(dw_ref)
            db_ref[...] = jnp.zeros_like(db_ref)

        pm, nm = _edge_masks(nb)
        for c0 in range(0, c, 512):
            cols = slice(c0, c0 + 512)
            dd = jnp.concatenate([dp_ref[:, cols] * pm, d_ref[:, cols], dn_ref[:, cols] * nm], axis=0)
            xv = x_ref[:, cols]
            acc = jnp.zeros((t, 512), F32)
            for j in range(4):
                dsh = _shifted(dd, 2 - j, t)
                acc = acc + w_ref[j:j + 1, cols] * dsh
                dw_ref[j:j + 1, cols] += _colsum(dsh * xv)
            dx_ref[:, cols] = acc
            db_ref[:, cols] += _colsum(d_ref[:, cols])

    prev, nxt = _halo_specs(s, t, c)
    return pl.pallas_call(
        body, name=name, grid=(nb,),
        in_specs=[_rows(t, c), _rows(t, c), prev, nxt, _full((8, c))],
        out_specs=[_rows(t, c), _full((8, c)), _full((1, c))],
        out_shape=[_sds((s, c)), _sds((8, c)), _sds((1, c))], compiler_params=_cparams(1),
    )(x, dc, dc, dc, w8)


def _rg_gates(xc, pre, lam_row):
    sp8 = RG_C * _softplus(-lam_row)
    out = []
    for d in range(2):
        r = _sig_pos(pre[:, RGW * d:RGW * (d + 1)])
        gi = _sig(pre[:, 2 * RGW + RGW * d:2 * RGW + RGW * (d + 1)])
        la = -r * sp8[:, RGW * d:RGW * (d + 1)]
        a = jnp.exp(la)
        mult = jnp.sqrt(_neg_expm1(2.0 * la))
        out.append((r, gi, a, mult))
    return out


def _mix_prep(c_rg, c_qkv, p_ba, wgates, gbias, lam_row, alog_row, dtb_row, name):
    s = c_rg.shape[0]
    t = min(256, s)

    def body(xc_ref, cq_ref, pc_ref, wg_ref, gb_ref, lam_ref, alog_ref, dtb_ref,
             a0_ref, b0_ref, a1_ref, b1_ref, q_ref, k_ref, v_ref, bg_ref):
        xc = xc_ref[...]
        pre = _dot(xc, wg_ref[...]) + gb_ref[...]
        gates = _rg_gates(xc, pre, lam_ref[...])
        for (r, gi, a, mult), a_ref, b_ref in zip(gates, (a0_ref, a1_ref), (b0_ref, b1_ref)):
            a_ref[...] = a
            b_ref[...] = mult * gi * xc
        cq = cq_ref[...]
        sq = cq * _sig(cq)
        for h in range(NH):
            sl = slice(DH * h, DH * (h + 1))
            qh = sq[:, sl]
            q_ref[:, sl] = qh * lax.rsqrt(jnp.sum(qh * qh, axis=-1, keepdims=True) + EPS) * (DH ** -0.5)
            kh = sq[:, RGW + DH * h:RGW + DH * (h + 1)]
            k_ref[:, sl] = kh * lax.rsqrt(jnp.sum(kh * kh, axis=-1, keepdims=True) + EPS)
        v_ref[...] = sq[:, 2 * RGW:]
        pc = pc_ref[...]
        lane = lax.broadcasted_iota(jnp.int32, pc.shape, 1)
        beta = _sig(pc)
        g = -jnp.exp(alog_ref[...]) * _softplus(pc + dtb_ref[...])
        bg_ref[...] = jnp.where(lane < 8, beta, jnp.where(lane < 16, g, 0.0))

    return pl.pallas_call(
        body, name=name, grid=(s // t,),
        in_specs=[_rows(t, RGW), _rows(t, QKVW), _rows(t, BAP), _full((RGW, 4 * RGW)), _full((1, 4 * RGW)),
                  _full((1, 2 * RGW)), _full((1, BAP)), _full((1, BAP))],
        out_specs=[_rows(t, RGW)] * 7 + [_rows(t, BAP)],
        out_shape=[_sds((s, RGW))] * 7 + [_sds((s, BAP))],
        compiler_params=_cparams(1),
    )(c_rg, c_qkv, p_ba, wgates, gbias, lam_row, alog_row, dtb_row)


def _scan_pair(af, bf, ar, br, shifted, name):
    s, c = af.shape
    t = min(512, s)
    nb = s // t
    ng = t // 8
    tb = t // 8
    up = lambda i: (i, 0)
    down = lambda i: (nb - 1 - i, 0)

    def body(*refs):
        if shifted:
            af_ref, bf_ref, ar_ref, br_ref, afp_ref, arn_ref, hf_ref, hr_ref, carry, fbuf, rbuf = refs
        else:
            af_ref, bf_ref, ar_ref, br_ref, hf_ref, hr_ref, carry = refs
        i = pl.program_id(0)

        @pl.when(i == 0)
        def _():
            carry[...] = jnp.zeros_like(carry)

        if shifted:
            edge = jnp.where(i > 0, 1.0, 0.0).astype(F32)
            fbuf[0:8, :] = afp_ref[...] * edge
            fbuf[8:t + 8, :] = af_ref[...]
            rbuf[0:t, :] = ar_ref[...]
            rbuf[t:t + 8, :] = arn_ref[...] * edge
        row = lax.broadcasted_iota(jnp.int32, (8, c), 0)

        def block_scan(av, bv, downwards):
            for k in (1, 2, 4):
                sh = (8 - k) if downwards else k
                m = (row < 8 - k) if downwards else (row >= k)
                a_s = pltpu.roll(av, sh, 0)
                b_s = pltpu.roll(bv, sh, 0)
                bv = jnp.where(m, av * b_s + bv, bv)
                av = jnp.where(m, av * a_s, av)
            return av, bv

        def group(gi, cvs):
            cf, cr = cvs
            rf = pl.multiple_of(gi * 8, 8)
            rr = pl.multiple_of((ng - 1 - gi) * 8, 8)
            if shifted:
                a_f = jnp.where(row > 0, pltpu.roll(fbuf[pl.ds(rf + 8, 8), :], 1, 0), pltpu.roll(fbuf[pl.ds(rf, 8), :], 1, 0))
                a_r = jnp.where(row < 7, pltpu.roll(rbuf[pl.ds(rr, 8), :], 7, 0), pltpu.roll(rbuf[pl.ds(rr + 8, 8), :], 7, 0))
            else:
                a_f = af_ref[pl.ds(rf, 8), :]
                a_r = ar_ref[pl.ds(rr, 8), :]
            a_f, b_f = block_scan(a_f, bf_ref[pl.ds(rf, 8), :], False)
            a_r, b_r = block_scan(a_r, br_ref[pl.ds(rr, 8), :], True)
            h_f = a_f * cf + b_f
            h_r = a_r * cr + b_r
            hf_ref[pl.ds(rf, 8), :] = h_f
            hr_ref[pl.ds(rr, 8), :] = h_r
            return h_f[7:8, :], h_r[0:1, :]

        cf, cr = lax.fori_loop(0, ng, group, (carry[0:1, :], carry[8:9, :]))
        carry[0:1, :] = cf
        carry[8:9, :] = cr

    in_specs = [pl.BlockSpec((t, c), up), pl.BlockSpec((t, c), up), pl.BlockSpec((t, c), down), pl.BlockSpec((t, c), down)]
    args = [af, bf, ar, br]
    scratch = [pltpu.VMEM((16, c), F32)]
    if shifted:
        in_specs += [pl.BlockSpec((8, c), lambda i: (jnp.maximum(i * tb - 1, 0), 0)),
                     pl.BlockSpec((8, c), lambda i: (jnp.minimum((nb - i) * tb, s // 8 - 1), 0))]
        args += [af, ar]
        scratch += [pltpu.VMEM((t + 8, c), F32), pltpu.VMEM((t + 8, c), F32)]
    return pl.pallas_call(
        body, name=name, grid=(nb,), in_specs=in_specs,
        out_specs=[pl.BlockSpec((t, c), up), pl.BlockSpec((t, c), down)], out_shape=[_sds((s, c)), _sds((s, c))],
        scratch_shapes=scratch, compiler_params=_cparams(1),
    )(*args)


def _gates_bwd(xc, wgates, gbias, lam_row, lam0, lam1, hf, hb, name):
    s = xc.shape[0]
    t = min(256, s)
    nb = s // t

    def body(xc_ref, wg_ref, gb_ref, lam_ref, l0_ref, l1_ref, hf_ref, hfp_ref, hfn_ref, hb_ref, hbp_ref, hbn_ref,
             dxc_ref, dpre_ref, xcb_ref, dgb_ref, dlam_ref):
        @pl.when(pl.program_id(0) == 0)
        def _():
            dgb_ref[...] = jnp.zeros_like(dgb_ref)
            dlam_ref[...] = jnp.zeros_like(dlam_ref)

        pm, nm = _edge_masks(nb)
        h_prev = _shifted(jnp.concatenate([hfp_ref[...] * pm, hf_ref[...], hfn_ref[...] * nm], axis=0), -1, t)
        h_next = _shifted(jnp.concatenate([hbp_ref[...] * pm, hb_ref[...], hbn_ref[...] * nm], axis=0), 1, t)
        h_shift = (h_prev, h_next)
        xv = xc_ref[...]
        pre = _dot(xv, wg_ref[...]) + gb_ref[...]
        lam_row_v = lam_ref[...]
        sp8 = RG_C * _softplus(-lam_row_v)
        dsp_dlam = -RG_C * _sig(-lam_row_v)
        gates = _rg_gates(xv, pre, lam_row_v)
        dxc = jnp.zeros((t, RGW), F32)
        dpre_r = []
        dpre_i = []
        for d, ((r, gi, a, mult), l_ref, hs) in enumerate(zip(gates, (l0_ref, l1_ref), h_shift)):
            dbb = l_ref[...]
            da = dbb * hs
            cs = slice(RGW * d, RGW * (d + 1))
            dmult = dbb * gi * xv
            dgi = dbb * mult * xv
            dxc = dxc + dbb * mult * gi
            dla = da * a - dmult * a * a / mult
            dr = -dla * sp8[:, cs]
            dlam_ref[:, cs] += _colsum(-dla * r) * dsp_dlam[:, cs]
            dpre_r.append(dr * r * (1.0 - r))
            dpre_i.append(dgi * gi * (1.0 - gi))
        dpre = jnp.concatenate(dpre_r + dpre_i, axis=1)
        dgb_ref[...] += _colsum(dpre)
        dpre_b = dpre.astype(BF16)
        dpre_ref[...] = dpre_b
        xcb_ref[...] = xv.astype(BF16)
        dxc_ref[...] = dxc + _dot_nt(dpre_b, wg_ref[...])

    prev, nxt = _halo_specs(s, t, RGW)
    return pl.pallas_call(
        body, name=name, grid=(s // t,),
        in_specs=[_rows(t, RGW), _full((RGW, 4 * RGW)), _full((1, 4 * RGW)), _full((1, 2 * RGW))] + [_rows(t, RGW)] * 2
        + [_rows(t, RGW), prev, nxt] * 2,
        out_specs=[_rows(t, RGW), _rows(t, 4 * RGW), _rows(t, RGW), _full((1, 4 * RGW)), _full((1, 2 * RGW))],
        out_shape=[_sds((s, RGW)), _sds((s, 4 * RGW), BF16), _sds((s, RGW), BF16), _sds((1, 4 * RGW)), _sds((1, 2 * RGW))],
        compiler_params=_cparams(1),
    )(xc, wgates, gbias, lam_row, lam0, lam1, hf, hf, hf, hb, hb, hb)


class _GdnMasks:
    def __init__(self, d):
        ri = lax.broadcasted_iota(jnp.int32, (CHUNK, CHUNK), 0)
        ci = lax.broadcasted_iota(jnp.int32, (CHUNK, CHUNK), 1)
        self.incl = (ri >= ci) if d == 0 else (ri <= ci)
        self.strict = (ri > ci) if d == 0 else (ri < ci)
        b16 = jnp.right_shift(ri, 4) == jnp.right_shift(ci, 4)
        b32 = jnp.right_shift(ri, 5) == jnp.right_shift(ci, 5)
        self.diag16 = b16
        self.off32 = jnp.logical_and(b32, jnp.logical_not(b16))
        self.off64 = jnp.logical_not(b32)
        self.eye = jnp.where(ri == ci, 1.0, 0.0).astype(F32)
        self.tri = jnp.where(self.incl, 1.0, 0.0).astype(F32)
        self.last = CHUNK - 1 if d == 0 else 0


def _tri_inv(lmat, m):
    return _tri_inv_many([lmat], [m])[0]


def _tri_inv_many(lmats, masks):
    n = len(lmats)
    ns = [jnp.where(masks[i].diag16, lmats[i], 0.0) for i in range(n)]
    ps = [masks[i].eye - ns[i] for i in range(n)]
    qs = [_dot3(ns[i], ns[i]) for i in range(n)]
    for step in range(3):
        ps = [_dot3(ps[i], masks[i].eye + qs[i]) for i in range(n)]
        if step < 2:
            qs = [_dot3(qs[i], qs[i]) for i in range(n)]
    for off in ("off32", "off64"):
        ts = [_dot3(ps[i], jnp.where(getattr(masks[i], off), lmats[i], 0.0)) for i in range(n)]
        ps = [ps[i] - _dot3(ts[i], ps[i]) for i in range(n)]
    return ps


def _chunk_cumsums(m, bgv):
    return _dot_exact(m.tri, bgv, _NN, True), _dot_exact(m.tri, bgv, ((0,), (1,)), False)


class _GdnHead:
    def __init__(self, qh, kh, vh, kk, q0, bg, gcs, gcs_t, d, h, m):
        cb = 4 * d + h
        cg = 8 + 4 * d + h
        self.q, self.k, self.v = qh, kh, vh
        self.beta = bg[:, cb:cb + 1]
        gcol = gcs[:, cg:cg + 1]
        grow = gcs_t[cg:cg + 1, :]
        gl = gcs[m.last:m.last + 1, cg:cg + 1]
        self.decay = jnp.exp(jnp.where(m.incl, gcol - grow, -1e30))
        self.kb = kh * self.beta
        self.vb = vh * self.beta
        self.a0 = kk * self.beta
        self.q0 = q0
        self.lmat = jnp.where(m.strict, self.a0 * self.decay, 0.0)
        self.attn = self.q0 * self.decay
        self.eg = jnp.exp(gcol)
        self.ek = jnp.exp(gl - gcol)
        self.cd = jnp.exp(gl)
        self.kg = self.kb * self.eg
        self.qd = qh * self.eg
        self.kd = kh * self.ek


HW = NH * DH
SEQ_CB = 4


def _head(h):
    return slice(DH * h, DH * (h + 1))


def _gdn_local_fwd(q, k, v, bg, name):
    s = q.shape[0]
    n = s // CHUNK

    def body(q_ref, k_ref, v_ref, bg_ref, t_ref, u_ref, w_ref, qd_ref, kd_ref, at_ref, cd_ref):
        bgv = bg_ref[...]
        qs = [q_ref[:, _head(h)] for h in range(NH)]
        ks = [k_ref[:, _head(h)] for h in range(NH)]
        kk = [_dot_nt(ks[h], ks[h]) for h in range(NH)]
        q0 = [_dot_nt(qs[h], ks[h]) for h in range(NH)]
        inst = []
        for d in range(2):
            m = _GdnMasks(d)
            gcs, gcs_t = _chunk_cumsums(m, bgv)
            for h in range(NH):
                c = _GdnHead(qs[h], ks[h], v_ref[:, _head(h)], kk[h], q0[h], bgv, gcs, gcs_t, d, h, m)
                inst.append((d, h, m, c))
        tms = _tri_inv_many([c.lmat for _, _, _, c in inst], [m for _, _, m, _ in inst])
        for (d, h, m, c), tm in zip(inst, tms):
            sl = _head(h)
            t_ref[0, d, h] = tm
            u_ref[d, :, sl] = _dot(tm, c.vb)
            w_ref[d, :, sl] = _dot(tm, c.kg).astype(BF16)
            qd_ref[d, :, sl] = c.qd.astype(BF16)
            kd_ref[d, :, sl] = c.kd.astype(BF16)
            at_ref[0, d, h] = c.attn.astype(BF16)
            cd_ref[0, 4 * d + h:4 * d + h + 1, :] = jnp.broadcast_to(c.cd, (1, DH))

    tok = _rows(CHUNK, HW)
    tok2 = pl.BlockSpec((2, CHUNK, HW), lambda i: (0, i, 0))
    mat = pl.BlockSpec((1, 2, NH, CHUNK, CHUNK), lambda i: (i, 0, 0, 0, 0))
    return pl.pallas_call(
        body, name=name, grid=(n,), in_specs=[tok, tok, tok, _rows(CHUNK, BAP)],
        out_specs=[mat, tok2, tok2, tok2, tok2, mat, pl.BlockSpec((1, 8, DH), lambda i: (i, 0, 0))],
        out_shape=[_sds((n, 2, NH, CHUNK, CHUNK)), _sds((2, s, HW)), _sds((2, s, HW), BF16), _sds((2, s, HW), BF16),
                   _sds((2, s, HW), BF16), _sds((n, 2, NH, CHUNK, CHUNK), BF16), _sds((n, 8, DH))],
        compiler_params=_cparams(1),
    )(q, k, v, bg)


def _seq_specs(s, order):
    n = s // CHUNK
    cb = min(SEQ_CB, n)
    nb = n // cb
    tb = cb * CHUNK

    def blk(d):
        return (lambda i: i) if order[d] else (lambda i: nb - 1 - i)

    def per_dir(make):
        return [make(d, blk(d)) for d in range(2)]

    tok2 = per_dir(lambda d, f: pl.BlockSpec((1, tb, HW), lambda i: (d, f(i), 0)))
    tok = per_dir(lambda d, f: pl.BlockSpec((tb, HW), lambda i: (f(i), 0)))
    mat = per_dir(lambda d, f: pl.BlockSpec((cb, 1, NH, CHUNK, CHUNK), lambda i: (f(i), d, 0, 0, 0)))
    cds = per_dir(lambda d, f: pl.BlockSpec((cb, 8, DH), lambda i: (f(i), 0, 0)))
    sts = per_dir(lambda d, f: pl.BlockSpec((cb, NH, DH, DH), lambda i: (f(i), 0, 0, 0)))
    dcd = per_dir(lambda d, f: pl.BlockSpec((cb, NH, DH), lambda i: (f(i), 0, 0)))
    return n, cb, nb, tok2, tok, mat, cds, sts, dcd


def _gdn_seq_fwd(u, w, qd, kd, at, cd, name):
    s = u.shape[1]
    n, cb, nb, tok2, tok, mat, cds, sts, _ = _seq_specs(s, (True, False))

    def body(*refs):
        ins = (refs[0:6], refs[6:12])
        outs = (refs[12:15], refs[15:18])
        st = refs[18]

        @pl.when(pl.program_id(0) == 0)
        def _():
            st[...] = jnp.zeros_like(st)

        for j in range(cb):
            items = []
            for d in range(2):
                jj = j if d == 0 else cb - 1 - j
                items += [(d, h, jj, slice(CHUNK * jj, CHUNK * (jj + 1)), _head(h)) for h in range(NH)]
            shs = [st[d, h] for d, h, _, _, _ in items]
            wss = [_dot(ins[d][1][0, rows, sl], sh) for (d, h, jj, rows, sl), sh in zip(items, shs)]
            vns = [ins[d][0][0, rows, sl] - ws for (d, h, jj, rows, sl), ws in zip(items, wss)]
            news = [sh * ins[d][5][jj, 4 * d + h:4 * d + h + 1, :] + _dot_tn(ins[d][3][0, rows, sl], vn)
                    for (d, h, jj, rows, sl), sh, vn in zip(items, shs, vns)]
            for (d, h, jj, rows, sl), sh, vn, new in zip(items, shs, vns, news):
                o_r, s_r, vn_r = outs[d]
                st[d, h] = new
                s_r[jj, h] = sh
                vn_r[rows, sl] = vn
                o_r[rows, sl] = _dot(ins[d][2][0, rows, sl], sh) + _dot(ins[d][4][jj, 0, h], vn)

    in_specs, out_specs, out_shape = [], [], []
    for d in range(2):
        in_specs += [tok2[d]] * 4 + [mat[d], cds[d]]
        out_specs += [tok[d], sts[d], tok[d]]
        out_shape += [_sds((s, HW)), _sds((n, NH, DH, DH)), _sds((s, HW))]
    return pl.pallas_call(
        body, name=name, grid=(nb,), in_specs=in_specs, out_specs=out_specs, out_shape=out_shape,
        scratch_shapes=[pltpu.VMEM((2, NH, DH, DH), F32)], compiler_params=_cparams(1),
    )(u, w, qd, kd, at, cd, u, w, qd, kd, at, cd)


def _gdn_seq_bwd(do, w, qd, kd, at, cd, states, vns, name):
    s = do.shape[0]
    n, cb, nb, tok2, tok, mat, cds, sts, dcd = _seq_specs(s, (False, True))

    def body(*refs):
        ins = (refs[0:8], refs[8:16])
        outs = (refs[16:21], refs[21:26])
        dst = refs[26]

        @pl.when(pl.program_id(0) == 0)
        def _():
            dst[...] = jnp.zeros_like(dst)

        for j in range(cb):
            items = []
            for d in range(2):
                jj = cb - 1 - j if d == 0 else j
                items += [(d, h, jj, slice(CHUNK * jj, CHUNK * (jj + 1)), _head(h)) for h in range(NH)]
            dsns = [dst[d, h] for d, h, _, _, _ in items]
            dohs = [ins[d][0][rows, sl] for d, h, jj, rows, sl in items]
            d_vns = [_dot_tn(ins[d][4][jj, 0, h], doh) + _dot(ins[d][3][0, rows, sl], dsn)
                     for (d, h, jj, rows, sl), doh, dsn in zip(items, dohs, dsns)]
            news = [ins[d][5][jj, 4 * d + h:4 * d + h + 1, :] * dsn + _dot_tn(ins[d][2][0, rows, sl], doh)
                    - _dot_tn(ins[d][1][0, rows, sl], d_vn)
                    for (d, h, jj, rows, sl), doh, dsn, d_vn in zip(items, dohs, dsns, d_vns)]
            for (d, h, jj, rows, sl), doh, dsn, d_vn, new in zip(items, dohs, dsns, d_vns, news):
                dvn_r, dkd_r, dqd_r, dw_r, dcd_r = outs[d]
                sh = ins[d][6][jj, h]
                dst[d, h] = new
                dvn_r[rows, sl] = d_vn
                dkd_r[rows, sl] = _dot_nt(ins[d][7][rows, sl], dsn)
                dqd_r[rows, sl] = _dot_nt(doh, sh)
                dw_r[rows, sl] = -_dot_nt(d_vn, sh)
                d_cd = jnp.sum(jnp.sum(sh * dsn, axis=1, keepdims=True), axis=0, keepdims=True)
                dcd_r[jj, h:h + 1, :] = jnp.broadcast_to(d_cd, (1, DH))

    in_specs, out_specs, out_shape, args = [], [], [], []
    for d in range(2):
        in_specs += [tok[d]] + [tok2[d]] * 3 + [mat[d], cds[d], sts[d], tok[d]]
        args += [do, w, qd, kd, at, cd, states[d], vns[d]]
        out_specs += [tok[d]] * 4 + [dcd[d]]
        out_shape += [_sds((s, HW))] * 4 + [_sds((n, NH, DH))]
    return pl.pallas_call(
        body, name=name, grid=(nb,), in_specs=in_specs, out_specs=out_specs, out_shape=out_shape,
        scratch_shapes=[pltpu.VMEM((2, NH, DH, DH), F32)], compiler_params=_cparams(1),
    )(*args)


def _gdn_local_bwd(q, k, v, bg, tmat, do, vns, seq_grads, name, comm=None):
    s = q.shape[0]
    n = s // CHUNK

    def body(*refs):
        q_ref, k_ref, v_ref, bg_ref, t_ref, do_ref = refs[0:6]
        vn_refs = refs[6:8]
        sg = (refs[8:13], refs[13:18])
        dq_ref, dk_ref, dv_ref, dbg_ref = refs[18:]
        bgv = bg_ref[...]
        qs = [q_ref[:, _head(h)] for h in range(NH)]
        ks = [k_ref[:, _head(h)] for h in range(NH)]
        kk = [_dot_nt(ks[h], ks[h]) for h in range(NH)]
        q0 = [_dot_nt(qs[h], ks[h]) for h in range(NH)]
        lane = lax.broadcasted_iota(jnp.int32, (CHUNK, BAP), 1)
        rowi = lax.broadcasted_iota(jnp.int32, (CHUNK, 1), 0)
        ones = jnp.ones((CHUNK, DH), F32)
        dbg = jnp.zeros((CHUNK, BAP), F32)
        acc = [[None, None, None] for _ in range(NH)]
        inst = []
        for d in range(2):
            m = _GdnMasks(d)
            gcs, gcs_t = _chunk_cumsums(m, bgv)
            for h in range(NH):
                c = _GdnHead(qs[h], ks[h], v_ref[:, _head(h)], kk[h], q0[h], bgv, gcs, gcs_t, d, h, m)
                inst.append((d, h, m, c))
        tms = [t_ref[0, d, h] for d, h, _, _ in inst]
        d_vns = [sg[d][0][:, _head(h)] for d, h, _, _ in inst]
        d_ws = [sg[d][3][:, _head(h)] for d, h, _, _ in inst]
        d_ts = [_dot_nt(d_vns[i], c.vb) + _dot_nt(d_ws[i], c.kg) for i, (_, _, _, c) in enumerate(inst)]
        xs = [_dot3(tms[i], d_ts[i], _TN) for i in range(8)]
        d_ls = [jnp.where(inst[i][2].strict, -_dot3(xs[i], tms[i], _NT), 0.0) for i in range(8)]
        d_attns = [jnp.where(m.incl, _dot_nt(do_ref[:, _head(h)], vn_refs[d][:, _head(h)]), 0.0) for d, h, m, _ in inst]
        d_vbs = [_dot_tn(tms[i], d_vns[i]) for i in range(8)]
        d_kgs = [_dot_tn(tms[i], d_ws[i]) for i in range(8)]
        d_a0s = [d_ls[i] * c.decay for i, (_, _, _, c) in enumerate(inst)]
        d_q0s = [d_attns[i] * c.decay for i, (_, _, _, c) in enumerate(inst)]
        es = [(d_ls[i] * c.a0 + d_attns[i] * c.q0) * c.decay for i, (_, _, _, c) in enumerate(inst)]
        kb_mm = [_dot(d_a0s[i], c.k) for i, (_, _, _, c) in enumerate(inst)]
        q_mm = [_dot(d_q0s[i], c.k) for i, (_, _, _, c) in enumerate(inst)]
        k_mm = [_dot_tn(d_a0s[i], c.kb) + _dot_tn(d_q0s[i], c.q) for i, (_, _, _, c) in enumerate(inst)]
        e_cols = [_dot_exact(ones, es[i], _TN, False)[:, 0:1] for i in range(8)]
        d_gcs, d_betas = [], []
        for i, (d, h, m, c) in enumerate(inst):
            sl = _head(h)
            d_kd, d_qd = sg[d][1][:, sl], sg[d][2][:, sl]
            d_cd = sg[d][4][0, h:h + 1, 0:1]
            d_vb, d_kg = d_vbs[i], d_kgs[i]
            d_kb = kb_mm[i] + d_kg * c.eg
            parts = (q_mm[i] + d_qd * c.eg, k_mm[i] + d_kd * c.ek + d_kb * c.beta, d_vb * c.beta)
            acc[h] = [p if a is None else a + p for a, p in zip(acc[h], parts)]
            s_kd = jnp.sum(d_kd * c.kd, axis=1, keepdims=True)
            d_gc = (jnp.sum(d_kg * c.kg, axis=1, keepdims=True) + jnp.sum(d_qd * c.qd, axis=1, keepdims=True) - s_kd
                    + jnp.sum(es[i], axis=1, keepdims=True) - e_cols[i])
            d_gl = jnp.sum(s_kd, axis=0, keepdims=True) + d_cd * c.cd
            d_gcs.append(d_gc + jnp.where(rowi == m.last, d_gl, 0.0))
            d_betas.append(jnp.sum(d_kb * c.k, axis=1, keepdims=True) + jnp.sum(d_vb * c.v, axis=1, keepdims=True))
        d_gs = [_dot_exact(m.tri, d_gcs[i] * ones, _TN, True)[:, 0:1] for i, (_, _, m, _) in enumerate(inst)]
        for i, (d, h, _, _) in enumerate(inst):
            dbg = dbg + jnp.where(lane == 4 * d + h, d_betas[i], 0.0) + jnp.where(lane == 8 + 4 * d + h, d_gs[i], 0.0)
        for h in range(NH):
            dq_ref[:, _head(h)], dk_ref[:, _head(h)], dv_ref[:, _head(h)] = acc[h]
        dbg_ref[...] = dbg

    tok = _rows(CHUNK, HW)
    bgs = _rows(CHUNK, BAP)
    mat = pl.BlockSpec((1, 2, NH, CHUNK, CHUNK), lambda i: (i, 0, 0, 0, 0))
    dcd = pl.BlockSpec((1, NH, DH), lambda i: (i, 0, 0))
    args = [q, k, v, bg, tmat, do, vns[0], vns[1]]
    in_specs = [tok, tok, tok, bgs, mat, tok, tok, tok]
    for d in range(2):
        args += list(seq_grads[d])
        in_specs += [tok] * 4 + [dcd]
    return _pallas(body, comm, name=name, grid=(n,), in_specs=in_specs, out_specs=[tok, tok, tok, bgs],
                   out_shape=[_sds((s, HW))] * 3 + [_sds((s, BAP))], scratch_shapes=[], args=args)


def _prep_bwd(c_qkv, p_ba, alog_row, dtb_row, dq, dk, dv, dbg, name):
    s = c_qkv.shape[0]
    t = min(256, s)

    def body(cq_ref, pc_ref, alog_ref, dtb_ref, dq_ref, dk_ref, dv_ref, dbg_ref,
             dcq_ref, dpc_ref, dalog_ref, ddtb_ref):
        @pl.when(pl.program_id(0) == 0)
        def _():
            dalog_ref[...] = jnp.zeros_like(dalog_ref)
            ddtb_ref[...] = jnp.zeros_like(ddtb_ref)

        cq = cq_ref[...]
        sq = cq * _sig(cq)
        sg = _silu_grad(cq)
        for h in range(NH):
            sl = slice(DH * h, DH * (h + 1))
            for off, d_ref, scale in ((0, dq_ref, DH ** -0.5), (RGW, dk_ref, 1.0)):
                csl = slice(off + DH * h, off + DH * (h + 1))
                xh = sq[:, csl]
                nrm = lax.rsqrt(jnp.sum(xh * xh, axis=-1, keepdims=True) + EPS)
                y = xh * nrm
                dy = d_ref[:, sl] * scale
                dcq_ref[:, csl] = nrm * (dy - y * jnp.sum(dy * y, axis=-1, keepdims=True)) * sg[:, csl]
        dcq_ref[:, 2 * RGW:] = dv_ref[...] * sg[:, 2 * RGW:]
        pc = pc_ref[...]
        lane = lax.broadcasted_iota(jnp.int32, pc.shape, 1)
        dbg = dbg_ref[...]
        beta = _sig(pc)
        ea = jnp.exp(alog_ref[...])
        z = pc + dtb_ref[...]
        g = -ea * _softplus(z)
        is_g = jnp.logical_and(lane >= 8, lane < 16)
        d_alpha = jnp.where(is_g, dbg * (-ea) * _sig(z), 0.0)
        dpc_ref[...] = jnp.where(lane < 8, dbg * beta * (1.0 - beta), d_alpha)
        dalog_ref[...] += _colsum(jnp.where(is_g, dbg * g, 0.0))
        ddtb_ref[...] += _colsum(d_alpha)

    return pl.pallas_call(
        body, name=name, grid=(s // t,),
        in_specs=[_rows(t, QKVW), _rows(t, BAP), _full((1, BAP)), _full((1, BAP))] + [_rows(t, HW)] * 3 + [_rows(t, BAP)],
        out_specs=[_rows(t, QKVW), _rows(t, BAP), _full((1, BAP)), _full((1, BAP))],
        out_shape=[_sds((s, QKVW)), _sds((s, BAP)), _sds((1, BAP)), _sds((1, BAP))],
        compiler_params=_cparams(1),
    )(c_qkv, p_ba, alog_row, dtb_row, dq, dk, dv, dbg)


def _mix_out_values(hf, hb, gate, of, ob, z, gn):
    hr = hf + hb
    y_rg = hr * _gelu(gate)
    osum = of + ob
    parts = []
    for h in range(NH):
        sl = slice(DH * h, DH * (h + 1))
        oh = osum[:, sl]
        r, ohat = _rms(oh)
        zh = z[:, sl]
        parts.append((r, ohat, zh))
    y_gdn = jnp.concatenate([ohat * gn * (zh * _sig(zh)) for (r, ohat, zh) in parts], axis=1)
    return hr, y_rg, y_gdn, parts


def _outproj(x1, hf, hb, gate, of, ob, z, gn, wout, name):
    s = x1.shape[0]
    t = min(256, s)

    def body(x_ref, hf_ref, hb_ref, gate_ref, of_ref, ob_ref, z_ref, gn_ref, w_ref, xo_ref, y_ref):
        _, y_rg, y_gdn, _ = _mix_out_values(hf_ref[...], hb_ref[...], gate_ref[...], of_ref[...], ob_ref[...],
                                            z_ref[...], gn_ref[...])
        y = jnp.concatenate([y_rg, y_gdn], axis=1).astype(BF16)
        y_ref[...] = y
        xo_ref[...] = x_ref[...] + jnp.dot(y, w_ref[...], preferred_element_type=F32)

    return pl.pallas_call(
        body, name=name, grid=(s // t,),
        in_specs=[_rows(t, D)] + [_rows(t, RGW)] * 6 + [_full((1, DH)), _full((D, D))],
        out_specs=[_rows(t, D), _rows(t, D)], out_shape=[_sds((s, D)), _sds((s, D), BF16)],
        compiler_params=_cparams(1),
    )(x1, hf, hb, gate, of, ob, z, gn, wout)


def _outproj_bwd(dx2, hf, hb, gate, of, ob, z, gn, wout, name):
    s = dx2.shape[0]
    t = min(256, s)

    def body(d_ref, hf_ref, hb_ref, gate_ref, of_ref, ob_ref, z_ref, gn_ref, w_ref,
             dhr_ref, dgate_ref, dos_ref, dz_ref, dgn_ref, db_ref):
        @pl.when(pl.program_id(0) == 0)
        def _():
            dgn_ref[...] = jnp.zeros_like(dgn_ref)

        gate = gate_ref[...]
        gn_v = gn_ref[...]
        hr, _, _, parts = _mix_out_values(hf_ref[...], hb_ref[...], gate, of_ref[...], ob_ref[...], z_ref[...], gn_v)
        dbf = d_ref[...].astype(BF16)
        db_ref[...] = dbf
        dy = _dot_nt(dbf, w_ref[...])
        dyr = dy[:, :RGW]
        dhr_ref[...] = dyr * _gelu(gate)
        dgate_ref[...] = dyr * hr * _gelu_grad(gate)
        dgn = jnp.zeros((1, DH), F32)
        for h, (r, ohat, zh) in enumerate(parts):
            sl = slice(DH * h, DH * (h + 1))
            dyh = dy[:, RGW + DH * h:RGW + DH * (h + 1)]
            sz = zh * _sig(zh)
            dn = dyh * sz
            dz_ref[:, sl] = dyh * ohat * gn_v * _silu_grad(zh)
            dgn = dgn + _colsum(dn * ohat)
            dos_ref[:, sl] = _rms_bwd(dn, ohat, r, gn_v)
        dgn_ref[...] += dgn

    return pl.pallas_call(
        body, name=name, grid=(s // t,),
        in_specs=[_rows(t, D)] + [_rows(t, RGW)] * 6 + [_full((1, DH)), _full((D, D))],
        out_specs=[_rows(t, RGW)] * 4 + [_full((1, DH)), _rows(t, D)],
        out_shape=[_sds((s, RGW))] * 4 + [_sds((1, DH)), _sds((s, D), BF16)],
        compiler_params=_cparams(1),
    )(dx2, hf, hb, gate, of, ob, z, gn, wout)


def _loss_head(x3, target, gain, name):
    s = x3.shape[0]
    t = min(256, s)

    def body(x_ref, t_ref, g_ref, dx_ref, loss_ref, dg_ref):
        @pl.when(pl.program_id(0) == 0)
        def _():
            loss_ref[...] = jnp.zeros_like(loss_ref)
            dg_ref[...] = jnp.zeros_like(dg_ref)

        r, xh = _rms(x_ref[...])
        gv = g_ref[...]
        err = xh * gv - t_ref[...]
        per_tok = jnp.mean(err * err, axis=-1, keepdims=True)
        loss_ref[...] += 0.5 * jnp.sum(per_tok, axis=0, keepdims=True)
        dy = err * (1.0 / D)
        dg_ref[...] += _colsum(dy * xh)
        dx_ref[...] = _rms_bwd(dy, xh, r, gv)

    return pl.pallas_call(
        body, name=name, grid=(s // t,), in_specs=[_rows(t, D), _rows(t, D), _full((1, D))],
        out_specs=[_rows(t, D), _full((8, 128)), _full((1, D))],
        out_shape=[_sds((s, D)), _sds((8, 128)), _sds((1, D))], compiler_params=_cparams(1),
    )(x3, target, gain)


def _adamw_math(wv, gv, mv, vv):
    mn = ADAM_B1 * mv + (1.0 - ADAM_B1) * gv
    vn = ADAM_B2 * vv + (1.0 - ADAM_B2) * (gv * gv)
    m_hat = mn / (1.0 - ADAM_B1 ** ADAM_STEP)
    v_hat = vn / (1.0 - ADAM_B2 ** ADAM_STEP)
    return -ADAM_LR * (m_hat / (jnp.sqrt(v_hat) + ADAM_EPS) + ADAM_WD * wv), mn, vn


def _row_tile(r, c):
    tr = r
    while tr * c * 4 > (1 << 20) and tr % 16 == 0:
        tr //= 2
    return tr


def _adamw(w, g, m, v, name):
    r, c = w.shape
    tr = _row_tile(r, c)

    def body(w_ref, g_ref, m_ref, v_ref, d_ref, nm_ref, nv_ref):
        d_ref[...], nm_ref[...], nv_ref[...] = _adamw_math(w_ref[...], g_ref[...], m_ref[...], v_ref[...])

    return pl.pallas_call(
        body, name=name, grid=(r // tr,), in_specs=[_rows(tr, c)] * 4, out_specs=[_rows(tr, c)] * 3,
        out_shape=[_sds((r, c))] * 3, compiler_params=_cparams(1),
    )(w, g, m, v)


def _adamw_halves(w, own, recv, m, v, c_arr, name):
    r, c = w.shape
    h = r // 2
    tr = _row_tile(h, c)
    nh = h // tr

    def body(c_ref, w_ref, own_ref, recv_ref, m_ref, v_ref, g_ref, d_ref, nm_ref, nv_ref):
        first_half = pl.program_id(0) < nh
        use_own = first_half == (c_ref[0] == 0)
        gv = jnp.where(use_own, own_ref[...], recv_ref[...])
        g_ref[...] = gv
        d_ref[...], nm_ref[...], nv_ref[...] = _adamw_math(w_ref[...], gv, m_ref[...], v_ref[...])

    full = pl.BlockSpec((tr, c), lambda i, c_ref: (i, 0))
    half = pl.BlockSpec((tr, c), lambda i, c_ref: (i % nh, 0))
    return pl.pallas_call(
        body, name=name, out_shape=[_sds((r, c))] * 4,
        grid_spec=pltpu.PrefetchScalarGridSpec(
            num_scalar_prefetch=1, grid=(2 * nh,), in_specs=[full, half, half, full, full], out_specs=[full] * 4),
        compiler_params=_cparams(1),
    )(c_arr, w, own, recv, m, v)


def _mesh_pos():
    return lax.axis_index("x"), lax.axis_index("y"), lax.axis_index("c")


def _other_chips(x, y):
    return [(1 - x, y), (x, 1 - y), (1 - x, 1 - y)]


class _Comm:
    def __init__(self, inputs, out_shapes, scratch, start, finish, space=pltpu.HBM):
        self.inputs, self.out_shapes, self.scratch = list(inputs), list(out_shapes), list(scratch)
        self.start, self.finish, self.space = start, finish, space


def _comm_call(comm, name):
    ni, no = len(comm.inputs), len(comm.out_shapes)

    def body(*refs):
        comm.start(refs[:ni], refs[ni:ni + no], refs[ni + no:])
        comm.finish(refs[:ni], refs[ni:ni + no], refs[ni + no:])

    spec = pl.BlockSpec(memory_space=comm.space)
    return list(pl.pallas_call(body, name=name, out_shape=comm.out_shapes, in_specs=[spec] * ni, out_specs=[spec] * no,
                               scratch_shapes=comm.scratch)(*comm.inputs))


def _pallas(body, comm, *, name, grid, in_specs, out_specs, out_shape, scratch_shapes, args):
    params = _cparams(len(grid))
    if comm is None:
        outs = pl.pallas_call(body, name=name, grid=grid, in_specs=in_specs, out_specs=out_specs, out_shape=out_shape,
                              scratch_shapes=scratch_shapes, compiler_params=params)(*args)
        return list(outs), []
    n_in, n_out, n_sc = len(in_specs), len(out_specs), len(scratch_shapes)
    ci, co = len(comm.inputs), len(comm.out_shapes)

    def carried(*refs):
        bounds = [0, n_in, n_in + ci, n_in + ci + n_out, n_in + ci + n_out + co, n_in + ci + n_out + co + n_sc, len(refs)]
        ins, cins, outs, couts, scr, csems = [refs[lo:hi] for lo, hi in zip(bounds[:-1], bounds[1:])]
        ids = [pl.program_id(k) for k in range(len(grid))]
        first = functools.reduce(jnp.logical_and, [i == 0 for i in ids])
        last = functools.reduce(jnp.logical_and, [i == g - 1 for i, g in zip(ids, grid)])

        @pl.when(first)
        def _():
            comm.start(cins, couts, csems)

        body(*ins, *outs, *scr)

        @pl.when(last)
        def _():
            comm.finish(cins, couts, csems)

    hbm = pl.BlockSpec(memory_space=pltpu.HBM)
    outs = pl.pallas_call(
        carried, name=name, grid=grid, in_specs=list(in_specs) + [hbm] * ci, out_specs=list(out_specs) + [hbm] * co,
        out_shape=list(out_shape) + comm.out_shapes, scratch_shapes=list(scratch_shapes) + comm.scratch,
        compiler_params=params)(*args, *comm.inputs)
    return list(outs[:n_out]), list(outs[n_out:])


def _gather_comm(arrays, space, block_rows):
    n_arr = len(arrays)

    def plan(x_refs, out_refs, sems):
        send_sems, recv_sems, local_sems = sems
        x, y, c = _mesh_pos()
        me, sibling = (x, y, c), (x, y, 1 - c)
        chips = _other_chips(x, y)

        def slot(a, px, py, pc):
            return out_refs[a].at[4 * px + 2 * py + pc]

        def copy(a, k, block, to, src=None):
            return pltpu.make_async_remote_copy(
                src_ref=slot(a, *block) if src is None else src, dst_ref=slot(a, *block),
                send_sem=send_sems.at[7 * a + k], recv_sem=recv_sems.at[7 * a + k], device_id=to, device_id_type=MESH)

        srcs = [x_refs[a] if block_rows[a] is None else
                x_refs[a].at[pl.ds(pl.multiple_of(c * block_rows[a], 16), block_rows[a]), :] for a in range(n_arr)]
        local = [pltpu.make_async_copy(srcs[a], slot(a, *me), local_sems.at[a]) for a in range(n_arr)]
        first = []
        for a in range(n_arr):
            first += [copy(a, 1 + j, me, (*chip, c), src=srcs[a]) for j, chip in enumerate(chips)]
            first.append(copy(a, 0, me, sibling, src=srcs[a]))
        return me, sibling, chips, c, copy, local, first

    def start(x_refs, out_refs, sems):
        _, _, _, _, _, local, first = plan(x_refs, out_refs, sems)
        for cp in local + first:
            cp.start()

    def finish(x_refs, out_refs, sems):
        me, sibling, chips, c, copy, local, first = plan(x_refs, out_refs, sems)
        passed = []
        for j, chip in enumerate(chips):
            for a in range(n_arr):
                copy(a, 1 + j, (*chip, c), me).wait_recv()
                fwd = copy(a, 4 + j, (*chip, c), sibling)
                fwd.start()
                passed.append(fwd)
        for a in range(n_arr):
            copy(a, 0, sibling, me).wait_recv()
            for j, chip in enumerate(chips):
                copy(a, 4 + j, (*chip, 1 - c), me).wait_recv()
        for cp in first + passed:
            cp.wait_send()
        for cp in local:
            cp.wait()

    out_shapes = [_sds((8, w.shape[0] if r is None else r) + w.shape[1:], w.dtype) for w, r in zip(arrays, block_rows)]
    scratch = [pltpu.SemaphoreType.DMA((7 * n_arr,)), pltpu.SemaphoreType.DMA((7 * n_arr,)), pltpu.SemaphoreType.DMA((n_arr,))]
    return _Comm(arrays, out_shapes, scratch, start, finish, space)


def _weights_gather_comm(shards):
    return _gather_comm(shards, pltpu.HBM, [w.shape[0] // 2 for w in shards])


def _all_shards(gathered):
    return [o.reshape(NSH, 2 * o.shape[1], o.shape[2]) for o in gathered]


def _gather_small(block, name):
    return _comm_call(_gather_comm([block], pltpu.VMEM, [None]), name)[0]


def _sibling_exchange(gs, name):
    n = len(gs)
    halves = [g.shape[1] // 2 for g in gs]

    def body(*refs):
        g_refs, land_refs = refs[:n], refs[n:2 * n]
        send_sems, recv_sems = refs[2 * n:]
        x, y, c = _mesh_pos()
        copies = []
        for a in range(n):
            h = halves[a]
            for s in range(NSH):
                copies.append(pltpu.make_async_remote_copy(
                    src_ref=g_refs[a].at[s, pl.ds(pl.multiple_of((1 - c) * h, 8), h), :], dst_ref=land_refs[a].at[s],
                    send_sem=send_sems.at[NSH * a + s], recv_sem=recv_sems.at[NSH * a + s],
                    device_id=(x, y, 1 - c), device_id_type=MESH))
        for cp in copies:
            cp.start()
        for cp in copies:
            cp.wait()

    return pl.pallas_call(
        body, name=name, out_shape=[_sds((NSH, h, g.shape[2])) for h, g in zip(halves, gs)],
        in_specs=[pl.BlockSpec(memory_space=pltpu.HBM)] * n, out_specs=[pl.BlockSpec(memory_space=pltpu.HBM)] * n,
        scratch_shapes=[pltpu.SemaphoreType.DMA((NSH * n,)), pltpu.SemaphoreType.DMA((NSH * n,))],
    )(*gs)


def _chip_sum(g, land, c_arr, name):
    _, h, cols = land.shape

    def body(c_ref, g_ref, l_ref, o_ref):
        o_ref[...] = (g_ref[...] + l_ref[...]).astype(BF16)

    return pl.pallas_call(
        body, name=name, out_shape=_sds((NSH, h, cols), BF16),
        grid_spec=pltpu.PrefetchScalarGridSpec(
            num_scalar_prefetch=1, grid=(NSH,),
            in_specs=[pl.BlockSpec((1, h, cols), lambda s, c_ref: (s, c_ref[0], 0)),
                      pl.BlockSpec((1, h, cols), lambda s, c_ref: (s, 0, 0))],
            out_specs=pl.BlockSpec((1, h, cols), lambda s, c_ref: (s, 0, 0))),
        compiler_params=_cparams(1),
    )(c_arr, g, land)


def _scatter_comm(parts):
    n = len(parts)

    def plan(p_refs, land_refs, sems):
        send_sems, recv_sems, local_sems = sems
        x, y, c = _mesh_pos()
        my_chip = 2 * x + y
        local = [pltpu.make_async_copy(p_refs[a].at[my_chip], land_refs[a].at[my_chip], local_sems.at[a]) for a in range(n)]
        copies = []
        for a in range(n):
            for j, (px, py) in enumerate(_other_chips(x, y)):
                copies.append(pltpu.make_async_remote_copy(
                    src_ref=p_refs[a].at[2 * px + py], dst_ref=land_refs[a].at[my_chip],
                    send_sem=send_sems.at[3 * a + j], recv_sem=recv_sems.at[3 * a + j],
                    device_id=(px, py, c), device_id_type=MESH))
        return local, copies

    def start(p_refs, land_refs, sems):
        local, copies = plan(p_refs, land_refs, sems)
        for cp in local + copies:
            cp.start()

    def finish(p_refs, land_refs, sems):
        local, copies = plan(p_refs, land_refs, sems)
        for cp in copies:
            cp.wait()
        for cp in local:
            cp.wait()

    scratch = [pltpu.SemaphoreType.DMA((3 * n,)), pltpu.SemaphoreType.DMA((3 * n,)), pltpu.SemaphoreType.DMA((n,))]
    return _Comm(parts, [_sds(p.shape, BF16) for p in parts], scratch, start, finish)


def _sum_slots(land, name):
    k, r, c = land.shape
    tr = r // 2 if r % 32 == 0 else r

    def body(l_ref, o_ref):
        acc = l_ref[0].astype(F32)
        for i in range(1, k):
            acc = acc + l_ref[i].astype(F32)
        o_ref[...] = acc

    return pl.pallas_call(
        body, name=name, grid=(r // tr,), in_specs=[pl.BlockSpec((k, tr, c), lambda i: (0, i, 0))],
        out_specs=_rows(tr, c), out_shape=_sds((r, c)), compiler_params=_cparams(1),
    )(land)


def _sibling_swap(halves):
    n = len(halves)

    def body(*refs):
        h_refs, out_refs = refs[:n], refs[n:2 * n]
        send_sems, recv_sems = refs[2 * n:]
        x, y, c = _mesh_pos()
        copies = [pltpu.make_async_remote_copy(
            src_ref=h_refs[a], dst_ref=out_refs[a], send_sem=send_sems.at[a], recv_sem=recv_sems.at[a],
            device_id=(x, y, 1 - c), device_id_type=MESH) for a in range(n)]
        for cp in copies:
            cp.start()
        for cp in copies:
            cp.wait()

    return pl.pallas_call(
        body, name="grad_sibling_swap", out_shape=[_sds(h.shape) for h in halves],
        in_specs=[pl.BlockSpec(memory_space=pltpu.HBM)] * n, out_specs=[pl.BlockSpec(memory_space=pltpu.HBM)] * n,
        scratch_shapes=[pltpu.SemaphoreType.DMA((n,)), pltpu.SemaphoreType.DMA((n,))],
    )(*halves)


def _pad_rows(v, width):
    flat = v.reshape(-1)
    rows = -(-flat.shape[0] // width)
    rows = -(-rows // 8) * 8
    return jnp.pad(flat, (0, rows * width - flat.shape[0])).reshape(rows, width)


def _size(shape):
    n = 1
    for dim in shape:
        n *= dim
    return n


def _row_pack(arrs):
    pieces = []
    for a in arrs:
        rows = -(-a.size // D)
        pieces.append(jnp.pad(a.reshape(-1), (0, rows * D - a.size)).reshape(rows, D))
    total = sum(p.shape[0] for p in pieces)
    if total % 8:
        pieces.append(jnp.zeros((8 - total % 8, D), F32))
    return jnp.concatenate(pieces, axis=0)


def _row_unpack(packed, shapes):
    out, r0 = [], 0
    for shp in shapes:
        n = _size(shp)
        rows = -(-n // D)
        out.append(packed[r0:r0 + rows].reshape(-1)[:n].reshape(shp))
        r0 += rows
    return out


def _block_diag(w):
    eye = jnp.eye(8, dtype=w.dtype)
    return (w[:, :, None, :] * eye[:, None, :, None]).reshape(RGW, RGW)


def _diag_blocks(dense):
    r = dense.reshape(8, 64, 8, 64)
    return jnp.stack([r[n, :, n, :] for n in range(8)])


def _lane_row(v8):
    return jnp.zeros((1, BAP), F32).at[0, 8:16].set(v8.reshape(8))


def _reduce_parts(gs, names, c_arr, tag):
    lands = _sibling_exchange(gs, "grad_sibling_exchange_" + tag)
    return [_chip_sum(g, l, c_arr, "chip_sum_" + n) for g, l, n in zip(gs, lands, names)]


def _local_step(x, target, sw, ffn1_w, later_shards, c_arr):
    (g1, gmix, rg_cw8, rg_cb, wgates, gbias, lam_row, gdn_cw8, alog_row, dtb_row, gn, g2, gfin) = sw
    wg1, wu1, wd1 = ffn1_w

    (x1, a1, b1), gathered = _ffn_fwd(x, g1, wg1, wu1, wd1, "ffn1_fwd", comm=_weights_gather_comm(later_shards))
    win_sh, wout_sh, wg2, wu2, wd2 = _all_shards(gathered)
    w_in_full = jnp.transpose(win_sh, (1, 0, 2)).reshape(D, NSH * INSH)
    wout = wout_sh.reshape(D, D)
    w_in_groups = (w_in_full[:, 0:512], w_in_full[:, 512:1024], w_in_full[:, 1024:2560], w_in_full[:, 2560:3072],
                   jnp.pad(w_in_full[:, 3072:3088], ((0, 0), (0, BAP - BAW))))
    h2, p_rgx, p_gate, p_qkv, p_z, p_ba = _inproj(x1, gmix, w_in_groups, "in_proj")
    c_rg = _conv(p_rgx, rg_cw8, rg_cb, "rg_conv")
    c_qkv = _conv(p_qkv, gdn_cw8, jnp.zeros((1, QKVW), F32), "gdn_conv")
    a0, bb0, a1s, bb1, q, k, v, bg = _mix_prep(c_rg, c_qkv, p_ba, wgates, gbias, lam_row, alog_row, dtb_row, "mix_prep")
    hf, hb = _scan_pair(a0, bb0, a1s, bb1, False, "rg_scan")
    tmat, gu, gw, gqd, gkd, gat, gcd = _gdn_local_fwd(q, k, v, bg, "gdn_local_fwd")
    of, s0, vn0, ob, s1, vn1 = _gdn_seq_fwd(gu, gw, gqd, gkd, gat, gcd, "gdn_seq_fwd")
    x2, ymix = _outproj(x1, hf, hb, p_gate, of, ob, p_z, gn, wout, "out_proj")
    (x3, a2, b2), _ = _ffn_fwd(x2, g2, wg2, wu2, wd2, "ffn2_fwd")
    dx3, loss_blk, d_gfin = _loss_head(x3, target, gfin, "loss_head")

    (dx2, d_g2, hb2, dob2, fb2, dab2, dbb2), _ = _ffn_bwd(x2, dx3, g2, a2, b2, wg2, wu2, wd2, "ffn2_bwd")
    d_ffn2 = [_tn(dab2, hb2, "ffn2_dwg"), _tn(dbb2, hb2, "ffn2_dwu"), _tn(fb2, dob2, "ffn2_dwd")]
    parts_ffn2 = _reduce_parts(d_ffn2, _BIG_NAMES[5:8], c_arr, "ffn2")

    d_hr, d_gate, d_os, d_z, d_gn, dx2b = _outproj_bwd(dx2, hf, hb, p_gate, of, ob, p_z, gn, wout, "out_proj_bwd")
    d_wout = _tn(ymix, dx2b, "dw_out")[0]

    lam1, lam0 = _scan_pair(a1s, d_hr, a0, d_hr, True, "rg_scan_bwd")
    d_xc, d_pre, xcb, d_gbias, d_lam = _gates_bwd(c_rg, wgates, gbias, lam_row, lam0, lam1, hf, hb, "rg_gates_bwd")
    d_wgates = _tn(xcb, d_pre, "dw_gates")[0]
    d_prgx, d_rgcw8, d_rgcb = _conv_bwd(p_rgx, d_xc, rg_cw8, "rg_conv_bwd")

    sg = _gdn_seq_bwd(d_os, gw, gqd, gkd, gat, gcd, (s0, s1), (vn0, vn1), "gdn_seq_bwd")
    (dq, dk, dv, dbg), lands_ffn2 = _gdn_local_bwd(q, k, v, bg, tmat, d_os, (vn0, vn1), (sg[0:5], sg[5:10]), "gdn_local_bwd",
                                                  comm=_scatter_comm(parts_ffn2))
    d_cqkv, d_pba, d_alog, d_dtb = _prep_bwd(c_qkv, p_ba, alog_row, dtb_row, dq, dk, dv, dbg, "gdn_prep_bwd")
    d_pqkv, d_gdncw8, _ = _conv_bwd(p_qkv, d_cqkv, gdn_cw8, "gdn_conv_bwd")

    dps = (d_prgx, d_gate, d_pqkv, d_z, d_pba)
    dx1, d_gmix = _inproj_bwd(x1, dx2, gmix, dps, w_in_groups, "in_proj_bwd")
    d_win_groups = [_tn(h2, dp, "dw_in_%d" % i)[0] for i, dp in enumerate(dps)]
    d_win = jnp.concatenate(d_win_groups[:4] + [d_win_groups[4][:, :BAW]], axis=1)
    d_mix = [jnp.transpose(d_win.reshape(D, NSH, INSH), (1, 0, 2)), d_wout.reshape(NSH, OUTSH, D)]
    parts_mix = _reduce_parts(d_mix, _BIG_NAMES[3:5], c_arr, "mix")

    (gx, d_g1, hb1, dob1, fb1, dab1, dbb1), _ = _ffn_bwd(x, dx1, g1, a1, b1, wg1, wu1, wd1, "ffn1_bwd")
    d_wg1, lands_mix = _tn(dab1, hb1, "ffn1_dwg", comm=_scatter_comm(parts_mix))
    parts_wg1 = _reduce_parts([d_wg1], _BIG_NAMES[0:1], c_arr, "ffn1_gate")
    d_wu1, lands_wg1 = _tn(dbb1, hb1, "ffn1_dwu", comm=_scatter_comm(parts_wg1))
    parts_wu1 = _reduce_parts([d_wu1], _BIG_NAMES[1:2], c_arr, "ffn1_up")
    d_wd1, lands_wu1 = _tn(fb1, dob1, "ffn1_dwd", comm=_scatter_comm(parts_wu1))
    parts_wd1 = _reduce_parts([d_wd1], _BIG_NAMES[2:3], c_arr, "ffn1_down")
    lands_ffn1 = lands_wg1 + lands_wu1 + _comm_call(_scatter_comm(parts_wd1), "grad_chip_scatter_ffn1_down")

    halves = [_sum_slots(l, "sum_chips_" + n) for l, n in zip(lands_ffn1 + lands_mix + lands_ffn2, _BIG_NAMES)]
    small = dict(
        ffn1_norm=d_g1, mix_norm=d_gmix, rg_conv_w=d_rgcw8[:4], rg_conv_b=d_rgcb,
        rg_gate_a_w=jnp.stack([_diag_blocks(d_wgates[:, RGW * i:RGW * (i + 1)]) for i in (0, 1)]),
        rg_gate_x_w=jnp.stack([_diag_blocks(d_wgates[:, RGW * i:RGW * (i + 1)]) for i in (2, 3)]),
        rg_gate_a_b=d_gbias[0, :2 * RGW].reshape(2, RGW), rg_gate_x_b=d_gbias[0, 2 * RGW:].reshape(2, RGW),
        rg_lambda=d_lam.reshape(2, RGW), gdn_conv_w=d_gdncw8[:4],
        gdn_a_log=d_alog[0, 8:16].reshape(2, NH), gdn_dt_bias=d_dtb[0, 8:16].reshape(2, NH),
        gdn_norm=d_gn, ffn2_norm=d_g2, final_norm=d_gfin)
    return loss_blk, gx, halves, small


_SMALL_NAMES = ("ffn1_norm", "mix_norm", "rg_conv_w", "rg_conv_b", "rg_gate_a_w", "rg_gate_a_b", "rg_gate_x_w",
                "rg_gate_x_b", "rg_lambda", "gdn_conv_w", "gdn_a_log", "gdn_dt_bias", "gdn_norm", "ffn2_norm", "final_norm")
_SMALL_SHARDED = dict(rg_conv_w=128, rg_gate_a_b=128, rg_gate_x_b=128, rg_lambda=128, gdn_conv_w=384)
_OUT_ORDER = ("ffn1_norm", "ffn1_w_gate", "ffn1_w_up", "ffn1_w_down", "mix_norm", "w_in", "w_out", "rg_conv_w", "rg_conv_b",
              "rg_gate_a_w", "rg_gate_a_b", "rg_gate_x_w", "rg_gate_x_b", "rg_lambda", "gdn_conv_w", "gdn_a_log",
              "gdn_dt_bias", "gdn_norm", "ffn2_norm", "ffn2_w_gate", "ffn2_w_up", "ffn2_w_down", "final_norm")
_BIG_NAMES = ("ffn1_w_gate", "ffn1_w_up", "ffn1_w_down", "w_in", "w_out", "ffn2_w_gate", "ffn2_w_up", "ffn2_w_down")
_TRANSPOSED = ("ffn1_w_gate", "ffn1_w_up", "ffn2_w_gate", "ffn2_w_up")


def kernel(x, ffn1_norm, ffn1_w_gate, ffn1_w_up, ffn1_w_down, mix_norm, w_in, w_out, rg_conv_w, rg_conv_b, rg_gate_a_w, rg_gate_a_b, rg_gate_x_w, rg_gate_x_b, rg_lambda, gdn_conv_w, gdn_a_log, gdn_dt_bias, gdn_norm, ffn2_norm, ffn2_w_gate, ffn2_w_up, ffn2_w_down, final_norm, loss_target, m_ffn1_norm, m_ffn1_w_gate, m_ffn1_w_up, m_ffn1_w_down, m_mix_norm, m_w_in, m_w_out, m_rg_conv_w, m_rg_conv_b, m_rg_gate_a_w, m_rg_gate_a_b, m_rg_gate_x_w, m_rg_gate_x_b, m_rg_lambda, m_gdn_conv_w, m_gdn_a_log, m_gdn_dt_bias, m_gdn_norm, m_ffn2_norm, m_ffn2_w_gate, m_ffn2_w_up, m_ffn2_w_down, m_final_norm, v_ffn1_norm, v_ffn1_w_gate, v_ffn1_w_up, v_ffn1_w_down, v_mix_norm, v_w_in, v_w_out, v_rg_conv_w, v_rg_conv_b, v_rg_gate_a_w, v_rg_gate_a_b, v_rg_gate_x_w, v_rg_gate_x_b, v_rg_lambda, v_gdn_conv_w, v_gdn_a_log, v_gdn_dt_bias, v_gdn_norm, v_ffn2_norm, v_ffn2_w_gate, v_ffn2_w_up, v_ffn2_w_down, v_final_norm):
    args = dict(locals())
    w = {n: args[n] for n in _OUT_ORDER}
    mom = {n: args["m_" + n] for n in _OUT_ORDER}
    var = {n: args["v_" + n] for n in _OUT_ORDER}
    xi, yi, ci = _mesh_pos()
    shard = 2 * xi + yi

    big_bf16 = [w[n][0].astype(BF16) for n in _BIG_NAMES]
    ffn1_w = _all_shards(_comm_call(_weights_gather_comm(big_bf16[0:3]), "gather_ffn1_weights"))
    sm_local = _pad_rows(jnp.concatenate([w[n][0].reshape(-1) for n in _SMALL_SHARDED]), 128)
    sm_all = _gather_small(sm_local, "gather_small_weights")[0::2].reshape(NSH, -1)
    sm_full, off = {}, 0
    for n, wd_ in _SMALL_SHARDED.items():
        rows = w[n].shape[1]
        piece = sm_all[:, off:off + rows * wd_].reshape(NSH, rows, wd_)
        sm_full[n] = jnp.transpose(piece, (1, 0, 2)).reshape(rows, NSH * wd_)
        off += rows * wd_

    wa, wx = rg_gate_a_w[0], rg_gate_x_w[0]
    wgates = jnp.concatenate([_block_diag(wa[0]), _block_diag(wa[1]), _block_diag(wx[0]), _block_diag(wx[1])],
                             axis=1).astype(BF16)
    gbias = jnp.concatenate([sm_full["rg_gate_a_b"].reshape(1, -1), sm_full["rg_gate_x_b"].reshape(1, -1)], axis=1)
    sw = (ffn1_norm, mix_norm, jnp.pad(sm_full["rg_conv_w"], ((0, 4), (0, 0))), rg_conv_b, wgates, gbias,
          sm_full["rg_lambda"].reshape(1, -1), jnp.pad(sm_full["gdn_conv_w"], ((0, 4), (0, 0))), _lane_row(gdn_a_log),
          _lane_row(gdn_dt_bias), gdn_norm, ffn2_norm, final_norm.reshape(1, D))
    c_arr = ci.reshape(1).astype(jnp.int32)

    loss_blk, gx, halves, small = _local_step(x[0], loss_target[0], sw, ffn1_w, big_bf16[3:], c_arr)
    loss = lax.psum(loss_blk[0, 0], ("x", "y", "c"))
    grads = {}

    sm_grad = _row_pack([small[n] for n in _SMALL_NAMES])
    sm_sum = _sum_slots(_gather_small(sm_grad, "gather_small_grads"), "small_grad_sum")
    for n, g in zip(_SMALL_NAMES, _row_unpack(sm_sum, [small[n].shape for n in _SMALL_NAMES])):
        if n in _SMALL_SHARDED:
            wd_ = _SMALL_SHARDED[n]
            g = lax.dynamic_slice_in_dim(g, shard * wd_, wd_, axis=1)
        grads[n] = g.reshape(w[n].shape)

    delta, new_m, new_v = {}, {}, {}
    for n, own, recv in zip(_BIG_NAMES, halves, _sibling_swap(halves)):
        to2d = jnp.transpose if n in _TRANSPOSED else (lambda t: t)
        outs4 = _adamw_halves(to2d(w[n][0]), own, recv, to2d(mom[n][0]), to2d(var[n][0]), c_arr, "adamw_" + n)
        grads[n], delta[n], new_m[n], new_v[n] = [to2d(o)[None] for o in outs4]
    packs = [_row_pack([t[n] for n in _SMALL_NAMES]) for t in (w, grads, mom, var)]
    sm_shapes = [w[n].shape for n in _SMALL_NAMES]
    for dst, src in zip((delta, new_m, new_v), _adamw(*packs, "adamw_small")):
        for n, val in zip(_SMALL_NAMES, _row_unpack(src, sm_shapes)):
            dst[n] = val

    outs = [loss, gx[None]]
    for group in (grads, delta, new_m, new_v):
        outs += [group[n] for n in _OUT_ORDER]
    return tuple(outs)
```

```python
import functools

import jax
import jax.numpy as jnp
from jax import lax
from jax.experimental import pallas as pl
from jax.experimental.pallas import tpu as pltpu

F32 = jnp.float32
BF16 = jnp.bfloat16
EPS = 1e-6
D = 1024
NSH = 4
FSH = 704
RGW = 512
QKVW = 1536
ZW = 512
BAW = 16
BAP = 128
INSH = 772
OUTSH = 256
CHUNK = 64
NH = 4
DH = 128
RG_C = 8.0
VMEM_LIMIT = 52 * 1024 * 1024
MESH = pl.DeviceIdType.MESH

ADAM_LR = 0.001
ADAM_B1 = 0.9
ADAM_B2 = 0.999
ADAM_EPS = 1e-08
ADAM_WD = 0.01
ADAM_STEP = 10


def _cparams(n_grid):
    return pltpu.CompilerParams(dimension_semantics=("arbitrary",) * n_grid, vmem_limit_bytes=VMEM_LIMIT)


def _sig(x):
    return 0.5 + 0.5 * jnp.tanh(0.5 * x)


def _sig_pos(x):
    return 1.0 / (1.0 + jnp.exp(-x))


def _softplus(x):
    return jnp.maximum(x, 0.0) + jnp.log(1.0 + jnp.exp(-jnp.abs(x)))


def _neg_expm1(y):
    series = -y * (1.0 + y * (0.5 + y * (1.0 / 6 + y * (1.0 / 24 + y * (1.0 / 120 + y * (1.0 / 720 + y / 5040))))))
    return jnp.where(y > -0.3, series, 1.0 - jnp.exp(y))


_GELU_C = 0.7978845608028654


def _gelu(x):
    t = jnp.tanh(_GELU_C * (x + 0.044715 * x * x * x))
    return 0.5 * x * (1.0 + t)


def _gelu_grad(x):
    t = jnp.tanh(_GELU_C * (x + 0.044715 * x * x * x))
    return 0.5 * (1.0 + t) + 0.5 * x * (1.0 - t * t) * _GELU_C * (1.0 + 3 * 0.044715 * x * x)


def _silu_grad(x):
    s = _sig(x)
    return s * (1.0 + x * (1.0 - s))


def _dot(a, b):
    return jnp.dot(a.astype(BF16), b.astype(BF16), preferred_element_type=F32)


def _dot_nt(a, b):
    return lax.dot_general(a.astype(BF16), b.astype(BF16), (((1,), (1,)), ((), ())), preferred_element_type=F32)


def _dot_tn(a, b):
    return lax.dot_general(a.astype(BF16), b.astype(BF16), (((0,), (0,)), ((), ())), preferred_element_type=F32)


_NN = ((1,), (0,))
_NT = ((1,), (1,))
_TN = ((0,), (0,))


def _dg(a, b, dims):
    return lax.dot_general(a, b, (dims, ((), ())), preferred_element_type=F32)


def _split2(a):
    hi = a.astype(BF16)
    return hi, (a - hi.astype(F32)).astype(BF16)


def _dot3(a, b, dims=_NN):
    ah, al = _split2(a)
    bh, bl = _split2(b)
    return _dg(ah, bh, dims) + _dg(ah, bl, dims) + _dg(al, bh, dims)


def _dot_exact(e, x, dims, e_is_lhs):
    x0 = x.astype(BF16)
    r = x - x0.astype(F32)
    x1 = r.astype(BF16)
    x2 = (r - x1.astype(F32)).astype(BF16)
    eb = e.astype(BF16)
    if e_is_lhs:
        return _dg(eb, x0, dims) + _dg(eb, x1, dims) + _dg(eb, x2, dims)
    return _dg(x0, eb, dims) + _dg(x1, eb, dims) + _dg(x2, eb, dims)


def _rms(xv):
    r = lax.rsqrt(jnp.mean(xv * xv, axis=-1, keepdims=True) + EPS)
    return r, xv * r


def _rms_bwd(dy, xh, r, gain):
    dxh = dy * gain
    return r * (dxh - xh * jnp.mean(dxh * xh, axis=-1, keepdims=True))


def _colsum(v):
    return jnp.sum(v, axis=0, keepdims=True)


def _rows(t, c):
    return pl.BlockSpec((t, c), lambda i: (i, 0))


def _full(shape):
    n = len(shape)
    return pl.BlockSpec(shape, lambda i: (0,) * n)


def _sds(shape, dtype=F32):
    return jax.ShapeDtypeStruct(shape, dtype)


def _ffn_fwd(x, gain, wg, wu, wd, name, comm=None):
    s = x.shape[0]
    tm = min(512, s)

    def body(x_ref, g_ref, wg_ref, wu_ref, wd_ref, xo_ref, a_ref, b_ref, f_ref, h_sc, acc):
        j = pl.program_id(1)

        @pl.when(j == 0)
        def _():
            _, xh = _rms(x_ref[...])
            h_sc[...] = (xh * g_ref[...]).astype(BF16)
            acc[...] = jnp.zeros_like(acc)

        h = h_sc[...]

        a = jnp.dot(h, wg_ref[0], preferred_element_type=F32)
        b = jnp.dot(h, wu_ref[0], preferred_element_type=F32)
        a_ref[0] = a.astype(BF16)
        b_ref[0] = b.astype(BF16)
        f = (a * _sig(a) * b).astype(BF16)
        f_ref[0] = f
        acc[...] += jnp.dot(f, wd_ref[0], preferred_element_type=F32)

        @pl.when(j == NSH - 1)
        def _():
            xo_ref[...] = x_ref[...] + 0.5 * acc[...]

    return _pallas(
        body, comm, name=name, grid=(s // tm, NSH),
        in_specs=[pl.BlockSpec((tm, D), lambda i, j: (i, 0)), pl.BlockSpec((1, D), lambda i, j: (0, 0)),
                  pl.BlockSpec((1, D, FSH), lambda i, j: (j, 0, 0)), pl.BlockSpec((1, D, FSH), lambda i, j: (j, 0, 0)),
                  pl.BlockSpec((1, FSH, D), lambda i, j: (j, 0, 0))],
        out_specs=[pl.BlockSpec((tm, D), lambda i, j: (i, 0))] + [pl.BlockSpec((1, tm, FSH), lambda i, j: (j, i, 0))] * 3,
        out_shape=[_sds((s, D))] + [_sds((NSH, s, FSH), BF16)] * 3,
        scratch_shapes=[pltpu.VMEM((tm, D), BF16), pltpu.VMEM((tm, D), F32)],
        args=(x, gain, wg, wu, wd))


def _ffn_bwd(x, dout, gain, a, b, wg, wu, wd, name):
    s = x.shape[0]
    tm = min(512, s)

    def hidden(d_ref, a_ref, b_ref, wd_ref, do_ref, da_ref, db_ref, do_sc):
        @pl.when(pl.program_id(1) == 0)
        def _():
            do = (0.5 * d_ref[...]).astype(BF16)
            do_sc[...] = do
            do_ref[...] = do

        df = _dot_nt(do_sc[...], wd_ref[0])
        av = a_ref[0].astype(F32)
        bv = b_ref[0].astype(F32)
        sa = _sig(av)
        da_ref[0] = (df * bv * sa * (1.0 + av * (1.0 - sa))).astype(BF16)
        db_ref[0] = (df * av * sa).astype(BF16)

    tok = pl.BlockSpec((tm, D), lambda i, j: (i, 0))
    sh = pl.BlockSpec((1, tm, FSH), lambda i, j: (j, i, 0))
    do, da, db = pl.pallas_call(
        hidden, name=name + "_hidden", grid=(s // tm, NSH),
        in_specs=[tok, sh, sh, pl.BlockSpec((1, FSH, D), lambda i, j: (j, 0, 0))], out_specs=[tok, sh, sh],
        out_shape=[_sds((s, D), BF16)] + [_sds((NSH, s, FSH), BF16)] * 2,
        scratch_shapes=[pltpu.VMEM((tm, D), BF16)], compiler_params=_cparams(2),
    )(dout, a, b, wd)

    def inputs(x_ref, d_ref, g_ref, da_ref, db_ref, wg_ref, wu_ref, dx_ref, dg_ref, h_ref):
        @pl.when(pl.program_id(0) == 0)
        def _():
            dg_ref[...] = jnp.zeros_like(dg_ref)

        dh = jnp.zeros((tm, D), F32)
        for j in range(NSH):
            dh = dh + _dot_nt(da_ref[j], wg_ref[j]) + _dot_nt(db_ref[j], wu_ref[j])
        r, xh = _rms(x_ref[...])
        gv = g_ref[...]
        h_ref[...] = (xh * gv).astype(BF16)
        dg_ref[...] += _colsum(dh * xh)
        dx_ref[...] = d_ref[...] + _rms_bwd(dh, xh, r, gv)

    grads = pl.BlockSpec((NSH, tm, FSH), lambda i: (0, i, 0))
    resident = pl.BlockSpec((NSH, D, FSH), lambda i: (0, 0, 0), pipeline_mode=pl.Buffered(1))
    dx, dg, h = pl.pallas_call(
        inputs, name=name + "_input", grid=(s // tm,),
        in_specs=[_rows(tm, D), _rows(tm, D), _full((1, D)), grads, grads, resident, resident],
        out_specs=[_rows(tm, D), _full((1, D)), _rows(tm, D)],
        out_shape=[_sds((s, D)), _sds((1, D)), _sds((s, D), BF16)], compiler_params=_cparams(1),
    )(x, dout, gain, da, db, wg, wu)
    return dx, dg, h, do, da, db


def _tn(a, b, name, comm=None):
    a_g = a.ndim == 3
    b_g = b.ndim == 3
    g = a.shape[0] if a_g else (b.shape[0] if b_g else 1)
    s, k = a.shape[-2:]
    n = b.shape[-1]
    ts = min(1024, s)

    def body(a_ref, b_ref, o_ref):
        @pl.when(pl.program_id(1) == 0)
        def _():
            o_ref[...] = jnp.zeros_like(o_ref)

        av = a_ref[0] if a_g else a_ref[...]
        bv = b_ref[0] if b_g else b_ref[...]
        o_ref[0] += _dot_tn(av, bv)

    a_spec = pl.BlockSpec((1, ts, k), lambda gi, si: (gi, si, 0)) if a_g else pl.BlockSpec((ts, k), lambda gi, si: (si, 0))
    b_spec = pl.BlockSpec((1, ts, n), lambda gi, si: (gi, si, 0)) if b_g else pl.BlockSpec((ts, n), lambda gi, si: (si, 0))
    outs, carried = _pallas(body, comm, name=name, grid=(g, s // ts), in_specs=[a_spec, b_spec],
                            out_specs=[pl.BlockSpec((1, k, n), lambda gi, si: (gi, 0, 0))], out_shape=[_sds((g, k, n))],
                            scratch_shapes=[], args=(a, b))
    return outs[0] if comm is None else (outs[0], carried)


_P_WIDTHS = (RGW, RGW, QKVW, ZW, BAP)


def _inproj(x1, gain, ws, name):
    s = x1.shape[0]
    tm = min(256, s)

    def body(x_ref, g_ref, *refs):
        w_refs = refs[:5]
        h_ref = refs[5]
        p_refs = refs[6:]
        _, xh = _rms(x_ref[...])
        h = (xh * g_ref[...]).astype(BF16)
        h_ref[...] = h
        for w_ref, p_ref in zip(w_refs, p_refs):
            p_ref[...] = jnp.dot(h, w_ref[...], preferred_element_type=F32)

    return pl.pallas_call(
        body, name=name, grid=(s // tm,),
        in_specs=[_rows(tm, D), _full((1, D))] + [_full((D, w)) for w in _P_WIDTHS],
        out_specs=[_rows(tm, D)] + [_rows(tm, w) for w in _P_WIDTHS],
        out_shape=[_sds((s, D), BF16)] + [_sds((s, w)) for w in _P_WIDTHS],
        compiler_params=_cparams(1),
    )(x1, gain, *ws)


def _inproj_bwd(x1, dx2, gain, dps, ws, name):
    s = x1.shape[0]
    tm = min(256, s)

    def body(x_ref, d_ref, g_ref, *refs):
        dp_refs = refs[:5]
        w_refs = refs[5:10]
        dx_ref, dg_ref = refs[10:]

        @pl.when(pl.program_id(0) == 0)
        def _():
            dg_ref[...] = jnp.zeros_like(dg_ref)

        dh = jnp.zeros((tm, D), F32)
        for dp_ref, w_ref in zip(dp_refs, w_refs):
            dh = dh + _dot_nt(dp_ref[...], w_ref[...])
        r, xh = _rms(x_ref[...])
        dg_ref[...] += _colsum(dh * xh)
        dx_ref[...] = d_ref[...] + _rms_bwd(dh, xh, r, g_ref[...])

    return pl.pallas_call(
        body, name=name, grid=(s // tm,),
        in_specs=[_rows(tm, D), _rows(tm, D), _full((1, D))] + [_rows(tm, w) for w in _P_WIDTHS]
        + [_full((D, w)) for w in _P_WIDTHS],
        out_specs=[_rows(tm, D), _full((1, D))],
        out_shape=[_sds((s, D)), _sds((1, D))],
        compiler_params=_cparams(1),
    )(x1, dx2, gain, *dps, *ws)


def _halo_specs(s, t, c):
    nb8 = s // 8
    tb = t // 8
    prev = pl.BlockSpec((8, c), lambda i: (jnp.maximum(i * tb - 1, 0), 0))
    nxt = pl.BlockSpec((8, c), lambda i: (jnp.minimum((i + 1) * tb, nb8 - 1), 0))
    return prev, nxt


def _edge_masks(nb):
    i = pl.program_id(0)
    return jnp.where(i > 0, 1.0, 0.0).astype(F32), jnp.where(i < nb - 1, 1.0, 0.0).astype(F32)


def _shifted(xx, off, t):
    n = t + 16
    sh = (-off) % n
    rolled = xx if sh == 0 else pltpu.roll(xx, sh, 0)
    return rolled[8:8 + t]


def _conv(x, w8, bias, name):
    s, c = x.shape
    t = min(256, s)
    nb = s // t

    def body(x_ref, xp_ref, xn_ref, w_ref, b_ref, o_ref):
        pm, nm = _edge_masks(nb)
        for c0 in range(0, c, 512):
            cols = slice(c0, c0 + 512)
            xx = jnp.concatenate([xp_ref[:, cols] * pm, x_ref[:, cols], xn_ref[:, cols] * nm], axis=0)
            acc = jnp.zeros((t, 512), F32) + b_ref[:, cols]
            for j in range(4):
                acc = acc + w_ref[j:j + 1, cols] * _shifted(xx, j - 2, t)
            o_ref[:, cols] = acc

    prev, nxt = _halo_specs(s, t, c)
    return pl.pallas_call(
        body, name=name, grid=(nb,),
        in_specs=[_rows(t, c), prev, nxt, _full((8, c)), _full((1, c))],
        out_specs=_rows(t, c), out_shape=_sds((s, c)), compiler_params=_cparams(1),
    )(x, x, x, w8, bias)


def _conv_bwd(x, dc, w8, name):
    s, c = x.shape
    t = min(256, s)
    nb = s // t

    def body(x_ref, d_ref, dp_ref, dn_ref, w_ref, dx_ref, dw_ref, db_ref):
        @pl.when(pl.program_id(0) == 0)
        def _():
            dw_ref[...] = jnp.zeros_like(dw_ref)
            db_ref[...] = jnp.zeros_like(db_ref)

        pm, nm = _edge_masks(nb)
        for c0 in range(0, c, 512):
            cols = slice(c0, c0 + 512)
            dd = jnp.concatenate([dp_ref[:, cols] * pm, d_ref[:, cols], dn_ref[:, cols] * nm], axis=0)
            xv = x_ref[:, cols]
            acc = jnp.zeros((t, 512), F32)
            for j in range(4):
                dsh = _shifted(dd, 2 - j, t)
                acc = acc + w_ref[j:j + 1, cols] * dsh
                dw_ref[j:j + 1, cols] += _colsum(dsh * xv)
            dx_ref[:, cols] = acc
            db_ref[:, cols] += _colsum(d_ref[:, cols])

    prev, nxt = _halo_specs(s, t, c)
    return pl.pallas_call(
        body, name=name, grid=(nb,),
        in_specs=[_rows(t, c), _rows(t, c), prev, nxt, _full((8, c))],
        out_specs=[_rows(t, c), _full((8, c)), _full((1, c))],
        out_shape=[_sds((s, c)), _sds((8, c)), _sds((1, c))], compiler_params=_cparams(1),
    )(x, dc, dc, dc, w8)


def _rg_gates(xc, pre, lam_row):
    sp8 = RG_C * _softplus(-lam_row)
    out = []
    for d in range(2):
        r = _sig_pos(pre[:, RGW * d:RGW * (d + 1)])
        gi = _sig(pre[:, 2 * RGW + RGW * d:2 * RGW + RGW * (d + 1)])
        la = -r * sp8[:, RGW * d:RGW * (d + 1)]
        a = jnp.exp(la)
        mult = jnp.sqrt(_neg_expm1(2.0 * la))
        out.append((r, gi, a, mult))
    return out


def _mix_prep(c_rg, c_qkv, p_ba, wgates, gbias, lam_row, alog_row, dtb_row, name):
    s = c_rg.shape[0]
    t = min(256, s)

    def body(xc_ref, cq_ref, pc_ref, wg_ref, gb_ref, lam_ref, alog_ref, dtb_ref,
             a0_ref, b0_ref, a1_ref, b1_ref, q_ref, k_ref, v_ref, bg_ref):
        xc = xc_ref[...]
        pre = _dot(xc, wg_ref[...]) + gb_ref[...]
        gates = _rg_gates(xc, pre, lam_ref[...])
        for (r, gi, a, mult), a_ref, b_ref in zip(gates, (a0_ref, a1_ref), (b0_ref, b1_ref)):
            a_ref[...] = a
            b_ref[...] = mult * gi * xc
        cq = cq_ref[...]
        sq = cq * _sig(cq)
        for h in range(NH):
            sl = slice(DH * h, DH * (h + 1))
            qh = sq[:, sl]
            q_ref[:, sl] = qh * lax.rsqrt(jnp.sum(qh * qh, axis=-1, keepdims=True) + EPS) * (DH ** -0.5)
            kh = sq[:, RGW + DH * h:RGW + DH * (h + 1)]
            k_ref[:, sl] = kh * lax.rsqrt(jnp.sum(kh * kh, axis=-1, keepdims=True) + EPS)
        v_ref[...] = sq[:, 2 * RGW:]
        pc = pc_ref[...]
        lane = lax.broadcasted_iota(jnp.int32, pc.shape, 1)
        beta = _sig(pc)
        g = -jnp.exp(alog_ref[...]) * _softplus(pc + dtb_ref[...])
        bg_ref[...] = jnp.where(lane < 8, beta, jnp.where(lane < 16, g, 0.0))

    return pl.pallas_call(
        body, name=name, grid=(s // t,),
        in_specs=[_rows(t, RGW), _rows(t, QKVW), _rows(t, BAP), _full((RGW, 4 * RGW)), _full((1, 4 * RGW)),
                  _full((1, 2 * RGW)), _full((1, BAP)), _full((1, BAP))],
        out_specs=[_rows(t, RGW)] * 7 + [_rows(t, BAP)],
        out_shape=[_sds((s, RGW))] * 7 + [_sds((s, BAP))],
        compiler_params=_cparams(1),
    )(c_rg, c_qkv, p_ba, wgates, gbias, lam_row, alog_row, dtb_row)


def _scan_pair(af, bf, ar, br, shifted, name):
    s, c = af.shape
    t = min(512, s)
    nb = s // t
    ng = t // 8
    tb = t // 8
    up = lambda i: (i, 0)
    down = lambda i: (nb - 1 - i, 0)

    def body(*refs):
        if shifted:
            af_ref, bf_ref, ar_ref, br_ref, afp_ref, arn_ref, hf_ref, hr_ref, carry, fbuf, rbuf = refs
        else:
            af_ref, bf_ref, ar_ref, br_ref, hf_ref, hr_ref, carry = refs
        i = pl.program_id(0)

        @pl.when(i == 0)
        def _():
            carry[...] = jnp.zeros_like(carry)

        if shifted:
            edge = jnp.where(i > 0, 1.0, 0.0).astype(F32)
            fbuf[0:8, :] = afp_ref[...] * edge
            fbuf[8:t + 8, :] = af_ref[...]
            rbuf[0:t, :] = ar_ref[...]
            rbuf[t:t + 8, :] = arn_ref[...] * edge
        row = lax.broadcasted_iota(jnp.int32, (8, c), 0)

        def block_scan(av, bv, downwards):
            for k in (1, 2, 4):
                sh = (8 - k) if downwards else k
                m = (row < 8 - k) if downwards else (row >= k)
                a_s = pltpu.roll(av, sh, 0)
                b_s = pltpu.roll(bv, sh, 0)
                bv = jnp.where(m, av * b_s + bv, bv)
                av = jnp.where(m, av * a_s, av)
            return av, bv

        def group(gi, cvs):
            cf, cr = cvs
            rf = pl.multiple_of(gi * 8, 8)
            rr = pl.multiple_of((ng - 1 - gi) * 8, 8)
            if shifted:
                a_f = jnp.where(row > 0, pltpu.roll(fbuf[pl.ds(rf + 8, 8), :], 1, 0), pltpu.roll(fbuf[pl.ds(rf, 8), :], 1, 0))
                a_r = jnp.where(row < 7, pltpu.roll(rbuf[pl.ds(rr, 8), :], 7, 0), pltpu.roll(rbuf[pl.ds(rr + 8, 8), :], 7, 0))
            else:
                a_f = af_ref[pl.ds(rf, 8), :]
                a_r = ar_ref[pl.ds(rr, 8), :]
            a_f, b_f = block_scan(a_f, bf_ref[pl.ds(rf, 8), :], False)
            a_r, b_r = block_scan(a_r, br_ref[pl.ds(rr, 8), :], True)
            h_f = a_f * cf + b_f
            h_r = a_r * cr + b_r
            hf_ref[pl.ds(rf, 8), :] = h_f
            hr_ref[pl.ds(rr, 8), :] = h_r
            return h_f[7:8, :], h_r[0:1, :]

        cf, cr = lax.fori_loop(0, ng, group, (carry[0:1, :], carry[8:9, :]))
        carry[0:1, :] = cf
        carry[8:9, :] = cr

    in_specs = [pl.BlockSpec((t, c), up), pl.BlockSpec((t, c), up), pl.BlockSpec((t, c), down), pl.BlockSpec((t, c), down)]
    args = [af, bf, ar, br]
    scratch = [pltpu.VMEM((16, c), F32)]
    if shifted:
        in_specs += [pl.BlockSpec((8, c), lambda i: (jnp.maximum(i * tb - 1, 0), 0)),
                     pl.BlockSpec((8, c), lambda i: (jnp.minimum((nb - i) * tb, s // 8 - 1), 0))]
        args += [af, ar]
        scratch += [pltpu.VMEM((t + 8, c), F32), pltpu.VMEM((t + 8, c), F32)]
    return pl.pallas_call(
        body, name=name, grid=(nb,), in_specs=in_specs,
        out_specs=[pl.BlockSpec((t, c), up), pl.BlockSpec((t, c), down)], out_shape=[_sds((s, c)), _sds((s, c))],
        scratch_shapes=scratch, compiler_params=_cparams(1),
    )(*args)


def _gates_bwd(xc, wgates, gbias, lam_row, lam0, lam1, hf, hb, name):
    s = xc.shape[0]
    t = min(256, s)
    nb = s // t

    def body(xc_ref, wg_ref, gb_ref, lam_ref, l0_ref, l1_ref, hf_ref, hfp_ref, hfn_ref, hb_ref, hbp_ref, hbn_ref,
             dxc_ref, dpre_ref, xcb_ref, dgb_ref, dlam_ref):
        @pl.when(pl.program_id(0) == 0)
        def _():
            dgb_ref[...] = jnp.zeros_like(dgb_ref)
            dlam_ref[...] = jnp.zeros_like(dlam_ref)

        pm, nm = _edge_masks(nb)
        h_prev = _shifted(jnp.concatenate([hfp_ref[...] * pm, hf_ref[...], hfn_ref[...] * nm], axis=0), -1, t)
        h_next = _shifted(jnp.concatenate([hbp_ref[...] * pm, hb_ref[...], hbn_ref[...] * nm], axis=0), 1, t)
        h_shift = (h_prev, h_next)
        xv = xc_ref[...]
        pre = _dot(xv, wg_ref[...]) + gb_ref[...]
        lam_row_v = lam_ref[...]
        sp8 = RG_C * _softplus(-lam_row_v)
        dsp_dlam = -RG_C * _sig(-lam_row_v)
        gates = _rg_gates(xv, pre, lam_row_v)
        dxc = jnp.zeros((t, RGW), F32)
        dpre_r = []
        dpre_i = []
        for d, ((r, gi, a, mult), l_ref, hs) in enumerate(zip(gates, (l0_ref, l1_ref), h_shift)):
            dbb = l_ref[...]
            da = dbb * hs
            cs = slice(RGW * d, RGW * (d + 1))
            dmult = dbb * gi * xv
            dgi = dbb * mult * xv
            dxc = dxc + dbb * mult * gi
            dla = da * a - dmult * a * a / mult
            dr = -dla * sp8[:, cs]
            dlam_ref[:, cs] += _colsum(-dla * r) * dsp_dlam[:, cs]
            dpre_r.append(dr * r * (1.0 - r))
            dpre_i.append(dgi * gi * (1.0 - gi))
        dpre = jnp.concatenate(dpre_r + dpre_i, axis=1)
        dgb_ref[...] += _colsum(dpre)
        dpre_b = dpre.astype(BF16)
        dpre_ref[...] = dpre_b
        xcb_ref[...] = xv.astype(BF16)
        dxc_ref[...] = dxc + _dot_nt(dpre_b, wg_ref[...])

    prev, nxt = _halo_specs(s, t, RGW)
    return pl.pallas_call(
        body, name=name, grid=(s // t,),
        in_specs=[_rows(t, RGW), _full((RGW, 4 * RGW)), _full((1, 4 * RGW)), _full((1, 2 * RGW))] + [_rows(t, RGW)] * 2
        + [_rows(t, RGW), prev, nxt] * 2,
        out_specs=[_rows(t, RGW), _rows(t, 4 * RGW), _rows(t, RGW), _full((1, 4 * RGW)), _full((1, 2 * RGW))],
        out_shape=[_sds((s, RGW)), _sds((s, 4 * RGW), BF16), _sds((s, RGW), BF16), _sds((1, 4 * RGW)), _sds((1, 2 * RGW))],
        compiler_params=_cparams(1),
    )(xc, wgates, gbias, lam_row, lam0, lam1, hf, hf, hf, hb, hb, hb)


class _GdnMasks:
    def __init__(self, d):
        ri = lax.broadcasted_iota(jnp.int32, (CHUNK, CHUNK), 0)
        ci = lax.broadcasted_iota(jnp.int32, (CHUNK, CHUNK), 1)
        self.incl = (ri >= ci) if d == 0 else (ri <= ci)
        self.strict = (ri > ci) if d == 0 else (ri < ci)
        b16 = jnp.right_shift(ri, 4) == jnp.right_shift(ci, 4)
        b32 = jnp.right_shift(ri, 5) == jnp.right_shift(ci, 5)
        self.diag16 = b16
        self.off32 = jnp.logical_and(b32, jnp.logical_not(b16))
        self.off64 = jnp.logical_not(b32)
        self.eye = jnp.where(ri == ci, 1.0, 0.0).astype(F32)
        self.tri = jnp.where(self.incl, 1.0, 0.0).astype(F32)
        self.last = CHUNK - 1 if d == 0 else 0


def _tri_inv(lmat, m):
    return _tri_inv_many([lmat], [m])[0]


def _tri_inv_many(lmats, masks):
    n = len(lmats)
    ns = [jnp.where(masks[i].diag16, lmats[i], 0.0) for i in range(n)]
    ps = [masks[i].eye - ns[i] for i in range(n)]
    qs = [_dot3(ns[i], ns[i]) for i in range(n)]
    for step in range(3):
        ps = [_dot3(ps[i], masks[i].eye + qs[i]) for i in range(n)]
        if step < 2:
            qs = [_dot3(qs[i], qs[i]) for i in range(n)]
    for off in ("off32", "off64"):
        ts = [_dot3(ps[i], jnp.where(getattr(masks[i], off), lmats[i], 0.0)) for i in range(n)]
        ps = [ps[i] - _dot3(ts[i], ps[i]) for i in range(n)]
    return ps


def _chunk_cumsums(m, bgv):
    return _dot_exact(m.tri, bgv, _NN, True), _dot_exact(m.tri, bgv, ((0,), (1,)), False)


class _GdnHead:
    def __init__(self, qh, kh, vh, kk, q0, bg, gcs, gcs_t, d, h, m):
        cb = 4 * d + h
        cg = 8 + 4 * d + h
        self.q, self.k, self.v = qh, kh, vh
        self.beta = bg[:, cb:cb + 1]
        gcol = gcs[:, cg:cg + 1]
        grow = gcs_t[cg:cg + 1, :]
        gl = gcs[m.last:m.last + 1, cg:cg + 1]
        self.decay = jnp.exp(jnp.where(m.incl, gcol - grow, -1e30))
        self.kb = kh * self.beta
        self.vb = vh * self.beta
        self.a0 = kk * self.beta
        self.q0 = q0
        self.lmat = jnp.where(m.strict, self.a0 * self.decay, 0.0)
        self.attn = self.q0 * self.decay
        self.eg = jnp.exp(gcol)
        self.ek = jnp.exp(gl - gcol)
        self.cd = jnp.exp(gl)
        self.kg = self.kb * self.eg
        self.qd = qh * self.eg
        self.kd = kh * self.ek


HW = NH * DH
SEQ_CB = 4


def _head(h):
    return slice(DH * h, DH * (h + 1))


def _gdn_local_fwd(q, k, v, bg, name):
    s = q.shape[0]
    n = s // CHUNK

    def body(q_ref, k_ref, v_ref, bg_ref, t_ref, u_ref, w_ref, qd_ref, kd_ref, at_ref, cd_ref):
        bgv = bg_ref[...]
        qs = [q_ref[:, _head(h)] for h in range(NH)]
        ks = [k_ref[:, _head(h)] for h in range(NH)]
        kk = [_dot_nt(ks[h], ks[h]) for h in range(NH)]
        q0 = [_dot_nt(qs[h], ks[h]) for h in range(NH)]
        inst = []
        for d in range(2):
            m = _GdnMasks(d)
            gcs, gcs_t = _chunk_cumsums(m, bgv)
            for h in range(NH):
                c = _GdnHead(qs[h], ks[h], v_ref[:, _head(h)], kk[h], q0[h], bgv, gcs, gcs_t, d, h, m)
                inst.append((d, h, m, c))
        tms = _tri_inv_many([c.lmat for _, _, _, c in inst], [m for _, _, m, _ in inst])
        for (d, h, m, c), tm in zip(inst, tms):
            sl = _head(h)
            t_ref[0, d, h] = tm
            u_ref[d, :, sl] = _dot(tm, c.vb)
            w_ref[d, :, sl] = _dot(tm, c.kg).astype(BF16)
            qd_ref[d, :, sl] = c.qd.astype(BF16)
            kd_ref[d, :, sl] = c.kd.astype(BF16)
            at_ref[0, d, h] = c.attn.astype(BF16)
            cd_ref[0, 4 * d + h:4 * d + h + 1, :] = jnp.broadcast_to(c.cd, (1, DH))

    tok = _rows(CHUNK, HW)
    tok2 = pl.BlockSpec((2, CHUNK, HW), lambda i: (0, i, 0))
    mat = pl.BlockSpec((1, 2, NH, CHUNK, CHUNK), lambda i: (i, 0, 0, 0, 0))
    return pl.pallas_call(
        body, name=name, grid=(n,), in_specs=[tok, tok, tok, _rows(CHUNK, BAP)],
        out_specs=[mat, tok2, tok2, tok2, tok2, mat, pl.BlockSpec((1, 8, DH), lambda i: (i, 0, 0))],
        out_shape=[_sds((n, 2, NH, CHUNK, CHUNK)), _sds((2, s, HW)), _sds((2, s, HW), BF16), _sds((2, s, HW), BF16),
                   _sds((2, s, HW), BF16), _sds((n, 2, NH, CHUNK, CHUNK), BF16), _sds((n, 8, DH))],
        compiler_params=_cparams(1),
    )(q, k, v, bg)


def _seq_specs(s, order):
    n = s // CHUNK
    cb = min(SEQ_CB, n)
    nb = n // cb
    tb = cb * CHUNK

    def blk(d):
        return (lambda i: i) if order[d] else (lambda i: nb - 1 - i)

    def per_dir(make):
        return [make(d, blk(d)) for d in range(2)]

    tok2 = per_dir(lambda d, f: pl.BlockSpec((1, tb, HW), lambda i: (d, f(i), 0)))
    tok = per_dir(lambda d, f: pl.BlockSpec((tb, HW), lambda i: (f(i), 0)))
    mat = per_dir(lambda d, f: pl.BlockSpec((cb, 1, NH, CHUNK, CHUNK), lambda i: (f(i), d, 0, 0, 0)))
    cds = per_dir(lambda d, f: pl.BlockSpec((cb, 8, DH), lambda i: (f(i), 0, 0)))
    sts = per_dir(lambda d, f: pl.BlockSpec((cb, NH, DH, DH), lambda i: (f(i), 0, 0, 0)))
    dcd = per_dir(lambda d, f: pl.BlockSpec((cb, NH, DH), lambda i: (f(i), 0, 0)))
    return n, cb, nb, tok2, tok, mat, cds, sts, dcd


def _gdn_seq_fwd(u, w, qd, kd, at, cd, name):
    s = u.shape[1]
    n, cb, nb, tok2, tok, mat, cds, sts, _ = _seq_specs(s, (True, False))

    def body(*refs):
        ins = (refs[0:6], refs[6:12])
        outs = (refs[12:15], refs[15:18])
        st = refs[18]

        @pl.when(pl.program_id(0) == 0)
        def _():
            st[...] = jnp.zeros_like(st)

        for j in range(cb):
            items = []
            for d in range(2):
                jj = j if d == 0 else cb - 1 - j
                items += [(d, h, jj, slice(CHUNK * jj, CHUNK * (jj + 1)), _head(h)) for h in range(NH)]
            shs = [st[d, h] for d, h, _, _, _ in items]
            wss = [_dot(ins[d][1][0, rows, sl], sh) for (d, h, jj, rows, sl), sh in zip(items, shs)]
            vns = [ins[d][0][0, rows, sl] - ws for (d, h, jj, rows, sl), ws in zip(items, wss)]
            news = [sh * ins[d][5][jj, 4 * d + h:4 * d + h + 1, :] + _dot_tn(ins[d][3][0, rows, sl], vn)
                    for (d, h, jj, rows, sl), sh, vn in zip(items, shs, vns)]
            for (d, h, jj, rows, sl), sh, vn, new in zip(items, shs, vns, news):
                o_r, s_r, vn_r = outs[d]
                st[d, h] = new
                s_r[jj, h] = sh
                vn_r[rows, sl] = vn
                o_r[rows, sl] = _dot(ins[d][2][0, rows, sl], sh) + _dot(ins[d][4][jj, 0, h], vn)

    in_specs, out_specs, out_shape = [], [], []
    for d in range(2):
        in_specs += [tok2[d]] * 4 + [mat[d], cds[d]]
        out_specs += [tok[d], sts[d], tok[d]]
        out_shape += [_sds((s, HW)), _sds((n, NH, DH, DH)), _sds((s, HW))]
    return pl.pallas_call(
        body, name=name, grid=(nb,), in_specs=in_specs, out_specs=out_specs, out_shape=out_shape,
        scratch_shapes=[pltpu.VMEM((2, NH, DH, DH), F32)], compiler_params=_cparams(1),
    )(u, w, qd, kd, at, cd, u, w, qd, kd, at, cd)


def _gdn_seq_bwd(do, w, qd, kd, at, cd, states, vns, name):
    s = do.shape[0]
    n, cb, nb, tok2, tok, mat, cds, sts, dcd = _seq_specs(s, (False, True))

    def body(*refs):
        ins = (refs[0:8], refs[8:16])
        outs = (refs[16:21], refs[21:26])
        dst = refs[26]

        @pl.when(pl.program_id(0) == 0)
        def _():
            dst[...] = jnp.zeros_like(dst)

        for j in range(cb):
            items = []
            for d in range(2):
                jj = cb - 1 - j if d == 0 else j
                items += [(d, h, jj, slice(CHUNK * jj, CHUNK * (jj + 1)), _head(h)) for h in range(NH)]
            dsns = [dst[d, h] for d, h, _, _, _ in items]
            dohs = [ins[d][0][rows, sl] for d, h, jj, rows, sl in items]
            d_vns = [_dot_tn(ins[d][4][jj, 0, h], doh) + _dot(ins[d][3][0, rows, sl], dsn)
                     for (d, h, jj, rows, sl), doh, dsn in zip(items, dohs, dsns)]
            news = [ins[d][5][jj, 4 * d + h:4 * d + h + 1, :] * dsn + _dot_tn(ins[d][2][0, rows, sl], doh)
                    - _dot_tn(ins[d][1][0, rows, sl], d_vn)
                    for (d, h, jj, rows, sl), doh, dsn, d_vn in zip(items, dohs, dsns, d_vns)]
            for (d, h, jj, rows, sl), doh, dsn, d_vn, new in zip(items, dohs, dsns, d_vns, news):
                dvn_r, dkd_r, dqd_r, dw_r, dcd_r = outs[d]
                sh = ins[d][6][jj, h]
                dst[d, h] = new
                dvn_r[rows, sl] = d_vn
                dkd_r[rows, sl] = _dot_nt(ins[d][7][rows, sl], dsn)
                dqd_r[rows, sl] = _dot_nt(doh, sh)
                dw_r[rows, sl] = -_dot_nt(d_vn, sh)
                d_cd = jnp.sum(jnp.sum(sh * dsn, axis=1, keepdims=True), axis=0, keepdims=True)
                dcd_r[jj, h:h + 1, :] = jnp.broadcast_to(d_cd, (1, DH))

    in_specs, out_specs, out_shape, args = [], [], [], []
    for d in range(2):
        in_specs += [tok[d]] + [tok2[d]] * 3 + [mat[d], cds[d], sts[d], tok[d]]
        args += [do, w, qd, kd, at, cd, states[d], vns[d]]
        out_specs += [tok[d]] * 4 + [dcd[d]]
        out_shape += [_sds((s, HW))] * 4 + [_sds((n, NH, DH))]
    return pl.pallas_call(
        body, name=name, grid=(nb,), in_specs=in_specs, out_specs=out_specs, out_shape=out_shape,
        scratch_shapes=[pltpu.VMEM((2, NH, DH, DH), F32)], compiler_params=_cparams(1),
    )(*args)


def _gdn_local_bwd(q, k, v, bg, tmat, do, vns, seq_grads, name, comm=None):
    s = q.shape[0]
    n = s // CHUNK

    def body(*refs):
        q_ref, k_ref, v_ref, bg_ref, t_ref, do_ref = refs[0:6]
        vn_refs = refs[6:8]
        sg = (refs[8:13], refs[13:18])
        dq_ref, dk_ref, dv_ref, dbg_ref = refs[18:]
        bgv = bg_ref[...]
        qs = [q_ref[:, _head(h)] for h in range(NH)]
        ks = [k_ref[:, _head(h)] for h in range(NH)]
        kk = [_dot_nt(ks[h], ks[h]) for h in range(NH)]
        q0 = [_dot_nt(qs[h], ks[h]) for h in range(NH)]
        lane = lax.broadcasted_iota(jnp.int32, (CHUNK, BAP), 1)
        rowi = lax.broadcasted_iota(jnp.int32, (CHUNK, 1), 0)
        ones = jnp.ones((CHUNK, DH), F32)
        dbg = jnp.zeros((CHUNK, BAP), F32)
        acc = [[None, None, None] for _ in range(NH)]
        inst = []
        for d in range(2):
            m = _GdnMasks(d)
            gcs, gcs_t = _chunk_cumsums(m, bgv)
            for h in range(NH):
                c = _GdnHead(qs[h], ks[h], v_ref[:, _head(h)], kk[h], q0[h], bgv, gcs, gcs_t, d, h, m)
                inst.append((d, h, m, c))
        tms = [t_ref[0, d, h] for d, h, _, _ in inst]
        d_vns = [sg[d][0][:, _head(h)] for d, h, _, _ in inst]
        d_ws = [sg[d][3][:, _head(h)] for d, h, _, _ in inst]
        d_ts = [_dot_nt(d_vns[i], c.vb) + _dot_nt(d_ws[i], c.kg) for i, (_, _, _, c) in enumerate(inst)]
        xs = [_dot3(tms[i], d_ts[i], _TN) for i in range(8)]
        d_ls = [jnp.where(inst[i][2].strict, -_dot3(xs[i], tms[i], _NT), 0.0) for i in range(8)]
        d_attns = [jnp.where(m.incl, _dot_nt(do_ref[:, _head(h)], vn_refs[d][:, _head(h)]), 0.0) for d, h, m, _ in inst]
        d_vbs = [_dot_tn(tms[i], d_vns[i]) for i in range(8)]
        d_kgs = [_dot_tn(tms[i], d_ws[i]) for i in range(8)]
        d_a0s = [d_ls[i] * c.decay for i, (_, _, _, c) in enumerate(inst)]
        d_q0s = [d_attns[i] * c.decay for i, (_, _, _, c) in enumerate(inst)]
        es = [(d_ls[i] * c.a0 + d_attns[i] * c.q0) * c.decay for i, (_, _, _, c) in enumerate(inst)]
        kb_mm = [_dot(d_a0s[i], c.k) for i, (_, _, _, c) in enumerate(inst)]
        q_mm = [_dot(d_q0s[i], c.k) for i, (_, _, _, c) in enumerate(inst)]
        k_mm = [_dot_tn(d_a0s[i], c.kb) + _dot_tn(d_q0s[i], c.q) for i, (_, _, _, c) in enumerate(inst)]
        e_cols = [_dot_exact(ones, es[i], _TN, False)[:, 0:1] for i in range(8)]
        d_gcs, d_betas = [], []
        for i, (d, h, m, c) in enumerate(inst):
            sl = _head(h)
            d_kd, d_qd = sg[d][1][:, sl], sg[d][2][:, sl]
            d_cd = sg[d][4][0, h:h + 1, 0:1]
            d_vb, d_kg = d_vbs[i], d_kgs[i]
            d_kb = kb_mm[i] + d_kg * c.eg
            parts = (q_mm[i] + d_qd * c.eg, k_mm[i] + d_kd * c.ek + d_kb * c.beta, d_vb * c.beta)
            acc[h] = [p if a is None else a + p for a, p in zip(acc[h], parts)]
            s_kd = jnp.sum(d_kd * c.kd, axis=1, keepdims=True)
            d_gc = (jnp.sum(d_kg * c.kg, axis=1, keepdims=True) + jnp.sum(d_qd * c.qd, axis=1, keepdims=True) - s_kd
                    + jnp.sum(es[i], axis=1, keepdims=True) - e_cols[i])
            d_gl = jnp.sum(s_kd, axis=0, keepdims=True) + d_cd * c.cd
            d_gcs.append(d_gc + jnp.where(rowi == m.last, d_gl, 0.0))
            d_betas.append(jnp.sum(d_kb * c.k, axis=1, keepdims=True) + jnp.sum(d_vb * c.v, axis=1, keepdims=True))
        d_gs = [_dot_exact(m.tri, d_gcs[i] * ones, _TN, True)[:, 0:1] for i, (_, _, m, _) in enumerate(inst)]
        for i, (d, h, _, _) in enumerate(inst):
            dbg = dbg + jnp.where(lane == 4 * d + h, d_betas[i], 0.0) + jnp.where(lane == 8 + 4 * d + h, d_gs[i], 0.0)
        for h in range(NH):
            dq_ref[:, _head(h)], dk_ref[:, _head(h)], dv_ref[:, _head(h)] = acc[h]
        dbg_ref[...] = dbg

    tok = _rows(CHUNK, HW)
    bgs = _rows(CHUNK, BAP)
    mat = pl.BlockSpec((1, 2, NH, CHUNK, CHUNK), lambda i: (i, 0, 0, 0, 0))
    dcd = pl.BlockSpec((1, NH, DH), lambda i: (i, 0, 0))
    args = [q, k, v, bg, tmat, do, vns[0], vns[1]]
    in_specs = [tok, tok, tok, bgs, mat, tok, tok, tok]
    for d in range(2):
        args += list(seq_grads[d])
        in_specs += [tok] * 4 + [dcd]
    return _pallas(body, comm, name=name, grid=(n,), in_specs=in_specs, out_specs=[tok, tok, tok, bgs],
                   out_shape=[_sds((s, HW))] * 3 + [_sds((s, BAP))], scratch_shapes=[], args=args)


def _prep_bwd(c_qkv, p_ba, alog_row, dtb_row, dq, dk, dv, dbg, name):
    s = c_qkv.shape[0]
    t = min(256, s)

    def body(cq_ref, pc_ref, alog_ref, dtb_ref, dq_ref, dk_ref, dv_ref, dbg_ref,
             dcq_ref, dpc_ref, dalog_ref, ddtb_ref):
        @pl.when(pl.program_id(0) == 0)
        def _():
            dalog_ref[...] = jnp.zeros_like(dalog_ref)
            ddtb_ref[...] = jnp.zeros_like(ddtb_ref)

        cq = cq_ref[...]
        sq = cq * _sig(cq)
        sg = _silu_grad(cq)
        for h in range(NH):
            sl = slice(DH * h, DH * (h + 1))
            for off, d_ref, scale in ((0, dq_ref, DH ** -0.5), (RGW, dk_ref, 1.0)):
                csl = slice(off + DH * h, off + DH * (h + 1))
                xh = sq[:, csl]
                nrm = lax.rsqrt(jnp.sum(xh * xh, axis=-1, keepdims=True) + EPS)
                y = xh * nrm
                dy = d_ref[:, sl] * scale
                dcq_ref[:, csl] = nrm * (dy - y * jnp.sum(dy * y, axis=-1, keepdims=True)) * sg[:, csl]
        dcq_ref[:, 2 * RGW:] = dv_ref[...] * sg[:, 2 * RGW:]
        pc = pc_ref[...]
        lane = lax.broadcasted_iota(jnp.int32, pc.shape, 1)
        dbg = dbg_ref[...]
        beta = _sig(pc)
        ea = jnp.exp(alog_ref[...])
        z = pc + dtb_ref[...]
        g = -ea * _softplus(z)
        is_g = jnp.logical_and(lane >= 8, lane < 16)
        d_alpha = jnp.where(is_g, dbg * (-ea) * _sig(z), 0.0)
        dpc_ref[...] = jnp.where(lane < 8, dbg * beta * (1.0 - beta), d_alpha)
        dalog_ref[...] += _colsum(jnp.where(is_g, dbg * g, 0.0))
        ddtb_ref[...] += _colsum(d_alpha)

    return pl.pallas_call(
        body, name=name, grid=(s // t,),
        in_specs=[_rows(t, QKVW), _rows(t, BAP), _full((1, BAP)), _full((1, BAP))] + [_rows(t, HW)] * 3 + [_rows(t, BAP)],
        out_specs=[_rows(t, QKVW), _rows(t, BAP), _full((1, BAP)), _full((1, BAP))],
        out_shape=[_sds((s, QKVW)), _sds((s, BAP)), _sds((1, BAP)), _sds((1, BAP))],
        compiler_params=_cparams(1),
    )(c_qkv, p_ba, alog_row, dtb_row, dq, dk, dv, dbg)


def _mix_out_values(hf, hb, gate, of, ob, z, gn):
    hr = hf + hb
    y_rg = hr * _gelu(gate)
    osum = of + ob
    parts = []
    for h in range(NH):
        sl = slice(DH * h, DH * (h + 1))
        oh = osum[:, sl]
        r, ohat = _rms(oh)
        zh = z[:, sl]
        parts.append((r, ohat, zh))
    y_gdn = jnp.concatenate([ohat * gn * (zh * _sig(zh)) for (r, ohat, zh) in parts], axis=1)
    return hr, y_rg, y_gdn, parts


def _outproj(x1, hf, hb, gate, of, ob, z, gn, wout, name):
    s = x1.shape[0]
    t = min(256, s)

    def body(x_ref, hf_ref, hb_ref, gate_ref, of_ref, ob_ref, z_ref, gn_ref, w_ref, xo_ref, y_ref):
        _, y_rg, y_gdn, _ = _mix_out_values(hf_ref[...], hb_ref[...], gate_ref[...], of_ref[...], ob_ref[...],
                                            z_ref[...], gn_ref[...])
        y = jnp.concatenate([y_rg, y_gdn], axis=1).astype(BF16)
        y_ref[...] = y
        xo_ref[...] = x_ref[...] + jnp.dot(y, w_ref[...], preferred_element_type=F32)

    return pl.pallas_call(
        body, name=name, grid=(s // t,),
        in_specs=[_rows(t, D)] + [_rows(t, RGW)] * 6 + [_full((1, DH)), _full((D, D))],
        out_specs=[_rows(t, D), _rows(t, D)], out_shape=[_sds((s, D)), _sds((s, D), BF16)],
        compiler_params=_cparams(1),
    )(x1, hf, hb, gate, of, ob, z, gn, wout)


def _outproj_bwd(dx2, hf, hb, gate, of, ob, z, gn, wout, name):
    s = dx2.shape[0]
    t = min(256, s)

    def body(d_ref, hf_ref, hb_ref, gate_ref, of_ref, ob_ref, z_ref, gn_ref, w_ref,
             dhr_ref, dgate_ref, dos_ref, dz_ref, dgn_ref, db_ref):
        @pl.when(pl.program_id(0) == 0)
        def _():
            dgn_ref[...] = jnp.zeros_like(dgn_ref)

        gate = gate_ref[...]
        gn_v = gn_ref[...]
        hr, _, _, parts = _mix_out_values(hf_ref[...], hb_ref[...], gate, of_ref[...], ob_ref[...], z_ref[...], gn_v)
        dbf = d_ref[...].astype(BF16)
        db_ref[...] = dbf
        dy = _dot_nt(dbf, w_ref[...])
        dyr = dy[:, :RGW]
        dhr_ref[...] = dyr * _gelu(gate)
        dgate_ref[...] = dyr * hr * _gelu_grad(gate)
        dgn = jnp.zeros((1, DH), F32)
        for h, (r, ohat, zh) in enumerate(parts):
            sl = slice(DH * h, DH * (h + 1))
            dyh = dy[:, RGW + DH * h:RGW + DH * (h + 1)]
            sz = zh * _sig(zh)
            dn = dyh * sz
            dz_ref[:, sl] = dyh * ohat * gn_v * _silu_grad(zh)
            dgn = dgn + _colsum(dn * ohat)
            dos_ref[:, sl] = _rms_bwd(dn, ohat, r, gn_v)
        dgn_ref[...] += dgn

    return pl.pallas_call(
        body, name=name, grid=(s // t,),
        in_specs=[_rows(t, D)] + [_rows(t, RGW)] * 6 + [_full((1, DH)), _full((D, D))],
        out_specs=[_rows(t, RGW)] * 4 + [_full((1, DH)), _rows(t, D)],
        out_shape=[_sds((s, RGW))] * 4 + [_sds((1, DH)), _sds((s, D), BF16)],
        compiler_params=_cparams(1),
    )(dx2, hf, hb, gate, of, ob, z, gn, wout)


def _loss_head(x3, target, gain, name):
    s = x3.shape[0]
    t = min(256, s)

    def body(x_ref, t_ref, g_ref, dx_ref, loss_ref, dg_ref):
        @pl.when(pl.program_id(0) == 0)
        def _():
            loss_ref[...] = jnp.zeros_like(loss_ref)
            dg_ref[...] = jnp.zeros_like(dg_ref)

        r, xh = _rms(x_ref[...])
        gv = g_ref[...]
        err = xh * gv - t_ref[...]
        per_tok = jnp.mean(err * err, axis=-1, keepdims=True)
        loss_ref[...] += 0.5 * jnp.sum(per_tok, axis=0, keepdims=True)
        dy = err * (1.0 / D)
        dg_ref[...] += _colsum(dy * xh)
        dx_ref[...] = _rms_bwd(dy, xh, r, gv)

    return pl.pallas_call(
        body, name=name, grid=(s // t,), in_specs=[_rows(t, D), _rows(t, D), _full((1, D))],
        out_specs=[_rows(t, D), _full((8, 128)), _full((1, D))],
        out_shape=[_sds((s, D)), _sds((8, 128)), _sds((1, D))], compiler_params=_cparams(1),
    )(x3, target, gain)


def _adamw_math(wv, gv, mv, vv):
    mn = ADAM_B1 * mv + (1.0 - ADAM_B1) * gv
    vn = ADAM_B2 * vv + (1.0 - ADAM_B2) * (gv * gv)
    m_hat = mn / (1.0 - ADAM_B1 ** ADAM_STEP)
    v_hat = vn / (1.0 - ADAM_B2 ** ADAM_STEP)
    return -ADAM_LR * (m_hat / (jnp.sqrt(v_hat) + ADAM_EPS) + ADAM_WD * wv), mn, vn


def _row_tile(r, c):
    tr = r
    while tr * c * 4 > (1 << 20) and tr % 16 == 0:
        tr //= 2
    return tr


def _adamw(w, g, m, v, name):
    r, c = w.shape
    tr = _row_tile(r, c)

    def body(w_ref, g_ref, m_ref, v_ref, d_ref, nm_ref, nv_ref):
        d_ref[...], nm_ref[...], nv_ref[...] = _adamw_math(w_ref[...], g_ref[...], m_ref[...], v_ref[...])

    return pl.pallas_call(
        body, name=name, grid=(r // tr,), in_specs=[_rows(tr, c)] * 4, out_specs=[_rows(tr, c)] * 3,
        out_shape=[_sds((r, c))] * 3, compiler_params=_cparams(1),
    )(w, g, m, v)


def _adamw_halves(w, own, recv, m, v, c_arr, name):
    r, c = w.shape
    h = r // 2
    tr = _row_tile(h, c)
    nh = h // tr

    def body(c_ref, w_ref, own_ref, recv_ref, m_ref, v_ref, g_ref, d_ref, nm_ref, nv_ref):
        first_half = pl.program_id(0) < nh
        use_own = first_half == (c_ref[0] == 0)
        gv = jnp.where(use_own, own_ref[...], recv_ref[...])
        g_ref[...] = gv
        d_ref[...], nm_ref[...], nv_ref[...] = _adamw_math(w_ref[...], gv, m_ref[...], v_ref[...])

    full = pl.BlockSpec((tr, c), lambda i, c_ref: (i, 0))
    half = pl.BlockSpec((tr, c), lambda i, c_ref: (i % nh, 0))
    return pl.pallas_call(
        body, name=name, out_shape=[_sds((r, c))] * 4,
        grid_spec=pltpu.PrefetchScalarGridSpec(
            num_scalar_prefetch=1, grid=(2 * nh,), in_specs=[full, half, half, full, full], out_specs=[full] * 4),
        compiler_params=_cparams(1),
    )(c_arr, w, own, recv, m, v)


def _mesh_pos():
    return lax.axis_index("x"), lax.axis_index("y"), lax.axis_index("c")


def _other_chips(x, y):
    return [(1 - x, y), (x, 1 - y), (1 - x, 1 - y)]


class _Comm:
    def __init__(self, inputs, out_shapes, scratch, start, finish, space=pltpu.HBM):
        self.inputs, self.out_shapes, self.scratch = list(inputs), list(out_shapes), list(scratch)
        self.start, self.finish, self.space = start, finish, space


def _comm_call(comm, name):
    ni, no = len(comm.inputs), len(comm.out_shapes)

    def body(*refs):
        comm.start(refs[:ni], refs[ni:ni + no], refs[ni + no:])
        comm.finish(refs[:ni], refs[ni:ni + no], refs[ni + no:])

    spec = pl.BlockSpec(memory_space=comm.space)
    return list(pl.pallas_call(body, name=name, out_shape=comm.out_shapes, in_specs=[spec] * ni, out_specs=[spec] * no,
                               scratch_shapes=comm.scratch)(*comm.inputs))


def _pallas(body, comm, *, name, grid, in_specs, out_specs, out_shape, scratch_shapes, args):
    params = _cparams(len(grid))
    if comm is None:
        outs = pl.pallas_call(body, name=name, grid=grid, in_specs=in_specs, out_specs=out_specs, out_shape=out_shape,
                              scratch_shapes=scratch_shapes, compiler_params=params)(*args)
        return list(outs), []
    n_in, n_out, n_sc = len(in_specs), len(out_specs), len(scratch_shapes)
    ci, co = len(comm.inputs), len(comm.out_shapes)

    def carried(*refs):
        bounds = [0, n_in, n_in + ci, n_in + ci + n_out, n_in + ci + n_out + co, n_in + ci + n_out + co + n_sc, len(refs)]
        ins, cins, outs, couts, scr, csems = [refs[lo:hi] for lo, hi in zip(bounds[:-1], bounds[1:])]
        ids = [pl.program_id(k) for k in range(len(grid))]
        first = functools.reduce(jnp.logical_and, [i == 0 for i in ids])
        last = functools.reduce(jnp.logical_and, [i == g - 1 for i, g in zip(ids, grid)])

        @pl.when(first)
        def _():
            comm.start(cins, couts, csems)

        body(*ins, *outs, *scr)

        @pl.when(last)
        def _():
            comm.finish(cins, couts, csems)

    hbm = pl.BlockSpec(memory_space=pltpu.HBM)
    outs = pl.pallas_call(
        carried, name=name, grid=grid, in_specs=list(in_specs) + [hbm] * ci, out_specs=list(out_specs) + [hbm] * co,
        out_shape=list(out_shape) + comm.out_shapes, scratch_shapes=list(scratch_shapes) + comm.scratch,
        compiler_params=params)(*args, *comm.inputs)
    return list(outs[:n_out]), list(outs[n_out:])


def _gather_comm(arrays, space, block_rows):
    n_arr = len(arrays)

    def plan(x_refs, out_refs, sems):
        send_sems, recv_sems, local_sems = sems
        x, y, c = _mesh_pos()
        me, sibling = (x, y, c), (x, y, 1 - c)
        chips = _other_chips(x, y)

        def slot(a, px, py, pc):
            return out_refs[a].at[4 * px + 2 * py + pc]

        def copy(a, k, block, to, src=None):
            return pltpu.make_async_remote_copy(
                src_ref=slot(a, *block) if src is None else src, dst_ref=slot(a, *block),
                send_sem=send_sems.at[7 * a + k], recv_sem=recv_sems.at[7 * a + k], device_id=to, device_id_type=MESH)

        srcs = [x_refs[a] if block_rows[a] is None else
                x_refs[a].at[pl.ds(pl.multiple_of(c * block_rows[a], 16), block_rows[a]), :] for a in range(n_arr)]
        local = [pltpu.make_async_copy(srcs[a], slot(a, *me), local_sems.at[a]) for a in range(n_arr)]
        first = []
        for a in range(n_arr):
            first += [copy(a, 1 + j, me, (*chip, c), src=srcs[a]) for j, chip in enumerate(chips)]
            first.append(copy(a, 0, me, sibling, src=srcs[a]))
        return me, sibling, chips, c, copy, local, first

    def start(x_refs, out_refs, sems):
        _, _, _, _, _, local, first = plan(x_refs, out_refs, sems)
        for cp in local + first:
            cp.start()

    def finish(x_refs, out_refs, sems):
        me, sibling, chips, c, copy, local, first = plan(x_refs, out_refs, sems)
        passed = []
        for j, chip in enumerate(chips):
            for a in range(n_arr):
                copy(a, 1 + j, (*chip, c), me).wait_recv()
                fwd = copy(a, 4 + j, (*chip, c), sibling)
                fwd.start()
                passed.append(fwd)
        for a in range(n_arr):
            copy(a, 0, sibling, me).wait_recv()
            for j, chip in enumerate(chips):
                copy(a, 4 + j, (*chip, 1 - c), me).wait_recv()
        for cp in first + passed:
            cp.wait_send()
        for cp in local:
            cp.wait()

    out_shapes = [_sds((8, w.shape[0] if r is None else r) + w.shape[1:], w.dtype) for w, r in zip(arrays, block_rows)]
    scratch = [pltpu.SemaphoreType.DMA((7 * n_arr,)), pltpu.SemaphoreType.DMA((7 * n_arr,)), pltpu.SemaphoreType.DMA((n_arr,))]
    return _Comm(arrays, out_shapes, scratch, start, finish, space)


def _weights_gather_comm(shards):
    return _gather_comm(shards, pltpu.HBM, [w.shape[0] // 2 for w in shards])


def _all_shards(gathered):
    return [o.reshape(NSH, 2 * o.shape[1], o.shape[2]) for o in gathered]


def _gather_small(block, name):
    return _comm_call(_gather_comm([block], pltpu.VMEM, [None]), name)[0]


def _sibling_exchange(gs, name):
    n = len(gs)
    halves = [g.shape[1] // 2 for g in gs]

    def body(*refs):
        g_refs, land_refs = refs[:n], refs[n:2 * n]
        send_sems, recv_sems = refs[2 * n:]
        x, y, c = _mesh_pos()
        copies = []
        for a in range(n):
            h = halves[a]
            for s in range(NSH):
                copies.append(pltpu.make_async_remote_copy(
                    src_ref=g_refs[a].at[s, pl.ds(pl.multiple_of((1 - c) * h, 8), h), :], dst_ref=land_refs[a].at[s],
                    send_sem=send_sems.at[NSH * a + s], recv_sem=recv_sems.at[NSH * a + s],
                    device_id=(x, y, 1 - c), device_id_type=MESH))
        for cp in copies:
            cp.start()
        for cp in copies:
            cp.wait()

    return pl.pallas_call(
        body, name=name, out_shape=[_sds((NSH, h, g.shape[2])) for h, g in zip(halves, gs)],
        in_specs=[pl.BlockSpec(memory_space=pltpu.HBM)] * n, out_specs=[pl.BlockSpec(memory_space=pltpu.HBM)] * n,
        scratch_shapes=[pltpu.SemaphoreType.DMA((NSH * n,)), pltpu.SemaphoreType.DMA((NSH * n,))],
    )(*gs)


def _chip_sum(g, land, c_arr, name):
    _, h, cols = land.shape

    def body(c_ref, g_ref, l_ref, o_ref):
        o_ref[...] = (g_ref[...] + l_ref[...]).astype(BF16)

    return pl.pallas_call(
        body, name=name, out_shape=_sds((NSH, h, cols), BF16),
        grid_spec=pltpu.PrefetchScalarGridSpec(
            num_scalar_prefetch=1, grid=(NSH,),
            in_specs=[pl.BlockSpec((1, h, cols), lambda s, c_ref: (s, c_ref[0], 0)),
                      pl.BlockSpec((1, h, cols), lambda s, c_ref: (s, 0, 0))],
            out_specs=pl.BlockSpec((1, h, cols), lambda s, c_ref: (s, 0, 0))),
        compiler_params=_cparams(1),
    )(c_arr, g, land)


def _scatter_comm(parts):
    n = len(parts)

    def plan(p_refs, land_refs, sems):
        send_sems, recv_sems, local_sems = sems
        x, y, c = _mesh_pos()
        my_chip = 2 * x + y
        local = [pltpu.make_async_copy(p_refs[a].at[my_chip], land_refs[a].at[my_chip], local_sems.at[a]) for a in range(n)]
        copies = []
        for a in range(n):
            for j, (px, py) in enumerate(_other_chips(x, y)):
                copies.append(pltpu.make_async_remote_copy(
                    src_ref=p_refs[a].at[2 * px + py], dst_ref=land_refs[a].at[my_chip],
                    send_sem=send_sems.at[3 * a + j], recv_sem=recv_sems.at[3 * a + j],
                    device_id=(px, py, c), device_id_type=MESH))
        return local, copies

    def start(p_refs, land_refs, sems):
        local, copies = plan(p_refs, land_refs, sems)
        for cp in local + copies:
            cp.start()

    def finish(p_refs, land_refs, sems):
        local, copies = plan(p_refs, land_refs, sems)
        for cp in copies:
            cp.wait()
        for cp in local:
            cp.wait()

    scratch = [pltpu.SemaphoreType.DMA((3 * n,)), pltpu.SemaphoreType.DMA((3 * n,)), pltpu.SemaphoreType.DMA((n,))]
    return _Comm(parts, [_sds(p.shape, BF16) for p in parts], scratch, start, finish)


def _sum_slots(land, name):
    k, r, c = land.shape
    tr = r // 2 if r % 32 == 0 else r

    def body(l_ref, o_ref):
        acc = l_ref[0].astype(F32)
        for i in range(1, k):
            acc = acc + l_ref[i].astype(F32)
        o_ref[...] = acc

    return pl.pallas_call(
        body, name=name, grid=(r // tr,), in_specs=[pl.BlockSpec((k, tr, c), lambda i: (0, i, 0))],
        out_specs=_rows(tr, c), out_shape=_sds((r, c)), compiler_params=_cparams(1),
    )(land)


def _sibling_swap(halves):
    n = len(halves)

    def body(*refs):
        h_refs, out_refs = refs[:n], refs[n:2 * n]
        send_sems, recv_sems = refs[2 * n:]
        x, y, c = _mesh_pos()
        copies = [pltpu.make_async_remote_copy(
            src_ref=h_refs[a], dst_ref=out_refs[a], send_sem=send_sems.at[a], recv_sem=recv_sems.at[a],
            device_id=(x, y, 1 - c), device_id_type=MESH) for a in range(n)]
        for cp in copies:
            cp.start()
        for cp in copies:
            cp.wait()

    return pl.pallas_call(
        body, name="grad_sibling_swap", out_shape=[_sds(h.shape) for h in halves],
        in_specs=[pl.BlockSpec(memory_space=pltpu.HBM)] * n, out_specs=[pl.BlockSpec(memory_space=pltpu.HBM)] * n,
        scratch_shapes=[pltpu.SemaphoreType.DMA((n,)), pltpu.SemaphoreType.DMA((n,))],
    )(*halves)


def _pad_rows(v, width):
    flat = v.reshape(-1)
    rows = -(-flat.shape[0] // width)
    rows = -(-rows // 8) * 8
    return jnp.pad(flat, (0, rows * width - flat.shape[0])).reshape(rows, width)


def _size(shape):
    n = 1
    for dim in shape:
        n *= dim
    return n


def _row_pack(arrs):
    pieces = []
    for a in arrs:
        rows = -(-a.size // D)
        pieces.append(jnp.pad(a.reshape(-1), (0, rows * D - a.size)).reshape(rows, D))
    total = sum(p.shape[0] for p in pieces)
    if total % 8:
        pieces.append(jnp.zeros((8 - total % 8, D), F32))
    return jnp.concatenate(pieces, axis=0)


def _row_unpack(packed, shapes):
    out, r0 = [], 0
    for shp in shapes:
        n = _size(shp)
        rows = -(-n // D)
        out.append(packed[r0:r0 + rows].reshape(-1)[:n].reshape(shp))
        r0 += rows
    return out


def _block_diag(w):
    eye = jnp.eye(8, dtype=w.dtype)
    return (w[:, :, None, :] * eye[:, None, :, None]).reshape(RGW, RGW)


def _diag_blocks(dense):
    r = dense.reshape(8, 64, 8, 64)
    return jnp.stack([r[n, :, n, :] for n in range(8)])


def _lane_row(v8):
    return jnp.zeros((1, BAP), F32).at[0, 8:16].set(v8.reshape(8))


def _reduce_parts(gs, names, c_arr, tag):
    lands = _sibling_exchange(gs, "grad_sibling_exchange_" + tag)
    return [_chip_sum(g, l, c_arr, "chip_sum_" + n) for g, l, n in zip(gs, lands, names)]


def _local_step(x, target, sw, ffn1_w, later_shards, c_arr):
    (g1, gmix, rg_cw8, rg_cb, wgates, gbias, lam_row, gdn_cw8, alog_row, dtb_row, gn, g2, gfin) = sw
    wg1, wu1, wd1 = ffn1_w

    (x1, a1, b1, fb1), gathered = _ffn_fwd(x, g1, wg1, wu1, wd1, "ffn1_fwd", comm=_weights_gather_comm(later_shards))
    win_sh, wout_sh, wg2, wu2, wd2 = _all_shards(gathered)
    w_in_full = jnp.transpose(win_sh, (1, 0, 2)).reshape(D, NSH * INSH)
    wout = wout_sh.reshape(D, D)
    w_in_groups = (w_in_full[:, 0:512], w_in_full[:, 512:1024], w_in_full[:, 1024:2560], w_in_full[:, 2560:3072],
                   jnp.pad(w_in_full[:, 3072:3088], ((0, 0), (0, BAP - BAW))))
    h2, p_rgx, p_gate, p_qkv, p_z, p_ba = _inproj(x1, gmix, w_in_groups, "in_proj")
    c_rg = _conv(p_rgx, rg_cw8, rg_cb, "rg_conv")
    c_qkv = _conv(p_qkv, gdn_cw8, jnp.zeros((1, QKVW), F32), "gdn_conv")
    a0, bb0, a1s, bb1, q, k, v, bg = _mix_prep(c_rg, c_qkv, p_ba, wgates, gbias, lam_row, alog_row, dtb_row, "mix_prep")
    hf, hb = _scan_pair(a0, bb0, a1s, bb1, False, "rg_scan")
    tmat, gu, gw, gqd, gkd, gat, gcd = _gdn_local_fwd(q, k, v, bg, "gdn_local_fwd")
    of, s0, vn0, ob, s1, vn1 = _gdn_seq_fwd(gu, gw, gqd, gkd, gat, gcd, "gdn_seq_fwd")
    x2, ymix = _outproj(x1, hf, hb, p_gate, of, ob, p_z, gn, wout, "out_proj")
    (x3, a2, b2, fb2), _ = _ffn_fwd(x2, g2, wg2, wu2, wd2, "ffn2_fwd")
    dx3, loss_blk, d_gfin = _loss_head(x3, target, gfin, "loss_head")

    dx2, d_g2, hb2, dob2, dab2, dbb2 = _ffn_bwd(x2, dx3, g2, a2, b2, wg2, wu2, wd2, "ffn2_bwd")
    d_ffn2 = [_tn(dab2, hb2, "ffn2_dwg"), _tn(dbb2, hb2, "ffn2_dwu"), _tn(fb2, dob2, "ffn2_dwd")]
    parts_ffn2 = _reduce_parts(d_ffn2, _BIG_NAMES[5:8], c_arr, "ffn2")

    d_hr, d_gate, d_os, d_z, d_gn, dx2b = _outproj_bwd(dx2, hf, hb, p_gate, of, ob, p_z, gn, wout, "out_proj_bwd")
    d_wout = _tn(ymix, dx2b, "dw_out")[0]

    lam1, lam0 = _scan_pair(a1s, d_hr, a0, d_hr, True, "rg_scan_bwd")
    d_xc, d_pre, xcb, d_gbias, d_lam = _gates_bwd(c_rg, wgates, gbias, lam_row, lam0, lam1, hf, hb, "rg_gates_bwd")
    d_wgates = _tn(xcb, d_pre, "dw_gates")[0]
    d_prgx, d_rgcw8, d_rgcb = _conv_bwd(p_rgx, d_xc, rg_cw8, "rg_conv_bwd")

    sg = _gdn_seq_bwd(d_os, gw, gqd, gkd, gat, gcd, (s0, s1), (vn0, vn1), "gdn_seq_bwd")
    (dq, dk, dv, dbg), lands_ffn2 = _gdn_local_bwd(q, k, v, bg, tmat, d_os, (vn0, vn1), (sg[0:5], sg[5:10]), "gdn_local_bwd",
                                                  comm=_scatter_comm(parts_ffn2))
    d_cqkv, d_pba, d_alog, d_dtb = _prep_bwd(c_qkv, p_ba, alog_row, dtb_row, dq, dk, dv, dbg, "gdn_prep_bwd")
    d_pqkv, d_gdncw8, _ = _conv_bwd(p_qkv, d_cqkv, gdn_cw8, "gdn_conv_bwd")

    dps = (d_prgx, d_gate, d_pqkv, d_z, d_pba)
    dx1, d_gmix = _inproj_bwd(x1, dx2, gmix, dps, w_in_groups, "in_proj_bwd")
    d_win_groups = [_tn(h2, dp, "dw_in_%d" % i)[0] for i, dp in enumerate(dps)]
    d_win = jnp.concatenate(d_win_groups[:4] + [d_win_groups[4][:, :BAW]], axis=1)
    d_mix = [jnp.transpose(d_win.reshape(D, NSH, INSH), (1, 0, 2)), d_wout.reshape(NSH, OUTSH, D)]
    parts_mix = _reduce_parts(d_mix, _BIG_NAMES[3:5], c_arr, "mix")

    gx, d_g1, hb1, dob1, dab1, dbb1 = _ffn_bwd(x, dx1, g1, a1, b1, wg1, wu1, wd1, "ffn1_bwd")
    d_wg1, lands_mix = _tn(dab1, hb1, "ffn1_dwg", comm=_scatter_comm(parts_mix))
    parts_wg1 = _reduce_parts([d_wg1], _BIG_NAMES[0:1], c_arr, "ffn1_gate")
    d_wu1, lands_wg1 = _tn(dbb1, hb1, "ffn1_dwu", comm=_scatter_comm(parts_wg1))
    parts_wu1 = _reduce_parts([d_wu1], _BIG_NAMES[1:2], c_arr, "ffn1_up")
    d_wd1, lands_wu1 = _tn(fb1, dob1, "ffn1_dwd", comm=_scatter_comm(parts_wu1))
    parts_wd1 = _reduce_parts([d_wd1], _BIG_NAMES[2:3], c_arr, "ffn1_down")
    lands_ffn1 = lands_wg1 + lands_wu1 + _comm_call(_scatter_comm(parts_wd1), "grad_chip_scatter_ffn1_down")

    halves = [_sum_slots(l, "sum_chips_" + n) for l, n in zip(lands_ffn1 + lands_mix + lands_ffn2, _BIG_NAMES)]
    small = dict(
        ffn1_norm=d_g1, mix_norm=d_gmix, rg_conv_w=d_rgcw8[:4], rg_conv_b=d_rgcb,
        rg_gate_a_w=jnp.stack([_diag_blocks(d_wgates[:, RGW * i:RGW * (i + 1)]) for i in (0, 1)]),
        rg_gate_x_w=jnp.stack([_diag_blocks(d_wgates[:, RGW * i:RGW * (i + 1)]) for i in (2, 3)]),
        rg_gate_a_b=d_gbias[0, :2 * RGW].reshape(2, RGW), rg_gate_x_b=d_gbias[0, 2 * RGW:].reshape(2, RGW),
        rg_lambda=d_lam.reshape(2, RGW), gdn_conv_w=d_gdncw8[:4],
        gdn_a_log=d_alog[0, 8:16].reshape(2, NH), gdn_dt_bias=d_dtb[0, 8:16].reshape(2, NH),
        gdn_norm=d_gn, ffn2_norm=d_g2, final_norm=d_gfin)
    return loss_blk, gx, halves, small


_SMALL_NAMES = ("ffn1_norm", "mix_norm", "rg_conv_w", "rg_conv_b", "rg_gate_a_w", "rg_gate_a_b", "rg_gate_x_w",
                "rg_gate_x_b", "rg_lambda", "gdn_conv_w", "gdn_a_log", "gdn_dt_bias", "gdn_norm", "ffn2_norm", "final_norm")
_SMALL_SHARDED = dict(rg_conv_w=128, rg_gate_a_b=128, rg_gate_x_b=128, rg_lambda=128, gdn_conv_w=384)
_OUT_ORDER = ("ffn1_norm", "ffn1_w_gate", "ffn1_w_up", "ffn1_w_down", "mix_norm", "w_in", "w_out", "rg_conv_w", "rg_conv_b",
              "rg_gate_a_w", "rg_gate_a_b", "rg_gate_x_w", "rg_gate_x_b", "rg_lambda", "gdn_conv_w", "gdn_a_log",
              "gdn_dt_bias", "gdn_norm", "ffn2_norm", "ffn2_w_gate", "ffn2_w_up", "ffn2_w_down", "final_norm")
_BIG_NAMES = ("ffn1_w_gate", "ffn1_w_up", "ffn1_w_down", "w_in", "w_out", "ffn2_w_gate", "ffn2_w_up", "ffn2_w_down")
_TRANSPOSED = ("ffn1_w_gate", "ffn1_w_up", "ffn2_w_gate", "ffn2_w_up")


def kernel(x, ffn1_norm, ffn1_w_gate, ffn1_w_up, ffn1_w_down, mix_norm, w_in, w_out, rg_conv_w, rg_conv_b, rg_gate_a_w, rg_gate_a_b, rg_gate_x_w, rg_gate_x_b, rg_lambda, gdn_conv_w, gdn_a_log, gdn_dt_bias, gdn_norm, ffn2_norm, ffn2_w_gate, ffn2_w_up, ffn2_w_down, final_norm, loss_target, m_ffn1_norm, m_ffn1_w_gate, m_ffn1_w_up, m_ffn1_w_down, m_mix_norm, m_w_in, m_w_out, m_rg_conv_w, m_rg_conv_b, m_rg_gate_a_w, m_rg_gate_a_b, m_rg_gate_x_w, m_rg_gate_x_b, m_rg_lambda, m_gdn_conv_w, m_gdn_a_log, m_gdn_dt_bias, m_gdn_norm, m_ffn2_norm, m_ffn2_w_gate, m_ffn2_w_up, m_ffn2_w_down, m_final_norm, v_ffn1_norm, v_ffn1_w_gate, v_ffn1_w_up, v_ffn1_w_down, v_mix_norm, v_w_in, v_w_out, v_rg_conv_w, v_rg_conv_b, v_rg_gate_a_w, v_rg_gate_a_b, v_rg_gate_x_w, v_rg_gate_x_b, v_rg_lambda, v_gdn_conv_w, v_gdn_a_log, v_gdn_dt_bias, v_gdn_norm, v_ffn2_norm, v_ffn2_w_gate, v_ffn2_w_up, v_ffn2_w_down, v_final_norm):
    args = dict(locals())
    w = {n: args[n] for n in _OUT_ORDER}
    mom = {n: args["m_" + n] for n in _OUT_ORDER}
    var = {n: args["v_" + n] for n in _OUT_ORDER}
    xi, yi, ci = _mesh_pos()
    shard = 2 * xi + yi

    big_bf16 = [w[n][0].astype(BF16) for n in _BIG_NAMES]
    ffn1_w = _all_shards(_comm_call(_weights_gather_comm(big_bf16[0:3]), "gather_ffn1_weights"))
    sm_local = _pad_rows(jnp.concatenate([w[n][0].reshape(-1) for n in _SMALL_SHARDED]), 128)
    sm_all = _gather_small(sm_local, "gather_small_weights")[0::2].reshape(NSH, -1)
    sm_full, off = {}, 0
    for n, wd_ in _SMALL_SHARDED.items():
        rows = w[n].shape[1]
        piece = sm_all[:, off:off + rows * wd_].reshape(NSH, rows, wd_)
        sm_full[n] = jnp.transpose(piece, (1, 0, 2)).reshape(rows, NSH * wd_)
        off += rows * wd_

    wa, wx = rg_gate_a_w[0], rg_gate_x_w[0]
    wgates = jnp.concatenate([_block_diag(wa[0]), _block_diag(wa[1]), _block_diag(wx[0]), _block_diag(wx[1])],
                             axis=1).astype(BF16)
    gbias = jnp.concatenate([sm_full["rg_gate_a_b"].reshape(1, -1), sm_full["rg_gate_x_b"].reshape(1, -1)], axis=1)
    sw = (ffn1_norm, mix_norm, jnp.pad(sm_full["rg_conv_w"], ((0, 4), (0, 0))), rg_conv_b, wgates, gbias,
          sm_full["rg_lambda"].reshape(1, -1), jnp.pad(sm_full["gdn_conv_w"], ((0, 4), (0, 0))), _lane_row(gdn_a_log),
          _lane_row(gdn_dt_bias), gdn_norm, ffn2_norm, final_norm.reshape(1, D))
    c_arr = ci.reshape(1).astype(jnp.int32)

    loss_blk, gx, halves, small = _local_step(x[0], loss_target[0], sw, ffn1_w, big_bf16[3:], c_arr)
    loss = lax.psum(loss_blk[0, 0], ("x", "y", "c"))
    grads = {}

    sm_grad = _row_pack([small[n] for n in _SMALL_NAMES])
    sm_sum = _sum_slots(_gather_small(sm_grad, "gather_small_grads"), "small_grad_sum")
    for n, g in zip(_SMALL_NAMES, _row_unpack(sm_sum, [small[n].shape for n in _SMALL_NAMES])):
        if n in _SMALL_SHARDED:
            wd_ = _SMALL_SHARDED[n]
            g = lax.dynamic_slice_in_dim(g, shard * wd_, wd_, axis=1)
        grads[n] = g.reshape(w[n].shape)

    delta, new_m, new_v = {}, {}, {}
    for n, own, recv in zip(_BIG_NAMES, halves, _sibling_swap(halves)):
        to2d = jnp.transpose if n in _TRANSPOSED else (lambda t: t)
        outs4 = _adamw_halves(to2d(w[n][0]), own, recv, to2d(mom[n][0]), to2d(var[n][0]), c_arr, "adamw_" + n)
        grads[n], delta[n], new_m[n], new_v[n] = [to2d(o)[None] for o in outs4]
    packs = [_row_pack([t[n] for n in _SMALL_NAMES]) for t in (w, grads, mom, var)]
    sm_shapes = [w[n].shape for n in _SMALL_NAMES]
    for dst, src in zip((delta, new_m, new_v), _adamw(*packs, "adamw_small")):
        for n, val in zip(_SMALL_NAMES, _row_unpack(src, sm_shapes)):
            dst[n] = val

    outs = [loss, gx[None]]
    for group in (grads, delta, new_m, new_v):
        outs += [group[n] for n in _OUT_ORDER]
    return tuple(outs)
```

```python
import functools

import jax
import jax.numpy as jnp
from jax import lax
from jax.experimental import pallas as pl
from jax.experimental.pallas import tpu as pltpu

F32 = jnp.float32
BF16 = jnp.bfloat16
EPS = 1e-6
D = 1024
NSH = 4
FSH = 704
RGW = 512
QKVW = 1536
ZW = 512
BAW = 16
BAP = 128
INSH = 772
OUTSH = 256
CHUNK = 64
NH = 4
DH = 128
RG_C = 8.0
VMEM_LIMIT = 52 * 1024 * 1024
MESH = pl.DeviceIdType.MESH

ADAM_LR = 0.001
ADAM_B1 = 0.9
ADAM_B2 = 0.999
ADAM_EPS = 1e-08
ADAM_WD = 0.01
ADAM_STEP = 10


def _cparams(n_grid):
    return pltpu.CompilerParams(dimension_semantics=("arbitrary",) * n_grid, vmem_limit_bytes=VMEM_LIMIT)


def _sig(x):
    return 0.5 + 0.5 * jnp.tanh(0.5 * x)


def _sig_pos(x):
    return 1.0 / (1.0 + jnp.exp(-x))


def _softplus(x):
    return jnp.maximum(x, 0.0) + jnp.log(1.0 + jnp.exp(-jnp.abs(x)))


def _neg_expm1(y):
    series = -y * (1.0 + y * (0.5 + y * (1.0 / 6 + y * (1.0 / 24 + y * (1.0 / 120 + y * (1.0 / 720 + y / 5040))))))
    return jnp.where(y > -0.3, series, 1.0 - jnp.exp(y))


_GELU_C = 0.7978845608028654


def _gelu(x):
    t = jnp.tanh(_GELU_C * (x + 0.044715 * x * x * x))
    return 0.5 * x * (1.0 + t)


def _gelu_grad(x):
    t = jnp.tanh(_GELU_C * (x + 0.044715 * x * x * x))
    return 0.5 * (1.0 + t) + 0.5 * x * (1.0 - t * t) * _GELU_C * (1.0 + 3 * 0.044715 * x * x)


def _silu_grad(x):
    s = _sig(x)
    return s * (1.0 + x * (1.0 - s))


def _dot(a, b):
    return jnp.dot(a.astype(BF16), b.astype(BF16), preferred_element_type=F32)


def _dot_nt(a, b):
    return lax.dot_general(a.astype(BF16), b.astype(BF16), (((1,), (1,)), ((), ())), preferred_element_type=F32)


def _dot_tn(a, b):
    return lax.dot_general(a.astype(BF16), b.astype(BF16), (((0,), (0,)), ((), ())), preferred_element_type=F32)


_NN = ((1,), (0,))
_NT = ((1,), (1,))
_TN = ((0,), (0,))


def _dg(a, b, dims):
    return lax.dot_general(a, b, (dims, ((), ())), preferred_element_type=F32)


def _split2(a):
    hi = a.astype(BF16)
    return hi, (a - hi.astype(F32)).astype(BF16)


def _dot3(a, b, dims=_NN):
    ah, al = _split2(a)
    bh, bl = _split2(b)
    return _dg(ah, bh, dims) + _dg(ah, bl, dims) + _dg(al, bh, dims)


def _dot_exact(e, x, dims, e_is_lhs):
    x0 = x.astype(BF16)
    r = x - x0.astype(F32)
    x1 = r.astype(BF16)
    x2 = (r - x1.astype(F32)).astype(BF16)
    eb = e.astype(BF16)
    if e_is_lhs:
        return _dg(eb, x0, dims) + _dg(eb, x1, dims) + _dg(eb, x2, dims)
    return _dg(x0, eb, dims) + _dg(x1, eb, dims) + _dg(x2, eb, dims)


def _rms(xv):
    r = lax.rsqrt(jnp.mean(xv * xv, axis=-1, keepdims=True) + EPS)
    return r, xv * r


def _rms_bwd(dy, xh, r, gain):
    dxh = dy * gain
    return r * (dxh - xh * jnp.mean(dxh * xh, axis=-1, keepdims=True))


def _colsum(v):
    return jnp.sum(v, axis=0, keepdims=True)


def _rows(t, c):
    return pl.BlockSpec((t, c), lambda i: (i, 0))


def _full(shape):
    n = len(shape)
    return pl.BlockSpec(shape, lambda i: (0,) * n)


def _sds(shape, dtype=F32):
    return jax.ShapeDtypeStruct(shape, dtype)


def _ffn_fwd(x, gain, wg, wu, wd, name, comm=None):
    s = x.shape[0]
    tm = min(512, s)

    def body(x_ref, g_ref, wg_ref, wu_ref, wd_ref, xo_ref, a_ref, b_ref, f_ref, h_sc, acc):
        j = pl.program_id(1)

        @pl.when(j == 0)
        def _():
            _, xh = _rms(x_ref[...])
            h_sc[...] = (xh * g_ref[...]).astype(BF16)
            acc[...] = jnp.zeros_like(acc)

        h = h_sc[...]

        a = jnp.dot(h, wg_ref[0], preferred_element_type=F32)
        b = jnp.dot(h, wu_ref[0], preferred_element_type=F32)
        a_ref[0] = a.astype(BF16)
        b_ref[0] = b.astype(BF16)
        f = (a * _sig(a) * b).astype(BF16)
        f_ref[0] = f
        acc[...] += jnp.dot(f, wd_ref[0], preferred_element_type=F32)

        @pl.when(j == NSH - 1)
        def _():
            xo_ref[...] = x_ref[...] + 0.5 * acc[...]

    return _pallas(
        body, comm, name=name, grid=(s // tm, NSH),
        in_specs=[pl.BlockSpec((tm, D), lambda i, j: (i, 0)), pl.BlockSpec((1, D), lambda i, j: (0, 0)),
                  pl.BlockSpec((1, D, FSH), lambda i, j: (j, 0, 0)), pl.BlockSpec((1, D, FSH), lambda i, j: (j, 0, 0)),
                  pl.BlockSpec((1, FSH, D), lambda i, j: (j, 0, 0))],
        out_specs=[pl.BlockSpec((tm, D), lambda i, j: (i, 0))] + [pl.BlockSpec((1, tm, FSH), lambda i, j: (j, i, 0))] * 3,
        out_shape=[_sds((s, D))] + [_sds((NSH, s, FSH), BF16)] * 3,
        scratch_shapes=[pltpu.VMEM((tm, D), BF16), pltpu.VMEM((tm, D), F32)],
        args=(x, gain, wg, wu, wd))


def _ffn_bwd(x, dout, gain, a, b, wg, wu, wd, name):
    s = x.shape[0]
    tm = min(512, s)

    def hidden(d_ref, a_ref, b_ref, wd_ref, do_ref, da_ref, db_ref, do_sc):
        @pl.when(pl.program_id(1) == 0)
        def _():
            do = (0.5 * d_ref[...]).astype(BF16)
            do_sc[...] = do
            do_ref[...] = do

        df = _dot_nt(do_sc[...], wd_ref[0])
        av = a_ref[0].astype(F32)
        bv = b_ref[0].astype(F32)
        sa = _sig(av)
        da_ref[0] = (df * bv * sa * (1.0 + av * (1.0 - sa))).astype(BF16)
        db_ref[0] = (df * av * sa).astype(BF16)

    th = min(1024, s)
    tok = pl.BlockSpec((th, D), lambda i, j: (i, 0))
    sh = pl.BlockSpec((1, th, FSH), lambda i, j: (j, i, 0))
    do, da, db = pl.pallas_call(
        hidden, name=name + "_hidden", grid=(s // th, NSH),
        in_specs=[tok, sh, sh, pl.BlockSpec((1, FSH, D), lambda i, j: (j, 0, 0))], out_specs=[tok, sh, sh],
        out_shape=[_sds((s, D), BF16)] + [_sds((NSH, s, FSH), BF16)] * 2,
        scratch_shapes=[pltpu.VMEM((th, D), BF16)], compiler_params=_cparams(2),
    )(dout, a, b, wd)

    def inputs(x_ref, d_ref, g_ref, da_ref, db_ref, wg_ref, wu_ref, dx_ref, dg_ref, h_ref):
        @pl.when(pl.program_id(0) == 0)
        def _():
            dg_ref[...] = jnp.zeros_like(dg_ref)

        dh = jnp.zeros((tm, D), F32)
        for j in range(NSH):
            dh = dh + _dot_nt(da_ref[j], wg_ref[j]) + _dot_nt(db_ref[j], wu_ref[j])
        r, xh = _rms(x_ref[...])
        gv = g_ref[...]
        h_ref[...] = (xh * gv).astype(BF16)
        dg_ref[...] += _colsum(dh * xh)
        dx_ref[...] = d_ref[...] + _rms_bwd(dh, xh, r, gv)

    grads = pl.BlockSpec((NSH, tm, FSH), lambda i: (0, i, 0))
    resident = pl.BlockSpec((NSH, D, FSH), lambda i: (0, 0, 0), pipeline_mode=pl.Buffered(1))
    dx, dg, h = pl.pallas_call(
        inputs, name=name + "_input", grid=(s // tm,),
        in_specs=[_rows(tm, D), _rows(tm, D), _full((1, D)), grads, grads, resident, resident],
        out_specs=[_rows(tm, D), _full((1, D)), _rows(tm, D)],
        out_shape=[_sds((s, D)), _sds((1, D)), _sds((s, D), BF16)], compiler_params=_cparams(1),
    )(x, dout, gain, da, db, wg, wu)
    return dx, dg, h, do, da, db


def _tn(a, b, name, comm=None):
    a_g = a.ndim == 3
    b_g = b.ndim == 3
    g = a.shape[0] if a_g else (b.shape[0] if b_g else 1)
    s, k = a.shape[-2:]
    n = b.shape[-1]
    ts = min(1024, s)

    def body(a_ref, b_ref, o_ref):
        @pl.when(pl.program_id(1) == 0)
        def _():
            o_ref[...] = jnp.zeros_like(o_ref)

        av = a_ref[0] if a_g else a_ref[...]
        bv = b_ref[0] if b_g else b_ref[...]
        o_ref[0] += _dot_tn(av, bv)

    a_spec = pl.BlockSpec((1, ts, k), lambda gi, si: (gi, si, 0)) if a_g else pl.BlockSpec((ts, k), lambda gi, si: (si, 0))
    b_spec = pl.BlockSpec((1, ts, n), lambda gi, si: (gi, si, 0)) if b_g else pl.BlockSpec((ts, n), lambda gi, si: (si, 0))
    outs, carried = _pallas(body, comm, name=name, grid=(g, s // ts), in_specs=[a_spec, b_spec],
                            out_specs=[pl.BlockSpec((1, k, n), lambda gi, si: (gi, 0, 0))], out_shape=[_sds((g, k, n))],
                            scratch_shapes=[], args=(a, b))
    return outs[0] if comm is None else (outs[0], carried)


_P_WIDTHS = (RGW, RGW, QKVW, ZW, BAP)


def _inproj(x1, gain, ws, name):
    s = x1.shape[0]
    tm = min(256, s)

    def body(x_ref, g_ref, *refs):
        w_refs = refs[:5]
        h_ref = refs[5]
        p_refs = refs[6:]
        _, xh = _rms(x_ref[...])
        h = (xh * g_ref[...]).astype(BF16)
        h_ref[...] = h
        for w_ref, p_ref in zip(w_refs, p_refs):
            p_ref[...] = jnp.dot(h, w_ref[...], preferred_element_type=F32)

    return pl.pallas_call(
        body, name=name, grid=(s // tm,),
        in_specs=[_rows(tm, D), _full((1, D))] + [_full((D, w)) for w in _P_WIDTHS],
        out_specs=[_rows(tm, D)] + [_rows(tm, w) for w in _P_WIDTHS],
        out_shape=[_sds((s, D), BF16)] + [_sds((s, w)) for w in _P_WIDTHS],
        compiler_params=_cparams(1),
    )(x1, gain, *ws)


def _inproj_bwd(x1, dx2, gain, dps, ws, name):
    s = x1.shape[0]
    tm = min(256, s)

    def body(x_ref, d_ref, g_ref, *refs):
        dp_refs = refs[:5]
        w_refs = refs[5:10]
        dx_ref, dg_ref = refs[10:]

        @pl.when(pl.program_id(0) == 0)
        def _():
            dg_ref[...] = jnp.zeros_like(dg_ref)

        dh = jnp.zeros((tm, D), F32)
        for dp_ref, w_ref in zip(dp_refs, w_refs):
            dh = dh + _dot_nt(dp_ref[...], w_ref[...])
        r, xh = _rms(x_ref[...])
        dg_ref[...] += _colsum(dh * xh)
        dx_ref[...] = d_ref[...] + _rms_bwd(dh, xh, r, g_ref[...])

    return pl.pallas_call(
        body, name=name, grid=(s // tm,),
        in_specs=[_rows(tm, D), _rows(tm, D), _full((1, D))] + [_rows(tm, w) for w in _P_WIDTHS]
        + [_full((D, w)) for w in _P_WIDTHS],
        out_specs=[_rows(tm, D), _full((1, D))],
        out_shape=[_sds((s, D)), _sds((1, D))],
        compiler_params=_cparams(1),
    )(x1, dx2, gain, *dps, *ws)


def _halo_specs(s, t, c):
    nb8 = s // 8
    tb = t // 8
    prev = pl.BlockSpec((8, c), lambda i: (jnp.maximum(i * tb - 1, 0), 0))
    nxt = pl.BlockSpec((8, c), lambda i: (jnp.minimum((i + 1) * tb, nb8 - 1), 0))
    return prev, nxt


def _edge_masks(nb):
    i = pl.program_id(0)
    return jnp.where(i > 0, 1.0, 0.0).astype(F32), jnp.where(i < nb - 1, 1.0, 0.0).astype(F32)


def _shifted(xx, off, t):
    n = t + 16
    sh = (-off) % n
    rolled = xx if sh == 0 else pltpu.roll(xx, sh, 0)
    return rolled[8:8 + t]


def _conv(x, w8, bias, name):
    s, c = x.shape
    t = min(256, s)
    nb = s // t

    def body(x_ref, xp_ref, xn_ref, w_ref, b_ref, o_ref):
        pm, nm = _edge_masks(nb)
        for c0 in range(0, c, 512):
            cols = slice(c0, c0 + 512)
            xx = jnp.concatenate([xp_ref[:, cols] * pm, x_ref[:, cols], xn_ref[:, cols] * nm], axis=0)
            acc = jnp.zeros((t, 512), F32) + b_ref[:, cols]
            for j in range(4):
                acc = acc + w_ref[j:j + 1, cols] * _shifted(xx, j - 2, t)
            o_ref[:, cols] = acc

    prev, nxt = _halo_specs(s, t, c)
    return pl.pallas_call(
        body, name=name, grid=(nb,),
        in_specs=[_rows(t, c), prev, nxt, _full((8, c)), _full((1, c))],
        out_specs=_rows(t, c), out_shape=_sds((s, c)), compiler_params=_cparams(1),
    )(x, x, x, w8, bias)


def _conv_bwd(x, dc, w8, name):
    s, c = x.shape
    t = min(256, s)
    nb = s // t

    def body(x_ref, d_ref, dp_ref, dn_ref, w_ref, dx_ref, dw_ref, db_ref):
        @pl.when(pl.program_id(0) == 0)
        def _():
            dw_ref[...] = jnp.zeros_like(dw_ref)
            db_ref[...] = jnp.zeros_like(db_ref)

        pm, nm = _edge_masks(nb)
        for c0 in range(0, c, 512):
            cols = slice(c0, c0 + 512)
            dd = jnp.concatenate([dp_ref[:, cols] * pm, d_ref[:, cols], dn_ref[:, cols] * nm], axis=0)
            xv = x_ref[:, cols]
            acc = jnp.zeros((t, 512), F32)
            for j in range(4):
                dsh = _shifted(dd, 2 - j, t)
                acc = acc + w_ref[j:j + 1, cols] * dsh
                dw_ref[j:j + 1, cols] += _colsum(dsh * xv)
            dx_ref[:, cols] = acc
            db_ref[:, cols] += _colsum(d_ref[:, cols])

    prev, nxt = _halo_specs(s, t, c)
    return pl.pallas_call(
        body, name=name, grid=(nb,),
        in_specs=[_rows(t, c), _rows(t, c), prev, nxt, _full((8, c))],
        out_specs=[_rows(t, c), _full((8, c)), _full((1, c))],
        out_shape=[_sds((s, c)), _sds((8, c)), _sds((1, c))], compiler_params=_cparams(1),
    )(x, dc, dc, dc, w8)


def _rg_gates(xc, pre, lam_row):
    sp8 = RG_C * _softplus(-lam_row)
    out = []
    for d in range(2):
        r = _sig_pos(pre[:, RGW * d:RGW * (d + 1)])
        gi = _sig(pre[:, 2 * RGW + RGW * d:2 * RGW + RGW * (d + 1)])
        la = -r * sp8[:, RGW * d:RGW * (d + 1)]
        a = jnp.exp(la)
        mult = jnp.sqrt(_neg_expm1(2.0 * la))
        out.append((r, gi, a, mult))
    return out


def _mix_prep(c_rg, c_qkv, p_ba, wgates, gbias, lam_row, alog_row, dtb_row, name):
    s = c_rg.shape[0]
    t = min(256, s)

    def body(xc_ref, cq_ref, pc_ref, wg_ref, gb_ref, lam_ref, alog_ref, dtb_ref,
             a0_ref, b0_ref, a1_ref, b1_ref, q_ref, k_ref, v_ref, bg_ref):
        xc = xc_ref[...]
        pre = _dot(xc, wg_ref[...]) + gb_ref[...]
        gates = _rg_gates(xc, pre, lam_ref[...])
        for (r, gi, a, mult), a_ref, b_ref in zip(gates, (a0_ref, a1_ref), (b0_ref, b1_ref)):
            a_ref[...] = a
            b_ref[...] = mult * gi * xc
        cq = cq_ref[...]
        sq = cq * _sig(cq)
        for h in range(NH):
            sl = slice(DH * h, DH * (h + 1))
            qh = sq[:, sl]
            q_ref[:, sl] = qh * lax.rsqrt(jnp.sum(qh * qh, axis=-1, keepdims=True) + EPS) * (DH ** -0.5)
            kh = sq[:, RGW + DH * h:RGW + DH * (h + 1)]
            k_ref[:, sl] = kh * lax.rsqrt(jnp.sum(kh * kh, axis=-1, keepdims=True) + EPS)
        v_ref[...] = sq[:, 2 * RGW:]
        pc = pc_ref[...]
        lane = lax.broadcasted_iota(jnp.int32, pc.shape, 1)
        beta = _sig(pc)
        g = -jnp.exp(alog_ref[...]) * _softplus(pc + dtb_ref[...])
        bg_ref[...] = jnp.where(lane < 8, beta, jnp.where(lane < 16, g, 0.0))

    return pl.pallas_call(
        body, name=name, grid=(s // t,),
        in_specs=[_rows(t, RGW), _rows(t, QKVW), _rows(t, BAP), _full((RGW, 4 * RGW)), _full((1, 4 * RGW)),
                  _full((1, 2 * RGW)), _full((1, BAP)), _full((1, BAP))],
        out_specs=[_rows(t, RGW)] * 7 + [_rows(t, BAP)],
        out_shape=[_sds((s, RGW))] * 7 + [_sds((s, BAP))],
        compiler_params=_cparams(1),
    )(c_rg, c_qkv, p_ba, wgates, gbias, lam_row, alog_row, dtb_row)


def _scan_pair(af, bf, ar, br, shifted, name):
    s, c = af.shape
    t = min(512, s)
    nb = s // t
    ng = t // 8
    tb = t // 8
    up = lambda i: (i, 0)
    down = lambda i: (nb - 1 - i, 0)

    def body(*refs):
        if shifted:
            af_ref, bf_ref, ar_ref, br_ref, afp_ref, arn_ref, hf_ref, hr_ref, carry, fbuf, rbuf = refs
        else:
            af_ref, bf_ref, ar_ref, br_ref, hf_ref, hr_ref, carry = refs
        i = pl.program_id(0)

        @pl.when(i == 0)
        def _():
            carry[...] = jnp.zeros_like(carry)

        if shifted:
            edge = jnp.where(i > 0, 1.0, 0.0).astype(F32)
            fbuf[0:8, :] = afp_ref[...] * edge
            fbuf[8:t + 8, :] = af_ref[...]
            rbuf[0:t, :] = ar_ref[...]
            rbuf[t:t + 8, :] = arn_ref[...] * edge
        row = lax.broadcasted_iota(jnp.int32, (8, c), 0)

        def block_scan(av, bv, downwards):
            for k in (1, 2, 4):
                sh = (8 - k) if downwards else k
                m = (row < 8 - k) if downwards else (row >= k)
                a_s = pltpu.roll(av, sh, 0)
                b_s = pltpu.roll(bv, sh, 0)
                bv = jnp.where(m, av * b_s + bv, bv)
                av = jnp.where(m, av * a_s, av)
            return av, bv

        def group(gi, cvs):
            cf, cr = cvs
            rf = pl.multiple_of(gi * 8, 8)
            rr = pl.multiple_of((ng - 1 - gi) * 8, 8)
            if shifted:
                a_f = jnp.where(row > 0, pltpu.roll(fbuf[pl.ds(rf + 8, 8), :], 1, 0), pltpu.roll(fbuf[pl.ds(rf, 8), :], 1, 0))
                a_r = jnp.where(row < 7, pltpu.roll(rbuf[pl.ds(rr, 8), :], 7, 0), pltpu.roll(rbuf[pl.ds(rr + 8, 8), :], 7, 0))
            else:
                a_f = af_ref[pl.ds(rf, 8), :]
                a_r = ar_ref[pl.ds(rr, 8), :]
            a_f, b_f = block_scan(a_f, bf_ref[pl.ds(rf, 8), :], False)
            a_r, b_r = block_scan(a_r, br_ref[pl.ds(rr, 8), :], True)
            h_f = a_f * cf + b_f
            h_r = a_r * cr + b_r
            hf_ref[pl.ds(rf, 8), :] = h_f
            hr_ref[pl.ds(rr, 8), :] = h_r
            return h_f[7:8, :], h_r[0:1, :]

        cf, cr = lax.fori_loop(0, ng, group, (carry[0:1, :], carry[8:9, :]))
        carry[0:1, :] = cf
        carry[8:9, :] = cr

    in_specs = [pl.BlockSpec((t, c), up), pl.BlockSpec((t, c), up), pl.BlockSpec((t, c), down), pl.BlockSpec((t, c), down)]
    args = [af, bf, ar, br]
    scratch = [pltpu.VMEM((16, c), F32)]
    if shifted:
        in_specs += [pl.BlockSpec((8, c), lambda i: (jnp.maximum(i * tb - 1, 0), 0)),
                     pl.BlockSpec((8, c), lambda i: (jnp.minimum((nb - i) * tb, s // 8 - 1), 0))]
        args += [af, ar]
        scratch += [pltpu.VMEM((t + 8, c), F32), pltpu.VMEM((t + 8, c), F32)]
    return pl.pallas_call(
        body, name=name, grid=(nb,), in_specs=in_specs,
        out_specs=[pl.BlockSpec((t, c), up), pl.BlockSpec((t, c), down)], out_shape=[_sds((s, c)), _sds((s, c))],
        scratch_shapes=scratch, compiler_params=_cparams(1),
    )(*args)


def _gates_bwd(xc, wgates, gbias, lam_row, lam0, lam1, hf, hb, name):
    s = xc.shape[0]
    t = min(256, s)
    nb = s // t

    def body(xc_ref, wg_ref, gb_ref, lam_ref, l0_ref, l1_ref, hf_ref, hfp_ref, hfn_ref, hb_ref, hbp_ref, hbn_ref,
             dxc_ref, dpre_ref, xcb_ref, dgb_ref, dlam_ref):
        @pl.when(pl.program_id(0) == 0)
        def _():
            dgb_ref[...] = jnp.zeros_like(dgb_ref)
            dlam_ref[...] = jnp.zeros_like(dlam_ref)

        pm, nm = _edge_masks(nb)
        h_prev = _shifted(jnp.concatenate([hfp_ref[...] * pm, hf_ref[...], hfn_ref[...] * nm], axis=0), -1, t)
        h_next = _shifted(jnp.concatenate([hbp_ref[...] * pm, hb_ref[...], hbn_ref[...] * nm], axis=0), 1, t)
        h_shift = (h_prev, h_next)
        xv = xc_ref[...]
        pre = _dot(xv, wg_ref[...]) + gb_ref[...]
        lam_row_v = lam_ref[...]
        sp8 = RG_C * _softplus(-lam_row_v)
        dsp_dlam = -RG_C * _sig(-lam_row_v)
        gates = _rg_gates(xv, pre, lam_row_v)
        dxc = jnp.zeros((t, RGW), F32)
        dpre_r = []
        dpre_i = []
        for d, ((r, gi, a, mult), l_ref, hs) in enumerate(zip(gates, (l0_ref, l1_ref), h_shift)):
            dbb = l_ref[...]
            da = dbb * hs
            cs = slice(RGW * d, RGW * (d + 1))
            dmult = dbb * gi * xv
            dgi = dbb * mult * xv
            dxc = dxc + dbb * mult * gi
            dla = da * a - dmult * a * a / mult
            dr = -dla * sp8[:, cs]
            dlam_ref[:, cs] += _colsum(-dla * r) * dsp_dlam[:, cs]
            dpre_r.append(dr * r * (1.0 - r))
            dpre_i.append(dgi * gi * (1.0 - gi))
        dpre = jnp.concatenate(dpre_r + dpre_i, axis=1)
        dgb_ref[...] += _colsum(dpre)
        dpre_b = dpre.astype(BF16)
        dpre_ref[...] = dpre_b
        xcb_ref[...] = xv.astype(BF16)
        dxc_ref[...] = dxc + _dot_nt(dpre_b, wg_ref[...])

    prev, nxt = _halo_specs(s, t, RGW)
    return pl.pallas_call(
        body, name=name, grid=(s // t,),
        in_specs=[_rows(t, RGW), _full((RGW, 4 * RGW)), _full((1, 4 * RGW)), _full((1, 2 * RGW))] + [_rows(t, RGW)] * 2
        + [_rows(t, RGW), prev, nxt] * 2,
        out_specs=[_rows(t, RGW), _rows(t, 4 * RGW), _rows(t, RGW), _full((1, 4 * RGW)), _full((1, 2 * RGW))],
        out_shape=[_sds((s, RGW)), _sds((s, 4 * RGW), BF16), _sds((s, RGW), BF16), _sds((1, 4 * RGW)), _sds((1, 2 * RGW))],
        compiler_params=_cparams(1),
    )(xc, wgates, gbias, lam_row, lam0, lam1, hf, hf, hf, hb, hb, hb)


class _GdnMasks:
    def __init__(self, d):
        ri = lax.broadcasted_iota(jnp.int32, (CHUNK, CHUNK), 0)
        ci = lax.broadcasted_iota(jnp.int32, (CHUNK, CHUNK), 1)
        self.incl = (ri >= ci) if d == 0 else (ri <= ci)
        self.strict = (ri > ci) if d == 0 else (ri < ci)
        b16 = jnp.right_shift(ri, 4) == jnp.right_shift(ci, 4)
        b32 = jnp.right_shift(ri, 5) == jnp.right_shift(ci, 5)
        self.diag16 = b16
        self.off32 = jnp.logical_and(b32, jnp.logical_not(b16))
        self.off64 = jnp.logical_not(b32)
        self.eye = jnp.where(ri == ci, 1.0, 0.0).astype(F32)
        self.tri = jnp.where(self.incl, 1.0, 0.0).astype(F32)
        self.last = CHUNK - 1 if d == 0 else 0


def _tri_inv(lmat, m):
    return _tri_inv_many([lmat], [m])[0]


def _tri_inv_many(lmats, masks):
    n = len(lmats)
    ns = [jnp.where(masks[i].diag16, lmats[i], 0.0) for i in range(n)]
    ps = [masks[i].eye - ns[i] for i in range(n)]
    qs = [_dot3(ns[i], ns[i]) for i in range(n)]
    for step in range(3):
        ps = [_dot3(ps[i], masks[i].eye + qs[i]) for i in range(n)]
        if step < 2:
            qs = [_dot3(qs[i], qs[i]) for i in range(n)]
    for off in ("off32", "off64"):
        ts = [_dot3(ps[i], jnp.where(getattr(masks[i], off), lmats[i], 0.0)) for i in range(n)]
        ps = [ps[i] - _dot3(ts[i], ps[i]) for i in range(n)]
    return ps


def _chunk_cumsums(m, bgv):
    return _dot_exact(m.tri, bgv, _NN, True), _dot_exact(m.tri, bgv, ((0,), (1,)), False)


class _GdnHead:
    def __init__(self, qh, kh, vh, kk, q0, bg, gcs, gcs_t, d, h, m):
        cb = 4 * d + h
        cg = 8 + 4 * d + h
        self.q, self.k, self.v = qh, kh, vh
        self.beta = bg[:, cb:cb + 1]
        gcol = gcs[:, cg:cg + 1]
        grow = gcs_t[cg:cg + 1, :]
        gl = gcs[m.last:m.last + 1, cg:cg + 1]
        self.decay = jnp.exp(jnp.where(m.incl, gcol - grow, -1e30))
        self.kb = kh * self.beta
        self.vb = vh * self.beta
        self.a0 = kk * self.beta
        self.q0 = q0
        self.lmat = jnp.where(m.strict, self.a0 * self.decay, 0.0)
        self.attn = self.q0 * self.decay
        self.eg = jnp.exp(gcol)
        self.ek = jnp.exp(gl - gcol)
        self.cd = jnp.exp(gl)
        self.kg = self.kb * self.eg
        self.qd = qh * self.eg
        self.kd = kh * self.ek


HW = NH * DH
SEQ_CB = 4
LOCAL_CB = 4


def _head(h):
    return slice(DH * h, DH * (h + 1))


def _gdn_local_fwd(q, k, v, bg, name):
    s = q.shape[0]
    n = s // CHUNK
    cb = min(LOCAL_CB, n)

    def body(q_ref, k_ref, v_ref, bg_ref, t_ref, u_ref, w_ref, qd_ref, kd_ref, at_ref, cd_ref):
        masks = [_GdnMasks(d) for d in range(2)]
        inst = []
        for jj in range(cb):
            rows = slice(CHUNK * jj, CHUNK * (jj + 1))
            bgv = bg_ref[rows, :]
            qs = [q_ref[rows, _head(h)] for h in range(NH)]
            ks = [k_ref[rows, _head(h)] for h in range(NH)]
            kk = [_dot_nt(ks[h], ks[h]) for h in range(NH)]
            q0 = [_dot_nt(qs[h], ks[h]) for h in range(NH)]
            for d, m in enumerate(masks):
                gcs, gcs_t = _chunk_cumsums(m, bgv)
                for h in range(NH):
                    c = _GdnHead(qs[h], ks[h], v_ref[rows, _head(h)], kk[h], q0[h], bgv, gcs, gcs_t, d, h, m)
                    inst.append((jj, rows, d, h, m, c))
        tms = _tri_inv_many([it[-1].lmat for it in inst], [it[-2] for it in inst])
        for (jj, rows, d, h, m, c), tm in zip(inst, tms):
            sl = _head(h)
            t_ref[jj, d, h] = tm
            u_ref[d, rows, sl] = _dot(tm, c.vb)
            w_ref[d, rows, sl] = _dot(tm, c.kg).astype(BF16)
            qd_ref[d, rows, sl] = c.qd.astype(BF16)
            kd_ref[d, rows, sl] = c.kd.astype(BF16)
            at_ref[jj, d, h] = c.attn.astype(BF16)
            cd_ref[jj, 4 * d + h:4 * d + h + 1, :] = jnp.broadcast_to(c.cd, (1, DH))

    tok = _rows(cb * CHUNK, HW)
    tok2 = pl.BlockSpec((2, cb * CHUNK, HW), lambda i: (0, i, 0))
    mat = pl.BlockSpec((cb, 2, NH, CHUNK, CHUNK), lambda i: (i, 0, 0, 0, 0))
    return pl.pallas_call(
        body, name=name, grid=(n // cb,), in_specs=[tok, tok, tok, _rows(cb * CHUNK, BAP)],
        out_specs=[mat, tok2, tok2, tok2, tok2, mat, pl.BlockSpec((cb, 8, DH), lambda i: (i, 0, 0))],
        out_shape=[_sds((n, 2, NH, CHUNK, CHUNK)), _sds((2, s, HW)), _sds((2, s, HW), BF16), _sds((2, s, HW), BF16),
                   _sds((2, s, HW), BF16), _sds((n, 2, NH, CHUNK, CHUNK), BF16), _sds((n, 8, DH))],
        compiler_params=_cparams(1),
    )(q, k, v, bg)


def _seq_specs(s, order):
    n = s // CHUNK
    cb = min(SEQ_CB, n)
    nb = n // cb
    tb = cb * CHUNK

    def blk(d):
        return (lambda i: i) if order[d] else (lambda i: nb - 1 - i)

    def per_dir(make):
        return [make(d, blk(d)) for d in range(2)]

    tok2 = per_dir(lambda d, f: pl.BlockSpec((1, tb, HW), lambda i: (d, f(i), 0)))
    tok = per_dir(lambda d, f: pl.BlockSpec((tb, HW), lambda i: (f(i), 0)))
    mat = per_dir(lambda d, f: pl.BlockSpec((cb, 1, NH, CHUNK, CHUNK), lambda i: (f(i), d, 0, 0, 0)))
    cds = per_dir(lambda d, f: pl.BlockSpec((cb, 8, DH), lambda i: (f(i), 0, 0)))
    sts = per_dir(lambda d, f: pl.BlockSpec((cb, NH, DH, DH), lambda i: (f(i), 0, 0, 0)))
    dcd = per_dir(lambda d, f: pl.BlockSpec((cb, NH, DH), lambda i: (f(i), 0, 0)))
    return n, cb, nb, tok2, tok, mat, cds, sts, dcd


def _gdn_seq_fwd(u, w, qd, kd, at, cd, name):
    s = u.shape[1]
    n, cb, nb, tok2, tok, mat, cds, sts, _ = _seq_specs(s, (True, False))

    def body(*refs):
        ins = (refs[0:6], refs[6:12])
        outs = (refs[12:15], refs[15:18])
        st = refs[18]

        @pl.when(pl.program_id(0) == 0)
        def _():
            st[...] = jnp.zeros_like(st)

        for j in range(cb):
            items = []
            for d in range(2):
                jj = j if d == 0 else cb - 1 - j
                items += [(d, h, jj, slice(CHUNK * jj, CHUNK * (jj + 1)), _head(h)) for h in range(NH)]
            shs = [st[d, h] for d, h, _, _, _ in items]
            wss = [_dot(ins[d][1][0, rows, sl], sh) for (d, h, jj, rows, sl), sh in zip(items, shs)]
            vns = [ins[d][0][0, rows, sl] - ws for (d, h, jj, rows, sl), ws in zip(items, wss)]
            news = [sh * ins[d][5][jj, 4 * d + h:4 * d + h + 1, :] + _dot_tn(ins[d][3][0, rows, sl], vn)
                    for (d, h, jj, rows, sl), sh, vn in zip(items, shs, vns)]
            for (d, h, jj, rows, sl), sh, vn, new in zip(items, shs, vns, news):
                o_r, s_r, vn_r = outs[d]
                st[d, h] = new
                s_r[jj, h] = sh
                vn_r[rows, sl] = vn
                o_r[rows, sl] = _dot(ins[d][2][0, rows, sl], sh) + _dot(ins[d][4][jj, 0, h], vn)

    in_specs, out_specs, out_shape = [], [], []
    for d in range(2):
        in_specs += [tok2[d]] * 4 + [mat[d], cds[d]]
        out_specs += [tok[d], sts[d], tok[d]]
        out_shape += [_sds((s, HW)), _sds((n, NH, DH, DH)), _sds((s, HW))]
    return pl.pallas_call(
        body, name=name, grid=(nb,), in_specs=in_specs, out_specs=out_specs, out_shape=out_shape,
        scratch_shapes=[pltpu.VMEM((2, NH, DH, DH), F32)], compiler_params=_cparams(1),
    )(u, w, qd, kd, at, cd, u, w, qd, kd, at, cd)


def _gdn_seq_bwd(do, w, qd, kd, at, cd, states, vns, name):
    s = do.shape[0]
    n, cb, nb, tok2, tok, mat, cds, sts, dcd = _seq_specs(s, (False, True))

    def body(*refs):
        ins = (refs[0:8], refs[8:16])
        outs = (refs[16:21], refs[21:26])
        dst = refs[26]

        @pl.when(pl.program_id(0) == 0)
        def _():
            dst[...] = jnp.zeros_like(dst)

        for j in range(cb):
            items = []
            for d in range(2):
                jj = cb - 1 - j if d == 0 else j
                items += [(d, h, jj, slice(CHUNK * jj, CHUNK * (jj + 1)), _head(h)) for h in range(NH)]
            dsns = [dst[d, h] for d, h, _, _, _ in items]
            dohs = [ins[d][0][rows, sl] for d, h, jj, rows, sl in items]
            d_vns = [_dot_tn(ins[d][4][jj, 0, h], doh) + _dot(ins[d][3][0, rows, sl], dsn)
                     for (d, h, jj, rows, sl), doh, dsn in zip(items, dohs, dsns)]
            news = [ins[d][5][jj, 4 * d + h:4 * d + h + 1, :] * dsn + _dot_tn(ins[d][2][0, rows, sl], doh)
                    - _dot_tn(ins[d][1][0, rows, sl], d_vn)
                    for (d, h, jj, rows, sl), doh, dsn, d_vn in zip(items, dohs, dsns, d_vns)]
            for (d, h, jj, rows, sl), doh, dsn, d_vn, new in zip(items, dohs, dsns, d_vns, news):
                dvn_r, dkd_r, dqd_r, dw_r, dcd_r = outs[d]
                sh = ins[d][6][jj, h]
                dst[d, h] = new
                dvn_r[rows, sl] = d_vn
                dkd_r[rows, sl] = _dot_nt(ins[d][7][rows, sl], dsn)
                dqd_r[rows, sl] = _dot_nt(doh, sh)
                dw_r[rows, sl] = -_dot_nt(d_vn, sh)
                d_cd = jnp.sum(jnp.sum(sh * dsn, axis=1, keepdims=True), axis=0, keepdims=True)
                dcd_r[jj, h:h + 1, :] = jnp.broadcast_to(d_cd, (1, DH))

    in_specs, out_specs, out_shape, args = [], [], [], []
    for d in range(2):
        in_specs += [tok[d]] + [tok2[d]] * 3 + [mat[d], cds[d], sts[d], tok[d]]
        args += [do, w, qd, kd, at, cd, states[d], vns[d]]
        out_specs += [tok[d]] * 4 + [dcd[d]]
        out_shape += [_sds((s, HW))] * 4 + [_sds((n, NH, DH))]
    return pl.pallas_call(
        body, name=name, grid=(nb,), in_specs=in_specs, out_specs=out_specs, out_shape=out_shape,
        scratch_shapes=[pltpu.VMEM((2, NH, DH, DH), F32)], compiler_params=_cparams(1),
    )(*args)


def _gdn_local_bwd(q, k, v, bg, tmat, do, vns, seq_grads, name, comm=None):
    s = q.shape[0]
    n = s // CHUNK
    cb = min(LOCAL_CB, n)

    def body(*refs):
        q_ref, k_ref, v_ref, bg_ref, t_ref, do_ref = refs[0:6]
        vn_refs = refs[6:8]
        sg = (refs[8:13], refs[13:18])
        dq_ref, dk_ref, dv_ref, dbg_ref = refs[18:]
        lane = lax.broadcasted_iota(jnp.int32, (CHUNK, BAP), 1)
        rowi = lax.broadcasted_iota(jnp.int32, (CHUNK, 1), 0)
        ones = jnp.ones((CHUNK, DH), F32)
        masks = [_GdnMasks(d) for d in range(2)]
        inst = []
        for jj in range(cb):
            rows = slice(CHUNK * jj, CHUNK * (jj + 1))
            bgv = bg_ref[rows, :]
            qs = [q_ref[rows, _head(h)] for h in range(NH)]
            ks = [k_ref[rows, _head(h)] for h in range(NH)]
            kk = [_dot_nt(ks[h], ks[h]) for h in range(NH)]
            q0 = [_dot_nt(qs[h], ks[h]) for h in range(NH)]
            for d, m in enumerate(masks):
                gcs, gcs_t = _chunk_cumsums(m, bgv)
                for h in range(NH):
                    c = _GdnHead(qs[h], ks[h], v_ref[rows, _head(h)], kk[h], q0[h], bgv, gcs, gcs_t, d, h, m)
                    inst.append((jj, rows, d, h, m, c))
        ni = len(inst)
        cs = [it[-1] for it in inst]
        tms = [t_ref[jj, d, h] for jj, _, d, h, _, _ in inst]
        d_vns = [sg[d][0][rows, _head(h)] for _, rows, d, h, _, _ in inst]
        d_ws = [sg[d][3][rows, _head(h)] for _, rows, d, h, _, _ in inst]
        d_ts = [_dot_nt(d_vns[i], cs[i].vb) + _dot_nt(d_ws[i], cs[i].kg) for i in range(ni)]
        xs = [_dot3(tms[i], d_ts[i], _TN) for i in range(ni)]
        d_ls = [jnp.where(inst[i][4].strict, -_dot3(xs[i], tms[i], _NT), 0.0) for i in range(ni)]
        d_attns = [jnp.where(m.incl, _dot_nt(do_ref[rows, _head(h)], vn_refs[d][rows, _head(h)]), 0.0)
                   for _, rows, d, h, m, _ in inst]
        d_vbs = [_dot_tn(tms[i], d_vns[i]) for i in range(ni)]
        d_kgs = [_dot_tn(tms[i], d_ws[i]) for i in range(ni)]
        d_a0s = [d_ls[i] * cs[i].decay for i in range(ni)]
        d_q0s = [d_attns[i] * cs[i].decay for i in range(ni)]
        es = [(d_ls[i] * cs[i].a0 + d_attns[i] * cs[i].q0) * cs[i].decay for i in range(ni)]
        kb_mm = [_dot(d_a0s[i], cs[i].k) for i in range(ni)]
        q_mm = [_dot(d_q0s[i], cs[i].k) for i in range(ni)]
        k_mm = [_dot_tn(d_a0s[i], cs[i].kb) + _dot_tn(d_q0s[i], cs[i].q) for i in range(ni)]
        e_cols = [_dot_exact(ones, es[i], _TN, False)[:, 0:1] for i in range(ni)]
        acc = {}
        d_gcs, d_betas = [], []
        for i, (jj, rows, d, h, m, c) in enumerate(inst):
            sl = _head(h)
            d_kd, d_qd = sg[d][1][rows, sl], sg[d][2][rows, sl]
            d_cd = sg[d][4][jj, h:h + 1, 0:1]
            d_vb, d_kg = d_vbs[i], d_kgs[i]
            d_kb = kb_mm[i] + d_kg * c.eg
            parts = (q_mm[i] + d_qd * c.eg, k_mm[i] + d_kd * c.ek + d_kb * c.beta, d_vb * c.beta)
            acc[jj, h] = [p + a for a, p in zip(acc[jj, h], parts)] if (jj, h) in acc else list(parts)
            s_kd = jnp.sum(d_kd * c.kd, axis=1, keepdims=True)
            d_gc = (jnp.sum(d_kg * c.kg, axis=1, keepdims=True) + jnp.sum(d_qd * c.qd, axis=1, keepdims=True) - s_kd
                    + jnp.sum(es[i], axis=1, keepdims=True) - e_cols[i])
            d_gl = jnp.sum(s_kd, axis=0, keepdims=True) + d_cd * c.cd
            d_gcs.append(d_gc + jnp.where(rowi == m.last, d_gl, 0.0))
            d_betas.append(jnp.sum(d_kb * c.k, axis=1, keepdims=True) + jnp.sum(d_vb * c.v, axis=1, keepdims=True))
        d_gs = [_dot_exact(inst[i][4].tri, d_gcs[i] * ones, _TN, True)[:, 0:1] for i in range(ni)]
        dbg = [jnp.zeros((CHUNK, BAP), F32) for _ in range(cb)]
        for i, (jj, _, d, h, _, _) in enumerate(inst):
            dbg[jj] = dbg[jj] + jnp.where(lane == 4 * d + h, d_betas[i], 0.0) + jnp.where(lane == 8 + 4 * d + h, d_gs[i], 0.0)
        for jj in range(cb):
            rows = slice(CHUNK * jj, CHUNK * (jj + 1))
            for h in range(NH):
                dq_ref[rows, _head(h)], dk_ref[rows, _head(h)], dv_ref[rows, _head(h)] = acc[jj, h]
            dbg_ref[rows, :] = dbg[jj]

    tok = _rows(cb * CHUNK, HW)
    bgs = _rows(cb * CHUNK, BAP)
    mat = pl.BlockSpec((cb, 2, NH, CHUNK, CHUNK), lambda i: (i, 0, 0, 0, 0))
    dcd = pl.BlockSpec((cb, NH, DH), lambda i: (i, 0, 0))
    args = [q, k, v, bg, tmat, do, vns[0], vns[1]]
    in_specs = [tok, tok, tok, bgs, mat, tok, tok, tok]
    for d in range(2):
        args += list(seq_grads[d])
        in_specs += [tok] * 4 + [dcd]
    return _pallas(body, comm, name=name, grid=(n // cb,), in_specs=in_specs, out_specs=[tok, tok, tok, bgs],
                   out_shape=[_sds((s, HW))] * 3 + [_sds((s, BAP))], scratch_shapes=[], args=args)


def _prep_bwd(c_qkv, p_ba, alog_row, dtb_row, dq, dk, dv, dbg, name):
    s = c_qkv.shape[0]
    t = min(256, s)

    def body(cq_ref, pc_ref, alog_ref, dtb_ref, dq_ref, dk_ref, dv_ref, dbg_ref,
             dcq_ref, dpc_ref, dalog_ref, ddtb_ref):
        @pl.when(pl.program_id(0) == 0)
        def _():
            dalog_ref[...] = jnp.zeros_like(dalog_ref)
            ddtb_ref[...] = jnp.zeros_like(ddtb_ref)

        cq = cq_ref[...]
        sq = cq * _sig(cq)
        sg = _silu_grad(cq)
        for h in range(NH):
            sl = slice(DH * h, DH * (h + 1))
            for off, d_ref, scale in ((0, dq_ref, DH ** -0.5), (RGW, dk_ref, 1.0)):
                csl = slice(off + DH * h, off + DH * (h + 1))
                xh = sq[:, csl]
                nrm = lax.rsqrt(jnp.sum(xh * xh, axis=-1, keepdims=True) + EPS)
                y = xh * nrm
                dy = d_ref[:, sl] * scale
                dcq_ref[:, csl] = nrm * (dy - y * jnp.sum(dy * y, axis=-1, keepdims=True)) * sg[:, csl]
        dcq_ref[:, 2 * RGW:] = dv_ref[...] * sg[:, 2 * RGW:]
        pc = pc_ref[...]
        lane = lax.broadcasted_iota(jnp.int32, pc.shape, 1)
        dbg = dbg_ref[...]
        beta = _sig(pc)
        ea = jnp.exp(alog_ref[...])
        z = pc + dtb_ref[...]
        g = -ea * _softplus(z)
        is_g = jnp.logical_and(lane >= 8, lane < 16)
        d_alpha = jnp.where(is_g, dbg * (-ea) * _sig(z), 0.0)
        dpc_ref[...] = jnp.where(lane < 8, dbg * beta * (1.0 - beta), d_alpha)
        dalog_ref[...] += _colsum(jnp.where(is_g, dbg * g, 0.0))
        ddtb_ref[...] += _colsum(d_alpha)

    return pl.pallas_call(
        body, name=name, grid=(s // t,),
        in_specs=[_rows(t, QKVW), _rows(t, BAP), _full((1, BAP)), _full((1, BAP))] + [_rows(t, HW)] * 3 + [_rows(t, BAP)],
        out_specs=[_rows(t, QKVW), _rows(t, BAP), _full((1, BAP)), _full((1, BAP))],
        out_shape=[_sds((s, QKVW)), _sds((s, BAP)), _sds((1, BAP)), _sds((1, BAP))],
        compiler_params=_cparams(1),
    )(c_qkv, p_ba, alog_row, dtb_row, dq, dk, dv, dbg)


def _mix_out_values(hf, hb, gate, of, ob, z, gn):
    hr = hf + hb
    y_rg = hr * _gelu(gate)
    osum = of + ob
    parts = []
    for h in range(NH):
        sl = slice(DH * h, DH * (h + 1))
        oh = osum[:, sl]
        r, ohat = _rms(oh)
        zh = z[:, sl]
        parts.append((r, ohat, zh))
    y_gdn = jnp.concatenate([ohat * gn * (zh * _sig(zh)) for (r, ohat, zh) in parts], axis=1)
    return hr, y_rg, y_gdn, parts


def _outproj(x1, hf, hb, gate, of, ob, z, gn, wout, name):
    s = x1.shape[0]
    t = min(256, s)

    def body(x_ref, hf_ref, hb_ref, gate_ref, of_ref, ob_ref, z_ref, gn_ref, w_ref, xo_ref, y_ref):
        _, y_rg, y_gdn, _ = _mix_out_values(hf_ref[...], hb_ref[...], gate_ref[...], of_ref[...], ob_ref[...],
                                            z_ref[...], gn_ref[...])
        y = jnp.concatenate([y_rg, y_gdn], axis=1).astype(BF16)
        y_ref[...] = y
        xo_ref[...] = x_ref[...] + jnp.dot(y, w_ref[...], preferred_element_type=F32)

    return pl.pallas_call(
        body, name=name, grid=(s // t,),
        in_specs=[_rows(t, D)] + [_rows(t, RGW)] * 6 + [_full((1, DH)), _full((D, D))],
        out_specs=[_rows(t, D), _rows(t, D)], out_shape=[_sds((s, D)), _sds((s, D), BF16)],
        compiler_params=_cparams(1),
    )(x1, hf, hb, gate, of, ob, z, gn, wout)


def _outproj_bwd(dx2, hf, hb, gate, of, ob, z, gn, wout, name):
    s = dx2.shape[0]
    t = min(256, s)

    def body(d_ref, hf_ref, hb_ref, gate_ref, of_ref, ob_ref, z_ref, gn_ref, w_ref,
             dhr_ref, dgate_ref, dos_ref, dz_ref, dgn_ref, db_ref):
        @pl.when(pl.program_id(0) == 0)
        def _():
            dgn_ref[...] = jnp.zeros_like(dgn_ref)

        gate = gate_ref[...]
        gn_v = gn_ref[...]
        hr, _, _, parts = _mix_out_values(hf_ref[...], hb_ref[...], gate, of_ref[...], ob_ref[...], z_ref[...], gn_v)
        dbf = d_ref[...].astype(BF16)
        db_ref[...] = dbf
        dy = _dot_nt(dbf, w_ref[...])
        dyr = dy[:, :RGW]
        dhr_ref[...] = dyr * _gelu(gate)
        dgate_ref[...] = dyr * hr * _gelu_grad(gate)
        dgn = jnp.zeros((1, DH), F32)
        for h, (r, ohat, zh) in enumerate(parts):
            sl = slice(DH * h, DH * (h + 1))
            dyh = dy[:, RGW + DH * h:RGW + DH * (h + 1)]
            sz = zh * _sig(zh)
            dn = dyh * sz
            dz_ref[:, sl] = dyh * ohat * gn_v * _silu_grad(zh)
            dgn = dgn + _colsum(dn * ohat)
            dos_ref[:, sl] = _rms_bwd(dn, ohat, r, gn_v)
        dgn_ref[...] += dgn

    return pl.pallas_call(
        body, name=name, grid=(s // t,),
        in_specs=[_rows(t, D)] + [_rows(t, RGW)] * 6 + [_full((1, DH)), _full((D, D))],
        out_specs=[_rows(t, RGW)] * 4 + [_full((1, DH)), _rows(t, D)],
        out_shape=[_sds((s, RGW))] * 4 + [_sds((1, DH)), _sds((s, D), BF16)],
        compiler_params=_cparams(1),
    )(dx2, hf, hb, gate, of, ob, z, gn, wout)


def _loss_head(x3, target, gain, name):
    s = x3.shape[0]
    t = min(256, s)

    def body(x_ref, t_ref, g_ref, dx_ref, loss_ref, dg_ref):
        @pl.when(pl.program_id(0) == 0)
        def _():
            loss_ref[...] = jnp.zeros_like(loss_ref)
            dg_ref[...] = jnp.zeros_like(dg_ref)

        r, xh = _rms(x_ref[...])
        gv = g_ref[...]
        err = xh * gv - t_ref[...]
        per_tok = jnp.mean(err * err, axis=-1, keepdims=True)
        loss_ref[...] += 0.5 * jnp.sum(per_tok, axis=0, keepdims=True)
        dy = err * (1.0 / D)
        dg_ref[...] += _colsum(dy * xh)
        dx_ref[...] = _rms_bwd(dy, xh, r, gv)

    return pl.pallas_call(
        body, name=name, grid=(s // t,), in_specs=[_rows(t, D), _rows(t, D), _full((1, D))],
        out_specs=[_rows(t, D), _full((8, 128)), _full((1, D))],
        out_shape=[_sds((s, D)), _sds((8, 128)), _sds((1, D))], compiler_params=_cparams(1),
    )(x3, target, gain)


def _adamw_math(wv, gv, mv, vv):
    mn = ADAM_B1 * mv + (1.0 - ADAM_B1) * gv
    vn = ADAM_B2 * vv + (1.0 - ADAM_B2) * (gv * gv)
    m_hat = mn / (1.0 - ADAM_B1 ** ADAM_STEP)
    v_hat = vn / (1.0 - ADAM_B2 ** ADAM_STEP)
    return -ADAM_LR * (m_hat / (jnp.sqrt(v_hat) + ADAM_EPS) + ADAM_WD * wv), mn, vn


def _row_tile(r, c):
    tr = r
    while tr * c * 4 > (1 << 20) and tr % 16 == 0:
        tr //= 2
    return tr


def _adamw(w, g, m, v, name):
    r, c = w.shape
    tr = _row_tile(r, c)

    def body(w_ref, g_ref, m_ref, v_ref, d_ref, nm_ref, nv_ref):
        d_ref[...], nm_ref[...], nv_ref[...] = _adamw_math(w_ref[...], g_ref[...], m_ref[...], v_ref[...])

    return pl.pallas_call(
        body, name=name, grid=(r // tr,), in_specs=[_rows(tr, c)] * 4, out_specs=[_rows(tr, c)] * 3,
        out_shape=[_sds((r, c))] * 3, compiler_params=_cparams(1),
    )(w, g, m, v)


def _adamw_halves(w, own, recv, m, v, c_arr, name):
    r, c = w.shape
    h = r // 2
    tr = _row_tile(h, c)
    nh = h // tr

    def body(c_ref, w_ref, own_ref, recv_ref, m_ref, v_ref, g_ref, d_ref, nm_ref, nv_ref):
        first_half = pl.program_id(0) < nh
        use_own = first_half == (c_ref[0] == 0)
        gv = jnp.where(use_own, own_ref[...], recv_ref[...])
        g_ref[...] = gv
        d_ref[...], nm_ref[...], nv_ref[...] = _adamw_math(w_ref[...], gv, m_ref[...], v_ref[...])

    full = pl.BlockSpec((tr, c), lambda i, c_ref: (i, 0))
    half = pl.BlockSpec((tr, c), lambda i, c_ref: (i % nh, 0))
    return pl.pallas_call(
        body, name=name, out_shape=[_sds((r, c))] * 4,
        grid_spec=pltpu.PrefetchScalarGridSpec(
            num_scalar_prefetch=1, grid=(2 * nh,), in_specs=[full, half, half, full, full], out_specs=[full] * 4),
        compiler_params=_cparams(1),
    )(c_arr, w, own, recv, m, v)


def _mesh_pos():
    return lax.axis_index("x"), lax.axis_index("y"), lax.axis_index("c")


def _other_chips(x, y):
    return [(1 - x, y), (x, 1 - y), (1 - x, 1 - y)]


class _Comm:
    def __init__(self, inputs, out_shapes, scratch, start, finish, space=pltpu.HBM):
        self.inputs, self.out_shapes, self.scratch = list(inputs), list(out_shapes), list(scratch)
        self.start, self.finish, self.space = start, finish, space


def _comm_call(comm, name):
    ni, no = len(comm.inputs), len(comm.out_shapes)

    def body(*refs):
        comm.start(refs[:ni], refs[ni:ni + no], refs[ni + no:])
        comm.finish(refs[:ni], refs[ni:ni + no], refs[ni + no:])

    spec = pl.BlockSpec(memory_space=comm.space)
    return list(pl.pallas_call(body, name=name, out_shape=comm.out_shapes, in_specs=[spec] * ni, out_specs=[spec] * no,
                               scratch_shapes=comm.scratch)(*comm.inputs))


def _pallas(body, comm, *, name, grid, in_specs, out_specs, out_shape, scratch_shapes, args):
    params = _cparams(len(grid))
    if comm is None:
        outs = pl.pallas_call(body, name=name, grid=grid, in_specs=in_specs, out_specs=out_specs, out_shape=out_shape,
                              scratch_shapes=scratch_shapes, compiler_params=params)(*args)
        return list(outs), []
    n_in, n_out, n_sc = len(in_specs), len(out_specs), len(scratch_shapes)
    ci, co = len(comm.inputs), len(comm.out_shapes)

    def carried(*refs):
        bounds = [0, n_in, n_in + ci, n_in + ci + n_out, n_in + ci + n_out + co, n_in + ci + n_out + co + n_sc, len(refs)]
        ins, cins, outs, couts, scr, csems = [refs[lo:hi] for lo, hi in zip(bounds[:-1], bounds[1:])]
        ids = [pl.program_id(k) for k in range(len(grid))]
        first = functools.reduce(jnp.logical_and, [i == 0 for i in ids])
        last = functools.reduce(jnp.logical_and, [i == g - 1 for i, g in zip(ids, grid)])

        @pl.when(first)
        def _():
            comm.start(cins, couts, csems)

        body(*ins, *outs, *scr)

        @pl.when(last)
        def _():
            comm.finish(cins, couts, csems)

    hbm = pl.BlockSpec(memory_space=pltpu.HBM)
    outs = pl.pallas_call(
        carried, name=name, grid=grid, in_specs=list(in_specs) + [hbm] * ci, out_specs=list(out_specs) + [hbm] * co,
        out_shape=list(out_shape) + comm.out_shapes, scratch_shapes=list(scratch_shapes) + comm.scratch,
        compiler_params=params)(*args, *comm.inputs)
    return list(outs[:n_out]), list(outs[n_out:])


def _gather_comm(arrays, space, block_rows):
    n_arr = len(arrays)

    def plan(x_refs, out_refs, sems):
        send_sems, recv_sems, local_sems = sems
        x, y, c = _mesh_pos()
        me, sibling = (x, y, c), (x, y, 1 - c)
        chips = _other_chips(x, y)

        def slot(a, px, py, pc):
            return out_refs[a].at[4 * px + 2 * py + pc]

        def copy(a, k, block, to, src=None):
            return pltpu.make_async_remote_copy(
                src_ref=slot(a, *block) if src is None else src, dst_ref=slot(a, *block),
                send_sem=send_sems.at[7 * a + k], recv_sem=recv_sems.at[7 * a + k], device_id=to, device_id_type=MESH)

        srcs = [x_refs[a] if block_rows[a] is None else
                x_refs[a].at[pl.ds(pl.multiple_of(c * block_rows[a], 16), block_rows[a]), :] for a in range(n_arr)]
        local = [pltpu.make_async_copy(srcs[a], slot(a, *me), local_sems.at[a]) for a in range(n_arr)]
        first = []
        for a in range(n_arr):
            first += [copy(a, 1 + j, me, (*chip, c), src=srcs[a]) for j, chip in enumerate(chips)]
            first.append(copy(a, 0, me, sibling, src=srcs[a]))
        return me, sibling, chips, c, copy, local, first

    def start(x_refs, out_refs, sems):
        _, _, _, _, _, local, first = plan(x_refs, out_refs, sems)
        for cp in local + first:
            cp.start()

    def finish(x_refs, out_refs, sems):
        me, sibling, chips, c, copy, local, first = plan(x_refs, out_refs, sems)
        passed = []
        for j, chip in enumerate(chips):
            for a in range(n_arr):
                copy(a, 1 + j, (*chip, c), me).wait_recv()
                fwd = copy(a, 4 + j, (*chip, c), sibling)
                fwd.start()
                passed.append(fwd)
        for a in range(n_arr):
            copy(a, 0, sibling, me).wait_recv()
            for j, chip in enumerate(chips):
                copy(a, 4 + j, (*chip, 1 - c), me).wait_recv()
        for cp in first + passed:
            cp.wait_send()
        for cp in local:
            cp.wait()

    out_shapes = [_sds((8, w.shape[0] if r is None else r) + w.shape[1:], w.dtype) for w, r in zip(arrays, block_rows)]
    scratch = [pltpu.SemaphoreType.DMA((7 * n_arr,)), pltpu.SemaphoreType.DMA((7 * n_arr,)), pltpu.SemaphoreType.DMA((n_arr,))]
    return _Comm(arrays, out_shapes, scratch, start, finish, space)


def _weights_gather_comm(shards):
    return _gather_comm(shards, pltpu.HBM, [w.shape[0] // 2 for w in shards])


def _all_shards(gathered):
    return [o.reshape(NSH, 2 * o.shape[1], o.shape[2]) for o in gathered]


def _gather_small(block, name):
    return _comm_call(_gather_comm([block], pltpu.VMEM, [None]), name)[0]


def _sibling_exchange(gs, name):
    n = len(gs)
    halves = [g.shape[1] // 2 for g in gs]

    def body(*refs):
        g_refs, land_refs = refs[:n], refs[n:2 * n]
        send_sems, recv_sems = refs[2 * n:]
        x, y, c = _mesh_pos()
        copies = []
        for a in range(n):
            h = halves[a]
            for s in range(NSH):
                copies.append(pltpu.make_async_remote_copy(
                    src_ref=g_refs[a].at[s, pl.ds(pl.multiple_of((1 - c) * h, 8), h), :], dst_ref=land_refs[a].at[s],
                    send_sem=send_sems.at[NSH * a + s], recv_sem=recv_sems.at[NSH * a + s],
                    device_id=(x, y, 1 - c), device_id_type=MESH))
        for cp in copies:
            cp.start()
        for cp in copies:
            cp.wait()

    return pl.pallas_call(
        body, name=name, out_shape=[_sds((NSH, h, g.shape[2])) for h, g in zip(halves, gs)],
        in_specs=[pl.BlockSpec(memory_space=pltpu.HBM)] * n, out_specs=[pl.BlockSpec(memory_space=pltpu.HBM)] * n,
        scratch_shapes=[pltpu.SemaphoreType.DMA((NSH * n,)), pltpu.SemaphoreType.DMA((NSH * n,))],
    )(*gs)


def _chip_sum(g, land, c_arr, name):
    _, h, cols = land.shape

    def body(c_ref, g_ref, l_ref, o_ref):
        o_ref[...] = (g_ref[...] + l_ref[...]).astype(BF16)

    return pl.pallas_call(
        body, name=name, out_shape=_sds((NSH, h, cols), BF16),
        grid_spec=pltpu.PrefetchScalarGridSpec(
            num_scalar_prefetch=1, grid=(NSH,),
            in_specs=[pl.BlockSpec((1, h, cols), lambda s, c_ref: (s, c_ref[0], 0)),
                      pl.BlockSpec((1, h, cols), lambda s, c_ref: (s, 0, 0))],
            out_specs=pl.BlockSpec((1, h, cols), lambda s, c_ref: (s, 0, 0))),
        compiler_params=_cparams(1),
    )(c_arr, g, land)


def _scatter_comm(parts):
    n = len(parts)

    def plan(p_refs, land_refs, sems):
        send_sems, recv_sems, local_sems = sems
        x, y, c = _mesh_pos()
        my_chip = 2 * x + y
        local = [pltpu.make_async_copy(p_refs[a].at[my_chip], land_refs[a].at[my_chip], local_sems.at[a]) for a in range(n)]
        copies = []
        for a in range(n):
            for j, (px, py) in enumerate(_other_chips(x, y)):
                copies.append(pltpu.make_async_remote_copy(
                    src_ref=p_refs[a].at[2 * px + py], dst_ref=land_refs[a].at[my_chip],
                    send_sem=send_sems.at[3 * a + j], recv_sem=recv_sems.at[3 * a + j],
                    device_id=(px, py, c), device_id_type=MESH))
        return local, copies

    def start(p_refs, land_refs, sems):
        local, copies = plan(p_refs, land_refs, sems)
        for cp in local + copies:
            cp.start()

    def finish(p_refs, land_refs, sems):
        local, copies = plan(p_refs, land_refs, sems)
        for cp in copies:
            cp.wait()
        for cp in local:
            cp.wait()

    scratch = [pltpu.SemaphoreType.DMA((3 * n,)), pltpu.SemaphoreType.DMA((3 * n,)), pltpu.SemaphoreType.DMA((n,))]
    return _Comm(parts, [_sds(p.shape, BF16) for p in parts], scratch, start, finish)


def _sum_slots(land, name):
    k, r, c = land.shape
    tr = r // 2 if r % 32 == 0 else r

    def body(l_ref, o_ref):
        acc = l_ref[0].astype(F32)
        for i in range(1, k):
            acc = acc + l_ref[i].astype(F32)
        o_ref[...] = acc

    return pl.pallas_call(
        body, name=name, grid=(r // tr,), in_specs=[pl.BlockSpec((k, tr, c), lambda i: (0, i, 0))],
        out_specs=_rows(tr, c), out_shape=_sds((r, c)), compiler_params=_cparams(1),
    )(land)


def _sibling_swap(halves):
    n = len(halves)

    def body(*refs):
        h_refs, out_refs = refs[:n], refs[n:2 * n]
        send_sems, recv_sems = refs[2 * n:]
        x, y, c = _mesh_pos()
        copies = [pltpu.make_async_remote_copy(
            src_ref=h_refs[a], dst_ref=out_refs[a], send_sem=send_sems.at[a], recv_sem=recv_sems.at[a],
            device_id=(x, y, 1 - c), device_id_type=MESH) for a in range(n)]
        for cp in copies:
            cp.start()
        for cp in copies:
            cp.wait()

    return pl.pallas_call(
        body, name="grad_sibling_swap", out_shape=[_sds(h.shape) for h in halves],
        in_specs=[pl.BlockSpec(memory_space=pltpu.HBM)] * n, out_specs=[pl.BlockSpec(memory_space=pltpu.HBM)] * n,
        scratch_shapes=[pltpu.SemaphoreType.DMA((n,)), pltpu.SemaphoreType.DMA((n,))],
    )(*halves)


def _pad_rows(v, width):
    flat = v.reshape(-1)
    rows = -(-flat.shape[0] // width)
    rows = -(-rows // 8) * 8
    return jnp.pad(flat, (0, rows * width - flat.shape[0])).reshape(rows, width)


def _size(shape):
    n = 1
    for dim in shape:
        n *= dim
    return n


def _row_pack(arrs):
    pieces = []
    for a in arrs:
        rows = -(-a.size // D)
        pieces.append(jnp.pad(a.reshape(-1), (0, rows * D - a.size)).reshape(rows, D))
    total = sum(p.shape[0] for p in pieces)
    if total % 8:
        pieces.append(jnp.zeros((8 - total % 8, D), F32))
    return jnp.concatenate(pieces, axis=0)


def _row_unpack(packed, shapes):
    out, r0 = [], 0
    for shp in shapes:
        n = _size(shp)
        rows = -(-n // D)
        out.append(packed[r0:r0 + rows].reshape(-1)[:n].reshape(shp))
        r0 += rows
    return out


def _block_diag(w):
    eye = jnp.eye(8, dtype=w.dtype)
    return (w[:, :, None, :] * eye[:, None, :, None]).reshape(RGW, RGW)


def _diag_blocks(dense):
    r = dense.reshape(8, 64, 8, 64)
    return jnp.stack([r[n, :, n, :] for n in range(8)])


def _lane_row(v8):
    return jnp.zeros((1, BAP), F32).at[0, 8:16].set(v8.reshape(8))


def _reduce_parts(gs, names, c_arr, tag):
    lands = _sibling_exchange(gs, "grad_sibling_exchange_" + tag)
    return [_chip_sum(g, l, c_arr, "chip_sum_" + n) for g, l, n in zip(gs, lands, names)]


def _local_step(x, target, sw, ffn1_w, later_shards, c_arr):
    (g1, gmix, rg_cw8, rg_cb, wgates, gbias, lam_row, gdn_cw8, alog_row, dtb_row, gn, g2, gfin) = sw
    wg1, wu1, wd1 = ffn1_w

    (x1, a1, b1, fb1), gathered = _ffn_fwd(x, g1, wg1, wu1, wd1, "ffn1_fwd", comm=_weights_gather_comm(later_shards))
    win_sh, wout_sh, wg2, wu2, wd2 = _all_shards(gathered)
    w_in_full = jnp.transpose(win_sh, (1, 0, 2)).reshape(D, NSH * INSH)
    wout = wout_sh.reshape(D, D)
    w_in_groups = (w_in_full[:, 0:512], w_in_full[:, 512:1024], w_in_full[:, 1024:2560], w_in_full[:, 2560:3072],
                   jnp.pad(w_in_full[:, 3072:3088], ((0, 0), (0, BAP - BAW))))
    h2, p_rgx, p_gate, p_qkv, p_z, p_ba = _inproj(x1, gmix, w_in_groups, "in_proj")
    c_rg = _conv(p_rgx, rg_cw8, rg_cb, "rg_conv")
    c_qkv = _conv(p_qkv, gdn_cw8, jnp.zeros((1, QKVW), F32), "gdn_conv")
    a0, bb0, a1s, bb1, q, k, v, bg = _mix_prep(c_rg, c_qkv, p_ba, wgates, gbias, lam_row, alog_row, dtb_row, "mix_prep")
    hf, hb = _scan_pair(a0, bb0, a1s, bb1, False, "rg_scan")
    tmat, gu, gw, gqd, gkd, gat, gcd = _gdn_local_fwd(q, k, v, bg, "gdn_local_fwd")
    of, s0, vn0, ob, s1, vn1 = _gdn_seq_fwd(gu, gw, gqd, gkd, gat, gcd, "gdn_seq_fwd")
    x2, ymix = _outproj(x1, hf, hb, p_gate, of, ob, p_z, gn, wout, "out_proj")
    (x3, a2, b2, fb2), _ = _ffn_fwd(x2, g2, wg2, wu2, wd2, "ffn2_fwd")
    dx3, loss_blk, d_gfin = _loss_head(x3, target, gfin, "loss_head")

    dx2, d_g2, hb2, dob2, dab2, dbb2 = _ffn_bwd(x2, dx3, g2, a2, b2, wg2, wu2, wd2, "ffn2_bwd")
    d_ffn2 = [_tn(dab2, hb2, "ffn2_dwg"), _tn(dbb2, hb2, "ffn2_dwu"), _tn(fb2, dob2, "ffn2_dwd")]
    parts_ffn2 = _reduce_parts(d_ffn2, _BIG_NAMES[5:8], c_arr, "ffn2")

    d_hr, d_gate, d_os, d_z, d_gn, dx2b = _outproj_bwd(dx2, hf, hb, p_gate, of, ob, p_z, gn, wout, "out_proj_bwd")
    d_wout = _tn(ymix, dx2b, "dw_out")[0]

    lam1, lam0 = _scan_pair(a1s, d_hr, a0, d_hr, True, "rg_scan_bwd")
    d_xc, d_pre, xcb, d_gbias, d_lam = _gates_bwd(c_rg, wgates, gbias, lam_row, lam0, lam1, hf, hb, "rg_gates_bwd")
    d_wgates = _tn(xcb, d_pre, "dw_gates")[0]
    d_prgx, d_rgcw8, d_rgcb = _conv_bwd(p_rgx, d_xc, rg_cw8, "rg_conv_bwd")

    sg = _gdn_seq_bwd(d_os, gw, gqd, gkd, gat, gcd, (s0, s1), (vn0, vn1), "gdn_seq_bwd")
    (dq, dk, dv, dbg), lands_ffn2 = _gdn_local_bwd(q, k, v, bg, tmat, d_os, (vn0, vn1), (sg[0:5], sg[5:10]), "gdn_local_bwd",
                                                  comm=_scatter_comm(parts_ffn2))
    d_cqkv, d_pba, d_alog, d_dtb = _prep_bwd(c_qkv, p_ba, alog_row, dtb_row, dq, dk, dv, dbg, "gdn_prep_bwd")
    d_pqkv, d_gdncw8, _ = _conv_bwd(p_qkv, d_cqkv, gdn_cw8, "gdn_conv_bwd")

    dps = (d_prgx, d_gate, d_pqkv, d_z, d_pba)
    dx1, d_gmix = _inproj_bwd(x1, dx2, gmix, dps, w_in_groups, "in_proj_bwd")
    d_win_groups = [_tn(h2, dp, "dw_in_%d" % i)[0] for i, dp in enumerate(dps)]
    d_win = jnp.concatenate(d_win_groups[:4] + [d_win_groups[4][:, :BAW]], axis=1)
    d_mix = [jnp.transpose(d_win.reshape(D, NSH, INSH), (1, 0, 2)), d_wout.reshape(NSH, OUTSH, D)]
    parts_mix = _reduce_parts(d_mix, _BIG_NAMES[3:5], c_arr, "mix")

    gx, d_g1, hb1, dob1, dab1, dbb1 = _ffn_bwd(x, dx1, g1, a1, b1, wg1, wu1, wd1, "ffn1_bwd")
    d_wg1, lands_mix = _tn(dab1, hb1, "ffn1_dwg", comm=_scatter_comm(parts_mix))
    parts_wg1 = _reduce_parts([d_wg1], _BIG_NAMES[0:1], c_arr, "ffn1_gate")
    d_wu1, lands_wg1 = _tn(dbb1, hb1, "ffn1_dwu", comm=_scatter_comm(parts_wg1))
    parts_wu1 = _reduce_parts([d_wu1], _BIG_NAMES[1:2], c_arr, "ffn1_up")
    d_wd1, lands_wu1 = _tn(fb1, dob1, "ffn1_dwd", comm=_scatter_comm(parts_wu1))
    parts_wd1 = _reduce_parts([d_wd1], _BIG_NAMES[2:3], c_arr, "ffn1_down")
    lands_ffn1 = lands_wg1 + lands_wu1 + _comm_call(_scatter_comm(parts_wd1), "grad_chip_scatter_ffn1_down")

    halves = [_sum_slots(l, "sum_chips_" + n) for l, n in zip(lands_ffn1 + lands_mix + lands_ffn2, _BIG_NAMES)]
    small = dict(
        ffn1_norm=d_g1, mix_norm=d_gmix, rg_conv_w=d_rgcw8[:4], rg_conv_b=d_rgcb,
        rg_gate_a_w=jnp.stack([_diag_blocks(d_wgates[:, RGW * i:RGW * (i + 1)]) for i in (0, 1)]),
        rg_gate_x_w=jnp.stack([_diag_blocks(d_wgates[:, RGW * i:RGW * (i + 1)]) for i in (2, 3)]),
        rg_gate_a_b=d_gbias[0, :2 * RGW].reshape(2, RGW), rg_gate_x_b=d_gbias[0, 2 * RGW:].reshape(2, RGW),
        rg_lambda=d_lam.reshape(2, RGW), gdn_conv_w=d_gdncw8[:4],
        gdn_a_log=d_alog[0, 8:16].reshape(2, NH), gdn_dt_bias=d_dtb[0, 8:16].reshape(2, NH),
        gdn_norm=d_gn, ffn2_norm=d_g2, final_norm=d_gfin)
    return loss_blk, gx, halves, small


_SMALL_NAMES = ("ffn1_norm", "mix_norm", "rg_conv_w", "rg_conv_b", "rg_gate_a_w", "rg_gate_a_b", "rg_gate_x_w",
                "rg_gate_x_b", "rg_lambda", "gdn_conv_w", "gdn_a_log", "gdn_dt_bias", "gdn_norm", "ffn2_norm", "final_norm")
_SMALL_SHARDED = dict(rg_conv_w=128, rg_gate_a_b=128, rg_gate_x_b=128, rg_lambda=128, gdn_conv_w=384)
_OUT_ORDER = ("ffn1_norm", "ffn1_w_gate", "ffn1_w_up", "ffn1_w_down", "mix_norm", "w_in", "w_out", "rg_conv_w", "rg_conv_b",
              "rg_gate_a_w", "rg_gate_a_b", "rg_gate_x_w", "rg_gate_x_b", "rg_lambda", "gdn_conv_w", "gdn_a_log",
              "gdn_dt_bias", "gdn_norm", "ffn2_norm", "ffn2_w_gate", "ffn2_w_up", "ffn2_w_down", "final_norm")
_BIG_NAMES = ("ffn1_w_gate", "ffn1_w_up", "ffn1_w_down", "w_in", "w_out", "ffn2_w_gate", "ffn2_w_up", "ffn2_w_down")
_TRANSPOSED = ("ffn1_w_gate", "ffn1_w_up", "ffn2_w_gate", "ffn2_w_up")


def kernel(x, ffn1_norm, ffn1_w_gate, ffn1_w_up, ffn1_w_down, mix_norm, w_in, w_out, rg_conv_w, rg_conv_b, rg_gate_a_w, rg_gate_a_b, rg_gate_x_w, rg_gate_x_b, rg_lambda, gdn_conv_w, gdn_a_log, gdn_dt_bias, gdn_norm, ffn2_norm, ffn2_w_gate, ffn2_w_up, ffn2_w_down, final_norm, loss_target, m_ffn1_norm, m_ffn1_w_gate, m_ffn1_w_up, m_ffn1_w_down, m_mix_norm, m_w_in, m_w_out, m_rg_conv_w, m_rg_conv_b, m_rg_gate_a_w, m_rg_gate_a_b, m_rg_gate_x_w, m_rg_gate_x_b, m_rg_lambda, m_gdn_conv_w, m_gdn_a_log, m_gdn_dt_bias, m_gdn_norm, m_ffn2_norm, m_ffn2_w_gate, m_ffn2_w_up, m_ffn2_w_down, m_final_norm, v_ffn1_norm, v_ffn1_w_gate, v_ffn1_w_up, v_ffn1_w_down, v_mix_norm, v_w_in, v_w_out, v_rg_conv_w, v_rg_conv_b, v_rg_gate_a_w, v_rg_gate_a_b, v_rg_gate_x_w, v_rg_gate_x_b, v_rg_lambda, v_gdn_conv_w, v_gdn_a_log, v_gdn_dt_bias, v_gdn_norm, v_ffn2_norm, v_ffn2_w_gate, v_ffn2_w_up, v_ffn2_w_down, v_final_norm):
    args = dict(locals())
    w = {n: args[n] for n in _OUT_ORDER}
    mom = {n: args["m_" + n] for n in _OUT_ORDER}
    var = {n: args["v_" + n] for n in _OUT_ORDER}
    xi, yi, ci = _mesh_pos()
    shard = 2 * xi + yi

    big_bf16 = [w[n][0].astype(BF16) for n in _BIG_NAMES]
    ffn1_w = _all_shards(_comm_call(_weights_gather_comm(big_bf16[0:3]), "gather_ffn1_weights"))
    sm_local = _pad_rows(jnp.concatenate([w[n][0].reshape(-1) for n in _SMALL_SHARDED]), 128)
    sm_all = _gather_small(sm_local, "gather_small_weights")[0::2].reshape(NSH, -1)
    sm_full, off = {}, 0
    for n, wd_ in _SMALL_SHARDED.items():
        rows = w[n].shape[1]
        piece = sm_all[:, off:off + rows * wd_].reshape(NSH, rows, wd_)
        sm_full[n] = jnp.transpose(piece, (1, 0, 2)).reshape(rows, NSH * wd_)
        off += rows * wd_

    wa, wx = rg_gate_a_w[0], rg_gate_x_w[0]
    wgates = jnp.concatenate([_block_diag(wa[0]), _block_diag(wa[1]), _block_diag(wx[0]), _block_diag(wx[1])],
                             axis=1).astype(BF16)
    gbias = jnp.concatenate([sm_full["rg_gate_a_b"].reshape(1, -1), sm_full["rg_gate_x_b"].reshape(1, -1)], axis=1)
    sw = (ffn1_norm, mix_norm, jnp.pad(sm_full["rg_conv_w"], ((0, 4), (0, 0))), rg_conv_b, wgates, gbias,
          sm_full["rg_lambda"].reshape(1, -1), jnp.pad(sm_full["gdn_conv_w"], ((0, 4), (0, 0))), _lane_row(gdn_a_log),
          _lane_row(gdn_dt_bias), gdn_norm, ffn2_norm, final_norm.reshape(1, D))
    c_arr = ci.reshape(1).astype(jnp.int32)

    loss_blk, gx, halves, small = _local_step(x[0], loss_target[0], sw, ffn1_w, big_bf16[3:], c_arr)
    loss = lax.psum(loss_blk[0, 0], ("x", "y", "c"))
    grads = {}

    sm_grad = _row_pack([small[n] for n in _SMALL_NAMES])
    sm_sum = _sum_slots(_gather_small(sm_grad, "gather_small_grads"), "small_grad_sum")
    for n, g in zip(_SMALL_NAMES, _row_unpack(sm_sum, [small[n].shape for n in _SMALL_NAMES])):
        if n in _SMALL_SHARDED:
            wd_ = _SMALL_SHARDED[n]
            g = lax.dynamic_slice_in_dim(g, shard * wd_, wd_, axis=1)
        grads[n] = g.reshape(w[n].shape)

    delta, new_m, new_v = {}, {}, {}
    for n, own, recv in zip(_BIG_NAMES, halves, _sibling_swap(halves)):
        to2d = jnp.transpose if n in _TRANSPOSED else (lambda t: t)
        outs4 = _adamw_halves(to2d(w[n][0]), own, recv, to2d(mom[n][0]), to2d(var[n][0]), c_arr, "adamw_" + n)
        grads[n], delta[n], new_m[n], new_v[n] = [to2d(o)[None] for o in outs4]
    packs = [_row_pack([t[n] for n in _SMALL_NAMES]) for t in (w, grads, mom, var)]
    sm_shapes = [w[n].shape for n in _SMALL_NAMES]
    for dst, src in zip((delta, new_m, new_v), _adamw(*packs, "adamw_small")):
        for n, val in zip(_SMALL_NAMES, _row_unpack(src, sm_shapes)):
            dst[n] = val

    outs = [loss, gx[None]]
    for group in (grads, delta, new_m, new_v):
        outs += [group[n] for n in _OUT_ORDER]
    return tuple(outs)
```

```python
import functools

import jax
import jax.numpy as jnp
from jax import lax
from jax.experimental import pallas as pl
from jax.experimental.pallas import tpu as pltpu

F32 = jnp.float32
BF16 = jnp.bfloat16
EPS = 1e-6
D = 1024
NSH = 4
FSH = 704
RGW = 512
QKVW = 1536
ZW = 512
BAW = 16
BAP = 128
INSH = 772
OUTSH = 256
CHUNK = 64
NH = 4
DH = 128
RG_C = 8.0
VMEM_LIMIT = 52 * 1024 * 1024
MESH = pl.DeviceIdType.MESH

ADAM_LR = 0.001
ADAM_B1 = 0.9
ADAM_B2 = 0.999
ADAM_EPS = 1e-08
ADAM_WD = 0.01
ADAM_STEP = 10


def _cparams(n_grid):
    return pltpu.CompilerParams(dimension_semantics=("arbitrary",) * n_grid, vmem_limit_bytes=VMEM_LIMIT)


def _sig(x):
    return 0.5 + 0.5 * jnp.tanh(0.5 * x)


def _sig_pos(x):
    return 1.0 / (1.0 + jnp.exp(-x))


def _softplus(x):
    return jnp.maximum(x, 0.0) + jnp.log(1.0 + jnp.exp(-jnp.abs(x)))


def _neg_expm1(y):
    series = -y * (1.0 + y * (0.5 + y * (1.0 / 6 + y * (1.0 / 24 + y * (1.0 / 120 + y * (1.0 / 720 + y / 5040))))))
    return jnp.where(y > -0.3, series, 1.0 - jnp.exp(y))


_GELU_C = 0.7978845608028654


def _gelu(x):
    t = jnp.tanh(_GELU_C * (x + 0.044715 * x * x * x))
    return 0.5 * x * (1.0 + t)


def _gelu_grad(x):
    t = jnp.tanh(_GELU_C * (x + 0.044715 * x * x * x))
    return 0.5 * (1.0 + t) + 0.5 * x * (1.0 - t * t) * _GELU_C * (1.0 + 3 * 0.044715 * x * x)


def _silu_grad(x):
    s = _sig(x)
    return s * (1.0 + x * (1.0 - s))


def _dot(a, b):
    return jnp.dot(a.astype(BF16), b.astype(BF16), preferred_element_type=F32)


def _dot_nt(a, b):
    return lax.dot_general(a.astype(BF16), b.astype(BF16), (((1,), (1,)), ((), ())), preferred_element_type=F32)


def _dot_tn(a, b):
    return lax.dot_general(a.astype(BF16), b.astype(BF16), (((0,), (0,)), ((), ())), preferred_element_type=F32)


_NN = ((1,), (0,))
_NT = ((1,), (1,))
_TN = ((0,), (0,))


def _dg(a, b, dims):
    return lax.dot_general(a, b, (dims, ((), ())), preferred_element_type=F32)


def _split2(a):
    hi = a.astype(BF16)
    return hi, (a - hi.astype(F32)).astype(BF16)


def _dot3(a, b, dims=_NN):
    ah, al = _split2(a)
    bh, bl = _split2(b)
    return _dg(ah, bh, dims) + _dg(ah, bl, dims) + _dg(al, bh, dims)


def _dot_exact(e, x, dims, e_is_lhs):
    x0 = x.astype(BF16)
    r = x - x0.astype(F32)
    x1 = r.astype(BF16)
    x2 = (r - x1.astype(F32)).astype(BF16)
    eb = e.astype(BF16)
    if e_is_lhs:
        return _dg(eb, x0, dims) + _dg(eb, x1, dims) + _dg(eb, x2, dims)
    return _dg(x0, eb, dims) + _dg(x1, eb, dims) + _dg(x2, eb, dims)


def _rms(xv):
    r = lax.rsqrt(jnp.mean(xv * xv, axis=-1, keepdims=True) + EPS)
    return r, xv * r


def _rms_bwd(dy, xh, r, gain):
    dxh = dy * gain
    return r * (dxh - xh * jnp.mean(dxh * xh, axis=-1, keepdims=True))


def _colsum(v):
    return jnp.sum(v, axis=0, keepdims=True)


def _rows(t, c):
    return pl.BlockSpec((t, c), lambda i: (i, 0))


def _full(shape):
    n = len(shape)
    return pl.BlockSpec(shape, lambda i: (0,) * n)


def _sds(shape, dtype=F32):
    return jax.ShapeDtypeStruct(shape, dtype)


def _ffn_fwd(x, gain, wg, wu, wd, name, comm=None):
    s = x.shape[0]
    tm = min(512, s)

    def body(x_ref, g_ref, wg_ref, wu_ref, wd_ref, xo_ref, ga_ref, gb_ref, f_ref, h_sc, acc):
        j = pl.program_id(1)

        @pl.when(j == 0)
        def _():
            _, xh = _rms(x_ref[...])
            h_sc[...] = (xh * g_ref[...]).astype(BF16)
            acc[...] = jnp.zeros_like(acc)

        h = h_sc[...]

        a = jnp.dot(h, wg_ref[0], preferred_element_type=F32)
        b = jnp.dot(h, wu_ref[0], preferred_element_type=F32)
        sa = _sig(a)
        silu = a * sa
        fv = silu * b
        f = fv.astype(BF16)
        f_ref[0] = f
        ga_ref[0] = (sa * b + fv * (1.0 - sa)).astype(BF16)
        gb_ref[0] = silu.astype(BF16)
        acc[...] += jnp.dot(f, wd_ref[0], preferred_element_type=F32)

        @pl.when(j == NSH - 1)
        def _():
            xo_ref[...] = x_ref[...] + 0.5 * acc[...]

    return _pallas(
        body, comm, name=name, grid=(s // tm, NSH),
        in_specs=[pl.BlockSpec((tm, D), lambda i, j: (i, 0)), pl.BlockSpec((1, D), lambda i, j: (0, 0)),
                  pl.BlockSpec((1, D, FSH), lambda i, j: (j, 0, 0)), pl.BlockSpec((1, D, FSH), lambda i, j: (j, 0, 0)),
                  pl.BlockSpec((1, FSH, D), lambda i, j: (j, 0, 0))],
        out_specs=[pl.BlockSpec((tm, D), lambda i, j: (i, 0))] + [pl.BlockSpec((1, tm, FSH), lambda i, j: (j, i, 0))] * 3,
        out_shape=[_sds((s, D))] + [_sds((NSH, s, FSH), BF16)] * 3,
        scratch_shapes=[pltpu.VMEM((tm, D), BF16), pltpu.VMEM((tm, D), F32)],
        args=(x, gain, wg, wu, wd))


def _ffn_bwd(x, dout, gain, ga, gb, wg, wu, wd, name):
    s = x.shape[0]
    tm = min(512, s)

    def hidden(d_ref, ga_ref, gb_ref, wd_ref, do_ref, da_ref, db_ref, do_sc):
        @pl.when(pl.program_id(1) == 0)
        def _():
            do = (0.5 * d_ref[...]).astype(BF16)
            do_sc[...] = do
            do_ref[...] = do

        df = _dot_nt(do_sc[...], wd_ref[0])
        da_ref[0] = (df * ga_ref[0].astype(F32)).astype(BF16)
        db_ref[0] = (df * gb_ref[0].astype(F32)).astype(BF16)

    th = min(1024, s)
    tok = pl.BlockSpec((th, D), lambda i, j: (i, 0))
    sh = pl.BlockSpec((1, th, FSH), lambda i, j: (j, i, 0))
    do, da, db = pl.pallas_call(
        hidden, name=name + "_hidden", grid=(s // th, NSH),
        in_specs=[tok, sh, sh, pl.BlockSpec((1, FSH, D), lambda i, j: (j, 0, 0))], out_specs=[tok, sh, sh],
        out_shape=[_sds((s, D), BF16)] + [_sds((NSH, s, FSH), BF16)] * 2,
        scratch_shapes=[pltpu.VMEM((th, D), BF16)], compiler_params=_cparams(2),
    )(dout, ga, gb, wd)

    def inputs(x_ref, d_ref, g_ref, da_ref, db_ref, wg_ref, wu_ref, dx_ref, dg_ref, h_ref):
        @pl.when(pl.program_id(0) == 0)
        def _():
            dg_ref[...] = jnp.zeros_like(dg_ref)

        dh = jnp.zeros((tm, D), F32)
        for j in range(NSH):
            dh = dh + _dot_nt(da_ref[j], wg_ref[j]) + _dot_nt(db_ref[j], wu_ref[j])
        r, xh = _rms(x_ref[...])
        gv = g_ref[...]
        h_ref[...] = (xh * gv).astype(BF16)
        dg_ref[...] += _colsum(dh * xh)
        dx_ref[...] = d_ref[...] + _rms_bwd(dh, xh, r, gv)

    grads = pl.BlockSpec((NSH, tm, FSH), lambda i: (0, i, 0))
    resident = pl.BlockSpec((NSH, D, FSH), lambda i: (0, 0, 0), pipeline_mode=pl.Buffered(1))
    dx, dg, h = pl.pallas_call(
        inputs, name=name + "_input", grid=(s // tm,),
        in_specs=[_rows(tm, D), _rows(tm, D), _full((1, D)), grads, grads, resident, resident],
        out_specs=[_rows(tm, D), _full((1, D)), _rows(tm, D)],
        out_shape=[_sds((s, D)), _sds((1, D)), _sds((s, D), BF16)], compiler_params=_cparams(1),
    )(x, dout, gain, da, db, wg, wu)
    return dx, dg, h, do, da, db


def _tn(a, b, name, comm=None):
    a_g = a.ndim == 3
    b_g = b.ndim == 3
    g = a.shape[0] if a_g else (b.shape[0] if b_g else 1)
    s, k = a.shape[-2:]
    n = b.shape[-1]
    ts = min(1024, s)

    def body(a_ref, b_ref, o_ref):
        @pl.when(pl.program_id(1) == 0)
        def _():
            o_ref[...] = jnp.zeros_like(o_ref)

        av = a_ref[0] if a_g else a_ref[...]
        bv = b_ref[0] if b_g else b_ref[...]
        o_ref[0] += _dot_tn(av, bv)

    a_spec = pl.BlockSpec((1, ts, k), lambda gi, si: (gi, si, 0)) if a_g else pl.BlockSpec((ts, k), lambda gi, si: (si, 0))
    b_spec = pl.BlockSpec((1, ts, n), lambda gi, si: (gi, si, 0)) if b_g else pl.BlockSpec((ts, n), lambda gi, si: (si, 0))
    outs, carried = _pallas(body, comm, name=name, grid=(g, s // ts), in_specs=[a_spec, b_spec],
                            out_specs=[pl.BlockSpec((1, k, n), lambda gi, si: (gi, 0, 0))], out_shape=[_sds((g, k, n))],
                            scratch_shapes=[], args=(a, b))
    return outs[0] if comm is None else (outs[0], carried)


_P_WIDTHS = (RGW, RGW, QKVW, ZW, BAP)


def _inproj(x1, gain, ws, name):
    s = x1.shape[0]
    tm = min(256, s)

    def body(x_ref, g_ref, *refs):
        w_refs = refs[:5]
        h_ref = refs[5]
        p_refs = refs[6:]
        _, xh = _rms(x_ref[...])
        h = (xh * g_ref[...]).astype(BF16)
        h_ref[...] = h
        for w_ref, p_ref in zip(w_refs, p_refs):
            p_ref[...] = jnp.dot(h, w_ref[...], preferred_element_type=F32)

    return pl.pallas_call(
        body, name=name, grid=(s // tm,),
        in_specs=[_rows(tm, D), _full((1, D))] + [_full((D, w)) for w in _P_WIDTHS],
        out_specs=[_rows(tm, D)] + [_rows(tm, w) for w in _P_WIDTHS],
        out_shape=[_sds((s, D), BF16)] + [_sds((s, w)) for w in _P_WIDTHS],
        compiler_params=_cparams(1),
    )(x1, gain, *ws)


def _inproj_bwd(x1, dx2, gain, dps, ws, name):
    s = x1.shape[0]
    tm = min(256, s)

    def body(x_ref, d_ref, g_ref, *refs):
        dp_refs = refs[:5]
        w_refs = refs[5:10]
        dx_ref, dg_ref = refs[10:]

        @pl.when(pl.program_id(0) == 0)
        def _():
            dg_ref[...] = jnp.zeros_like(dg_ref)

        dh = jnp.zeros((tm, D), F32)
        for dp_ref, w_ref in zip(dp_refs, w_refs):
            dh = dh + _dot_nt(dp_ref[...], w_ref[...])
        r, xh = _rms(x_ref[...])
        dg_ref[...] += _colsum(dh * xh)
        dx_ref[...] = d_ref[...] + _rms_bwd(dh, xh, r, g_ref[...])

    return pl.pallas_call(
        body, name=name, grid=(s // tm,),
        in_specs=[_rows(tm, D), _rows(tm, D), _full((1, D))] + [_rows(tm, w) for w in _P_WIDTHS]
        + [_full((D, w)) for w in _P_WIDTHS],
        out_specs=[_rows(tm, D), _full((1, D))],
        out_shape=[_sds((s, D)), _sds((1, D))],
        compiler_params=_cparams(1),
    )(x1, dx2, gain, *dps, *ws)


def _halo_specs(s, t, c):
    nb8 = s // 8
    tb = t // 8
    prev = pl.BlockSpec((8, c), lambda i: (jnp.maximum(i * tb - 1, 0), 0))
    nxt = pl.BlockSpec((8, c), lambda i: (jnp.minimum((i + 1) * tb, nb8 - 1), 0))
    return prev, nxt


def _edge_masks(nb):
    i = pl.program_id(0)
    return jnp.where(i > 0, 1.0, 0.0).astype(F32), jnp.where(i < nb - 1, 1.0, 0.0).astype(F32)


def _shifted(xx, off, t):
    n = t + 16
    sh = (-off) % n
    rolled = xx if sh == 0 else pltpu.roll(xx, sh, 0)
    return rolled[8:8 + t]


def _conv(x, w8, bias, name):
    s, c = x.shape
    t = min(256, s)
    nb = s // t

    def body(x_ref, xp_ref, xn_ref, w_ref, b_ref, o_ref):
        pm, nm = _edge_masks(nb)
        for c0 in range(0, c, 512):
            cols = slice(c0, c0 + 512)
            xx = jnp.concatenate([xp_ref[:, cols] * pm, x_ref[:, cols], xn_ref[:, cols] * nm], axis=0)
            acc = jnp.zeros((t, 512), F32) + b_ref[:, cols]
            for j in range(4):
                acc = acc + w_ref[j:j + 1, cols] * _shifted(xx, j - 2, t)
            o_ref[:, cols] = acc

    prev, nxt = _halo_specs(s, t, c)
    return pl.pallas_call(
        body, name=name, grid=(nb,),
        in_specs=[_rows(t, c), prev, nxt, _full((8, c)), _full((1, c))],
        out_specs=_rows(t, c), out_shape=_sds((s, c)), compiler_params=_cparams(1),
    )(x, x, x, w8, bias)


def _conv_bwd(x, dc, w8, name):
    s, c = x.shape
    t = min(256, s)
    nb = s // t

    def body(x_ref, d_ref, dp_ref, dn_ref, w_ref, dx_ref, dw_ref, db_ref):
        @pl.when(pl.program_id(0) == 0)
        def _():
            dw_ref[...] = jnp.zeros_like(dw_ref)
            db_ref[...] = jnp.zeros_like(db_ref)

        pm, nm = _edge_masks(nb)
        for c0 in range(0, c, 512):
            cols = slice(c0, c0 + 512)
            dd = jnp.concatenate([dp_ref[:, cols] * pm, d_ref[:, cols], dn_ref[:, cols] * nm], axis=0)
            xv = x_ref[:, cols]
            acc = jnp.zeros((t, 512), F32)
            for j in range(4):
                dsh = _shifted(dd, 2 - j, t)
                acc = acc + w_ref[j:j + 1, cols] * dsh
                dw_ref[j:j + 1, cols] += _colsum(dsh * xv)
            dx_ref[:, cols] = acc
            db_ref[:, cols] += _colsum(d_ref[:, cols])

    prev, nxt = _halo_specs(s, t, c)
    return pl.pallas_call(
        body, name=name, grid=(nb,),
        in_specs=[_rows(t, c), _rows(t, c), prev, nxt, _full((8, c))],
        out_specs=[_rows(t, c), _full((8, c)), _full((1, c))],
        out_shape=[_sds((s, c)), _sds((8, c)), _sds((1, c))], compiler_params=_cparams(1),
    )(x, dc, dc, dc, w8)


def _rg_gates(xc, pre, lam_row):
    sp8 = RG_C * _softplus(-lam_row)
    out = []
    for d in range(2):
        r = _sig_pos(pre[:, RGW * d:RGW * (d + 1)])
        gi = _sig(pre[:, 2 * RGW + RGW * d:2 * RGW + RGW * (d + 1)])
        la = -r * sp8[:, RGW * d:RGW * (d + 1)]
        a = jnp.exp(la)
        mult = jnp.sqrt(_neg_expm1(2.0 * la))
        out.append((r, gi, a, mult))
    return out


def _mix_prep(c_rg, c_qkv, p_ba, wgates, gbias, lam_row, alog_row, dtb_row, name):
    s = c_rg.shape[0]
    t = min(256, s)

    def body(xc_ref, cq_ref, pc_ref, wg_ref, gb_ref, lam_ref, alog_ref, dtb_ref,
             a0_ref, b0_ref, a1_ref, b1_ref, q_ref, k_ref, v_ref, bg_ref):
        xc = xc_ref[...]
        pre = _dot(xc, wg_ref[...]) + gb_ref[...]
        gates = _rg_gates(xc, pre, lam_ref[...])
        for (r, gi, a, mult), a_ref, b_ref in zip(gates, (a0_ref, a1_ref), (b0_ref, b1_ref)):
            a_ref[...] = a
            b_ref[...] = mult * gi * xc
        cq = cq_ref[...]
        sq = cq * _sig(cq)
        for h in range(NH):
            sl = slice(DH * h, DH * (h + 1))
            qh = sq[:, sl]
            q_ref[:, sl] = qh * lax.rsqrt(jnp.sum(qh * qh, axis=-1, keepdims=True) + EPS) * (DH ** -0.5)
            kh = sq[:, RGW + DH * h:RGW + DH * (h + 1)]
            k_ref[:, sl] = kh * lax.rsqrt(jnp.sum(kh * kh, axis=-1, keepdims=True) + EPS)
        v_ref[...] = sq[:, 2 * RGW:]
        pc = pc_ref[...]
        lane = lax.broadcasted_iota(jnp.int32, pc.shape, 1)
        beta = _sig(pc)
        g = -jnp.exp(alog_ref[...]) * _softplus(pc + dtb_ref[...])
        bg_ref[...] = jnp.where(lane < 8, beta, jnp.where(lane < 16, g, 0.0))

    return pl.pallas_call(
        body, name=name, grid=(s // t,),
        in_specs=[_rows(t, RGW), _rows(t, QKVW), _rows(t, BAP), _full((RGW, 4 * RGW)), _full((1, 4 * RGW)),
                  _full((1, 2 * RGW)), _full((1, BAP)), _full((1, BAP))],
        out_specs=[_rows(t, RGW)] * 7 + [_rows(t, BAP)],
        out_shape=[_sds((s, RGW))] * 7 + [_sds((s, BAP))],
        compiler_params=_cparams(1),
    )(c_rg, c_qkv, p_ba, wgates, gbias, lam_row, alog_row, dtb_row)


def _scan_pair(af, bf, ar, br, shifted, name):
    s, c = af.shape
    t = min(512, s)
    nb = s // t
    ng = t // 8
    tb = t // 8
    up = lambda i: (i, 0)
    down = lambda i: (nb - 1 - i, 0)

    def body(*refs):
        if shifted:
            af_ref, bf_ref, ar_ref, br_ref, afp_ref, arn_ref, hf_ref, hr_ref, carry, fbuf, rbuf = refs
        else:
            af_ref, bf_ref, ar_ref, br_ref, hf_ref, hr_ref, carry = refs
        i = pl.program_id(0)

        @pl.when(i == 0)
        def _():
            carry[...] = jnp.zeros_like(carry)

        if shifted:
            edge = jnp.where(i > 0, 1.0, 0.0).astype(F32)
            fbuf[0:8, :] = afp_ref[...] * edge
            fbuf[8:t + 8, :] = af_ref[...]
            rbuf[0:t, :] = ar_ref[...]
            rbuf[t:t + 8, :] = arn_ref[...] * edge
        row = lax.broadcasted_iota(jnp.int32, (8, c), 0)

        def block_scan(av, bv, downwards):
            for k in (1, 2, 4):
                sh = (8 - k) if downwards else k
                m = (row < 8 - k) if downwards else (row >= k)
                a_s = pltpu.roll(av, sh, 0)
                b_s = pltpu.roll(bv, sh, 0)
                bv = jnp.where(m, av * b_s + bv, bv)
                av = jnp.where(m, av * a_s, av)
            return av, bv

        def group(gi, cvs):
            cf, cr = cvs
            rf = pl.multiple_of(gi * 8, 8)
            rr = pl.multiple_of((ng - 1 - gi) * 8, 8)
            if shifted:
                a_f = jnp.where(row > 0, pltpu.roll(fbuf[pl.ds(rf + 8, 8), :], 1, 0), pltpu.roll(fbuf[pl.ds(rf, 8), :], 1, 0))
                a_r = jnp.where(row < 7, pltpu.roll(rbuf[pl.ds(rr, 8), :], 7, 0), pltpu.roll(rbuf[pl.ds(rr + 8, 8), :], 7, 0))
            else:
                a_f = af_ref[pl.ds(rf, 8), :]
                a_r = ar_ref[pl.ds(rr, 8), :]
            a_f, b_f = block_scan(a_f, bf_ref[pl.ds(rf, 8), :], False)
            a_r, b_r = block_scan(a_r, br_ref[pl.ds(rr, 8), :], True)
            h_f = a_f * cf + b_f
            h_r = a_r * cr + b_r
            hf_ref[pl.ds(rf, 8), :] = h_f
            hr_ref[pl.ds(rr, 8), :] = h_r
            return h_f[7:8, :], h_r[0:1, :]

        cf, cr = lax.fori_loop(0, ng, group, (carry[0:1, :], carry[8:9, :]))
        carry[0:1, :] = cf
        carry[8:9, :] = cr

    in_specs = [pl.BlockSpec((t, c), up), pl.BlockSpec((t, c), up), pl.BlockSpec((t, c), down), pl.BlockSpec((t, c), down)]
    args = [af, bf, ar, br]
    scratch = [pltpu.VMEM((16, c), F32)]
    if shifted:
        in_specs += [pl.BlockSpec((8, c), lambda i: (jnp.maximum(i * tb - 1, 0), 0)),
                     pl.BlockSpec((8, c), lambda i: (jnp.minimum((nb - i) * tb, s // 8 - 1), 0))]
        args += [af, ar]
        scratch += [pltpu.VMEM((t + 8, c), F32), pltpu.VMEM((t + 8, c), F32)]
    return pl.pallas_call(
        body, name=name, grid=(nb,), in_specs=in_specs,
        out_specs=[pl.BlockSpec((t, c), up), pl.BlockSpec((t, c), down)], out_shape=[_sds((s, c)), _sds((s, c))],
        scratch_shapes=scratch, compiler_params=_cparams(1),
    )(*args)


def _gates_bwd(xc, wgates, gbias, lam_row, lam0, lam1, hf, hb, name):
    s = xc.shape[0]
    t = min(256, s)
    nb = s // t

    def body(xc_ref, wg_ref, gb_ref, lam_ref, l0_ref, l1_ref, hf_ref, hfp_ref, hfn_ref, hb_ref, hbp_ref, hbn_ref,
             dxc_ref, dpre_ref, xcb_ref, dgb_ref, dlam_ref):
        @pl.when(pl.program_id(0) == 0)
        def _():
            dgb_ref[...] = jnp.zeros_like(dgb_ref)
            dlam_ref[...] = jnp.zeros_like(dlam_ref)

        pm, nm = _edge_masks(nb)
        h_prev = _shifted(jnp.concatenate([hfp_ref[...] * pm, hf_ref[...], hfn_ref[...] * nm], axis=0), -1, t)
        h_next = _shifted(jnp.concatenate([hbp_ref[...] * pm, hb_ref[...], hbn_ref[...] * nm], axis=0), 1, t)
        h_shift = (h_prev, h_next)
        xv = xc_ref[...]
        pre = _dot(xv, wg_ref[...]) + gb_ref[...]
        lam_row_v = lam_ref[...]
        sp8 = RG_C * _softplus(-lam_row_v)
        dsp_dlam = -RG_C * _sig(-lam_row_v)
        gates = _rg_gates(xv, pre, lam_row_v)
        dxc = jnp.zeros((t, RGW), F32)
        dpre_r = []
        dpre_i = []
        for d, ((r, gi, a, mult), l_ref, hs) in enumerate(zip(gates, (l0_ref, l1_ref), h_shift)):
            dbb = l_ref[...]
            da = dbb * hs
            cs = slice(RGW * d, RGW * (d + 1))
            dmult = dbb * gi * xv
            dgi = dbb * mult * xv
            dxc = dxc + dbb * mult * gi
            dla = da * a - dmult * a * a / mult
            dr = -dla * sp8[:, cs]
            dlam_ref[:, cs] += _colsum(-dla * r) * dsp_dlam[:, cs]
            dpre_r.append(dr * r * (1.0 - r))
            dpre_i.append(dgi * gi * (1.0 - gi))
        dpre = jnp.concatenate(dpre_r + dpre_i, axis=1)
        dgb_ref[...] += _colsum(dpre)
        dpre_b = dpre.astype(BF16)
        dpre_ref[...] = dpre_b
        xcb_ref[...] = xv.astype(BF16)
        dxc_ref[...] = dxc + _dot_nt(dpre_b, wg_ref[...])

    prev, nxt = _halo_specs(s, t, RGW)
    return pl.pallas_call(
        body, name=name, grid=(s // t,),
        in_specs=[_rows(t, RGW), _full((RGW, 4 * RGW)), _full((1, 4 * RGW)), _full((1, 2 * RGW))] + [_rows(t, RGW)] * 2
        + [_rows(t, RGW), prev, nxt] * 2,
        out_specs=[_rows(t, RGW), _rows(t, 4 * RGW), _rows(t, RGW), _full((1, 4 * RGW)), _full((1, 2 * RGW))],
        out_shape=[_sds((s, RGW)), _sds((s, 4 * RGW), BF16), _sds((s, RGW), BF16), _sds((1, 4 * RGW)), _sds((1, 2 * RGW))],
        compiler_params=_cparams(1),
    )(xc, wgates, gbias, lam_row, lam0, lam1, hf, hf, hf, hb, hb, hb)


class _GdnMasks:
    def __init__(self, d):
        ri = lax.broadcasted_iota(jnp.int32, (CHUNK, CHUNK), 0)
        ci = lax.broadcasted_iota(jnp.int32, (CHUNK, CHUNK), 1)
        self.incl = (ri >= ci) if d == 0 else (ri <= ci)
        self.strict = (ri > ci) if d == 0 else (ri < ci)
        b16 = jnp.right_shift(ri, 4) == jnp.right_shift(ci, 4)
        b32 = jnp.right_shift(ri, 5) == jnp.right_shift(ci, 5)
        self.diag16 = b16
        self.off32 = jnp.logical_and(b32, jnp.logical_not(b16))
        self.off64 = jnp.logical_not(b32)
        self.eye = jnp.where(ri == ci, 1.0, 0.0).astype(F32)
        self.tri = jnp.where(self.incl, 1.0, 0.0).astype(F32)
        self.last = CHUNK - 1 if d == 0 else 0


def _tri_inv(lmat, m):
    return _tri_inv_many([lmat], [m])[0]


def _tri_inv_many(lmats, masks):
    n = len(lmats)
    ns = [jnp.where(masks[i].diag16, lmats[i], 0.0) for i in range(n)]
    ps = [masks[i].eye - ns[i] for i in range(n)]
    qs = [_dot3(ns[i], ns[i]) for i in range(n)]
    for step in range(3):
        ps = [_dot3(ps[i], masks[i].eye + qs[i]) for i in range(n)]
        if step < 2:
            qs = [_dot3(qs[i], qs[i]) for i in range(n)]
    for off in ("off32", "off64"):
        ts = [_dot3(ps[i], jnp.where(getattr(masks[i], off), lmats[i], 0.0)) for i in range(n)]
        ps = [ps[i] - _dot3(ts[i], ps[i]) for i in range(n)]
    return ps


def _chunk_cumsums(m, bgv):
    return _dot_exact(m.tri, bgv, _NN, True), _dot_exact(m.tri, bgv, ((0,), (1,)), False)


class _GdnHead:
    def __init__(self, qh, kh, vh, kk, q0, bg, gcs, gcs_t, d, h, m):
        cb = 4 * d + h
        cg = 8 + 4 * d + h
        self.q, self.k, self.v = qh, kh, vh
        self.beta = bg[:, cb:cb + 1]
        gcol = gcs[:, cg:cg + 1]
        grow = gcs_t[cg:cg + 1, :]
        gl = gcs[m.last:m.last + 1, cg:cg + 1]
        self.decay = jnp.exp(jnp.where(m.incl, gcol - grow, -1e30))
        self.kb = kh * self.beta
        self.vb = vh * self.beta
        self.a0 = kk * self.beta
        self.q0 = q0
        self.lmat = jnp.where(m.strict, self.a0 * self.decay, 0.0)
        self.attn = self.q0 * self.decay
        self.eg = jnp.exp(gcol)
        self.ek = jnp.exp(gl - gcol)
        self.cd = jnp.exp(gl)
        self.kg = self.kb * self.eg
        self.qd = qh * self.eg
        self.kd = kh * self.ek


HW = NH * DH
SEQ_CB = 4
LOCAL_CB = 4


def _head(h):
    return slice(DH * h, DH * (h + 1))


def _gdn_local_fwd(q, k, v, bg, name):
    s = q.shape[0]
    n = s // CHUNK
    cb = min(LOCAL_CB, n)

    def body(q_ref, k_ref, v_ref, bg_ref, t_ref, u_ref, w_ref, qd_ref, kd_ref, at_ref, cd_ref):
        masks = [_GdnMasks(d) for d in range(2)]
        inst = []
        for jj in range(cb):
            rows = slice(CHUNK * jj, CHUNK * (jj + 1))
            bgv = bg_ref[rows, :]
            qs = [q_ref[rows, _head(h)] for h in range(NH)]
            ks = [k_ref[rows, _head(h)] for h in range(NH)]
            kk = [_dot_nt(ks[h], ks[h]) for h in range(NH)]
            q0 = [_dot_nt(qs[h], ks[h]) for h in range(NH)]
            for d, m in enumerate(masks):
                gcs, gcs_t = _chunk_cumsums(m, bgv)
                for h in range(NH):
                    c = _GdnHead(qs[h], ks[h], v_ref[rows, _head(h)], kk[h], q0[h], bgv, gcs, gcs_t, d, h, m)
                    inst.append((jj, rows, d, h, m, c))
        tms = _tri_inv_many([it[-1].lmat for it in inst], [it[-2] for it in inst])
        for (jj, rows, d, h, m, c), tm in zip(inst, tms):
            sl = _head(h)
            t_ref[jj, d, h] = tm
            u_ref[d, rows, sl] = _dot(tm, c.vb)
            w_ref[d, rows, sl] = _dot(tm, c.kg).astype(BF16)
            qd_ref[d, rows, sl] = c.qd.astype(BF16)
            kd_ref[d, rows, sl] = c.kd.astype(BF16)
            at_ref[jj, d, h] = c.attn.astype(BF16)
            cd_ref[jj, 4 * d + h:4 * d + h + 1, :] = jnp.broadcast_to(c.cd, (1, DH))

    tok = _rows(cb * CHUNK, HW)
    tok2 = pl.BlockSpec((2, cb * CHUNK, HW), lambda i: (0, i, 0))
    mat = pl.BlockSpec((cb, 2, NH, CHUNK, CHUNK), lambda i: (i, 0, 0, 0, 0))
    return pl.pallas_call(
        body, name=name, grid=(n // cb,), in_specs=[tok, tok, tok, _rows(cb * CHUNK, BAP)],
        out_specs=[mat, tok2, tok2, tok2, tok2, mat, pl.BlockSpec((cb, 8, DH), lambda i: (i, 0, 0))],
        out_shape=[_sds((n, 2, NH, CHUNK, CHUNK)), _sds((2, s, HW)), _sds((2, s, HW), BF16), _sds((2, s, HW), BF16),
                   _sds((2, s, HW), BF16), _sds((n, 2, NH, CHUNK, CHUNK), BF16), _sds((n, 8, DH))],
        compiler_params=_cparams(1),
    )(q, k, v, bg)


def _seq_specs(s, order):
    n = s // CHUNK
    cb = min(SEQ_CB, n)
    nb = n // cb
    tb = cb * CHUNK

    def blk(d):
        return (lambda i: i) if order[d] else (lambda i: nb - 1 - i)

    def per_dir(make):
        return [make(d, blk(d)) for d in range(2)]

    tok2 = per_dir(lambda d, f: pl.BlockSpec((1, tb, HW), lambda i: (d, f(i), 0)))
    tok = per_dir(lambda d, f: pl.BlockSpec((tb, HW), lambda i: (f(i), 0)))
    mat = per_dir(lambda d, f: pl.BlockSpec((cb, 1, NH, CHUNK, CHUNK), lambda i: (f(i), d, 0, 0, 0)))
    cds = per_dir(lambda d, f: pl.BlockSpec((cb, 8, DH), lambda i: (f(i), 0, 0)))
    sts = per_dir(lambda d, f: pl.BlockSpec((cb, NH, DH, DH), lambda i: (f(i), 0, 0, 0)))
    dcd = per_dir(lambda d, f: pl.BlockSpec((cb, NH, DH), lambda i: (f(i), 0, 0)))
    return n, cb, nb, tok2, tok, mat, cds, sts, dcd


def _gdn_seq_fwd(u, w, qd, kd, at, cd, name):
    s = u.shape[1]
    n, cb, nb, tok2, tok, mat, cds, sts, _ = _seq_specs(s, (True, False))

    def body(*refs):
        ins = (refs[0:6], refs[6:12])
        outs = (refs[12:15], refs[15:18])
        st = refs[18]

        @pl.when(pl.program_id(0) == 0)
        def _():
            st[...] = jnp.zeros_like(st)

        for j in range(cb):
            items = []
            for d in range(2):
                jj = j if d == 0 else cb - 1 - j
                items += [(d, h, jj, slice(CHUNK * jj, CHUNK * (jj + 1)), _head(h)) for h in range(NH)]
            shs = [st[d, h] for d, h, _, _, _ in items]
            wss = [_dot(ins[d][1][0, rows, sl], sh) for (d, h, jj, rows, sl), sh in zip(items, shs)]
            vns = [ins[d][0][0, rows, sl] - ws for (d, h, jj, rows, sl), ws in zip(items, wss)]
            news = [sh * ins[d][5][jj, 4 * d + h:4 * d + h + 1, :] + _dot_tn(ins[d][3][0, rows, sl], vn)
                    for (d, h, jj, rows, sl), sh, vn in zip(items, shs, vns)]
            for (d, h, jj, rows, sl), sh, vn, new in zip(items, shs, vns, news):
                o_r, s_r, vn_r = outs[d]
                st[d, h] = new
                s_r[jj, h] = sh
                vn_r[rows, sl] = vn
                o_r[rows, sl] = _dot(ins[d][2][0, rows, sl], sh) + _dot(ins[d][4][jj, 0, h], vn)

    in_specs, out_specs, out_shape = [], [], []
    for d in range(2):
        in_specs += [tok2[d]] * 4 + [mat[d], cds[d]]
        out_specs += [tok[d], sts[d], tok[d]]
        out_shape += [_sds((s, HW)), _sds((n, NH, DH, DH)), _sds((s, HW))]
    return pl.pallas_call(
        body, name=name, grid=(nb,), in_specs=in_specs, out_specs=out_specs, out_shape=out_shape,
        scratch_shapes=[pltpu.VMEM((2, NH, DH, DH), F32)], compiler_params=_cparams(1),
    )(u, w, qd, kd, at, cd, u, w, qd, kd, at, cd)


def _gdn_seq_bwd(do, w, qd, kd, at, cd, states, vns, name):
    s = do.shape[0]
    n, cb, nb, tok2, tok, mat, cds, sts, dcd = _seq_specs(s, (False, True))

    def body(*refs):
        ins = (refs[0:8], refs[8:16])
        outs = (refs[16:21], refs[21:26])
        dst = refs[26]

        @pl.when(pl.program_id(0) == 0)
        def _():
            dst[...] = jnp.zeros_like(dst)

        for j in range(cb):
            items = []
            for d in range(2):
                jj = cb - 1 - j if d == 0 else j
                items += [(d, h, jj, slice(CHUNK * jj, CHUNK * (jj + 1)), _head(h)) for h in range(NH)]
            dsns = [dst[d, h] for d, h, _, _, _ in items]
            dohs = [ins[d][0][rows, sl] for d, h, jj, rows, sl in items]
            d_vns = [_dot_tn(ins[d][4][jj, 0, h], doh) + _dot(ins[d][3][0, rows, sl], dsn)
                     for (d, h, jj, rows, sl), doh, dsn in zip(items, dohs, dsns)]
            news = [ins[d][5][jj, 4 * d + h:4 * d + h + 1, :] * dsn + _dot_tn(ins[d][2][0, rows, sl], doh)
                    - _dot_tn(ins[d][1][0, rows, sl], d_vn)
                    for (d, h, jj, rows, sl), doh, dsn, d_vn in zip(items, dohs, dsns, d_vns)]
            for (d, h, jj, rows, sl), doh, dsn, d_vn, new in zip(items, dohs, dsns, d_vns, news):
                dvn_r, dkd_r, dqd_r, dw_r, dcd_r = outs[d]
                sh = ins[d][6][jj, h]
                dst[d, h] = new
                dvn_r[rows, sl] = d_vn
                dkd_r[rows, sl] = _dot_nt(ins[d][7][rows, sl], dsn)
                dqd_r[rows, sl] = _dot_nt(doh, sh)
                dw_r[rows, sl] = -_dot_nt(d_vn, sh)
                d_cd = jnp.sum(jnp.sum(sh * dsn, axis=1, keepdims=True), axis=0, keepdims=True)
                dcd_r[jj, h:h + 1, :] = jnp.broadcast_to(d_cd, (1, DH))

    in_specs, out_specs, out_shape, args = [], [], [], []
    for d in range(2):
        in_specs += [tok[d]] + [tok2[d]] * 3 + [mat[d], cds[d], sts[d], tok[d]]
        args += [do, w, qd, kd, at, cd, states[d], vns[d]]
        out_specs += [tok[d]] * 4 + [dcd[d]]
        out_shape += [_sds((s, HW))] * 4 + [_sds((n, NH, DH))]
    return pl.pallas_call(
        body, name=name, grid=(nb,), in_specs=in_specs, out_specs=out_specs, out_shape=out_shape,
        scratch_shapes=[pltpu.VMEM((2, NH, DH, DH), F32)], compiler_params=_cparams(1),
    )(*args)


def _gdn_local_bwd(q, k, v, bg, tmat, do, vns, seq_grads, name, comm=None):
    s = q.shape[0]
    n = s // CHUNK
    cb = min(LOCAL_CB, n)

    def body(*refs):
        q_ref, k_ref, v_ref, bg_ref, t_ref, do_ref = refs[0:6]
        vn_refs = refs[6:8]
        sg = (refs[8:13], refs[13:18])
        dq_ref, dk_ref, dv_ref, dbg_ref = refs[18:]
        lane = lax.broadcasted_iota(jnp.int32, (CHUNK, BAP), 1)
        rowi = lax.broadcasted_iota(jnp.int32, (CHUNK, 1), 0)
        ones = jnp.ones((CHUNK, DH), F32)
        masks = [_GdnMasks(d) for d in range(2)]
        inst = []
        for jj in range(cb):
            rows = slice(CHUNK * jj, CHUNK * (jj + 1))
            bgv = bg_ref[rows, :]
            qs = [q_ref[rows, _head(h)] for h in range(NH)]
            ks = [k_ref[rows, _head(h)] for h in range(NH)]
            kk = [_dot_nt(ks[h], ks[h]) for h in range(NH)]
            q0 = [_dot_nt(qs[h], ks[h]) for h in range(NH)]
            for d, m in enumerate(masks):
                gcs, gcs_t = _chunk_cumsums(m, bgv)
                for h in range(NH):
                    c = _GdnHead(qs[h], ks[h], v_ref[rows, _head(h)], kk[h], q0[h], bgv, gcs, gcs_t, d, h, m)
                    inst.append((jj, rows, d, h, m, c))
        ni = len(inst)
        cs = [it[-1] for it in inst]
        tms = [t_ref[jj, d, h] for jj, _, d, h, _, _ in inst]
        d_vns = [sg[d][0][rows, _head(h)] for _, rows, d, h, _, _ in inst]
        d_ws = [sg[d][3][rows, _head(h)] for _, rows, d, h, _, _ in inst]
        d_ts = [_dot_nt(d_vns[i], cs[i].vb) + _dot_nt(d_ws[i], cs[i].kg) for i in range(ni)]
        xs = [_dot3(tms[i], d_ts[i], _TN) for i in range(ni)]
        d_ls = [jnp.where(inst[i][4].strict, -_dot3(xs[i], tms[i], _NT), 0.0) for i in range(ni)]
        d_attns = [jnp.where(m.incl, _dot_nt(do_ref[rows, _head(h)], vn_refs[d][rows, _head(h)]), 0.0)
                   for _, rows, d, h, m, _ in inst]
        d_vbs = [_dot_tn(tms[i], d_vns[i]) for i in range(ni)]
        d_kgs = [_dot_tn(tms[i], d_ws[i]) for i in range(ni)]
        d_a0s = [d_ls[i] * cs[i].decay for i in range(ni)]
        d_q0s = [d_attns[i] * cs[i].decay for i in range(ni)]
        es = [(d_ls[i] * cs[i].a0 + d_attns[i] * cs[i].q0) * cs[i].decay for i in range(ni)]
        kb_mm = [_dot(d_a0s[i], cs[i].k) for i in range(ni)]
        q_mm = [_dot(d_q0s[i], cs[i].k) for i in range(ni)]
        k_mm = [_dot_tn(d_a0s[i], cs[i].kb) + _dot_tn(d_q0s[i], cs[i].q) for i in range(ni)]
        e_cols = [_dot_exact(ones, es[i], _TN, False)[:, 0:1] for i in range(ni)]
        acc = {}
        d_gcs, d_betas = [], []
        for i, (jj, rows, d, h, m, c) in enumerate(inst):
            sl = _head(h)
            d_kd, d_qd = sg[d][1][rows, sl], sg[d][2][rows, sl]
            d_cd = sg[d][4][jj, h:h + 1, 0:1]
            d_vb, d_kg = d_vbs[i], d_kgs[i]
            d_kb = kb_mm[i] + d_kg * c.eg
            parts = (q_mm[i] + d_qd * c.eg, k_mm[i] + d_kd * c.ek + d_kb * c.beta, d_vb * c.beta)
            acc[jj, h] = [p + a for a, p in zip(acc[jj, h], parts)] if (jj, h) in acc else list(parts)
            s_kd = jnp.sum(d_kd * c.kd, axis=1, keepdims=True)
            d_gc = (jnp.sum(d_kg * c.kg, axis=1, keepdims=True) + jnp.sum(d_qd * c.qd, axis=1, keepdims=True) - s_kd
                    + jnp.sum(es[i], axis=1, keepdims=True) - e_cols[i])
            d_gl = jnp.sum(s_kd, axis=0, keepdims=True) + d_cd * c.cd
            d_gcs.append(d_gc + jnp.where(rowi == m.last, d_gl, 0.0))
            d_betas.append(jnp.sum(d_kb * c.k, axis=1, keepdims=True) + jnp.sum(d_vb * c.v, axis=1, keepdims=True))
        d_gs = [_dot_exact(inst[i][4].tri, d_gcs[i] * ones, _TN, True)[:, 0:1] for i in range(ni)]
        dbg = [jnp.zeros((CHUNK, BAP), F32) for _ in range(cb)]
        for i, (jj, _, d, h, _, _) in enumerate(inst):
            dbg[jj] = dbg[jj] + jnp.where(lane == 4 * d + h, d_betas[i], 0.0) + jnp.where(lane == 8 + 4 * d + h, d_gs[i], 0.0)
        for jj in range(cb):
            rows = slice(CHUNK * jj, CHUNK * (jj + 1))
            for h in range(NH):
                dq_ref[rows, _head(h)], dk_ref[rows, _head(h)], dv_ref[rows, _head(h)] = acc[jj, h]
            dbg_ref[rows, :] = dbg[jj]

    tok = _rows(cb * CHUNK, HW)
    bgs = _rows(cb * CHUNK, BAP)
    mat = pl.BlockSpec((cb, 2, NH, CHUNK, CHUNK), lambda i: (i, 0, 0, 0, 0))
    dcd = pl.BlockSpec((cb, NH, DH), lambda i: (i, 0, 0))
    args = [q, k, v, bg, tmat, do, vns[0], vns[1]]
    in_specs = [tok, tok, tok, bgs, mat, tok, tok, tok]
    for d in range(2):
        args += list(seq_grads[d])
        in_specs += [tok] * 4 + [dcd]
    return _pallas(body, comm, name=name, grid=(n // cb,), in_specs=in_specs, out_specs=[tok, tok, tok, bgs],
                   out_shape=[_sds((s, HW))] * 3 + [_sds((s, BAP))], scratch_shapes=[], args=args)


def _prep_bwd(c_qkv, p_ba, alog_row, dtb_row, dq, dk, dv, dbg, name):
    s = c_qkv.shape[0]
    t = min(256, s)

    def body(cq_ref, pc_ref, alog_ref, dtb_ref, dq_ref, dk_ref, dv_ref, dbg_ref,
             dcq_ref, dpc_ref, dalog_ref, ddtb_ref):
        @pl.when(pl.program_id(0) == 0)
        def _():
            dalog_ref[...] = jnp.zeros_like(dalog_ref)
            ddtb_ref[...] = jnp.zeros_like(ddtb_ref)

        cq = cq_ref[...]
        sq = cq * _sig(cq)
        sg = _silu_grad(cq)
        for h in range(NH):
            sl = slice(DH * h, DH * (h + 1))
            for off, d_ref, scale in ((0, dq_ref, DH ** -0.5), (RGW, dk_ref, 1.0)):
                csl = slice(off + DH * h, off + DH * (h + 1))
                xh = sq[:, csl]
                nrm = lax.rsqrt(jnp.sum(xh * xh, axis=-1, keepdims=True) + EPS)
                y = xh * nrm
                dy = d_ref[:, sl] * scale
                dcq_ref[:, csl] = nrm * (dy - y * jnp.sum(dy * y, axis=-1, keepdims=True)) * sg[:, csl]
        dcq_ref[:, 2 * RGW:] = dv_ref[...] * sg[:, 2 * RGW:]
        pc = pc_ref[...]
        lane = lax.broadcasted_iota(jnp.int32, pc.shape, 1)
        dbg = dbg_ref[...]
        beta = _sig(pc)
        ea = jnp.exp(alog_ref[...])
        z = pc + dtb_ref[...]
        g = -ea * _softplus(z)
        is_g = jnp.logical_and(lane >= 8, lane < 16)
        d_alpha = jnp.where(is_g, dbg * (-ea) * _sig(z), 0.0)
        dpc_ref[...] = jnp.where(lane < 8, dbg * beta * (1.0 - beta), d_alpha)
        dalog_ref[...] += _colsum(jnp.where(is_g, dbg * g, 0.0))
        ddtb_ref[...] += _colsum(d_alpha)

    return pl.pallas_call(
        body, name=name, grid=(s // t,),
        in_specs=[_rows(t, QKVW), _rows(t, BAP), _full((1, BAP)), _full((1, BAP))] + [_rows(t, HW)] * 3 + [_rows(t, BAP)],
        out_specs=[_rows(t, QKVW), _rows(t, BAP), _full((1, BAP)), _full((1, BAP))],
        out_shape=[_sds((s, QKVW)), _sds((s, BAP)), _sds((1, BAP)), _sds((1, BAP))],
        compiler_params=_cparams(1),
    )(c_qkv, p_ba, alog_row, dtb_row, dq, dk, dv, dbg)


def _mix_out_values(hf, hb, gate, of, ob, z, gn):
    hr = hf + hb
    y_rg = hr * _gelu(gate)
    osum = of + ob
    parts = []
    for h in range(NH):
        sl = slice(DH * h, DH * (h + 1))
        oh = osum[:, sl]
        r, ohat = _rms(oh)
        zh = z[:, sl]
        parts.append((r, ohat, zh))
    y_gdn = jnp.concatenate([ohat * gn * (zh * _sig(zh)) for (r, ohat, zh) in parts], axis=1)
    return hr, y_rg, y_gdn, parts


def _outproj(x1, hf, hb, gate, of, ob, z, gn, wout, name):
    s = x1.shape[0]
    t = min(256, s)

    def body(x_ref, hf_ref, hb_ref, gate_ref, of_ref, ob_ref, z_ref, gn_ref, w_ref, xo_ref, y_ref):
        _, y_rg, y_gdn, _ = _mix_out_values(hf_ref[...], hb_ref[...], gate_ref[...], of_ref[...], ob_ref[...],
                                            z_ref[...], gn_ref[...])
        y = jnp.concatenate([y_rg, y_gdn], axis=1).astype(BF16)
        y_ref[...] = y
        xo_ref[...] = x_ref[...] + jnp.dot(y, w_ref[...], preferred_element_type=F32)

    return pl.pallas_call(
        body, name=name, grid=(s // t,),
        in_specs=[_rows(t, D)] + [_rows(t, RGW)] * 6 + [_full((1, DH)), _full((D, D))],
        out_specs=[_rows(t, D), _rows(t, D)], out_shape=[_sds((s, D)), _sds((s, D), BF16)],
        compiler_params=_cparams(1),
    )(x1, hf, hb, gate, of, ob, z, gn, wout)


def _outproj_bwd(dx2, hf, hb, gate, of, ob, z, gn, wout, name):
    s = dx2.shape[0]
    t = min(256, s)

    def body(d_ref, hf_ref, hb_ref, gate_ref, of_ref, ob_ref, z_ref, gn_ref, w_ref,
             dhr_ref, dgate_ref, dos_ref, dz_ref, dgn_ref, db_ref):
        @pl.when(pl.program_id(0) == 0)
        def _():
            dgn_ref[...] = jnp.zeros_like(dgn_ref)

        gate = gate_ref[...]
        gn_v = gn_ref[...]
        hr, _, _, parts = _mix_out_values(hf_ref[...], hb_ref[...], gate, of_ref[...], ob_ref[...], z_ref[...], gn_v)
        dbf = d_ref[...].astype(BF16)
        db_ref[...] = dbf
        dy = _dot_nt(dbf, w_ref[...])
        dyr = dy[:, :RGW]
        dhr_ref[...] = dyr * _gelu(gate)
        dgate_ref[...] = dyr * hr * _gelu_grad(gate)
        dgn = jnp.zeros((1, DH), F32)
        for h, (r, ohat, zh) in enumerate(parts):
            sl = slice(DH * h, DH * (h + 1))
            dyh = dy[:, RGW + DH * h:RGW + DH * (h + 1)]
            sz = zh * _sig(zh)
            dn = dyh * sz
            dz_ref[:, sl] = dyh * ohat * gn_v * _silu_grad(zh)
            dgn = dgn + _colsum(dn * ohat)
            dos_ref[:, sl] = _rms_bwd(dn, ohat, r, gn_v)
        dgn_ref[...] += dgn

    return pl.pallas_call(
        body, name=name, grid=(s // t,),
        in_specs=[_rows(t, D)] + [_rows(t, RGW)] * 6 + [_full((1, DH)), _full((D, D))],
        out_specs=[_rows(t, RGW)] * 4 + [_full((1, DH)), _rows(t, D)],
        out_shape=[_sds((s, RGW))] * 4 + [_sds((1, DH)), _sds((s, D), BF16)],
        compiler_params=_cparams(1),
    )(dx2, hf, hb, gate, of, ob, z, gn, wout)


def _loss_head(x3, target, gain, name):
    s = x3.shape[0]
    t = min(256, s)

    def body(x_ref, t_ref, g_ref, dx_ref, loss_ref, dg_ref):
        @pl.when(pl.program_id(0) == 0)
        def _():
            loss_ref[...] = jnp.zeros_like(loss_ref)
            dg_ref[...] = jnp.zeros_like(dg_ref)

        r, xh = _rms(x_ref[...])
        gv = g_ref[...]
        err = xh * gv - t_ref[...]
        per_tok = jnp.mean(err * err, axis=-1, keepdims=True)
        loss_ref[...] += 0.5 * jnp.sum(per_tok, axis=0, keepdims=True)
        dy = err * (1.0 / D)
        dg_ref[...] += _colsum(dy * xh)
        dx_ref[...] = _rms_bwd(dy, xh, r, gv)

    return pl.pallas_call(
        body, name=name, grid=(s // t,), in_specs=[_rows(t, D), _rows(t, D), _full((1, D))],
        out_specs=[_rows(t, D), _full((8, 128)), _full((1, D))],
        out_shape=[_sds((s, D)), _sds((8, 128)), _sds((1, D))], compiler_params=_cparams(1),
    )(x3, target, gain)


def _adamw_math(wv, gv, mv, vv):
    mn = ADAM_B1 * mv + (1.0 - ADAM_B1) * gv
    vn = ADAM_B2 * vv + (1.0 - ADAM_B2) * (gv * gv)
    m_hat = mn / (1.0 - ADAM_B1 ** ADAM_STEP)
    v_hat = vn / (1.0 - ADAM_B2 ** ADAM_STEP)
    return -ADAM_LR * (m_hat / (jnp.sqrt(v_hat) + ADAM_EPS) + ADAM_WD * wv), mn, vn


def _row_tile(r, c):
    tr = r
    while tr * c * 4 > (1 << 20) and tr % 16 == 0:
        tr //= 2
    return tr


def _adamw(w, g, m, v, name):
    r, c = w.shape
    tr = _row_tile(r, c)

    def body(w_ref, g_ref, m_ref, v_ref, d_ref, nm_ref, nv_ref):
        d_ref[...], nm_ref[...], nv_ref[...] = _adamw_math(w_ref[...], g_ref[...], m_ref[...], v_ref[...])

    return pl.pallas_call(
        body, name=name, grid=(r // tr,), in_specs=[_rows(tr, c)] * 4, out_specs=[_rows(tr, c)] * 3,
        out_shape=[_sds((r, c))] * 3, compiler_params=_cparams(1),
    )(w, g, m, v)


def _adamw_halves(w, own, recv, m, v, c_arr, name):
    r, c = w.shape
    h = r // 2
    tr = _row_tile(h, c)
    nh = h // tr

    def body(c_ref, w_ref, own_ref, recv_ref, m_ref, v_ref, g_ref, d_ref, nm_ref, nv_ref):
        first_half = pl.program_id(0) < nh
        use_own = first_half == (c_ref[0] == 0)
        gv = jnp.where(use_own, own_ref[...], recv_ref[...])
        g_ref[...] = gv
        d_ref[...], nm_ref[...], nv_ref[...] = _adamw_math(w_ref[...], gv, m_ref[...], v_ref[...])

    full = pl.BlockSpec((tr, c), lambda i, c_ref: (i, 0))
    half = pl.BlockSpec((tr, c), lambda i, c_ref: (i % nh, 0))
    return pl.pallas_call(
        body, name=name, out_shape=[_sds((r, c))] * 4,
        grid_spec=pltpu.PrefetchScalarGridSpec(
            num_scalar_prefetch=1, grid=(2 * nh,), in_specs=[full, half, half, full, full], out_specs=[full] * 4),
        compiler_params=_cparams(1),
    )(c_arr, w, own, recv, m, v)


def _mesh_pos():
    return lax.axis_index("x"), lax.axis_index("y"), lax.axis_index("c")


def _other_chips(x, y):
    return [(1 - x, y), (x, 1 - y), (1 - x, 1 - y)]


class _Comm:
    def __init__(self, inputs, out_shapes, scratch, start, finish, space=pltpu.HBM):
        self.inputs, self.out_shapes, self.scratch = list(inputs), list(out_shapes), list(scratch)
        self.start, self.finish, self.space = start, finish, space


def _comm_call(comm, name):
    ni, no = len(comm.inputs), len(comm.out_shapes)

    def body(*refs):
        comm.start(refs[:ni], refs[ni:ni + no], refs[ni + no:])
        comm.finish(refs[:ni], refs[ni:ni + no], refs[ni + no:])

    spec = pl.BlockSpec(memory_space=comm.space)
    return list(pl.pallas_call(body, name=name, out_shape=comm.out_shapes, in_specs=[spec] * ni, out_specs=[spec] * no,
                               scratch_shapes=comm.scratch)(*comm.inputs))


def _pallas(body, comm, *, name, grid, in_specs, out_specs, out_shape, scratch_shapes, args):
    params = _cparams(len(grid))
    if comm is None:
        outs = pl.pallas_call(body, name=name, grid=grid, in_specs=in_specs, out_specs=out_specs, out_shape=out_shape,
                              scratch_shapes=scratch_shapes, compiler_params=params)(*args)
        return list(outs), []
    n_in, n_out, n_sc = len(in_specs), len(out_specs), len(scratch_shapes)
    ci, co = len(comm.inputs), len(comm.out_shapes)

    def carried(*refs):
        bounds = [0, n_in, n_in + ci, n_in + ci + n_out, n_in + ci + n_out + co, n_in + ci + n_out + co + n_sc, len(refs)]
        ins, cins, outs, couts, scr, csems = [refs[lo:hi] for lo, hi in zip(bounds[:-1], bounds[1:])]
        ids = [pl.program_id(k) for k in range(len(grid))]
        first = functools.reduce(jnp.logical_and, [i == 0 for i in ids])
        last = functools.reduce(jnp.logical_and, [i == g - 1 for i, g in zip(ids, grid)])

        @pl.when(first)
        def _():
            comm.start(cins, couts, csems)

        body(*ins, *outs, *scr)

        @pl.when(last)
        def _():
            comm.finish(cins, couts, csems)

    hbm = pl.BlockSpec(memory_space=pltpu.HBM)
    outs = pl.pallas_call(
        carried, name=name, grid=grid, in_specs=list(in_specs) + [hbm] * ci, out_specs=list(out_specs) + [hbm] * co,
        out_shape=list(out_shape) + comm.out_shapes, scratch_shapes=list(scratch_shapes) + comm.scratch,
        compiler_params=params)(*args, *comm.inputs)
    return list(outs[:n_out]), list(outs[n_out:])


def _gather_comm(arrays, space, block_rows):
    n_arr = len(arrays)

    def plan(x_refs, out_refs, sems):
        send_sems, recv_sems, local_sems = sems
        x, y, c = _mesh_pos()
        me, sibling = (x, y, c), (x, y, 1 - c)
        chips = _other_chips(x, y)

        def slot(a, px, py, pc):
            return out_refs[a].at[4 * px + 2 * py + pc]

        def copy(a, k, block, to, src=None):
            return pltpu.make_async_remote_copy(
                src_ref=slot(a, *block) if src is None else src, dst_ref=slot(a, *block),
                send_sem=send_sems.at[7 * a + k], recv_sem=recv_sems.at[7 * a + k], device_id=to, device_id_type=MESH)

        srcs = [x_refs[a] if block_rows[a] is None else
                x_refs[a].at[pl.ds(pl.multiple_of(c * block_rows[a], 16), block_rows[a]), :] for a in range(n_arr)]
        local = [pltpu.make_async_copy(srcs[a], slot(a, *me), local_sems.at[a]) for a in range(n_arr)]
        first = []
        for a in range(n_arr):
            first += [copy(a, 1 + j, me, (*chip, c), src=srcs[a]) for j, chip in enumerate(chips)]
            first.append(copy(a, 0, me, sibling, src=srcs[a]))
        return me, sibling, chips, c, copy, local, first

    def start(x_refs, out_refs, sems):
        _, _, _, _, _, local, first = plan(x_refs, out_refs, sems)
        for cp in local + first:
            cp.start()

    def finish(x_refs, out_refs, sems):
        me, sibling, chips, c, copy, local, first = plan(x_refs, out_refs, sems)
        passed = []
        for j, chip in enumerate(chips):
            for a in range(n_arr):
                copy(a, 1 + j, (*chip, c), me).wait_recv()
                fwd = copy(a, 4 + j, (*chip, c), sibling)
                fwd.start()
                passed.append(fwd)
        for a in range(n_arr):
            copy(a, 0, sibling, me).wait_recv()
            for j, chip in enumerate(chips):
                copy(a, 4 + j, (*chip, 1 - c), me).wait_recv()
        for cp in first + passed:
            cp.wait_send()
        for cp in local:
            cp.wait()

    out_shapes = [_sds((8, w.shape[0] if r is None else r) + w.shape[1:], w.dtype) for w, r in zip(arrays, block_rows)]
    scratch = [pltpu.SemaphoreType.DMA((7 * n_arr,)), pltpu.SemaphoreType.DMA((7 * n_arr,)), pltpu.SemaphoreType.DMA((n_arr,))]
    return _Comm(arrays, out_shapes, scratch, start, finish, space)


def _weights_gather_comm(shards):
    return _gather_comm(shards, pltpu.HBM, [w.shape[0] // 2 for w in shards])


def _all_shards(gathered):
    return [o.reshape(NSH, 2 * o.shape[1], o.shape[2]) for o in gathered]


def _gather_small(block, name):
    return _comm_call(_gather_comm([block], pltpu.VMEM, [None]), name)[0]


def _sibling_exchange(gs, name):
    n = len(gs)
    halves = [g.shape[1] // 2 for g in gs]

    def body(*refs):
        g_refs, land_refs = refs[:n], refs[n:2 * n]
        send_sems, recv_sems = refs[2 * n:]
        x, y, c = _mesh_pos()
        copies = []
        for a in range(n):
            h = halves[a]
            for s in range(NSH):
                copies.append(pltpu.make_async_remote_copy(
                    src_ref=g_refs[a].at[s, pl.ds(pl.multiple_of((1 - c) * h, 8), h), :], dst_ref=land_refs[a].at[s],
                    send_sem=send_sems.at[NSH * a + s], recv_sem=recv_sems.at[NSH * a + s],
                    device_id=(x, y, 1 - c), device_id_type=MESH))
        for cp in copies:
            cp.start()
        for cp in copies:
            cp.wait()

    return pl.pallas_call(
        body, name=name, out_shape=[_sds((NSH, h, g.shape[2])) for h, g in zip(halves, gs)],
        in_specs=[pl.BlockSpec(memory_space=pltpu.HBM)] * n, out_specs=[pl.BlockSpec(memory_space=pltpu.HBM)] * n,
        scratch_shapes=[pltpu.SemaphoreType.DMA((NSH * n,)), pltpu.SemaphoreType.DMA((NSH * n,))],
    )(*gs)


def _chip_sum(g, land, c_arr, name):
    _, h, cols = land.shape

    def body(c_ref, g_ref, l_ref, o_ref):
        o_ref[...] = (g_ref[...] + l_ref[...]).astype(BF16)

    return pl.pallas_call(
        body, name=name, out_shape=_sds((NSH, h, cols), BF16),
        grid_spec=pltpu.PrefetchScalarGridSpec(
            num_scalar_prefetch=1, grid=(NSH,),
            in_specs=[pl.BlockSpec((1, h, cols), lambda s, c_ref: (s, c_ref[0], 0)),
                      pl.BlockSpec((1, h, cols), lambda s, c_ref: (s, 0, 0))],
            out_specs=pl.BlockSpec((1, h, cols), lambda s, c_ref: (s, 0, 0))),
        compiler_params=_cparams(1),
    )(c_arr, g, land)


def _scatter_comm(parts):
    n = len(parts)

    def plan(p_refs, land_refs, sems):
        send_sems, recv_sems, local_sems = sems
        x, y, c = _mesh_pos()
        my_chip = 2 * x + y
        local = [pltpu.make_async_copy(p_refs[a].at[my_chip], land_refs[a].at[my_chip], local_sems.at[a]) for a in range(n)]
        copies = []
        for a in range(n):
            for j, (px, py) in enumerate(_other_chips(x, y)):
                copies.append(pltpu.make_async_remote_copy(
                    src_ref=p_refs[a].at[2 * px + py], dst_ref=land_refs[a].at[my_chip],
                    send_sem=send_sems.at[3 * a + j], recv_sem=recv_sems.at[3 * a + j],
                    device_id=(px, py, c), device_id_type=MESH))
        return local, copies

    def start(p_refs, land_refs, sems):
        local, copies = plan(p_refs, land_refs, sems)
        for cp in local + copies:
            cp.start()

    def finish(p_refs, land_refs, sems):
        local, copies = plan(p_refs, land_refs, sems)
        for cp in copies:
            cp.wait()
        for cp in local:
            cp.wait()

    scratch = [pltpu.SemaphoreType.DMA((3 * n,)), pltpu.SemaphoreType.DMA((3 * n,)), pltpu.SemaphoreType.DMA((n,))]
    return _Comm(parts, [_sds(p.shape, BF16) for p in parts], scratch, start, finish)


def _sum_slots(land, name):
    k, r, c = land.shape
    tr = r // 2 if r % 32 == 0 else r

    def body(l_ref, o_ref):
        acc = l_ref[0].astype(F32)
        for i in range(1, k):
            acc = acc + l_ref[i].astype(F32)
        o_ref[...] = acc

    return pl.pallas_call(
        body, name=name, grid=(r // tr,), in_specs=[pl.BlockSpec((k, tr, c), lambda i: (0, i, 0))],
        out_specs=_rows(tr, c), out_shape=_sds((r, c)), compiler_params=_cparams(1),
    )(land)


def _sibling_swap(halves):
    n = len(halves)

    def body(*refs):
        h_refs, out_refs = refs[:n], refs[n:2 * n]
        send_sems, recv_sems = refs[2 * n:]
        x, y, c = _mesh_pos()
        copies = [pltpu.make_async_remote_copy(
            src_ref=h_refs[a], dst_ref=out_refs[a], send_sem=send_sems.at[a], recv_sem=recv_sems.at[a],
            device_id=(x, y, 1 - c), device_id_type=MESH) for a in range(n)]
        for cp in copies:
            cp.start()
        for cp in copies:
            cp.wait()

    return pl.pallas_call(
        body, name="grad_sibling_swap", out_shape=[_sds(h.shape) for h in halves],
        in_specs=[pl.BlockSpec(memory_space=pltpu.HBM)] * n, out_specs=[pl.BlockSpec(memory_space=pltpu.HBM)] * n,
        scratch_shapes=[pltpu.SemaphoreType.DMA((n,)), pltpu.SemaphoreType.DMA((n,))],
    )(*halves)


def _pad_rows(v, width):
    flat = v.reshape(-1)
    rows = -(-flat.shape[0] // width)
    rows = -(-rows // 8) * 8
    return jnp.pad(flat, (0, rows * width - flat.shape[0])).reshape(rows, width)


def _size(shape):
    n = 1
    for dim in shape:
        n *= dim
    return n


def _row_pack(arrs):
    pieces = []
    for a in arrs:
        rows = -(-a.size // D)
        pieces.append(jnp.pad(a.reshape(-1), (0, rows * D - a.size)).reshape(rows, D))
    total = sum(p.shape[0] for p in pieces)
    if total % 8:
        pieces.append(jnp.zeros((8 - total % 8, D), F32))
    return jnp.concatenate(pieces, axis=0)


def _row_unpack(packed, shapes):
    out, r0 = [], 0
    for shp in shapes:
        n = _size(shp)
        rows = -(-n // D)
        out.append(packed[r0:r0 + rows].reshape(-1)[:n].reshape(shp))
        r0 += rows
    return out


def _block_diag(w):
    eye = jnp.eye(8, dtype=w.dtype)
    return (w[:, :, None, :] * eye[:, None, :, None]).reshape(RGW, RGW)


def _diag_blocks(dense):
    r = dense.reshape(8, 64, 8, 64)
    return jnp.stack([r[n, :, n, :] for n in range(8)])


def _lane_row(v8):
    return jnp.zeros((1, BAP), F32).at[0, 8:16].set(v8.reshape(8))


def _reduce_parts(gs, names, c_arr, tag):
    lands = _sibling_exchange(gs, "grad_sibling_exchange_" + tag)
    return [_chip_sum(g, l, c_arr, "chip_sum_" + n) for g, l, n in zip(gs, lands, names)]


def _local_step(x, target, sw, ffn1_w, later_shards, c_arr):
    (g1, gmix, rg_cw8, rg_cb, wgates, gbias, lam_row, gdn_cw8, alog_row, dtb_row, gn, g2, gfin) = sw
    wg1, wu1, wd1 = ffn1_w

    (x1, a1, b1, fb1), gathered = _ffn_fwd(x, g1, wg1, wu1, wd1, "ffn1_fwd", comm=_weights_gather_comm(later_shards))
    win_sh, wout_sh, wg2, wu2, wd2 = _all_shards(gathered)
    w_in_full = jnp.transpose(win_sh, (1, 0, 2)).reshape(D, NSH * INSH)
    wout = wout_sh.reshape(D, D)
    w_in_groups = (w_in_full[:, 0:512], w_in_full[:, 512:1024], w_in_full[:, 1024:2560], w_in_full[:, 2560:3072],
                   jnp.pad(w_in_full[:, 3072:3088], ((0, 0), (0, BAP - BAW))))
    h2, p_rgx, p_gate, p_qkv, p_z, p_ba = _inproj(x1, gmix, w_in_groups, "in_proj")
    c_rg = _conv(p_rgx, rg_cw8, rg_cb, "rg_conv")
    c_qkv = _conv(p_qkv, gdn_cw8, jnp.zeros((1, QKVW), F32), "gdn_conv")
    a0, bb0, a1s, bb1, q, k, v, bg = _mix_prep(c_rg, c_qkv, p_ba, wgates, gbias, lam_row, alog_row, dtb_row, "mix_prep")
    hf, hb = _scan_pair(a0, bb0, a1s, bb1, False, "rg_scan")
    tmat, gu, gw, gqd, gkd, gat, gcd = _gdn_local_fwd(q, k, v, bg, "gdn_local_fwd")
    of, s0, vn0, ob, s1, vn1 = _gdn_seq_fwd(gu, gw, gqd, gkd, gat, gcd, "gdn_seq_fwd")
    x2, ymix = _outproj(x1, hf, hb, p_gate, of, ob, p_z, gn, wout, "out_proj")
    (x3, a2, b2, fb2), _ = _ffn_fwd(x2, g2, wg2, wu2, wd2, "ffn2_fwd")
    dx3, loss_blk, d_gfin = _loss_head(x3, target, gfin, "loss_head")

    dx2, d_g2, hb2, dob2, dab2, dbb2 = _ffn_bwd(x2, dx3, g2, a2, b2, wg2, wu2, wd2, "ffn2_bwd")
    d_ffn2 = [_tn(dab2, hb2, "ffn2_dwg"), _tn(dbb2, hb2, "ffn2_dwu"), _tn(fb2, dob2, "ffn2_dwd")]
    parts_ffn2 = _reduce_parts(d_ffn2, _BIG_NAMES[5:8], c_arr, "ffn2")

    d_hr, d_gate, d_os, d_z, d_gn, dx2b = _outproj_bwd(dx2, hf, hb, p_gate, of, ob, p_z, gn, wout, "out_proj_bwd")
    d_wout = _tn(ymix, dx2b, "dw_out")[0]

    lam1, lam0 = _scan_pair(a1s, d_hr, a0, d_hr, True, "rg_scan_bwd")
    d_xc, d_pre, xcb, d_gbias, d_lam = _gates_bwd(c_rg, wgates, gbias, lam_row, lam0, lam1, hf, hb, "rg_gates_bwd")
    d_wgates = _tn(xcb, d_pre, "dw_gates")[0]
    d_prgx, d_rgcw8, d_rgcb = _conv_bwd(p_rgx, d_xc, rg_cw8, "rg_conv_bwd")

    sg = _gdn_seq_bwd(d_os, gw, gqd, gkd, gat, gcd, (s0, s1), (vn0, vn1), "gdn_seq_bwd")
    (dq, dk, dv, dbg), lands_ffn2 = _gdn_local_bwd(q, k, v, bg, tmat, d_os, (vn0, vn1), (sg[0:5], sg[5:10]), "gdn_local_bwd",
                                                  comm=_scatter_comm(parts_ffn2))
    d_cqkv, d_pba, d_alog, d_dtb = _prep_bwd(c_qkv, p_ba, alog_row, dtb_row, dq, dk, dv, dbg, "gdn_prep_bwd")
    d_pqkv, d_gdncw8, _ = _conv_bwd(p_qkv, d_cqkv, gdn_cw8, "gdn_conv_bwd")

    dps = (d_prgx, d_gate, d_pqkv, d_z, d_pba)
    dx1, d_gmix = _inproj_bwd(x1, dx2, gmix, dps, w_in_groups, "in_proj_bwd")
    d_win_groups = [_tn(h2, dp, "dw_in_%d" % i)[0] for i, dp in enumerate(dps)]
    d_win = jnp.concatenate(d_win_groups[:4] + [d_win_groups[4][:, :BAW]], axis=1)
    d_mix = [jnp.transpose(d_win.reshape(D, NSH, INSH), (1, 0, 2)), d_wout.reshape(NSH, OUTSH, D)]
    parts_mix = _reduce_parts(d_mix, _BIG_NAMES[3:5], c_arr, "mix")

    gx, d_g1, hb1, dob1, dab1, dbb1 = _ffn_bwd(x, dx1, g1, a1, b1, wg1, wu1, wd1, "ffn1_bwd")
    d_wg1, lands_mix = _tn(dab1, hb1, "ffn1_dwg", comm=_scatter_comm(parts_mix))
    parts_wg1 = _reduce_parts([d_wg1], _BIG_NAMES[0:1], c_arr, "ffn1_gate")
    d_wu1, lands_wg1 = _tn(dbb1, hb1, "ffn1_dwu", comm=_scatter_comm(parts_wg1))
    parts_wu1 = _reduce_parts([d_wu1], _BIG_NAMES[1:2], c_arr, "ffn1_up")
    d_wd1, lands_wu1 = _tn(fb1, dob1, "ffn1_dwd", comm=_scatter_comm(parts_wu1))
    parts_wd1 = _reduce_parts([d_wd1], _BIG_NAMES[2:3], c_arr, "ffn1_down")
    lands_ffn1 = lands_wg1 + lands_wu1 + _comm_call(_scatter_comm(parts_wd1), "grad_chip_scatter_ffn1_down")

    halves = [_sum_slots(l, "sum_chips_" + n) for l, n in zip(lands_ffn1 + lands_mix + lands_ffn2, _BIG_NAMES)]
    small = dict(
        ffn1_norm=d_g1, mix_norm=d_gmix, rg_conv_w=d_rgcw8[:4], rg_conv_b=d_rgcb,
        rg_gate_a_w=jnp.stack([_diag_blocks(d_wgates[:, RGW * i:RGW * (i + 1)]) for i in (0, 1)]),
        rg_gate_x_w=jnp.stack([_diag_blocks(d_wgates[:, RGW * i:RGW * (i + 1)]) for i in (2, 3)]),
        rg_gate_a_b=d_gbias[0, :2 * RGW].reshape(2, RGW), rg_gate_x_b=d_gbias[0, 2 * RGW:].reshape(2, RGW),
        rg_lambda=d_lam.reshape(2, RGW), gdn_conv_w=d_gdncw8[:4],
        gdn_a_log=d_alog[0, 8:16].reshape(2, NH), gdn_dt_bias=d_dtb[0, 8:16].reshape(2, NH),
        gdn_norm=d_gn, ffn2_norm=d_g2, final_norm=d_gfin)
    return loss_blk, gx, halves, small


_SMALL_NAMES = ("ffn1_norm", "mix_norm", "rg_conv_w", "rg_conv_b", "rg_gate_a_w", "rg_gate_a_b", "rg_gate_x_w",
                "rg_gate_x_b", "rg_lambda", "gdn_conv_w", "gdn_a_log", "gdn_dt_bias", "gdn_norm", "ffn2_norm", "final_norm")
_SMALL_SHARDED = dict(rg_conv_w=128, rg_gate_a_b=128, rg_gate_x_b=128, rg_lambda=128, gdn_conv_w=384)
_OUT_ORDER = ("ffn1_norm", "ffn1_w_gate", "ffn1_w_up", "ffn1_w_down", "mix_norm", "w_in", "w_out", "rg_conv_w", "rg_conv_b",
              "rg_gate_a_w", "rg_gate_a_b", "rg_gate_x_w", "rg_gate_x_b", "rg_lambda", "gdn_conv_w", "gdn_a_log",
              "gdn_dt_bias", "gdn_norm", "ffn2_norm", "ffn2_w_gate", "ffn2_w_up", "ffn2_w_down", "final_norm")
_BIG_NAMES = ("ffn1_w_gate", "ffn1_w_up", "ffn1_w_down", "w_in", "w_out", "ffn2_w_gate", "ffn2_w_up", "ffn2_w_down")
_TRANSPOSED = ("ffn1_w_gate", "ffn1_w_up", "ffn2_w_gate", "ffn2_w_up")


def kernel(x, ffn1_norm, ffn1_w_gate, ffn1_w_up, ffn1_w_down, mix_norm, w_in, w_out, rg_conv_w, rg_conv_b, rg_gate_a_w, rg_gate_a_b, rg_gate_x_w, rg_gate_x_b, rg_lambda, gdn_conv_w, gdn_a_log, gdn_dt_bias, gdn_norm, ffn2_norm, ffn2_w_gate, ffn2_w_up, ffn2_w_down, final_norm, loss_target, m_ffn1_norm, m_ffn1_w_gate, m_ffn1_w_up, m_ffn1_w_down, m_mix_norm, m_w_in, m_w_out, m_rg_conv_w, m_rg_conv_b, m_rg_gate_a_w, m_rg_gate_a_b, m_rg_gate_x_w, m_rg_gate_x_b, m_rg_lambda, m_gdn_conv_w, m_gdn_a_log, m_gdn_dt_bias, m_gdn_norm, m_ffn2_norm, m_ffn2_w_gate, m_ffn2_w_up, m_ffn2_w_down, m_final_norm, v_ffn1_norm, v_ffn1_w_gate, v_ffn1_w_up, v_ffn1_w_down, v_mix_norm, v_w_in, v_w_out, v_rg_conv_w, v_rg_conv_b, v_rg_gate_a_w, v_rg_gate_a_b, v_rg_gate_x_w, v_rg_gate_x_b, v_rg_lambda, v_gdn_conv_w, v_gdn_a_log, v_gdn_dt_bias, v_gdn_norm, v_ffn2_norm, v_ffn2_w_gate, v_ffn2_w_up, v_ffn2_w_down, v_final_norm):
    args = dict(locals())
    w = {n: args[n] for n in _OUT_ORDER}
    mom = {n: args["m_" + n] for n in _OUT_ORDER}
    var = {n: args["v_" + n] for n in _OUT_ORDER}
    xi, yi, ci = _mesh_pos()
    shard = 2 * xi + yi

    big_bf16 = [w[n][0].astype(BF16) for n in _BIG_NAMES]
    ffn1_w = _all_shards(_comm_call(_weights_gather_comm(big_bf16[0:3]), "gather_ffn1_weights"))
    sm_local = _pad_rows(jnp.concatenate([w[n][0].reshape(-1) for n in _SMALL_SHARDED]), 128)
    sm_all = _gather_small(sm_local, "gather_small_weights")[0::2].reshape(NSH, -1)
    sm_full, off = {}, 0
    for n, wd_ in _SMALL_SHARDED.items():
        rows = w[n].shape[1]
        piece = sm_all[:, off:off + rows * wd_].reshape(NSH, rows, wd_)
        sm_full[n] = jnp.transpose(piece, (1, 0, 2)).reshape(rows, NSH * wd_)
        off += rows * wd_

    wa, wx = rg_gate_a_w[0], rg_gate_x_w[0]
    wgates = jnp.concatenate([_block_diag(wa[0]), _block_diag(wa[1]), _block_diag(wx[0]), _block_diag(wx[1])],
                             axis=1).astype(BF16)
    gbias = jnp.concatenate([sm_full["rg_gate_a_b"].reshape(1, -1), sm_full["rg_gate_x_b"].reshape(1, -1)], axis=1)
    sw = (ffn1_norm, mix_norm, jnp.pad(sm_full["rg_conv_w"], ((0, 4), (0, 0))), rg_conv_b, wgates, gbias,
          sm_full["rg_lambda"].reshape(1, -1), jnp.pad(sm_full["gdn_conv_w"], ((0, 4), (0, 0))), _lane_row(gdn_a_log),
          _lane_row(gdn_dt_bias), gdn_norm, ffn2_norm, final_norm.reshape(1, D))
    c_arr = ci.reshape(1).astype(jnp.int32)

    loss_blk, gx, halves, small = _local_step(x[0], loss_target[0], sw, ffn1_w, big_bf16[3:], c_arr)
    loss = lax.psum(loss_blk[0, 0], ("x", "y", "c"))
    grads = {}

    sm_grad = _row_pack([small[n] for n in _SMALL_NAMES])
    sm_sum = _sum_slots(_gather_small(sm_grad, "gather_small_grads"), "small_grad_sum")
    for n, g in zip(_SMALL_NAMES, _row_unpack(sm_sum, [small[n].shape for n in _SMALL_NAMES])):
        if n in _SMALL_SHARDED:
            wd_ = _SMALL_SHARDED[n]
            g = lax.dynamic_slice_in_dim(g, shard * wd_, wd_, axis=1)
        grads[n] = g.reshape(w[n].shape)

    delta, new_m, new_v = {}, {}, {}
    for n, own, recv in zip(_BIG_NAMES, halves, _sibling_swap(halves)):
        to2d = jnp.transpose if n in _TRANSPOSED else (lambda t: t)
        outs4 = _adamw_halves(to2d(w[n][0]), own, recv, to2d(mom[n][0]), to2d(var[n][0]), c_arr, "adamw_" + n)
        grads[n], delta[n], new_m[n], new_v[n] = [to2d(o)[None] for o in outs4]
    packs = [_row_pack([t[n] for n in _SMALL_NAMES]) for t in (w, grads, mom, var)]
    sm_shapes = [w[n].shape for n in _SMALL_NAMES]
    for dst, src in zip((delta, new_m, new_v), _adamw(*packs, "adamw_small")):
        for n, val in zip(_SMALL_NAMES, _row_unpack(src, sm_shapes)):
            dst[n] = val

    outs = [loss, gx[None]]
    for group in (grads, delta, new_m, new_v):
        outs += [group[n] for n in _OUT_ORDER]
    return tuple(outs)
```

```python
import functools

import jax
import jax.numpy as jnp
from jax import lax
from jax.experimental import pallas as pl
from jax.experimental.pallas import tpu as pltpu

F32 = jnp.float32
BF16 = jnp.bfloat16
EPS = 1e-6
D = 1024
NSH = 4
FSH = 704
RGW = 512
QKVW = 1536
ZW = 512
BAW = 16
BAP = 128
INSH = 772
OUTSH = 256
CHUNK = 64
NH = 4
DH = 128
RG_C = 8.0
VMEM_LIMIT = 52 * 1024 * 1024
MESH = pl.DeviceIdType.MESH

ADAM_LR = 0.001
ADAM_B1 = 0.9
ADAM_B2 = 0.999
ADAM_EPS = 1e-08
ADAM_WD = 0.01
ADAM_STEP = 10


def _cparams(n_grid):
    return pltpu.CompilerParams(dimension_semantics=("arbitrary",) * n_grid, vmem_limit_bytes=VMEM_LIMIT)


def _sig(x):
    return 0.5 + 0.5 * jnp.tanh(0.5 * x)


def _sig_pos(x):
    return 1.0 / (1.0 + jnp.exp(-x))


def _softplus(x):
    return jnp.maximum(x, 0.0) + jnp.log(1.0 + jnp.exp(-jnp.abs(x)))


def _neg_expm1(y):
    series = -y * (1.0 + y * (0.5 + y * (1.0 / 6 + y * (1.0 / 24 + y * (1.0 / 120 + y * (1.0 / 720 + y / 5040))))))
    return jnp.where(y > -0.3, series, 1.0 - jnp.exp(y))


_GELU_C = 0.7978845608028654


def _gelu(x):
    t = jnp.tanh(_GELU_C * (x + 0.044715 * x * x * x))
    return 0.5 * x * (1.0 + t)


def _gelu_grad(x):
    t = jnp.tanh(_GELU_C * (x + 0.044715 * x * x * x))
    return 0.5 * (1.0 + t) + 0.5 * x * (1.0 - t * t) * _GELU_C * (1.0 + 3 * 0.044715 * x * x)


def _silu_grad(x):
    s = _sig(x)
    return s * (1.0 + x * (1.0 - s))


def _dot(a, b):
    return jnp.dot(a.astype(BF16), b.astype(BF16), preferred_element_type=F32)


def _dot_nt(a, b):
    return lax.dot_general(a.astype(BF16), b.astype(BF16), (((1,), (1,)), ((), ())), preferred_element_type=F32)


def _dot_tn(a, b):
    return lax.dot_general(a.astype(BF16), b.astype(BF16), (((0,), (0,)), ((), ())), preferred_element_type=F32)


_NN = ((1,), (0,))
_NT = ((1,), (1,))
_TN = ((0,), (0,))


def _dg(a, b, dims):
    return lax.dot_general(a, b, (dims, ((), ())), preferred_element_type=F32)


def _split2(a):
    hi = a.astype(BF16)
    return hi, (a - hi.astype(F32)).astype(BF16)


def _dot3(a, b, dims=_NN):
    ah, al = _split2(a)
    bh, bl = _split2(b)
    return _dg(ah, bh, dims) + _dg(ah, bl, dims) + _dg(al, bh, dims)


def _dot_exact(e, x, dims, e_is_lhs):
    x0 = x.astype(BF16)
    r = x - x0.astype(F32)
    x1 = r.astype(BF16)
    x2 = (r - x1.astype(F32)).astype(BF16)
    eb = e.astype(BF16)
    if e_is_lhs:
        return _dg(eb, x0, dims) + _dg(eb, x1, dims) + _dg(eb, x2, dims)
    return _dg(x0, eb, dims) + _dg(x1, eb, dims) + _dg(x2, eb, dims)


def _rms(xv):
    r = lax.rsqrt(jnp.mean(xv * xv, axis=-1, keepdims=True) + EPS)
    return r, xv * r


def _rms_bwd(dy, xh, r, gain):
    dxh = dy * gain
    return r * (dxh - xh * jnp.mean(dxh * xh, axis=-1, keepdims=True))


def _colsum(v):
    return jnp.sum(v, axis=0, keepdims=True)


def _rows(t, c):
    return pl.BlockSpec((t, c), lambda i: (i, 0))


def _full(shape):
    n = len(shape)
    return pl.BlockSpec(shape, lambda i: (0,) * n)


def _sds(shape, dtype=F32):
    return jax.ShapeDtypeStruct(shape, dtype)


def _ffn_fwd(x, gain, wg, wu, wd, name, comm=None):
    s = x.shape[0]
    tm = min(512, s)

    def body(x_ref, g_ref, wg_ref, wu_ref, wd_ref, xo_ref, ga_ref, gb_ref, f_ref, h_sc, acc):
        j = pl.program_id(1)

        @pl.when(j == 0)
        def _():
            _, xh = _rms(x_ref[...])
            h_sc[...] = (xh * g_ref[...]).astype(BF16)
            acc[...] = jnp.zeros_like(acc)

        h = h_sc[...]

        a = jnp.dot(h, wg_ref[0], preferred_element_type=F32)
        b = jnp.dot(h, wu_ref[0], preferred_element_type=F32)
        sa = _sig(a)
        silu = a * sa
        fv = silu * b
        f = fv.astype(BF16)
        f_ref[0] = f
        ga_ref[0] = (sa * b + fv * (1.0 - sa)).astype(BF16)
        gb_ref[0] = silu.astype(BF16)
        acc[...] += jnp.dot(f, wd_ref[0], preferred_element_type=F32)

        @pl.when(j == NSH - 1)
        def _():
            xo_ref[...] = x_ref[...] + 0.5 * acc[...]

    return _pallas(
        body, comm, name=name, grid=(s // tm, NSH),
        in_specs=[pl.BlockSpec((tm, D), lambda i, j: (i, 0)), pl.BlockSpec((1, D), lambda i, j: (0, 0)),
                  pl.BlockSpec((1, D, FSH), lambda i, j: (j, 0, 0)), pl.BlockSpec((1, D, FSH), lambda i, j: (j, 0, 0)),
                  pl.BlockSpec((1, FSH, D), lambda i, j: (j, 0, 0))],
        out_specs=[pl.BlockSpec((tm, D), lambda i, j: (i, 0))] + [pl.BlockSpec((1, tm, FSH), lambda i, j: (j, i, 0))] * 3,
        out_shape=[_sds((s, D))] + [_sds((NSH, s, FSH), BF16)] * 3,
        scratch_shapes=[pltpu.VMEM((tm, D), BF16), pltpu.VMEM((tm, D), F32)],
        args=(x, gain, wg, wu, wd))


def _ffn_bwd(x, dout, do, gain, ga, gb, wg, wu, wd, name):
    s = x.shape[0]
    tm = min(512, s)

    def hidden(do_ref, ga_ref, gb_ref, wd_ref, da_ref, db_ref):
        df = _dot_nt(do_ref[...], wd_ref[0])
        da_ref[0] = (df * ga_ref[0].astype(F32)).astype(BF16)
        db_ref[0] = (df * gb_ref[0].astype(F32)).astype(BF16)

    th = min(1024, s)
    tok = pl.BlockSpec((th, D), lambda i, j: (i, 0))
    sh = pl.BlockSpec((1, th, FSH), lambda i, j: (j, i, 0))
    da, db = pl.pallas_call(
        hidden, name=name + "_hidden", grid=(s // th, NSH),
        in_specs=[tok, sh, sh, pl.BlockSpec((1, FSH, D), lambda i, j: (j, 0, 0))], out_specs=[sh, sh],
        out_shape=[_sds((NSH, s, FSH), BF16)] * 2, compiler_params=_cparams(2),
    )(do, ga, gb, wd)

    def inputs(x_ref, d_ref, g_ref, da_ref, db_ref, wg_ref, wu_ref, dx_ref, dg_ref, h_ref):
        @pl.when(pl.program_id(0) == 0)
        def _():
            dg_ref[...] = jnp.zeros_like(dg_ref)

        dh = jnp.zeros((tm, D), F32)
        for j in range(NSH):
            dh = dh + _dot_nt(da_ref[j], wg_ref[j]) + _dot_nt(db_ref[j], wu_ref[j])
        r, xh = _rms(x_ref[...])
        gv = g_ref[...]
        h_ref[...] = (xh * gv).astype(BF16)
        dg_ref[...] += _colsum(dh * xh)
        dx_ref[...] = d_ref[...] + _rms_bwd(dh, xh, r, gv)

    grads = pl.BlockSpec((NSH, tm, FSH), lambda i: (0, i, 0))
    resident = pl.BlockSpec((NSH, D, FSH), lambda i: (0, 0, 0), pipeline_mode=pl.Buffered(1))
    dx, dg, h = pl.pallas_call(
        inputs, name=name + "_input", grid=(s // tm,),
        in_specs=[_rows(tm, D), _rows(tm, D), _full((1, D)), grads, grads, resident, resident],
        out_specs=[_rows(tm, D), _full((1, D)), _rows(tm, D)],
        out_shape=[_sds((s, D)), _sds((1, D)), _sds((s, D), BF16)], compiler_params=_cparams(1),
    )(x, dout, gain, da, db, wg, wu)
    return dx, dg, h, da, db


def _tn(a, b, name, comm=None):
    a_g = a.ndim == 3
    b_g = b.ndim == 3
    g = a.shape[0] if a_g else (b.shape[0] if b_g else 1)
    s, k = a.shape[-2:]
    n = b.shape[-1]
    ts = min(2048 if b.dtype == BF16 else 1024, s)

    def body(a_ref, b_ref, o_ref):
        @pl.when(pl.program_id(1) == 0)
        def _():
            o_ref[...] = jnp.zeros_like(o_ref)

        av = a_ref[0] if a_g else a_ref[...]
        bv = b_ref[0] if b_g else b_ref[...]
        o_ref[0] += _dot_tn(av, bv)

    a_spec = pl.BlockSpec((1, ts, k), lambda gi, si: (gi, si, 0)) if a_g else pl.BlockSpec((ts, k), lambda gi, si: (si, 0))
    b_spec = pl.BlockSpec((1, ts, n), lambda gi, si: (gi, si, 0)) if b_g else pl.BlockSpec((ts, n), lambda gi, si: (si, 0))
    outs, carried = _pallas(body, comm, name=name, grid=(g, s // ts), in_specs=[a_spec, b_spec],
                            out_specs=[pl.BlockSpec((1, k, n), lambda gi, si: (gi, 0, 0))], out_shape=[_sds((g, k, n))],
                            scratch_shapes=[], args=(a, b))
    return outs[0] if comm is None else (outs[0], carried)


_P_WIDTHS = (RGW, RGW, QKVW, ZW, BAP)


def _inproj(x1, gain, ws, name):
    s = x1.shape[0]
    tm = min(256, s)

    def body(x_ref, g_ref, *refs):
        w_refs = refs[:5]
        h_ref = refs[5]
        p_refs = refs[6:]
        _, xh = _rms(x_ref[...])
        h = (xh * g_ref[...]).astype(BF16)
        h_ref[...] = h
        for w_ref, p_ref in zip(w_refs, p_refs):
            p_ref[...] = jnp.dot(h, w_ref[...], preferred_element_type=F32)

    return pl.pallas_call(
        body, name=name, grid=(s // tm,),
        in_specs=[_rows(tm, D), _full((1, D))] + [_full((D, w)) for w in _P_WIDTHS],
        out_specs=[_rows(tm, D)] + [_rows(tm, w) for w in _P_WIDTHS],
        out_shape=[_sds((s, D), BF16)] + [_sds((s, w)) for w in _P_WIDTHS],
        compiler_params=_cparams(1),
    )(x1, gain, *ws)


def _inproj_bwd(x1, dx2, gain, dps, ws, name):
    s = x1.shape[0]
    tm = min(256, s)

    def body(x_ref, d_ref, g_ref, *refs):
        dp_refs = refs[:5]
        w_refs = refs[5:10]
        dx_ref, dxh_ref, dg_ref = refs[10:]

        @pl.when(pl.program_id(0) == 0)
        def _():
            dg_ref[...] = jnp.zeros_like(dg_ref)

        dh = jnp.zeros((tm, D), F32)
        for dp_ref, w_ref in zip(dp_refs, w_refs):
            dh = dh + _dot_nt(dp_ref[...], w_ref[...])
        r, xh = _rms(x_ref[...])
        dg_ref[...] += _colsum(dh * xh)
        dx = d_ref[...] + _rms_bwd(dh, xh, r, g_ref[...])
        dx_ref[...] = dx
        dxh_ref[...] = (0.5 * dx).astype(BF16)

    return pl.pallas_call(
        body, name=name, grid=(s // tm,),
        in_specs=[_rows(tm, D), _rows(tm, D), _full((1, D))] + [_rows(tm, w) for w in _P_WIDTHS]
        + [_full((D, w)) for w in _P_WIDTHS],
        out_specs=[_rows(tm, D), _rows(tm, D), _full((1, D))],
        out_shape=[_sds((s, D)), _sds((s, D), BF16), _sds((1, D))],
        compiler_params=_cparams(1),
    )(x1, dx2, gain, *dps, *ws)


def _halo_specs(s, t, c):
    nb8 = s // 8
    tb = t // 8
    prev = pl.BlockSpec((8, c), lambda i: (jnp.maximum(i * tb - 1, 0), 0))
    nxt = pl.BlockSpec((8, c), lambda i: (jnp.minimum((i + 1) * tb, nb8 - 1), 0))
    return prev, nxt


def _edge_masks(nb):
    i = pl.program_id(0)
    return jnp.where(i > 0, 1.0, 0.0).astype(F32), jnp.where(i < nb - 1, 1.0, 0.0).astype(F32)


def _shifted(xx, off, t):
    n = t + 16
    sh = (-off) % n
    rolled = xx if sh == 0 else pltpu.roll(xx, sh, 0)
    return rolled[8:8 + t]


def _conv(x, w8, bias, name):
    s, c = x.shape
    t = min(256, s)
    nb = s // t

    def body(x_ref, xp_ref, xn_ref, w_ref, b_ref, o_ref):
        pm, nm = _edge_masks(nb)
        for c0 in range(0, c, 512):
            cols = slice(c0, c0 + 512)
            xx = jnp.concatenate([xp_ref[:, cols] * pm, x_ref[:, cols], xn_ref[:, cols] * nm], axis=0)
            acc = jnp.zeros((t, 512), F32) + b_ref[:, cols]
            for j in range(4):
                acc = acc + w_ref[j:j + 1, cols] * _shifted(xx, j - 2, t)
            o_ref[:, cols] = acc

    prev, nxt = _halo_specs(s, t, c)
    return pl.pallas_call(
        body, name=name, grid=(nb,),
        in_specs=[_rows(t, c), prev, nxt, _full((8, c)), _full((1, c))],
        out_specs=_rows(t, c), out_shape=_sds((s, c)), compiler_params=_cparams(1),
    )(x, x, x, w8, bias)


def _conv_bwd(x, dc, w8, name):
    s, c = x.shape
    t = min(256, s)
    nb = s // t

    def body(x_ref, d_ref, dp_ref, dn_ref, w_ref, dx_ref, dw_ref, db_ref):
        @pl.when(pl.program_id(0) == 0)
        def _():
            dw_ref[...] = jnp.zeros_like(dw_ref)
            db_ref[...] = jnp.zeros_like(db_ref)

        pm, nm = _edge_masks(nb)
        for c0 in range(0, c, 512):
            cols = slice(c0, c0 + 512)
            dd = jnp.concatenate([dp_ref[:, cols] * pm, d_ref[:, cols], dn_ref[:, cols] * nm], axis=0)
            xv = x_ref[:, cols]
            acc = jnp.zeros((t, 512), F32)
            for j in range(4):
                dsh = _shifted(dd, 2 - j, t)
                acc = acc + w_ref[j:j + 1, cols] * dsh
                dw_ref[j:j + 1, cols] += _colsum(dsh * xv)
            dx_ref[:, cols] = acc
            db_ref[:, cols] += _colsum(d_ref[:, cols])

    prev, nxt = _halo_specs(s, t, c)
    return pl.pallas_call(
        body, name=name, grid=(nb,),
        in_specs=[_rows(t, c), _rows(t, c), prev, nxt, _full((8, c))],
        out_specs=[_rows(t, c), _full((8, c)), _full((1, c))],
        out_shape=[_sds((s, c)), _sds((8, c)), _sds((1, c))], compiler_params=_cparams(1),
    )(x, dc, dc, dc, w8)


def _rg_gates(xc, pre, lam_row):
    sp8 = RG_C * _softplus(-lam_row)
    out = []
    for d in range(2):
        r = _sig_pos(pre[:, RGW * d:RGW * (d + 1)])
        gi = _sig(pre[:, 2 * RGW + RGW * d:2 * RGW + RGW * (d + 1)])
        la = -r * sp8[:, RGW * d:RGW * (d + 1)]
        a = jnp.exp(la)
        mult = jnp.sqrt(_neg_expm1(2.0 * la))
        out.append((r, gi, a, mult))
    return out


def _mix_prep(c_rg, c_qkv, p_ba, wgates, gbias, lam_row, alog_row, dtb_row, name):
    s = c_rg.shape[0]
    t = min(256, s)

    def body(xc_ref, cq_ref, pc_ref, wg_ref, gb_ref, lam_ref, alog_ref, dtb_ref,
             a0_ref, b0_ref, a1_ref, b1_ref, q_ref, k_ref, v_ref, bg_ref):
        xc = xc_ref[...]
        pre = _dot(xc, wg_ref[...]) + gb_ref[...]
        gates = _rg_gates(xc, pre, lam_ref[...])
        for (r, gi, a, mult), a_ref, b_ref in zip(gates, (a0_ref, a1_ref), (b0_ref, b1_ref)):
            a_ref[...] = a
            b_ref[...] = mult * gi * xc
        cq = cq_ref[...]
        sq = cq * _sig(cq)
        for h in range(NH):
            sl = slice(DH * h, DH * (h + 1))
            qh = sq[:, sl]
            q_ref[:, sl] = qh * lax.rsqrt(jnp.sum(qh * qh, axis=-1, keepdims=True) + EPS) * (DH ** -0.5)
            kh = sq[:, RGW + DH * h:RGW + DH * (h + 1)]
            k_ref[:, sl] = kh * lax.rsqrt(jnp.sum(kh * kh, axis=-1, keepdims=True) + EPS)
        v_ref[...] = sq[:, 2 * RGW:]
        pc = pc_ref[...]
        lane = lax.broadcasted_iota(jnp.int32, pc.shape, 1)
        beta = _sig(pc)
        g = -jnp.exp(alog_ref[...]) * _softplus(pc + dtb_ref[...])
        bg_ref[...] = jnp.where(lane < 8, beta, jnp.where(lane < 16, g, 0.0))

    return pl.pallas_call(
        body, name=name, grid=(s // t,),
        in_specs=[_rows(t, RGW), _rows(t, QKVW), _rows(t, BAP), _full((RGW, 4 * RGW)), _full((1, 4 * RGW)),
                  _full((1, 2 * RGW)), _full((1, BAP)), _full((1, BAP))],
        out_specs=[_rows(t, RGW)] * 7 + [_rows(t, BAP)],
        out_shape=[_sds((s, RGW))] * 7 + [_sds((s, BAP))],
        compiler_params=_cparams(1),
    )(c_rg, c_qkv, p_ba, wgates, gbias, lam_row, alog_row, dtb_row)


def _scan_pair(af, bf, ar, br, shifted, name):
    s, c = af.shape
    t = min(512, s)
    nb = s // t
    ng = t // 8
    tb = t // 8
    up = lambda i: (i, 0)
    down = lambda i: (nb - 1 - i, 0)

    def body(*refs):
        if shifted:
            af_ref, bf_ref, ar_ref, br_ref, afp_ref, arn_ref, hf_ref, hr_ref, carry, fbuf, rbuf = refs
        else:
            af_ref, bf_ref, ar_ref, br_ref, hf_ref, hr_ref, carry = refs
        i = pl.program_id(0)

        @pl.when(i == 0)
        def _():
            carry[...] = jnp.zeros_like(carry)

        if shifted:
            edge = jnp.where(i > 0, 1.0, 0.0).astype(F32)
            fbuf[0:8, :] = afp_ref[...] * edge
            fbuf[8:t + 8, :] = af_ref[...]
            rbuf[0:t, :] = ar_ref[...]
            rbuf[t:t + 8, :] = arn_ref[...] * edge
        row = lax.broadcasted_iota(jnp.int32, (8, c), 0)

        def block_scan(av, bv, downwards):
            for k in (1, 2, 4):
                sh = (8 - k) if downwards else k
                m = (row < 8 - k) if downwards else (row >= k)
                a_s = pltpu.roll(av, sh, 0)
                b_s = pltpu.roll(bv, sh, 0)
                bv = jnp.where(m, av * b_s + bv, bv)
                av = jnp.where(m, av * a_s, av)
            return av, bv

        def group(gi, cvs):
            cf, cr = cvs
            rf = pl.multiple_of(gi * 8, 8)
            rr = pl.multiple_of((ng - 1 - gi) * 8, 8)
            if shifted:
                a_f = jnp.where(row > 0, pltpu.roll(fbuf[pl.ds(rf + 8, 8), :], 1, 0), pltpu.roll(fbuf[pl.ds(rf, 8), :], 1, 0))
                a_r = jnp.where(row < 7, pltpu.roll(rbuf[pl.ds(rr, 8), :], 7, 0), pltpu.roll(rbuf[pl.ds(rr + 8, 8), :], 7, 0))
            else:
                a_f = af_ref[pl.ds(rf, 8), :]
                a_r = ar_ref[pl.ds(rr, 8), :]
            a_f, b_f = block_scan(a_f, bf_ref[pl.ds(rf, 8), :], False)
            a_r, b_r = block_scan(a_r, br_ref[pl.ds(rr, 8), :], True)
            h_f = a_f * cf + b_f
            h_r = a_r * cr + b_r
            hf_ref[pl.ds(rf, 8), :] = h_f
            hr_ref[pl.ds(rr, 8), :] = h_r
            return h_f[7:8, :], h_r[0:1, :]

        cf, cr = lax.fori_loop(0, ng, group, (carry[0:1, :], carry[8:9, :]))
        carry[0:1, :] = cf
        carry[8:9, :] = cr

    in_specs = [pl.BlockSpec((t, c), up), pl.BlockSpec((t, c), up), pl.BlockSpec((t, c), down), pl.BlockSpec((t, c), down)]
    args = [af, bf, ar, br]
    scratch = [pltpu.VMEM((16, c), F32)]
    if shifted:
        in_specs += [pl.BlockSpec((8, c), lambda i: (jnp.maximum(i * tb - 1, 0), 0)),
                     pl.BlockSpec((8, c), lambda i: (jnp.minimum((nb - i) * tb, s // 8 - 1), 0))]
        args += [af, ar]
        scratch += [pltpu.VMEM((t + 8, c), F32), pltpu.VMEM((t + 8, c), F32)]
    return pl.pallas_call(
        body, name=name, grid=(nb,), in_specs=in_specs,
        out_specs=[pl.BlockSpec((t, c), up), pl.BlockSpec((t, c), down)], out_shape=[_sds((s, c)), _sds((s, c))],
        scratch_shapes=scratch, compiler_params=_cparams(1),
    )(*args)


def _gates_bwd(xc, wgates, gbias, lam_row, lam0, lam1, hf, hb, name):
    s = xc.shape[0]
    t = min(256, s)
    nb = s // t

    def body(xc_ref, wg_ref, gb_ref, lam_ref, l0_ref, l1_ref, hf_ref, hfp_ref, hfn_ref, hb_ref, hbp_ref, hbn_ref,
             dxc_ref, dpre_ref, xcb_ref, dgb_ref, dlam_ref):
        @pl.when(pl.program_id(0) == 0)
        def _():
            dgb_ref[...] = jnp.zeros_like(dgb_ref)
            dlam_ref[...] = jnp.zeros_like(dlam_ref)

        pm, nm = _edge_masks(nb)
        h_prev = _shifted(jnp.concatenate([hfp_ref[...] * pm, hf_ref[...], hfn_ref[...] * nm], axis=0), -1, t)
        h_next = _shifted(jnp.concatenate([hbp_ref[...] * pm, hb_ref[...], hbn_ref[...] * nm], axis=0), 1, t)
        h_shift = (h_prev, h_next)
        xv = xc_ref[...]
        pre = _dot(xv, wg_ref[...]) + gb_ref[...]
        lam_row_v = lam_ref[...]
        sp8 = RG_C * _softplus(-lam_row_v)
        dsp_dlam = -RG_C * _sig(-lam_row_v)
        gates = _rg_gates(xv, pre, lam_row_v)
        dxc = jnp.zeros((t, RGW), F32)
        dpre_r = []
        dpre_i = []
        for d, ((r, gi, a, mult), l_ref, hs) in enumerate(zip(gates, (l0_ref, l1_ref), h_shift)):
            dbb = l_ref[...]
            da = dbb * hs
            cs = slice(RGW * d, RGW * (d + 1))
            dmult = dbb * gi * xv
            dgi = dbb * mult * xv
            dxc = dxc + dbb * mult * gi
            dla = da * a - dmult * a * a / mult
            dr = -dla * sp8[:, cs]
            dlam_ref[:, cs] += _colsum(-dla * r) * dsp_dlam[:, cs]
            dpre_r.append(dr * r * (1.0 - r))
            dpre_i.append(dgi * gi * (1.0 - gi))
        dpre = jnp.concatenate(dpre_r + dpre_i, axis=1)
        dgb_ref[...] += _colsum(dpre)
        dpre_b = dpre.astype(BF16)
        dpre_ref[...] = dpre_b
        xcb_ref[...] = xv.astype(BF16)
        dxc_ref[...] = dxc + _dot_nt(dpre_b, wg_ref[...])

    prev, nxt = _halo_specs(s, t, RGW)
    return pl.pallas_call(
        body, name=name, grid=(s // t,),
        in_specs=[_rows(t, RGW), _full((RGW, 4 * RGW)), _full((1, 4 * RGW)), _full((1, 2 * RGW))] + [_rows(t, RGW)] * 2
        + [_rows(t, RGW), prev, nxt] * 2,
        out_specs=[_rows(t, RGW), _rows(t, 4 * RGW), _rows(t, RGW), _full((1, 4 * RGW)), _full((1, 2 * RGW))],
        out_shape=[_sds((s, RGW)), _sds((s, 4 * RGW), BF16), _sds((s, RGW), BF16), _sds((1, 4 * RGW)), _sds((1, 2 * RGW))],
        compiler_params=_cparams(1),
    )(xc, wgates, gbias, lam_row, lam0, lam1, hf, hf, hf, hb, hb, hb)


class _GdnMasks:
    def __init__(self, d):
        ri = lax.broadcasted_iota(jnp.int32, (CHUNK, CHUNK), 0)
        ci = lax.broadcasted_iota(jnp.int32, (CHUNK, CHUNK), 1)
        self.incl = (ri >= ci) if d == 0 else (ri <= ci)
        self.strict = (ri > ci) if d == 0 else (ri < ci)
        b16 = jnp.right_shift(ri, 4) == jnp.right_shift(ci, 4)
        b32 = jnp.right_shift(ri, 5) == jnp.right_shift(ci, 5)
        self.diag16 = b16
        self.off32 = jnp.logical_and(b32, jnp.logical_not(b16))
        self.off64 = jnp.logical_not(b32)
        self.eye = jnp.where(ri == ci, 1.0, 0.0).astype(F32)
        self.tri = jnp.where(self.incl, 1.0, 0.0).astype(F32)
        self.last = CHUNK - 1 if d == 0 else 0


def _tri_inv(lmat, m):
    return _tri_inv_many([lmat], [m])[0]


def _tri_inv_many(lmats, masks):
    n = len(lmats)
    ns = [jnp.where(masks[i].diag16, lmats[i], 0.0) for i in range(n)]
    ps = [masks[i].eye - ns[i] for i in range(n)]
    qs = [_dot3(ns[i], ns[i]) for i in range(n)]
    for step in range(3):
        ps = [_dot3(ps[i], masks[i].eye + qs[i]) for i in range(n)]
        if step < 2:
            qs = [_dot3(qs[i], qs[i]) for i in range(n)]
    for off in ("off32", "off64"):
        ts = [_dot3(ps[i], jnp.where(getattr(masks[i], off), lmats[i], 0.0)) for i in range(n)]
        ps = [ps[i] - _dot3(ts[i], ps[i]) for i in range(n)]
    return ps


def _chunk_cumsums(m, bgv):
    return _dot_exact(m.tri, bgv, _NN, True), _dot_exact(m.tri, bgv, ((0,), (1,)), False)


class _GdnHead:
    def __init__(self, qh, kh, vh, kk, q0, bg, gcs, gcs_t, d, h, m):
        cb = 4 * d + h
        cg = 8 + 4 * d + h
        self.q, self.k, self.v = qh, kh, vh
        self.beta = bg[:, cb:cb + 1]
        gcol = gcs[:, cg:cg + 1]
        grow = gcs_t[cg:cg + 1, :]
        gl = gcs[m.last:m.last + 1, cg:cg + 1]
        self.decay = jnp.exp(jnp.where(m.incl, gcol - grow, -1e30))
        self.kb = kh * self.beta
        self.vb = vh * self.beta
        self.a0 = kk * self.beta
        self.q0 = q0
        self.lmat = jnp.where(m.strict, self.a0 * self.decay, 0.0)
        self.attn = self.q0 * self.decay
        self.eg = jnp.exp(gcol)
        self.ek = jnp.exp(gl - gcol)
        self.cd = jnp.exp(gl)
        self.kg = self.kb * self.eg
        self.qd = qh * self.eg
        self.kd = kh * self.ek


HW = NH * DH
SEQ_CB = 8
LOCAL_CB = 4


def _head(h):
    return slice(DH * h, DH * (h + 1))


def _gdn_local_fwd(q, k, v, bg, name):
    s = q.shape[0]
    n = s // CHUNK
    cb = min(LOCAL_CB, n)

    def body(q_ref, k_ref, v_ref, bg_ref, t_ref, u_ref, w_ref, qd_ref, kd_ref, at_ref, cd_ref):
        masks = [_GdnMasks(d) for d in range(2)]
        inst = []
        for jj in range(cb):
            rows = slice(CHUNK * jj, CHUNK * (jj + 1))
            bgv = bg_ref[rows, :]
            qs = [q_ref[rows, _head(h)] for h in range(NH)]
            ks = [k_ref[rows, _head(h)] for h in range(NH)]
            kk = [_dot_nt(ks[h], ks[h]) for h in range(NH)]
            q0 = [_dot_nt(qs[h], ks[h]) for h in range(NH)]
            for d, m in enumerate(masks):
                gcs, gcs_t = _chunk_cumsums(m, bgv)
                for h in range(NH):
                    c = _GdnHead(qs[h], ks[h], v_ref[rows, _head(h)], kk[h], q0[h], bgv, gcs, gcs_t, d, h, m)
                    inst.append((jj, rows, d, h, m, c))
        tms = _tri_inv_many([it[-1].lmat for it in inst], [it[-2] for it in inst])
        for (jj, rows, d, h, m, c), tm in zip(inst, tms):
            sl = _head(h)
            t_ref[jj, d, h] = tm
            u_ref[d, rows, sl] = _dot(tm, c.vb)
            w_ref[d, rows, sl] = _dot(tm, c.kg).astype(BF16)
            qd_ref[d, rows, sl] = c.qd.astype(BF16)
            kd_ref[d, rows, sl] = c.kd.astype(BF16)
            at_ref[jj, d, h] = c.attn.astype(BF16)
            cd_ref[jj, 4 * d + h:4 * d + h + 1, :] = jnp.broadcast_to(c.cd, (1, DH))

    tok = _rows(cb * CHUNK, HW)
    tok2 = pl.BlockSpec((2, cb * CHUNK, HW), lambda i: (0, i, 0))
    mat = pl.BlockSpec((cb, 2, NH, CHUNK, CHUNK), lambda i: (i, 0, 0, 0, 0))
    return pl.pallas_call(
        body, name=name, grid=(n // cb,), in_specs=[tok, tok, tok, _rows(cb * CHUNK, BAP)],
        out_specs=[mat, tok2, tok2, tok2, tok2, mat, pl.BlockSpec((cb, 8, DH), lambda i: (i, 0, 0))],
        out_shape=[_sds((n, 2, NH, CHUNK, CHUNK)), _sds((2, s, HW)), _sds((2, s, HW), BF16), _sds((2, s, HW), BF16),
                   _sds((2, s, HW), BF16), _sds((n, 2, NH, CHUNK, CHUNK), BF16), _sds((n, 8, DH))],
        compiler_params=_cparams(1),
    )(q, k, v, bg)


def _seq_specs(s, order):
    n = s // CHUNK
    cb = min(SEQ_CB, n)
    nb = n // cb
    tb = cb * CHUNK

    def blk(d):
        return (lambda i: i) if order[d] else (lambda i: nb - 1 - i)

    def per_dir(make):
        return [make(d, blk(d)) for d in range(2)]

    tok2 = per_dir(lambda d, f: pl.BlockSpec((1, tb, HW), lambda i: (d, f(i), 0)))
    tok = per_dir(lambda d, f: pl.BlockSpec((tb, HW), lambda i: (f(i), 0)))
    mat = per_dir(lambda d, f: pl.BlockSpec((cb, 1, NH, CHUNK, CHUNK), lambda i: (f(i), d, 0, 0, 0)))
    cds = per_dir(lambda d, f: pl.BlockSpec((cb, 8, DH), lambda i: (f(i), 0, 0)))
    sts = per_dir(lambda d, f: pl.BlockSpec((cb, NH, DH, DH), lambda i: (f(i), 0, 0, 0)))
    dcd = per_dir(lambda d, f: pl.BlockSpec((cb, NH, DH), lambda i: (f(i), 0, 0)))
    return n, cb, nb, tok2, tok, mat, cds, sts, dcd


def _gdn_seq_fwd(u, w, qd, kd, at, cd, name):
    s = u.shape[1]
    n, cb, nb, tok2, tok, mat, cds, sts, _ = _seq_specs(s, (True, False))

    def body(*refs):
        ins = (refs[0:6], refs[6:12])
        outs = (refs[12:15], refs[15:18])
        st = refs[18]

        @pl.when(pl.program_id(0) == 0)
        def _():
            st[...] = jnp.zeros_like(st)

        for j in range(cb):
            items = []
            for d in range(2):
                jj = j if d == 0 else cb - 1 - j
                items += [(d, h, jj, slice(CHUNK * jj, CHUNK * (jj + 1)), _head(h)) for h in range(NH)]
            shs = [st[d, h] for d, h, _, _, _ in items]
            wss = [_dot(ins[d][1][0, rows, sl], sh) for (d, h, jj, rows, sl), sh in zip(items, shs)]
            vns = [ins[d][0][0, rows, sl] - ws for (d, h, jj, rows, sl), ws in zip(items, wss)]
            news = [sh * ins[d][5][jj, 4 * d + h:4 * d + h + 1, :] + _dot_tn(ins[d][3][0, rows, sl], vn)
                    for (d, h, jj, rows, sl), sh, vn in zip(items, shs, vns)]
            for (d, h, jj, rows, sl), sh, vn, new in zip(items, shs, vns, news):
                o_r, s_r, vn_r = outs[d]
                st[d, h] = new
                s_r[jj, h] = sh
                vn_r[rows, sl] = vn
                o_r[rows, sl] = _dot(ins[d][2][0, rows, sl], sh) + _dot(ins[d][4][jj, 0, h], vn)

    in_specs, out_specs, out_shape = [], [], []
    for d in range(2):
        in_specs += [tok2[d]] * 4 + [mat[d], cds[d]]
        out_specs += [tok[d], sts[d], tok[d]]
        out_shape += [_sds((s, HW)), _sds((n, NH, DH, DH)), _sds((s, HW))]
    return pl.pallas_call(
        body, name=name, grid=(nb,), in_specs=in_specs, out_specs=out_specs, out_shape=out_shape,
        scratch_shapes=[pltpu.VMEM((2, NH, DH, DH), F32)], compiler_params=_cparams(1),
    )(u, w, qd, kd, at, cd, u, w, qd, kd, at, cd)


def _gdn_seq_bwd(do, w, qd, kd, at, cd, states, vns, name):
    s = do.shape[0]
    n, cb, nb, tok2, tok, mat, cds, sts, dcd = _seq_specs(s, (False, True))

    def body(*refs):
        ins = (refs[0:8], refs[8:16])
        outs = (refs[16:21], refs[21:26])
        dst = refs[26]

        @pl.when(pl.program_id(0) == 0)
        def _():
            dst[...] = jnp.zeros_like(dst)

        for j in range(cb):
            items = []
            for d in range(2):
                jj = cb - 1 - j if d == 0 else j
                items += [(d, h, jj, slice(CHUNK * jj, CHUNK * (jj + 1)), _head(h)) for h in range(NH)]
            dsns = [dst[d, h] for d, h, _, _, _ in items]
            dohs = [ins[d][0][rows, sl] for d, h, jj, rows, sl in items]
            d_vns = [_dot_tn(ins[d][4][jj, 0, h], doh) + _dot(ins[d][3][0, rows, sl], dsn)
                     for (d, h, jj, rows, sl), doh, dsn in zip(items, dohs, dsns)]
            news = [ins[d][5][jj, 4 * d + h:4 * d + h + 1, :] * dsn + _dot_tn(ins[d][2][0, rows, sl], doh)
                    - _dot_tn(ins[d][1][0, rows, sl], d_vn)
                    for (d, h, jj, rows, sl), doh, dsn, d_vn in zip(items, dohs, dsns, d_vns)]
            for (d, h, jj, rows, sl), doh, dsn, d_vn, new in zip(items, dohs, dsns, d_vns, news):
                dvn_r, dkd_r, dqd_r, dw_r, dcd_r = outs[d]
                sh = ins[d][6][jj, h]
                dst[d, h] = new
                dvn_r[rows, sl] = d_vn
                dkd_r[rows, sl] = _dot_nt(ins[d][7][rows, sl], dsn)
                dqd_r[rows, sl] = _dot_nt(doh, sh)
                dw_r[rows, sl] = -_dot_nt(d_vn, sh)
                d_cd = jnp.sum(jnp.sum(sh * dsn, axis=1, keepdims=True), axis=0, keepdims=True)
                dcd_r[jj, h:h + 1, :] = jnp.broadcast_to(d_cd, (1, DH))

    in_specs, out_specs, out_shape, args = [], [], [], []
    for d in range(2):
        in_specs += [tok[d]] + [tok2[d]] * 3 + [mat[d], cds[d], sts[d], tok[d]]
        args += [do, w, qd, kd, at, cd, states[d], vns[d]]
        out_specs += [tok[d]] * 4 + [dcd[d]]
        out_shape += [_sds((s, HW))] * 4 + [_sds((n, NH, DH))]
    return pl.pallas_call(
        body, name=name, grid=(nb,), in_specs=in_specs, out_specs=out_specs, out_shape=out_shape,
        scratch_shapes=[pltpu.VMEM((2, NH, DH, DH), F32)], compiler_params=_cparams(1),
    )(*args)


def _gdn_local_bwd(q, k, v, bg, tmat, do, vns, seq_grads, name, comm=None):
    s = q.shape[0]
    n = s // CHUNK
    cb = min(LOCAL_CB, n)

    def body(*refs):
        q_ref, k_ref, v_ref, bg_ref, t_ref, do_ref = refs[0:6]
        vn_refs = refs[6:8]
        sg = (refs[8:13], refs[13:18])
        dq_ref, dk_ref, dv_ref, dbg_ref = refs[18:]
        lane = lax.broadcasted_iota(jnp.int32, (CHUNK, BAP), 1)
        rowi = lax.broadcasted_iota(jnp.int32, (CHUNK, 1), 0)
        ones = jnp.ones((CHUNK, DH), F32)
        masks = [_GdnMasks(d) for d in range(2)]
        inst = []
        for jj in range(cb):
            rows = slice(CHUNK * jj, CHUNK * (jj + 1))
            bgv = bg_ref[rows, :]
            qs = [q_ref[rows, _head(h)] for h in range(NH)]
            ks = [k_ref[rows, _head(h)] for h in range(NH)]
            kk = [_dot_nt(ks[h], ks[h]) for h in range(NH)]
            q0 = [_dot_nt(qs[h], ks[h]) for h in range(NH)]
            for d, m in enumerate(masks):
                gcs, gcs_t = _chunk_cumsums(m, bgv)
                for h in range(NH):
                    c = _GdnHead(qs[h], ks[h], v_ref[rows, _head(h)], kk[h], q0[h], bgv, gcs, gcs_t, d, h, m)
                    inst.append((jj, rows, d, h, m, c))
        ni = len(inst)
        cs = [it[-1] for it in inst]
        tms = [t_ref[jj, d, h] for jj, _, d, h, _, _ in inst]
        d_vns = [sg[d][0][rows, _head(h)] for _, rows, d, h, _, _ in inst]
        d_ws = [sg[d][3][rows, _head(h)] for _, rows, d, h, _, _ in inst]
        d_ts = [_dot_nt(d_vns[i], cs[i].vb) + _dot_nt(d_ws[i], cs[i].kg) for i in range(ni)]
        tts = [tm.T for tm in tms]
        xs = [_dot3(tts[i], d_ts[i]) for i in range(ni)]
        d_ls = [jnp.where(inst[i][4].strict, -_dot3(xs[i], tts[i]), 0.0) for i in range(ni)]
        d_attns = [jnp.where(m.incl, _dot_nt(do_ref[rows, _head(h)], vn_refs[d][rows, _head(h)]), 0.0)
                   for _, rows, d, h, m, _ in inst]
        d_vbs = [_dot(tts[i], d_vns[i]) for i in range(ni)]
        d_kgs = [_dot(tts[i], d_ws[i]) for i in range(ni)]
        d_a0s = [d_ls[i] * cs[i].decay for i in range(ni)]
        d_q0s = [d_attns[i] * cs[i].decay for i in range(ni)]
        es = [(d_ls[i] * cs[i].a0 + d_attns[i] * cs[i].q0) * cs[i].decay for i in range(ni)]
        kb_mm = [_dot(d_a0s[i], cs[i].k) for i in range(ni)]
        q_mm = [_dot(d_q0s[i], cs[i].k) for i in range(ni)]
        k_mm = [_dot_tn(d_a0s[i], cs[i].kb) + _dot_tn(d_q0s[i], cs[i].q) for i in range(ni)]
        e_cols = [_dot_exact(ones, es[i], _TN, False)[:, 0:1] for i in range(ni)]
        acc = {}
        d_gcs, d_betas = [], []
        for i, (jj, rows, d, h, m, c) in enumerate(inst):
            sl = _head(h)
            d_kd, d_qd = sg[d][1][rows, sl], sg[d][2][rows, sl]
            d_cd = sg[d][4][jj, h:h + 1, 0:1]
            d_vb, d_kg = d_vbs[i], d_kgs[i]
            d_kb = kb_mm[i] + d_kg * c.eg
            parts = (q_mm[i] + d_qd * c.eg, k_mm[i] + d_kd * c.ek + d_kb * c.beta, d_vb * c.beta)
            acc[jj, h] = [p + a for a, p in zip(acc[jj, h], parts)] if (jj, h) in acc else list(parts)
            s_kd = jnp.sum(d_kd * c.kd, axis=1, keepdims=True)
            d_gc = (jnp.sum(d_kg * c.kg, axis=1, keepdims=True) + jnp.sum(d_qd * c.qd, axis=1, keepdims=True) - s_kd
                    + jnp.sum(es[i], axis=1, keepdims=True) - e_cols[i])
            d_gl = jnp.sum(s_kd, axis=0, keepdims=True) + d_cd * c.cd
            d_gcs.append(d_gc + jnp.where(rowi == m.last, d_gl, 0.0))
            d_betas.append(jnp.sum(d_kb * c.k, axis=1, keepdims=True) + jnp.sum(d_vb * c.v, axis=1, keepdims=True))
        d_gs = [_dot_exact(inst[i][4].tri, d_gcs[i] * ones, _TN, True)[:, 0:1] for i in range(ni)]
        dbg = [jnp.zeros((CHUNK, BAP), F32) for _ in range(cb)]
        for i, (jj, _, d, h, _, _) in enumerate(inst):
            dbg[jj] = dbg[jj] + jnp.where(lane == 4 * d + h, d_betas[i], 0.0) + jnp.where(lane == 8 + 4 * d + h, d_gs[i], 0.0)
        for jj in range(cb):
            rows = slice(CHUNK * jj, CHUNK * (jj + 1))
            for h in range(NH):
                dq_ref[rows, _head(h)], dk_ref[rows, _head(h)], dv_ref[rows, _head(h)] = acc[jj, h]
            dbg_ref[rows, :] = dbg[jj]

    tok = _rows(cb * CHUNK, HW)
    bgs = _rows(cb * CHUNK, BAP)
    mat = pl.BlockSpec((cb, 2, NH, CHUNK, CHUNK), lambda i: (i, 0, 0, 0, 0))
    dcd = pl.BlockSpec((cb, NH, DH), lambda i: (i, 0, 0))
    args = [q, k, v, bg, tmat, do, vns[0], vns[1]]
    in_specs = [tok, tok, tok, bgs, mat, tok, tok, tok]
    for d in range(2):
        args += list(seq_grads[d])
        in_specs += [tok] * 4 + [dcd]
    return _pallas(body, comm, name=name, grid=(n // cb,), in_specs=in_specs, out_specs=[tok, tok, tok, bgs],
                   out_shape=[_sds((s, HW))] * 3 + [_sds((s, BAP))], scratch_shapes=[], args=args)


def _prep_bwd(c_qkv, p_ba, alog_row, dtb_row, dq, dk, dv, dbg, name):
    s = c_qkv.shape[0]
    t = min(256, s)

    def body(cq_ref, pc_ref, alog_ref, dtb_ref, dq_ref, dk_ref, dv_ref, dbg_ref,
             dcq_ref, dpc_ref, dalog_ref, ddtb_ref):
        @pl.when(pl.program_id(0) == 0)
        def _():
            dalog_ref[...] = jnp.zeros_like(dalog_ref)
            ddtb_ref[...] = jnp.zeros_like(ddtb_ref)

        cq = cq_ref[...]
        sq = cq * _sig(cq)
        sg = _silu_grad(cq)
        for h in range(NH):
            sl = slice(DH * h, DH * (h + 1))
            for off, d_ref, scale in ((0, dq_ref, DH ** -0.5), (RGW, dk_ref, 1.0)):
                csl = slice(off + DH * h, off + DH * (h + 1))
                xh = sq[:, csl]
                nrm = lax.rsqrt(jnp.sum(xh * xh, axis=-1, keepdims=True) + EPS)
                y = xh * nrm
                dy = d_ref[:, sl] * scale
                dcq_ref[:, csl] = nrm * (dy - y * jnp.sum(dy * y, axis=-1, keepdims=True)) * sg[:, csl]
        dcq_ref[:, 2 * RGW:] = dv_ref[...] * sg[:, 2 * RGW:]
        pc = pc_ref[...]
        lane = lax.broadcasted_iota(jnp.int32, pc.shape, 1)
        dbg = dbg_ref[...]
        beta = _sig(pc)
        ea = jnp.exp(alog_ref[...])
        z = pc + dtb_ref[...]
        g = -ea * _softplus(z)
        is_g = jnp.logical_and(lane >= 8, lane < 16)
        d_alpha = jnp.where(is_g, dbg * (-ea) * _sig(z), 0.0)
        dpc_ref[...] = jnp.where(lane < 8, dbg * beta * (1.0 - beta), d_alpha)
        dalog_ref[...] += _colsum(jnp.where(is_g, dbg * g, 0.0))
        ddtb_ref[...] += _colsum(d_alpha)

    return pl.pallas_call(
        body, name=name, grid=(s // t,),
        in_specs=[_rows(t, QKVW), _rows(t, BAP), _full((1, BAP)), _full((1, BAP))] + [_rows(t, HW)] * 3 + [_rows(t, BAP)],
        out_specs=[_rows(t, QKVW), _rows(t, BAP), _full((1, BAP)), _full((1, BAP))],
        out_shape=[_sds((s, QKVW)), _sds((s, BAP)), _sds((1, BAP)), _sds((1, BAP))],
        compiler_params=_cparams(1),
    )(c_qkv, p_ba, alog_row, dtb_row, dq, dk, dv, dbg)


def _mix_out_values(hf, hb, gate, of, ob, z, gn):
    hr = hf + hb
    y_rg = hr * _gelu(gate)
    osum = of + ob
    parts = []
    for h in range(NH):
        sl = slice(DH * h, DH * (h + 1))
        oh = osum[:, sl]
        r, ohat = _rms(oh)
        zh = z[:, sl]
        parts.append((r, ohat, zh))
    y_gdn = jnp.concatenate([ohat * gn * (zh * _sig(zh)) for (r, ohat, zh) in parts], axis=1)
    return hr, y_rg, y_gdn, parts


def _outproj(x1, hf, hb, gate, of, ob, z, gn, wout, name):
    s = x1.shape[0]
    t = min(256, s)

    def body(x_ref, hf_ref, hb_ref, gate_ref, of_ref, ob_ref, z_ref, gn_ref, w_ref, xo_ref, y_ref):
        _, y_rg, y_gdn, _ = _mix_out_values(hf_ref[...], hb_ref[...], gate_ref[...], of_ref[...], ob_ref[...],
                                            z_ref[...], gn_ref[...])
        y = jnp.concatenate([y_rg, y_gdn], axis=1).astype(BF16)
        y_ref[...] = y
        xo_ref[...] = x_ref[...] + jnp.dot(y, w_ref[...], preferred_element_type=F32)

    return pl.pallas_call(
        body, name=name, grid=(s // t,),
        in_specs=[_rows(t, D)] + [_rows(t, RGW)] * 6 + [_full((1, DH)), _full((D, D))],
        out_specs=[_rows(t, D), _rows(t, D)], out_shape=[_sds((s, D)), _sds((s, D), BF16)],
        compiler_params=_cparams(1),
    )(x1, hf, hb, gate, of, ob, z, gn, wout)


def _outproj_bwd(dx2, hf, hb, gate, of, ob, z, gn, wout, name):
    s = dx2.shape[0]
    t = min(256, s)

    def body(d_ref, hf_ref, hb_ref, gate_ref, of_ref, ob_ref, z_ref, gn_ref, w_ref,
             dhr_ref, dgate_ref, dos_ref, dz_ref, dgn_ref, db_ref):
        @pl.when(pl.program_id(0) == 0)
        def _():
            dgn_ref[...] = jnp.zeros_like(dgn_ref)

        gate = gate_ref[...]
        gn_v = gn_ref[...]
        hr, _, _, parts = _mix_out_values(hf_ref[...], hb_ref[...], gate, of_ref[...], ob_ref[...], z_ref[...], gn_v)
        dbf = d_ref[...].astype(BF16)
        db_ref[...] = dbf
        dy = _dot_nt(dbf, w_ref[...])
        dyr = dy[:, :RGW]
        dhr_ref[...] = dyr * _gelu(gate)
        dgate_ref[...] = dyr * hr * _gelu_grad(gate)
        dgn = jnp.zeros((1, DH), F32)
        for h, (r, ohat, zh) in enumerate(parts):
            sl = slice(DH * h, DH * (h + 1))
            dyh = dy[:, RGW + DH * h:RGW + DH * (h + 1)]
            sz = zh * _sig(zh)
            dn = dyh * sz
            dz_ref[:, sl] = dyh * ohat * gn_v * _silu_grad(zh)
            dgn = dgn + _colsum(dn * ohat)
            dos_ref[:, sl] = _rms_bwd(dn, ohat, r, gn_v)
        dgn_ref[...] += dgn

    return pl.pallas_call(
        body, name=name, grid=(s // t,),
        in_specs=[_rows(t, D)] + [_rows(t, RGW)] * 6 + [_full((1, DH)), _full((D, D))],
        out_specs=[_rows(t, RGW)] * 4 + [_full((1, DH)), _rows(t, D)],
        out_shape=[_sds((s, RGW))] * 4 + [_sds((1, DH)), _sds((s, D), BF16)],
        compiler_params=_cparams(1),
    )(dx2, hf, hb, gate, of, ob, z, gn, wout)


def _loss_head(x3, target, gain, name):
    s = x3.shape[0]
    t = min(256, s)

    def body(x_ref, t_ref, g_ref, dx_ref, dxh_ref, loss_ref, dg_ref):
        @pl.when(pl.program_id(0) == 0)
        def _():
            loss_ref[...] = jnp.zeros_like(loss_ref)
            dg_ref[...] = jnp.zeros_like(dg_ref)

        r, xh = _rms(x_ref[...])
        gv = g_ref[...]
        err = xh * gv - t_ref[...]
        per_tok = jnp.mean(err * err, axis=-1, keepdims=True)
        loss_ref[...] += 0.5 * jnp.sum(per_tok, axis=0, keepdims=True)
        dy = err * (1.0 / D)
        dg_ref[...] += _colsum(dy * xh)
        dx = _rms_bwd(dy, xh, r, gv)
        dx_ref[...] = dx
        dxh_ref[...] = (0.5 * dx).astype(BF16)

    return pl.pallas_call(
        body, name=name, grid=(s // t,), in_specs=[_rows(t, D), _rows(t, D), _full((1, D))],
        out_specs=[_rows(t, D), _rows(t, D), _full((8, 128)), _full((1, D))],
        out_shape=[_sds((s, D)), _sds((s, D), BF16), _sds((8, 128)), _sds((1, D))], compiler_params=_cparams(1),
    )(x3, target, gain)


def _adamw_math(wv, gv, mv, vv):
    mn = ADAM_B1 * mv + (1.0 - ADAM_B1) * gv
    vn = ADAM_B2 * vv + (1.0 - ADAM_B2) * (gv * gv)
    m_hat = mn / (1.0 - ADAM_B1 ** ADAM_STEP)
    v_hat = vn / (1.0 - ADAM_B2 ** ADAM_STEP)
    return -ADAM_LR * (m_hat / (jnp.sqrt(v_hat) + ADAM_EPS) + ADAM_WD * wv), mn, vn


def _row_tile(r, c):
    tr = r
    while tr * c * 4 > (1 << 20) and tr % 16 == 0:
        tr //= 2
    return tr


def _adamw(w, g, m, v, name):
    r, c = w.shape
    tr = _row_tile(r, c)

    def body(w_ref, g_ref, m_ref, v_ref, d_ref, nm_ref, nv_ref):
        d_ref[...], nm_ref[...], nv_ref[...] = _adamw_math(w_ref[...], g_ref[...], m_ref[...], v_ref[...])

    return pl.pallas_call(
        body, name=name, grid=(r // tr,), in_specs=[_rows(tr, c)] * 4, out_specs=[_rows(tr, c)] * 3,
        out_shape=[_sds((r, c))] * 3, compiler_params=_cparams(1),
    )(w, g, m, v)


def _adamw_halves(w, own, recv, m, v, c_arr, name):
    r, c = w.shape
    h = r // 2
    tr = _row_tile(h, c)
    nh = h // tr

    def body(c_ref, w_ref, own_ref, recv_ref, m_ref, v_ref, g_ref, d_ref, nm_ref, nv_ref):
        first_half = pl.program_id(0) < nh
        use_own = first_half == (c_ref[0] == 0)
        gv = jnp.where(use_own, own_ref[...], recv_ref[...])
        g_ref[...] = gv
        d_ref[...], nm_ref[...], nv_ref[...] = _adamw_math(w_ref[...], gv, m_ref[...], v_ref[...])

    full = pl.BlockSpec((tr, c), lambda i, c_ref: (i, 0))
    half = pl.BlockSpec((tr, c), lambda i, c_ref: (i % nh, 0))
    return pl.pallas_call(
        body, name=name, out_shape=[_sds((r, c))] * 4,
        grid_spec=pltpu.PrefetchScalarGridSpec(
            num_scalar_prefetch=1, grid=(2 * nh,), in_specs=[full, half, half, full, full], out_specs=[full] * 4),
        compiler_params=_cparams(1),
    )(c_arr, w, own, recv, m, v)


def _mesh_pos():
    return lax.axis_index("x"), lax.axis_index("y"), lax.axis_index("c")


def _other_chips(x, y):
    return [(1 - x, y), (x, 1 - y), (1 - x, 1 - y)]


class _Comm:
    def __init__(self, inputs, out_shapes, scratch, start, finish, space=pltpu.HBM):
        self.inputs, self.out_shapes, self.scratch = list(inputs), list(out_shapes), list(scratch)
        self.start, self.finish, self.space = start, finish, space


def _comm_call(comm, name):
    ni, no = len(comm.inputs), len(comm.out_shapes)

    def body(*refs):
        comm.start(refs[:ni], refs[ni:ni + no], refs[ni + no:])
        comm.finish(refs[:ni], refs[ni:ni + no], refs[ni + no:])

    spec = pl.BlockSpec(memory_space=comm.space)
    return list(pl.pallas_call(body, name=name, out_shape=comm.out_shapes, in_specs=[spec] * ni, out_specs=[spec] * no,
                               scratch_shapes=comm.scratch)(*comm.inputs))


def _pallas(body, comm, *, name, grid, in_specs, out_specs, out_shape, scratch_shapes, args):
    params = _cparams(len(grid))
    if comm is None:
        outs = pl.pallas_call(body, name=name, grid=grid, in_specs=in_specs, out_specs=out_specs, out_shape=out_shape,
                              scratch_shapes=scratch_shapes, compiler_params=params)(*args)
        return list(outs), []
    n_in, n_out, n_sc = len(in_specs), len(out_specs), len(scratch_shapes)
    ci, co = len(comm.inputs), len(comm.out_shapes)

    def carried(*refs):
        bounds = [0, n_in, n_in + ci, n_in + ci + n_out, n_in + ci + n_out + co, n_in + ci + n_out + co + n_sc, len(refs)]
        ins, cins, outs, couts, scr, csems = [refs[lo:hi] for lo, hi in zip(bounds[:-1], bounds[1:])]
        ids = [pl.program_id(k) for k in range(len(grid))]
        first = functools.reduce(jnp.logical_and, [i == 0 for i in ids])
        last = functools.reduce(jnp.logical_and, [i == g - 1 for i, g in zip(ids, grid)])

        @pl.when(first)
        def _():
            comm.start(cins, couts, csems)

        body(*ins, *outs, *scr)

        @pl.when(last)
        def _():
            comm.finish(cins, couts, csems)

    hbm = pl.BlockSpec(memory_space=pltpu.HBM)
    outs = pl.pallas_call(
        carried, name=name, grid=grid, in_specs=list(in_specs) + [hbm] * ci, out_specs=list(out_specs) + [hbm] * co,
        out_shape=list(out_shape) + comm.out_shapes, scratch_shapes=list(scratch_shapes) + comm.scratch,
        compiler_params=params)(*args, *comm.inputs)
    return list(outs[:n_out]), list(outs[n_out:])


def _gather_comm(arrays, space, block_rows):
    n_arr = len(arrays)

    def plan(x_refs, out_refs, sems):
        send_sems, recv_sems, local_sems = sems
        x, y, c = _mesh_pos()
        me, sibling = (x, y, c), (x, y, 1 - c)
        chips = _other_chips(x, y)

        def slot(a, px, py, pc):
            return out_refs[a].at[4 * px + 2 * py + pc]

        def copy(a, k, block, to, src=None):
            return pltpu.make_async_remote_copy(
                src_ref=slot(a, *block) if src is None else src, dst_ref=slot(a, *block),
                send_sem=send_sems.at[7 * a + k], recv_sem=recv_sems.at[7 * a + k], device_id=to, device_id_type=MESH)

        srcs = [x_refs[a] if block_rows[a] is None else
                x_refs[a].at[pl.ds(pl.multiple_of(c * block_rows[a], 16), block_rows[a]), :] for a in range(n_arr)]
        local = [pltpu.make_async_copy(srcs[a], slot(a, *me), local_sems.at[a]) for a in range(n_arr)]
        first = []
        for a in range(n_arr):
            first += [copy(a, 1 + j, me, (*chip, c), src=srcs[a]) for j, chip in enumerate(chips)]
            first.append(copy(a, 0, me, sibling, src=srcs[a]))
        return me, sibling, chips, c, copy, local, first

    def start(x_refs, out_refs, sems):
        _, _, _, _, _, local, first = plan(x_refs, out_refs, sems)
        for cp in local + first:
            cp.start()

    def finish(x_refs, out_refs, sems):
        me, sibling, chips, c, copy, local, first = plan(x_refs, out_refs, sems)
        passed = []
        for j, chip in enumerate(chips):
            for a in range(n_arr):
                copy(a, 1 + j, (*chip, c), me).wait_recv()
                fwd = copy(a, 4 + j, (*chip, c), sibling)
                fwd.start()
                passed.append(fwd)
        for a in range(n_arr):
            copy(a, 0, sibling, me).wait_recv()
            for j, chip in enumerate(chips):
                copy(a, 4 + j, (*chip, 1 - c), me).wait_recv()
        for cp in first + passed:
            cp.wait_send()
        for cp in local:
            cp.wait()

    out_shapes = [_sds((8, w.shape[0] if r is None else r) + w.shape[1:], w.dtype) for w, r in zip(arrays, block_rows)]
    scratch = [pltpu.SemaphoreType.DMA((7 * n_arr,)), pltpu.SemaphoreType.DMA((7 * n_arr,)), pltpu.SemaphoreType.DMA((n_arr,))]
    return _Comm(arrays, out_shapes, scratch, start, finish, space)


def _weights_gather_comm(shards):
    return _gather_comm(shards, pltpu.HBM, [w.shape[0] // 2 for w in shards])


def _all_shards(gathered):
    return [o.reshape(NSH, 2 * o.shape[1], o.shape[2]) for o in gathered]


def _gather_small(block, name):
    return _comm_call(_gather_comm([block], pltpu.VMEM, [None]), name)[0]


def _sibling_exchange(gs, name):
    n = len(gs)
    halves = [g.shape[1] // 2 for g in gs]

    def body(*refs):
        g_refs, land_refs = refs[:n], refs[n:2 * n]
        send_sems, recv_sems = refs[2 * n:]
        x, y, c = _mesh_pos()
        copies = []
        for a in range(n):
            h = halves[a]
            for s in range(NSH):
                copies.append(pltpu.make_async_remote_copy(
                    src_ref=g_refs[a].at[s, pl.ds(pl.multiple_of((1 - c) * h, 8), h), :], dst_ref=land_refs[a].at[s],
                    send_sem=send_sems.at[NSH * a + s], recv_sem=recv_sems.at[NSH * a + s],
                    device_id=(x, y, 1 - c), device_id_type=MESH))
        for cp in copies:
            cp.start()
        for cp in copies:
            cp.wait()

    return pl.pallas_call(
        body, name=name, out_shape=[_sds((NSH, h, g.shape[2])) for h, g in zip(halves, gs)],
        in_specs=[pl.BlockSpec(memory_space=pltpu.HBM)] * n, out_specs=[pl.BlockSpec(memory_space=pltpu.HBM)] * n,
        scratch_shapes=[pltpu.SemaphoreType.DMA((NSH * n,)), pltpu.SemaphoreType.DMA((NSH * n,))],
    )(*gs)


def _chip_sum(g, land, c_arr, name):
    _, h, cols = land.shape

    def body(c_ref, g_ref, l_ref, o_ref):
        o_ref[...] = (g_ref[...] + l_ref[...]).astype(BF16)

    return pl.pallas_call(
        body, name=name, out_shape=_sds((NSH, h, cols), BF16),
        grid_spec=pltpu.PrefetchScalarGridSpec(
            num_scalar_prefetch=1, grid=(NSH,),
            in_specs=[pl.BlockSpec((1, h, cols), lambda s, c_ref: (s, c_ref[0], 0)),
                      pl.BlockSpec((1, h, cols), lambda s, c_ref: (s, 0, 0))],
            out_specs=pl.BlockSpec((1, h, cols), lambda s, c_ref: (s, 0, 0))),
        compiler_params=_cparams(1),
    )(c_arr, g, land)


def _scatter_comm(parts):
    n = len(parts)

    def plan(p_refs, land_refs, sems):
        send_sems, recv_sems, local_sems = sems
        x, y, c = _mesh_pos()
        my_chip = 2 * x + y
        local = [pltpu.make_async_copy(p_refs[a].at[my_chip], land_refs[a].at[my_chip], local_sems.at[a]) for a in range(n)]
        copies = []
        for a in range(n):
            for j, (px, py) in enumerate(_other_chips(x, y)):
                copies.append(pltpu.make_async_remote_copy(
                    src_ref=p_refs[a].at[2 * px + py], dst_ref=land_refs[a].at[my_chip],
                    send_sem=send_sems.at[3 * a + j], recv_sem=recv_sems.at[3 * a + j],
                    device_id=(px, py, c), device_id_type=MESH))
        return local, copies

    def start(p_refs, land_refs, sems):
        local, copies = plan(p_refs, land_refs, sems)
        for cp in local + copies:
            cp.start()

    def finish(p_refs, land_refs, sems):
        local, copies = plan(p_refs, land_refs, sems)
        for cp in copies:
            cp.wait()
        for cp in local:
            cp.wait()

    scratch = [pltpu.SemaphoreType.DMA((3 * n,)), pltpu.SemaphoreType.DMA((3 * n,)), pltpu.SemaphoreType.DMA((n,))]
    return _Comm(parts, [_sds(p.shape, BF16) for p in parts], scratch, start, finish)


def _sum_slots(land, name):
    k, r, c = land.shape
    tr = r // 2 if r % 32 == 0 else r

    def body(l_ref, o_ref):
        acc = l_ref[0].astype(F32)
        for i in range(1, k):
            acc = acc + l_ref[i].astype(F32)
        o_ref[...] = acc

    return pl.pallas_call(
        body, name=name, grid=(r // tr,), in_specs=[pl.BlockSpec((k, tr, c), lambda i: (0, i, 0))],
        out_specs=_rows(tr, c), out_shape=_sds((r, c)), compiler_params=_cparams(1),
    )(land)


def _sibling_swap(halves):
    n = len(halves)

    def body(*refs):
        h_refs, out_refs = refs[:n], refs[n:2 * n]
        send_sems, recv_sems = refs[2 * n:]
        x, y, c = _mesh_pos()
        copies = [pltpu.make_async_remote_copy(
            src_ref=h_refs[a], dst_ref=out_refs[a], send_sem=send_sems.at[a], recv_sem=recv_sems.at[a],
            device_id=(x, y, 1 - c), device_id_type=MESH) for a in range(n)]
        for cp in copies:
            cp.start()
        for cp in copies:
            cp.wait()

    return pl.pallas_call(
        body, name="grad_sibling_swap", out_shape=[_sds(h.shape) for h in halves],
        in_specs=[pl.BlockSpec(memory_space=pltpu.HBM)] * n, out_specs=[pl.BlockSpec(memory_space=pltpu.HBM)] * n,
        scratch_shapes=[pltpu.SemaphoreType.DMA((n,)), pltpu.SemaphoreType.DMA((n,))],
    )(*halves)


def _pad_rows(v, width):
    flat = v.reshape(-1)
    rows = -(-flat.shape[0] // width)
    rows = -(-rows // 8) * 8
    return jnp.pad(flat, (0, rows * width - flat.shape[0])).reshape(rows, width)


def _size(shape):
    n = 1
    for dim in shape:
        n *= dim
    return n


def _row_pack(arrs):
    pieces = []
    for a in arrs:
        rows = -(-a.size // D)
        pieces.append(jnp.pad(a.reshape(-1), (0, rows * D - a.size)).reshape(rows, D))
    total = sum(p.shape[0] for p in pieces)
    if total % 8:
        pieces.append(jnp.zeros((8 - total % 8, D), F32))
    return jnp.concatenate(pieces, axis=0)


def _row_unpack(packed, shapes):
    out, r0 = [], 0
    for shp in shapes:
        n = _size(shp)
        rows = -(-n // D)
        out.append(packed[r0:r0 + rows].reshape(-1)[:n].reshape(shp))
        r0 += rows
    return out


def _block_diag(w):
    eye = jnp.eye(8, dtype=w.dtype)
    return (w[:, :, None, :] * eye[:, None, :, None]).reshape(RGW, RGW)


def _diag_blocks(dense):
    r = dense.reshape(8, 64, 8, 64)
    return jnp.stack([r[n, :, n, :] for n in range(8)])


def _lane_row(v8):
    return jnp.zeros((1, BAP), F32).at[0, 8:16].set(v8.reshape(8))


def _reduce_parts(gs, names, c_arr, tag):
    lands = _sibling_exchange(gs, "grad_sibling_exchange_" + tag)
    return [_chip_sum(g, l, c_arr, "chip_sum_" + n) for g, l, n in zip(gs, lands, names)]


def _local_step(x, target, sw, ffn1_w, later_shards, c_arr):
    (g1, gmix, rg_cw8, rg_cb, wgates, gbias, lam_row, gdn_cw8, alog_row, dtb_row, gn, g2, gfin) = sw
    wg1, wu1, wd1 = ffn1_w

    (x1, a1, b1, fb1), gathered = _ffn_fwd(x, g1, wg1, wu1, wd1, "ffn1_fwd", comm=_weights_gather_comm(later_shards))
    win_sh, wout_sh, wg2, wu2, wd2 = _all_shards(gathered)
    w_in_full = jnp.transpose(win_sh, (1, 0, 2)).reshape(D, NSH * INSH)
    wout = wout_sh.reshape(D, D)
    w_in_groups = (w_in_full[:, 0:512], w_in_full[:, 512:1024], w_in_full[:, 1024:2560], w_in_full[:, 2560:3072],
                   jnp.pad(w_in_full[:, 3072:3088], ((0, 0), (0, BAP - BAW))))
    h2, p_rgx, p_gate, p_qkv, p_z, p_ba = _inproj(x1, gmix, w_in_groups, "in_proj")
    c_rg = _conv(p_rgx, rg_cw8, rg_cb, "rg_conv")
    c_qkv = _conv(p_qkv, gdn_cw8, jnp.zeros((1, QKVW), F32), "gdn_conv")
    a0, bb0, a1s, bb1, q, k, v, bg = _mix_prep(c_rg, c_qkv, p_ba, wgates, gbias, lam_row, alog_row, dtb_row, "mix_prep")
    hf, hb = _scan_pair(a0, bb0, a1s, bb1, False, "rg_scan")
    tmat, gu, gw, gqd, gkd, gat, gcd = _gdn_local_fwd(q, k, v, bg, "gdn_local_fwd")
    of, s0, vn0, ob, s1, vn1 = _gdn_seq_fwd(gu, gw, gqd, gkd, gat, gcd, "gdn_seq_fwd")
    x2, ymix = _outproj(x1, hf, hb, p_gate, of, ob, p_z, gn, wout, "out_proj")
    (x3, a2, b2, fb2), _ = _ffn_fwd(x2, g2, wg2, wu2, wd2, "ffn2_fwd")
    dx3, dob2, loss_blk, d_gfin = _loss_head(x3, target, gfin, "loss_head")

    dx2, d_g2, hb2, dab2, dbb2 = _ffn_bwd(x2, dx3, dob2, g2, a2, b2, wg2, wu2, wd2, "ffn2_bwd")
    d_ffn2 = [_tn(dab2, hb2, "ffn2_dwg"), _tn(dbb2, hb2, "ffn2_dwu"), _tn(fb2, dob2, "ffn2_dwd")]
    parts_ffn2 = _reduce_parts(d_ffn2, _BIG_NAMES[5:8], c_arr, "ffn2")

    d_hr, d_gate, d_os, d_z, d_gn, dx2b = _outproj_bwd(dx2, hf, hb, p_gate, of, ob, p_z, gn, wout, "out_proj_bwd")
    d_wout = _tn(ymix, dx2b, "dw_out")[0]

    lam1, lam0 = _scan_pair(a1s, d_hr, a0, d_hr, True, "rg_scan_bwd")
    d_xc, d_pre, xcb, d_gbias, d_lam = _gates_bwd(c_rg, wgates, gbias, lam_row, lam0, lam1, hf, hb, "rg_gates_bwd")
    d_wgates = _tn(xcb, d_pre, "dw_gates")[0]
    d_prgx, d_rgcw8, d_rgcb = _conv_bwd(p_rgx, d_xc, rg_cw8, "rg_conv_bwd")

    sg = _gdn_seq_bwd(d_os, gw, gqd, gkd, gat, gcd, (s0, s1), (vn0, vn1), "gdn_seq_bwd")
    (dq, dk, dv, dbg), lands_ffn2 = _gdn_local_bwd(q, k, v, bg, tmat, d_os, (vn0, vn1), (sg[0:5], sg[5:10]), "gdn_local_bwd",
                                                  comm=_scatter_comm(parts_ffn2))
    d_cqkv, d_pba, d_alog, d_dtb = _prep_bwd(c_qkv, p_ba, alog_row, dtb_row, dq, dk, dv, dbg, "gdn_prep_bwd")
    d_pqkv, d_gdncw8, _ = _conv_bwd(p_qkv, d_cqkv, gdn_cw8, "gdn_conv_bwd")

    dps = (d_prgx, d_gate, d_pqkv, d_z, d_pba)
    dx1, dob1, d_gmix = _inproj_bwd(x1, dx2, gmix, dps, w_in_groups, "in_proj_bwd")
    d_win_groups = [_tn(h2, dp, "dw_in_%d" % i)[0] for i, dp in enumerate(dps)]
    d_win = jnp.concatenate(d_win_groups[:4] + [d_win_groups[4][:, :BAW]], axis=1)
    d_mix = [jnp.transpose(d_win.reshape(D, NSH, INSH), (1, 0, 2)), d_wout.reshape(NSH, OUTSH, D)]
    parts_mix = _reduce_parts(d_mix, _BIG_NAMES[3:5], c_arr, "mix")

    gx, d_g1, hb1, dab1, dbb1 = _ffn_bwd(x, dx1, dob1, g1, a1, b1, wg1, wu1, wd1, "ffn1_bwd")
    d_wg1, lands_mix = _tn(dab1, hb1, "ffn1_dwg", comm=_scatter_comm(parts_mix))
    parts_wg1 = _reduce_parts([d_wg1], _BIG_NAMES[0:1], c_arr, "ffn1_gate")
    d_wu1, lands_wg1 = _tn(dbb1, hb1, "ffn1_dwu", comm=_scatter_comm(parts_wg1))
    parts_wu1 = _reduce_parts([d_wu1], _BIG_NAMES[1:2], c_arr, "ffn1_up")
    d_wd1, lands_wu1 = _tn(fb1, dob1, "ffn1_dwd", comm=_scatter_comm(parts_wu1))
    parts_wd1 = _reduce_parts([d_wd1], _BIG_NAMES[2:3], c_arr, "ffn1_down")
    lands_ffn1 = lands_wg1 + lands_wu1 + _comm_call(_scatter_comm(parts_wd1), "grad_chip_scatter_ffn1_down")

    halves = [_sum_slots(l, "sum_chips_" + n) for l, n in zip(lands_ffn1 + lands_mix + lands_ffn2, _BIG_NAMES)]
    small = dict(
        ffn1_norm=d_g1, mix_norm=d_gmix, rg_conv_w=d_rgcw8[:4], rg_conv_b=d_rgcb,
        rg_gate_a_w=jnp.stack([_diag_blocks(d_wgates[:, RGW * i:RGW * (i + 1)]) for i in (0, 1)]),
        rg_gate_x_w=jnp.stack([_diag_blocks(d_wgates[:, RGW * i:RGW * (i + 1)]) for i in (2, 3)]),
        rg_gate_a_b=d_gbias[0, :2 * RGW].reshape(2, RGW), rg_gate_x_b=d_gbias[0, 2 * RGW:].reshape(2, RGW),
        rg_lambda=d_lam.reshape(2, RGW), gdn_conv_w=d_gdncw8[:4],
        gdn_a_log=d_alog[0, 8:16].reshape(2, NH), gdn_dt_bias=d_dtb[0, 8:16].reshape(2, NH),
        gdn_norm=d_gn, ffn2_norm=d_g2, final_norm=d_gfin)
    return loss_blk, gx, halves, small


_SMALL_NAMES = ("ffn1_norm", "mix_norm", "rg_conv_w", "rg_conv_b", "rg_gate_a_w", "rg_gate_a_b", "rg_gate_x_w",
                "rg_gate_x_b", "rg_lambda", "gdn_conv_w", "gdn_a_log", "gdn_dt_bias", "gdn_norm", "ffn2_norm", "final_norm")
_SMALL_SHARDED = dict(rg_conv_w=128, rg_gate_a_b=128, rg_gate_x_b=128, rg_lambda=128, gdn_conv_w=384)
_OUT_ORDER = ("ffn1_norm", "ffn1_w_gate", "ffn1_w_up", "ffn1_w_down", "mix_norm", "w_in", "w_out", "rg_conv_w", "rg_conv_b",
              "rg_gate_a_w", "rg_gate_a_b", "rg_gate_x_w", "rg_gate_x_b", "rg_lambda", "gdn_conv_w", "gdn_a_log",
              "gdn_dt_bias", "gdn_norm", "ffn2_norm", "ffn2_w_gate", "ffn2_w_up", "ffn2_w_down", "final_norm")
_BIG_NAMES = ("ffn1_w_gate", "ffn1_w_up", "ffn1_w_down", "w_in", "w_out", "ffn2_w_gate", "ffn2_w_up", "ffn2_w_down")
_TRANSPOSED = ("ffn1_w_gate", "ffn1_w_up", "ffn2_w_gate", "ffn2_w_up")


def kernel(x, ffn1_norm, ffn1_w_gate, ffn1_w_up, ffn1_w_down, mix_norm, w_in, w_out, rg_conv_w, rg_conv_b, rg_gate_a_w, rg_gate_a_b, rg_gate_x_w, rg_gate_x_b, rg_lambda, gdn_conv_w, gdn_a_log, gdn_dt_bias, gdn_norm, ffn2_norm, ffn2_w_gate, ffn2_w_up, ffn2_w_down, final_norm, loss_target, m_ffn1_norm, m_ffn1_w_gate, m_ffn1_w_up, m_ffn1_w_down, m_mix_norm, m_w_in, m_w_out, m_rg_conv_w, m_rg_conv_b, m_rg_gate_a_w, m_rg_gate_a_b, m_rg_gate_x_w, m_rg_gate_x_b, m_rg_lambda, m_gdn_conv_w, m_gdn_a_log, m_gdn_dt_bias, m_gdn_norm, m_ffn2_norm, m_ffn2_w_gate, m_ffn2_w_up, m_ffn2_w_down, m_final_norm, v_ffn1_norm, v_ffn1_w_gate, v_ffn1_w_up, v_ffn1_w_down, v_mix_norm, v_w_in, v_w_out, v_rg_conv_w, v_rg_conv_b, v_rg_gate_a_w, v_rg_gate_a_b, v_rg_gate_x_w, v_rg_gate_x_b, v_rg_lambda, v_gdn_conv_w, v_gdn_a_log, v_gdn_dt_bias, v_gdn_norm, v_ffn2_norm, v_ffn2_w_gate, v_ffn2_w_up, v_ffn2_w_down, v_final_norm):
    args = dict(locals())
    w = {n: args[n] for n in _OUT_ORDER}
    mom = {n: args["m_" + n] for n in _OUT_ORDER}
    var = {n: args["v_" + n] for n in _OUT_ORDER}
    xi, yi, ci = _mesh_pos()
    shard = 2 * xi + yi

    big_bf16 = [w[n][0].astype(BF16) for n in _BIG_NAMES]
    ffn1_w = _all_shards(_comm_call(_weights_gather_comm(big_bf16[0:3]), "gather_ffn1_weights"))
    sm_local = _pad_rows(jnp.concatenate([w[n][0].reshape(-1) for n in _SMALL_SHARDED]), 128)
    sm_all = _gather_small(sm_local, "gather_small_weights")[0::2].reshape(NSH, -1)
    sm_full, off = {}, 0
    for n, wd_ in _SMALL_SHARDED.items():
        rows = w[n].shape[1]
        piece = sm_all[:, off:off + rows * wd_].reshape(NSH, rows, wd_)
        sm_full[n] = jnp.transpose(piece, (1, 0, 2)).reshape(rows, NSH * wd_)
        off += rows * wd_

    wa, wx = rg_gate_a_w[0], rg_gate_x_w[0]
    wgates = jnp.concatenate([_block_diag(wa[0]), _block_diag(wa[1]), _block_diag(wx[0]), _block_diag(wx[1])],
                             axis=1).astype(BF16)
    gbias = jnp.concatenate([sm_full["rg_gate_a_b"].reshape(1, -1), sm_full["rg_gate_x_b"].reshape(1, -1)], axis=1)
    sw = (ffn1_norm, mix_norm, jnp.pad(sm_full["rg_conv_w"], ((0, 4), (0, 0))), rg_conv_b, wgates, gbias,
          sm_full["rg_lambda"].reshape(1, -1), jnp.pad(sm_full["gdn_conv_w"], ((0, 4), (0, 0))), _lane_row(gdn_a_log),
          _lane_row(gdn_dt_bias), gdn_norm, ffn2_norm, final_norm.reshape(1, D))
    c_arr = ci.reshape(1).astype(jnp.int32)

    loss_blk, gx, halves, small = _local_step(x[0], loss_target[0], sw, ffn1_w, big_bf16[3:], c_arr)
    loss = lax.psum(loss_blk[0, 0], ("x", "y", "c"))
    grads = {}

    sm_grad = _row_pack([small[n] for n in _SMALL_NAMES])
    sm_sum = _sum_slots(_gather_small(sm_grad, "gather_small_grads"), "small_grad_sum")
    for n, g in zip(_SMALL_NAMES, _row_unpack(sm_sum, [small[n].shape for n in _SMALL_NAMES])):
        if n in _SMALL_SHARDED:
            wd_ = _SMALL_SHARDED[n]
            g = lax.dynamic_slice_in_dim(g, shard * wd_, wd_, axis=1)
        grads[n] = g.reshape(w[n].shape)

    delta, new_m, new_v = {}, {}, {}
    for n, own, recv in zip(_BIG_NAMES, halves, _sibling_swap(halves)):
        to2d = jnp.transpose if n in _TRANSPOSED else (lambda t: t)
        outs4 = _adamw_halves(to2d(w[n][0]), own, recv, to2d(mom[n][0]), to2d(var[n][0]), c_arr, "adamw_" + n)
        grads[n], delta[n], new_m[n], new_v[n] = [to2d(o)[None] for o in outs4]
    packs = [_row_pack([t[n] for n in _SMALL_NAMES]) for t in (w, grads, mom, var)]
    sm_shapes = [w[n].shape for n in _SMALL_NAMES]
    for dst, src in zip((delta, new_m, new_v), _adamw(*packs, "adamw_small")):
        for n, val in zip(_SMALL_NAMES, _row_unpack(src, sm_shapes)):
            dst[n] = val

    outs = [loss, gx[None]]
    for group in (grads, delta, new_m, new_v):
        outs += [group[n] for n in _OUT_ORDER]
    return tuple(outs)
```

```python
import functools

import jax
import jax.numpy as jnp
from jax import lax
from jax.experimental import pallas as pl
from jax.experimental.pallas import tpu as pltpu

F32 = jnp.float32
BF16 = jnp.bfloat16
EPS = 1e-6
D = 1024
NSH = 4
FSH = 704
RGW = 512
QKVW = 1536
ZW = 512
BAW = 16
BAP = 128
INSH = 772
OUTSH = 256
CHUNK = 64
NH = 4
DH = 128
RG_C = 8.0
VMEM_LIMIT = 52 * 1024 * 1024
MESH = pl.DeviceIdType.MESH

ADAM_LR = 0.001
ADAM_B1 = 0.9
ADAM_B2 = 0.999
ADAM_EPS = 1e-08
ADAM_WD = 0.01
ADAM_STEP = 10


def _cparams(n_grid):
    return pltpu.CompilerParams(dimension_semantics=("arbitrary",) * n_grid, vmem_limit_bytes=VMEM_LIMIT)


def _sig(x):
    return 0.5 + 0.5 * jnp.tanh(0.5 * x)


def _sig_pos(x):
    return 1.0 / (1.0 + jnp.exp(-x))


def _softplus(x):
    return jnp.maximum(x, 0.0) + jnp.log(1.0 + jnp.exp(-jnp.abs(x)))


def _neg_expm1(y):
    series = -y * (1.0 + y * (0.5 + y * (1.0 / 6 + y * (1.0 / 24 + y * (1.0 / 120 + y * (1.0 / 720 + y / 5040))))))
    return jnp.where(y > -0.3, series, 1.0 - jnp.exp(y))


_GELU_C = 0.7978845608028654


def _gelu(x):
    t = jnp.tanh(_GELU_C * (x + 0.044715 * x * x * x))
    return 0.5 * x * (1.0 + t)


def _gelu_grad(x):
    t = jnp.tanh(_GELU_C * (x + 0.044715 * x * x * x))
    return 0.5 * (1.0 + t) + 0.5 * x * (1.0 - t * t) * _GELU_C * (1.0 + 3 * 0.044715 * x * x)


def _silu_grad(x):
    s = _sig(x)
    return s * (1.0 + x * (1.0 - s))


def _dot(a, b):
    return jnp.dot(a.astype(BF16), b.astype(BF16), preferred_element_type=F32)


def _dot_nt(a, b):
    return lax.dot_general(a.astype(BF16), b.astype(BF16), (((1,), (1,)), ((), ())), preferred_element_type=F32)


def _dot_tn(a, b):
    return lax.dot_general(a.astype(BF16), b.astype(BF16), (((0,), (0,)), ((), ())), preferred_element_type=F32)


_NN = ((1,), (0,))
_NT = ((1,), (1,))
_TN = ((0,), (0,))


def _dg(a, b, dims):
    return lax.dot_general(a, b, (dims, ((), ())), preferred_element_type=F32)


def _split2(a):
    hi = a.astype(BF16)
    return hi, (a - hi.astype(F32)).astype(BF16)


def _dot3(a, b, dims=_NN):
    ah, al = _split2(a)
    bh, bl = _split2(b)
    return _dg(ah, bh, dims) + _dg(ah, bl, dims) + _dg(al, bh, dims)


def _dot_exact(e, x, dims, e_is_lhs):
    x0 = x.astype(BF16)
    r = x - x0.astype(F32)
    x1 = r.astype(BF16)
    x2 = (r - x1.astype(F32)).astype(BF16)
    eb = e.astype(BF16)
    if e_is_lhs:
        return _dg(eb, x0, dims) + _dg(eb, x1, dims) + _dg(eb, x2, dims)
    return _dg(x0, eb, dims) + _dg(x1, eb, dims) + _dg(x2, eb, dims)


def _rms(xv):
    r = lax.rsqrt(jnp.mean(xv * xv, axis=-1, keepdims=True) + EPS)
    return r, xv * r


def _rms_bwd(dy, xh, r, gain):
    dxh = dy * gain
    return r * (dxh - xh * jnp.mean(dxh * xh, axis=-1, keepdims=True))


def _colsum(v):
    return jnp.sum(v, axis=0, keepdims=True)


def _rows(t, c):
    return pl.BlockSpec((t, c), lambda i: (i, 0))


def _full(shape):
    n = len(shape)
    return pl.BlockSpec(shape, lambda i: (0,) * n)


def _sds(shape, dtype=F32):
    return jax.ShapeDtypeStruct(shape, dtype)


def _ffn_fwd(x, gain, wg, wu, wd, name, comm=None):
    s = x.shape[0]
    tm = min(512, s)

    def body(x_ref, g_ref, wg_ref, wu_ref, wd_ref, xo_ref, ga_ref, gb_ref, f_ref, h_sc, acc):
        j = pl.program_id(1)

        @pl.when(j == 0)
        def _():
            _, xh = _rms(x_ref[...])
            h_sc[...] = (xh * g_ref[...]).astype(BF16)
            acc[...] = jnp.zeros_like(acc)

        h = h_sc[...]

        a = jnp.dot(h, wg_ref[0], preferred_element_type=F32)
        b = jnp.dot(h, wu_ref[0], preferred_element_type=F32)
        sa = _sig(a)
        silu = a * sa
        fv = silu * b
        f = fv.astype(BF16)
        f_ref[0] = f
        ga_ref[0] = (sa * b + fv * (1.0 - sa)).astype(BF16)
        gb_ref[0] = silu.astype(BF16)
        acc[...] += jnp.dot(f, wd_ref[0], preferred_element_type=F32)

        @pl.when(j == NSH - 1)
        def _():
            xo_ref[...] = x_ref[...] + 0.5 * acc[...]

    return _pallas(
        body, comm, name=name, grid=(s // tm, NSH),
        in_specs=[pl.BlockSpec((tm, D), lambda i, j: (i, 0)), pl.BlockSpec((1, D), lambda i, j: (0, 0)),
                  pl.BlockSpec((1, D, FSH), lambda i, j: (j, 0, 0)), pl.BlockSpec((1, D, FSH), lambda i, j: (j, 0, 0)),
                  pl.BlockSpec((1, FSH, D), lambda i, j: (j, 0, 0))],
        out_specs=[pl.BlockSpec((tm, D), lambda i, j: (i, 0))] + [pl.BlockSpec((1, tm, FSH), lambda i, j: (j, i, 0))] * 3,
        out_shape=[_sds((s, D))] + [_sds((NSH, s, FSH), BF16)] * 3,
        scratch_shapes=[pltpu.VMEM((tm, D), BF16), pltpu.VMEM((tm, D), F32)],
        args=(x, gain, wg, wu, wd))


def _ffn_bwd(x, dout, do, gain, ga, gb, wg, wu, wd, name, comm=None):
    s = x.shape[0]
    tm = min(512, s)

    def hidden(do_ref, ga_ref, gb_ref, wd_ref, da_ref, db_ref):
        df = _dot_nt(do_ref[...], wd_ref[0])
        da_ref[0] = (df * ga_ref[0].astype(F32)).astype(BF16)
        db_ref[0] = (df * gb_ref[0].astype(F32)).astype(BF16)

    th = min(1024, s)
    tok = pl.BlockSpec((th, D), lambda i, j: (i, 0))
    sh = pl.BlockSpec((1, th, FSH), lambda i, j: (j, i, 0))
    (da, db), carried = _pallas(
        hidden, comm, name=name + "_hidden", grid=(s // th, NSH),
        in_specs=[tok, sh, sh, pl.BlockSpec((1, FSH, D), lambda i, j: (j, 0, 0))], out_specs=[sh, sh],
        out_shape=[_sds((NSH, s, FSH), BF16)] * 2, scratch_shapes=[], args=(do, ga, gb, wd))

    def inputs(x_ref, d_ref, g_ref, da_ref, db_ref, wg_ref, wu_ref, dx_ref, dg_ref, h_ref):
        @pl.when(pl.program_id(0) == 0)
        def _():
            dg_ref[...] = jnp.zeros_like(dg_ref)

        dh = jnp.zeros((tm, D), F32)
        for j in range(NSH):
            dh = dh + _dot_nt(da_ref[j], wg_ref[j]) + _dot_nt(db_ref[j], wu_ref[j])
        r, xh = _rms(x_ref[...])
        gv = g_ref[...]
        h_ref[...] = (xh * gv).astype(BF16)
        dg_ref[...] += _colsum(dh * xh)
        dx_ref[...] = d_ref[...] + _rms_bwd(dh, xh, r, gv)

    grads = pl.BlockSpec((NSH, tm, FSH), lambda i: (0, i, 0))
    resident = pl.BlockSpec((NSH, D, FSH), lambda i: (0, 0, 0), pipeline_mode=pl.Buffered(1))
    dx, dg, h = pl.pallas_call(
        inputs, name=name + "_input", grid=(s // tm,),
        in_specs=[_rows(tm, D), _rows(tm, D), _full((1, D)), grads, grads, resident, resident],
        out_specs=[_rows(tm, D), _full((1, D)), _rows(tm, D)],
        out_shape=[_sds((s, D)), _sds((1, D)), _sds((s, D), BF16)], compiler_params=_cparams(1),
    )(x, dout, gain, da, db, wg, wu)
    return dx, dg, h, da, db, carried


def _tn(a, b, name, comm=None):
    a_g = a.ndim == 3
    b_g = b.ndim == 3
    g = a.shape[0] if a_g else (b.shape[0] if b_g else 1)
    s, k = a.shape[-2:]
    n = b.shape[-1]
    ts = min(2048 if b.dtype == BF16 else 1024, s)

    def body(a_ref, b_ref, o_ref):
        @pl.when(pl.program_id(1) == 0)
        def _():
            o_ref[...] = jnp.zeros_like(o_ref)

        av = a_ref[0] if a_g else a_ref[...]
        bv = b_ref[0] if b_g else b_ref[...]
        o_ref[0] += _dot_tn(av, bv)

    a_spec = pl.BlockSpec((1, ts, k), lambda gi, si: (gi, si, 0)) if a_g else pl.BlockSpec((ts, k), lambda gi, si: (si, 0))
    b_spec = pl.BlockSpec((1, ts, n), lambda gi, si: (gi, si, 0)) if b_g else pl.BlockSpec((ts, n), lambda gi, si: (si, 0))
    outs, carried = _pallas(body, comm, name=name, grid=(g, s // ts), in_specs=[a_spec, b_spec],
                            out_specs=[pl.BlockSpec((1, k, n), lambda gi, si: (gi, 0, 0))], out_shape=[_sds((g, k, n))],
                            scratch_shapes=[], args=(a, b))
    return outs[0] if comm is None else (outs[0], carried)


_P_WIDTHS = (RGW, RGW, QKVW, ZW, BAP)


def _inproj(x1, gain, ws, name):
    s = x1.shape[0]
    tm = min(256, s)

    def body(x_ref, g_ref, *refs):
        w_refs = refs[:5]
        h_ref = refs[5]
        p_refs = refs[6:]
        _, xh = _rms(x_ref[...])
        h = (xh * g_ref[...]).astype(BF16)
        h_ref[...] = h
        for w_ref, p_ref in zip(w_refs, p_refs):
            p_ref[...] = jnp.dot(h, w_ref[...], preferred_element_type=F32)

    return pl.pallas_call(
        body, name=name, grid=(s // tm,),
        in_specs=[_rows(tm, D), _full((1, D))] + [_full((D, w)) for w in _P_WIDTHS],
        out_specs=[_rows(tm, D)] + [_rows(tm, w) for w in _P_WIDTHS],
        out_shape=[_sds((s, D), BF16)] + [_sds((s, w)) for w in _P_WIDTHS],
        compiler_params=_cparams(1),
    )(x1, gain, *ws)


def _inproj_bwd(x1, dx2, gain, dps, ws, name):
    s = x1.shape[0]
    tm = min(256, s)

    def body(x_ref, d_ref, g_ref, *refs):
        dp_refs = refs[:5]
        w_refs = refs[5:10]
        dx_ref, dxh_ref, dg_ref = refs[10:]

        @pl.when(pl.program_id(0) == 0)
        def _():
            dg_ref[...] = jnp.zeros_like(dg_ref)

        dh = jnp.zeros((tm, D), F32)
        for dp_ref, w_ref in zip(dp_refs, w_refs):
            dh = dh + _dot_nt(dp_ref[...], w_ref[...])
        r, xh = _rms(x_ref[...])
        dg_ref[...] += _colsum(dh * xh)
        dx = d_ref[...] + _rms_bwd(dh, xh, r, g_ref[...])
        dx_ref[...] = dx
        dxh_ref[...] = (0.5 * dx).astype(BF16)

    return pl.pallas_call(
        body, name=name, grid=(s // tm,),
        in_specs=[_rows(tm, D), _rows(tm, D), _full((1, D))] + [_rows(tm, w) for w in _P_WIDTHS]
        + [_full((D, w)) for w in _P_WIDTHS],
        out_specs=[_rows(tm, D), _rows(tm, D), _full((1, D))],
        out_shape=[_sds((s, D)), _sds((s, D), BF16), _sds((1, D))],
        compiler_params=_cparams(1),
    )(x1, dx2, gain, *dps, *ws)


def _halo_specs(s, t, c):
    nb8 = s // 8
    tb = t // 8
    prev = pl.BlockSpec((8, c), lambda i: (jnp.maximum(i * tb - 1, 0), 0))
    nxt = pl.BlockSpec((8, c), lambda i: (jnp.minimum((i + 1) * tb, nb8 - 1), 0))
    return prev, nxt


def _edge_masks(nb):
    i = pl.program_id(0)
    return jnp.where(i > 0, 1.0, 0.0).astype(F32), jnp.where(i < nb - 1, 1.0, 0.0).astype(F32)


def _shifted(xx, off, t):
    n = t + 16
    sh = (-off) % n
    rolled = xx if sh == 0 else pltpu.roll(xx, sh, 0)
    return rolled[8:8 + t]


def _conv(x, w8, bias, name):
    s, c = x.shape
    t = min(256, s)
    nb = s // t

    def body(x_ref, xp_ref, xn_ref, w_ref, b_ref, o_ref):
        pm, nm = _edge_masks(nb)
        for c0 in range(0, c, 512):
            cols = slice(c0, c0 + 512)
            xx = jnp.concatenate([xp_ref[:, cols] * pm, x_ref[:, cols], xn_ref[:, cols] * nm], axis=0)
            acc = jnp.zeros((t, 512), F32) + b_ref[:, cols]
            for j in range(4):
                acc = acc + w_ref[j:j + 1, cols] * _shifted(xx, j - 2, t)
            o_ref[:, cols] = acc

    prev, nxt = _halo_specs(s, t, c)
    return pl.pallas_call(
        body, name=name, grid=(nb,),
        in_specs=[_rows(t, c), prev, nxt, _full((8, c)), _full((1, c))],
        out_specs=_rows(t, c), out_shape=_sds((s, c)), compiler_params=_cparams(1),
    )(x, x, x, w8, bias)


def _conv_bwd(x, dc, w8, name):
    s, c = x.shape
    t = min(256, s)
    nb = s // t

    def body(x_ref, d_ref, dp_ref, dn_ref, w_ref, dx_ref, dw_ref, db_ref):
        @pl.when(pl.program_id(0) == 0)
        def _():
            dw_ref[...] = jnp.zeros_like(dw_ref)
            db_ref[...] = jnp.zeros_like(db_ref)

        pm, nm = _edge_masks(nb)
        for c0 in range(0, c, 512):
            cols = slice(c0, c0 + 512)
            dd = jnp.concatenate([dp_ref[:, cols] * pm, d_ref[:, cols], dn_ref[:, cols] * nm], axis=0)
            xv = x_ref[:, cols]
            acc = jnp.zeros((t, 512), F32)
            for j in range(4):
                dsh = _shifted(dd, 2 - j, t)
                acc = acc + w_ref[j:j + 1, cols] * dsh
                dw_ref[j:j + 1, cols] += _colsum(dsh * xv)
            dx_ref[:, cols] = acc
            db_ref[:, cols] += _colsum(d_ref[:, cols])

    prev, nxt = _halo_specs(s, t, c)
    return pl.pallas_call(
        body, name=name, grid=(nb,),
        in_specs=[_rows(t, c), _rows(t, c), prev, nxt, _full((8, c))],
        out_specs=[_rows(t, c), _full((8, c)), _full((1, c))],
        out_shape=[_sds((s, c)), _sds((8, c)), _sds((1, c))], compiler_params=_cparams(1),
    )(x, dc, dc, dc, w8)


def _rg_gates(xc, pre, lam_row):
    sp8 = RG_C * _softplus(-lam_row)
    out = []
    for d in range(2):
        r = _sig_pos(pre[:, RGW * d:RGW * (d + 1)])
        gi = _sig(pre[:, 2 * RGW + RGW * d:2 * RGW + RGW * (d + 1)])
        la = -r * sp8[:, RGW * d:RGW * (d + 1)]
        a = jnp.exp(la)
        mult = jnp.sqrt(_neg_expm1(2.0 * la))
        out.append((r, gi, a, mult))
    return out


def _mix_prep(c_rg, c_qkv, p_ba, wgates, gbias, lam_row, alog_row, dtb_row, name):
    s = c_rg.shape[0]
    t = min(256, s)

    def body(xc_ref, cq_ref, pc_ref, wg_ref, gb_ref, lam_ref, alog_ref, dtb_ref,
             a0_ref, b0_ref, a1_ref, b1_ref, q_ref, k_ref, v_ref, bg_ref):
        xc = xc_ref[...]
        pre = _dot(xc, wg_ref[...]) + gb_ref[...]
        gates = _rg_gates(xc, pre, lam_ref[...])
        for (r, gi, a, mult), a_ref, b_ref in zip(gates, (a0_ref, a1_ref), (b0_ref, b1_ref)):
            a_ref[...] = a
            b_ref[...] = mult * gi * xc
        cq = cq_ref[...]
        sq = cq * _sig(cq)
        for h in range(NH):
            sl = slice(DH * h, DH * (h + 1))
            qh = sq[:, sl]
            q_ref[:, sl] = qh * lax.rsqrt(jnp.sum(qh * qh, axis=-1, keepdims=True) + EPS) * (DH ** -0.5)
            kh = sq[:, RGW + DH * h:RGW + DH * (h + 1)]
            k_ref[:, sl] = kh * lax.rsqrt(jnp.sum(kh * kh, axis=-1, keepdims=True) + EPS)
        v_ref[...] = sq[:, 2 * RGW:]
        pc = pc_ref[...]
        lane = lax.broadcasted_iota(jnp.int32, pc.shape, 1)
        beta = _sig(pc)
        g = -jnp.exp(alog_ref[...]) * _softplus(pc + dtb_ref[...])
        bg_ref[...] = jnp.where(lane < 8, beta, jnp.where(lane < 16, g, 0.0))

    return pl.pallas_call(
        body, name=name, grid=(s // t,),
        in_specs=[_rows(t, RGW), _rows(t, QKVW), _rows(t, BAP), _full((RGW, 4 * RGW)), _full((1, 4 * RGW)),
                  _full((1, 2 * RGW)), _full((1, BAP)), _full((1, BAP))],
        out_specs=[_rows(t, RGW)] * 7 + [_rows(t, BAP)],
        out_shape=[_sds((s, RGW))] * 7 + [_sds((s, BAP))],
        compiler_params=_cparams(1),
    )(c_rg, c_qkv, p_ba, wgates, gbias, lam_row, alog_row, dtb_row)


def _scan_pair(af, bf, ar, br, shifted, name):
    s, c = af.shape
    t = min(512, s)
    nb = s // t
    ng = t // 8
    tb = t // 8
    up = lambda i: (i, 0)
    down = lambda i: (nb - 1 - i, 0)

    def body(*refs):
        if shifted:
            af_ref, bf_ref, ar_ref, br_ref, afp_ref, arn_ref, hf_ref, hr_ref, carry, fbuf, rbuf = refs
        else:
            af_ref, bf_ref, ar_ref, br_ref, hf_ref, hr_ref, carry = refs
        i = pl.program_id(0)

        @pl.when(i == 0)
        def _():
            carry[...] = jnp.zeros_like(carry)

        if shifted:
            edge = jnp.where(i > 0, 1.0, 0.0).astype(F32)
            fbuf[0:8, :] = afp_ref[...] * edge
            fbuf[8:t + 8, :] = af_ref[...]
            rbuf[0:t, :] = ar_ref[...]
            rbuf[t:t + 8, :] = arn_ref[...] * edge
        row = lax.broadcasted_iota(jnp.int32, (8, c), 0)

        def block_scan(av, bv, downwards):
            for k in (1, 2, 4):
                sh = (8 - k) if downwards else k
                m = (row < 8 - k) if downwards else (row >= k)
                a_s = pltpu.roll(av, sh, 0)
                b_s = pltpu.roll(bv, sh, 0)
                bv = jnp.where(m, av * b_s + bv, bv)
                av = jnp.where(m, av * a_s, av)
            return av, bv

        def group(gi, cvs):
            cf, cr = cvs
            rf = pl.multiple_of(gi * 8, 8)
            rr = pl.multiple_of((ng - 1 - gi) * 8, 8)
            if shifted:
                a_f = jnp.where(row > 0, pltpu.roll(fbuf[pl.ds(rf + 8, 8), :], 1, 0), pltpu.roll(fbuf[pl.ds(rf, 8), :], 1, 0))
                a_r = jnp.where(row < 7, pltpu.roll(rbuf[pl.ds(rr, 8), :], 7, 0), pltpu.roll(rbuf[pl.ds(rr + 8, 8), :], 7, 0))
            else:
                a_f = af_ref[pl.ds(rf, 8), :]
                a_r = ar_ref[pl.ds(rr, 8), :]
            a_f, b_f = block_scan(a_f, bf_ref[pl.ds(rf, 8), :], False)
            a_r, b_r = block_scan(a_r, br_ref[pl.ds(rr, 8), :], True)
            h_f = a_f * cf + b_f
            h_r = a_r * cr + b_r
            hf_ref[pl.ds(rf, 8), :] = h_f
            hr_ref[pl.ds(rr, 8), :] = h_r
            return h_f[7:8, :], h_r[0:1, :]

        cf, cr = lax.fori_loop(0, ng, group, (carry[0:1, :], carry[8:9, :]))
        carry[0:1, :] = cf
        carry[8:9, :] = cr

    in_specs = [pl.BlockSpec((t, c), up), pl.BlockSpec((t, c), up), pl.BlockSpec((t, c), down), pl.BlockSpec((t, c), down)]
    args = [af, bf, ar, br]
    scratch = [pltpu.VMEM((16, c), F32)]
    if shifted:
        in_specs += [pl.BlockSpec((8, c), lambda i: (jnp.maximum(i * tb - 1, 0), 0)),
                     pl.BlockSpec((8, c), lambda i: (jnp.minimum((nb - i) * tb, s // 8 - 1), 0))]
        args += [af, ar]
        scratch += [pltpu.VMEM((t + 8, c), F32), pltpu.VMEM((t + 8, c), F32)]
    return pl.pallas_call(
        body, name=name, grid=(nb,), in_specs=in_specs,
        out_specs=[pl.BlockSpec((t, c), up), pl.BlockSpec((t, c), down)], out_shape=[_sds((s, c)), _sds((s, c))],
        scratch_shapes=scratch, compiler_params=_cparams(1),
    )(*args)


def _gates_bwd(xc, wgates, gbias, lam_row, lam0, lam1, hf, hb, name):
    s = xc.shape[0]
    t = min(256, s)
    nb = s // t

    def body(xc_ref, wg_ref, gb_ref, lam_ref, l0_ref, l1_ref, hf_ref, hfp_ref, hfn_ref, hb_ref, hbp_ref, hbn_ref,
             dxc_ref, dpre_ref, xcb_ref, dgb_ref, dlam_ref):
        @pl.when(pl.program_id(0) == 0)
        def _():
            dgb_ref[...] = jnp.zeros_like(dgb_ref)
            dlam_ref[...] = jnp.zeros_like(dlam_ref)

        pm, nm = _edge_masks(nb)
        h_prev = _shifted(jnp.concatenate([hfp_ref[...] * pm, hf_ref[...], hfn_ref[...] * nm], axis=0), -1, t)
        h_next = _shifted(jnp.concatenate([hbp_ref[...] * pm, hb_ref[...], hbn_ref[...] * nm], axis=0), 1, t)
        h_shift = (h_prev, h_next)
        xv = xc_ref[...]
        pre = _dot(xv, wg_ref[...]) + gb_ref[...]
        lam_row_v = lam_ref[...]
        sp8 = RG_C * _softplus(-lam_row_v)
        dsp_dlam = -RG_C * _sig(-lam_row_v)
        gates = _rg_gates(xv, pre, lam_row_v)
        dxc = jnp.zeros((t, RGW), F32)
        dpre_r = []
        dpre_i = []
        for d, ((r, gi, a, mult), l_ref, hs) in enumerate(zip(gates, (l0_ref, l1_ref), h_shift)):
            dbb = l_ref[...]
            da = dbb * hs
            cs = slice(RGW * d, RGW * (d + 1))
            dmult = dbb * gi * xv
            dgi = dbb * mult * xv
            dxc = dxc + dbb * mult * gi
            dla = da * a - dmult * a * a / mult
            dr = -dla * sp8[:, cs]
            dlam_ref[:, cs] += _colsum(-dla * r) * dsp_dlam[:, cs]
            dpre_r.append(dr * r * (1.0 - r))
            dpre_i.append(dgi * gi * (1.0 - gi))
        dpre = jnp.concatenate(dpre_r + dpre_i, axis=1)
        dgb_ref[...] += _colsum(dpre)
        dpre_b = dpre.astype(BF16)
        dpre_ref[...] = dpre_b
        xcb_ref[...] = xv.astype(BF16)
        dxc_ref[...] = dxc + _dot_nt(dpre_b, wg_ref[...])

    prev, nxt = _halo_specs(s, t, RGW)
    return pl.pallas_call(
        body, name=name, grid=(s // t,),
        in_specs=[_rows(t, RGW), _full((RGW, 4 * RGW)), _full((1, 4 * RGW)), _full((1, 2 * RGW))] + [_rows(t, RGW)] * 2
        + [_rows(t, RGW), prev, nxt] * 2,
        out_specs=[_rows(t, RGW), _rows(t, 4 * RGW), _rows(t, RGW), _full((1, 4 * RGW)), _full((1, 2 * RGW))],
        out_shape=[_sds((s, RGW)), _sds((s, 4 * RGW), BF16), _sds((s, RGW), BF16), _sds((1, 4 * RGW)), _sds((1, 2 * RGW))],
        compiler_params=_cparams(1),
    )(xc, wgates, gbias, lam_row, lam0, lam1, hf, hf, hf, hb, hb, hb)


class _GdnMasks:
    def __init__(self, d):
        ri = lax.broadcasted_iota(jnp.int32, (CHUNK, CHUNK), 0)
        ci = lax.broadcasted_iota(jnp.int32, (CHUNK, CHUNK), 1)
        self.incl = (ri >= ci) if d == 0 else (ri <= ci)
        self.strict = (ri > ci) if d == 0 else (ri < ci)
        b16 = jnp.right_shift(ri, 4) == jnp.right_shift(ci, 4)
        b32 = jnp.right_shift(ri, 5) == jnp.right_shift(ci, 5)
        self.diag16 = b16
        self.off32 = jnp.logical_and(b32, jnp.logical_not(b16))
        self.off64 = jnp.logical_not(b32)
        self.eye = jnp.where(ri == ci, 1.0, 0.0).astype(F32)
        self.tri = jnp.where(self.incl, 1.0, 0.0).astype(F32)
        self.last = CHUNK - 1 if d == 0 else 0


def _tri_inv(lmat, m):
    return _tri_inv_many([lmat], [m])[0]


def _tri_inv_many(lmats, masks):
    n = len(lmats)
    ns = [jnp.where(masks[i].diag16, lmats[i], 0.0) for i in range(n)]
    ps = [masks[i].eye - ns[i] for i in range(n)]
    qs = [_dot3(ns[i], ns[i]) for i in range(n)]
    for step in range(3):
        ps = [_dot3(ps[i], masks[i].eye + qs[i]) for i in range(n)]
        if step < 2:
            qs = [_dot3(qs[i], qs[i]) for i in range(n)]
    for off in ("off32", "off64"):
        ts = [_dot3(ps[i], jnp.where(getattr(masks[i], off), lmats[i], 0.0)) for i in range(n)]
        ps = [ps[i] - _dot3(ts[i], ps[i]) for i in range(n)]
    return ps


def _chunk_cumsums(m, bgv):
    return _dot_exact(m.tri, bgv, _NN, True), _dot_exact(m.tri, bgv, ((0,), (1,)), False)


class _GdnHead:
    def __init__(self, qh, kh, vh, kk, q0, bg, gcs, gcs_t, d, h, m):
        cb = 4 * d + h
        cg = 8 + 4 * d + h
        self.q, self.k, self.v = qh, kh, vh
        self.beta = bg[:, cb:cb + 1]
        gcol = gcs[:, cg:cg + 1]
        grow = gcs_t[cg:cg + 1, :]
        gl = gcs[m.last:m.last + 1, cg:cg + 1]
        self.decay = jnp.exp(jnp.where(m.incl, gcol - grow, -1e30))
        self.kb = kh * self.beta
        self.vb = vh * self.beta
        self.a0 = kk * self.beta
        self.q0 = q0
        self.lmat = jnp.where(m.strict, self.a0 * self.decay, 0.0)
        self.attn = self.q0 * self.decay
        self.eg = jnp.exp(gcol)
        self.ek = jnp.exp(gl - gcol)
        self.cd = jnp.exp(gl)
        self.kg = self.kb * self.eg
        self.qd = qh * self.eg
        self.kd = kh * self.ek


HW = NH * DH
SEQ_CB = 8
LOCAL_CB = 4


def _head(h):
    return slice(DH * h, DH * (h + 1))


def _gdn_local_fwd(q, k, v, bg, name):
    s = q.shape[0]
    n = s // CHUNK
    cb = min(LOCAL_CB, n)

    def body(q_ref, k_ref, v_ref, bg_ref, t_ref, u_ref, w_ref, qd_ref, kd_ref, at_ref, cd_ref):
        masks = [_GdnMasks(d) for d in range(2)]
        inst = []
        for jj in range(cb):
            rows = slice(CHUNK * jj, CHUNK * (jj + 1))
            bgv = bg_ref[rows, :]
            qs = [q_ref[rows, _head(h)] for h in range(NH)]
            ks = [k_ref[rows, _head(h)] for h in range(NH)]
            kk = [_dot_nt(ks[h], ks[h]) for h in range(NH)]
            q0 = [_dot_nt(qs[h], ks[h]) for h in range(NH)]
            for d, m in enumerate(masks):
                gcs, gcs_t = _chunk_cumsums(m, bgv)
                for h in range(NH):
                    c = _GdnHead(qs[h], ks[h], v_ref[rows, _head(h)], kk[h], q0[h], bgv, gcs, gcs_t, d, h, m)
                    inst.append((jj, rows, d, h, m, c))
        tms = _tri_inv_many([it[-1].lmat for it in inst], [it[-2] for it in inst])
        for (jj, rows, d, h, m, c), tm in zip(inst, tms):
            sl = _head(h)
            t_ref[jj, d, h] = tm
            u_ref[d, rows, sl] = _dot(tm, c.vb)
            w_ref[d, rows, sl] = _dot(tm, c.kg).astype(BF16)
            qd_ref[d, rows, sl] = c.qd.astype(BF16)
            kd_ref[d, rows, sl] = c.kd.astype(BF16)
            at_ref[jj, d, h] = c.attn.astype(BF16)
            cd_ref[jj, 4 * d + h:4 * d + h + 1, :] = jnp.broadcast_to(c.cd, (1, DH))

    tok = _rows(cb * CHUNK, HW)
    tok2 = pl.BlockSpec((2, cb * CHUNK, HW), lambda i: (0, i, 0))
    mat = pl.BlockSpec((cb, 2, NH, CHUNK, CHUNK), lambda i: (i, 0, 0, 0, 0))
    return pl.pallas_call(
        body, name=name, grid=(n // cb,), in_specs=[tok, tok, tok, _rows(cb * CHUNK, BAP)],
        out_specs=[mat, tok2, tok2, tok2, tok2, mat, pl.BlockSpec((cb, 8, DH), lambda i: (i, 0, 0))],
        out_shape=[_sds((n, 2, NH, CHUNK, CHUNK)), _sds((2, s, HW)), _sds((2, s, HW), BF16), _sds((2, s, HW), BF16),
                   _sds((2, s, HW), BF16), _sds((n, 2, NH, CHUNK, CHUNK), BF16), _sds((n, 8, DH))],
        compiler_params=_cparams(1),
    )(q, k, v, bg)


def _seq_specs(s, order):
    n = s // CHUNK
    cb = min(SEQ_CB, n)
    nb = n // cb
    tb = cb * CHUNK

    def blk(d):
        return (lambda i: i) if order[d] else (lambda i: nb - 1 - i)

    def per_dir(make):
        return [make(d, blk(d)) for d in range(2)]

    tok2 = per_dir(lambda d, f: pl.BlockSpec((1, tb, HW), lambda i: (d, f(i), 0)))
    tok = per_dir(lambda d, f: pl.BlockSpec((tb, HW), lambda i: (f(i), 0)))
    mat = per_dir(lambda d, f: pl.BlockSpec((cb, 1, NH, CHUNK, CHUNK), lambda i: (f(i), d, 0, 0, 0)))
    cds = per_dir(lambda d, f: pl.BlockSpec((cb, 8, DH), lambda i: (f(i), 0, 0)))
    sts = per_dir(lambda d, f: pl.BlockSpec((cb, NH, DH, DH), lambda i: (f(i), 0, 0, 0)))
    dcd = per_dir(lambda d, f: pl.BlockSpec((cb, NH, DH), lambda i: (f(i), 0, 0)))
    return n, cb, nb, tok2, tok, mat, cds, sts, dcd


def _gdn_seq_fwd(u, w, qd, kd, at, cd, name):
    s = u.shape[1]
    n, cb, nb, tok2, tok, mat, cds, sts, _ = _seq_specs(s, (True, False))

    def body(*refs):
        ins = (refs[0:6], refs[6:12])
        outs = (refs[12:15], refs[15:18])
        st = refs[18]

        @pl.when(pl.program_id(0) == 0)
        def _():
            st[...] = jnp.zeros_like(st)

        for j in range(cb):
            items = []
            for d in range(2):
                jj = j if d == 0 else cb - 1 - j
                items += [(d, h, jj, slice(CHUNK * jj, CHUNK * (jj + 1)), _head(h)) for h in range(NH)]
            shs = [st[d, h] for d, h, _, _, _ in items]
            wss = [_dot(ins[d][1][0, rows, sl], sh) for (d, h, jj, rows, sl), sh in zip(items, shs)]
            vns = [ins[d][0][0, rows, sl] - ws for (d, h, jj, rows, sl), ws in zip(items, wss)]
            news = [sh * ins[d][5][jj, 4 * d + h:4 * d + h + 1, :] + _dot_tn(ins[d][3][0, rows, sl], vn)
                    for (d, h, jj, rows, sl), sh, vn in zip(items, shs, vns)]
            for (d, h, jj, rows, sl), sh, vn, new in zip(items, shs, vns, news):
                o_r, s_r, vn_r = outs[d]
                st[d, h] = new
                s_r[jj, h] = sh
                vn_r[rows, sl] = vn
                o_r[rows, sl] = _dot(ins[d][2][0, rows, sl], sh) + _dot(ins[d][4][jj, 0, h], vn)

    in_specs, out_specs, out_shape = [], [], []
    for d in range(2):
        in_specs += [tok2[d]] * 4 + [mat[d], cds[d]]
        out_specs += [tok[d], sts[d], tok[d]]
        out_shape += [_sds((s, HW)), _sds((n, NH, DH, DH)), _sds((s, HW))]
    return pl.pallas_call(
        body, name=name, grid=(nb,), in_specs=in_specs, out_specs=out_specs, out_shape=out_shape,
        scratch_shapes=[pltpu.VMEM((2, NH, DH, DH), F32)], compiler_params=_cparams(1),
    )(u, w, qd, kd, at, cd, u, w, qd, kd, at, cd)


def _gdn_seq_bwd(do, w, qd, kd, at, cd, states, vns, name):
    s = do.shape[0]
    n, cb, nb, tok2, tok, mat, cds, sts, dcd = _seq_specs(s, (False, True))

    def body(*refs):
        ins = (refs[0:8], refs[8:16])
        outs = (refs[16:21], refs[21:26])
        dst = refs[26]

        @pl.when(pl.program_id(0) == 0)
        def _():
            dst[...] = jnp.zeros_like(dst)

        for j in range(cb):
            items = []
            for d in range(2):
                jj = cb - 1 - j if d == 0 else j
                items += [(d, h, jj, slice(CHUNK * jj, CHUNK * (jj + 1)), _head(h)) for h in range(NH)]
            dsns = [dst[d, h] for d, h, _, _, _ in items]
            dohs = [ins[d][0][rows, sl] for d, h, jj, rows, sl in items]
            d_vns = [_dot_tn(ins[d][4][jj, 0, h], doh) + _dot(ins[d][3][0, rows, sl], dsn)
                     for (d, h, jj, rows, sl), doh, dsn in zip(items, dohs, dsns)]
            news = [ins[d][5][jj, 4 * d + h:4 * d + h + 1, :] * dsn + _dot_tn(ins[d][2][0, rows, sl], doh)
                    - _dot_tn(ins[d][1][0, rows, sl], d_vn)
                    for (d, h, jj, rows, sl), doh, dsn, d_vn in zip(items, dohs, dsns, d_vns)]
            for (d, h, jj, rows, sl), doh, dsn, d_vn, new in zip(items, dohs, dsns, d_vns, news):
                dvn_r, dkd_r, dqd_r, dw_r, dcd_r = outs[d]
                sh = ins[d][6][jj, h]
                dst[d, h] = new
                dvn_r[rows, sl] = d_vn
                dkd_r[rows, sl] = _dot_nt(ins[d][7][rows, sl], dsn)
                dqd_r[rows, sl] = _dot_nt(doh, sh)
                dw_r[rows, sl] = -_dot_nt(d_vn, sh)
                d_cd = jnp.sum(jnp.sum(sh * dsn, axis=1, keepdims=True), axis=0, keepdims=True)
                dcd_r[jj, h:h + 1, :] = jnp.broadcast_to(d_cd, (1, DH))

    in_specs, out_specs, out_shape, args = [], [], [], []
    for d in range(2):
        in_specs += [tok[d]] + [tok2[d]] * 3 + [mat[d], cds[d], sts[d], tok[d]]
        args += [do, w, qd, kd, at, cd, states[d], vns[d]]
        out_specs += [tok[d]] * 4 + [dcd[d]]
        out_shape += [_sds((s, HW))] * 4 + [_sds((n, NH, DH))]
    return pl.pallas_call(
        body, name=name, grid=(nb,), in_specs=in_specs, out_specs=out_specs, out_shape=out_shape,
        scratch_shapes=[pltpu.VMEM((2, NH, DH, DH), F32)], compiler_params=_cparams(1),
    )(*args)


def _gdn_local_bwd(q, k, v, bg, tmat, do, vns, seq_grads, name, comm=None):
    s = q.shape[0]
    n = s // CHUNK
    cb = min(LOCAL_CB, n)

    def body(*refs):
        q_ref, k_ref, v_ref, bg_ref, t_ref, do_ref = refs[0:6]
        vn_refs = refs[6:8]
        sg = (refs[8:13], refs[13:18])
        dq_ref, dk_ref, dv_ref, dbg_ref = refs[18:]
        lane = lax.broadcasted_iota(jnp.int32, (CHUNK, BAP), 1)
        rowi = lax.broadcasted_iota(jnp.int32, (CHUNK, 1), 0)
        ones = jnp.ones((CHUNK, DH), F32)
        masks = [_GdnMasks(d) for d in range(2)]
        inst = []
        for jj in range(cb):
            rows = slice(CHUNK * jj, CHUNK * (jj + 1))
            bgv = bg_ref[rows, :]
            qs = [q_ref[rows, _head(h)] for h in range(NH)]
            ks = [k_ref[rows, _head(h)] for h in range(NH)]
            kk = [_dot_nt(ks[h], ks[h]) for h in range(NH)]
            q0 = [_dot_nt(qs[h], ks[h]) for h in range(NH)]
            for d, m in enumerate(masks):
                gcs, gcs_t = _chunk_cumsums(m, bgv)
                for h in range(NH):
                    c = _GdnHead(qs[h], ks[h], v_ref[rows, _head(h)], kk[h], q0[h], bgv, gcs, gcs_t, d, h, m)
                    inst.append((jj, rows, d, h, m, c))
        ni = len(inst)
        cs = [it[-1] for it in inst]
        tms = [t_ref[jj, d, h] for jj, _, d, h, _, _ in inst]
        d_vns = [sg[d][0][rows, _head(h)] for _, rows, d, h, _, _ in inst]
        d_ws = [sg[d][3][rows, _head(h)] for _, rows, d, h, _, _ in inst]
        d_ts = [_dot_nt(d_vns[i], cs[i].vb) + _dot_nt(d_ws[i], cs[i].kg) for i in range(ni)]
        tts = [tm.T for tm in tms]
        xs = [_dot3(tts[i], d_ts[i]) for i in range(ni)]
        d_ls = [jnp.where(inst[i][4].strict, -_dot3(xs[i], tts[i]), 0.0) for i in range(ni)]
        d_attns = [jnp.where(m.incl, _dot_nt(do_ref[rows, _head(h)], vn_refs[d][rows, _head(h)]), 0.0)
                   for _, rows, d, h, m, _ in inst]
        d_vbs = [_dot(tts[i], d_vns[i]) for i in range(ni)]
        d_kgs = [_dot(tts[i], d_ws[i]) for i in range(ni)]
        d_a0s = [d_ls[i] * cs[i].decay for i in range(ni)]
        d_q0s = [d_attns[i] * cs[i].decay for i in range(ni)]
        es = [(d_ls[i] * cs[i].a0 + d_attns[i] * cs[i].q0) * cs[i].decay for i in range(ni)]
        kb_mm = [_dot(d_a0s[i], cs[i].k) for i in range(ni)]
        q_mm = [_dot(d_q0s[i], cs[i].k) for i in range(ni)]
        k_mm = [_dot_tn(d_a0s[i], cs[i].kb) + _dot_tn(d_q0s[i], cs[i].q) for i in range(ni)]
        e_cols = [_dot_exact(ones, es[i], _TN, False)[:, 0:1] for i in range(ni)]
        acc = {}
        d_gcs, d_betas = [], []
        for i, (jj, rows, d, h, m, c) in enumerate(inst):
            sl = _head(h)
            d_kd, d_qd = sg[d][1][rows, sl], sg[d][2][rows, sl]
            d_cd = sg[d][4][jj, h:h + 1, 0:1]
            d_vb, d_kg = d_vbs[i], d_kgs[i]
            d_kb = kb_mm[i] + d_kg * c.eg
            parts = (q_mm[i] + d_qd * c.eg, k_mm[i] + d_kd * c.ek + d_kb * c.beta, d_vb * c.beta)
            acc[jj, h] = [p + a for a, p in zip(acc[jj, h], parts)] if (jj, h) in acc else list(parts)
            s_kd = jnp.sum(d_kd * c.kd, axis=1, keepdims=True)
            d_gc = (jnp.sum(d_kg * c.kg, axis=1, keepdims=True) + jnp.sum(d_qd * c.qd, axis=1, keepdims=True) - s_kd
                    + jnp.sum(es[i], axis=1, keepdims=True) - e_cols[i])
            d_gl = jnp.sum(s_kd, axis=0, keepdims=True) + d_cd * c.cd
            d_gcs.append(d_gc + jnp.where(rowi == m.last, d_gl, 0.0))
            d_betas.append(jnp.sum(d_kb * c.k, axis=1, keepdims=True) + jnp.sum(d_vb * c.v, axis=1, keepdims=True))
        d_gs = [_dot_exact(inst[i][4].tri, d_gcs[i] * ones, _TN, True)[:, 0:1] for i in range(ni)]
        dbg = [jnp.zeros((CHUNK, BAP), F32) for _ in range(cb)]
        for i, (jj, _, d, h, _, _) in enumerate(inst):
            dbg[jj] = dbg[jj] + jnp.where(lane == 4 * d + h, d_betas[i], 0.0) + jnp.where(lane == 8 + 4 * d + h, d_gs[i], 0.0)
        for jj in range(cb):
            rows = slice(CHUNK * jj, CHUNK * (jj + 1))
            for h in range(NH):
                dq_ref[rows, _head(h)], dk_ref[rows, _head(h)], dv_ref[rows, _head(h)] = acc[jj, h]
            dbg_ref[rows, :] = dbg[jj]

    tok = _rows(cb * CHUNK, HW)
    bgs = _rows(cb * CHUNK, BAP)
    mat = pl.BlockSpec((cb, 2, NH, CHUNK, CHUNK), lambda i: (i, 0, 0, 0, 0))
    dcd = pl.BlockSpec((cb, NH, DH), lambda i: (i, 0, 0))
    args = [q, k, v, bg, tmat, do, vns[0], vns[1]]
    in_specs = [tok, tok, tok, bgs, mat, tok, tok, tok]
    for d in range(2):
        args += list(seq_grads[d])
        in_specs += [tok] * 4 + [dcd]
    return _pallas(body, comm, name=name, grid=(n // cb,), in_specs=in_specs, out_specs=[tok, tok, tok, bgs],
                   out_shape=[_sds((s, HW))] * 3 + [_sds((s, BAP))], scratch_shapes=[], args=args)


def _prep_bwd(c_qkv, p_ba, alog_row, dtb_row, dq, dk, dv, dbg, name):
    s = c_qkv.shape[0]
    t = min(256, s)

    def body(cq_ref, pc_ref, alog_ref, dtb_ref, dq_ref, dk_ref, dv_ref, dbg_ref,
             dcq_ref, dpc_ref, dalog_ref, ddtb_ref):
        @pl.when(pl.program_id(0) == 0)
        def _():
            dalog_ref[...] = jnp.zeros_like(dalog_ref)
            ddtb_ref[...] = jnp.zeros_like(ddtb_ref)

        cq = cq_ref[...]
        sq = cq * _sig(cq)
        sg = _silu_grad(cq)
        for h in range(NH):
            sl = slice(DH * h, DH * (h + 1))
            for off, d_ref, scale in ((0, dq_ref, DH ** -0.5), (RGW, dk_ref, 1.0)):
                csl = slice(off + DH * h, off + DH * (h + 1))
                xh = sq[:, csl]
                nrm = lax.rsqrt(jnp.sum(xh * xh, axis=-1, keepdims=True) + EPS)
                y = xh * nrm
                dy = d_ref[:, sl] * scale
                dcq_ref[:, csl] = nrm * (dy - y * jnp.sum(dy * y, axis=-1, keepdims=True)) * sg[:, csl]
        dcq_ref[:, 2 * RGW:] = dv_ref[...] * sg[:, 2 * RGW:]
        pc = pc_ref[...]
        lane = lax.broadcasted_iota(jnp.int32, pc.shape, 1)
        dbg = dbg_ref[...]
        beta = _sig(pc)
        ea = jnp.exp(alog_ref[...])
        z = pc + dtb_ref[...]
        g = -ea * _softplus(z)
        is_g = jnp.logical_and(lane >= 8, lane < 16)
        d_alpha = jnp.where(is_g, dbg * (-ea) * _sig(z), 0.0)
        dpc_ref[...] = jnp.where(lane < 8, dbg * beta * (1.0 - beta), d_alpha)
        dalog_ref[...] += _colsum(jnp.where(is_g, dbg * g, 0.0))
        ddtb_ref[...] += _colsum(d_alpha)

    return pl.pallas_call(
        body, name=name, grid=(s // t,),
        in_specs=[_rows(t, QKVW), _rows(t, BAP), _full((1, BAP)), _full((1, BAP))] + [_rows(t, HW)] * 3 + [_rows(t, BAP)],
        out_specs=[_rows(t, QKVW), _rows(t, BAP), _full((1, BAP)), _full((1, BAP))],
        out_shape=[_sds((s, QKVW)), _sds((s, BAP)), _sds((1, BAP)), _sds((1, BAP))],
        compiler_params=_cparams(1),
    )(c_qkv, p_ba, alog_row, dtb_row, dq, dk, dv, dbg)


def _mix_out_values(hf, hb, gate, of, ob, z, gn):
    hr = hf + hb
    y_rg = hr * _gelu(gate)
    osum = of + ob
    parts = []
    for h in range(NH):
        sl = slice(DH * h, DH * (h + 1))
        oh = osum[:, sl]
        r, ohat = _rms(oh)
        zh = z[:, sl]
        parts.append((r, ohat, zh))
    y_gdn = jnp.concatenate([ohat * gn * (zh * _sig(zh)) for (r, ohat, zh) in parts], axis=1)
    return hr, y_rg, y_gdn, parts


def _outproj(x1, hf, hb, gate, of, ob, z, gn, wout, name):
    s = x1.shape[0]
    t = min(256, s)

    def body(x_ref, hf_ref, hb_ref, gate_ref, of_ref, ob_ref, z_ref, gn_ref, w_ref, xo_ref, y_ref):
        _, y_rg, y_gdn, _ = _mix_out_values(hf_ref[...], hb_ref[...], gate_ref[...], of_ref[...], ob_ref[...],
                                            z_ref[...], gn_ref[...])
        y = jnp.concatenate([y_rg, y_gdn], axis=1).astype(BF16)
        y_ref[...] = y
        xo_ref[...] = x_ref[...] + jnp.dot(y, w_ref[...], preferred_element_type=F32)

    return pl.pallas_call(
        body, name=name, grid=(s // t,),
        in_specs=[_rows(t, D)] + [_rows(t, RGW)] * 6 + [_full((1, DH)), _full((D, D))],
        out_specs=[_rows(t, D), _rows(t, D)], out_shape=[_sds((s, D)), _sds((s, D), BF16)],
        compiler_params=_cparams(1),
    )(x1, hf, hb, gate, of, ob, z, gn, wout)


def _outproj_bwd(dx2, hf, hb, gate, of, ob, z, gn, wout, name, comm=None):
    s = dx2.shape[0]
    t = min(256, s)

    def body(d_ref, hf_ref, hb_ref, gate_ref, of_ref, ob_ref, z_ref, gn_ref, w_ref,
             dhr_ref, dgate_ref, dos_ref, dz_ref, dgn_ref, db_ref):
        @pl.when(pl.program_id(0) == 0)
        def _():
            dgn_ref[...] = jnp.zeros_like(dgn_ref)

        gate = gate_ref[...]
        gn_v = gn_ref[...]
        hr, _, _, parts = _mix_out_values(hf_ref[...], hb_ref[...], gate, of_ref[...], ob_ref[...], z_ref[...], gn_v)
        dbf = d_ref[...].astype(BF16)
        db_ref[...] = dbf
        dy = _dot_nt(dbf, w_ref[...])
        dyr = dy[:, :RGW]
        dhr_ref[...] = dyr * _gelu(gate)
        dgate_ref[...] = dyr * hr * _gelu_grad(gate)
        dgn = jnp.zeros((1, DH), F32)
        for h, (r, ohat, zh) in enumerate(parts):
            sl = slice(DH * h, DH * (h + 1))
            dyh = dy[:, RGW + DH * h:RGW + DH * (h + 1)]
            sz = zh * _sig(zh)
            dn = dyh * sz
            dz_ref[:, sl] = dyh * ohat * gn_v * _silu_grad(zh)
            dgn = dgn + _colsum(dn * ohat)
            dos_ref[:, sl] = _rms_bwd(dn, ohat, r, gn_v)
        dgn_ref[...] += dgn

    return _pallas(
        body, comm, name=name, grid=(s // t,),
        in_specs=[_rows(t, D)] + [_rows(t, RGW)] * 6 + [_full((1, DH)), _full((D, D))],
        out_specs=[_rows(t, RGW)] * 4 + [_full((1, DH)), _rows(t, D)],
        out_shape=[_sds((s, RGW))] * 4 + [_sds((1, DH)), _sds((s, D), BF16)],
        scratch_shapes=[], args=(dx2, hf, hb, gate, of, ob, z, gn, wout))


def _loss_head(x3, target, gain, name):
    s = x3.shape[0]
    t = min(256, s)

    def body(x_ref, t_ref, g_ref, dx_ref, dxh_ref, loss_ref, dg_ref):
        @pl.when(pl.program_id(0) == 0)
        def _():
            loss_ref[...] = jnp.zeros_like(loss_ref)
            dg_ref[...] = jnp.zeros_like(dg_ref)

        r, xh = _rms(x_ref[...])
        gv = g_ref[...]
        err = xh * gv - t_ref[...]
        per_tok = jnp.mean(err * err, axis=-1, keepdims=True)
        loss_ref[...] += 0.5 * jnp.sum(per_tok, axis=0, keepdims=True)
        dy = err * (1.0 / D)
        dg_ref[...] += _colsum(dy * xh)
        dx = _rms_bwd(dy, xh, r, gv)
        dx_ref[...] = dx
        dxh_ref[...] = (0.5 * dx).astype(BF16)

    return pl.pallas_call(
        body, name=name, grid=(s // t,), in_specs=[_rows(t, D), _rows(t, D), _full((1, D))],
        out_specs=[_rows(t, D), _rows(t, D), _full((8, 128)), _full((1, D))],
        out_shape=[_sds((s, D)), _sds((s, D), BF16), _sds((8, 128)), _sds((1, D))], compiler_params=_cparams(1),
    )(x3, target, gain)


def _adamw_math(wv, gv, mv, vv):
    mn = ADAM_B1 * mv + (1.0 - ADAM_B1) * gv
    vn = ADAM_B2 * vv + (1.0 - ADAM_B2) * (gv * gv)
    m_hat = mn / (1.0 - ADAM_B1 ** ADAM_STEP)
    v_hat = vn / (1.0 - ADAM_B2 ** ADAM_STEP)
    return -ADAM_LR * (m_hat / (jnp.sqrt(v_hat) + ADAM_EPS) + ADAM_WD * wv), mn, vn


def _row_tile(r, c):
    tr = r
    while tr * c * 4 > (1 << 20) and tr % 16 == 0:
        tr //= 2
    return tr


def _adamw(w, g, m, v, name):
    r, c = w.shape
    tr = _row_tile(r, c)

    def body(w_ref, g_ref, m_ref, v_ref, d_ref, nm_ref, nv_ref):
        d_ref[...], nm_ref[...], nv_ref[...] = _adamw_math(w_ref[...], g_ref[...], m_ref[...], v_ref[...])

    return pl.pallas_call(
        body, name=name, grid=(r // tr,), in_specs=[_rows(tr, c)] * 4, out_specs=[_rows(tr, c)] * 3,
        out_shape=[_sds((r, c))] * 3, compiler_params=_cparams(1),
    )(w, g, m, v)


def _adamw_halves(w, own, recv, m, v, c_arr, name):
    r, c = w.shape
    h = r // 2
    tr = _row_tile(h, c)
    nh = h // tr

    def body(c_ref, w_ref, own_ref, recv_ref, m_ref, v_ref, g_ref, d_ref, nm_ref, nv_ref):
        first_half = pl.program_id(0) < nh
        use_own = first_half == (c_ref[0] == 0)
        gv = jnp.where(use_own, own_ref[...], recv_ref[...])
        g_ref[...] = gv
        d_ref[...], nm_ref[...], nv_ref[...] = _adamw_math(w_ref[...], gv, m_ref[...], v_ref[...])

    full = pl.BlockSpec((tr, c), lambda i, c_ref: (i, 0))
    half = pl.BlockSpec((tr, c), lambda i, c_ref: (i % nh, 0))
    return pl.pallas_call(
        body, name=name, out_shape=[_sds((r, c))] * 4,
        grid_spec=pltpu.PrefetchScalarGridSpec(
            num_scalar_prefetch=1, grid=(2 * nh,), in_specs=[full, half, half, full, full], out_specs=[full] * 4),
        compiler_params=_cparams(1),
    )(c_arr, w, own, recv, m, v)


def _mesh_pos():
    return lax.axis_index("x"), lax.axis_index("y"), lax.axis_index("c")


def _other_chips(x, y):
    return [(1 - x, y), (x, 1 - y), (1 - x, 1 - y)]


class _Comm:
    def __init__(self, inputs, out_shapes, scratch, start, finish, space=pltpu.HBM):
        self.inputs, self.out_shapes, self.scratch = list(inputs), list(out_shapes), list(scratch)
        self.start, self.finish, self.space = start, finish, space


def _comm_call(comm, name):
    ni, no = len(comm.inputs), len(comm.out_shapes)

    def body(*refs):
        comm.start(refs[:ni], refs[ni:ni + no], refs[ni + no:])
        comm.finish(refs[:ni], refs[ni:ni + no], refs[ni + no:])

    spec = pl.BlockSpec(memory_space=comm.space)
    return list(pl.pallas_call(body, name=name, out_shape=comm.out_shapes, in_specs=[spec] * ni, out_specs=[spec] * no,
                               scratch_shapes=comm.scratch)(*comm.inputs))


def _pallas(body, comm, *, name, grid, in_specs, out_specs, out_shape, scratch_shapes, args):
    params = _cparams(len(grid))
    if comm is None:
        outs = pl.pallas_call(body, name=name, grid=grid, in_specs=in_specs, out_specs=out_specs, out_shape=out_shape,
                              scratch_shapes=scratch_shapes, compiler_params=params)(*args)
        return list(outs), []
    n_in, n_out, n_sc = len(in_specs), len(out_specs), len(scratch_shapes)
    ci, co = len(comm.inputs), len(comm.out_shapes)

    def carried(*refs):
        bounds = [0, n_in, n_in + ci, n_in + ci + n_out, n_in + ci + n_out + co, n_in + ci + n_out + co + n_sc, len(refs)]
        ins, cins, outs, couts, scr, csems = [refs[lo:hi] for lo, hi in zip(bounds[:-1], bounds[1:])]
        ids = [pl.program_id(k) for k in range(len(grid))]
        first = functools.reduce(jnp.logical_and, [i == 0 for i in ids])
        last = functools.reduce(jnp.logical_and, [i == g - 1 for i, g in zip(ids, grid)])

        @pl.when(first)
        def _():
            comm.start(cins, couts, csems)

        body(*ins, *outs, *scr)

        @pl.when(last)
        def _():
            comm.finish(cins, couts, csems)

    hbm = pl.BlockSpec(memory_space=pltpu.HBM)
    outs = pl.pallas_call(
        carried, name=name, grid=grid, in_specs=list(in_specs) + [hbm] * ci, out_specs=list(out_specs) + [hbm] * co,
        out_shape=list(out_shape) + comm.out_shapes, scratch_shapes=list(scratch_shapes) + comm.scratch,
        compiler_params=params)(*args, *comm.inputs)
    return list(outs[:n_out]), list(outs[n_out:])


def _gather_comm(arrays, space, block_rows):
    n_arr = len(arrays)

    def plan(x_refs, out_refs, sems):
        send_sems, recv_sems, local_sems = sems
        x, y, c = _mesh_pos()
        me, sibling = (x, y, c), (x, y, 1 - c)
        chips = _other_chips(x, y)

        def slot(a, px, py, pc):
            return out_refs[a].at[4 * px + 2 * py + pc]

        def copy(a, k, block, to, src=None):
            return pltpu.make_async_remote_copy(
                src_ref=slot(a, *block) if src is None else src, dst_ref=slot(a, *block),
                send_sem=send_sems.at[7 * a + k], recv_sem=recv_sems.at[7 * a + k], device_id=to, device_id_type=MESH)

        srcs = [x_refs[a] if block_rows[a] is None else
                x_refs[a].at[pl.ds(pl.multiple_of(c * block_rows[a], 16), block_rows[a]), :] for a in range(n_arr)]
        local = [pltpu.make_async_copy(srcs[a], slot(a, *me), local_sems.at[a]) for a in range(n_arr)]
        first = []
        for a in range(n_arr):
            first += [copy(a, 1 + j, me, (*chip, c), src=srcs[a]) for j, chip in enumerate(chips)]
            first.append(copy(a, 0, me, sibling, src=srcs[a]))
        return me, sibling, chips, c, copy, local, first

    def start(x_refs, out_refs, sems):
        _, _, _, _, _, local, first = plan(x_refs, out_refs, sems)
        for cp in local + first:
            cp.start()

    def finish(x_refs, out_refs, sems):
        me, sibling, chips, c, copy, local, first = plan(x_refs, out_refs, sems)
        passed = []
        for j, chip in enumerate(chips):
            for a in range(n_arr):
                copy(a, 1 + j, (*chip, c), me).wait_recv()
                fwd = copy(a, 4 + j, (*chip, c), sibling)
                fwd.start()
                passed.append(fwd)
        for a in range(n_arr):
            copy(a, 0, sibling, me).wait_recv()
            for j, chip in enumerate(chips):
                copy(a, 4 + j, (*chip, 1 - c), me).wait_recv()
        for cp in first + passed:
            cp.wait_send()
        for cp in local:
            cp.wait()

    out_shapes = [_sds((8, w.shape[0] if r is None else r) + w.shape[1:], w.dtype) for w, r in zip(arrays, block_rows)]
    scratch = [pltpu.SemaphoreType.DMA((7 * n_arr,)), pltpu.SemaphoreType.DMA((7 * n_arr,)), pltpu.SemaphoreType.DMA((n_arr,))]
    return _Comm(arrays, out_shapes, scratch, start, finish, space)


def _weights_gather_comm(shards):
    return _gather_comm(shards, pltpu.HBM, [w.shape[0] // 2 for w in shards])


def _all_shards(gathered):
    return [o.reshape(NSH, 2 * o.shape[1], o.shape[2]) for o in gathered]


def _gather_small(block, name):
    return _comm_call(_gather_comm([block], pltpu.VMEM, [None]), name)[0]


def _exchange_comm(gs):
    n = len(gs)
    halves = [g.shape[1] // 2 for g in gs]

    def plan(g_refs, land_refs, sems):
        send_sems, recv_sems = sems
        x, y, c = _mesh_pos()
        copies = []
        for a in range(n):
            h = halves[a]
            for s in range(NSH):
                copies.append(pltpu.make_async_remote_copy(
                    src_ref=g_refs[a].at[s, pl.ds(pl.multiple_of((1 - c) * h, 8), h), :], dst_ref=land_refs[a].at[s],
                    send_sem=send_sems.at[NSH * a + s], recv_sem=recv_sems.at[NSH * a + s],
                    device_id=(x, y, 1 - c), device_id_type=MESH))
        return copies

    def start(g_refs, land_refs, sems):
        for cp in plan(g_refs, land_refs, sems):
            cp.start()

    def finish(g_refs, land_refs, sems):
        for cp in plan(g_refs, land_refs, sems):
            cp.wait()

    scratch = [pltpu.SemaphoreType.DMA((NSH * n,)), pltpu.SemaphoreType.DMA((NSH * n,))]
    return _Comm(gs, [_sds((NSH, h, g.shape[2])) for h, g in zip(halves, gs)], scratch, start, finish)


def _chip_sum(g, land, c_arr, name):
    _, h, cols = land.shape

    def body(c_ref, g_ref, l_ref, o_ref):
        o_ref[...] = (g_ref[...] + l_ref[...]).astype(BF16)

    return pl.pallas_call(
        body, name=name, out_shape=_sds((NSH, h, cols), BF16),
        grid_spec=pltpu.PrefetchScalarGridSpec(
            num_scalar_prefetch=1, grid=(NSH,),
            in_specs=[pl.BlockSpec((1, h, cols), lambda s, c_ref: (s, c_ref[0], 0)),
                      pl.BlockSpec((1, h, cols), lambda s, c_ref: (s, 0, 0))],
            out_specs=pl.BlockSpec((1, h, cols), lambda s, c_ref: (s, 0, 0))),
        compiler_params=_cparams(1),
    )(c_arr, g, land)


def _scatter_comm(parts):
    n = len(parts)

    def plan(p_refs, land_refs, sems):
        send_sems, recv_sems, local_sems = sems
        x, y, c = _mesh_pos()
        my_chip = 2 * x + y
        local = [pltpu.make_async_copy(p_refs[a].at[my_chip], land_refs[a].at[my_chip], local_sems.at[a]) for a in range(n)]
        copies = []
        for a in range(n):
            for j, (px, py) in enumerate(_other_chips(x, y)):
                copies.append(pltpu.make_async_remote_copy(
                    src_ref=p_refs[a].at[2 * px + py], dst_ref=land_refs[a].at[my_chip],
                    send_sem=send_sems.at[3 * a + j], recv_sem=recv_sems.at[3 * a + j],
                    device_id=(px, py, c), device_id_type=MESH))
        return local, copies

    def start(p_refs, land_refs, sems):
        local, copies = plan(p_refs, land_refs, sems)
        for cp in local + copies:
            cp.start()

    def finish(p_refs, land_refs, sems):
        local, copies = plan(p_refs, land_refs, sems)
        for cp in copies:
            cp.wait()
        for cp in local:
            cp.wait()

    scratch = [pltpu.SemaphoreType.DMA((3 * n,)), pltpu.SemaphoreType.DMA((3 * n,)), pltpu.SemaphoreType.DMA((n,))]
    return _Comm(parts, [_sds(p.shape, BF16) for p in parts], scratch, start, finish)


def _sum_slots(land, name):
    k, r, c = land.shape
    tr = r // 2 if r % 32 == 0 else r

    def body(l_ref, o_ref):
        acc = l_ref[0].astype(F32)
        for i in range(1, k):
            acc = acc + l_ref[i].astype(F32)
        o_ref[...] = acc

    return pl.pallas_call(
        body, name=name, grid=(r // tr,), in_specs=[pl.BlockSpec((k, tr, c), lambda i: (0, i, 0))],
        out_specs=_rows(tr, c), out_shape=_sds((r, c)), compiler_params=_cparams(1),
    )(land)


def _sibling_swap(halves):
    n = len(halves)

    def body(*refs):
        h_refs, out_refs = refs[:n], refs[n:2 * n]
        send_sems, recv_sems = refs[2 * n:]
        x, y, c = _mesh_pos()
        copies = [pltpu.make_async_remote_copy(
            src_ref=h_refs[a], dst_ref=out_refs[a], send_sem=send_sems.at[a], recv_sem=recv_sems.at[a],
            device_id=(x, y, 1 - c), device_id_type=MESH) for a in range(n)]
        for cp in copies:
            cp.start()
        for cp in copies:
            cp.wait()

    return pl.pallas_call(
        body, name="grad_sibling_swap", out_shape=[_sds(h.shape) for h in halves],
        in_specs=[pl.BlockSpec(memory_space=pltpu.HBM)] * n, out_specs=[pl.BlockSpec(memory_space=pltpu.HBM)] * n,
        scratch_shapes=[pltpu.SemaphoreType.DMA((n,)), pltpu.SemaphoreType.DMA((n,))],
    )(*halves)


def _pad_rows(v, width):
    flat = v.reshape(-1)
    rows = -(-flat.shape[0] // width)
    rows = -(-rows // 8) * 8
    return jnp.pad(flat, (0, rows * width - flat.shape[0])).reshape(rows, width)


def _size(shape):
    n = 1
    for dim in shape:
        n *= dim
    return n


def _row_pack(arrs):
    pieces = []
    for a in arrs:
        rows = -(-a.size // D)
        pieces.append(jnp.pad(a.reshape(-1), (0, rows * D - a.size)).reshape(rows, D))
    total = sum(p.shape[0] for p in pieces)
    if total % 8:
        pieces.append(jnp.zeros((8 - total % 8, D), F32))
    return jnp.concatenate(pieces, axis=0)


def _row_unpack(packed, shapes):
    out, r0 = [], 0
    for shp in shapes:
        n = _size(shp)
        rows = -(-n // D)
        out.append(packed[r0:r0 + rows].reshape(-1)[:n].reshape(shp))
        r0 += rows
    return out


def _block_diag(w):
    eye = jnp.eye(8, dtype=w.dtype)
    return (w[:, :, None, :] * eye[:, None, :, None]).reshape(RGW, RGW)


def _diag_blocks(dense):
    r = dense.reshape(8, 64, 8, 64)
    return jnp.stack([r[n, :, n, :] for n in range(8)])


def _lane_row(v8):
    return jnp.zeros((1, BAP), F32).at[0, 8:16].set(v8.reshape(8))


def _chip_sums(gs, lands, names, c_arr):
    return [_chip_sum(g, l, c_arr, "chip_sum_" + n) for g, l, n in zip(gs, lands, names)]


def _reduce_parts(gs, names, c_arr, tag):
    return _chip_sums(gs, _comm_call(_exchange_comm(gs), "grad_sibling_exchange_" + tag), names, c_arr)


def _local_step(x, target, sw, ffn1_w, later_shards, c_arr):
    (g1, gmix, rg_cw8, rg_cb, wgates, gbias, lam_row, gdn_cw8, alog_row, dtb_row, gn, g2, gfin) = sw
    wg1, wu1, wd1 = ffn1_w

    (x1, a1, b1, fb1), gathered = _ffn_fwd(x, g1, wg1, wu1, wd1, "ffn1_fwd", comm=_weights_gather_comm(later_shards))
    win_sh, wout_sh, wg2, wu2, wd2 = _all_shards(gathered)
    w_in_full = jnp.transpose(win_sh, (1, 0, 2)).reshape(D, NSH * INSH)
    wout = wout_sh.reshape(D, D)
    w_in_groups = (w_in_full[:, 0:512], w_in_full[:, 512:1024], w_in_full[:, 1024:2560], w_in_full[:, 2560:3072],
                   jnp.pad(w_in_full[:, 3072:3088], ((0, 0), (0, BAP - BAW))))
    h2, p_rgx, p_gate, p_qkv, p_z, p_ba = _inproj(x1, gmix, w_in_groups, "in_proj")
    c_rg = _conv(p_rgx, rg_cw8, rg_cb, "rg_conv")
    c_qkv = _conv(p_qkv, gdn_cw8, jnp.zeros((1, QKVW), F32), "gdn_conv")
    a0, bb0, a1s, bb1, q, k, v, bg = _mix_prep(c_rg, c_qkv, p_ba, wgates, gbias, lam_row, alog_row, dtb_row, "mix_prep")
    hf, hb = _scan_pair(a0, bb0, a1s, bb1, False, "rg_scan")
    tmat, gu, gw, gqd, gkd, gat, gcd = _gdn_local_fwd(q, k, v, bg, "gdn_local_fwd")
    of, s0, vn0, ob, s1, vn1 = _gdn_seq_fwd(gu, gw, gqd, gkd, gat, gcd, "gdn_seq_fwd")
    x2, ymix = _outproj(x1, hf, hb, p_gate, of, ob, p_z, gn, wout, "out_proj")
    (x3, a2, b2, fb2), _ = _ffn_fwd(x2, g2, wg2, wu2, wd2, "ffn2_fwd")
    dx3, dob2, loss_blk, d_gfin = _loss_head(x3, target, gfin, "loss_head")

    dx2, d_g2, hb2, dab2, dbb2, _ = _ffn_bwd(x2, dx3, dob2, g2, a2, b2, wg2, wu2, wd2, "ffn2_bwd")
    d_ffn2 = [_tn(dab2, hb2, "ffn2_dwg"), _tn(dbb2, hb2, "ffn2_dwu"), _tn(fb2, dob2, "ffn2_dwd")]

    (d_hr, d_gate, d_os, d_z, d_gn, dx2b), lands = _outproj_bwd(dx2, hf, hb, p_gate, of, ob, p_z, gn, wout, "out_proj_bwd",
                                                               comm=_exchange_comm(d_ffn2))
    parts_ffn2 = _chip_sums(d_ffn2, lands, _BIG_NAMES[5:8], c_arr)
    d_wout = _tn(ymix, dx2b, "dw_out")[0]

    lam1, lam0 = _scan_pair(a1s, d_hr, a0, d_hr, True, "rg_scan_bwd")
    d_xc, d_pre, xcb, d_gbias, d_lam = _gates_bwd(c_rg, wgates, gbias, lam_row, lam0, lam1, hf, hb, "rg_gates_bwd")
    d_wgates = _tn(xcb, d_pre, "dw_gates")[0]
    d_prgx, d_rgcw8, d_rgcb = _conv_bwd(p_rgx, d_xc, rg_cw8, "rg_conv_bwd")

    sg = _gdn_seq_bwd(d_os, gw, gqd, gkd, gat, gcd, (s0, s1), (vn0, vn1), "gdn_seq_bwd")
    (dq, dk, dv, dbg), lands_ffn2 = _gdn_local_bwd(q, k, v, bg, tmat, d_os, (vn0, vn1), (sg[0:5], sg[5:10]), "gdn_local_bwd",
                                                  comm=_scatter_comm(parts_ffn2))
    d_cqkv, d_pba, d_alog, d_dtb = _prep_bwd(c_qkv, p_ba, alog_row, dtb_row, dq, dk, dv, dbg, "gdn_prep_bwd")
    d_pqkv, d_gdncw8, _ = _conv_bwd(p_qkv, d_cqkv, gdn_cw8, "gdn_conv_bwd")

    dps = (d_prgx, d_gate, d_pqkv, d_z, d_pba)
    dx1, dob1, d_gmix = _inproj_bwd(x1, dx2, gmix, dps, w_in_groups, "in_proj_bwd")
    d_win_groups = [_tn(h2, dp, "dw_in_%d" % i)[0] for i, dp in enumerate(dps)]
    d_win = jnp.concatenate(d_win_groups[:4] + [d_win_groups[4][:, :BAW]], axis=1)
    d_mix = [jnp.transpose(d_win.reshape(D, NSH, INSH), (1, 0, 2)), d_wout.reshape(NSH, OUTSH, D)]

    gx, d_g1, hb1, dab1, dbb1, lands = _ffn_bwd(x, dx1, dob1, g1, a1, b1, wg1, wu1, wd1, "ffn1_bwd", comm=_exchange_comm(d_mix))
    parts_mix = _chip_sums(d_mix, lands, _BIG_NAMES[3:5], c_arr)
    d_wg1, lands_mix = _tn(dab1, hb1, "ffn1_dwg", comm=_scatter_comm(parts_mix))
    parts_wg1 = _reduce_parts([d_wg1], _BIG_NAMES[0:1], c_arr, "ffn1_gate")
    d_wu1, lands_wg1 = _tn(dbb1, hb1, "ffn1_dwu", comm=_scatter_comm(parts_wg1))
    parts_wu1 = _reduce_parts([d_wu1], _BIG_NAMES[1:2], c_arr, "ffn1_up")
    d_wd1, lands_wu1 = _tn(fb1, dob1, "ffn1_dwd", comm=_scatter_comm(parts_wu1))
    parts_wd1 = _reduce_parts([d_wd1], _BIG_NAMES[2:3], c_arr, "ffn1_down")
    lands_ffn1 = lands_wg1 + lands_wu1 + _comm_call(_scatter_comm(parts_wd1), "grad_chip_scatter_ffn1_down")

    halves = [_sum_slots(l, "sum_chips_" + n) for l, n in zip(lands_ffn1 + lands_mix + lands_ffn2, _BIG_NAMES)]
    small = dict(
        ffn1_norm=d_g1, mix_norm=d_gmix, rg_conv_w=d_rgcw8[:4], rg_conv_b=d_rgcb,
        rg_gate_a_w=jnp.stack([_diag_blocks(d_wgates[:, RGW * i:RGW * (i + 1)]) for i in (0, 1)]),
        rg_gate_x_w=jnp.stack([_diag_blocks(d_wgates[:, RGW * i:RGW * (i + 1)]) for i in (2, 3)]),
        rg_gate_a_b=d_gbias[0, :2 * RGW].reshape(2, RGW), rg_gate_x_b=d_gbias[0, 2 * RGW:].reshape(2, RGW),
        rg_lambda=d_lam.reshape(2, RGW), gdn_conv_w=d_gdncw8[:4],
        gdn_a_log=d_alog[0, 8:16].reshape(2, NH), gdn_dt_bias=d_dtb[0, 8:16].reshape(2, NH),
        gdn_norm=d_gn, ffn2_norm=d_g2, final_norm=d_gfin)
    return loss_blk, gx, halves, small


_SMALL_NAMES = ("ffn1_norm", "mix_norm", "rg_conv_w", "rg_conv_b", "rg_gate_a_w", "rg_gate_a_b", "rg_gate_x_w",
                "rg_gate_x_b", "rg_lambda", "gdn_conv_w", "gdn_a_log", "gdn_dt_bias", "gdn_norm", "ffn2_norm", "final_norm")
_SMALL_SHARDED = dict(rg_conv_w=128, rg_gate_a_b=128, rg_gate_x_b=128, rg_lambda=128, gdn_conv_w=384)
_OUT_ORDER = ("ffn1_norm", "ffn1_w_gate", "ffn1_w_up", "ffn1_w_down", "mix_norm", "w_in", "w_out", "rg_conv_w", "rg_conv_b",
              "rg_gate_a_w", "rg_gate_a_b", "rg_gate_x_w", "rg_gate_x_b", "rg_lambda", "gdn_conv_w", "gdn_a_log",
              "gdn_dt_bias", "gdn_norm", "ffn2_norm", "ffn2_w_gate", "ffn2_w_up", "ffn2_w_down", "final_norm")
_BIG_NAMES = ("ffn1_w_gate", "ffn1_w_up", "ffn1_w_down", "w_in", "w_out", "ffn2_w_gate", "ffn2_w_up", "ffn2_w_down")
_TRANSPOSED = ("ffn1_w_gate", "ffn1_w_up", "ffn2_w_gate", "ffn2_w_up")


def kernel(x, ffn1_norm, ffn1_w_gate, ffn1_w_up, ffn1_w_down, mix_norm, w_in, w_out, rg_conv_w, rg_conv_b, rg_gate_a_w, rg_gate_a_b, rg_gate_x_w, rg_gate_x_b, rg_lambda, gdn_conv_w, gdn_a_log, gdn_dt_bias, gdn_norm, ffn2_norm, ffn2_w_gate, ffn2_w_up, ffn2_w_down, final_norm, loss_target, m_ffn1_norm, m_ffn1_w_gate, m_ffn1_w_up, m_ffn1_w_down, m_mix_norm, m_w_in, m_w_out, m_rg_conv_w, m_rg_conv_b, m_rg_gate_a_w, m_rg_gate_a_b, m_rg_gate_x_w, m_rg_gate_x_b, m_rg_lambda, m_gdn_conv_w, m_gdn_a_log, m_gdn_dt_bias, m_gdn_norm, m_ffn2_norm, m_ffn2_w_gate, m_ffn2_w_up, m_ffn2_w_down, m_final_norm, v_ffn1_norm, v_ffn1_w_gate, v_ffn1_w_up, v_ffn1_w_down, v_mix_norm, v_w_in, v_w_out, v_rg_conv_w, v_rg_conv_b, v_rg_gate_a_w, v_rg_gate_a_b, v_rg_gate_x_w, v_rg_gate_x_b, v_rg_lambda, v_gdn_conv_w, v_gdn_a_log, v_gdn_dt_bias, v_gdn_norm, v_ffn2_norm, v_ffn2_w_gate, v_ffn2_w_up, v_ffn2_w_down, v_final_norm):
    args = dict(locals())
    w = {n: args[n] for n in _OUT_ORDER}
    mom = {n: args["m_" + n] for n in _OUT_ORDER}
    var = {n: args["v_" + n] for n in _OUT_ORDER}
    xi, yi, ci = _mesh_pos()
    shard = 2 * xi + yi

    big_bf16 = [w[n][0].astype(BF16) for n in _BIG_NAMES]
    sm_local = _pad_rows(jnp.concatenate([w[n][0].reshape(-1) for n in _SMALL_SHARDED]), 128)
    first = _comm_call(_gather_comm(big_bf16[0:3] + [sm_local], pltpu.HBM, [t.shape[0] // 2 for t in big_bf16[0:3]] + [None]),
                       "gather_first_weights")
    ffn1_w = _all_shards(first[0:3])
    sm_all = first[3][0::2].reshape(NSH, -1)
    sm_full, off = {}, 0
    for n, wd_ in _SMALL_SHARDED.items():
        rows = w[n].shape[1]
        piece = sm_all[:, off:off + rows * wd_].reshape(NSH, rows, wd_)
        sm_full[n] = jnp.transpose(piece, (1, 0, 2)).reshape(rows, NSH * wd_)
        off += rows * wd_

    wa, wx = rg_gate_a_w[0], rg_gate_x_w[0]
    wgates = jnp.concatenate([_block_diag(wa[0]), _block_diag(wa[1]), _block_diag(wx[0]), _block_diag(wx[1])],
                             axis=1).astype(BF16)
    gbias = jnp.concatenate([sm_full["rg_gate_a_b"].reshape(1, -1), sm_full["rg_gate_x_b"].reshape(1, -1)], axis=1)
    sw = (ffn1_norm, mix_norm, jnp.pad(sm_full["rg_conv_w"], ((0, 4), (0, 0))), rg_conv_b, wgates, gbias,
          sm_full["rg_lambda"].reshape(1, -1), jnp.pad(sm_full["gdn_conv_w"], ((0, 4), (0, 0))), _lane_row(gdn_a_log),
          _lane_row(gdn_dt_bias), gdn_norm, ffn2_norm, final_norm.reshape(1, D))
    c_arr = ci.reshape(1).astype(jnp.int32)

    loss_blk, gx, halves, small = _local_step(x[0], loss_target[0], sw, ffn1_w, big_bf16[3:], c_arr)
    loss = lax.psum(loss_blk[0, 0], ("x", "y", "c"))
    grads = {}

    sm_grad = _row_pack([small[n] for n in _SMALL_NAMES])
    sm_sum = _sum_slots(_gather_small(sm_grad, "gather_small_grads"), "small_grad_sum")
    for n, g in zip(_SMALL_NAMES, _row_unpack(sm_sum, [small[n].shape for n in _SMALL_NAMES])):
        if n in _SMALL_SHARDED:
            wd_ = _SMALL_SHARDED[n]
            g = lax.dynamic_slice_in_dim(g, shard * wd_, wd_, axis=1)
        grads[n] = g.reshape(w[n].shape)

    delta, new_m, new_v = {}, {}, {}
    for n, own, recv in zip(_BIG_NAMES, halves, _sibling_swap(halves)):
        to2d = jnp.transpose if n in _TRANSPOSED else (lambda t: t)
        outs4 = _adamw_halves(to2d(w[n][0]), own, recv, to2d(mom[n][0]), to2d(var[n][0]), c_arr, "adamw_" + n)
        grads[n], delta[n], new_m[n], new_v[n] = [to2d(o)[None] for o in outs4]
    packs = [_row_pack([t[n] for n in _SMALL_NAMES]) for t in (w, grads, mom, var)]
    sm_shapes = [w[n].shape for n in _SMALL_NAMES]
    for dst, src in zip((delta, new_m, new_v), _adamw(*packs, "adamw_small")):
        for n, val in zip(_SMALL_NAMES, _row_unpack(src, sm_shapes)):
            dst[n] = val

    outs = [loss, gx[None]]
    for group in (grads, delta, new_m, new_v):
        outs += [group[n] for n in _OUT_ORDER]
    return tuple(outs)
```

```python
import functools

import jax
import jax.numpy as jnp
from jax import lax
from jax.experimental import pallas as pl
from jax.experimental.pallas import tpu as pltpu

F32 = jnp.float32
BF16 = jnp.bfloat16
EPS = 1e-6
D = 1024
NSH = 4
FSH = 704
RGW = 512
QKVW = 1536
ZW = 512
BAW = 16
BAP = 128
INSH = 772
OUTSH = 256
CHUNK = 64
NH = 4
DH = 128
RG_C = 8.0
VMEM_LIMIT = 52 * 1024 * 1024
MESH = pl.DeviceIdType.MESH

ADAM_LR = 0.001
ADAM_B1 = 0.9
ADAM_B2 = 0.999
ADAM_EPS = 1e-08
ADAM_WD = 0.01
ADAM_STEP = 10


def _cparams(n_grid):
    return pltpu.CompilerParams(dimension_semantics=("arbitrary",) * n_grid, vmem_limit_bytes=VMEM_LIMIT)


def _sig(x):
    return 0.5 + 0.5 * jnp.tanh(0.5 * x)


def _sig_pos(x):
    return 1.0 / (1.0 + jnp.exp(-x))


def _softplus(x):
    return jnp.maximum(x, 0.0) + jnp.log(1.0 + jnp.exp(-jnp.abs(x)))


def _neg_expm1(y):
    series = -y * (1.0 + y * (0.5 + y * (1.0 / 6 + y * (1.0 / 24 + y * (1.0 / 120 + y * (1.0 / 720 + y / 5040))))))
    return jnp.where(y > -0.3, series, 1.0 - jnp.exp(y))


_GELU_C = 0.7978845608028654


def _gelu(x):
    t = jnp.tanh(_GELU_C * (x + 0.044715 * x * x * x))
    return 0.5 * x * (1.0 + t)


def _gelu_grad(x):
    t = jnp.tanh(_GELU_C * (x + 0.044715 * x * x * x))
    return 0.5 * (1.0 + t) + 0.5 * x * (1.0 - t * t) * _GELU_C * (1.0 + 3 * 0.044715 * x * x)


def _silu_grad(x):
    s = _sig(x)
    return s * (1.0 + x * (1.0 - s))


def _dot(a, b):
    return jnp.dot(a.astype(BF16), b.astype(BF16), preferred_element_type=F32)


def _dot_nt(a, b):
    return lax.dot_general(a.astype(BF16), b.astype(BF16), (((1,), (1,)), ((), ())), preferred_element_type=F32)


def _dot_tn(a, b):
    return lax.dot_general(a.astype(BF16), b.astype(BF16), (((0,), (0,)), ((), ())), preferred_element_type=F32)


_NN = ((1,), (0,))
_NT = ((1,), (1,))
_TN = ((0,), (0,))


def _dg(a, b, dims):
    return lax.dot_general(a, b, (dims, ((), ())), preferred_element_type=F32)


def _split2(a):
    hi = a.astype(BF16)
    return hi, (a - hi.astype(F32)).astype(BF16)


def _dot3(a, b, dims=_NN):
    ah, al = _split2(a)
    bh, bl = _split2(b)
    return _dg(ah, bh, dims) + _dg(ah, bl, dims) + _dg(al, bh, dims)


def _dot_exact(e, x, dims, e_is_lhs):
    x0 = x.astype(BF16)
    r = x - x0.astype(F32)
    x1 = r.astype(BF16)
    x2 = (r - x1.astype(F32)).astype(BF16)
    eb = e.astype(BF16)
    if e_is_lhs:
        return _dg(eb, x0, dims) + _dg(eb, x1, dims) + _dg(eb, x2, dims)
    return _dg(x0, eb, dims) + _dg(x1, eb, dims) + _dg(x2, eb, dims)


def _rms(xv):
    r = lax.rsqrt(jnp.mean(xv * xv, axis=-1, keepdims=True) + EPS)
    return r, xv * r


def _rms_bwd(dy, xh, r, gain):
    dxh = dy * gain
    return r * (dxh - xh * jnp.mean(dxh * xh, axis=-1, keepdims=True))


def _colsum(v):
    return jnp.sum(v, axis=0, keepdims=True)


def _rows(t, c):
    return pl.BlockSpec((t, c), lambda i: (i, 0))


def _full(shape):
    n = len(shape)
    return pl.BlockSpec(shape, lambda i: (0,) * n)


def _sds(shape, dtype=F32):
    return jax.ShapeDtypeStruct(shape, dtype)


def _ffn_fwd(x, gain, wg, wu, wd, name, comm=None):
    s = x.shape[0]
    tm = min(512, s)

    def body(x_ref, g_ref, wg_ref, wu_ref, wd_ref, xo_ref, ga_ref, gb_ref, f_ref, h_sc, acc):
        j = pl.program_id(1)

        @pl.when(j == 0)
        def _():
            _, xh = _rms(x_ref[...])
            h_sc[...] = (xh * g_ref[...]).astype(BF16)
            acc[...] = jnp.zeros_like(acc)

        h = h_sc[...]

        a = jnp.dot(h, wg_ref[0], preferred_element_type=F32)
        b = jnp.dot(h, wu_ref[0], preferred_element_type=F32)
        sa = _sig(a)
        silu = a * sa
        fv = silu * b
        f = fv.astype(BF16)
        f_ref[0] = f
        ga_ref[0] = (sa * b + fv * (1.0 - sa)).astype(BF16)
        gb_ref[0] = silu.astype(BF16)
        acc[...] += jnp.dot(f, wd_ref[0], preferred_element_type=F32)

        @pl.when(j == NSH - 1)
        def _():
            xo_ref[...] = x_ref[...] + 0.5 * acc[...]

    return _pallas(
        body, comm, name=name, grid=(s // tm, NSH),
        in_specs=[pl.BlockSpec((tm, D), lambda i, j: (i, 0)), pl.BlockSpec((1, D), lambda i, j: (0, 0)),
                  pl.BlockSpec((1, D, FSH), lambda i, j: (j, 0, 0)), pl.BlockSpec((1, D, FSH), lambda i, j: (j, 0, 0)),
                  pl.BlockSpec((1, FSH, D), lambda i, j: (j, 0, 0))],
        out_specs=[pl.BlockSpec((tm, D), lambda i, j: (i, 0))] + [pl.BlockSpec((1, tm, FSH), lambda i, j: (j, i, 0))] * 3,
        out_shape=[_sds((s, D))] + [_sds((NSH, s, FSH), BF16)] * 3,
        scratch_shapes=[pltpu.VMEM((tm, D), BF16), pltpu.VMEM((tm, D), F32)],
        args=(x, gain, wg, wu, wd))


def _ffn_bwd(x, dout, do, gain, ga, gb, wg, wu, wd, name, comm=None):
    s = x.shape[0]
    tm = min(512, s)

    def hidden(do_ref, ga_ref, gb_ref, wd_ref, da_ref, db_ref):
        df = _dot_nt(do_ref[...], wd_ref[0])
        da_ref[0] = (df * ga_ref[0].astype(F32)).astype(BF16)
        db_ref[0] = (df * gb_ref[0].astype(F32)).astype(BF16)

    th = min(1024, s)
    tok = pl.BlockSpec((th, D), lambda i, j: (i, 0))
    sh = pl.BlockSpec((1, th, FSH), lambda i, j: (j, i, 0))
    (da, db), carried = _pallas(
        hidden, comm, name=name + "_hidden", grid=(s // th, NSH),
        in_specs=[tok, sh, sh, pl.BlockSpec((1, FSH, D), lambda i, j: (j, 0, 0))], out_specs=[sh, sh],
        out_shape=[_sds((NSH, s, FSH), BF16)] * 2, scratch_shapes=[], args=(do, ga, gb, wd))

    def inputs(x_ref, d_ref, g_ref, da_ref, db_ref, wg_ref, wu_ref, dx_ref, dg_ref, h_ref):
        @pl.when(pl.program_id(0) == 0)
        def _():
            dg_ref[...] = jnp.zeros_like(dg_ref)

        dh = jnp.zeros((tm, D), F32)
        for j in range(NSH):
            dh = dh + _dot_nt(da_ref[j], wg_ref[j]) + _dot_nt(db_ref[j], wu_ref[j])
        r, xh = _rms(x_ref[...])
        gv = g_ref[...]
        h_ref[...] = (xh * gv).astype(BF16)
        dg_ref[...] += _colsum(dh * xh)
        dx_ref[...] = d_ref[...] + _rms_bwd(dh, xh, r, gv)

    grads = pl.BlockSpec((NSH, tm, FSH), lambda i: (0, i, 0))
    resident = pl.BlockSpec((NSH, D, FSH), lambda i: (0, 0, 0), pipeline_mode=pl.Buffered(1))
    dx, dg, h = pl.pallas_call(
        inputs, name=name + "_input", grid=(s // tm,),
        in_specs=[_rows(tm, D), _rows(tm, D), _full((1, D)), grads, grads, resident, resident],
        out_specs=[_rows(tm, D), _full((1, D)), _rows(tm, D)],
        out_shape=[_sds((s, D)), _sds((1, D)), _sds((s, D), BF16)], compiler_params=_cparams(1),
    )(x, dout, gain, da, db, wg, wu)
    return dx, dg, h, da, db, carried


def _tn(a, b, name, comm=None):
    a_g = a.ndim == 3
    b_g = b.ndim == 3
    g = a.shape[0] if a_g else (b.shape[0] if b_g else 1)
    s, k = a.shape[-2:]
    n = b.shape[-1]
    ts = min(2048 if b.dtype == BF16 else 1024, s)

    def body(a_ref, b_ref, o_ref):
        @pl.when(pl.program_id(1) == 0)
        def _():
            o_ref[...] = jnp.zeros_like(o_ref)

        av = a_ref[0] if a_g else a_ref[...]
        bv = b_ref[0] if b_g else b_ref[...]
        o_ref[0] += _dot_tn(av, bv)

    a_spec = pl.BlockSpec((1, ts, k), lambda gi, si: (gi, si, 0)) if a_g else pl.BlockSpec((ts, k), lambda gi, si: (si, 0))
    b_spec = pl.BlockSpec((1, ts, n), lambda gi, si: (gi, si, 0)) if b_g else pl.BlockSpec((ts, n), lambda gi, si: (si, 0))
    outs, carried = _pallas(body, comm, name=name, grid=(g, s // ts), in_specs=[a_spec, b_spec],
                            out_specs=[pl.BlockSpec((1, k, n), lambda gi, si: (gi, 0, 0))], out_shape=[_sds((g, k, n))],
                            scratch_shapes=[], args=(a, b))
    return outs[0] if comm is None else (outs[0], carried)


_P_WIDTHS = (RGW, RGW, QKVW, ZW, BAP)


def _inproj(x1, gain, ws, name):
    s = x1.shape[0]
    tm = min(256, s)

    def body(x_ref, g_ref, *refs):
        w_refs = refs[:5]
        h_ref = refs[5]
        p_refs = refs[6:]
        _, xh = _rms(x_ref[...])
        h = (xh * g_ref[...]).astype(BF16)
        h_ref[...] = h
        for w_ref, p_ref in zip(w_refs, p_refs):
            p_ref[...] = jnp.dot(h, w_ref[...], preferred_element_type=F32)

    return pl.pallas_call(
        body, name=name, grid=(s // tm,),
        in_specs=[_rows(tm, D), _full((1, D))] + [_full((D, w)) for w in _P_WIDTHS],
        out_specs=[_rows(tm, D)] + [_rows(tm, w) for w in _P_WIDTHS],
        out_shape=[_sds((s, D), BF16)] + [_sds((s, w)) for w in _P_WIDTHS],
        compiler_params=_cparams(1),
    )(x1, gain, *ws)


def _inproj_bwd(x1, dx2, gain, dps, ws, name):
    s = x1.shape[0]
    tm = min(256, s)

    def body(x_ref, d_ref, g_ref, *refs):
        dp_refs = refs[:5]
        w_refs = refs[5:10]
        dx_ref, dxh_ref, dg_ref = refs[10:]

        @pl.when(pl.program_id(0) == 0)
        def _():
            dg_ref[...] = jnp.zeros_like(dg_ref)

        dh = jnp.zeros((tm, D), F32)
        for dp_ref, w_ref in zip(dp_refs, w_refs):
            dh = dh + _dot_nt(dp_ref[...], w_ref[...])
        r, xh = _rms(x_ref[...])
        dg_ref[...] += _colsum(dh * xh)
        dx = d_ref[...] + _rms_bwd(dh, xh, r, g_ref[...])
        dx_ref[...] = dx
        dxh_ref[...] = (0.5 * dx).astype(BF16)

    return pl.pallas_call(
        body, name=name, grid=(s // tm,),
        in_specs=[_rows(tm, D), _rows(tm, D), _full((1, D))] + [_rows(tm, w) for w in _P_WIDTHS]
        + [_full((D, w)) for w in _P_WIDTHS],
        out_specs=[_rows(tm, D), _rows(tm, D), _full((1, D))],
        out_shape=[_sds((s, D)), _sds((s, D), BF16), _sds((1, D))],
        compiler_params=_cparams(1),
    )(x1, dx2, gain, *dps, *ws)


def _halo_specs(s, t, c):
    nb8 = s // 8
    tb = t // 8
    prev = pl.BlockSpec((8, c), lambda i: (jnp.maximum(i * tb - 1, 0), 0))
    nxt = pl.BlockSpec((8, c), lambda i: (jnp.minimum((i + 1) * tb, nb8 - 1), 0))
    return prev, nxt


def _edge_masks(nb):
    i = pl.program_id(0)
    return jnp.where(i > 0, 1.0, 0.0).astype(F32), jnp.where(i < nb - 1, 1.0, 0.0).astype(F32)


def _shifted(xx, off, t):
    n = t + 16
    sh = (-off) % n
    rolled = xx if sh == 0 else pltpu.roll(xx, sh, 0)
    return rolled[8:8 + t]


def _conv(x, w8, bias, name):
    s, c = x.shape
    t = min(256, s)
    nb = s // t

    def body(x_ref, xp_ref, xn_ref, w_ref, b_ref, o_ref):
        pm, nm = _edge_masks(nb)
        for c0 in range(0, c, 512):
            cols = slice(c0, c0 + 512)
            xx = jnp.concatenate([xp_ref[:, cols] * pm, x_ref[:, cols], xn_ref[:, cols] * nm], axis=0)
            acc = jnp.zeros((t, 512), F32) + b_ref[:, cols]
            for j in range(4):
                acc = acc + w_ref[j:j + 1, cols] * _shifted(xx, j - 2, t)
            o_ref[:, cols] = acc

    prev, nxt = _halo_specs(s, t, c)
    return pl.pallas_call(
        body, name=name, grid=(nb,),
        in_specs=[_rows(t, c), prev, nxt, _full((8, c)), _full((1, c))],
        out_specs=_rows(t, c), out_shape=_sds((s, c)), compiler_params=_cparams(1),
    )(x, x, x, w8, bias)


def _conv_bwd(x, dc, w8, name):
    s, c = x.shape
    t = min(256, s)
    nb = s // t

    def body(x_ref, d_ref, dp_ref, dn_ref, w_ref, dx_ref, dw_ref, db_ref):
        @pl.when(pl.program_id(0) == 0)
        def _():
            dw_ref[...] = jnp.zeros_like(dw_ref)
            db_ref[...] = jnp.zeros_like(db_ref)

        pm, nm = _edge_masks(nb)
        for c0 in range(0, c, 512):
            cols = slice(c0, c0 + 512)
            dd = jnp.concatenate([dp_ref[:, cols] * pm, d_ref[:, cols], dn_ref[:, cols] * nm], axis=0)
            xv = x_ref[:, cols]
            acc = jnp.zeros((t, 512), F32)
            for j in range(4):
                dsh = _shifted(dd, 2 - j, t)
                acc = acc + w_ref[j:j + 1, cols] * dsh
                dw_ref[j:j + 1, cols] += _colsum(dsh * xv)
            dx_ref[:, cols] = acc
            db_ref[:, cols] += _colsum(d_ref[:, cols])

    prev, nxt = _halo_specs(s, t, c)
    return pl.pallas_call(
        body, name=name, grid=(nb,),
        in_specs=[_rows(t, c), _rows(t, c), prev, nxt, _full((8, c))],
        out_specs=[_rows(t, c), _full((8, c)), _full((1, c))],
        out_shape=[_sds((s, c)), _sds((8, c)), _sds((1, c))], compiler_params=_cparams(1),
    )(x, dc, dc, dc, w8)


def _rg_gates(xc, pre, lam_row):
    sp8 = RG_C * _softplus(-lam_row)
    out = []
    for d in range(2):
        r = _sig_pos(pre[:, RGW * d:RGW * (d + 1)])
        gi = _sig(pre[:, 2 * RGW + RGW * d:2 * RGW + RGW * (d + 1)])
        la = -r * sp8[:, RGW * d:RGW * (d + 1)]
        a = jnp.exp(la)
        mult = jnp.sqrt(_neg_expm1(2.0 * la))
        out.append((r, gi, a, mult))
    return out


def _mix_prep(c_rg, c_qkv, p_ba, wgates, gbias, lam_row, alog_row, dtb_row, name):
    s = c_rg.shape[0]
    t = min(256, s)

    def body(xc_ref, cq_ref, pc_ref, wg_ref, gb_ref, lam_ref, alog_ref, dtb_ref,
             a0_ref, b0_ref, a1_ref, b1_ref, q_ref, k_ref, v_ref, bg_ref):
        xc = xc_ref[...]
        pre = _dot(xc, wg_ref[...]) + gb_ref[...]
        gates = _rg_gates(xc, pre, lam_ref[...])
        for (r, gi, a, mult), a_ref, b_ref in zip(gates, (a0_ref, a1_ref), (b0_ref, b1_ref)):
            a_ref[...] = a
            b_ref[...] = mult * gi * xc
        cq = cq_ref[...]
        sq = cq * _sig(cq)
        for h in range(NH):
            sl = slice(DH * h, DH * (h + 1))
            qh = sq[:, sl]
            q_ref[:, sl] = qh * lax.rsqrt(jnp.sum(qh * qh, axis=-1, keepdims=True) + EPS) * (DH ** -0.5)
            kh = sq[:, RGW + DH * h:RGW + DH * (h + 1)]
            k_ref[:, sl] = kh * lax.rsqrt(jnp.sum(kh * kh, axis=-1, keepdims=True) + EPS)
        v_ref[...] = sq[:, 2 * RGW:]
        pc = pc_ref[...]
        lane = lax.broadcasted_iota(jnp.int32, pc.shape, 1)
        beta = _sig(pc)
        g = -jnp.exp(alog_ref[...]) * _softplus(pc + dtb_ref[...])
        bg_ref[...] = jnp.where(lane < 8, beta, jnp.where(lane < 16, g, 0.0))

    return pl.pallas_call(
        body, name=name, grid=(s // t,),
        in_specs=[_rows(t, RGW), _rows(t, QKVW), _rows(t, BAP), _full((RGW, 4 * RGW)), _full((1, 4 * RGW)),
                  _full((1, 2 * RGW)), _full((1, BAP)), _full((1, BAP))],
        out_specs=[_rows(t, RGW)] * 7 + [_rows(t, BAP)],
        out_shape=[_sds((s, RGW))] * 7 + [_sds((s, BAP))],
        compiler_params=_cparams(1),
    )(c_rg, c_qkv, p_ba, wgates, gbias, lam_row, alog_row, dtb_row)


def _block_scan(av, bv, row, downwards):
    for k in (1, 2, 4):
        sh = (8 - k) if downwards else k
        m = (row < 8 - k) if downwards else (row >= k)
        a_s = pltpu.roll(av, sh, 0)
        b_s = pltpu.roll(bv, sh, 0)
        bv = jnp.where(m, av * b_s + bv, bv)
        av = jnp.where(m, av * a_s, av)
    return av, bv


def _scan_pair(af, bf, ar, br, shifted, name):
    s, c = af.shape
    t = min(512, s)
    nb = s // t
    ng = t // 8
    tb = t // 8
    up = lambda i: (i, 0)
    down = lambda i: (nb - 1 - i, 0)

    def body(*refs):
        if shifted:
            af_ref, bf_ref, ar_ref, br_ref, afp_ref, arn_ref, hf_ref, hr_ref, carry, fbuf, rbuf = refs
        else:
            af_ref, bf_ref, ar_ref, br_ref, hf_ref, hr_ref, carry = refs
        i = pl.program_id(0)

        @pl.when(i == 0)
        def _():
            carry[...] = jnp.zeros_like(carry)

        if shifted:
            edge = jnp.where(i > 0, 1.0, 0.0).astype(F32)
            fbuf[0:8, :] = afp_ref[...] * edge
            fbuf[8:t + 8, :] = af_ref[...]
            rbuf[0:t, :] = ar_ref[...]
            rbuf[t:t + 8, :] = arn_ref[...] * edge
        row = lax.broadcasted_iota(jnp.int32, (8, c), 0)

        def group(gi, cvs):
            cf, cr = cvs
            rf = pl.multiple_of(gi * 8, 8)
            rr = pl.multiple_of((ng - 1 - gi) * 8, 8)
            if shifted:
                a_f = jnp.where(row > 0, pltpu.roll(fbuf[pl.ds(rf + 8, 8), :], 1, 0), pltpu.roll(fbuf[pl.ds(rf, 8), :], 1, 0))
                a_r = jnp.where(row < 7, pltpu.roll(rbuf[pl.ds(rr, 8), :], 7, 0), pltpu.roll(rbuf[pl.ds(rr + 8, 8), :], 7, 0))
            else:
                a_f = af_ref[pl.ds(rf, 8), :]
                a_r = ar_ref[pl.ds(rr, 8), :]
            a_f, b_f = _block_scan(a_f, bf_ref[pl.ds(rf, 8), :], row, False)
            a_r, b_r = _block_scan(a_r, br_ref[pl.ds(rr, 8), :], row, True)
            h_f = a_f * cf + b_f
            h_r = a_r * cr + b_r
            hf_ref[pl.ds(rf, 8), :] = h_f
            hr_ref[pl.ds(rr, 8), :] = h_r
            return h_f[7:8, :], h_r[0:1, :]

        cf, cr = lax.fori_loop(0, ng, group, (carry[0:1, :], carry[8:9, :]))
        carry[0:1, :] = cf
        carry[8:9, :] = cr

    in_specs = [pl.BlockSpec((t, c), up), pl.BlockSpec((t, c), up), pl.BlockSpec((t, c), down), pl.BlockSpec((t, c), down)]
    args = [af, bf, ar, br]
    scratch = [pltpu.VMEM((16, c), F32)]
    if shifted:
        in_specs += [pl.BlockSpec((8, c), lambda i: (jnp.maximum(i * tb - 1, 0), 0)),
                     pl.BlockSpec((8, c), lambda i: (jnp.minimum((nb - i) * tb, s // 8 - 1), 0))]
        args += [af, ar]
        scratch += [pltpu.VMEM((t + 8, c), F32), pltpu.VMEM((t + 8, c), F32)]
    return pl.pallas_call(
        body, name=name, grid=(nb,), in_specs=in_specs,
        out_specs=[pl.BlockSpec((t, c), up), pl.BlockSpec((t, c), down)], out_shape=[_sds((s, c)), _sds((s, c))],
        scratch_shapes=scratch, compiler_params=_cparams(1),
    )(*args)


def _gates_bwd(xc, wgates, gbias, lam_row, lam0, lam1, hf, hb, name):
    s = xc.shape[0]
    t = min(256, s)
    nb = s // t

    def body(xc_ref, wg_ref, gb_ref, lam_ref, l0_ref, l1_ref, hf_ref, hfp_ref, hfn_ref, hb_ref, hbp_ref, hbn_ref,
             dxc_ref, dpre_ref, xcb_ref, dgb_ref, dlam_ref):
        @pl.when(pl.program_id(0) == 0)
        def _():
            dgb_ref[...] = jnp.zeros_like(dgb_ref)
            dlam_ref[...] = jnp.zeros_like(dlam_ref)

        pm, nm = _edge_masks(nb)
        h_prev = _shifted(jnp.concatenate([hfp_ref[...] * pm, hf_ref[...], hfn_ref[...] * nm], axis=0), -1, t)
        h_next = _shifted(jnp.concatenate([hbp_ref[...] * pm, hb_ref[...], hbn_ref[...] * nm], axis=0), 1, t)
        h_shift = (h_prev, h_next)
        xv = xc_ref[...]
        pre = _dot(xv, wg_ref[...]) + gb_ref[...]
        lam_row_v = lam_ref[...]
        sp8 = RG_C * _softplus(-lam_row_v)
        dsp_dlam = -RG_C * _sig(-lam_row_v)
        gates = _rg_gates(xv, pre, lam_row_v)
        dxc = jnp.zeros((t, RGW), F32)
        dpre_r = []
        dpre_i = []
        for d, ((r, gi, a, mult), l_ref, hs) in enumerate(zip(gates, (l0_ref, l1_ref), h_shift)):
            dbb = l_ref[...]
            da = dbb * hs
            cs = slice(RGW * d, RGW * (d + 1))
            dmult = dbb * gi * xv
            dgi = dbb * mult * xv
            dxc = dxc + dbb * mult * gi
            dla = da * a - dmult * a * a / mult
            dr = -dla * sp8[:, cs]
            dlam_ref[:, cs] += _colsum(-dla * r) * dsp_dlam[:, cs]
            dpre_r.append(dr * r * (1.0 - r))
            dpre_i.append(dgi * gi * (1.0 - gi))
        dpre = jnp.concatenate(dpre_r + dpre_i, axis=1)
        dgb_ref[...] += _colsum(dpre)
        dpre_b = dpre.astype(BF16)
        dpre_ref[...] = dpre_b
        xcb_ref[...] = xv.astype(BF16)
        dxc_ref[...] = dxc + _dot_nt(dpre_b, wg_ref[...])

    prev, nxt = _halo_specs(s, t, RGW)
    return pl.pallas_call(
        body, name=name, grid=(s // t,),
        in_specs=[_rows(t, RGW), _full((RGW, 4 * RGW)), _full((1, 4 * RGW)), _full((1, 2 * RGW))] + [_rows(t, RGW)] * 2
        + [_rows(t, RGW), prev, nxt] * 2,
        out_specs=[_rows(t, RGW), _rows(t, 4 * RGW), _rows(t, RGW), _full((1, 4 * RGW)), _full((1, 2 * RGW))],
        out_shape=[_sds((s, RGW)), _sds((s, 4 * RGW), BF16), _sds((s, RGW), BF16), _sds((1, 4 * RGW)), _sds((1, 2 * RGW))],
        compiler_params=_cparams(1),
    )(xc, wgates, gbias, lam_row, lam0, lam1, hf, hf, hf, hb, hb, hb)


class _GdnMasks:
    def __init__(self, d):
        ri = lax.broadcasted_iota(jnp.int32, (CHUNK, CHUNK), 0)
        ci = lax.broadcasted_iota(jnp.int32, (CHUNK, CHUNK), 1)
        self.incl = (ri >= ci) if d == 0 else (ri <= ci)
        self.strict = (ri > ci) if d == 0 else (ri < ci)
        b16 = jnp.right_shift(ri, 4) == jnp.right_shift(ci, 4)
        b32 = jnp.right_shift(ri, 5) == jnp.right_shift(ci, 5)
        self.diag16 = b16
        self.off32 = jnp.logical_and(b32, jnp.logical_not(b16))
        self.off64 = jnp.logical_not(b32)
        self.eye = jnp.where(ri == ci, 1.0, 0.0).astype(F32)
        self.tri = jnp.where(self.incl, 1.0, 0.0).astype(F32)
        self.last = CHUNK - 1 if d == 0 else 0


def _tri_inv(lmat, m):
    return _tri_inv_many([lmat], [m])[0]


def _tri_inv_many(lmats, masks):
    n = len(lmats)
    ns = [jnp.where(masks[i].diag16, lmats[i], 0.0) for i in range(n)]
    ps = [masks[i].eye - ns[i] for i in range(n)]
    qs = [_dot3(ns[i], ns[i]) for i in range(n)]
    for step in range(3):
        ps = [_dot3(ps[i], masks[i].eye + qs[i]) for i in range(n)]
        if step < 2:
            qs = [_dot3(qs[i], qs[i]) for i in range(n)]
    for off in ("off32", "off64"):
        ts = [_dot3(ps[i], jnp.where(getattr(masks[i], off), lmats[i], 0.0)) for i in range(n)]
        ps = [ps[i] - _dot3(ts[i], ps[i]) for i in range(n)]
    return ps


def _chunk_cumsums(m, bgv):
    return _dot_exact(m.tri, bgv, _NN, True), _dot_exact(m.tri, bgv, ((0,), (1,)), False)


class _GdnHead:
    def __init__(self, qh, kh, vh, kk, q0, bg, gcs, gcs_t, d, h, m):
        cb = 4 * d + h
        cg = 8 + 4 * d + h
        self.q, self.k, self.v = qh, kh, vh
        self.beta = bg[:, cb:cb + 1]
        gcol = gcs[:, cg:cg + 1]
        grow = gcs_t[cg:cg + 1, :]
        gl = gcs[m.last:m.last + 1, cg:cg + 1]
        self.decay = jnp.exp(jnp.where(m.incl, gcol - grow, -1e30))
        self.kb = kh * self.beta
        self.vb = vh * self.beta
        self.a0 = kk * self.beta
        self.q0 = q0
        self.lmat = jnp.where(m.strict, self.a0 * self.decay, 0.0)
        self.attn = self.q0 * self.decay
        self.eg = jnp.exp(gcol)
        self.ek = jnp.exp(gl - gcol)
        self.cd = jnp.exp(gl)
        self.kg = self.kb * self.eg
        self.qd = qh * self.eg
        self.kd = kh * self.ek


HW = NH * DH
SEQ_CB = 4
LOCAL_CB = 4


def _head(h):
    return slice(DH * h, DH * (h + 1))


def _gdn_local_fwd(q, k, v, bg, name):
    s = q.shape[0]
    n = s // CHUNK
    cb = min(LOCAL_CB, n)

    def body(q_ref, k_ref, v_ref, bg_ref, t_ref, u_ref, w_ref, qd_ref, kd_ref, at_ref, cd_ref):
        masks = [_GdnMasks(d) for d in range(2)]
        inst = []
        for jj in range(cb):
            rows = slice(CHUNK * jj, CHUNK * (jj + 1))
            bgv = bg_ref[rows, :]
            qs = [q_ref[rows, _head(h)] for h in range(NH)]
            ks = [k_ref[rows, _head(h)] for h in range(NH)]
            kk = [_dot_nt(ks[h], ks[h]) for h in range(NH)]
            q0 = [_dot_nt(qs[h], ks[h]) for h in range(NH)]
            for d, m in enumerate(masks):
                gcs, gcs_t = _chunk_cumsums(m, bgv)
                for h in range(NH):
                    c = _GdnHead(qs[h], ks[h], v_ref[rows, _head(h)], kk[h], q0[h], bgv, gcs, gcs_t, d, h, m)
                    inst.append((jj, rows, d, h, m, c))
        tms = _tri_inv_many([it[-1].lmat for it in inst], [it[-2] for it in inst])
        for (jj, rows, d, h, m, c), tm in zip(inst, tms):
            sl = _head(h)
            t_ref[jj, d, h] = tm
            u_ref[d, rows, sl] = _dot(tm, c.vb)
            w_ref[d, rows, sl] = _dot(tm, c.kg).astype(BF16)
            qd_ref[d, rows, sl] = c.qd.astype(BF16)
            kd_ref[d, rows, sl] = c.kd.astype(BF16)
            at_ref[jj, d, h] = c.attn.astype(BF16)
            cd_ref[jj, 4 * d + h:4 * d + h + 1, :] = jnp.broadcast_to(c.cd, (1, DH))

    tok = _rows(cb * CHUNK, HW)
    tok2 = pl.BlockSpec((2, cb * CHUNK, HW), lambda i: (0, i, 0))
    mat = pl.BlockSpec((cb, 2, NH, CHUNK, CHUNK), lambda i: (i, 0, 0, 0, 0))
    return pl.pallas_call(
        body, name=name, grid=(n // cb,), in_specs=[tok, tok, tok, _rows(cb * CHUNK, BAP)],
        out_specs=[mat, tok2, tok2, tok2, tok2, mat, pl.BlockSpec((cb, 8, DH), lambda i: (i, 0, 0))],
        out_shape=[_sds((n, 2, NH, CHUNK, CHUNK)), _sds((2, s, HW)), _sds((2, s, HW), BF16), _sds((2, s, HW), BF16),
                   _sds((2, s, HW), BF16), _sds((n, 2, NH, CHUNK, CHUNK), BF16), _sds((n, 8, DH))],
        compiler_params=_cparams(1),
    )(q, k, v, bg)


def _seq_specs(s, order):
    n = s // CHUNK
    cb = min(SEQ_CB, n)
    nb = n // cb
    tb = cb * CHUNK

    def blk(d):
        return (lambda i: i) if order[d] else (lambda i: nb - 1 - i)

    def per_dir(make):
        return [make(d, blk(d)) for d in range(2)]

    tok2 = per_dir(lambda d, f: pl.BlockSpec((1, tb, HW), lambda i: (d, f(i), 0)))
    tok = per_dir(lambda d, f: pl.BlockSpec((tb, HW), lambda i: (f(i), 0)))
    mat = per_dir(lambda d, f: pl.BlockSpec((cb, 1, NH, CHUNK, CHUNK), lambda i: (f(i), d, 0, 0, 0)))
    cds = per_dir(lambda d, f: pl.BlockSpec((cb, 8, DH), lambda i: (f(i), 0, 0)))
    sts = per_dir(lambda d, f: pl.BlockSpec((cb, NH, DH, DH), lambda i: (f(i), 0, 0, 0)))
    dcd = per_dir(lambda d, f: pl.BlockSpec((cb, NH, DH), lambda i: (f(i), 0, 0)))
    return n, cb, nb, tok2, tok, mat, cds, sts, dcd


class _ScanRider:
    def __init__(self, af, bf, ar, br, shifted, tb, nb, up_spec, down_spec):
        s, c = af.shape
        self.shifted, self.t, self.c, self.nb = shifted, tb, c, nb
        self.args = [af, bf, ar, br]
        self.in_specs = [up_spec, up_spec, down_spec, down_spec]
        self.scratch = [pltpu.VMEM((16, c), F32)]
        if shifted:
            tb8 = tb // 8
            self.args += [af, ar]
            self.in_specs += [pl.BlockSpec((8, c), lambda i: (jnp.maximum(i * tb8 - 1, 0), 0)),
                              pl.BlockSpec((8, c), lambda i: (jnp.minimum((nb - i) * tb8, s // 8 - 1), 0))]
            self.scratch += [pltpu.VMEM((tb + 8, c), F32), pltpu.VMEM((tb + 8, c), F32)]
        self.out_specs = [up_spec, down_spec]
        self.out_shape = [_sds((s, c)), _sds((s, c))]

    def begin(self, in_refs, out_refs, scratch_refs):
        i = pl.program_id(0)
        self.carry = scratch_refs[0]

        @pl.when(i == 0)
        def _():
            self.carry[...] = jnp.zeros_like(self.carry)

        af_ref, self.bf_ref, ar_ref, self.br_ref = in_refs[0:4]
        self.hf_ref, self.hr_ref = out_refs
        self.a_up, self.a_dn = af_ref, ar_ref
        if self.shifted:
            t = self.t
            edge = jnp.where(i > 0, 1.0, 0.0).astype(F32)
            fbuf, rbuf = scratch_refs[1:3]
            fbuf[0:8, :] = in_refs[4][...] * edge
            fbuf[8:t + 8, :] = af_ref[...]
            rbuf[0:t, :] = ar_ref[...]
            rbuf[t:t + 8, :] = in_refs[5][...] * edge
            self.a_up, self.a_dn = fbuf, rbuf
        self.row = lax.broadcasted_iota(jnp.int32, (8, self.c), 0)
        self.cf, self.cr = self.carry[0:1, :], self.carry[8:9, :]

    def groups(self, lo, hi):
        ng = self.t // 8
        row = self.row
        for gi in range(lo, hi):
            rf, rr = 8 * gi, 8 * (ng - 1 - gi)
            if self.shifted:
                a_f = jnp.where(row > 0, pltpu.roll(self.a_up[rf + 8:rf + 16, :], 1, 0), pltpu.roll(self.a_up[rf:rf + 8, :], 1, 0))
                a_r = jnp.where(row < 7, pltpu.roll(self.a_dn[rr:rr + 8, :], 7, 0), pltpu.roll(self.a_dn[rr + 8:rr + 16, :], 7, 0))
            else:
                a_f, a_r = self.a_up[rf:rf + 8, :], self.a_dn[rr:rr + 8, :]
            a_f, b_f = _block_scan(a_f, self.bf_ref[rf:rf + 8, :], row, False)
            a_r, b_r = _block_scan(a_r, self.br_ref[rr:rr + 8, :], row, True)
            h_f = a_f * self.cf + b_f
            h_r = a_r * self.cr + b_r
            self.hf_ref[rf:rf + 8, :] = h_f
            self.hr_ref[rr:rr + 8, :] = h_r
            self.cf, self.cr = h_f[7:8, :], h_r[0:1, :]

    def end(self):
        self.carry[0:1, :] = self.cf
        self.carry[8:9, :] = self.cr


def _gdn_seq_fwd(u, w, qd, kd, at, cd, name, scan=None):
    s = u.shape[1]
    n, cb, nb, tok2, tok, mat, cds, sts, _ = _seq_specs(s, (True, False))
    rider = _ScanRider(*scan, False, cb * CHUNK, nb, tok[0], tok[1]) if scan else None
    ri = len(rider.args) if rider else 0

    def body(*refs):
        ins = (refs[0:6], refs[6:12])
        outs = (refs[12 + ri:15 + ri], refs[15 + ri:18 + ri])
        st = refs[18 + ri + (2 if rider else 0)]
        if rider:
            rider.begin(refs[12:12 + ri], refs[18 + ri:20 + ri], refs[21 + ri:])

        @pl.when(pl.program_id(0) == 0)
        def _():
            st[...] = jnp.zeros_like(st)

        for j in range(cb):
            items = []
            for d in range(2):
                jj = j if d == 0 else cb - 1 - j
                items += [(d, h, jj, slice(CHUNK * jj, CHUNK * (jj + 1)), _head(h)) for h in range(NH)]
            shs = [st[d, h] for d, h, _, _, _ in items]
            wss = [_dot(ins[d][1][0, rows, sl], sh) for (d, h, jj, rows, sl), sh in zip(items, shs)]
            vns = [ins[d][0][0, rows, sl] - ws for (d, h, jj, rows, sl), ws in zip(items, wss)]
            news = [sh * ins[d][5][jj, 4 * d + h:4 * d + h + 1, :] + _dot_tn(ins[d][3][0, rows, sl], vn)
                    for (d, h, jj, rows, sl), sh, vn in zip(items, shs, vns)]
            for (d, h, jj, rows, sl), sh, vn, new in zip(items, shs, vns, news):
                o_r, s_r, vn_r = outs[d]
                st[d, h] = new
                s_r[jj, h] = sh.astype(BF16)
                vn_r[rows, sl] = vn
                o_r[rows, sl] = _dot(ins[d][2][0, rows, sl], sh) + _dot(ins[d][4][jj, 0, h], vn)
            if rider:
                rider.groups(8 * j, 8 * (j + 1))
        if rider:
            rider.end()

    in_specs, out_specs, out_shape = [], [], []
    for d in range(2):
        in_specs += [tok2[d]] * 4 + [mat[d], cds[d]]
        out_specs += [tok[d], sts[d], tok[d]]
        out_shape += [_sds((s, HW)), _sds((n, NH, DH, DH), BF16), _sds((s, HW))]
    args = [u, w, qd, kd, at, cd, u, w, qd, kd, at, cd]
    scratch = [pltpu.VMEM((2, NH, DH, DH), F32)]
    if rider:
        in_specs, args = in_specs + rider.in_specs, args + rider.args
        out_specs, out_shape, scratch = out_specs + rider.out_specs, out_shape + rider.out_shape, scratch + rider.scratch
    return pl.pallas_call(
        body, name=name, grid=(nb,), in_specs=in_specs, out_specs=out_specs, out_shape=out_shape,
        scratch_shapes=scratch, compiler_params=_cparams(1),
    )(*args)


def _gdn_seq_bwd(do, w, qd, kd, at, cd, states, vns, name, scan=None):
    s = do.shape[0]
    n, cb, nb, tok2, tok, mat, cds, sts, dcd = _seq_specs(s, (False, True))
    rider = _ScanRider(*scan, True, cb * CHUNK, nb, tok[1], tok[0]) if scan else None
    ri = len(rider.args) if rider else 0

    def body(*refs):
        ins = (refs[0:8], refs[8:16])
        outs = (refs[16 + ri:21 + ri], refs[21 + ri:26 + ri])
        dst = refs[26 + ri + (2 if rider else 0)]
        if rider:
            rider.begin(refs[16:16 + ri], refs[26 + ri:28 + ri], refs[29 + ri:])

        @pl.when(pl.program_id(0) == 0)
        def _():
            dst[...] = jnp.zeros_like(dst)

        for j in range(cb):
            items = []
            for d in range(2):
                jj = cb - 1 - j if d == 0 else j
                items += [(d, h, jj, slice(CHUNK * jj, CHUNK * (jj + 1)), _head(h)) for h in range(NH)]
            dsns = [dst[d, h] for d, h, _, _, _ in items]
            dohs = [ins[d][0][rows, sl] for d, h, jj, rows, sl in items]
            d_vns = [_dot_tn(ins[d][4][jj, 0, h], doh) + _dot(ins[d][3][0, rows, sl], dsn)
                     for (d, h, jj, rows, sl), doh, dsn in zip(items, dohs, dsns)]
            news = [ins[d][5][jj, 4 * d + h:4 * d + h + 1, :] * dsn + _dot_tn(ins[d][2][0, rows, sl], doh)
                    - _dot_tn(ins[d][1][0, rows, sl], d_vn)
                    for (d, h, jj, rows, sl), doh, dsn, d_vn in zip(items, dohs, dsns, d_vns)]
            for (d, h, jj, rows, sl), doh, dsn, d_vn, new in zip(items, dohs, dsns, d_vns, news):
                dvn_r, dkd_r, dqd_r, dw_r, dcd_r = outs[d]
                sh = ins[d][6][jj, h].astype(F32)
                dst[d, h] = new
                dvn_r[rows, sl] = d_vn
                dkd_r[rows, sl] = _dot_nt(ins[d][7][rows, sl], dsn)
                dqd_r[rows, sl] = _dot_nt(doh, sh)
                dw_r[rows, sl] = -_dot_nt(d_vn, sh)
                d_cd = jnp.sum(jnp.sum(sh * dsn, axis=1, keepdims=True), axis=0, keepdims=True)
                dcd_r[jj, h:h + 1, :] = jnp.broadcast_to(d_cd, (1, DH))
            if rider:
                rider.groups(8 * j, 8 * (j + 1))
        if rider:
            rider.end()

    in_specs, out_specs, out_shape, args = [], [], [], []
    for d in range(2):
        in_specs += [tok[d]] + [tok2[d]] * 3 + [mat[d], cds[d], sts[d], tok[d]]
        args += [do, w, qd, kd, at, cd, states[d], vns[d]]
        out_specs += [tok[d]] * 4 + [dcd[d]]
        out_shape += [_sds((s, HW))] * 4 + [_sds((n, NH, DH))]
    scratch = [pltpu.VMEM((2, NH, DH, DH), F32)]
    if rider:
        in_specs, args = in_specs + rider.in_specs, args + rider.args
        out_specs, out_shape, scratch = out_specs + rider.out_specs, out_shape + rider.out_shape, scratch + rider.scratch
    return pl.pallas_call(
        body, name=name, grid=(nb,), in_specs=in_specs, out_specs=out_specs, out_shape=out_shape,
        scratch_shapes=scratch, compiler_params=_cparams(1),
    )(*args)


def _gdn_local_bwd(q, k, v, bg, tmat, do, vns, seq_grads, name, comm=None):
    s = q.shape[0]
    n = s // CHUNK
    cb = min(LOCAL_CB, n)

    def body(*refs):
        q_ref, k_ref, v_ref, bg_ref, t_ref, do_ref = refs[0:6]
        vn_refs = refs[6:8]
        sg = (refs[8:13], refs[13:18])
        dq_ref, dk_ref, dv_ref, dbg_ref = refs[18:]
        lane = lax.broadcasted_iota(jnp.int32, (CHUNK, BAP), 1)
        rowi = lax.broadcasted_iota(jnp.int32, (CHUNK, 1), 0)
        ones = jnp.ones((CHUNK, DH), F32)
        masks = [_GdnMasks(d) for d in range(2)]
        inst = []
        for jj in range(cb):
            rows = slice(CHUNK * jj, CHUNK * (jj + 1))
            bgv = bg_ref[rows, :]
            qs = [q_ref[rows, _head(h)] for h in range(NH)]
            ks = [k_ref[rows, _head(h)] for h in range(NH)]
            kk = [_dot_nt(ks[h], ks[h]) for h in range(NH)]
            q0 = [_dot_nt(qs[h], ks[h]) for h in range(NH)]
            for d, m in enumerate(masks):
                gcs, gcs_t = _chunk_cumsums(m, bgv)
                for h in range(NH):
                    c = _GdnHead(qs[h], ks[h], v_ref[rows, _head(h)], kk[h], q0[h], bgv, gcs, gcs_t, d, h, m)
                    inst.append((jj, rows, d, h, m, c))
        ni = len(inst)
        cs = [it[-1] for it in inst]
        tms = [t_ref[jj, d, h] for jj, _, d, h, _, _ in inst]
        d_vns = [sg[d][0][rows, _head(h)] for _, rows, d, h, _, _ in inst]
        d_ws = [sg[d][3][rows, _head(h)] for _, rows, d, h, _, _ in inst]
        d_ts = [_dot_nt(d_vns[i], cs[i].vb) + _dot_nt(d_ws[i], cs[i].kg) for i in range(ni)]
        tts = [tm.T for tm in tms]
        xs = [_dot3(tts[i], d_ts[i]) for i in range(ni)]
        d_ls = [jnp.where(inst[i][4].strict, -_dot3(xs[i], tts[i]), 0.0) for i in range(ni)]
        d_attns = [jnp.where(m.incl, _dot_nt(do_ref[rows, _head(h)], vn_refs[d][rows, _head(h)]), 0.0)
                   for _, rows, d, h, m, _ in inst]
        d_vbs = [_dot(tts[i], d_vns[i]) for i in range(ni)]
        d_kgs = [_dot(tts[i], d_ws[i]) for i in range(ni)]
        d_a0s = [d_ls[i] * cs[i].decay for i in range(ni)]
        d_q0s = [d_attns[i] * cs[i].decay for i in range(ni)]
        es = [(d_ls[i] * cs[i].a0 + d_attns[i] * cs[i].q0) * cs[i].decay for i in range(ni)]
        kb_mm = [_dot(d_a0s[i], cs[i].k) for i in range(ni)]
        q_mm = [_dot(d_q0s[i], cs[i].k) for i in range(ni)]
        k_mm = [_dot_tn(d_a0s[i], cs[i].kb) + _dot_tn(d_q0s[i], cs[i].q) for i in range(ni)]
        e_cols = [_dot_exact(ones, es[i], _TN, False)[:, 0:1] for i in range(ni)]
        acc = {}
        d_gcs, d_betas = [], []
        for i, (jj, rows, d, h, m, c) in enumerate(inst):
            sl = _head(h)
            d_kd, d_qd = sg[d][1][rows, sl], sg[d][2][rows, sl]
            d_cd = sg[d][4][jj, h:h + 1, 0:1]
            d_vb, d_kg = d_vbs[i], d_kgs[i]
            d_kb = kb_mm[i] + d_kg * c.eg
            parts = (q_mm[i] + d_qd * c.eg, k_mm[i] + d_kd * c.ek + d_kb * c.beta, d_vb * c.beta)
            acc[jj, h] = [p + a for a, p in zip(acc[jj, h], parts)] if (jj, h) in acc else list(parts)
            s_kd = jnp.sum(d_kd * c.kd, axis=1, keepdims=True)
            d_gc = (jnp.sum(d_kg * c.kg, axis=1, keepdims=True) + jnp.sum(d_qd * c.qd, axis=1, keepdims=True) - s_kd
                    + jnp.sum(es[i], axis=1, keepdims=True) - e_cols[i])
            d_gl = jnp.sum(s_kd, axis=0, keepdims=True) + d_cd * c.cd
            d_gcs.append(d_gc + jnp.where(rowi == m.last, d_gl, 0.0))
            d_betas.append(jnp.sum(d_kb * c.k, axis=1, keepdims=True) + jnp.sum(d_vb * c.v, axis=1, keepdims=True))
        d_gs = [_dot_exact(inst[i][4].tri, d_gcs[i] * ones, _TN, True)[:, 0:1] for i in range(ni)]
        dbg = [jnp.zeros((CHUNK, BAP), F32) for _ in range(cb)]
        for i, (jj, _, d, h, _, _) in enumerate(inst):
            dbg[jj] = dbg[jj] + jnp.where(lane == 4 * d + h, d_betas[i], 0.0) + jnp.where(lane == 8 + 4 * d + h, d_gs[i], 0.0)
        for jj in range(cb):
            rows = slice(CHUNK * jj, CHUNK * (jj + 1))
            for h in range(NH):
                dq_ref[rows, _head(h)], dk_ref[rows, _head(h)], dv_ref[rows, _head(h)] = acc[jj, h]
            dbg_ref[rows, :] = dbg[jj]

    tok = _rows(cb * CHUNK, HW)
    bgs = _rows(cb * CHUNK, BAP)
    mat = pl.BlockSpec((cb, 2, NH, CHUNK, CHUNK), lambda i: (i, 0, 0, 0, 0))
    dcd = pl.BlockSpec((cb, NH, DH), lambda i: (i, 0, 0))
    args = [q, k, v, bg, tmat, do, vns[0], vns[1]]
    in_specs = [tok, tok, tok, bgs, mat, tok, tok, tok]
    for d in range(2):
        args += list(seq_grads[d])
        in_specs += [tok] * 4 + [dcd]
    return _pallas(body, comm, name=name, grid=(n // cb,), in_specs=in_specs, out_specs=[tok, tok, tok, bgs],
                   out_shape=[_sds((s, HW))] * 3 + [_sds((s, BAP))], scratch_shapes=[], args=args)


def _prep_bwd(c_qkv, p_ba, alog_row, dtb_row, dq, dk, dv, dbg, name):
    s = c_qkv.shape[0]
    t = min(256, s)

    def body(cq_ref, pc_ref, alog_ref, dtb_ref, dq_ref, dk_ref, dv_ref, dbg_ref,
             dcq_ref, dpc_ref, dalog_ref, ddtb_ref):
        @pl.when(pl.program_id(0) == 0)
        def _():
            dalog_ref[...] = jnp.zeros_like(dalog_ref)
            ddtb_ref[...] = jnp.zeros_like(ddtb_ref)

        cq = cq_ref[...]
        sq = cq * _sig(cq)
        sg = _silu_grad(cq)
        for h in range(NH):
            sl = slice(DH * h, DH * (h + 1))
            for off, d_ref, scale in ((0, dq_ref, DH ** -0.5), (RGW, dk_ref, 1.0)):
                csl = slice(off + DH * h, off + DH * (h + 1))
                xh = sq[:, csl]
                nrm = lax.rsqrt(jnp.sum(xh * xh, axis=-1, keepdims=True) + EPS)
                y = xh * nrm
                dy = d_ref[:, sl] * scale
                dcq_ref[:, csl] = nrm * (dy - y * jnp.sum(dy * y, axis=-1, keepdims=True)) * sg[:, csl]
        dcq_ref[:, 2 * RGW:] = dv_ref[...] * sg[:, 2 * RGW:]
        pc = pc_ref[...]
        lane = lax.broadcasted_iota(jnp.int32, pc.shape, 1)
        dbg = dbg_ref[...]
        beta = _sig(pc)
        ea = jnp.exp(alog_ref[...])
        z = pc + dtb_ref[...]
        g = -ea * _softplus(z)
        is_g = jnp.logical_and(lane >= 8, lane < 16)
        d_alpha = jnp.where(is_g, dbg * (-ea) * _sig(z), 0.0)
        dpc_ref[...] = jnp.where(lane < 8, dbg * beta * (1.0 - beta), d_alpha)
        dalog_ref[...] += _colsum(jnp.where(is_g, dbg * g, 0.0))
        ddtb_ref[...] += _colsum(d_alpha)

    return pl.pallas_call(
        body, name=name, grid=(s // t,),
        in_specs=[_rows(t, QKVW), _rows(t, BAP), _full((1, BAP)), _full((1, BAP))] + [_rows(t, HW)] * 3 + [_rows(t, BAP)],
        out_specs=[_rows(t, QKVW), _rows(t, BAP), _full((1, BAP)), _full((1, BAP))],
        out_shape=[_sds((s, QKVW)), _sds((s, BAP)), _sds((1, BAP)), _sds((1, BAP))],
        compiler_params=_cparams(1),
    )(c_qkv, p_ba, alog_row, dtb_row, dq, dk, dv, dbg)


def _mix_out_values(hf, hb, gate, of, ob, z, gn):
    hr = hf + hb
    y_rg = hr * _gelu(gate)
    osum = of + ob
    parts = []
    for h in range(NH):
        sl = slice(DH * h, DH * (h + 1))
        oh = osum[:, sl]
        r, ohat = _rms(oh)
        zh = z[:, sl]
        parts.append((r, ohat, zh))
    y_gdn = jnp.concatenate([ohat * gn * (zh * _sig(zh)) for (r, ohat, zh) in parts], axis=1)
    return hr, y_rg, y_gdn, parts


def _outproj(x1, hf, hb, gate, of, ob, z, gn, wout, name):
    s = x1.shape[0]
    t = min(256, s)

    def body(x_ref, hf_ref, hb_ref, gate_ref, of_ref, ob_ref, z_ref, gn_ref, w_ref, xo_ref, y_ref):
        _, y_rg, y_gdn, _ = _mix_out_values(hf_ref[...], hb_ref[...], gate_ref[...], of_ref[...], ob_ref[...],
                                            z_ref[...], gn_ref[...])
        y = jnp.concatenate([y_rg, y_gdn], axis=1).astype(BF16)
        y_ref[...] = y
        xo_ref[...] = x_ref[...] + jnp.dot(y, w_ref[...], preferred_element_type=F32)

    return pl.pallas_call(
        body, name=name, grid=(s // t,),
        in_specs=[_rows(t, D)] + [_rows(t, RGW)] * 6 + [_full((1, DH)), _full((D, D))],
        out_specs=[_rows(t, D), _rows(t, D)], out_shape=[_sds((s, D)), _sds((s, D), BF16)],
        compiler_params=_cparams(1),
    )(x1, hf, hb, gate, of, ob, z, gn, wout)


def _outproj_bwd(dx2, hf, hb, gate, of, ob, z, gn, wout, name, comm=None):
    s = dx2.shape[0]
    t = min(256, s)

    def body(d_ref, hf_ref, hb_ref, gate_ref, of_ref, ob_ref, z_ref, gn_ref, w_ref,
             dhr_ref, dgate_ref, dos_ref, dz_ref, dgn_ref, db_ref):
        @pl.when(pl.program_id(0) == 0)
        def _():
            dgn_ref[...] = jnp.zeros_like(dgn_ref)

        gate = gate_ref[...]
        gn_v = gn_ref[...]
        hr, _, _, parts = _mix_out_values(hf_ref[...], hb_ref[...], gate, of_ref[...], ob_ref[...], z_ref[...], gn_v)
        dbf = d_ref[...].astype(BF16)
        db_ref[...] = dbf
        dy = _dot_nt(dbf, w_ref[...])
        dyr = dy[:, :RGW]
        dhr_ref[...] = dyr * _gelu(gate)
        dgate_ref[...] = dyr * hr * _gelu_grad(gate)
        dgn = jnp.zeros((1, DH), F32)
        for h, (r, ohat, zh) in enumerate(parts):
            sl = slice(DH * h, DH * (h + 1))
            dyh = dy[:, RGW + DH * h:RGW + DH * (h + 1)]
            sz = zh * _sig(zh)
            dn = dyh * sz
            dz_ref[:, sl] = dyh * ohat * gn_v * _silu_grad(zh)
            dgn = dgn + _colsum(dn * ohat)
            dos_ref[:, sl] = _rms_bwd(dn, ohat, r, gn_v)
        dgn_ref[...] += dgn

    return _pallas(
        body, comm, name=name, grid=(s // t,),
        in_specs=[_rows(t, D)] + [_rows(t, RGW)] * 6 + [_full((1, DH)), _full((D, D))],
        out_specs=[_rows(t, RGW)] * 4 + [_full((1, DH)), _rows(t, D)],
        out_shape=[_sds((s, RGW))] * 4 + [_sds((1, DH)), _sds((s, D), BF16)],
        scratch_shapes=[], args=(dx2, hf, hb, gate, of, ob, z, gn, wout))


def _loss_head(x3, target, gain, name):
    s = x3.shape[0]
    t = min(256, s)

    def body(x_ref, t_ref, g_ref, dx_ref, dxh_ref, loss_ref, dg_ref):
        @pl.when(pl.program_id(0) == 0)
        def _():
            loss_ref[...] = jnp.zeros_like(loss_ref)
            dg_ref[...] = jnp.zeros_like(dg_ref)

        r, xh = _rms(x_ref[...])
        gv = g_ref[...]
        err = xh * gv - t_ref[...]
        per_tok = jnp.mean(err * err, axis=-1, keepdims=True)
        loss_ref[...] += 0.5 * jnp.sum(per_tok, axis=0, keepdims=True)
        dy = err * (1.0 / D)
        dg_ref[...] += _colsum(dy * xh)
        dx = _rms_bwd(dy, xh, r, gv)
        dx_ref[...] = dx
        dxh_ref[...] = (0.5 * dx).astype(BF16)

    return pl.pallas_call(
        body, name=name, grid=(s // t,), in_specs=[_rows(t, D), _rows(t, D), _full((1, D))],
        out_specs=[_rows(t, D), _rows(t, D), _full((8, 128)), _full((1, D))],
        out_shape=[_sds((s, D)), _sds((s, D), BF16), _sds((8, 128)), _sds((1, D))], compiler_params=_cparams(1),
    )(x3, target, gain)


def _adamw_math(wv, gv, mv, vv):
    mn = ADAM_B1 * mv + (1.0 - ADAM_B1) * gv
    vn = ADAM_B2 * vv + (1.0 - ADAM_B2) * (gv * gv)
    m_hat = mn / (1.0 - ADAM_B1 ** ADAM_STEP)
    v_hat = vn / (1.0 - ADAM_B2 ** ADAM_STEP)
    return -ADAM_LR * (m_hat / (jnp.sqrt(v_hat) + ADAM_EPS) + ADAM_WD * wv), mn, vn


def _row_tile(r, c):
    tr = r
    while tr * c * 4 > (1 << 20) and tr % 16 == 0:
        tr //= 2
    return tr


def _adamw(w, g, m, v, name):
    r, c = w.shape
    tr = _row_tile(r, c)

    def body(w_ref, g_ref, m_ref, v_ref, d_ref, nm_ref, nv_ref):
        d_ref[...], nm_ref[...], nv_ref[...] = _adamw_math(w_ref[...], g_ref[...], m_ref[...], v_ref[...])

    return pl.pallas_call(
        body, name=name, grid=(r // tr,), in_specs=[_rows(tr, c)] * 4, out_specs=[_rows(tr, c)] * 3,
        out_shape=[_sds((r, c))] * 3, compiler_params=_cparams(1),
    )(w, g, m, v)


def _adamw_halves(w, own, recv, m, v, c_arr, name):
    r, c = w.shape
    h = r // 2
    tr = _row_tile(h, c)
    nh = h // tr

    def body(c_ref, w_ref, own_ref, recv_ref, m_ref, v_ref, g_ref, d_ref, nm_ref, nv_ref):
        first_half = pl.program_id(0) < nh
        use_own = first_half == (c_ref[0] == 0)
        gv = jnp.where(use_own, own_ref[...], recv_ref[...])
        g_ref[...] = gv
        d_ref[...], nm_ref[...], nv_ref[...] = _adamw_math(w_ref[...], gv, m_ref[...], v_ref[...])

    full = pl.BlockSpec((tr, c), lambda i, c_ref: (i, 0))
    half = pl.BlockSpec((tr, c), lambda i, c_ref: (i % nh, 0))
    return pl.pallas_call(
        body, name=name, out_shape=[_sds((r, c))] * 4,
        grid_spec=pltpu.PrefetchScalarGridSpec(
            num_scalar_prefetch=1, grid=(2 * nh,), in_specs=[full, half, half, full, full], out_specs=[full] * 4),
        compiler_params=_cparams(1),
    )(c_arr, w, own, recv, m, v)


def _mesh_pos():
    return lax.axis_index("x"), lax.axis_index("y"), lax.axis_index("c")


def _other_chips(x, y):
    return [(1 - x, y), (x, 1 - y), (1 - x, 1 - y)]


class _Comm:
    def __init__(self, inputs, out_shapes, scratch, start, finish, space=pltpu.HBM):
        self.inputs, self.out_shapes, self.scratch = list(inputs), list(out_shapes), list(scratch)
        self.start, self.finish, self.space = start, finish, space


def _comm_call(comm, name):
    ni, no = len(comm.inputs), len(comm.out_shapes)

    def body(*refs):
        comm.start(refs[:ni], refs[ni:ni + no], refs[ni + no:])
        comm.finish(refs[:ni], refs[ni:ni + no], refs[ni + no:])

    spec = pl.BlockSpec(memory_space=comm.space)
    return list(pl.pallas_call(body, name=name, out_shape=comm.out_shapes, in_specs=[spec] * ni, out_specs=[spec] * no,
                               scratch_shapes=comm.scratch)(*comm.inputs))


def _pallas(body, comm, *, name, grid, in_specs, out_specs, out_shape, scratch_shapes, args):
    params = _cparams(len(grid))
    if comm is None:
        outs = pl.pallas_call(body, name=name, grid=grid, in_specs=in_specs, out_specs=out_specs, out_shape=out_shape,
                              scratch_shapes=scratch_shapes, compiler_params=params)(*args)
        return list(outs), []
    n_in, n_out, n_sc = len(in_specs), len(out_specs), len(scratch_shapes)
    ci, co = len(comm.inputs), len(comm.out_shapes)

    def carried(*refs):
        bounds = [0, n_in, n_in + ci, n_in + ci + n_out, n_in + ci + n_out + co, n_in + ci + n_out + co + n_sc, len(refs)]
        ins, cins, outs, couts, scr, csems = [refs[lo:hi] for lo, hi in zip(bounds[:-1], bounds[1:])]
        ids = [pl.program_id(k) for k in range(len(grid))]
        first = functools.reduce(jnp.logical_and, [i == 0 for i in ids])
        last = functools.reduce(jnp.logical_and, [i == g - 1 for i, g in zip(ids, grid)])

        @pl.when(first)
        def _():
            comm.start(cins, couts, csems)

        body(*ins, *outs, *scr)

        @pl.when(last)
        def _():
            comm.finish(cins, couts, csems)

    hbm = pl.BlockSpec(memory_space=pltpu.HBM)
    outs = pl.pallas_call(
        carried, name=name, grid=grid, in_specs=list(in_specs) + [hbm] * ci, out_specs=list(out_specs) + [hbm] * co,
        out_shape=list(out_shape) + comm.out_shapes, scratch_shapes=list(scratch_shapes) + comm.scratch,
        compiler_params=params)(*args, *comm.inputs)
    return list(outs[:n_out]), list(outs[n_out:])


def _gather_comm(arrays, space, block_rows):
    n_arr = len(arrays)

    def plan(x_refs, out_refs, sems):
        send_sems, recv_sems, local_sems = sems
        x, y, c = _mesh_pos()
        me, sibling = (x, y, c), (x, y, 1 - c)
        chips = _other_chips(x, y)

        def slot(a, px, py, pc):
            return out_refs[a].at[4 * px + 2 * py + pc]

        def copy(a, k, block, to, src=None):
            return pltpu.make_async_remote_copy(
                src_ref=slot(a, *block) if src is None else src, dst_ref=slot(a, *block),
                send_sem=send_sems.at[7 * a + k], recv_sem=recv_sems.at[7 * a + k], device_id=to, device_id_type=MESH)

        srcs = [x_refs[a] if block_rows[a] is None else
                x_refs[a].at[pl.ds(pl.multiple_of(c * block_rows[a], 16), block_rows[a]), :] for a in range(n_arr)]
        local = [pltpu.make_async_copy(srcs[a], slot(a, *me), local_sems.at[a]) for a in range(n_arr)]
        first = []
        for a in range(n_arr):
            first += [copy(a, 1 + j, me, (*chip, c), src=srcs[a]) for j, chip in enumerate(chips)]
            first.append(copy(a, 0, me, sibling, src=srcs[a]))
        return me, sibling, chips, c, copy, local, first

    def start(x_refs, out_refs, sems):
        _, _, _, _, _, local, first = plan(x_refs, out_refs, sems)
        for cp in local + first:
            cp.start()

    def finish(x_refs, out_refs, sems):
        me, sibling, chips, c, copy, local, first = plan(x_refs, out_refs, sems)
        passed = []
        for j, chip in enumerate(chips):
            for a in range(n_arr):
                copy(a, 1 + j, (*chip, c), me).wait_recv()
                fwd = copy(a, 4 + j, (*chip, c), sibling)
                fwd.start()
                passed.append(fwd)
        for a in range(n_arr):
            copy(a, 0, sibling, me).wait_recv()
            for j, chip in enumerate(chips):
                copy(a, 4 + j, (*chip, 1 - c), me).wait_recv()
        for cp in first + passed:
            cp.wait_send()
        for cp in local:
            cp.wait()

    out_shapes = [_sds((8, w.shape[0] if r is None else r) + w.shape[1:], w.dtype) for w, r in zip(arrays, block_rows)]
    scratch = [pltpu.SemaphoreType.DMA((7 * n_arr,)), pltpu.SemaphoreType.DMA((7 * n_arr,)), pltpu.SemaphoreType.DMA((n_arr,))]
    return _Comm(arrays, out_shapes, scratch, start, finish, space)


def _weights_gather_comm(shards):
    return _gather_comm(shards, pltpu.HBM, [w.shape[0] // 2 for w in shards])


def _all_shards(gathered):
    return [o.reshape(NSH, 2 * o.shape[1], o.shape[2]) for o in gathered]


def _gather_small(block, name):
    return _comm_call(_gather_comm([block], pltpu.VMEM, [None]), name)[0]


def _exchange_comm(gs):
    n = len(gs)
    halves = [g.shape[1] // 2 for g in gs]

    def plan(g_refs, land_refs, sems):
        send_sems, recv_sems = sems
        x, y, c = _mesh_pos()
        copies = []
        for a in range(n):
            h = halves[a]
            for s in range(NSH):
                copies.append(pltpu.make_async_remote_copy(
                    src_ref=g_refs[a].at[s, pl.ds(pl.multiple_of((1 - c) * h, 8), h), :], dst_ref=land_refs[a].at[s],
                    send_sem=send_sems.at[NSH * a + s], recv_sem=recv_sems.at[NSH * a + s],
                    device_id=(x, y, 1 - c), device_id_type=MESH))
        return copies

    def start(g_refs, land_refs, sems):
        for cp in plan(g_refs, land_refs, sems):
            cp.start()

    def finish(g_refs, land_refs, sems):
        for cp in plan(g_refs, land_refs, sems):
            cp.wait()

    scratch = [pltpu.SemaphoreType.DMA((NSH * n,)), pltpu.SemaphoreType.DMA((NSH * n,))]
    return _Comm(gs, [_sds((NSH, h, g.shape[2])) for h, g in zip(halves, gs)], scratch, start, finish)


def _chip_sum(g, land, c_arr, name):
    _, h, cols = land.shape

    def body(c_ref, g_ref, l_ref, o_ref):
        o_ref[...] = (g_ref[...] + l_ref[...]).astype(BF16)

    return pl.pallas_call(
        body, name=name, out_shape=_sds((NSH, h, cols), BF16),
        grid_spec=pltpu.PrefetchScalarGridSpec(
            num_scalar_prefetch=1, grid=(NSH,),
            in_specs=[pl.BlockSpec((1, h, cols), lambda s, c_ref: (s, c_ref[0], 0)),
                      pl.BlockSpec((1, h, cols), lambda s, c_ref: (s, 0, 0))],
            out_specs=pl.BlockSpec((1, h, cols), lambda s, c_ref: (s, 0, 0))),
        compiler_params=_cparams(1),
    )(c_arr, g, land)


def _scatter_comm(parts):
    n = len(parts)

    def plan(p_refs, land_refs, sems):
        send_sems, recv_sems, local_sems = sems
        x, y, c = _mesh_pos()
        my_chip = 2 * x + y
        local = [pltpu.make_async_copy(p_refs[a].at[my_chip], land_refs[a].at[my_chip], local_sems.at[a]) for a in range(n)]
        copies = []
        for a in range(n):
            for j, (px, py) in enumerate(_other_chips(x, y)):
                copies.append(pltpu.make_async_remote_copy(
                    src_ref=p_refs[a].at[2 * px + py], dst_ref=land_refs[a].at[my_chip],
                    send_sem=send_sems.at[3 * a + j], recv_sem=recv_sems.at[3 * a + j],
                    device_id=(px, py, c), device_id_type=MESH))
        return local, copies

    def start(p_refs, land_refs, sems):
        local, copies = plan(p_refs, land_refs, sems)
        for cp in local + copies:
            cp.start()

    def finish(p_refs, land_refs, sems):
        local, copies = plan(p_refs, land_refs, sems)
        for cp in copies:
            cp.wait()
        for cp in local:
            cp.wait()

    scratch = [pltpu.SemaphoreType.DMA((3 * n,)), pltpu.SemaphoreType.DMA((3 * n,)), pltpu.SemaphoreType.DMA((n,))]
    return _Comm(parts, [_sds(p.shape, BF16) for p in parts], scratch, start, finish)


def _sum_slots(land, name):
    k, r, c = land.shape
    tr = r // 2 if r % 32 == 0 else r

    def body(l_ref, o_ref):
        acc = l_ref[0].astype(F32)
        for i in range(1, k):
            acc = acc + l_ref[i].astype(F32)
        o_ref[...] = acc

    return pl.pallas_call(
        body, name=name, grid=(r // tr,), in_specs=[pl.BlockSpec((k, tr, c), lambda i: (0, i, 0))],
        out_specs=_rows(tr, c), out_shape=_sds((r, c)), compiler_params=_cparams(1),
    )(land)


def _sibling_swap(halves):
    n = len(halves)

    def body(*refs):
        h_refs, out_refs = refs[:n], refs[n:2 * n]
        send_sems, recv_sems = refs[2 * n:]
        x, y, c = _mesh_pos()
        copies = [pltpu.make_async_remote_copy(
            src_ref=h_refs[a], dst_ref=out_refs[a], send_sem=send_sems.at[a], recv_sem=recv_sems.at[a],
            device_id=(x, y, 1 - c), device_id_type=MESH) for a in range(n)]
        for cp in copies:
            cp.start()
        for cp in copies:
            cp.wait()

    return pl.pallas_call(
        body, name="grad_sibling_swap", out_shape=[_sds(h.shape) for h in halves],
        in_specs=[pl.BlockSpec(memory_space=pltpu.HBM)] * n, out_specs=[pl.BlockSpec(memory_space=pltpu.HBM)] * n,
        scratch_shapes=[pltpu.SemaphoreType.DMA((n,)), pltpu.SemaphoreType.DMA((n,))],
    )(*halves)


def _pad_rows(v, width):
    flat = v.reshape(-1)
    rows = -(-flat.shape[0] // width)
    rows = -(-rows // 8) * 8
    return jnp.pad(flat, (0, rows * width - flat.shape[0])).reshape(rows, width)


def _size(shape):
    n = 1
    for dim in shape:
        n *= dim
    return n


def _row_pack(arrs):
    pieces = []
    for a in arrs:
        rows = -(-a.size // D)
        pieces.append(jnp.pad(a.reshape(-1), (0, rows * D - a.size)).reshape(rows, D))
    total = sum(p.shape[0] for p in pieces)
    if total % 8:
        pieces.append(jnp.zeros((8 - total % 8, D), F32))
    return jnp.concatenate(pieces, axis=0)


def _row_unpack(packed, shapes):
    out, r0 = [], 0
    for shp in shapes:
        n = _size(shp)
        rows = -(-n // D)
        out.append(packed[r0:r0 + rows].reshape(-1)[:n].reshape(shp))
        r0 += rows
    return out


def _block_diag(w):
    eye = jnp.eye(8, dtype=w.dtype)
    return (w[:, :, None, :] * eye[:, None, :, None]).reshape(RGW, RGW)


def _diag_blocks(dense):
    r = dense.reshape(8, 64, 8, 64)
    return jnp.stack([r[n, :, n, :] for n in range(8)])


def _lane_row(v8):
    return jnp.zeros((1, BAP), F32).at[0, 8:16].set(v8.reshape(8))


def _chip_sums(gs, lands, names, c_arr):
    return [_chip_sum(g, l, c_arr, "chip_sum_" + n) for g, l, n in zip(gs, lands, names)]


def _reduce_parts(gs, names, c_arr, tag):
    return _chip_sums(gs, _comm_call(_exchange_comm(gs), "grad_sibling_exchange_" + tag), names, c_arr)


def _local_step(x, target, sw, ffn1_w, later_shards, c_arr):
    (g1, gmix, rg_cw8, rg_cb, wgates, gbias, lam_row, gdn_cw8, alog_row, dtb_row, gn, g2, gfin) = sw
    wg1, wu1, wd1 = ffn1_w

    (x1, a1, b1, fb1), gathered = _ffn_fwd(x, g1, wg1, wu1, wd1, "ffn1_fwd", comm=_weights_gather_comm(later_shards))
    win_sh, wout_sh, wg2, wu2, wd2 = _all_shards(gathered)
    w_in_full = jnp.transpose(win_sh, (1, 0, 2)).reshape(D, NSH * INSH)
    wout = wout_sh.reshape(D, D)
    w_in_groups = (w_in_full[:, 0:512], w_in_full[:, 512:1024], w_in_full[:, 1024:2560], w_in_full[:, 2560:3072],
                   jnp.pad(w_in_full[:, 3072:3088], ((0, 0), (0, BAP - BAW))))
    h2, p_rgx, p_gate, p_qkv, p_z, p_ba = _inproj(x1, gmix, w_in_groups, "in_proj")
    c_rg = _conv(p_rgx, rg_cw8, rg_cb, "rg_conv")
    c_qkv = _conv(p_qkv, gdn_cw8, jnp.zeros((1, QKVW), F32), "gdn_conv")
    a0, bb0, a1s, bb1, q, k, v, bg = _mix_prep(c_rg, c_qkv, p_ba, wgates, gbias, lam_row, alog_row, dtb_row, "mix_prep")
    tmat, gu, gw, gqd, gkd, gat, gcd = _gdn_local_fwd(q, k, v, bg, "gdn_local_fwd")
    of, s0, vn0, ob, s1, vn1, hf, hb = _gdn_seq_fwd(gu, gw, gqd, gkd, gat, gcd, "gdn_seq_fwd", scan=(a0, bb0, a1s, bb1))
    x2, ymix = _outproj(x1, hf, hb, p_gate, of, ob, p_z, gn, wout, "out_proj")
    (x3, a2, b2, fb2), _ = _ffn_fwd(x2, g2, wg2, wu2, wd2, "ffn2_fwd")
    dx3, dob2, loss_blk, d_gfin = _loss_head(x3, target, gfin, "loss_head")

    dx2, d_g2, hb2, dab2, dbb2, _ = _ffn_bwd(x2, dx3, dob2, g2, a2, b2, wg2, wu2, wd2, "ffn2_bwd")
    d_ffn2 = [_tn(dab2, hb2, "ffn2_dwg"), _tn(dbb2, hb2, "ffn2_dwu"), _tn(fb2, dob2, "ffn2_dwd")]

    (d_hr, d_gate, d_os, d_z, d_gn, dx2b), lands = _outproj_bwd(dx2, hf, hb, p_gate, of, ob, p_z, gn, wout, "out_proj_bwd",
                                                               comm=_exchange_comm(d_ffn2))
    parts_ffn2 = _chip_sums(d_ffn2, lands, _BIG_NAMES[5:8], c_arr)
    d_wout = _tn(ymix, dx2b, "dw_out")[0]

    sg = _gdn_seq_bwd(d_os, gw, gqd, gkd, gat, gcd, (s0, s1), (vn0, vn1), "gdn_seq_bwd", scan=(a1s, d_hr, a0, d_hr))
    lam1, lam0 = sg[10:12]
    d_xc, d_pre, xcb, d_gbias, d_lam = _gates_bwd(c_rg, wgates, gbias, lam_row, lam0, lam1, hf, hb, "rg_gates_bwd")
    d_wgates = _tn(xcb, d_pre, "dw_gates")[0]
    d_prgx, d_rgcw8, d_rgcb = _conv_bwd(p_rgx, d_xc, rg_cw8, "rg_conv_bwd")

    (dq, dk, dv, dbg), lands_ffn2 = _gdn_local_bwd(q, k, v, bg, tmat, d_os, (vn0, vn1), (sg[0:5], sg[5:10]), "gdn_local_bwd",
                                                  comm=_scatter_comm(parts_ffn2))
    d_cqkv, d_pba, d_alog, d_dtb = _prep_bwd(c_qkv, p_ba, alog_row, dtb_row, dq, dk, dv, dbg, "gdn_prep_bwd")
    d_pqkv, d_gdncw8, _ = _conv_bwd(p_qkv, d_cqkv, gdn_cw8, "gdn_conv_bwd")

    dps = (d_prgx, d_gate, d_pqkv, d_z, d_pba)
    dx1, dob1, d_gmix = _inproj_bwd(x1, dx2, gmix, dps, w_in_groups, "in_proj_bwd")
    d_win_groups = [_tn(h2, dp, "dw_in_%d" % i)[0] for i, dp in enumerate(dps)]
    d_win = jnp.concatenate(d_win_groups[:4] + [d_win_groups[4][:, :BAW]], axis=1)
    d_mix = [jnp.transpose(d_win.reshape(D, NSH, INSH), (1, 0, 2)), d_wout.reshape(NSH, OUTSH, D)]

    gx, d_g1, hb1, dab1, dbb1, lands = _ffn_bwd(x, dx1, dob1, g1, a1, b1, wg1, wu1, wd1, "ffn1_bwd", comm=_exchange_comm(d_mix))
    parts_mix = _chip_sums(d_mix, lands, _BIG_NAMES[3:5], c_arr)
    d_wg1, lands_mix = _tn(dab1, hb1, "ffn1_dwg", comm=_scatter_comm(parts_mix))
    parts_wg1 = _reduce_parts([d_wg1], _BIG_NAMES[0:1], c_arr, "ffn1_gate")
    d_wu1, lands_wg1 = _tn(dbb1, hb1, "ffn1_dwu", comm=_scatter_comm(parts_wg1))
    parts_wu1 = _reduce_parts([d_wu1], _BIG_NAMES[1:2], c_arr, "ffn1_up")
    d_wd1, lands_wu1 = _tn(fb1, dob1, "ffn1_dwd", comm=_scatter_comm(parts_wu1))
    parts_wd1 = _reduce_parts([d_wd1], _BIG_NAMES[2:3], c_arr, "ffn1_down")
    lands_ffn1 = lands_wg1 + lands_wu1 + _comm_call(_scatter_comm(parts_wd1), "grad_chip_scatter_ffn1_down")

    halves = [_sum_slots(l, "sum_chips_" + n) for l, n in zip(lands_ffn1 + lands_mix + lands_ffn2, _BIG_NAMES)]
    small = dict(
        ffn1_norm=d_g1, mix_norm=d_gmix, rg_conv_w=d_rgcw8[:4], rg_conv_b=d_rgcb,
        rg_gate_a_w=jnp.stack([_diag_blocks(d_wgates[:, RGW * i:RGW * (i + 1)]) for i in (0, 1)]),
        rg_gate_x_w=jnp.stack([_diag_blocks(d_wgates[:, RGW * i:RGW * (i + 1)]) for i in (2, 3)]),
        rg_gate_a_b=d_gbias[0, :2 * RGW].reshape(2, RGW), rg_gate_x_b=d_gbias[0, 2 * RGW:].reshape(2, RGW),
        rg_lambda=d_lam.reshape(2, RGW), gdn_conv_w=d_gdncw8[:4],
        gdn_a_log=d_alog[0, 8:16].reshape(2, NH), gdn_dt_bias=d_dtb[0, 8:16].reshape(2, NH),
        gdn_norm=d_gn, ffn2_norm=d_g2, final_norm=d_gfin)
    return loss_blk, gx, halves, small


_SMALL_NAMES = ("ffn1_norm", "mix_norm", "rg_conv_w", "rg_conv_b", "rg_gate_a_w", "rg_gate_a_b", "rg_gate_x_w",
                "rg_gate_x_b", "rg_lambda", "gdn_conv_w", "gdn_a_log", "gdn_dt_bias", "gdn_norm", "ffn2_norm", "final_norm")
_SMALL_SHARDED = dict(rg_conv_w=128, rg_gate_a_b=128, rg_gate_x_b=128, rg_lambda=128, gdn_conv_w=384)
_OUT_ORDER = ("ffn1_norm", "ffn1_w_gate", "ffn1_w_up", "ffn1_w_down", "mix_norm", "w_in", "w_out", "rg_conv_w", "rg_conv_b",
              "rg_gate_a_w", "rg_gate_a_b", "rg_gate_x_w", "rg_gate_x_b", "rg_lambda", "gdn_conv_w", "gdn_a_log",
              "gdn_dt_bias", "gdn_norm", "ffn2_norm", "ffn2_w_gate", "ffn2_w_up", "ffn2_w_down", "final_norm")
_BIG_NAMES = ("ffn1_w_gate", "ffn1_w_up", "ffn1_w_down", "w_in", "w_out", "ffn2_w_gate", "ffn2_w_up", "ffn2_w_down")
_TRANSPOSED = ("ffn1_w_gate", "ffn1_w_up", "ffn2_w_gate", "ffn2_w_up")


def kernel(x, ffn1_norm, ffn1_w_gate, ffn1_w_up, ffn1_w_down, mix_norm, w_in, w_out, rg_conv_w, rg_conv_b, rg_gate_a_w, rg_gate_a_b, rg_gate_x_w, rg_gate_x_b, rg_lambda, gdn_conv_w, gdn_a_log, gdn_dt_bias, gdn_norm, ffn2_norm, ffn2_w_gate, ffn2_w_up, ffn2_w_down, final_norm, loss_target, m_ffn1_norm, m_ffn1_w_gate, m_ffn1_w_up, m_ffn1_w_down, m_mix_norm, m_w_in, m_w_out, m_rg_conv_w, m_rg_conv_b, m_rg_gate_a_w, m_rg_gate_a_b, m_rg_gate_x_w, m_rg_gate_x_b, m_rg_lambda, m_gdn_conv_w, m_gdn_a_log, m_gdn_dt_bias, m_gdn_norm, m_ffn2_norm, m_ffn2_w_gate, m_ffn2_w_up, m_ffn2_w_down, m_final_norm, v_ffn1_norm, v_ffn1_w_gate, v_ffn1_w_up, v_ffn1_w_down, v_mix_norm, v_w_in, v_w_out, v_rg_conv_w, v_rg_conv_b, v_rg_gate_a_w, v_rg_gate_a_b, v_rg_gate_x_w, v_rg_gate_x_b, v_rg_lambda, v_gdn_conv_w, v_gdn_a_log, v_gdn_dt_bias, v_gdn_norm, v_ffn2_norm, v_ffn2_w_gate, v_ffn2_w_up, v_ffn2_w_down, v_final_norm):
    args = dict(locals())
    w = {n: args[n] for n in _OUT_ORDER}
    mom = {n: args["m_" + n] for n in _OUT_ORDER}
    var = {n: args["v_" + n] for n in _OUT_ORDER}
    xi, yi, ci = _mesh_pos()
    shard = 2 * xi + yi

    big_bf16 = [w[n][0].astype(BF16) for n in _BIG_NAMES]
    sm_local = _pad_rows(jnp.concatenate([w[n][0].reshape(-1) for n in _SMALL_SHARDED]), 128)
    first = _comm_call(_gather_comm(big_bf16[0:3] + [sm_local], pltpu.HBM, [t.shape[0] // 2 for t in big_bf16[0:3]] + [None]),
                       "gather_first_weights")
    ffn1_w = _all_shards(first[0:3])
    sm_all = first[3][0::2].reshape(NSH, -1)
    sm_full, off = {}, 0
    for n, wd_ in _SMALL_SHARDED.items():
        rows = w[n].shape[1]
        piece = sm_all[:, off:off + rows * wd_].reshape(NSH, rows, wd_)
        sm_full[n] = jnp.transpose(piece, (1, 0, 2)).reshape(rows, NSH * wd_)
        off += rows * wd_

    wa, wx = rg_gate_a_w[0], rg_gate_x_w[0]
    wgates = jnp.concatenate([_block_diag(wa[0]), _block_diag(wa[1]), _block_diag(wx[0]), _block_diag(wx[1])],
                             axis=1).astype(BF16)
    gbias = jnp.concatenate([sm_full["rg_gate_a_b"].reshape(1, -1), sm_full["rg_gate_x_b"].reshape(1, -1)], axis=1)
    sw = (ffn1_norm, mix_norm, jnp.pad(sm_full["rg_conv_w"], ((0, 4), (0, 0))), rg_conv_b, wgates, gbias,
          sm_full["rg_lambda"].reshape(1, -1), jnp.pad(sm_full["gdn_conv_w"], ((0, 4), (0, 0))), _lane_row(gdn_a_log),
          _lane_row(gdn_dt_bias), gdn_norm, ffn2_norm, final_norm.reshape(1, D))
    c_arr = ci.reshape(1).astype(jnp.int32)

    loss_blk, gx, halves, small = _local_step(x[0], loss_target[0], sw, ffn1_w, big_bf16[3:], c_arr)
    loss = lax.psum(loss_blk[0, 0], ("x", "y", "c"))
    grads = {}

    sm_grad = _row_pack([small[n] for n in _SMALL_NAMES])
    sm_sum = _sum_slots(_gather_small(sm_grad, "gather_small_grads"), "small_grad_sum")
    for n, g in zip(_SMALL_NAMES, _row_unpack(sm_sum, [small[n].shape for n in _SMALL_NAMES])):
        if n in _SMALL_SHARDED:
            wd_ = _SMALL_SHARDED[n]
            g = lax.dynamic_slice_in_dim(g, shard * wd_, wd_, axis=1)
        grads[n] = g.reshape(w[n].shape)

    delta, new_m, new_v = {}, {}, {}
    for n, own, recv in zip(_BIG_NAMES, halves, _sibling_swap(halves)):
        to2d = jnp.transpose if n in _TRANSPOSED else (lambda t: t)
        outs4 = _adamw_halves(to2d(w[n][0]), own, recv, to2d(mom[n][0]), to2d(var[n][0]), c_arr, "adamw_" + n)
        grads[n], delta[n], new_m[n], new_v[n] = [to2d(o)[None] for o in outs4]
    packs = [_row_pack([t[n] for n in _SMALL_NAMES]) for t in (w, grads, mom, var)]
    sm_shapes = [w[n].shape for n in _SMALL_NAMES]
    for dst, src in zip((delta, new_m, new_v), _adamw(*packs, "adamw_small")):
        for n, val in zip(_SMALL_NAMES, _row_unpack(src, sm_shapes)):
            dst[n] = val

    outs = [loss, gx[None]]
    for group in (grads, delta, new_m, new_v):
        outs += [group[n] for n in _OUT_ORDER]
    return tuple(outs)
```

```python
import functools

import jax
import jax.numpy as jnp
from jax import lax
from jax.experimental import pallas as pl
from jax.experimental.pallas import tpu as pltpu

F32 = jnp.float32
BF16 = jnp.bfloat16
EPS = 1e-6
D = 1024
NSH = 4
FSH = 704
RGW = 512
QKVW = 1536
ZW = 512
BAW = 16
BAP = 128
INSH = 772
OUTSH = 256
CHUNK = 64
NH = 4
DH = 128
RG_C = 8.0
VMEM_LIMIT = 52 * 1024 * 1024
MESH = pl.DeviceIdType.MESH

ADAM_LR = 0.001
ADAM_B1 = 0.9
ADAM_B2 = 0.999
ADAM_EPS = 1e-08
ADAM_WD = 0.01
ADAM_STEP = 10


def _cparams(n_grid):
    return pltpu.CompilerParams(dimension_semantics=("arbitrary",) * n_grid, vmem_limit_bytes=VMEM_LIMIT)


def _sig(x):
    return 0.5 + 0.5 * jnp.tanh(0.5 * x)


def _sig_pos(x):
    return 1.0 / (1.0 + jnp.exp(-x))


def _softplus(x):
    return jnp.maximum(x, 0.0) + jnp.log(1.0 + jnp.exp(-jnp.abs(x)))


def _neg_expm1(y):
    series = -y * (1.0 + y * (0.5 + y * (1.0 / 6 + y * (1.0 / 24 + y * (1.0 / 120 + y * (1.0 / 720 + y / 5040))))))
    return jnp.where(y > -0.3, series, 1.0 - jnp.exp(y))


_GELU_C = 0.7978845608028654


def _gelu(x):
    t = jnp.tanh(_GELU_C * (x + 0.044715 * x * x * x))
    return 0.5 * x * (1.0 + t)


def _gelu_grad(x):
    t = jnp.tanh(_GELU_C * (x + 0.044715 * x * x * x))
    return 0.5 * (1.0 + t) + 0.5 * x * (1.0 - t * t) * _GELU_C * (1.0 + 3 * 0.044715 * x * x)


def _silu_grad(x):
    s = _sig(x)
    return s * (1.0 + x * (1.0 - s))


def _dot(a, b):
    return jnp.dot(a.astype(BF16), b.astype(BF16), preferred_element_type=F32)


def _dot_nt(a, b):
    return lax.dot_general(a.astype(BF16), b.astype(BF16), (((1,), (1,)), ((), ())), preferred_element_type=F32)


def _dot_tn(a, b):
    return lax.dot_general(a.astype(BF16), b.astype(BF16), (((0,), (0,)), ((), ())), preferred_element_type=F32)


_NN = ((1,), (0,))
_NT = ((1,), (1,))
_TN = ((0,), (0,))


def _dg(a, b, dims):
    return lax.dot_general(a, b, (dims, ((), ())), preferred_element_type=F32)


def _split2(a):
    hi = a.astype(BF16)
    return hi, (a - hi.astype(F32)).astype(BF16)


def _dot3(a, b, dims=_NN):
    ah, al = _split2(a)
    bh, bl = _split2(b)
    return _dg(ah, bh, dims) + _dg(ah, bl, dims) + _dg(al, bh, dims)


def _dot_exact(e, x, dims, e_is_lhs):
    x0 = x.astype(BF16)
    r = x - x0.astype(F32)
    x1 = r.astype(BF16)
    x2 = (r - x1.astype(F32)).astype(BF16)
    eb = e.astype(BF16)
    if e_is_lhs:
        return _dg(eb, x0, dims) + _dg(eb, x1, dims) + _dg(eb, x2, dims)
    return _dg(x0, eb, dims) + _dg(x1, eb, dims) + _dg(x2, eb, dims)


def _rms(xv):
    r = lax.rsqrt(jnp.mean(xv * xv, axis=-1, keepdims=True) + EPS)
    return r, xv * r


def _rms_bwd(dy, xh, r, gain):
    dxh = dy * gain
    return r * (dxh - xh * jnp.mean(dxh * xh, axis=-1, keepdims=True))


def _colsum(v):
    return jnp.sum(v, axis=0, keepdims=True)


def _rows(t, c):
    return pl.BlockSpec((t, c), lambda i: (i, 0))


def _full(shape):
    n = len(shape)
    return pl.BlockSpec(shape, lambda i: (0,) * n)


def _sds(shape, dtype=F32):
    return jax.ShapeDtypeStruct(shape, dtype)


def _ffn_fwd(x, gain, wg, wu, wd, name, comm=None):
    s = x.shape[0]
    tm = min(512, s)

    def body(x_ref, g_ref, wg_ref, wu_ref, wd_ref, xo_ref, ga_ref, gb_ref, f_ref, h_sc, acc):
        j = pl.program_id(1)

        @pl.when(j == 0)
        def _():
            _, xh = _rms(x_ref[...])
            h_sc[...] = (xh * g_ref[...]).astype(BF16)
            acc[...] = jnp.zeros_like(acc)

        h = h_sc[...]

        a = jnp.dot(h, wg_ref[0], preferred_element_type=F32)
        b = jnp.dot(h, wu_ref[0], preferred_element_type=F32)
        sa = _sig(a)
        silu = a * sa
        fv = silu * b
        f = fv.astype(BF16)
        f_ref[0] = f
        ga_ref[0] = (sa * b + fv * (1.0 - sa)).astype(BF16)
        gb_ref[0] = silu.astype(BF16)
        acc[...] += jnp.dot(f, wd_ref[0], preferred_element_type=F32)

        @pl.when(j == NSH - 1)
        def _():
            xo_ref[...] = x_ref[...] + 0.5 * acc[...]

    return _pallas(
        body, comm, name=name, grid=(s // tm, NSH),
        in_specs=[pl.BlockSpec((tm, D), lambda i, j: (i, 0)), pl.BlockSpec((1, D), lambda i, j: (0, 0)),
                  pl.BlockSpec((1, D, FSH), lambda i, j: (j, 0, 0)), pl.BlockSpec((1, D, FSH), lambda i, j: (j, 0, 0)),
                  pl.BlockSpec((1, FSH, D), lambda i, j: (j, 0, 0))],
        out_specs=[pl.BlockSpec((tm, D), lambda i, j: (i, 0))] + [pl.BlockSpec((1, tm, FSH), lambda i, j: (j, i, 0))] * 3,
        out_shape=[_sds((s, D))] + [_sds((NSH, s, FSH), BF16)] * 3,
        scratch_shapes=[pltpu.VMEM((tm, D), BF16), pltpu.VMEM((tm, D), F32)],
        args=(x, gain, wg, wu, wd))


def _ffn_bwd(x, dout, do, gain, ga, gb, wg, wu, wd, name, comm=None):
    s = x.shape[0]
    tm = min(512, s)

    def hidden(do_ref, ga_ref, gb_ref, wd_ref, da_ref, db_ref):
        df = _dot_nt(do_ref[...], wd_ref[0])
        da_ref[0] = (df * ga_ref[0].astype(F32)).astype(BF16)
        db_ref[0] = (df * gb_ref[0].astype(F32)).astype(BF16)

    th = min(1024, s)
    tok = pl.BlockSpec((th, D), lambda i, j: (i, 0))
    sh = pl.BlockSpec((1, th, FSH), lambda i, j: (j, i, 0))
    (da, db), carried = _pallas(
        hidden, comm, name=name + "_hidden", grid=(s // th, NSH),
        in_specs=[tok, sh, sh, pl.BlockSpec((1, FSH, D), lambda i, j: (j, 0, 0))], out_specs=[sh, sh],
        out_shape=[_sds((NSH, s, FSH), BF16)] * 2, scratch_shapes=[], args=(do, ga, gb, wd))

    def inputs(x_ref, d_ref, g_ref, da_ref, db_ref, wg_ref, wu_ref, dx_ref, dg_ref, h_ref):
        @pl.when(pl.program_id(0) == 0)
        def _():
            dg_ref[...] = jnp.zeros_like(dg_ref)

        dh = jnp.zeros((tm, D), F32)
        for j in range(NSH):
            dh = dh + _dot_nt(da_ref[j], wg_ref[j]) + _dot_nt(db_ref[j], wu_ref[j])
        r, xh = _rms(x_ref[...])
        gv = g_ref[...]
        h_ref[...] = (xh * gv).astype(BF16)
        dg_ref[...] += _colsum(dh * xh)
        dx_ref[...] = d_ref[...] + _rms_bwd(dh, xh, r, gv)

    grads = pl.BlockSpec((NSH, tm, FSH), lambda i: (0, i, 0))
    resident = pl.BlockSpec((NSH, D, FSH), lambda i: (0, 0, 0), pipeline_mode=pl.Buffered(1))
    dx, dg, h = pl.pallas_call(
        inputs, name=name + "_input", grid=(s // tm,),
        in_specs=[_rows(tm, D), _rows(tm, D), _full((1, D)), grads, grads, resident, resident],
        out_specs=[_rows(tm, D), _full((1, D)), _rows(tm, D)],
        out_shape=[_sds((s, D)), _sds((1, D)), _sds((s, D), BF16)], compiler_params=_cparams(1),
    )(x, dout, gain, da, db, wg, wu)
    return dx, dg, h, da, db, carried


def _tn(a, b, name, comm=None):
    a_g = a.ndim == 3
    b_g = b.ndim == 3
    g = a.shape[0] if a_g else (b.shape[0] if b_g else 1)
    s, k = a.shape[-2:]
    n = b.shape[-1]
    ts = min(2048 if b.dtype == BF16 else 1024, s)

    def body(a_ref, b_ref, o_ref):
        @pl.when(pl.program_id(1) == 0)
        def _():
            o_ref[...] = jnp.zeros_like(o_ref)

        av = a_ref[0] if a_g else a_ref[...]
        bv = b_ref[0] if b_g else b_ref[...]
        o_ref[0] += _dot_tn(av, bv)

    a_spec = pl.BlockSpec((1, ts, k), lambda gi, si: (gi, si, 0)) if a_g else pl.BlockSpec((ts, k), lambda gi, si: (si, 0))
    b_spec = pl.BlockSpec((1, ts, n), lambda gi, si: (gi, si, 0)) if b_g else pl.BlockSpec((ts, n), lambda gi, si: (si, 0))
    outs, carried = _pallas(body, comm, name=name, grid=(g, s // ts), in_specs=[a_spec, b_spec],
                            out_specs=[pl.BlockSpec((1, k, n), lambda gi, si: (gi, 0, 0))], out_shape=[_sds((g, k, n))],
                            scratch_shapes=[], args=(a, b))
    return outs[0] if comm is None else (outs[0], carried)


_P_WIDTHS = (RGW, RGW, QKVW, ZW, BAP)


def _inproj(x1, gain, ws, name):
    s = x1.shape[0]
    tm = min(256, s)

    def body(x_ref, g_ref, *refs):
        w_refs = refs[:5]
        h_ref = refs[5]
        p_refs = refs[6:]
        _, xh = _rms(x_ref[...])
        h = (xh * g_ref[...]).astype(BF16)
        h_ref[...] = h
        for w_ref, p_ref in zip(w_refs, p_refs):
            p_ref[...] = jnp.dot(h, w_ref[...], preferred_element_type=F32)

    return pl.pallas_call(
        body, name=name, grid=(s // tm,),
        in_specs=[_rows(tm, D), _full((1, D))] + [_full((D, w)) for w in _P_WIDTHS],
        out_specs=[_rows(tm, D)] + [_rows(tm, w) for w in _P_WIDTHS],
        out_shape=[_sds((s, D), BF16)] + [_sds((s, w)) for w in _P_WIDTHS],
        compiler_params=_cparams(1),
    )(x1, gain, *ws)


def _inproj_bwd(x1, dx2, gain, dps, ws, name):
    s = x1.shape[0]
    tm = min(256, s)

    def body(x_ref, d_ref, g_ref, *refs):
        dp_refs = refs[:5]
        w_refs = refs[5:10]
        dx_ref, dxh_ref, dg_ref = refs[10:]

        @pl.when(pl.program_id(0) == 0)
        def _():
            dg_ref[...] = jnp.zeros_like(dg_ref)

        dh = jnp.zeros((tm, D), F32)
        for dp_ref, w_ref in zip(dp_refs, w_refs):
            dh = dh + _dot_nt(dp_ref[...], w_ref[...])
        r, xh = _rms(x_ref[...])
        dg_ref[...] += _colsum(dh * xh)
        dx = d_ref[...] + _rms_bwd(dh, xh, r, g_ref[...])
        dx_ref[...] = dx
        dxh_ref[...] = (0.5 * dx).astype(BF16)

    return pl.pallas_call(
        body, name=name, grid=(s // tm,),
        in_specs=[_rows(tm, D), _rows(tm, D), _full((1, D))] + [_rows(tm, w) for w in _P_WIDTHS]
        + [_full((D, w)) for w in _P_WIDTHS],
        out_specs=[_rows(tm, D), _rows(tm, D), _full((1, D))],
        out_shape=[_sds((s, D)), _sds((s, D), BF16), _sds((1, D))],
        compiler_params=_cparams(1),
    )(x1, dx2, gain, *dps, *ws)


def _halo_specs(s, t, c):
    nb8 = s // 8
    tb = t // 8
    prev = pl.BlockSpec((8, c), lambda i: (jnp.maximum(i * tb - 1, 0), 0))
    nxt = pl.BlockSpec((8, c), lambda i: (jnp.minimum((i + 1) * tb, nb8 - 1), 0))
    return prev, nxt


def _edge_masks(nb):
    i = pl.program_id(0)
    return jnp.where(i > 0, 1.0, 0.0).astype(F32), jnp.where(i < nb - 1, 1.0, 0.0).astype(F32)


def _shifted(xx, off, t):
    n = t + 16
    sh = (-off) % n
    rolled = xx if sh == 0 else pltpu.roll(xx, sh, 0)
    return rolled[8:8 + t]


def _conv(x, w8, bias, name):
    s, c = x.shape
    t = min(256, s)
    nb = s // t

    def body(x_ref, xp_ref, xn_ref, w_ref, b_ref, o_ref):
        pm, nm = _edge_masks(nb)
        for c0 in range(0, c, 512):
            cols = slice(c0, c0 + 512)
            xx = jnp.concatenate([xp_ref[:, cols] * pm, x_ref[:, cols], xn_ref[:, cols] * nm], axis=0)
            acc = jnp.zeros((t, 512), F32) + b_ref[:, cols]
            for j in range(4):
                acc = acc + w_ref[j:j + 1, cols] * _shifted(xx, j - 2, t)
            o_ref[:, cols] = acc

    prev, nxt = _halo_specs(s, t, c)
    return pl.pallas_call(
        body, name=name, grid=(nb,),
        in_specs=[_rows(t, c), prev, nxt, _full((8, c)), _full((1, c))],
        out_specs=_rows(t, c), out_shape=_sds((s, c)), compiler_params=_cparams(1),
    )(x, x, x, w8, bias)


def _conv_bwd(x, dc, w8, name):
    s, c = x.shape
    t = min(256, s)
    nb = s // t

    def body(x_ref, d_ref, dp_ref, dn_ref, w_ref, dx_ref, dw_ref, db_ref):
        @pl.when(pl.program_id(0) == 0)
        def _():
            dw_ref[...] = jnp.zeros_like(dw_ref)
            db_ref[...] = jnp.zeros_like(db_ref)

        pm, nm = _edge_masks(nb)
        for c0 in range(0, c, 512):
            cols = slice(c0, c0 + 512)
            dd = jnp.concatenate([dp_ref[:, cols] * pm, d_ref[:, cols], dn_ref[:, cols] * nm], axis=0)
            xv = x_ref[:, cols]
            acc = jnp.zeros((t, 512), F32)
            for j in range(4):
                dsh = _shifted(dd, 2 - j, t)
                acc = acc + w_ref[j:j + 1, cols] * dsh
                dw_ref[j:j + 1, cols] += _colsum(dsh * xv)
            dx_ref[:, cols] = acc
            db_ref[:, cols] += _colsum(d_ref[:, cols])

    prev, nxt = _halo_specs(s, t, c)
    return pl.pallas_call(
        body, name=name, grid=(nb,),
        in_specs=[_rows(t, c), _rows(t, c), prev, nxt, _full((8, c))],
        out_specs=[_rows(t, c), _full((8, c)), _full((1, c))],
        out_shape=[_sds((s, c)), _sds((8, c)), _sds((1, c))], compiler_params=_cparams(1),
    )(x, dc, dc, dc, w8)


def _rg_gates(xc, pre, lam_row):
    sp8 = RG_C * _softplus(-lam_row)
    out = []
    for d in range(2):
        r = _sig_pos(pre[:, RGW * d:RGW * (d + 1)])
        gi = _sig(pre[:, 2 * RGW + RGW * d:2 * RGW + RGW * (d + 1)])
        la = -r * sp8[:, RGW * d:RGW * (d + 1)]
        a = jnp.exp(la)
        mult = jnp.sqrt(_neg_expm1(2.0 * la))
        out.append((r, gi, a, mult))
    return out


def _mix_prep(c_rg, c_qkv, p_ba, wgates, gbias, lam_row, alog_row, dtb_row, name):
    s = c_rg.shape[0]
    t = min(256, s)

    def body(xc_ref, cq_ref, pc_ref, wg_ref, gb_ref, lam_ref, alog_ref, dtb_ref,
             a0_ref, b0_ref, a1_ref, b1_ref, q_ref, k_ref, v_ref, bg_ref):
        xc = xc_ref[...]
        pre = _dot(xc, wg_ref[...]) + gb_ref[...]
        gates = _rg_gates(xc, pre, lam_ref[...])
        for (r, gi, a, mult), a_ref, b_ref in zip(gates, (a0_ref, a1_ref), (b0_ref, b1_ref)):
            a_ref[...] = a
            b_ref[...] = mult * gi * xc
        cq = cq_ref[...]
        sq = cq * _sig(cq)
        for h in range(NH):
            sl = slice(DH * h, DH * (h + 1))
            qh = sq[:, sl]
            q_ref[:, sl] = qh * lax.rsqrt(jnp.sum(qh * qh, axis=-1, keepdims=True) + EPS) * (DH ** -0.5)
            kh = sq[:, RGW + DH * h:RGW + DH * (h + 1)]
            k_ref[:, sl] = kh * lax.rsqrt(jnp.sum(kh * kh, axis=-1, keepdims=True) + EPS)
        v_ref[...] = sq[:, 2 * RGW:]
        pc = pc_ref[...]
        lane = lax.broadcasted_iota(jnp.int32, pc.shape, 1)
        beta = _sig(pc)
        g = -jnp.exp(alog_ref[...]) * _softplus(pc + dtb_ref[...])
        bg_ref[...] = jnp.where(lane < 8, beta, jnp.where(lane < 16, g, 0.0))

    return pl.pallas_call(
        body, name=name, grid=(s // t,),
        in_specs=[_rows(t, RGW), _rows(t, QKVW), _rows(t, BAP), _full((RGW, 4 * RGW)), _full((1, 4 * RGW)),
                  _full((1, 2 * RGW)), _full((1, BAP)), _full((1, BAP))],
        out_specs=[_rows(t, RGW)] * 7 + [_rows(t, BAP)],
        out_shape=[_sds((s, RGW))] * 7 + [_sds((s, BAP))],
        compiler_params=_cparams(1),
    )(c_rg, c_qkv, p_ba, wgates, gbias, lam_row, alog_row, dtb_row)


def _block_scan(av, bv, row, downwards):
    for k in (1, 2, 4):
        sh = (8 - k) if downwards else k
        m = (row < 8 - k) if downwards else (row >= k)
        a_s = pltpu.roll(av, sh, 0)
        b_s = pltpu.roll(bv, sh, 0)
        bv = jnp.where(m, av * b_s + bv, bv)
        av = jnp.where(m, av * a_s, av)
    return av, bv


def _gates_bwd(xc, wgates, gbias, lam_row, lam0, lam1, hf, hb, name):
    s = xc.shape[0]
    t = min(256, s)
    nb = s // t

    def body(xc_ref, wg_ref, gb_ref, lam_ref, l0_ref, l1_ref, hf_ref, hfp_ref, hfn_ref, hb_ref, hbp_ref, hbn_ref,
             dxc_ref, dpre_ref, xcb_ref, dgb_ref, dlam_ref):
        @pl.when(pl.program_id(0) == 0)
        def _():
            dgb_ref[...] = jnp.zeros_like(dgb_ref)
            dlam_ref[...] = jnp.zeros_like(dlam_ref)

        pm, nm = _edge_masks(nb)
        h_prev = _shifted(jnp.concatenate([hfp_ref[...] * pm, hf_ref[...], hfn_ref[...] * nm], axis=0), -1, t)
        h_next = _shifted(jnp.concatenate([hbp_ref[...] * pm, hb_ref[...], hbn_ref[...] * nm], axis=0), 1, t)
        h_shift = (h_prev, h_next)
        xv = xc_ref[...]
        pre = _dot(xv, wg_ref[...]) + gb_ref[...]
        lam_row_v = lam_ref[...]
        sp8 = RG_C * _softplus(-lam_row_v)
        dsp_dlam = -RG_C * _sig(-lam_row_v)
        gates = _rg_gates(xv, pre, lam_row_v)
        dxc = jnp.zeros((t, RGW), F32)
        dpre_r = []
        dpre_i = []
        for d, ((r, gi, a, mult), l_ref, hs) in enumerate(zip(gates, (l0_ref, l1_ref), h_shift)):
            dbb = l_ref[...]
            da = dbb * hs
            cs = slice(RGW * d, RGW * (d + 1))
            dmult = dbb * gi * xv
            dgi = dbb * mult * xv
            dxc = dxc + dbb * mult * gi
            dla = da * a - dmult * a * a / mult
            dr = -dla * sp8[:, cs]
            dlam_ref[:, cs] += _colsum(-dla * r) * dsp_dlam[:, cs]
            dpre_r.append(dr * r * (1.0 - r))
            dpre_i.append(dgi * gi * (1.0 - gi))
        dpre = jnp.concatenate(dpre_r + dpre_i, axis=1)
        dgb_ref[...] += _colsum(dpre)
        dpre_b = dpre.astype(BF16)
        dpre_ref[...] = dpre_b
        xcb_ref[...] = xv.astype(BF16)
        dxc_ref[...] = dxc + _dot_nt(dpre_b, wg_ref[...])

    prev, nxt = _halo_specs(s, t, RGW)
    return pl.pallas_call(
        body, name=name, grid=(s // t,),
        in_specs=[_rows(t, RGW), _full((RGW, 4 * RGW)), _full((1, 4 * RGW)), _full((1, 2 * RGW))] + [_rows(t, RGW)] * 2
        + [_rows(t, RGW), prev, nxt] * 2,
        out_specs=[_rows(t, RGW), _rows(t, 4 * RGW), _rows(t, RGW), _full((1, 4 * RGW)), _full((1, 2 * RGW))],
        out_shape=[_sds((s, RGW)), _sds((s, 4 * RGW), BF16), _sds((s, RGW), BF16), _sds((1, 4 * RGW)), _sds((1, 2 * RGW))],
        compiler_params=_cparams(1),
    )(xc, wgates, gbias, lam_row, lam0, lam1, hf, hf, hf, hb, hb, hb)


class _GdnMasks:
    def __init__(self, d):
        ri = lax.broadcasted_iota(jnp.int32, (CHUNK, CHUNK), 0)
        ci = lax.broadcasted_iota(jnp.int32, (CHUNK, CHUNK), 1)
        self.incl = (ri >= ci) if d == 0 else (ri <= ci)
        self.strict = (ri > ci) if d == 0 else (ri < ci)
        b16 = jnp.right_shift(ri, 4) == jnp.right_shift(ci, 4)
        b32 = jnp.right_shift(ri, 5) == jnp.right_shift(ci, 5)
        self.diag16 = b16
        self.off32 = jnp.logical_and(b32, jnp.logical_not(b16))
        self.off64 = jnp.logical_not(b32)
        self.eye = jnp.where(ri == ci, 1.0, 0.0).astype(F32)
        self.tri = jnp.where(self.incl, 1.0, 0.0).astype(F32)
        self.last = CHUNK - 1 if d == 0 else 0


def _tri_inv(lmat, m):
    return _tri_inv_many([lmat], [m])[0]


def _tri_inv_many(lmats, masks):
    n = len(lmats)
    ns = [jnp.where(masks[i].diag16, lmats[i], 0.0) for i in range(n)]
    ps = [masks[i].eye - ns[i] for i in range(n)]
    qs = [_dot3(ns[i], ns[i]) for i in range(n)]
    for step in range(3):
        ps = [_dot3(ps[i], masks[i].eye + qs[i]) for i in range(n)]
        if step < 2:
            qs = [_dot3(qs[i], qs[i]) for i in range(n)]
    for off in ("off32", "off64"):
        ts = [_dot3(ps[i], jnp.where(getattr(masks[i], off), lmats[i], 0.0)) for i in range(n)]
        ps = [ps[i] - _dot3(ts[i], ps[i]) for i in range(n)]
    return ps


def _chunk_cumsums(m, bgv):
    return _dot_exact(m.tri, bgv, _NN, True), _dot_exact(m.tri, bgv, ((0,), (1,)), False)


class _GdnHead:
    def __init__(self, qh, kh, vh, kk, q0, bg, gcs, gcs_t, d, h, m):
        cb = 4 * d + h
        cg = 8 + 4 * d + h
        self.q, self.k, self.v = qh, kh, vh
        self.beta = bg[:, cb:cb + 1]
        gcol = gcs[:, cg:cg + 1]
        grow = gcs_t[cg:cg + 1, :]
        gl = gcs[m.last:m.last + 1, cg:cg + 1]
        self.decay = jnp.exp(jnp.where(m.incl, gcol - grow, -1e30))
        self.kb = kh * self.beta
        self.vb = vh * self.beta
        self.a0 = kk * self.beta
        self.q0 = q0
        self.lmat = jnp.where(m.strict, self.a0 * self.decay, 0.0)
        self.attn = self.q0 * self.decay
        self.eg = jnp.exp(gcol)
        self.ek = jnp.exp(gl - gcol)
        self.cd = jnp.exp(gl)
        self.kg = self.kb * self.eg
        self.qd = qh * self.eg
        self.kd = kh * self.ek


HW = NH * DH
SEQ_CB = 4
LOCAL_CB = 4


def _head(h):
    return slice(DH * h, DH * (h + 1))


def _gdn_local_fwd(q, k, v, bg, name):
    s = q.shape[0]
    n = s // CHUNK
    cb = min(LOCAL_CB, n)

    def body(q_ref, k_ref, v_ref, bg_ref, t_ref, u_ref, w_ref, qd_ref, kd_ref, at_ref, cd_ref):
        masks = [_GdnMasks(d) for d in range(2)]
        inst = []
        for jj in range(cb):
            rows = slice(CHUNK * jj, CHUNK * (jj + 1))
            bgv = bg_ref[rows, :]
            qs = [q_ref[rows, _head(h)] for h in range(NH)]
            ks = [k_ref[rows, _head(h)] for h in range(NH)]
            kk = [_dot_nt(ks[h], ks[h]) for h in range(NH)]
            q0 = [_dot_nt(qs[h], ks[h]) for h in range(NH)]
            for d, m in enumerate(masks):
                gcs, gcs_t = _chunk_cumsums(m, bgv)
                for h in range(NH):
                    c = _GdnHead(qs[h], ks[h], v_ref[rows, _head(h)], kk[h], q0[h], bgv, gcs, gcs_t, d, h, m)
                    inst.append((jj, rows, d, h, m, c))
        tms = _tri_inv_many([it[-1].lmat for it in inst], [it[-2] for it in inst])
        for (jj, rows, d, h, m, c), tm in zip(inst, tms):
            sl = _head(h)
            t_ref[jj, d, h] = tm
            u_ref[d, rows, sl] = _dot(tm, c.vb)
            w_ref[d, rows, sl] = _dot(tm, c.kg).astype(BF16)
            qd_ref[d, rows, sl] = c.qd.astype(BF16)
            kd_ref[d, rows, sl] = c.kd.astype(BF16)
            at_ref[jj, d, h] = c.attn.astype(BF16)
            cd_ref[jj, 4 * d + h:4 * d + h + 1, :] = jnp.broadcast_to(c.cd, (1, DH))

    tok = _rows(cb * CHUNK, HW)
    tok2 = pl.BlockSpec((2, cb * CHUNK, HW), lambda i: (0, i, 0))
    mat = pl.BlockSpec((cb, 2, NH, CHUNK, CHUNK), lambda i: (i, 0, 0, 0, 0))
    return pl.pallas_call(
        body, name=name, grid=(n // cb,), in_specs=[tok, tok, tok, _rows(cb * CHUNK, BAP)],
        out_specs=[mat, tok2, tok2, tok2, tok2, mat, pl.BlockSpec((cb, 8, DH), lambda i: (i, 0, 0))],
        out_shape=[_sds((n, 2, NH, CHUNK, CHUNK)), _sds((2, s, HW)), _sds((2, s, HW), BF16), _sds((2, s, HW), BF16),
                   _sds((2, s, HW), BF16), _sds((n, 2, NH, CHUNK, CHUNK), BF16), _sds((n, 8, DH))],
        compiler_params=_cparams(1),
    )(q, k, v, bg)


def _seq_specs(s, order):
    n = s // CHUNK
    cb = min(SEQ_CB, n)
    nb = n // cb
    tb = cb * CHUNK

    def blk(d):
        return (lambda i: i) if order[d] else (lambda i: nb - 1 - i)

    def per_dir(make):
        return [make(d, blk(d)) for d in range(2)]

    tok2 = per_dir(lambda d, f: pl.BlockSpec((1, tb, HW), lambda i: (d, f(i), 0)))
    tok = per_dir(lambda d, f: pl.BlockSpec((tb, HW), lambda i: (f(i), 0)))
    mat = per_dir(lambda d, f: pl.BlockSpec((cb, 1, NH, CHUNK, CHUNK), lambda i: (f(i), d, 0, 0, 0)))
    cds = per_dir(lambda d, f: pl.BlockSpec((cb, 8, DH), lambda i: (f(i), 0, 0)))
    sts = per_dir(lambda d, f: pl.BlockSpec((cb, NH, DH, DH), lambda i: (f(i), 0, 0, 0)))
    dcd = per_dir(lambda d, f: pl.BlockSpec((cb, NH, DH), lambda i: (f(i), 0, 0)))
    return n, cb, nb, tok2, tok, mat, cds, sts, dcd


class _ScanRider:
    def __init__(self, af, bf, ar, br, shifted, tb, nb, up_spec, down_spec):
        s, c = af.shape
        self.shifted, self.t, self.c, self.nb = shifted, tb, c, nb
        self.args = [af, bf, ar, br]
        self.in_specs = [up_spec, up_spec, down_spec, down_spec]
        self.scratch = [pltpu.VMEM((16, c), F32)]
        if shifted:
            tb8 = tb // 8
            self.args += [af, ar]
            self.in_specs += [pl.BlockSpec((8, c), lambda i: (jnp.maximum(i * tb8 - 1, 0), 0)),
                              pl.BlockSpec((8, c), lambda i: (jnp.minimum((nb - i) * tb8, s // 8 - 1), 0))]
            self.scratch += [pltpu.VMEM((tb + 8, c), F32), pltpu.VMEM((tb + 8, c), F32)]
        self.out_specs = [up_spec, down_spec]
        self.out_shape = [_sds((s, c)), _sds((s, c))]

    def begin(self, in_refs, out_refs, scratch_refs):
        i = pl.program_id(0)
        self.carry = scratch_refs[0]

        @pl.when(i == 0)
        def _():
            self.carry[...] = jnp.zeros_like(self.carry)

        af_ref, self.bf_ref, ar_ref, self.br_ref = in_refs[0:4]
        self.hf_ref, self.hr_ref = out_refs
        self.a_up, self.a_dn = af_ref, ar_ref
        if self.shifted:
            t = self.t
            edge = jnp.where(i > 0, 1.0, 0.0).astype(F32)
            fbuf, rbuf = scratch_refs[1:3]
            fbuf[0:8, :] = in_refs[4][...] * edge
            fbuf[8:t + 8, :] = af_ref[...]
            rbuf[0:t, :] = ar_ref[...]
            rbuf[t:t + 8, :] = in_refs[5][...] * edge
            self.a_up, self.a_dn = fbuf, rbuf
        self.row = lax.broadcasted_iota(jnp.int32, (8, self.c), 0)
        self.cf, self.cr = self.carry[0:1, :], self.carry[8:9, :]

    def groups(self, lo, hi):
        ng = self.t // 8
        row = self.row
        for gi in range(lo, hi):
            rf, rr = 8 * gi, 8 * (ng - 1 - gi)
            if self.shifted:
                a_f = jnp.where(row > 0, pltpu.roll(self.a_up[rf + 8:rf + 16, :], 1, 0), pltpu.roll(self.a_up[rf:rf + 8, :], 1, 0))
                a_r = jnp.where(row < 7, pltpu.roll(self.a_dn[rr:rr + 8, :], 7, 0), pltpu.roll(self.a_dn[rr + 8:rr + 16, :], 7, 0))
            else:
                a_f, a_r = self.a_up[rf:rf + 8, :], self.a_dn[rr:rr + 8, :]
            a_f, b_f = _block_scan(a_f, self.bf_ref[rf:rf + 8, :], row, False)
            a_r, b_r = _block_scan(a_r, self.br_ref[rr:rr + 8, :], row, True)
            h_f = a_f * self.cf + b_f
            h_r = a_r * self.cr + b_r
            self.hf_ref[rf:rf + 8, :] = h_f
            self.hr_ref[rr:rr + 8, :] = h_r
            self.cf, self.cr = h_f[7:8, :], h_r[0:1, :]

    def end(self):
        self.carry[0:1, :] = self.cf
        self.carry[8:9, :] = self.cr


def _gdn_seq_fwd(u, w, qd, kd, at, cd, name, scan=None):
    s = u.shape[1]
    n, cb, nb, tok2, tok, mat, cds, sts, _ = _seq_specs(s, (True, False))
    rider = _ScanRider(*scan, False, cb * CHUNK, nb, tok[0], tok[1]) if scan else None
    ri = len(rider.args) if rider else 0

    def body(*refs):
        ins = (refs[0:6], refs[6:12])
        outs = (refs[12 + ri:15 + ri], refs[15 + ri:18 + ri])
        st = refs[18 + ri + (2 if rider else 0)]
        if rider:
            rider.begin(refs[12:12 + ri], refs[18 + ri:20 + ri], refs[21 + ri:])

        @pl.when(pl.program_id(0) == 0)
        def _():
            st[...] = jnp.zeros_like(st)

        for j in range(cb):
            items = []
            for d in range(2):
                jj = j if d == 0 else cb - 1 - j
                items += [(d, h, jj, slice(CHUNK * jj, CHUNK * (jj + 1)), _head(h)) for h in range(NH)]
            shs = [st[d, h] for d, h, _, _, _ in items]
            wss = [_dot(ins[d][1][0, rows, sl], sh) for (d, h, jj, rows, sl), sh in zip(items, shs)]
            vns = [ins[d][0][0, rows, sl] - ws for (d, h, jj, rows, sl), ws in zip(items, wss)]
            news = [sh * ins[d][5][jj, 4 * d + h:4 * d + h + 1, :] + _dot_tn(ins[d][3][0, rows, sl], vn)
                    for (d, h, jj, rows, sl), sh, vn in zip(items, shs, vns)]
            for (d, h, jj, rows, sl), sh, vn, new in zip(items, shs, vns, news):
                o_r, s_r, vn_r = outs[d]
                st[d, h] = new
                s_r[jj, h] = sh.astype(BF16)
                vn_r[rows, sl] = vn.astype(BF16)
                o_r[rows, sl] = _dot(ins[d][2][0, rows, sl], sh) + _dot(ins[d][4][jj, 0, h], vn)
            if rider:
                rider.groups(8 * j, 8 * (j + 1))
        if rider:
            rider.end()

    in_specs, out_specs, out_shape = [], [], []
    for d in range(2):
        in_specs += [tok2[d]] * 4 + [mat[d], cds[d]]
        out_specs += [tok[d], sts[d], tok[d]]
        out_shape += [_sds((s, HW)), _sds((n, NH, DH, DH), BF16), _sds((s, HW), BF16)]
    args = [u, w, qd, kd, at, cd, u, w, qd, kd, at, cd]
    scratch = [pltpu.VMEM((2, NH, DH, DH), F32)]
    if rider:
        in_specs, args = in_specs + rider.in_specs, args + rider.args
        out_specs, out_shape, scratch = out_specs + rider.out_specs, out_shape + rider.out_shape, scratch + rider.scratch
    return pl.pallas_call(
        body, name=name, grid=(nb,), in_specs=in_specs, out_specs=out_specs, out_shape=out_shape,
        scratch_shapes=scratch, compiler_params=_cparams(1),
    )(*args)


def _gdn_seq_bwd(do, w, qd, kd, at, cd, states, vns, name, scan=None):
    s = do.shape[0]
    n, cb, nb, tok2, tok, mat, cds, sts, dcd = _seq_specs(s, (False, True))
    rider = _ScanRider(*scan, True, cb * CHUNK, nb, tok[1], tok[0]) if scan else None
    ri = len(rider.args) if rider else 0

    def body(*refs):
        ins = (refs[0:8], refs[8:16])
        outs = (refs[16 + ri:21 + ri], refs[21 + ri:26 + ri])
        dst = refs[26 + ri + (2 if rider else 0)]
        if rider:
            rider.begin(refs[16:16 + ri], refs[26 + ri:28 + ri], refs[29 + ri:])

        @pl.when(pl.program_id(0) == 0)
        def _():
            dst[...] = jnp.zeros_like(dst)

        for j in range(cb):
            items = []
            for d in range(2):
                jj = cb - 1 - j if d == 0 else j
                items += [(d, h, jj, slice(CHUNK * jj, CHUNK * (jj + 1)), _head(h)) for h in range(NH)]
            dsns = [dst[d, h] for d, h, _, _, _ in items]
            dohs = [ins[d][0][rows, sl] for d, h, jj, rows, sl in items]
            d_vns = [_dot_tn(ins[d][4][jj, 0, h], doh) + _dot(ins[d][3][0, rows, sl], dsn)
                     for (d, h, jj, rows, sl), doh, dsn in zip(items, dohs, dsns)]
            news = [ins[d][5][jj, 4 * d + h:4 * d + h + 1, :] * dsn + _dot_tn(ins[d][2][0, rows, sl], doh)
                    - _dot_tn(ins[d][1][0, rows, sl], d_vn)
                    for (d, h, jj, rows, sl), doh, dsn, d_vn in zip(items, dohs, dsns, d_vns)]
            for (d, h, jj, rows, sl), doh, dsn, d_vn, new in zip(items, dohs, dsns, d_vns, news):
                dvn_r, dkd_r, dqd_r, dw_r, dcd_r = outs[d]
                sh = ins[d][6][jj, h].astype(F32)
                dst[d, h] = new
                dvn_r[rows, sl] = d_vn.astype(BF16)
                dkd_r[rows, sl] = _dot_nt(ins[d][7][rows, sl], dsn)
                dqd_r[rows, sl] = _dot_nt(doh, sh)
                dw_r[rows, sl] = (-_dot_nt(d_vn, sh)).astype(BF16)
                d_cd = jnp.sum(jnp.sum(sh * dsn, axis=1, keepdims=True), axis=0, keepdims=True)
                dcd_r[jj, h:h + 1, :] = jnp.broadcast_to(d_cd, (1, DH))
            if rider:
                rider.groups(8 * j, 8 * (j + 1))
        if rider:
            rider.end()

    in_specs, out_specs, out_shape, args = [], [], [], []
    for d in range(2):
        in_specs += [tok[d]] + [tok2[d]] * 3 + [mat[d], cds[d], sts[d], tok[d]]
        args += [do, w, qd, kd, at, cd, states[d], vns[d]]
        out_specs += [tok[d]] * 4 + [dcd[d]]
        out_shape += [_sds((s, HW), BF16), _sds((s, HW)), _sds((s, HW)), _sds((s, HW), BF16), _sds((n, NH, DH))]
    scratch = [pltpu.VMEM((2, NH, DH, DH), F32)]
    if rider:
        in_specs, args = in_specs + rider.in_specs, args + rider.args
        out_specs, out_shape, scratch = out_specs + rider.out_specs, out_shape + rider.out_shape, scratch + rider.scratch
    return pl.pallas_call(
        body, name=name, grid=(nb,), in_specs=in_specs, out_specs=out_specs, out_shape=out_shape,
        scratch_shapes=scratch, compiler_params=_cparams(1),
    )(*args)


def _gdn_local_bwd(q, k, v, bg, tmat, do, vns, seq_grads, name, comm=None):
    s = q.shape[0]
    n = s // CHUNK
    cb = min(LOCAL_CB, n)

    def body(*refs):
        q_ref, k_ref, v_ref, bg_ref, t_ref, do_ref = refs[0:6]
        vn_refs = refs[6:8]
        sg = (refs[8:13], refs[13:18])
        dq_ref, dk_ref, dv_ref, dbg_ref = refs[18:]
        lane = lax.broadcasted_iota(jnp.int32, (CHUNK, BAP), 1)
        rowi = lax.broadcasted_iota(jnp.int32, (CHUNK, 1), 0)
        ones = jnp.ones((CHUNK, DH), F32)
        masks = [_GdnMasks(d) for d in range(2)]
        inst = []
        for jj in range(cb):
            rows = slice(CHUNK * jj, CHUNK * (jj + 1))
            bgv = bg_ref[rows, :]
            qs = [q_ref[rows, _head(h)] for h in range(NH)]
            ks = [k_ref[rows, _head(h)] for h in range(NH)]
            kk = [_dot_nt(ks[h], ks[h]) for h in range(NH)]
            q0 = [_dot_nt(qs[h], ks[h]) for h in range(NH)]
            for d, m in enumerate(masks):
                gcs, gcs_t = _chunk_cumsums(m, bgv)
                for h in range(NH):
                    c = _GdnHead(qs[h], ks[h], v_ref[rows, _head(h)], kk[h], q0[h], bgv, gcs, gcs_t, d, h, m)
                    inst.append((jj, rows, d, h, m, c))
        ni = len(inst)
        cs = [it[-1] for it in inst]
        tms = [t_ref[jj, d, h] for jj, _, d, h, _, _ in inst]
        d_vns = [sg[d][0][rows, _head(h)] for _, rows, d, h, _, _ in inst]
        d_ws = [sg[d][3][rows, _head(h)] for _, rows, d, h, _, _ in inst]
        d_ts = [_dot_nt(d_vns[i], cs[i].vb) + _dot_nt(d_ws[i], cs[i].kg) for i in range(ni)]
        tts = [tm.T for tm in tms]
        xs = [_dot3(tts[i], d_ts[i]) for i in range(ni)]
        d_ls = [jnp.where(inst[i][4].strict, -_dot3(xs[i], tts[i]), 0.0) for i in range(ni)]
        d_attns = [jnp.where(m.incl, _dot_nt(do_ref[rows, _head(h)], vn_refs[d][rows, _head(h)]), 0.0)
                   for _, rows, d, h, m, _ in inst]
        d_vbs = [_dot(tts[i], d_vns[i]) for i in range(ni)]
        d_kgs = [_dot(tts[i], d_ws[i]) for i in range(ni)]
        d_a0s = [d_ls[i] * cs[i].decay for i in range(ni)]
        d_q0s = [d_attns[i] * cs[i].decay for i in range(ni)]
        es = [(d_ls[i] * cs[i].a0 + d_attns[i] * cs[i].q0) * cs[i].decay for i in range(ni)]
        kb_mm = [_dot(d_a0s[i], cs[i].k) for i in range(ni)]
        q_mm = [_dot(d_q0s[i], cs[i].k) for i in range(ni)]
        k_mm = [_dot_tn(d_a0s[i], cs[i].kb) + _dot_tn(d_q0s[i], cs[i].q) for i in range(ni)]
        e_cols = [_dot_exact(ones, es[i], _TN, False)[:, 0:1] for i in range(ni)]
        acc = {}
        d_gcs, d_betas = [], []
        for i, (jj, rows, d, h, m, c) in enumerate(inst):
            sl = _head(h)
            d_kd, d_qd = sg[d][1][rows, sl], sg[d][2][rows, sl]
            d_cd = sg[d][4][jj, h:h + 1, 0:1]
            d_vb, d_kg = d_vbs[i], d_kgs[i]
            d_kb = kb_mm[i] + d_kg * c.eg
            parts = (q_mm[i] + d_qd * c.eg, k_mm[i] + d_kd * c.ek + d_kb * c.beta, d_vb * c.beta)
            acc[jj, h] = [p + a for a, p in zip(acc[jj, h], parts)] if (jj, h) in acc else list(parts)
            s_kd = jnp.sum(d_kd * c.kd, axis=1, keepdims=True)
            d_gc = (jnp.sum(d_kg * c.kg, axis=1, keepdims=True) + jnp.sum(d_qd * c.qd, axis=1, keepdims=True) - s_kd
                    + jnp.sum(es[i], axis=1, keepdims=True) - e_cols[i])
            d_gl = jnp.sum(s_kd, axis=0, keepdims=True) + d_cd * c.cd
            d_gcs.append(d_gc + jnp.where(rowi == m.last, d_gl, 0.0))
            d_betas.append(jnp.sum(d_kb * c.k, axis=1, keepdims=True) + jnp.sum(d_vb * c.v, axis=1, keepdims=True))
        d_gs = [_dot_exact(inst[i][4].tri, d_gcs[i] * ones, _TN, True)[:, 0:1] for i in range(ni)]
        dbg = [jnp.zeros((CHUNK, BAP), F32) for _ in range(cb)]
        for i, (jj, _, d, h, _, _) in enumerate(inst):
            dbg[jj] = dbg[jj] + jnp.where(lane == 4 * d + h, d_betas[i], 0.0) + jnp.where(lane == 8 + 4 * d + h, d_gs[i], 0.0)
        for jj in range(cb):
            rows = slice(CHUNK * jj, CHUNK * (jj + 1))
            for h in range(NH):
                dq_ref[rows, _head(h)], dk_ref[rows, _head(h)], dv_ref[rows, _head(h)] = acc[jj, h]
            dbg_ref[rows, :] = dbg[jj]

    tok = _rows(cb * CHUNK, HW)
    bgs = _rows(cb * CHUNK, BAP)
    mat = pl.BlockSpec((cb, 2, NH, CHUNK, CHUNK), lambda i: (i, 0, 0, 0, 0))
    dcd = pl.BlockSpec((cb, NH, DH), lambda i: (i, 0, 0))
    args = [q, k, v, bg, tmat, do, vns[0], vns[1]]
    in_specs = [tok, tok, tok, bgs, mat, tok, tok, tok]
    for d in range(2):
        args += list(seq_grads[d])
        in_specs += [tok] * 4 + [dcd]
    return _pallas(body, comm, name=name, grid=(n // cb,), in_specs=in_specs, out_specs=[tok, tok, tok, bgs],
                   out_shape=[_sds((s, HW))] * 3 + [_sds((s, BAP))], scratch_shapes=[], args=args)


def _prep_bwd(c_qkv, p_ba, alog_row, dtb_row, dq, dk, dv, dbg, name):
    s = c_qkv.shape[0]
    t = min(256, s)

    def body(cq_ref, pc_ref, alog_ref, dtb_ref, dq_ref, dk_ref, dv_ref, dbg_ref,
             dcq_ref, dpc_ref, dalog_ref, ddtb_ref):
        @pl.when(pl.program_id(0) == 0)
        def _():
            dalog_ref[...] = jnp.zeros_like(dalog_ref)
            ddtb_ref[...] = jnp.zeros_like(ddtb_ref)

        cq = cq_ref[...]
        sq = cq * _sig(cq)
        sg = _silu_grad(cq)
        for h in range(NH):
            sl = slice(DH * h, DH * (h + 1))
            for off, d_ref, scale in ((0, dq_ref, DH ** -0.5), (RGW, dk_ref, 1.0)):
                csl = slice(off + DH * h, off + DH * (h + 1))
                xh = sq[:, csl]
                nrm = lax.rsqrt(jnp.sum(xh * xh, axis=-1, keepdims=True) + EPS)
                y = xh * nrm
                dy = d_ref[:, sl] * scale
                dcq_ref[:, csl] = nrm * (dy - y * jnp.sum(dy * y, axis=-1, keepdims=True)) * sg[:, csl]
        dcq_ref[:, 2 * RGW:] = dv_ref[...] * sg[:, 2 * RGW:]
        pc = pc_ref[...]
        lane = lax.broadcasted_iota(jnp.int32, pc.shape, 1)
        dbg = dbg_ref[...]
        beta = _sig(pc)
        ea = jnp.exp(alog_ref[...])
        z = pc + dtb_ref[...]
        g = -ea * _softplus(z)
        is_g = jnp.logical_and(lane >= 8, lane < 16)
        d_alpha = jnp.where(is_g, dbg * (-ea) * _sig(z), 0.0)
        dpc_ref[...] = jnp.where(lane < 8, dbg * beta * (1.0 - beta), d_alpha)
        dalog_ref[...] += _colsum(jnp.where(is_g, dbg * g, 0.0))
        ddtb_ref[...] += _colsum(d_alpha)

    return pl.pallas_call(
        body, name=name, grid=(s // t,),
        in_specs=[_rows(t, QKVW), _rows(t, BAP), _full((1, BAP)), _full((1, BAP))] + [_rows(t, HW)] * 3 + [_rows(t, BAP)],
        out_specs=[_rows(t, QKVW), _rows(t, BAP), _full((1, BAP)), _full((1, BAP))],
        out_shape=[_sds((s, QKVW)), _sds((s, BAP)), _sds((1, BAP)), _sds((1, BAP))],
        compiler_params=_cparams(1),
    )(c_qkv, p_ba, alog_row, dtb_row, dq, dk, dv, dbg)


def _mix_out_values(hf, hb, gate, of, ob, z, gn):
    hr = hf + hb
    y_rg = hr * _gelu(gate)
    osum = of + ob
    parts = []
    for h in range(NH):
        sl = slice(DH * h, DH * (h + 1))
        oh = osum[:, sl]
        r, ohat = _rms(oh)
        zh = z[:, sl]
        parts.append((r, ohat, zh))
    y_gdn = jnp.concatenate([ohat * gn * (zh * _sig(zh)) for (r, ohat, zh) in parts], axis=1)
    return hr, y_rg, y_gdn, parts


def _outproj(x1, hf, hb, gate, of, ob, z, gn, wout, name):
    s = x1.shape[0]
    t = min(256, s)

    def body(x_ref, hf_ref, hb_ref, gate_ref, of_ref, ob_ref, z_ref, gn_ref, w_ref, xo_ref, y_ref):
        _, y_rg, y_gdn, _ = _mix_out_values(hf_ref[...], hb_ref[...], gate_ref[...], of_ref[...], ob_ref[...],
                                            z_ref[...], gn_ref[...])
        y = jnp.concatenate([y_rg, y_gdn], axis=1).astype(BF16)
        y_ref[...] = y
        xo_ref[...] = x_ref[...] + jnp.dot(y, w_ref[...], preferred_element_type=F32)

    return pl.pallas_call(
        body, name=name, grid=(s // t,),
        in_specs=[_rows(t, D)] + [_rows(t, RGW)] * 6 + [_full((1, DH)), _full((D, D))],
        out_specs=[_rows(t, D), _rows(t, D)], out_shape=[_sds((s, D)), _sds((s, D), BF16)],
        compiler_params=_cparams(1),
    )(x1, hf, hb, gate, of, ob, z, gn, wout)


def _outproj_bwd(dx2, hf, hb, gate, of, ob, z, gn, wout, name, comm=None):
    s = dx2.shape[0]
    t = min(256, s)

    def body(d_ref, hf_ref, hb_ref, gate_ref, of_ref, ob_ref, z_ref, gn_ref, w_ref,
             dhr_ref, dgate_ref, dos_ref, dz_ref, dgn_ref, db_ref):
        @pl.when(pl.program_id(0) == 0)
        def _():
            dgn_ref[...] = jnp.zeros_like(dgn_ref)

        gate = gate_ref[...]
        gn_v = gn_ref[...]
        hr, _, _, parts = _mix_out_values(hf_ref[...], hb_ref[...], gate, of_ref[...], ob_ref[...], z_ref[...], gn_v)
        dbf = d_ref[...].astype(BF16)
        db_ref[...] = dbf
        dy = _dot_nt(dbf, w_ref[...])
        dyr = dy[:, :RGW]
        dhr_ref[...] = dyr * _gelu(gate)
        dgate_ref[...] = dyr * hr * _gelu_grad(gate)
        dgn = jnp.zeros((1, DH), F32)
        for h, (r, ohat, zh) in enumerate(parts):
            sl = slice(DH * h, DH * (h + 1))
            dyh = dy[:, RGW + DH * h:RGW + DH * (h + 1)]
            sz = zh * _sig(zh)
            dn = dyh * sz
            dz_ref[:, sl] = dyh * ohat * gn_v * _silu_grad(zh)
            dgn = dgn + _colsum(dn * ohat)
            dos_ref[:, sl] = _rms_bwd(dn, ohat, r, gn_v)
        dgn_ref[...] += dgn

    return _pallas(
        body, comm, name=name, grid=(s // t,),
        in_specs=[_rows(t, D)] + [_rows(t, RGW)] * 6 + [_full((1, DH)), _full((D, D))],
        out_specs=[_rows(t, RGW)] * 4 + [_full((1, DH)), _rows(t, D)],
        out_shape=[_sds((s, RGW))] * 4 + [_sds((1, DH)), _sds((s, D), BF16)],
        scratch_shapes=[], args=(dx2, hf, hb, gate, of, ob, z, gn, wout))


def _loss_head(x3, target, gain, name):
    s = x3.shape[0]
    t = min(256, s)

    def body(x_ref, t_ref, g_ref, dx_ref, dxh_ref, loss_ref, dg_ref):
        @pl.when(pl.program_id(0) == 0)
        def _():
            loss_ref[...] = jnp.zeros_like(loss_ref)
            dg_ref[...] = jnp.zeros_like(dg_ref)

        r, xh = _rms(x_ref[...])
        gv = g_ref[...]
        err = xh * gv - t_ref[...]
        per_tok = jnp.mean(err * err, axis=-1, keepdims=True)
        loss_ref[...] += 0.5 * jnp.sum(per_tok, axis=0, keepdims=True)
        dy = err * (1.0 / D)
        dg_ref[...] += _colsum(dy * xh)
        dx = _rms_bwd(dy, xh, r, gv)
        dx_ref[...] = dx
        dxh_ref[...] = (0.5 * dx).astype(BF16)

    return pl.pallas_call(
        body, name=name, grid=(s // t,), in_specs=[_rows(t, D), _rows(t, D), _full((1, D))],
        out_specs=[_rows(t, D), _rows(t, D), _full((8, 128)), _full((1, D))],
        out_shape=[_sds((s, D)), _sds((s, D), BF16), _sds((8, 128)), _sds((1, D))], compiler_params=_cparams(1),
    )(x3, target, gain)


def _adamw_math(wv, gv, mv, vv):
    mn = ADAM_B1 * mv + (1.0 - ADAM_B1) * gv
    vn = ADAM_B2 * vv + (1.0 - ADAM_B2) * (gv * gv)
    m_hat = mn / (1.0 - ADAM_B1 ** ADAM_STEP)
    v_hat = vn / (1.0 - ADAM_B2 ** ADAM_STEP)
    return -ADAM_LR * (m_hat / (jnp.sqrt(v_hat) + ADAM_EPS) + ADAM_WD * wv), mn, vn


def _row_tile(r, c):
    tr = r
    while tr * c * 4 > (1 << 20) and tr % 16 == 0:
        tr //= 2
    return tr


def _adamw(w, g, m, v, name):
    r, c = w.shape
    tr = _row_tile(r, c)

    def body(w_ref, g_ref, m_ref, v_ref, d_ref, nm_ref, nv_ref):
        d_ref[...], nm_ref[...], nv_ref[...] = _adamw_math(w_ref[...], g_ref[...], m_ref[...], v_ref[...])

    return pl.pallas_call(
        body, name=name, grid=(r // tr,), in_specs=[_rows(tr, c)] * 4, out_specs=[_rows(tr, c)] * 3,
        out_shape=[_sds((r, c))] * 3, compiler_params=_cparams(1),
    )(w, g, m, v)


def _adamw_halves(w, own, recv, m, v, c_arr, name):
    r, c = w.shape
    h = r // 2
    tr = _row_tile(h, c)
    nh = h // tr

    def body(c_ref, w_ref, own_ref, recv_ref, m_ref, v_ref, g_ref, d_ref, nm_ref, nv_ref):
        first_half = pl.program_id(0) < nh
        use_own = first_half == (c_ref[0] == 0)
        gv = jnp.where(use_own, own_ref[...], recv_ref[...])
        g_ref[...] = gv
        d_ref[...], nm_ref[...], nv_ref[...] = _adamw_math(w_ref[...], gv, m_ref[...], v_ref[...])

    full = pl.BlockSpec((tr, c), lambda i, c_ref: (i, 0))
    half = pl.BlockSpec((tr, c), lambda i, c_ref: (i % nh, 0))
    return pl.pallas_call(
        body, name=name, out_shape=[_sds((r, c))] * 4,
        grid_spec=pltpu.PrefetchScalarGridSpec(
            num_scalar_prefetch=1, grid=(2 * nh,), in_specs=[full, half, half, full, full], out_specs=[full] * 4),
        compiler_params=_cparams(1),
    )(c_arr, w, own, recv, m, v)


def _mesh_pos():
    return lax.axis_index("x"), lax.axis_index("y"), lax.axis_index("c")


def _other_chips(x, y):
    return [(1 - x, y), (x, 1 - y), (1 - x, 1 - y)]


class _Comm:
    def __init__(self, inputs, out_shapes, scratch, start, finish, space=pltpu.HBM):
        self.inputs, self.out_shapes, self.scratch = list(inputs), list(out_shapes), list(scratch)
        self.start, self.finish, self.space = start, finish, space


def _comm_call(comm, name):
    ni, no = len(comm.inputs), len(comm.out_shapes)

    def body(*refs):
        comm.start(refs[:ni], refs[ni:ni + no], refs[ni + no:])
        comm.finish(refs[:ni], refs[ni:ni + no], refs[ni + no:])

    spec = pl.BlockSpec(memory_space=comm.space)
    return list(pl.pallas_call(body, name=name, out_shape=comm.out_shapes, in_specs=[spec] * ni, out_specs=[spec] * no,
                               scratch_shapes=comm.scratch)(*comm.inputs))


def _join_comm(a, b):
    ia, oa, sa = len(a.inputs), len(a.out_shapes), len(a.scratch)

    def both(method):
        def run(ins, outs, sems):
            getattr(a, method)(ins[:ia], outs[:oa], sems[:sa])
            getattr(b, method)(ins[ia:], outs[oa:], sems[sa:])
        return run

    return _Comm(a.inputs + b.inputs, a.out_shapes + b.out_shapes, a.scratch + b.scratch, both("start"), both("finish"))


def _pallas(body, comm, *, name, grid, in_specs, out_specs, out_shape, scratch_shapes, args):
    params = _cparams(len(grid))
    if comm is None:
        outs = pl.pallas_call(body, name=name, grid=grid, in_specs=in_specs, out_specs=out_specs, out_shape=out_shape,
                              scratch_shapes=scratch_shapes, compiler_params=params)(*args)
        return list(outs), []
    n_in, n_out, n_sc = len(in_specs), len(out_specs), len(scratch_shapes)
    ci, co = len(comm.inputs), len(comm.out_shapes)

    def carried(*refs):
        bounds = [0, n_in, n_in + ci, n_in + ci + n_out, n_in + ci + n_out + co, n_in + ci + n_out + co + n_sc, len(refs)]
        ins, cins, outs, couts, scr, csems = [refs[lo:hi] for lo, hi in zip(bounds[:-1], bounds[1:])]
        ids = [pl.program_id(k) for k in range(len(grid))]
        first = functools.reduce(jnp.logical_and, [i == 0 for i in ids])
        last = functools.reduce(jnp.logical_and, [i == g - 1 for i, g in zip(ids, grid)])

        @pl.when(first)
        def _():
            comm.start(cins, couts, csems)

        body(*ins, *outs, *scr)

        @pl.when(last)
        def _():
            comm.finish(cins, couts, csems)

    hbm = pl.BlockSpec(memory_space=pltpu.HBM)
    outs = pl.pallas_call(
        carried, name=name, grid=grid, in_specs=list(in_specs) + [hbm] * ci, out_specs=list(out_specs) + [hbm] * co,
        out_shape=list(out_shape) + comm.out_shapes, scratch_shapes=list(scratch_shapes) + comm.scratch,
        compiler_params=params)(*args, *comm.inputs)
    return list(outs[:n_out]), list(outs[n_out:])


def _gather_comm(arrays, space, block_rows):
    n_arr = len(arrays)

    def plan(x_refs, out_refs, sems):
        send_sems, recv_sems, local_sems = sems
        x, y, c = _mesh_pos()
        me, sibling = (x, y, c), (x, y, 1 - c)
        chips = _other_chips(x, y)

        def slot(a, px, py, pc):
            return out_refs[a].at[4 * px + 2 * py + pc]

        def copy(a, k, block, to, src=None):
            return pltpu.make_async_remote_copy(
                src_ref=slot(a, *block) if src is None else src, dst_ref=slot(a, *block),
                send_sem=send_sems.at[7 * a + k], recv_sem=recv_sems.at[7 * a + k], device_id=to, device_id_type=MESH)

        srcs = [x_refs[a] if block_rows[a] is None else
                x_refs[a].at[pl.ds(pl.multiple_of(c * block_rows[a], 16), block_rows[a]), :] for a in range(n_arr)]
        local = [pltpu.make_async_copy(srcs[a], slot(a, *me), local_sems.at[a]) for a in range(n_arr)]
        first = []
        for a in range(n_arr):
            first += [copy(a, 1 + j, me, (*chip, c), src=srcs[a]) for j, chip in enumerate(chips)]
            first.append(copy(a, 0, me, sibling, src=srcs[a]))
        return me, sibling, chips, c, copy, local, first

    def start(x_refs, out_refs, sems):
        _, _, _, _, _, local, first = plan(x_refs, out_refs, sems)
        for cp in local + first:
            cp.start()

    def finish(x_refs, out_refs, sems):
        me, sibling, chips, c, copy, local, first = plan(x_refs, out_refs, sems)
        passed = []
        for j, chip in enumerate(chips):
            for a in range(n_arr):
                copy(a, 1 + j, (*chip, c), me).wait_recv()
                fwd = copy(a, 4 + j, (*chip, c), sibling)
                fwd.start()
                passed.append(fwd)
        for a in range(n_arr):
            copy(a, 0, sibling, me).wait_recv()
            for j, chip in enumerate(chips):
                copy(a, 4 + j, (*chip, 1 - c), me).wait_recv()
        for cp in first + passed:
            cp.wait_send()
        for cp in local:
            cp.wait()

    out_shapes = [_sds((8, w.shape[0] if r is None else r) + w.shape[1:], w.dtype) for w, r in zip(arrays, block_rows)]
    scratch = [pltpu.SemaphoreType.DMA((7 * n_arr,)), pltpu.SemaphoreType.DMA((7 * n_arr,)), pltpu.SemaphoreType.DMA((n_arr,))]
    return _Comm(arrays, out_shapes, scratch, start, finish, space)


def _weights_gather_comm(shards):
    return _gather_comm(shards, pltpu.HBM, [w.shape[0] // 2 for w in shards])


def _all_shards(gathered):
    return [o.reshape(NSH, 2 * o.shape[1], o.shape[2]) for o in gathered]


def _gather_small(block, name):
    return _comm_call(_gather_comm([block], pltpu.VMEM, [None]), name)[0]


def _exchange_comm(gs):
    n = len(gs)
    halves = [g.shape[1] // 2 for g in gs]

    def plan(g_refs, land_refs, sems):
        send_sems, recv_sems = sems
        x, y, c = _mesh_pos()
        copies = []
        for a in range(n):
            h = halves[a]
            for s in range(NSH):
                copies.append(pltpu.make_async_remote_copy(
                    src_ref=g_refs[a].at[s, pl.ds(pl.multiple_of((1 - c) * h, 8), h), :], dst_ref=land_refs[a].at[s],
                    send_sem=send_sems.at[NSH * a + s], recv_sem=recv_sems.at[NSH * a + s],
                    device_id=(x, y, 1 - c), device_id_type=MESH))
        return copies

    def start(g_refs, land_refs, sems):
        for cp in plan(g_refs, land_refs, sems):
            cp.start()

    def finish(g_refs, land_refs, sems):
        for cp in plan(g_refs, land_refs, sems):
            cp.wait()

    scratch = [pltpu.SemaphoreType.DMA((NSH * n,)), pltpu.SemaphoreType.DMA((NSH * n,))]
    return _Comm(gs, [_sds((NSH, h, g.shape[2])) for h, g in zip(halves, gs)], scratch, start, finish)


def _chip_sum(g, land, c_arr, name):
    _, h, cols = land.shape

    def body(c_ref, g_ref, l_ref, o_ref):
        o_ref[...] = (g_ref[...] + l_ref[...]).astype(BF16)

    return pl.pallas_call(
        body, name=name, out_shape=_sds((NSH, h, cols), BF16),
        grid_spec=pltpu.PrefetchScalarGridSpec(
            num_scalar_prefetch=1, grid=(NSH,),
            in_specs=[pl.BlockSpec((1, h, cols), lambda s, c_ref: (s, c_ref[0], 0)),
                      pl.BlockSpec((1, h, cols), lambda s, c_ref: (s, 0, 0))],
            out_specs=pl.BlockSpec((1, h, cols), lambda s, c_ref: (s, 0, 0))),
        compiler_params=_cparams(1),
    )(c_arr, g, land)


def _scatter_comm(parts):
    n = len(parts)

    def plan(p_refs, land_refs, sems):
        send_sems, recv_sems, local_sems = sems
        x, y, c = _mesh_pos()
        my_chip = 2 * x + y
        local = [pltpu.make_async_copy(p_refs[a].at[my_chip], land_refs[a].at[my_chip], local_sems.at[a]) for a in range(n)]
        copies = []
        for a in range(n):
            for j, (px, py) in enumerate(_other_chips(x, y)):
                copies.append(pltpu.make_async_remote_copy(
                    src_ref=p_refs[a].at[2 * px + py], dst_ref=land_refs[a].at[my_chip],
                    send_sem=send_sems.at[3 * a + j], recv_sem=recv_sems.at[3 * a + j],
                    device_id=(px, py, c), device_id_type=MESH))
        return local, copies

    def start(p_refs, land_refs, sems):
        local, copies = plan(p_refs, land_refs, sems)
        for cp in local + copies:
            cp.start()

    def finish(p_refs, land_refs, sems):
        local, copies = plan(p_refs, land_refs, sems)
        for cp in copies:
            cp.wait()
        for cp in local:
            cp.wait()

    scratch = [pltpu.SemaphoreType.DMA((3 * n,)), pltpu.SemaphoreType.DMA((3 * n,)), pltpu.SemaphoreType.DMA((n,))]
    return _Comm(parts, [_sds(p.shape, BF16) for p in parts], scratch, start, finish)


def _sum_slots(land, name):
    k, r, c = land.shape
    tr = r // 2 if r % 32 == 0 else r

    def body(l_ref, o_ref):
        acc = l_ref[0].astype(F32)
        for i in range(1, k):
            acc = acc + l_ref[i].astype(F32)
        o_ref[...] = acc

    return pl.pallas_call(
        body, name=name, grid=(r // tr,), in_specs=[pl.BlockSpec((k, tr, c), lambda i: (0, i, 0))],
        out_specs=_rows(tr, c), out_shape=_sds((r, c)), compiler_params=_cparams(1),
    )(land)


def _sibling_swap(halves):
    n = len(halves)

    def body(*refs):
        h_refs, out_refs = refs[:n], refs[n:2 * n]
        send_sems, recv_sems = refs[2 * n:]
        x, y, c = _mesh_pos()
        copies = [pltpu.make_async_remote_copy(
            src_ref=h_refs[a], dst_ref=out_refs[a], send_sem=send_sems.at[a], recv_sem=recv_sems.at[a],
            device_id=(x, y, 1 - c), device_id_type=MESH) for a in range(n)]
        for cp in copies:
            cp.start()
        for cp in copies:
            cp.wait()

    return pl.pallas_call(
        body, name="grad_sibling_swap", out_shape=[_sds(h.shape) for h in halves],
        in_specs=[pl.BlockSpec(memory_space=pltpu.HBM)] * n, out_specs=[pl.BlockSpec(memory_space=pltpu.HBM)] * n,
        scratch_shapes=[pltpu.SemaphoreType.DMA((n,)), pltpu.SemaphoreType.DMA((n,))],
    )(*halves)


def _pad_rows(v, width):
    flat = v.reshape(-1)
    rows = -(-flat.shape[0] // width)
    rows = -(-rows // 8) * 8
    return jnp.pad(flat, (0, rows * width - flat.shape[0])).reshape(rows, width)


def _size(shape):
    n = 1
    for dim in shape:
        n *= dim
    return n


def _row_pack(arrs):
    pieces = []
    for a in arrs:
        rows = -(-a.size // D)
        pieces.append(jnp.pad(a.reshape(-1), (0, rows * D - a.size)).reshape(rows, D))
    total = sum(p.shape[0] for p in pieces)
    if total % 8:
        pieces.append(jnp.zeros((8 - total % 8, D), F32))
    return jnp.concatenate(pieces, axis=0)


def _row_unpack(packed, shapes):
    out, r0 = [], 0
    for shp in shapes:
        n = _size(shp)
        rows = -(-n // D)
        out.append(packed[r0:r0 + rows].reshape(-1)[:n].reshape(shp))
        r0 += rows
    return out


def _block_diag(w):
    eye = jnp.eye(8, dtype=w.dtype)
    return (w[:, :, None, :] * eye[:, None, :, None]).reshape(RGW, RGW)


def _diag_blocks(dense):
    r = dense.reshape(8, 64, 8, 64)
    return jnp.stack([r[n, :, n, :] for n in range(8)])


def _lane_row(v8):
    return jnp.zeros((1, BAP), F32).at[0, 8:16].set(v8.reshape(8))


def _chip_sums(gs, lands, names, c_arr):
    return [_chip_sum(g, l, c_arr, "chip_sum_" + n) for g, l, n in zip(gs, lands, names)]


def _reduce_parts(gs, names, c_arr, tag):
    return _chip_sums(gs, _comm_call(_exchange_comm(gs), "grad_sibling_exchange_" + tag), names, c_arr)


def _local_step(x, target, sw, ffn1_w, later_shards, c_arr):
    (g1, gmix, rg_cw8, rg_cb, wgates, gbias, lam_row, gdn_cw8, alog_row, dtb_row, gn, g2, gfin) = sw
    wg1, wu1, wd1 = ffn1_w

    (x1, a1, b1, fb1), gathered = _ffn_fwd(x, g1, wg1, wu1, wd1, "ffn1_fwd", comm=_weights_gather_comm(later_shards))
    win_sh, wout_sh, wg2, wu2, wd2 = _all_shards(gathered)
    w_in_full = jnp.transpose(win_sh, (1, 0, 2)).reshape(D, NSH * INSH)
    wout = wout_sh.reshape(D, D)
    w_in_groups = (w_in_full[:, 0:512], w_in_full[:, 512:1024], w_in_full[:, 1024:2560], w_in_full[:, 2560:3072],
                   jnp.pad(w_in_full[:, 3072:3088], ((0, 0), (0, BAP - BAW))))
    h2, p_rgx, p_gate, p_qkv, p_z, p_ba = _inproj(x1, gmix, w_in_groups, "in_proj")
    c_rg = _conv(p_rgx, rg_cw8, rg_cb, "rg_conv")
    c_qkv = _conv(p_qkv, gdn_cw8, jnp.zeros((1, QKVW), F32), "gdn_conv")
    a0, bb0, a1s, bb1, q, k, v, bg = _mix_prep(c_rg, c_qkv, p_ba, wgates, gbias, lam_row, alog_row, dtb_row, "mix_prep")
    tmat, gu, gw, gqd, gkd, gat, gcd = _gdn_local_fwd(q, k, v, bg, "gdn_local_fwd")
    of, s0, vn0, ob, s1, vn1, hf, hb = _gdn_seq_fwd(gu, gw, gqd, gkd, gat, gcd, "gdn_seq_fwd", scan=(a0, bb0, a1s, bb1))
    x2, ymix = _outproj(x1, hf, hb, p_gate, of, ob, p_z, gn, wout, "out_proj")
    (x3, a2, b2, fb2), _ = _ffn_fwd(x2, g2, wg2, wu2, wd2, "ffn2_fwd")
    dx3, dob2, loss_blk, d_gfin = _loss_head(x3, target, gfin, "loss_head")

    dx2, d_g2, hb2, dab2, dbb2, _ = _ffn_bwd(x2, dx3, dob2, g2, a2, b2, wg2, wu2, wd2, "ffn2_bwd")
    d_ffn2 = [_tn(dab2, hb2, "ffn2_dwg"), _tn(dbb2, hb2, "ffn2_dwu"), _tn(fb2, dob2, "ffn2_dwd")]

    (d_hr, d_gate, d_os, d_z, d_gn, dx2b), lands = _outproj_bwd(dx2, hf, hb, p_gate, of, ob, p_z, gn, wout, "out_proj_bwd",
                                                               comm=_exchange_comm(d_ffn2))
    parts_ffn2 = _chip_sums(d_ffn2, lands, _BIG_NAMES[5:8], c_arr)
    d_wout = _tn(ymix, dx2b, "dw_out")[0]

    sg = _gdn_seq_bwd(d_os, gw, gqd, gkd, gat, gcd, (s0, s1), (vn0, vn1), "gdn_seq_bwd", scan=(a1s, d_hr, a0, d_hr))
    lam1, lam0 = sg[10:12]
    d_xc, d_pre, xcb, d_gbias, d_lam = _gates_bwd(c_rg, wgates, gbias, lam_row, lam0, lam1, hf, hb, "rg_gates_bwd")
    d_wgates = _tn(xcb, d_pre, "dw_gates")[0]
    d_prgx, d_rgcw8, d_rgcb = _conv_bwd(p_rgx, d_xc, rg_cw8, "rg_conv_bwd")

    (dq, dk, dv, dbg), lands_ffn2 = _gdn_local_bwd(q, k, v, bg, tmat, d_os, (vn0, vn1), (sg[0:5], sg[5:10]), "gdn_local_bwd",
                                                  comm=_scatter_comm(parts_ffn2))
    d_cqkv, d_pba, d_alog, d_dtb = _prep_bwd(c_qkv, p_ba, alog_row, dtb_row, dq, dk, dv, dbg, "gdn_prep_bwd")
    d_pqkv, d_gdncw8, _ = _conv_bwd(p_qkv, d_cqkv, gdn_cw8, "gdn_conv_bwd")

    dps = (d_prgx, d_gate, d_pqkv, d_z, d_pba)
    dx1, dob1, d_gmix = _inproj_bwd(x1, dx2, gmix, dps, w_in_groups, "in_proj_bwd")
    d_win_groups = [_tn(h2, dp, "dw_in_%d" % i)[0] for i, dp in enumerate(dps)]
    d_win = jnp.concatenate(d_win_groups[:4] + [d_win_groups[4][:, :BAW]], axis=1)
    d_mix = [jnp.transpose(d_win.reshape(D, NSH, INSH), (1, 0, 2)), d_wout.reshape(NSH, OUTSH, D)]

    small = dict(
        mix_norm=d_gmix, rg_conv_w=d_rgcw8[:4], rg_conv_b=d_rgcb,
        rg_gate_a_w=jnp.stack([_diag_blocks(d_wgates[:, RGW * i:RGW * (i + 1)]) for i in (0, 1)]),
        rg_gate_x_w=jnp.stack([_diag_blocks(d_wgates[:, RGW * i:RGW * (i + 1)]) for i in (2, 3)]),
        rg_gate_a_b=d_gbias[0, :2 * RGW].reshape(2, RGW), rg_gate_x_b=d_gbias[0, 2 * RGW:].reshape(2, RGW),
        rg_lambda=d_lam.reshape(2, RGW), gdn_conv_w=d_gdncw8[:4],
        gdn_a_log=d_alog[0, 8:16].reshape(2, NH), gdn_dt_bias=d_dtb[0, 8:16].reshape(2, NH),
        gdn_norm=d_gn, ffn2_norm=d_g2, final_norm=d_gfin)
    small_pack = _row_pack([small[n] for n in _SMALL_NAMES[1:]])

    riders = _join_comm(_exchange_comm(d_mix), _gather_comm([small_pack], pltpu.HBM, [None]))
    gx, d_g1, hb1, dab1, dbb1, carried = _ffn_bwd(x, dx1, dob1, g1, a1, b1, wg1, wu1, wd1, "ffn1_bwd", comm=riders)
    parts_mix = _chip_sums(d_mix, carried[0:2], _BIG_NAMES[3:5], c_arr)
    d_wg1, lands_mix = _tn(dab1, hb1, "ffn1_dwg", comm=_scatter_comm(parts_mix))
    parts_wg1 = _reduce_parts([d_wg1], _BIG_NAMES[0:1], c_arr, "ffn1_gate")
    d_wu1, lands_wg1 = _tn(dbb1, hb1, "ffn1_dwu", comm=_scatter_comm(parts_wg1))
    parts_wu1 = _reduce_parts([d_wu1], _BIG_NAMES[1:2], c_arr, "ffn1_up")
    d_wd1, lands_wu1 = _tn(fb1, dob1, "ffn1_dwd", comm=_scatter_comm(parts_wu1))
    parts_wd1 = _reduce_parts([d_wd1], _BIG_NAMES[2:3], c_arr, "ffn1_down")
    lands_ffn1 = lands_wg1 + lands_wu1 + _comm_call(_scatter_comm(parts_wd1), "grad_chip_scatter_ffn1_down")

    halves = [_sum_slots(l, "sum_chips_" + n) for l, n in zip(lands_ffn1 + lands_mix + lands_ffn2, _BIG_NAMES)]
    small_shapes = [small[n].shape for n in _SMALL_NAMES[1:]]
    return loss_blk, gx, halves, d_g1, carried[2], small_shapes


_SMALL_NAMES = ("ffn1_norm", "mix_norm", "rg_conv_w", "rg_conv_b", "rg_gate_a_w", "rg_gate_a_b", "rg_gate_x_w",
                "rg_gate_x_b", "rg_lambda", "gdn_conv_w", "gdn_a_log", "gdn_dt_bias", "gdn_norm", "ffn2_norm", "final_norm")
_SMALL_SHARDED = dict(rg_conv_w=128, rg_gate_a_b=128, rg_gate_x_b=128, rg_lambda=128, gdn_conv_w=384)
_OUT_ORDER = ("ffn1_norm", "ffn1_w_gate", "ffn1_w_up", "ffn1_w_down", "mix_norm", "w_in", "w_out", "rg_conv_w", "rg_conv_b",
              "rg_gate_a_w", "rg_gate_a_b", "rg_gate_x_w", "rg_gate_x_b", "rg_lambda", "gdn_conv_w", "gdn_a_log",
              "gdn_dt_bias", "gdn_norm", "ffn2_norm", "ffn2_w_gate", "ffn2_w_up", "ffn2_w_down", "final_norm")
_BIG_NAMES = ("ffn1_w_gate", "ffn1_w_up", "ffn1_w_down", "w_in", "w_out", "ffn2_w_gate", "ffn2_w_up", "ffn2_w_down")
_TRANSPOSED = ("ffn1_w_gate", "ffn1_w_up", "ffn2_w_gate", "ffn2_w_up")


def kernel(x, ffn1_norm, ffn1_w_gate, ffn1_w_up, ffn1_w_down, mix_norm, w_in, w_out, rg_conv_w, rg_conv_b, rg_gate_a_w, rg_gate_a_b, rg_gate_x_w, rg_gate_x_b, rg_lambda, gdn_conv_w, gdn_a_log, gdn_dt_bias, gdn_norm, ffn2_norm, ffn2_w_gate, ffn2_w_up, ffn2_w_down, final_norm, loss_target, m_ffn1_norm, m_ffn1_w_gate, m_ffn1_w_up, m_ffn1_w_down, m_mix_norm, m_w_in, m_w_out, m_rg_conv_w, m_rg_conv_b, m_rg_gate_a_w, m_rg_gate_a_b, m_rg_gate_x_w, m_rg_gate_x_b, m_rg_lambda, m_gdn_conv_w, m_gdn_a_log, m_gdn_dt_bias, m_gdn_norm, m_ffn2_norm, m_ffn2_w_gate, m_ffn2_w_up, m_ffn2_w_down, m_final_norm, v_ffn1_norm, v_ffn1_w_gate, v_ffn1_w_up, v_ffn1_w_down, v_mix_norm, v_w_in, v_w_out, v_rg_conv_w, v_rg_conv_b, v_rg_gate_a_w, v_rg_gate_a_b, v_rg_gate_x_w, v_rg_gate_x_b, v_rg_lambda, v_gdn_conv_w, v_gdn_a_log, v_gdn_dt_bias, v_gdn_norm, v_ffn2_norm, v_ffn2_w_gate, v_ffn2_w_up, v_ffn2_w_down, v_final_norm):
    args = dict(locals())
    w = {n: args[n] for n in _OUT_ORDER}
    mom = {n: args["m_" + n] for n in _OUT_ORDER}
    var = {n: args["v_" + n] for n in _OUT_ORDER}
    xi, yi, ci = _mesh_pos()
    shard = 2 * xi + yi

    big_bf16 = [w[n][0].astype(BF16) for n in _BIG_NAMES]
    sm_local = _pad_rows(jnp.concatenate([w[n][0].reshape(-1) for n in _SMALL_SHARDED]), 128)
    first = _comm_call(_gather_comm(big_bf16[0:3] + [sm_local], pltpu.HBM, [t.shape[0] // 2 for t in big_bf16[0:3]] + [None]),
                       "gather_first_weights")
    ffn1_w = _all_shards(first[0:3])
    sm_all = first[3][0::2].reshape(NSH, -1)
    sm_full, off = {}, 0
    for n, wd_ in _SMALL_SHARDED.items():
        rows = w[n].shape[1]
        piece = sm_all[:, off:off + rows * wd_].reshape(NSH, rows, wd_)
        sm_full[n] = jnp.transpose(piece, (1, 0, 2)).reshape(rows, NSH * wd_)
        off += rows * wd_

    wa, wx = rg_gate_a_w[0], rg_gate_x_w[0]
    wgates = jnp.concatenate([_block_diag(wa[0]), _block_diag(wa[1]), _block_diag(wx[0]), _block_diag(wx[1])],
                             axis=1).astype(BF16)
    gbias = jnp.concatenate([sm_full["rg_gate_a_b"].reshape(1, -1), sm_full["rg_gate_x_b"].reshape(1, -1)], axis=1)
    sw = (ffn1_norm, mix_norm, jnp.pad(sm_full["rg_conv_w"], ((0, 4), (0, 0))), rg_conv_b, wgates, gbias,
          sm_full["rg_lambda"].reshape(1, -1), jnp.pad(sm_full["gdn_conv_w"], ((0, 4), (0, 0))), _lane_row(gdn_a_log),
          _lane_row(gdn_dt_bias), gdn_norm, ffn2_norm, final_norm.reshape(1, D))
    c_arr = ci.reshape(1).astype(jnp.int32)

    loss_blk, gx, halves, d_g1, small_packs, small_shapes = _local_step(x[0], loss_target[0], sw, ffn1_w, big_bf16[3:], c_arr)
    loss = lax.psum(loss_blk[0, 0], ("x", "y", "c"))
    grads = {}

    g1_all = _gather_small(jnp.pad(d_g1, ((0, 7), (0, 0))), "gather_ffn1_norm_grad")
    sm_sums = [_sum_slots(g1_all, "ffn1_norm_grad_sum")[0:1]] + _row_unpack(_sum_slots(small_packs, "small_grad_sum"), small_shapes)
    for n, g in zip(_SMALL_NAMES, sm_sums):
        if n in _SMALL_SHARDED:
            wd_ = _SMALL_SHARDED[n]
            g = lax.dynamic_slice_in_dim(g, shard * wd_, wd_, axis=1)
        grads[n] = g.reshape(w[n].shape)

    delta, new_m, new_v = {}, {}, {}
    for n, own, recv in zip(_BIG_NAMES, halves, _sibling_swap(halves)):
        to2d = jnp.transpose if n in _TRANSPOSED else (lambda t: t)
        outs4 = _adamw_halves(to2d(w[n][0]), own, recv, to2d(mom[n][0]), to2d(var[n][0]), c_arr, "adamw_" + n)
        grads[n], delta[n], new_m[n], new_v[n] = [to2d(o)[None] for o in outs4]
    packs = [_row_pack([t[n] for n in _SMALL_NAMES]) for t in (w, grads, mom, var)]
    sm_shapes = [w[n].shape for n in _SMALL_NAMES]
    for dst, src in zip((delta, new_m, new_v), _adamw(*packs, "adamw_small")):
        for n, val in zip(_SMALL_NAMES, _row_unpack(src, sm_shapes)):
            dst[n] = val

    outs = [loss, gx[None]]
    for group in (grads, delta, new_m, new_v):
        outs += [group[n] for n in _OUT_ORDER]
    return tuple(outs)
```

```python
import functools

import jax
import jax.numpy as jnp
from jax import lax
from jax.experimental import pallas as pl
from jax.experimental.pallas import tpu as pltpu

F32 = jnp.float32
BF16 = jnp.bfloat16
EPS = 1e-6
D = 1024
NSH = 4
FSH = 704
RGW = 512
QKVW = 1536
ZW = 512
BAW = 16
BAP = 128
INSH = 772
OUTSH = 256
CHUNK = 64
NH = 4
DH = 128
RG_C = 8.0
VMEM_LIMIT = 52 * 1024 * 1024
MESH = pl.DeviceIdType.MESH

ADAM_LR = 0.001
ADAM_B1 = 0.9
ADAM_B2 = 0.999
ADAM_EPS = 1e-08
ADAM_WD = 0.01
ADAM_STEP = 10


def _cparams(n_grid):
    return pltpu.CompilerParams(dimension_semantics=("arbitrary",) * n_grid, vmem_limit_bytes=VMEM_LIMIT)


def _sig(x):
    return 0.5 + 0.5 * jnp.tanh(0.5 * x)


def _sig_pos(x):
    return 1.0 / (1.0 + jnp.exp(-x))


def _softplus(x):
    return jnp.maximum(x, 0.0) + jnp.log(1.0 + jnp.exp(-jnp.abs(x)))


def _neg_expm1(y):
    series = -y * (1.0 + y * (0.5 + y * (1.0 / 6 + y * (1.0 / 24 + y * (1.0 / 120 + y * (1.0 / 720 + y / 5040))))))
    return jnp.where(y > -0.3, series, 1.0 - jnp.exp(y))


_GELU_C = 0.7978845608028654


def _gelu(x):
    t = jnp.tanh(_GELU_C * (x + 0.044715 * x * x * x))
    return 0.5 * x * (1.0 + t)


def _gelu_grad(x):
    t = jnp.tanh(_GELU_C * (x + 0.044715 * x * x * x))
    return 0.5 * (1.0 + t) + 0.5 * x * (1.0 - t * t) * _GELU_C * (1.0 + 3 * 0.044715 * x * x)


def _silu_grad(x):
    s = _sig(x)
    return s * (1.0 + x * (1.0 - s))


def _dot(a, b):
    return jnp.dot(a.astype(BF16), b.astype(BF16), preferred_element_type=F32)


def _dot_nt(a, b):
    return lax.dot_general(a.astype(BF16), b.astype(BF16), (((1,), (1,)), ((), ())), preferred_element_type=F32)


def _dot_tn(a, b):
    return lax.dot_general(a.astype(BF16), b.astype(BF16), (((0,), (0,)), ((), ())), preferred_element_type=F32)


_NN = ((1,), (0,))
_NT = ((1,), (1,))
_TN = ((0,), (0,))


def _dg(a, b, dims):
    return lax.dot_general(a, b, (dims, ((), ())), preferred_element_type=F32)


def _split2(a):
    hi = a.astype(BF16)
    return hi, (a - hi.astype(F32)).astype(BF16)


def _dot3(a, b, dims=_NN):
    ah, al = _split2(a)
    bh, bl = _split2(b)
    return _dg(ah, bh, dims) + _dg(ah, bl, dims) + _dg(al, bh, dims)


def _dot_exact(e, x, dims, e_is_lhs):
    x0 = x.astype(BF16)
    r = x - x0.astype(F32)
    x1 = r.astype(BF16)
    x2 = (r - x1.astype(F32)).astype(BF16)
    eb = e.astype(BF16)
    if e_is_lhs:
        return _dg(eb, x0, dims) + _dg(eb, x1, dims) + _dg(eb, x2, dims)
    return _dg(x0, eb, dims) + _dg(x1, eb, dims) + _dg(x2, eb, dims)


def _rms(xv):
    r = lax.rsqrt(jnp.mean(xv * xv, axis=-1, keepdims=True) + EPS)
    return r, xv * r


def _rms_bwd(dy, xh, r, gain):
    dxh = dy * gain
    return r * (dxh - xh * jnp.mean(dxh * xh, axis=-1, keepdims=True))


def _colsum(v):
    return jnp.sum(v, axis=0, keepdims=True)


def _rows(t, c):
    return pl.BlockSpec((t, c), lambda i: (i, 0))


def _full(shape):
    n = len(shape)
    return pl.BlockSpec(shape, lambda i: (0,) * n)


def _sds(shape, dtype=F32):
    return jax.ShapeDtypeStruct(shape, dtype)


def _ffn_fwd(x, gain, wg, wu, wd, name, comm=None):
    s = x.shape[0]
    tm = min(512, s)

    def body(x_ref, g_ref, wg_ref, wu_ref, wd_ref, xo_ref, ga_ref, gb_ref, f_ref, h_sc, acc):
        j = pl.program_id(1)

        @pl.when(j == 0)
        def _():
            _, xh = _rms(x_ref[...])
            h_sc[...] = (xh * g_ref[...]).astype(BF16)
            acc[...] = jnp.zeros_like(acc)

        h = h_sc[...]

        a = jnp.dot(h, wg_ref[0], preferred_element_type=F32)
        b = jnp.dot(h, wu_ref[0], preferred_element_type=F32)
        sa = _sig(a)
        silu = a * sa
        fv = silu * b
        f = fv.astype(BF16)
        f_ref[0] = f
        ga_ref[0] = (sa * b + fv * (1.0 - sa)).astype(BF16)
        gb_ref[0] = silu.astype(BF16)
        acc[...] += jnp.dot(f, wd_ref[0], preferred_element_type=F32)

        @pl.when(j == NSH - 1)
        def _():
            xo_ref[...] = x_ref[...] + 0.5 * acc[...]

    return _pallas(
        body, comm, name=name, grid=(s // tm, NSH),
        in_specs=[pl.BlockSpec((tm, D), lambda i, j: (i, 0)), pl.BlockSpec((1, D), lambda i, j: (0, 0)),
                  pl.BlockSpec((1, D, FSH), lambda i, j: (j, 0, 0)), pl.BlockSpec((1, D, FSH), lambda i, j: (j, 0, 0)),
                  pl.BlockSpec((1, FSH, D), lambda i, j: (j, 0, 0))],
        out_specs=[pl.BlockSpec((tm, D), lambda i, j: (i, 0))] + [pl.BlockSpec((1, tm, FSH), lambda i, j: (j, i, 0))] * 3,
        out_shape=[_sds((s, D))] + [_sds((NSH, s, FSH), BF16)] * 3,
        scratch_shapes=[pltpu.VMEM((tm, D), BF16), pltpu.VMEM((tm, D), F32)],
        args=(x, gain, wg, wu, wd))


def _ffn_bwd(x, dout, do, gain, ga, gb, wg, wu, wd, name, comm=None):
    s = x.shape[0]
    tm = min(512, s)

    def hidden(do_ref, ga_ref, gb_ref, wd_ref, da_ref, db_ref):
        df = _dot_nt(do_ref[...], wd_ref[0])
        da_ref[0] = (df * ga_ref[0].astype(F32)).astype(BF16)
        db_ref[0] = (df * gb_ref[0].astype(F32)).astype(BF16)

    th = min(1024, s)
    tok = pl.BlockSpec((th, D), lambda i, j: (i, 0))
    sh = pl.BlockSpec((1, th, FSH), lambda i, j: (j, i, 0))
    (da, db), carried = _pallas(
        hidden, comm, name=name + "_hidden", grid=(s // th, NSH),
        in_specs=[tok, sh, sh, pl.BlockSpec((1, FSH, D), lambda i, j: (j, 0, 0))], out_specs=[sh, sh],
        out_shape=[_sds((NSH, s, FSH), BF16)] * 2, scratch_shapes=[], args=(do, ga, gb, wd))

    def inputs(x_ref, d_ref, g_ref, da_ref, db_ref, wg_ref, wu_ref, dx_ref, dg_ref, h_ref):
        @pl.when(pl.program_id(0) == 0)
        def _():
            dg_ref[...] = jnp.zeros_like(dg_ref)

        dh = jnp.zeros((tm, D), F32)
        for j in range(NSH):
            dh = dh + _dot_nt(da_ref[j], wg_ref[j]) + _dot_nt(db_ref[j], wu_ref[j])
        r, xh = _rms(x_ref[...])
        gv = g_ref[...]
        h_ref[...] = (xh * gv).astype(BF16)
        dg_ref[...] += _colsum(dh * xh)
        dx_ref[...] = d_ref[...] + _rms_bwd(dh, xh, r, gv)

    grads = pl.BlockSpec((NSH, tm, FSH), lambda i: (0, i, 0))
    resident = pl.BlockSpec((NSH, D, FSH), lambda i: (0, 0, 0), pipeline_mode=pl.Buffered(1))
    dx, dg, h = pl.pallas_call(
        inputs, name=name + "_input", grid=(s // tm,),
        in_specs=[_rows(tm, D), _rows(tm, D), _full((1, D)), grads, grads, resident, resident],
        out_specs=[_rows(tm, D), _full((1, D)), _rows(tm, D)],
        out_shape=[_sds((s, D)), _sds((1, D)), _sds((s, D), BF16)], compiler_params=_cparams(1),
    )(x, dout, gain, da, db, wg, wu)
    return dx, dg, h, da, db, carried


def _tn(a, b, name, comm=None):
    a_g = a.ndim == 3
    b_g = b.ndim == 3
    g = a.shape[0] if a_g else (b.shape[0] if b_g else 1)
    s, k = a.shape[-2:]
    n = b.shape[-1]
    ts = min(2048 if b.dtype == BF16 else 1024, s)

    def body(a_ref, b_ref, o_ref):
        @pl.when(pl.program_id(1) == 0)
        def _():
            o_ref[...] = jnp.zeros_like(o_ref)

        av = a_ref[0] if a_g else a_ref[...]
        bv = b_ref[0] if b_g else b_ref[...]
        o_ref[0] += _dot_tn(av, bv)

    a_spec = pl.BlockSpec((1, ts, k), lambda gi, si: (gi, si, 0)) if a_g else pl.BlockSpec((ts, k), lambda gi, si: (si, 0))
    b_spec = pl.BlockSpec((1, ts, n), lambda gi, si: (gi, si, 0)) if b_g else pl.BlockSpec((ts, n), lambda gi, si: (si, 0))
    outs, carried = _pallas(body, comm, name=name, grid=(g, s // ts), in_specs=[a_spec, b_spec],
                            out_specs=[pl.BlockSpec((1, k, n), lambda gi, si: (gi, 0, 0))], out_shape=[_sds((g, k, n))],
                            scratch_shapes=[], args=(a, b))
    return outs[0] if comm is None else (outs[0], carried)


_P_WIDTHS = (RGW, RGW, QKVW, ZW, BAP)


def _inproj(x1, gain, ws, name):
    s = x1.shape[0]
    tm = min(256, s)

    def body(x_ref, g_ref, *refs):
        w_refs = refs[:5]
        h_ref = refs[5]
        p_refs = refs[6:]
        _, xh = _rms(x_ref[...])
        h = (xh * g_ref[...]).astype(BF16)
        h_ref[...] = h
        for w_ref, p_ref in zip(w_refs, p_refs):
            p_ref[...] = jnp.dot(h, w_ref[...], preferred_element_type=F32)

    return pl.pallas_call(
        body, name=name, grid=(s // tm,),
        in_specs=[_rows(tm, D), _full((1, D))] + [_full((D, w)) for w in _P_WIDTHS],
        out_specs=[_rows(tm, D)] + [_rows(tm, w) for w in _P_WIDTHS],
        out_shape=[_sds((s, D), BF16)] + [_sds((s, w)) for w in _P_WIDTHS],
        compiler_params=_cparams(1),
    )(x1, gain, *ws)


def _inproj_bwd(x1, dx2, gain, dps, ws, name):
    s = x1.shape[0]
    tm = min(256, s)

    def body(x_ref, d_ref, g_ref, *refs):
        dp_refs = refs[:5]
        w_refs = refs[5:10]
        dx_ref, dxh_ref, dg_ref = refs[10:]

        @pl.when(pl.program_id(0) == 0)
        def _():
            dg_ref[...] = jnp.zeros_like(dg_ref)

        dh = jnp.zeros((tm, D), F32)
        for dp_ref, w_ref in zip(dp_refs, w_refs):
            dh = dh + _dot_nt(dp_ref[...], w_ref[...])
        r, xh = _rms(x_ref[...])
        dg_ref[...] += _colsum(dh * xh)
        dx = d_ref[...] + _rms_bwd(dh, xh, r, g_ref[...])
        dx_ref[...] = dx
        dxh_ref[...] = (0.5 * dx).astype(BF16)

    return pl.pallas_call(
        body, name=name, grid=(s // tm,),
        in_specs=[_rows(tm, D), _rows(tm, D), _full((1, D))] + [_rows(tm, w) for w in _P_WIDTHS]
        + [_full((D, w)) for w in _P_WIDTHS],
        out_specs=[_rows(tm, D), _rows(tm, D), _full((1, D))],
        out_shape=[_sds((s, D)), _sds((s, D), BF16), _sds((1, D))],
        compiler_params=_cparams(1),
    )(x1, dx2, gain, *dps, *ws)


def _halo_specs(s, t, c):
    nb8 = s // 8
    tb = t // 8
    prev = pl.BlockSpec((8, c), lambda i: (jnp.maximum(i * tb - 1, 0), 0))
    nxt = pl.BlockSpec((8, c), lambda i: (jnp.minimum((i + 1) * tb, nb8 - 1), 0))
    return prev, nxt


def _edge_masks(nb):
    i = pl.program_id(0)
    return jnp.where(i > 0, 1.0, 0.0).astype(F32), jnp.where(i < nb - 1, 1.0, 0.0).astype(F32)


def _shifted(xx, off, t):
    n = t + 16
    sh = (-off) % n
    rolled = xx if sh == 0 else pltpu.roll(xx, sh, 0)
    return rolled[8:8 + t]


def _conv(x, w8, bias, name):
    s, c = x.shape
    t = min(256, s)
    nb = s // t

    def body(x_ref, xp_ref, xn_ref, w_ref, b_ref, o_ref):
        pm, nm = _edge_masks(nb)
        for c0 in range(0, c, 512):
            cols = slice(c0, c0 + 512)
            xx = jnp.concatenate([xp_ref[:, cols] * pm, x_ref[:, cols], xn_ref[:, cols] * nm], axis=0)
            acc = jnp.zeros((t, 512), F32) + b_ref[:, cols]
            for j in range(4):
                acc = acc + w_ref[j:j + 1, cols] * _shifted(xx, j - 2, t)
            o_ref[:, cols] = acc

    prev, nxt = _halo_specs(s, t, c)
    return pl.pallas_call(
        body, name=name, grid=(nb,),
        in_specs=[_rows(t, c), prev, nxt, _full((8, c)), _full((1, c))],
        out_specs=_rows(t, c), out_shape=_sds((s, c)), compiler_params=_cparams(1),
    )(x, x, x, w8, bias)


def _conv_bwd(x, dc, w8, name):
    s, c = x.shape
    t = min(256, s)
    nb = s // t

    def body(x_ref, d_ref, dp_ref, dn_ref, w_ref, dx_ref, dw_ref, db_ref):
        @pl.when(pl.program_id(0) == 0)
        def _():
            dw_ref[...] = jnp.zeros_like(dw_ref)
            db_ref[...] = jnp.zeros_like(db_ref)

        pm, nm = _edge_masks(nb)
        for c0 in range(0, c, 512):
            cols = slice(c0, c0 + 512)
            dd = jnp.concatenate([dp_ref[:, cols] * pm, d_ref[:, cols], dn_ref[:, cols] * nm], axis=0)
            xv = x_ref[:, cols]
            acc = jnp.zeros((t, 512), F32)
            for j in range(4):
                dsh = _shifted(dd, 2 - j, t)
                acc = acc + w_ref[j:j + 1, cols] * dsh
                dw_ref[j:j + 1, cols] += _colsum(dsh * xv)
            dx_ref[:, cols] = acc.astype(BF16)
            db_ref[:, cols] += _colsum(d_ref[:, cols])

    prev, nxt = _halo_specs(s, t, c)
    return pl.pallas_call(
        body, name=name, grid=(nb,),
        in_specs=[_rows(t, c), _rows(t, c), prev, nxt, _full((8, c))],
        out_specs=[_rows(t, c), _full((8, c)), _full((1, c))],
        out_shape=[_sds((s, c), BF16), _sds((8, c)), _sds((1, c))], compiler_params=_cparams(1),
    )(x, dc, dc, dc, w8)


def _rg_gates(xc, pre, lam_row):
    sp8 = RG_C * _softplus(-lam_row)
    out = []
    for d in range(2):
        r = _sig_pos(pre[:, RGW * d:RGW * (d + 1)])
        gi = _sig(pre[:, 2 * RGW + RGW * d:2 * RGW + RGW * (d + 1)])
        la = -r * sp8[:, RGW * d:RGW * (d + 1)]
        a = jnp.exp(la)
        mult = jnp.sqrt(_neg_expm1(2.0 * la))
        out.append((r, gi, a, mult))
    return out


def _mix_prep(c_rg, c_qkv, p_ba, wgates, gbias, lam_row, alog_row, dtb_row, name):
    s = c_rg.shape[0]
    t = min(256, s)

    def body(xc_ref, cq_ref, pc_ref, wg_ref, gb_ref, lam_ref, alog_ref, dtb_ref,
             a0_ref, b0_ref, a1_ref, b1_ref, q_ref, k_ref, v_ref, bg_ref):
        xc = xc_ref[...]
        pre = _dot(xc, wg_ref[...]) + gb_ref[...]
        gates = _rg_gates(xc, pre, lam_ref[...])
        for (r, gi, a, mult), a_ref, b_ref in zip(gates, (a0_ref, a1_ref), (b0_ref, b1_ref)):
            a_ref[...] = a
            b_ref[...] = mult * gi * xc
        cq = cq_ref[...]
        sq = cq * _sig(cq)
        for h in range(NH):
            sl = slice(DH * h, DH * (h + 1))
            qh = sq[:, sl]
            q_ref[:, sl] = qh * lax.rsqrt(jnp.sum(qh * qh, axis=-1, keepdims=True) + EPS) * (DH ** -0.5)
            kh = sq[:, RGW + DH * h:RGW + DH * (h + 1)]
            k_ref[:, sl] = kh * lax.rsqrt(jnp.sum(kh * kh, axis=-1, keepdims=True) + EPS)
        v_ref[...] = sq[:, 2 * RGW:]
        pc = pc_ref[...]
        lane = lax.broadcasted_iota(jnp.int32, pc.shape, 1)
        beta = _sig(pc)
        g = -jnp.exp(alog_ref[...]) * _softplus(pc + dtb_ref[...])
        bg_ref[...] = jnp.where(lane < 8, beta, jnp.where(lane < 16, g, 0.0))

    return pl.pallas_call(
        body, name=name, grid=(s // t,),
        in_specs=[_rows(t, RGW), _rows(t, QKVW), _rows(t, BAP), _full((RGW, 4 * RGW)), _full((1, 4 * RGW)),
                  _full((1, 2 * RGW)), _full((1, BAP)), _full((1, BAP))],
        out_specs=[_rows(t, RGW)] * 7 + [_rows(t, BAP)],
        out_shape=[_sds((s, RGW))] * 7 + [_sds((s, BAP))],
        compiler_params=_cparams(1),
    )(c_rg, c_qkv, p_ba, wgates, gbias, lam_row, alog_row, dtb_row)


def _block_scan(av, bv, row, downwards):
    for k in (1, 2, 4):
        sh = (8 - k) if downwards else k
        m = (row < 8 - k) if downwards else (row >= k)
        a_s = pltpu.roll(av, sh, 0)
        b_s = pltpu.roll(bv, sh, 0)
        bv = jnp.where(m, av * b_s + bv, bv)
        av = jnp.where(m, av * a_s, av)
    return av, bv


def _gates_bwd(xc, wgates, gbias, lam_row, lam0, lam1, hf, hb, name):
    s = xc.shape[0]
    t = min(256, s)
    nb = s // t

    def body(xc_ref, wg_ref, gb_ref, lam_ref, l0_ref, l1_ref, hf_ref, hfp_ref, hfn_ref, hb_ref, hbp_ref, hbn_ref,
             dxc_ref, dpre_ref, xcb_ref, dgb_ref, dlam_ref):
        @pl.when(pl.program_id(0) == 0)
        def _():
            dgb_ref[...] = jnp.zeros_like(dgb_ref)
            dlam_ref[...] = jnp.zeros_like(dlam_ref)

        pm, nm = _edge_masks(nb)
        h_prev = _shifted(jnp.concatenate([hfp_ref[...] * pm, hf_ref[...], hfn_ref[...] * nm], axis=0), -1, t)
        h_next = _shifted(jnp.concatenate([hbp_ref[...] * pm, hb_ref[...], hbn_ref[...] * nm], axis=0), 1, t)
        h_shift = (h_prev, h_next)
        xv = xc_ref[...]
        pre = _dot(xv, wg_ref[...]) + gb_ref[...]
        lam_row_v = lam_ref[...]
        sp8 = RG_C * _softplus(-lam_row_v)
        dsp_dlam = -RG_C * _sig(-lam_row_v)
        gates = _rg_gates(xv, pre, lam_row_v)
        dxc = jnp.zeros((t, RGW), F32)
        dpre_r = []
        dpre_i = []
        for d, ((r, gi, a, mult), l_ref, hs) in enumerate(zip(gates, (l0_ref, l1_ref), h_shift)):
            dbb = l_ref[...]
            da = dbb * hs
            cs = slice(RGW * d, RGW * (d + 1))
            dmult = dbb * gi * xv
            dgi = dbb * mult * xv
            dxc = dxc + dbb * mult * gi
            dla = da * a - dmult * a * a / mult
            dr = -dla * sp8[:, cs]
            dlam_ref[:, cs] += _colsum(-dla * r) * dsp_dlam[:, cs]
            dpre_r.append(dr * r * (1.0 - r))
            dpre_i.append(dgi * gi * (1.0 - gi))
        dpre = jnp.concatenate(dpre_r + dpre_i, axis=1)
        dgb_ref[...] += _colsum(dpre)
        dpre_b = dpre.astype(BF16)
        dpre_ref[...] = dpre_b
        xcb_ref[...] = xv.astype(BF16)
        dxc_ref[...] = dxc + _dot_nt(dpre_b, wg_ref[...])

    prev, nxt = _halo_specs(s, t, RGW)
    return pl.pallas_call(
        body, name=name, grid=(s // t,),
        in_specs=[_rows(t, RGW), _full((RGW, 4 * RGW)), _full((1, 4 * RGW)), _full((1, 2 * RGW))] + [_rows(t, RGW)] * 2
        + [_rows(t, RGW), prev, nxt] * 2,
        out_specs=[_rows(t, RGW), _rows(t, 4 * RGW), _rows(t, RGW), _full((1, 4 * RGW)), _full((1, 2 * RGW))],
        out_shape=[_sds((s, RGW)), _sds((s, 4 * RGW), BF16), _sds((s, RGW), BF16), _sds((1, 4 * RGW)), _sds((1, 2 * RGW))],
        compiler_params=_cparams(1),
    )(xc, wgates, gbias, lam_row, lam0, lam1, hf, hf, hf, hb, hb, hb)


class _GdnMasks:
    def __init__(self, d):
        ri = lax.broadcasted_iota(jnp.int32, (CHUNK, CHUNK), 0)
        ci = lax.broadcasted_iota(jnp.int32, (CHUNK, CHUNK), 1)
        self.incl = (ri >= ci) if d == 0 else (ri <= ci)
        self.strict = (ri > ci) if d == 0 else (ri < ci)
        b16 = jnp.right_shift(ri, 4) == jnp.right_shift(ci, 4)
        b32 = jnp.right_shift(ri, 5) == jnp.right_shift(ci, 5)
        self.diag16 = b16
        self.off32 = jnp.logical_and(b32, jnp.logical_not(b16))
        self.off64 = jnp.logical_not(b32)
        self.eye = jnp.where(ri == ci, 1.0, 0.0).astype(F32)
        self.tri = jnp.where(self.incl, 1.0, 0.0).astype(F32)
        self.last = CHUNK - 1 if d == 0 else 0


def _tri_inv(lmat, m):
    return _tri_inv_many([lmat], [m])[0]


def _tri_inv_many(lmats, masks):
    n = len(lmats)
    ns = [jnp.where(masks[i].diag16, lmats[i], 0.0) for i in range(n)]
    ps = [masks[i].eye - ns[i] for i in range(n)]
    qs = [_dot3(ns[i], ns[i]) for i in range(n)]
    for step in range(3):
        ps = [_dot3(ps[i], masks[i].eye + qs[i]) for i in range(n)]
        if step < 2:
            qs = [_dot3(qs[i], qs[i]) for i in range(n)]
    for off in ("off32", "off64"):
        ts = [_dot3(ps[i], jnp.where(getattr(masks[i], off), lmats[i], 0.0)) for i in range(n)]
        ps = [ps[i] - _dot3(ts[i], ps[i]) for i in range(n)]
    return ps


def _chunk_cumsums(m, bgv):
    return _dot_exact(m.tri, bgv, _NN, True), _dot_exact(m.tri, bgv, ((0,), (1,)), False)


class _GdnHead:
    def __init__(self, qh, kh, vh, kk, q0, bg, gcs, gcs_t, d, h, m):
        cb = 4 * d + h
        cg = 8 + 4 * d + h
        self.q, self.k, self.v = qh, kh, vh
        self.beta = bg[:, cb:cb + 1]
        gcol = gcs[:, cg:cg + 1]
        grow = gcs_t[cg:cg + 1, :]
        gl = gcs[m.last:m.last + 1, cg:cg + 1]
        self.decay = jnp.exp(jnp.where(m.incl, gcol - grow, -1e30))
        self.kb = kh * self.beta
        self.vb = vh * self.beta
        self.a0 = kk * self.beta
        self.q0 = q0
        self.lmat = jnp.where(m.strict, self.a0 * self.decay, 0.0)
        self.attn = self.q0 * self.decay
        self.eg = jnp.exp(gcol)
        self.ek = jnp.exp(gl - gcol)
        self.cd = jnp.exp(gl)
        self.kg = self.kb * self.eg
        self.qd = qh * self.eg
        self.kd = kh * self.ek


HW = NH * DH
SEQ_CB = 4
LOCAL_CB = 4


def _head(h):
    return slice(DH * h, DH * (h + 1))


def _gdn_local_fwd(q, k, v, bg, name):
    s = q.shape[0]
    n = s // CHUNK
    cb = min(LOCAL_CB, n)

    def body(q_ref, k_ref, v_ref, bg_ref, t_ref, u_ref, w_ref, qd_ref, kd_ref, at_ref, cd_ref):
        masks = [_GdnMasks(d) for d in range(2)]
        inst = []
        for jj in range(cb):
            rows = slice(CHUNK * jj, CHUNK * (jj + 1))
            bgv = bg_ref[rows, :]
            qs = [q_ref[rows, _head(h)] for h in range(NH)]
            ks = [k_ref[rows, _head(h)] for h in range(NH)]
            kk = [_dot_nt(ks[h], ks[h]) for h in range(NH)]
            q0 = [_dot_nt(qs[h], ks[h]) for h in range(NH)]
            for d, m in enumerate(masks):
                gcs, gcs_t = _chunk_cumsums(m, bgv)
                for h in range(NH):
                    c = _GdnHead(qs[h], ks[h], v_ref[rows, _head(h)], kk[h], q0[h], bgv, gcs, gcs_t, d, h, m)
                    inst.append((jj, rows, d, h, m, c))
        tms = _tri_inv_many([it[-1].lmat for it in inst], [it[-2] for it in inst])
        for (jj, rows, d, h, m, c), tm in zip(inst, tms):
            sl = _head(h)
            t_ref[jj, d, h] = tm
            u_ref[d, rows, sl] = _dot(tm, c.vb)
            w_ref[d, rows, sl] = _dot(tm, c.kg).astype(BF16)
            qd_ref[d, rows, sl] = c.qd.astype(BF16)
            kd_ref[d, rows, sl] = c.kd.astype(BF16)
            at_ref[jj, d, h] = c.attn.astype(BF16)
            cd_ref[jj, 4 * d + h:4 * d + h + 1, :] = jnp.broadcast_to(c.cd, (1, DH))

    tok = _rows(cb * CHUNK, HW)
    tok2 = pl.BlockSpec((2, cb * CHUNK, HW), lambda i: (0, i, 0))
    mat = pl.BlockSpec((cb, 2, NH, CHUNK, CHUNK), lambda i: (i, 0, 0, 0, 0))
    return pl.pallas_call(
        body, name=name, grid=(n // cb,), in_specs=[tok, tok, tok, _rows(cb * CHUNK, BAP)],
        out_specs=[mat, tok2, tok2, tok2, tok2, mat, pl.BlockSpec((cb, 8, DH), lambda i: (i, 0, 0))],
        out_shape=[_sds((n, 2, NH, CHUNK, CHUNK)), _sds((2, s, HW)), _sds((2, s, HW), BF16), _sds((2, s, HW), BF16),
                   _sds((2, s, HW), BF16), _sds((n, 2, NH, CHUNK, CHUNK), BF16), _sds((n, 8, DH))],
        compiler_params=_cparams(1),
    )(q, k, v, bg)


def _seq_specs(s, order):
    n = s // CHUNK
    cb = min(SEQ_CB, n)
    nb = n // cb
    tb = cb * CHUNK

    def blk(d):
        return (lambda i: i) if order[d] else (lambda i: nb - 1 - i)

    def per_dir(make):
        return [make(d, blk(d)) for d in range(2)]

    tok2 = per_dir(lambda d, f: pl.BlockSpec((1, tb, HW), lambda i: (d, f(i), 0)))
    tok = per_dir(lambda d, f: pl.BlockSpec((tb, HW), lambda i: (f(i), 0)))
    mat = per_dir(lambda d, f: pl.BlockSpec((cb, 1, NH, CHUNK, CHUNK), lambda i: (f(i), d, 0, 0, 0)))
    cds = per_dir(lambda d, f: pl.BlockSpec((cb, 8, DH), lambda i: (f(i), 0, 0)))
    sts = per_dir(lambda d, f: pl.BlockSpec((cb, NH, DH, DH), lambda i: (f(i), 0, 0, 0)))
    dcd = per_dir(lambda d, f: pl.BlockSpec((cb, NH, DH), lambda i: (f(i), 0, 0)))
    return n, cb, nb, tok2, tok, mat, cds, sts, dcd


class _ScanRider:
    def __init__(self, af, bf, ar, br, shifted, tb, nb, up_spec, down_spec):
        s, c = af.shape
        self.shifted, self.t, self.c, self.nb = shifted, tb, c, nb
        self.args = [af, bf, ar, br]
        self.in_specs = [up_spec, up_spec, down_spec, down_spec]
        self.scratch = [pltpu.VMEM((16, c), F32)]
        if shifted:
            tb8 = tb // 8
            self.args += [af, ar]
            self.in_specs += [pl.BlockSpec((8, c), lambda i: (jnp.maximum(i * tb8 - 1, 0), 0)),
                              pl.BlockSpec((8, c), lambda i: (jnp.minimum((nb - i) * tb8, s // 8 - 1), 0))]
            self.scratch += [pltpu.VMEM((tb + 8, c), F32), pltpu.VMEM((tb + 8, c), F32)]
        self.out_specs = [up_spec, down_spec]
        self.out_shape = [_sds((s, c)), _sds((s, c))]

    def begin(self, in_refs, out_refs, scratch_refs):
        i = pl.program_id(0)
        self.carry = scratch_refs[0]

        @pl.when(i == 0)
        def _():
            self.carry[...] = jnp.zeros_like(self.carry)

        af_ref, self.bf_ref, ar_ref, self.br_ref = in_refs[0:4]
        self.hf_ref, self.hr_ref = out_refs
        self.a_up, self.a_dn = af_ref, ar_ref
        if self.shifted:
            t = self.t
            edge = jnp.where(i > 0, 1.0, 0.0).astype(F32)
            fbuf, rbuf = scratch_refs[1:3]
            fbuf[0:8, :] = in_refs[4][...] * edge
            fbuf[8:t + 8, :] = af_ref[...]
            rbuf[0:t, :] = ar_ref[...]
            rbuf[t:t + 8, :] = in_refs[5][...] * edge
            self.a_up, self.a_dn = fbuf, rbuf
        self.row = lax.broadcasted_iota(jnp.int32, (8, self.c), 0)
        self.cf, self.cr = self.carry[0:1, :], self.carry[8:9, :]

    def groups(self, lo, hi):
        ng = self.t // 8
        row = self.row
        for gi in range(lo, hi):
            rf, rr = 8 * gi, 8 * (ng - 1 - gi)
            if self.shifted:
                a_f = jnp.where(row > 0, pltpu.roll(self.a_up[rf + 8:rf + 16, :], 1, 0), pltpu.roll(self.a_up[rf:rf + 8, :], 1, 0))
                a_r = jnp.where(row < 7, pltpu.roll(self.a_dn[rr:rr + 8, :], 7, 0), pltpu.roll(self.a_dn[rr + 8:rr + 16, :], 7, 0))
            else:
                a_f, a_r = self.a_up[rf:rf + 8, :], self.a_dn[rr:rr + 8, :]
            a_f, b_f = _block_scan(a_f, self.bf_ref[rf:rf + 8, :], row, False)
            a_r, b_r = _block_scan(a_r, self.br_ref[rr:rr + 8, :], row, True)
            h_f = a_f * self.cf + b_f
            h_r = a_r * self.cr + b_r
            self.hf_ref[rf:rf + 8, :] = h_f
            self.hr_ref[rr:rr + 8, :] = h_r
            self.cf, self.cr = h_f[7:8, :], h_r[0:1, :]

    def end(self):
        self.carry[0:1, :] = self.cf
        self.carry[8:9, :] = self.cr


def _gdn_seq_fwd(u, w, qd, kd, at, cd, name, scan=None):
    s = u.shape[1]
    n, cb, nb, tok2, tok, mat, cds, sts, _ = _seq_specs(s, (True, False))
    rider = _ScanRider(*scan, False, cb * CHUNK, nb, tok[0], tok[1]) if scan else None
    ri = len(rider.args) if rider else 0

    def body(*refs):
        ins = (refs[0:6], refs[6:12])
        outs = (refs[12 + ri:15 + ri], refs[15 + ri:18 + ri])
        st = refs[18 + ri + (2 if rider else 0)]
        if rider:
            rider.begin(refs[12:12 + ri], refs[18 + ri:20 + ri], refs[21 + ri:])

        @pl.when(pl.program_id(0) == 0)
        def _():
            st[...] = jnp.zeros_like(st)

        for j in range(cb):
            items = []
            for d in range(2):
                jj = j if d == 0 else cb - 1 - j
                items += [(d, h, jj, slice(CHUNK * jj, CHUNK * (jj + 1)), _head(h)) for h in range(NH)]
            shs = [st[d, h] for d, h, _, _, _ in items]
            wss = [_dot(ins[d][1][0, rows, sl], sh) for (d, h, jj, rows, sl), sh in zip(items, shs)]
            vns = [ins[d][0][0, rows, sl] - ws for (d, h, jj, rows, sl), ws in zip(items, wss)]
            news = [sh * ins[d][5][jj, 4 * d + h:4 * d + h + 1, :] + _dot_tn(ins[d][3][0, rows, sl], vn)
                    for (d, h, jj, rows, sl), sh, vn in zip(items, shs, vns)]
            for (d, h, jj, rows, sl), sh, vn, new in zip(items, shs, vns, news):
                o_r, s_r, vn_r = outs[d]
                st[d, h] = new
                s_r[jj, h] = sh.astype(BF16)
                vn_r[rows, sl] = vn.astype(BF16)
                o_r[rows, sl] = _dot(ins[d][2][0, rows, sl], sh) + _dot(ins[d][4][jj, 0, h], vn)
            if rider:
                rider.groups(8 * j, 8 * (j + 1))
        if rider:
            rider.end()

    in_specs, out_specs, out_shape = [], [], []
    for d in range(2):
        in_specs += [tok2[d]] * 4 + [mat[d], cds[d]]
        out_specs += [tok[d], sts[d], tok[d]]
        out_shape += [_sds((s, HW)), _sds((n, NH, DH, DH), BF16), _sds((s, HW), BF16)]
    args = [u, w, qd, kd, at, cd, u, w, qd, kd, at, cd]
    scratch = [pltpu.VMEM((2, NH, DH, DH), F32)]
    if rider:
        in_specs, args = in_specs + rider.in_specs, args + rider.args
        out_specs, out_shape, scratch = out_specs + rider.out_specs, out_shape + rider.out_shape, scratch + rider.scratch
    return pl.pallas_call(
        body, name=name, grid=(nb,), in_specs=in_specs, out_specs=out_specs, out_shape=out_shape,
        scratch_shapes=scratch, compiler_params=_cparams(1),
    )(*args)


def _gdn_seq_bwd(do, w, qd, kd, at, cd, states, vns, name, scan=None):
    s = do.shape[0]
    n, cb, nb, tok2, tok, mat, cds, sts, dcd = _seq_specs(s, (False, True))
    rider = _ScanRider(*scan, True, cb * CHUNK, nb, tok[1], tok[0]) if scan else None
    ri = len(rider.args) if rider else 0

    def body(*refs):
        ins = (refs[0:8], refs[8:16])
        outs = (refs[16 + ri:21 + ri], refs[21 + ri:26 + ri])
        dst = refs[26 + ri + (2 if rider else 0)]
        if rider:
            rider.begin(refs[16:16 + ri], refs[26 + ri:28 + ri], refs[29 + ri:])

        @pl.when(pl.program_id(0) == 0)
        def _():
            dst[...] = jnp.zeros_like(dst)

        for j in range(cb):
            items = []
            for d in range(2):
                jj = cb - 1 - j if d == 0 else j
                items += [(d, h, jj, slice(CHUNK * jj, CHUNK * (jj + 1)), _head(h)) for h in range(NH)]
            dsns = [dst[d, h] for d, h, _, _, _ in items]
            dohs = [ins[d][0][rows, sl] for d, h, jj, rows, sl in items]
            d_vns = [_dot_tn(ins[d][4][jj, 0, h], doh) + _dot(ins[d][3][0, rows, sl], dsn)
                     for (d, h, jj, rows, sl), doh, dsn in zip(items, dohs, dsns)]
            news = [ins[d][5][jj, 4 * d + h:4 * d + h + 1, :] * dsn + _dot_tn(ins[d][2][0, rows, sl], doh)
                    - _dot_tn(ins[d][1][0, rows, sl], d_vn)
                    for (d, h, jj, rows, sl), doh, dsn, d_vn in zip(items, dohs, dsns, d_vns)]
            for (d, h, jj, rows, sl), doh, dsn, d_vn, new in zip(items, dohs, dsns, d_vns, news):
                dvn_r, dkd_r, dqd_r, dw_r, dcd_r = outs[d]
                sh = ins[d][6][jj, h].astype(F32)
                dst[d, h] = new
                dvn_r[rows, sl] = d_vn.astype(BF16)
                dkd_r[rows, sl] = _dot_nt(ins[d][7][rows, sl], dsn)
                dqd_r[rows, sl] = _dot_nt(doh, sh)
                dw_r[rows, sl] = (-_dot_nt(d_vn, sh)).astype(BF16)
                d_cd = jnp.sum(jnp.sum(sh * dsn, axis=1, keepdims=True), axis=0, keepdims=True)
                dcd_r[jj, h:h + 1, :] = jnp.broadcast_to(d_cd, (1, DH))
            if rider:
                rider.groups(8 * j, 8 * (j + 1))
        if rider:
            rider.end()

    in_specs, out_specs, out_shape, args = [], [], [], []
    for d in range(2):
        in_specs += [tok[d]] + [tok2[d]] * 3 + [mat[d], cds[d], sts[d], tok[d]]
        args += [do, w, qd, kd, at, cd, states[d], vns[d]]
        out_specs += [tok[d]] * 4 + [dcd[d]]
        out_shape += [_sds((s, HW), BF16), _sds((s, HW)), _sds((s, HW)), _sds((s, HW), BF16), _sds((n, NH, DH))]
    scratch = [pltpu.VMEM((2, NH, DH, DH), F32)]
    if rider:
        in_specs, args = in_specs + rider.in_specs, args + rider.args
        out_specs, out_shape, scratch = out_specs + rider.out_specs, out_shape + rider.out_shape, scratch + rider.scratch
    return pl.pallas_call(
        body, name=name, grid=(nb,), in_specs=in_specs, out_specs=out_specs, out_shape=out_shape,
        scratch_shapes=scratch, compiler_params=_cparams(1),
    )(*args)


def _gdn_local_bwd(q, k, v, bg, tmat, do, vns, seq_grads, name, comm=None):
    s = q.shape[0]
    n = s // CHUNK
    cb = min(LOCAL_CB, n)

    def body(*refs):
        q_ref, k_ref, v_ref, bg_ref, t_ref, do_ref = refs[0:6]
        vn_refs = refs[6:8]
        sg = (refs[8:13], refs[13:18])
        dq_ref, dk_ref, dv_ref, dbg_ref = refs[18:]
        lane = lax.broadcasted_iota(jnp.int32, (CHUNK, BAP), 1)
        rowi = lax.broadcasted_iota(jnp.int32, (CHUNK, 1), 0)
        ones = jnp.ones((CHUNK, DH), F32)
        masks = [_GdnMasks(d) for d in range(2)]
        inst = []
        for jj in range(cb):
            rows = slice(CHUNK * jj, CHUNK * (jj + 1))
            bgv = bg_ref[rows, :]
            qs = [q_ref[rows, _head(h)] for h in range(NH)]
            ks = [k_ref[rows, _head(h)] for h in range(NH)]
            kk = [_dot_nt(ks[h], ks[h]) for h in range(NH)]
            q0 = [_dot_nt(qs[h], ks[h]) for h in range(NH)]
            for d, m in enumerate(masks):
                gcs, gcs_t = _chunk_cumsums(m, bgv)
                for h in range(NH):
                    c = _GdnHead(qs[h], ks[h], v_ref[rows, _head(h)], kk[h], q0[h], bgv, gcs, gcs_t, d, h, m)
                    inst.append((jj, rows, d, h, m, c))
        ni = len(inst)
        cs = [it[-1] for it in inst]
        tms = [t_ref[jj, d, h] for jj, _, d, h, _, _ in inst]
        d_vns = [sg[d][0][rows, _head(h)] for _, rows, d, h, _, _ in inst]
        d_ws = [sg[d][3][rows, _head(h)] for _, rows, d, h, _, _ in inst]
        d_ts = [_dot_nt(d_vns[i], cs[i].vb) + _dot_nt(d_ws[i], cs[i].kg) for i in range(ni)]
        tts = [tm.T for tm in tms]
        xs = [_dot3(tts[i], d_ts[i]) for i in range(ni)]
        d_ls = [jnp.where(inst[i][4].strict, -_dot3(xs[i], tts[i]), 0.0) for i in range(ni)]
        d_attns = [jnp.where(m.incl, _dot_nt(do_ref[rows, _head(h)], vn_refs[d][rows, _head(h)]), 0.0)
                   for _, rows, d, h, m, _ in inst]
        d_vbs = [_dot(tts[i], d_vns[i]) for i in range(ni)]
        d_kgs = [_dot(tts[i], d_ws[i]) for i in range(ni)]
        d_a0s = [d_ls[i] * cs[i].decay for i in range(ni)]
        d_q0s = [d_attns[i] * cs[i].decay for i in range(ni)]
        es = [(d_ls[i] * cs[i].a0 + d_attns[i] * cs[i].q0) * cs[i].decay for i in range(ni)]
        kb_mm = [_dot(d_a0s[i], cs[i].k) for i in range(ni)]
        q_mm = [_dot(d_q0s[i], cs[i].k) for i in range(ni)]
        k_mm = [_dot_tn(d_a0s[i], cs[i].kb) + _dot_tn(d_q0s[i], cs[i].q) for i in range(ni)]
        e_cols = [_dot_exact(ones, es[i], _TN, False)[:, 0:1] for i in range(ni)]
        acc = {}
        d_gcs, d_betas = [], []
        for i, (jj, rows, d, h, m, c) in enumerate(inst):
            sl = _head(h)
            d_kd, d_qd = sg[d][1][rows, sl], sg[d][2][rows, sl]
            d_cd = sg[d][4][jj, h:h + 1, 0:1]
            d_vb, d_kg = d_vbs[i], d_kgs[i]
            d_kb = kb_mm[i] + d_kg * c.eg
            parts = (q_mm[i] + d_qd * c.eg, k_mm[i] + d_kd * c.ek + d_kb * c.beta, d_vb * c.beta)
            acc[jj, h] = [p + a for a, p in zip(acc[jj, h], parts)] if (jj, h) in acc else list(parts)
            s_kd = jnp.sum(d_kd * c.kd, axis=1, keepdims=True)
            d_gc = (jnp.sum(d_kg * c.kg, axis=1, keepdims=True) + jnp.sum(d_qd * c.qd, axis=1, keepdims=True) - s_kd
                    + jnp.sum(es[i], axis=1, keepdims=True) - e_cols[i])
            d_gl = jnp.sum(s_kd, axis=0, keepdims=True) + d_cd * c.cd
            d_gcs.append(d_gc + jnp.where(rowi == m.last, d_gl, 0.0))
            d_betas.append(jnp.sum(d_kb * c.k, axis=1, keepdims=True) + jnp.sum(d_vb * c.v, axis=1, keepdims=True))
        d_gs = [_dot_exact(inst[i][4].tri, d_gcs[i] * ones, _TN, True)[:, 0:1] for i in range(ni)]
        dbg = [jnp.zeros((CHUNK, BAP), F32) for _ in range(cb)]
        for i, (jj, _, d, h, _, _) in enumerate(inst):
            dbg[jj] = dbg[jj] + jnp.where(lane == 4 * d + h, d_betas[i], 0.0) + jnp.where(lane == 8 + 4 * d + h, d_gs[i], 0.0)
        for jj in range(cb):
            rows = slice(CHUNK * jj, CHUNK * (jj + 1))
            for h in range(NH):
                dq_ref[rows, _head(h)], dk_ref[rows, _head(h)], dv_ref[rows, _head(h)] = acc[jj, h]
            dbg_ref[rows, :] = dbg[jj]

    tok = _rows(cb * CHUNK, HW)
    bgs = _rows(cb * CHUNK, BAP)
    mat = pl.BlockSpec((cb, 2, NH, CHUNK, CHUNK), lambda i: (i, 0, 0, 0, 0))
    dcd = pl.BlockSpec((cb, NH, DH), lambda i: (i, 0, 0))
    args = [q, k, v, bg, tmat, do, vns[0], vns[1]]
    in_specs = [tok, tok, tok, bgs, mat, tok, tok, tok]
    for d in range(2):
        args += list(seq_grads[d])
        in_specs += [tok] * 4 + [dcd]
    return _pallas(body, comm, name=name, grid=(n // cb,), in_specs=in_specs, out_specs=[tok, tok, tok, bgs],
                   out_shape=[_sds((s, HW))] * 3 + [_sds((s, BAP))], scratch_shapes=[], args=args)


def _prep_bwd(c_qkv, p_ba, alog_row, dtb_row, dq, dk, dv, dbg, name):
    s = c_qkv.shape[0]
    t = min(256, s)

    def body(cq_ref, pc_ref, alog_ref, dtb_ref, dq_ref, dk_ref, dv_ref, dbg_ref,
             dcq_ref, dpc_ref, dalog_ref, ddtb_ref):
        @pl.when(pl.program_id(0) == 0)
        def _():
            dalog_ref[...] = jnp.zeros_like(dalog_ref)
            ddtb_ref[...] = jnp.zeros_like(ddtb_ref)

        cq = cq_ref[...]
        sq = cq * _sig(cq)
        sg = _silu_grad(cq)
        for h in range(NH):
            sl = slice(DH * h, DH * (h + 1))
            for off, d_ref, scale in ((0, dq_ref, DH ** -0.5), (RGW, dk_ref, 1.0)):
                csl = slice(off + DH * h, off + DH * (h + 1))
                xh = sq[:, csl]
                nrm = lax.rsqrt(jnp.sum(xh * xh, axis=-1, keepdims=True) + EPS)
                y = xh * nrm
                dy = d_ref[:, sl] * scale
                dcq_ref[:, csl] = nrm * (dy - y * jnp.sum(dy * y, axis=-1, keepdims=True)) * sg[:, csl]
        dcq_ref[:, 2 * RGW:] = dv_ref[...] * sg[:, 2 * RGW:]
        pc = pc_ref[...]
        lane = lax.broadcasted_iota(jnp.int32, pc.shape, 1)
        dbg = dbg_ref[...]
        beta = _sig(pc)
        ea = jnp.exp(alog_ref[...])
        z = pc + dtb_ref[...]
        g = -ea * _softplus(z)
        is_g = jnp.logical_and(lane >= 8, lane < 16)
        d_alpha = jnp.where(is_g, dbg * (-ea) * _sig(z), 0.0)
        dpc_ref[...] = jnp.where(lane < 8, dbg * beta * (1.0 - beta), d_alpha).astype(BF16)
        dalog_ref[...] += _colsum(jnp.where(is_g, dbg * g, 0.0))
        ddtb_ref[...] += _colsum(d_alpha)

    return pl.pallas_call(
        body, name=name, grid=(s // t,),
        in_specs=[_rows(t, QKVW), _rows(t, BAP), _full((1, BAP)), _full((1, BAP))] + [_rows(t, HW)] * 3 + [_rows(t, BAP)],
        out_specs=[_rows(t, QKVW), _rows(t, BAP), _full((1, BAP)), _full((1, BAP))],
        out_shape=[_sds((s, QKVW)), _sds((s, BAP), BF16), _sds((1, BAP)), _sds((1, BAP))],
        compiler_params=_cparams(1),
    )(c_qkv, p_ba, alog_row, dtb_row, dq, dk, dv, dbg)


def _mix_out_values(hf, hb, gate, of, ob, z, gn):
    hr = hf + hb
    y_rg = hr * _gelu(gate)
    osum = of + ob
    parts = []
    for h in range(NH):
        sl = slice(DH * h, DH * (h + 1))
        oh = osum[:, sl]
        r, ohat = _rms(oh)
        zh = z[:, sl]
        parts.append((r, ohat, zh))
    y_gdn = jnp.concatenate([ohat * gn * (zh * _sig(zh)) for (r, ohat, zh) in parts], axis=1)
    return hr, y_rg, y_gdn, parts


def _outproj(x1, hf, hb, gate, of, ob, z, gn, wout, name):
    s = x1.shape[0]
    t = min(256, s)

    def body(x_ref, hf_ref, hb_ref, gate_ref, of_ref, ob_ref, z_ref, gn_ref, w_ref, xo_ref, y_ref):
        _, y_rg, y_gdn, _ = _mix_out_values(hf_ref[...], hb_ref[...], gate_ref[...], of_ref[...], ob_ref[...],
                                            z_ref[...], gn_ref[...])
        y = jnp.concatenate([y_rg, y_gdn], axis=1).astype(BF16)
        y_ref[...] = y
        xo_ref[...] = x_ref[...] + jnp.dot(y, w_ref[...], preferred_element_type=F32)

    return pl.pallas_call(
        body, name=name, grid=(s // t,),
        in_specs=[_rows(t, D)] + [_rows(t, RGW)] * 6 + [_full((1, DH)), _full((D, D))],
        out_specs=[_rows(t, D), _rows(t, D)], out_shape=[_sds((s, D)), _sds((s, D), BF16)],
        compiler_params=_cparams(1),
    )(x1, hf, hb, gate, of, ob, z, gn, wout)


def _outproj_bwd(dx2, hf, hb, gate, of, ob, z, gn, wout, name, comm=None):
    s = dx2.shape[0]
    t = min(256, s)

    def body(d_ref, hf_ref, hb_ref, gate_ref, of_ref, ob_ref, z_ref, gn_ref, w_ref,
             dhr_ref, dgate_ref, dos_ref, dz_ref, dgn_ref, db_ref):
        @pl.when(pl.program_id(0) == 0)
        def _():
            dgn_ref[...] = jnp.zeros_like(dgn_ref)

        gate = gate_ref[...]
        gn_v = gn_ref[...]
        hr, _, _, parts = _mix_out_values(hf_ref[...], hb_ref[...], gate, of_ref[...], ob_ref[...], z_ref[...], gn_v)
        dbf = d_ref[...].astype(BF16)
        db_ref[...] = dbf
        dy = _dot_nt(dbf, w_ref[...])
        dyr = dy[:, :RGW]
        dhr_ref[...] = dyr * _gelu(gate)
        dgate_ref[...] = (dyr * hr * _gelu_grad(gate)).astype(BF16)
        dgn = jnp.zeros((1, DH), F32)
        for h, (r, ohat, zh) in enumerate(parts):
            sl = slice(DH * h, DH * (h + 1))
            dyh = dy[:, RGW + DH * h:RGW + DH * (h + 1)]
            sz = zh * _sig(zh)
            dn = dyh * sz
            dz_ref[:, sl] = (dyh * ohat * gn_v * _silu_grad(zh)).astype(BF16)
            dgn = dgn + _colsum(dn * ohat)
            dos_ref[:, sl] = _rms_bwd(dn, ohat, r, gn_v).astype(BF16)
        dgn_ref[...] += dgn

    return _pallas(
        body, comm, name=name, grid=(s // t,),
        in_specs=[_rows(t, D)] + [_rows(t, RGW)] * 6 + [_full((1, DH)), _full((D, D))],
        out_specs=[_rows(t, RGW)] * 4 + [_full((1, DH)), _rows(t, D)],
        out_shape=[_sds((s, RGW))] + [_sds((s, RGW), BF16)] * 3 + [_sds((1, DH)), _sds((s, D), BF16)],
        scratch_shapes=[], args=(dx2, hf, hb, gate, of, ob, z, gn, wout))


def _loss_head(x3, target, gain, name):
    s = x3.shape[0]
    t = min(256, s)

    def body(x_ref, t_ref, g_ref, dx_ref, dxh_ref, loss_ref, dg_ref):
        @pl.when(pl.program_id(0) == 0)
        def _():
            loss_ref[...] = jnp.zeros_like(loss_ref)
            dg_ref[...] = jnp.zeros_like(dg_ref)

        r, xh = _rms(x_ref[...])
        gv = g_ref[...]
        err = xh * gv - t_ref[...]
        per_tok = jnp.mean(err * err, axis=-1, keepdims=True)
        loss_ref[...] += 0.5 * jnp.sum(per_tok, axis=0, keepdims=True)
        dy = err * (1.0 / D)
        dg_ref[...] += _colsum(dy * xh)
        dx = _rms_bwd(dy, xh, r, gv)
        dx_ref[...] = dx
        dxh_ref[...] = (0.5 * dx).astype(BF16)

    return pl.pallas_call(
        body, name=name, grid=(s // t,), in_specs=[_rows(t, D), _rows(t, D), _full((1, D))],
        out_specs=[_rows(t, D), _rows(t, D), _full((8, 128)), _full((1, D))],
        out_shape=[_sds((s, D)), _sds((s, D), BF16), _sds((8, 128)), _sds((1, D))], compiler_params=_cparams(1),
    )(x3, target, gain)


def _adamw_math(wv, gv, mv, vv):
    mn = ADAM_B1 * mv + (1.0 - ADAM_B1) * gv
    vn = ADAM_B2 * vv + (1.0 - ADAM_B2) * (gv * gv)
    m_hat = mn / (1.0 - ADAM_B1 ** ADAM_STEP)
    v_hat = vn / (1.0 - ADAM_B2 ** ADAM_STEP)
    return -ADAM_LR * (m_hat / (jnp.sqrt(v_hat) + ADAM_EPS) + ADAM_WD * wv), mn, vn


def _row_tile(r, c):
    tr = r
    while tr * c * 4 > (1 << 20) and tr % 16 == 0:
        tr //= 2
    return tr


def _adamw(w, g, m, v, name):
    r, c = w.shape
    tr = _row_tile(r, c)

    def body(w_ref, g_ref, m_ref, v_ref, d_ref, nm_ref, nv_ref):
        d_ref[...], nm_ref[...], nv_ref[...] = _adamw_math(w_ref[...], g_ref[...], m_ref[...], v_ref[...])

    return pl.pallas_call(
        body, name=name, grid=(r // tr,), in_specs=[_rows(tr, c)] * 4, out_specs=[_rows(tr, c)] * 3,
        out_shape=[_sds((r, c))] * 3, compiler_params=_cparams(1),
    )(w, g, m, v)


def _adamw_halves(w, own, recv, m, v, c_arr, name):
    r, c = w.shape
    h = r // 2
    tr = _row_tile(h, c)
    nh = h // tr

    def body(c_ref, w_ref, own_ref, recv_ref, m_ref, v_ref, g_ref, d_ref, nm_ref, nv_ref):
        first_half = pl.program_id(0) < nh
        use_own = first_half == (c_ref[0] == 0)
        gv = jnp.where(use_own, own_ref[...], recv_ref[...])
        g_ref[...] = gv
        d_ref[...], nm_ref[...], nv_ref[...] = _adamw_math(w_ref[...], gv, m_ref[...], v_ref[...])

    full = pl.BlockSpec((tr, c), lambda i, c_ref: (i, 0))
    half = pl.BlockSpec((tr, c), lambda i, c_ref: (i % nh, 0))
    return pl.pallas_call(
        body, name=name, out_shape=[_sds((r, c))] * 4,
        grid_spec=pltpu.PrefetchScalarGridSpec(
            num_scalar_prefetch=1, grid=(2 * nh,), in_specs=[full, half, half, full, full], out_specs=[full] * 4),
        compiler_params=_cparams(1),
    )(c_arr, w, own, recv, m, v)


def _mesh_pos():
    return lax.axis_index("x"), lax.axis_index("y"), lax.axis_index("c")


def _other_chips(x, y):
    return [(1 - x, y), (x, 1 - y), (1 - x, 1 - y)]


class _Comm:
    def __init__(self, inputs, out_shapes, scratch, start, finish, space=pltpu.HBM):
        self.inputs, self.out_shapes, self.scratch = list(inputs), list(out_shapes), list(scratch)
        self.start, self.finish, self.space = start, finish, space


def _comm_call(comm, name):
    ni, no = len(comm.inputs), len(comm.out_shapes)

    def body(*refs):
        comm.start(refs[:ni], refs[ni:ni + no], refs[ni + no:])
        comm.finish(refs[:ni], refs[ni:ni + no], refs[ni + no:])

    spec = pl.BlockSpec(memory_space=comm.space)
    return list(pl.pallas_call(body, name=name, out_shape=comm.out_shapes, in_specs=[spec] * ni, out_specs=[spec] * no,
                               scratch_shapes=comm.scratch)(*comm.inputs))


def _join_comm(a, b):
    ia, oa, sa = len(a.inputs), len(a.out_shapes), len(a.scratch)

    def both(method):
        def run(ins, outs, sems):
            getattr(a, method)(ins[:ia], outs[:oa], sems[:sa])
            getattr(b, method)(ins[ia:], outs[oa:], sems[sa:])
        return run

    return _Comm(a.inputs + b.inputs, a.out_shapes + b.out_shapes, a.scratch + b.scratch, both("start"), both("finish"))


def _pallas(body, comm, *, name, grid, in_specs, out_specs, out_shape, scratch_shapes, args):
    params = _cparams(len(grid))
    if comm is None:
        outs = pl.pallas_call(body, name=name, grid=grid, in_specs=in_specs, out_specs=out_specs, out_shape=out_shape,
                              scratch_shapes=scratch_shapes, compiler_params=params)(*args)
        return list(outs), []
    n_in, n_out, n_sc = len(in_specs), len(out_specs), len(scratch_shapes)
    ci, co = len(comm.inputs), len(comm.out_shapes)

    def carried(*refs):
        bounds = [0, n_in, n_in + ci, n_in + ci + n_out, n_in + ci + n_out + co, n_in + ci + n_out + co + n_sc, len(refs)]
        ins, cins, outs, couts, scr, csems = [refs[lo:hi] for lo, hi in zip(bounds[:-1], bounds[1:])]
        ids = [pl.program_id(k) for k in range(len(grid))]
        first = functools.reduce(jnp.logical_and, [i == 0 for i in ids])
        last = functools.reduce(jnp.logical_and, [i == g - 1 for i, g in zip(ids, grid)])

        @pl.when(first)
        def _():
            comm.start(cins, couts, csems)

        body(*ins, *outs, *scr)

        @pl.when(last)
        def _():
            comm.finish(cins, couts, csems)

    hbm = pl.BlockSpec(memory_space=pltpu.HBM)
    outs = pl.pallas_call(
        carried, name=name, grid=grid, in_specs=list(in_specs) + [hbm] * ci, out_specs=list(out_specs) + [hbm] * co,
        out_shape=list(out_shape) + comm.out_shapes, scratch_shapes=list(scratch_shapes) + comm.scratch,
        compiler_params=params)(*args, *comm.inputs)
    return list(outs[:n_out]), list(outs[n_out:])


def _gather_comm(arrays, space, block_rows):
    n_arr = len(arrays)

    def plan(x_refs, out_refs, sems):
        send_sems, recv_sems, local_sems = sems
        x, y, c = _mesh_pos()
        me, sibling = (x, y, c), (x, y, 1 - c)
        chips = _other_chips(x, y)

        def slot(a, px, py, pc):
            return out_refs[a].at[4 * px + 2 * py + pc]

        def copy(a, k, block, to, src=None):
            return pltpu.make_async_remote_copy(
                src_ref=slot(a, *block) if src is None else src, dst_ref=slot(a, *block),
                send_sem=send_sems.at[7 * a + k], recv_sem=recv_sems.at[7 * a + k], device_id=to, device_id_type=MESH)

        srcs = [x_refs[a] if block_rows[a] is None else
                x_refs[a].at[pl.ds(pl.multiple_of(c * block_rows[a], 16), block_rows[a]), :] for a in range(n_arr)]
        local = [pltpu.make_async_copy(srcs[a], slot(a, *me), local_sems.at[a]) for a in range(n_arr)]
        first = []
        for a in range(n_arr):
            first += [copy(a, 1 + j, me, (*chip, c), src=srcs[a]) for j, chip in enumerate(chips)]
            first.append(copy(a, 0, me, sibling, src=srcs[a]))
        return me, sibling, chips, c, copy, local, first

    def start(x_refs, out_refs, sems):
        _, _, _, _, _, local, first = plan(x_refs, out_refs, sems)
        for cp in local + first:
            cp.start()

    def finish(x_refs, out_refs, sems):
        me, sibling, chips, c, copy, local, first = plan(x_refs, out_refs, sems)
        passed = []
        for j, chip in enumerate(chips):
            for a in range(n_arr):
                copy(a, 1 + j, (*chip, c), me).wait_recv()
                fwd = copy(a, 4 + j, (*chip, c), sibling)
                fwd.start()
                passed.append(fwd)
        for a in range(n_arr):
            copy(a, 0, sibling, me).wait_recv()
            for j, chip in enumerate(chips):
                copy(a, 4 + j, (*chip, 1 - c), me).wait_recv()
        for cp in first + passed:
            cp.wait_send()
        for cp in local:
            cp.wait()

    out_shapes = [_sds((8, w.shape[0] if r is None else r) + w.shape[1:], w.dtype) for w, r in zip(arrays, block_rows)]
    scratch = [pltpu.SemaphoreType.DMA((7 * n_arr,)), pltpu.SemaphoreType.DMA((7 * n_arr,)), pltpu.SemaphoreType.DMA((n_arr,))]
    return _Comm(arrays, out_shapes, scratch, start, finish, space)


def _weights_gather_comm(shards):
    return _gather_comm(shards, pltpu.HBM, [w.shape[0] // 2 for w in shards])


def _all_shards(gathered):
    return [o.reshape(NSH, 2 * o.shape[1], o.shape[2]) for o in gathered]


def _gather_small(block, name):
    return _comm_call(_gather_comm([block], pltpu.VMEM, [None]), name)[0]


def _exchange_comm(gs):
    n = len(gs)
    halves = [g.shape[1] // 2 for g in gs]

    def plan(g_refs, land_refs, sems):
        send_sems, recv_sems = sems
        x, y, c = _mesh_pos()
        copies = []
        for a in range(n):
            h = halves[a]
            for s in range(NSH):
                copies.append(pltpu.make_async_remote_copy(
                    src_ref=g_refs[a].at[s, pl.ds(pl.multiple_of((1 - c) * h, 8), h), :], dst_ref=land_refs[a].at[s],
                    send_sem=send_sems.at[NSH * a + s], recv_sem=recv_sems.at[NSH * a + s],
                    device_id=(x, y, 1 - c), device_id_type=MESH))
        return copies

    def start(g_refs, land_refs, sems):
        for cp in plan(g_refs, land_refs, sems):
            cp.start()

    def finish(g_refs, land_refs, sems):
        for cp in plan(g_refs, land_refs, sems):
            cp.wait()

    scratch = [pltpu.SemaphoreType.DMA((NSH * n,)), pltpu.SemaphoreType.DMA((NSH * n,))]
    return _Comm(gs, [_sds((NSH, h, g.shape[2])) for h, g in zip(halves, gs)], scratch, start, finish)


def _chip_sum(g, land, c_arr, name):
    _, h, cols = land.shape

    def body(c_ref, g_ref, l_ref, o_ref):
        o_ref[...] = (g_ref[...] + l_ref[...]).astype(BF16)

    return pl.pallas_call(
        body, name=name, out_shape=_sds((NSH, h, cols), BF16),
        grid_spec=pltpu.PrefetchScalarGridSpec(
            num_scalar_prefetch=1, grid=(NSH,),
            in_specs=[pl.BlockSpec((1, h, cols), lambda s, c_ref: (s, c_ref[0], 0)),
                      pl.BlockSpec((1, h, cols), lambda s, c_ref: (s, 0, 0))],
            out_specs=pl.BlockSpec((1, h, cols), lambda s, c_ref: (s, 0, 0))),
        compiler_params=_cparams(1),
    )(c_arr, g, land)


def _scatter_comm(parts):
    n = len(parts)

    def plan(p_refs, land_refs, sems):
        send_sems, recv_sems, local_sems = sems
        x, y, c = _mesh_pos()
        my_chip = 2 * x + y
        local = [pltpu.make_async_copy(p_refs[a].at[my_chip], land_refs[a].at[my_chip], local_sems.at[a]) for a in range(n)]
        copies = []
        for a in range(n):
            for j, (px, py) in enumerate(_other_chips(x, y)):
                copies.append(pltpu.make_async_remote_copy(
                    src_ref=p_refs[a].at[2 * px + py], dst_ref=land_refs[a].at[my_chip],
                    send_sem=send_sems.at[3 * a + j], recv_sem=recv_sems.at[3 * a + j],
                    device_id=(px, py, c), device_id_type=MESH))
        return local, copies

    def start(p_refs, land_refs, sems):
        local, copies = plan(p_refs, land_refs, sems)
        for cp in local + copies:
            cp.start()

    def finish(p_refs, land_refs, sems):
        local, copies = plan(p_refs, land_refs, sems)
        for cp in copies:
            cp.wait()
        for cp in local:
            cp.wait()

    scratch = [pltpu.SemaphoreType.DMA((3 * n,)), pltpu.SemaphoreType.DMA((3 * n,)), pltpu.SemaphoreType.DMA((n,))]
    return _Comm(parts, [_sds(p.shape, BF16) for p in parts], scratch, start, finish)


def _sum_slots(land, name):
    k, r, c = land.shape
    tr = r // 2 if r % 32 == 0 else r

    def body(l_ref, o_ref):
        acc = l_ref[0].astype(F32)
        for i in range(1, k):
            acc = acc + l_ref[i].astype(F32)
        o_ref[...] = acc

    return pl.pallas_call(
        body, name=name, grid=(r // tr,), in_specs=[pl.BlockSpec((k, tr, c), lambda i: (0, i, 0))],
        out_specs=_rows(tr, c), out_shape=_sds((r, c)), compiler_params=_cparams(1),
    )(land)


def _sibling_swap(halves):
    n = len(halves)

    def body(*refs):
        h_refs, out_refs = refs[:n], refs[n:2 * n]
        send_sems, recv_sems = refs[2 * n:]
        x, y, c = _mesh_pos()
        copies = [pltpu.make_async_remote_copy(
            src_ref=h_refs[a], dst_ref=out_refs[a], send_sem=send_sems.at[a], recv_sem=recv_sems.at[a],
            device_id=(x, y, 1 - c), device_id_type=MESH) for a in range(n)]
        for cp in copies:
            cp.start()
        for cp in copies:
            cp.wait()

    return pl.pallas_call(
        body, name="grad_sibling_swap", out_shape=[_sds(h.shape) for h in halves],
        in_specs=[pl.BlockSpec(memory_space=pltpu.HBM)] * n, out_specs=[pl.BlockSpec(memory_space=pltpu.HBM)] * n,
        scratch_shapes=[pltpu.SemaphoreType.DMA((n,)), pltpu.SemaphoreType.DMA((n,))],
    )(*halves)


def _pad_rows(v, width):
    flat = v.reshape(-1)
    rows = -(-flat.shape[0] // width)
    rows = -(-rows // 8) * 8
    return jnp.pad(flat, (0, rows * width - flat.shape[0])).reshape(rows, width)


def _size(shape):
    n = 1
    for dim in shape:
        n *= dim
    return n


def _row_pack(arrs):
    pieces = []
    for a in arrs:
        rows = -(-a.size // D)
        pieces.append(jnp.pad(a.reshape(-1), (0, rows * D - a.size)).reshape(rows, D))
    total = sum(p.shape[0] for p in pieces)
    if total % 8:
        pieces.append(jnp.zeros((8 - total % 8, D), F32))
    return jnp.concatenate(pieces, axis=0)


def _row_unpack(packed, shapes):
    out, r0 = [], 0
    for shp in shapes:
        n = _size(shp)
        rows = -(-n // D)
        out.append(packed[r0:r0 + rows].reshape(-1)[:n].reshape(shp))
        r0 += rows
    return out


def _block_diag(w):
    eye = jnp.eye(8, dtype=w.dtype)
    return (w[:, :, None, :] * eye[:, None, :, None]).reshape(RGW, RGW)


def _diag_blocks(dense):
    r = dense.reshape(8, 64, 8, 64)
    return jnp.stack([r[n, :, n, :] for n in range(8)])


def _lane_row(v8):
    return jnp.zeros((1, BAP), F32).at[0, 8:16].set(v8.reshape(8))


def _chip_sums(gs, lands, names, c_arr):
    return [_chip_sum(g, l, c_arr, "chip_sum_" + n) for g, l, n in zip(gs, lands, names)]


def _reduce_parts(gs, names, c_arr, tag):
    return _chip_sums(gs, _comm_call(_exchange_comm(gs), "grad_sibling_exchange_" + tag), names, c_arr)


def _local_step(x, target, sw, ffn1_w, later_shards, c_arr):
    (g1, gmix, rg_cw8, rg_cb, wgates, gbias, lam_row, gdn_cw8, alog_row, dtb_row, gn, g2, gfin) = sw
    wg1, wu1, wd1 = ffn1_w

    (x1, a1, b1, fb1), gathered = _ffn_fwd(x, g1, wg1, wu1, wd1, "ffn1_fwd", comm=_weights_gather_comm(later_shards))
    win_sh, wout_sh, wg2, wu2, wd2 = _all_shards(gathered)
    w_in_full = jnp.transpose(win_sh, (1, 0, 2)).reshape(D, NSH * INSH)
    wout = wout_sh.reshape(D, D)
    w_in_groups = (w_in_full[:, 0:512], w_in_full[:, 512:1024], w_in_full[:, 1024:2560], w_in_full[:, 2560:3072],
                   jnp.pad(w_in_full[:, 3072:3088], ((0, 0), (0, BAP - BAW))))
    h2, p_rgx, p_gate, p_qkv, p_z, p_ba = _inproj(x1, gmix, w_in_groups, "in_proj")
    c_rg = _conv(p_rgx, rg_cw8, rg_cb, "rg_conv")
    c_qkv = _conv(p_qkv, gdn_cw8, jnp.zeros((1, QKVW), F32), "gdn_conv")
    a0, bb0, a1s, bb1, q, k, v, bg = _mix_prep(c_rg, c_qkv, p_ba, wgates, gbias, lam_row, alog_row, dtb_row, "mix_prep")
    tmat, gu, gw, gqd, gkd, gat, gcd = _gdn_local_fwd(q, k, v, bg, "gdn_local_fwd")
    of, s0, vn0, ob, s1, vn1, hf, hb = _gdn_seq_fwd(gu, gw, gqd, gkd, gat, gcd, "gdn_seq_fwd", scan=(a0, bb0, a1s, bb1))
    x2, ymix = _outproj(x1, hf, hb, p_gate, of, ob, p_z, gn, wout, "out_proj")
    (x3, a2, b2, fb2), _ = _ffn_fwd(x2, g2, wg2, wu2, wd2, "ffn2_fwd")
    dx3, dob2, loss_blk, d_gfin = _loss_head(x3, target, gfin, "loss_head")

    dx2, d_g2, hb2, dab2, dbb2, _ = _ffn_bwd(x2, dx3, dob2, g2, a2, b2, wg2, wu2, wd2, "ffn2_bwd")
    d_ffn2 = [_tn(dab2, hb2, "ffn2_dwg"), _tn(dbb2, hb2, "ffn2_dwu"), _tn(fb2, dob2, "ffn2_dwd")]

    (d_hr, d_gate, d_os, d_z, d_gn, dx2b), lands = _outproj_bwd(dx2, hf, hb, p_gate, of, ob, p_z, gn, wout, "out_proj_bwd",
                                                               comm=_exchange_comm(d_ffn2))
    parts_ffn2 = _chip_sums(d_ffn2, lands, _BIG_NAMES[5:8], c_arr)
    d_wout = _tn(ymix, dx2b, "dw_out")[0]

    sg = _gdn_seq_bwd(d_os, gw, gqd, gkd, gat, gcd, (s0, s1), (vn0, vn1), "gdn_seq_bwd", scan=(a1s, d_hr, a0, d_hr))
    lam1, lam0 = sg[10:12]
    d_xc, d_pre, xcb, d_gbias, d_lam = _gates_bwd(c_rg, wgates, gbias, lam_row, lam0, lam1, hf, hb, "rg_gates_bwd")
    d_wgates = _tn(xcb, d_pre, "dw_gates")[0]
    d_prgx, d_rgcw8, d_rgcb = _conv_bwd(p_rgx, d_xc, rg_cw8, "rg_conv_bwd")

    (dq, dk, dv, dbg), lands_ffn2 = _gdn_local_bwd(q, k, v, bg, tmat, d_os, (vn0, vn1), (sg[0:5], sg[5:10]), "gdn_local_bwd",
                                                  comm=_scatter_comm(parts_ffn2))
    d_cqkv, d_pba, d_alog, d_dtb = _prep_bwd(c_qkv, p_ba, alog_row, dtb_row, dq, dk, dv, dbg, "gdn_prep_bwd")
    d_pqkv, d_gdncw8, _ = _conv_bwd(p_qkv, d_cqkv, gdn_cw8, "gdn_conv_bwd")

    dps = (d_prgx, d_gate, d_pqkv, d_z, d_pba)
    dx1, dob1, d_gmix = _inproj_bwd(x1, dx2, gmix, dps, w_in_groups, "in_proj_bwd")
    d_win_groups = [_tn(h2, dp, "dw_in_%d" % i)[0] for i, dp in enumerate(dps)]
    d_win = jnp.concatenate(d_win_groups[:4] + [d_win_groups[4][:, :BAW]], axis=1)
    d_mix = [jnp.transpose(d_win.reshape(D, NSH, INSH), (1, 0, 2)), d_wout.reshape(NSH, OUTSH, D)]

    small = dict(
        mix_norm=d_gmix, rg_conv_w=d_rgcw8[:4], rg_conv_b=d_rgcb,
        rg_gate_a_w=jnp.stack([_diag_blocks(d_wgates[:, RGW * i:RGW * (i + 1)]) for i in (0, 1)]),
        rg_gate_x_w=jnp.stack([_diag_blocks(d_wgates[:, RGW * i:RGW * (i + 1)]) for i in (2, 3)]),
        rg_gate_a_b=d_gbias[0, :2 * RGW].reshape(2, RGW), rg_gate_x_b=d_gbias[0, 2 * RGW:].reshape(2, RGW),
        rg_lambda=d_lam.reshape(2, RGW), gdn_conv_w=d_gdncw8[:4],
        gdn_a_log=d_alog[0, 8:16].reshape(2, NH), gdn_dt_bias=d_dtb[0, 8:16].reshape(2, NH),
        gdn_norm=d_gn, ffn2_norm=d_g2, final_norm=d_gfin)
    small_pack = _row_pack([small[n] for n in _SMALL_NAMES[1:]])

    riders = _join_comm(_exchange_comm(d_mix), _gather_comm([small_pack], pltpu.HBM, [None]))
    gx, d_g1, hb1, dab1, dbb1, carried = _ffn_bwd(x, dx1, dob1, g1, a1, b1, wg1, wu1, wd1, "ffn1_bwd", comm=riders)
    parts_mix = _chip_sums(d_mix, carried[0:2], _BIG_NAMES[3:5], c_arr)
    d_wg1, lands_mix = _tn(dab1, hb1, "ffn1_dwg", comm=_scatter_comm(parts_mix))
    parts_wg1 = _reduce_parts([d_wg1], _BIG_NAMES[0:1], c_arr, "ffn1_gate")
    d_wu1, lands_wg1 = _tn(dbb1, hb1, "ffn1_dwu", comm=_scatter_comm(parts_wg1))
    parts_wu1 = _reduce_parts([d_wu1], _BIG_NAMES[1:2], c_arr, "ffn1_up")
    d_wd1, lands_wu1 = _tn(fb1, dob1, "ffn1_dwd", comm=_scatter_comm(parts_wu1))
    parts_wd1 = _reduce_parts([d_wd1], _BIG_NAMES[2:3], c_arr, "ffn1_down")
    lands_ffn1 = lands_wg1 + lands_wu1 + _comm_call(_scatter_comm(parts_wd1), "grad_chip_scatter_ffn1_down")

    halves = [_sum_slots(l, "sum_chips_" + n) for l, n in zip(lands_ffn1 + lands_mix + lands_ffn2, _BIG_NAMES)]
    small_shapes = [small[n].shape for n in _SMALL_NAMES[1:]]
    return loss_blk, gx, halves, d_g1, carried[2], small_shapes


_SMALL_NAMES = ("ffn1_norm", "mix_norm", "rg_conv_w", "rg_conv_b", "rg_gate_a_w", "rg_gate_a_b", "rg_gate_x_w",
                "rg_gate_x_b", "rg_lambda", "gdn_conv_w", "gdn_a_log", "gdn_dt_bias", "gdn_norm", "ffn2_norm", "final_norm")
_SMALL_SHARDED = dict(rg_conv_w=128, rg_gate_a_b=128, rg_gate_x_b=128, rg_lambda=128, gdn_conv_w=384)
_OUT_ORDER = ("ffn1_norm", "ffn1_w_gate", "ffn1_w_up", "ffn1_w_down", "mix_norm", "w_in", "w_out", "rg_conv_w", "rg_conv_b",
              "rg_gate_a_w", "rg_gate_a_b", "rg_gate_x_w", "rg_gate_x_b", "rg_lambda", "gdn_conv_w", "gdn_a_log",
              "gdn_dt_bias", "gdn_norm", "ffn2_norm", "ffn2_w_gate", "ffn2_w_up", "ffn2_w_down", "final_norm")
_BIG_NAMES = ("ffn1_w_gate", "ffn1_w_up", "ffn1_w_down", "w_in", "w_out", "ffn2_w_gate", "ffn2_w_up", "ffn2_w_down")
_TRANSPOSED = ("ffn1_w_gate", "ffn1_w_up", "ffn2_w_gate", "ffn2_w_up")


def kernel(x, ffn1_norm, ffn1_w_gate, ffn1_w_up, ffn1_w_down, mix_norm, w_in, w_out, rg_conv_w, rg_conv_b, rg_gate_a_w, rg_gate_a_b, rg_gate_x_w, rg_gate_x_b, rg_lambda, gdn_conv_w, gdn_a_log, gdn_dt_bias, gdn_norm, ffn2_norm, ffn2_w_gate, ffn2_w_up, ffn2_w_down, final_norm, loss_target, m_ffn1_norm, m_ffn1_w_gate, m_ffn1_w_up, m_ffn1_w_down, m_mix_norm, m_w_in, m_w_out, m_rg_conv_w, m_rg_conv_b, m_rg_gate_a_w, m_rg_gate_a_b, m_rg_gate_x_w, m_rg_gate_x_b, m_rg_lambda, m_gdn_conv_w, m_gdn_a_log, m_gdn_dt_bias, m_gdn_norm, m_ffn2_norm, m_ffn2_w_gate, m_ffn2_w_up, m_ffn2_w_down, m_final_norm, v_ffn1_norm, v_ffn1_w_gate, v_ffn1_w_up, v_ffn1_w_down, v_mix_norm, v_w_in, v_w_out, v_rg_conv_w, v_rg_conv_b, v_rg_gate_a_w, v_rg_gate_a_b, v_rg_gate_x_w, v_rg_gate_x_b, v_rg_lambda, v_gdn_conv_w, v_gdn_a_log, v_gdn_dt_bias, v_gdn_norm, v_ffn2_norm, v_ffn2_w_gate, v_ffn2_w_up, v_ffn2_w_down, v_final_norm):
    args = dict(locals())
    w = {n: args[n] for n in _OUT_ORDER}
    mom = {n: args["m_" + n] for n in _OUT_ORDER}
    var = {n: args["v_" + n] for n in _OUT_ORDER}
    xi, yi, ci = _mesh_pos()
    shard = 2 * xi + yi

    big_bf16 = [w[n][0].astype(BF16) for n in _BIG_NAMES]
    sm_local = _pad_rows(jnp.concatenate([w[n][0].reshape(-1) for n in _SMALL_SHARDED]), 128)
    first = _comm_call(_gather_comm(big_bf16[0:3] + [sm_local], pltpu.HBM, [t.shape[0] // 2 for t in big_bf16[0:3]] + [None]),
                       "gather_first_weights")
    ffn1_w = _all_shards(first[0:3])
    sm_all = first[3][0::2].reshape(NSH, -1)
    sm_full, off = {}, 0
    for n, wd_ in _SMALL_SHARDED.items():
        rows = w[n].shape[1]
        piece = sm_all[:, off:off + rows * wd_].reshape(NSH, rows, wd_)
        sm_full[n] = jnp.transpose(piece, (1, 0, 2)).reshape(rows, NSH * wd_)
        off += rows * wd_

    wa, wx = rg_gate_a_w[0], rg_gate_x_w[0]
    wgates = jnp.concatenate([_block_diag(wa[0]), _block_diag(wa[1]), _block_diag(wx[0]), _block_diag(wx[1])],
                             axis=1).astype(BF16)
    gbias = jnp.concatenate([sm_full["rg_gate_a_b"].reshape(1, -1), sm_full["rg_gate_x_b"].reshape(1, -1)], axis=1)
    sw = (ffn1_norm, mix_norm, jnp.pad(sm_full["rg_conv_w"], ((0, 4), (0, 0))), rg_conv_b, wgates, gbias,
          sm_full["rg_lambda"].reshape(1, -1), jnp.pad(sm_full["gdn_conv_w"], ((0, 4), (0, 0))), _lane_row(gdn_a_log),
          _lane_row(gdn_dt_bias), gdn_norm, ffn2_norm, final_norm.reshape(1, D))
    c_arr = ci.reshape(1).astype(jnp.int32)

    loss_blk, gx, halves, d_g1, small_packs, small_shapes = _local_step(x[0], loss_target[0], sw, ffn1_w, big_bf16[3:], c_arr)
    loss = lax.psum(loss_blk[0, 0], ("x", "y", "c"))
    grads = {}

    g1_all = _gather_small(jnp.pad(d_g1, ((0, 7), (0, 0))), "gather_ffn1_norm_grad")
    sm_sums = [_sum_slots(g1_all, "ffn1_norm_grad_sum")[0:1]] + _row_unpack(_sum_slots(small_packs, "small_grad_sum"), small_shapes)
    for n, g in zip(_SMALL_NAMES, sm_sums):
        if n in _SMALL_SHARDED:
            wd_ = _SMALL_SHARDED[n]
            g = lax.dynamic_slice_in_dim(g, shard * wd_, wd_, axis=1)
        grads[n] = g.reshape(w[n].shape)

    delta, new_m, new_v = {}, {}, {}
    for n, own, recv in zip(_BIG_NAMES, halves, _sibling_swap(halves)):
        to2d = jnp.transpose if n in _TRANSPOSED else (lambda t: t)
        outs4 = _adamw_halves(to2d(w[n][0]), own, recv, to2d(mom[n][0]), to2d(var[n][0]), c_arr, "adamw_" + n)
        grads[n], delta[n], new_m[n], new_v[n] = [to2d(o)[None] for o in outs4]
    packs = [_row_pack([t[n] for n in _SMALL_NAMES]) for t in (w, grads, mom, var)]
    sm_shapes = [w[n].shape for n in _SMALL_NAMES]
    for dst, src in zip((delta, new_m, new_v), _adamw(*packs, "adamw_small")):
        for n, val in zip(_SMALL_NAMES, _row_unpack(src, sm_shapes)):
            dst[n] = val

    outs = [loss, gx[None]]
    for group in (grads, delta, new_m, new_v):
        outs += [group[n] for n in _OUT_ORDER]
    return tuple(outs)
```

```python
import functools

import jax
import jax.numpy as jnp
from jax import lax
from jax.experimental import pallas as pl
from jax.experimental.pallas import tpu as pltpu

F32 = jnp.float32
BF16 = jnp.bfloat16
EPS = 1e-6
D = 1024
NSH = 4
FSH = 704
RGW = 512
QKVW = 1536
ZW = 512
BAW = 16
BAP = 128
INSH = 772
OUTSH = 256
CHUNK = 64
NH = 4
DH = 128
RG_C = 8.0
VMEM_LIMIT = 52 * 1024 * 1024
MESH = pl.DeviceIdType.MESH

ADAM_LR = 0.001
ADAM_B1 = 0.9
ADAM_B2 = 0.999
ADAM_EPS = 1e-08
ADAM_WD = 0.01
ADAM_STEP = 10


def _cparams(n_grid):
    return pltpu.CompilerParams(dimension_semantics=("arbitrary",) * n_grid, vmem_limit_bytes=VMEM_LIMIT)


def _sig(x):
    return 0.5 + 0.5 * jnp.tanh(0.5 * x)


def _sig_pos(x):
    return 1.0 / (1.0 + jnp.exp(-x))


def _softplus(x):
    return jnp.maximum(x, 0.0) + jnp.log(1.0 + jnp.exp(-jnp.abs(x)))


def _neg_expm1(y):
    series = -y * (1.0 + y * (0.5 + y * (1.0 / 6 + y * (1.0 / 24 + y * (1.0 / 120 + y * (1.0 / 720 + y / 5040))))))
    return jnp.where(y > -0.3, series, 1.0 - jnp.exp(y))


_GELU_C = 0.7978845608028654


def _gelu(x):
    t = jnp.tanh(_GELU_C * (x + 0.044715 * x * x * x))
    return 0.5 * x * (1.0 + t)


def _gelu_grad(x):
    t = jnp.tanh(_GELU_C * (x + 0.044715 * x * x * x))
    return 0.5 * (1.0 + t) + 0.5 * x * (1.0 - t * t) * _GELU_C * (1.0 + 3 * 0.044715 * x * x)


def _silu_grad(x):
    s = _sig(x)
    return s * (1.0 + x * (1.0 - s))


def _dot(a, b):
    return jnp.dot(a.astype(BF16), b.astype(BF16), preferred_element_type=F32)


def _dot_nt(a, b):
    return lax.dot_general(a.astype(BF16), b.astype(BF16), (((1,), (1,)), ((), ())), preferred_element_type=F32)


def _dot_tn(a, b):
    return lax.dot_general(a.astype(BF16), b.astype(BF16), (((0,), (0,)), ((), ())), preferred_element_type=F32)


_NN = ((1,), (0,))
_NT = ((1,), (1,))
_TN = ((0,), (0,))


def _dg(a, b, dims):
    return lax.dot_general(a, b, (dims, ((), ())), preferred_element_type=F32)


def _split2(a):
    hi = a.astype(BF16)
    return hi, (a - hi.astype(F32)).astype(BF16)


def _dot3(a, b, dims=_NN):
    ah, al = _split2(a)
    bh, bl = _split2(b)
    return _dg(ah, bh, dims) + _dg(ah, bl, dims) + _dg(al, bh, dims)


def _dot_exact(e, x, dims, e_is_lhs):
    x0 = x.astype(BF16)
    r = x - x0.astype(F32)
    x1 = r.astype(BF16)
    x2 = (r - x1.astype(F32)).astype(BF16)
    eb = e.astype(BF16)
    if e_is_lhs:
        return _dg(eb, x0, dims) + _dg(eb, x1, dims) + _dg(eb, x2, dims)
    return _dg(x0, eb, dims) + _dg(x1, eb, dims) + _dg(x2, eb, dims)


def _rms(xv):
    r = lax.rsqrt(jnp.mean(xv * xv, axis=-1, keepdims=True) + EPS)
    return r, xv * r


def _rms_bwd(dy, xh, r, gain):
    dxh = dy * gain
    return r * (dxh - xh * jnp.mean(dxh * xh, axis=-1, keepdims=True))


def _colsum(v):
    return jnp.sum(v, axis=0, keepdims=True)


def _rows(t, c):
    return pl.BlockSpec((t, c), lambda i: (i, 0))


def _full(shape):
    n = len(shape)
    return pl.BlockSpec(shape, lambda i: (0,) * n)


def _sds(shape, dtype=F32):
    return jax.ShapeDtypeStruct(shape, dtype)


def _ffn_fwd(x, gain, wg, wu, wd, name, comm=None):
    s = x.shape[0]
    tm = min(256, s)

    def body(x_ref, g_ref, wg_ref, wu_ref, wd_ref, xo_ref, ga_ref, gb_ref, f_ref):
        xv = x_ref[...]
        _, xh = _rms(xv)
        h = (xh * g_ref[...]).astype(BF16)
        acc = None
        for j in range(NSH):
            a = jnp.dot(h, wg_ref[j], preferred_element_type=F32)
            b = jnp.dot(h, wu_ref[j], preferred_element_type=F32)
            sa = _sig(a)
            silu = a * sa
            fv = silu * b
            f = fv.astype(BF16)
            f_ref[j] = f
            ga_ref[j] = (sa * b + fv * (1.0 - sa)).astype(BF16)
            gb_ref[j] = silu.astype(BF16)
            part = jnp.dot(f, wd_ref[j], preferred_element_type=F32)
            acc = part if acc is None else acc + part
        xo_ref[...] = xv + 0.5 * acc

    hidden = pl.BlockSpec((NSH, tm, FSH), lambda i: (0, i, 0))
    return _pallas(
        body, comm, name=name, grid=(s // tm,),
        in_specs=[_rows(tm, D), _full((1, D)),
                  pl.BlockSpec((NSH, D, FSH), lambda i: (0, 0, 0), pipeline_mode=pl.Buffered(1)),
                  pl.BlockSpec((NSH, D, FSH), lambda i: (0, 0, 0), pipeline_mode=pl.Buffered(1)),
                  pl.BlockSpec((NSH, FSH, D), lambda i: (0, 0, 0), pipeline_mode=pl.Buffered(1))],
        out_specs=[_rows(tm, D), hidden, hidden, hidden],
        out_shape=[_sds((s, D))] + [_sds((NSH, s, FSH), BF16)] * 3,
        scratch_shapes=[], args=(x, gain, wg, wu, wd))


def _ffn_bwd(x, dout, do, gain, ga, gb, wg, wu, wd, name, comm=None):
    s = x.shape[0]
    tm = min(512, s)

    def hidden(do_ref, ga_ref, gb_ref, wd_ref, da_ref, db_ref):
        df = _dot_nt(do_ref[...], wd_ref[0])
        da_ref[0] = (df * ga_ref[0].astype(F32)).astype(BF16)
        db_ref[0] = (df * gb_ref[0].astype(F32)).astype(BF16)

    th = min(1024, s)
    tok = pl.BlockSpec((th, D), lambda i, j: (i, 0))
    sh = pl.BlockSpec((1, th, FSH), lambda i, j: (j, i, 0))
    (da, db), carried = _pallas(
        hidden, comm, name=name + "_hidden", grid=(s // th, NSH),
        in_specs=[tok, sh, sh, pl.BlockSpec((1, FSH, D), lambda i, j: (j, 0, 0))], out_specs=[sh, sh],
        out_shape=[_sds((NSH, s, FSH), BF16)] * 2, scratch_shapes=[], args=(do, ga, gb, wd))

    def inputs(x_ref, d_ref, g_ref, da_ref, db_ref, wg_ref, wu_ref, dx_ref, dg_ref, h_ref):
        @pl.when(pl.program_id(0) == 0)
        def _():
            dg_ref[...] = jnp.zeros_like(dg_ref)

        dh = jnp.zeros((tm, D), F32)
        for j in range(NSH):
            dh = dh + _dot_nt(da_ref[j], wg_ref[j]) + _dot_nt(db_ref[j], wu_ref[j])
        r, xh = _rms(x_ref[...])
        gv = g_ref[...]
        h_ref[...] = (xh * gv).astype(BF16)
        dg_ref[...] += _colsum(dh * xh)
        dx_ref[...] = d_ref[...] + _rms_bwd(dh, xh, r, gv)

    grads = pl.BlockSpec((NSH, tm, FSH), lambda i: (0, i, 0))
    resident = pl.BlockSpec((NSH, D, FSH), lambda i: (0, 0, 0), pipeline_mode=pl.Buffered(1))
    dx, dg, h = pl.pallas_call(
        inputs, name=name + "_input", grid=(s // tm,),
        in_specs=[_rows(tm, D), _rows(tm, D), _full((1, D)), grads, grads, resident, resident],
        out_specs=[_rows(tm, D), _full((1, D)), _rows(tm, D)],
        out_shape=[_sds((s, D)), _sds((1, D)), _sds((s, D), BF16)], compiler_params=_cparams(1),
    )(x, dout, gain, da, db, wg, wu)
    return dx, dg, h, da, db, carried


def _tn(a, b, name, comm=None):
    a_g = a.ndim == 3
    b_g = b.ndim == 3
    g = a.shape[0] if a_g else (b.shape[0] if b_g else 1)
    s, k = a.shape[-2:]
    n = b.shape[-1]
    ts = min(2048 if b.dtype == BF16 else 1024, s)

    def body(a_ref, b_ref, o_ref):
        @pl.when(pl.program_id(1) == 0)
        def _():
            o_ref[...] = jnp.zeros_like(o_ref)

        av = a_ref[0] if a_g else a_ref[...]
        bv = b_ref[0] if b_g else b_ref[...]
        o_ref[0] += _dot_tn(av, bv)

    a_spec = pl.BlockSpec((1, ts, k), lambda gi, si: (gi, si, 0)) if a_g else pl.BlockSpec((ts, k), lambda gi, si: (si, 0))
    b_spec = pl.BlockSpec((1, ts, n), lambda gi, si: (gi, si, 0)) if b_g else pl.BlockSpec((ts, n), lambda gi, si: (si, 0))
    outs, carried = _pallas(body, comm, name=name, grid=(g, s // ts), in_specs=[a_spec, b_spec],
                            out_specs=[pl.BlockSpec((1, k, n), lambda gi, si: (gi, 0, 0))], out_shape=[_sds((g, k, n))],
                            scratch_shapes=[], args=(a, b))
    return outs[0] if comm is None else (outs[0], carried)


_P_WIDTHS = (RGW, RGW, QKVW, ZW, BAP)


def _inproj(x1, gain, ws, name):
    s = x1.shape[0]
    tm = min(256, s)

    def body(x_ref, g_ref, *refs):
        w_refs = refs[:5]
        h_ref = refs[5]
        p_refs = refs[6:]
        _, xh = _rms(x_ref[...])
        h = (xh * g_ref[...]).astype(BF16)
        h_ref[...] = h
        for w_ref, p_ref in zip(w_refs, p_refs):
            p_ref[...] = jnp.dot(h, w_ref[...], preferred_element_type=F32)

    return pl.pallas_call(
        body, name=name, grid=(s // tm,),
        in_specs=[_rows(tm, D), _full((1, D))] + [_full((D, w)) for w in _P_WIDTHS],
        out_specs=[_rows(tm, D)] + [_rows(tm, w) for w in _P_WIDTHS],
        out_shape=[_sds((s, D), BF16)] + [_sds((s, w)) for w in _P_WIDTHS],
        compiler_params=_cparams(1),
    )(x1, gain, *ws)


def _inproj_bwd(x1, dx2, gain, dps, ws, name):
    s = x1.shape[0]
    tm = min(256, s)

    def body(x_ref, d_ref, g_ref, *refs):
        dp_refs = refs[:5]
        w_refs = refs[5:10]
        dx_ref, dxh_ref, dg_ref = refs[10:]

        @pl.when(pl.program_id(0) == 0)
        def _():
            dg_ref[...] = jnp.zeros_like(dg_ref)

        dh = jnp.zeros((tm, D), F32)
        for dp_ref, w_ref in zip(dp_refs, w_refs):
            dh = dh + _dot_nt(dp_ref[...], w_ref[...])
        r, xh = _rms(x_ref[...])
        dg_ref[...] += _colsum(dh * xh)
        dx = d_ref[...] + _rms_bwd(dh, xh, r, g_ref[...])
        dx_ref[...] = dx
        dxh_ref[...] = (0.5 * dx).astype(BF16)

    return pl.pallas_call(
        body, name=name, grid=(s // tm,),
        in_specs=[_rows(tm, D), _rows(tm, D), _full((1, D))] + [_rows(tm, w) for w in _P_WIDTHS]
        + [_full((D, w)) for w in _P_WIDTHS],
        out_specs=[_rows(tm, D), _rows(tm, D), _full((1, D))],
        out_shape=[_sds((s, D)), _sds((s, D), BF16), _sds((1, D))],
        compiler_params=_cparams(1),
    )(x1, dx2, gain, *dps, *ws)


def _halo_specs(s, t, c):
    nb8 = s // 8
    tb = t // 8
    prev = pl.BlockSpec((8, c), lambda i: (jnp.maximum(i * tb - 1, 0), 0))
    nxt = pl.BlockSpec((8, c), lambda i: (jnp.minimum((i + 1) * tb, nb8 - 1), 0))
    return prev, nxt


def _edge_masks(nb):
    i = pl.program_id(0)
    return jnp.where(i > 0, 1.0, 0.0).astype(F32), jnp.where(i < nb - 1, 1.0, 0.0).astype(F32)


def _shifted(xx, off, t):
    n = t + 16
    sh = (-off) % n
    rolled = xx if sh == 0 else pltpu.roll(xx, sh, 0)
    return rolled[8:8 + t]


def _conv(x, w8, bias, name):
    s, c = x.shape
    t = min(256, s)
    nb = s // t

    def body(x_ref, xp_ref, xn_ref, w_ref, b_ref, o_ref):
        pm, nm = _edge_masks(nb)
        for c0 in range(0, c, 512):
            cols = slice(c0, c0 + 512)
            xx = jnp.concatenate([xp_ref[:, cols] * pm, x_ref[:, cols], xn_ref[:, cols] * nm], axis=0)
            acc = jnp.zeros((t, 512), F32) + b_ref[:, cols]
            for j in range(4):
                acc = acc + w_ref[j:j + 1, cols] * _shifted(xx, j - 2, t)
            o_ref[:, cols] = acc

    prev, nxt = _halo_specs(s, t, c)
    return pl.pallas_call(
        body, name=name, grid=(nb,),
        in_specs=[_rows(t, c), prev, nxt, _full((8, c)), _full((1, c))],
        out_specs=_rows(t, c), out_shape=_sds((s, c)), compiler_params=_cparams(1),
    )(x, x, x, w8, bias)


def _conv_bwd(x, dc, w8, name):
    s, c = x.shape
    t = min(256, s)
    nb = s // t

    def body(x_ref, d_ref, dp_ref, dn_ref, w_ref, dx_ref, dw_ref, db_ref):
        @pl.when(pl.program_id(0) == 0)
        def _():
            dw_ref[...] = jnp.zeros_like(dw_ref)
            db_ref[...] = jnp.zeros_like(db_ref)

        pm, nm = _edge_masks(nb)
        for c0 in range(0, c, 512):
            cols = slice(c0, c0 + 512)
            dd = jnp.concatenate([dp_ref[:, cols] * pm, d_ref[:, cols], dn_ref[:, cols] * nm], axis=0)
            xv = x_ref[:, cols]
            acc = jnp.zeros((t, 512), F32)
            for j in range(4):
                dsh = _shifted(dd, 2 - j, t)
                acc = acc + w_ref[j:j + 1, cols] * dsh
                dw_ref[j:j + 1, cols] += _colsum(dsh * xv)
            dx_ref[:, cols] = acc.astype(BF16)
            db_ref[:, cols] += _colsum(d_ref[:, cols])

    prev, nxt = _halo_specs(s, t, c)
    return pl.pallas_call(
        body, name=name, grid=(nb,),
        in_specs=[_rows(t, c), _rows(t, c), prev, nxt, _full((8, c))],
        out_specs=[_rows(t, c), _full((8, c)), _full((1, c))],
        out_shape=[_sds((s, c), BF16), _sds((8, c)), _sds((1, c))], compiler_params=_cparams(1),
    )(x, dc, dc, dc, w8)


def _rg_gates(xc, pre, lam_row):
    sp8 = RG_C * _softplus(-lam_row)
    out = []
    for d in range(2):
        r = _sig_pos(pre[:, RGW * d:RGW * (d + 1)])
        gi = _sig(pre[:, 2 * RGW + RGW * d:2 * RGW + RGW * (d + 1)])
        la = -r * sp8[:, RGW * d:RGW * (d + 1)]
        a = jnp.exp(la)
        mult = jnp.sqrt(_neg_expm1(2.0 * la))
        out.append((r, gi, a, mult))
    return out


def _mix_prep(c_rg, c_qkv, p_ba, wgates, gbias, lam_row, alog_row, dtb_row, name):
    s = c_rg.shape[0]
    t = min(256, s)

    def body(xc_ref, cq_ref, pc_ref, wg_ref, gb_ref, lam_ref, alog_ref, dtb_ref,
             a0_ref, b0_ref, a1_ref, b1_ref, q_ref, k_ref, v_ref, bg_ref):
        xc = xc_ref[...]
        pre = _dot(xc, wg_ref[...]) + gb_ref[...]
        gates = _rg_gates(xc, pre, lam_ref[...])
        for (r, gi, a, mult), a_ref, b_ref in zip(gates, (a0_ref, a1_ref), (b0_ref, b1_ref)):
            a_ref[...] = a
            b_ref[...] = mult * gi * xc
        cq = cq_ref[...]
        sq = cq * _sig(cq)
        for h in range(NH):
            sl = slice(DH * h, DH * (h + 1))
            qh = sq[:, sl]
            q_ref[:, sl] = qh * lax.rsqrt(jnp.sum(qh * qh, axis=-1, keepdims=True) + EPS) * (DH ** -0.5)
            kh = sq[:, RGW + DH * h:RGW + DH * (h + 1)]
            k_ref[:, sl] = kh * lax.rsqrt(jnp.sum(kh * kh, axis=-1, keepdims=True) + EPS)
        v_ref[...] = sq[:, 2 * RGW:]
        pc = pc_ref[...]
        lane = lax.broadcasted_iota(jnp.int32, pc.shape, 1)
        beta = _sig(pc)
        g = -jnp.exp(alog_ref[...]) * _softplus(pc + dtb_ref[...])
        bg_ref[...] = jnp.where(lane < 8, beta, jnp.where(lane < 16, g, 0.0))

    return pl.pallas_call(
        body, name=name, grid=(s // t,),
        in_specs=[_rows(t, RGW), _rows(t, QKVW), _rows(t, BAP), _full((RGW, 4 * RGW)), _full((1, 4 * RGW)),
                  _full((1, 2 * RGW)), _full((1, BAP)), _full((1, BAP))],
        out_specs=[_rows(t, RGW)] * 7 + [_rows(t, BAP)],
        out_shape=[_sds((s, RGW))] * 7 + [_sds((s, BAP))],
        compiler_params=_cparams(1),
    )(c_rg, c_qkv, p_ba, wgates, gbias, lam_row, alog_row, dtb_row)


def _block_scan(av, bv, row, downwards):
    for k in (1, 2, 4):
        sh = (8 - k) if downwards else k
        m = (row < 8 - k) if downwards else (row >= k)
        a_s = pltpu.roll(av, sh, 0)
        b_s = pltpu.roll(bv, sh, 0)
        bv = jnp.where(m, av * b_s + bv, bv)
        av = jnp.where(m, av * a_s, av)
    return av, bv


def _gates_bwd(xc, wgates, gbias, lam_row, lam0, lam1, hf, hb, name):
    s = xc.shape[0]
    t = min(256, s)
    nb = s // t

    def body(xc_ref, wg_ref, gb_ref, lam_ref, l0_ref, l1_ref, hf_ref, hfp_ref, hfn_ref, hb_ref, hbp_ref, hbn_ref,
             dxc_ref, dpre_ref, xcb_ref, dgb_ref, dlam_ref):
        @pl.when(pl.program_id(0) == 0)
        def _():
            dgb_ref[...] = jnp.zeros_like(dgb_ref)
            dlam_ref[...] = jnp.zeros_like(dlam_ref)

        pm, nm = _edge_masks(nb)
        h_prev = _shifted(jnp.concatenate([hfp_ref[...] * pm, hf_ref[...], hfn_ref[...] * nm], axis=0), -1, t)
        h_next = _shifted(jnp.concatenate([hbp_ref[...] * pm, hb_ref[...], hbn_ref[...] * nm], axis=0), 1, t)
        h_shift = (h_prev, h_next)
        xv = xc_ref[...]
        pre = _dot(xv, wg_ref[...]) + gb_ref[...]
        lam_row_v = lam_ref[...]
        sp8 = RG_C * _softplus(-lam_row_v)
        dsp_dlam = -RG_C * _sig(-lam_row_v)
        gates = _rg_gates(xv, pre, lam_row_v)
        dxc = jnp.zeros((t, RGW), F32)
        dpre_r = []
        dpre_i = []
        for d, ((r, gi, a, mult), l_ref, hs) in enumerate(zip(gates, (l0_ref, l1_ref), h_shift)):
            dbb = l_ref[...]
            da = dbb * hs
            cs = slice(RGW * d, RGW * (d + 1))
            dmult = dbb * gi * xv
            dgi = dbb * mult * xv
            dxc = dxc + dbb * mult * gi
            dla = da * a - dmult * a * a / mult
            dr = -dla * sp8[:, cs]
            dlam_ref[:, cs] += _colsum(-dla * r) * dsp_dlam[:, cs]
            dpre_r.append(dr * r * (1.0 - r))
            dpre_i.append(dgi * gi * (1.0 - gi))
        dpre = jnp.concatenate(dpre_r + dpre_i, axis=1)
        dgb_ref[...] += _colsum(dpre)
        dpre_b = dpre.astype(BF16)
        dpre_ref[...] = dpre_b
        xcb_ref[...] = xv.astype(BF16)
        dxc_ref[...] = dxc + _dot_nt(dpre_b, wg_ref[...])

    prev, nxt = _halo_specs(s, t, RGW)
    return pl.pallas_call(
        body, name=name, grid=(s // t,),
        in_specs=[_rows(t, RGW), _full((RGW, 4 * RGW)), _full((1, 4 * RGW)), _full((1, 2 * RGW))] + [_rows(t, RGW)] * 2
        + [_rows(t, RGW), prev, nxt] * 2,
        out_specs=[_rows(t, RGW), _rows(t, 4 * RGW), _rows(t, RGW), _full((1, 4 * RGW)), _full((1, 2 * RGW))],
        out_shape=[_sds((s, RGW)), _sds((s, 4 * RGW), BF16), _sds((s, RGW), BF16), _sds((1, 4 * RGW)), _sds((1, 2 * RGW))],
        compiler_params=_cparams(1),
    )(xc, wgates, gbias, lam_row, lam0, lam1, hf, hf, hf, hb, hb, hb)


class _GdnMasks:
    def __init__(self, d):
        ri = lax.broadcasted_iota(jnp.int32, (CHUNK, CHUNK), 0)
        ci = lax.broadcasted_iota(jnp.int32, (CHUNK, CHUNK), 1)
        self.incl = (ri >= ci) if d == 0 else (ri <= ci)
        self.strict = (ri > ci) if d == 0 else (ri < ci)
        b16 = jnp.right_shift(ri, 4) == jnp.right_shift(ci, 4)
        b32 = jnp.right_shift(ri, 5) == jnp.right_shift(ci, 5)
        self.diag16 = b16
        self.off32 = jnp.logical_and(b32, jnp.logical_not(b16))
        self.off64 = jnp.logical_not(b32)
        self.eye = jnp.where(ri == ci, 1.0, 0.0).astype(F32)
        self.tri = jnp.where(self.incl, 1.0, 0.0).astype(F32)
        self.last = CHUNK - 1 if d == 0 else 0


def _tri_inv(lmat, m):
    return _tri_inv_many([lmat], [m])[0]


def _tri_inv_many(lmats, masks):
    n = len(lmats)
    ns = [jnp.where(masks[i].diag16, lmats[i], 0.0) for i in range(n)]
    ps = [masks[i].eye - ns[i] for i in range(n)]
    qs = [_dot3(ns[i], ns[i]) for i in range(n)]
    for step in range(3):
        ps = [_dot3(ps[i], masks[i].eye + qs[i]) for i in range(n)]
        if step < 2:
            qs = [_dot3(qs[i], qs[i]) for i in range(n)]
    for off in ("off32", "off64"):
        ts = [_dot3(ps[i], jnp.where(getattr(masks[i], off), lmats[i], 0.0)) for i in range(n)]
        ps = [ps[i] - _dot3(ts[i], ps[i]) for i in range(n)]
    return ps


def _chunk_cumsums(m, bgv):
    return _dot_exact(m.tri, bgv, _NN, True), _dot_exact(m.tri, bgv, ((0,), (1,)), False)


class _GdnHead:
    def __init__(self, qh, kh, vh, kk, q0, bg, gcs, gcs_t, d, h, m):
        cb = 4 * d + h
        cg = 8 + 4 * d + h
        self.q, self.k, self.v = qh, kh, vh
        self.beta = bg[:, cb:cb + 1]
        gcol = gcs[:, cg:cg + 1]
        grow = gcs_t[cg:cg + 1, :]
        gl = gcs[m.last:m.last + 1, cg:cg + 1]
        self.decay = jnp.exp(jnp.where(m.incl, gcol - grow, -1e30))
        self.kb = kh * self.beta
        self.vb = vh * self.beta
        self.a0 = kk * self.beta
        self.q0 = q0
        self.lmat = jnp.where(m.strict, self.a0 * self.decay, 0.0)
        self.attn = self.q0 * self.decay
        self.eg = jnp.exp(gcol)
        self.ek = jnp.exp(gl - gcol)
        self.cd = jnp.exp(gl)
        self.kg = self.kb * self.eg
        self.qd = qh * self.eg
        self.kd = kh * self.ek


HW = NH * DH
SEQ_CB = 4
LOCAL_CB = 4


def _head(h):
    return slice(DH * h, DH * (h + 1))


def _gdn_local_fwd(q, k, v, bg, name):
    s = q.shape[0]
    n = s // CHUNK
    cb = min(LOCAL_CB, n)

    def body(q_ref, k_ref, v_ref, bg_ref, t_ref, u_ref, w_ref, qd_ref, kd_ref, at_ref, cd_ref):
        masks = [_GdnMasks(d) for d in range(2)]
        inst = []
        for jj in range(cb):
            rows = slice(CHUNK * jj, CHUNK * (jj + 1))
            bgv = bg_ref[rows, :]
            qs = [q_ref[rows, _head(h)] for h in range(NH)]
            ks = [k_ref[rows, _head(h)] for h in range(NH)]
            kk = [_dot_nt(ks[h], ks[h]) for h in range(NH)]
            q0 = [_dot_nt(qs[h], ks[h]) for h in range(NH)]
            for d, m in enumerate(masks):
                gcs, gcs_t = _chunk_cumsums(m, bgv)
                for h in range(NH):
                    c = _GdnHead(qs[h], ks[h], v_ref[rows, _head(h)], kk[h], q0[h], bgv, gcs, gcs_t, d, h, m)
                    inst.append((jj, rows, d, h, m, c))
        tms = _tri_inv_many([it[-1].lmat for it in inst], [it[-2] for it in inst])
        for (jj, rows, d, h, m, c), tm in zip(inst, tms):
            sl = _head(h)
            t_ref[jj, d, h] = tm
            u_ref[d, rows, sl] = _dot(tm, c.vb)
            w_ref[d, rows, sl] = _dot(tm, c.kg).astype(BF16)
            qd_ref[d, rows, sl] = c.qd.astype(BF16)
            kd_ref[d, rows, sl] = c.kd.astype(BF16)
            at_ref[jj, d, h] = c.attn.astype(BF16)
            cd_ref[jj, 4 * d + h:4 * d + h + 1, :] = jnp.broadcast_to(c.cd, (1, DH))

    tok = _rows(cb * CHUNK, HW)
    tok2 = pl.BlockSpec((2, cb * CHUNK, HW), lambda i: (0, i, 0))
    mat = pl.BlockSpec((cb, 2, NH, CHUNK, CHUNK), lambda i: (i, 0, 0, 0, 0))
    return pl.pallas_call(
        body, name=name, grid=(n // cb,), in_specs=[tok, tok, tok, _rows(cb * CHUNK, BAP)],
        out_specs=[mat, tok2, tok2, tok2, tok2, mat, pl.BlockSpec((cb, 8, DH), lambda i: (i, 0, 0))],
        out_shape=[_sds((n, 2, NH, CHUNK, CHUNK)), _sds((2, s, HW)), _sds((2, s, HW), BF16), _sds((2, s, HW), BF16),
                   _sds((2, s, HW), BF16), _sds((n, 2, NH, CHUNK, CHUNK), BF16), _sds((n, 8, DH))],
        compiler_params=_cparams(1),
    )(q, k, v, bg)


def _seq_specs(s, order):
    n = s // CHUNK
    cb = min(SEQ_CB, n)
    nb = n // cb
    tb = cb * CHUNK

    def blk(d):
        return (lambda i: i) if order[d] else (lambda i: nb - 1 - i)

    def per_dir(make):
        return [make(d, blk(d)) for d in range(2)]

    tok2 = per_dir(lambda d, f: pl.BlockSpec((1, tb, HW), lambda i: (d, f(i), 0)))
    tok = per_dir(lambda d, f: pl.BlockSpec((tb, HW), lambda i: (f(i), 0)))
    mat = per_dir(lambda d, f: pl.BlockSpec((cb, 1, NH, CHUNK, CHUNK), lambda i: (f(i), d, 0, 0, 0)))
    cds = per_dir(lambda d, f: pl.BlockSpec((cb, 8, DH), lambda i: (f(i), 0, 0)))
    sts = per_dir(lambda d, f: pl.BlockSpec((cb, NH, DH, DH), lambda i: (f(i), 0, 0, 0)))
    dcd = per_dir(lambda d, f: pl.BlockSpec((cb, NH, DH), lambda i: (f(i), 0, 0)))
    return n, cb, nb, tok2, tok, mat, cds, sts, dcd


class _ScanRider:
    def __init__(self, af, bf, ar, br, shifted, tb, nb, up_spec, down_spec):
        s, c = af.shape
        self.shifted, self.t, self.c, self.nb = shifted, tb, c, nb
        self.args = [af, bf, ar, br]
        self.in_specs = [up_spec, up_spec, down_spec, down_spec]
        self.scratch = [pltpu.VMEM((16, c), F32)]
        if shifted:
            tb8 = tb // 8
            self.args += [af, ar]
            self.in_specs += [pl.BlockSpec((8, c), lambda i: (jnp.maximum(i * tb8 - 1, 0), 0)),
                              pl.BlockSpec((8, c), lambda i: (jnp.minimum((nb - i) * tb8, s // 8 - 1), 0))]
            self.scratch += [pltpu.VMEM((tb + 8, c), F32), pltpu.VMEM((tb + 8, c), F32)]
        self.out_specs = [up_spec, down_spec]
        self.out_shape = [_sds((s, c)), _sds((s, c))]

    def begin(self, in_refs, out_refs, scratch_refs):
        i = pl.program_id(0)
        self.carry = scratch_refs[0]

        @pl.when(i == 0)
        def _():
            self.carry[...] = jnp.zeros_like(self.carry)

        af_ref, self.bf_ref, ar_ref, self.br_ref = in_refs[0:4]
        self.hf_ref, self.hr_ref = out_refs
        self.a_up, self.a_dn = af_ref, ar_ref
        if self.shifted:
            t = self.t
            edge = jnp.where(i > 0, 1.0, 0.0).astype(F32)
            fbuf, rbuf = scratch_refs[1:3]
            fbuf[0:8, :] = in_refs[4][...] * edge
            fbuf[8:t + 8, :] = af_ref[...]
            rbuf[0:t, :] = ar_ref[...]
            rbuf[t:t + 8, :] = in_refs[5][...] * edge
            self.a_up, self.a_dn = fbuf, rbuf
        self.row = lax.broadcasted_iota(jnp.int32, (8, self.c), 0)
        self.cf, self.cr = self.carry[0:1, :], self.carry[8:9, :]

    def groups(self, lo, hi):
        ng = self.t // 8
        row = self.row
        for gi in range(lo, hi):
            rf, rr = 8 * gi, 8 * (ng - 1 - gi)
            if self.shifted:
                a_f = jnp.where(row > 0, pltpu.roll(self.a_up[rf + 8:rf + 16, :], 1, 0), pltpu.roll(self.a_up[rf:rf + 8, :], 1, 0))
                a_r = jnp.where(row < 7, pltpu.roll(self.a_dn[rr:rr + 8, :], 7, 0), pltpu.roll(self.a_dn[rr + 8:rr + 16, :], 7, 0))
            else:
                a_f, a_r = self.a_up[rf:rf + 8, :], self.a_dn[rr:rr + 8, :]
            a_f, b_f = _block_scan(a_f, self.bf_ref[rf:rf + 8, :], row, False)
            a_r, b_r = _block_scan(a_r, self.br_ref[rr:rr + 8, :], row, True)
            h_f = a_f * self.cf + b_f
            h_r = a_r * self.cr + b_r
            self.hf_ref[rf:rf + 8, :] = h_f
            self.hr_ref[rr:rr + 8, :] = h_r
            self.cf, self.cr = h_f[7:8, :], h_r[0:1, :]

    def end(self):
        self.carry[0:1, :] = self.cf
        self.carry[8:9, :] = self.cr


def _gdn_seq_fwd(u, w, qd, kd, at, cd, name, scan=None):
    s = u.shape[1]
    n, cb, nb, tok2, tok, mat, cds, sts, _ = _seq_specs(s, (True, False))
    rider = _ScanRider(*scan, False, cb * CHUNK, nb, tok[0], tok[1]) if scan else None
    ri = len(rider.args) if rider else 0

    def body(*refs):
        ins = (refs[0:6], refs[6:12])
        outs = (refs[12 + ri:15 + ri], refs[15 + ri:18 + ri])
        st = refs[18 + ri + (2 if rider else 0)]
        if rider:
            rider.begin(refs[12:12 + ri], refs[18 + ri:20 + ri], refs[21 + ri:])

        @pl.when(pl.program_id(0) == 0)
        def _():
            st[...] = jnp.zeros_like(st)

        for j in range(cb):
            items = []
            for d in range(2):
                jj = j if d == 0 else cb - 1 - j
                items += [(d, h, jj, slice(CHUNK * jj, CHUNK * (jj + 1)), _head(h)) for h in range(NH)]
            shs = [st[d, h] for d, h, _, _, _ in items]
            wss = [_dot(ins[d][1][0, rows, sl], sh) for (d, h, jj, rows, sl), sh in zip(items, shs)]
            vns = [ins[d][0][0, rows, sl] - ws for (d, h, jj, rows, sl), ws in zip(items, wss)]
            news = [sh * ins[d][5][jj, 4 * d + h:4 * d + h + 1, :] + _dot_tn(ins[d][3][0, rows, sl], vn)
                    for (d, h, jj, rows, sl), sh, vn in zip(items, shs, vns)]
            for (d, h, jj, rows, sl), sh, vn, new in zip(items, shs, vns, news):
                o_r, s_r, vn_r = outs[d]
                st[d, h] = new
                s_r[jj, h] = sh.astype(BF16)
                vn_r[rows, sl] = vn.astype(BF16)
                o_r[rows, sl] = _dot(ins[d][2][0, rows, sl], sh) + _dot(ins[d][4][jj, 0, h], vn)
            if rider:
                rider.groups(8 * j, 8 * (j + 1))
        if rider:
            rider.end()

    in_specs, out_specs, out_shape = [], [], []
    for d in range(2):
        in_specs += [tok2[d]] * 4 + [mat[d], cds[d]]
        out_specs += [tok[d], sts[d], tok[d]]
        out_shape += [_sds((s, HW)), _sds((n, NH, DH, DH), BF16), _sds((s, HW), BF16)]
    args = [u, w, qd, kd, at, cd, u, w, qd, kd, at, cd]
    scratch = [pltpu.VMEM((2, NH, DH, DH), F32)]
    if rider:
        in_specs, args = in_specs + rider.in_specs, args + rider.args
        out_specs, out_shape, scratch = out_specs + rider.out_specs, out_shape + rider.out_shape, scratch + rider.scratch
    return pl.pallas_call(
        body, name=name, grid=(nb,), in_specs=in_specs, out_specs=out_specs, out_shape=out_shape,
        scratch_shapes=scratch, compiler_params=_cparams(1),
    )(*args)


def _gdn_seq_bwd(do, w, qd, kd, at, cd, states, vns, name, scan=None):
    s = do.shape[0]
    n, cb, nb, tok2, tok, mat, cds, sts, dcd = _seq_specs(s, (False, True))
    rider = _ScanRider(*scan, True, cb * CHUNK, nb, tok[1], tok[0]) if scan else None
    ri = len(rider.args) if rider else 0

    def body(*refs):
        ins = (refs[0:8], refs[8:16])
        outs = (refs[16 + ri:21 + ri], refs[21 + ri:26 + ri])
        dst = refs[26 + ri + (2 if rider else 0)]
        if rider:
            rider.begin(refs[16:16 + ri], refs[26 + ri:28 + ri], refs[29 + ri:])

        @pl.when(pl.program_id(0) == 0)
        def _():
            dst[...] = jnp.zeros_like(dst)

        for j in range(cb):
            items = []
            for d in range(2):
                jj = cb - 1 - j if d == 0 else j
                items += [(d, h, jj, slice(CHUNK * jj, CHUNK * (jj + 1)), _head(h)) for h in range(NH)]
            dsns = [dst[d, h] for d, h, _, _, _ in items]
            dohs = [ins[d][0][rows, sl] for d, h, jj, rows, sl in items]
            d_vns = [_dot_tn(ins[d][4][jj, 0, h], doh) + _dot(ins[d][3][0, rows, sl], dsn)
                     for (d, h, jj, rows, sl), doh, dsn in zip(items, dohs, dsns)]
            news = [ins[d][5][jj, 4 * d + h:4 * d + h + 1, :] * dsn + _dot_tn(ins[d][2][0, rows, sl], doh)
                    - _dot_tn(ins[d][1][0, rows, sl], d_vn)
                    for (d, h, jj, rows, sl), doh, dsn, d_vn in zip(items, dohs, dsns, d_vns)]
            for (d, h, jj, rows, sl), doh, dsn, d_vn, new in zip(items, dohs, dsns, d_vns, news):
                dvn_r, dkd_r, dqd_r, dw_r, dcd_r = outs[d]
                sh = ins[d][6][jj, h].astype(F32)
                dst[d, h] = new
                dvn_r[rows, sl] = d_vn.astype(BF16)
                dkd_r[rows, sl] = _dot_nt(ins[d][7][rows, sl], dsn)
                dqd_r[rows, sl] = _dot_nt(doh, sh)
                dw_r[rows, sl] = (-_dot_nt(d_vn, sh)).astype(BF16)
                d_cd = jnp.sum(jnp.sum(sh * dsn, axis=1, keepdims=True), axis=0, keepdims=True)
                dcd_r[jj, h:h + 1, :] = jnp.broadcast_to(d_cd, (1, DH))
            if rider:
                rider.groups(8 * j, 8 * (j + 1))
        if rider:
            rider.end()

    in_specs, out_specs, out_shape, args = [], [], [], []
    for d in range(2):
        in_specs += [tok[d]] + [tok2[d]] * 3 + [mat[d], cds[d], sts[d], tok[d]]
        args += [do, w, qd, kd, at, cd, states[d], vns[d]]
        out_specs += [tok[d]] * 4 + [dcd[d]]
        out_shape += [_sds((s, HW), BF16), _sds((s, HW)), _sds((s, HW)), _sds((s, HW), BF16), _sds((n, NH, DH))]
    scratch = [pltpu.VMEM((2, NH, DH, DH), F32)]
    if rider:
        in_specs, args = in_specs + rider.in_specs, args + rider.args
        out_specs, out_shape, scratch = out_specs + rider.out_specs, out_shape + rider.out_shape, scratch + rider.scratch
    return pl.pallas_call(
        body, name=name, grid=(nb,), in_specs=in_specs, out_specs=out_specs, out_shape=out_shape,
        scratch_shapes=scratch, compiler_params=_cparams(1),
    )(*args)


def _gdn_local_bwd(q, k, v, bg, tmat, do, vns, seq_grads, name, comm=None):
    s = q.shape[0]
    n = s // CHUNK
    cb = min(LOCAL_CB, n)

    def body(*refs):
        q_ref, k_ref, v_ref, bg_ref, t_ref, do_ref = refs[0:6]
        vn_refs = refs[6:8]
        sg = (refs[8:13], refs[13:18])
        dq_ref, dk_ref, dv_ref, dbg_ref = refs[18:]
        lane = lax.broadcasted_iota(jnp.int32, (CHUNK, BAP), 1)
        rowi = lax.broadcasted_iota(jnp.int32, (CHUNK, 1), 0)
        ones = jnp.ones((CHUNK, DH), F32)
        masks = [_GdnMasks(d) for d in range(2)]
        inst = []
        for jj in range(cb):
            rows = slice(CHUNK * jj, CHUNK * (jj + 1))
            bgv = bg_ref[rows, :]
            qs = [q_ref[rows, _head(h)] for h in range(NH)]
            ks = [k_ref[rows, _head(h)] for h in range(NH)]
            kk = [_dot_nt(ks[h], ks[h]) for h in range(NH)]
            q0 = [_dot_nt(qs[h], ks[h]) for h in range(NH)]
            for d, m in enumerate(masks):
                gcs, gcs_t = _chunk_cumsums(m, bgv)
                for h in range(NH):
                    c = _GdnHead(qs[h], ks[h], v_ref[rows, _head(h)], kk[h], q0[h], bgv, gcs, gcs_t, d, h, m)
                    inst.append((jj, rows, d, h, m, c))
        ni = len(inst)
        cs = [it[-1] for it in inst]
        tms = [t_ref[jj, d, h] for jj, _, d, h, _, _ in inst]
        d_vns = [sg[d][0][rows, _head(h)] for _, rows, d, h, _, _ in inst]
        d_ws = [sg[d][3][rows, _head(h)] for _, rows, d, h, _, _ in inst]
        d_ts = [_dot_nt(d_vns[i], cs[i].vb) + _dot_nt(d_ws[i], cs[i].kg) for i in range(ni)]
        tts = [tm.T for tm in tms]
        xs = [_dot3(tts[i], d_ts[i]) for i in range(ni)]
        d_ls = [jnp.where(inst[i][4].strict, -_dot3(xs[i], tts[i]), 0.0) for i in range(ni)]
        d_attns = [jnp.where(m.incl, _dot_nt(do_ref[rows, _head(h)], vn_refs[d][rows, _head(h)]), 0.0)
                   for _, rows, d, h, m, _ in inst]
        d_vbs = [_dot(tts[i], d_vns[i]) for i in range(ni)]
        d_kgs = [_dot(tts[i], d_ws[i]) for i in range(ni)]
        d_a0s = [d_ls[i] * cs[i].decay for i in range(ni)]
        d_q0s = [d_attns[i] * cs[i].decay for i in range(ni)]
        es = [(d_ls[i] * cs[i].a0 + d_attns[i] * cs[i].q0) * cs[i].decay for i in range(ni)]
        kb_mm = [_dot(d_a0s[i], cs[i].k) for i in range(ni)]
        q_mm = [_dot(d_q0s[i], cs[i].k) for i in range(ni)]
        k_mm = [_dot_tn(d_a0s[i], cs[i].kb) + _dot_tn(d_q0s[i], cs[i].q) for i in range(ni)]
        e_cols = [_dot_exact(ones, es[i], _TN, False)[:, 0:1] for i in range(ni)]
        acc = {}
        d_gcs, d_betas = [], []
        for i, (jj, rows, d, h, m, c) in enumerate(inst):
            sl = _head(h)
            d_kd, d_qd = sg[d][1][rows, sl], sg[d][2][rows, sl]
            d_cd = sg[d][4][jj, h:h + 1, 0:1]
            d_vb, d_kg = d_vbs[i], d_kgs[i]
            d_kb = kb_mm[i] + d_kg * c.eg
            parts = (q_mm[i] + d_qd * c.eg, k_mm[i] + d_kd * c.ek + d_kb * c.beta, d_vb * c.beta)
            acc[jj, h] = [p + a for a, p in zip(acc[jj, h], parts)] if (jj, h) in acc else list(parts)
            s_kd = jnp.sum(d_kd * c.kd, axis=1, keepdims=True)
            d_gc = (jnp.sum(d_kg * c.kg, axis=1, keepdims=True) + jnp.sum(d_qd * c.qd, axis=1, keepdims=True) - s_kd
                    + jnp.sum(es[i], axis=1, keepdims=True) - e_cols[i])
            d_gl = jnp.sum(s_kd, axis=0, keepdims=True) + d_cd * c.cd
            d_gcs.append(d_gc + jnp.where(rowi == m.last, d_gl, 0.0))
            d_betas.append(jnp.sum(d_kb * c.k, axis=1, keepdims=True) + jnp.sum(d_vb * c.v, axis=1, keepdims=True))
        d_gs = [_dot_exact(inst[i][4].tri, d_gcs[i] * ones, _TN, True)[:, 0:1] for i in range(ni)]
        dbg = [jnp.zeros((CHUNK, BAP), F32) for _ in range(cb)]
        for i, (jj, _, d, h, _, _) in enumerate(inst):
            dbg[jj] = dbg[jj] + jnp.where(lane == 4 * d + h, d_betas[i], 0.0) + jnp.where(lane == 8 + 4 * d + h, d_gs[i], 0.0)
        for jj in range(cb):
            rows = slice(CHUNK * jj, CHUNK * (jj + 1))
            for h in range(NH):
                dq_ref[rows, _head(h)], dk_ref[rows, _head(h)], dv_ref[rows, _head(h)] = acc[jj, h]
            dbg_ref[rows, :] = dbg[jj]

    tok = _rows(cb * CHUNK, HW)
    bgs = _rows(cb * CHUNK, BAP)
    mat = pl.BlockSpec((cb, 2, NH, CHUNK, CHUNK), lambda i: (i, 0, 0, 0, 0))
    dcd = pl.BlockSpec((cb, NH, DH), lambda i: (i, 0, 0))
    args = [q, k, v, bg, tmat, do, vns[0], vns[1]]
    in_specs = [tok, tok, tok, bgs, mat, tok, tok, tok]
    for d in range(2):
        args += list(seq_grads[d])
        in_specs += [tok] * 4 + [dcd]
    return _pallas(body, comm, name=name, grid=(n // cb,), in_specs=in_specs, out_specs=[tok, tok, tok, bgs],
                   out_shape=[_sds((s, HW))] * 3 + [_sds((s, BAP))], scratch_shapes=[], args=args)


def _prep_bwd(c_qkv, p_ba, alog_row, dtb_row, dq, dk, dv, dbg, name):
    s = c_qkv.shape[0]
    t = min(256, s)

    def body(cq_ref, pc_ref, alog_ref, dtb_ref, dq_ref, dk_ref, dv_ref, dbg_ref,
             dcq_ref, dpc_ref, dalog_ref, ddtb_ref):
        @pl.when(pl.program_id(0) == 0)
        def _():
            dalog_ref[...] = jnp.zeros_like(dalog_ref)
            ddtb_ref[...] = jnp.zeros_like(ddtb_ref)

        cq = cq_ref[...]
        sq = cq * _sig(cq)
        sg = _silu_grad(cq)
        for h in range(NH):
            sl = slice(DH * h, DH * (h + 1))
            for off, d_ref, scale in ((0, dq_ref, DH ** -0.5), (RGW, dk_ref, 1.0)):
                csl = slice(off + DH * h, off + DH * (h + 1))
                xh = sq[:, csl]
                nrm = lax.rsqrt(jnp.sum(xh * xh, axis=-1, keepdims=True) + EPS)
                y = xh * nrm
                dy = d_ref[:, sl] * scale
                dcq_ref[:, csl] = nrm * (dy - y * jnp.sum(dy * y, axis=-1, keepdims=True)) * sg[:, csl]
        dcq_ref[:, 2 * RGW:] = dv_ref[...] * sg[:, 2 * RGW:]
        pc = pc_ref[...]
        lane = lax.broadcasted_iota(jnp.int32, pc.shape, 1)
        dbg = dbg_ref[...]
        beta = _sig(pc)
        ea = jnp.exp(alog_ref[...])
        z = pc + dtb_ref[...]
        g = -ea * _softplus(z)
        is_g = jnp.logical_and(lane >= 8, lane < 16)
        d_alpha = jnp.where(is_g, dbg * (-ea) * _sig(z), 0.0)
        dpc_ref[...] = jnp.where(lane < 8, dbg * beta * (1.0 - beta), d_alpha).astype(BF16)
        dalog_ref[...] += _colsum(jnp.where(is_g, dbg * g, 0.0))
        ddtb_ref[...] += _colsum(d_alpha)

    return pl.pallas_call(
        body, name=name, grid=(s // t,),
        in_specs=[_rows(t, QKVW), _rows(t, BAP), _full((1, BAP)), _full((1, BAP))] + [_rows(t, HW)] * 3 + [_rows(t, BAP)],
        out_specs=[_rows(t, QKVW), _rows(t, BAP), _full((1, BAP)), _full((1, BAP))],
        out_shape=[_sds((s, QKVW)), _sds((s, BAP), BF16), _sds((1, BAP)), _sds((1, BAP))],
        compiler_params=_cparams(1),
    )(c_qkv, p_ba, alog_row, dtb_row, dq, dk, dv, dbg)


def _mix_out_values(hf, hb, gate, of, ob, z, gn):
    hr = hf + hb
    y_rg = hr * _gelu(gate)
    osum = of + ob
    parts = []
    for h in range(NH):
        sl = slice(DH * h, DH * (h + 1))
        oh = osum[:, sl]
        r, ohat = _rms(oh)
        zh = z[:, sl]
        parts.append((r, ohat, zh))
    y_gdn = jnp.concatenate([ohat * gn * (zh * _sig(zh)) for (r, ohat, zh) in parts], axis=1)
    return hr, y_rg, y_gdn, parts


def _outproj(x1, hf, hb, gate, of, ob, z, gn, wout, name):
    s = x1.shape[0]
    t = min(256, s)

    def body(x_ref, hf_ref, hb_ref, gate_ref, of_ref, ob_ref, z_ref, gn_ref, w_ref, xo_ref, y_ref):
        _, y_rg, y_gdn, _ = _mix_out_values(hf_ref[...], hb_ref[...], gate_ref[...], of_ref[...], ob_ref[...],
                                            z_ref[...], gn_ref[...])
        y = jnp.concatenate([y_rg, y_gdn], axis=1).astype(BF16)
        y_ref[...] = y
        xo_ref[...] = x_ref[...] + jnp.dot(y, w_ref[...], preferred_element_type=F32)

    return pl.pallas_call(
        body, name=name, grid=(s // t,),
        in_specs=[_rows(t, D)] + [_rows(t, RGW)] * 6 + [_full((1, DH)), _full((D, D))],
        out_specs=[_rows(t, D), _rows(t, D)], out_shape=[_sds((s, D)), _sds((s, D), BF16)],
        compiler_params=_cparams(1),
    )(x1, hf, hb, gate, of, ob, z, gn, wout)


def _outproj_bwd(dx2, hf, hb, gate, of, ob, z, gn, wout, name, comm=None):
    s = dx2.shape[0]
    t = min(256, s)

    def body(d_ref, hf_ref, hb_ref, gate_ref, of_ref, ob_ref, z_ref, gn_ref, w_ref,
             dhr_ref, dgate_ref, dos_ref, dz_ref, dgn_ref, db_ref):
        @pl.when(pl.program_id(0) == 0)
        def _():
            dgn_ref[...] = jnp.zeros_like(dgn_ref)

        gate = gate_ref[...]
        gn_v = gn_ref[...]
        hr, _, _, parts = _mix_out_values(hf_ref[...], hb_ref[...], gate, of_ref[...], ob_ref[...], z_ref[...], gn_v)
        dbf = d_ref[...].astype(BF16)
        db_ref[...] = dbf
        dy = _dot_nt(dbf, w_ref[...])
        dyr = dy[:, :RGW]
        dhr_ref[...] = dyr * _gelu(gate)
        dgate_ref[...] = (dyr * hr * _gelu_grad(gate)).astype(BF16)
        dgn = jnp.zeros((1, DH), F32)
        for h, (r, ohat, zh) in enumerate(parts):
            sl = slice(DH * h, DH * (h + 1))
            dyh = dy[:, RGW + DH * h:RGW + DH * (h + 1)]
            sz = zh * _sig(zh)
            dn = dyh * sz
            dz_ref[:, sl] = (dyh * ohat * gn_v * _silu_grad(zh)).astype(BF16)
            dgn = dgn + _colsum(dn * ohat)
            dos_ref[:, sl] = _rms_bwd(dn, ohat, r, gn_v).astype(BF16)
        dgn_ref[...] += dgn

    return _pallas(
        body, comm, name=name, grid=(s // t,),
        in_specs=[_rows(t, D)] + [_rows(t, RGW)] * 6 + [_full((1, DH)), _full((D, D))],
        out_specs=[_rows(t, RGW)] * 4 + [_full((1, DH)), _rows(t, D)],
        out_shape=[_sds((s, RGW))] + [_sds((s, RGW), BF16)] * 3 + [_sds((1, DH)), _sds((s, D), BF16)],
        scratch_shapes=[], args=(dx2, hf, hb, gate, of, ob, z, gn, wout))


def _loss_head(x3, target, gain, name):
    s = x3.shape[0]
    t = min(256, s)

    def body(x_ref, t_ref, g_ref, dx_ref, dxh_ref, loss_ref, dg_ref):
        @pl.when(pl.program_id(0) == 0)
        def _():
            loss_ref[...] = jnp.zeros_like(loss_ref)
            dg_ref[...] = jnp.zeros_like(dg_ref)

        r, xh = _rms(x_ref[...])
        gv = g_ref[...]
        err = xh * gv - t_ref[...]
        per_tok = jnp.mean(err * err, axis=-1, keepdims=True)
        loss_ref[...] += 0.5 * jnp.sum(per_tok, axis=0, keepdims=True)
        dy = err * (1.0 / D)
        dg_ref[...] += _colsum(dy * xh)
        dx = _rms_bwd(dy, xh, r, gv)
        dx_ref[...] = dx
        dxh_ref[...] = (0.5 * dx).astype(BF16)

    return pl.pallas_call(
        body, name=name, grid=(s // t,), in_specs=[_rows(t, D), _rows(t, D), _full((1, D))],
        out_specs=[_rows(t, D), _rows(t, D), _full((8, 128)), _full((1, D))],
        out_shape=[_sds((s, D)), _sds((s, D), BF16), _sds((8, 128)), _sds((1, D))], compiler_params=_cparams(1),
    )(x3, target, gain)


def _adamw_math(wv, gv, mv, vv):
    mn = ADAM_B1 * mv + (1.0 - ADAM_B1) * gv
    vn = ADAM_B2 * vv + (1.0 - ADAM_B2) * (gv * gv)
    m_hat = mn / (1.0 - ADAM_B1 ** ADAM_STEP)
    v_hat = vn / (1.0 - ADAM_B2 ** ADAM_STEP)
    return -ADAM_LR * (m_hat / (jnp.sqrt(v_hat) + ADAM_EPS) + ADAM_WD * wv), mn, vn


def _row_tile(r, c):
    tr = r
    while tr * c * 4 > (1 << 20) and tr % 16 == 0:
        tr //= 2
    return tr


def _adamw(w, g, m, v, name):
    r, c = w.shape
    tr = _row_tile(r, c)

    def body(w_ref, g_ref, m_ref, v_ref, d_ref, nm_ref, nv_ref):
        d_ref[...], nm_ref[...], nv_ref[...] = _adamw_math(w_ref[...], g_ref[...], m_ref[...], v_ref[...])

    return pl.pallas_call(
        body, name=name, grid=(r // tr,), in_specs=[_rows(tr, c)] * 4, out_specs=[_rows(tr, c)] * 3,
        out_shape=[_sds((r, c))] * 3, compiler_params=_cparams(1),
    )(w, g, m, v)


def _adamw_halves(w, own, recv, m, v, c_arr, name):
    r, c = w.shape
    h = r // 2
    tr = _row_tile(h, c)
    nh = h // tr

    def body(c_ref, w_ref, own_ref, recv_ref, m_ref, v_ref, g_ref, d_ref, nm_ref, nv_ref):
        first_half = pl.program_id(0) < nh
        use_own = first_half == (c_ref[0] == 0)
        gv = jnp.where(use_own, own_ref[...], recv_ref[...])
        g_ref[...] = gv
        d_ref[...], nm_ref[...], nv_ref[...] = _adamw_math(w_ref[...], gv, m_ref[...], v_ref[...])

    full = pl.BlockSpec((tr, c), lambda i, c_ref: (i, 0))
    half = pl.BlockSpec((tr, c), lambda i, c_ref: (i % nh, 0))
    return pl.pallas_call(
        body, name=name, out_shape=[_sds((r, c))] * 4,
        grid_spec=pltpu.PrefetchScalarGridSpec(
            num_scalar_prefetch=1, grid=(2 * nh,), in_specs=[full, half, half, full, full], out_specs=[full] * 4),
        compiler_params=_cparams(1),
    )(c_arr, w, own, recv, m, v)


def _mesh_pos():
    return lax.axis_index("x"), lax.axis_index("y"), lax.axis_index("c")


def _other_chips(x, y):
    return [(1 - x, y), (x, 1 - y), (1 - x, 1 - y)]


class _Comm:
    def __init__(self, inputs, out_shapes, scratch, start, finish, space=pltpu.HBM):
        self.inputs, self.out_shapes, self.scratch = list(inputs), list(out_shapes), list(scratch)
        self.start, self.finish, self.space = start, finish, space


def _comm_call(comm, name):
    ni, no = len(comm.inputs), len(comm.out_shapes)

    def body(*refs):
        comm.start(refs[:ni], refs[ni:ni + no], refs[ni + no:])
        comm.finish(refs[:ni], refs[ni:ni + no], refs[ni + no:])

    spec = pl.BlockSpec(memory_space=comm.space)
    return list(pl.pallas_call(body, name=name, out_shape=comm.out_shapes, in_specs=[spec] * ni, out_specs=[spec] * no,
                               scratch_shapes=comm.scratch)(*comm.inputs))


def _join_comm(a, b):
    ia, oa, sa = len(a.inputs), len(a.out_shapes), len(a.scratch)

    def both(method):
        def run(ins, outs, sems):
            getattr(a, method)(ins[:ia], outs[:oa], sems[:sa])
            getattr(b, method)(ins[ia:], outs[oa:], sems[sa:])
        return run

    return _Comm(a.inputs + b.inputs, a.out_shapes + b.out_shapes, a.scratch + b.scratch, both("start"), both("finish"))


def _pallas(body, comm, *, name, grid, in_specs, out_specs, out_shape, scratch_shapes, args):
    params = _cparams(len(grid))
    if comm is None:
        outs = pl.pallas_call(body, name=name, grid=grid, in_specs=in_specs, out_specs=out_specs, out_shape=out_shape,
                              scratch_shapes=scratch_shapes, compiler_params=params)(*args)
        return list(outs), []
    n_in, n_out, n_sc = len(in_specs), len(out_specs), len(scratch_shapes)
    ci, co = len(comm.inputs), len(comm.out_shapes)

    def carried(*refs):
        bounds = [0, n_in, n_in + ci, n_in + ci + n_out, n_in + ci + n_out + co, n_in + ci + n_out + co + n_sc, len(refs)]
        ins, cins, outs, couts, scr, csems = [refs[lo:hi] for lo, hi in zip(bounds[:-1], bounds[1:])]
        ids = [pl.program_id(k) for k in range(len(grid))]
        first = functools.reduce(jnp.logical_and, [i == 0 for i in ids])
        last = functools.reduce(jnp.logical_and, [i == g - 1 for i, g in zip(ids, grid)])

        @pl.when(first)
        def _():
            comm.start(cins, couts, csems)

        body(*ins, *outs, *scr)

        @pl.when(last)
        def _():
            comm.finish(cins, couts, csems)

    hbm = pl.BlockSpec(memory_space=pltpu.HBM)
    outs = pl.pallas_call(
        carried, name=name, grid=grid, in_specs=list(in_specs) + [hbm] * ci, out_specs=list(out_specs) + [hbm] * co,
        out_shape=list(out_shape) + comm.out_shapes, scratch_shapes=list(scratch_shapes) + comm.scratch,
        compiler_params=params)(*args, *comm.inputs)
    return list(outs[:n_out]), list(outs[n_out:])


def _gather_comm(arrays, space, block_rows):
    n_arr = len(arrays)

    def plan(x_refs, out_refs, sems):
        send_sems, recv_sems, local_sems = sems
        x, y, c = _mesh_pos()
        me, sibling = (x, y, c), (x, y, 1 - c)
        chips = _other_chips(x, y)

        def slot(a, px, py, pc):
            return out_refs[a].at[4 * px + 2 * py + pc]

        def copy(a, k, block, to, src=None):
            return pltpu.make_async_remote_copy(
                src_ref=slot(a, *block) if src is None else src, dst_ref=slot(a, *block),
                send_sem=send_sems.at[7 * a + k], recv_sem=recv_sems.at[7 * a + k], device_id=to, device_id_type=MESH)

        srcs = [x_refs[a] if block_rows[a] is None else
                x_refs[a].at[pl.ds(pl.multiple_of(c * block_rows[a], 16), block_rows[a]), :] for a in range(n_arr)]
        local = [pltpu.make_async_copy(srcs[a], slot(a, *me), local_sems.at[a]) for a in range(n_arr)]
        first = []
        for a in range(n_arr):
            first += [copy(a, 1 + j, me, (*chip, c), src=srcs[a]) for j, chip in enumerate(chips)]
            first.append(copy(a, 0, me, sibling, src=srcs[a]))
        return me, sibling, chips, c, copy, local, first

    def start(x_refs, out_refs, sems):
        _, _, _, _, _, local, first = plan(x_refs, out_refs, sems)
        for cp in local + first:
            cp.start()

    def finish(x_refs, out_refs, sems):
        me, sibling, chips, c, copy, local, first = plan(x_refs, out_refs, sems)
        passed = []
        for j, chip in enumerate(chips):
            for a in range(n_arr):
                copy(a, 1 + j, (*chip, c), me).wait_recv()
                fwd = copy(a, 4 + j, (*chip, c), sibling)
                fwd.start()
                passed.append(fwd)
        for a in range(n_arr):
            copy(a, 0, sibling, me).wait_recv()
            for j, chip in enumerate(chips):
                copy(a, 4 + j, (*chip, 1 - c), me).wait_recv()
        for cp in first + passed:
            cp.wait_send()
        for cp in local:
            cp.wait()

    out_shapes = [_sds((8, w.shape[0] if r is None else r) + w.shape[1:], w.dtype) for w, r in zip(arrays, block_rows)]
    scratch = [pltpu.SemaphoreType.DMA((7 * n_arr,)), pltpu.SemaphoreType.DMA((7 * n_arr,)), pltpu.SemaphoreType.DMA((n_arr,))]
    return _Comm(arrays, out_shapes, scratch, start, finish, space)


def _weights_gather_comm(shards):
    return _gather_comm(shards, pltpu.HBM, [w.shape[0] // 2 for w in shards])


def _all_shards(gathered):
    return [o.reshape(NSH, 2 * o.shape[1], o.shape[2]) for o in gathered]


def _gather_small(block, name):
    return _comm_call(_gather_comm([block], pltpu.VMEM, [None]), name)[0]


def _exchange_comm(gs):
    n = len(gs)
    halves = [g.shape[1] // 2 for g in gs]

    def plan(g_refs, land_refs, sems):
        send_sems, recv_sems = sems
        x, y, c = _mesh_pos()
        copies = []
        for a in range(n):
            h = halves[a]
            for s in range(NSH):
                copies.append(pltpu.make_async_remote_copy(
                    src_ref=g_refs[a].at[s, pl.ds(pl.multiple_of((1 - c) * h, 8), h), :], dst_ref=land_refs[a].at[s],
                    send_sem=send_sems.at[NSH * a + s], recv_sem=recv_sems.at[NSH * a + s],
                    device_id=(x, y, 1 - c), device_id_type=MESH))
        return copies

    def start(g_refs, land_refs, sems):
        for cp in plan(g_refs, land_refs, sems):
            cp.start()

    def finish(g_refs, land_refs, sems):
        for cp in plan(g_refs, land_refs, sems):
            cp.wait()

    scratch = [pltpu.SemaphoreType.DMA((NSH * n,)), pltpu.SemaphoreType.DMA((NSH * n,))]
    return _Comm(gs, [_sds((NSH, h, g.shape[2])) for h, g in zip(halves, gs)], scratch, start, finish)


def _chip_sum(g, land, c_arr, name):
    _, h, cols = land.shape

    def body(c_ref, g_ref, l_ref, o_ref):
        o_ref[...] = (g_ref[...] + l_ref[...]).astype(BF16)

    return pl.pallas_call(
        body, name=name, out_shape=_sds((NSH, h, cols), BF16),
        grid_spec=pltpu.PrefetchScalarGridSpec(
            num_scalar_prefetch=1, grid=(NSH,),
            in_specs=[pl.BlockSpec((1, h, cols), lambda s, c_ref: (s, c_ref[0], 0)),
                      pl.BlockSpec((1, h, cols), lambda s, c_ref: (s, 0, 0))],
            out_specs=pl.BlockSpec((1, h, cols), lambda s, c_ref: (s, 0, 0))),
        compiler_params=_cparams(1),
    )(c_arr, g, land)


def _scatter_comm(parts):
    n = len(parts)

    def plan(p_refs, land_refs, sems):
        send_sems, recv_sems, local_sems = sems
        x, y, c = _mesh_pos()
        my_chip = 2 * x + y
        local = [pltpu.make_async_copy(p_refs[a].at[my_chip], land_refs[a].at[my_chip], local_sems.at[a]) for a in range(n)]
        copies = []
        for a in range(n):
            for j, (px, py) in enumerate(_other_chips(x, y)):
                copies.append(pltpu.make_async_remote_copy(
                    src_ref=p_refs[a].at[2 * px + py], dst_ref=land_refs[a].at[my_chip],
                    send_sem=send_sems.at[3 * a + j], recv_sem=recv_sems.at[3 * a + j],
                    device_id=(px, py, c), device_id_type=MESH))
        return local, copies

    def start(p_refs, land_refs, sems):
        local, copies = plan(p_refs, land_refs, sems)
        for cp in local + copies:
            cp.start()

    def finish(p_refs, land_refs, sems):
        local, copies = plan(p_refs, land_refs, sems)
        for cp in copies:
            cp.wait()
        for cp in local:
            cp.wait()

    scratch = [pltpu.SemaphoreType.DMA((3 * n,)), pltpu.SemaphoreType.DMA((3 * n,)), pltpu.SemaphoreType.DMA((n,))]
    return _Comm(parts, [_sds(p.shape, BF16) for p in parts], scratch, start, finish)


def _sum_slots(land, name):
    k, r, c = land.shape
    tr = r // 2 if r % 32 == 0 else r

    def body(l_ref, o_ref):
        acc = l_ref[0].astype(F32)
        for i in range(1, k):
            acc = acc + l_ref[i].astype(F32)
        o_ref[...] = acc

    return pl.pallas_call(
        body, name=name, grid=(r // tr,), in_specs=[pl.BlockSpec((k, tr, c), lambda i: (0, i, 0))],
        out_specs=_rows(tr, c), out_shape=_sds((r, c)), compiler_params=_cparams(1),
    )(land)


def _sibling_swap(halves):
    n = len(halves)

    def body(*refs):
        h_refs, out_refs = refs[:n], refs[n:2 * n]
        send_sems, recv_sems = refs[2 * n:]
        x, y, c = _mesh_pos()
        copies = [pltpu.make_async_remote_copy(
            src_ref=h_refs[a], dst_ref=out_refs[a], send_sem=send_sems.at[a], recv_sem=recv_sems.at[a],
            device_id=(x, y, 1 - c), device_id_type=MESH) for a in range(n)]
        for cp in copies:
            cp.start()
        for cp in copies:
            cp.wait()

    return pl.pallas_call(
        body, name="grad_sibling_swap", out_shape=[_sds(h.shape) for h in halves],
        in_specs=[pl.BlockSpec(memory_space=pltpu.HBM)] * n, out_specs=[pl.BlockSpec(memory_space=pltpu.HBM)] * n,
        scratch_shapes=[pltpu.SemaphoreType.DMA((n,)), pltpu.SemaphoreType.DMA((n,))],
    )(*halves)


def _pad_rows(v, width):
    flat = v.reshape(-1)
    rows = -(-flat.shape[0] // width)
    rows = -(-rows // 8) * 8
    return jnp.pad(flat, (0, rows * width - flat.shape[0])).reshape(rows, width)


def _size(shape):
    n = 1
    for dim in shape:
        n *= dim
    return n


def _row_pack(arrs):
    pieces = []
    for a in arrs:
        rows = -(-a.size // D)
        pieces.append(jnp.pad(a.reshape(-1), (0, rows * D - a.size)).reshape(rows, D))
    total = sum(p.shape[0] for p in pieces)
    if total % 8:
        pieces.append(jnp.zeros((8 - total % 8, D), F32))
    return jnp.concatenate(pieces, axis=0)


def _row_unpack(packed, shapes):
    out, r0 = [], 0
    for shp in shapes:
        n = _size(shp)
        rows = -(-n // D)
        out.append(packed[r0:r0 + rows].reshape(-1)[:n].reshape(shp))
        r0 += rows
    return out


def _block_diag(w):
    eye = jnp.eye(8, dtype=w.dtype)
    return (w[:, :, None, :] * eye[:, None, :, None]).reshape(RGW, RGW)


def _diag_blocks(dense):
    r = dense.reshape(8, 64, 8, 64)
    return jnp.stack([r[n, :, n, :] for n in range(8)])


def _lane_row(v8):
    return jnp.zeros((1, BAP), F32).at[0, 8:16].set(v8.reshape(8))


def _chip_sums(gs, lands, names, c_arr):
    return [_chip_sum(g, l, c_arr, "chip_sum_" + n) for g, l, n in zip(gs, lands, names)]


def _reduce_parts(gs, names, c_arr, tag):
    return _chip_sums(gs, _comm_call(_exchange_comm(gs), "grad_sibling_exchange_" + tag), names, c_arr)


def _local_step(x, target, sw, ffn1_w, later_shards, c_arr):
    (g1, gmix, rg_cw8, rg_cb, wgates, gbias, lam_row, gdn_cw8, alog_row, dtb_row, gn, g2, gfin) = sw
    wg1, wu1, wd1 = ffn1_w

    (x1, a1, b1, fb1), gathered = _ffn_fwd(x, g1, wg1, wu1, wd1, "ffn1_fwd", comm=_weights_gather_comm(later_shards))
    win_sh, wout_sh, wg2, wu2, wd2 = _all_shards(gathered)
    w_in_full = jnp.transpose(win_sh, (1, 0, 2)).reshape(D, NSH * INSH)
    wout = wout_sh.reshape(D, D)
    w_in_groups = (w_in_full[:, 0:512], w_in_full[:, 512:1024], w_in_full[:, 1024:2560], w_in_full[:, 2560:3072],
                   jnp.pad(w_in_full[:, 3072:3088], ((0, 0), (0, BAP - BAW))))
    h2, p_rgx, p_gate, p_qkv, p_z, p_ba = _inproj(x1, gmix, w_in_groups, "in_proj")
    c_rg = _conv(p_rgx, rg_cw8, rg_cb, "rg_conv")
    c_qkv = _conv(p_qkv, gdn_cw8, jnp.zeros((1, QKVW), F32), "gdn_conv")
    a0, bb0, a1s, bb1, q, k, v, bg = _mix_prep(c_rg, c_qkv, p_ba, wgates, gbias, lam_row, alog_row, dtb_row, "mix_prep")
    tmat, gu, gw, gqd, gkd, gat, gcd = _gdn_local_fwd(q, k, v, bg, "gdn_local_fwd")
    of, s0, vn0, ob, s1, vn1, hf, hb = _gdn_seq_fwd(gu, gw, gqd, gkd, gat, gcd, "gdn_seq_fwd", scan=(a0, bb0, a1s, bb1))
    x2, ymix = _outproj(x1, hf, hb, p_gate, of, ob, p_z, gn, wout, "out_proj")
    (x3, a2, b2, fb2), _ = _ffn_fwd(x2, g2, wg2, wu2, wd2, "ffn2_fwd")
    dx3, dob2, loss_blk, d_gfin = _loss_head(x3, target, gfin, "loss_head")

    dx2, d_g2, hb2, dab2, dbb2, _ = _ffn_bwd(x2, dx3, dob2, g2, a2, b2, wg2, wu2, wd2, "ffn2_bwd")
    d_ffn2 = [_tn(dab2, hb2, "ffn2_dwg"), _tn(dbb2, hb2, "ffn2_dwu"), _tn(fb2, dob2, "ffn2_dwd")]

    (d_hr, d_gate, d_os, d_z, d_gn, dx2b), lands = _outproj_bwd(dx2, hf, hb, p_gate, of, ob, p_z, gn, wout, "out_proj_bwd",
                                                               comm=_exchange_comm(d_ffn2))
    parts_ffn2 = _chip_sums(d_ffn2, lands, _BIG_NAMES[5:8], c_arr)
    d_wout = _tn(ymix, dx2b, "dw_out")[0]

    sg = _gdn_seq_bwd(d_os, gw, gqd, gkd, gat, gcd, (s0, s1), (vn0, vn1), "gdn_seq_bwd", scan=(a1s, d_hr, a0, d_hr))
    lam1, lam0 = sg[10:12]
    d_xc, d_pre, xcb, d_gbias, d_lam = _gates_bwd(c_rg, wgates, gbias, lam_row, lam0, lam1, hf, hb, "rg_gates_bwd")
    d_wgates = _tn(xcb, d_pre, "dw_gates")[0]
    d_prgx, d_rgcw8, d_rgcb = _conv_bwd(p_rgx, d_xc, rg_cw8, "rg_conv_bwd")

    (dq, dk, dv, dbg), lands_ffn2 = _gdn_local_bwd(q, k, v, bg, tmat, d_os, (vn0, vn1), (sg[0:5], sg[5:10]), "gdn_local_bwd",
                                                  comm=_scatter_comm(parts_ffn2))
    d_cqkv, d_pba, d_alog, d_dtb = _prep_bwd(c_qkv, p_ba, alog_row, dtb_row, dq, dk, dv, dbg, "gdn_prep_bwd")
    d_pqkv, d_gdncw8, _ = _conv_bwd(p_qkv, d_cqkv, gdn_cw8, "gdn_conv_bwd")

    dps = (d_prgx, d_gate, d_pqkv, d_z, d_pba)
    dx1, dob1, d_gmix = _inproj_bwd(x1, dx2, gmix, dps, w_in_groups, "in_proj_bwd")
    d_win_groups = [_tn(h2, dp, "dw_in_%d" % i)[0] for i, dp in enumerate(dps)]
    d_win = jnp.concatenate(d_win_groups[:4] + [d_win_groups[4][:, :BAW]], axis=1)
    d_mix = [jnp.transpose(d_win.reshape(D, NSH, INSH), (1, 0, 2)), d_wout.reshape(NSH, OUTSH, D)]

    small = dict(
        mix_norm=d_gmix, rg_conv_w=d_rgcw8[:4], rg_conv_b=d_rgcb,
        rg_gate_a_w=jnp.stack([_diag_blocks(d_wgates[:, RGW * i:RGW * (i + 1)]) for i in (0, 1)]),
        rg_gate_x_w=jnp.stack([_diag_blocks(d_wgates[:, RGW * i:RGW * (i + 1)]) for i in (2, 3)]),
        rg_gate_a_b=d_gbias[0, :2 * RGW].reshape(2, RGW), rg_gate_x_b=d_gbias[0, 2 * RGW:].reshape(2, RGW),
        rg_lambda=d_lam.reshape(2, RGW), gdn_conv_w=d_gdncw8[:4],
        gdn_a_log=d_alog[0, 8:16].reshape(2, NH), gdn_dt_bias=d_dtb[0, 8:16].reshape(2, NH),
        gdn_norm=d_gn, ffn2_norm=d_g2, final_norm=d_gfin)
    small_pack = _row_pack([small[n] for n in _SMALL_NAMES[1:]])

    riders = _join_comm(_exchange_comm(d_mix), _gather_comm([small_pack], pltpu.HBM, [None]))
    gx, d_g1, hb1, dab1, dbb1, carried = _ffn_bwd(x, dx1, dob1, g1, a1, b1, wg1, wu1, wd1, "ffn1_bwd", comm=riders)
    parts_mix = _chip_sums(d_mix, carried[0:2], _BIG_NAMES[3:5], c_arr)
    d_wg1, lands_mix = _tn(dab1, hb1, "ffn1_dwg", comm=_scatter_comm(parts_mix))
    parts_wg1 = _reduce_parts([d_wg1], _BIG_NAMES[0:1], c_arr, "ffn1_gate")
    d_wu1, lands_wg1 = _tn(dbb1, hb1, "ffn1_dwu", comm=_scatter_comm(parts_wg1))
    parts_wu1 = _reduce_parts([d_wu1], _BIG_NAMES[1:2], c_arr, "ffn1_up")
    d_wd1, lands_wu1 = _tn(fb1, dob1, "ffn1_dwd", comm=_scatter_comm(parts_wu1))
    parts_wd1 = _reduce_parts([d_wd1], _BIG_NAMES[2:3], c_arr, "ffn1_down")
    lands_ffn1 = lands_wg1 + lands_wu1 + _comm_call(_scatter_comm(parts_wd1), "grad_chip_scatter_ffn1_down")

    halves = [_sum_slots(l, "sum_chips_" + n) for l, n in zip(lands_ffn1 + lands_mix + lands_ffn2, _BIG_NAMES)]
    small_shapes = [small[n].shape for n in _SMALL_NAMES[1:]]
    return loss_blk, gx, halves, d_g1, carried[2], small_shapes


_SMALL_NAMES = ("ffn1_norm", "mix_norm", "rg_conv_w", "rg_conv_b", "rg_gate_a_w", "rg_gate_a_b", "rg_gate_x_w",
                "rg_gate_x_b", "rg_lambda", "gdn_conv_w", "gdn_a_log", "gdn_dt_bias", "gdn_norm", "ffn2_norm", "final_norm")
_SMALL_SHARDED = dict(rg_conv_w=128, rg_gate_a_b=128, rg_gate_x_b=128, rg_lambda=128, gdn_conv_w=384)
_OUT_ORDER = ("ffn1_norm", "ffn1_w_gate", "ffn1_w_up", "ffn1_w_down", "mix_norm", "w_in", "w_out", "rg_conv_w", "rg_conv_b",
              "rg_gate_a_w", "rg_gate_a_b", "rg_gate_x_w", "rg_gate_x_b", "rg_lambda", "gdn_conv_w", "gdn_a_log",
              "gdn_dt_bias", "gdn_norm", "ffn2_norm", "ffn2_w_gate", "ffn2_w_up", "ffn2_w_down", "final_norm")
_BIG_NAMES = ("ffn1_w_gate", "ffn1_w_up", "ffn1_w_down", "w_in", "w_out", "ffn2_w_gate", "ffn2_w_up", "ffn2_w_down")
_TRANSPOSED = ("ffn1_w_gate", "ffn1_w_up", "ffn2_w_gate", "ffn2_w_up")


def kernel(x, ffn1_norm, ffn1_w_gate, ffn1_w_up, ffn1_w_down, mix_norm, w_in, w_out, rg_conv_w, rg_conv_b, rg_gate_a_w, rg_gate_a_b, rg_gate_x_w, rg_gate_x_b, rg_lambda, gdn_conv_w, gdn_a_log, gdn_dt_bias, gdn_norm, ffn2_norm, ffn2_w_gate, ffn2_w_up, ffn2_w_down, final_norm, loss_target, m_ffn1_norm, m_ffn1_w_gate, m_ffn1_w_up, m_ffn1_w_down, m_mix_norm, m_w_in, m_w_out, m_rg_conv_w, m_rg_conv_b, m_rg_gate_a_w, m_rg_gate_a_b, m_rg_gate_x_w, m_rg_gate_x_b, m_rg_lambda, m_gdn_conv_w, m_gdn_a_log, m_gdn_dt_bias, m_gdn_norm, m_ffn2_norm, m_ffn2_w_gate, m_ffn2_w_up, m_ffn2_w_down, m_final_norm, v_ffn1_norm, v_ffn1_w_gate, v_ffn1_w_up, v_ffn1_w_down, v_mix_norm, v_w_in, v_w_out, v_rg_conv_w, v_rg_conv_b, v_rg_gate_a_w, v_rg_gate_a_b, v_rg_gate_x_w, v_rg_gate_x_b, v_rg_lambda, v_gdn_conv_w, v_gdn_a_log, v_gdn_dt_bias, v_gdn_norm, v_ffn2_norm, v_ffn2_w_gate, v_ffn2_w_up, v_ffn2_w_down, v_final_norm):
    args = dict(locals())
    w = {n: args[n] for n in _OUT_ORDER}
    mom = {n: args["m_" + n] for n in _OUT_ORDER}
    var = {n: args["v_" + n] for n in _OUT_ORDER}
    xi, yi, ci = _mesh_pos()
    shard = 2 * xi + yi

    big_bf16 = [w[n][0].astype(BF16) for n in _BIG_NAMES]
    sm_local = _pad_rows(jnp.concatenate([w[n][0].reshape(-1) for n in _SMALL_SHARDED]), 128)
    first = _comm_call(_gather_comm(big_bf16[0:3] + [sm_local], pltpu.HBM, [t.shape[0] // 2 for t in big_bf16[0:3]] + [None]),
                       "gather_first_weights")
    ffn1_w = _all_shards(first[0:3])
    sm_all = first[3][0::2].reshape(NSH, -1)
    sm_full, off = {}, 0
    for n, wd_ in _SMALL_SHARDED.items():
        rows = w[n].shape[1]
        piece = sm_all[:, off:off + rows * wd_].reshape(NSH, rows, wd_)
        sm_full[n] = jnp.transpose(piece, (1, 0, 2)).reshape(rows, NSH * wd_)
        off += rows * wd_

    wa, wx = rg_gate_a_w[0], rg_gate_x_w[0]
    wgates = jnp.concatenate([_block_diag(wa[0]), _block_diag(wa[1]), _block_diag(wx[0]), _block_diag(wx[1])],
                             axis=1).astype(BF16)
    gbias = jnp.concatenate([sm_full["rg_gate_a_b"].reshape(1, -1), sm_full["rg_gate_x_b"].reshape(1, -1)], axis=1)
    sw = (ffn1_norm, mix_norm, jnp.pad(sm_full["rg_conv_w"], ((0, 4), (0, 0))), rg_conv_b, wgates, gbias,
          sm_full["rg_lambda"].reshape(1, -1), jnp.pad(sm_full["gdn_conv_w"], ((0, 4), (0, 0))), _lane_row(gdn_a_log),
          _lane_row(gdn_dt_bias), gdn_norm, ffn2_norm, final_norm.reshape(1, D))
    c_arr = ci.reshape(1).astype(jnp.int32)

    loss_blk, gx, halves, d_g1, small_packs, small_shapes = _local_step(x[0], loss_target[0], sw, ffn1_w, big_bf16[3:], c_arr)
    loss = lax.psum(loss_blk[0, 0], ("x", "y", "c"))
    grads = {}

    g1_all = _gather_small(jnp.pad(d_g1, ((0, 7), (0, 0))), "gather_ffn1_norm_grad")
    sm_sums = [_sum_slots(g1_all, "ffn1_norm_grad_sum")[0:1]] + _row_unpack(_sum_slots(small_packs, "small_grad_sum"), small_shapes)
    for n, g in zip(_SMALL_NAMES, sm_sums):
        if n in _SMALL_SHARDED:
            wd_ = _SMALL_SHARDED[n]
            g = lax.dynamic_slice_in_dim(g, shard * wd_, wd_, axis=1)
        grads[n] = g.reshape(w[n].shape)

    delta, new_m, new_v = {}, {}, {}
    for n, own, recv in zip(_BIG_NAMES, halves, _sibling_swap(halves)):
        to2d = jnp.transpose if n in _TRANSPOSED else (lambda t: t)
        outs4 = _adamw_halves(to2d(w[n][0]), own, recv, to2d(mom[n][0]), to2d(var[n][0]), c_arr, "adamw_" + n)
        grads[n], delta[n], new_m[n], new_v[n] = [to2d(o)[None] for o in outs4]
    packs = [_row_pack([t[n] for n in _SMALL_NAMES]) for t in (w, grads, mom, var)]
    sm_shapes = [w[n].shape for n in _SMALL_NAMES]
    for dst, src in zip((delta, new_m, new_v), _adamw(*packs, "adamw_small")):
        for n, val in zip(_SMALL_NAMES, _row_unpack(src, sm_shapes)):
            dst[n] = val

    outs = [loss, gx[None]]
    for group in (grads, delta, new_m, new_v):
        outs += [group[n] for n in _OUT_ORDER]
    return tuple(outs)
```

```python
import functools

import jax
import jax.numpy as jnp
from jax import lax
from jax.experimental import pallas as pl
from jax.experimental.pallas import tpu as pltpu

F32 = jnp.float32
BF16 = jnp.bfloat16
EPS = 1e-6
D = 1024
NSH = 4
FSH = 704
RGW = 512
QKVW = 1536
ZW = 512
BAW = 16
BAP = 128
INSH = 772
OUTSH = 256
CHUNK = 64
NH = 4
DH = 128
RG_C = 8.0
VMEM_LIMIT = 52 * 1024 * 1024
MESH = pl.DeviceIdType.MESH

ADAM_LR = 0.001
ADAM_B1 = 0.9
ADAM_B2 = 0.999
ADAM_EPS = 1e-08
ADAM_WD = 0.01
ADAM_STEP = 10


def _cparams(n_grid):
    return pltpu.CompilerParams(dimension_semantics=("arbitrary",) * n_grid, vmem_limit_bytes=VMEM_LIMIT)


def _sig(x):
    return 0.5 + 0.5 * jnp.tanh(0.5 * x)


def _sig_pos(x):
    return 1.0 / (1.0 + jnp.exp(-x))


def _softplus(x):
    return jnp.maximum(x, 0.0) + jnp.log(1.0 + jnp.exp(-jnp.abs(x)))


def _neg_expm1(y):
    series = -y * (1.0 + y * (0.5 + y * (1.0 / 6 + y * (1.0 / 24 + y * (1.0 / 120 + y * (1.0 / 720 + y / 5040))))))
    return jnp.where(y > -0.3, series, 1.0 - jnp.exp(y))


_GELU_C = 0.7978845608028654


def _gelu(x):
    t = jnp.tanh(_GELU_C * (x + 0.044715 * x * x * x))
    return 0.5 * x * (1.0 + t)


def _gelu_grad(x):
    t = jnp.tanh(_GELU_C * (x + 0.044715 * x * x * x))
    return 0.5 * (1.0 + t) + 0.5 * x * (1.0 - t * t) * _GELU_C * (1.0 + 3 * 0.044715 * x * x)


def _silu_grad(x):
    s = _sig(x)
    return s * (1.0 + x * (1.0 - s))


def _dot(a, b):
    return jnp.dot(a.astype(BF16), b.astype(BF16), preferred_element_type=F32)


def _dot_nt(a, b):
    return lax.dot_general(a.astype(BF16), b.astype(BF16), (((1,), (1,)), ((), ())), preferred_element_type=F32)


def _dot_tn(a, b):
    return lax.dot_general(a.astype(BF16), b.astype(BF16), (((0,), (0,)), ((), ())), preferred_element_type=F32)


_NN = ((1,), (0,))
_NT = ((1,), (1,))
_TN = ((0,), (0,))


def _dg(a, b, dims):
    return lax.dot_general(a, b, (dims, ((), ())), preferred_element_type=F32)


def _split2(a):
    hi = a.astype(BF16)
    return hi, (a - hi.astype(F32)).astype(BF16)


def _dot3(a, b, dims=_NN):
    ah, al = _split2(a)
    bh, bl = _split2(b)
    return _dg(ah, bh, dims) + _dg(ah, bl, dims) + _dg(al, bh, dims)


def _dot_exact(e, x, dims, e_is_lhs):
    x0 = x.astype(BF16)
    r = x - x0.astype(F32)
    x1 = r.astype(BF16)
    x2 = (r - x1.astype(F32)).astype(BF16)
    eb = e.astype(BF16)
    if e_is_lhs:
        return _dg(eb, x0, dims) + _dg(eb, x1, dims) + _dg(eb, x2, dims)
    return _dg(x0, eb, dims) + _dg(x1, eb, dims) + _dg(x2, eb, dims)


def _rms(xv):
    r = lax.rsqrt(jnp.mean(xv * xv, axis=-1, keepdims=True) + EPS)
    return r, xv * r


def _rms_bwd(dy, xh, r, gain):
    dxh = dy * gain
    return r * (dxh - xh * jnp.mean(dxh * xh, axis=-1, keepdims=True))


def _colsum(v):
    return jnp.sum(v, axis=0, keepdims=True)


def _rows(t, c):
    return pl.BlockSpec((t, c), lambda i: (i, 0))


def _full(shape):
    n = len(shape)
    return pl.BlockSpec(shape, lambda i: (0,) * n)


def _sds(shape, dtype=F32):
    return jax.ShapeDtypeStruct(shape, dtype)


def _ffn_fwd(x, gain, wg, wu, wd, name, comm=None):
    s = x.shape[0]
    tm = min(256, s)

    def body(x_ref, g_ref, wg_ref, wu_ref, wd_ref, xo_ref, ga_ref, gb_ref, f_ref):
        xv = x_ref[...]
        _, xh = _rms(xv)
        h = (xh * g_ref[...]).astype(BF16)
        acc = None
        for j in range(NSH):
            a = jnp.dot(h, wg_ref[j], preferred_element_type=F32)
            b = jnp.dot(h, wu_ref[j], preferred_element_type=F32)
            sa = _sig(a)
            silu = a * sa
            fv = silu * b
            f = fv.astype(BF16)
            f_ref[j] = f
            ga_ref[j] = (sa * b + fv * (1.0 - sa)).astype(BF16)
            gb_ref[j] = silu.astype(BF16)
            part = jnp.dot(f, wd_ref[j], preferred_element_type=F32)
            acc = part if acc is None else acc + part
        xo_ref[...] = xv + 0.5 * acc

    hidden = pl.BlockSpec((NSH, tm, FSH), lambda i: (0, i, 0))
    return _pallas(
        body, comm, name=name, grid=(s // tm,),
        in_specs=[_rows(tm, D), _full((1, D)),
                  pl.BlockSpec((NSH, D, FSH), lambda i: (0, 0, 0), pipeline_mode=pl.Buffered(1)),
                  pl.BlockSpec((NSH, D, FSH), lambda i: (0, 0, 0), pipeline_mode=pl.Buffered(1)),
                  pl.BlockSpec((NSH, FSH, D), lambda i: (0, 0, 0), pipeline_mode=pl.Buffered(1))],
        out_specs=[_rows(tm, D), hidden, hidden, hidden],
        out_shape=[_sds((s, D))] + [_sds((NSH, s, FSH), BF16)] * 3,
        scratch_shapes=[], args=(x, gain, wg, wu, wd))


def _ffn_bwd(x, dout, do, gain, ga, gb, wg, wu, wd, name, comm=None):
    s = x.shape[0]
    tm = min(512, s)

    def hidden(do_ref, ga_ref, gb_ref, wd_ref, da_ref, db_ref):
        dov = do_ref[...]
        for j in range(NSH):
            df = _dot_nt(dov, wd_ref[j])
            da_ref[j] = (df * ga_ref[j].astype(F32)).astype(BF16)
            db_ref[j] = (df * gb_ref[j].astype(F32)).astype(BF16)

    sh = pl.BlockSpec((NSH, tm, FSH), lambda i: (0, i, 0))
    (da, db), carried = _pallas(
        hidden, comm, name=name + "_hidden", grid=(s // tm,),
        in_specs=[_rows(tm, D), sh, sh, pl.BlockSpec((NSH, FSH, D), lambda i: (0, 0, 0), pipeline_mode=pl.Buffered(1))],
        out_specs=[sh, sh], out_shape=[_sds((NSH, s, FSH), BF16)] * 2, scratch_shapes=[], args=(do, ga, gb, wd))

    def inputs(x_ref, d_ref, g_ref, da_ref, db_ref, wg_ref, wu_ref, dx_ref, dg_ref, h_ref):
        @pl.when(pl.program_id(0) == 0)
        def _():
            dg_ref[...] = jnp.zeros_like(dg_ref)

        dh = jnp.zeros((tm, D), F32)
        for j in range(NSH):
            dh = dh + _dot_nt(da_ref[j], wg_ref[j]) + _dot_nt(db_ref[j], wu_ref[j])
        r, xh = _rms(x_ref[...])
        gv = g_ref[...]
        h_ref[...] = (xh * gv).astype(BF16)
        dg_ref[...] += _colsum(dh * xh)
        dx_ref[...] = d_ref[...] + _rms_bwd(dh, xh, r, gv)

    grads = pl.BlockSpec((NSH, tm, FSH), lambda i: (0, i, 0))
    resident = pl.BlockSpec((NSH, D, FSH), lambda i: (0, 0, 0), pipeline_mode=pl.Buffered(1))
    dx, dg, h = pl.pallas_call(
        inputs, name=name + "_input", grid=(s // tm,),
        in_specs=[_rows(tm, D), _rows(tm, D), _full((1, D)), grads, grads, resident, resident],
        out_specs=[_rows(tm, D), _full((1, D)), _rows(tm, D)],
        out_shape=[_sds((s, D)), _sds((1, D)), _sds((s, D), BF16)], compiler_params=_cparams(1),
    )(x, dout, gain, da, db, wg, wu)
    return dx, dg, h, da, db, carried


def _tn(a, b, name, comm=None):
    a_g = a.ndim == 3
    b_g = b.ndim == 3
    g = a.shape[0] if a_g else (b.shape[0] if b_g else 1)
    s, k = a.shape[-2:]
    n = b.shape[-1]
    ts = min(2048 if b.dtype == BF16 else 1024, s)

    def body(a_ref, b_ref, o_ref):
        @pl.when(pl.program_id(1) == 0)
        def _():
            o_ref[...] = jnp.zeros_like(o_ref)

        av = a_ref[0] if a_g else a_ref[...]
        bv = b_ref[0] if b_g else b_ref[...]
        o_ref[0] += _dot_tn(av, bv)

    a_spec = pl.BlockSpec((1, ts, k), lambda gi, si: (gi, si, 0)) if a_g else pl.BlockSpec((ts, k), lambda gi, si: (si, 0))
    b_spec = pl.BlockSpec((1, ts, n), lambda gi, si: (gi, si, 0)) if b_g else pl.BlockSpec((ts, n), lambda gi, si: (si, 0))
    outs, carried = _pallas(body, comm, name=name, grid=(g, s // ts), in_specs=[a_spec, b_spec],
                            out_specs=[pl.BlockSpec((1, k, n), lambda gi, si: (gi, 0, 0))], out_shape=[_sds((g, k, n))],
                            scratch_shapes=[], args=(a, b))
    return outs[0] if comm is None else (outs[0], carried)


_P_WIDTHS = (RGW, RGW, QKVW, ZW, BAP)


def _inproj(x1, gain, ws, name):
    s = x1.shape[0]
    tm = min(256, s)

    def body(x_ref, g_ref, *refs):
        w_refs = refs[:5]
        h_ref = refs[5]
        p_refs = refs[6:]
        _, xh = _rms(x_ref[...])
        h = (xh * g_ref[...]).astype(BF16)
        h_ref[...] = h
        for w_ref, p_ref in zip(w_refs, p_refs):
            p_ref[...] = jnp.dot(h, w_ref[...], preferred_element_type=F32)

    return pl.pallas_call(
        body, name=name, grid=(s // tm,),
        in_specs=[_rows(tm, D), _full((1, D))] + [_full((D, w)) for w in _P_WIDTHS],
        out_specs=[_rows(tm, D)] + [_rows(tm, w) for w in _P_WIDTHS],
        out_shape=[_sds((s, D), BF16)] + [_sds((s, w)) for w in _P_WIDTHS],
        compiler_params=_cparams(1),
    )(x1, gain, *ws)


def _inproj_bwd(x1, dx2, gain, dps, ws, name):
    s = x1.shape[0]
    tm = min(256, s)

    def body(x_ref, d_ref, g_ref, *refs):
        dp_refs = refs[:5]
        w_refs = refs[5:10]
        dx_ref, dxh_ref, dg_ref = refs[10:]

        @pl.when(pl.program_id(0) == 0)
        def _():
            dg_ref[...] = jnp.zeros_like(dg_ref)

        dh = jnp.zeros((tm, D), F32)
        for dp_ref, w_ref in zip(dp_refs, w_refs):
            dh = dh + _dot_nt(dp_ref[...], w_ref[...])
        r, xh = _rms(x_ref[...])
        dg_ref[...] += _colsum(dh * xh)
        dx = d_ref[...] + _rms_bwd(dh, xh, r, g_ref[...])
        dx_ref[...] = dx
        dxh_ref[...] = (0.5 * dx).astype(BF16)

    return pl.pallas_call(
        body, name=name, grid=(s // tm,),
        in_specs=[_rows(tm, D), _rows(tm, D), _full((1, D))] + [_rows(tm, w) for w in _P_WIDTHS]
        + [_full((D, w)) for w in _P_WIDTHS],
        out_specs=[_rows(tm, D), _rows(tm, D), _full((1, D))],
        out_shape=[_sds((s, D)), _sds((s, D), BF16), _sds((1, D))],
        compiler_params=_cparams(1),
    )(x1, dx2, gain, *dps, *ws)


def _halo_specs(s, t, c):
    nb8 = s // 8
    tb = t // 8
    prev = pl.BlockSpec((8, c), lambda i: (jnp.maximum(i * tb - 1, 0), 0))
    nxt = pl.BlockSpec((8, c), lambda i: (jnp.minimum((i + 1) * tb, nb8 - 1), 0))
    return prev, nxt


def _edge_masks(nb):
    i = pl.program_id(0)
    return jnp.where(i > 0, 1.0, 0.0).astype(F32), jnp.where(i < nb - 1, 1.0, 0.0).astype(F32)


def _shifted(xx, off, t):
    n = t + 16
    sh = (-off) % n
    rolled = xx if sh == 0 else pltpu.roll(xx, sh, 0)
    return rolled[8:8 + t]


def _conv(x, w8, bias, name):
    s, c = x.shape
    t = min(256, s)
    nb = s // t

    def body(x_ref, xp_ref, xn_ref, w_ref, b_ref, o_ref):
        pm, nm = _edge_masks(nb)
        for c0 in range(0, c, 512):
            cols = slice(c0, c0 + 512)
            xx = jnp.concatenate([xp_ref[:, cols] * pm, x_ref[:, cols], xn_ref[:, cols] * nm], axis=0)
            acc = jnp.zeros((t, 512), F32) + b_ref[:, cols]
            for j in range(4):
                acc = acc + w_ref[j:j + 1, cols] * _shifted(xx, j - 2, t)
            o_ref[:, cols] = acc

    prev, nxt = _halo_specs(s, t, c)
    return pl.pallas_call(
        body, name=name, grid=(nb,),
        in_specs=[_rows(t, c), prev, nxt, _full((8, c)), _full((1, c))],
        out_specs=_rows(t, c), out_shape=_sds((s, c)), compiler_params=_cparams(1),
    )(x, x, x, w8, bias)


def _conv_bwd(x, dc, w8, name):
    s, c = x.shape
    t = min(256, s)
    nb = s // t

    def body(x_ref, d_ref, dp_ref, dn_ref, w_ref, dx_ref, dw_ref, db_ref):
        @pl.when(pl.program_id(0) == 0)
        def _():
            dw_ref[...] = jnp.zeros_like(dw_ref)
            db_ref[...] = jnp.zeros_like(db_ref)

        pm, nm = _edge_masks(nb)
        for c0 in range(0, c, 512):
            cols = slice(c0, c0 + 512)
            dd = jnp.concatenate([dp_ref[:, cols] * pm, d_ref[:, cols], dn_ref[:, cols] * nm], axis=0)
            xv = x_ref[:, cols]
            acc = jnp.zeros((t, 512), F32)
            for j in range(4):
                dsh = _shifted(dd, 2 - j, t)
                acc = acc + w_ref[j:j + 1, cols] * dsh
                dw_ref[j:j + 1, cols] += _colsum(dsh * xv)
            dx_ref[:, cols] = acc.astype(BF16)
            db_ref[:, cols] += _colsum(d_ref[:, cols])

    prev, nxt = _halo_specs(s, t, c)
    return pl.pallas_call(
        body, name=name, grid=(nb,),
        in_specs=[_rows(t, c), _rows(t, c), prev, nxt, _full((8, c))],
        out_specs=[_rows(t, c), _full((8, c)), _full((1, c))],
        out_shape=[_sds((s, c), BF16), _sds((8, c)), _sds((1, c))], compiler_params=_cparams(1),
    )(x, dc, dc, dc, w8)


def _rg_gates(xc, pre, lam_row):
    sp8 = RG_C * _softplus(-lam_row)
    out = []
    for d in range(2):
        r = _sig_pos(pre[:, RGW * d:RGW * (d + 1)])
        gi = _sig(pre[:, 2 * RGW + RGW * d:2 * RGW + RGW * (d + 1)])
        la = -r * sp8[:, RGW * d:RGW * (d + 1)]
        a = jnp.exp(la)
        mult = jnp.sqrt(_neg_expm1(2.0 * la))
        out.append((r, gi, a, mult))
    return out


def _mix_prep(c_rg, c_qkv, p_ba, wgates, gbias, lam_row, alog_row, dtb_row, name):
    s = c_rg.shape[0]
    t = min(256, s)

    def body(xc_ref, cq_ref, pc_ref, wg_ref, gb_ref, lam_ref, alog_ref, dtb_ref,
             a0_ref, b0_ref, a1_ref, b1_ref, q_ref, k_ref, v_ref, bg_ref):
        xc = xc_ref[...]
        pre = _dot(xc, wg_ref[...]) + gb_ref[...]
        gates = _rg_gates(xc, pre, lam_ref[...])
        for (r, gi, a, mult), a_ref, b_ref in zip(gates, (a0_ref, a1_ref), (b0_ref, b1_ref)):
            a_ref[...] = a
            b_ref[...] = mult * gi * xc
        cq = cq_ref[...]
        sq = cq * _sig(cq)
        for h in range(NH):
            sl = slice(DH * h, DH * (h + 1))
            qh = sq[:, sl]
            q_ref[:, sl] = qh * lax.rsqrt(jnp.sum(qh * qh, axis=-1, keepdims=True) + EPS) * (DH ** -0.5)
            kh = sq[:, RGW + DH * h:RGW + DH * (h + 1)]
            k_ref[:, sl] = kh * lax.rsqrt(jnp.sum(kh * kh, axis=-1, keepdims=True) + EPS)
        v_ref[...] = sq[:, 2 * RGW:]
        pc = pc_ref[...]
        lane = lax.broadcasted_iota(jnp.int32, pc.shape, 1)
        beta = _sig(pc)
        g = -jnp.exp(alog_ref[...]) * _softplus(pc + dtb_ref[...])
        bg_ref[...] = jnp.where(lane < 8, beta, jnp.where(lane < 16, g, 0.0))

    return pl.pallas_call(
        body, name=name, grid=(s // t,),
        in_specs=[_rows(t, RGW), _rows(t, QKVW), _rows(t, BAP), _full((RGW, 4 * RGW)), _full((1, 4 * RGW)),
                  _full((1, 2 * RGW)), _full((1, BAP)), _full((1, BAP))],
        out_specs=[_rows(t, RGW)] * 7 + [_rows(t, BAP)],
        out_shape=[_sds((s, RGW))] * 7 + [_sds((s, BAP))],
        compiler_params=_cparams(1),
    )(c_rg, c_qkv, p_ba, wgates, gbias, lam_row, alog_row, dtb_row)


def _block_scan(av, bv, row, downwards):
    for k in (1, 2, 4):
        sh = (8 - k) if downwards else k
        m = (row < 8 - k) if downwards else (row >= k)
        a_s = pltpu.roll(av, sh, 0)
        b_s = pltpu.roll(bv, sh, 0)
        bv = jnp.where(m, av * b_s + bv, bv)
        av = jnp.where(m, av * a_s, av)
    return av, bv


def _gates_bwd(xc, wgates, gbias, lam_row, lam0, lam1, hf, hb, name):
    s = xc.shape[0]
    t = min(256, s)
    nb = s // t

    def body(xc_ref, wg_ref, gb_ref, lam_ref, l0_ref, l1_ref, hf_ref, hfp_ref, hfn_ref, hb_ref, hbp_ref, hbn_ref,
             dxc_ref, dpre_ref, xcb_ref, dgb_ref, dlam_ref):
        @pl.when(pl.program_id(0) == 0)
        def _():
            dgb_ref[...] = jnp.zeros_like(dgb_ref)
            dlam_ref[...] = jnp.zeros_like(dlam_ref)

        pm, nm = _edge_masks(nb)
        h_prev = _shifted(jnp.concatenate([hfp_ref[...] * pm, hf_ref[...], hfn_ref[...] * nm], axis=0), -1, t)
        h_next = _shifted(jnp.concatenate([hbp_ref[...] * pm, hb_ref[...], hbn_ref[...] * nm], axis=0), 1, t)
        h_shift = (h_prev, h_next)
        xv = xc_ref[...]
        pre = _dot(xv, wg_ref[...]) + gb_ref[...]
        lam_row_v = lam_ref[...]
        sp8 = RG_C * _softplus(-lam_row_v)
        dsp_dlam = -RG_C * _sig(-lam_row_v)
        gates = _rg_gates(xv, pre, lam_row_v)
        dxc = jnp.zeros((t, RGW), F32)
        dpre_r = []
        dpre_i = []
        for d, ((r, gi, a, mult), l_ref, hs) in enumerate(zip(gates, (l0_ref, l1_ref), h_shift)):
            dbb = l_ref[...]
            da = dbb * hs
            cs = slice(RGW * d, RGW * (d + 1))
            dmult = dbb * gi * xv
            dgi = dbb * mult * xv
            dxc = dxc + dbb * mult * gi
            dla = da * a - dmult * a * a / mult
            dr = -dla * sp8[:, cs]
            dlam_ref[:, cs] += _colsum(-dla * r) * dsp_dlam[:, cs]
            dpre_r.append(dr * r * (1.0 - r))
            dpre_i.append(dgi * gi * (1.0 - gi))
        dpre = jnp.concatenate(dpre_r + dpre_i, axis=1)
        dgb_ref[...] += _colsum(dpre)
        dpre_b = dpre.astype(BF16)
        dpre_ref[...] = dpre_b
        xcb_ref[...] = xv.astype(BF16)
        dxc_ref[...] = dxc + _dot_nt(dpre_b, wg_ref[...])

    prev, nxt = _halo_specs(s, t, RGW)
    return pl.pallas_call(
        body, name=name, grid=(s // t,),
        in_specs=[_rows(t, RGW), _full((RGW, 4 * RGW)), _full((1, 4 * RGW)), _full((1, 2 * RGW))] + [_rows(t, RGW)] * 2
        + [_rows(t, RGW), prev, nxt] * 2,
        out_specs=[_rows(t, RGW), _rows(t, 4 * RGW), _rows(t, RGW), _full((1, 4 * RGW)), _full((1, 2 * RGW))],
        out_shape=[_sds((s, RGW)), _sds((s, 4 * RGW), BF16), _sds((s, RGW), BF16), _sds((1, 4 * RGW)), _sds((1, 2 * RGW))],
        compiler_params=_cparams(1),
    )(xc, wgates, gbias, lam_row, lam0, lam1, hf, hf, hf, hb, hb, hb)


class _GdnMasks:
    def __init__(self, d):
        ri = lax.broadcasted_iota(jnp.int32, (CHUNK, CHUNK), 0)
        ci = lax.broadcasted_iota(jnp.int32, (CHUNK, CHUNK), 1)
        self.incl = (ri >= ci) if d == 0 else (ri <= ci)
        self.strict = (ri > ci) if d == 0 else (ri < ci)
        b16 = jnp.right_shift(ri, 4) == jnp.right_shift(ci, 4)
        b32 = jnp.right_shift(ri, 5) == jnp.right_shift(ci, 5)
        self.diag16 = b16
        self.off32 = jnp.logical_and(b32, jnp.logical_not(b16))
        self.off64 = jnp.logical_not(b32)
        self.eye = jnp.where(ri == ci, 1.0, 0.0).astype(F32)
        self.tri = jnp.where(self.incl, 1.0, 0.0).astype(F32)
        self.last = CHUNK - 1 if d == 0 else 0


def _tri_inv(lmat, m):
    return _tri_inv_many([lmat], [m])[0]


def _tri_inv_many(lmats, masks):
    n = len(lmats)
    ns = [jnp.where(masks[i].diag16, lmats[i], 0.0) for i in range(n)]
    ps = [masks[i].eye - ns[i] for i in range(n)]
    qs = [_dot3(ns[i], ns[i]) for i in range(n)]
    for step in range(3):
        ps = [_dot3(ps[i], masks[i].eye + qs[i]) for i in range(n)]
        if step < 2:
            qs = [_dot3(qs[i], qs[i]) for i in range(n)]
    for off in ("off32", "off64"):
        ts = [_dot3(ps[i], jnp.where(getattr(masks[i], off), lmats[i], 0.0)) for i in range(n)]
        ps = [ps[i] - _dot3(ts[i], ps[i]) for i in range(n)]
    return ps


def _chunk_cumsums(m, bgv):
    return _dot_exact(m.tri, bgv, _NN, True), _dot_exact(m.tri, bgv, ((0,), (1,)), False)


class _GdnHead:
    def __init__(self, qh, kh, vh, kk, q0, bg, gcs, gcs_t, d, h, m):
        cb = 4 * d + h
        cg = 8 + 4 * d + h
        self.q, self.k, self.v = qh, kh, vh
        self.beta = bg[:, cb:cb + 1]
        gcol = gcs[:, cg:cg + 1]
        grow = gcs_t[cg:cg + 1, :]
        gl = gcs[m.last:m.last + 1, cg:cg + 1]
        self.decay = jnp.exp(jnp.where(m.incl, gcol - grow, -1e30))
        self.kb = kh * self.beta
        self.vb = vh * self.beta
        self.a0 = kk * self.beta
        self.q0 = q0
        self.lmat = jnp.where(m.strict, self.a0 * self.decay, 0.0)
        self.attn = self.q0 * self.decay
        self.eg = jnp.exp(gcol)
        self.ek = jnp.exp(gl - gcol)
        self.cd = jnp.exp(gl)
        self.kg = self.kb * self.eg
        self.qd = qh * self.eg
        self.kd = kh * self.ek


HW = NH * DH
SEQ_CB = 4
LOCAL_CB = 4


def _head(h):
    return slice(DH * h, DH * (h + 1))


def _gdn_local_fwd(q, k, v, bg, name):
    s = q.shape[0]
    n = s // CHUNK
    cb = min(LOCAL_CB, n)

    def body(q_ref, k_ref, v_ref, bg_ref, t_ref, u_ref, w_ref, qd_ref, kd_ref, at_ref, cd_ref):
        masks = [_GdnMasks(d) for d in range(2)]
        inst = []
        for jj in range(cb):
            rows = slice(CHUNK * jj, CHUNK * (jj + 1))
            bgv = bg_ref[rows, :]
            qs = [q_ref[rows, _head(h)] for h in range(NH)]
            ks = [k_ref[rows, _head(h)] for h in range(NH)]
            kk = [_dot_nt(ks[h], ks[h]) for h in range(NH)]
            q0 = [_dot_nt(qs[h], ks[h]) for h in range(NH)]
            for d, m in enumerate(masks):
                gcs, gcs_t = _chunk_cumsums(m, bgv)
                for h in range(NH):
                    c = _GdnHead(qs[h], ks[h], v_ref[rows, _head(h)], kk[h], q0[h], bgv, gcs, gcs_t, d, h, m)
                    inst.append((jj, rows, d, h, m, c))
        tms = _tri_inv_many([it[-1].lmat for it in inst], [it[-2] for it in inst])
        for (jj, rows, d, h, m, c), tm in zip(inst, tms):
            sl = _head(h)
            t_ref[jj, d, h] = tm
            u_ref[d, rows, sl] = _dot(tm, c.vb)
            w_ref[d, rows, sl] = _dot(tm, c.kg).astype(BF16)
            qd_ref[d, rows, sl] = c.qd.astype(BF16)
            kd_ref[d, rows, sl] = c.kd.astype(BF16)
            at_ref[jj, d, h] = c.attn.astype(BF16)
            cd_ref[jj, 4 * d + h:4 * d + h + 1, :] = jnp.broadcast_to(c.cd, (1, DH))

    tok = _rows(cb * CHUNK, HW)
    tok2 = pl.BlockSpec((2, cb * CHUNK, HW), lambda i: (0, i, 0))
    mat = pl.BlockSpec((cb, 2, NH, CHUNK, CHUNK), lambda i: (i, 0, 0, 0, 0))
    return pl.pallas_call(
        body, name=name, grid=(n // cb,), in_specs=[tok, tok, tok, _rows(cb * CHUNK, BAP)],
        out_specs=[mat, tok2, tok2, tok2, tok2, mat, pl.BlockSpec((cb, 8, DH), lambda i: (i, 0, 0))],
        out_shape=[_sds((n, 2, NH, CHUNK, CHUNK)), _sds((2, s, HW)), _sds((2, s, HW), BF16), _sds((2, s, HW), BF16),
                   _sds((2, s, HW), BF16), _sds((n, 2, NH, CHUNK, CHUNK), BF16), _sds((n, 8, DH))],
        compiler_params=_cparams(1),
    )(q, k, v, bg)


def _seq_specs(s, order):
    n = s // CHUNK
    cb = min(SEQ_CB, n)
    nb = n // cb
    tb = cb * CHUNK

    def blk(d):
        return (lambda i: i) if order[d] else (lambda i: nb - 1 - i)

    def per_dir(make):
        return [make(d, blk(d)) for d in range(2)]

    tok2 = per_dir(lambda d, f: pl.BlockSpec((1, tb, HW), lambda i: (d, f(i), 0)))
    tok = per_dir(lambda d, f: pl.BlockSpec((tb, HW), lambda i: (f(i), 0)))
    mat = per_dir(lambda d, f: pl.BlockSpec((cb, 1, NH, CHUNK, CHUNK), lambda i: (f(i), d, 0, 0, 0)))
    cds = per_dir(lambda d, f: pl.BlockSpec((cb, 8, DH), lambda i: (f(i), 0, 0)))
    sts = per_dir(lambda d, f: pl.BlockSpec((cb, NH, DH, DH), lambda i: (f(i), 0, 0, 0)))
    dcd = per_dir(lambda d, f: pl.BlockSpec((cb, NH, DH), lambda i: (f(i), 0, 0)))
    return n, cb, nb, tok2, tok, mat, cds, sts, dcd


class _ScanRider:
    def __init__(self, af, bf, ar, br, shifted, tb, nb, up_spec, down_spec):
        s, c = af.shape
        self.shifted, self.t, self.c, self.nb = shifted, tb, c, nb
        self.args = [af, bf, ar, br]
        self.in_specs = [up_spec, up_spec, down_spec, down_spec]
        self.scratch = [pltpu.VMEM((16, c), F32)]
        if shifted:
            tb8 = tb // 8
            self.args += [af, ar]
            self.in_specs += [pl.BlockSpec((8, c), lambda i: (jnp.maximum(i * tb8 - 1, 0), 0)),
                              pl.BlockSpec((8, c), lambda i: (jnp.minimum((nb - i) * tb8, s // 8 - 1), 0))]
            self.scratch += [pltpu.VMEM((tb + 8, c), F32), pltpu.VMEM((tb + 8, c), F32)]
        self.out_specs = [up_spec, down_spec]
        self.out_shape = [_sds((s, c)), _sds((s, c))]

    def begin(self, in_refs, out_refs, scratch_refs):
        i = pl.program_id(0)
        self.carry = scratch_refs[0]

        @pl.when(i == 0)
        def _():
            self.carry[...] = jnp.zeros_like(self.carry)

        af_ref, self.bf_ref, ar_ref, self.br_ref = in_refs[0:4]
        self.hf_ref, self.hr_ref = out_refs
        self.a_up, self.a_dn = af_ref, ar_ref
        if self.shifted:
            t = self.t
            edge = jnp.where(i > 0, 1.0, 0.0).astype(F32)
            fbuf, rbuf = scratch_refs[1:3]
            fbuf[0:8, :] = in_refs[4][...] * edge
            fbuf[8:t + 8, :] = af_ref[...]
            rbuf[0:t, :] = ar_ref[...]
            rbuf[t:t + 8, :] = in_refs[5][...] * edge
            self.a_up, self.a_dn = fbuf, rbuf
        self.row = lax.broadcasted_iota(jnp.int32, (8, self.c), 0)
        self.cf, self.cr = self.carry[0:1, :], self.carry[8:9, :]

    def groups(self, lo, hi):
        ng = self.t // 8
        row = self.row
        for gi in range(lo, hi):
            rf, rr = 8 * gi, 8 * (ng - 1 - gi)
            if self.shifted:
                a_f = jnp.where(row > 0, pltpu.roll(self.a_up[rf + 8:rf + 16, :], 1, 0), pltpu.roll(self.a_up[rf:rf + 8, :], 1, 0))
                a_r = jnp.where(row < 7, pltpu.roll(self.a_dn[rr:rr + 8, :], 7, 0), pltpu.roll(self.a_dn[rr + 8:rr + 16, :], 7, 0))
            else:
                a_f, a_r = self.a_up[rf:rf + 8, :], self.a_dn[rr:rr + 8, :]
            a_f, b_f = _block_scan(a_f, self.bf_ref[rf:rf + 8, :], row, False)
            a_r, b_r = _block_scan(a_r, self.br_ref[rr:rr + 8, :], row, True)
            h_f = a_f * self.cf + b_f
            h_r = a_r * self.cr + b_r
            self.hf_ref[rf:rf + 8, :] = h_f
            self.hr_ref[rr:rr + 8, :] = h_r
            self.cf, self.cr = h_f[7:8, :], h_r[0:1, :]

    def end(self):
        self.carry[0:1, :] = self.cf
        self.carry[8:9, :] = self.cr


def _gdn_seq_fwd(u, w, qd, kd, at, cd, name, scan=None):
    s = u.shape[1]
    n, cb, nb, tok2, tok, mat, cds, sts, _ = _seq_specs(s, (True, False))
    rider = _ScanRider(*scan, False, cb * CHUNK, nb, tok[0], tok[1]) if scan else None
    ri = len(rider.args) if rider else 0

    def body(*refs):
        ins = (refs[0:6], refs[6:12])
        outs = (refs[12 + ri:15 + ri], refs[15 + ri:18 + ri])
        st = refs[18 + ri + (2 if rider else 0)]
        if rider:
            rider.begin(refs[12:12 + ri], refs[18 + ri:20 + ri], refs[21 + ri:])

        @pl.when(pl.program_id(0) == 0)
        def _():
            st[...] = jnp.zeros_like(st)

        for j in range(cb):
            items = []
            for d in range(2):
                jj = j if d == 0 else cb - 1 - j
                items += [(d, h, jj, slice(CHUNK * jj, CHUNK * (jj + 1)), _head(h)) for h in range(NH)]
            shs = [st[d, h] for d, h, _, _, _ in items]
            wss = [_dot(ins[d][1][0, rows, sl], sh) for (d, h, jj, rows, sl), sh in zip(items, shs)]
            vns = [ins[d][0][0, rows, sl] - ws for (d, h, jj, rows, sl), ws in zip(items, wss)]
            news = [sh * ins[d][5][jj, 4 * d + h:4 * d + h + 1, :] + _dot_tn(ins[d][3][0, rows, sl], vn)
                    for (d, h, jj, rows, sl), sh, vn in zip(items, shs, vns)]
            for (d, h, jj, rows, sl), sh, vn, new in zip(items, shs, vns, news):
                o_r, s_r, vn_r = outs[d]
                st[d, h] = new
                s_r[jj, h] = sh.astype(BF16)
                vn_r[rows, sl] = vn.astype(BF16)
                o_r[rows, sl] = _dot(ins[d][2][0, rows, sl], sh) + _dot(ins[d][4][jj, 0, h], vn)
            if rider:
                rider.groups(8 * j, 8 * (j + 1))
        if rider:
            rider.end()

    in_specs, out_specs, out_shape = [], [], []
    for d in range(2):
        in_specs += [tok2[d]] * 4 + [mat[d], cds[d]]
        out_specs += [tok[d], sts[d], tok[d]]
        out_shape += [_sds((s, HW)), _sds((n, NH, DH, DH), BF16), _sds((s, HW), BF16)]
    args = [u, w, qd, kd, at, cd, u, w, qd, kd, at, cd]
    scratch = [pltpu.VMEM((2, NH, DH, DH), F32)]
    if rider:
        in_specs, args = in_specs + rider.in_specs, args + rider.args
        out_specs, out_shape, scratch = out_specs + rider.out_specs, out_shape + rider.out_shape, scratch + rider.scratch
    return pl.pallas_call(
        body, name=name, grid=(nb,), in_specs=in_specs, out_specs=out_specs, out_shape=out_shape,
        scratch_shapes=scratch, compiler_params=_cparams(1),
    )(*args)


def _gdn_seq_bwd(do, w, qd, kd, at, cd, states, vns, name, scan=None):
    s = do.shape[0]
    n, cb, nb, tok2, tok, mat, cds, sts, dcd = _seq_specs(s, (False, True))
    rider = _ScanRider(*scan, True, cb * CHUNK, nb, tok[1], tok[0]) if scan else None
    ri = len(rider.args) if rider else 0

    def body(*refs):
        ins = (refs[0:8], refs[8:16])
        outs = (refs[16 + ri:21 + ri], refs[21 + ri:26 + ri])
        dst = refs[26 + ri + (2 if rider else 0)]
        if rider:
            rider.begin(refs[16:16 + ri], refs[26 + ri:28 + ri], refs[29 + ri:])

        @pl.when(pl.program_id(0) == 0)
        def _():
            dst[...] = jnp.zeros_like(dst)

        for j in range(cb):
            items = []
            for d in range(2):
                jj = cb - 1 - j if d == 0 else j
                items += [(d, h, jj, slice(CHUNK * jj, CHUNK * (jj + 1)), _head(h)) for h in range(NH)]
            dsns = [dst[d, h] for d, h, _, _, _ in items]
            dohs = [ins[d][0][rows, sl] for d, h, jj, rows, sl in items]
            d_vns = [_dot_tn(ins[d][4][jj, 0, h], doh) + _dot(ins[d][3][0, rows, sl], dsn)
                     for (d, h, jj, rows, sl), doh, dsn in zip(items, dohs, dsns)]
            news = [ins[d][5][jj, 4 * d + h:4 * d + h + 1, :] * dsn + _dot_tn(ins[d][2][0, rows, sl], doh)
                    - _dot_tn(ins[d][1][0, rows, sl], d_vn)
                    for (d, h, jj, rows, sl), doh, dsn, d_vn in zip(items, dohs, dsns, d_vns)]
            for (d, h, jj, rows, sl), doh, dsn, d_vn, new in zip(items, dohs, dsns, d_vns, news):
                dvn_r, dkd_r, dqd_r, dw_r, dcd_r = outs[d]
                sh = ins[d][6][jj, h].astype(F32)
                dst[d, h] = new
                dvn_r[rows, sl] = d_vn.astype(BF16)
                dkd_r[rows, sl] = _dot_nt(ins[d][7][rows, sl], dsn)
                dqd_r[rows, sl] = _dot_nt(doh, sh)
                dw_r[rows, sl] = (-_dot_nt(d_vn, sh)).astype(BF16)
                d_cd = jnp.sum(jnp.sum(sh * dsn, axis=1, keepdims=True), axis=0, keepdims=True)
                dcd_r[jj, h:h + 1, :] = jnp.broadcast_to(d_cd, (1, DH))
            if rider:
                rider.groups(8 * j, 8 * (j + 1))
        if rider:
            rider.end()

    in_specs, out_specs, out_shape, args = [], [], [], []
    for d in range(2):
        in_specs += [tok[d]] + [tok2[d]] * 3 + [mat[d], cds[d], sts[d], tok[d]]
        args += [do, w, qd, kd, at, cd, states[d], vns[d]]
        out_specs += [tok[d]] * 4 + [dcd[d]]
        out_shape += [_sds((s, HW), BF16), _sds((s, HW)), _sds((s, HW)), _sds((s, HW), BF16), _sds((n, NH, DH))]
    scratch = [pltpu.VMEM((2, NH, DH, DH), F32)]
    if rider:
        in_specs, args = in_specs + rider.in_specs, args + rider.args
        out_specs, out_shape, scratch = out_specs + rider.out_specs, out_shape + rider.out_shape, scratch + rider.scratch
    return pl.pallas_call(
        body, name=name, grid=(nb,), in_specs=in_specs, out_specs=out_specs, out_shape=out_shape,
        scratch_shapes=scratch, compiler_params=_cparams(1),
    )(*args)


def _gdn_local_bwd(q, k, v, bg, tmat, do, vns, seq_grads, name, comm=None):
    s = q.shape[0]
    n = s // CHUNK
    cb = min(LOCAL_CB, n)

    def body(*refs):
        q_ref, k_ref, v_ref, bg_ref, t_ref, do_ref = refs[0:6]
        vn_refs = refs[6:8]
        sg = (refs[8:13], refs[13:18])
        dq_ref, dk_ref, dv_ref, dbg_ref = refs[18:]
        lane = lax.broadcasted_iota(jnp.int32, (CHUNK, BAP), 1)
        rowi = lax.broadcasted_iota(jnp.int32, (CHUNK, 1), 0)
        ones = jnp.ones((CHUNK, DH), F32)
        masks = [_GdnMasks(d) for d in range(2)]
        inst = []
        for jj in range(cb):
            rows = slice(CHUNK * jj, CHUNK * (jj + 1))
            bgv = bg_ref[rows, :]
            qs = [q_ref[rows, _head(h)] for h in range(NH)]
            ks = [k_ref[rows, _head(h)] for h in range(NH)]
            kk = [_dot_nt(ks[h], ks[h]) for h in range(NH)]
            q0 = [_dot_nt(qs[h], ks[h]) for h in range(NH)]
            for d, m in enumerate(masks):
                gcs, gcs_t = _chunk_cumsums(m, bgv)
                for h in range(NH):
                    c = _GdnHead(qs[h], ks[h], v_ref[rows, _head(h)], kk[h], q0[h], bgv, gcs, gcs_t, d, h, m)
                    inst.append((jj, rows, d, h, m, c))
        ni = len(inst)
        cs = [it[-1] for it in inst]
        tms = [t_ref[jj, d, h] for jj, _, d, h, _, _ in inst]
        d_vns = [sg[d][0][rows, _head(h)] for _, rows, d, h, _, _ in inst]
        d_ws = [sg[d][3][rows, _head(h)] for _, rows, d, h, _, _ in inst]
        d_ts = [_dot_nt(d_vns[i], cs[i].vb) + _dot_nt(d_ws[i], cs[i].kg) for i in range(ni)]
        tts = [tm.T for tm in tms]
        xs = [_dot3(tts[i], d_ts[i]) for i in range(ni)]
        d_ls = [jnp.where(inst[i][4].strict, -_dot3(xs[i], tts[i]), 0.0) for i in range(ni)]
        d_attns = [jnp.where(m.incl, _dot_nt(do_ref[rows, _head(h)], vn_refs[d][rows, _head(h)]), 0.0)
                   for _, rows, d, h, m, _ in inst]
        d_vbs = [_dot(tts[i], d_vns[i]) for i in range(ni)]
        d_kgs = [_dot(tts[i], d_ws[i]) for i in range(ni)]
        d_a0s = [d_ls[i] * cs[i].decay for i in range(ni)]
        d_q0s = [d_attns[i] * cs[i].decay for i in range(ni)]
        es = [(d_ls[i] * cs[i].a0 + d_attns[i] * cs[i].q0) * cs[i].decay for i in range(ni)]
        kb_mm = [_dot(d_a0s[i], cs[i].k) for i in range(ni)]
        q_mm = [_dot(d_q0s[i], cs[i].k) for i in range(ni)]
        k_mm = [_dot_tn(d_a0s[i], cs[i].kb) + _dot_tn(d_q0s[i], cs[i].q) for i in range(ni)]
        e_cols = [_dot_exact(ones, es[i], _TN, False)[:, 0:1] for i in range(ni)]
        acc = {}
        d_gcs, d_betas = [], []
        for i, (jj, rows, d, h, m, c) in enumerate(inst):
            sl = _head(h)
            d_kd, d_qd = sg[d][1][rows, sl], sg[d][2][rows, sl]
            d_cd = sg[d][4][jj, h:h + 1, 0:1]
            d_vb, d_kg = d_vbs[i], d_kgs[i]
            d_kb = kb_mm[i] + d_kg * c.eg
            parts = (q_mm[i] + d_qd * c.eg, k_mm[i] + d_kd * c.ek + d_kb * c.beta, d_vb * c.beta)
            acc[jj, h] = [p + a for a, p in zip(acc[jj, h], parts)] if (jj, h) in acc else list(parts)
            kd_term = d_kd * c.kd
            d_gc = (jnp.sum(d_kg * c.kg + d_qd * c.qd - kd_term, axis=1, keepdims=True)
                    + jnp.sum(es[i], axis=1, keepdims=True) - e_cols[i])
            d_gl = jnp.sum(jnp.sum(kd_term, axis=0, keepdims=True), axis=1, keepdims=True) + d_cd * c.cd
            d_gcs.append(d_gc + jnp.where(rowi == m.last, d_gl, 0.0))
            d_betas.append(jnp.sum(d_kb * c.k + d_vb * c.v, axis=1, keepdims=True))
        d_gs = [_dot_exact(inst[i][4].tri, d_gcs[i] * ones, _TN, True)[:, 0:1] for i in range(ni)]
        dbg = [jnp.zeros((CHUNK, BAP), F32) for _ in range(cb)]
        for i, (jj, _, d, h, _, _) in enumerate(inst):
            dbg[jj] = dbg[jj] + jnp.where(lane == 4 * d + h, d_betas[i], 0.0) + jnp.where(lane == 8 + 4 * d + h, d_gs[i], 0.0)
        for jj in range(cb):
            rows = slice(CHUNK * jj, CHUNK * (jj + 1))
            for h in range(NH):
                dq_ref[rows, _head(h)], dk_ref[rows, _head(h)], dv_ref[rows, _head(h)] = acc[jj, h]
            dbg_ref[rows, :] = dbg[jj]

    tok = _rows(cb * CHUNK, HW)
    bgs = _rows(cb * CHUNK, BAP)
    mat = pl.BlockSpec((cb, 2, NH, CHUNK, CHUNK), lambda i: (i, 0, 0, 0, 0))
    dcd = pl.BlockSpec((cb, NH, DH), lambda i: (i, 0, 0))
    args = [q, k, v, bg, tmat, do, vns[0], vns[1]]
    in_specs = [tok, tok, tok, bgs, mat, tok, tok, tok]
    for d in range(2):
        args += list(seq_grads[d])
        in_specs += [tok] * 4 + [dcd]
    return _pallas(body, comm, name=name, grid=(n // cb,), in_specs=in_specs, out_specs=[tok, tok, tok, bgs],
                   out_shape=[_sds((s, HW))] * 3 + [_sds((s, BAP))], scratch_shapes=[], args=args)


def _prep_bwd(c_qkv, p_ba, alog_row, dtb_row, dq, dk, dv, dbg, name):
    s = c_qkv.shape[0]
    t = min(256, s)

    def body(cq_ref, pc_ref, alog_ref, dtb_ref, dq_ref, dk_ref, dv_ref, dbg_ref,
             dcq_ref, dpc_ref, dalog_ref, ddtb_ref):
        @pl.when(pl.program_id(0) == 0)
        def _():
            dalog_ref[...] = jnp.zeros_like(dalog_ref)
            ddtb_ref[...] = jnp.zeros_like(ddtb_ref)

        cq = cq_ref[...]
        sq = cq * _sig(cq)
        sg = _silu_grad(cq)
        for h in range(NH):
            sl = slice(DH * h, DH * (h + 1))
            for off, d_ref, scale in ((0, dq_ref, DH ** -0.5), (RGW, dk_ref, 1.0)):
                csl = slice(off + DH * h, off + DH * (h + 1))
                xh = sq[:, csl]
                nrm = lax.rsqrt(jnp.sum(xh * xh, axis=-1, keepdims=True) + EPS)
                y = xh * nrm
                dy = d_ref[:, sl] * scale
                dcq_ref[:, csl] = nrm * (dy - y * jnp.sum(dy * y, axis=-1, keepdims=True)) * sg[:, csl]
        dcq_ref[:, 2 * RGW:] = dv_ref[...] * sg[:, 2 * RGW:]
        pc = pc_ref[...]
        lane = lax.broadcasted_iota(jnp.int32, pc.shape, 1)
        dbg = dbg_ref[...]
        beta = _sig(pc)
        ea = jnp.exp(alog_ref[...])
        z = pc + dtb_ref[...]
        g = -ea * _softplus(z)
        is_g = jnp.logical_and(lane >= 8, lane < 16)
        d_alpha = jnp.where(is_g, dbg * (-ea) * _sig(z), 0.0)
        dpc_ref[...] = jnp.where(lane < 8, dbg * beta * (1.0 - beta), d_alpha).astype(BF16)
        dalog_ref[...] += _colsum(jnp.where(is_g, dbg * g, 0.0))
        ddtb_ref[...] += _colsum(d_alpha)

    return pl.pallas_call(
        body, name=name, grid=(s // t,),
        in_specs=[_rows(t, QKVW), _rows(t, BAP), _full((1, BAP)), _full((1, BAP))] + [_rows(t, HW)] * 3 + [_rows(t, BAP)],
        out_specs=[_rows(t, QKVW), _rows(t, BAP), _full((1, BAP)), _full((1, BAP))],
        out_shape=[_sds((s, QKVW)), _sds((s, BAP), BF16), _sds((1, BAP)), _sds((1, BAP))],
        compiler_params=_cparams(1),
    )(c_qkv, p_ba, alog_row, dtb_row, dq, dk, dv, dbg)


def _mix_out_values(hf, hb, gate, of, ob, z, gn):
    hr = hf + hb
    y_rg = hr * _gelu(gate)
    osum = of + ob
    parts = []
    for h in range(NH):
        sl = slice(DH * h, DH * (h + 1))
        oh = osum[:, sl]
        r, ohat = _rms(oh)
        zh = z[:, sl]
        parts.append((r, ohat, zh))
    y_gdn = jnp.concatenate([ohat * gn * (zh * _sig(zh)) for (r, ohat, zh) in parts], axis=1)
    return hr, y_rg, y_gdn, parts


def _outproj(x1, hf, hb, gate, of, ob, z, gn, wout, name):
    s = x1.shape[0]
    t = min(256, s)

    def body(x_ref, hf_ref, hb_ref, gate_ref, of_ref, ob_ref, z_ref, gn_ref, w_ref, xo_ref, y_ref):
        _, y_rg, y_gdn, _ = _mix_out_values(hf_ref[...], hb_ref[...], gate_ref[...], of_ref[...], ob_ref[...],
                                            z_ref[...], gn_ref[...])
        y = jnp.concatenate([y_rg, y_gdn], axis=1).astype(BF16)
        y_ref[...] = y
        xo_ref[...] = x_ref[...] + jnp.dot(y, w_ref[...], preferred_element_type=F32)

    return pl.pallas_call(
        body, name=name, grid=(s // t,),
        in_specs=[_rows(t, D)] + [_rows(t, RGW)] * 6 + [_full((1, DH)), _full((D, D))],
        out_specs=[_rows(t, D), _rows(t, D)], out_shape=[_sds((s, D)), _sds((s, D), BF16)],
        compiler_params=_cparams(1),
    )(x1, hf, hb, gate, of, ob, z, gn, wout)


def _outproj_bwd(dx2, hf, hb, gate, of, ob, z, gn, wout, name, comm=None):
    s = dx2.shape[0]
    t = min(256, s)

    def body(d_ref, hf_ref, hb_ref, gate_ref, of_ref, ob_ref, z_ref, gn_ref, w_ref,
             dhr_ref, dgate_ref, dos_ref, dz_ref, dgn_ref, db_ref):
        @pl.when(pl.program_id(0) == 0)
        def _():
            dgn_ref[...] = jnp.zeros_like(dgn_ref)

        gate = gate_ref[...]
        gn_v = gn_ref[...]
        hr, _, _, parts = _mix_out_values(hf_ref[...], hb_ref[...], gate, of_ref[...], ob_ref[...], z_ref[...], gn_v)
        dbf = d_ref[...].astype(BF16)
        db_ref[...] = dbf
        dy = _dot_nt(dbf, w_ref[...])
        dyr = dy[:, :RGW]
        dhr_ref[...] = dyr * _gelu(gate)
        dgate_ref[...] = (dyr * hr * _gelu_grad(gate)).astype(BF16)
        dgn = jnp.zeros((1, DH), F32)
        for h, (r, ohat, zh) in enumerate(parts):
            sl = slice(DH * h, DH * (h + 1))
            dyh = dy[:, RGW + DH * h:RGW + DH * (h + 1)]
            sz = zh * _sig(zh)
            dn = dyh * sz
            dz_ref[:, sl] = (dyh * ohat * gn_v * _silu_grad(zh)).astype(BF16)
            dgn = dgn + _colsum(dn * ohat)
            dos_ref[:, sl] = _rms_bwd(dn, ohat, r, gn_v).astype(BF16)
        dgn_ref[...] += dgn

    return _pallas(
        body, comm, name=name, grid=(s // t,),
        in_specs=[_rows(t, D)] + [_rows(t, RGW)] * 6 + [_full((1, DH)), _full((D, D))],
        out_specs=[_rows(t, RGW)] * 4 + [_full((1, DH)), _rows(t, D)],
        out_shape=[_sds((s, RGW))] + [_sds((s, RGW), BF16)] * 3 + [_sds((1, DH)), _sds((s, D), BF16)],
        scratch_shapes=[], args=(dx2, hf, hb, gate, of, ob, z, gn, wout))


def _loss_head(x3, target, gain, name):
    s = x3.shape[0]
    t = min(256, s)

    def body(x_ref, t_ref, g_ref, dx_ref, dxh_ref, loss_ref, dg_ref):
        @pl.when(pl.program_id(0) == 0)
        def _():
            loss_ref[...] = jnp.zeros_like(loss_ref)
            dg_ref[...] = jnp.zeros_like(dg_ref)

        r, xh = _rms(x_ref[...])
        gv = g_ref[...]
        err = xh * gv - t_ref[...]
        per_tok = jnp.mean(err * err, axis=-1, keepdims=True)
        loss_ref[...] += 0.5 * jnp.sum(per_tok, axis=0, keepdims=True)
        dy = err * (1.0 / D)
        dg_ref[...] += _colsum(dy * xh)
        dx = _rms_bwd(dy, xh, r, gv)
        dx_ref[...] = dx
        dxh_ref[...] = (0.5 * dx).astype(BF16)

    return pl.pallas_call(
        body, name=name, grid=(s // t,), in_specs=[_rows(t, D), _rows(t, D), _full((1, D))],
        out_specs=[_rows(t, D), _rows(t, D), _full((8, 128)), _full((1, D))],
        out_shape=[_sds((s, D)), _sds((s, D), BF16), _sds((8, 128)), _sds((1, D))], compiler_params=_cparams(1),
    )(x3, target, gain)


def _adamw_math(wv, gv, mv, vv):
    mn = ADAM_B1 * mv + (1.0 - ADAM_B1) * gv
    vn = ADAM_B2 * vv + (1.0 - ADAM_B2) * (gv * gv)
    m_hat = mn / (1.0 - ADAM_B1 ** ADAM_STEP)
    v_hat = vn / (1.0 - ADAM_B2 ** ADAM_STEP)
    return -ADAM_LR * (m_hat / (jnp.sqrt(v_hat) + ADAM_EPS) + ADAM_WD * wv), mn, vn


def _row_tile(r, c):
    tr = r
    while tr * c * 4 > (1 << 20) and tr % 16 == 0:
        tr //= 2
    return tr


def _adamw(w, g, m, v, name):
    r, c = w.shape
    tr = _row_tile(r, c)

    def body(w_ref, g_ref, m_ref, v_ref, d_ref, nm_ref, nv_ref):
        d_ref[...], nm_ref[...], nv_ref[...] = _adamw_math(w_ref[...], g_ref[...], m_ref[...], v_ref[...])

    return pl.pallas_call(
        body, name=name, grid=(r // tr,), in_specs=[_rows(tr, c)] * 4, out_specs=[_rows(tr, c)] * 3,
        out_shape=[_sds((r, c))] * 3, compiler_params=_cparams(1),
    )(w, g, m, v)


def _adamw_halves(w, own, recv, m, v, c_arr, name):
    r, c = w.shape
    h = r // 2
    tr = _row_tile(h, c)
    nh = h // tr

    def body(c_ref, w_ref, own_ref, recv_ref, m_ref, v_ref, g_ref, d_ref, nm_ref, nv_ref):
        first_half = pl.program_id(0) < nh
        use_own = first_half == (c_ref[0] == 0)
        gv = jnp.where(use_own, own_ref[...], recv_ref[...])
        g_ref[...] = gv
        d_ref[...], nm_ref[...], nv_ref[...] = _adamw_math(w_ref[...], gv, m_ref[...], v_ref[...])

    full = pl.BlockSpec((tr, c), lambda i, c_ref: (i, 0))
    half = pl.BlockSpec((tr, c), lambda i, c_ref: (i % nh, 0))
    return pl.pallas_call(
        body, name=name, out_shape=[_sds((r, c))] * 4,
        grid_spec=pltpu.PrefetchScalarGridSpec(
            num_scalar_prefetch=1, grid=(2 * nh,), in_specs=[full, half, half, full, full], out_specs=[full] * 4),
        compiler_params=_cparams(1),
    )(c_arr, w, own, recv, m, v)


def _mesh_pos():
    return lax.axis_index("x"), lax.axis_index("y"), lax.axis_index("c")


def _other_chips(x, y):
    return [(1 - x, y), (x, 1 - y), (1 - x, 1 - y)]


class _Comm:
    def __init__(self, inputs, out_shapes, scratch, start, finish, space=pltpu.HBM):
        self.inputs, self.out_shapes, self.scratch = list(inputs), list(out_shapes), list(scratch)
        self.start, self.finish, self.space = start, finish, space


def _comm_call(comm, name):
    ni, no = len(comm.inputs), len(comm.out_shapes)

    def body(*refs):
        comm.start(refs[:ni], refs[ni:ni + no], refs[ni + no:])
        comm.finish(refs[:ni], refs[ni:ni + no], refs[ni + no:])

    spec = pl.BlockSpec(memory_space=comm.space)
    return list(pl.pallas_call(body, name=name, out_shape=comm.out_shapes, in_specs=[spec] * ni, out_specs=[spec] * no,
                               scratch_shapes=comm.scratch)(*comm.inputs))


def _join_comm(a, b):
    ia, oa, sa = len(a.inputs), len(a.out_shapes), len(a.scratch)

    def both(method):
        def run(ins, outs, sems):
            getattr(a, method)(ins[:ia], outs[:oa], sems[:sa])
            getattr(b, method)(ins[ia:], outs[oa:], sems[sa:])
        return run

    return _Comm(a.inputs + b.inputs, a.out_shapes + b.out_shapes, a.scratch + b.scratch, both("start"), both("finish"))


def _pallas(body, comm, *, name, grid, in_specs, out_specs, out_shape, scratch_shapes, args):
    params = _cparams(len(grid))
    if comm is None:
        outs = pl.pallas_call(body, name=name, grid=grid, in_specs=in_specs, out_specs=out_specs, out_shape=out_shape,
                              scratch_shapes=scratch_shapes, compiler_params=params)(*args)
        return list(outs), []
    n_in, n_out, n_sc = len(in_specs), len(out_specs), len(scratch_shapes)
    ci, co = len(comm.inputs), len(comm.out_shapes)

    def carried(*refs):
        bounds = [0, n_in, n_in + ci, n_in + ci + n_out, n_in + ci + n_out + co, n_in + ci + n_out + co + n_sc, len(refs)]
        ins, cins, outs, couts, scr, csems = [refs[lo:hi] for lo, hi in zip(bounds[:-1], bounds[1:])]
        ids = [pl.program_id(k) for k in range(len(grid))]
        first = functools.reduce(jnp.logical_and, [i == 0 for i in ids])
        last = functools.reduce(jnp.logical_and, [i == g - 1 for i, g in zip(ids, grid)])

        @pl.when(first)
        def _():
            comm.start(cins, couts, csems)

        body(*ins, *outs, *scr)

        @pl.when(last)
        def _():
            comm.finish(cins, couts, csems)

    hbm = pl.BlockSpec(memory_space=pltpu.HBM)
    outs = pl.pallas_call(
        carried, name=name, grid=grid, in_specs=list(in_specs) + [hbm] * ci, out_specs=list(out_specs) + [hbm] * co,
        out_shape=list(out_shape) + comm.out_shapes, scratch_shapes=list(scratch_shapes) + comm.scratch,
        compiler_params=params)(*args, *comm.inputs)
    return list(outs[:n_out]), list(outs[n_out:])


def _gather_comm(arrays, space, block_rows):
    n_arr = len(arrays)

    def plan(x_refs, out_refs, sems):
        send_sems, recv_sems, local_sems = sems
        x, y, c = _mesh_pos()
        me, sibling = (x, y, c), (x, y, 1 - c)
        chips = _other_chips(x, y)

        def slot(a, px, py, pc):
            return out_refs[a].at[4 * px + 2 * py + pc]

        def copy(a, k, block, to, src=None):
            return pltpu.make_async_remote_copy(
                src_ref=slot(a, *block) if src is None else src, dst_ref=slot(a, *block),
                send_sem=send_sems.at[7 * a + k], recv_sem=recv_sems.at[7 * a + k], device_id=to, device_id_type=MESH)

        srcs = [x_refs[a] if block_rows[a] is None else
                x_refs[a].at[pl.ds(pl.multiple_of(c * block_rows[a], 16), block_rows[a]), :] for a in range(n_arr)]
        local = [pltpu.make_async_copy(srcs[a], slot(a, *me), local_sems.at[a]) for a in range(n_arr)]
        first = []
        for a in range(n_arr):
            first += [copy(a, 1 + j, me, (*chip, c), src=srcs[a]) for j, chip in enumerate(chips)]
            first.append(copy(a, 0, me, sibling, src=srcs[a]))
        return me, sibling, chips, c, copy, local, first

    def start(x_refs, out_refs, sems):
        _, _, _, _, _, local, first = plan(x_refs, out_refs, sems)
        for cp in local + first:
            cp.start()

    def finish(x_refs, out_refs, sems):
        me, sibling, chips, c, copy, local, first = plan(x_refs, out_refs, sems)
        passed = []
        for j, chip in enumerate(chips):
            for a in range(n_arr):
                copy(a, 1 + j, (*chip, c), me).wait_recv()
                fwd = copy(a, 4 + j, (*chip, c), sibling)
                fwd.start()
                passed.append(fwd)
        for a in range(n_arr):
            copy(a, 0, sibling, me).wait_recv()
            for j, chip in enumerate(chips):
                copy(a, 4 + j, (*chip, 1 - c), me).wait_recv()
        for cp in first + passed:
            cp.wait_send()
        for cp in local:
            cp.wait()

    out_shapes = [_sds((8, w.shape[0] if r is None else r) + w.shape[1:], w.dtype) for w, r in zip(arrays, block_rows)]
    scratch = [pltpu.SemaphoreType.DMA((7 * n_arr,)), pltpu.SemaphoreType.DMA((7 * n_arr,)), pltpu.SemaphoreType.DMA((n_arr,))]
    return _Comm(arrays, out_shapes, scratch, start, finish, space)


def _weights_gather_comm(shards):
    return _gather_comm(shards, pltpu.HBM, [w.shape[0] // 2 for w in shards])


def _all_shards(gathered):
    return [o.reshape(NSH, 2 * o.shape[1], o.shape[2]) for o in gathered]


def _gather_small(block, name):
    return _comm_call(_gather_comm([block], pltpu.VMEM, [None]), name)[0]


def _exchange_comm(gs):
    n = len(gs)
    halves = [g.shape[1] // 2 for g in gs]

    def plan(g_refs, land_refs, sems):
        send_sems, recv_sems = sems
        x, y, c = _mesh_pos()
        copies = []
        for a in range(n):
            h = halves[a]
            for s in range(NSH):
                copies.append(pltpu.make_async_remote_copy(
                    src_ref=g_refs[a].at[s, pl.ds(pl.multiple_of((1 - c) * h, 8), h), :], dst_ref=land_refs[a].at[s],
                    send_sem=send_sems.at[NSH * a + s], recv_sem=recv_sems.at[NSH * a + s],
                    device_id=(x, y, 1 - c), device_id_type=MESH))
        return copies

    def start(g_refs, land_refs, sems):
        for cp in plan(g_refs, land_refs, sems):
            cp.start()

    def finish(g_refs, land_refs, sems):
        for cp in plan(g_refs, land_refs, sems):
            cp.wait()

    scratch = [pltpu.SemaphoreType.DMA((NSH * n,)), pltpu.SemaphoreType.DMA((NSH * n,))]
    return _Comm(gs, [_sds((NSH, h, g.shape[2])) for h, g in zip(halves, gs)], scratch, start, finish)


def _chip_sum(g, land, c_arr, name):
    _, h, cols = land.shape

    def body(c_ref, g_ref, l_ref, o_ref):
        o_ref[...] = (g_ref[...] + l_ref[...]).astype(BF16)

    return pl.pallas_call(
        body, name=name, out_shape=_sds((NSH, h, cols), BF16),
        grid_spec=pltpu.PrefetchScalarGridSpec(
            num_scalar_prefetch=1, grid=(NSH,),
            in_specs=[pl.BlockSpec((1, h, cols), lambda s, c_ref: (s, c_ref[0], 0)),
                      pl.BlockSpec((1, h, cols), lambda s, c_ref: (s, 0, 0))],
            out_specs=pl.BlockSpec((1, h, cols), lambda s, c_ref: (s, 0, 0))),
        compiler_params=_cparams(1),
    )(c_arr, g, land)


def _scatter_comm(parts):
    n = len(parts)

    def plan(p_refs, land_refs, sems):
        send_sems, recv_sems, local_sems = sems
        x, y, c = _mesh_pos()
        my_chip = 2 * x + y
        local = [pltpu.make_async_copy(p_refs[a].at[my_chip], land_refs[a].at[my_chip], local_sems.at[a]) for a in range(n)]
        copies = []
        for a in range(n):
            for j, (px, py) in enumerate(_other_chips(x, y)):
                copies.append(pltpu.make_async_remote_copy(
                    src_ref=p_refs[a].at[2 * px + py], dst_ref=land_refs[a].at[my_chip],
                    send_sem=send_sems.at[3 * a + j], recv_sem=recv_sems.at[3 * a + j],
                    device_id=(px, py, c), device_id_type=MESH))
        return local, copies

    def start(p_refs, land_refs, sems):
        local, copies = plan(p_refs, land_refs, sems)
        for cp in local + copies:
            cp.start()

    def finish(p_refs, land_refs, sems):
        local, copies = plan(p_refs, land_refs, sems)
        for cp in copies:
            cp.wait()
        for cp in local:
            cp.wait()

    scratch = [pltpu.SemaphoreType.DMA((3 * n,)), pltpu.SemaphoreType.DMA((3 * n,)), pltpu.SemaphoreType.DMA((n,))]
    return _Comm(parts, [_sds(p.shape, BF16) for p in parts], scratch, start, finish)


def _sum_slots(land, name):
    k, r, c = land.shape
    tr = r // 2 if r % 32 == 0 else r

    def body(l_ref, o_ref):
        acc = l_ref[0].astype(F32)
        for i in range(1, k):
            acc = acc + l_ref[i].astype(F32)
        o_ref[...] = acc

    return pl.pallas_call(
        body, name=name, grid=(r // tr,), in_specs=[pl.BlockSpec((k, tr, c), lambda i: (0, i, 0))],
        out_specs=_rows(tr, c), out_shape=_sds((r, c)), compiler_params=_cparams(1),
    )(land)


def _sibling_swap(halves):
    n = len(halves)

    def body(*refs):
        h_refs, out_refs = refs[:n], refs[n:2 * n]
        send_sems, recv_sems = refs[2 * n:]
        x, y, c = _mesh_pos()
        copies = [pltpu.make_async_remote_copy(
            src_ref=h_refs[a], dst_ref=out_refs[a], send_sem=send_sems.at[a], recv_sem=recv_sems.at[a],
            device_id=(x, y, 1 - c), device_id_type=MESH) for a in range(n)]
        for cp in copies:
            cp.start()
        for cp in copies:
            cp.wait()

    return pl.pallas_call(
        body, name="grad_sibling_swap", out_shape=[_sds(h.shape) for h in halves],
        in_specs=[pl.BlockSpec(memory_space=pltpu.HBM)] * n, out_specs=[pl.BlockSpec(memory_space=pltpu.HBM)] * n,
        scratch_shapes=[pltpu.SemaphoreType.DMA((n,)), pltpu.SemaphoreType.DMA((n,))],
    )(*halves)


def _pad_rows(v, width):
    flat = v.reshape(-1)
    rows = -(-flat.shape[0] // width)
    rows = -(-rows // 8) * 8
    return jnp.pad(flat, (0, rows * width - flat.shape[0])).reshape(rows, width)


def _size(shape):
    n = 1
    for dim in shape:
        n *= dim
    return n


def _row_pack(arrs):
    pieces = []
    for a in arrs:
        rows = -(-a.size // D)
        pieces.append(jnp.pad(a.reshape(-1), (0, rows * D - a.size)).reshape(rows, D))
    total = sum(p.shape[0] for p in pieces)
    if total % 8:
        pieces.append(jnp.zeros((8 - total % 8, D), F32))
    return jnp.concatenate(pieces, axis=0)


def _row_unpack(packed, shapes):
    out, r0 = [], 0
    for shp in shapes:
        n = _size(shp)
        rows = -(-n // D)
        out.append(packed[r0:r0 + rows].reshape(-1)[:n].reshape(shp))
        r0 += rows
    return out


def _block_diag(w):
    eye = jnp.eye(8, dtype=w.dtype)
    return (w[:, :, None, :] * eye[:, None, :, None]).reshape(RGW, RGW)


def _diag_blocks(dense):
    r = dense.reshape(8, 64, 8, 64)
    return jnp.stack([r[n, :, n, :] for n in range(8)])


def _lane_row(v8):
    return jnp.zeros((1, BAP), F32).at[0, 8:16].set(v8.reshape(8))


def _chip_sums(gs, lands, names, c_arr):
    return [_chip_sum(g, l, c_arr, "chip_sum_" + n) for g, l, n in zip(gs, lands, names)]


def _reduce_parts(gs, names, c_arr, tag):
    return _chip_sums(gs, _comm_call(_exchange_comm(gs), "grad_sibling_exchange_" + tag), names, c_arr)


def _local_step(x, target, sw, ffn1_w, later_shards, c_arr):
    (g1, gmix, rg_cw8, rg_cb, wgates, gbias, lam_row, gdn_cw8, alog_row, dtb_row, gn, g2, gfin) = sw
    wg1, wu1, wd1 = ffn1_w

    (x1, a1, b1, fb1), gathered = _ffn_fwd(x, g1, wg1, wu1, wd1, "ffn1_fwd", comm=_weights_gather_comm(later_shards))
    win_sh, wout_sh, wg2, wu2, wd2 = _all_shards(gathered)
    w_in_full = jnp.transpose(win_sh, (1, 0, 2)).reshape(D, NSH * INSH)
    wout = wout_sh.reshape(D, D)
    w_in_groups = (w_in_full[:, 0:512], w_in_full[:, 512:1024], w_in_full[:, 1024:2560], w_in_full[:, 2560:3072],
                   jnp.pad(w_in_full[:, 3072:3088], ((0, 0), (0, BAP - BAW))))
    h2, p_rgx, p_gate, p_qkv, p_z, p_ba = _inproj(x1, gmix, w_in_groups, "in_proj")
    c_rg = _conv(p_rgx, rg_cw8, rg_cb, "rg_conv")
    c_qkv = _conv(p_qkv, gdn_cw8, jnp.zeros((1, QKVW), F32), "gdn_conv")
    a0, bb0, a1s, bb1, q, k, v, bg = _mix_prep(c_rg, c_qkv, p_ba, wgates, gbias, lam_row, alog_row, dtb_row, "mix_prep")
    tmat, gu, gw, gqd, gkd, gat, gcd = _gdn_local_fwd(q, k, v, bg, "gdn_local_fwd")
    of, s0, vn0, ob, s1, vn1, hf, hb = _gdn_seq_fwd(gu, gw, gqd, gkd, gat, gcd, "gdn_seq_fwd", scan=(a0, bb0, a1s, bb1))
    x2, ymix = _outproj(x1, hf, hb, p_gate, of, ob, p_z, gn, wout, "out_proj")
    (x3, a2, b2, fb2), _ = _ffn_fwd(x2, g2, wg2, wu2, wd2, "ffn2_fwd")
    dx3, dob2, loss_blk, d_gfin = _loss_head(x3, target, gfin, "loss_head")

    dx2, d_g2, hb2, dab2, dbb2, _ = _ffn_bwd(x2, dx3, dob2, g2, a2, b2, wg2, wu2, wd2, "ffn2_bwd")
    d_ffn2 = [_tn(dab2, hb2, "ffn2_dwg"), _tn(dbb2, hb2, "ffn2_dwu"), _tn(fb2, dob2, "ffn2_dwd")]

    (d_hr, d_gate, d_os, d_z, d_gn, dx2b), lands = _outproj_bwd(dx2, hf, hb, p_gate, of, ob, p_z, gn, wout, "out_proj_bwd",
                                                               comm=_exchange_comm(d_ffn2))
    parts_ffn2 = _chip_sums(d_ffn2, lands, _BIG_NAMES[5:8], c_arr)
    d_wout = _tn(ymix, dx2b, "dw_out")[0]

    sg = _gdn_seq_bwd(d_os, gw, gqd, gkd, gat, gcd, (s0, s1), (vn0, vn1), "gdn_seq_bwd", scan=(a1s, d_hr, a0, d_hr))
    lam1, lam0 = sg[10:12]
    d_xc, d_pre, xcb, d_gbias, d_lam = _gates_bwd(c_rg, wgates, gbias, lam_row, lam0, lam1, hf, hb, "rg_gates_bwd")
    d_wgates = _tn(xcb, d_pre, "dw_gates")[0]
    d_prgx, d_rgcw8, d_rgcb = _conv_bwd(p_rgx, d_xc, rg_cw8, "rg_conv_bwd")

    (dq, dk, dv, dbg), lands_ffn2 = _gdn_local_bwd(q, k, v, bg, tmat, d_os, (vn0, vn1), (sg[0:5], sg[5:10]), "gdn_local_bwd",
                                                  comm=_scatter_comm(parts_ffn2))
    d_cqkv, d_pba, d_alog, d_dtb = _prep_bwd(c_qkv, p_ba, alog_row, dtb_row, dq, dk, dv, dbg, "gdn_prep_bwd")
    d_pqkv, d_gdncw8, _ = _conv_bwd(p_qkv, d_cqkv, gdn_cw8, "gdn_conv_bwd")

    dps = (d_prgx, d_gate, d_pqkv, d_z, d_pba)
    dx1, dob1, d_gmix = _inproj_bwd(x1, dx2, gmix, dps, w_in_groups, "in_proj_bwd")
    d_win_groups = [_tn(h2, dp, "dw_in_%d" % i)[0] for i, dp in enumerate(dps)]
    d_win = jnp.concatenate(d_win_groups[:4] + [d_win_groups[4][:, :BAW]], axis=1)
    d_mix = [jnp.transpose(d_win.reshape(D, NSH, INSH), (1, 0, 2)), d_wout.reshape(NSH, OUTSH, D)]

    small = dict(
        mix_norm=d_gmix, rg_conv_w=d_rgcw8[:4], rg_conv_b=d_rgcb,
        rg_gate_a_w=jnp.stack([_diag_blocks(d_wgates[:, RGW * i:RGW * (i + 1)]) for i in (0, 1)]),
        rg_gate_x_w=jnp.stack([_diag_blocks(d_wgates[:, RGW * i:RGW * (i + 1)]) for i in (2, 3)]),
        rg_gate_a_b=d_gbias[0, :2 * RGW].reshape(2, RGW), rg_gate_x_b=d_gbias[0, 2 * RGW:].reshape(2, RGW),
        rg_lambda=d_lam.reshape(2, RGW), gdn_conv_w=d_gdncw8[:4],
        gdn_a_log=d_alog[0, 8:16].reshape(2, NH), gdn_dt_bias=d_dtb[0, 8:16].reshape(2, NH),
        gdn_norm=d_gn, ffn2_norm=d_g2, final_norm=d_gfin)
    small_pack = _row_pack([small[n] for n in _SMALL_NAMES[1:]])

    riders = _join_comm(_exchange_comm(d_mix), _gather_comm([small_pack], pltpu.HBM, [None]))
    gx, d_g1, hb1, dab1, dbb1, carried = _ffn_bwd(x, dx1, dob1, g1, a1, b1, wg1, wu1, wd1, "ffn1_bwd", comm=riders)
    parts_mix = _chip_sums(d_mix, carried[0:2], _BIG_NAMES[3:5], c_arr)
    d_wg1, lands_mix = _tn(dab1, hb1, "ffn1_dwg", comm=_scatter_comm(parts_mix))
    parts_wg1 = _reduce_parts([d_wg1], _BIG_NAMES[0:1], c_arr, "ffn1_gate")
    d_wu1, lands_wg1 = _tn(dbb1, hb1, "ffn1_dwu", comm=_scatter_comm(parts_wg1))
    parts_wu1 = _reduce_parts([d_wu1], _BIG_NAMES[1:2], c_arr, "ffn1_up")
    d_wd1, lands_wu1 = _tn(fb1, dob1, "ffn1_dwd", comm=_scatter_comm(parts_wu1))
    parts_wd1 = _reduce_parts([d_wd1], _BIG_NAMES[2:3], c_arr, "ffn1_down")
    lands_ffn1 = lands_wg1 + lands_wu1 + _comm_call(_scatter_comm(parts_wd1), "grad_chip_scatter_ffn1_down")

    halves = [_sum_slots(l, "sum_chips_" + n) for l, n in zip(lands_ffn1 + lands_mix + lands_ffn2, _BIG_NAMES)]
    small_shapes = [small[n].shape for n in _SMALL_NAMES[1:]]
    return loss_blk, gx, halves, d_g1, carried[2], small_shapes


_SMALL_NAMES = ("ffn1_norm", "mix_norm", "rg_conv_w", "rg_conv_b", "rg_gate_a_w", "rg_gate_a_b", "rg_gate_x_w",
                "rg_gate_x_b", "rg_lambda", "gdn_conv_w", "gdn_a_log", "gdn_dt_bias", "gdn_norm", "ffn2_norm", "final_norm")
_SMALL_SHARDED = dict(rg_conv_w=128, rg_gate_a_b=128, rg_gate_x_b=128, rg_lambda=128, gdn_conv_w=384)
_OUT_ORDER = ("ffn1_norm", "ffn1_w_gate", "ffn1_w_up", "ffn1_w_down", "mix_norm", "w_in", "w_out", "rg_conv_w", "rg_conv_b",
              "rg_gate_a_w", "rg_gate_a_b", "rg_gate_x_w", "rg_gate_x_b", "rg_lambda", "gdn_conv_w", "gdn_a_log",
              "gdn_dt_bias", "gdn_norm", "ffn2_norm", "ffn2_w_gate", "ffn2_w_up", "ffn2_w_down", "final_norm")
_BIG_NAMES = ("ffn1_w_gate", "ffn1_w_up", "ffn1_w_down", "w_in", "w_out", "ffn2_w_gate", "ffn2_w_up", "ffn2_w_down")
_TRANSPOSED = ("ffn1_w_gate", "ffn1_w_up", "ffn2_w_gate", "ffn2_w_up")


def kernel(x, ffn1_norm, ffn1_w_gate, ffn1_w_up, ffn1_w_down, mix_norm, w_in, w_out, rg_conv_w, rg_conv_b, rg_gate_a_w, rg_gate_a_b, rg_gate_x_w, rg_gate_x_b, rg_lambda, gdn_conv_w, gdn_a_log, gdn_dt_bias, gdn_norm, ffn2_norm, ffn2_w_gate, ffn2_w_up, ffn2_w_down, final_norm, loss_target, m_ffn1_norm, m_ffn1_w_gate, m_ffn1_w_up, m_ffn1_w_down, m_mix_norm, m_w_in, m_w_out, m_rg_conv_w, m_rg_conv_b, m_rg_gate_a_w, m_rg_gate_a_b, m_rg_gate_x_w, m_rg_gate_x_b, m_rg_lambda, m_gdn_conv_w, m_gdn_a_log, m_gdn_dt_bias, m_gdn_norm, m_ffn2_norm, m_ffn2_w_gate, m_ffn2_w_up, m_ffn2_w_down, m_final_norm, v_ffn1_norm, v_ffn1_w_gate, v_ffn1_w_up, v_ffn1_w_down, v_mix_norm, v_w_in, v_w_out, v_rg_conv_w, v_rg_conv_b, v_rg_gate_a_w, v_rg_gate_a_b, v_rg_gate_x_w, v_rg_gate_x_b, v_rg_lambda, v_gdn_conv_w, v_gdn_a_log, v_gdn_dt_bias, v_gdn_norm, v_ffn2_norm, v_ffn2_w_gate, v_ffn2_w_up, v_ffn2_w_down, v_final_norm):
    args = dict(locals())
    w = {n: args[n] for n in _OUT_ORDER}
    mom = {n: args["m_" + n] for n in _OUT_ORDER}
    var = {n: args["v_" + n] for n in _OUT_ORDER}
    xi, yi, ci = _mesh_pos()
    shard = 2 * xi + yi

    big_bf16 = [w[n][0].astype(BF16) for n in _BIG_NAMES]
    sm_local = _pad_rows(jnp.concatenate([w[n][0].reshape(-1) for n in _SMALL_SHARDED]), 128)
    first = _comm_call(_gather_comm(big_bf16[0:3] + [sm_local], pltpu.HBM, [t.shape[0] // 2 for t in big_bf16[0:3]] + [None]),
                       "gather_first_weights")
    ffn1_w = _all_shards(first[0:3])
    sm_all = first[3][0::2].reshape(NSH, -1)
    sm_full, off = {}, 0
    for n, wd_ in _SMALL_SHARDED.items():
        rows = w[n].shape[1]
        piece = sm_all[:, off:off + rows * wd_].reshape(NSH, rows, wd_)
        sm_full[n] = jnp.transpose(piece, (1, 0, 2)).reshape(rows, NSH * wd_)
        off += rows * wd_

    wa, wx = rg_gate_a_w[0], rg_gate_x_w[0]
    wgates = jnp.concatenate([_block_diag(wa[0]), _block_diag(wa[1]), _block_diag(wx[0]), _block_diag(wx[1])],
                             axis=1).astype(BF16)
    gbias = jnp.concatenate([sm_full["rg_gate_a_b"].reshape(1, -1), sm_full["rg_gate_x_b"].reshape(1, -1)], axis=1)
    sw = (ffn1_norm, mix_norm, jnp.pad(sm_full["rg_conv_w"], ((0, 4), (0, 0))), rg_conv_b, wgates, gbias,
          sm_full["rg_lambda"].reshape(1, -1), jnp.pad(sm_full["gdn_conv_w"], ((0, 4), (0, 0))), _lane_row(gdn_a_log),
          _lane_row(gdn_dt_bias), gdn_norm, ffn2_norm, final_norm.reshape(1, D))
    c_arr = ci.reshape(1).astype(jnp.int32)

    loss_blk, gx, halves, d_g1, small_packs, small_shapes = _local_step(x[0], loss_target[0], sw, ffn1_w, big_bf16[3:], c_arr)
    loss = lax.psum(loss_blk[0, 0], ("x", "y", "c"))
    grads = {}

    g1_all = _gather_small(jnp.pad(d_g1, ((0, 7), (0, 0))), "gather_ffn1_norm_grad")
    sm_sums = [_sum_slots(g1_all, "ffn1_norm_grad_sum")[0:1]] + _row_unpack(_sum_slots(small_packs, "small_grad_sum"), small_shapes)
    for n, g in zip(_SMALL_NAMES, sm_sums):
        if n in _SMALL_SHARDED:
            wd_ = _SMALL_SHARDED[n]
            g = lax.dynamic_slice_in_dim(g, shard * wd_, wd_, axis=1)
        grads[n] = g.reshape(w[n].shape)

    delta, new_m, new_v = {}, {}, {}
    for n, own, recv in zip(_BIG_NAMES, halves, _sibling_swap(halves)):
        to2d = jnp.transpose if n in _TRANSPOSED else (lambda t: t)
        outs4 = _adamw_halves(to2d(w[n][0]), own, recv, to2d(mom[n][0]), to2d(var[n][0]), c_arr, "adamw_" + n)
        grads[n], delta[n], new_m[n], new_v[n] = [to2d(o)[None] for o in outs4]
    packs = [_row_pack([t[n] for n in _SMALL_NAMES]) for t in (w, grads, mom, var)]
    sm_shapes = [w[n].shape for n in _SMALL_NAMES]
    for dst, src in zip((delta, new_m, new_v), _adamw(*packs, "adamw_small")):
        for n, val in zip(_SMALL_NAMES, _row_unpack(src, sm_shapes)):
            dst[n] = val

    outs = [loss, gx[None]]
    for group in (grads, delta, new_m, new_v):
        outs += [group[n] for n in _OUT_ORDER]
    return tuple(outs)
```

```python
import functools

import jax
import jax.numpy as jnp
from jax import lax
from jax.experimental import pallas as pl
from jax.experimental.pallas import tpu as pltpu

F32 = jnp.float32
BF16 = jnp.bfloat16
EPS = 1e-6
D = 1024
NSH = 4
FSH = 704
RGW = 512
QKVW = 1536
ZW = 512
BAW = 16
BAP = 128
INSH = 772
OUTSH = 256
CHUNK = 64
NH = 4
DH = 128
RG_C = 8.0
VMEM_LIMIT = 52 * 1024 * 1024
TN_VMEM_BUDGET = 40 * 1024 * 1024
MESH = pl.DeviceIdType.MESH

ADAM_LR = 0.001
ADAM_B1 = 0.9
ADAM_B2 = 0.999
ADAM_EPS = 1e-08
ADAM_WD = 0.01
ADAM_STEP = 10


def _cparams(n_grid):
    return pltpu.CompilerParams(dimension_semantics=("arbitrary",) * n_grid, vmem_limit_bytes=VMEM_LIMIT)


def _sig(x):
    return 0.5 + 0.5 * jnp.tanh(0.5 * x)


def _sig_pos(x):
    return 1.0 / (1.0 + jnp.exp(-x))


def _softplus(x):
    return jnp.maximum(x, 0.0) + jnp.log(1.0 + jnp.exp(-jnp.abs(x)))


def _one_minus_sq_exp(la, a):
    y = 2.0 * la
    series = -y * (1.0 + y * (0.5 + y * (1.0 / 6 + y * (1.0 / 24 + y * (1.0 / 120 + y * (1.0 / 720))))))
    return jnp.where(y > -0.1, series, 1.0 - a * a)


_GELU_C = 0.7978845608028654


def _gelu(x):
    t = jnp.tanh(_GELU_C * (x + 0.044715 * x * x * x))
    return 0.5 * x * (1.0 + t)


def _gelu_grad(x):
    t = jnp.tanh(_GELU_C * (x + 0.044715 * x * x * x))
    return 0.5 * (1.0 + t) + 0.5 * x * (1.0 - t * t) * _GELU_C * (1.0 + 3 * 0.044715 * x * x)


def _silu_grad(x):
    s = _sig(x)
    return s * (1.0 + x * (1.0 - s))


def _dot(a, b):
    return jnp.dot(a.astype(BF16), b.astype(BF16), preferred_element_type=F32)


def _dot_nt(a, b):
    return lax.dot_general(a.astype(BF16), b.astype(BF16), (((1,), (1,)), ((), ())), preferred_element_type=F32)


def _dot_tn(a, b):
    return lax.dot_general(a.astype(BF16), b.astype(BF16), (((0,), (0,)), ((), ())), preferred_element_type=F32)


_NN = ((1,), (0,))
_NT = ((1,), (1,))
_TN = ((0,), (0,))


def _dg(a, b, dims):
    return lax.dot_general(a, b, (dims, ((), ())), preferred_element_type=F32)


def _split2(a):
    hi = a.astype(BF16)
    return hi, (a - hi.astype(F32)).astype(BF16)


def _dot3(a, b, dims=_NN):
    ah, al = _split2(a)
    bh, bl = _split2(b)
    return _dg(ah, bh, dims) + _dg(ah, bl, dims) + _dg(al, bh, dims)


def _dot_exact(e, x, dims, e_is_lhs):
    x0 = x.astype(BF16)
    r = x - x0.astype(F32)
    x1 = r.astype(BF16)
    x2 = (r - x1.astype(F32)).astype(BF16)
    eb = e.astype(BF16)
    if e_is_lhs:
        return _dg(eb, x0, dims) + _dg(eb, x1, dims) + _dg(eb, x2, dims)
    return _dg(x0, eb, dims) + _dg(x1, eb, dims) + _dg(x2, eb, dims)


def _rms(xv):
    r = lax.rsqrt(jnp.mean(xv * xv, axis=-1, keepdims=True) + EPS)
    return r, xv * r


def _rms_bwd(dy, xh, r, gain):
    dxh = dy * gain
    return r * (dxh - xh * jnp.mean(dxh * xh, axis=-1, keepdims=True))


def _colsum(v):
    return jnp.sum(v, axis=0, keepdims=True)


def _rows(t, c):
    return pl.BlockSpec((t, c), lambda i: (i, 0))


def _full(shape):
    n = len(shape)
    return pl.BlockSpec(shape, lambda i: (0,) * n)


def _sds(shape, dtype=F32):
    return jax.ShapeDtypeStruct(shape, dtype)


def _ffn_fwd(x, gain, wg, wu, wd, name, comm=None):
    s = x.shape[0]
    tm = min(256, s)

    def body(x_ref, g_ref, wg_ref, wu_ref, wd_ref, xo_ref, ga_ref, gb_ref, f_ref):
        xv = x_ref[...]
        _, xh = _rms(xv)
        h = (xh * g_ref[...]).astype(BF16)
        acc = None
        for j in range(NSH):
            a = jnp.dot(h, wg_ref[j], preferred_element_type=F32)
            b = jnp.dot(h, wu_ref[j], preferred_element_type=F32)
            sa = _sig(a)
            silu = a * sa
            fv = silu * b
            f = fv.astype(BF16)
            f_ref[j] = f
            ga_ref[j] = (sa * b + fv * (1.0 - sa)).astype(BF16)
            gb_ref[j] = silu.astype(BF16)
            part = jnp.dot(f, wd_ref[j], preferred_element_type=F32)
            acc = part if acc is None else acc + part
        xo_ref[...] = xv + 0.5 * acc

    hidden = pl.BlockSpec((NSH, tm, FSH), lambda i: (0, i, 0))
    return _pallas(
        body, comm, name=name, grid=(s // tm,),
        in_specs=[_rows(tm, D), _full((1, D)),
                  pl.BlockSpec((NSH, D, FSH), lambda i: (0, 0, 0), pipeline_mode=pl.Buffered(1)),
                  pl.BlockSpec((NSH, D, FSH), lambda i: (0, 0, 0), pipeline_mode=pl.Buffered(1)),
                  pl.BlockSpec((NSH, FSH, D), lambda i: (0, 0, 0), pipeline_mode=pl.Buffered(1))],
        out_specs=[_rows(tm, D), hidden, hidden, hidden],
        out_shape=[_sds((s, D))] + [_sds((NSH, s, FSH), BF16)] * 3,
        scratch_shapes=[], args=(x, gain, wg, wu, wd))


def _ffn_bwd(x, dout, do, gain, ga, gb, wg, wu, wd, name, comm=None):
    s = x.shape[0]
    tm = min(512, s)

    def hidden(do_ref, ga_ref, gb_ref, wd_ref, da_ref, db_ref):
        dov = do_ref[...]
        for j in range(NSH):
            df = _dot_nt(dov, wd_ref[j])
            da_ref[j] = (df * ga_ref[j].astype(F32)).astype(BF16)
            db_ref[j] = (df * gb_ref[j].astype(F32)).astype(BF16)

    sh = pl.BlockSpec((NSH, tm, FSH), lambda i: (0, i, 0))
    (da, db), carried = _pallas(
        hidden, comm, name=name + "_hidden", grid=(s // tm,),
        in_specs=[_rows(tm, D), sh, sh, pl.BlockSpec((NSH, FSH, D), lambda i: (0, 0, 0), pipeline_mode=pl.Buffered(1))],
        out_specs=[sh, sh], out_shape=[_sds((NSH, s, FSH), BF16)] * 2, scratch_shapes=[], args=(do, ga, gb, wd))

    def inputs(x_ref, d_ref, g_ref, da_ref, db_ref, wg_ref, wu_ref, dx_ref, dg_ref, h_ref):
        @pl.when(pl.program_id(0) == 0)
        def _():
            dg_ref[...] = jnp.zeros_like(dg_ref)

        dh = jnp.zeros((tm, D), F32)
        for j in range(NSH):
            dh = dh + _dot_nt(da_ref[j], wg_ref[j]) + _dot_nt(db_ref[j], wu_ref[j])
        r, xh = _rms(x_ref[...])
        gv = g_ref[...]
        h_ref[...] = (xh * gv).astype(BF16)
        dg_ref[...] += _colsum(dh * xh)
        dx_ref[...] = d_ref[...] + _rms_bwd(dh, xh, r, gv)

    grads = pl.BlockSpec((NSH, tm, FSH), lambda i: (0, i, 0))
    resident = pl.BlockSpec((NSH, D, FSH), lambda i: (0, 0, 0), pipeline_mode=pl.Buffered(1))
    dx, dg, h = pl.pallas_call(
        inputs, name=name + "_input", grid=(s // tm,),
        in_specs=[_rows(tm, D), _rows(tm, D), _full((1, D)), grads, grads, resident, resident],
        out_specs=[_rows(tm, D), _full((1, D)), _rows(tm, D)],
        out_shape=[_sds((s, D)), _sds((1, D)), _sds((s, D), BF16)], compiler_params=_cparams(1),
    )(x, dout, gain, da, db, wg, wu)
    return dx, dg, h, da, db, carried


def _tn(a, b, name, comm=None):
    a_g = a.ndim == 3
    b_g = b.ndim == 3
    g = a.shape[0] if a_g else (b.shape[0] if b_g else 1)
    s, k = a.shape[-2:]
    n = b.shape[-1]
    ts = min(4096, s)
    while ts > 256 and 2 * ts * (k + n) * max(a.dtype.itemsize, b.dtype.itemsize) + 2 * k * n * 4 > TN_VMEM_BUDGET:
        ts //= 2

    def body(a_ref, b_ref, o_ref):
        @pl.when(pl.program_id(1) == 0)
        def _():
            o_ref[...] = jnp.zeros_like(o_ref)

        av = a_ref[0] if a_g else a_ref[...]
        bv = b_ref[0] if b_g else b_ref[...]
        o_ref[0] += _dot_tn(av, bv)

    a_spec = pl.BlockSpec((1, ts, k), lambda gi, si: (gi, si, 0)) if a_g else pl.BlockSpec((ts, k), lambda gi, si: (si, 0))
    b_spec = pl.BlockSpec((1, ts, n), lambda gi, si: (gi, si, 0)) if b_g else pl.BlockSpec((ts, n), lambda gi, si: (si, 0))
    outs, carried = _pallas(body, comm, name=name, grid=(g, s // ts), in_specs=[a_spec, b_spec],
                            out_specs=[pl.BlockSpec((1, k, n), lambda gi, si: (gi, 0, 0))], out_shape=[_sds((g, k, n))],
                            scratch_shapes=[], args=(a, b))
    return outs[0] if comm is None else (outs[0], carried)


_P_WIDTHS = (RGW, RGW, QKVW, ZW, BAP)


def _inproj(x1, gain, ws, name):
    s = x1.shape[0]
    tm = min(256, s)

    def body(x_ref, g_ref, *refs):
        w_refs = refs[:5]
        h_ref = refs[5]
        p_refs = refs[6:]
        _, xh = _rms(x_ref[...])
        h = (xh * g_ref[...]).astype(BF16)
        h_ref[...] = h
        for w_ref, p_ref in zip(w_refs, p_refs):
            p_ref[...] = jnp.dot(h, w_ref[...], preferred_element_type=F32)

    return pl.pallas_call(
        body, name=name, grid=(s // tm,),
        in_specs=[_rows(tm, D), _full((1, D))] + [_full((D, w)) for w in _P_WIDTHS],
        out_specs=[_rows(tm, D)] + [_rows(tm, w) for w in _P_WIDTHS],
        out_shape=[_sds((s, D), BF16)] + [_sds((s, w)) for w in _P_WIDTHS],
        compiler_params=_cparams(1),
    )(x1, gain, *ws)


def _inproj_bwd(x1, dx2, gain, dps, ws, name):
    s = x1.shape[0]
    tm = min(256, s)

    def body(x_ref, d_ref, g_ref, *refs):
        dp_refs = refs[:5]
        w_refs = refs[5:10]
        dx_ref, dxh_ref, dg_ref = refs[10:]

        @pl.when(pl.program_id(0) == 0)
        def _():
            dg_ref[...] = jnp.zeros_like(dg_ref)

        dh = jnp.zeros((tm, D), F32)
        for dp_ref, w_ref in zip(dp_refs, w_refs):
            dh = dh + _dot_nt(dp_ref[...], w_ref[...])
        r, xh = _rms(x_ref[...])
        dg_ref[...] += _colsum(dh * xh)
        dx = d_ref[...] + _rms_bwd(dh, xh, r, g_ref[...])
        dx_ref[...] = dx
        dxh_ref[...] = (0.5 * dx).astype(BF16)

    return pl.pallas_call(
        body, name=name, grid=(s // tm,),
        in_specs=[_rows(tm, D), _rows(tm, D), _full((1, D))] + [_rows(tm, w) for w in _P_WIDTHS]
        + [_full((D, w)) for w in _P_WIDTHS],
        out_specs=[_rows(tm, D), _rows(tm, D), _full((1, D))],
        out_shape=[_sds((s, D)), _sds((s, D), BF16), _sds((1, D))],
        compiler_params=_cparams(1),
    )(x1, dx2, gain, *dps, *ws)


def _halo_specs(s, t, c):
    nb8 = s // 8
    tb = t // 8
    prev = pl.BlockSpec((8, c), lambda i: (jnp.maximum(i * tb - 1, 0), 0))
    nxt = pl.BlockSpec((8, c), lambda i: (jnp.minimum((i + 1) * tb, nb8 - 1), 0))
    return prev, nxt


def _edge_masks(nb):
    i = pl.program_id(0)
    return jnp.where(i > 0, 1.0, 0.0).astype(F32), jnp.where(i < nb - 1, 1.0, 0.0).astype(F32)


def _shifted(xx, off, t):
    n = t + 16
    sh = (-off) % n
    rolled = xx if sh == 0 else pltpu.roll(xx, sh, 0)
    return rolled[8:8 + t]


def _conv(x, w8, bias, name):
    s, c = x.shape
    t = min(256, s)
    nb = s // t

    def body(x_ref, xp_ref, xn_ref, w_ref, b_ref, o_ref):
        pm, nm = _edge_masks(nb)
        for c0 in range(0, c, 512):
            cols = slice(c0, c0 + 512)
            xx = jnp.concatenate([xp_ref[:, cols] * pm, x_ref[:, cols], xn_ref[:, cols] * nm], axis=0)
            acc = jnp.zeros((t, 512), F32) + b_ref[:, cols]
            for j in range(4):
                acc = acc + w_ref[j:j + 1, cols] * _shifted(xx, j - 2, t)
            o_ref[:, cols] = acc

    prev, nxt = _halo_specs(s, t, c)
    return pl.pallas_call(
        body, name=name, grid=(nb,),
        in_specs=[_rows(t, c), prev, nxt, _full((8, c)), _full((1, c))],
        out_specs=_rows(t, c), out_shape=_sds((s, c)), compiler_params=_cparams(1),
    )(x, x, x, w8, bias)


def _conv_bwd(x, dc, w8, name):
    s, c = x.shape
    t = min(256, s)
    nb = s // t

    def body(x_ref, d_ref, dp_ref, dn_ref, w_ref, dx_ref, dw_ref, db_ref):
        @pl.when(pl.program_id(0) == 0)
        def _():
            dw_ref[...] = jnp.zeros_like(dw_ref)
            db_ref[...] = jnp.zeros_like(db_ref)

        pm, nm = _edge_masks(nb)
        for c0 in range(0, c, 512):
            cols = slice(c0, c0 + 512)
            dd = jnp.concatenate([dp_ref[:, cols] * pm, d_ref[:, cols], dn_ref[:, cols] * nm], axis=0)
            xv = x_ref[:, cols]
            acc = jnp.zeros((t, 512), F32)
            for j in range(4):
                dsh = _shifted(dd, 2 - j, t)
                acc = acc + w_ref[j:j + 1, cols] * dsh
                dw_ref[j:j + 1, cols] += _colsum(dsh * xv)
            dx_ref[:, cols] = acc.astype(BF16)
            db_ref[:, cols] += _colsum(d_ref[:, cols])

    prev, nxt = _halo_specs(s, t, c)
    return pl.pallas_call(
        body, name=name, grid=(nb,),
        in_specs=[_rows(t, c), _rows(t, c), prev, nxt, _full((8, c))],
        out_specs=[_rows(t, c), _full((8, c)), _full((1, c))],
        out_shape=[_sds((s, c), BF16), _sds((8, c)), _sds((1, c))], compiler_params=_cparams(1),
    )(x, dc, dc, dc, w8)


def _rg_gates(xc, pre, lam_row):
    sp8 = RG_C * _softplus(-lam_row)
    out = []
    for d in range(2):
        r = _sig_pos(pre[:, RGW * d:RGW * (d + 1)])
        gi = _sig(pre[:, 2 * RGW + RGW * d:2 * RGW + RGW * (d + 1)])
        la = -r * sp8[:, RGW * d:RGW * (d + 1)]
        a = jnp.exp(la)
        mult = jnp.sqrt(_one_minus_sq_exp(la, a))
        out.append((r, gi, a, mult))
    return out


def _mix_prep(c_rg, c_qkv, p_ba, wgates, gbias, lam_row, alog_row, dtb_row, name):
    s = c_rg.shape[0]
    t = min(256, s)

    def body(xc_ref, cq_ref, pc_ref, wg_ref, gb_ref, lam_ref, alog_ref, dtb_ref,
             a0_ref, b0_ref, a1_ref, b1_ref, q_ref, k_ref, v_ref, bg_ref):
        xc = xc_ref[...]
        pre = _dot(xc, wg_ref[...]) + gb_ref[...]
        gates = _rg_gates(xc, pre, lam_ref[...])
        for (r, gi, a, mult), a_ref, b_ref in zip(gates, (a0_ref, a1_ref), (b0_ref, b1_ref)):
            a_ref[...] = a
            b_ref[...] = mult * gi * xc
        cq = cq_ref[...]
        sq = cq * _sig(cq)
        for h in range(NH):
            sl = slice(DH * h, DH * (h + 1))
            qh = sq[:, sl]
            q_ref[:, sl] = qh * lax.rsqrt(jnp.sum(qh * qh, axis=-1, keepdims=True) + EPS) * (DH ** -0.5)
            kh = sq[:, RGW + DH * h:RGW + DH * (h + 1)]
            k_ref[:, sl] = kh * lax.rsqrt(jnp.sum(kh * kh, axis=-1, keepdims=True) + EPS)
        v_ref[...] = sq[:, 2 * RGW:]
        pc = pc_ref[...]
        lane = lax.broadcasted_iota(jnp.int32, pc.shape, 1)
        beta = _sig(pc)
        g = -jnp.exp(alog_ref[...]) * _softplus(pc + dtb_ref[...])
        bg_ref[...] = jnp.where(lane < 8, beta, jnp.where(lane < 16, g, 0.0))

    return pl.pallas_call(
        body, name=name, grid=(s // t,),
        in_specs=[_rows(t, RGW), _rows(t, QKVW), _rows(t, BAP), _full((RGW, 4 * RGW)), _full((1, 4 * RGW)),
                  _full((1, 2 * RGW)), _full((1, BAP)), _full((1, BAP))],
        out_specs=[_rows(t, RGW)] * 7 + [_rows(t, BAP)],
        out_shape=[_sds((s, RGW))] * 7 + [_sds((s, BAP))],
        compiler_params=_cparams(1),
    )(c_rg, c_qkv, p_ba, wgates, gbias, lam_row, alog_row, dtb_row)


def _block_scan(av, bv, row, downwards):
    for k in (1, 2, 4):
        sh = (8 - k) if downwards else k
        m = (row < 8 - k) if downwards else (row >= k)
        a_s = pltpu.roll(av, sh, 0)
        b_s = pltpu.roll(bv, sh, 0)
        bv = jnp.where(m, av * b_s + bv, bv)
        av = jnp.where(m, av * a_s, av)
    return av, bv


def _gates_bwd(xc, wgates, gbias, lam_row, lam0, lam1, hf, hb, name):
    s = xc.shape[0]
    t = min(256, s)
    nb = s // t

    def body(xc_ref, wg_ref, gb_ref, lam_ref, l0_ref, l1_ref, hf_ref, hfp_ref, hfn_ref, hb_ref, hbp_ref, hbn_ref,
             dxc_ref, dpre_ref, xcb_ref, dgb_ref, dlam_ref):
        @pl.when(pl.program_id(0) == 0)
        def _():
            dgb_ref[...] = jnp.zeros_like(dgb_ref)
            dlam_ref[...] = jnp.zeros_like(dlam_ref)

        pm, nm = _edge_masks(nb)
        h_prev = _shifted(jnp.concatenate([hfp_ref[...] * pm, hf_ref[...], hfn_ref[...] * nm], axis=0), -1, t)
        h_next = _shifted(jnp.concatenate([hbp_ref[...] * pm, hb_ref[...], hbn_ref[...] * nm], axis=0), 1, t)
        h_shift = (h_prev, h_next)
        xv = xc_ref[...]
        pre = _dot(xv, wg_ref[...]) + gb_ref[...]
        lam_row_v = lam_ref[...]
        sp8 = RG_C * _softplus(-lam_row_v)
        dsp_dlam = -RG_C * _sig(-lam_row_v)
        gates = _rg_gates(xv, pre, lam_row_v)
        dxc = jnp.zeros((t, RGW), F32)
        dpre_r = []
        dpre_i = []
        for d, ((r, gi, a, mult), l_ref, hs) in enumerate(zip(gates, (l0_ref, l1_ref), h_shift)):
            dbb = l_ref[...]
            da = dbb * hs
            cs = slice(RGW * d, RGW * (d + 1))
            dmult = dbb * gi * xv
            dgi = dbb * mult * xv
            dxc = dxc + dbb * mult * gi
            dla = da * a - dmult * a * a / mult
            dr = -dla * sp8[:, cs]
            dlam_ref[:, cs] += _colsum(-dla * r) * dsp_dlam[:, cs]
            dpre_r.append(dr * r * (1.0 - r))
            dpre_i.append(dgi * gi * (1.0 - gi))
        dpre = jnp.concatenate(dpre_r + dpre_i, axis=1)
        dgb_ref[...] += _colsum(dpre)
        dpre_b = dpre.astype(BF16)
        dpre_ref[...] = dpre_b
        xcb_ref[...] = xv.astype(BF16)
        dxc_ref[...] = dxc + _dot_nt(dpre_b, wg_ref[...])

    prev, nxt = _halo_specs(s, t, RGW)
    return pl.pallas_call(
        body, name=name, grid=(s // t,),
        in_specs=[_rows(t, RGW), _full((RGW, 4 * RGW)), _full((1, 4 * RGW)), _full((1, 2 * RGW))] + [_rows(t, RGW)] * 2
        + [_rows(t, RGW), prev, nxt] * 2,
        out_specs=[_rows(t, RGW), _rows(t, 4 * RGW), _rows(t, RGW), _full((1, 4 * RGW)), _full((1, 2 * RGW))],
        out_shape=[_sds((s, RGW)), _sds((s, 4 * RGW), BF16), _sds((s, RGW), BF16), _sds((1, 4 * RGW)), _sds((1, 2 * RGW))],
        compiler_params=_cparams(1),
    )(xc, wgates, gbias, lam_row, lam0, lam1, hf, hf, hf, hb, hb, hb)


class _GdnMasks:
    def __init__(self, d):
        ri = lax.broadcasted_iota(jnp.int32, (CHUNK, CHUNK), 0)
        ci = lax.broadcasted_iota(jnp.int32, (CHUNK, CHUNK), 1)
        self.incl = (ri >= ci) if d == 0 else (ri <= ci)
        self.strict = (ri > ci) if d == 0 else (ri < ci)
        b16 = jnp.right_shift(ri, 4) == jnp.right_shift(ci, 4)
        b32 = jnp.right_shift(ri, 5) == jnp.right_shift(ci, 5)
        self.diag16 = b16
        self.off32 = jnp.logical_and(b32, jnp.logical_not(b16))
        self.off64 = jnp.logical_not(b32)
        self.eye = jnp.where(ri == ci, 1.0, 0.0).astype(F32)
        self.tri = jnp.where(self.incl, 1.0, 0.0).astype(F32)
        self.last = CHUNK - 1 if d == 0 else 0


def _tri_inv(lmat, m):
    return _tri_inv_many([lmat], [m])[0]


def _tri_inv_many(lmats, masks):
    n = len(lmats)
    ns = [jnp.where(masks[i].diag16, lmats[i], 0.0) for i in range(n)]
    ps = [masks[i].eye - ns[i] for i in range(n)]
    qs = [_dot3(ns[i], ns[i]) for i in range(n)]
    for step in range(3):
        ps = [_dot3(ps[i], masks[i].eye + qs[i]) for i in range(n)]
        if step < 2:
            qs = [_dot3(qs[i], qs[i]) for i in range(n)]
    for off in ("off32", "off64"):
        ts = [_dot3(ps[i], jnp.where(getattr(masks[i], off), lmats[i], 0.0)) for i in range(n)]
        ps = [ps[i] - _dot3(ts[i], ps[i]) for i in range(n)]
    return ps


def _chunk_cumsums(m, bgv):
    return _dot_exact(m.tri, bgv, _NN, True), _dot_exact(m.tri, bgv, ((0,), (1,)), False)


class _GdnHead:
    def __init__(self, qh, kh, vh, kk, q0, bg, gcs, gcs_t, d, h, m):
        cb = 4 * d + h
        cg = 8 + 4 * d + h
        self.q, self.k, self.v = qh, kh, vh
        self.beta = bg[:, cb:cb + 1]
        gcol = gcs[:, cg:cg + 1]
        grow = gcs_t[cg:cg + 1, :]
        gl = gcs[m.last:m.last + 1, cg:cg + 1]
        self.decay = jnp.exp(jnp.where(m.incl, gcol - grow, -1e30))
        self.kb = kh * self.beta
        self.vb = vh * self.beta
        self.a0 = kk * self.beta
        self.q0 = q0
        self.lmat = jnp.where(m.strict, self.a0 * self.decay, 0.0)
        self.attn = self.q0 * self.decay
        self.eg = jnp.exp(gcol)
        self.ek = jnp.exp(gl - gcol)
        self.cd = jnp.exp(gl)
        self.kg = self.kb * self.eg
        self.qd = qh * self.eg
        self.kd = kh * self.ek


HW = NH * DH
SEQ_CB = 4
LOCAL_CB = 4


def _head(h):
    return slice(DH * h, DH * (h + 1))


def _gdn_local_fwd(q, k, v, bg, name):
    s = q.shape[0]
    n = s // CHUNK
    cb = min(LOCAL_CB, n)

    def body(q_ref, k_ref, v_ref, bg_ref, t_ref, u_ref, w_ref, qd_ref, kd_ref, at_ref, cd_ref):
        masks = [_GdnMasks(d) for d in range(2)]
        inst = []
        for jj in range(cb):
            rows = slice(CHUNK * jj, CHUNK * (jj + 1))
            bgv = bg_ref[rows, :]
            qs = [q_ref[rows, _head(h)] for h in range(NH)]
            ks = [k_ref[rows, _head(h)] for h in range(NH)]
            kk = [_dot_nt(ks[h], ks[h]) for h in range(NH)]
            q0 = [_dot_nt(qs[h], ks[h]) for h in range(NH)]
            for d, m in enumerate(masks):
                gcs, gcs_t = _chunk_cumsums(m, bgv)
                for h in range(NH):
                    c = _GdnHead(qs[h], ks[h], v_ref[rows, _head(h)], kk[h], q0[h], bgv, gcs, gcs_t, d, h, m)
                    inst.append((jj, rows, d, h, m, c))
        tms = _tri_inv_many([it[-1].lmat for it in inst], [it[-2] for it in inst])
        for (jj, rows, d, h, m, c), tm in zip(inst, tms):
            sl = _head(h)
            t_ref[jj, d, h] = tm
            u_ref[d, rows, sl] = _dot(tm, c.vb).astype(BF16)
            w_ref[d, rows, sl] = _dot(tm, c.kg).astype(BF16)
            qd_ref[d, rows, sl] = c.qd.astype(BF16)
            kd_ref[d, rows, sl] = c.kd.astype(BF16)
            at_ref[jj, d, h] = c.attn.astype(BF16)
            cd_ref[jj, 4 * d + h:4 * d + h + 1, :] = jnp.broadcast_to(c.cd, (1, DH))

    tok = _rows(cb * CHUNK, HW)
    tok2 = pl.BlockSpec((2, cb * CHUNK, HW), lambda i: (0, i, 0))
    mat = pl.BlockSpec((cb, 2, NH, CHUNK, CHUNK), lambda i: (i, 0, 0, 0, 0))
    return pl.pallas_call(
        body, name=name, grid=(n // cb,), in_specs=[tok, tok, tok, _rows(cb * CHUNK, BAP)],
        out_specs=[mat, tok2, tok2, tok2, tok2, mat, pl.BlockSpec((cb, 8, DH), lambda i: (i, 0, 0))],
        out_shape=[_sds((n, 2, NH, CHUNK, CHUNK)), _sds((2, s, HW), BF16), _sds((2, s, HW), BF16), _sds((2, s, HW), BF16),
                   _sds((2, s, HW), BF16), _sds((n, 2, NH, CHUNK, CHUNK), BF16), _sds((n, 8, DH))],
        compiler_params=_cparams(1),
    )(q, k, v, bg)


def _seq_specs(s, order):
    n = s // CHUNK
    cb = min(SEQ_CB, n)
    nb = n // cb
    tb = cb * CHUNK

    def blk(d):
        return (lambda i: i) if order[d] else (lambda i: nb - 1 - i)

    def per_dir(make):
        return [make(d, blk(d)) for d in range(2)]

    tok2 = per_dir(lambda d, f: pl.BlockSpec((1, tb, HW), lambda i: (d, f(i), 0)))
    tok = per_dir(lambda d, f: pl.BlockSpec((tb, HW), lambda i: (f(i), 0)))
    mat = per_dir(lambda d, f: pl.BlockSpec((cb, 1, NH, CHUNK, CHUNK), lambda i: (f(i), d, 0, 0, 0)))
    cds = per_dir(lambda d, f: pl.BlockSpec((cb, 8, DH), lambda i: (f(i), 0, 0)))
    sts = per_dir(lambda d, f: pl.BlockSpec((cb, NH, DH, DH), lambda i: (f(i), 0, 0, 0)))
    dcd = per_dir(lambda d, f: pl.BlockSpec((cb, NH, DH), lambda i: (f(i), 0, 0)))
    return n, cb, nb, tok2, tok, mat, cds, sts, dcd


class _ScanRider:
    def __init__(self, af, bf, ar, br, shifted, tb, nb, up_spec, down_spec):
        s, c = af.shape
        self.shifted, self.t, self.c, self.nb = shifted, tb, c, nb
        self.args = [af, bf, ar, br]
        self.in_specs = [up_spec, up_spec, down_spec, down_spec]
        self.scratch = [pltpu.VMEM((16, c), F32)]
        if shifted:
            tb8 = tb // 8
            self.args += [af, ar]
            self.in_specs += [pl.BlockSpec((8, c), lambda i: (jnp.maximum(i * tb8 - 1, 0), 0)),
                              pl.BlockSpec((8, c), lambda i: (jnp.minimum((nb - i) * tb8, s // 8 - 1), 0))]
            self.scratch += [pltpu.VMEM((tb + 8, c), F32), pltpu.VMEM((tb + 8, c), F32)]
        self.out_specs = [up_spec, down_spec]
        self.out_shape = [_sds((s, c)), _sds((s, c))]

    def begin(self, in_refs, out_refs, scratch_refs):
        i = pl.program_id(0)
        self.carry = scratch_refs[0]

        @pl.when(i == 0)
        def _():
            self.carry[...] = jnp.zeros_like(self.carry)

        af_ref, self.bf_ref, ar_ref, self.br_ref = in_refs[0:4]
        self.hf_ref, self.hr_ref = out_refs
        self.a_up, self.a_dn = af_ref, ar_ref
        if self.shifted:
            t = self.t
            edge = jnp.where(i > 0, 1.0, 0.0).astype(F32)
            fbuf, rbuf = scratch_refs[1:3]
            fbuf[0:8, :] = in_refs[4][...] * edge
            fbuf[8:t + 8, :] = af_ref[...]
            rbuf[0:t, :] = ar_ref[...]
            rbuf[t:t + 8, :] = in_refs[5][...] * edge
            self.a_up, self.a_dn = fbuf, rbuf
        self.row = lax.broadcasted_iota(jnp.int32, (8, self.c), 0)
        self.cf, self.cr = self.carry[0:1, :], self.carry[8:9, :]

    def groups(self, lo, hi):
        ng = self.t // 8
        row = self.row
        for gi in range(lo, hi):
            rf, rr = 8 * gi, 8 * (ng - 1 - gi)
            if self.shifted:
                a_f = jnp.where(row > 0, pltpu.roll(self.a_up[rf + 8:rf + 16, :], 1, 0), pltpu.roll(self.a_up[rf:rf + 8, :], 1, 0))
                a_r = jnp.where(row < 7, pltpu.roll(self.a_dn[rr:rr + 8, :], 7, 0), pltpu.roll(self.a_dn[rr + 8:rr + 16, :], 7, 0))
            else:
                a_f, a_r = self.a_up[rf:rf + 8, :], self.a_dn[rr:rr + 8, :]
            a_f, b_f = _block_scan(a_f, self.bf_ref[rf:rf + 8, :], row, False)
            a_r, b_r = _block_scan(a_r, self.br_ref[rr:rr + 8, :], row, True)
            h_f = a_f * self.cf + b_f
            h_r = a_r * self.cr + b_r
            self.hf_ref[rf:rf + 8, :] = h_f
            self.hr_ref[rr:rr + 8, :] = h_r
            self.cf, self.cr = h_f[7:8, :], h_r[0:1, :]

    def end(self):
        self.carry[0:1, :] = self.cf
        self.carry[8:9, :] = self.cr


def _gdn_seq_fwd(u, w, qd, kd, at, cd, name, scan=None):
    s = u.shape[1]
    n, cb, nb, tok2, tok, mat, cds, sts, _ = _seq_specs(s, (True, False))
    rider = _ScanRider(*scan, False, cb * CHUNK, nb, tok[0], tok[1]) if scan else None
    ri = len(rider.args) if rider else 0

    def body(*refs):
        ins = (refs[0:6], refs[6:12])
        outs = (refs[12 + ri:15 + ri], refs[15 + ri:18 + ri])
        st = refs[18 + ri + (2 if rider else 0)]
        if rider:
            rider.begin(refs[12:12 + ri], refs[18 + ri:20 + ri], refs[21 + ri:])

        @pl.when(pl.program_id(0) == 0)
        def _():
            st[...] = jnp.zeros_like(st)

        for j in range(cb):
            items = []
            for d in range(2):
                jj = j if d == 0 else cb - 1 - j
                items += [(d, h, jj, slice(CHUNK * jj, CHUNK * (jj + 1)), _head(h)) for h in range(NH)]
            shs = [st[d, h] for d, h, _, _, _ in items]
            wss = [_dot(ins[d][1][0, rows, sl], sh) for (d, h, jj, rows, sl), sh in zip(items, shs)]
            vns = [ins[d][0][0, rows, sl].astype(F32) - ws for (d, h, jj, rows, sl), ws in zip(items, wss)]
            news = [sh * ins[d][5][jj, 4 * d + h:4 * d + h + 1, :] + _dot_tn(ins[d][3][0, rows, sl], vn)
                    for (d, h, jj, rows, sl), sh, vn in zip(items, shs, vns)]
            for (d, h, jj, rows, sl), sh, vn, new in zip(items, shs, vns, news):
                o_r, s_r, vn_r = outs[d]
                st[d, h] = new
                s_r[jj, h] = sh.astype(BF16)
                vn_r[rows, sl] = vn.astype(BF16)
                o_r[rows, sl] = _dot(ins[d][2][0, rows, sl], sh) + _dot(ins[d][4][jj, 0, h], vn)
            if rider:
                rider.groups(8 * j, 8 * (j + 1))
        if rider:
            rider.end()

    in_specs, out_specs, out_shape = [], [], []
    for d in range(2):
        in_specs += [tok2[d]] * 4 + [mat[d], cds[d]]
        out_specs += [tok[d], sts[d], tok[d]]
        out_shape += [_sds((s, HW)), _sds((n, NH, DH, DH), BF16), _sds((s, HW), BF16)]
    args = [u, w, qd, kd, at, cd, u, w, qd, kd, at, cd]
    scratch = [pltpu.VMEM((2, NH, DH, DH), F32)]
    if rider:
        in_specs, args = in_specs + rider.in_specs, args + rider.args
        out_specs, out_shape, scratch = out_specs + rider.out_specs, out_shape + rider.out_shape, scratch + rider.scratch
    return pl.pallas_call(
        body, name=name, grid=(nb,), in_specs=in_specs, out_specs=out_specs, out_shape=out_shape,
        scratch_shapes=scratch, compiler_params=_cparams(1),
    )(*args)


def _gdn_seq_bwd(do, w, qd, kd, at, cd, states, vns, name, scan=None):
    s = do.shape[0]
    n, cb, nb, tok2, tok, mat, cds, sts, dcd = _seq_specs(s, (False, True))
    rider = _ScanRider(*scan, True, cb * CHUNK, nb, tok[1], tok[0]) if scan else None
    ri = len(rider.args) if rider else 0

    def body(*refs):
        ins = (refs[0:8], refs[8:16])
        outs = (refs[16 + ri:21 + ri], refs[21 + ri:26 + ri])
        dst = refs[26 + ri + (2 if rider else 0)]
        if rider:
            rider.begin(refs[16:16 + ri], refs[26 + ri:28 + ri], refs[29 + ri:])

        @pl.when(pl.program_id(0) == 0)
        def _():
            dst[...] = jnp.zeros_like(dst)

        for j in range(cb):
            items = []
            for d in range(2):
                jj = cb - 1 - j if d == 0 else j
                items += [(d, h, jj, slice(CHUNK * jj, CHUNK * (jj + 1)), _head(h)) for h in range(NH)]
            dsns = [dst[d, h] for d, h, _, _, _ in items]
            dohs = [ins[d][0][rows, sl] for d, h, jj, rows, sl in items]
            d_vns = [_dot_tn(ins[d][4][jj, 0, h], doh) + _dot(ins[d][3][0, rows, sl], dsn)
                     for (d, h, jj, rows, sl), doh, dsn in zip(items, dohs, dsns)]
            news = [ins[d][5][jj, 4 * d + h:4 * d + h + 1, :] * dsn + _dot_tn(ins[d][2][0, rows, sl], doh)
                    - _dot_tn(ins[d][1][0, rows, sl], d_vn)
                    for (d, h, jj, rows, sl), doh, dsn, d_vn in zip(items, dohs, dsns, d_vns)]
            for (d, h, jj, rows, sl), doh, dsn, d_vn, new in zip(items, dohs, dsns, d_vns, news):
                dvn_r, dkd_r, dqd_r, dw_r, dcd_r = outs[d]
                sh = ins[d][6][jj, h].astype(F32)
                dst[d, h] = new
                dvn_r[rows, sl] = d_vn.astype(BF16)
                dkd_r[rows, sl] = _dot_nt(ins[d][7][rows, sl], dsn)
                dqd_r[rows, sl] = _dot_nt(doh, sh)
                dw_r[rows, sl] = (-_dot_nt(d_vn, sh)).astype(BF16)
                d_cd = jnp.sum(jnp.sum(sh * dsn, axis=1, keepdims=True), axis=0, keepdims=True)
                dcd_r[jj, h:h + 1, :] = jnp.broadcast_to(d_cd, (1, DH))
            if rider:
                rider.groups(8 * j, 8 * (j + 1))
        if rider:
            rider.end()

    in_specs, out_specs, out_shape, args = [], [], [], []
    for d in range(2):
        in_specs += [tok[d]] + [tok2[d]] * 3 + [mat[d], cds[d], sts[d], tok[d]]
        args += [do, w, qd, kd, at, cd, states[d], vns[d]]
        out_specs += [tok[d]] * 4 + [dcd[d]]
        out_shape += [_sds((s, HW), BF16), _sds((s, HW)), _sds((s, HW)), _sds((s, HW), BF16), _sds((n, NH, DH))]
    scratch = [pltpu.VMEM((2, NH, DH, DH), F32)]
    if rider:
        in_specs, args = in_specs + rider.in_specs, args + rider.args
        out_specs, out_shape, scratch = out_specs + rider.out_specs, out_shape + rider.out_shape, scratch + rider.scratch
    return pl.pallas_call(
        body, name=name, grid=(nb,), in_specs=in_specs, out_specs=out_specs, out_shape=out_shape,
        scratch_shapes=scratch, compiler_params=_cparams(1),
    )(*args)


def _gdn_local_bwd(q, k, v, bg, tmat, do, vns, seq_grads, name, comm=None):
    s = q.shape[0]
    n = s // CHUNK
    cb = min(LOCAL_CB, n)

    def body(*refs):
        q_ref, k_ref, v_ref, bg_ref, t_ref, do_ref = refs[0:6]
        vn_refs = refs[6:8]
        sg = (refs[8:13], refs[13:18])
        dq_ref, dk_ref, dv_ref, dbg_ref = refs[18:]
        lane = lax.broadcasted_iota(jnp.int32, (CHUNK, BAP), 1)
        rowi = lax.broadcasted_iota(jnp.int32, (CHUNK, 1), 0)
        ones = jnp.ones((CHUNK, DH), F32)
        masks = [_GdnMasks(d) for d in range(2)]
        inst = []
        for jj in range(cb):
            rows = slice(CHUNK * jj, CHUNK * (jj + 1))
            bgv = bg_ref[rows, :]
            qs = [q_ref[rows, _head(h)] for h in range(NH)]
            ks = [k_ref[rows, _head(h)] for h in range(NH)]
            kk = [_dot_nt(ks[h], ks[h]) for h in range(NH)]
            q0 = [_dot_nt(qs[h], ks[h]) for h in range(NH)]
            for d, m in enumerate(masks):
                gcs, gcs_t = _chunk_cumsums(m, bgv)
                for h in range(NH):
                    c = _GdnHead(qs[h], ks[h], v_ref[rows, _head(h)], kk[h], q0[h], bgv, gcs, gcs_t, d, h, m)
                    inst.append((jj, rows, d, h, m, c))
        ni = len(inst)
        cs = [it[-1] for it in inst]
        tms = [t_ref[jj, d, h] for jj, _, d, h, _, _ in inst]
        d_vns = [sg[d][0][rows, _head(h)] for _, rows, d, h, _, _ in inst]
        d_ws = [sg[d][3][rows, _head(h)] for _, rows, d, h, _, _ in inst]
        d_ts = [_dot_nt(d_vns[i], cs[i].vb) + _dot_nt(d_ws[i], cs[i].kg) for i in range(ni)]
        tts = [tm.T for tm in tms]
        xs = [_dot3(tts[i], d_ts[i]) for i in range(ni)]
        d_ls = [jnp.where(inst[i][4].strict, -_dot3(xs[i], tts[i]), 0.0) for i in range(ni)]
        d_attns = [jnp.where(m.incl, _dot_nt(do_ref[rows, _head(h)], vn_refs[d][rows, _head(h)]), 0.0)
                   for _, rows, d, h, m, _ in inst]
        d_vbs = [_dot(tts[i], d_vns[i]) for i in range(ni)]
        d_kgs = [_dot(tts[i], d_ws[i]) for i in range(ni)]
        d_a0s = [d_ls[i] * cs[i].decay for i in range(ni)]
        d_q0s = [d_attns[i] * cs[i].decay for i in range(ni)]
        es = [(d_ls[i] * cs[i].a0 + d_attns[i] * cs[i].q0) * cs[i].decay for i in range(ni)]
        kb_mm = [_dot(d_a0s[i], cs[i].k) for i in range(ni)]
        q_mm = [_dot(d_q0s[i], cs[i].k) for i in range(ni)]
        k_mm = [_dot_tn(d_a0s[i], cs[i].kb) + _dot_tn(d_q0s[i], cs[i].q) for i in range(ni)]
        e_cols = [_dot_exact(ones, es[i], _TN, False)[:, 0:1] for i in range(ni)]
        acc = {}
        d_gcs, d_betas = [], []
        for i, (jj, rows, d, h, m, c) in enumerate(inst):
            sl = _head(h)
            d_kd, d_qd = sg[d][1][rows, sl], sg[d][2][rows, sl]
            d_cd = sg[d][4][jj, h:h + 1, 0:1]
            d_vb, d_kg = d_vbs[i], d_kgs[i]
            d_kb = kb_mm[i] + d_kg * c.eg
            parts = (q_mm[i] + d_qd * c.eg, k_mm[i] + d_kd * c.ek + d_kb * c.beta, d_vb * c.beta)
            acc[jj, h] = [p + a for a, p in zip(acc[jj, h], parts)] if (jj, h) in acc else list(parts)
            kd_term = d_kd * c.kd
            d_gc = (jnp.sum(d_kg * c.kg + d_qd * c.qd - kd_term, axis=1, keepdims=True)
                    + jnp.sum(es[i], axis=1, keepdims=True) - e_cols[i])
            d_gl = jnp.sum(jnp.sum(kd_term, axis=0, keepdims=True), axis=1, keepdims=True) + d_cd * c.cd
            d_gcs.append(d_gc + jnp.where(rowi == m.last, d_gl, 0.0))
            d_betas.append(jnp.sum(d_kb * c.k + d_vb * c.v, axis=1, keepdims=True))
        d_gs = [_dot_exact(inst[i][4].tri, d_gcs[i] * ones, _TN, True)[:, 0:1] for i in range(ni)]
        dbg = [jnp.zeros((CHUNK, BAP), F32) for _ in range(cb)]
        for i, (jj, _, d, h, _, _) in enumerate(inst):
            dbg[jj] = dbg[jj] + jnp.where(lane == 4 * d + h, d_betas[i], 0.0) + jnp.where(lane == 8 + 4 * d + h, d_gs[i], 0.0)
        for jj in range(cb):
            rows = slice(CHUNK * jj, CHUNK * (jj + 1))
            for h in range(NH):
                dq_ref[rows, _head(h)], dk_ref[rows, _head(h)], dv_ref[rows, _head(h)] = acc[jj, h]
            dbg_ref[rows, :] = dbg[jj]

    tok = _rows(cb * CHUNK, HW)
    bgs = _rows(cb * CHUNK, BAP)
    mat = pl.BlockSpec((cb, 2, NH, CHUNK, CHUNK), lambda i: (i, 0, 0, 0, 0))
    dcd = pl.BlockSpec((cb, NH, DH), lambda i: (i, 0, 0))
    args = [q, k, v, bg, tmat, do, vns[0], vns[1]]
    in_specs = [tok, tok, tok, bgs, mat, tok, tok, tok]
    for d in range(2):
        args += list(seq_grads[d])
        in_specs += [tok] * 4 + [dcd]
    return _pallas(body, comm, name=name, grid=(n // cb,), in_specs=in_specs, out_specs=[tok, tok, tok, bgs],
                   out_shape=[_sds((s, HW))] * 3 + [_sds((s, BAP))], scratch_shapes=[], args=args)


def _prep_bwd(c_qkv, p_ba, alog_row, dtb_row, dq, dk, dv, dbg, name):
    s = c_qkv.shape[0]
    t = min(256, s)

    def body(cq_ref, pc_ref, alog_ref, dtb_ref, dq_ref, dk_ref, dv_ref, dbg_ref,
             dcq_ref, dpc_ref, dalog_ref, ddtb_ref):
        @pl.when(pl.program_id(0) == 0)
        def _():
            dalog_ref[...] = jnp.zeros_like(dalog_ref)
            ddtb_ref[...] = jnp.zeros_like(ddtb_ref)

        cq = cq_ref[...]
        sq = cq * _sig(cq)
        sg = _silu_grad(cq)
        for h in range(NH):
            sl = slice(DH * h, DH * (h + 1))
            for off, d_ref, scale in ((0, dq_ref, DH ** -0.5), (RGW, dk_ref, 1.0)):
                csl = slice(off + DH * h, off + DH * (h + 1))
                xh = sq[:, csl]
                nrm = lax.rsqrt(jnp.sum(xh * xh, axis=-1, keepdims=True) + EPS)
                y = xh * nrm
                dy = d_ref[:, sl] * scale
                dcq_ref[:, csl] = nrm * (dy - y * jnp.sum(dy * y, axis=-1, keepdims=True)) * sg[:, csl]
        dcq_ref[:, 2 * RGW:] = dv_ref[...] * sg[:, 2 * RGW:]
        pc = pc_ref[...]
        lane = lax.broadcasted_iota(jnp.int32, pc.shape, 1)
        dbg = dbg_ref[...]
        beta = _sig(pc)
        ea = jnp.exp(alog_ref[...])
        z = pc + dtb_ref[...]
        g = -ea * _softplus(z)
        is_g = jnp.logical_and(lane >= 8, lane < 16)
        d_alpha = jnp.where(is_g, dbg * (-ea) * _sig(z), 0.0)
        dpc_ref[...] = jnp.where(lane < 8, dbg * beta * (1.0 - beta), d_alpha).astype(BF16)
        dalog_ref[...] += _colsum(jnp.where(is_g, dbg * g, 0.0))
        ddtb_ref[...] += _colsum(d_alpha)

    return pl.pallas_call(
        body, name=name, grid=(s // t,),
        in_specs=[_rows(t, QKVW), _rows(t, BAP), _full((1, BAP)), _full((1, BAP))] + [_rows(t, HW)] * 3 + [_rows(t, BAP)],
        out_specs=[_rows(t, QKVW), _rows(t, BAP), _full((1, BAP)), _full((1, BAP))],
        out_shape=[_sds((s, QKVW)), _sds((s, BAP), BF16), _sds((1, BAP)), _sds((1, BAP))],
        compiler_params=_cparams(1),
    )(c_qkv, p_ba, alog_row, dtb_row, dq, dk, dv, dbg)


def _mix_out_values(hf, hb, gate, of, ob, z, gn):
    hr = hf + hb
    y_rg = hr * _gelu(gate)
    osum = of + ob
    parts = []
    for h in range(NH):
        sl = slice(DH * h, DH * (h + 1))
        oh = osum[:, sl]
        r, ohat = _rms(oh)
        zh = z[:, sl]
        parts.append((r, ohat, zh))
    y_gdn = jnp.concatenate([ohat * gn * (zh * _sig(zh)) for (r, ohat, zh) in parts], axis=1)
    return hr, y_rg, y_gdn, parts


def _outproj(x1, hf, hb, gate, of, ob, z, gn, wout, name):
    s = x1.shape[0]
    t = min(256, s)

    def body(x_ref, hf_ref, hb_ref, gate_ref, of_ref, ob_ref, z_ref, gn_ref, w_ref, xo_ref, y_ref):
        _, y_rg, y_gdn, _ = _mix_out_values(hf_ref[...], hb_ref[...], gate_ref[...], of_ref[...], ob_ref[...],
                                            z_ref[...], gn_ref[...])
        y = jnp.concatenate([y_rg, y_gdn], axis=1).astype(BF16)
        y_ref[...] = y
        xo_ref[...] = x_ref[...] + jnp.dot(y, w_ref[...], preferred_element_type=F32)

    return pl.pallas_call(
        body, name=name, grid=(s // t,),
        in_specs=[_rows(t, D)] + [_rows(t, RGW)] * 6 + [_full((1, DH)), _full((D, D))],
        out_specs=[_rows(t, D), _rows(t, D)], out_shape=[_sds((s, D)), _sds((s, D), BF16)],
        compiler_params=_cparams(1),
    )(x1, hf, hb, gate, of, ob, z, gn, wout)


def _outproj_bwd(dx2, hf, hb, gate, of, ob, z, gn, wout, name, comm=None):
    s = dx2.shape[0]
    t = min(256, s)

    def body(d_ref, hf_ref, hb_ref, gate_ref, of_ref, ob_ref, z_ref, gn_ref, w_ref,
             dhr_ref, dgate_ref, dos_ref, dz_ref, dgn_ref, db_ref):
        @pl.when(pl.program_id(0) == 0)
        def _():
            dgn_ref[...] = jnp.zeros_like(dgn_ref)

        gate = gate_ref[...]
        gn_v = gn_ref[...]
        hr, _, _, parts = _mix_out_values(hf_ref[...], hb_ref[...], gate, of_ref[...], ob_ref[...], z_ref[...], gn_v)
        dbf = d_ref[...].astype(BF16)
        db_ref[...] = dbf
        dy = _dot_nt(dbf, w_ref[...])
        dyr = dy[:, :RGW]
        dhr_ref[...] = dyr * _gelu(gate)
        dgate_ref[...] = (dyr * hr * _gelu_grad(gate)).astype(BF16)
        dgn = jnp.zeros((1, DH), F32)
        for h, (r, ohat, zh) in enumerate(parts):
            sl = slice(DH * h, DH * (h + 1))
            dyh = dy[:, RGW + DH * h:RGW + DH * (h + 1)]
            sz = zh * _sig(zh)
            dn = dyh * sz
            dz_ref[:, sl] = (dyh * ohat * gn_v * _silu_grad(zh)).astype(BF16)
            dgn = dgn + _colsum(dn * ohat)
            dos_ref[:, sl] = _rms_bwd(dn, ohat, r, gn_v).astype(BF16)
        dgn_ref[...] += dgn

    return _pallas(
        body, comm, name=name, grid=(s // t,),
        in_specs=[_rows(t, D)] + [_rows(t, RGW)] * 6 + [_full((1, DH)), _full((D, D))],
        out_specs=[_rows(t, RGW)] * 4 + [_full((1, DH)), _rows(t, D)],
        out_shape=[_sds((s, RGW))] + [_sds((s, RGW), BF16)] * 3 + [_sds((1, DH)), _sds((s, D), BF16)],
        scratch_shapes=[], args=(dx2, hf, hb, gate, of, ob, z, gn, wout))


def _loss_head(x3, target, gain, name):
    s = x3.shape[0]
    t = min(256, s)

    def body(x_ref, t_ref, g_ref, dx_ref, dxh_ref, loss_ref, dg_ref):
        @pl.when(pl.program_id(0) == 0)
        def _():
            loss_ref[...] = jnp.zeros_like(loss_ref)
            dg_ref[...] = jnp.zeros_like(dg_ref)

        r, xh = _rms(x_ref[...])
        gv = g_ref[...]
        err = xh * gv - t_ref[...]
        per_tok = jnp.mean(err * err, axis=-1, keepdims=True)
        loss_ref[...] += 0.5 * jnp.sum(per_tok, axis=0, keepdims=True)
        dy = err * (1.0 / D)
        dg_ref[...] += _colsum(dy * xh)
        dx = _rms_bwd(dy, xh, r, gv)
        dx_ref[...] = dx
        dxh_ref[...] = (0.5 * dx).astype(BF16)

    return pl.pallas_call(
        body, name=name, grid=(s // t,), in_specs=[_rows(t, D), _rows(t, D), _full((1, D))],
        out_specs=[_rows(t, D), _rows(t, D), _full((8, 128)), _full((1, D))],
        out_shape=[_sds((s, D)), _sds((s, D), BF16), _sds((8, 128)), _sds((1, D))], compiler_params=_cparams(1),
    )(x3, target, gain)


def _adamw_math(wv, gv, mv, vv):
    mn = ADAM_B1 * mv + (1.0 - ADAM_B1) * gv
    vn = ADAM_B2 * vv + (1.0 - ADAM_B2) * (gv * gv)
    m_hat = mn / (1.0 - ADAM_B1 ** ADAM_STEP)
    v_hat = vn / (1.0 - ADAM_B2 ** ADAM_STEP)
    return -ADAM_LR * (m_hat / (jnp.sqrt(v_hat) + ADAM_EPS) + ADAM_WD * wv), mn, vn


def _row_tile(r, c):
    tr = r
    while tr * c * 4 > (1 << 20) and tr % 16 == 0:
        tr //= 2
    return tr


def _adamw(w, g, m, v, name):
    r, c = w.shape
    tr = _row_tile(r, c)

    def body(w_ref, g_ref, m_ref, v_ref, d_ref, nm_ref, nv_ref):
        d_ref[...], nm_ref[...], nv_ref[...] = _adamw_math(w_ref[...], g_ref[...], m_ref[...], v_ref[...])

    return pl.pallas_call(
        body, name=name, grid=(r // tr,), in_specs=[_rows(tr, c)] * 4, out_specs=[_rows(tr, c)] * 3,
        out_shape=[_sds((r, c))] * 3, compiler_params=_cparams(1),
    )(w, g, m, v)


def _adamw_halves(w, own, recv, m, v, c_arr, name):
    r, c = w.shape
    h = r // 2
    tr = _row_tile(h, c)
    nh = h // tr

    def body(c_ref, w_ref, own_ref, recv_ref, m_ref, v_ref, g_ref, d_ref, nm_ref, nv_ref):
        first_half = pl.program_id(0) < nh
        use_own = first_half == (c_ref[0] == 0)
        gv = jnp.where(use_own, own_ref[...], recv_ref[...])
        g_ref[...] = gv
        d_ref[...], nm_ref[...], nv_ref[...] = _adamw_math(w_ref[...], gv, m_ref[...], v_ref[...])

    full = pl.BlockSpec((tr, c), lambda i, c_ref: (i, 0))
    half = pl.BlockSpec((tr, c), lambda i, c_ref: (i % nh, 0))
    return pl.pallas_call(
        body, name=name, out_shape=[_sds((r, c))] * 4,
        grid_spec=pltpu.PrefetchScalarGridSpec(
            num_scalar_prefetch=1, grid=(2 * nh,), in_specs=[full, half, half, full, full], out_specs=[full] * 4),
        compiler_params=_cparams(1),
    )(c_arr, w, own, recv, m, v)


def _mesh_pos():
    return lax.axis_index("x"), lax.axis_index("y"), lax.axis_index("c")


def _other_chips(x, y):
    return [(1 - x, y), (x, 1 - y), (1 - x, 1 - y)]


class _Comm:
    def __init__(self, inputs, out_shapes, scratch, start, finish, space=pltpu.HBM):
        self.inputs, self.out_shapes, self.scratch = list(inputs), list(out_shapes), list(scratch)
        self.start, self.finish, self.space = start, finish, space


def _comm_call(comm, name):
    ni, no = len(comm.inputs), len(comm.out_shapes)

    def body(*refs):
        comm.start(refs[:ni], refs[ni:ni + no], refs[ni + no:])
        comm.finish(refs[:ni], refs[ni:ni + no], refs[ni + no:])

    spec = pl.BlockSpec(memory_space=comm.space)
    return list(pl.pallas_call(body, name=name, out_shape=comm.out_shapes, in_specs=[spec] * ni, out_specs=[spec] * no,
                               scratch_shapes=comm.scratch)(*comm.inputs))


def _join_comm(a, b):
    ia, oa, sa = len(a.inputs), len(a.out_shapes), len(a.scratch)

    def both(method):
        def run(ins, outs, sems):
            getattr(a, method)(ins[:ia], outs[:oa], sems[:sa])
            getattr(b, method)(ins[ia:], outs[oa:], sems[sa:])
        return run

    return _Comm(a.inputs + b.inputs, a.out_shapes + b.out_shapes, a.scratch + b.scratch, both("start"), both("finish"))


def _pallas(body, comm, *, name, grid, in_specs, out_specs, out_shape, scratch_shapes, args):
    params = _cparams(len(grid))
    if comm is None:
        outs = pl.pallas_call(body, name=name, grid=grid, in_specs=in_specs, out_specs=out_specs, out_shape=out_shape,
                              scratch_shapes=scratch_shapes, compiler_params=params)(*args)
        return list(outs), []
    n_in, n_out, n_sc = len(in_specs), len(out_specs), len(scratch_shapes)
    ci, co = len(comm.inputs), len(comm.out_shapes)

    def carried(*refs):
        bounds = [0, n_in, n_in + ci, n_in + ci + n_out, n_in + ci + n_out + co, n_in + ci + n_out + co + n_sc, len(refs)]
        ins, cins, outs, couts, scr, csems = [refs[lo:hi] for lo, hi in zip(bounds[:-1], bounds[1:])]
        ids = [pl.program_id(k) for k in range(len(grid))]
        first = functools.reduce(jnp.logical_and, [i == 0 for i in ids])
        last = functools.reduce(jnp.logical_and, [i == g - 1 for i, g in zip(ids, grid)])

        @pl.when(first)
        def _():
            comm.start(cins, couts, csems)

        body(*ins, *outs, *scr)

        @pl.when(last)
        def _():
            comm.finish(cins, couts, csems)

    hbm = pl.BlockSpec(memory_space=pltpu.HBM)
    outs = pl.pallas_call(
        carried, name=name, grid=grid, in_specs=list(in_specs) + [hbm] * ci, out_specs=list(out_specs) + [hbm] * co,
        out_shape=list(out_shape) + comm.out_shapes, scratch_shapes=list(scratch_shapes) + comm.scratch,
        compiler_params=params)(*args, *comm.inputs)
    return list(outs[:n_out]), list(outs[n_out:])


def _gather_comm(arrays, space, block_rows):
    n_arr = len(arrays)

    def plan(x_refs, out_refs, sems):
        send_sems, recv_sems, local_sems = sems
        x, y, c = _mesh_pos()
        me, sibling = (x, y, c), (x, y, 1 - c)
        chips = _other_chips(x, y)

        def slot(a, px, py, pc):
            return out_refs[a].at[4 * px + 2 * py + pc]

        def copy(a, k, block, to, src=None):
            return pltpu.make_async_remote_copy(
                src_ref=slot(a, *block) if src is None else src, dst_ref=slot(a, *block),
                send_sem=send_sems.at[7 * a + k], recv_sem=recv_sems.at[7 * a + k], device_id=to, device_id_type=MESH)

        srcs = [x_refs[a] if block_rows[a] is None else
                x_refs[a].at[pl.ds(pl.multiple_of(c * block_rows[a], 16), block_rows[a]), :] for a in range(n_arr)]
        local = [pltpu.make_async_copy(srcs[a], slot(a, *me), local_sems.at[a]) for a in range(n_arr)]
        first = []
        for a in range(n_arr):
            first += [copy(a, 1 + j, me, (*chip, c), src=srcs[a]) for j, chip in enumerate(chips)]
            first.append(copy(a, 0, me, sibling, src=srcs[a]))
        return me, sibling, chips, c, copy, local, first

    def start(x_refs, out_refs, sems):
        _, _, _, _, _, local, first = plan(x_refs, out_refs, sems)
        for cp in local + first:
            cp.start()

    def finish(x_refs, out_refs, sems):
        me, sibling, chips, c, copy, local, first = plan(x_refs, out_refs, sems)
        passed = []
        for j, chip in enumerate(chips):
            for a in range(n_arr):
                copy(a, 1 + j, (*chip, c), me).wait_recv()
                fwd = copy(a, 4 + j, (*chip, c), sibling)
                fwd.start()
                passed.append(fwd)
        for a in range(n_arr):
            copy(a, 0, sibling, me).wait_recv()
            for j, chip in enumerate(chips):
                copy(a, 4 + j, (*chip, 1 - c), me).wait_recv()
        for cp in first + passed:
            cp.wait_send()
        for cp in local:
            cp.wait()

    out_shapes = [_sds((8, w.shape[0] if r is None else r) + w.shape[1:], w.dtype) for w, r in zip(arrays, block_rows)]
    scratch = [pltpu.SemaphoreType.DMA((7 * n_arr,)), pltpu.SemaphoreType.DMA((7 * n_arr,)), pltpu.SemaphoreType.DMA((n_arr,))]
    return _Comm(arrays, out_shapes, scratch, start, finish, space)


def _weights_gather_comm(shards):
    return _gather_comm(shards, pltpu.HBM, [w.shape[0] // 2 for w in shards])


def _all_shards(gathered):
    return [o.reshape(NSH, 2 * o.shape[1], o.shape[2]) for o in gathered]


def _gather_small(block, name):
    return _comm_call(_gather_comm([block], pltpu.VMEM, [None]), name)[0]


def _exchange_comm(gs):
    n = len(gs)
    halves = [g.shape[1] // 2 for g in gs]

    def plan(g_refs, land_refs, sems):
        send_sems, recv_sems = sems
        x, y, c = _mesh_pos()
        copies = []
        for a in range(n):
            h = halves[a]
            for s in range(NSH):
                copies.append(pltpu.make_async_remote_copy(
                    src_ref=g_refs[a].at[s, pl.ds(pl.multiple_of((1 - c) * h, 8), h), :], dst_ref=land_refs[a].at[s],
                    send_sem=send_sems.at[NSH * a + s], recv_sem=recv_sems.at[NSH * a + s],
                    device_id=(x, y, 1 - c), device_id_type=MESH))
        return copies

    def start(g_refs, land_refs, sems):
        for cp in plan(g_refs, land_refs, sems):
            cp.start()

    def finish(g_refs, land_refs, sems):
        for cp in plan(g_refs, land_refs, sems):
            cp.wait()

    scratch = [pltpu.SemaphoreType.DMA((NSH * n,)), pltpu.SemaphoreType.DMA((NSH * n,))]
    return _Comm(gs, [_sds((NSH, h, g.shape[2])) for h, g in zip(halves, gs)], scratch, start, finish)


def _chip_sum(g, land, c_arr, name):
    _, h, cols = land.shape

    def body(c_ref, g_ref, l_ref, o_ref):
        o_ref[...] = (g_ref[...] + l_ref[...]).astype(BF16)

    return pl.pallas_call(
        body, name=name, out_shape=_sds((NSH, h, cols), BF16),
        grid_spec=pltpu.PrefetchScalarGridSpec(
            num_scalar_prefetch=1, grid=(NSH,),
            in_specs=[pl.BlockSpec((1, h, cols), lambda s, c_ref: (s, c_ref[0], 0)),
                      pl.BlockSpec((1, h, cols), lambda s, c_ref: (s, 0, 0))],
            out_specs=pl.BlockSpec((1, h, cols), lambda s, c_ref: (s, 0, 0))),
        compiler_params=_cparams(1),
    )(c_arr, g, land)


def _scatter_comm(parts):
    n = len(parts)

    def plan(p_refs, land_refs, sems):
        send_sems, recv_sems, local_sems = sems
        x, y, c = _mesh_pos()
        my_chip = 2 * x + y
        local = [pltpu.make_async_copy(p_refs[a].at[my_chip], land_refs[a].at[my_chip], local_sems.at[a]) for a in range(n)]
        copies = []
        for a in range(n):
            for j, (px, py) in enumerate(_other_chips(x, y)):
                copies.append(pltpu.make_async_remote_copy(
                    src_ref=p_refs[a].at[2 * px + py], dst_ref=land_refs[a].at[my_chip],
                    send_sem=send_sems.at[3 * a + j], recv_sem=recv_sems.at[3 * a + j],
                    device_id=(px, py, c), device_id_type=MESH))
        return local, copies

    def start(p_refs, land_refs, sems):
        local, copies = plan(p_refs, land_refs, sems)
        for cp in local + copies:
            cp.start()

    def finish(p_refs, land_refs, sems):
        local, copies = plan(p_refs, land_refs, sems)
        for cp in copies:
            cp.wait()
        for cp in local:
            cp.wait()

    scratch = [pltpu.SemaphoreType.DMA((3 * n,)), pltpu.SemaphoreType.DMA((3 * n,)), pltpu.SemaphoreType.DMA((n,))]
    return _Comm(parts, [_sds(p.shape, BF16) for p in parts], scratch, start, finish)


def _sum_slots(land, name):
    k, r, c = land.shape
    tr = r // 2 if r % 32 == 0 else r

    def body(l_ref, o_ref):
        acc = l_ref[0].astype(F32)
        for i in range(1, k):
            acc = acc + l_ref[i].astype(F32)
        o_ref[...] = acc

    return pl.pallas_call(
        body, name=name, grid=(r // tr,), in_specs=[pl.BlockSpec((k, tr, c), lambda i: (0, i, 0))],
        out_specs=_rows(tr, c), out_shape=_sds((r, c)), compiler_params=_cparams(1),
    )(land)


def _sibling_swap(halves):
    n = len(halves)

    def body(*refs):
        h_refs, out_refs = refs[:n], refs[n:2 * n]
        send_sems, recv_sems = refs[2 * n:]
        x, y, c = _mesh_pos()
        copies = [pltpu.make_async_remote_copy(
            src_ref=h_refs[a], dst_ref=out_refs[a], send_sem=send_sems.at[a], recv_sem=recv_sems.at[a],
            device_id=(x, y, 1 - c), device_id_type=MESH) for a in range(n)]
        for cp in copies:
            cp.start()
        for cp in copies:
            cp.wait()

    return pl.pallas_call(
        body, name="grad_sibling_swap", out_shape=[_sds(h.shape) for h in halves],
        in_specs=[pl.BlockSpec(memory_space=pltpu.HBM)] * n, out_specs=[pl.BlockSpec(memory_space=pltpu.HBM)] * n,
        scratch_shapes=[pltpu.SemaphoreType.DMA((n,)), pltpu.SemaphoreType.DMA((n,))],
    )(*halves)


def _pad_rows(v, width):
    flat = v.reshape(-1)
    rows = -(-flat.shape[0] // width)
    rows = -(-rows // 8) * 8
    return jnp.pad(flat, (0, rows * width - flat.shape[0])).reshape(rows, width)


def _size(shape):
    n = 1
    for dim in shape:
        n *= dim
    return n


def _row_pack(arrs):
    pieces = []
    for a in arrs:
        rows = -(-a.size // D)
        pieces.append(jnp.pad(a.reshape(-1), (0, rows * D - a.size)).reshape(rows, D))
    total = sum(p.shape[0] for p in pieces)
    if total % 8:
        pieces.append(jnp.zeros((8 - total % 8, D), F32))
    return jnp.concatenate(pieces, axis=0)


def _row_unpack(packed, shapes):
    out, r0 = [], 0
    for shp in shapes:
        n = _size(shp)
        rows = -(-n // D)
        out.append(packed[r0:r0 + rows].reshape(-1)[:n].reshape(shp))
        r0 += rows
    return out


def _block_diag(w):
    eye = jnp.eye(8, dtype=w.dtype)
    return (w[:, :, None, :] * eye[:, None, :, None]).reshape(RGW, RGW)


def _diag_blocks(dense):
    r = dense.reshape(8, 64, 8, 64)
    return jnp.stack([r[n, :, n, :] for n in range(8)])


def _lane_row(v8):
    return jnp.zeros((1, BAP), F32).at[0, 8:16].set(v8.reshape(8))


def _chip_sums(gs, lands, names, c_arr):
    return [_chip_sum(g, l, c_arr, "chip_sum_" + n) for g, l, n in zip(gs, lands, names)]


def _reduce_parts(gs, names, c_arr, tag):
    return _chip_sums(gs, _comm_call(_exchange_comm(gs), "grad_sibling_exchange_" + tag), names, c_arr)


def _local_step(x, target, sw, ffn1_w, later_shards, c_arr):
    (g1, gmix, rg_cw8, rg_cb, wgates, gbias, lam_row, gdn_cw8, alog_row, dtb_row, gn, g2, gfin) = sw
    wg1, wu1, wd1 = ffn1_w

    (x1, a1, b1, fb1), gathered = _ffn_fwd(x, g1, wg1, wu1, wd1, "ffn1_fwd", comm=_weights_gather_comm(later_shards))
    win_sh, wout_sh, wg2, wu2, wd2 = _all_shards(gathered)
    w_in_full = jnp.transpose(win_sh, (1, 0, 2)).reshape(D, NSH * INSH)
    wout = wout_sh.reshape(D, D)
    w_in_groups = (w_in_full[:, 0:512], w_in_full[:, 512:1024], w_in_full[:, 1024:2560], w_in_full[:, 2560:3072],
                   jnp.pad(w_in_full[:, 3072:3088], ((0, 0), (0, BAP - BAW))))
    h2, p_rgx, p_gate, p_qkv, p_z, p_ba = _inproj(x1, gmix, w_in_groups, "in_proj")
    c_rg = _conv(p_rgx, rg_cw8, rg_cb, "rg_conv")
    c_qkv = _conv(p_qkv, gdn_cw8, jnp.zeros((1, QKVW), F32), "gdn_conv")
    a0, bb0, a1s, bb1, q, k, v, bg = _mix_prep(c_rg, c_qkv, p_ba, wgates, gbias, lam_row, alog_row, dtb_row, "mix_prep")
    tmat, gu, gw, gqd, gkd, gat, gcd = _gdn_local_fwd(q, k, v, bg, "gdn_local_fwd")
    of, s0, vn0, ob, s1, vn1, hf, hb = _gdn_seq_fwd(gu, gw, gqd, gkd, gat, gcd, "gdn_seq_fwd", scan=(a0, bb0, a1s, bb1))
    x2, ymix = _outproj(x1, hf, hb, p_gate, of, ob, p_z, gn, wout, "out_proj")
    (x3, a2, b2, fb2), _ = _ffn_fwd(x2, g2, wg2, wu2, wd2, "ffn2_fwd")
    dx3, dob2, loss_blk, d_gfin = _loss_head(x3, target, gfin, "loss_head")

    dx2, d_g2, hb2, dab2, dbb2, _ = _ffn_bwd(x2, dx3, dob2, g2, a2, b2, wg2, wu2, wd2, "ffn2_bwd")
    d_ffn2 = [_tn(dab2, hb2, "ffn2_dwg"), _tn(dbb2, hb2, "ffn2_dwu"), _tn(fb2, dob2, "ffn2_dwd")]

    (d_hr, d_gate, d_os, d_z, d_gn, dx2b), lands = _outproj_bwd(dx2, hf, hb, p_gate, of, ob, p_z, gn, wout, "out_proj_bwd",
                                                               comm=_exchange_comm(d_ffn2))
    parts_ffn2 = _chip_sums(d_ffn2, lands, _BIG_NAMES[5:8], c_arr)
    d_wout = _tn(ymix, dx2b, "dw_out")[0]

    sg = _gdn_seq_bwd(d_os, gw, gqd, gkd, gat, gcd, (s0, s1), (vn0, vn1), "gdn_seq_bwd", scan=(a1s, d_hr, a0, d_hr))
    lam1, lam0 = sg[10:12]
    d_xc, d_pre, xcb, d_gbias, d_lam = _gates_bwd(c_rg, wgates, gbias, lam_row, lam0, lam1, hf, hb, "rg_gates_bwd")
    d_wgates = _tn(xcb, d_pre, "dw_gates")[0]
    d_prgx, d_rgcw8, d_rgcb = _conv_bwd(p_rgx, d_xc, rg_cw8, "rg_conv_bwd")

    (dq, dk, dv, dbg), lands_ffn2 = _gdn_local_bwd(q, k, v, bg, tmat, d_os, (vn0, vn1), (sg[0:5], sg[5:10]), "gdn_local_bwd",
                                                  comm=_scatter_comm(parts_ffn2))
    d_cqkv, d_pba, d_alog, d_dtb = _prep_bwd(c_qkv, p_ba, alog_row, dtb_row, dq, dk, dv, dbg, "gdn_prep_bwd")
    d_pqkv, d_gdncw8, _ = _conv_bwd(p_qkv, d_cqkv, gdn_cw8, "gdn_conv_bwd")

    dps = (d_prgx, d_gate, d_pqkv, d_z, d_pba)
    dx1, dob1, d_gmix = _inproj_bwd(x1, dx2, gmix, dps, w_in_groups, "in_proj_bwd")
    d_win_groups = [_tn(h2, dp, "dw_in_%d" % i)[0] for i, dp in enumerate(dps)]
    d_win = jnp.concatenate(d_win_groups[:4] + [d_win_groups[4][:, :BAW]], axis=1)
    d_mix = [jnp.transpose(d_win.reshape(D, NSH, INSH), (1, 0, 2)), d_wout.reshape(NSH, OUTSH, D)]

    small = dict(
        mix_norm=d_gmix, rg_conv_w=d_rgcw8[:4], rg_conv_b=d_rgcb,
        rg_gate_a_w=jnp.stack([_diag_blocks(d_wgates[:, RGW * i:RGW * (i + 1)]) for i in (0, 1)]),
        rg_gate_x_w=jnp.stack([_diag_blocks(d_wgates[:, RGW * i:RGW * (i + 1)]) for i in (2, 3)]),
        rg_gate_a_b=d_gbias[0, :2 * RGW].reshape(2, RGW), rg_gate_x_b=d_gbias[0, 2 * RGW:].reshape(2, RGW),
        rg_lambda=d_lam.reshape(2, RGW), gdn_conv_w=d_gdncw8[:4],
        gdn_a_log=d_alog[0, 8:16].reshape(2, NH), gdn_dt_bias=d_dtb[0, 8:16].reshape(2, NH),
        gdn_norm=d_gn, ffn2_norm=d_g2, final_norm=d_gfin)
    small_pack = _row_pack([small[n] for n in _SMALL_NAMES[1:]])

    riders = _join_comm(_exchange_comm(d_mix), _gather_comm([small_pack], pltpu.HBM, [None]))
    gx, d_g1, hb1, dab1, dbb1, carried = _ffn_bwd(x, dx1, dob1, g1, a1, b1, wg1, wu1, wd1, "ffn1_bwd", comm=riders)
    parts_mix = _chip_sums(d_mix, carried[0:2], _BIG_NAMES[3:5], c_arr)
    d_wg1, lands_mix = _tn(dab1, hb1, "ffn1_dwg", comm=_scatter_comm(parts_mix))
    parts_wg1 = _reduce_parts([d_wg1], _BIG_NAMES[0:1], c_arr, "ffn1_gate")
    d_wu1, lands_wg1 = _tn(dbb1, hb1, "ffn1_dwu", comm=_scatter_comm(parts_wg1))
    parts_wu1 = _reduce_parts([d_wu1], _BIG_NAMES[1:2], c_arr, "ffn1_up")
    d_wd1, lands_wu1 = _tn(fb1, dob1, "ffn1_dwd", comm=_scatter_comm(parts_wu1))
    parts_wd1 = _reduce_parts([d_wd1], _BIG_NAMES[2:3], c_arr, "ffn1_down")
    lands_ffn1 = lands_wg1 + lands_wu1 + _comm_call(_scatter_comm(parts_wd1), "grad_chip_scatter_ffn1_down")

    halves = [_sum_slots(l, "sum_chips_" + n) for l, n in zip(lands_ffn1 + lands_mix + lands_ffn2, _BIG_NAMES)]
    small_shapes = [small[n].shape for n in _SMALL_NAMES[1:]]
    return loss_blk, gx, halves, d_g1, carried[2], small_shapes


_SMALL_NAMES = ("ffn1_norm", "mix_norm", "rg_conv_w", "rg_conv_b", "rg_gate_a_w", "rg_gate_a_b", "rg_gate_x_w",
                "rg_gate_x_b", "rg_lambda", "gdn_conv_w", "gdn_a_log", "gdn_dt_bias", "gdn_norm", "ffn2_norm", "final_norm")
_SMALL_SHARDED = dict(rg_conv_w=128, rg_gate_a_b=128, rg_gate_x_b=128, rg_lambda=128, gdn_conv_w=384)
_OUT_ORDER = ("ffn1_norm", "ffn1_w_gate", "ffn1_w_up", "ffn1_w_down", "mix_norm", "w_in", "w_out", "rg_conv_w", "rg_conv_b",
              "rg_gate_a_w", "rg_gate_a_b", "rg_gate_x_w", "rg_gate_x_b", "rg_lambda", "gdn_conv_w", "gdn_a_log",
              "gdn_dt_bias", "gdn_norm", "ffn2_norm", "ffn2_w_gate", "ffn2_w_up", "ffn2_w_down", "final_norm")
_BIG_NAMES = ("ffn1_w_gate", "ffn1_w_up", "ffn1_w_down", "w_in", "w_out", "ffn2_w_gate", "ffn2_w_up", "ffn2_w_down")
_TRANSPOSED = ("ffn1_w_gate", "ffn1_w_up", "ffn2_w_gate", "ffn2_w_up")


def kernel(x, ffn1_norm, ffn1_w_gate, ffn1_w_up, ffn1_w_down, mix_norm, w_in, w_out, rg_conv_w, rg_conv_b, rg_gate_a_w, rg_gate_a_b, rg_gate_x_w, rg_gate_x_b, rg_lambda, gdn_conv_w, gdn_a_log, gdn_dt_bias, gdn_norm, ffn2_norm, ffn2_w_gate, ffn2_w_up, ffn2_w_down, final_norm, loss_target, m_ffn1_norm, m_ffn1_w_gate, m_ffn1_w_up, m_ffn1_w_down, m_mix_norm, m_w_in, m_w_out, m_rg_conv_w, m_rg_conv_b, m_rg_gate_a_w, m_rg_gate_a_b, m_rg_gate_x_w, m_rg_gate_x_b, m_rg_lambda, m_gdn_conv_w, m_gdn_a_log, m_gdn_dt_bias, m_gdn_norm, m_ffn2_norm, m_ffn2_w_gate, m_ffn2_w_up, m_ffn2_w_down, m_final_norm, v_ffn1_norm, v_ffn1_w_gate, v_ffn1_w_up, v_ffn1_w_down, v_mix_norm, v_w_in, v_w_out, v_rg_conv_w, v_rg_conv_b, v_rg_gate_a_w, v_rg_gate_a_b, v_rg_gate_x_w, v_rg_gate_x_b, v_rg_lambda, v_gdn_conv_w, v_gdn_a_log, v_gdn_dt_bias, v_gdn_norm, v_ffn2_norm, v_ffn2_w_gate, v_ffn2_w_up, v_ffn2_w_down, v_final_norm):
    args = dict(locals())
    w = {n: args[n] for n in _OUT_ORDER}
    mom = {n: args["m_" + n] for n in _OUT_ORDER}
    var = {n: args["v_" + n] for n in _OUT_ORDER}
    xi, yi, ci = _mesh_pos()
    shard = 2 * xi + yi

    big_bf16 = [w[n][0].astype(BF16) for n in _BIG_NAMES]
    sm_local = _pad_rows(jnp.concatenate([w[n][0].reshape(-1) for n in _SMALL_SHARDED]), 128)
    first = _comm_call(_gather_comm(big_bf16[0:3] + [sm_local], pltpu.HBM, [t.shape[0] // 2 for t in big_bf16[0:3]] + [None]),
                       "gather_first_weights")
    ffn1_w = _all_shards(first[0:3])
    sm_all = first[3][0::2].reshape(NSH, -1)
    sm_full, off = {}, 0
    for n, wd_ in _SMALL_SHARDED.items():
        rows = w[n].shape[1]
        piece = sm_all[:, off:off + rows * wd_].reshape(NSH, rows, wd_)
        sm_full[n] = jnp.transpose(piece, (1, 0, 2)).reshape(rows, NSH * wd_)
        off += rows * wd_

    wa, wx = rg_gate_a_w[0], rg_gate_x_w[0]
    wgates = jnp.concatenate([_block_diag(wa[0]), _block_diag(wa[1]), _block_diag(wx[0]), _block_diag(wx[1])],
                             axis=1).astype(BF16)
    gbias = jnp.concatenate([sm_full["rg_gate_a_b"].reshape(1, -1), sm_full["rg_gate_x_b"].reshape(1, -1)], axis=1)
    sw = (ffn1_norm, mix_norm, jnp.pad(sm_full["rg_conv_w"], ((0, 4), (0, 0))), rg_conv_b, wgates, gbias,
          sm_full["rg_lambda"].reshape(1, -1), jnp.pad(sm_full["gdn_conv_w"], ((0, 4), (0, 0))), _lane_row(gdn_a_log),
          _lane_row(gdn_dt_bias), gdn_norm, ffn2_norm, final_norm.reshape(1, D))
    c_arr = ci.reshape(1).astype(jnp.int32)

    loss_blk, gx, halves, d_g1, small_packs, small_shapes = _local_step(x[0], loss_target[0], sw, ffn1_w, big_bf16[3:], c_arr)
    loss = lax.psum(loss_blk[0, 0], ("x", "y", "c"))
    grads = {}

    g1_all = _gather_small(jnp.pad(d_g1, ((0, 7), (0, 0))), "gather_ffn1_norm_grad")
    sm_sums = [_sum_slots(g1_all, "ffn1_norm_grad_sum")[0:1]] + _row_unpack(_sum_slots(small_packs, "small_grad_sum"), small_shapes)
    for n, g in zip(_SMALL_NAMES, sm_sums):
        if n in _SMALL_SHARDED:
            wd_ = _SMALL_SHARDED[n]
            g = lax.dynamic_slice_in_dim(g, shard * wd_, wd_, axis=1)
        grads[n] = g.reshape(w[n].shape)

    delta, new_m, new_v = {}, {}, {}
    for n, own, recv in zip(_BIG_NAMES, halves, _sibling_swap(halves)):
        to2d = jnp.transpose if n in _TRANSPOSED else (lambda t: t)
        outs4 = _adamw_halves(to2d(w[n][0]), own, recv, to2d(mom[n][0]), to2d(var[n][0]), c_arr, "adamw_" + n)
        grads[n], delta[n], new_m[n], new_v[n] = [to2d(o)[None] for o in outs4]
    packs = [_row_pack([t[n] for n in _SMALL_NAMES]) for t in (w, grads, mom, var)]
    sm_shapes = [w[n].shape for n in _SMALL_NAMES]
    for dst, src in zip((delta, new_m, new_v), _adamw(*packs, "adamw_small")):
        for n, val in zip(_SMALL_NAMES, _row_unpack(src, sm_shapes)):
            dst[n] = val

    outs = [loss, gx[None]]
    for group in (grads, delta, new_m, new_v):
        outs += [group[n] for n in _OUT_ORDER]
    return tuple(outs)
```

```python
import functools

import jax
import jax.numpy as jnp
from jax import lax
from jax.experimental import pallas as pl
from jax.experimental.pallas import tpu as pltpu

F32 = jnp.float32
BF16 = jnp.bfloat16
EPS = 1e-6
D = 1024
NSH = 4
FSH = 704
RGW = 512
QKVW = 1536
ZW = 512
BAW = 16
BAP = 128
INSH = 772
OUTSH = 256
CHUNK = 64
NH = 4
DH = 128
RG_C = 8.0
VMEM_LIMIT = 52 * 1024 * 1024
TN_VMEM_BUDGET = 40 * 1024 * 1024
ROW_TILE = 512
MESH = pl.DeviceIdType.MESH

ADAM_LR = 0.001
ADAM_B1 = 0.9
ADAM_B2 = 0.999
ADAM_EPS = 1e-08
ADAM_WD = 0.01
ADAM_STEP = 10


def _cparams(n_grid):
    return pltpu.CompilerParams(dimension_semantics=("arbitrary",) * n_grid, vmem_limit_bytes=VMEM_LIMIT)


def _sig(x):
    return 0.5 + 0.5 * jnp.tanh(0.5 * x)


def _sig_pos(x):
    return 1.0 / (1.0 + jnp.exp(-x))


def _softplus(x):
    return jnp.maximum(x, 0.0) + jnp.log(1.0 + jnp.exp(-jnp.abs(x)))


def _one_minus_sq_exp(la, a):
    y = 2.0 * la
    series = -y * (1.0 + y * (0.5 + y * (1.0 / 6 + y * (1.0 / 24 + y * (1.0 / 120 + y * (1.0 / 720))))))
    return jnp.where(y > -0.1, series, 1.0 - a * a)


_GELU_C = 0.7978845608028654


def _gelu(x):
    t = jnp.tanh(_GELU_C * (x + 0.044715 * x * x * x))
    return 0.5 * x * (1.0 + t)


def _gelu_grad(x):
    t = jnp.tanh(_GELU_C * (x + 0.044715 * x * x * x))
    return 0.5 * (1.0 + t) + 0.5 * x * (1.0 - t * t) * _GELU_C * (1.0 + 3 * 0.044715 * x * x)


def _silu_grad(x):
    s = _sig(x)
    return s * (1.0 + x * (1.0 - s))


def _dot(a, b):
    return jnp.dot(a.astype(BF16), b.astype(BF16), preferred_element_type=F32)


def _dot_nt(a, b):
    return lax.dot_general(a.astype(BF16), b.astype(BF16), (((1,), (1,)), ((), ())), preferred_element_type=F32)


def _dot_tn(a, b):
    return lax.dot_general(a.astype(BF16), b.astype(BF16), (((0,), (0,)), ((), ())), preferred_element_type=F32)


_NN = ((1,), (0,))
_NT = ((1,), (1,))
_TN = ((0,), (0,))


def _dg(a, b, dims):
    return lax.dot_general(a, b, (dims, ((), ())), preferred_element_type=F32)


def _split2(a):
    hi = a.astype(BF16)
    return hi, (a - hi.astype(F32)).astype(BF16)


def _dot3(a, b, dims=_NN):
    ah, al = _split2(a)
    bh, bl = _split2(b)
    return _dg(ah, bh, dims) + _dg(ah, bl, dims) + _dg(al, bh, dims)


def _dot_exact(e, x, dims, e_is_lhs):
    x0 = x.astype(BF16)
    r = x - x0.astype(F32)
    x1 = r.astype(BF16)
    x2 = (r - x1.astype(F32)).astype(BF16)
    eb = e.astype(BF16)
    if e_is_lhs:
        return _dg(eb, x0, dims) + _dg(eb, x1, dims) + _dg(eb, x2, dims)
    return _dg(x0, eb, dims) + _dg(x1, eb, dims) + _dg(x2, eb, dims)


def _rms(xv):
    r = lax.rsqrt(jnp.mean(xv * xv, axis=-1, keepdims=True) + EPS)
    return r, xv * r


def _rms_bwd(dy, xh, r, gain):
    dxh = dy * gain
    return r * (dxh - xh * jnp.mean(dxh * xh, axis=-1, keepdims=True))


def _colsum(v):
    return jnp.sum(v, axis=0, keepdims=True)


def _rows(t, c):
    return pl.BlockSpec((t, c), lambda i: (i, 0))


def _full(shape):
    n = len(shape)
    return pl.BlockSpec(shape, lambda i: (0,) * n)


def _sds(shape, dtype=F32):
    return jax.ShapeDtypeStruct(shape, dtype)


def _ffn_fwd(x, gain, wg, wu, wd, name, comm=None):
    s = x.shape[0]
    tm = min(256, s)

    def body(x_ref, g_ref, wg_ref, wu_ref, wd_ref, xo_ref, ga_ref, gb_ref, f_ref):
        xv = x_ref[...]
        _, xh = _rms(xv)
        h = (xh * g_ref[...]).astype(BF16)
        acc = None
        for j in range(NSH):
            a = jnp.dot(h, wg_ref[j], preferred_element_type=F32)
            b = jnp.dot(h, wu_ref[j], preferred_element_type=F32)
            sa = _sig(a)
            silu = a * sa
            fv = silu * b
            f = fv.astype(BF16)
            f_ref[j] = f
            ga_ref[j] = (sa * b + fv * (1.0 - sa)).astype(BF16)
            gb_ref[j] = silu.astype(BF16)
            part = jnp.dot(f, wd_ref[j], preferred_element_type=F32)
            acc = part if acc is None else acc + part
        xo_ref[...] = xv + 0.5 * acc

    hidden = pl.BlockSpec((NSH, tm, FSH), lambda i: (0, i, 0))
    return _pallas(
        body, comm, name=name, grid=(s // tm,),
        in_specs=[_rows(tm, D), _full((1, D)),
                  pl.BlockSpec((NSH, D, FSH), lambda i: (0, 0, 0), pipeline_mode=pl.Buffered(1)),
                  pl.BlockSpec((NSH, D, FSH), lambda i: (0, 0, 0), pipeline_mode=pl.Buffered(1)),
                  pl.BlockSpec((NSH, FSH, D), lambda i: (0, 0, 0), pipeline_mode=pl.Buffered(1))],
        out_specs=[_rows(tm, D), hidden, hidden, hidden],
        out_shape=[_sds((s, D))] + [_sds((NSH, s, FSH), BF16)] * 3,
        scratch_shapes=[], args=(x, gain, wg, wu, wd))


def _ffn_bwd(x, dout, do, gain, ga, gb, wg, wu, wd, name, comm=None):
    s = x.shape[0]
    tm = min(512, s)

    def hidden(do_ref, ga_ref, gb_ref, wd_ref, da_ref, db_ref):
        dov = do_ref[...]
        for j in range(NSH):
            df = _dot_nt(dov, wd_ref[j])
            da_ref[j] = (df * ga_ref[j].astype(F32)).astype(BF16)
            db_ref[j] = (df * gb_ref[j].astype(F32)).astype(BF16)

    sh = pl.BlockSpec((NSH, tm, FSH), lambda i: (0, i, 0))
    (da, db), carried = _pallas(
        hidden, comm, name=name + "_hidden", grid=(s // tm,),
        in_specs=[_rows(tm, D), sh, sh, pl.BlockSpec((NSH, FSH, D), lambda i: (0, 0, 0), pipeline_mode=pl.Buffered(1))],
        out_specs=[sh, sh], out_shape=[_sds((NSH, s, FSH), BF16)] * 2, scratch_shapes=[], args=(do, ga, gb, wd))

    def inputs(x_ref, d_ref, g_ref, da_ref, db_ref, wg_ref, wu_ref, dx_ref, dg_ref, h_ref):
        @pl.when(pl.program_id(0) == 0)
        def _():
            dg_ref[...] = jnp.zeros_like(dg_ref)

        dh = jnp.zeros((tm, D), F32)
        for j in range(NSH):
            dh = dh + _dot_nt(da_ref[j], wg_ref[j]) + _dot_nt(db_ref[j], wu_ref[j])
        r, xh = _rms(x_ref[...])
        gv = g_ref[...]
        h_ref[...] = (xh * gv).astype(BF16)
        dg_ref[...] += _colsum(dh * xh)
        dx_ref[...] = d_ref[...] + _rms_bwd(dh, xh, r, gv)

    grads = pl.BlockSpec((NSH, tm, FSH), lambda i: (0, i, 0))
    resident = pl.BlockSpec((NSH, D, FSH), lambda i: (0, 0, 0), pipeline_mode=pl.Buffered(1))
    dx, dg, h = pl.pallas_call(
        inputs, name=name + "_input", grid=(s // tm,),
        in_specs=[_rows(tm, D), _rows(tm, D), _full((1, D)), grads, grads, resident, resident],
        out_specs=[_rows(tm, D), _full((1, D)), _rows(tm, D)],
        out_shape=[_sds((s, D)), _sds((1, D)), _sds((s, D), BF16)], compiler_params=_cparams(1),
    )(x, dout, gain, da, db, wg, wu)
    return dx, dg, h, da, db, carried


def _tn(a, b, name, comm=None):
    a_g = a.ndim == 3
    b_g = b.ndim == 3
    g = a.shape[0] if a_g else (b.shape[0] if b_g else 1)
    s, k = a.shape[-2:]
    n = b.shape[-1]
    ts = min(4096, s)
    while ts > 256 and 2 * ts * (k + n) * max(a.dtype.itemsize, b.dtype.itemsize) + 2 * k * n * 4 > TN_VMEM_BUDGET:
        ts //= 2

    def body(a_ref, b_ref, o_ref):
        @pl.when(pl.program_id(1) == 0)
        def _():
            o_ref[...] = jnp.zeros_like(o_ref)

        av = a_ref[0] if a_g else a_ref[...]
        bv = b_ref[0] if b_g else b_ref[...]
        o_ref[0] += _dot_tn(av, bv)

    a_spec = pl.BlockSpec((1, ts, k), lambda gi, si: (gi, si, 0)) if a_g else pl.BlockSpec((ts, k), lambda gi, si: (si, 0))
    b_spec = pl.BlockSpec((1, ts, n), lambda gi, si: (gi, si, 0)) if b_g else pl.BlockSpec((ts, n), lambda gi, si: (si, 0))
    outs, carried = _pallas(body, comm, name=name, grid=(g, s // ts), in_specs=[a_spec, b_spec],
                            out_specs=[pl.BlockSpec((1, k, n), lambda gi, si: (gi, 0, 0))], out_shape=[_sds((g, k, n))],
                            scratch_shapes=[], args=(a, b))
    return outs[0] if comm is None else (outs[0], carried)


_P_WIDTHS = (RGW, RGW, QKVW, ZW, BAP)


def _inproj(x1, gain, ws, name):
    s = x1.shape[0]
    tm = min(ROW_TILE, s)

    def body(x_ref, g_ref, *refs):
        w_refs = refs[:5]
        h_ref = refs[5]
        p_refs = refs[6:]
        _, xh = _rms(x_ref[...])
        h = (xh * g_ref[...]).astype(BF16)
        h_ref[...] = h
        for w_ref, p_ref in zip(w_refs, p_refs):
            p_ref[...] = jnp.dot(h, w_ref[...], preferred_element_type=F32)

    return pl.pallas_call(
        body, name=name, grid=(s // tm,),
        in_specs=[_rows(tm, D), _full((1, D))] + [_full((D, w)) for w in _P_WIDTHS],
        out_specs=[_rows(tm, D)] + [_rows(tm, w) for w in _P_WIDTHS],
        out_shape=[_sds((s, D), BF16)] + [_sds((s, w)) for w in _P_WIDTHS],
        compiler_params=_cparams(1),
    )(x1, gain, *ws)


def _inproj_bwd(x1, dx2, gain, dps, ws, name):
    s = x1.shape[0]
    tm = min(ROW_TILE, s)

    def body(x_ref, d_ref, g_ref, *refs):
        dp_refs = refs[:5]
        w_refs = refs[5:10]
        dx_ref, dxh_ref, dg_ref = refs[10:]

        @pl.when(pl.program_id(0) == 0)
        def _():
            dg_ref[...] = jnp.zeros_like(dg_ref)

        dh = jnp.zeros((tm, D), F32)
        for dp_ref, w_ref in zip(dp_refs, w_refs):
            dh = dh + _dot_nt(dp_ref[...], w_ref[...])
        r, xh = _rms(x_ref[...])
        dg_ref[...] += _colsum(dh * xh)
        dx = d_ref[...] + _rms_bwd(dh, xh, r, g_ref[...])
        dx_ref[...] = dx
        dxh_ref[...] = (0.5 * dx).astype(BF16)

    return pl.pallas_call(
        body, name=name, grid=(s // tm,),
        in_specs=[_rows(tm, D), _rows(tm, D), _full((1, D))] + [_rows(tm, w) for w in _P_WIDTHS]
        + [_full((D, w)) for w in _P_WIDTHS],
        out_specs=[_rows(tm, D), _rows(tm, D), _full((1, D))],
        out_shape=[_sds((s, D)), _sds((s, D), BF16), _sds((1, D))],
        compiler_params=_cparams(1),
    )(x1, dx2, gain, *dps, *ws)


def _halo_specs(s, t, c):
    nb8 = s // 8
    tb = t // 8
    prev = pl.BlockSpec((8, c), lambda i: (jnp.maximum(i * tb - 1, 0), 0))
    nxt = pl.BlockSpec((8, c), lambda i: (jnp.minimum((i + 1) * tb, nb8 - 1), 0))
    return prev, nxt


def _edge_masks(nb):
    i = pl.program_id(0)
    return jnp.where(i > 0, 1.0, 0.0).astype(F32), jnp.where(i < nb - 1, 1.0, 0.0).astype(F32)


def _shifted(xx, off, t):
    n = t + 16
    sh = (-off) % n
    rolled = xx if sh == 0 else pltpu.roll(xx, sh, 0)
    return rolled[8:8 + t]


def _conv(x, w8, bias, name):
    s, c = x.shape
    t = min(ROW_TILE, s)
    nb = s // t

    def body(x_ref, xp_ref, xn_ref, w_ref, b_ref, o_ref):
        pm, nm = _edge_masks(nb)
        for c0 in range(0, c, 512):
            cols = slice(c0, c0 + 512)
            xx = jnp.concatenate([xp_ref[:, cols] * pm, x_ref[:, cols], xn_ref[:, cols] * nm], axis=0)
            acc = jnp.zeros((t, 512), F32) + b_ref[:, cols]
            for j in range(4):
                acc = acc + w_ref[j:j + 1, cols] * _shifted(xx, j - 2, t)
            o_ref[:, cols] = acc

    prev, nxt = _halo_specs(s, t, c)
    return pl.pallas_call(
        body, name=name, grid=(nb,),
        in_specs=[_rows(t, c), prev, nxt, _full((8, c)), _full((1, c))],
        out_specs=_rows(t, c), out_shape=_sds((s, c)), compiler_params=_cparams(1),
    )(x, x, x, w8, bias)


def _conv_bwd(x, dc, w8, name):
    s, c = x.shape
    t = min(ROW_TILE, s)
    nb = s // t

    def body(x_ref, d_ref, dp_ref, dn_ref, w_ref, dx_ref, dw_ref, db_ref):
        @pl.when(pl.program_id(0) == 0)
        def _():
            dw_ref[...] = jnp.zeros_like(dw_ref)
            db_ref[...] = jnp.zeros_like(db_ref)

        pm, nm = _edge_masks(nb)
        for c0 in range(0, c, 512):
            cols = slice(c0, c0 + 512)
            dd = jnp.concatenate([dp_ref[:, cols] * pm, d_ref[:, cols], dn_ref[:, cols] * nm], axis=0)
            xv = x_ref[:, cols]
            acc = jnp.zeros((t, 512), F32)
            for j in range(4):
                dsh = _shifted(dd, 2 - j, t)
                acc = acc + w_ref[j:j + 1, cols] * dsh
                dw_ref[j:j + 1, cols] += _colsum(dsh * xv)
            dx_ref[:, cols] = acc.astype(BF16)
            db_ref[:, cols] += _colsum(d_ref[:, cols])

    prev, nxt = _halo_specs(s, t, c)
    return pl.pallas_call(
        body, name=name, grid=(nb,),
        in_specs=[_rows(t, c), _rows(t, c), prev, nxt, _full((8, c))],
        out_specs=[_rows(t, c), _full((8, c)), _full((1, c))],
        out_shape=[_sds((s, c), BF16), _sds((8, c)), _sds((1, c))], compiler_params=_cparams(1),
    )(x, dc, dc, dc, w8)


def _rg_gates(xc, pre, lam_row):
    sp8 = RG_C * _softplus(-lam_row)
    out = []
    for d in range(2):
        r = _sig_pos(pre[:, RGW * d:RGW * (d + 1)])
        gi = _sig(pre[:, 2 * RGW + RGW * d:2 * RGW + RGW * (d + 1)])
        la = -r * sp8[:, RGW * d:RGW * (d + 1)]
        a = jnp.exp(la)
        mult = jnp.sqrt(_one_minus_sq_exp(la, a))
        out.append((r, gi, a, mult))
    return out


def _mix_prep(c_rg, c_qkv, p_ba, wgates, gbias, lam_row, alog_row, dtb_row, name):
    s = c_rg.shape[0]
    t = min(ROW_TILE, s)

    def body(xc_ref, cq_ref, pc_ref, wg_ref, gb_ref, lam_ref, alog_ref, dtb_ref,
             a0_ref, b0_ref, a1_ref, b1_ref, q_ref, k_ref, v_ref, bg_ref):
        xc = xc_ref[...]
        pre = _dot(xc, wg_ref[...]) + gb_ref[...]
        gates = _rg_gates(xc, pre, lam_ref[...])
        for (r, gi, a, mult), a_ref, b_ref in zip(gates, (a0_ref, a1_ref), (b0_ref, b1_ref)):
            a_ref[...] = a
            b_ref[...] = mult * gi * xc
        cq = cq_ref[...]
        sq = cq * _sig(cq)
        for h in range(NH):
            sl = slice(DH * h, DH * (h + 1))
            qh = sq[:, sl]
            q_ref[:, sl] = qh * lax.rsqrt(jnp.sum(qh * qh, axis=-1, keepdims=True) + EPS) * (DH ** -0.5)
            kh = sq[:, RGW + DH * h:RGW + DH * (h + 1)]
            k_ref[:, sl] = kh * lax.rsqrt(jnp.sum(kh * kh, axis=-1, keepdims=True) + EPS)
        v_ref[...] = sq[:, 2 * RGW:]
        pc = pc_ref[...]
        lane = lax.broadcasted_iota(jnp.int32, pc.shape, 1)
        beta = _sig(pc)
        g = -jnp.exp(alog_ref[...]) * _softplus(pc + dtb_ref[...])
        bg_ref[...] = jnp.where(lane < 8, beta, jnp.where(lane < 16, g, 0.0))

    return pl.pallas_call(
        body, name=name, grid=(s // t,),
        in_specs=[_rows(t, RGW), _rows(t, QKVW), _rows(t, BAP), _full((RGW, 4 * RGW)), _full((1, 4 * RGW)),
                  _full((1, 2 * RGW)), _full((1, BAP)), _full((1, BAP))],
        out_specs=[_rows(t, RGW)] * 7 + [_rows(t, BAP)],
        out_shape=[_sds((s, RGW))] * 7 + [_sds((s, BAP))],
        compiler_params=_cparams(1),
    )(c_rg, c_qkv, p_ba, wgates, gbias, lam_row, alog_row, dtb_row)


def _block_scan(av, bv, row, downwards):
    for k in (1, 2, 4):
        sh = (8 - k) if downwards else k
        m = (row < 8 - k) if downwards else (row >= k)
        a_s = pltpu.roll(av, sh, 0)
        b_s = pltpu.roll(bv, sh, 0)
        bv = jnp.where(m, av * b_s + bv, bv)
        av = jnp.where(m, av * a_s, av)
    return av, bv


def _gates_bwd(xc, wgates, gbias, lam_row, lam0, lam1, hf, hb, name):
    s = xc.shape[0]
    t = min(ROW_TILE, s)
    nb = s // t

    def body(xc_ref, wg_ref, gb_ref, lam_ref, l0_ref, l1_ref, hf_ref, hfp_ref, hfn_ref, hb_ref, hbp_ref, hbn_ref,
             dxc_ref, dpre_ref, xcb_ref, dgb_ref, dlam_ref):
        @pl.when(pl.program_id(0) == 0)
        def _():
            dgb_ref[...] = jnp.zeros_like(dgb_ref)
            dlam_ref[...] = jnp.zeros_like(dlam_ref)

        pm, nm = _edge_masks(nb)
        h_prev = _shifted(jnp.concatenate([hfp_ref[...] * pm, hf_ref[...], hfn_ref[...] * nm], axis=0), -1, t)
        h_next = _shifted(jnp.concatenate([hbp_ref[...] * pm, hb_ref[...], hbn_ref[...] * nm], axis=0), 1, t)
        h_shift = (h_prev, h_next)
        xv = xc_ref[...]
        pre = _dot(xv, wg_ref[...]) + gb_ref[...]
        lam_row_v = lam_ref[...]
        sp8 = RG_C * _softplus(-lam_row_v)
        dsp_dlam = -RG_C * _sig(-lam_row_v)
        gates = _rg_gates(xv, pre, lam_row_v)
        dxc = jnp.zeros((t, RGW), F32)
        dpre_r = []
        dpre_i = []
        for d, ((r, gi, a, mult), l_ref, hs) in enumerate(zip(gates, (l0_ref, l1_ref), h_shift)):
            dbb = l_ref[...]
            da = dbb * hs
            cs = slice(RGW * d, RGW * (d + 1))
            dmult = dbb * gi * xv
            dgi = dbb * mult * xv
            dxc = dxc + dbb * mult * gi
            dla = da * a - dmult * a * a / mult
            dr = -dla * sp8[:, cs]
            dlam_ref[:, cs] += _colsum(-dla * r) * dsp_dlam[:, cs]
            dpre_r.append(dr * r * (1.0 - r))
            dpre_i.append(dgi * gi * (1.0 - gi))
        dpre = jnp.concatenate(dpre_r + dpre_i, axis=1)
        dgb_ref[...] += _colsum(dpre)
        dpre_b = dpre.astype(BF16)
        dpre_ref[...] = dpre_b
        xcb_ref[...] = xv.astype(BF16)
        dxc_ref[...] = dxc + _dot_nt(dpre_b, wg_ref[...])

    prev, nxt = _halo_specs(s, t, RGW)
    return pl.pallas_call(
        body, name=name, grid=(s // t,),
        in_specs=[_rows(t, RGW), _full((RGW, 4 * RGW)), _full((1, 4 * RGW)), _full((1, 2 * RGW))] + [_rows(t, RGW)] * 2
        + [_rows(t, RGW), prev, nxt] * 2,
        out_specs=[_rows(t, RGW), _rows(t, 4 * RGW), _rows(t, RGW), _full((1, 4 * RGW)), _full((1, 2 * RGW))],
        out_shape=[_sds((s, RGW)), _sds((s, 4 * RGW), BF16), _sds((s, RGW), BF16), _sds((1, 4 * RGW)), _sds((1, 2 * RGW))],
        compiler_params=_cparams(1),
    )(xc, wgates, gbias, lam_row, lam0, lam1, hf, hf, hf, hb, hb, hb)


class _GdnMasks:
    def __init__(self, d):
        ri = lax.broadcasted_iota(jnp.int32, (CHUNK, CHUNK), 0)
        ci = lax.broadcasted_iota(jnp.int32, (CHUNK, CHUNK), 1)
        self.incl = (ri >= ci) if d == 0 else (ri <= ci)
        self.strict = (ri > ci) if d == 0 else (ri < ci)
        b16 = jnp.right_shift(ri, 4) == jnp.right_shift(ci, 4)
        b32 = jnp.right_shift(ri, 5) == jnp.right_shift(ci, 5)
        self.diag16 = b16
        self.off32 = jnp.logical_and(b32, jnp.logical_not(b16))
        self.off64 = jnp.logical_not(b32)
        self.eye = jnp.where(ri == ci, 1.0, 0.0).astype(F32)
        self.tri = jnp.where(self.incl, 1.0, 0.0).astype(F32)
        self.last = CHUNK - 1 if d == 0 else 0


def _tri_inv(lmat, m):
    return _tri_inv_many([lmat], [m])[0]


def _tri_inv_many(lmats, masks):
    n = len(lmats)
    ns = [jnp.where(masks[i].diag16, lmats[i], 0.0) for i in range(n)]
    ps = [masks[i].eye - ns[i] for i in range(n)]
    qs = [_dot3(ns[i], ns[i]) for i in range(n)]
    for step in range(3):
        ps = [_dot3(ps[i], masks[i].eye + qs[i]) for i in range(n)]
        if step < 2:
            qs = [_dot3(qs[i], qs[i]) for i in range(n)]
    for off in ("off32", "off64"):
        ts = [_dot3(ps[i], jnp.where(getattr(masks[i], off), lmats[i], 0.0)) for i in range(n)]
        ps = [ps[i] - _dot3(ts[i], ps[i]) for i in range(n)]
    return ps


def _chunk_cumsums(m, bgv):
    return _dot_exact(m.tri, bgv, _NN, True), _dot_exact(m.tri, bgv, ((0,), (1,)), False)


class _GdnHead:
    def __init__(self, qh, kh, vh, kk, q0, bg, gcs, gcs_t, d, h, m):
        cb = 4 * d + h
        cg = 8 + 4 * d + h
        self.q, self.k, self.v = qh, kh, vh
        self.beta = bg[:, cb:cb + 1]
        gcol = gcs[:, cg:cg + 1]
        grow = gcs_t[cg:cg + 1, :]
        gl = gcs[m.last:m.last + 1, cg:cg + 1]
        self.decay = jnp.exp(jnp.where(m.incl, gcol - grow, -1e30))
        self.kb = kh * self.beta
        self.vb = vh * self.beta
        self.a0 = kk * self.beta
        self.q0 = q0
        self.lmat = jnp.where(m.strict, self.a0 * self.decay, 0.0)
        self.attn = self.q0 * self.decay
        self.eg = jnp.exp(gcol)
        self.ek = jnp.exp(gl - gcol)
        self.cd = jnp.exp(gl)
        self.kg = self.kb * self.eg
        self.qd = qh * self.eg
        self.kd = kh * self.ek


HW = NH * DH
SEQ_CB = 4
LOCAL_CB = 4


def _head(h):
    return slice(DH * h, DH * (h + 1))


def _gdn_local_fwd(q, k, v, bg, name):
    s = q.shape[0]
    n = s // CHUNK
    cb = min(LOCAL_CB, n)

    def body(q_ref, k_ref, v_ref, bg_ref, t_ref, u_ref, w_ref, qd_ref, kd_ref, at_ref, cd_ref):
        masks = [_GdnMasks(d) for d in range(2)]
        inst = []
        for jj in range(cb):
            rows = slice(CHUNK * jj, CHUNK * (jj + 1))
            bgv = bg_ref[rows, :]
            qs = [q_ref[rows, _head(h)] for h in range(NH)]
            ks = [k_ref[rows, _head(h)] for h in range(NH)]
            kk = [_dot_nt(ks[h], ks[h]) for h in range(NH)]
            q0 = [_dot_nt(qs[h], ks[h]) for h in range(NH)]
            for d, m in enumerate(masks):
                gcs, gcs_t = _chunk_cumsums(m, bgv)
                for h in range(NH):
                    c = _GdnHead(qs[h], ks[h], v_ref[rows, _head(h)], kk[h], q0[h], bgv, gcs, gcs_t, d, h, m)
                    inst.append((jj, rows, d, h, m, c))
        tms = _tri_inv_many([it[-1].lmat for it in inst], [it[-2] for it in inst])
        for (jj, rows, d, h, m, c), tm in zip(inst, tms):
            sl = _head(h)
            t_ref[jj, d, h] = tm
            u_ref[d, rows, sl] = _dot(tm, c.vb).astype(BF16)
            w_ref[d, rows, sl] = _dot(tm, c.kg).astype(BF16)
            qd_ref[d, rows, sl] = c.qd.astype(BF16)
            kd_ref[d, rows, sl] = c.kd.astype(BF16)
            at_ref[jj, d, h] = c.attn.astype(BF16)
            cd_ref[jj, 4 * d + h:4 * d + h + 1, :] = jnp.broadcast_to(c.cd, (1, DH))

    tok = _rows(cb * CHUNK, HW)
    tok2 = pl.BlockSpec((2, cb * CHUNK, HW), lambda i: (0, i, 0))
    mat = pl.BlockSpec((cb, 2, NH, CHUNK, CHUNK), lambda i: (i, 0, 0, 0, 0))
    return pl.pallas_call(
        body, name=name, grid=(n // cb,), in_specs=[tok, tok, tok, _rows(cb * CHUNK, BAP)],
        out_specs=[mat, tok2, tok2, tok2, tok2, mat, pl.BlockSpec((cb, 8, DH), lambda i: (i, 0, 0))],
        out_shape=[_sds((n, 2, NH, CHUNK, CHUNK)), _sds((2, s, HW), BF16), _sds((2, s, HW), BF16), _sds((2, s, HW), BF16),
                   _sds((2, s, HW), BF16), _sds((n, 2, NH, CHUNK, CHUNK), BF16), _sds((n, 8, DH))],
        compiler_params=_cparams(1),
    )(q, k, v, bg)


def _seq_specs(s, order):
    n = s // CHUNK
    cb = min(SEQ_CB, n)
    nb = n // cb
    tb = cb * CHUNK

    def blk(d):
        return (lambda i: i) if order[d] else (lambda i: nb - 1 - i)

    def per_dir(make):
        return [make(d, blk(d)) for d in range(2)]

    tok2 = per_dir(lambda d, f: pl.BlockSpec((1, tb, HW), lambda i: (d, f(i), 0)))
    tok = per_dir(lambda d, f: pl.BlockSpec((tb, HW), lambda i: (f(i), 0)))
    mat = per_dir(lambda d, f: pl.BlockSpec((cb, 1, NH, CHUNK, CHUNK), lambda i: (f(i), d, 0, 0, 0)))
    cds = per_dir(lambda d, f: pl.BlockSpec((cb, 8, DH), lambda i: (f(i), 0, 0)))
    sts = per_dir(lambda d, f: pl.BlockSpec((cb, NH, DH, DH), lambda i: (f(i), 0, 0, 0)))
    dcd = per_dir(lambda d, f: pl.BlockSpec((cb, NH, DH), lambda i: (f(i), 0, 0)))
    return n, cb, nb, tok2, tok, mat, cds, sts, dcd


class _ScanRider:
    def __init__(self, af, bf, ar, br, shifted, tb, nb, up_spec, down_spec):
        s, c = af.shape
        self.shifted, self.t, self.c, self.nb = shifted, tb, c, nb
        self.args = [af, bf, ar, br]
        self.in_specs = [up_spec, up_spec, down_spec, down_spec]
        self.scratch = [pltpu.VMEM((16, c), F32)]
        if shifted:
            tb8 = tb // 8
            self.args += [af, ar]
            self.in_specs += [pl.BlockSpec((8, c), lambda i: (jnp.maximum(i * tb8 - 1, 0), 0)),
                              pl.BlockSpec((8, c), lambda i: (jnp.minimum((nb - i) * tb8, s // 8 - 1), 0))]
            self.scratch += [pltpu.VMEM((tb + 8, c), F32), pltpu.VMEM((tb + 8, c), F32)]
        self.out_specs = [up_spec, down_spec]
        self.out_shape = [_sds((s, c)), _sds((s, c))]

    def begin(self, in_refs, out_refs, scratch_refs):
        i = pl.program_id(0)
        self.carry = scratch_refs[0]

        @pl.when(i == 0)
        def _():
            self.carry[...] = jnp.zeros_like(self.carry)

        af_ref, self.bf_ref, ar_ref, self.br_ref = in_refs[0:4]
        self.hf_ref, self.hr_ref = out_refs
        self.a_up, self.a_dn = af_ref, ar_ref
        if self.shifted:
            t = self.t
            edge = jnp.where(i > 0, 1.0, 0.0).astype(F32)
            fbuf, rbuf = scratch_refs[1:3]
            fbuf[0:8, :] = in_refs[4][...] * edge
            fbuf[8:t + 8, :] = af_ref[...]
            rbuf[0:t, :] = ar_ref[...]
            rbuf[t:t + 8, :] = in_refs[5][...] * edge
            self.a_up, self.a_dn = fbuf, rbuf
        self.row = lax.broadcasted_iota(jnp.int32, (8, self.c), 0)
        self.cf, self.cr = self.carry[0:1, :], self.carry[8:9, :]

    def groups(self, lo, hi):
        ng = self.t // 8
        row = self.row
        for gi in range(lo, hi):
            rf, rr = 8 * gi, 8 * (ng - 1 - gi)
            if self.shifted:
                a_f = jnp.where(row > 0, pltpu.roll(self.a_up[rf + 8:rf + 16, :], 1, 0), pltpu.roll(self.a_up[rf:rf + 8, :], 1, 0))
                a_r = jnp.where(row < 7, pltpu.roll(self.a_dn[rr:rr + 8, :], 7, 0), pltpu.roll(self.a_dn[rr + 8:rr + 16, :], 7, 0))
            else:
                a_f, a_r = self.a_up[rf:rf + 8, :], self.a_dn[rr:rr + 8, :]
            a_f, b_f = _block_scan(a_f, self.bf_ref[rf:rf + 8, :], row, False)
            a_r, b_r = _block_scan(a_r, self.br_ref[rr:rr + 8, :], row, True)
            h_f = a_f * self.cf + b_f
            h_r = a_r * self.cr + b_r
            self.hf_ref[rf:rf + 8, :] = h_f
            self.hr_ref[rr:rr + 8, :] = h_r
            self.cf, self.cr = h_f[7:8, :], h_r[0:1, :]

    def end(self):
        self.carry[0:1, :] = self.cf
        self.carry[8:9, :] = self.cr


def _gdn_seq_fwd(u, w, qd, kd, at, cd, name, scan=None):
    s = u.shape[1]
    n, cb, nb, tok2, tok, mat, cds, sts, _ = _seq_specs(s, (True, False))
    rider = _ScanRider(*scan, False, cb * CHUNK, nb, tok[0], tok[1]) if scan else None
    ri = len(rider.args) if rider else 0

    def body(*refs):
        ins = (refs[0:6], refs[6:12])
        outs = (refs[12 + ri:15 + ri], refs[15 + ri:18 + ri])
        st = refs[18 + ri + (2 if rider else 0)]
        if rider:
            rider.begin(refs[12:12 + ri], refs[18 + ri:20 + ri], refs[21 + ri:])

        @pl.when(pl.program_id(0) == 0)
        def _():
            st[...] = jnp.zeros_like(st)

        for j in range(cb):
            items = []
            for d in range(2):
                jj = j if d == 0 else cb - 1 - j
                items += [(d, h, jj, slice(CHUNK * jj, CHUNK * (jj + 1)), _head(h)) for h in range(NH)]
            shs = [st[d, h] for d, h, _, _, _ in items]
            wss = [_dot(ins[d][1][0, rows, sl], sh) for (d, h, jj, rows, sl), sh in zip(items, shs)]
            vns = [ins[d][0][0, rows, sl].astype(F32) - ws for (d, h, jj, rows, sl), ws in zip(items, wss)]
            news = [sh * ins[d][5][jj, 4 * d + h:4 * d + h + 1, :] + _dot_tn(ins[d][3][0, rows, sl], vn)
                    for (d, h, jj, rows, sl), sh, vn in zip(items, shs, vns)]
            for (d, h, jj, rows, sl), sh, vn, new in zip(items, shs, vns, news):
                o_r, s_r, vn_r = outs[d]
                st[d, h] = new
                s_r[jj, h] = sh.astype(BF16)
                vn_r[rows, sl] = vn.astype(BF16)
                o_r[rows, sl] = _dot(ins[d][2][0, rows, sl], sh) + _dot(ins[d][4][jj, 0, h], vn)
            if rider:
                rider.groups(8 * j, 8 * (j + 1))
        if rider:
            rider.end()

    in_specs, out_specs, out_shape = [], [], []
    for d in range(2):
        in_specs += [tok2[d]] * 4 + [mat[d], cds[d]]
        out_specs += [tok[d], sts[d], tok[d]]
        out_shape += [_sds((s, HW)), _sds((n, NH, DH, DH), BF16), _sds((s, HW), BF16)]
    args = [u, w, qd, kd, at, cd, u, w, qd, kd, at, cd]
    scratch = [pltpu.VMEM((2, NH, DH, DH), F32)]
    if rider:
        in_specs, args = in_specs + rider.in_specs, args + rider.args
        out_specs, out_shape, scratch = out_specs + rider.out_specs, out_shape + rider.out_shape, scratch + rider.scratch
    return pl.pallas_call(
        body, name=name, grid=(nb,), in_specs=in_specs, out_specs=out_specs, out_shape=out_shape,
        scratch_shapes=scratch, compiler_params=_cparams(1),
    )(*args)


def _gdn_seq_bwd(do, w, qd, kd, at, cd, states, vns, name, scan=None):
    s = do.shape[0]
    n, cb, nb, tok2, tok, mat, cds, sts, dcd = _seq_specs(s, (False, True))
    rider = _ScanRider(*scan, True, cb * CHUNK, nb, tok[1], tok[0]) if scan else None
    ri = len(rider.args) if rider else 0

    def body(*refs):
        ins = (refs[0:8], refs[8:16])
        outs = (refs[16 + ri:21 + ri], refs[21 + ri:26 + ri])
        dst = refs[26 + ri + (2 if rider else 0)]
        if rider:
            rider.begin(refs[16:16 + ri], refs[26 + ri:28 + ri], refs[29 + ri:])

        @pl.when(pl.program_id(0) == 0)
        def _():
            dst[...] = jnp.zeros_like(dst)

        for j in range(cb):
            items = []
            for d in range(2):
                jj = cb - 1 - j if d == 0 else j
                items += [(d, h, jj, slice(CHUNK * jj, CHUNK * (jj + 1)), _head(h)) for h in range(NH)]
            dsns = [dst[d, h] for d, h, _, _, _ in items]
            dohs = [ins[d][0][rows, sl] for d, h, jj, rows, sl in items]
            d_vns = [_dot_tn(ins[d][4][jj, 0, h], doh) + _dot(ins[d][3][0, rows, sl], dsn)
                     for (d, h, jj, rows, sl), doh, dsn in zip(items, dohs, dsns)]
            news = [ins[d][5][jj, 4 * d + h:4 * d + h + 1, :] * dsn + _dot_tn(ins[d][2][0, rows, sl], doh)
                    - _dot_tn(ins[d][1][0, rows, sl], d_vn)
                    for (d, h, jj, rows, sl), doh, dsn, d_vn in zip(items, dohs, dsns, d_vns)]
            for (d, h, jj, rows, sl), doh, dsn, d_vn, new in zip(items, dohs, dsns, d_vns, news):
                dvn_r, dkd_r, dqd_r, dw_r, dcd_r = outs[d]
                sh = ins[d][6][jj, h].astype(F32)
                dst[d, h] = new
                dvn_r[rows, sl] = d_vn.astype(BF16)
                dkd_r[rows, sl] = _dot_nt(ins[d][7][rows, sl], dsn)
                dqd_r[rows, sl] = _dot_nt(doh, sh)
                dw_r[rows, sl] = (-_dot_nt(d_vn, sh)).astype(BF16)
                d_cd = jnp.sum(jnp.sum(sh * dsn, axis=1, keepdims=True), axis=0, keepdims=True)
                dcd_r[jj, h:h + 1, :] = jnp.broadcast_to(d_cd, (1, DH))
            if rider:
                rider.groups(8 * j, 8 * (j + 1))
        if rider:
            rider.end()

    in_specs, out_specs, out_shape, args = [], [], [], []
    for d in range(2):
        in_specs += [tok[d]] + [tok2[d]] * 3 + [mat[d], cds[d], sts[d], tok[d]]
        args += [do, w, qd, kd, at, cd, states[d], vns[d]]
        out_specs += [tok[d]] * 4 + [dcd[d]]
        out_shape += [_sds((s, HW), BF16), _sds((s, HW)), _sds((s, HW)), _sds((s, HW), BF16), _sds((n, NH, DH))]
    scratch = [pltpu.VMEM((2, NH, DH, DH), F32)]
    if rider:
        in_specs, args = in_specs + rider.in_specs, args + rider.args
        out_specs, out_shape, scratch = out_specs + rider.out_specs, out_shape + rider.out_shape, scratch + rider.scratch
    return pl.pallas_call(
        body, name=name, grid=(nb,), in_specs=in_specs, out_specs=out_specs, out_shape=out_shape,
        scratch_shapes=scratch, compiler_params=_cparams(1),
    )(*args)


def _gdn_local_bwd(q, k, v, bg, tmat, do, vns, seq_grads, name, comm=None):
    s = q.shape[0]
    n = s // CHUNK
    cb = min(LOCAL_CB, n)

    def body(*refs):
        q_ref, k_ref, v_ref, bg_ref, t_ref, do_ref = refs[0:6]
        vn_refs = refs[6:8]
        sg = (refs[8:13], refs[13:18])
        dq_ref, dk_ref, dv_ref, dbg_ref = refs[18:]
        lane = lax.broadcasted_iota(jnp.int32, (CHUNK, BAP), 1)
        rowi = lax.broadcasted_iota(jnp.int32, (CHUNK, 1), 0)
        ones = jnp.ones((CHUNK, DH), F32)
        masks = [_GdnMasks(d) for d in range(2)]
        inst = []
        for jj in range(cb):
            rows = slice(CHUNK * jj, CHUNK * (jj + 1))
            bgv = bg_ref[rows, :]
            qs = [q_ref[rows, _head(h)] for h in range(NH)]
            ks = [k_ref[rows, _head(h)] for h in range(NH)]
            kk = [_dot_nt(ks[h], ks[h]) for h in range(NH)]
            q0 = [_dot_nt(qs[h], ks[h]) for h in range(NH)]
            for d, m in enumerate(masks):
                gcs, gcs_t = _chunk_cumsums(m, bgv)
                for h in range(NH):
                    c = _GdnHead(qs[h], ks[h], v_ref[rows, _head(h)], kk[h], q0[h], bgv, gcs, gcs_t, d, h, m)
                    inst.append((jj, rows, d, h, m, c))
        ni = len(inst)
        cs = [it[-1] for it in inst]
        tms = [t_ref[jj, d, h] for jj, _, d, h, _, _ in inst]
        d_vns = [sg[d][0][rows, _head(h)] for _, rows, d, h, _, _ in inst]
        d_ws = [sg[d][3][rows, _head(h)] for _, rows, d, h, _, _ in inst]
        d_ts = [_dot_nt(d_vns[i], cs[i].vb) + _dot_nt(d_ws[i], cs[i].kg) for i in range(ni)]
        tts = [tm.T for tm in tms]
        xs = [_dot3(tts[i], d_ts[i]) for i in range(ni)]
        d_ls = [jnp.where(inst[i][4].strict, -_dot3(xs[i], tts[i]), 0.0) for i in range(ni)]
        d_attns = [jnp.where(m.incl, _dot_nt(do_ref[rows, _head(h)], vn_refs[d][rows, _head(h)]), 0.0)
                   for _, rows, d, h, m, _ in inst]
        d_vbs = [_dot(tts[i], d_vns[i]) for i in range(ni)]
        d_kgs = [_dot(tts[i], d_ws[i]) for i in range(ni)]
        d_a0s = [d_ls[i] * cs[i].decay for i in range(ni)]
        d_q0s = [d_attns[i] * cs[i].decay for i in range(ni)]
        es = [(d_ls[i] * cs[i].a0 + d_attns[i] * cs[i].q0) * cs[i].decay for i in range(ni)]
        kb_mm = [_dot(d_a0s[i], cs[i].k) for i in range(ni)]
        q_mm = [_dot(d_q0s[i], cs[i].k) for i in range(ni)]
        k_mm = [_dot_tn(d_a0s[i], cs[i].kb) + _dot_tn(d_q0s[i], cs[i].q) for i in range(ni)]
        e_cols = [_dot_exact(ones, es[i], _TN, False)[:, 0:1] for i in range(ni)]
        acc = {}
        d_gcs, d_betas = [], []
        for i, (jj, rows, d, h, m, c) in enumerate(inst):
            sl = _head(h)
            d_kd, d_qd = sg[d][1][rows, sl], sg[d][2][rows, sl]
            d_cd = sg[d][4][jj, h:h + 1, 0:1]
            d_vb, d_kg = d_vbs[i], d_kgs[i]
            d_kb = kb_mm[i] + d_kg * c.eg
            parts = (q_mm[i] + d_qd * c.eg, k_mm[i] + d_kd * c.ek + d_kb * c.beta, d_vb * c.beta)
            acc[jj, h] = [p + a for a, p in zip(acc[jj, h], parts)] if (jj, h) in acc else list(parts)
            kd_term = d_kd * c.kd
            d_gc = (jnp.sum(d_kg * c.kg + d_qd * c.qd - kd_term, axis=1, keepdims=True)
                    + jnp.sum(es[i], axis=1, keepdims=True) - e_cols[i])
            d_gl = jnp.sum(jnp.sum(kd_term, axis=0, keepdims=True), axis=1, keepdims=True) + d_cd * c.cd
            d_gcs.append(d_gc + jnp.where(rowi == m.last, d_gl, 0.0))
            d_betas.append(jnp.sum(d_kb * c.k + d_vb * c.v, axis=1, keepdims=True))
        d_gs = [_dot_exact(inst[i][4].tri, d_gcs[i] * ones, _TN, True)[:, 0:1] for i in range(ni)]
        dbg = [jnp.zeros((CHUNK, BAP), F32) for _ in range(cb)]
        for i, (jj, _, d, h, _, _) in enumerate(inst):
            dbg[jj] = dbg[jj] + jnp.where(lane == 4 * d + h, d_betas[i], 0.0) + jnp.where(lane == 8 + 4 * d + h, d_gs[i], 0.0)
        for jj in range(cb):
            rows = slice(CHUNK * jj, CHUNK * (jj + 1))
            for h in range(NH):
                dq_ref[rows, _head(h)], dk_ref[rows, _head(h)], dv_ref[rows, _head(h)] = acc[jj, h]
            dbg_ref[rows, :] = dbg[jj]

    tok = _rows(cb * CHUNK, HW)
    bgs = _rows(cb * CHUNK, BAP)
    mat = pl.BlockSpec((cb, 2, NH, CHUNK, CHUNK), lambda i: (i, 0, 0, 0, 0))
    dcd = pl.BlockSpec((cb, NH, DH), lambda i: (i, 0, 0))
    args = [q, k, v, bg, tmat, do, vns[0], vns[1]]
    in_specs = [tok, tok, tok, bgs, mat, tok, tok, tok]
    for d in range(2):
        args += list(seq_grads[d])
        in_specs += [tok] * 4 + [dcd]
    return _pallas(body, comm, name=name, grid=(n // cb,), in_specs=in_specs, out_specs=[tok, tok, tok, bgs],
                   out_shape=[_sds((s, HW))] * 3 + [_sds((s, BAP))], scratch_shapes=[], args=args)


def _prep_bwd(c_qkv, p_ba, alog_row, dtb_row, dq, dk, dv, dbg, name):
    s = c_qkv.shape[0]
    t = min(ROW_TILE, s)

    def body(cq_ref, pc_ref, alog_ref, dtb_ref, dq_ref, dk_ref, dv_ref, dbg_ref,
             dcq_ref, dpc_ref, dalog_ref, ddtb_ref):
        @pl.when(pl.program_id(0) == 0)
        def _():
            dalog_ref[...] = jnp.zeros_like(dalog_ref)
            ddtb_ref[...] = jnp.zeros_like(ddtb_ref)

        cq = cq_ref[...]
        sq = cq * _sig(cq)
        sg = _silu_grad(cq)
        for h in range(NH):
            sl = slice(DH * h, DH * (h + 1))
            for off, d_ref, scale in ((0, dq_ref, DH ** -0.5), (RGW, dk_ref, 1.0)):
                csl = slice(off + DH * h, off + DH * (h + 1))
                xh = sq[:, csl]
                nrm = lax.rsqrt(jnp.sum(xh * xh, axis=-1, keepdims=True) + EPS)
                y = xh * nrm
                dy = d_ref[:, sl] * scale
                dcq_ref[:, csl] = nrm * (dy - y * jnp.sum(dy * y, axis=-1, keepdims=True)) * sg[:, csl]
        dcq_ref[:, 2 * RGW:] = dv_ref[...] * sg[:, 2 * RGW:]
        pc = pc_ref[...]
        lane = lax.broadcasted_iota(jnp.int32, pc.shape, 1)
        dbg = dbg_ref[...]
        beta = _sig(pc)
        ea = jnp.exp(alog_ref[...])
        z = pc + dtb_ref[...]
        g = -ea * _softplus(z)
        is_g = jnp.logical_and(lane >= 8, lane < 16)
        d_alpha = jnp.where(is_g, dbg * (-ea) * _sig(z), 0.0)
        dpc_ref[...] = jnp.where(lane < 8, dbg * beta * (1.0 - beta), d_alpha).astype(BF16)
        dalog_ref[...] += _colsum(jnp.where(is_g, dbg * g, 0.0))
        ddtb_ref[...] += _colsum(d_alpha)

    return pl.pallas_call(
        body, name=name, grid=(s // t,),
        in_specs=[_rows(t, QKVW), _rows(t, BAP), _full((1, BAP)), _full((1, BAP))] + [_rows(t, HW)] * 3 + [_rows(t, BAP)],
        out_specs=[_rows(t, QKVW), _rows(t, BAP), _full((1, BAP)), _full((1, BAP))],
        out_shape=[_sds((s, QKVW)), _sds((s, BAP), BF16), _sds((1, BAP)), _sds((1, BAP))],
        compiler_params=_cparams(1),
    )(c_qkv, p_ba, alog_row, dtb_row, dq, dk, dv, dbg)


def _mix_out_values(hf, hb, gate, of, ob, z, gn):
    hr = hf + hb
    y_rg = hr * _gelu(gate)
    osum = of + ob
    parts = []
    for h in range(NH):
        sl = slice(DH * h, DH * (h + 1))
        oh = osum[:, sl]
        r, ohat = _rms(oh)
        zh = z[:, sl]
        parts.append((r, ohat, zh))
    y_gdn = jnp.concatenate([ohat * gn * (zh * _sig(zh)) for (r, ohat, zh) in parts], axis=1)
    return hr, y_rg, y_gdn, parts


def _outproj(x1, hf, hb, gate, of, ob, z, gn, wout, name):
    s = x1.shape[0]
    t = min(ROW_TILE, s)

    def body(x_ref, hf_ref, hb_ref, gate_ref, of_ref, ob_ref, z_ref, gn_ref, w_ref, xo_ref, y_ref):
        _, y_rg, y_gdn, _ = _mix_out_values(hf_ref[...], hb_ref[...], gate_ref[...], of_ref[...], ob_ref[...],
                                            z_ref[...], gn_ref[...])
        y = jnp.concatenate([y_rg, y_gdn], axis=1).astype(BF16)
        y_ref[...] = y
        xo_ref[...] = x_ref[...] + jnp.dot(y, w_ref[...], preferred_element_type=F32)

    return pl.pallas_call(
        body, name=name, grid=(s // t,),
        in_specs=[_rows(t, D)] + [_rows(t, RGW)] * 6 + [_full((1, DH)), _full((D, D))],
        out_specs=[_rows(t, D), _rows(t, D)], out_shape=[_sds((s, D)), _sds((s, D), BF16)],
        compiler_params=_cparams(1),
    )(x1, hf, hb, gate, of, ob, z, gn, wout)


def _outproj_bwd(dx2, hf, hb, gate, of, ob, z, gn, wout, name, comm=None):
    s = dx2.shape[0]
    t = min(ROW_TILE, s)

    def body(d_ref, hf_ref, hb_ref, gate_ref, of_ref, ob_ref, z_ref, gn_ref, w_ref,
             dhr_ref, dgate_ref, dos_ref, dz_ref, dgn_ref, db_ref):
        @pl.when(pl.program_id(0) == 0)
        def _():
            dgn_ref[...] = jnp.zeros_like(dgn_ref)

        gate = gate_ref[...]
        gn_v = gn_ref[...]
        hr, _, _, parts = _mix_out_values(hf_ref[...], hb_ref[...], gate, of_ref[...], ob_ref[...], z_ref[...], gn_v)
        dbf = d_ref[...].astype(BF16)
        db_ref[...] = dbf
        dy = _dot_nt(dbf, w_ref[...])
        dyr = dy[:, :RGW]
        dhr_ref[...] = dyr * _gelu(gate)
        dgate_ref[...] = (dyr * hr * _gelu_grad(gate)).astype(BF16)
        dgn = jnp.zeros((1, DH), F32)
        for h, (r, ohat, zh) in enumerate(parts):
            sl = slice(DH * h, DH * (h + 1))
            dyh = dy[:, RGW + DH * h:RGW + DH * (h + 1)]
            sz = zh * _sig(zh)
            dn = dyh * sz
            dz_ref[:, sl] = (dyh * ohat * gn_v * _silu_grad(zh)).astype(BF16)
            dgn = dgn + _colsum(dn * ohat)
            dos_ref[:, sl] = _rms_bwd(dn, ohat, r, gn_v).astype(BF16)
        dgn_ref[...] += dgn

    return _pallas(
        body, comm, name=name, grid=(s // t,),
        in_specs=[_rows(t, D)] + [_rows(t, RGW)] * 6 + [_full((1, DH)), _full((D, D))],
        out_specs=[_rows(t, RGW)] * 4 + [_full((1, DH)), _rows(t, D)],
        out_shape=[_sds((s, RGW))] + [_sds((s, RGW), BF16)] * 3 + [_sds((1, DH)), _sds((s, D), BF16)],
        scratch_shapes=[], args=(dx2, hf, hb, gate, of, ob, z, gn, wout))


def _loss_head(x3, target, gain, name):
    s = x3.shape[0]
    t = min(ROW_TILE, s)

    def body(x_ref, t_ref, g_ref, dx_ref, dxh_ref, loss_ref, dg_ref):
        @pl.when(pl.program_id(0) == 0)
        def _():
            loss_ref[...] = jnp.zeros_like(loss_ref)
            dg_ref[...] = jnp.zeros_like(dg_ref)

        r, xh = _rms(x_ref[...])
        gv = g_ref[...]
        err = xh * gv - t_ref[...]
        per_tok = jnp.mean(err * err, axis=-1, keepdims=True)
        loss_ref[...] += 0.5 * jnp.sum(per_tok, axis=0, keepdims=True)
        dy = err * (1.0 / D)
        dg_ref[...] += _colsum(dy * xh)
        dx = _rms_bwd(dy, xh, r, gv)
        dx_ref[...] = dx
        dxh_ref[...] = (0.5 * dx).astype(BF16)

    return pl.pallas_call(
        body, name=name, grid=(s // t,), in_specs=[_rows(t, D), _rows(t, D), _full((1, D))],
        out_specs=[_rows(t, D), _rows(t, D), _full((8, 128)), _full((1, D))],
        out_shape=[_sds((s, D)), _sds((s, D), BF16), _sds((8, 128)), _sds((1, D))], compiler_params=_cparams(1),
    )(x3, target, gain)


def _adamw_math(wv, gv, mv, vv):
    mn = ADAM_B1 * mv + (1.0 - ADAM_B1) * gv
    vn = ADAM_B2 * vv + (1.0 - ADAM_B2) * (gv * gv)
    m_hat = mn / (1.0 - ADAM_B1 ** ADAM_STEP)
    v_hat = vn / (1.0 - ADAM_B2 ** ADAM_STEP)
    return -ADAM_LR * (m_hat / (jnp.sqrt(v_hat) + ADAM_EPS) + ADAM_WD * wv), mn, vn


def _row_tile(r, c):
    tr = r
    while tr * c * 4 > (1 << 20) and tr % 16 == 0:
        tr //= 2
    return tr


def _adamw(w, g, m, v, name):
    r, c = w.shape
    tr = _row_tile(r, c)

    def body(w_ref, g_ref, m_ref, v_ref, d_ref, nm_ref, nv_ref):
        d_ref[...], nm_ref[...], nv_ref[...] = _adamw_math(w_ref[...], g_ref[...], m_ref[...], v_ref[...])

    return pl.pallas_call(
        body, name=name, grid=(r // tr,), in_specs=[_rows(tr, c)] * 4, out_specs=[_rows(tr, c)] * 3,
        out_shape=[_sds((r, c))] * 3, compiler_params=_cparams(1),
    )(w, g, m, v)


def _adamw_halves(w, own, recv, m, v, c_arr, name):
    r, c = w.shape
    h = r // 2
    tr = _row_tile(h, c)
    nh = h // tr

    def body(c_ref, w_ref, own_ref, recv_ref, m_ref, v_ref, g_ref, d_ref, nm_ref, nv_ref):
        first_half = pl.program_id(0) < nh
        use_own = first_half == (c_ref[0] == 0)
        gv = jnp.where(use_own, own_ref[...], recv_ref[...])
        g_ref[...] = gv
        d_ref[...], nm_ref[...], nv_ref[...] = _adamw_math(w_ref[...], gv, m_ref[...], v_ref[...])

    full = pl.BlockSpec((tr, c), lambda i, c_ref: (i, 0))
    half = pl.BlockSpec((tr, c), lambda i, c_ref: (i % nh, 0))
    return pl.pallas_call(
        body, name=name, out_shape=[_sds((r, c))] * 4,
        grid_spec=pltpu.PrefetchScalarGridSpec(
            num_scalar_prefetch=1, grid=(2 * nh,), in_specs=[full, half, half, full, full], out_specs=[full] * 4),
        compiler_params=_cparams(1),
    )(c_arr, w, own, recv, m, v)


def _mesh_pos():
    return lax.axis_index("x"), lax.axis_index("y"), lax.axis_index("c")


def _other_chips(x, y):
    return [(1 - x, y), (x, 1 - y), (1 - x, 1 - y)]


class _Comm:
    def __init__(self, inputs, out_shapes, scratch, start, finish, space=pltpu.HBM):
        self.inputs, self.out_shapes, self.scratch = list(inputs), list(out_shapes), list(scratch)
        self.start, self.finish, self.space = start, finish, space


def _comm_call(comm, name):
    ni, no = len(comm.inputs), len(comm.out_shapes)

    def body(*refs):
        comm.start(refs[:ni], refs[ni:ni + no], refs[ni + no:])
        comm.finish(refs[:ni], refs[ni:ni + no], refs[ni + no:])

    spec = pl.BlockSpec(memory_space=comm.space)
    return list(pl.pallas_call(body, name=name, out_shape=comm.out_shapes, in_specs=[spec] * ni, out_specs=[spec] * no,
                               scratch_shapes=comm.scratch)(*comm.inputs))


def _join_comm(a, b):
    ia, oa, sa = len(a.inputs), len(a.out_shapes), len(a.scratch)

    def both(method):
        def run(ins, outs, sems):
            getattr(a, method)(ins[:ia], outs[:oa], sems[:sa])
            getattr(b, method)(ins[ia:], outs[oa:], sems[sa:])
        return run

    return _Comm(a.inputs + b.inputs, a.out_shapes + b.out_shapes, a.scratch + b.scratch, both("start"), both("finish"))


def _pallas(body, comm, *, name, grid, in_specs, out_specs, out_shape, scratch_shapes, args):
    params = _cparams(len(grid))
    if comm is None:
        outs = pl.pallas_call(body, name=name, grid=grid, in_specs=in_specs, out_specs=out_specs, out_shape=out_shape,
                              scratch_shapes=scratch_shapes, compiler_params=params)(*args)
        return list(outs), []
    n_in, n_out, n_sc = len(in_specs), len(out_specs), len(scratch_shapes)
    ci, co = len(comm.inputs), len(comm.out_shapes)

    def carried(*refs):
        bounds = [0, n_in, n_in + ci, n_in + ci + n_out, n_in + ci + n_out + co, n_in + ci + n_out + co + n_sc, len(refs)]
        ins, cins, outs, couts, scr, csems = [refs[lo:hi] for lo, hi in zip(bounds[:-1], bounds[1:])]
        ids = [pl.program_id(k) for k in range(len(grid))]
        first = functools.reduce(jnp.logical_and, [i == 0 for i in ids])
        last = functools.reduce(jnp.logical_and, [i == g - 1 for i, g in zip(ids, grid)])

        @pl.when(first)
        def _():
            comm.start(cins, couts, csems)

        body(*ins, *outs, *scr)

        @pl.when(last)
        def _():
            comm.finish(cins, couts, csems)

    hbm = pl.BlockSpec(memory_space=pltpu.HBM)
    outs = pl.pallas_call(
        carried, name=name, grid=grid, in_specs=list(in_specs) + [hbm] * ci, out_specs=list(out_specs) + [hbm] * co,
        out_shape=list(out_shape) + comm.out_shapes, scratch_shapes=list(scratch_shapes) + comm.scratch,
        compiler_params=params)(*args, *comm.inputs)
    return list(outs[:n_out]), list(outs[n_out:])


def _gather_comm(arrays, space, block_rows):
    n_arr = len(arrays)

    def plan(x_refs, out_refs, sems):
        send_sems, recv_sems, local_sems = sems
        x, y, c = _mesh_pos()
        me, sibling = (x, y, c), (x, y, 1 - c)
        chips = _other_chips(x, y)

        def slot(a, px, py, pc):
            return out_refs[a].at[4 * px + 2 * py + pc]

        def copy(a, k, block, to, src=None):
            return pltpu.make_async_remote_copy(
                src_ref=slot(a, *block) if src is None else src, dst_ref=slot(a, *block),
                send_sem=send_sems.at[7 * a + k], recv_sem=recv_sems.at[7 * a + k], device_id=to, device_id_type=MESH)

        srcs = [x_refs[a] if block_rows[a] is None else
                x_refs[a].at[pl.ds(pl.multiple_of(c * block_rows[a], 16), block_rows[a]), :] for a in range(n_arr)]
        local = [pltpu.make_async_copy(srcs[a], slot(a, *me), local_sems.at[a]) for a in range(n_arr)]
        first = []
        for a in range(n_arr):
            first += [copy(a, 1 + j, me, (*chip, c), src=srcs[a]) for j, chip in enumerate(chips)]
            first.append(copy(a, 0, me, sibling, src=srcs[a]))
        return me, sibling, chips, c, copy, local, first

    def start(x_refs, out_refs, sems):
        _, _, _, _, _, local, first = plan(x_refs, out_refs, sems)
        for cp in local + first:
            cp.start()

    def finish(x_refs, out_refs, sems):
        me, sibling, chips, c, copy, local, first = plan(x_refs, out_refs, sems)
        passed = []
        for j, chip in enumerate(chips):
            for a in range(n_arr):
                copy(a, 1 + j, (*chip, c), me).wait_recv()
                fwd = copy(a, 4 + j, (*chip, c), sibling)
                fwd.start()
                passed.append(fwd)
        for a in range(n_arr):
            copy(a, 0, sibling, me).wait_recv()
            for j, chip in enumerate(chips):
                copy(a, 4 + j, (*chip, 1 - c), me).wait_recv()
        for cp in first + passed:
            cp.wait_send()
        for cp in local:
            cp.wait()

    out_shapes = [_sds((8, w.shape[0] if r is None else r) + w.shape[1:], w.dtype) for w, r in zip(arrays, block_rows)]
    scratch = [pltpu.SemaphoreType.DMA((7 * n_arr,)), pltpu.SemaphoreType.DMA((7 * n_arr,)), pltpu.SemaphoreType.DMA((n_arr,))]
    return _Comm(arrays, out_shapes, scratch, start, finish, space)


def _weights_gather_comm(shards):
    return _gather_comm(shards, pltpu.HBM, [w.shape[0] // 2 for w in shards])


def _all_shards(gathered):
    return [o.reshape(NSH, 2 * o.shape[1], o.shape[2]) for o in gathered]


def _gather_small(block, name):
    return _comm_call(_gather_comm([block], pltpu.VMEM, [None]), name)[0]


def _exchange_comm(gs):
    n = len(gs)
    halves = [g.shape[1] // 2 for g in gs]

    def plan(g_refs, land_refs, sems):
        send_sems, recv_sems = sems
        x, y, c = _mesh_pos()
        copies = []
        for a in range(n):
            h = halves[a]
            for s in range(NSH):
                copies.append(pltpu.make_async_remote_copy(
                    src_ref=g_refs[a].at[s, pl.ds(pl.multiple_of((1 - c) * h, 8), h), :], dst_ref=land_refs[a].at[s],
                    send_sem=send_sems.at[NSH * a + s], recv_sem=recv_sems.at[NSH * a + s],
                    device_id=(x, y, 1 - c), device_id_type=MESH))
        return copies

    def start(g_refs, land_refs, sems):
        for cp in plan(g_refs, land_refs, sems):
            cp.start()

    def finish(g_refs, land_refs, sems):
        for cp in plan(g_refs, land_refs, sems):
            cp.wait()

    scratch = [pltpu.SemaphoreType.DMA((NSH * n,)), pltpu.SemaphoreType.DMA((NSH * n,))]
    return _Comm(gs, [_sds((NSH, h, g.shape[2])) for h, g in zip(halves, gs)], scratch, start, finish)


def _chip_sum(g, land, c_arr, name):
    _, h, cols = land.shape

    def body(c_ref, g_ref, l_ref, o_ref):
        o_ref[...] = (g_ref[...] + l_ref[...]).astype(BF16)

    return pl.pallas_call(
        body, name=name, out_shape=_sds((NSH, h, cols), BF16),
        grid_spec=pltpu.PrefetchScalarGridSpec(
            num_scalar_prefetch=1, grid=(NSH,),
            in_specs=[pl.BlockSpec((1, h, cols), lambda s, c_ref: (s, c_ref[0], 0)),
                      pl.BlockSpec((1, h, cols), lambda s, c_ref: (s, 0, 0))],
            out_specs=pl.BlockSpec((1, h, cols), lambda s, c_ref: (s, 0, 0))),
        compiler_params=_cparams(1),
    )(c_arr, g, land)


def _scatter_comm(parts):
    n = len(parts)

    def plan(p_refs, land_refs, sems):
        send_sems, recv_sems, local_sems = sems
        x, y, c = _mesh_pos()
        my_chip = 2 * x + y
        local = [pltpu.make_async_copy(p_refs[a].at[my_chip], land_refs[a].at[my_chip], local_sems.at[a]) for a in range(n)]
        copies = []
        for a in range(n):
            for j, (px, py) in enumerate(_other_chips(x, y)):
                copies.append(pltpu.make_async_remote_copy(
                    src_ref=p_refs[a].at[2 * px + py], dst_ref=land_refs[a].at[my_chip],
                    send_sem=send_sems.at[3 * a + j], recv_sem=recv_sems.at[3 * a + j],
                    device_id=(px, py, c), device_id_type=MESH))
        return local, copies

    def start(p_refs, land_refs, sems):
        local, copies = plan(p_refs, land_refs, sems)
        for cp in local + copies:
            cp.start()

    def finish(p_refs, land_refs, sems):
        local, copies = plan(p_refs, land_refs, sems)
        for cp in copies:
            cp.wait()
        for cp in local:
            cp.wait()

    scratch = [pltpu.SemaphoreType.DMA((3 * n,)), pltpu.SemaphoreType.DMA((3 * n,)), pltpu.SemaphoreType.DMA((n,))]
    return _Comm(parts, [_sds(p.shape, BF16) for p in parts], scratch, start, finish)


def _sum_slots(land, name):
    k, r, c = land.shape
    tr = r // 2 if r % 32 == 0 else r

    def body(l_ref, o_ref):
        acc = l_ref[0].astype(F32)
        for i in range(1, k):
            acc = acc + l_ref[i].astype(F32)
        o_ref[...] = acc

    return pl.pallas_call(
        body, name=name, grid=(r // tr,), in_specs=[pl.BlockSpec((k, tr, c), lambda i: (0, i, 0))],
        out_specs=_rows(tr, c), out_shape=_sds((r, c)), compiler_params=_cparams(1),
    )(land)


def _sibling_swap(halves):
    n = len(halves)

    def body(*refs):
        h_refs, out_refs = refs[:n], refs[n:2 * n]
        send_sems, recv_sems = refs[2 * n:]
        x, y, c = _mesh_pos()
        copies = [pltpu.make_async_remote_copy(
            src_ref=h_refs[a], dst_ref=out_refs[a], send_sem=send_sems.at[a], recv_sem=recv_sems.at[a],
            device_id=(x, y, 1 - c), device_id_type=MESH) for a in range(n)]
        for cp in copies:
            cp.start()
        for cp in copies:
            cp.wait()

    return pl.pallas_call(
        body, name="grad_sibling_swap", out_shape=[_sds(h.shape) for h in halves],
        in_specs=[pl.BlockSpec(memory_space=pltpu.HBM)] * n, out_specs=[pl.BlockSpec(memory_space=pltpu.HBM)] * n,
        scratch_shapes=[pltpu.SemaphoreType.DMA((n,)), pltpu.SemaphoreType.DMA((n,))],
    )(*halves)


def _pad_rows(v, width):
    flat = v.reshape(-1)
    rows = -(-flat.shape[0] // width)
    rows = -(-rows // 8) * 8
    return jnp.pad(flat, (0, rows * width - flat.shape[0])).reshape(rows, width)


def _size(shape):
    n = 1
    for dim in shape:
        n *= dim
    return n


def _row_pack(arrs):
    pieces = []
    for a in arrs:
        rows = -(-a.size // D)
        pieces.append(jnp.pad(a.reshape(-1), (0, rows * D - a.size)).reshape(rows, D))
    total = sum(p.shape[0] for p in pieces)
    if total % 8:
        pieces.append(jnp.zeros((8 - total % 8, D), F32))
    return jnp.concatenate(pieces, axis=0)


def _row_unpack(packed, shapes):
    out, r0 = [], 0
    for shp in shapes:
        n = _size(shp)
        rows = -(-n // D)
        out.append(packed[r0:r0 + rows].reshape(-1)[:n].reshape(shp))
        r0 += rows
    return out


def _block_diag(w):
    eye = jnp.eye(8, dtype=w.dtype)
    return (w[:, :, None, :] * eye[:, None, :, None]).reshape(RGW, RGW)


def _diag_blocks(dense):
    r = dense.reshape(8, 64, 8, 64)
    return jnp.stack([r[n, :, n, :] for n in range(8)])


def _lane_row(v8):
    return jnp.zeros((1, BAP), F32).at[0, 8:16].set(v8.reshape(8))


def _chip_sums(gs, lands, names, c_arr):
    return [_chip_sum(g, l, c_arr, "chip_sum_" + n) for g, l, n in zip(gs, lands, names)]


def _reduce_parts(gs, names, c_arr, tag):
    return _chip_sums(gs, _comm_call(_exchange_comm(gs), "grad_sibling_exchange_" + tag), names, c_arr)


def _local_step(x, target, sw, ffn1_w, later_shards, c_arr):
    (g1, gmix, rg_cw8, rg_cb, wgates, gbias, lam_row, gdn_cw8, alog_row, dtb_row, gn, g2, gfin) = sw
    wg1, wu1, wd1 = ffn1_w

    (x1, a1, b1, fb1), gathered = _ffn_fwd(x, g1, wg1, wu1, wd1, "ffn1_fwd", comm=_weights_gather_comm(later_shards))
    win_sh, wout_sh, wg2, wu2, wd2 = _all_shards(gathered)
    w_in_full = jnp.transpose(win_sh, (1, 0, 2)).reshape(D, NSH * INSH)
    wout = wout_sh.reshape(D, D)
    w_in_groups = (w_in_full[:, 0:512], w_in_full[:, 512:1024], w_in_full[:, 1024:2560], w_in_full[:, 2560:3072],
                   jnp.pad(w_in_full[:, 3072:3088], ((0, 0), (0, BAP - BAW))))
    h2, p_rgx, p_gate, p_qkv, p_z, p_ba = _inproj(x1, gmix, w_in_groups, "in_proj")
    c_rg = _conv(p_rgx, rg_cw8, rg_cb, "rg_conv")
    c_qkv = _conv(p_qkv, gdn_cw8, jnp.zeros((1, QKVW), F32), "gdn_conv")
    a0, bb0, a1s, bb1, q, k, v, bg = _mix_prep(c_rg, c_qkv, p_ba, wgates, gbias, lam_row, alog_row, dtb_row, "mix_prep")
    tmat, gu, gw, gqd, gkd, gat, gcd = _gdn_local_fwd(q, k, v, bg, "gdn_local_fwd")
    of, s0, vn0, ob, s1, vn1, hf, hb = _gdn_seq_fwd(gu, gw, gqd, gkd, gat, gcd, "gdn_seq_fwd", scan=(a0, bb0, a1s, bb1))
    x2, ymix = _outproj(x1, hf, hb, p_gate, of, ob, p_z, gn, wout, "out_proj")
    (x3, a2, b2, fb2), _ = _ffn_fwd(x2, g2, wg2, wu2, wd2, "ffn2_fwd")
    dx3, dob2, loss_blk, d_gfin = _loss_head(x3, target, gfin, "loss_head")

    dx2, d_g2, hb2, dab2, dbb2, _ = _ffn_bwd(x2, dx3, dob2, g2, a2, b2, wg2, wu2, wd2, "ffn2_bwd")
    d_ffn2 = [_tn(dab2, hb2, "ffn2_dwg"), _tn(dbb2, hb2, "ffn2_dwu"), _tn(fb2, dob2, "ffn2_dwd")]

    (d_hr, d_gate, d_os, d_z, d_gn, dx2b), lands = _outproj_bwd(dx2, hf, hb, p_gate, of, ob, p_z, gn, wout, "out_proj_bwd",
                                                               comm=_exchange_comm(d_ffn2))
    parts_ffn2 = _chip_sums(d_ffn2, lands, _BIG_NAMES[5:8], c_arr)
    d_wout = _tn(ymix, dx2b, "dw_out")[0]

    sg = _gdn_seq_bwd(d_os, gw, gqd, gkd, gat, gcd, (s0, s1), (vn0, vn1), "gdn_seq_bwd", scan=(a1s, d_hr, a0, d_hr))
    lam1, lam0 = sg[10:12]
    d_xc, d_pre, xcb, d_gbias, d_lam = _gates_bwd(c_rg, wgates, gbias, lam_row, lam0, lam1, hf, hb, "rg_gates_bwd")
    d_wgates = _tn(xcb, d_pre, "dw_gates")[0]
    d_prgx, d_rgcw8, d_rgcb = _conv_bwd(p_rgx, d_xc, rg_cw8, "rg_conv_bwd")

    (dq, dk, dv, dbg), lands_ffn2 = _gdn_local_bwd(q, k, v, bg, tmat, d_os, (vn0, vn1), (sg[0:5], sg[5:10]), "gdn_local_bwd",
                                                  comm=_scatter_comm(parts_ffn2))
    d_cqkv, d_pba, d_alog, d_dtb = _prep_bwd(c_qkv, p_ba, alog_row, dtb_row, dq, dk, dv, dbg, "gdn_prep_bwd")
    d_pqkv, d_gdncw8, _ = _conv_bwd(p_qkv, d_cqkv, gdn_cw8, "gdn_conv_bwd")

    dps = (d_prgx, d_gate, d_pqkv, d_z, d_pba)
    dx1, dob1, d_gmix = _inproj_bwd(x1, dx2, gmix, dps, w_in_groups, "in_proj_bwd")
    d_win_groups = [_tn(h2, dp, "dw_in_%d" % i)[0] for i, dp in enumerate(dps)]
    d_win = jnp.concatenate(d_win_groups[:4] + [d_win_groups[4][:, :BAW]], axis=1)
    d_mix = [jnp.transpose(d_win.reshape(D, NSH, INSH), (1, 0, 2)), d_wout.reshape(NSH, OUTSH, D)]

    small = dict(
        mix_norm=d_gmix, rg_conv_w=d_rgcw8[:4], rg_conv_b=d_rgcb,
        rg_gate_a_w=jnp.stack([_diag_blocks(d_wgates[:, RGW * i:RGW * (i + 1)]) for i in (0, 1)]),
        rg_gate_x_w=jnp.stack([_diag_blocks(d_wgates[:, RGW * i:RGW * (i + 1)]) for i in (2, 3)]),
        rg_gate_a_b=d_gbias[0, :2 * RGW].reshape(2, RGW), rg_gate_x_b=d_gbias[0, 2 * RGW:].reshape(2, RGW),
        rg_lambda=d_lam.reshape(2, RGW), gdn_conv_w=d_gdncw8[:4],
        gdn_a_log=d_alog[0, 8:16].reshape(2, NH), gdn_dt_bias=d_dtb[0, 8:16].reshape(2, NH),
        gdn_norm=d_gn, ffn2_norm=d_g2, final_norm=d_gfin)
    small_pack = _row_pack([small[n] for n in _SMALL_NAMES[1:]])

    riders = _join_comm(_exchange_comm(d_mix), _gather_comm([small_pack], pltpu.HBM, [None]))
    gx, d_g1, hb1, dab1, dbb1, carried = _ffn_bwd(x, dx1, dob1, g1, a1, b1, wg1, wu1, wd1, "ffn1_bwd", comm=riders)
    parts_mix = _chip_sums(d_mix, carried[0:2], _BIG_NAMES[3:5], c_arr)
    d_wg1, lands_mix = _tn(dab1, hb1, "ffn1_dwg", comm=_scatter_comm(parts_mix))
    parts_wg1 = _reduce_parts([d_wg1], _BIG_NAMES[0:1], c_arr, "ffn1_gate")
    d_wu1, lands_wg1 = _tn(dbb1, hb1, "ffn1_dwu", comm=_scatter_comm(parts_wg1))
    parts_wu1 = _reduce_parts([d_wu1], _BIG_NAMES[1:2], c_arr, "ffn1_up")
    d_wd1, lands_wu1 = _tn(fb1, dob1, "ffn1_dwd", comm=_scatter_comm(parts_wu1))
    parts_wd1 = _reduce_parts([d_wd1], _BIG_NAMES[2:3], c_arr, "ffn1_down")
    lands_ffn1 = lands_wg1 + lands_wu1 + _comm_call(_scatter_comm(parts_wd1), "grad_chip_scatter_ffn1_down")

    halves = [_sum_slots(l, "sum_chips_" + n) for l, n in zip(lands_ffn1 + lands_mix + lands_ffn2, _BIG_NAMES)]
    small_shapes = [small[n].shape for n in _SMALL_NAMES[1:]]
    return loss_blk, gx, halves, d_g1, carried[2], small_shapes


_SMALL_NAMES = ("ffn1_norm", "mix_norm", "rg_conv_w", "rg_conv_b", "rg_gate_a_w", "rg_gate_a_b", "rg_gate_x_w",
                "rg_gate_x_b", "rg_lambda", "gdn_conv_w", "gdn_a_log", "gdn_dt_bias", "gdn_norm", "ffn2_norm", "final_norm")
_SMALL_SHARDED = dict(rg_conv_w=128, rg_gate_a_b=128, rg_gate_x_b=128, rg_lambda=128, gdn_conv_w=384)
_OUT_ORDER = ("ffn1_norm", "ffn1_w_gate", "ffn1_w_up", "ffn1_w_down", "mix_norm", "w_in", "w_out", "rg_conv_w", "rg_conv_b",
              "rg_gate_a_w", "rg_gate_a_b", "rg_gate_x_w", "rg_gate_x_b", "rg_lambda", "gdn_conv_w", "gdn_a_log",
              "gdn_dt_bias", "gdn_norm", "ffn2_norm", "ffn2_w_gate", "ffn2_w_up", "ffn2_w_down", "final_norm")
_BIG_NAMES = ("ffn1_w_gate", "ffn1_w_up", "ffn1_w_down", "w_in", "w_out", "ffn2_w_gate", "ffn2_w_up", "ffn2_w_down")
_TRANSPOSED = ("ffn1_w_gate", "ffn1_w_up", "ffn2_w_gate", "ffn2_w_up")


def kernel(x, ffn1_norm, ffn1_w_gate, ffn1_w_up, ffn1_w_down, mix_norm, w_in, w_out, rg_conv_w, rg_conv_b, rg_gate_a_w, rg_gate_a_b, rg_gate_x_w, rg_gate_x_b, rg_lambda, gdn_conv_w, gdn_a_log, gdn_dt_bias, gdn_norm, ffn2_norm, ffn2_w_gate, ffn2_w_up, ffn2_w_down, final_norm, loss_target, m_ffn1_norm, m_ffn1_w_gate, m_ffn1_w_up, m_ffn1_w_down, m_mix_norm, m_w_in, m_w_out, m_rg_conv_w, m_rg_conv_b, m_rg_gate_a_w, m_rg_gate_a_b, m_rg_gate_x_w, m_rg_gate_x_b, m_rg_lambda, m_gdn_conv_w, m_gdn_a_log, m_gdn_dt_bias, m_gdn_norm, m_ffn2_norm, m_ffn2_w_gate, m_ffn2_w_up, m_ffn2_w_down, m_final_norm, v_ffn1_norm, v_ffn1_w_gate, v_ffn1_w_up, v_ffn1_w_down, v_mix_norm, v_w_in, v_w_out, v_rg_conv_w, v_rg_conv_b, v_rg_gate_a_w, v_rg_gate_a_b, v_rg_gate_x_w, v_rg_gate_x_b, v_rg_lambda, v_gdn_conv_w, v_gdn_a_log, v_gdn_dt_bias, v_gdn_norm, v_ffn2_norm, v_ffn2_w_gate, v_ffn2_w_up, v_ffn2_w_down, v_final_norm):
    args = dict(locals())
    w = {n: args[n] for n in _OUT_ORDER}
    mom = {n: args["m_" + n] for n in _OUT_ORDER}
    var = {n: args["v_" + n] for n in _OUT_ORDER}
    xi, yi, ci = _mesh_pos()
    shard = 2 * xi + yi

    big_bf16 = [w[n][0].astype(BF16) for n in _BIG_NAMES]
    sm_local = _pad_rows(jnp.concatenate([w[n][0].reshape(-1) for n in _SMALL_SHARDED]), 128)
    first = _comm_call(_gather_comm(big_bf16[0:3] + [sm_local], pltpu.HBM, [t.shape[0] // 2 for t in big_bf16[0:3]] + [None]),
                       "gather_first_weights")
    ffn1_w = _all_shards(first[0:3])
    sm_all = first[3][0::2].reshape(NSH, -1)
    sm_full, off = {}, 0
    for n, wd_ in _SMALL_SHARDED.items():
        rows = w[n].shape[1]
        piece = sm_all[:, off:off + rows * wd_].reshape(NSH, rows, wd_)
        sm_full[n] = jnp.transpose(piece, (1, 0, 2)).reshape(rows, NSH * wd_)
        off += rows * wd_

    wa, wx = rg_gate_a_w[0], rg_gate_x_w[0]
    wgates = jnp.concatenate([_block_diag(wa[0]), _block_diag(wa[1]), _block_diag(wx[0]), _block_diag(wx[1])],
                             axis=1).astype(BF16)
    gbias = jnp.concatenate([sm_full["rg_gate_a_b"].reshape(1, -1), sm_full["rg_gate_x_b"].reshape(1, -1)], axis=1)
    sw = (ffn1_norm, mix_norm, jnp.pad(sm_full["rg_conv_w"], ((0, 4), (0, 0))), rg_conv_b, wgates, gbias,
          sm_full["rg_lambda"].reshape(1, -1), jnp.pad(sm_full["gdn_conv_w"], ((0, 4), (0, 0))), _lane_row(gdn_a_log),
          _lane_row(gdn_dt_bias), gdn_norm, ffn2_norm, final_norm.reshape(1, D))
    c_arr = ci.reshape(1).astype(jnp.int32)

    loss_blk, gx, halves, d_g1, small_packs, small_shapes = _local_step(x[0], loss_target[0], sw, ffn1_w, big_bf16[3:], c_arr)
    loss = lax.psum(loss_blk[0, 0], ("x", "y", "c"))
    grads = {}

    g1_all = _gather_small(jnp.pad(d_g1, ((0, 7), (0, 0))), "gather_ffn1_norm_grad")
    sm_sums = [_sum_slots(g1_all, "ffn1_norm_grad_sum")[0:1]] + _row_unpack(_sum_slots(small_packs, "small_grad_sum"), small_shapes)
    for n, g in zip(_SMALL_NAMES, sm_sums):
        if n in _SMALL_SHARDED:
            wd_ = _SMALL_SHARDED[n]
            g = lax.dynamic_slice_in_dim(g, shard * wd_, wd_, axis=1)
        grads[n] = g.reshape(w[n].shape)

    delta, new_m, new_v = {}, {}, {}
    for n, own, recv in zip(_BIG_NAMES, halves, _sibling_swap(halves)):
        to2d = jnp.transpose if n in _TRANSPOSED else (lambda t: t)
        outs4 = _adamw_halves(to2d(w[n][0]), own, recv, to2d(mom[n][0]), to2d(var[n][0]), c_arr, "adamw_" + n)
        grads[n], delta[n], new_m[n], new_v[n] = [to2d(o)[None] for o in outs4]
    packs = [_row_pack([t[n] for n in _SMALL_NAMES]) for t in (w, grads, mom, var)]
    sm_shapes = [w[n].shape for n in _SMALL_NAMES]
    for dst, src in zip((delta, new_m, new_v), _adamw(*packs, "adamw_small")):
        for n, val in zip(_SMALL_NAMES, _row_unpack(src, sm_shapes)):
            dst[n] = val

    outs = [loss, gx[None]]
    for group in (grads, delta, new_m, new_v):
        outs += [group[n] for n in _OUT_ORDER]
    return tuple(outs)
```

```python
import functools

import jax
import jax.numpy as jnp
from jax import lax
from jax.experimental import pallas as pl
from jax.experimental.pallas import tpu as pltpu

F32 = jnp.float32
BF16 = jnp.bfloat16
EPS = 1e-6
D = 1024
NSH = 4
FSH = 704
RGW = 512
QKVW = 1536
ZW = 512
BAW = 16
BAP = 128
INSH = 772
OUTSH = 256
CHUNK = 64
NH = 4
DH = 128
RG_C = 8.0
VMEM_LIMIT = 52 * 1024 * 1024
TN_VMEM_BUDGET = 40 * 1024 * 1024
ROW_TILE = 512
MESH = pl.DeviceIdType.MESH

ADAM_LR = 0.001
ADAM_B1 = 0.9
ADAM_B2 = 0.999
ADAM_EPS = 1e-08
ADAM_WD = 0.01
ADAM_STEP = 10


def _cparams(n_grid):
    return pltpu.CompilerParams(dimension_semantics=("arbitrary",) * n_grid, vmem_limit_bytes=VMEM_LIMIT)


def _sig(x):
    return 0.5 + 0.5 * jnp.tanh(0.5 * x)


def _sig_pos(x):
    return 1.0 / (1.0 + jnp.exp(-x))


def _softplus(x):
    return jnp.maximum(x, 0.0) + jnp.log(1.0 + jnp.exp(-jnp.abs(x)))


def _one_minus_sq_exp(la, a):
    y = 2.0 * la
    series = -y * (1.0 + y * (0.5 + y * (1.0 / 6 + y * (1.0 / 24 + y * (1.0 / 120 + y * (1.0 / 720))))))
    return jnp.where(y > -0.1, series, 1.0 - a * a)


_GELU_C = 0.7978845608028654


def _gelu(x):
    t = jnp.tanh(_GELU_C * (x + 0.044715 * x * x * x))
    return 0.5 * x * (1.0 + t)


def _gelu_grad(x):
    t = jnp.tanh(_GELU_C * (x + 0.044715 * x * x * x))
    return 0.5 * (1.0 + t) + 0.5 * x * (1.0 - t * t) * _GELU_C * (1.0 + 3 * 0.044715 * x * x)


def _silu_grad(x):
    s = _sig(x)
    return s * (1.0 + x * (1.0 - s))


def _dot(a, b):
    return jnp.dot(a.astype(BF16), b.astype(BF16), preferred_element_type=F32)


def _dot_nt(a, b):
    return lax.dot_general(a.astype(BF16), b.astype(BF16), (((1,), (1,)), ((), ())), preferred_element_type=F32)


def _dot_tn(a, b):
    return lax.dot_general(a.astype(BF16), b.astype(BF16), (((0,), (0,)), ((), ())), preferred_element_type=F32)


_NN = ((1,), (0,))
_NT = ((1,), (1,))
_TN = ((0,), (0,))


def _dg(a, b, dims):
    return lax.dot_general(a, b, (dims, ((), ())), preferred_element_type=F32)


def _split2(a):
    hi = a.astype(BF16)
    return hi, (a - hi.astype(F32)).astype(BF16)


def _dot3(a, b, dims=_NN):
    ah, al = _split2(a)
    bh, bl = _split2(b)
    return _dg(ah, bh, dims) + _dg(ah, bl, dims) + _dg(al, bh, dims)


def _dot_exact(e, x, dims, e_is_lhs):
    x0 = x.astype(BF16)
    r = x - x0.astype(F32)
    x1 = r.astype(BF16)
    x2 = (r - x1.astype(F32)).astype(BF16)
    eb = e.astype(BF16)
    if e_is_lhs:
        return _dg(eb, x0, dims) + _dg(eb, x1, dims) + _dg(eb, x2, dims)
    return _dg(x0, eb, dims) + _dg(x1, eb, dims) + _dg(x2, eb, dims)


def _rms(xv):
    r = lax.rsqrt(jnp.mean(xv * xv, axis=-1, keepdims=True) + EPS)
    return r, xv * r


def _rms_bwd(dy, xh, r, gain):
    dxh = dy * gain
    return r * (dxh - xh * jnp.mean(dxh * xh, axis=-1, keepdims=True))


def _colsum(v):
    return jnp.sum(v, axis=0, keepdims=True)


def _rows(t, c):
    return pl.BlockSpec((t, c), lambda i: (i, 0))


def _full(shape):
    n = len(shape)
    return pl.BlockSpec(shape, lambda i: (0,) * n)


def _sds(shape, dtype=F32):
    return jax.ShapeDtypeStruct(shape, dtype)


def _ffn_fwd(x, gain, wg, wu, wd, name, comm=None):
    s = x.shape[0]
    tm = min(256, s)

    def body(x_ref, g_ref, wg_ref, wu_ref, wd_ref, xo_ref, ga_ref, gb_ref, f_ref):
        xv = x_ref[...]
        _, xh = _rms(xv)
        h = (xh * g_ref[...]).astype(BF16)
        acc = None
        for j in range(NSH):
            a = jnp.dot(h, wg_ref[j], preferred_element_type=F32)
            b = jnp.dot(h, wu_ref[j], preferred_element_type=F32)
            sa = _sig(a)
            silu = a * sa
            fv = silu * b
            f = fv.astype(BF16)
            f_ref[j] = f
            ga_ref[j] = (sa * b + fv * (1.0 - sa)).astype(BF16)
            gb_ref[j] = silu.astype(BF16)
            part = jnp.dot(f, wd_ref[j], preferred_element_type=F32)
            acc = part if acc is None else acc + part
        xo_ref[...] = xv + 0.5 * acc

    hidden = pl.BlockSpec((NSH, tm, FSH), lambda i: (0, i, 0))
    return _pallas(
        body, comm, name=name, grid=(s // tm,),
        in_specs=[_rows(tm, D), _full((1, D)),
                  pl.BlockSpec((NSH, D, FSH), lambda i: (0, 0, 0), pipeline_mode=pl.Buffered(1)),
                  pl.BlockSpec((NSH, D, FSH), lambda i: (0, 0, 0), pipeline_mode=pl.Buffered(1)),
                  pl.BlockSpec((NSH, FSH, D), lambda i: (0, 0, 0), pipeline_mode=pl.Buffered(1))],
        out_specs=[_rows(tm, D), hidden, hidden, hidden],
        out_shape=[_sds((s, D))] + [_sds((NSH, s, FSH), BF16)] * 3,
        scratch_shapes=[], args=(x, gain, wg, wu, wd))


def _ffn_bwd(x, dout, do, gain, ga, gb, wg, wu, wd, name, comm=None):
    s = x.shape[0]
    tm = min(512, s)

    def hidden(do_ref, ga_ref, gb_ref, wd_ref, da_ref, db_ref):
        dov = do_ref[...]
        for j in range(NSH):
            df = _dot_nt(dov, wd_ref[j])
            da_ref[j] = (df * ga_ref[j].astype(F32)).astype(BF16)
            db_ref[j] = (df * gb_ref[j].astype(F32)).astype(BF16)

    sh = pl.BlockSpec((NSH, tm, FSH), lambda i: (0, i, 0))
    (da, db), carried = _pallas(
        hidden, comm, name=name + "_hidden", grid=(s // tm,),
        in_specs=[_rows(tm, D), sh, sh, pl.BlockSpec((NSH, FSH, D), lambda i: (0, 0, 0), pipeline_mode=pl.Buffered(1))],
        out_specs=[sh, sh], out_shape=[_sds((NSH, s, FSH), BF16)] * 2, scratch_shapes=[], args=(do, ga, gb, wd))

    def inputs(x_ref, d_ref, g_ref, da_ref, db_ref, wg_ref, wu_ref, dx_ref, dg_ref, h_ref):
        @pl.when(pl.program_id(0) == 0)
        def _():
            dg_ref[...] = jnp.zeros_like(dg_ref)

        dh = jnp.zeros((tm, D), F32)
        for j in range(NSH):
            dh = dh + _dot_nt(da_ref[j], wg_ref[j]) + _dot_nt(db_ref[j], wu_ref[j])
        r, xh = _rms(x_ref[...])
        gv = g_ref[...]
        h_ref[...] = (xh * gv).astype(BF16)
        dg_ref[...] += _colsum(dh * xh)
        dx_ref[...] = d_ref[...] + _rms_bwd(dh, xh, r, gv)

    grads = pl.BlockSpec((NSH, tm, FSH), lambda i: (0, i, 0))
    resident = pl.BlockSpec((NSH, D, FSH), lambda i: (0, 0, 0), pipeline_mode=pl.Buffered(1))
    dx, dg, h = pl.pallas_call(
        inputs, name=name + "_input", grid=(s // tm,),
        in_specs=[_rows(tm, D), _rows(tm, D), _full((1, D)), grads, grads, resident, resident],
        out_specs=[_rows(tm, D), _full((1, D)), _rows(tm, D)],
        out_shape=[_sds((s, D)), _sds((1, D)), _sds((s, D), BF16)], compiler_params=_cparams(1),
    )(x, dout, gain, da, db, wg, wu)
    return dx, dg, h, da, db, carried


def _tn(a, b, name, comm=None):
    a_g = a.ndim == 3
    b_g = b.ndim == 3
    g = a.shape[0] if a_g else (b.shape[0] if b_g else 1)
    s, k = a.shape[-2:]
    n = b.shape[-1]
    ts = min(4096, s)
    while ts > 256 and 2 * ts * (k + n) * max(a.dtype.itemsize, b.dtype.itemsize) + 2 * k * n * 4 > TN_VMEM_BUDGET:
        ts //= 2

    def body(a_ref, b_ref, o_ref):
        @pl.when(pl.program_id(1) == 0)
        def _():
            o_ref[...] = jnp.zeros_like(o_ref)

        av = a_ref[0] if a_g else a_ref[...]
        bv = b_ref[0] if b_g else b_ref[...]
        o_ref[0] += _dot_tn(av, bv)

    a_spec = pl.BlockSpec((1, ts, k), lambda gi, si: (gi, si, 0)) if a_g else pl.BlockSpec((ts, k), lambda gi, si: (si, 0))
    b_spec = pl.BlockSpec((1, ts, n), lambda gi, si: (gi, si, 0)) if b_g else pl.BlockSpec((ts, n), lambda gi, si: (si, 0))
    outs, carried = _pallas(body, comm, name=name, grid=(g, s // ts), in_specs=[a_spec, b_spec],
                            out_specs=[pl.BlockSpec((1, k, n), lambda gi, si: (gi, 0, 0))], out_shape=[_sds((g, k, n))],
                            scratch_shapes=[], args=(a, b))
    return outs[0] if comm is None else (outs[0], carried)


_P_WIDTHS = (RGW, RGW, QKVW, ZW, BAP)


def _inproj(x1, gain, ws, name):
    s = x1.shape[0]
    tm = min(ROW_TILE, s)

    def body(x_ref, g_ref, *refs):
        w_refs = refs[:5]
        h_ref = refs[5]
        p_refs = refs[6:]
        _, xh = _rms(x_ref[...])
        h = (xh * g_ref[...]).astype(BF16)
        h_ref[...] = h
        for w_ref, p_ref in zip(w_refs, p_refs):
            p_ref[...] = jnp.dot(h, w_ref[...], preferred_element_type=F32)

    return pl.pallas_call(
        body, name=name, grid=(s // tm,),
        in_specs=[_rows(tm, D), _full((1, D))] + [_full((D, w)) for w in _P_WIDTHS],
        out_specs=[_rows(tm, D)] + [_rows(tm, w) for w in _P_WIDTHS],
        out_shape=[_sds((s, D), BF16)] + [_sds((s, w)) for w in _P_WIDTHS],
        compiler_params=_cparams(1),
    )(x1, gain, *ws)


def _inproj_bwd(x1, dx2, gain, dps, ws, name):
    s = x1.shape[0]
    tm = min(ROW_TILE, s)

    def body(x_ref, d_ref, g_ref, *refs):
        dp_refs = refs[:5]
        w_refs = refs[5:10]
        dx_ref, dxh_ref, dg_ref = refs[10:]

        @pl.when(pl.program_id(0) == 0)
        def _():
            dg_ref[...] = jnp.zeros_like(dg_ref)

        dh = jnp.zeros((tm, D), F32)
        for dp_ref, w_ref in zip(dp_refs, w_refs):
            dh = dh + _dot_nt(dp_ref[...], w_ref[...])
        r, xh = _rms(x_ref[...])
        dg_ref[...] += _colsum(dh * xh)
        dx = d_ref[...] + _rms_bwd(dh, xh, r, g_ref[...])
        dx_ref[...] = dx
        dxh_ref[...] = (0.5 * dx).astype(BF16)

    return pl.pallas_call(
        body, name=name, grid=(s // tm,),
        in_specs=[_rows(tm, D), _rows(tm, D), _full((1, D))] + [_rows(tm, w) for w in _P_WIDTHS]
        + [_full((D, w)) for w in _P_WIDTHS],
        out_specs=[_rows(tm, D), _rows(tm, D), _full((1, D))],
        out_shape=[_sds((s, D)), _sds((s, D), BF16), _sds((1, D))],
        compiler_params=_cparams(1),
    )(x1, dx2, gain, *dps, *ws)


def _halo_specs(s, t, c):
    nb8 = s // 8
    tb = t // 8
    prev = pl.BlockSpec((8, c), lambda i: (jnp.maximum(i * tb - 1, 0), 0))
    nxt = pl.BlockSpec((8, c), lambda i: (jnp.minimum((i + 1) * tb, nb8 - 1), 0))
    return prev, nxt


def _edge_masks(nb):
    i = pl.program_id(0)
    return jnp.where(i > 0, 1.0, 0.0).astype(F32), jnp.where(i < nb - 1, 1.0, 0.0).astype(F32)


def _shifted(xx, off, t):
    n = t + 16
    sh = (-off) % n
    rolled = xx if sh == 0 else pltpu.roll(xx, sh, 0)
    return rolled[8:8 + t]


def _conv(x, w8, bias, name):
    s, c = x.shape
    t = min(ROW_TILE, s)
    nb = s // t

    def body(x_ref, xp_ref, xn_ref, w_ref, b_ref, o_ref):
        pm, nm = _edge_masks(nb)
        for c0 in range(0, c, 512):
            cols = slice(c0, c0 + 512)
            xx = jnp.concatenate([xp_ref[:, cols] * pm, x_ref[:, cols], xn_ref[:, cols] * nm], axis=0)
            acc = jnp.zeros((t, 512), F32) + b_ref[:, cols]
            for j in range(4):
                acc = acc + w_ref[j:j + 1, cols] * _shifted(xx, j - 2, t)
            o_ref[:, cols] = acc

    prev, nxt = _halo_specs(s, t, c)
    return pl.pallas_call(
        body, name=name, grid=(nb,),
        in_specs=[_rows(t, c), prev, nxt, _full((8, c)), _full((1, c))],
        out_specs=_rows(t, c), out_shape=_sds((s, c)), compiler_params=_cparams(1),
    )(x, x, x, w8, bias)


def _conv_bwd(x, dc, w8, name):
    s, c = x.shape
    t = min(ROW_TILE, s)
    nb = s // t

    def body(x_ref, d_ref, dp_ref, dn_ref, w_ref, dx_ref, dw_ref, db_ref):
        @pl.when(pl.program_id(0) == 0)
        def _():
            dw_ref[...] = jnp.zeros_like(dw_ref)
            db_ref[...] = jnp.zeros_like(db_ref)

        pm, nm = _edge_masks(nb)
        for c0 in range(0, c, 512):
            cols = slice(c0, c0 + 512)
            dd = jnp.concatenate([dp_ref[:, cols] * pm, d_ref[:, cols], dn_ref[:, cols] * nm], axis=0)
            xv = x_ref[:, cols]
            acc = jnp.zeros((t, 512), F32)
            for j in range(4):
                dsh = _shifted(dd, 2 - j, t)
                acc = acc + w_ref[j:j + 1, cols] * dsh
                dw_ref[j:j + 1, cols] += _colsum(dsh * xv)
            dx_ref[:, cols] = acc.astype(BF16)
            db_ref[:, cols] += _colsum(d_ref[:, cols])

    prev, nxt = _halo_specs(s, t, c)
    return pl.pallas_call(
        body, name=name, grid=(nb,),
        in_specs=[_rows(t, c), _rows(t, c), prev, nxt, _full((8, c))],
        out_specs=[_rows(t, c), _full((8, c)), _full((1, c))],
        out_shape=[_sds((s, c), BF16), _sds((8, c)), _sds((1, c))], compiler_params=_cparams(1),
    )(x, dc, dc, dc, w8)


def _rg_gates(xc, pre, lam_row):
    sp8 = RG_C * _softplus(-lam_row)
    out = []
    for d in range(2):
        r = _sig_pos(pre[:, RGW * d:RGW * (d + 1)])
        gi = _sig(pre[:, 2 * RGW + RGW * d:2 * RGW + RGW * (d + 1)])
        la = -r * sp8[:, RGW * d:RGW * (d + 1)]
        a = jnp.exp(la)
        mult = jnp.sqrt(_one_minus_sq_exp(la, a))
        out.append((r, gi, a, mult))
    return out


def _mix_prep(c_rg, c_qkv, p_ba, wgates, gbias, lam_row, alog_row, dtb_row, name):
    s = c_rg.shape[0]
    t = min(ROW_TILE, s)

    def body(xc_ref, cq_ref, pc_ref, wg_ref, gb_ref, lam_ref, alog_ref, dtb_ref,
             a0_ref, b0_ref, a1_ref, b1_ref, q_ref, k_ref, v_ref, bg_ref):
        xc = xc_ref[...]
        pre = _dot(xc, wg_ref[...]) + gb_ref[...]
        gates = _rg_gates(xc, pre, lam_ref[...])
        for (r, gi, a, mult), a_ref, b_ref in zip(gates, (a0_ref, a1_ref), (b0_ref, b1_ref)):
            a_ref[...] = a
            b_ref[...] = mult * gi * xc
        cq = cq_ref[...]
        sq = cq * _sig(cq)
        for h in range(NH):
            sl = slice(DH * h, DH * (h + 1))
            qh = sq[:, sl]
            q_ref[:, sl] = qh * lax.rsqrt(jnp.sum(qh * qh, axis=-1, keepdims=True) + EPS) * (DH ** -0.5)
            kh = sq[:, RGW + DH * h:RGW + DH * (h + 1)]
            k_ref[:, sl] = kh * lax.rsqrt(jnp.sum(kh * kh, axis=-1, keepdims=True) + EPS)
        v_ref[...] = sq[:, 2 * RGW:]
        pc = pc_ref[...]
        lane = lax.broadcasted_iota(jnp.int32, pc.shape, 1)
        beta = _sig(pc)
        g = -jnp.exp(alog_ref[...]) * _softplus(pc + dtb_ref[...])
        bg_ref[...] = jnp.where(lane < 8, beta, jnp.where(lane < 16, g, 0.0))

    return pl.pallas_call(
        body, name=name, grid=(s // t,),
        in_specs=[_rows(t, RGW), _rows(t, QKVW), _rows(t, BAP), _full((RGW, 4 * RGW)), _full((1, 4 * RGW)),
                  _full((1, 2 * RGW)), _full((1, BAP)), _full((1, BAP))],
        out_specs=[_rows(t, RGW)] * 7 + [_rows(t, BAP)],
        out_shape=[_sds((s, RGW))] * 7 + [_sds((s, BAP))],
        compiler_params=_cparams(1),
    )(c_rg, c_qkv, p_ba, wgates, gbias, lam_row, alog_row, dtb_row)


def _block_scan(av, bv, row, downwards):
    for k in (1, 2, 4):
        sh = (8 - k) if downwards else k
        m = (row < 8 - k) if downwards else (row >= k)
        a_s = pltpu.roll(av, sh, 0)
        b_s = pltpu.roll(bv, sh, 0)
        bv = jnp.where(m, av * b_s + bv, bv)
        av = jnp.where(m, av * a_s, av)
    return av, bv


def _gates_bwd(xc, wgates, gbias, lam_row, lam0, lam1, hf, hb, name):
    s = xc.shape[0]
    t = min(ROW_TILE, s)
    nb = s // t

    def body(xc_ref, wg_ref, gb_ref, lam_ref, l0_ref, l1_ref, hf_ref, hfp_ref, hfn_ref, hb_ref, hbp_ref, hbn_ref,
             dxc_ref, dpre_ref, xcb_ref, dgb_ref, dlam_ref):
        @pl.when(pl.program_id(0) == 0)
        def _():
            dgb_ref[...] = jnp.zeros_like(dgb_ref)
            dlam_ref[...] = jnp.zeros_like(dlam_ref)

        pm, nm = _edge_masks(nb)
        h_prev = _shifted(jnp.concatenate([hfp_ref[...] * pm, hf_ref[...], hfn_ref[...] * nm], axis=0), -1, t)
        h_next = _shifted(jnp.concatenate([hbp_ref[...] * pm, hb_ref[...], hbn_ref[...] * nm], axis=0), 1, t)
        h_shift = (h_prev, h_next)
        xv = xc_ref[...]
        pre = _dot(xv, wg_ref[...]) + gb_ref[...]
        lam_row_v = lam_ref[...]
        sp8 = RG_C * _softplus(-lam_row_v)
        dsp_dlam = -RG_C * _sig(-lam_row_v)
        gates = _rg_gates(xv, pre, lam_row_v)
        dxc = jnp.zeros((t, RGW), F32)
        dpre_r = []
        dpre_i = []
        for d, ((r, gi, a, mult), l_ref, hs) in enumerate(zip(gates, (l0_ref, l1_ref), h_shift)):
            dbb = l_ref[...]
            da = dbb * hs
            cs = slice(RGW * d, RGW * (d + 1))
            dmult = dbb * gi * xv
            dgi = dbb * mult * xv
            dxc = dxc + dbb * mult * gi
            dla = da * a - dmult * a * a / mult
            dr = -dla * sp8[:, cs]
            dlam_ref[:, cs] += _colsum(-dla * r) * dsp_dlam[:, cs]
            dpre_r.append(dr * r * (1.0 - r))
            dpre_i.append(dgi * gi * (1.0 - gi))
        dpre = jnp.concatenate(dpre_r + dpre_i, axis=1)
        dgb_ref[...] += _colsum(dpre)
        dpre_b = dpre.astype(BF16)
        dpre_ref[...] = dpre_b
        xcb_ref[...] = xv.astype(BF16)
        dxc_ref[...] = dxc + _dot_nt(dpre_b, wg_ref[...])

    prev, nxt = _halo_specs(s, t, RGW)
    return pl.pallas_call(
        body, name=name, grid=(s // t,),
        in_specs=[_rows(t, RGW), _full((RGW, 4 * RGW)), _full((1, 4 * RGW)), _full((1, 2 * RGW))] + [_rows(t, RGW)] * 2
        + [_rows(t, RGW), prev, nxt] * 2,
        out_specs=[_rows(t, RGW), _rows(t, 4 * RGW), _rows(t, RGW), _full((1, 4 * RGW)), _full((1, 2 * RGW))],
        out_shape=[_sds((s, RGW)), _sds((s, 4 * RGW), BF16), _sds((s, RGW), BF16), _sds((1, 4 * RGW)), _sds((1, 2 * RGW))],
        compiler_params=_cparams(1),
    )(xc, wgates, gbias, lam_row, lam0, lam1, hf, hf, hf, hb, hb, hb)


class _GdnMasks:
    def __init__(self, d):
        ri = lax.broadcasted_iota(jnp.int32, (CHUNK, CHUNK), 0)
        ci = lax.broadcasted_iota(jnp.int32, (CHUNK, CHUNK), 1)
        self.incl = (ri >= ci) if d == 0 else (ri <= ci)
        self.strict = (ri > ci) if d == 0 else (ri < ci)
        b16 = jnp.right_shift(ri, 4) == jnp.right_shift(ci, 4)
        b32 = jnp.right_shift(ri, 5) == jnp.right_shift(ci, 5)
        self.diag16 = b16
        self.off32 = jnp.logical_and(b32, jnp.logical_not(b16))
        self.off64 = jnp.logical_not(b32)
        self.eye = jnp.where(ri == ci, 1.0, 0.0).astype(F32)
        self.tri = jnp.where(self.incl, 1.0, 0.0).astype(F32)
        self.last = CHUNK - 1 if d == 0 else 0


def _tri_inv(lmat, m):
    return _tri_inv_many([lmat], [m])[0]


def _tri_inv_many(lmats, masks):
    n = len(lmats)
    ns = [jnp.where(masks[i].diag16, lmats[i], 0.0) for i in range(n)]
    ps = [masks[i].eye - ns[i] for i in range(n)]
    qs = [_dot3(ns[i], ns[i]) for i in range(n)]
    for step in range(3):
        ps = [_dot3(ps[i], masks[i].eye + qs[i]) for i in range(n)]
        if step < 2:
            qs = [_dot3(qs[i], qs[i]) for i in range(n)]
    for off in ("off32", "off64"):
        ts = [_dot3(ps[i], jnp.where(getattr(masks[i], off), lmats[i], 0.0)) for i in range(n)]
        ps = [ps[i] - _dot3(ts[i], ps[i]) for i in range(n)]
    return ps


def _chunk_cumsums(m, bgv):
    return _dot_exact(m.tri, bgv, _NN, True), _dot_exact(m.tri, bgv, ((0,), (1,)), False)


class _GdnHead:
    def __init__(self, qh, kh, vh, kk, q0, bg, gcs, gcs_t, d, h, m):
        cb = 4 * d + h
        cg = 8 + 4 * d + h
        self.q, self.k, self.v = qh, kh, vh
        self.beta = bg[:, cb:cb + 1]
        gcol = gcs[:, cg:cg + 1]
        grow = gcs_t[cg:cg + 1, :]
        gl = gcs[m.last:m.last + 1, cg:cg + 1]
        self.decay = jnp.exp(jnp.where(m.incl, gcol - grow, -1e30))
        self.kb = kh * self.beta
        self.vb = vh * self.beta
        self.a0 = kk * self.beta
        self.q0 = q0
        self.lmat = jnp.where(m.strict, self.a0 * self.decay, 0.0)
        self.attn = self.q0 * self.decay
        self.eg = jnp.exp(gcol)
        self.ek = jnp.exp(gl - gcol)
        self.cd = jnp.exp(gl)
        self.kg = self.kb * self.eg
        self.qd = qh * self.eg
        self.kd = kh * self.ek


HW = NH * DH
SEQ_CB = 4
LOCAL_CB = 4


def _head(h):
    return slice(DH * h, DH * (h + 1))


def _gdn_local_fwd(q, k, v, bg, name):
    s = q.shape[0]
    n = s // CHUNK
    cb = min(LOCAL_CB, n)

    def body(q_ref, k_ref, v_ref, bg_ref, t_ref, u_ref, w_ref, qd_ref, kd_ref, at_ref, cd_ref):
        masks = [_GdnMasks(d) for d in range(2)]
        inst = []
        for jj in range(cb):
            rows = slice(CHUNK * jj, CHUNK * (jj + 1))
            bgv = bg_ref[rows, :]
            qs = [q_ref[rows, _head(h)] for h in range(NH)]
            ks = [k_ref[rows, _head(h)] for h in range(NH)]
            kk = [_dot_nt(ks[h], ks[h]) for h in range(NH)]
            q0 = [_dot_nt(qs[h], ks[h]) for h in range(NH)]
            for d, m in enumerate(masks):
                gcs, gcs_t = _chunk_cumsums(m, bgv)
                for h in range(NH):
                    c = _GdnHead(qs[h], ks[h], v_ref[rows, _head(h)], kk[h], q0[h], bgv, gcs, gcs_t, d, h, m)
                    inst.append((jj, rows, d, h, m, c))
        tms = _tri_inv_many([it[-1].lmat for it in inst], [it[-2] for it in inst])
        for (jj, rows, d, h, m, c), tm in zip(inst, tms):
            sl = _head(h)
            t_ref[jj, d, h] = tm
            u_ref[d, rows, sl] = _dot(tm, c.vb).astype(BF16)
            w_ref[d, rows, sl] = _dot(tm, c.kg).astype(BF16)
            qd_ref[d, rows, sl] = c.qd.astype(BF16)
            kd_ref[d, rows, sl] = c.kd.astype(BF16)
            at_ref[jj, d, h] = c.attn.astype(BF16)
            cd_ref[jj, 4 * d + h:4 * d + h + 1, :] = jnp.broadcast_to(c.cd, (1, DH))

    tok = _rows(cb * CHUNK, HW)
    tok2 = pl.BlockSpec((2, cb * CHUNK, HW), lambda i: (0, i, 0))
    mat = pl.BlockSpec((cb, 2, NH, CHUNK, CHUNK), lambda i: (i, 0, 0, 0, 0))
    return pl.pallas_call(
        body, name=name, grid=(n // cb,), in_specs=[tok, tok, tok, _rows(cb * CHUNK, BAP)],
        out_specs=[mat, tok2, tok2, tok2, tok2, mat, pl.BlockSpec((cb, 8, DH), lambda i: (i, 0, 0))],
        out_shape=[_sds((n, 2, NH, CHUNK, CHUNK)), _sds((2, s, HW), BF16), _sds((2, s, HW), BF16), _sds((2, s, HW), BF16),
                   _sds((2, s, HW), BF16), _sds((n, 2, NH, CHUNK, CHUNK), BF16), _sds((n, 8, DH))],
        compiler_params=_cparams(1),
    )(q, k, v, bg)


def _seq_specs(s, order):
    n = s // CHUNK
    cb = min(SEQ_CB, n)
    nb = n // cb
    tb = cb * CHUNK

    def blk(d):
        return (lambda i: i) if order[d] else (lambda i: nb - 1 - i)

    def per_dir(make):
        return [make(d, blk(d)) for d in range(2)]

    tok2 = per_dir(lambda d, f: pl.BlockSpec((1, tb, HW), lambda i: (d, f(i), 0)))
    tok = per_dir(lambda d, f: pl.BlockSpec((tb, HW), lambda i: (f(i), 0)))
    mat = per_dir(lambda d, f: pl.BlockSpec((cb, 1, NH, CHUNK, CHUNK), lambda i: (f(i), d, 0, 0, 0)))
    cds = per_dir(lambda d, f: pl.BlockSpec((cb, 8, DH), lambda i: (f(i), 0, 0)))
    sts = per_dir(lambda d, f: pl.BlockSpec((cb, NH, DH, DH), lambda i: (f(i), 0, 0, 0)))
    dcd = per_dir(lambda d, f: pl.BlockSpec((cb, NH, DH), lambda i: (f(i), 0, 0)))
    return n, cb, nb, tok2, tok, mat, cds, sts, dcd


class _ScanRider:
    def __init__(self, af, bf, ar, br, shifted, tb, nb, up_spec, down_spec):
        s, c = af.shape
        self.shifted, self.t, self.c, self.nb = shifted, tb, c, nb
        self.args = [af, bf, ar, br]
        self.in_specs = [up_spec, up_spec, down_spec, down_spec]
        self.scratch = [pltpu.VMEM((16, c), F32)]
        if shifted:
            tb8 = tb // 8
            self.args += [af, ar]
            self.in_specs += [pl.BlockSpec((8, c), lambda i: (jnp.maximum(i * tb8 - 1, 0), 0)),
                              pl.BlockSpec((8, c), lambda i: (jnp.minimum((nb - i) * tb8, s // 8 - 1), 0))]
            self.scratch += [pltpu.VMEM((tb + 8, c), F32), pltpu.VMEM((tb + 8, c), F32)]
        self.out_specs = [up_spec, down_spec]
        self.out_shape = [_sds((s, c)), _sds((s, c))]

    def begin(self, in_refs, out_refs, scratch_refs):
        i = pl.program_id(0)
        self.carry = scratch_refs[0]

        @pl.when(i == 0)
        def _():
            self.carry[...] = jnp.zeros_like(self.carry)

        af_ref, self.bf_ref, ar_ref, self.br_ref = in_refs[0:4]
        self.hf_ref, self.hr_ref = out_refs
        self.a_up, self.a_dn = af_ref, ar_ref
        if self.shifted:
            t = self.t
            edge = jnp.where(i > 0, 1.0, 0.0).astype(F32)
            fbuf, rbuf = scratch_refs[1:3]
            fbuf[0:8, :] = in_refs[4][...] * edge
            fbuf[8:t + 8, :] = af_ref[...]
            rbuf[0:t, :] = ar_ref[...]
            rbuf[t:t + 8, :] = in_refs[5][...] * edge
            self.a_up, self.a_dn = fbuf, rbuf
        self.row = lax.broadcasted_iota(jnp.int32, (8, self.c), 0)
        self.cf, self.cr = self.carry[0:1, :], self.carry[8:9, :]

    def groups(self, lo, hi):
        ng = self.t // 8
        row = self.row
        for gi in range(lo, hi):
            rf, rr = 8 * gi, 8 * (ng - 1 - gi)
            if self.shifted:
                a_f = jnp.where(row > 0, pltpu.roll(self.a_up[rf + 8:rf + 16, :], 1, 0), pltpu.roll(self.a_up[rf:rf + 8, :], 1, 0))
                a_r = jnp.where(row < 7, pltpu.roll(self.a_dn[rr:rr + 8, :], 7, 0), pltpu.roll(self.a_dn[rr + 8:rr + 16, :], 7, 0))
            else:
                a_f, a_r = self.a_up[rf:rf + 8, :], self.a_dn[rr:rr + 8, :]
            a_f, b_f = _block_scan(a_f, self.bf_ref[rf:rf + 8, :], row, False)
            a_r, b_r = _block_scan(a_r, self.br_ref[rr:rr + 8, :], row, True)
            h_f = a_f * self.cf + b_f
            h_r = a_r * self.cr + b_r
            self.hf_ref[rf:rf + 8, :] = h_f
            self.hr_ref[rr:rr + 8, :] = h_r
            self.cf, self.cr = h_f[7:8, :], h_r[0:1, :]

    def end(self):
        self.carry[0:1, :] = self.cf
        self.carry[8:9, :] = self.cr


def _gdn_seq_fwd(u, w, qd, kd, at, cd, name, scan=None):
    s = u.shape[1]
    n, cb, nb, tok2, tok, mat, cds, sts, _ = _seq_specs(s, (True, False))
    rider = _ScanRider(*scan, False, cb * CHUNK, nb, tok[0], tok[1]) if scan else None
    ri = len(rider.args) if rider else 0

    def body(*refs):
        ins = (refs[0:6], refs[6:12])
        outs = (refs[12 + ri:15 + ri], refs[15 + ri:18 + ri])
        st = refs[18 + ri + (2 if rider else 0)]
        if rider:
            rider.begin(refs[12:12 + ri], refs[18 + ri:20 + ri], refs[21 + ri:])

        @pl.when(pl.program_id(0) == 0)
        def _():
            st[...] = jnp.zeros_like(st)

        for j in range(cb):
            items = []
            for d in range(2):
                jj = j if d == 0 else cb - 1 - j
                items += [(d, h, jj, slice(CHUNK * jj, CHUNK * (jj + 1)), _head(h)) for h in range(NH)]
            shs = [st[d, h] for d, h, _, _, _ in items]
            wss = [_dot(ins[d][1][0, rows, sl], sh) for (d, h, jj, rows, sl), sh in zip(items, shs)]
            vns = [ins[d][0][0, rows, sl].astype(F32) - ws for (d, h, jj, rows, sl), ws in zip(items, wss)]
            news = [sh * ins[d][5][jj, 4 * d + h:4 * d + h + 1, :] + _dot_tn(ins[d][3][0, rows, sl], vn)
                    for (d, h, jj, rows, sl), sh, vn in zip(items, shs, vns)]
            for (d, h, jj, rows, sl), sh, vn, new in zip(items, shs, vns, news):
                o_r, s_r, vn_r = outs[d]
                st[d, h] = new
                s_r[jj, h] = sh.astype(BF16)
                vn_r[rows, sl] = vn.astype(BF16)
                o_r[rows, sl] = _dot(ins[d][2][0, rows, sl], sh) + _dot(ins[d][4][jj, 0, h], vn)
            if rider:
                rider.groups(8 * j, 8 * (j + 1))
        if rider:
            rider.end()

    in_specs, out_specs, out_shape = [], [], []
    for d in range(2):
        in_specs += [tok2[d]] * 4 + [mat[d], cds[d]]
        out_specs += [tok[d], sts[d], tok[d]]
        out_shape += [_sds((s, HW)), _sds((n, NH, DH, DH), BF16), _sds((s, HW), BF16)]
    args = [u, w, qd, kd, at, cd, u, w, qd, kd, at, cd]
    scratch = [pltpu.VMEM((2, NH, DH, DH), F32)]
    if rider:
        in_specs, args = in_specs + rider.in_specs, args + rider.args
        out_specs, out_shape, scratch = out_specs + rider.out_specs, out_shape + rider.out_shape, scratch + rider.scratch
    return pl.pallas_call(
        body, name=name, grid=(nb,), in_specs=in_specs, out_specs=out_specs, out_shape=out_shape,
        scratch_shapes=scratch, compiler_params=_cparams(1),
    )(*args)


def _gdn_seq_bwd(do, w, qd, kd, at, cd, states, vns, name, scan=None):
    s = do.shape[0]
    n, cb, nb, tok2, tok, mat, cds, sts, dcd = _seq_specs(s, (False, True))
    rider = _ScanRider(*scan, True, cb * CHUNK, nb, tok[1], tok[0]) if scan else None
    ri = len(rider.args) if rider else 0

    def body(*refs):
        ins = (refs[0:8], refs[8:16])
        outs = (refs[16 + ri:21 + ri], refs[21 + ri:26 + ri])
        dst = refs[26 + ri + (2 if rider else 0)]
        if rider:
            rider.begin(refs[16:16 + ri], refs[26 + ri:28 + ri], refs[29 + ri:])

        @pl.when(pl.program_id(0) == 0)
        def _():
            dst[...] = jnp.zeros_like(dst)

        for j in range(cb):
            items = []
            for d in range(2):
                jj = cb - 1 - j if d == 0 else j
                items += [(d, h, jj, slice(CHUNK * jj, CHUNK * (jj + 1)), _head(h)) for h in range(NH)]
            dsns = [dst[d, h] for d, h, _, _, _ in items]
            dohs = [ins[d][0][rows, sl] for d, h, jj, rows, sl in items]
            d_vns = [_dot_tn(ins[d][4][jj, 0, h], doh) + _dot(ins[d][3][0, rows, sl], dsn)
                     for (d, h, jj, rows, sl), doh, dsn in zip(items, dohs, dsns)]
            news = [ins[d][5][jj, 4 * d + h:4 * d + h + 1, :] * dsn + _dot_tn(ins[d][2][0, rows, sl], doh)
                    - _dot_tn(ins[d][1][0, rows, sl], d_vn)
                    for (d, h, jj, rows, sl), doh, dsn, d_vn in zip(items, dohs, dsns, d_vns)]
            for (d, h, jj, rows, sl), doh, dsn, d_vn, new in zip(items, dohs, dsns, d_vns, news):
                dvn_r, dkd_r, dqd_r, dw_r, dcd_r = outs[d]
                sh = ins[d][6][jj, h].astype(F32)
                dst[d, h] = new
                dvn_r[rows, sl] = d_vn.astype(BF16)
                dkd_r[rows, sl] = _dot_nt(ins[d][7][rows, sl], dsn)
                dqd_r[rows, sl] = _dot_nt(doh, sh)
                dw_r[rows, sl] = (-_dot_nt(d_vn, sh)).astype(BF16)
                d_cd = jnp.sum(jnp.sum(sh * dsn, axis=1, keepdims=True), axis=0, keepdims=True)
                dcd_r[jj, h:h + 1, :] = jnp.broadcast_to(d_cd, (1, DH))
            if rider:
                rider.groups(8 * j, 8 * (j + 1))
        if rider:
            rider.end()

    in_specs, out_specs, out_shape, args = [], [], [], []
    for d in range(2):
        in_specs += [tok[d]] + [tok2[d]] * 3 + [mat[d], cds[d], sts[d], tok[d]]
        args += [do, w, qd, kd, at, cd, states[d], vns[d]]
        out_specs += [tok[d]] * 4 + [dcd[d]]
        out_shape += [_sds((s, HW), BF16), _sds((s, HW)), _sds((s, HW)), _sds((s, HW), BF16), _sds((n, NH, DH))]
    scratch = [pltpu.VMEM((2, NH, DH, DH), F32)]
    if rider:
        in_specs, args = in_specs + rider.in_specs, args + rider.args
        out_specs, out_shape, scratch = out_specs + rider.out_specs, out_shape + rider.out_shape, scratch + rider.scratch
    return pl.pallas_call(
        body, name=name, grid=(nb,), in_specs=in_specs, out_specs=out_specs, out_shape=out_shape,
        scratch_shapes=scratch, compiler_params=_cparams(1),
    )(*args)


def _gdn_local_bwd(q, k, v, bg, tmat, do, vns, seq_grads, name, comm=None):
    s = q.shape[0]
    n = s // CHUNK
    cb = min(LOCAL_CB, n)

    def body(*refs):
        q_ref, k_ref, v_ref, bg_ref, t_ref, do_ref = refs[0:6]
        vn_refs = refs[6:8]
        sg = (refs[8:13], refs[13:18])
        dq_ref, dk_ref, dv_ref, dbg_ref = refs[18:]
        lane = lax.broadcasted_iota(jnp.int32, (CHUNK, BAP), 1)
        rowi = lax.broadcasted_iota(jnp.int32, (CHUNK, 1), 0)
        ones = jnp.ones((CHUNK, DH), F32)
        masks = [_GdnMasks(d) for d in range(2)]
        inst = []
        for jj in range(cb):
            rows = slice(CHUNK * jj, CHUNK * (jj + 1))
            bgv = bg_ref[rows, :]
            qs = [q_ref[rows, _head(h)] for h in range(NH)]
            ks = [k_ref[rows, _head(h)] for h in range(NH)]
            kk = [_dot_nt(ks[h], ks[h]) for h in range(NH)]
            q0 = [_dot_nt(qs[h], ks[h]) for h in range(NH)]
            for d, m in enumerate(masks):
                gcs, gcs_t = _chunk_cumsums(m, bgv)
                for h in range(NH):
                    c = _GdnHead(qs[h], ks[h], v_ref[rows, _head(h)], kk[h], q0[h], bgv, gcs, gcs_t, d, h, m)
                    inst.append((jj, rows, d, h, m, c))
        ni = len(inst)
        cs = [it[-1] for it in inst]
        tms = [t_ref[jj, d, h] for jj, _, d, h, _, _ in inst]
        d_vns = [sg[d][0][rows, _head(h)] for _, rows, d, h, _, _ in inst]
        d_ws = [sg[d][3][rows, _head(h)] for _, rows, d, h, _, _ in inst]
        d_ts = [_dot_nt(d_vns[i], cs[i].vb) + _dot_nt(d_ws[i], cs[i].kg) for i in range(ni)]
        tts = [tm.T for tm in tms]
        xs = [_dot3(tts[i], d_ts[i]) for i in range(ni)]
        d_ls = [jnp.where(inst[i][4].strict, -_dot3(xs[i], tts[i]), 0.0) for i in range(ni)]
        d_attns = [jnp.where(m.incl, _dot_nt(do_ref[rows, _head(h)], vn_refs[d][rows, _head(h)]), 0.0)
                   for _, rows, d, h, m, _ in inst]
        d_vbs = [_dot(tts[i], d_vns[i]) for i in range(ni)]
        d_kgs = [_dot(tts[i], d_ws[i]) for i in range(ni)]
        d_a0s = [d_ls[i] * cs[i].decay for i in range(ni)]
        d_q0s = [d_attns[i] * cs[i].decay for i in range(ni)]
        es = [(d_ls[i] * cs[i].a0 + d_attns[i] * cs[i].q0) * cs[i].decay for i in range(ni)]
        kb_mm = [_dot(d_a0s[i], cs[i].k) for i in range(ni)]
        q_mm = [_dot(d_q0s[i], cs[i].k) for i in range(ni)]
        k_mm = [_dot_tn(d_a0s[i], cs[i].kb) + _dot_tn(d_q0s[i], cs[i].q) for i in range(ni)]
        e_cols = [_dot_exact(ones, es[i], _TN, False)[:, 0:1] for i in range(ni)]
        acc = {}
        d_gcs, d_betas = [], []
        for i, (jj, rows, d, h, m, c) in enumerate(inst):
            sl = _head(h)
            d_kd, d_qd = sg[d][1][rows, sl], sg[d][2][rows, sl]
            d_cd = sg[d][4][jj, h:h + 1, 0:1]
            d_vb, d_kg = d_vbs[i], d_kgs[i]
            d_kb = kb_mm[i] + d_kg * c.eg
            parts = (q_mm[i] + d_qd * c.eg, k_mm[i] + d_kd * c.ek + d_kb * c.beta, d_vb * c.beta)
            acc[jj, h] = [p + a for a, p in zip(acc[jj, h], parts)] if (jj, h) in acc else list(parts)
            kd_term = d_kd * c.kd
            d_gc = (jnp.sum(d_kg * c.kg + d_qd * c.qd - kd_term, axis=1, keepdims=True)
                    + jnp.sum(es[i], axis=1, keepdims=True) - e_cols[i])
            d_gl = jnp.sum(jnp.sum(kd_term, axis=0, keepdims=True), axis=1, keepdims=True) + d_cd * c.cd
            d_gcs.append(d_gc + jnp.where(rowi == m.last, d_gl, 0.0))
            d_betas.append(jnp.sum(d_kb * c.k + d_vb * c.v, axis=1, keepdims=True))
        d_gs = [_dot_exact(inst[i][4].tri, d_gcs[i] * ones, _TN, True)[:, 0:1] for i in range(ni)]
        dbg = [jnp.zeros((CHUNK, BAP), F32) for _ in range(cb)]
        for i, (jj, _, d, h, _, _) in enumerate(inst):
            dbg[jj] = dbg[jj] + jnp.where(lane == 4 * d + h, d_betas[i], 0.0) + jnp.where(lane == 8 + 4 * d + h, d_gs[i], 0.0)
        for jj in range(cb):
            rows = slice(CHUNK * jj, CHUNK * (jj + 1))
            for h in range(NH):
                dq_ref[rows, _head(h)], dk_ref[rows, _head(h)], dv_ref[rows, _head(h)] = acc[jj, h]
            dbg_ref[rows, :] = dbg[jj]

    tok = _rows(cb * CHUNK, HW)
    bgs = _rows(cb * CHUNK, BAP)
    mat = pl.BlockSpec((cb, 2, NH, CHUNK, CHUNK), lambda i: (i, 0, 0, 0, 0))
    dcd = pl.BlockSpec((cb, NH, DH), lambda i: (i, 0, 0))
    args = [q, k, v, bg, tmat, do, vns[0], vns[1]]
    in_specs = [tok, tok, tok, bgs, mat, tok, tok, tok]
    for d in range(2):
        args += list(seq_grads[d])
        in_specs += [tok] * 4 + [dcd]
    return _pallas(body, comm, name=name, grid=(n // cb,), in_specs=in_specs, out_specs=[tok, tok, tok, bgs],
                   out_shape=[_sds((s, HW))] * 3 + [_sds((s, BAP))], scratch_shapes=[], args=args)


def _prep_bwd(c_qkv, p_ba, alog_row, dtb_row, dq, dk, dv, dbg, name):
    s = c_qkv.shape[0]
    t = min(ROW_TILE, s)

    def body(cq_ref, pc_ref, alog_ref, dtb_ref, dq_ref, dk_ref, dv_ref, dbg_ref,
             dcq_ref, dpc_ref, dalog_ref, ddtb_ref):
        @pl.when(pl.program_id(0) == 0)
        def _():
            dalog_ref[...] = jnp.zeros_like(dalog_ref)
            ddtb_ref[...] = jnp.zeros_like(ddtb_ref)

        cq = cq_ref[...]
        sq = cq * _sig(cq)
        sg = _silu_grad(cq)
        for h in range(NH):
            sl = slice(DH * h, DH * (h + 1))
            for off, d_ref, scale in ((0, dq_ref, DH ** -0.5), (RGW, dk_ref, 1.0)):
                csl = slice(off + DH * h, off + DH * (h + 1))
                xh = sq[:, csl]
                nrm = lax.rsqrt(jnp.sum(xh * xh, axis=-1, keepdims=True) + EPS)
                y = xh * nrm
                dy = d_ref[:, sl] * scale
                dcq_ref[:, csl] = nrm * (dy - y * jnp.sum(dy * y, axis=-1, keepdims=True)) * sg[:, csl]
        dcq_ref[:, 2 * RGW:] = dv_ref[...] * sg[:, 2 * RGW:]
        pc = pc_ref[...]
        lane = lax.broadcasted_iota(jnp.int32, pc.shape, 1)
        dbg = dbg_ref[...]
        beta = _sig(pc)
        ea = jnp.exp(alog_ref[...])
        z = pc + dtb_ref[...]
        g = -ea * _softplus(z)
        is_g = jnp.logical_and(lane >= 8, lane < 16)
        d_alpha = jnp.where(is_g, dbg * (-ea) * _sig(z), 0.0)
        dpc_ref[...] = jnp.where(lane < 8, dbg * beta * (1.0 - beta), d_alpha).astype(BF16)
        dalog_ref[...] += _colsum(jnp.where(is_g, dbg * g, 0.0))
        ddtb_ref[...] += _colsum(d_alpha)

    return pl.pallas_call(
        body, name=name, grid=(s // t,),
        in_specs=[_rows(t, QKVW), _rows(t, BAP), _full((1, BAP)), _full((1, BAP))] + [_rows(t, HW)] * 3 + [_rows(t, BAP)],
        out_specs=[_rows(t, QKVW), _rows(t, BAP), _full((1, BAP)), _full((1, BAP))],
        out_shape=[_sds((s, QKVW)), _sds((s, BAP), BF16), _sds((1, BAP)), _sds((1, BAP))],
        compiler_params=_cparams(1),
    )(c_qkv, p_ba, alog_row, dtb_row, dq, dk, dv, dbg)


def _mix_out_values(hf, hb, gate, of, ob, z, gn):
    hr = hf + hb
    y_rg = hr * _gelu(gate)
    osum = of + ob
    parts = []
    for h in range(NH):
        sl = slice(DH * h, DH * (h + 1))
        oh = osum[:, sl]
        r, ohat = _rms(oh)
        zh = z[:, sl]
        parts.append((r, ohat, zh))
    y_gdn = jnp.concatenate([ohat * gn * (zh * _sig(zh)) for (r, ohat, zh) in parts], axis=1)
    return hr, y_rg, y_gdn, parts


def _outproj(x1, hf, hb, gate, of, ob, z, gn, wout, name):
    s = x1.shape[0]
    t = min(ROW_TILE, s)

    def body(x_ref, hf_ref, hb_ref, gate_ref, of_ref, ob_ref, z_ref, gn_ref, w_ref, xo_ref, y_ref):
        _, y_rg, y_gdn, _ = _mix_out_values(hf_ref[...], hb_ref[...], gate_ref[...], of_ref[...], ob_ref[...],
                                            z_ref[...], gn_ref[...])
        y = jnp.concatenate([y_rg, y_gdn], axis=1).astype(BF16)
        y_ref[...] = y
        xo_ref[...] = x_ref[...] + jnp.dot(y, w_ref[...], preferred_element_type=F32)

    return pl.pallas_call(
        body, name=name, grid=(s // t,),
        in_specs=[_rows(t, D)] + [_rows(t, RGW)] * 6 + [_full((1, DH)), _full((D, D))],
        out_specs=[_rows(t, D), _rows(t, D)], out_shape=[_sds((s, D)), _sds((s, D), BF16)],
        compiler_params=_cparams(1),
    )(x1, hf, hb, gate, of, ob, z, gn, wout)


def _outproj_bwd(dx2, hf, hb, gate, of, ob, z, gn, wout, name, comm=None):
    s = dx2.shape[0]
    t = min(ROW_TILE, s)

    def body(d_ref, hf_ref, hb_ref, gate_ref, of_ref, ob_ref, z_ref, gn_ref, w_ref,
             dhr_ref, dgate_ref, dos_ref, dz_ref, dgn_ref, db_ref):
        @pl.when(pl.program_id(0) == 0)
        def _():
            dgn_ref[...] = jnp.zeros_like(dgn_ref)

        gate = gate_ref[...]
        gn_v = gn_ref[...]
        hr, _, _, parts = _mix_out_values(hf_ref[...], hb_ref[...], gate, of_ref[...], ob_ref[...], z_ref[...], gn_v)
        dbf = d_ref[...].astype(BF16)
        db_ref[...] = dbf
        dy = _dot_nt(dbf, w_ref[...])
        dyr = dy[:, :RGW]
        dhr_ref[...] = dyr * _gelu(gate)
        dgate_ref[...] = (dyr * hr * _gelu_grad(gate)).astype(BF16)
        dgn = jnp.zeros((1, DH), F32)
        for h, (r, ohat, zh) in enumerate(parts):
            sl = slice(DH * h, DH * (h + 1))
            dyh = dy[:, RGW + DH * h:RGW + DH * (h + 1)]
            sz = zh * _sig(zh)
            dn = dyh * sz
            dz_ref[:, sl] = (dyh * ohat * gn_v * _silu_grad(zh)).astype(BF16)
            dgn = dgn + _colsum(dn * ohat)
            dos_ref[:, sl] = _rms_bwd(dn, ohat, r, gn_v).astype(BF16)
        dgn_ref[...] += dgn

    return _pallas(
        body, comm, name=name, grid=(s // t,),
        in_specs=[_rows(t, D)] + [_rows(t, RGW)] * 6 + [_full((1, DH)), _full((D, D))],
        out_specs=[_rows(t, RGW)] * 4 + [_full((1, DH)), _rows(t, D)],
        out_shape=[_sds((s, RGW))] + [_sds((s, RGW), BF16)] * 3 + [_sds((1, DH)), _sds((s, D), BF16)],
        scratch_shapes=[], args=(dx2, hf, hb, gate, of, ob, z, gn, wout))


def _loss_head(x3, target, gain, name):
    s = x3.shape[0]
    t = min(ROW_TILE, s)

    def body(x_ref, t_ref, g_ref, dx_ref, dxh_ref, loss_ref, dg_ref):
        @pl.when(pl.program_id(0) == 0)
        def _():
            loss_ref[...] = jnp.zeros_like(loss_ref)
            dg_ref[...] = jnp.zeros_like(dg_ref)

        r, xh = _rms(x_ref[...])
        gv = g_ref[...]
        err = xh * gv - t_ref[...]
        per_tok = jnp.mean(err * err, axis=-1, keepdims=True)
        loss_ref[...] += 0.5 * jnp.sum(per_tok, axis=0, keepdims=True)
        dy = err * (1.0 / D)
        dg_ref[...] += _colsum(dy * xh)
        dx = _rms_bwd(dy, xh, r, gv)
        dx_ref[...] = dx
        dxh_ref[...] = (0.5 * dx).astype(BF16)

    return pl.pallas_call(
        body, name=name, grid=(s // t,), in_specs=[_rows(t, D), _rows(t, D), _full((1, D))],
        out_specs=[_rows(t, D), _rows(t, D), _full((8, 128)), _full((1, D))],
        out_shape=[_sds((s, D)), _sds((s, D), BF16), _sds((8, 128)), _sds((1, D))], compiler_params=_cparams(1),
    )(x3, target, gain)


def _adamw_math(wv, gv, mv, vv):
    mn = ADAM_B1 * mv + (1.0 - ADAM_B1) * gv
    vn = ADAM_B2 * vv + (1.0 - ADAM_B2) * (gv * gv)
    m_hat = mn / (1.0 - ADAM_B1 ** ADAM_STEP)
    v_hat = vn / (1.0 - ADAM_B2 ** ADAM_STEP)
    return -ADAM_LR * (m_hat / (jnp.sqrt(v_hat) + ADAM_EPS) + ADAM_WD * wv), mn, vn


def _row_tile(r, c):
    tr = r
    while tr * c * 4 > (1 << 20) and tr % 16 == 0:
        tr //= 2
    return tr


def _adamw(w, g, m, v, name):
    r, c = w.shape
    tr = _row_tile(r, c)

    def body(w_ref, g_ref, m_ref, v_ref, d_ref, nm_ref, nv_ref):
        d_ref[...], nm_ref[...], nv_ref[...] = _adamw_math(w_ref[...], g_ref[...], m_ref[...], v_ref[...])

    return pl.pallas_call(
        body, name=name, grid=(r // tr,), in_specs=[_rows(tr, c)] * 4, out_specs=[_rows(tr, c)] * 3,
        out_shape=[_sds((r, c))] * 3, compiler_params=_cparams(1),
    )(w, g, m, v)


def _adamw_halves(w, own, recv, m, v, c_arr, name):
    r, c = w.shape
    h = r // 2
    tr = _row_tile(h, c)
    nh = h // tr

    def body(c_ref, w_ref, own_ref, recv_ref, m_ref, v_ref, g_ref, d_ref, nm_ref, nv_ref):
        first_half = pl.program_id(0) < nh
        use_own = first_half == (c_ref[0] == 0)
        gv = jnp.where(use_own, own_ref[...], recv_ref[...])
        g_ref[...] = gv
        d_ref[...], nm_ref[...], nv_ref[...] = _adamw_math(w_ref[...], gv, m_ref[...], v_ref[...])

    full = pl.BlockSpec((tr, c), lambda i, c_ref: (i, 0))
    half = pl.BlockSpec((tr, c), lambda i, c_ref: (i % nh, 0))
    return pl.pallas_call(
        body, name=name, out_shape=[_sds((r, c))] * 4,
        grid_spec=pltpu.PrefetchScalarGridSpec(
            num_scalar_prefetch=1, grid=(2 * nh,), in_specs=[full, half, half, full, full], out_specs=[full] * 4),
        compiler_params=_cparams(1),
    )(c_arr, w, own, recv, m, v)


def _mesh_pos():
    return lax.axis_index("x"), lax.axis_index("y"), lax.axis_index("c")


def _other_chips(x, y):
    return [(1 - x, y), (x, 1 - y), (1 - x, 1 - y)]


class _Comm:
    def __init__(self, inputs, out_shapes, scratch, start, finish, space=pltpu.HBM, relay=None):
        self.inputs, self.out_shapes, self.scratch = list(inputs), list(out_shapes), list(scratch)
        self.start, self.finish, self.space = start, finish, space
        self.relay = relay if relay is not None else (lambda ins, outs, sems: None)


def _comm_call(comm, name):
    ni, no = len(comm.inputs), len(comm.out_shapes)

    def body(*refs):
        comm.start(refs[:ni], refs[ni:ni + no], refs[ni + no:])
        comm.relay(refs[:ni], refs[ni:ni + no], refs[ni + no:])
        comm.finish(refs[:ni], refs[ni:ni + no], refs[ni + no:])

    spec = pl.BlockSpec(memory_space=comm.space)
    return list(pl.pallas_call(body, name=name, out_shape=comm.out_shapes, in_specs=[spec] * ni, out_specs=[spec] * no,
                               scratch_shapes=comm.scratch)(*comm.inputs))


def _join_comm(a, b):
    ia, oa, sa = len(a.inputs), len(a.out_shapes), len(a.scratch)

    def both(method):
        def run(ins, outs, sems):
            getattr(a, method)(ins[:ia], outs[:oa], sems[:sa])
            getattr(b, method)(ins[ia:], outs[oa:], sems[sa:])
        return run

    return _Comm(a.inputs + b.inputs, a.out_shapes + b.out_shapes, a.scratch + b.scratch, both("start"), both("finish"),
                 relay=both("relay"))


def _pallas(body, comm, *, name, grid, in_specs, out_specs, out_shape, scratch_shapes, args):
    params = _cparams(len(grid))
    if comm is None:
        outs = pl.pallas_call(body, name=name, grid=grid, in_specs=in_specs, out_specs=out_specs, out_shape=out_shape,
                              scratch_shapes=scratch_shapes, compiler_params=params)(*args)
        return list(outs), []
    n_in, n_out, n_sc = len(in_specs), len(out_specs), len(scratch_shapes)
    ci, co = len(comm.inputs), len(comm.out_shapes)

    def carried(*refs):
        bounds = [0, n_in, n_in + ci, n_in + ci + n_out, n_in + ci + n_out + co, n_in + ci + n_out + co + n_sc, len(refs)]
        ins, cins, outs, couts, scr, csems = [refs[lo:hi] for lo, hi in zip(bounds[:-1], bounds[1:])]
        ids = [pl.program_id(k) for k in range(len(grid))]
        first = functools.reduce(jnp.logical_and, [i == 0 for i in ids])
        last = functools.reduce(jnp.logical_and, [i == g - 1 for i, g in zip(ids, grid)])
        late = functools.reduce(jnp.logical_and, [ids[0] == (3 * grid[0]) // 4] + [i == 0 for i in ids[1:]])

        @pl.when(first)
        def _():
            comm.start(cins, couts, csems)

        body(*ins, *outs, *scr)

        @pl.when(late)
        def _():
            comm.relay(cins, couts, csems)

        @pl.when(last)
        def _():
            comm.finish(cins, couts, csems)

    hbm = pl.BlockSpec(memory_space=pltpu.HBM)
    outs = pl.pallas_call(
        carried, name=name, grid=grid, in_specs=list(in_specs) + [hbm] * ci, out_specs=list(out_specs) + [hbm] * co,
        out_shape=list(out_shape) + comm.out_shapes, scratch_shapes=list(scratch_shapes) + comm.scratch,
        compiler_params=params)(*args, *comm.inputs)
    return list(outs[:n_out]), list(outs[n_out:])


def _gather_comm(arrays, space, block_rows):
    n_arr = len(arrays)

    def plan(x_refs, out_refs, sems):
        send_sems, recv_sems, local_sems = sems
        x, y, c = _mesh_pos()
        me, sibling = (x, y, c), (x, y, 1 - c)
        chips = _other_chips(x, y)

        def slot(a, px, py, pc):
            return out_refs[a].at[4 * px + 2 * py + pc]

        def copy(a, k, block, to, src=None):
            return pltpu.make_async_remote_copy(
                src_ref=slot(a, *block) if src is None else src, dst_ref=slot(a, *block),
                send_sem=send_sems.at[7 * a + k], recv_sem=recv_sems.at[7 * a + k], device_id=to, device_id_type=MESH)

        srcs = [x_refs[a] if block_rows[a] is None else
                x_refs[a].at[pl.ds(pl.multiple_of(c * block_rows[a], 16), block_rows[a]), :] for a in range(n_arr)]
        local = [pltpu.make_async_copy(srcs[a], slot(a, *me), local_sems.at[a]) for a in range(n_arr)]
        first = []
        for a in range(n_arr):
            first += [copy(a, 1 + j, me, (*chip, c), src=srcs[a]) for j, chip in enumerate(chips)]
            first.append(copy(a, 0, me, sibling, src=srcs[a]))
        return me, sibling, chips, c, copy, local, first

    def start(x_refs, out_refs, sems):
        _, _, _, _, _, local, first = plan(x_refs, out_refs, sems)
        for cp in local + first:
            cp.start()

    def relay(x_refs, out_refs, sems):
        me, sibling, chips, c, copy, _, _ = plan(x_refs, out_refs, sems)
        for j, chip in enumerate(chips):
            for a in range(n_arr):
                copy(a, 1 + j, (*chip, c), me).wait_recv()
                copy(a, 4 + j, (*chip, c), sibling).start()

    def finish(x_refs, out_refs, sems):
        me, sibling, chips, c, copy, local, first = plan(x_refs, out_refs, sems)
        passed = [copy(a, 4 + j, (*chip, c), sibling) for j, chip in enumerate(chips) for a in range(n_arr)]
        for a in range(n_arr):
            copy(a, 0, sibling, me).wait_recv()
            for j, chip in enumerate(chips):
                copy(a, 4 + j, (*chip, 1 - c), me).wait_recv()
        for cp in first + passed:
            cp.wait_send()
        for cp in local:
            cp.wait()

    out_shapes = [_sds((8, w.shape[0] if r is None else r) + w.shape[1:], w.dtype) for w, r in zip(arrays, block_rows)]
    scratch = [pltpu.SemaphoreType.DMA((7 * n_arr,)), pltpu.SemaphoreType.DMA((7 * n_arr,)), pltpu.SemaphoreType.DMA((n_arr,))]
    return _Comm(arrays, out_shapes, scratch, start, finish, space, relay=relay)


def _weights_gather_comm(shards):
    return _gather_comm(shards, pltpu.HBM, [w.shape[0] // 2 for w in shards])


def _all_shards(gathered):
    return [o.reshape(NSH, 2 * o.shape[1], o.shape[2]) for o in gathered]


def _gather_small(block, name):
    return _comm_call(_gather_comm([block], pltpu.VMEM, [None]), name)[0]


def _exchange_comm(gs):
    n = len(gs)
    halves = [g.shape[1] // 2 for g in gs]

    def plan(g_refs, land_refs, sems):
        send_sems, recv_sems = sems
        x, y, c = _mesh_pos()
        copies = []
        for a in range(n):
            h = halves[a]
            for s in range(NSH):
                copies.append(pltpu.make_async_remote_copy(
                    src_ref=g_refs[a].at[s, pl.ds(pl.multiple_of((1 - c) * h, 8), h), :], dst_ref=land_refs[a].at[s],
                    send_sem=send_sems.at[NSH * a + s], recv_sem=recv_sems.at[NSH * a + s],
                    device_id=(x, y, 1 - c), device_id_type=MESH))
        return copies

    def start(g_refs, land_refs, sems):
        for cp in plan(g_refs, land_refs, sems):
            cp.start()

    def finish(g_refs, land_refs, sems):
        for cp in plan(g_refs, land_refs, sems):
            cp.wait()

    scratch = [pltpu.SemaphoreType.DMA((NSH * n,)), pltpu.SemaphoreType.DMA((NSH * n,))]
    return _Comm(gs, [_sds((NSH, h, g.shape[2])) for h, g in zip(halves, gs)], scratch, start, finish)


def _chip_sum(g, land, c_arr, name):
    _, h, cols = land.shape

    def body(c_ref, g_ref, l_ref, o_ref):
        o_ref[...] = (g_ref[...] + l_ref[...]).astype(BF16)

    return pl.pallas_call(
        body, name=name, out_shape=_sds((NSH, h, cols), BF16),
        grid_spec=pltpu.PrefetchScalarGridSpec(
            num_scalar_prefetch=1, grid=(NSH,),
            in_specs=[pl.BlockSpec((1, h, cols), lambda s, c_ref: (s, c_ref[0], 0)),
                      pl.BlockSpec((1, h, cols), lambda s, c_ref: (s, 0, 0))],
            out_specs=pl.BlockSpec((1, h, cols), lambda s, c_ref: (s, 0, 0))),
        compiler_params=_cparams(1),
    )(c_arr, g, land)


def _scatter_comm(parts):
    n = len(parts)

    def plan(p_refs, land_refs, sems):
        send_sems, recv_sems, local_sems = sems
        x, y, c = _mesh_pos()
        my_chip = 2 * x + y
        local = [pltpu.make_async_copy(p_refs[a].at[my_chip], land_refs[a].at[my_chip], local_sems.at[a]) for a in range(n)]
        copies = []
        for a in range(n):
            for j, (px, py) in enumerate(_other_chips(x, y)):
                copies.append(pltpu.make_async_remote_copy(
                    src_ref=p_refs[a].at[2 * px + py], dst_ref=land_refs[a].at[my_chip],
                    send_sem=send_sems.at[3 * a + j], recv_sem=recv_sems.at[3 * a + j],
                    device_id=(px, py, c), device_id_type=MESH))
        return local, copies

    def start(p_refs, land_refs, sems):
        local, copies = plan(p_refs, land_refs, sems)
        for cp in local + copies:
            cp.start()

    def finish(p_refs, land_refs, sems):
        local, copies = plan(p_refs, land_refs, sems)
        for cp in copies:
            cp.wait()
        for cp in local:
            cp.wait()

    scratch = [pltpu.SemaphoreType.DMA((3 * n,)), pltpu.SemaphoreType.DMA((3 * n,)), pltpu.SemaphoreType.DMA((n,))]
    return _Comm(parts, [_sds(p.shape, BF16) for p in parts], scratch, start, finish)


def _sum_slots(land, name):
    k, r, c = land.shape
    tr = r // 2 if r % 32 == 0 else r

    def body(l_ref, o_ref):
        acc = l_ref[0].astype(F32)
        for i in range(1, k):
            acc = acc + l_ref[i].astype(F32)
        o_ref[...] = acc

    return pl.pallas_call(
        body, name=name, grid=(r // tr,), in_specs=[pl.BlockSpec((k, tr, c), lambda i: (0, i, 0))],
        out_specs=_rows(tr, c), out_shape=_sds((r, c)), compiler_params=_cparams(1),
    )(land)


def _sibling_swap(halves):
    n = len(halves)

    def body(*refs):
        h_refs, out_refs = refs[:n], refs[n:2 * n]
        send_sems, recv_sems = refs[2 * n:]
        x, y, c = _mesh_pos()
        copies = [pltpu.make_async_remote_copy(
            src_ref=h_refs[a], dst_ref=out_refs[a], send_sem=send_sems.at[a], recv_sem=recv_sems.at[a],
            device_id=(x, y, 1 - c), device_id_type=MESH) for a in range(n)]
        for cp in copies:
            cp.start()
        for cp in copies:
            cp.wait()

    return pl.pallas_call(
        body, name="grad_sibling_swap", out_shape=[_sds(h.shape) for h in halves],
        in_specs=[pl.BlockSpec(memory_space=pltpu.HBM)] * n, out_specs=[pl.BlockSpec(memory_space=pltpu.HBM)] * n,
        scratch_shapes=[pltpu.SemaphoreType.DMA((n,)), pltpu.SemaphoreType.DMA((n,))],
    )(*halves)


def _pad_rows(v, width):
    flat = v.reshape(-1)
    rows = -(-flat.shape[0] // width)
    rows = -(-rows // 8) * 8
    return jnp.pad(flat, (0, rows * width - flat.shape[0])).reshape(rows, width)


def _size(shape):
    n = 1
    for dim in shape:
        n *= dim
    return n


def _row_pack(arrs):
    pieces = []
    for a in arrs:
        rows = -(-a.size // D)
        pieces.append(jnp.pad(a.reshape(-1), (0, rows * D - a.size)).reshape(rows, D))
    total = sum(p.shape[0] for p in pieces)
    if total % 8:
        pieces.append(jnp.zeros((8 - total % 8, D), F32))
    return jnp.concatenate(pieces, axis=0)


def _row_unpack(packed, shapes):
    out, r0 = [], 0
    for shp in shapes:
        n = _size(shp)
        rows = -(-n // D)
        out.append(packed[r0:r0 + rows].reshape(-1)[:n].reshape(shp))
        r0 += rows
    return out


def _block_diag(w):
    eye = jnp.eye(8, dtype=w.dtype)
    return (w[:, :, None, :] * eye[:, None, :, None]).reshape(RGW, RGW)


def _diag_blocks(dense):
    r = dense.reshape(8, 64, 8, 64)
    return jnp.stack([r[n, :, n, :] for n in range(8)])


def _lane_row(v8):
    return jnp.zeros((1, BAP), F32).at[0, 8:16].set(v8.reshape(8))


def _chip_sums(gs, lands, names, c_arr):
    return [_chip_sum(g, l, c_arr, "chip_sum_" + n) for g, l, n in zip(gs, lands, names)]


def _reduce_parts(gs, names, c_arr, tag):
    return _chip_sums(gs, _comm_call(_exchange_comm(gs), "grad_sibling_exchange_" + tag), names, c_arr)


def _local_step(x, target, sw, ffn1_w, later_shards, c_arr):
    (g1, gmix, rg_cw8, rg_cb, wgates, gbias, lam_row, gdn_cw8, alog_row, dtb_row, gn, g2, gfin) = sw
    wg1, wu1, wd1 = ffn1_w

    (x1, a1, b1, fb1), gathered = _ffn_fwd(x, g1, wg1, wu1, wd1, "ffn1_fwd", comm=_weights_gather_comm(later_shards))
    win_sh, wout_sh, wg2, wu2, wd2 = _all_shards(gathered)
    w_in_full = jnp.transpose(win_sh, (1, 0, 2)).reshape(D, NSH * INSH)
    wout = wout_sh.reshape(D, D)
    w_in_groups = (w_in_full[:, 0:512], w_in_full[:, 512:1024], w_in_full[:, 1024:2560], w_in_full[:, 2560:3072],
                   jnp.pad(w_in_full[:, 3072:3088], ((0, 0), (0, BAP - BAW))))
    h2, p_rgx, p_gate, p_qkv, p_z, p_ba = _inproj(x1, gmix, w_in_groups, "in_proj")
    c_rg = _conv(p_rgx, rg_cw8, rg_cb, "rg_conv")
    c_qkv = _conv(p_qkv, gdn_cw8, jnp.zeros((1, QKVW), F32), "gdn_conv")
    a0, bb0, a1s, bb1, q, k, v, bg = _mix_prep(c_rg, c_qkv, p_ba, wgates, gbias, lam_row, alog_row, dtb_row, "mix_prep")
    tmat, gu, gw, gqd, gkd, gat, gcd = _gdn_local_fwd(q, k, v, bg, "gdn_local_fwd")
    of, s0, vn0, ob, s1, vn1, hf, hb = _gdn_seq_fwd(gu, gw, gqd, gkd, gat, gcd, "gdn_seq_fwd", scan=(a0, bb0, a1s, bb1))
    x2, ymix = _outproj(x1, hf, hb, p_gate, of, ob, p_z, gn, wout, "out_proj")
    (x3, a2, b2, fb2), _ = _ffn_fwd(x2, g2, wg2, wu2, wd2, "ffn2_fwd")
    dx3, dob2, loss_blk, d_gfin = _loss_head(x3, target, gfin, "loss_head")

    dx2, d_g2, hb2, dab2, dbb2, _ = _ffn_bwd(x2, dx3, dob2, g2, a2, b2, wg2, wu2, wd2, "ffn2_bwd")
    d_ffn2 = [_tn(dab2, hb2, "ffn2_dwg"), _tn(dbb2, hb2, "ffn2_dwu"), _tn(fb2, dob2, "ffn2_dwd")]

    (d_hr, d_gate, d_os, d_z, d_gn, dx2b), lands = _outproj_bwd(dx2, hf, hb, p_gate, of, ob, p_z, gn, wout, "out_proj_bwd",
                                                               comm=_exchange_comm(d_ffn2))
    parts_ffn2 = _chip_sums(d_ffn2, lands, _BIG_NAMES[5:8], c_arr)
    d_wout = _tn(ymix, dx2b, "dw_out")[0]

    sg = _gdn_seq_bwd(d_os, gw, gqd, gkd, gat, gcd, (s0, s1), (vn0, vn1), "gdn_seq_bwd", scan=(a1s, d_hr, a0, d_hr))
    lam1, lam0 = sg[10:12]
    d_xc, d_pre, xcb, d_gbias, d_lam = _gates_bwd(c_rg, wgates, gbias, lam_row, lam0, lam1, hf, hb, "rg_gates_bwd")
    d_wgates = _tn(xcb, d_pre, "dw_gates")[0]
    d_prgx, d_rgcw8, d_rgcb = _conv_bwd(p_rgx, d_xc, rg_cw8, "rg_conv_bwd")

    (dq, dk, dv, dbg), lands_ffn2 = _gdn_local_bwd(q, k, v, bg, tmat, d_os, (vn0, vn1), (sg[0:5], sg[5:10]), "gdn_local_bwd",
                                                  comm=_scatter_comm(parts_ffn2))
    d_cqkv, d_pba, d_alog, d_dtb = _prep_bwd(c_qkv, p_ba, alog_row, dtb_row, dq, dk, dv, dbg, "gdn_prep_bwd")
    d_pqkv, d_gdncw8, _ = _conv_bwd(p_qkv, d_cqkv, gdn_cw8, "gdn_conv_bwd")

    dps = (d_prgx, d_gate, d_pqkv, d_z, d_pba)
    dx1, dob1, d_gmix = _inproj_bwd(x1, dx2, gmix, dps, w_in_groups, "in_proj_bwd")
    d_win_groups = [_tn(h2, dp, "dw_in_%d" % i)[0] for i, dp in enumerate(dps)]
    d_win = jnp.concatenate(d_win_groups[:4] + [d_win_groups[4][:, :BAW]], axis=1)
    d_mix = [jnp.transpose(d_win.reshape(D, NSH, INSH), (1, 0, 2)), d_wout.reshape(NSH, OUTSH, D)]

    small = dict(
        mix_norm=d_gmix, rg_conv_w=d_rgcw8[:4], rg_conv_b=d_rgcb,
        rg_gate_a_w=jnp.stack([_diag_blocks(d_wgates[:, RGW * i:RGW * (i + 1)]) for i in (0, 1)]),
        rg_gate_x_w=jnp.stack([_diag_blocks(d_wgates[:, RGW * i:RGW * (i + 1)]) for i in (2, 3)]),
        rg_gate_a_b=d_gbias[0, :2 * RGW].reshape(2, RGW), rg_gate_x_b=d_gbias[0, 2 * RGW:].reshape(2, RGW),
        rg_lambda=d_lam.reshape(2, RGW), gdn_conv_w=d_gdncw8[:4],
        gdn_a_log=d_alog[0, 8:16].reshape(2, NH), gdn_dt_bias=d_dtb[0, 8:16].reshape(2, NH),
        gdn_norm=d_gn, ffn2_norm=d_g2, final_norm=d_gfin)
    small_pack = _row_pack([small[n] for n in _SMALL_NAMES[1:]])

    riders = _join_comm(_exchange_comm(d_mix), _gather_comm([small_pack], pltpu.HBM, [None]))
    gx, d_g1, hb1, dab1, dbb1, carried = _ffn_bwd(x, dx1, dob1, g1, a1, b1, wg1, wu1, wd1, "ffn1_bwd", comm=riders)
    parts_mix = _chip_sums(d_mix, carried[0:2], _BIG_NAMES[3:5], c_arr)
    d_wg1, lands_mix = _tn(dab1, hb1, "ffn1_dwg", comm=_scatter_comm(parts_mix))
    parts_wg1 = _reduce_parts([d_wg1], _BIG_NAMES[0:1], c_arr, "ffn1_gate")
    d_wu1, lands_wg1 = _tn(dbb1, hb1, "ffn1_dwu", comm=_scatter_comm(parts_wg1))
    parts_wu1 = _reduce_parts([d_wu1], _BIG_NAMES[1:2], c_arr, "ffn1_up")
    d_wd1, lands_wu1 = _tn(fb1, dob1, "ffn1_dwd", comm=_scatter_comm(parts_wu1))
    parts_wd1 = _reduce_parts([d_wd1], _BIG_NAMES[2:3], c_arr, "ffn1_down")
    lands_ffn1 = lands_wg1 + lands_wu1 + _comm_call(_scatter_comm(parts_wd1), "grad_chip_scatter_ffn1_down")

    halves = [_sum_slots(l, "sum_chips_" + n) for l, n in zip(lands_ffn1 + lands_mix + lands_ffn2, _BIG_NAMES)]
    small_shapes = [small[n].shape for n in _SMALL_NAMES[1:]]
    return loss_blk, gx, halves, d_g1, carried[2], small_shapes


_SMALL_NAMES = ("ffn1_norm", "mix_norm", "rg_conv_w", "rg_conv_b", "rg_gate_a_w", "rg_gate_a_b", "rg_gate_x_w",
                "rg_gate_x_b", "rg_lambda", "gdn_conv_w", "gdn_a_log", "gdn_dt_bias", "gdn_norm", "ffn2_norm", "final_norm")
_SMALL_SHARDED = dict(rg_conv_w=128, rg_gate_a_b=128, rg_gate_x_b=128, rg_lambda=128, gdn_conv_w=384)
_OUT_ORDER = ("ffn1_norm", "ffn1_w_gate", "ffn1_w_up", "ffn1_w_down", "mix_norm", "w_in", "w_out", "rg_conv_w", "rg_conv_b",
              "rg_gate_a_w", "rg_gate_a_b", "rg_gate_x_w", "rg_gate_x_b", "rg_lambda", "gdn_conv_w", "gdn_a_log",
              "gdn_dt_bias", "gdn_norm", "ffn2_norm", "ffn2_w_gate", "ffn2_w_up", "ffn2_w_down", "final_norm")
_BIG_NAMES = ("ffn1_w_gate", "ffn1_w_up", "ffn1_w_down", "w_in", "w_out", "ffn2_w_gate", "ffn2_w_up", "ffn2_w_down")
_TRANSPOSED = ("ffn1_w_gate", "ffn1_w_up", "ffn2_w_gate", "ffn2_w_up")


def kernel(x, ffn1_norm, ffn1_w_gate, ffn1_w_up, ffn1_w_down, mix_norm, w_in, w_out, rg_conv_w, rg_conv_b, rg_gate_a_w, rg_gate_a_b, rg_gate_x_w, rg_gate_x_b, rg_lambda, gdn_conv_w, gdn_a_log, gdn_dt_bias, gdn_norm, ffn2_norm, ffn2_w_gate, ffn2_w_up, ffn2_w_down, final_norm, loss_target, m_ffn1_norm, m_ffn1_w_gate, m_ffn1_w_up, m_ffn1_w_down, m_mix_norm, m_w_in, m_w_out, m_rg_conv_w, m_rg_conv_b, m_rg_gate_a_w, m_rg_gate_a_b, m_rg_gate_x_w, m_rg_gate_x_b, m_rg_lambda, m_gdn_conv_w, m_gdn_a_log, m_gdn_dt_bias, m_gdn_norm, m_ffn2_norm, m_ffn2_w_gate, m_ffn2_w_up, m_ffn2_w_down, m_final_norm, v_ffn1_norm, v_ffn1_w_gate, v_ffn1_w_up, v_ffn1_w_down, v_mix_norm, v_w_in, v_w_out, v_rg_conv_w, v_rg_conv_b, v_rg_gate_a_w, v_rg_gate_a_b, v_rg_gate_x_w, v_rg_gate_x_b, v_rg_lambda, v_gdn_conv_w, v_gdn_a_log, v_gdn_dt_bias, v_gdn_norm, v_ffn2_norm, v_ffn2_w_gate, v_ffn2_w_up, v_ffn2_w_down, v_final_norm):
    args = dict(locals())
    w = {n: args[n] for n in _OUT_ORDER}
    mom = {n: args["m_" + n] for n in _OUT_ORDER}
    var = {n: args["v_" + n] for n in _OUT_ORDER}
    xi, yi, ci = _mesh_pos()
    shard = 2 * xi + yi

    big_bf16 = [w[n][0].astype(BF16) for n in _BIG_NAMES]
    sm_local = _pad_rows(jnp.concatenate([w[n][0].reshape(-1) for n in _SMALL_SHARDED]), 128)
    first = _comm_call(_gather_comm(big_bf16[0:3] + [sm_local], pltpu.HBM, [t.shape[0] // 2 for t in big_bf16[0:3]] + [None]),
                       "gather_first_weights")
    ffn1_w = _all_shards(first[0:3])
    sm_all = first[3][0::2].reshape(NSH, -1)
    sm_full, off = {}, 0
    for n, wd_ in _SMALL_SHARDED.items():
        rows = w[n].shape[1]
        piece = sm_all[:, off:off + rows * wd_].reshape(NSH, rows, wd_)
        sm_full[n] = jnp.transpose(piece, (1, 0, 2)).reshape(rows, NSH * wd_)
        off += rows * wd_

    wa, wx = rg_gate_a_w[0], rg_gate_x_w[0]
    wgates = jnp.concatenate([_block_diag(wa[0]), _block_diag(wa[1]), _block_diag(wx[0]), _block_diag(wx[1])],
                             axis=1).astype(BF16)
    gbias = jnp.concatenate([sm_full["rg_gate_a_b"].reshape(1, -1), sm_full["rg_gate_x_b"].reshape(1, -1)], axis=1)
    sw = (ffn1_norm, mix_norm, jnp.pad(sm_full["rg_conv_w"], ((0, 4), (0, 0))), rg_conv_b, wgates, gbias,
          sm_full["rg_lambda"].reshape(1, -1), jnp.pad(sm_full["gdn_conv_w"], ((0, 4), (0, 0))), _lane_row(gdn_a_log),
          _lane_row(gdn_dt_bias), gdn_norm, ffn2_norm, final_norm.reshape(1, D))
    c_arr = ci.reshape(1).astype(jnp.int32)

    loss_blk, gx, halves, d_g1, small_packs, small_shapes = _local_step(x[0], loss_target[0], sw, ffn1_w, big_bf16[3:], c_arr)
    loss = lax.psum(loss_blk[0, 0], ("x", "y", "c"))
    grads = {}

    g1_all = _gather_small(jnp.pad(d_g1, ((0, 7), (0, 0))), "gather_ffn1_norm_grad")
    sm_sums = [_sum_slots(g1_all, "ffn1_norm_grad_sum")[0:1]] + _row_unpack(_sum_slots(small_packs, "small_grad_sum"), small_shapes)
    for n, g in zip(_SMALL_NAMES, sm_sums):
        if n in _SMALL_SHARDED:
            wd_ = _SMALL_SHARDED[n]
            g = lax.dynamic_slice_in_dim(g, shard * wd_, wd_, axis=1)
        grads[n] = g.reshape(w[n].shape)

    delta, new_m, new_v = {}, {}, {}
    for n, own, recv in zip(_BIG_NAMES, halves, _sibling_swap(halves)):
        to2d = jnp.transpose if n in _TRANSPOSED else (lambda t: t)
        outs4 = _adamw_halves(to2d(w[n][0]), own, recv, to2d(mom[n][0]), to2d(var[n][0]), c_arr, "adamw_" + n)
        grads[n], delta[n], new_m[n], new_v[n] = [to2d(o)[None] for o in outs4]
    packs = [_row_pack([t[n] for n in _SMALL_NAMES]) for t in (w, grads, mom, var)]
    sm_shapes = [w[n].shape for n in _SMALL_NAMES]
    for dst, src in zip((delta, new_m, new_v), _adamw(*packs, "adamw_small")):
        for n, val in zip(_SMALL_NAMES, _row_unpack(src, sm_shapes)):
            dst[n] = val

    outs = [loss, gx[None]]
    for group in (grads, delta, new_m, new_v):
        outs += [group[n] for n in _OUT_ORDER]
    return tuple(outs)
```

```python
import functools

import jax
import jax.numpy as jnp
from jax import lax
from jax.experimental import pallas as pl
from jax.experimental.pallas import tpu as pltpu

F32 = jnp.float32
BF16 = jnp.bfloat16
EPS = 1e-6
D = 1024
NSH = 4
FSH = 704
RGW = 512
QKVW = 1536
ZW = 512
BAW = 16
BAP = 128
INSH = 772
OUTSH = 256
CHUNK = 64
NH = 4
DH = 128
RG_C = 8.0
VMEM_LIMIT = 52 * 1024 * 1024
TN_VMEM_BUDGET = 40 * 1024 * 1024
ROW_TILE = 512
MESH = pl.DeviceIdType.MESH

ADAM_LR = 0.001
ADAM_B1 = 0.9
ADAM_B2 = 0.999
ADAM_EPS = 1e-08
ADAM_WD = 0.01
ADAM_STEP = 10


def _cparams(n_grid):
    return pltpu.CompilerParams(dimension_semantics=("arbitrary",) * n_grid, vmem_limit_bytes=VMEM_LIMIT)


def _sig(x):
    return 0.5 + 0.5 * jnp.tanh(0.5 * x)


def _sig_pos(x):
    return 1.0 / (1.0 + jnp.exp(-x))


def _softplus(x):
    return jnp.maximum(x, 0.0) + jnp.log(1.0 + jnp.exp(-jnp.abs(x)))


def _one_minus_sq_exp(la, a):
    y = 2.0 * la
    series = -y * (1.0 + y * (0.5 + y * (1.0 / 6 + y * (1.0 / 24 + y * (1.0 / 120 + y * (1.0 / 720))))))
    return jnp.where(y > -0.1, series, 1.0 - a * a)


_GELU_C = 0.7978845608028654


def _gelu(x):
    t = jnp.tanh(_GELU_C * (x + 0.044715 * x * x * x))
    return 0.5 * x * (1.0 + t)


def _gelu_grad(x):
    t = jnp.tanh(_GELU_C * (x + 0.044715 * x * x * x))
    return 0.5 * (1.0 + t) + 0.5 * x * (1.0 - t * t) * _GELU_C * (1.0 + 3 * 0.044715 * x * x)


def _silu_grad(x):
    s = _sig(x)
    return s * (1.0 + x * (1.0 - s))


def _dot(a, b):
    return jnp.dot(a.astype(BF16), b.astype(BF16), preferred_element_type=F32)


def _dot_nt(a, b):
    return lax.dot_general(a.astype(BF16), b.astype(BF16), (((1,), (1,)), ((), ())), preferred_element_type=F32)


def _dot_tn(a, b):
    return lax.dot_general(a.astype(BF16), b.astype(BF16), (((0,), (0,)), ((), ())), preferred_element_type=F32)


_NN = ((1,), (0,))
_NT = ((1,), (1,))
_TN = ((0,), (0,))


def _dg(a, b, dims):
    return lax.dot_general(a, b, (dims, ((), ())), preferred_element_type=F32)


def _split2(a):
    hi = a.astype(BF16)
    return hi, (a - hi.astype(F32)).astype(BF16)


def _dot3(a, b, dims=_NN):
    ah, al = _split2(a)
    bh, bl = _split2(b)
    return _dg(ah, bh, dims) + _dg(ah, bl, dims) + _dg(al, bh, dims)


def _dot_exact(e, x, dims, e_is_lhs):
    x0 = x.astype(BF16)
    r = x - x0.astype(F32)
    x1 = r.astype(BF16)
    x2 = (r - x1.astype(F32)).astype(BF16)
    eb = e.astype(BF16)
    if e_is_lhs:
        return _dg(eb, x0, dims) + _dg(eb, x1, dims) + _dg(eb, x2, dims)
    return _dg(x0, eb, dims) + _dg(x1, eb, dims) + _dg(x2, eb, dims)


def _rms(xv):
    r = lax.rsqrt(jnp.mean(xv * xv, axis=-1, keepdims=True) + EPS)
    return r, xv * r


def _rms_bwd(dy, xh, r, gain):
    dxh = dy * gain
    return r * (dxh - xh * jnp.mean(dxh * xh, axis=-1, keepdims=True))


def _colsum(v):
    return jnp.sum(v, axis=0, keepdims=True)


def _rows(t, c):
    return pl.BlockSpec((t, c), lambda i: (i, 0))


def _full(shape):
    n = len(shape)
    return pl.BlockSpec(shape, lambda i: (0,) * n)


def _sds(shape, dtype=F32):
    return jax.ShapeDtypeStruct(shape, dtype)


def _ffn_fwd(x, gain, wg, wu, wd, name, comm=None):
    s = x.shape[0]
    tm = min(256, s)

    def body(x_ref, g_ref, wg_ref, wu_ref, wd_ref, xo_ref, ga_ref, gb_ref, ft_ref):
        xv = x_ref[...]
        _, xh = _rms(xv)
        h = (xh * g_ref[...]).astype(BF16)
        acc = None
        for j in range(NSH):
            a = jnp.dot(h, wg_ref[j], preferred_element_type=F32)
            b = jnp.dot(h, wu_ref[j], preferred_element_type=F32)
            sa = _sig(a)
            silu = a * sa
            fv = silu * b
            f = fv.astype(BF16)
            ft_ref[j] = f.T
            ga_ref[j] = (sa * b + fv * (1.0 - sa)).astype(BF16)
            gb_ref[j] = silu.astype(BF16)
            part = jnp.dot(f, wd_ref[j], preferred_element_type=F32)
            acc = part if acc is None else acc + part
        xo_ref[...] = xv + 0.5 * acc

    hidden = pl.BlockSpec((NSH, tm, FSH), lambda i: (0, i, 0))
    return _pallas(
        body, comm, name=name, grid=(s // tm,),
        in_specs=[_rows(tm, D), _full((1, D)),
                  pl.BlockSpec((NSH, D, FSH), lambda i: (0, 0, 0), pipeline_mode=pl.Buffered(1)),
                  pl.BlockSpec((NSH, D, FSH), lambda i: (0, 0, 0), pipeline_mode=pl.Buffered(1)),
                  pl.BlockSpec((NSH, FSH, D), lambda i: (0, 0, 0), pipeline_mode=pl.Buffered(1))],
        out_specs=[_rows(tm, D), hidden, hidden, pl.BlockSpec((NSH, FSH, tm), lambda i: (0, 0, i))],
        out_shape=[_sds((s, D))] + [_sds((NSH, s, FSH), BF16)] * 2 + [_sds((NSH, FSH, s), BF16)],
        scratch_shapes=[], args=(x, gain, wg, wu, wd))


def _ffn_bwd(x, dout, do, gain, ga, gb, wg, wu, wd, name, comm=None):
    s = x.shape[0]
    tm = min(512, s)

    def hidden(do_ref, ga_ref, gb_ref, wd_ref, da_ref, db_ref):
        dov = do_ref[...]
        for j in range(NSH):
            df = _dot_nt(dov, wd_ref[j])
            da_ref[j] = (df * ga_ref[j].astype(F32)).astype(BF16)
            db_ref[j] = (df * gb_ref[j].astype(F32)).astype(BF16)

    sh = pl.BlockSpec((NSH, tm, FSH), lambda i: (0, i, 0))
    (da, db), carried = _pallas(
        hidden, comm, name=name + "_hidden", grid=(s // tm,),
        in_specs=[_rows(tm, D), sh, sh, pl.BlockSpec((NSH, FSH, D), lambda i: (0, 0, 0), pipeline_mode=pl.Buffered(1))],
        out_specs=[sh, sh], out_shape=[_sds((NSH, s, FSH), BF16)] * 2, scratch_shapes=[], args=(do, ga, gb, wd))

    def inputs(x_ref, d_ref, g_ref, da_ref, db_ref, wg_ref, wu_ref, dx_ref, dg_ref, h_ref):
        @pl.when(pl.program_id(0) == 0)
        def _():
            dg_ref[...] = jnp.zeros_like(dg_ref)

        dh = jnp.zeros((tm, D), F32)
        for j in range(NSH):
            dh = dh + _dot_nt(da_ref[j], wg_ref[j]) + _dot_nt(db_ref[j], wu_ref[j])
        r, xh = _rms(x_ref[...])
        gv = g_ref[...]
        h_ref[...] = (xh * gv).astype(BF16)
        dg_ref[...] += _colsum(dh * xh)
        dx_ref[...] = d_ref[...] + _rms_bwd(dh, xh, r, gv)

    grads = pl.BlockSpec((NSH, tm, FSH), lambda i: (0, i, 0))
    resident = pl.BlockSpec((NSH, D, FSH), lambda i: (0, 0, 0), pipeline_mode=pl.Buffered(1))
    dx, dg, h = pl.pallas_call(
        inputs, name=name + "_input", grid=(s // tm,),
        in_specs=[_rows(tm, D), _rows(tm, D), _full((1, D)), grads, grads, resident, resident],
        out_specs=[_rows(tm, D), _full((1, D)), _rows(tm, D)],
        out_shape=[_sds((s, D)), _sds((1, D)), _sds((s, D), BF16)], compiler_params=_cparams(1),
    )(x, dout, gain, da, db, wg, wu)
    return dx, dg, h, da, db, carried


def _tn(a, b, name, comm=None, a_transposed=False):
    a_g = a.ndim == 3
    b_g = b.ndim == 3
    g = a.shape[0] if a_g else (b.shape[0] if b_g else 1)
    k, s = a.shape[-2:] if a_transposed else a.shape[-2:][::-1]
    n = b.shape[-1]
    ts = min(4096, s)
    while ts > 256 and 2 * ts * (k + n) * max(a.dtype.itemsize, b.dtype.itemsize) + 2 * k * n * 4 > TN_VMEM_BUDGET:
        ts //= 2

    def body(a_ref, b_ref, o_ref):
        @pl.when(pl.program_id(1) == 0)
        def _():
            o_ref[...] = jnp.zeros_like(o_ref)

        av = a_ref[0] if a_g else a_ref[...]
        bv = b_ref[0] if b_g else b_ref[...]
        o_ref[0] += _dot(av, bv) if a_transposed else _dot_tn(av, bv)

    if a_transposed:
        a_spec = pl.BlockSpec((1, k, ts), lambda gi, si: (gi, 0, si))
    else:
        a_spec = pl.BlockSpec((1, ts, k), lambda gi, si: (gi, si, 0)) if a_g else pl.BlockSpec((ts, k), lambda gi, si: (si, 0))
    b_spec = pl.BlockSpec((1, ts, n), lambda gi, si: (gi, si, 0)) if b_g else pl.BlockSpec((ts, n), lambda gi, si: (si, 0))
    outs, carried = _pallas(body, comm, name=name, grid=(g, s // ts), in_specs=[a_spec, b_spec],
                            out_specs=[pl.BlockSpec((1, k, n), lambda gi, si: (gi, 0, 0))], out_shape=[_sds((g, k, n))],
                            scratch_shapes=[], args=(a, b))
    return outs[0] if comm is None else (outs[0], carried)


_P_WIDTHS = (RGW, RGW, QKVW, ZW, BAP)


def _inproj(x1, gain, ws, name):
    s = x1.shape[0]
    tm = min(ROW_TILE, s)

    def body(x_ref, g_ref, *refs):
        w_refs = refs[:5]
        h_ref = refs[5]
        p_refs = refs[6:]
        _, xh = _rms(x_ref[...])
        h = (xh * g_ref[...]).astype(BF16)
        h_ref[...] = h
        for w_ref, p_ref in zip(w_refs, p_refs):
            p_ref[...] = jnp.dot(h, w_ref[...], preferred_element_type=F32)

    return pl.pallas_call(
        body, name=name, grid=(s // tm,),
        in_specs=[_rows(tm, D), _full((1, D))] + [_full((D, w)) for w in _P_WIDTHS],
        out_specs=[_rows(tm, D)] + [_rows(tm, w) for w in _P_WIDTHS],
        out_shape=[_sds((s, D), BF16)] + [_sds((s, w)) for w in _P_WIDTHS],
        compiler_params=_cparams(1),
    )(x1, gain, *ws)


def _inproj_bwd(x1, dx2, gain, dps, ws, name):
    s = x1.shape[0]
    tm = min(ROW_TILE, s)

    def body(x_ref, d_ref, g_ref, *refs):
        dp_refs = refs[:5]
        w_refs = refs[5:10]
        dx_ref, dxh_ref, dg_ref = refs[10:]

        @pl.when(pl.program_id(0) == 0)
        def _():
            dg_ref[...] = jnp.zeros_like(dg_ref)

        dh = jnp.zeros((tm, D), F32)
        for dp_ref, w_ref in zip(dp_refs, w_refs):
            dh = dh + _dot_nt(dp_ref[...], w_ref[...])
        r, xh = _rms(x_ref[...])
        dg_ref[...] += _colsum(dh * xh)
        dx = d_ref[...] + _rms_bwd(dh, xh, r, g_ref[...])
        dx_ref[...] = dx
        dxh_ref[...] = (0.5 * dx).astype(BF16)

    return pl.pallas_call(
        body, name=name, grid=(s // tm,),
        in_specs=[_rows(tm, D), _rows(tm, D), _full((1, D))] + [_rows(tm, w) for w in _P_WIDTHS]
        + [_full((D, w)) for w in _P_WIDTHS],
        out_specs=[_rows(tm, D), _rows(tm, D), _full((1, D))],
        out_shape=[_sds((s, D)), _sds((s, D), BF16), _sds((1, D))],
        compiler_params=_cparams(1),
    )(x1, dx2, gain, *dps, *ws)


def _halo_specs(s, t, c):
    nb8 = s // 8
    tb = t // 8
    prev = pl.BlockSpec((8, c), lambda i: (jnp.maximum(i * tb - 1, 0), 0))
    nxt = pl.BlockSpec((8, c), lambda i: (jnp.minimum((i + 1) * tb, nb8 - 1), 0))
    return prev, nxt


def _edge_masks(nb):
    i = pl.program_id(0)
    return jnp.where(i > 0, 1.0, 0.0).astype(F32), jnp.where(i < nb - 1, 1.0, 0.0).astype(F32)


def _shifted(xx, off, t):
    n = t + 16
    sh = (-off) % n
    rolled = xx if sh == 0 else pltpu.roll(xx, sh, 0)
    return rolled[8:8 + t]


def _conv(x, w8, bias, name):
    s, c = x.shape
    t = min(ROW_TILE, s)
    nb = s // t

    def body(x_ref, xp_ref, xn_ref, w_ref, b_ref, o_ref):
        pm, nm = _edge_masks(nb)
        for c0 in range(0, c, 512):
            cols = slice(c0, c0 + 512)
            xx = jnp.concatenate([xp_ref[:, cols] * pm, x_ref[:, cols], xn_ref[:, cols] * nm], axis=0)
            acc = jnp.zeros((t, 512), F32) + b_ref[:, cols]
            for j in range(4):
                acc = acc + w_ref[j:j + 1, cols] * _shifted(xx, j - 2, t)
            o_ref[:, cols] = acc

    prev, nxt = _halo_specs(s, t, c)
    return pl.pallas_call(
        body, name=name, grid=(nb,),
        in_specs=[_rows(t, c), prev, nxt, _full((8, c)), _full((1, c))],
        out_specs=_rows(t, c), out_shape=_sds((s, c)), compiler_params=_cparams(1),
    )(x, x, x, w8, bias)


def _conv_bwd(x, dc, w8, name):
    s, c = x.shape
    t = min(ROW_TILE, s)
    nb = s // t

    def body(x_ref, d_ref, dp_ref, dn_ref, w_ref, dx_ref, dw_ref, db_ref):
        @pl.when(pl.program_id(0) == 0)
        def _():
            dw_ref[...] = jnp.zeros_like(dw_ref)
            db_ref[...] = jnp.zeros_like(db_ref)

        pm, nm = _edge_masks(nb)
        for c0 in range(0, c, 512):
            cols = slice(c0, c0 + 512)
            dd = jnp.concatenate([dp_ref[:, cols] * pm, d_ref[:, cols], dn_ref[:, cols] * nm], axis=0)
            xv = x_ref[:, cols]
            acc = jnp.zeros((t, 512), F32)
            for j in range(4):
                dsh = _shifted(dd, 2 - j, t)
                acc = acc + w_ref[j:j + 1, cols] * dsh
                dw_ref[j:j + 1, cols] += _colsum(dsh * xv)
            dx_ref[:, cols] = acc.astype(BF16)
            db_ref[:, cols] += _colsum(d_ref[:, cols])

    prev, nxt = _halo_specs(s, t, c)
    return pl.pallas_call(
        body, name=name, grid=(nb,),
        in_specs=[_rows(t, c), _rows(t, c), prev, nxt, _full((8, c))],
        out_specs=[_rows(t, c), _full((8, c)), _full((1, c))],
        out_shape=[_sds((s, c), BF16), _sds((8, c)), _sds((1, c))], compiler_params=_cparams(1),
    )(x, dc, dc, dc, w8)


def _rg_gates(xc, pre, lam_row):
    sp8 = RG_C * _softplus(-lam_row)
    out = []
    for d in range(2):
        r = _sig_pos(pre[:, RGW * d:RGW * (d + 1)])
        gi = _sig(pre[:, 2 * RGW + RGW * d:2 * RGW + RGW * (d + 1)])
        la = -r * sp8[:, RGW * d:RGW * (d + 1)]
        a = jnp.exp(la)
        mult = jnp.sqrt(_one_minus_sq_exp(la, a))
        out.append((r, gi, a, mult))
    return out


def _mix_prep(c_rg, c_qkv, p_ba, wgates, gbias, lam_row, alog_row, dtb_row, name):
    s = c_rg.shape[0]
    t = min(ROW_TILE, s)

    def body(xc_ref, cq_ref, pc_ref, wg_ref, gb_ref, lam_ref, alog_ref, dtb_ref,
             a0_ref, b0_ref, a1_ref, b1_ref, q_ref, k_ref, v_ref, bg_ref):
        xc = xc_ref[...]
        pre = _dot(xc, wg_ref[...]) + gb_ref[...]
        gates = _rg_gates(xc, pre, lam_ref[...])
        for (r, gi, a, mult), a_ref, b_ref in zip(gates, (a0_ref, a1_ref), (b0_ref, b1_ref)):
            a_ref[...] = a
            b_ref[...] = mult * gi * xc
        cq = cq_ref[...]
        sq = cq * _sig(cq)
        for h in range(NH):
            sl = slice(DH * h, DH * (h + 1))
            qh = sq[:, sl]
            q_ref[:, sl] = qh * lax.rsqrt(jnp.sum(qh * qh, axis=-1, keepdims=True) + EPS) * (DH ** -0.5)
            kh = sq[:, RGW + DH * h:RGW + DH * (h + 1)]
            k_ref[:, sl] = kh * lax.rsqrt(jnp.sum(kh * kh, axis=-1, keepdims=True) + EPS)
        v_ref[...] = sq[:, 2 * RGW:]
        pc = pc_ref[...]
        lane = lax.broadcasted_iota(jnp.int32, pc.shape, 1)
        beta = _sig(pc)
        g = -jnp.exp(alog_ref[...]) * _softplus(pc + dtb_ref[...])
        bg_ref[...] = jnp.where(lane < 8, beta, jnp.where(lane < 16, g, 0.0))

    return pl.pallas_call(
        body, name=name, grid=(s // t,),
        in_specs=[_rows(t, RGW), _rows(t, QKVW), _rows(t, BAP), _full((RGW, 4 * RGW)), _full((1, 4 * RGW)),
                  _full((1, 2 * RGW)), _full((1, BAP)), _full((1, BAP))],
        out_specs=[_rows(t, RGW)] * 7 + [_rows(t, BAP)],
        out_shape=[_sds((s, RGW))] * 7 + [_sds((s, BAP))],
        compiler_params=_cparams(1),
    )(c_rg, c_qkv, p_ba, wgates, gbias, lam_row, alog_row, dtb_row)


def _block_scan(av, bv, row, downwards):
    for k in (1, 2, 4):
        sh = (8 - k) if downwards else k
        m = (row < 8 - k) if downwards else (row >= k)
        a_s = pltpu.roll(av, sh, 0)
        b_s = pltpu.roll(bv, sh, 0)
        bv = jnp.where(m, av * b_s + bv, bv)
        av = jnp.where(m, av * a_s, av)
    return av, bv


def _gates_bwd(xc, wgates, gbias, lam_row, lam0, lam1, hf, hb, name):
    s = xc.shape[0]
    t = min(ROW_TILE, s)
    nb = s // t

    def body(xc_ref, wg_ref, gb_ref, lam_ref, l0_ref, l1_ref, hf_ref, hfp_ref, hfn_ref, hb_ref, hbp_ref, hbn_ref,
             dxc_ref, dpre_ref, xcb_ref, dgb_ref, dlam_ref):
        @pl.when(pl.program_id(0) == 0)
        def _():
            dgb_ref[...] = jnp.zeros_like(dgb_ref)
            dlam_ref[...] = jnp.zeros_like(dlam_ref)

        pm, nm = _edge_masks(nb)
        h_prev = _shifted(jnp.concatenate([hfp_ref[...] * pm, hf_ref[...], hfn_ref[...] * nm], axis=0), -1, t)
        h_next = _shifted(jnp.concatenate([hbp_ref[...] * pm, hb_ref[...], hbn_ref[...] * nm], axis=0), 1, t)
        h_shift = (h_prev, h_next)
        xv = xc_ref[...]
        pre = _dot(xv, wg_ref[...]) + gb_ref[...]
        lam_row_v = lam_ref[...]
        sp8 = RG_C * _softplus(-lam_row_v)
        dsp_dlam = -RG_C * _sig(-lam_row_v)
        gates = _rg_gates(xv, pre, lam_row_v)
        dxc = jnp.zeros((t, RGW), F32)
        dpre_r = []
        dpre_i = []
        for d, ((r, gi, a, mult), l_ref, hs) in enumerate(zip(gates, (l0_ref, l1_ref), h_shift)):
            dbb = l_ref[...]
            da = dbb * hs
            cs = slice(RGW * d, RGW * (d + 1))
            dmult = dbb * gi * xv
            dgi = dbb * mult * xv
            dxc = dxc + dbb * mult * gi
            dla = da * a - dmult * a * a / mult
            dr = -dla * sp8[:, cs]
            dlam_ref[:, cs] += _colsum(-dla * r) * dsp_dlam[:, cs]
            dpre_r.append(dr * r * (1.0 - r))
            dpre_i.append(dgi * gi * (1.0 - gi))
        dpre = jnp.concatenate(dpre_r + dpre_i, axis=1)
        dgb_ref[...] += _colsum(dpre)
        dpre_b = dpre.astype(BF16)
        dpre_ref[...] = dpre_b
        xcb_ref[...] = xv.astype(BF16)
        dxc_ref[...] = dxc + _dot_nt(dpre_b, wg_ref[...])

    prev, nxt = _halo_specs(s, t, RGW)
    return pl.pallas_call(
        body, name=name, grid=(s // t,),
        in_specs=[_rows(t, RGW), _full((RGW, 4 * RGW)), _full((1, 4 * RGW)), _full((1, 2 * RGW))] + [_rows(t, RGW)] * 2
        + [_rows(t, RGW), prev, nxt] * 2,
        out_specs=[_rows(t, RGW), _rows(t, 4 * RGW), _rows(t, RGW), _full((1, 4 * RGW)), _full((1, 2 * RGW))],
        out_shape=[_sds((s, RGW)), _sds((s, 4 * RGW), BF16), _sds((s, RGW), BF16), _sds((1, 4 * RGW)), _sds((1, 2 * RGW))],
        compiler_params=_cparams(1),
    )(xc, wgates, gbias, lam_row, lam0, lam1, hf, hf, hf, hb, hb, hb)


class _GdnMasks:
    def __init__(self, d):
        ri = lax.broadcasted_iota(jnp.int32, (CHUNK, CHUNK), 0)
        ci = lax.broadcasted_iota(jnp.int32, (CHUNK, CHUNK), 1)
        self.incl = (ri >= ci) if d == 0 else (ri <= ci)
        self.strict = (ri > ci) if d == 0 else (ri < ci)
        b16 = jnp.right_shift(ri, 4) == jnp.right_shift(ci, 4)
        b32 = jnp.right_shift(ri, 5) == jnp.right_shift(ci, 5)
        self.diag16 = b16
        self.off32 = jnp.logical_and(b32, jnp.logical_not(b16))
        self.off64 = jnp.logical_not(b32)
        self.eye = jnp.where(ri == ci, 1.0, 0.0).astype(F32)
        self.tri = jnp.where(self.incl, 1.0, 0.0).astype(F32)
        self.last = CHUNK - 1 if d == 0 else 0


def _tri_inv(lmat, m):
    return _tri_inv_many([lmat], [m])[0]


def _tri_inv_many(lmats, masks):
    n = len(lmats)
    ns = [jnp.where(masks[i].diag16, lmats[i], 0.0) for i in range(n)]
    ps = [masks[i].eye - ns[i] for i in range(n)]
    qs = [_dot3(ns[i], ns[i]) for i in range(n)]
    for step in range(3):
        ps = [_dot3(ps[i], masks[i].eye + qs[i]) for i in range(n)]
        if step < 2:
            qs = [_dot3(qs[i], qs[i]) for i in range(n)]
    for off in ("off32", "off64"):
        ts = [_dot3(ps[i], jnp.where(getattr(masks[i], off), lmats[i], 0.0)) for i in range(n)]
        ps = [ps[i] - _dot3(ts[i], ps[i]) for i in range(n)]
    return ps


def _chunk_cumsums(m, bgv):
    return _dot_exact(m.tri, bgv, _NN, True), _dot_exact(m.tri, bgv, ((0,), (1,)), False)


class _GdnHead:
    def __init__(self, qh, kh, vh, kk, q0, bg, gcs, gcs_t, d, h, m):
        cb = 4 * d + h
        cg = 8 + 4 * d + h
        self.q, self.k, self.v = qh, kh, vh
        self.beta = bg[:, cb:cb + 1]
        gcol = gcs[:, cg:cg + 1]
        grow = gcs_t[cg:cg + 1, :]
        gl = gcs[m.last:m.last + 1, cg:cg + 1]
        self.decay = jnp.exp(jnp.where(m.incl, gcol - grow, -1e30))
        self.kb = kh * self.beta
        self.vb = vh * self.beta
        self.a0 = kk * self.beta
        self.q0 = q0
        self.lmat = jnp.where(m.strict, self.a0 * self.decay, 0.0)
        self.attn = self.q0 * self.decay
        self.eg = jnp.exp(gcol)
        self.ek = jnp.exp(gl - gcol)
        self.cd = jnp.exp(gl)
        self.kg = self.kb * self.eg
        self.qd = qh * self.eg
        self.kd = kh * self.ek


HW = NH * DH
SEQ_CB = 4
LOCAL_CB = 4


def _head(h):
    return slice(DH * h, DH * (h + 1))


def _gdn_local_fwd(q, k, v, bg, name):
    s = q.shape[0]
    n = s // CHUNK
    cb = min(LOCAL_CB, n)

    def body(q_ref, k_ref, v_ref, bg_ref, t_ref, u_ref, w_ref, qd_ref, kd_ref, at_ref, cd_ref):
        masks = [_GdnMasks(d) for d in range(2)]
        inst = []
        for jj in range(cb):
            rows = slice(CHUNK * jj, CHUNK * (jj + 1))
            bgv = bg_ref[rows, :]
            qs = [q_ref[rows, _head(h)] for h in range(NH)]
            ks = [k_ref[rows, _head(h)] for h in range(NH)]
            kk = [_dot_nt(ks[h], ks[h]) for h in range(NH)]
            q0 = [_dot_nt(qs[h], ks[h]) for h in range(NH)]
            for d, m in enumerate(masks):
                gcs, gcs_t = _chunk_cumsums(m, bgv)
                for h in range(NH):
                    c = _GdnHead(qs[h], ks[h], v_ref[rows, _head(h)], kk[h], q0[h], bgv, gcs, gcs_t, d, h, m)
                    inst.append((jj, rows, d, h, m, c))
        tms = _tri_inv_many([it[-1].lmat for it in inst], [it[-2] for it in inst])
        for (jj, rows, d, h, m, c), tm in zip(inst, tms):
            sl = _head(h)
            t_ref[jj, d, h] = tm
            u_ref[d, rows, sl] = _dot(tm, c.vb).astype(BF16)
            w_ref[d, rows, sl] = _dot(tm, c.kg).astype(BF16)
            qd_ref[d, rows, sl] = c.qd.astype(BF16)
            kd_ref[d, rows, sl] = c.kd.astype(BF16)
            at_ref[jj, d, h] = c.attn.astype(BF16)
            cd_ref[jj, 4 * d + h:4 * d + h + 1, :] = jnp.broadcast_to(c.cd, (1, DH))

    tok = _rows(cb * CHUNK, HW)
    tok2 = pl.BlockSpec((2, cb * CHUNK, HW), lambda i: (0, i, 0))
    mat = pl.BlockSpec((cb, 2, NH, CHUNK, CHUNK), lambda i: (i, 0, 0, 0, 0))
    return pl.pallas_call(
        body, name=name, grid=(n // cb,), in_specs=[tok, tok, tok, _rows(cb * CHUNK, BAP)],
        out_specs=[mat, tok2, tok2, tok2, tok2, mat, pl.BlockSpec((cb, 8, DH), lambda i: (i, 0, 0))],
        out_shape=[_sds((n, 2, NH, CHUNK, CHUNK)), _sds((2, s, HW), BF16), _sds((2, s, HW), BF16), _sds((2, s, HW), BF16),
                   _sds((2, s, HW), BF16), _sds((n, 2, NH, CHUNK, CHUNK), BF16), _sds((n, 8, DH))],
        compiler_params=_cparams(1),
    )(q, k, v, bg)


def _seq_specs(s, order):
    n = s // CHUNK
    cb = min(SEQ_CB, n)
    nb = n // cb
    tb = cb * CHUNK

    def blk(d):
        return (lambda i: i) if order[d] else (lambda i: nb - 1 - i)

    def per_dir(make):
        return [make(d, blk(d)) for d in range(2)]

    tok2 = per_dir(lambda d, f: pl.BlockSpec((1, tb, HW), lambda i: (d, f(i), 0)))
    tok = per_dir(lambda d, f: pl.BlockSpec((tb, HW), lambda i: (f(i), 0)))
    mat = per_dir(lambda d, f: pl.BlockSpec((cb, 1, NH, CHUNK, CHUNK), lambda i: (f(i), d, 0, 0, 0)))
    cds = per_dir(lambda d, f: pl.BlockSpec((cb, 8, DH), lambda i: (f(i), 0, 0)))
    sts = per_dir(lambda d, f: pl.BlockSpec((cb, NH, DH, DH), lambda i: (f(i), 0, 0, 0)))
    dcd = per_dir(lambda d, f: pl.BlockSpec((cb, NH, DH), lambda i: (f(i), 0, 0)))
    return n, cb, nb, tok2, tok, mat, cds, sts, dcd


class _ScanRider:
    def __init__(self, af, bf, ar, br, shifted, tb, nb, up_spec, down_spec):
        s, c = af.shape
        self.shifted, self.t, self.c, self.nb = shifted, tb, c, nb
        self.args = [af, bf, ar, br]
        self.in_specs = [up_spec, up_spec, down_spec, down_spec]
        self.scratch = [pltpu.VMEM((16, c), F32)]
        if shifted:
            tb8 = tb // 8
            self.args += [af, ar]
            self.in_specs += [pl.BlockSpec((8, c), lambda i: (jnp.maximum(i * tb8 - 1, 0), 0)),
                              pl.BlockSpec((8, c), lambda i: (jnp.minimum((nb - i) * tb8, s // 8 - 1), 0))]
            self.scratch += [pltpu.VMEM((tb + 8, c), F32), pltpu.VMEM((tb + 8, c), F32)]
        self.out_specs = [up_spec, down_spec]
        self.out_shape = [_sds((s, c)), _sds((s, c))]

    def begin(self, in_refs, out_refs, scratch_refs):
        i = pl.program_id(0)
        self.carry = scratch_refs[0]

        @pl.when(i == 0)
        def _():
            self.carry[...] = jnp.zeros_like(self.carry)

        af_ref, self.bf_ref, ar_ref, self.br_ref = in_refs[0:4]
        self.hf_ref, self.hr_ref = out_refs
        self.a_up, self.a_dn = af_ref, ar_ref
        if self.shifted:
            t = self.t
            edge = jnp.where(i > 0, 1.0, 0.0).astype(F32)
            fbuf, rbuf = scratch_refs[1:3]
            fbuf[0:8, :] = in_refs[4][...] * edge
            fbuf[8:t + 8, :] = af_ref[...]
            rbuf[0:t, :] = ar_ref[...]
            rbuf[t:t + 8, :] = in_refs[5][...] * edge
            self.a_up, self.a_dn = fbuf, rbuf
        self.row = lax.broadcasted_iota(jnp.int32, (8, self.c), 0)
        self.cf, self.cr = self.carry[0:1, :], self.carry[8:9, :]

    def groups(self, lo, hi):
        ng = self.t // 8
        row = self.row
        for gi in range(lo, hi):
            rf, rr = 8 * gi, 8 * (ng - 1 - gi)
            if self.shifted:
                a_f = jnp.where(row > 0, pltpu.roll(self.a_up[rf + 8:rf + 16, :], 1, 0), pltpu.roll(self.a_up[rf:rf + 8, :], 1, 0))
                a_r = jnp.where(row < 7, pltpu.roll(self.a_dn[rr:rr + 8, :], 7, 0), pltpu.roll(self.a_dn[rr + 8:rr + 16, :], 7, 0))
            else:
                a_f, a_r = self.a_up[rf:rf + 8, :], self.a_dn[rr:rr + 8, :]
            a_f, b_f = _block_scan(a_f, self.bf_ref[rf:rf + 8, :], row, False)
            a_r, b_r = _block_scan(a_r, self.br_ref[rr:rr + 8, :], row, True)
            h_f = a_f * self.cf + b_f
            h_r = a_r * self.cr + b_r
            self.hf_ref[rf:rf + 8, :] = h_f
            self.hr_ref[rr:rr + 8, :] = h_r
            self.cf, self.cr = h_f[7:8, :], h_r[0:1, :]

    def end(self):
        self.carry[0:1, :] = self.cf
        self.carry[8:9, :] = self.cr


def _gdn_seq_fwd(u, w, qd, kd, at, cd, name, scan=None):
    s = u.shape[1]
    n, cb, nb, tok2, tok, mat, cds, sts, _ = _seq_specs(s, (True, False))
    rider = _ScanRider(*scan, False, cb * CHUNK, nb, tok[0], tok[1]) if scan else None
    ri = len(rider.args) if rider else 0

    def body(*refs):
        ins = (refs[0:6], refs[6:12])
        outs = (refs[12 + ri:15 + ri], refs[15 + ri:18 + ri])
        st = refs[18 + ri + (2 if rider else 0)]
        if rider:
            rider.begin(refs[12:12 + ri], refs[18 + ri:20 + ri], refs[21 + ri:])

        @pl.when(pl.program_id(0) == 0)
        def _():
            st[...] = jnp.zeros_like(st)

        for j in range(cb):
            items = []
            for d in range(2):
                jj = j if d == 0 else cb - 1 - j
                items += [(d, h, jj, slice(CHUNK * jj, CHUNK * (jj + 1)), _head(h)) for h in range(NH)]
            shs = [st[d, h] for d, h, _, _, _ in items]
            wss = [_dot(ins[d][1][0, rows, sl], sh) for (d, h, jj, rows, sl), sh in zip(items, shs)]
            vns = [ins[d][0][0, rows, sl].astype(F32) - ws for (d, h, jj, rows, sl), ws in zip(items, wss)]
            news = [sh * ins[d][5][jj, 4 * d + h:4 * d + h + 1, :] + _dot_tn(ins[d][3][0, rows, sl], vn)
                    for (d, h, jj, rows, sl), sh, vn in zip(items, shs, vns)]
            for (d, h, jj, rows, sl), sh, vn, new in zip(items, shs, vns, news):
                o_r, s_r, vn_r = outs[d]
                st[d, h] = new
                s_r[jj, h] = sh.astype(BF16)
                vn_r[rows, sl] = vn.astype(BF16)
                o_r[rows, sl] = _dot(ins[d][2][0, rows, sl], sh) + _dot(ins[d][4][jj, 0, h], vn)
            if rider:
                rider.groups(8 * j, 8 * (j + 1))
        if rider:
            rider.end()

    in_specs, out_specs, out_shape = [], [], []
    for d in range(2):
        in_specs += [tok2[d]] * 4 + [mat[d], cds[d]]
        out_specs += [tok[d], sts[d], tok[d]]
        out_shape += [_sds((s, HW)), _sds((n, NH, DH, DH), BF16), _sds((s, HW), BF16)]
    args = [u, w, qd, kd, at, cd, u, w, qd, kd, at, cd]
    scratch = [pltpu.VMEM((2, NH, DH, DH), F32)]
    if rider:
        in_specs, args = in_specs + rider.in_specs, args + rider.args
        out_specs, out_shape, scratch = out_specs + rider.out_specs, out_shape + rider.out_shape, scratch + rider.scratch
    return pl.pallas_call(
        body, name=name, grid=(nb,), in_specs=in_specs, out_specs=out_specs, out_shape=out_shape,
        scratch_shapes=scratch, compiler_params=_cparams(1),
    )(*args)


def _gdn_seq_bwd(do, w, qd, kd, at, cd, states, vns, name, scan=None):
    s = do.shape[0]
    n, cb, nb, tok2, tok, mat, cds, sts, dcd = _seq_specs(s, (False, True))
    rider = _ScanRider(*scan, True, cb * CHUNK, nb, tok[1], tok[0]) if scan else None
    ri = len(rider.args) if rider else 0

    def body(*refs):
        ins = (refs[0:8], refs[8:16])
        outs = (refs[16 + ri:21 + ri], refs[21 + ri:26 + ri])
        dst = refs[26 + ri + (2 if rider else 0)]
        if rider:
            rider.begin(refs[16:16 + ri], refs[26 + ri:28 + ri], refs[29 + ri:])

        @pl.when(pl.program_id(0) == 0)
        def _():
            dst[...] = jnp.zeros_like(dst)

        for j in range(cb):
            items = []
            for d in range(2):
                jj = cb - 1 - j if d == 0 else j
                items += [(d, h, jj, slice(CHUNK * jj, CHUNK * (jj + 1)), _head(h)) for h in range(NH)]
            dsns = [dst[d, h] for d, h, _, _, _ in items]
            dohs = [ins[d][0][rows, sl] for d, h, jj, rows, sl in items]
            d_vns = [_dot_tn(ins[d][4][jj, 0, h], doh) + _dot(ins[d][3][0, rows, sl], dsn)
                     for (d, h, jj, rows, sl), doh, dsn in zip(items, dohs, dsns)]
            news = [ins[d][5][jj, 4 * d + h:4 * d + h + 1, :] * dsn + _dot_tn(ins[d][2][0, rows, sl], doh)
                    - _dot_tn(ins[d][1][0, rows, sl], d_vn)
                    for (d, h, jj, rows, sl), doh, dsn, d_vn in zip(items, dohs, dsns, d_vns)]
            for (d, h, jj, rows, sl), doh, dsn, d_vn, new in zip(items, dohs, dsns, d_vns, news):
                dvn_r, dkd_r, dqd_r, dw_r, dcd_r = outs[d]
                sh = ins[d][6][jj, h].astype(F32)
                dst[d, h] = new
                dvn_r[rows, sl] = d_vn.astype(BF16)
                dkd_r[rows, sl] = _dot_nt(ins[d][7][rows, sl], dsn)
                dqd_r[rows, sl] = _dot_nt(doh, sh)
                dw_r[rows, sl] = (-_dot_nt(d_vn, sh)).astype(BF16)
                d_cd = jnp.sum(jnp.sum(sh * dsn, axis=1, keepdims=True), axis=0, keepdims=True)
                dcd_r[jj, h:h + 1, :] = jnp.broadcast_to(d_cd, (1, DH))
            if rider:
                rider.groups(8 * j, 8 * (j + 1))
        if rider:
            rider.end()

    in_specs, out_specs, out_shape, args = [], [], [], []
    for d in range(2):
        in_specs += [tok[d]] + [tok2[d]] * 3 + [mat[d], cds[d], sts[d], tok[d]]
        args += [do, w, qd, kd, at, cd, states[d], vns[d]]
        out_specs += [tok[d]] * 4 + [dcd[d]]
        out_shape += [_sds((s, HW), BF16), _sds((s, HW)), _sds((s, HW)), _sds((s, HW), BF16), _sds((n, NH, DH))]
    scratch = [pltpu.VMEM((2, NH, DH, DH), F32)]
    if rider:
        in_specs, args = in_specs + rider.in_specs, args + rider.args
        out_specs, out_shape, scratch = out_specs + rider.out_specs, out_shape + rider.out_shape, scratch + rider.scratch
    return pl.pallas_call(
        body, name=name, grid=(nb,), in_specs=in_specs, out_specs=out_specs, out_shape=out_shape,
        scratch_shapes=scratch, compiler_params=_cparams(1),
    )(*args)


def _gdn_local_bwd(q, k, v, bg, tmat, do, vns, seq_grads, name, comm=None):
    s = q.shape[0]
    n = s // CHUNK
    cb = min(LOCAL_CB, n)

    def body(*refs):
        q_ref, k_ref, v_ref, bg_ref, t_ref, do_ref = refs[0:6]
        vn_refs = refs[6:8]
        sg = (refs[8:13], refs[13:18])
        dq_ref, dk_ref, dv_ref, dbg_ref = refs[18:]
        lane = lax.broadcasted_iota(jnp.int32, (CHUNK, BAP), 1)
        rowi = lax.broadcasted_iota(jnp.int32, (CHUNK, 1), 0)
        ones = jnp.ones((CHUNK, DH), F32)
        masks = [_GdnMasks(d) for d in range(2)]
        inst = []
        for jj in range(cb):
            rows = slice(CHUNK * jj, CHUNK * (jj + 1))
            bgv = bg_ref[rows, :]
            qs = [q_ref[rows, _head(h)] for h in range(NH)]
            ks = [k_ref[rows, _head(h)] for h in range(NH)]
            kk = [_dot_nt(ks[h], ks[h]) for h in range(NH)]
            q0 = [_dot_nt(qs[h], ks[h]) for h in range(NH)]
            for d, m in enumerate(masks):
                gcs, gcs_t = _chunk_cumsums(m, bgv)
                for h in range(NH):
                    c = _GdnHead(qs[h], ks[h], v_ref[rows, _head(h)], kk[h], q0[h], bgv, gcs, gcs_t, d, h, m)
                    inst.append((jj, rows, d, h, m, c))
        ni = len(inst)
        cs = [it[-1] for it in inst]
        tms = [t_ref[jj, d, h] for jj, _, d, h, _, _ in inst]
        d_vns = [sg[d][0][rows, _head(h)] for _, rows, d, h, _, _ in inst]
        d_ws = [sg[d][3][rows, _head(h)] for _, rows, d, h, _, _ in inst]
        d_ts = [_dot_nt(d_vns[i], cs[i].vb) + _dot_nt(d_ws[i], cs[i].kg) for i in range(ni)]
        tts = [tm.T for tm in tms]
        xs = [_dot3(tts[i], d_ts[i]) for i in range(ni)]
        d_ls = [jnp.where(inst[i][4].strict, -_dot3(xs[i], tts[i]), 0.0) for i in range(ni)]
        d_attns = [jnp.where(m.incl, _dot_nt(do_ref[rows, _head(h)], vn_refs[d][rows, _head(h)]), 0.0)
                   for _, rows, d, h, m, _ in inst]
        d_vbs = [_dot(tts[i], d_vns[i]) for i in range(ni)]
        d_kgs = [_dot(tts[i], d_ws[i]) for i in range(ni)]
        d_a0s = [d_ls[i] * cs[i].decay for i in range(ni)]
        d_q0s = [d_attns[i] * cs[i].decay for i in range(ni)]
        es = [(d_ls[i] * cs[i].a0 + d_attns[i] * cs[i].q0) * cs[i].decay for i in range(ni)]
        kb_mm = [_dot(d_a0s[i], cs[i].k) for i in range(ni)]
        q_mm = [_dot(d_q0s[i], cs[i].k) for i in range(ni)]
        k_mm = [_dot_tn(d_a0s[i], cs[i].kb) + _dot_tn(d_q0s[i], cs[i].q) for i in range(ni)]
        e_cols = [_dot_exact(ones, es[i], _TN, False)[:, 0:1] for i in range(ni)]
        acc = {}
        d_gcs, d_betas = [], []
        for i, (jj, rows, d, h, m, c) in enumerate(inst):
            sl = _head(h)
            d_kd, d_qd = sg[d][1][rows, sl], sg[d][2][rows, sl]
            d_cd = sg[d][4][jj, h:h + 1, 0:1]
            d_vb, d_kg = d_vbs[i], d_kgs[i]
            d_kb = kb_mm[i] + d_kg * c.eg
            parts = (q_mm[i] + d_qd * c.eg, k_mm[i] + d_kd * c.ek + d_kb * c.beta, d_vb * c.beta)
            acc[jj, h] = [p + a for a, p in zip(acc[jj, h], parts)] if (jj, h) in acc else list(parts)
            kd_term = d_kd * c.kd
            d_gc = (jnp.sum(d_kg * c.kg + d_qd * c.qd - kd_term, axis=1, keepdims=True)
                    + jnp.sum(es[i], axis=1, keepdims=True) - e_cols[i])
            d_gl = jnp.sum(jnp.sum(kd_term, axis=0, keepdims=True), axis=1, keepdims=True) + d_cd * c.cd
            d_gcs.append(d_gc + jnp.where(rowi == m.last, d_gl, 0.0))
            d_betas.append(jnp.sum(d_kb * c.k + d_vb * c.v, axis=1, keepdims=True))
        d_gs = [_dot_exact(inst[i][4].tri, d_gcs[i] * ones, _TN, True)[:, 0:1] for i in range(ni)]
        dbg = [jnp.zeros((CHUNK, BAP), F32) for _ in range(cb)]
        for i, (jj, _, d, h, _, _) in enumerate(inst):
            dbg[jj] = dbg[jj] + jnp.where(lane == 4 * d + h, d_betas[i], 0.0) + jnp.where(lane == 8 + 4 * d + h, d_gs[i], 0.0)
        for jj in range(cb):
            rows = slice(CHUNK * jj, CHUNK * (jj + 1))
            for h in range(NH):
                dq_ref[rows, _head(h)], dk_ref[rows, _head(h)], dv_ref[rows, _head(h)] = acc[jj, h]
            dbg_ref[rows, :] = dbg[jj]

    tok = _rows(cb * CHUNK, HW)
    bgs = _rows(cb * CHUNK, BAP)
    mat = pl.BlockSpec((cb, 2, NH, CHUNK, CHUNK), lambda i: (i, 0, 0, 0, 0))
    dcd = pl.BlockSpec((cb, NH, DH), lambda i: (i, 0, 0))
    args = [q, k, v, bg, tmat, do, vns[0], vns[1]]
    in_specs = [tok, tok, tok, bgs, mat, tok, tok, tok]
    for d in range(2):
        args += list(seq_grads[d])
        in_specs += [tok] * 4 + [dcd]
    return _pallas(body, comm, name=name, grid=(n // cb,), in_specs=in_specs, out_specs=[tok, tok, tok, bgs],
                   out_shape=[_sds((s, HW))] * 3 + [_sds((s, BAP))], scratch_shapes=[], args=args)


def _prep_bwd(c_qkv, p_ba, alog_row, dtb_row, dq, dk, dv, dbg, name):
    s = c_qkv.shape[0]
    t = min(ROW_TILE, s)

    def body(cq_ref, pc_ref, alog_ref, dtb_ref, dq_ref, dk_ref, dv_ref, dbg_ref,
             dcq_ref, dpc_ref, dalog_ref, ddtb_ref):
        @pl.when(pl.program_id(0) == 0)
        def _():
            dalog_ref[...] = jnp.zeros_like(dalog_ref)
            ddtb_ref[...] = jnp.zeros_like(ddtb_ref)

        cq = cq_ref[...]
        sq = cq * _sig(cq)
        sg = _silu_grad(cq)
        for h in range(NH):
            sl = slice(DH * h, DH * (h + 1))
            for off, d_ref, scale in ((0, dq_ref, DH ** -0.5), (RGW, dk_ref, 1.0)):
                csl = slice(off + DH * h, off + DH * (h + 1))
                xh = sq[:, csl]
                nrm = lax.rsqrt(jnp.sum(xh * xh, axis=-1, keepdims=True) + EPS)
                y = xh * nrm
                dy = d_ref[:, sl] * scale
                dcq_ref[:, csl] = nrm * (dy - y * jnp.sum(dy * y, axis=-1, keepdims=True)) * sg[:, csl]
        dcq_ref[:, 2 * RGW:] = dv_ref[...] * sg[:, 2 * RGW:]
        pc = pc_ref[...]
        lane = lax.broadcasted_iota(jnp.int32, pc.shape, 1)
        dbg = dbg_ref[...]
        beta = _sig(pc)
        ea = jnp.exp(alog_ref[...])
        z = pc + dtb_ref[...]
        g = -ea * _softplus(z)
        is_g = jnp.logical_and(lane >= 8, lane < 16)
        d_alpha = jnp.where(is_g, dbg * (-ea) * _sig(z), 0.0)
        dpc_ref[...] = jnp.where(lane < 8, dbg * beta * (1.0 - beta), d_alpha).astype(BF16)
        dalog_ref[...] += _colsum(jnp.where(is_g, dbg * g, 0.0))
        ddtb_ref[...] += _colsum(d_alpha)

    return pl.pallas_call(
        body, name=name, grid=(s // t,),
        in_specs=[_rows(t, QKVW), _rows(t, BAP), _full((1, BAP)), _full((1, BAP))] + [_rows(t, HW)] * 3 + [_rows(t, BAP)],
        out_specs=[_rows(t, QKVW), _rows(t, BAP), _full((1, BAP)), _full((1, BAP))],
        out_shape=[_sds((s, QKVW)), _sds((s, BAP), BF16), _sds((1, BAP)), _sds((1, BAP))],
        compiler_params=_cparams(1),
    )(c_qkv, p_ba, alog_row, dtb_row, dq, dk, dv, dbg)


def _mix_out_values(hf, hb, gate, of, ob, z, gn):
    hr = hf + hb
    y_rg = hr * _gelu(gate)
    osum = of + ob
    parts = []
    for h in range(NH):
        sl = slice(DH * h, DH * (h + 1))
        oh = osum[:, sl]
        r, ohat = _rms(oh)
        zh = z[:, sl]
        parts.append((r, ohat, zh))
    y_gdn = jnp.concatenate([ohat * gn * (zh * _sig(zh)) for (r, ohat, zh) in parts], axis=1)
    return hr, y_rg, y_gdn, parts


def _outproj(x1, hf, hb, gate, of, ob, z, gn, wout, name):
    s = x1.shape[0]
    t = min(ROW_TILE, s)

    def body(x_ref, hf_ref, hb_ref, gate_ref, of_ref, ob_ref, z_ref, gn_ref, w_ref, xo_ref, y_ref):
        _, y_rg, y_gdn, _ = _mix_out_values(hf_ref[...], hb_ref[...], gate_ref[...], of_ref[...], ob_ref[...],
                                            z_ref[...], gn_ref[...])
        y = jnp.concatenate([y_rg, y_gdn], axis=1).astype(BF16)
        y_ref[...] = y
        xo_ref[...] = x_ref[...] + jnp.dot(y, w_ref[...], preferred_element_type=F32)

    return pl.pallas_call(
        body, name=name, grid=(s // t,),
        in_specs=[_rows(t, D)] + [_rows(t, RGW)] * 6 + [_full((1, DH)), _full((D, D))],
        out_specs=[_rows(t, D), _rows(t, D)], out_shape=[_sds((s, D)), _sds((s, D), BF16)],
        compiler_params=_cparams(1),
    )(x1, hf, hb, gate, of, ob, z, gn, wout)


def _outproj_bwd(dx2, hf, hb, gate, of, ob, z, gn, wout, name, comm=None):
    s = dx2.shape[0]
    t = min(ROW_TILE, s)

    def body(d_ref, hf_ref, hb_ref, gate_ref, of_ref, ob_ref, z_ref, gn_ref, w_ref,
             dhr_ref, dgate_ref, dos_ref, dz_ref, dgn_ref, db_ref):
        @pl.when(pl.program_id(0) == 0)
        def _():
            dgn_ref[...] = jnp.zeros_like(dgn_ref)

        gate = gate_ref[...]
        gn_v = gn_ref[...]
        hr, _, _, parts = _mix_out_values(hf_ref[...], hb_ref[...], gate, of_ref[...], ob_ref[...], z_ref[...], gn_v)
        dbf = d_ref[...].astype(BF16)
        db_ref[...] = dbf
        dy = _dot_nt(dbf, w_ref[...])
        dyr = dy[:, :RGW]
        dhr_ref[...] = dyr * _gelu(gate)
        dgate_ref[...] = (dyr * hr * _gelu_grad(gate)).astype(BF16)
        dgn = jnp.zeros((1, DH), F32)
        for h, (r, ohat, zh) in enumerate(parts):
            sl = slice(DH * h, DH * (h + 1))
            dyh = dy[:, RGW + DH * h:RGW + DH * (h + 1)]
            sz = zh * _sig(zh)
            dn = dyh * sz
            dz_ref[:, sl] = (dyh * ohat * gn_v * _silu_grad(zh)).astype(BF16)
            dgn = dgn + _colsum(dn * ohat)
            dos_ref[:, sl] = _rms_bwd(dn, ohat, r, gn_v).astype(BF16)
        dgn_ref[...] += dgn

    return _pallas(
        body, comm, name=name, grid=(s // t,),
        in_specs=[_rows(t, D)] + [_rows(t, RGW)] * 6 + [_full((1, DH)), _full((D, D))],
        out_specs=[_rows(t, RGW)] * 4 + [_full((1, DH)), _rows(t, D)],
        out_shape=[_sds((s, RGW))] + [_sds((s, RGW), BF16)] * 3 + [_sds((1, DH)), _sds((s, D), BF16)],
        scratch_shapes=[], args=(dx2, hf, hb, gate, of, ob, z, gn, wout))


def _loss_head(x3, target, gain, name):
    s = x3.shape[0]
    t = min(ROW_TILE, s)

    def body(x_ref, t_ref, g_ref, dx_ref, dxh_ref, loss_ref, dg_ref):
        @pl.when(pl.program_id(0) == 0)
        def _():
            loss_ref[...] = jnp.zeros_like(loss_ref)
            dg_ref[...] = jnp.zeros_like(dg_ref)

        r, xh = _rms(x_ref[...])
        gv = g_ref[...]
        err = xh * gv - t_ref[...]
        per_tok = jnp.mean(err * err, axis=-1, keepdims=True)
        loss_ref[...] += 0.5 * jnp.sum(per_tok, axis=0, keepdims=True)
        dy = err * (1.0 / D)
        dg_ref[...] += _colsum(dy * xh)
        dx = _rms_bwd(dy, xh, r, gv)
        dx_ref[...] = dx
        dxh_ref[...] = (0.5 * dx).astype(BF16)

    return pl.pallas_call(
        body, name=name, grid=(s // t,), in_specs=[_rows(t, D), _rows(t, D), _full((1, D))],
        out_specs=[_rows(t, D), _rows(t, D), _full((8, 128)), _full((1, D))],
        out_shape=[_sds((s, D)), _sds((s, D), BF16), _sds((8, 128)), _sds((1, D))], compiler_params=_cparams(1),
    )(x3, target, gain)


def _adamw_math(wv, gv, mv, vv):
    mn = ADAM_B1 * mv + (1.0 - ADAM_B1) * gv
    vn = ADAM_B2 * vv + (1.0 - ADAM_B2) * (gv * gv)
    m_hat = mn / (1.0 - ADAM_B1 ** ADAM_STEP)
    v_hat = vn / (1.0 - ADAM_B2 ** ADAM_STEP)
    return -ADAM_LR * (m_hat / (jnp.sqrt(v_hat) + ADAM_EPS) + ADAM_WD * wv), mn, vn


def _row_tile(r, c):
    tr = r
    while tr * c * 4 > (1 << 20) and tr % 16 == 0:
        tr //= 2
    return tr


def _adamw(w, g, m, v, name):
    r, c = w.shape
    tr = _row_tile(r, c)

    def body(w_ref, g_ref, m_ref, v_ref, d_ref, nm_ref, nv_ref):
        d_ref[...], nm_ref[...], nv_ref[...] = _adamw_math(w_ref[...], g_ref[...], m_ref[...], v_ref[...])

    return pl.pallas_call(
        body, name=name, grid=(r // tr,), in_specs=[_rows(tr, c)] * 4, out_specs=[_rows(tr, c)] * 3,
        out_shape=[_sds((r, c))] * 3, compiler_params=_cparams(1),
    )(w, g, m, v)


def _adamw_halves(w, own, recv, m, v, c_arr, name):
    r, c = w.shape
    h = r // 2
    tr = _row_tile(h, c)
    nh = h // tr

    def body(c_ref, w_ref, own_ref, recv_ref, m_ref, v_ref, g_ref, d_ref, nm_ref, nv_ref):
        first_half = pl.program_id(0) < nh
        use_own = first_half == (c_ref[0] == 0)
        gv = jnp.where(use_own, own_ref[...], recv_ref[...])
        g_ref[...] = gv
        d_ref[...], nm_ref[...], nv_ref[...] = _adamw_math(w_ref[...], gv, m_ref[...], v_ref[...])

    full = pl.BlockSpec((tr, c), lambda i, c_ref: (i, 0))
    half = pl.BlockSpec((tr, c), lambda i, c_ref: (i % nh, 0))
    return pl.pallas_call(
        body, name=name, out_shape=[_sds((r, c))] * 4,
        grid_spec=pltpu.PrefetchScalarGridSpec(
            num_scalar_prefetch=1, grid=(2 * nh,), in_specs=[full, half, half, full, full], out_specs=[full] * 4),
        compiler_params=_cparams(1),
    )(c_arr, w, own, recv, m, v)


def _mesh_pos():
    return lax.axis_index("x"), lax.axis_index("y"), lax.axis_index("c")


def _other_chips(x, y):
    return [(1 - x, y), (x, 1 - y), (1 - x, 1 - y)]


class _Comm:
    def __init__(self, inputs, out_shapes, scratch, start, finish, space=pltpu.HBM, relay=None):
        self.inputs, self.out_shapes, self.scratch = list(inputs), list(out_shapes), list(scratch)
        self.start, self.finish, self.space = start, finish, space
        self.relay = relay if relay is not None else (lambda ins, outs, sems: None)


def _comm_call(comm, name):
    ni, no = len(comm.inputs), len(comm.out_shapes)

    def body(*refs):
        comm.start(refs[:ni], refs[ni:ni + no], refs[ni + no:])
        comm.relay(refs[:ni], refs[ni:ni + no], refs[ni + no:])
        comm.finish(refs[:ni], refs[ni:ni + no], refs[ni + no:])

    spec = pl.BlockSpec(memory_space=comm.space)
    return list(pl.pallas_call(body, name=name, out_shape=comm.out_shapes, in_specs=[spec] * ni, out_specs=[spec] * no,
                               scratch_shapes=comm.scratch)(*comm.inputs))


def _join_comm(a, b):
    ia, oa, sa = len(a.inputs), len(a.out_shapes), len(a.scratch)

    def both(method):
        def run(ins, outs, sems):
            getattr(a, method)(ins[:ia], outs[:oa], sems[:sa])
            getattr(b, method)(ins[ia:], outs[oa:], sems[sa:])
        return run

    return _Comm(a.inputs + b.inputs, a.out_shapes + b.out_shapes, a.scratch + b.scratch, both("start"), both("finish"),
                 relay=both("relay"))


def _pallas(body, comm, *, name, grid, in_specs, out_specs, out_shape, scratch_shapes, args):
    params = _cparams(len(grid))
    if comm is None:
        outs = pl.pallas_call(body, name=name, grid=grid, in_specs=in_specs, out_specs=out_specs, out_shape=out_shape,
                              scratch_shapes=scratch_shapes, compiler_params=params)(*args)
        return list(outs), []
    n_in, n_out, n_sc = len(in_specs), len(out_specs), len(scratch_shapes)
    ci, co = len(comm.inputs), len(comm.out_shapes)

    def carried(*refs):
        bounds = [0, n_in, n_in + ci, n_in + ci + n_out, n_in + ci + n_out + co, n_in + ci + n_out + co + n_sc, len(refs)]
        ins, cins, outs, couts, scr, csems = [refs[lo:hi] for lo, hi in zip(bounds[:-1], bounds[1:])]
        ids = [pl.program_id(k) for k in range(len(grid))]
        first = functools.reduce(jnp.logical_and, [i == 0 for i in ids])
        last = functools.reduce(jnp.logical_and, [i == g - 1 for i, g in zip(ids, grid)])
        late = functools.reduce(jnp.logical_and, [ids[0] == (3 * grid[0]) // 4] + [i == 0 for i in ids[1:]])

        @pl.when(first)
        def _():
            comm.start(cins, couts, csems)

        body(*ins, *outs, *scr)

        @pl.when(late)
        def _():
            comm.relay(cins, couts, csems)

        @pl.when(last)
        def _():
            comm.finish(cins, couts, csems)

    hbm = pl.BlockSpec(memory_space=pltpu.HBM)
    outs = pl.pallas_call(
        carried, name=name, grid=grid, in_specs=list(in_specs) + [hbm] * ci, out_specs=list(out_specs) + [hbm] * co,
        out_shape=list(out_shape) + comm.out_shapes, scratch_shapes=list(scratch_shapes) + comm.scratch,
        compiler_params=params)(*args, *comm.inputs)
    return list(outs[:n_out]), list(outs[n_out:])


def _gather_comm(arrays, space, block_rows):
    n_arr = len(arrays)

    def plan(x_refs, out_refs, sems):
        send_sems, recv_sems, local_sems = sems
        x, y, c = _mesh_pos()
        me, sibling = (x, y, c), (x, y, 1 - c)
        chips = _other_chips(x, y)

        def slot(a, px, py, pc):
            return out_refs[a].at[4 * px + 2 * py + pc]

        def copy(a, k, block, to, src=None):
            return pltpu.make_async_remote_copy(
                src_ref=slot(a, *block) if src is None else src, dst_ref=slot(a, *block),
                send_sem=send_sems.at[7 * a + k], recv_sem=recv_sems.at[7 * a + k], device_id=to, device_id_type=MESH)

        srcs = [x_refs[a] if block_rows[a] is None else
                x_refs[a].at[pl.ds(pl.multiple_of(c * block_rows[a], 16), block_rows[a]), :] for a in range(n_arr)]
        local = [pltpu.make_async_copy(srcs[a], slot(a, *me), local_sems.at[a]) for a in range(n_arr)]
        first = []
        for a in range(n_arr):
            first += [copy(a, 1 + j, me, (*chip, c), src=srcs[a]) for j, chip in enumerate(chips)]
            first.append(copy(a, 0, me, sibling, src=srcs[a]))
        return me, sibling, chips, c, copy, local, first

    def start(x_refs, out_refs, sems):
        _, _, _, _, _, local, first = plan(x_refs, out_refs, sems)
        for cp in local + first:
            cp.start()

    def relay(x_refs, out_refs, sems):
        me, sibling, chips, c, copy, _, _ = plan(x_refs, out_refs, sems)
        for j, chip in enumerate(chips):
            for a in range(n_arr):
                copy(a, 1 + j, (*chip, c), me).wait_recv()
                copy(a, 4 + j, (*chip, c), sibling).start()

    def finish(x_refs, out_refs, sems):
        me, sibling, chips, c, copy, local, first = plan(x_refs, out_refs, sems)
        passed = [copy(a, 4 + j, (*chip, c), sibling) for j, chip in enumerate(chips) for a in range(n_arr)]
        for a in range(n_arr):
            copy(a, 0, sibling, me).wait_recv()
            for j, chip in enumerate(chips):
                copy(a, 4 + j, (*chip, 1 - c), me).wait_recv()
        for cp in first + passed:
            cp.wait_send()
        for cp in local:
            cp.wait()

    out_shapes = [_sds((8, w.shape[0] if r is None else r) + w.shape[1:], w.dtype) for w, r in zip(arrays, block_rows)]
    scratch = [pltpu.SemaphoreType.DMA((7 * n_arr,)), pltpu.SemaphoreType.DMA((7 * n_arr,)), pltpu.SemaphoreType.DMA((n_arr,))]
    return _Comm(arrays, out_shapes, scratch, start, finish, space, relay=relay)


def _weights_gather_comm(shards):
    return _gather_comm(shards, pltpu.HBM, [w.shape[0] // 2 for w in shards])


def _all_shards(gathered):
    return [o.reshape(NSH, 2 * o.shape[1], o.shape[2]) for o in gathered]


def _gather_small(block, name):
    return _comm_call(_gather_comm([block], pltpu.VMEM, [None]), name)[0]


def _exchange_comm(gs):
    n = len(gs)
    halves = [g.shape[1] // 2 for g in gs]

    def plan(g_refs, land_refs, sems):
        send_sems, recv_sems = sems
        x, y, c = _mesh_pos()
        copies = []
        for a in range(n):
            h = halves[a]
            for s in range(NSH):
                copies.append(pltpu.make_async_remote_copy(
                    src_ref=g_refs[a].at[s, pl.ds(pl.multiple_of((1 - c) * h, 8), h), :], dst_ref=land_refs[a].at[s],
                    send_sem=send_sems.at[NSH * a + s], recv_sem=recv_sems.at[NSH * a + s],
                    device_id=(x, y, 1 - c), device_id_type=MESH))
        return copies

    def start(g_refs, land_refs, sems):
        for cp in plan(g_refs, land_refs, sems):
            cp.start()

    def finish(g_refs, land_refs, sems):
        for cp in plan(g_refs, land_refs, sems):
            cp.wait()

    scratch = [pltpu.SemaphoreType.DMA((NSH * n,)), pltpu.SemaphoreType.DMA((NSH * n,))]
    return _Comm(gs, [_sds((NSH, h, g.shape[2])) for h, g in zip(halves, gs)], scratch, start, finish)


def _chip_sum(g, land, c_arr, name):
    _, h, cols = land.shape

    def body(c_ref, g_ref, l_ref, o_ref):
        o_ref[...] = (g_ref[...] + l_ref[...]).astype(BF16)

    return pl.pallas_call(
        body, name=name, out_shape=_sds((NSH, h, cols), BF16),
        grid_spec=pltpu.PrefetchScalarGridSpec(
            num_scalar_prefetch=1, grid=(NSH,),
            in_specs=[pl.BlockSpec((1, h, cols), lambda s, c_ref: (s, c_ref[0], 0)),
                      pl.BlockSpec((1, h, cols), lambda s, c_ref: (s, 0, 0))],
            out_specs=pl.BlockSpec((1, h, cols), lambda s, c_ref: (s, 0, 0))),
        compiler_params=_cparams(1),
    )(c_arr, g, land)


def _scatter_comm(parts):
    n = len(parts)

    def plan(p_refs, land_refs, sems):
        send_sems, recv_sems, local_sems = sems
        x, y, c = _mesh_pos()
        my_chip = 2 * x + y
        local = [pltpu.make_async_copy(p_refs[a].at[my_chip], land_refs[a].at[my_chip], local_sems.at[a]) for a in range(n)]
        copies = []
        for a in range(n):
            for j, (px, py) in enumerate(_other_chips(x, y)):
                copies.append(pltpu.make_async_remote_copy(
                    src_ref=p_refs[a].at[2 * px + py], dst_ref=land_refs[a].at[my_chip],
                    send_sem=send_sems.at[3 * a + j], recv_sem=recv_sems.at[3 * a + j],
                    device_id=(px, py, c), device_id_type=MESH))
        return local, copies

    def start(p_refs, land_refs, sems):
        local, copies = plan(p_refs, land_refs, sems)
        for cp in local + copies:
            cp.start()

    def finish(p_refs, land_refs, sems):
        local, copies = plan(p_refs, land_refs, sems)
        for cp in copies:
            cp.wait()
        for cp in local:
            cp.wait()

    scratch = [pltpu.SemaphoreType.DMA((3 * n,)), pltpu.SemaphoreType.DMA((3 * n,)), pltpu.SemaphoreType.DMA((n,))]
    return _Comm(parts, [_sds(p.shape, BF16) for p in parts], scratch, start, finish)


def _sum_slots(land, name):
    k, r, c = land.shape
    tr = r // 2 if r % 32 == 0 else r

    def body(l_ref, o_ref):
        acc = l_ref[0].astype(F32)
        for i in range(1, k):
            acc = acc + l_ref[i].astype(F32)
        o_ref[...] = acc

    return pl.pallas_call(
        body, name=name, grid=(r // tr,), in_specs=[pl.BlockSpec((k, tr, c), lambda i: (0, i, 0))],
        out_specs=_rows(tr, c), out_shape=_sds((r, c)), compiler_params=_cparams(1),
    )(land)


def _sibling_swap(halves):
    n = len(halves)

    def body(*refs):
        h_refs, out_refs = refs[:n], refs[n:2 * n]
        send_sems, recv_sems = refs[2 * n:]
        x, y, c = _mesh_pos()
        copies = [pltpu.make_async_remote_copy(
            src_ref=h_refs[a], dst_ref=out_refs[a], send_sem=send_sems.at[a], recv_sem=recv_sems.at[a],
            device_id=(x, y, 1 - c), device_id_type=MESH) for a in range(n)]
        for cp in copies:
            cp.start()
        for cp in copies:
            cp.wait()

    return pl.pallas_call(
        body, name="grad_sibling_swap", out_shape=[_sds(h.shape) for h in halves],
        in_specs=[pl.BlockSpec(memory_space=pltpu.HBM)] * n, out_specs=[pl.BlockSpec(memory_space=pltpu.HBM)] * n,
        scratch_shapes=[pltpu.SemaphoreType.DMA((n,)), pltpu.SemaphoreType.DMA((n,))],
    )(*halves)


def _pad_rows(v, width):
    flat = v.reshape(-1)
    rows = -(-flat.shape[0] // width)
    rows = -(-rows // 8) * 8
    return jnp.pad(flat, (0, rows * width - flat.shape[0])).reshape(rows, width)


def _size(shape):
    n = 1
    for dim in shape:
        n *= dim
    return n


def _row_pack(arrs):
    pieces = []
    for a in arrs:
        rows = -(-a.size // D)
        pieces.append(jnp.pad(a.reshape(-1), (0, rows * D - a.size)).reshape(rows, D))
    total = sum(p.shape[0] for p in pieces)
    if total % 8:
        pieces.append(jnp.zeros((8 - total % 8, D), F32))
    return jnp.concatenate(pieces, axis=0)


def _row_unpack(packed, shapes):
    out, r0 = [], 0
    for shp in shapes:
        n = _size(shp)
        rows = -(-n // D)
        out.append(packed[r0:r0 + rows].reshape(-1)[:n].reshape(shp))
        r0 += rows
    return out


def _block_diag(w):
    eye = jnp.eye(8, dtype=w.dtype)
    return (w[:, :, None, :] * eye[:, None, :, None]).reshape(RGW, RGW)


def _diag_blocks(dense):
    r = dense.reshape(8, 64, 8, 64)
    return jnp.stack([r[n, :, n, :] for n in range(8)])


def _lane_row(v8):
    return jnp.zeros((1, BAP), F32).at[0, 8:16].set(v8.reshape(8))


def _chip_sums(gs, lands, names, c_arr):
    return [_chip_sum(g, l, c_arr, "chip_sum_" + n) for g, l, n in zip(gs, lands, names)]


def _reduce_parts(gs, names, c_arr, tag):
    return _chip_sums(gs, _comm_call(_exchange_comm(gs), "grad_sibling_exchange_" + tag), names, c_arr)


def _local_step(x, target, sw, ffn1_w, later_shards, c_arr):
    (g1, gmix, rg_cw8, rg_cb, wgates, gbias, lam_row, gdn_cw8, alog_row, dtb_row, gn, g2, gfin) = sw
    wg1, wu1, wd1 = ffn1_w

    (x1, a1, b1, fb1), gathered = _ffn_fwd(x, g1, wg1, wu1, wd1, "ffn1_fwd", comm=_weights_gather_comm(later_shards))
    win_sh, wout_sh, wg2, wu2, wd2 = _all_shards(gathered)
    w_in_full = jnp.transpose(win_sh, (1, 0, 2)).reshape(D, NSH * INSH)
    wout = wout_sh.reshape(D, D)
    w_in_groups = (w_in_full[:, 0:512], w_in_full[:, 512:1024], w_in_full[:, 1024:2560], w_in_full[:, 2560:3072],
                   jnp.pad(w_in_full[:, 3072:3088], ((0, 0), (0, BAP - BAW))))
    h2, p_rgx, p_gate, p_qkv, p_z, p_ba = _inproj(x1, gmix, w_in_groups, "in_proj")
    c_rg = _conv(p_rgx, rg_cw8, rg_cb, "rg_conv")
    c_qkv = _conv(p_qkv, gdn_cw8, jnp.zeros((1, QKVW), F32), "gdn_conv")
    a0, bb0, a1s, bb1, q, k, v, bg = _mix_prep(c_rg, c_qkv, p_ba, wgates, gbias, lam_row, alog_row, dtb_row, "mix_prep")
    tmat, gu, gw, gqd, gkd, gat, gcd = _gdn_local_fwd(q, k, v, bg, "gdn_local_fwd")
    of, s0, vn0, ob, s1, vn1, hf, hb = _gdn_seq_fwd(gu, gw, gqd, gkd, gat, gcd, "gdn_seq_fwd", scan=(a0, bb0, a1s, bb1))
    x2, ymix = _outproj(x1, hf, hb, p_gate, of, ob, p_z, gn, wout, "out_proj")
    (x3, a2, b2, fb2), _ = _ffn_fwd(x2, g2, wg2, wu2, wd2, "ffn2_fwd")
    dx3, dob2, loss_blk, d_gfin = _loss_head(x3, target, gfin, "loss_head")

    dx2, d_g2, hb2, dab2, dbb2, _ = _ffn_bwd(x2, dx3, dob2, g2, a2, b2, wg2, wu2, wd2, "ffn2_bwd")
    d_ffn2 = [_tn(dab2, hb2, "ffn2_dwg"), _tn(dbb2, hb2, "ffn2_dwu"), _tn(fb2, dob2, "ffn2_dwd", a_transposed=True)]

    (d_hr, d_gate, d_os, d_z, d_gn, dx2b), lands = _outproj_bwd(dx2, hf, hb, p_gate, of, ob, p_z, gn, wout, "out_proj_bwd",
                                                               comm=_exchange_comm(d_ffn2))
    parts_ffn2 = _chip_sums(d_ffn2, lands, _BIG_NAMES[5:8], c_arr)
    d_wout = _tn(ymix, dx2b, "dw_out")[0]

    sg = _gdn_seq_bwd(d_os, gw, gqd, gkd, gat, gcd, (s0, s1), (vn0, vn1), "gdn_seq_bwd", scan=(a1s, d_hr, a0, d_hr))
    lam1, lam0 = sg[10:12]
    d_xc, d_pre, xcb, d_gbias, d_lam = _gates_bwd(c_rg, wgates, gbias, lam_row, lam0, lam1, hf, hb, "rg_gates_bwd")
    d_wgates = _tn(xcb, d_pre, "dw_gates")[0]
    d_prgx, d_rgcw8, d_rgcb = _conv_bwd(p_rgx, d_xc, rg_cw8, "rg_conv_bwd")

    (dq, dk, dv, dbg), lands_ffn2 = _gdn_local_bwd(q, k, v, bg, tmat, d_os, (vn0, vn1), (sg[0:5], sg[5:10]), "gdn_local_bwd",
                                                  comm=_scatter_comm(parts_ffn2))
    d_cqkv, d_pba, d_alog, d_dtb = _prep_bwd(c_qkv, p_ba, alog_row, dtb_row, dq, dk, dv, dbg, "gdn_prep_bwd")
    d_pqkv, d_gdncw8, _ = _conv_bwd(p_qkv, d_cqkv, gdn_cw8, "gdn_conv_bwd")

    dps = (d_prgx, d_gate, d_pqkv, d_z, d_pba)
    dx1, dob1, d_gmix = _inproj_bwd(x1, dx2, gmix, dps, w_in_groups, "in_proj_bwd")
    d_win_groups = [_tn(h2, dp, "dw_in_%d" % i)[0] for i, dp in enumerate(dps)]
    d_win = jnp.concatenate(d_win_groups[:4] + [d_win_groups[4][:, :BAW]], axis=1)
    d_mix = [jnp.transpose(d_win.reshape(D, NSH, INSH), (1, 0, 2)), d_wout.reshape(NSH, OUTSH, D)]

    small = dict(
        mix_norm=d_gmix, rg_conv_w=d_rgcw8[:4], rg_conv_b=d_rgcb,
        rg_gate_a_w=jnp.stack([_diag_blocks(d_wgates[:, RGW * i:RGW * (i + 1)]) for i in (0, 1)]),
        rg_gate_x_w=jnp.stack([_diag_blocks(d_wgates[:, RGW * i:RGW * (i + 1)]) for i in (2, 3)]),
        rg_gate_a_b=d_gbias[0, :2 * RGW].reshape(2, RGW), rg_gate_x_b=d_gbias[0, 2 * RGW:].reshape(2, RGW),
        rg_lambda=d_lam.reshape(2, RGW), gdn_conv_w=d_gdncw8[:4],
        gdn_a_log=d_alog[0, 8:16].reshape(2, NH), gdn_dt_bias=d_dtb[0, 8:16].reshape(2, NH),
        gdn_norm=d_gn, ffn2_norm=d_g2, final_norm=d_gfin)
    small_pack = _row_pack([small[n] for n in _SMALL_NAMES[1:]])

    riders = _join_comm(_exchange_comm(d_mix), _gather_comm([small_pack], pltpu.HBM, [None]))
    gx, d_g1, hb1, dab1, dbb1, carried = _ffn_bwd(x, dx1, dob1, g1, a1, b1, wg1, wu1, wd1, "ffn1_bwd", comm=riders)
    parts_mix = _chip_sums(d_mix, carried[0:2], _BIG_NAMES[3:5], c_arr)
    d_wg1, lands_mix = _tn(dab1, hb1, "ffn1_dwg", comm=_scatter_comm(parts_mix))
    parts_wg1 = _reduce_parts([d_wg1], _BIG_NAMES[0:1], c_arr, "ffn1_gate")
    d_wu1, lands_wg1 = _tn(dbb1, hb1, "ffn1_dwu", comm=_scatter_comm(parts_wg1))
    parts_wu1 = _reduce_parts([d_wu1], _BIG_NAMES[1:2], c_arr, "ffn1_up")
    d_wd1, lands_wu1 = _tn(fb1, dob1, "ffn1_dwd", comm=_scatter_comm(parts_wu1), a_transposed=True)
    parts_wd1 = _reduce_parts([d_wd1], _BIG_NAMES[2:3], c_arr, "ffn1_down")
    lands_ffn1 = lands_wg1 + lands_wu1 + _comm_call(_scatter_comm(parts_wd1), "grad_chip_scatter_ffn1_down")

    halves = [_sum_slots(l, "sum_chips_" + n) for l, n in zip(lands_ffn1 + lands_mix + lands_ffn2, _BIG_NAMES)]
    small_shapes = [small[n].shape for n in _SMALL_NAMES[1:]]
    return loss_blk, gx, halves, d_g1, carried[2], small_shapes


_SMALL_NAMES = ("ffn1_norm", "mix_norm", "rg_conv_w", "rg_conv_b", "rg_gate_a_w", "rg_gate_a_b", "rg_gate_x_w",
                "rg_gate_x_b", "rg_lambda", "gdn_conv_w", "gdn_a_log", "gdn_dt_bias", "gdn_norm", "ffn2_norm", "final_norm")
_SMALL_SHARDED = dict(rg_conv_w=128, rg_gate_a_b=128, rg_gate_x_b=128, rg_lambda=128, gdn_conv_w=384)
_OUT_ORDER = ("ffn1_norm", "ffn1_w_gate", "ffn1_w_up", "ffn1_w_down", "mix_norm", "w_in", "w_out", "rg_conv_w", "rg_conv_b",
              "rg_gate_a_w", "rg_gate_a_b", "rg_gate_x_w", "rg_gate_x_b", "rg_lambda", "gdn_conv_w", "gdn_a_log",
              "gdn_dt_bias", "gdn_norm", "ffn2_norm", "ffn2_w_gate", "ffn2_w_up", "ffn2_w_down", "final_norm")
_BIG_NAMES = ("ffn1_w_gate", "ffn1_w_up", "ffn1_w_down", "w_in", "w_out", "ffn2_w_gate", "ffn2_w_up", "ffn2_w_down")
_TRANSPOSED = ("ffn1_w_gate", "ffn1_w_up", "ffn2_w_gate", "ffn2_w_up")


def kernel(x, ffn1_norm, ffn1_w_gate, ffn1_w_up, ffn1_w_down, mix_norm, w_in, w_out, rg_conv_w, rg_conv_b, rg_gate_a_w, rg_gate_a_b, rg_gate_x_w, rg_gate_x_b, rg_lambda, gdn_conv_w, gdn_a_log, gdn_dt_bias, gdn_norm, ffn2_norm, ffn2_w_gate, ffn2_w_up, ffn2_w_down, final_norm, loss_target, m_ffn1_norm, m_ffn1_w_gate, m_ffn1_w_up, m_ffn1_w_down, m_mix_norm, m_w_in, m_w_out, m_rg_conv_w, m_rg_conv_b, m_rg_gate_a_w, m_rg_gate_a_b, m_rg_gate_x_w, m_rg_gate_x_b, m_rg_lambda, m_gdn_conv_w, m_gdn_a_log, m_gdn_dt_bias, m_gdn_norm, m_ffn2_norm, m_ffn2_w_gate, m_ffn2_w_up, m_ffn2_w_down, m_final_norm, v_ffn1_norm, v_ffn1_w_gate, v_ffn1_w_up, v_ffn1_w_down, v_mix_norm, v_w_in, v_w_out, v_rg_conv_w, v_rg_conv_b, v_rg_gate_a_w, v_rg_gate_a_b, v_rg_gate_x_w, v_rg_gate_x_b, v_rg_lambda, v_gdn_conv_w, v_gdn_a_log, v_gdn_dt_bias, v_gdn_norm, v_ffn2_norm, v_ffn2_w_gate, v_ffn2_w_up, v_ffn2_w_down, v_final_norm):
    args = dict(locals())
    w = {n: args[n] for n in _OUT_ORDER}
    mom = {n: args["m_" + n] for n in _OUT_ORDER}
    var = {n: args["v_" + n] for n in _OUT_ORDER}
    xi, yi, ci = _mesh_pos()
    shard = 2 * xi + yi

    big_bf16 = [w[n][0].astype(BF16) for n in _BIG_NAMES]
    sm_local = _pad_rows(jnp.concatenate([w[n][0].reshape(-1) for n in _SMALL_SHARDED]), 128)
    first = _comm_call(_gather_comm(big_bf16[0:3] + [sm_local], pltpu.HBM, [t.shape[0] // 2 for t in big_bf16[0:3]] + [None]),
                       "gather_first_weights")
    ffn1_w = _all_shards(first[0:3])
    sm_all = first[3][0::2].reshape(NSH, -1)
    sm_full, off = {}, 0
    for n, wd_ in _SMALL_SHARDED.items():
        rows = w[n].shape[1]
        piece = sm_all[:, off:off + rows * wd_].reshape(NSH, rows, wd_)
        sm_full[n] = jnp.transpose(piece, (1, 0, 2)).reshape(rows, NSH * wd_)
        off += rows * wd_

    wa, wx = rg_gate_a_w[0], rg_gate_x_w[0]
    wgates = jnp.concatenate([_block_diag(wa[0]), _block_diag(wa[1]), _block_diag(wx[0]), _block_diag(wx[1])],
                             axis=1).astype(BF16)
    gbias = jnp.concatenate([sm_full["rg_gate_a_b"].reshape(1, -1), sm_full["rg_gate_x_b"].reshape(1, -1)], axis=1)
    sw = (ffn1_norm, mix_norm, jnp.pad(sm_full["rg_conv_w"], ((0, 4), (0, 0))), rg_conv_b, wgates, gbias,
          sm_full["rg_lambda"].reshape(1, -1), jnp.pad(sm_full["gdn_conv_w"], ((0, 4), (0, 0))), _lane_row(gdn_a_log),
          _lane_row(gdn_dt_bias), gdn_norm, ffn2_norm, final_norm.reshape(1, D))
    c_arr = ci.reshape(1).astype(jnp.int32)

    loss_blk, gx, halves, d_g1, small_packs, small_shapes = _local_step(x[0], loss_target[0], sw, ffn1_w, big_bf16[3:], c_arr)
    loss = lax.psum(loss_blk[0, 0], ("x", "y", "c"))
    grads = {}

    g1_all = _gather_small(jnp.pad(d_g1, ((0, 7), (0, 0))), "gather_ffn1_norm_grad")
    sm_sums = [_sum_slots(g1_all, "ffn1_norm_grad_sum")[0:1]] + _row_unpack(_sum_slots(small_packs, "small_grad_sum"), small_shapes)
    for n, g in zip(_SMALL_NAMES, sm_sums):
        if n in _SMALL_SHARDED:
            wd_ = _SMALL_SHARDED[n]
            g = lax.dynamic_slice_in_dim(g, shard * wd_, wd_, axis=1)
        grads[n] = g.reshape(w[n].shape)

    delta, new_m, new_v = {}, {}, {}
    for n, own, recv in zip(_BIG_NAMES, halves, _sibling_swap(halves)):
        to2d = jnp.transpose if n in _TRANSPOSED else (lambda t: t)
        outs4 = _adamw_halves(to2d(w[n][0]), own, recv, to2d(mom[n][0]), to2d(var[n][0]), c_arr, "adamw_" + n)
        grads[n], delta[n], new_m[n], new_v[n] = [to2d(o)[None] for o in outs4]
    packs = [_row_pack([t[n] for n in _SMALL_NAMES]) for t in (w, grads, mom, var)]
    sm_shapes = [w[n].shape for n in _SMALL_NAMES]
    for dst, src in zip((delta, new_m, new_v), _adamw(*packs, "adamw_small")):
        for n, val in zip(_SMALL_NAMES, _row_unpack(src, sm_shapes)):
            dst[n] = val

    outs = [loss, gx[None]]
    for group in (grads, delta, new_m, new_v):
        outs += [group[n] for n in _OUT_ORDER]
    return tuple(outs)
```

```python
import functools

import jax
import jax.numpy as jnp
from jax import lax
from jax.experimental import pallas as pl
from jax.experimental.pallas import tpu as pltpu

F32 = jnp.float32
BF16 = jnp.bfloat16
EPS = 1e-6
D = 1024
NSH = 4
FSH = 704
RGW = 512
QKVW = 1536
ZW = 512
BAW = 16
BAP = 128
INSH = 772
OUTSH = 256
CHUNK = 64
NH = 4
DH = 128
RG_C = 8.0
VMEM_LIMIT = 52 * 1024 * 1024
TN_VMEM_BUDGET = 40 * 1024 * 1024
ROW_TILE = 512
MESH = pl.DeviceIdType.MESH

ADAM_LR = 0.001
ADAM_B1 = 0.9
ADAM_B2 = 0.999
ADAM_EPS = 1e-08
ADAM_WD = 0.01
ADAM_STEP = 10


def _cparams(n_grid):
    return pltpu.CompilerParams(dimension_semantics=("arbitrary",) * n_grid, vmem_limit_bytes=VMEM_LIMIT)


def _sig(x):
    return 0.5 + 0.5 * jnp.tanh(0.5 * x)


def _sig_pos(x):
    return 1.0 / (1.0 + jnp.exp(-x))


def _softplus(x):
    return jnp.maximum(x, 0.0) + jnp.log(1.0 + jnp.exp(-jnp.abs(x)))


def _one_minus_sq_exp(la, a):
    y = 2.0 * la
    series = -y * (1.0 + y * (0.5 + y * (1.0 / 6 + y * (1.0 / 24 + y * (1.0 / 120 + y * (1.0 / 720))))))
    return jnp.where(y > -0.1, series, 1.0 - a * a)


_GELU_C = 0.7978845608028654


def _gelu(x):
    t = jnp.tanh(_GELU_C * (x + 0.044715 * x * x * x))
    return 0.5 * x * (1.0 + t)


def _gelu_grad(x):
    t = jnp.tanh(_GELU_C * (x + 0.044715 * x * x * x))
    return 0.5 * (1.0 + t) + 0.5 * x * (1.0 - t * t) * _GELU_C * (1.0 + 3 * 0.044715 * x * x)


def _silu_grad(x):
    s = _sig(x)
    return s * (1.0 + x * (1.0 - s))


def _dot(a, b):
    return jnp.dot(a.astype(BF16), b.astype(BF16), preferred_element_type=F32)


def _dot_nt(a, b):
    return lax.dot_general(a.astype(BF16), b.astype(BF16), (((1,), (1,)), ((), ())), preferred_element_type=F32)


def _dot_tn(a, b):
    return lax.dot_general(a.astype(BF16), b.astype(BF16), (((0,), (0,)), ((), ())), preferred_element_type=F32)


_NN = ((1,), (0,))
_NT = ((1,), (1,))
_TN = ((0,), (0,))


def _dg(a, b, dims):
    return lax.dot_general(a, b, (dims, ((), ())), preferred_element_type=F32)


def _split2(a):
    hi = a.astype(BF16)
    return hi, (a - hi.astype(F32)).astype(BF16)


def _dot3(a, b, dims=_NN):
    ah, al = _split2(a)
    bh, bl = _split2(b)
    return _dg(ah, bh, dims) + _dg(ah, bl, dims) + _dg(al, bh, dims)


def _dot_exact(e, x, dims, e_is_lhs):
    x0 = x.astype(BF16)
    r = x - x0.astype(F32)
    x1 = r.astype(BF16)
    x2 = (r - x1.astype(F32)).astype(BF16)
    eb = e.astype(BF16)
    if e_is_lhs:
        return _dg(eb, x0, dims) + _dg(eb, x1, dims) + _dg(eb, x2, dims)
    return _dg(x0, eb, dims) + _dg(x1, eb, dims) + _dg(x2, eb, dims)


def _rms(xv):
    r = lax.rsqrt(jnp.mean(xv * xv, axis=-1, keepdims=True) + EPS)
    return r, xv * r


def _rms_bwd(dy, xh, r, gain):
    dxh = dy * gain
    return r * (dxh - xh * jnp.mean(dxh * xh, axis=-1, keepdims=True))


def _colsum(v):
    return jnp.sum(v, axis=0, keepdims=True)


def _rows(t, c):
    return pl.BlockSpec((t, c), lambda i: (i, 0))


def _full(shape):
    n = len(shape)
    return pl.BlockSpec(shape, lambda i: (0,) * n)


def _sds(shape, dtype=F32):
    return jax.ShapeDtypeStruct(shape, dtype)


def _ffn_fwd(x, gain, wg, wu, wd, name, comm=None):
    s = x.shape[0]
    tm = min(256, s)

    def body(x_ref, g_ref, wg_ref, wu_ref, wd_ref, xo_ref, ga_ref, gb_ref, f_ref):
        xv = x_ref[...]
        _, xh = _rms(xv)
        h = (xh * g_ref[...]).astype(BF16)
        acc = None
        for j in range(NSH):
            a = jnp.dot(h, wg_ref[j], preferred_element_type=F32)
            b = jnp.dot(h, wu_ref[j], preferred_element_type=F32)
            sa = _sig(a)
            silu = a * sa
            fv = silu * b
            f = fv.astype(BF16)
            f_ref[j] = f
            ga_ref[j] = (sa * b + fv * (1.0 - sa)).astype(BF16)
            gb_ref[j] = silu.astype(BF16)
            part = jnp.dot(f, wd_ref[j], preferred_element_type=F32)
            acc = part if acc is None else acc + part
        xo_ref[...] = xv + 0.5 * acc

    hidden = pl.BlockSpec((NSH, tm, FSH), lambda i: (0, i, 0))
    return _pallas(
        body, comm, name=name, grid=(s // tm,),
        in_specs=[_rows(tm, D), _full((1, D)),
                  pl.BlockSpec((NSH, D, FSH), lambda i: (0, 0, 0), pipeline_mode=pl.Buffered(1)),
                  pl.BlockSpec((NSH, D, FSH), lambda i: (0, 0, 0), pipeline_mode=pl.Buffered(1)),
                  pl.BlockSpec((NSH, FSH, D), lambda i: (0, 0, 0), pipeline_mode=pl.Buffered(1))],
        out_specs=[_rows(tm, D), hidden, hidden, hidden],
        out_shape=[_sds((s, D))] + [_sds((NSH, s, FSH), BF16)] * 3,
        scratch_shapes=[], args=(x, gain, wg, wu, wd))


def _ffn_bwd(x, dout, do, gain, ga, gb, wg, wu, wd, name, comm=None):
    s = x.shape[0]
    tm = min(512, s)

    def hidden(do_ref, ga_ref, gb_ref, wd_ref, da_ref, db_ref):
        dov = do_ref[...]
        for j in range(NSH):
            df = _dot_nt(dov, wd_ref[j])
            da_ref[j] = (df * ga_ref[j].astype(F32)).astype(BF16)
            db_ref[j] = (df * gb_ref[j].astype(F32)).astype(BF16)

    sh = pl.BlockSpec((NSH, tm, FSH), lambda i: (0, i, 0))
    (da, db), carried = _pallas(
        hidden, comm, name=name + "_hidden", grid=(s // tm,),
        in_specs=[_rows(tm, D), sh, sh, pl.BlockSpec((NSH, FSH, D), lambda i: (0, 0, 0), pipeline_mode=pl.Buffered(1))],
        out_specs=[sh, sh], out_shape=[_sds((NSH, s, FSH), BF16)] * 2, scratch_shapes=[], args=(do, ga, gb, wd))

    def inputs(x_ref, d_ref, g_ref, da_ref, db_ref, wg_ref, wu_ref, dx_ref, dg_ref, h_ref):
        @pl.when(pl.program_id(0) == 0)
        def _():
            dg_ref[...] = jnp.zeros_like(dg_ref)

        dh = jnp.zeros((tm, D), F32)
        for j in range(NSH):
            dh = dh + _dot_nt(da_ref[j], wg_ref[j]) + _dot_nt(db_ref[j], wu_ref[j])
        r, xh = _rms(x_ref[...])
        gv = g_ref[...]
        h_ref[...] = (xh * gv).astype(BF16)
        dg_ref[...] += _colsum(dh * xh)
        dx_ref[...] = d_ref[...] + _rms_bwd(dh, xh, r, gv)

    grads = pl.BlockSpec((NSH, tm, FSH), lambda i: (0, i, 0))
    resident = pl.BlockSpec((NSH, D, FSH), lambda i: (0, 0, 0), pipeline_mode=pl.Buffered(1))
    dx, dg, h = pl.pallas_call(
        inputs, name=name + "_input", grid=(s // tm,),
        in_specs=[_rows(tm, D), _rows(tm, D), _full((1, D)), grads, grads, resident, resident],
        out_specs=[_rows(tm, D), _full((1, D)), _rows(tm, D)],
        out_shape=[_sds((s, D)), _sds((1, D)), _sds((s, D), BF16)], compiler_params=_cparams(1),
    )(x, dout, gain, da, db, wg, wu)
    return dx, dg, h, da, db, carried


def _tn(a, b, name, comm=None):
    a_g = a.ndim == 3
    b_g = b.ndim == 3
    g = a.shape[0] if a_g else (b.shape[0] if b_g else 1)
    s, k = a.shape[-2:]
    n = b.shape[-1]
    ts = min(4096, s)
    while ts > 256 and 2 * ts * (k + n) * max(a.dtype.itemsize, b.dtype.itemsize) + 2 * k * n * 4 > TN_VMEM_BUDGET:
        ts //= 2

    def body(a_ref, b_ref, o_ref):
        @pl.when(pl.program_id(1) == 0)
        def _():
            o_ref[...] = jnp.zeros_like(o_ref)

        av = a_ref[0] if a_g else a_ref[...]
        bv = b_ref[0] if b_g else b_ref[...]
        o_ref[0] += _dot_tn(av, bv)

    a_spec = pl.BlockSpec((1, ts, k), lambda gi, si: (gi, si, 0)) if a_g else pl.BlockSpec((ts, k), lambda gi, si: (si, 0))
    b_spec = pl.BlockSpec((1, ts, n), lambda gi, si: (gi, si, 0)) if b_g else pl.BlockSpec((ts, n), lambda gi, si: (si, 0))
    outs, carried = _pallas(body, comm, name=name, grid=(g, s // ts), in_specs=[a_spec, b_spec],
                            out_specs=[pl.BlockSpec((1, k, n), lambda gi, si: (gi, 0, 0))], out_shape=[_sds((g, k, n))],
                            scratch_shapes=[], args=(a, b))
    return outs[0] if comm is None else (outs[0], carried)


_P_WIDTHS = (RGW, RGW, QKVW, ZW, BAP)


def _inproj(x1, gain, ws, name):
    s = x1.shape[0]
    tm = min(ROW_TILE, s)

    def body(x_ref, g_ref, *refs):
        w_refs = refs[:5]
        h_ref = refs[5]
        p_refs = refs[6:]
        _, xh = _rms(x_ref[...])
        h = (xh * g_ref[...]).astype(BF16)
        h_ref[...] = h
        for w_ref, p_ref in zip(w_refs, p_refs):
            p_ref[...] = jnp.dot(h, w_ref[...], preferred_element_type=F32)

    return pl.pallas_call(
        body, name=name, grid=(s // tm,),
        in_specs=[_rows(tm, D), _full((1, D))] + [_full((D, w)) for w in _P_WIDTHS],
        out_specs=[_rows(tm, D)] + [_rows(tm, w) for w in _P_WIDTHS],
        out_shape=[_sds((s, D), BF16)] + [_sds((s, w)) for w in _P_WIDTHS],
        compiler_params=_cparams(1),
    )(x1, gain, *ws)


def _inproj_bwd(x1, dx2, gain, dps, ws, name):
    s = x1.shape[0]
    tm = min(ROW_TILE, s)

    def body(x_ref, d_ref, g_ref, *refs):
        dp_refs = refs[:5]
        w_refs = refs[5:10]
        dx_ref, dxh_ref, dg_ref = refs[10:]

        @pl.when(pl.program_id(0) == 0)
        def _():
            dg_ref[...] = jnp.zeros_like(dg_ref)

        dh = jnp.zeros((tm, D), F32)
        for dp_ref, w_ref in zip(dp_refs, w_refs):
            dh = dh + _dot_nt(dp_ref[...], w_ref[...])
        r, xh = _rms(x_ref[...])
        dg_ref[...] += _colsum(dh * xh)
        dx = d_ref[...] + _rms_bwd(dh, xh, r, g_ref[...])
        dx_ref[...] = dx
        dxh_ref[...] = (0.5 * dx).astype(BF16)

    return pl.pallas_call(
        body, name=name, grid=(s // tm,),
        in_specs=[_rows(tm, D), _rows(tm, D), _full((1, D))] + [_rows(tm, w) for w in _P_WIDTHS]
        + [_full((D, w)) for w in _P_WIDTHS],
        out_specs=[_rows(tm, D), _rows(tm, D), _full((1, D))],
        out_shape=[_sds((s, D)), _sds((s, D), BF16), _sds((1, D))],
        compiler_params=_cparams(1),
    )(x1, dx2, gain, *dps, *ws)


def _halo_specs(s, t, c):
    nb8 = s // 8
    tb = t // 8
    prev = pl.BlockSpec((8, c), lambda i: (jnp.maximum(i * tb - 1, 0), 0))
    nxt = pl.BlockSpec((8, c), lambda i: (jnp.minimum((i + 1) * tb, nb8 - 1), 0))
    return prev, nxt


def _edge_masks(nb):
    i = pl.program_id(0)
    return jnp.where(i > 0, 1.0, 0.0).astype(F32), jnp.where(i < nb - 1, 1.0, 0.0).astype(F32)


def _shifted(xx, off, t):
    n = t + 16
    sh = (-off) % n
    rolled = xx if sh == 0 else pltpu.roll(xx, sh, 0)
    return rolled[8:8 + t]


def _conv(x, w8, bias, name):
    s, c = x.shape
    t = min(ROW_TILE, s)
    nb = s // t

    def body(x_ref, xp_ref, xn_ref, w_ref, b_ref, o_ref):
        pm, nm = _edge_masks(nb)
        for c0 in range(0, c, 512):
            cols = slice(c0, c0 + 512)
            xx = jnp.concatenate([xp_ref[:, cols] * pm, x_ref[:, cols], xn_ref[:, cols] * nm], axis=0)
            acc = jnp.zeros((t, 512), F32) + b_ref[:, cols]
            for j in range(4):
                acc = acc + w_ref[j:j + 1, cols] * _shifted(xx, j - 2, t)
            o_ref[:, cols] = acc

    prev, nxt = _halo_specs(s, t, c)
    return pl.pallas_call(
        body, name=name, grid=(nb,),
        in_specs=[_rows(t, c), prev, nxt, _full((8, c)), _full((1, c))],
        out_specs=_rows(t, c), out_shape=_sds((s, c)), compiler_params=_cparams(1),
    )(x, x, x, w8, bias)


def _conv_bwd(x, dc, w8, name):
    s, c = x.shape
    t = min(ROW_TILE, s)
    nb = s // t

    def body(x_ref, d_ref, dp_ref, dn_ref, w_ref, dx_ref, dw_ref, db_ref):
        @pl.when(pl.program_id(0) == 0)
        def _():
            dw_ref[...] = jnp.zeros_like(dw_ref)
            db_ref[...] = jnp.zeros_like(db_ref)

        pm, nm = _edge_masks(nb)
        for c0 in range(0, c, 512):
            cols = slice(c0, c0 + 512)
            dd = jnp.concatenate([dp_ref[:, cols] * pm, d_ref[:, cols], dn_ref[:, cols] * nm], axis=0)
            xv = x_ref[:, cols]
            acc = jnp.zeros((t, 512), F32)
            for j in range(4):
                dsh = _shifted(dd, 2 - j, t)
                acc = acc + w_ref[j:j + 1, cols] * dsh
                dw_ref[j:j + 1, cols] += _colsum(dsh * xv)
            dx_ref[:, cols] = acc.astype(BF16)
            db_ref[:, cols] += _colsum(d_ref[:, cols])

    prev, nxt = _halo_specs(s, t, c)
    return pl.pallas_call(
        body, name=name, grid=(nb,),
        in_specs=[_rows(t, c), _rows(t, c), prev, nxt, _full((8, c))],
        out_specs=[_rows(t, c), _full((8, c)), _full((1, c))],
        out_shape=[_sds((s, c), BF16), _sds((8, c)), _sds((1, c))], compiler_params=_cparams(1),
    )(x, dc, dc, dc, w8)


def _rg_gates(xc, pre, lam_row):
    sp8 = RG_C * _softplus(-lam_row)
    out = []
    for d in range(2):
        r = _sig_pos(pre[:, RGW * d:RGW * (d + 1)])
        gi = _sig(pre[:, 2 * RGW + RGW * d:2 * RGW + RGW * (d + 1)])
        la = -r * sp8[:, RGW * d:RGW * (d + 1)]
        a = jnp.exp(la)
        mult = jnp.sqrt(_one_minus_sq_exp(la, a))
        out.append((r, gi, a, mult))
    return out


def _mix_prep(c_rg, c_qkv, p_ba, wgates, gbias, lam_row, alog_row, dtb_row, name):
    s = c_rg.shape[0]
    t = min(ROW_TILE, s)

    def body(xc_ref, cq_ref, pc_ref, wg_ref, gb_ref, lam_ref, alog_ref, dtb_ref,
             a0_ref, b0_ref, a1_ref, b1_ref, q_ref, k_ref, v_ref, bg_ref):
        xc = xc_ref[...]
        pre = _dot(xc, wg_ref[...]) + gb_ref[...]
        gates = _rg_gates(xc, pre, lam_ref[...])
        for (r, gi, a, mult), a_ref, b_ref in zip(gates, (a0_ref, a1_ref), (b0_ref, b1_ref)):
            a_ref[...] = a
            b_ref[...] = mult * gi * xc
        cq = cq_ref[...]
        sq = cq * _sig(cq)
        for h in range(NH):
            sl = slice(DH * h, DH * (h + 1))
            qh = sq[:, sl]
            q_ref[:, sl] = qh * lax.rsqrt(jnp.sum(qh * qh, axis=-1, keepdims=True) + EPS) * (DH ** -0.5)
            kh = sq[:, RGW + DH * h:RGW + DH * (h + 1)]
            k_ref[:, sl] = kh * lax.rsqrt(jnp.sum(kh * kh, axis=-1, keepdims=True) + EPS)
        v_ref[...] = sq[:, 2 * RGW:]
        pc = pc_ref[...]
        lane = lax.broadcasted_iota(jnp.int32, pc.shape, 1)
        beta = _sig(pc)
        g = -jnp.exp(alog_ref[...]) * _softplus(pc + dtb_ref[...])
        bg_ref[...] = jnp.where(lane < 8, beta, jnp.where(lane < 16, g, 0.0))

    return pl.pallas_call(
        body, name=name, grid=(s // t,),
        in_specs=[_rows(t, RGW), _rows(t, QKVW), _rows(t, BAP), _full((RGW, 4 * RGW)), _full((1, 4 * RGW)),
                  _full((1, 2 * RGW)), _full((1, BAP)), _full((1, BAP))],
        out_specs=[_rows(t, RGW)] * 7 + [_rows(t, BAP)],
        out_shape=[_sds((s, RGW))] * 7 + [_sds((s, BAP))],
        compiler_params=_cparams(1),
    )(c_rg, c_qkv, p_ba, wgates, gbias, lam_row, alog_row, dtb_row)


def _block_scan(av, bv, row, downwards):
    for k in (1, 2, 4):
        sh = (8 - k) if downwards else k
        m = (row < 8 - k) if downwards else (row >= k)
        a_s = pltpu.roll(av, sh, 0)
        b_s = pltpu.roll(bv, sh, 0)
        bv = jnp.where(m, av * b_s + bv, bv)
        av = jnp.where(m, av * a_s, av)
    return av, bv


def _gates_bwd(xc, wgates, gbias, lam_row, lam0, lam1, hf, hb, name):
    s = xc.shape[0]
    t = min(ROW_TILE, s)
    nb = s // t

    def body(xc_ref, wg_ref, gb_ref, lam_ref, l0_ref, l1_ref, hf_ref, hfp_ref, hfn_ref, hb_ref, hbp_ref, hbn_ref,
             dxc_ref, dpre_ref, xcb_ref, dgb_ref, dlam_ref):
        @pl.when(pl.program_id(0) == 0)
        def _():
            dgb_ref[...] = jnp.zeros_like(dgb_ref)
            dlam_ref[...] = jnp.zeros_like(dlam_ref)

        pm, nm = _edge_masks(nb)
        h_prev = _shifted(jnp.concatenate([hfp_ref[...] * pm, hf_ref[...], hfn_ref[...] * nm], axis=0), -1, t)
        h_next = _shifted(jnp.concatenate([hbp_ref[...] * pm, hb_ref[...], hbn_ref[...] * nm], axis=0), 1, t)
        h_shift = (h_prev, h_next)
        xv = xc_ref[...]
        pre = _dot(xv, wg_ref[...]) + gb_ref[...]
        lam_row_v = lam_ref[...]
        sp8 = RG_C * _softplus(-lam_row_v)
        dsp_dlam = -RG_C * _sig(-lam_row_v)
        gates = _rg_gates(xv, pre, lam_row_v)
        dxc = jnp.zeros((t, RGW), F32)
        dpre_r = []
        dpre_i = []
        for d, ((r, gi, a, mult), l_ref, hs) in enumerate(zip(gates, (l0_ref, l1_ref), h_shift)):
            dbb = l_ref[...]
            da = dbb * hs
            cs = slice(RGW * d, RGW * (d + 1))
            dmult = dbb * gi * xv
            dgi = dbb * mult * xv
            dxc = dxc + dbb * mult * gi
            dla = da * a - dmult * a * a / mult
            dr = -dla * sp8[:, cs]
            dlam_ref[:, cs] += _colsum(-dla * r) * dsp_dlam[:, cs]
            dpre_r.append(dr * r * (1.0 - r))
            dpre_i.append(dgi * gi * (1.0 - gi))
        dpre = jnp.concatenate(dpre_r + dpre_i, axis=1)
        dgb_ref[...] += _colsum(dpre)
        dpre_b = dpre.astype(BF16)
        dpre_ref[...] = dpre_b
        xcb_ref[...] = xv.astype(BF16)
        dxc_ref[...] = dxc + _dot_nt(dpre_b, wg_ref[...])

    prev, nxt = _halo_specs(s, t, RGW)
    return pl.pallas_call(
        body, name=name, grid=(s // t,),
        in_specs=[_rows(t, RGW), _full((RGW, 4 * RGW)), _full((1, 4 * RGW)), _full((1, 2 * RGW))] + [_rows(t, RGW)] * 2
        + [_rows(t, RGW), prev, nxt] * 2,
        out_specs=[_rows(t, RGW), _rows(t, 4 * RGW), _rows(t, RGW), _full((1, 4 * RGW)), _full((1, 2 * RGW))],
        out_shape=[_sds((s, RGW)), _sds((s, 4 * RGW), BF16), _sds((s, RGW), BF16), _sds((1, 4 * RGW)), _sds((1, 2 * RGW))],
        compiler_params=_cparams(1),
    )(xc, wgates, gbias, lam_row, lam0, lam1, hf, hf, hf, hb, hb, hb)


class _GdnMasks:
    def __init__(self, d):
        ri = lax.broadcasted_iota(jnp.int32, (CHUNK, CHUNK), 0)
        ci = lax.broadcasted_iota(jnp.int32, (CHUNK, CHUNK), 1)
        self.incl = (ri >= ci) if d == 0 else (ri <= ci)
        self.strict = (ri > ci) if d == 0 else (ri < ci)
        b16 = jnp.right_shift(ri, 4) == jnp.right_shift(ci, 4)
        b32 = jnp.right_shift(ri, 5) == jnp.right_shift(ci, 5)
        self.diag16 = b16
        self.off32 = jnp.logical_and(b32, jnp.logical_not(b16))
        self.off64 = jnp.logical_not(b32)
        self.eye = jnp.where(ri == ci, 1.0, 0.0).astype(F32)
        self.tri = jnp.where(self.incl, 1.0, 0.0).astype(F32)
        self.last = CHUNK - 1 if d == 0 else 0


def _tri_inv(lmat, m):
    return _tri_inv_many([lmat], [m])[0]


def _tri_inv_many(lmats, masks):
    n = len(lmats)
    ns = [jnp.where(masks[i].diag16, lmats[i], 0.0) for i in range(n)]
    ps = [masks[i].eye - ns[i] for i in range(n)]
    qs = [_dot3(ns[i], ns[i]) for i in range(n)]
    for step in range(3):
        ps = [_dot3(ps[i], masks[i].eye + qs[i]) for i in range(n)]
        if step < 2:
            qs = [_dot3(qs[i], qs[i]) for i in range(n)]
    for off in ("off32", "off64"):
        ts = [_dot3(ps[i], jnp.where(getattr(masks[i], off), lmats[i], 0.0)) for i in range(n)]
        ps = [ps[i] - _dot3(ts[i], ps[i]) for i in range(n)]
    return ps


def _chunk_cumsums(m, bgv):
    return _dot_exact(m.tri, bgv, _NN, True), _dot_exact(m.tri, bgv, ((0,), (1,)), False)


class _GdnHead:
    def __init__(self, qh, kh, vh, kk, q0, bg, gcs, gcs_t, d, h, m):
        cb = 4 * d + h
        cg = 8 + 4 * d + h
        self.q, self.k, self.v = qh, kh, vh
        self.beta = bg[:, cb:cb + 1]
        gcol = gcs[:, cg:cg + 1]
        grow = gcs_t[cg:cg + 1, :]
        gl = gcs[m.last:m.last + 1, cg:cg + 1]
        self.decay = jnp.exp(jnp.where(m.incl, gcol - grow, -1e30))
        self.kb = kh * self.beta
        self.vb = vh * self.beta
        self.a0 = kk * self.beta
        self.q0 = q0
        self.lmat = jnp.where(m.strict, self.a0 * self.decay, 0.0)
        self.attn = self.q0 * self.decay
        self.eg = jnp.exp(gcol)
        self.ek = jnp.exp(gl - gcol)
        self.cd = jnp.exp(gl)
        self.kg = self.kb * self.eg
        self.qd = qh * self.eg
        self.kd = kh * self.ek


HW = NH * DH
SEQ_CB = 4
LOCAL_CB = 4


def _head(h):
    return slice(DH * h, DH * (h + 1))


def _gdn_local_fwd(q, k, v, bg, name):
    s = q.shape[0]
    n = s // CHUNK
    cb = min(LOCAL_CB, n)

    def body(q_ref, k_ref, v_ref, bg_ref, t_ref, u_ref, w_ref, qd_ref, kd_ref, at_ref, cd_ref):
        masks = [_GdnMasks(d) for d in range(2)]
        inst = []
        for jj in range(cb):
            rows = slice(CHUNK * jj, CHUNK * (jj + 1))
            bgv = bg_ref[rows, :]
            qs = [q_ref[rows, _head(h)] for h in range(NH)]
            ks = [k_ref[rows, _head(h)] for h in range(NH)]
            kk = [_dot_nt(ks[h], ks[h]) for h in range(NH)]
            q0 = [_dot_nt(qs[h], ks[h]) for h in range(NH)]
            for d, m in enumerate(masks):
                gcs, gcs_t = _chunk_cumsums(m, bgv)
                for h in range(NH):
                    c = _GdnHead(qs[h], ks[h], v_ref[rows, _head(h)], kk[h], q0[h], bgv, gcs, gcs_t, d, h, m)
                    inst.append((jj, rows, d, h, m, c))
        tms = _tri_inv_many([it[-1].lmat for it in inst], [it[-2] for it in inst])
        for (jj, rows, d, h, m, c), tm in zip(inst, tms):
            sl = _head(h)
            t_ref[jj, d, h] = tm
            u_ref[d, rows, sl] = _dot(tm, c.vb).astype(BF16)
            w_ref[d, rows, sl] = _dot(tm, c.kg).astype(BF16)
            qd_ref[d, rows, sl] = c.qd.astype(BF16)
            kd_ref[d, rows, sl] = c.kd.astype(BF16)
            at_ref[jj, d, h] = c.attn.astype(BF16)
            cd_ref[jj, 4 * d + h:4 * d + h + 1, :] = jnp.broadcast_to(c.cd, (1, DH))

    tok = _rows(cb * CHUNK, HW)
    tok2 = pl.BlockSpec((2, cb * CHUNK, HW), lambda i: (0, i, 0))
    mat = pl.BlockSpec((cb, 2, NH, CHUNK, CHUNK), lambda i: (i, 0, 0, 0, 0))
    return pl.pallas_call(
        body, name=name, grid=(n // cb,), in_specs=[tok, tok, tok, _rows(cb * CHUNK, BAP)],
        out_specs=[mat, tok2, tok2, tok2, tok2, mat, pl.BlockSpec((cb, 8, DH), lambda i: (i, 0, 0))],
        out_shape=[_sds((n, 2, NH, CHUNK, CHUNK)), _sds((2, s, HW), BF16), _sds((2, s, HW), BF16), _sds((2, s, HW), BF16),
                   _sds((2, s, HW), BF16), _sds((n, 2, NH, CHUNK, CHUNK), BF16), _sds((n, 8, DH))],
        compiler_params=_cparams(1),
    )(q, k, v, bg)


def _seq_specs(s, order):
    n = s // CHUNK
    cb = min(SEQ_CB, n)
    nb = n // cb
    tb = cb * CHUNK

    def blk(d):
        return (lambda i: i) if order[d] else (lambda i: nb - 1 - i)

    def per_dir(make):
        return [make(d, blk(d)) for d in range(2)]

    tok2 = per_dir(lambda d, f: pl.BlockSpec((1, tb, HW), lambda i: (d, f(i), 0)))
    tok = per_dir(lambda d, f: pl.BlockSpec((tb, HW), lambda i: (f(i), 0)))
    mat = per_dir(lambda d, f: pl.BlockSpec((cb, 1, NH, CHUNK, CHUNK), lambda i: (f(i), d, 0, 0, 0)))
    cds = per_dir(lambda d, f: pl.BlockSpec((cb, 8, DH), lambda i: (f(i), 0, 0)))
    sts = per_dir(lambda d, f: pl.BlockSpec((cb, NH, DH, DH), lambda i: (f(i), 0, 0, 0)))
    dcd = per_dir(lambda d, f: pl.BlockSpec((cb, NH, DH), lambda i: (f(i), 0, 0)))
    return n, cb, nb, tok2, tok, mat, cds, sts, dcd


class _ScanRider:
    def __init__(self, af, bf, ar, br, shifted, tb, nb, up_spec, down_spec):
        s, c = af.shape
        self.shifted, self.t, self.c, self.nb = shifted, tb, c, nb
        self.args = [af, bf, ar, br]
        self.in_specs = [up_spec, up_spec, down_spec, down_spec]
        self.scratch = [pltpu.VMEM((16, c), F32)]
        if shifted:
            tb8 = tb // 8
            self.args += [af, ar]
            self.in_specs += [pl.BlockSpec((8, c), lambda i: (jnp.maximum(i * tb8 - 1, 0), 0)),
                              pl.BlockSpec((8, c), lambda i: (jnp.minimum((nb - i) * tb8, s // 8 - 1), 0))]
            self.scratch += [pltpu.VMEM((tb + 8, c), F32), pltpu.VMEM((tb + 8, c), F32)]
        self.out_specs = [up_spec, down_spec]
        self.out_shape = [_sds((s, c)), _sds((s, c))]

    def begin(self, in_refs, out_refs, scratch_refs):
        i = pl.program_id(0)
        self.carry = scratch_refs[0]

        @pl.when(i == 0)
        def _():
            self.carry[...] = jnp.zeros_like(self.carry)

        af_ref, self.bf_ref, ar_ref, self.br_ref = in_refs[0:4]
        self.hf_ref, self.hr_ref = out_refs
        self.a_up, self.a_dn = af_ref, ar_ref
        if self.shifted:
            t = self.t
            edge = jnp.where(i > 0, 1.0, 0.0).astype(F32)
            fbuf, rbuf = scratch_refs[1:3]
            fbuf[0:8, :] = in_refs[4][...] * edge
            fbuf[8:t + 8, :] = af_ref[...]
            rbuf[0:t, :] = ar_ref[...]
            rbuf[t:t + 8, :] = in_refs[5][...] * edge
            self.a_up, self.a_dn = fbuf, rbuf
        self.row = lax.broadcasted_iota(jnp.int32, (8, self.c), 0)
        self.cf, self.cr = self.carry[0:1, :], self.carry[8:9, :]

    def groups(self, lo, hi):
        ng = self.t // 8
        row = self.row
        for gi in range(lo, hi):
            rf, rr = 8 * gi, 8 * (ng - 1 - gi)
            if self.shifted:
                a_f = jnp.where(row > 0, pltpu.roll(self.a_up[rf + 8:rf + 16, :], 1, 0), pltpu.roll(self.a_up[rf:rf + 8, :], 1, 0))
                a_r = jnp.where(row < 7, pltpu.roll(self.a_dn[rr:rr + 8, :], 7, 0), pltpu.roll(self.a_dn[rr + 8:rr + 16, :], 7, 0))
            else:
                a_f, a_r = self.a_up[rf:rf + 8, :], self.a_dn[rr:rr + 8, :]
            a_f, b_f = _block_scan(a_f, self.bf_ref[rf:rf + 8, :], row, False)
            a_r, b_r = _block_scan(a_r, self.br_ref[rr:rr + 8, :], row, True)
            h_f = a_f * self.cf + b_f
            h_r = a_r * self.cr + b_r
            self.hf_ref[rf:rf + 8, :] = h_f
            self.hr_ref[rr:rr + 8, :] = h_r
            self.cf, self.cr = h_f[7:8, :], h_r[0:1, :]

    def end(self):
        self.carry[0:1, :] = self.cf
        self.carry[8:9, :] = self.cr


def _gdn_seq_fwd(u, w, qd, kd, at, cd, name, scan=None):
    s = u.shape[1]
    n, cb, nb, tok2, tok, mat, cds, sts, _ = _seq_specs(s, (True, False))
    rider = _ScanRider(*scan, False, cb * CHUNK, nb, tok[0], tok[1]) if scan else None
    ri = len(rider.args) if rider else 0

    def body(*refs):
        ins = (refs[0:6], refs[6:12])
        outs = (refs[12 + ri:15 + ri], refs[15 + ri:18 + ri])
        st = refs[18 + ri + (2 if rider else 0)]
        if rider:
            rider.begin(refs[12:12 + ri], refs[18 + ri:20 + ri], refs[21 + ri:])

        @pl.when(pl.program_id(0) == 0)
        def _():
            st[...] = jnp.zeros_like(st)

        for j in range(cb):
            items = []
            for d in range(2):
                jj = j if d == 0 else cb - 1 - j
                items += [(d, h, jj, slice(CHUNK * jj, CHUNK * (jj + 1)), _head(h)) for h in range(NH)]
            shs = [st[d, h] for d, h, _, _, _ in items]
            wss = [_dot(ins[d][1][0, rows, sl], sh) for (d, h, jj, rows, sl), sh in zip(items, shs)]
            vns = [ins[d][0][0, rows, sl].astype(F32) - ws for (d, h, jj, rows, sl), ws in zip(items, wss)]
            news = [sh * ins[d][5][jj, 4 * d + h:4 * d + h + 1, :] + _dot_tn(ins[d][3][0, rows, sl], vn)
                    for (d, h, jj, rows, sl), sh, vn in zip(items, shs, vns)]
            for (d, h, jj, rows, sl), sh, vn, new in zip(items, shs, vns, news):
                o_r, s_r, vn_r = outs[d]
                st[d, h] = new
                s_r[jj, h] = sh.astype(BF16)
                vn_r[rows, sl] = vn.astype(BF16)
                o_r[rows, sl] = _dot(ins[d][2][0, rows, sl], sh) + _dot(ins[d][4][jj, 0, h], vn)
            if rider:
                rider.groups(8 * j, 8 * (j + 1))
        if rider:
            rider.end()

    in_specs, out_specs, out_shape = [], [], []
    for d in range(2):
        in_specs += [tok2[d]] * 4 + [mat[d], cds[d]]
        out_specs += [tok[d], sts[d], tok[d]]
        out_shape += [_sds((s, HW)), _sds((n, NH, DH, DH), BF16), _sds((s, HW), BF16)]
    args = [u, w, qd, kd, at, cd, u, w, qd, kd, at, cd]
    scratch = [pltpu.VMEM((2, NH, DH, DH), F32)]
    if rider:
        in_specs, args = in_specs + rider.in_specs, args + rider.args
        out_specs, out_shape, scratch = out_specs + rider.out_specs, out_shape + rider.out_shape, scratch + rider.scratch
    return pl.pallas_call(
        body, name=name, grid=(nb,), in_specs=in_specs, out_specs=out_specs, out_shape=out_shape,
        scratch_shapes=scratch, compiler_params=_cparams(1),
    )(*args)


def _gdn_seq_bwd(do, w, qd, kd, at, cd, states, vns, name, scan=None):
    s = do.shape[0]
    n, cb, nb, tok2, tok, mat, cds, sts, dcd = _seq_specs(s, (False, True))
    rider = _ScanRider(*scan, True, cb * CHUNK, nb, tok[1], tok[0]) if scan else None
    ri = len(rider.args) if rider else 0

    def body(*refs):
        ins = (refs[0:8], refs[8:16])
        outs = (refs[16 + ri:21 + ri], refs[21 + ri:26 + ri])
        dst = refs[26 + ri + (2 if rider else 0)]
        if rider:
            rider.begin(refs[16:16 + ri], refs[26 + ri:28 + ri], refs[29 + ri:])

        @pl.when(pl.program_id(0) == 0)
        def _():
            dst[...] = jnp.zeros_like(dst)

        for j in range(cb):
            items = []
            for d in range(2):
                jj = cb - 1 - j if d == 0 else j
                items += [(d, h, jj, slice(CHUNK * jj, CHUNK * (jj + 1)), _head(h)) for h in range(NH)]
            dsns = [dst[d, h] for d, h, _, _, _ in items]
            dohs = [ins[d][0][rows, sl] for d, h, jj, rows, sl in items]
            d_vns = [_dot_tn(ins[d][4][jj, 0, h], doh) + _dot(ins[d][3][0, rows, sl], dsn)
                     for (d, h, jj, rows, sl), doh, dsn in zip(items, dohs, dsns)]
            news = [ins[d][5][jj, 4 * d + h:4 * d + h + 1, :] * dsn + _dot_tn(ins[d][2][0, rows, sl], doh)
                    - _dot_tn(ins[d][1][0, rows, sl], d_vn)
                    for (d, h, jj, rows, sl), doh, dsn, d_vn in zip(items, dohs, dsns, d_vns)]
            for (d, h, jj, rows, sl), doh, dsn, d_vn, new in zip(items, dohs, dsns, d_vns, news):
                dvn_r, dkd_r, dqd_r, dw_r, dcd_r = outs[d]
                sh = ins[d][6][jj, h].astype(F32)
                dst[d, h] = new
                dvn_r[rows, sl] = d_vn.astype(BF16)
                dkd_r[rows, sl] = _dot_nt(ins[d][7][rows, sl], dsn)
                dqd_r[rows, sl] = _dot_nt(doh, sh)
                dw_r[rows, sl] = (-_dot_nt(d_vn, sh)).astype(BF16)
                d_cd = jnp.sum(jnp.sum(sh * dsn, axis=1, keepdims=True), axis=0, keepdims=True)
                dcd_r[jj, h:h + 1, :] = jnp.broadcast_to(d_cd, (1, DH))
            if rider:
                rider.groups(8 * j, 8 * (j + 1))
        if rider:
            rider.end()

    in_specs, out_specs, out_shape, args = [], [], [], []
    for d in range(2):
        in_specs += [tok[d]] + [tok2[d]] * 3 + [mat[d], cds[d], sts[d], tok[d]]
        args += [do, w, qd, kd, at, cd, states[d], vns[d]]
        out_specs += [tok[d]] * 4 + [dcd[d]]
        out_shape += [_sds((s, HW), BF16), _sds((s, HW)), _sds((s, HW)), _sds((s, HW), BF16), _sds((n, NH, DH))]
    scratch = [pltpu.VMEM((2, NH, DH, DH), F32)]
    if rider:
        in_specs, args = in_specs + rider.in_specs, args + rider.args
        out_specs, out_shape, scratch = out_specs + rider.out_specs, out_shape + rider.out_shape, scratch + rider.scratch
    return pl.pallas_call(
        body, name=name, grid=(nb,), in_specs=in_specs, out_specs=out_specs, out_shape=out_shape,
        scratch_shapes=scratch, compiler_params=_cparams(1),
    )(*args)


def _gdn_local_bwd(q, k, v, bg, tmat, do, vns, seq_grads, name, comm=None):
    s = q.shape[0]
    n = s // CHUNK
    cb = min(LOCAL_CB, n)

    def body(*refs):
        q_ref, k_ref, v_ref, bg_ref, t_ref, do_ref = refs[0:6]
        vn_refs = refs[6:8]
        sg = (refs[8:13], refs[13:18])
        dq_ref, dk_ref, dv_ref, dbg_ref = refs[18:]
        lane = lax.broadcasted_iota(jnp.int32, (CHUNK, BAP), 1)
        rowi = lax.broadcasted_iota(jnp.int32, (CHUNK, 1), 0)
        ones = jnp.ones((CHUNK, DH), F32)
        masks = [_GdnMasks(d) for d in range(2)]
        inst = []
        for jj in range(cb):
            rows = slice(CHUNK * jj, CHUNK * (jj + 1))
            bgv = bg_ref[rows, :]
            qs = [q_ref[rows, _head(h)] for h in range(NH)]
            ks = [k_ref[rows, _head(h)] for h in range(NH)]
            kk = [_dot_nt(ks[h], ks[h]) for h in range(NH)]
            q0 = [_dot_nt(qs[h], ks[h]) for h in range(NH)]
            for d, m in enumerate(masks):
                gcs, gcs_t = _chunk_cumsums(m, bgv)
                for h in range(NH):
                    c = _GdnHead(qs[h], ks[h], v_ref[rows, _head(h)], kk[h], q0[h], bgv, gcs, gcs_t, d, h, m)
                    inst.append((jj, rows, d, h, m, c))
        ni = len(inst)
        cs = [it[-1] for it in inst]
        tms = [t_ref[jj, d, h] for jj, _, d, h, _, _ in inst]
        d_vns = [sg[d][0][rows, _head(h)] for _, rows, d, h, _, _ in inst]
        d_ws = [sg[d][3][rows, _head(h)] for _, rows, d, h, _, _ in inst]
        d_ts = [_dot_nt(d_vns[i], cs[i].vb) + _dot_nt(d_ws[i], cs[i].kg) for i in range(ni)]
        tts = [tm.T for tm in tms]
        xs = [_dot3(tts[i], d_ts[i]) for i in range(ni)]
        d_ls = [jnp.where(inst[i][4].strict, -_dot3(xs[i], tts[i]), 0.0) for i in range(ni)]
        d_attns = [jnp.where(m.incl, _dot_nt(do_ref[rows, _head(h)], vn_refs[d][rows, _head(h)]), 0.0)
                   for _, rows, d, h, m, _ in inst]
        d_vbs = [_dot(tts[i], d_vns[i]) for i in range(ni)]
        d_kgs = [_dot(tts[i], d_ws[i]) for i in range(ni)]
        d_a0s = [d_ls[i] * cs[i].decay for i in range(ni)]
        d_q0s = [d_attns[i] * cs[i].decay for i in range(ni)]
        es = [(d_ls[i] * cs[i].a0 + d_attns[i] * cs[i].q0) * cs[i].decay for i in range(ni)]
        kb_mm = [_dot(d_a0s[i], cs[i].k) for i in range(ni)]
        q_mm = [_dot(d_q0s[i], cs[i].k) for i in range(ni)]
        k_mm = [_dot_tn(d_a0s[i], cs[i].kb) + _dot_tn(d_q0s[i], cs[i].q) for i in range(ni)]
        e_cols = [_dot_exact(ones, es[i], _TN, False)[:, 0:1] for i in range(ni)]
        acc = {}
        d_gcs, d_betas = [], []
        for i, (jj, rows, d, h, m, c) in enumerate(inst):
            sl = _head(h)
            d_kd, d_qd = sg[d][1][rows, sl], sg[d][2][rows, sl]
            d_cd = sg[d][4][jj, h:h + 1, 0:1]
            d_vb, d_kg = d_vbs[i], d_kgs[i]
            d_kb = kb_mm[i] + d_kg * c.eg
            parts = (q_mm[i] + d_qd * c.eg, k_mm[i] + d_kd * c.ek + d_kb * c.beta, d_vb * c.beta)
            acc[jj, h] = [p + a for a, p in zip(acc[jj, h], parts)] if (jj, h) in acc else list(parts)
            kd_term = d_kd * c.kd
            d_gc = (jnp.sum(d_kg * c.kg + d_qd * c.qd - kd_term, axis=1, keepdims=True)
                    + jnp.sum(es[i], axis=1, keepdims=True) - e_cols[i])
            d_gl = jnp.sum(jnp.sum(kd_term, axis=0, keepdims=True), axis=1, keepdims=True) + d_cd * c.cd
            d_gcs.append(d_gc + jnp.where(rowi == m.last, d_gl, 0.0))
            d_betas.append(jnp.sum(d_kb * c.k + d_vb * c.v, axis=1, keepdims=True))
        d_gs = [_dot_exact(inst[i][4].tri, d_gcs[i] * ones, _TN, True)[:, 0:1] for i in range(ni)]
        dbg = [jnp.zeros((CHUNK, BAP), F32) for _ in range(cb)]
        for i, (jj, _, d, h, _, _) in enumerate(inst):
            dbg[jj] = dbg[jj] + jnp.where(lane == 4 * d + h, d_betas[i], 0.0) + jnp.where(lane == 8 + 4 * d + h, d_gs[i], 0.0)
        for jj in range(cb):
            rows = slice(CHUNK * jj, CHUNK * (jj + 1))
            for h in range(NH):
                dq_ref[rows, _head(h)], dk_ref[rows, _head(h)], dv_ref[rows, _head(h)] = acc[jj, h]
            dbg_ref[rows, :] = dbg[jj]

    tok = _rows(cb * CHUNK, HW)
    bgs = _rows(cb * CHUNK, BAP)
    mat = pl.BlockSpec((cb, 2, NH, CHUNK, CHUNK), lambda i: (i, 0, 0, 0, 0))
    dcd = pl.BlockSpec((cb, NH, DH), lambda i: (i, 0, 0))
    args = [q, k, v, bg, tmat, do, vns[0], vns[1]]
    in_specs = [tok, tok, tok, bgs, mat, tok, tok, tok]
    for d in range(2):
        args += list(seq_grads[d])
        in_specs += [tok] * 4 + [dcd]
    return _pallas(body, comm, name=name, grid=(n // cb,), in_specs=in_specs, out_specs=[tok, tok, tok, bgs],
                   out_shape=[_sds((s, HW))] * 3 + [_sds((s, BAP))], scratch_shapes=[], args=args)


def _prep_bwd(c_qkv, p_ba, alog_row, dtb_row, dq, dk, dv, dbg, name):
    s = c_qkv.shape[0]
    t = min(ROW_TILE, s)

    def body(cq_ref, pc_ref, alog_ref, dtb_ref, dq_ref, dk_ref, dv_ref, dbg_ref,
             dcq_ref, dpc_ref, dalog_ref, ddtb_ref):
        @pl.when(pl.program_id(0) == 0)
        def _():
            dalog_ref[...] = jnp.zeros_like(dalog_ref)
            ddtb_ref[...] = jnp.zeros_like(ddtb_ref)

        cq = cq_ref[...]
        sq = cq * _sig(cq)
        sg = _silu_grad(cq)
        for h in range(NH):
            sl = slice(DH * h, DH * (h + 1))
            for off, d_ref, scale in ((0, dq_ref, DH ** -0.5), (RGW, dk_ref, 1.0)):
                csl = slice(off + DH * h, off + DH * (h + 1))
                xh = sq[:, csl]
                nrm = lax.rsqrt(jnp.sum(xh * xh, axis=-1, keepdims=True) + EPS)
                y = xh * nrm
                dy = d_ref[:, sl] * scale
                dcq_ref[:, csl] = nrm * (dy - y * jnp.sum(dy * y, axis=-1, keepdims=True)) * sg[:, csl]
        dcq_ref[:, 2 * RGW:] = dv_ref[...] * sg[:, 2 * RGW:]
        pc = pc_ref[...]
        lane = lax.broadcasted_iota(jnp.int32, pc.shape, 1)
        dbg = dbg_ref[...]
        beta = _sig(pc)
        ea = jnp.exp(alog_ref[...])
        z = pc + dtb_ref[...]
        g = -ea * _softplus(z)
        is_g = jnp.logical_and(lane >= 8, lane < 16)
        d_alpha = jnp.where(is_g, dbg * (-ea) * _sig(z), 0.0)
        dpc_ref[...] = jnp.where(lane < 8, dbg * beta * (1.0 - beta), d_alpha).astype(BF16)
        dalog_ref[...] += _colsum(jnp.where(is_g, dbg * g, 0.0))
        ddtb_ref[...] += _colsum(d_alpha)

    return pl.pallas_call(
        body, name=name, grid=(s // t,),
        in_specs=[_rows(t, QKVW), _rows(t, BAP), _full((1, BAP)), _full((1, BAP))] + [_rows(t, HW)] * 3 + [_rows(t, BAP)],
        out_specs=[_rows(t, QKVW), _rows(t, BAP), _full((1, BAP)), _full((1, BAP))],
        out_shape=[_sds((s, QKVW)), _sds((s, BAP), BF16), _sds((1, BAP)), _sds((1, BAP))],
        compiler_params=_cparams(1),
    )(c_qkv, p_ba, alog_row, dtb_row, dq, dk, dv, dbg)


def _mix_out_values(hf, hb, gate, of, ob, z, gn):
    hr = hf + hb
    y_rg = hr * _gelu(gate)
    osum = of + ob
    parts = []
    for h in range(NH):
        sl = slice(DH * h, DH * (h + 1))
        oh = osum[:, sl]
        r, ohat = _rms(oh)
        zh = z[:, sl]
        parts.append((r, ohat, zh))
    y_gdn = jnp.concatenate([ohat * gn * (zh * _sig(zh)) for (r, ohat, zh) in parts], axis=1)
    return hr, y_rg, y_gdn, parts


def _outproj(x1, hf, hb, gate, of, ob, z, gn, wout, name):
    s = x1.shape[0]
    t = min(ROW_TILE, s)

    def body(x_ref, hf_ref, hb_ref, gate_ref, of_ref, ob_ref, z_ref, gn_ref, w_ref, xo_ref, y_ref):
        _, y_rg, y_gdn, _ = _mix_out_values(hf_ref[...], hb_ref[...], gate_ref[...], of_ref[...], ob_ref[...],
                                            z_ref[...], gn_ref[...])
        y = jnp.concatenate([y_rg, y_gdn], axis=1).astype(BF16)
        y_ref[...] = y
        xo_ref[...] = x_ref[...] + jnp.dot(y, w_ref[...], preferred_element_type=F32)

    return pl.pallas_call(
        body, name=name, grid=(s // t,),
        in_specs=[_rows(t, D)] + [_rows(t, RGW)] * 6 + [_full((1, DH)), _full((D, D))],
        out_specs=[_rows(t, D), _rows(t, D)], out_shape=[_sds((s, D)), _sds((s, D), BF16)],
        compiler_params=_cparams(1),
    )(x1, hf, hb, gate, of, ob, z, gn, wout)


def _outproj_bwd(dx2, hf, hb, gate, of, ob, z, gn, wout, name, comm=None):
    s = dx2.shape[0]
    t = min(ROW_TILE, s)

    def body(d_ref, hf_ref, hb_ref, gate_ref, of_ref, ob_ref, z_ref, gn_ref, w_ref,
             dhr_ref, dgate_ref, dos_ref, dz_ref, dgn_ref, db_ref):
        @pl.when(pl.program_id(0) == 0)
        def _():
            dgn_ref[...] = jnp.zeros_like(dgn_ref)

        gate = gate_ref[...]
        gn_v = gn_ref[...]
        hr, _, _, parts = _mix_out_values(hf_ref[...], hb_ref[...], gate, of_ref[...], ob_ref[...], z_ref[...], gn_v)
        dbf = d_ref[...].astype(BF16)
        db_ref[...] = dbf
        dy = _dot_nt(dbf, w_ref[...])
        dyr = dy[:, :RGW]
        dhr_ref[...] = dyr * _gelu(gate)
        dgate_ref[...] = (dyr * hr * _gelu_grad(gate)).astype(BF16)
        dgn = jnp.zeros((1, DH), F32)
        for h, (r, ohat, zh) in enumerate(parts):
            sl = slice(DH * h, DH * (h + 1))
            dyh = dy[:, RGW + DH * h:RGW + DH * (h + 1)]
            sz = zh * _sig(zh)
            dn = dyh * sz
            dz_ref[:, sl] = (dyh * ohat * gn_v * _silu_grad(zh)).astype(BF16)
            dgn = dgn + _colsum(dn * ohat)
            dos_ref[:, sl] = _rms_bwd(dn, ohat, r, gn_v).astype(BF16)
        dgn_ref[...] += dgn

    return _pallas(
        body, comm, name=name, grid=(s // t,),
        in_specs=[_rows(t, D)] + [_rows(t, RGW)] * 6 + [_full((1, DH)), _full((D, D))],
        out_specs=[_rows(t, RGW)] * 4 + [_full((1, DH)), _rows(t, D)],
        out_shape=[_sds((s, RGW))] + [_sds((s, RGW), BF16)] * 3 + [_sds((1, DH)), _sds((s, D), BF16)],
        scratch_shapes=[], args=(dx2, hf, hb, gate, of, ob, z, gn, wout))


def _loss_head(x3, target, gain, name):
    s = x3.shape[0]
    t = min(ROW_TILE, s)

    def body(x_ref, t_ref, g_ref, dx_ref, dxh_ref, loss_ref, dg_ref):
        @pl.when(pl.program_id(0) == 0)
        def _():
            loss_ref[...] = jnp.zeros_like(loss_ref)
            dg_ref[...] = jnp.zeros_like(dg_ref)

        r, xh = _rms(x_ref[...])
        gv = g_ref[...]
        err = xh * gv - t_ref[...]
        per_tok = jnp.mean(err * err, axis=-1, keepdims=True)
        loss_ref[...] += 0.5 * jnp.sum(per_tok, axis=0, keepdims=True)
        dy = err * (1.0 / D)
        dg_ref[...] += _colsum(dy * xh)
        dx = _rms_bwd(dy, xh, r, gv)
        dx_ref[...] = dx
        dxh_ref[...] = (0.5 * dx).astype(BF16)

    return pl.pallas_call(
        body, name=name, grid=(s // t,), in_specs=[_rows(t, D), _rows(t, D), _full((1, D))],
        out_specs=[_rows(t, D), _rows(t, D), _full((8, 128)), _full((1, D))],
        out_shape=[_sds((s, D)), _sds((s, D), BF16), _sds((8, 128)), _sds((1, D))], compiler_params=_cparams(1),
    )(x3, target, gain)


def _adamw_math(wv, gv, mv, vv):
    mn = ADAM_B1 * mv + (1.0 - ADAM_B1) * gv
    vn = ADAM_B2 * vv + (1.0 - ADAM_B2) * (gv * gv)
    m_hat = mn / (1.0 - ADAM_B1 ** ADAM_STEP)
    v_hat = vn / (1.0 - ADAM_B2 ** ADAM_STEP)
    return -ADAM_LR * (m_hat / (jnp.sqrt(v_hat) + ADAM_EPS) + ADAM_WD * wv), mn, vn


def _row_tile(r, c):
    tr = r
    while tr * c * 4 > (1 << 20) and tr % 16 == 0:
        tr //= 2
    return tr


def _adamw(w, g, m, v, name):
    r, c = w.shape
    tr = _row_tile(r, c)

    def body(w_ref, g_ref, m_ref, v_ref, d_ref, nm_ref, nv_ref):
        d_ref[...], nm_ref[...], nv_ref[...] = _adamw_math(w_ref[...], g_ref[...], m_ref[...], v_ref[...])

    return pl.pallas_call(
        body, name=name, grid=(r // tr,), in_specs=[_rows(tr, c)] * 4, out_specs=[_rows(tr, c)] * 3,
        out_shape=[_sds((r, c))] * 3, compiler_params=_cparams(1),
    )(w, g, m, v)


def _adamw_halves(w, own, recv, m, v, c_arr, name):
    r, c = w.shape
    h = r // 2
    tr = _row_tile(h, c)
    nh = h // tr

    def body(c_ref, w_ref, own_ref, recv_ref, m_ref, v_ref, g_ref, d_ref, nm_ref, nv_ref):
        first_half = pl.program_id(0) < nh
        use_own = first_half == (c_ref[0] == 0)

        def chip_sum(ref):
            acc = ref[0].astype(F32)
            for k in range(1, NSH):
                acc = acc + ref[k].astype(F32)
            return acc

        gv = jnp.where(use_own, chip_sum(own_ref), chip_sum(recv_ref))
        g_ref[...] = gv
        d_ref[...], nm_ref[...], nv_ref[...] = _adamw_math(w_ref[...], gv, m_ref[...], v_ref[...])

    full = pl.BlockSpec((tr, c), lambda i, c_ref: (i, 0))
    half = pl.BlockSpec((NSH, tr, c), lambda i, c_ref: (0, i % nh, 0))
    return pl.pallas_call(
        body, name=name, out_shape=[_sds((r, c))] * 4,
        grid_spec=pltpu.PrefetchScalarGridSpec(
            num_scalar_prefetch=1, grid=(2 * nh,), in_specs=[full, half, half, full, full], out_specs=[full] * 4),
        compiler_params=_cparams(1),
    )(c_arr, w, own, recv, m, v)


def _mesh_pos():
    return lax.axis_index("x"), lax.axis_index("y"), lax.axis_index("c")


def _other_chips(x, y):
    return [(1 - x, y), (x, 1 - y), (1 - x, 1 - y)]


class _Comm:
    def __init__(self, inputs, out_shapes, scratch, start, finish, space=pltpu.HBM, relay=None):
        self.inputs, self.out_shapes, self.scratch = list(inputs), list(out_shapes), list(scratch)
        self.start, self.finish, self.space = start, finish, space
        self.relay = relay if relay is not None else (lambda ins, outs, sems: None)


def _comm_call(comm, name):
    ni, no = len(comm.inputs), len(comm.out_shapes)

    def body(*refs):
        comm.start(refs[:ni], refs[ni:ni + no], refs[ni + no:])
        comm.relay(refs[:ni], refs[ni:ni + no], refs[ni + no:])
        comm.finish(refs[:ni], refs[ni:ni + no], refs[ni + no:])

    spec = pl.BlockSpec(memory_space=comm.space)
    return list(pl.pallas_call(body, name=name, out_shape=comm.out_shapes, in_specs=[spec] * ni, out_specs=[spec] * no,
                               scratch_shapes=comm.scratch)(*comm.inputs))


def _join_comm(a, b):
    ia, oa, sa = len(a.inputs), len(a.out_shapes), len(a.scratch)

    def both(method):
        def run(ins, outs, sems):
            getattr(a, method)(ins[:ia], outs[:oa], sems[:sa])
            getattr(b, method)(ins[ia:], outs[oa:], sems[sa:])
        return run

    return _Comm(a.inputs + b.inputs, a.out_shapes + b.out_shapes, a.scratch + b.scratch, both("start"), both("finish"),
                 relay=both("relay"))


def _pallas(body, comm, *, name, grid, in_specs, out_specs, out_shape, scratch_shapes, args):
    params = _cparams(len(grid))
    if comm is None:
        outs = pl.pallas_call(body, name=name, grid=grid, in_specs=in_specs, out_specs=out_specs, out_shape=out_shape,
                              scratch_shapes=scratch_shapes, compiler_params=params)(*args)
        return list(outs), []
    n_in, n_out, n_sc = len(in_specs), len(out_specs), len(scratch_shapes)
    ci, co = len(comm.inputs), len(comm.out_shapes)

    def carried(*refs):
        bounds = [0, n_in, n_in + ci, n_in + ci + n_out, n_in + ci + n_out + co, n_in + ci + n_out + co + n_sc, len(refs)]
        ins, cins, outs, couts, scr, csems = [refs[lo:hi] for lo, hi in zip(bounds[:-1], bounds[1:])]
        ids = [pl.program_id(k) for k in range(len(grid))]
        first = functools.reduce(jnp.logical_and, [i == 0 for i in ids])
        last = functools.reduce(jnp.logical_and, [i == g - 1 for i, g in zip(ids, grid)])
        late = functools.reduce(jnp.logical_and, [ids[0] == (3 * grid[0]) // 4] + [i == 0 for i in ids[1:]])

        @pl.when(first)
        def _():
            comm.start(cins, couts, csems)

        body(*ins, *outs, *scr)

        @pl.when(late)
        def _():
            comm.relay(cins, couts, csems)

        @pl.when(last)
        def _():
            comm.finish(cins, couts, csems)

    hbm = pl.BlockSpec(memory_space=pltpu.HBM)
    outs = pl.pallas_call(
        carried, name=name, grid=grid, in_specs=list(in_specs) + [hbm] * ci, out_specs=list(out_specs) + [hbm] * co,
        out_shape=list(out_shape) + comm.out_shapes, scratch_shapes=list(scratch_shapes) + comm.scratch,
        compiler_params=params)(*args, *comm.inputs)
    return list(outs[:n_out]), list(outs[n_out:])


def _gather_comm(arrays, space, block_rows):
    n_arr = len(arrays)

    def plan(x_refs, out_refs, sems):
        send_sems, recv_sems, local_sems = sems
        x, y, c = _mesh_pos()
        me, sibling = (x, y, c), (x, y, 1 - c)
        chips = _other_chips(x, y)

        def slot(a, px, py, pc):
            return out_refs[a].at[4 * px + 2 * py + pc]

        def copy(a, k, block, to, src=None):
            return pltpu.make_async_remote_copy(
                src_ref=slot(a, *block) if src is None else src, dst_ref=slot(a, *block),
                send_sem=send_sems.at[7 * a + k], recv_sem=recv_sems.at[7 * a + k], device_id=to, device_id_type=MESH)

        srcs = [x_refs[a] if block_rows[a] is None else
                x_refs[a].at[pl.ds(pl.multiple_of(c * block_rows[a], 16), block_rows[a]), :] for a in range(n_arr)]
        local = [pltpu.make_async_copy(srcs[a], slot(a, *me), local_sems.at[a]) for a in range(n_arr)]
        first = []
        for a in range(n_arr):
            first += [copy(a, 1 + j, me, (*chip, c), src=srcs[a]) for j, chip in enumerate(chips)]
            first.append(copy(a, 0, me, sibling, src=srcs[a]))
        return me, sibling, chips, c, copy, local, first

    def start(x_refs, out_refs, sems):
        _, _, _, _, _, local, first = plan(x_refs, out_refs, sems)
        for cp in local + first:
            cp.start()

    def relay(x_refs, out_refs, sems):
        me, sibling, chips, c, copy, _, _ = plan(x_refs, out_refs, sems)
        for j, chip in enumerate(chips):
            for a in range(n_arr):
                copy(a, 1 + j, (*chip, c), me).wait_recv()
                copy(a, 4 + j, (*chip, c), sibling).start()

    def finish(x_refs, out_refs, sems):
        me, sibling, chips, c, copy, local, first = plan(x_refs, out_refs, sems)
        passed = [copy(a, 4 + j, (*chip, c), sibling) for j, chip in enumerate(chips) for a in range(n_arr)]
        for a in range(n_arr):
            copy(a, 0, sibling, me).wait_recv()
            for j, chip in enumerate(chips):
                copy(a, 4 + j, (*chip, 1 - c), me).wait_recv()
        for cp in first + passed:
            cp.wait_send()
        for cp in local:
            cp.wait()

    out_shapes = [_sds((8, w.shape[0] if r is None else r) + w.shape[1:], w.dtype) for w, r in zip(arrays, block_rows)]
    scratch = [pltpu.SemaphoreType.DMA((7 * n_arr,)), pltpu.SemaphoreType.DMA((7 * n_arr,)), pltpu.SemaphoreType.DMA((n_arr,))]
    return _Comm(arrays, out_shapes, scratch, start, finish, space, relay=relay)


def _weights_gather_comm(shards):
    return _gather_comm(shards, pltpu.HBM, [w.shape[0] // 2 for w in shards])


def _all_shards(gathered):
    return [o.reshape(NSH, 2 * o.shape[1], o.shape[2]) for o in gathered]


def _gather_small(block, name):
    return _comm_call(_gather_comm([block], pltpu.VMEM, [None]), name)[0]


def _exchange_comm(gs):
    n = len(gs)
    halves = [g.shape[1] // 2 for g in gs]

    def plan(g_refs, land_refs, sems):
        send_sems, recv_sems = sems
        x, y, c = _mesh_pos()
        copies = []
        for a in range(n):
            h = halves[a]
            for s in range(NSH):
                copies.append(pltpu.make_async_remote_copy(
                    src_ref=g_refs[a].at[s, pl.ds(pl.multiple_of((1 - c) * h, 8), h), :], dst_ref=land_refs[a].at[s],
                    send_sem=send_sems.at[NSH * a + s], recv_sem=recv_sems.at[NSH * a + s],
                    device_id=(x, y, 1 - c), device_id_type=MESH))
        return copies

    def start(g_refs, land_refs, sems):
        for cp in plan(g_refs, land_refs, sems):
            cp.start()

    def finish(g_refs, land_refs, sems):
        for cp in plan(g_refs, land_refs, sems):
            cp.wait()

    scratch = [pltpu.SemaphoreType.DMA((NSH * n,)), pltpu.SemaphoreType.DMA((NSH * n,))]
    return _Comm(gs, [_sds((NSH, h, g.shape[2])) for h, g in zip(halves, gs)], scratch, start, finish)


def _chip_sum(g, land, c_arr, name):
    _, h, cols = land.shape

    def body(c_ref, g_ref, l_ref, o_ref):
        o_ref[...] = (g_ref[...] + l_ref[...]).astype(BF16)

    return pl.pallas_call(
        body, name=name, out_shape=_sds((NSH, h, cols), BF16),
        grid_spec=pltpu.PrefetchScalarGridSpec(
            num_scalar_prefetch=1, grid=(NSH,),
            in_specs=[pl.BlockSpec((1, h, cols), lambda s, c_ref: (s, c_ref[0], 0)),
                      pl.BlockSpec((1, h, cols), lambda s, c_ref: (s, 0, 0))],
            out_specs=pl.BlockSpec((1, h, cols), lambda s, c_ref: (s, 0, 0))),
        compiler_params=_cparams(1),
    )(c_arr, g, land)


def _scatter_comm(parts):
    n = len(parts)

    def plan(p_refs, land_refs, sems):
        send_sems, recv_sems, local_sems = sems
        x, y, c = _mesh_pos()
        my_chip = 2 * x + y
        local = [pltpu.make_async_copy(p_refs[a].at[my_chip], land_refs[a].at[my_chip], local_sems.at[a]) for a in range(n)]
        copies = []
        for a in range(n):
            for j, (px, py) in enumerate(_other_chips(x, y)):
                copies.append(pltpu.make_async_remote_copy(
                    src_ref=p_refs[a].at[2 * px + py], dst_ref=land_refs[a].at[my_chip],
                    send_sem=send_sems.at[3 * a + j], recv_sem=recv_sems.at[3 * a + j],
                    device_id=(px, py, c), device_id_type=MESH))
        return local, copies

    def start(p_refs, land_refs, sems):
        local, copies = plan(p_refs, land_refs, sems)
        for cp in local + copies:
            cp.start()

    def finish(p_refs, land_refs, sems):
        local, copies = plan(p_refs, land_refs, sems)
        for cp in copies:
            cp.wait()
        for cp in local:
            cp.wait()

    scratch = [pltpu.SemaphoreType.DMA((3 * n,)), pltpu.SemaphoreType.DMA((3 * n,)), pltpu.SemaphoreType.DMA((n,))]
    return _Comm(parts, [_sds(p.shape, BF16) for p in parts], scratch, start, finish)


def _sum_slots(land, name):
    k, r, c = land.shape
    tr = r // 2 if r % 32 == 0 else r

    def body(l_ref, o_ref):
        acc = l_ref[0].astype(F32)
        for i in range(1, k):
            acc = acc + l_ref[i].astype(F32)
        o_ref[...] = acc

    return pl.pallas_call(
        body, name=name, grid=(r // tr,), in_specs=[pl.BlockSpec((k, tr, c), lambda i: (0, i, 0))],
        out_specs=_rows(tr, c), out_shape=_sds((r, c)), compiler_params=_cparams(1),
    )(land)


def _sibling_swap(halves):
    n = len(halves)

    def body(*refs):
        h_refs, out_refs = refs[:n], refs[n:2 * n]
        send_sems, recv_sems = refs[2 * n:]
        x, y, c = _mesh_pos()
        copies = [pltpu.make_async_remote_copy(
            src_ref=h_refs[a], dst_ref=out_refs[a], send_sem=send_sems.at[a], recv_sem=recv_sems.at[a],
            device_id=(x, y, 1 - c), device_id_type=MESH) for a in range(n)]
        for cp in copies:
            cp.start()
        for cp in copies:
            cp.wait()

    return pl.pallas_call(
        body, name="grad_sibling_swap", out_shape=[_sds(h.shape, h.dtype) for h in halves],
        in_specs=[pl.BlockSpec(memory_space=pltpu.HBM)] * n, out_specs=[pl.BlockSpec(memory_space=pltpu.HBM)] * n,
        scratch_shapes=[pltpu.SemaphoreType.DMA((n,)), pltpu.SemaphoreType.DMA((n,))],
    )(*halves)


def _pad_rows(v, width):
    flat = v.reshape(-1)
    rows = -(-flat.shape[0] // width)
    rows = -(-rows // 8) * 8
    return jnp.pad(flat, (0, rows * width - flat.shape[0])).reshape(rows, width)


def _size(shape):
    n = 1
    for dim in shape:
        n *= dim
    return n


def _row_pack(arrs):
    pieces = []
    for a in arrs:
        rows = -(-a.size // D)
        pieces.append(jnp.pad(a.reshape(-1), (0, rows * D - a.size)).reshape(rows, D))
    total = sum(p.shape[0] for p in pieces)
    if total % 8:
        pieces.append(jnp.zeros((8 - total % 8, D), F32))
    return jnp.concatenate(pieces, axis=0)


def _row_unpack(packed, shapes):
    out, r0 = [], 0
    for shp in shapes:
        n = _size(shp)
        rows = -(-n // D)
        out.append(packed[r0:r0 + rows].reshape(-1)[:n].reshape(shp))
        r0 += rows
    return out


def _block_diag(w):
    eye = jnp.eye(8, dtype=w.dtype)
    return (w[:, :, None, :] * eye[:, None, :, None]).reshape(RGW, RGW)


def _diag_blocks(dense):
    r = dense.reshape(8, 64, 8, 64)
    return jnp.stack([r[n, :, n, :] for n in range(8)])


def _lane_row(v8):
    return jnp.zeros((1, BAP), F32).at[0, 8:16].set(v8.reshape(8))


def _chip_sums(gs, lands, names, c_arr):
    return [_chip_sum(g, l, c_arr, "chip_sum_" + n) for g, l, n in zip(gs, lands, names)]


def _reduce_parts(gs, names, c_arr, tag):
    return _chip_sums(gs, _comm_call(_exchange_comm(gs), "grad_sibling_exchange_" + tag), names, c_arr)


def _local_step(x, target, sw, ffn1_w, later_shards, c_arr):
    (g1, gmix, rg_cw8, rg_cb, wgates, gbias, lam_row, gdn_cw8, alog_row, dtb_row, gn, g2, gfin) = sw
    wg1, wu1, wd1 = ffn1_w

    (x1, a1, b1, fb1), gathered = _ffn_fwd(x, g1, wg1, wu1, wd1, "ffn1_fwd", comm=_weights_gather_comm(later_shards))
    win_sh, wout_sh, wg2, wu2, wd2 = _all_shards(gathered)
    w_in_full = jnp.transpose(win_sh, (1, 0, 2)).reshape(D, NSH * INSH)
    wout = wout_sh.reshape(D, D)
    w_in_groups = (w_in_full[:, 0:512], w_in_full[:, 512:1024], w_in_full[:, 1024:2560], w_in_full[:, 2560:3072],
                   jnp.pad(w_in_full[:, 3072:3088], ((0, 0), (0, BAP - BAW))))
    h2, p_rgx, p_gate, p_qkv, p_z, p_ba = _inproj(x1, gmix, w_in_groups, "in_proj")
    c_rg = _conv(p_rgx, rg_cw8, rg_cb, "rg_conv")
    c_qkv = _conv(p_qkv, gdn_cw8, jnp.zeros((1, QKVW), F32), "gdn_conv")
    a0, bb0, a1s, bb1, q, k, v, bg = _mix_prep(c_rg, c_qkv, p_ba, wgates, gbias, lam_row, alog_row, dtb_row, "mix_prep")
    tmat, gu, gw, gqd, gkd, gat, gcd = _gdn_local_fwd(q, k, v, bg, "gdn_local_fwd")
    of, s0, vn0, ob, s1, vn1, hf, hb = _gdn_seq_fwd(gu, gw, gqd, gkd, gat, gcd, "gdn_seq_fwd", scan=(a0, bb0, a1s, bb1))
    x2, ymix = _outproj(x1, hf, hb, p_gate, of, ob, p_z, gn, wout, "out_proj")
    (x3, a2, b2, fb2), _ = _ffn_fwd(x2, g2, wg2, wu2, wd2, "ffn2_fwd")
    dx3, dob2, loss_blk, d_gfin = _loss_head(x3, target, gfin, "loss_head")

    dx2, d_g2, hb2, dab2, dbb2, _ = _ffn_bwd(x2, dx3, dob2, g2, a2, b2, wg2, wu2, wd2, "ffn2_bwd")
    d_ffn2 = [_tn(dab2, hb2, "ffn2_dwg"), _tn(dbb2, hb2, "ffn2_dwu"), _tn(fb2, dob2, "ffn2_dwd")]

    (d_hr, d_gate, d_os, d_z, d_gn, dx2b), lands = _outproj_bwd(dx2, hf, hb, p_gate, of, ob, p_z, gn, wout, "out_proj_bwd",
                                                               comm=_exchange_comm(d_ffn2))
    parts_ffn2 = _chip_sums(d_ffn2, lands, _BIG_NAMES[5:8], c_arr)
    d_wout = _tn(ymix, dx2b, "dw_out")[0]

    sg = _gdn_seq_bwd(d_os, gw, gqd, gkd, gat, gcd, (s0, s1), (vn0, vn1), "gdn_seq_bwd", scan=(a1s, d_hr, a0, d_hr))
    lam1, lam0 = sg[10:12]
    d_xc, d_pre, xcb, d_gbias, d_lam = _gates_bwd(c_rg, wgates, gbias, lam_row, lam0, lam1, hf, hb, "rg_gates_bwd")
    d_wgates = _tn(xcb, d_pre, "dw_gates")[0]
    d_prgx, d_rgcw8, d_rgcb = _conv_bwd(p_rgx, d_xc, rg_cw8, "rg_conv_bwd")

    (dq, dk, dv, dbg), lands_ffn2 = _gdn_local_bwd(q, k, v, bg, tmat, d_os, (vn0, vn1), (sg[0:5], sg[5:10]), "gdn_local_bwd",
                                                  comm=_scatter_comm(parts_ffn2))
    d_cqkv, d_pba, d_alog, d_dtb = _prep_bwd(c_qkv, p_ba, alog_row, dtb_row, dq, dk, dv, dbg, "gdn_prep_bwd")
    d_pqkv, d_gdncw8, _ = _conv_bwd(p_qkv, d_cqkv, gdn_cw8, "gdn_conv_bwd")

    dps = (d_prgx, d_gate, d_pqkv, d_z, d_pba)
    dx1, dob1, d_gmix = _inproj_bwd(x1, dx2, gmix, dps, w_in_groups, "in_proj_bwd")
    d_win_groups = [_tn(h2, dp, "dw_in_%d" % i)[0] for i, dp in enumerate(dps)]
    d_win = jnp.concatenate(d_win_groups[:4] + [d_win_groups[4][:, :BAW]], axis=1)
    d_mix = [jnp.transpose(d_win.reshape(D, NSH, INSH), (1, 0, 2)), d_wout.reshape(NSH, OUTSH, D)]

    small = dict(
        mix_norm=d_gmix, rg_conv_w=d_rgcw8[:4], rg_conv_b=d_rgcb,
        rg_gate_a_w=jnp.stack([_diag_blocks(d_wgates[:, RGW * i:RGW * (i + 1)]) for i in (0, 1)]),
        rg_gate_x_w=jnp.stack([_diag_blocks(d_wgates[:, RGW * i:RGW * (i + 1)]) for i in (2, 3)]),
        rg_gate_a_b=d_gbias[0, :2 * RGW].reshape(2, RGW), rg_gate_x_b=d_gbias[0, 2 * RGW:].reshape(2, RGW),
        rg_lambda=d_lam.reshape(2, RGW), gdn_conv_w=d_gdncw8[:4],
        gdn_a_log=d_alog[0, 8:16].reshape(2, NH), gdn_dt_bias=d_dtb[0, 8:16].reshape(2, NH),
        gdn_norm=d_gn, ffn2_norm=d_g2, final_norm=d_gfin)
    small_pack = _row_pack([small[n] for n in _SMALL_NAMES[1:]])

    riders = _join_comm(_exchange_comm(d_mix), _gather_comm([small_pack], pltpu.HBM, [None]))
    gx, d_g1, hb1, dab1, dbb1, carried = _ffn_bwd(x, dx1, dob1, g1, a1, b1, wg1, wu1, wd1, "ffn1_bwd", comm=riders)
    parts_mix = _chip_sums(d_mix, carried[0:2], _BIG_NAMES[3:5], c_arr)
    d_wg1, lands_mix = _tn(dab1, hb1, "ffn1_dwg", comm=_scatter_comm(parts_mix))
    parts_wg1 = _reduce_parts([d_wg1], _BIG_NAMES[0:1], c_arr, "ffn1_gate")
    d_wu1, lands_wg1 = _tn(dbb1, hb1, "ffn1_dwu", comm=_scatter_comm(parts_wg1))
    parts_wu1 = _reduce_parts([d_wu1], _BIG_NAMES[1:2], c_arr, "ffn1_up")
    d_wd1, lands_wu1 = _tn(fb1, dob1, "ffn1_dwd", comm=_scatter_comm(parts_wu1))
    parts_wd1 = _reduce_parts([d_wd1], _BIG_NAMES[2:3], c_arr, "ffn1_down")
    lands_ffn1 = lands_wg1 + lands_wu1 + _comm_call(_scatter_comm(parts_wd1), "grad_chip_scatter_ffn1_down")

    halves = lands_ffn1 + lands_mix + lands_ffn2
    small_shapes = [small[n].shape for n in _SMALL_NAMES[1:]]
    return loss_blk, gx, halves, d_g1, carried[2], small_shapes


_SMALL_NAMES = ("ffn1_norm", "mix_norm", "rg_conv_w", "rg_conv_b", "rg_gate_a_w", "rg_gate_a_b", "rg_gate_x_w",
                "rg_gate_x_b", "rg_lambda", "gdn_conv_w", "gdn_a_log", "gdn_dt_bias", "gdn_norm", "ffn2_norm", "final_norm")
_SMALL_SHARDED = dict(rg_conv_w=128, rg_gate_a_b=128, rg_gate_x_b=128, rg_lambda=128, gdn_conv_w=384)
_OUT_ORDER = ("ffn1_norm", "ffn1_w_gate", "ffn1_w_up", "ffn1_w_down", "mix_norm", "w_in", "w_out", "rg_conv_w", "rg_conv_b",
              "rg_gate_a_w", "rg_gate_a_b", "rg_gate_x_w", "rg_gate_x_b", "rg_lambda", "gdn_conv_w", "gdn_a_log",
              "gdn_dt_bias", "gdn_norm", "ffn2_norm", "ffn2_w_gate", "ffn2_w_up", "ffn2_w_down", "final_norm")
_BIG_NAMES = ("ffn1_w_gate", "ffn1_w_up", "ffn1_w_down", "w_in", "w_out", "ffn2_w_gate", "ffn2_w_up", "ffn2_w_down")
_TRANSPOSED = ("ffn1_w_gate", "ffn1_w_up", "ffn2_w_gate", "ffn2_w_up")


def kernel(x, ffn1_norm, ffn1_w_gate, ffn1_w_up, ffn1_w_down, mix_norm, w_in, w_out, rg_conv_w, rg_conv_b, rg_gate_a_w, rg_gate_a_b, rg_gate_x_w, rg_gate_x_b, rg_lambda, gdn_conv_w, gdn_a_log, gdn_dt_bias, gdn_norm, ffn2_norm, ffn2_w_gate, ffn2_w_up, ffn2_w_down, final_norm, loss_target, m_ffn1_norm, m_ffn1_w_gate, m_ffn1_w_up, m_ffn1_w_down, m_mix_norm, m_w_in, m_w_out, m_rg_conv_w, m_rg_conv_b, m_rg_gate_a_w, m_rg_gate_a_b, m_rg_gate_x_w, m_rg_gate_x_b, m_rg_lambda, m_gdn_conv_w, m_gdn_a_log, m_gdn_dt_bias, m_gdn_norm, m_ffn2_norm, m_ffn2_w_gate, m_ffn2_w_up, m_ffn2_w_down, m_final_norm, v_ffn1_norm, v_ffn1_w_gate, v_ffn1_w_up, v_ffn1_w_down, v_mix_norm, v_w_in, v_w_out, v_rg_conv_w, v_rg_conv_b, v_rg_gate_a_w, v_rg_gate_a_b, v_rg_gate_x_w, v_rg_gate_x_b, v_rg_lambda, v_gdn_conv_w, v_gdn_a_log, v_gdn_dt_bias, v_gdn_norm, v_ffn2_norm, v_ffn2_w_gate, v_ffn2_w_up, v_ffn2_w_down, v_final_norm):
    args = dict(locals())
    w = {n: args[n] for n in _OUT_ORDER}
    mom = {n: args["m_" + n] for n in _OUT_ORDER}
    var = {n: args["v_" + n] for n in _OUT_ORDER}
    xi, yi, ci = _mesh_pos()
    shard = 2 * xi + yi

    big_bf16 = [w[n][0].astype(BF16) for n in _BIG_NAMES]
    sm_local = _pad_rows(jnp.concatenate([w[n][0].reshape(-1) for n in _SMALL_SHARDED]), 128)
    first = _comm_call(_gather_comm(big_bf16[0:3] + [sm_local], pltpu.HBM, [t.shape[0] // 2 for t in big_bf16[0:3]] + [None]),
                       "gather_first_weights")
    ffn1_w = _all_shards(first[0:3])
    sm_all = first[3][0::2].reshape(NSH, -1)
    sm_full, off = {}, 0
    for n, wd_ in _SMALL_SHARDED.items():
        rows = w[n].shape[1]
        piece = sm_all[:, off:off + rows * wd_].reshape(NSH, rows, wd_)
        sm_full[n] = jnp.transpose(piece, (1, 0, 2)).reshape(rows, NSH * wd_)
        off += rows * wd_

    wa, wx = rg_gate_a_w[0], rg_gate_x_w[0]
    wgates = jnp.concatenate([_block_diag(wa[0]), _block_diag(wa[1]), _block_diag(wx[0]), _block_diag(wx[1])],
                             axis=1).astype(BF16)
    gbias = jnp.concatenate([sm_full["rg_gate_a_b"].reshape(1, -1), sm_full["rg_gate_x_b"].reshape(1, -1)], axis=1)
    sw = (ffn1_norm, mix_norm, jnp.pad(sm_full["rg_conv_w"], ((0, 4), (0, 0))), rg_conv_b, wgates, gbias,
          sm_full["rg_lambda"].reshape(1, -1), jnp.pad(sm_full["gdn_conv_w"], ((0, 4), (0, 0))), _lane_row(gdn_a_log),
          _lane_row(gdn_dt_bias), gdn_norm, ffn2_norm, final_norm.reshape(1, D))
    c_arr = ci.reshape(1).astype(jnp.int32)

    loss_blk, gx, halves, d_g1, small_packs, small_shapes = _local_step(x[0], loss_target[0], sw, ffn1_w, big_bf16[3:], c_arr)
    loss = lax.psum(loss_blk[0, 0], ("x", "y", "c"))
    grads = {}

    g1_all = _gather_small(jnp.pad(d_g1, ((0, 7), (0, 0))), "gather_ffn1_norm_grad")
    sm_sums = [_sum_slots(g1_all, "ffn1_norm_grad_sum")[0:1]] + _row_unpack(_sum_slots(small_packs, "small_grad_sum"), small_shapes)
    for n, g in zip(_SMALL_NAMES, sm_sums):
        if n in _SMALL_SHARDED:
            wd_ = _SMALL_SHARDED[n]
            g = lax.dynamic_slice_in_dim(g, shard * wd_, wd_, axis=1)
        grads[n] = g.reshape(w[n].shape)

    delta, new_m, new_v = {}, {}, {}
    for n, own, recv in zip(_BIG_NAMES, halves, _sibling_swap(halves)):
        to2d = jnp.transpose if n in _TRANSPOSED else (lambda t: t)
        outs4 = _adamw_halves(to2d(w[n][0]), own, recv, to2d(mom[n][0]), to2d(var[n][0]), c_arr, "adamw_" + n)
        grads[n], delta[n], new_m[n], new_v[n] = [to2d(o)[None] for o in outs4]
    packs = [_row_pack([t[n] for n in _SMALL_NAMES]) for t in (w, grads, mom, var)]
    sm_shapes = [w[n].shape for n in _SMALL_NAMES]
    for dst, src in zip((delta, new_m, new_v), _adamw(*packs, "adamw_small")):
        for n, val in zip(_SMALL_NAMES, _row_unpack(src, sm_shapes)):
            dst[n] = val

    outs = [loss, gx[None]]
    for group in (grads, delta, new_m, new_v):
        outs += [group[n] for n in _OUT_ORDER]
    return tuple(outs)
```
